```python
import math
import jax, jax.numpy as jnp
from jax import lax
import numpy as np

D_MODEL = 1024
BATCH = 8
SEQ = 2048
DEPTH = 1

HEAD_DIM = 64
D_ATTN = D_MODEL // 2
D_RWKV = D_MODEL - D_ATTN
N_Q_HEADS = D_ATTN // HEAD_DIM
N_KV_HEADS = max(1, N_Q_HEADS // 4)
Q_PER_KV = N_Q_HEADS // N_KV_HEADS
D_KV = N_KV_HEADS * HEAD_DIM
WINDOW = 128
BLOCK = 128
N_BUCKETS = 32
MAX_DISTANCE = 128
N_RWKV_HEADS = D_RWKV // HEAD_DIM
LORA_DECAY = 64
LORA_ICLR = 64
LORA_GATE = 128
RWKV_COLS = 3 * D_RWKV + LORA_DECAY + LORA_ICLR + LORA_GATE
RWKV_SPLITS = (D_RWKV, 2 * D_RWKV, 3 * D_RWKV, 3 * D_RWKV + LORA_DECAY, 3 * D_RWKV + LORA_DECAY + LORA_ICLR)
D_IN = D_ATTN + 2 * D_KV + RWKV_COLS
D_FF = 4 * D_MODEL
CONV_WIDTH = 3
NORM_EPS = 1e-6
GN_EPS = 64e-5
NEG_INF = -1e30

kernel_name = "hymba_swa_sink_rwkv7_convffn_sandwich"


def rms_norm(x, g):
    xf = x.astype(jnp.float32)
    y = xf * lax.rsqrt(jnp.mean(xf * xf, axis=-1, keepdims=True) + NORM_EPS) * g.astype(jnp.float32)
    return y.astype(x.dtype)


def t5_bucket(rel):
    n = jnp.maximum(rel, 0)
    max_exact = N_BUCKETS // 2
    large = max_exact + (jnp.log(jnp.maximum(n, 1).astype(jnp.float32) / max_exact)
                         / math.log(MAX_DISTANCE / max_exact) * (N_BUCKETS - max_exact)).astype(jnp.int32)
    large = jnp.minimum(large, N_BUCKETS - 1)
    return jnp.where(n < max_exact, n, large)


def sliding_window_sink_attention(q, k, v, rel_bias, sinks):
    B, S, _ = q.shape
    NB = S // BLOCK
    q = q.reshape(B, NB, BLOCK, N_KV_HEADS, Q_PER_KV, HEAD_DIM)

    def band(t):
        t = t.reshape(B, NB, BLOCK, N_KV_HEADS, HEAD_DIM)
        prev = jnp.concatenate([jnp.zeros_like(t[:, :1]), t[:, :-1]], axis=1)
        return jnp.concatenate([prev, t], axis=2)

    kb, vb = band(k), band(v)
    rel = (jnp.arange(BLOCK)[:, None] + BLOCK) - jnp.arange(2 * BLOCK)[None, :]
    in_window = (rel >= 0) & (rel < WINDOW)
    key_pos = (jnp.arange(NB)[:, None] - 1) * BLOCK + jnp.arange(2 * BLOCK)[None, :]
    mask = in_window[None] & (key_pos >= 0)[:, None, :]
    bias = rel_bias.astype(jnp.float32)[t5_bucket(rel)]
    bias = bias.transpose(2, 0, 1).reshape(N_KV_HEADS, Q_PER_KV, BLOCK, 2 * BLOCK)
    scores = jnp.einsum('bnqhgd,bnkhd->bnhgqk', q, kb).astype(jnp.float32) * (HEAD_DIM ** -0.5) + bias
    scores = jnp.where(mask[None, :, None, None], scores, NEG_INF)
    sink = sinks.astype(jnp.float32).reshape(N_KV_HEADS, Q_PER_KV)[:, :, None, None]
    m = jnp.maximum(scores.max(axis=-1, keepdims=True), sink)
    p = jnp.exp(scores - m)
    probs = p / (p.sum(axis=-1, keepdims=True) + jnp.exp(sink - m))
    out = jnp.einsum('bnhgqk,bnkhd->bnqhgd', probs.astype(v.dtype), vb)
    return out.reshape(B, S, D_ATTN)


def rwkv7_time_mix(p, w0, w_decay_up, a0, w_iclr_up, w_gate_up, k_k, k_a, r_k, ln_x_g, ln_x_b):
    B, S, _ = p.shape
    H, N = N_RWKV_HEADS, HEAD_DIM
    p = p.astype(jnp.float32)
    r, k, v, zw, za, zg = jnp.split(p, RWKV_SPLITS, axis=-1)
    w_log = -jax.nn.softplus(-(w0 + jnp.tanh(zw) @ w_decay_up)) - 0.5
    decay = jnp.exp(-jnp.exp(w_log))
    a = jax.nn.sigmoid(a0 + za @ w_iclr_up)
    g = jax.nn.sigmoid(zg) @ w_gate_up
    kk = (k * k_k).reshape(B, S, H, N)
    kk = kk / jnp.maximum(jnp.sqrt(jnp.sum(kk * kk, axis=-1, keepdims=True)), 1e-12)
    k = k * (1.0 + (a - 1.0) * k_a)

    def heads(t):
        return t.reshape(B, S, H, N)

    def tmaj(t):
        return t.swapaxes(0, 1)

    rh, kh, vh = heads(r), heads(k), heads(v)

    def step(state, inp):
        r_t, w_t, k_t, v_t, kk_t, a_t = inp
        sa = jnp.einsum('bhvk,bhk->bhv', state, -kk_t)
        state = (state * w_t[:, :, None, :]
                 + sa[..., None] * (kk_t * a_t)[:, :, None, :]
                 + v_t[..., None] * k_t[:, :, None, :])
        return state, jnp.einsum('bhvk,bhk->bhv', state, r_t)

    state0 = jnp.zeros((B, H, N, N), jnp.float32)
    _, o = lax.scan(step, state0, (tmaj(rh), tmaj(heads(decay)), tmaj(kh), tmaj(vh), tmaj(kk), tmaj(heads(a))))
    o = o.swapaxes(0, 1)
    mu = jnp.mean(o, axis=-1, keepdims=True)
    var = jnp.mean(jnp.square(o - mu), axis=-1, keepdims=True)
    o = ((o - mu) * lax.rsqrt(var + GN_EPS)).reshape(B, S, D_RWKV) * ln_x_g + ln_x_b
    bonus = jnp.sum(rh * kh * r_k, axis=-1, keepdims=True) * vh
    o = o + bonus.reshape(B, S, D_RWKV)
    return o * g


def conv_gated_ffn(h, w_up, conv_w, conv_b, w_down):
    S = h.shape[1]
    u = h @ w_up
    u_pad = jnp.pad(u, ((0, 0), (CONV_WIDTH - 1, 0), (0, 0)))
    u = conv_b + sum(conv_w[j] * u_pad[:, j:j + S] for j in range(CONV_WIDTH))
    gate, val = jnp.split(u, 2, axis=-1)
    return (jax.nn.gelu(gate, approximate=True) * val) @ w_down


def _fwd_setup_inputs(seed: int = 0) -> dict:
    key = jax.random.key(seed)
    ks = jax.random.split(key, 24)
    L = DEPTH
    nrm = lambda k, shape, s: jax.random.normal(k, shape, jnp.float32) * s
    return {
        "x": jax.random.normal(ks[0], (BATCH, SEQ, D_MODEL), jnp.float32),
        "norm_mix_pre": 1.0 + nrm(ks[1], (L, D_MODEL), 0.02),
        "norm_mix_post": 1.0 + nrm(ks[2], (L, D_MODEL), 0.02),
        "norm_ffn_pre": 1.0 + nrm(ks[3], (L, D_MODEL), 0.02),
        "norm_ffn_post": 1.0 + nrm(ks[4], (L, D_MODEL), 0.02),
        "w_in": nrm(ks[5], (L, D_MODEL, D_IN), D_MODEL ** -0.5),
        "rel_bias": nrm(ks[6], (N_BUCKETS, N_Q_HEADS), 0.5),
        "sinks": nrm(ks[7], (L, N_Q_HEADS), 0.5),
        "rwkv_shift_mix": jax.random.uniform(ks[8], (L, RWKV_COLS), jnp.float32),
        "w0": jax.random.uniform(ks[9], (L, D_RWKV), jnp.float32, -5.0, 0.0),
        "w_decay_up": nrm(ks[10], (L, LORA_DECAY, D_RWKV), 0.5 * LORA_DECAY ** -0.5),
        "a0": nrm(ks[11], (L, D_RWKV), 0.1),
        "w_iclr_up": nrm(ks[12], (L, LORA_ICLR, D_RWKV), 0.5 * LORA_ICLR ** -0.5),
        "w_gate_up": nrm(ks[13], (L, LORA_GATE, D_RWKV), LORA_GATE ** -0.5),
        "k_k": 0.85 + nrm(ks[14], (L, D_RWKV), 0.05),
        "k_a": 1.0 + nrm(ks[15], (L, D_RWKV), 0.05),
        "r_k": nrm(ks[16], (L, N_RWKV_HEADS, HEAD_DIM), 0.1),
        "ln_x_g": 1.0 + nrm(ks[17], (L, D_RWKV), 0.02),
        "ln_x_b": nrm(ks[18], (L, D_RWKV), 0.02),
        "w_out": nrm(ks[19], (L, D_MODEL, D_MODEL), D_MODEL ** -0.5),
        "w_ffn_up": nrm(ks[20], (L, D_MODEL, 2 * D_FF), D_MODEL ** -0.5),
        "conv_w": nrm(ks[21], (L, CONV_WIDTH, 2 * D_FF), CONV_WIDTH ** -0.5),
        "conv_b": nrm(ks[22], (L, 2 * D_FF), 0.02),
        "w_ffn_down": nrm(ks[23], (L, D_FF, D_MODEL), D_FF ** -0.5),
    }


def _fwd_reference(x, norm_mix_pre, norm_mix_post, norm_ffn_pre, norm_ffn_post, w_in, rel_bias, sinks,
              rwkv_shift_mix, w0, w_decay_up, a0, w_iclr_up, w_gate_up, k_k, k_a, r_k, ln_x_g, ln_x_b,
              w_out, w_ffn_up, conv_w, conv_b, w_ffn_down):
    for l in range(DEPTH):
        h = rms_norm(x, norm_mix_pre[l])
        proj = h @ w_in[l]
        q, k, v, p = jnp.split(proj, (D_ATTN, D_ATTN + D_KV, D_ATTN + 2 * D_KV), axis=-1)
        attn = sliding_window_sink_attention(q, k, v, rel_bias, sinks[l])
        p_prev = jnp.concatenate([jnp.zeros_like(p[:, :1]), p[:, :-1]], axis=1)
        p = p + (p_prev - p) * rwkv_shift_mix[l]
        rw = rwkv7_time_mix(p, w0[l], w_decay_up[l], a0[l], w_iclr_up[l], w_gate_up[l],
                            k_k[l], k_a[l], r_k[l], ln_x_g[l], ln_x_b[l])
        mix = jnp.concatenate([attn, rw.astype(x.dtype)], axis=-1) @ w_out[l]
        x = x + rms_norm(mix, norm_mix_post[l])
        f = conv_gated_ffn(rms_norm(x, norm_ffn_pre[l]), w_ffn_up[l], conv_w[l], conv_b[l], w_ffn_down[l])
        x = x + rms_norm(f, norm_ffn_post[l])
    return x


import jax as _jax
import jax.numpy as _jnp

TWIN_FORMAT = 'train_step'
FWD_PARAMS = ['x', 'norm_mix_pre', 'norm_mix_post', 'norm_ffn_pre', 'norm_ffn_post', 'w_in', 'rel_bias', 'sinks', 'rwkv_shift_mix', 'w0', 'w_decay_up', 'a0', 'w_iclr_up', 'w_gate_up', 'k_k', 'k_a', 'r_k', 'ln_x_g', 'ln_x_b', 'w_out', 'w_ffn_up', 'conv_w', 'conv_b', 'w_ffn_down']
TWIN_WEIGHTS = ['norm_mix_pre', 'norm_mix_post', 'norm_ffn_pre', 'norm_ffn_post', 'w_in', 'rel_bias', 'sinks', 'rwkv_shift_mix', 'w0', 'w_decay_up', 'a0', 'w_iclr_up', 'w_gate_up', 'k_k', 'k_a', 'r_k', 'ln_x_g', 'ln_x_b', 'w_out', 'w_ffn_up', 'conv_w', 'conv_b', 'w_ffn_down']
TWIN_DIFF_INPUT = 'x'
TWIN_INPUTS = ['x', 'norm_mix_pre', 'norm_mix_post', 'norm_ffn_pre', 'norm_ffn_post', 'w_in', 'rel_bias', 'sinks', 'rwkv_shift_mix', 'w0', 'w_decay_up', 'a0', 'w_iclr_up', 'w_gate_up', 'k_k', 'k_a', 'r_k', 'ln_x_g', 'ln_x_b', 'w_out', 'w_ffn_up', 'conv_w', 'conv_b', 'w_ffn_down', 'loss_target', 'm_norm_mix_pre', 'm_norm_mix_post', 'm_norm_ffn_pre', 'm_norm_ffn_post', 'm_w_in', 'm_rel_bias', 'm_sinks', 'm_rwkv_shift_mix', 'm_w0', 'm_w_decay_up', 'm_a0', 'm_w_iclr_up', 'm_w_gate_up', 'm_k_k', 'm_k_a', 'm_r_k', 'm_ln_x_g', 'm_ln_x_b', 'm_w_out', 'm_w_ffn_up', 'm_conv_w', 'm_conv_b', 'm_w_ffn_down', 'v_norm_mix_pre', 'v_norm_mix_post', 'v_norm_ffn_pre', 'v_norm_ffn_post', 'v_w_in', 'v_rel_bias', 'v_sinks', 'v_rwkv_shift_mix', 'v_w0', 'v_w_decay_up', 'v_a0', 'v_w_iclr_up', 'v_w_gate_up', 'v_k_k', 'v_k_a', 'v_r_k', 'v_ln_x_g', 'v_ln_x_b', 'v_w_out', 'v_w_ffn_up', 'v_conv_w', 'v_conv_b', 'v_w_ffn_down']
TWIN_OUTPUTS = ['loss', 'grad_x', 'grad_norm_mix_pre', 'grad_norm_mix_post', 'grad_norm_ffn_pre', 'grad_norm_ffn_post', 'grad_w_in', 'grad_rel_bias', 'grad_sinks', 'grad_rwkv_shift_mix', 'grad_w0', 'grad_w_decay_up', 'grad_a0', 'grad_w_iclr_up', 'grad_w_gate_up', 'grad_k_k', 'grad_k_a', 'grad_r_k', 'grad_ln_x_g', 'grad_ln_x_b', 'grad_w_out', 'grad_w_ffn_up', 'grad_conv_w', 'grad_conv_b', 'grad_w_ffn_down', 'delta_norm_mix_pre', 'delta_norm_mix_post', 'delta_norm_ffn_pre', 'delta_norm_ffn_post', 'delta_w_in', 'delta_rel_bias', 'delta_sinks', 'delta_rwkv_shift_mix', 'delta_w0', 'delta_w_decay_up', 'delta_a0', 'delta_w_iclr_up', 'delta_w_gate_up', 'delta_k_k', 'delta_k_a', 'delta_r_k', 'delta_ln_x_g', 'delta_ln_x_b', 'delta_w_out', 'delta_w_ffn_up', 'delta_conv_w', 'delta_conv_b', 'delta_w_ffn_down', 'new_m_norm_mix_pre', 'new_m_norm_mix_post', 'new_m_norm_ffn_pre', 'new_m_norm_ffn_post', 'new_m_w_in', 'new_m_rel_bias', 'new_m_sinks', 'new_m_rwkv_shift_mix', 'new_m_w0', 'new_m_w_decay_up', 'new_m_a0', 'new_m_w_iclr_up', 'new_m_w_gate_up', 'new_m_k_k', 'new_m_k_a', 'new_m_r_k', 'new_m_ln_x_g', 'new_m_ln_x_b', 'new_m_w_out', 'new_m_w_ffn_up', 'new_m_conv_w', 'new_m_conv_b', 'new_m_w_ffn_down', 'new_v_norm_mix_pre', 'new_v_norm_mix_post', 'new_v_norm_ffn_pre', 'new_v_norm_ffn_post', 'new_v_w_in', 'new_v_rel_bias', 'new_v_sinks', 'new_v_rwkv_shift_mix', 'new_v_w0', 'new_v_w_decay_up', 'new_v_a0', 'new_v_w_iclr_up', 'new_v_w_gate_up', 'new_v_k_k', 'new_v_k_a', 'new_v_r_k', 'new_v_ln_x_g', 'new_v_ln_x_b', 'new_v_w_out', 'new_v_w_ffn_up', 'new_v_conv_w', 'new_v_conv_b', 'new_v_w_ffn_down']
TWIN_LEAF_KINDS = {'loss': 'loss', 'grad_x': 'grad_x', 'grad_norm_mix_pre': 'grad_w', 'grad_norm_mix_post': 'grad_w', 'grad_norm_ffn_pre': 'grad_w', 'grad_norm_ffn_post': 'grad_w', 'grad_w_in': 'grad_w', 'grad_rel_bias': 'grad_w', 'grad_sinks': 'grad_w', 'grad_rwkv_shift_mix': 'grad_w', 'grad_w0': 'grad_w', 'grad_w_decay_up': 'grad_w', 'grad_a0': 'grad_w', 'grad_w_iclr_up': 'grad_w', 'grad_w_gate_up': 'grad_w', 'grad_k_k': 'grad_w', 'grad_k_a': 'grad_w', 'grad_r_k': 'grad_w', 'grad_ln_x_g': 'grad_w', 'grad_ln_x_b': 'grad_w', 'grad_w_out': 'grad_w', 'grad_w_ffn_up': 'grad_w', 'grad_conv_w': 'grad_w', 'grad_conv_b': 'grad_w', 'grad_w_ffn_down': 'grad_w', 'delta_norm_mix_pre': 'delta_w', 'delta_norm_mix_post': 'delta_w', 'delta_norm_ffn_pre': 'delta_w', 'delta_norm_ffn_post': 'delta_w', 'delta_w_in': 'delta_w', 'delta_rel_bias': 'delta_w', 'delta_sinks': 'delta_w', 'delta_rwkv_shift_mix': 'delta_w', 'delta_w0': 'delta_w', 'delta_w_decay_up': 'delta_w', 'delta_a0': 'delta_w', 'delta_w_iclr_up': 'delta_w', 'delta_w_gate_up': 'delta_w', 'delta_k_k': 'delta_w', 'delta_k_a': 'delta_w', 'delta_r_k': 'delta_w', 'delta_ln_x_g': 'delta_w', 'delta_ln_x_b': 'delta_w', 'delta_w_out': 'delta_w', 'delta_w_ffn_up': 'delta_w', 'delta_conv_w': 'delta_w', 'delta_conv_b': 'delta_w', 'delta_w_ffn_down': 'delta_w', 'new_m_norm_mix_pre': 'new_m', 'new_m_norm_mix_post': 'new_m', 'new_m_norm_ffn_pre': 'new_m', 'new_m_norm_ffn_post': 'new_m', 'new_m_w_in': 'new_m', 'new_m_rel_bias': 'new_m', 'new_m_sinks': 'new_m', 'new_m_rwkv_shift_mix': 'new_m', 'new_m_w0': 'new_m', 'new_m_w_decay_up': 'new_m', 'new_m_a0': 'new_m', 'new_m_w_iclr_up': 'new_m', 'new_m_w_gate_up': 'new_m', 'new_m_k_k': 'new_m', 'new_m_k_a': 'new_m', 'new_m_r_k': 'new_m', 'new_m_ln_x_g': 'new_m', 'new_m_ln_x_b': 'new_m', 'new_m_w_out': 'new_m', 'new_m_w_ffn_up': 'new_m', 'new_m_conv_w': 'new_m', 'new_m_conv_b': 'new_m', 'new_m_w_ffn_down': 'new_m', 'new_v_norm_mix_pre': 'new_v', 'new_v_norm_mix_post': 'new_v', 'new_v_norm_ffn_pre': 'new_v', 'new_v_norm_ffn_post': 'new_v', 'new_v_w_in': 'new_v', 'new_v_rel_bias': 'new_v', 'new_v_sinks': 'new_v', 'new_v_rwkv_shift_mix': 'new_v', 'new_v_w0': 'new_v', 'new_v_w_decay_up': 'new_v', 'new_v_a0': 'new_v', 'new_v_w_iclr_up': 'new_v', 'new_v_w_gate_up': 'new_v', 'new_v_k_k': 'new_v', 'new_v_k_a': 'new_v', 'new_v_r_k': 'new_v', 'new_v_ln_x_g': 'new_v', 'new_v_ln_x_b': 'new_v', 'new_v_w_out': 'new_v', 'new_v_w_ffn_up': 'new_v', 'new_v_conv_w': 'new_v', 'new_v_conv_b': 'new_v', 'new_v_w_ffn_down': 'new_v'}


def _forward(args):
    return _fwd_reference(*[args[k] for k in FWD_PARAMS])


def _output_shape():
    out = _jax.eval_shape(lambda: _forward(_fwd_setup_inputs(0)))
    return out.shape, out.dtype

N_MICROBATCH = 1
ADAM_LR = 0.001
ADAM_B1 = 0.9
ADAM_B2 = 0.999
ADAM_EPS = 1e-08
ADAM_WD = 0.01
ADAM_STEP = 10
PER_EXAMPLE_BATCH_AXIS = {'x': 0, 'loss_target': 0}
SHARED_INPUTS = []
_WEIGHT_DTYPES = {'norm_mix_pre': _jnp.float32, 'norm_mix_post': _jnp.float32, 'norm_ffn_pre': _jnp.float32, 'norm_ffn_post': _jnp.float32, 'w_in': _jnp.float32, 'rel_bias': _jnp.float32, 'sinks': _jnp.float32, 'rwkv_shift_mix': _jnp.float32, 'w0': _jnp.float32, 'w_decay_up': _jnp.float32, 'a0': _jnp.float32, 'w_iclr_up': _jnp.float32, 'w_gate_up': _jnp.float32, 'k_k': _jnp.float32, 'k_a': _jnp.float32, 'r_k': _jnp.float32, 'ln_x_g': _jnp.float32, 'ln_x_b': _jnp.float32, 'w_out': _jnp.float32, 'w_ffn_up': _jnp.float32, 'conv_w': _jnp.float32, 'conv_b': _jnp.float32, 'w_ffn_down': _jnp.float32}
MOMENT_SCALE = {'norm_mix_pre': 5.754468e-01, 'norm_mix_post': 1.603019e+01, 'norm_ffn_pre': 3.884047e-01, 'norm_ffn_post': 1.599063e+01, 'w_in': 3.691716e-01, 'rel_bias': 2.044896e-01, 'sinks': 1.307057e-01, 'rwkv_shift_mix': 6.995429e-01, 'w0': 2.047346e-01, 'w_decay_up': 2.750808e-02, 'a0': 1.704107e-01, 'w_iclr_up': 1.614514e-01, 'w_gate_up': 5.450035e-01, 'k_k': 3.721396e-01, 'k_a': 4.587555e-01, 'r_k': 1.041058e+00, 'ln_x_g': 6.723486e-01, 'ln_x_b': 5.832385e-01, 'w_out': 3.758165e-01, 'w_ffn_up': 1.292257e-01, 'conv_w': 1.418402e-01, 'conv_b': 2.082904e-01, 'w_ffn_down': 2.933140e-01}


def _to_microbatches(a, axis):
    t = _jnp.moveaxis(a, axis, 0)
    t = t.reshape((N_MICROBATCH, t.shape[0] // N_MICROBATCH) + t.shape[1:])
    return _jnp.moveaxis(t, 1, axis + 1)


def setup_inputs(seed: int = 0) -> dict:
    inp = _fwd_setup_inputs(seed)
    key = _jax.random.fold_in(_jax.random.key(seed), 7919)
    shape, _ = _output_shape()
    out = dict(inp)
    out["loss_target"] = _jax.random.normal(_jax.random.fold_in(key, 0), shape, _jnp.float32)
    for i, name in enumerate(TWIN_WEIGHTS):
        w = inp[name].astype(_jnp.float32)
        if MOMENT_SCALE is None:
            s = _jnp.sqrt(_jnp.mean(_jnp.square(w)) + 1e-30)
        else:
            s = MOMENT_SCALE[name]
        km, kv = _jax.random.split(_jax.random.fold_in(key, i + 1))
        out[name] = w
        out["m_" + name] = s * _jax.random.normal(km, w.shape, _jnp.float32)
        out["v_" + name] = (s * s) * _jax.random.uniform(kv, w.shape, _jnp.float32, 0.5, 1.5)
    if N_MICROBATCH > 1:
        for name, axis in PER_EXAMPLE_BATCH_AXIS.items():
            out[name] = _to_microbatches(out[name], axis)
    return {'x': out['x'], 'norm_mix_pre': out['norm_mix_pre'], 'norm_mix_post': out['norm_mix_post'], 'norm_ffn_pre': out['norm_ffn_pre'], 'norm_ffn_post': out['norm_ffn_post'], 'w_in': out['w_in'], 'rel_bias': out['rel_bias'], 'sinks': out['sinks'], 'rwkv_shift_mix': out['rwkv_shift_mix'], 'w0': out['w0'], 'w_decay_up': out['w_decay_up'], 'a0': out['a0'], 'w_iclr_up': out['w_iclr_up'], 'w_gate_up': out['w_gate_up'], 'k_k': out['k_k'], 'k_a': out['k_a'], 'r_k': out['r_k'], 'ln_x_g': out['ln_x_g'], 'ln_x_b': out['ln_x_b'], 'w_out': out['w_out'], 'w_ffn_up': out['w_ffn_up'], 'conv_w': out['conv_w'], 'conv_b': out['conv_b'], 'w_ffn_down': out['w_ffn_down'], 'loss_target': out['loss_target'], 'm_norm_mix_pre': out['m_norm_mix_pre'], 'm_norm_mix_post': out['m_norm_mix_post'], 'm_norm_ffn_pre': out['m_norm_ffn_pre'], 'm_norm_ffn_post': out['m_norm_ffn_post'], 'm_w_in': out['m_w_in'], 'm_rel_bias': out['m_rel_bias'], 'm_sinks': out['m_sinks'], 'm_rwkv_shift_mix': out['m_rwkv_shift_mix'], 'm_w0': out['m_w0'], 'm_w_decay_up': out['m_w_decay_up'], 'm_a0': out['m_a0'], 'm_w_iclr_up': out['m_w_iclr_up'], 'm_w_gate_up': out['m_w_gate_up'], 'm_k_k': out['m_k_k'], 'm_k_a': out['m_k_a'], 'm_r_k': out['m_r_k'], 'm_ln_x_g': out['m_ln_x_g'], 'm_ln_x_b': out['m_ln_x_b'], 'm_w_out': out['m_w_out'], 'm_w_ffn_up': out['m_w_ffn_up'], 'm_conv_w': out['m_conv_w'], 'm_conv_b': out['m_conv_b'], 'm_w_ffn_down': out['m_w_ffn_down'], 'v_norm_mix_pre': out['v_norm_mix_pre'], 'v_norm_mix_post': out['v_norm_mix_post'], 'v_norm_ffn_pre': out['v_norm_ffn_pre'], 'v_norm_ffn_post': out['v_norm_ffn_post'], 'v_w_in': out['v_w_in'], 'v_rel_bias': out['v_rel_bias'], 'v_sinks': out['v_sinks'], 'v_rwkv_shift_mix': out['v_rwkv_shift_mix'], 'v_w0': out['v_w0'], 'v_w_decay_up': out['v_w_decay_up'], 'v_a0': out['v_a0'], 'v_w_iclr_up': out['v_w_iclr_up'], 'v_w_gate_up': out['v_w_gate_up'], 'v_k_k': out['v_k_k'], 'v_k_a': out['v_k_a'], 'v_r_k': out['v_r_k'], 'v_ln_x_g': out['v_ln_x_g'], 'v_ln_x_b': out['v_ln_x_b'], 'v_w_out': out['v_w_out'], 'v_w_ffn_up': out['v_w_ffn_up'], 'v_conv_w': out['v_conv_w'], 'v_conv_b': out['v_conv_b'], 'v_w_ffn_down': out['v_w_ffn_down']}


def _loss(weights, diff, rest, loss_target):
    with _jax.named_scope("forward"):
        args = {**rest, TWIN_DIFF_INPUT: diff, **{k: w.astype(_WEIGHT_DTYPES[k]) for k, w in weights.items()}}
        y = _forward(args)
    with _jax.named_scope("loss_head"):
        err = _jnp.square(y.astype(_jnp.float32) - loss_target)
        return 0.5 * _jnp.sum(_jnp.mean(err, axis=-1)) if err.ndim else 0.5 * err


def _adamw(w, g, m, v):
    m = ADAM_B1 * m + (1.0 - ADAM_B1) * g
    v = ADAM_B2 * v + (1.0 - ADAM_B2) * _jnp.square(g)
    m_hat = m / (1.0 - ADAM_B1 ** ADAM_STEP)
    v_hat = v / (1.0 - ADAM_B2 ** ADAM_STEP)
    delta = -ADAM_LR * (m_hat / (_jnp.sqrt(v_hat) + ADAM_EPS) + ADAM_WD * w)
    return delta, m, v


def reference(x, norm_mix_pre, norm_mix_post, norm_ffn_pre, norm_ffn_post, w_in, rel_bias, sinks, rwkv_shift_mix, w0, w_decay_up, a0, w_iclr_up, w_gate_up, k_k, k_a, r_k, ln_x_g, ln_x_b, w_out, w_ffn_up, conv_w, conv_b, w_ffn_down, loss_target, m_norm_mix_pre, m_norm_mix_post, m_norm_ffn_pre, m_norm_ffn_post, m_w_in, m_rel_bias, m_sinks, m_rwkv_shift_mix, m_w0, m_w_decay_up, m_a0, m_w_iclr_up, m_w_gate_up, m_k_k, m_k_a, m_r_k, m_ln_x_g, m_ln_x_b, m_w_out, m_w_ffn_up, m_conv_w, m_conv_b, m_w_ffn_down, v_norm_mix_pre, v_norm_mix_post, v_norm_ffn_pre, v_norm_ffn_post, v_w_in, v_rel_bias, v_sinks, v_rwkv_shift_mix, v_w0, v_w_decay_up, v_a0, v_w_iclr_up, v_w_gate_up, v_k_k, v_k_a, v_r_k, v_ln_x_g, v_ln_x_b, v_w_out, v_w_ffn_up, v_conv_w, v_conv_b, v_w_ffn_down):
    given = dict(x=x, norm_mix_pre=norm_mix_pre, norm_mix_post=norm_mix_post, norm_ffn_pre=norm_ffn_pre, norm_ffn_post=norm_ffn_post, w_in=w_in, rel_bias=rel_bias, sinks=sinks, rwkv_shift_mix=rwkv_shift_mix, w0=w0, w_decay_up=w_decay_up, a0=a0, w_iclr_up=w_iclr_up, w_gate_up=w_gate_up, k_k=k_k, k_a=k_a, r_k=r_k, ln_x_g=ln_x_g, ln_x_b=ln_x_b, w_out=w_out, w_ffn_up=w_ffn_up, conv_w=conv_w, conv_b=conv_b, w_ffn_down=w_ffn_down, loss_target=loss_target, m_norm_mix_pre=m_norm_mix_pre, m_norm_mix_post=m_norm_mix_post, m_norm_ffn_pre=m_norm_ffn_pre, m_norm_ffn_post=m_norm_ffn_post, m_w_in=m_w_in, m_rel_bias=m_rel_bias, m_sinks=m_sinks, m_rwkv_shift_mix=m_rwkv_shift_mix, m_w0=m_w0, m_w_decay_up=m_w_decay_up, m_a0=m_a0, m_w_iclr_up=m_w_iclr_up, m_w_gate_up=m_w_gate_up, m_k_k=m_k_k, m_k_a=m_k_a, m_r_k=m_r_k, m_ln_x_g=m_ln_x_g, m_ln_x_b=m_ln_x_b, m_w_out=m_w_out, m_w_ffn_up=m_w_ffn_up, m_conv_w=m_conv_w, m_conv_b=m_conv_b, m_w_ffn_down=m_w_ffn_down, v_norm_mix_pre=v_norm_mix_pre, v_norm_mix_post=v_norm_mix_post, v_norm_ffn_pre=v_norm_ffn_pre, v_norm_ffn_post=v_norm_ffn_post, v_w_in=v_w_in, v_rel_bias=v_rel_bias, v_sinks=v_sinks, v_rwkv_shift_mix=v_rwkv_shift_mix, v_w0=v_w0, v_w_decay_up=v_w_decay_up, v_a0=v_a0, v_w_iclr_up=v_w_iclr_up, v_w_gate_up=v_w_gate_up, v_k_k=v_k_k, v_k_a=v_k_a, v_r_k=v_r_k, v_ln_x_g=v_ln_x_g, v_ln_x_b=v_ln_x_b, v_w_out=v_w_out, v_w_ffn_up=v_w_ffn_up, v_conv_w=v_conv_w, v_conv_b=v_conv_b, v_w_ffn_down=v_w_ffn_down)
    weights = {n: given[n] for n in TWIN_WEIGHTS}
    shared = {n: given[n] for n in SHARED_INPUTS}
    per_example = {n: given[n] for n in ['x']}
    grad_fn = _jax.value_and_grad(_loss, argnums=(0, 1))

    def one_microbatch(ex, loss_target):
        ex = dict(ex)
        diff = ex.pop(TWIN_DIFF_INPUT)
        return grad_fn(weights, diff, {**shared, **ex}, loss_target)

    if N_MICROBATCH == 1:
        loss, (grad_w, grad_x) = one_microbatch(per_example, given["loss_target"])
    else:
        def body(carry, xs):
            loss_sum, grad_sum = carry
            l_k, (gw_k, gx_k) = one_microbatch(xs[0], xs[1])
            with _jax.named_scope("update"):
                return (loss_sum + l_k, _jax.tree.map(_jnp.add, grad_sum, gw_k)), gx_k

        init = (_jnp.zeros((), _jnp.float32), _jax.tree.map(_jnp.zeros_like, weights))
        (loss, grad_w), grad_x = _jax.lax.scan(body, init, (per_example, given["loss_target"]))
    with _jax.named_scope("update"):
        delta_w, new_m, new_v = {}, {}, {}
        for n in TWIN_WEIGHTS:
            delta_w[n], new_m[n], new_v[n] = _adamw(weights[n], grad_w[n], given["m_" + n], given["v_" + n])
    return (loss, grad_x, *[grad_w[n] for n in TWIN_WEIGHTS], *[delta_w[n] for n in TWIN_WEIGHTS],
            *[new_m[n] for n in TWIN_WEIGHTS], *[new_v[n] for n in TWIN_WEIGHTS])
```

```python
import functools
import math

import numpy as np
import jax
import jax.numpy as jnp
from jax import lax
from jax.experimental import pallas as pl
from jax.experimental.pallas import tpu as pltpu

F32 = jnp.float32
BF16 = jnp.bfloat16
MESH = pl.DeviceIdType.MESH

SEQ = 2048
D_MODEL = 1024
HEAD_DIM = 64
D_ATTN = 512
D_RWKV = 512
D_KV = 128
N_Q_HEADS = 8
N_KV_HEADS = 2
Q_PER_KV = 4
BLOCK = 128
N_BUCKETS = 32
MAX_DISTANCE = 128
LORA_DECAY = 64
LORA_ICLR = 64
LORA_GATE = 128
RWKV_COLS = 3 * D_RWKV + LORA_DECAY + LORA_ICLR + LORA_GATE
P_OFF = D_ATTN + 2 * D_KV
D_IN = P_OFF + RWKV_COLS
D_FF = 4096
NORM_EPS = 1e-6
GN_EPS = 64e-5
NEG_INF = -1e30
N_CHIPS = 4
N_DEV = 8

ADAM_LR = 0.001
ADAM_B1 = 0.9
ADAM_B2 = 0.999
ADAM_EPS = 1e-08
ADAM_WD = 0.01
ADAM_STEP = 10

VMEM_LIMIT = 52 * 1024 * 1024
LANES = 128
SCAN_T = 64


def _cp(sem=None, vmem=VMEM_LIMIT):
    kw = dict(vmem_limit_bytes=vmem)
    if sem is not None:
        kw["dimension_semantics"] = sem
    return pltpu.CompilerParams(**kw)


def _rows(tr, nc):
    return pl.BlockSpec((tr, nc), lambda i: (i, 0))


def _const(shape):
    return pl.BlockSpec(shape, lambda *_: (0,) * len(shape))


def _split(x, n):
    parts = []
    for _ in range(n - 1):
        h = x.astype(BF16)
        parts.append(h)
        x = x - h.astype(F32)
    parts.append(x.astype(BF16))
    return parts


def _dot(a, b, dn=(((1,), (0,)), ((), ()))):
    return lax.dot_general(a, b, dn, preferred_element_type=F32)


NN = (((1,), (0,)), ((), ()))
NT = (((1,), (1,)), ((), ()))
TN = (((0,), (0,)), ((), ()))


def _dot_ind(x, ind_bf16, n=3):
    acc = None
    for part in _split(x, n):
        t = _dot(part, ind_bf16)
        acc = t if acc is None else acc + t
    return acc


def _head_ones(n, scale=1.0):
    r = lax.broadcasted_iota(jnp.int32, (n, n), 0) >> 6
    c = lax.broadcasted_iota(jnp.int32, (n, n), 1) >> 6
    return jnp.where(r == c, 1.0, 0.0).astype(BF16)


def _matmul(a, b, mode, name, *, m, n, k, tm, tn, tk, a_spec=None, b_spec=None, out=None, out_dtype=F32):
    nk = k // tk
    dn = {"nn": NN, "nt": NT, "tn": TN}[mode]

    def body(a_ref, b_ref, o_ref, *scratch):
        part = _dot(a_ref[...], b_ref[...], dn)
        if nk == 1:
            o_ref[...] = part.astype(out_dtype)
        else:
            acc_ref, = scratch
            kk = pl.program_id(2)

            @pl.when(kk == 0)
            def _():
                acc_ref[...] = part

            @pl.when(kk > 0)
            def _():
                acc_ref[...] += part

            @pl.when(kk == nk - 1)
            def _():
                o_ref[...] = acc_ref[...].astype(out_dtype)

    if a_spec is None:
        a_spec = (pl.BlockSpec((tk, tm), lambda i, j, kk: (kk, i)) if mode == "tn"
                  else pl.BlockSpec((tm, tk), lambda i, j, kk: (i, kk)))
    if b_spec is None:
        b_spec = (pl.BlockSpec((tn, tk), lambda i, j, kk: (j, kk)) if mode == "nt"
                  else pl.BlockSpec((tk, tn), lambda i, j, kk: (kk, j)))
    return pl.pallas_call(
        body, name=name, grid=(m // tm, n // tn, nk),
        in_specs=[a_spec, b_spec],
        out_specs=pl.BlockSpec((tm, tn), lambda i, j, kk: (i, j)) if out is None else out[1],
        out_shape=jax.ShapeDtypeStruct((m, n) if out is None else out[0], out_dtype),
        scratch_shapes=[] if nk == 1 else [pltpu.VMEM((tm, tn), F32)],
        compiler_params=_cp(("parallel", "parallel", "arbitrary")),
    )(a, b)


def _rstd(x):
    return lax.rsqrt(jnp.mean(x * x, axis=-1, keepdims=True) + NORM_EPS)


def _rms_bwd(x, r, g, dy):
    gy = dy * g
    return r * gy - x * ((r * r * r) * (jnp.sum(x * gy, axis=-1, keepdims=True) / x.shape[-1]))


TR = 256


def _norm_cast(x, g, name):
    def body(x_ref, g_ref, h_ref):
        x = x_ref[...]
        h_ref[...] = (x * _rstd(x) * g_ref[...]).astype(BF16)

    return pl.pallas_call(
        body, name=name, grid=(SEQ // TR,),
        in_specs=[_rows(TR, D_MODEL), _const((1, D_MODEL))],
        out_specs=_rows(TR, D_MODEL),
        out_shape=jax.ShapeDtypeStruct((SEQ, D_MODEL), BF16),
        compiler_params=_cp(("parallel",)),
    )(x, g)


def _mix_norm(x, mix, g2, g3):
    def body(x_ref, mix_ref, g2_ref, g3_ref, x2_ref, h3_ref):
        mixv = mix_ref[...]
        x2 = x_ref[...] + mixv * _rstd(mixv) * g2_ref[...]
        x2_ref[...] = x2
        h3_ref[...] = (x2 * _rstd(x2) * g3_ref[...]).astype(BF16)

    return pl.pallas_call(
        body, name="mix_norm", grid=(SEQ // TR,),
        in_specs=[_rows(TR, D_MODEL), _rows(TR, D_MODEL), _const((1, D_MODEL)), _const((1, D_MODEL))],
        out_specs=[_rows(TR, D_MODEL), _rows(TR, D_MODEL)],
        out_shape=[jax.ShapeDtypeStruct((SEQ, D_MODEL), F32), jax.ShapeDtypeStruct((SEQ, D_MODEL), BF16)],
        compiler_params=_cp(("parallel",)),
    )(x, mix, g2, g3)


def _loss_head(x2, f, g4, target):
    def body(x2_ref, f_ref, g4_ref, t_ref, loss_ref, dy_ref, df_ref, dg_ref):
        i = pl.program_id(0)
        f = f_ref[...]
        g4 = g4_ref[...]
        r = _rstd(f)
        e = x2_ref[...] + f * r * g4 - t_ref[...]
        dy = e * (1.0 / D_MODEL)
        dy_ref[...] = dy
        df_ref[...] = _rms_bwd(f, r, g4, dy).astype(BF16)
        part = 0.5 * jnp.sum(jnp.sum(e * e, axis=-1, keepdims=True), axis=0, keepdims=True) * (1.0 / D_MODEL)
        dg = jnp.sum(dy * f * r, axis=0, keepdims=True)

        @pl.when(i == 0)
        def _():
            loss_ref[...] = jnp.zeros_like(loss_ref)
            dg_ref[...] = jnp.zeros_like(dg_ref)

        loss_ref[...] += jnp.broadcast_to(part, loss_ref.shape)
        dg_ref[...] += dg

    return pl.pallas_call(
        body, name="loss_head", grid=(SEQ // TR,),
        in_specs=[_rows(TR, D_MODEL), _rows(TR, D_MODEL), _const((1, D_MODEL)), _rows(TR, D_MODEL)],
        out_specs=[_const((8, LANES)), _rows(TR, D_MODEL), _rows(TR, D_MODEL), _const((1, D_MODEL))],
        out_shape=[jax.ShapeDtypeStruct((8, LANES), F32), jax.ShapeDtypeStruct((SEQ, D_MODEL), F32),
                   jax.ShapeDtypeStruct((SEQ, D_MODEL), BF16), jax.ShapeDtypeStruct((1, D_MODEL), F32)],
        compiler_params=_cp(("arbitrary",)),
    )(x2, f, g4, target)


def _mid_bwd(x2, mix, dy, dh3, g2, g3):
    def body(x2_ref, mix_ref, dy_ref, dh3_ref, g2_ref, g3_ref, dx2_ref, dmix_ref, dg2_ref, dg3_ref):
        i = pl.program_id(0)
        x2 = x2_ref[...]
        mixv = mix_ref[...]
        dh3 = dh3_ref[...]
        r3 = _rstd(x2)
        dx2 = dy_ref[...] + _rms_bwd(x2, r3, g3_ref[...], dh3)
        dx2_ref[...] = dx2
        r2 = _rstd(mixv)
        dmix_ref[...] = _rms_bwd(mixv, r2, g2_ref[...], dx2).astype(BF16)

        @pl.when(i == 0)
        def _():
            dg2_ref[...] = jnp.zeros_like(dg2_ref)
            dg3_ref[...] = jnp.zeros_like(dg3_ref)

        dg3_ref[...] += jnp.sum(dh3 * x2 * r3, axis=0, keepdims=True)
        dg2_ref[...] += jnp.sum(dx2 * mixv * r2, axis=0, keepdims=True)

    return pl.pallas_call(
        body, name="mid_bwd", grid=(SEQ // TR,),
        in_specs=[_rows(TR, D_MODEL)] * 4 + [_const((1, D_MODEL))] * 2,
        out_specs=[_rows(TR, D_MODEL), _rows(TR, D_MODEL), _const((1, D_MODEL)), _const((1, D_MODEL))],
        out_shape=[jax.ShapeDtypeStruct((SEQ, D_MODEL), F32), jax.ShapeDtypeStruct((SEQ, D_MODEL), BF16),
                   jax.ShapeDtypeStruct((1, D_MODEL), F32), jax.ShapeDtypeStruct((1, D_MODEL), F32)],
        compiler_params=_cp(("arbitrary",)),
    )(x2, mix, dy, dh3, g2, g3)


def _first_bwd(x, dx2, dh1, g1):
    def body(x_ref, dx2_ref, dh1_ref, g1_ref, dx_ref, dg1_ref):
        i = pl.program_id(0)
        x = x_ref[...]
        dh1 = dh1_ref[...]
        r = _rstd(x)
        dx_ref[...] = dx2_ref[...] + _rms_bwd(x, r, g1_ref[...], dh1)

        @pl.when(i == 0)
        def _():
            dg1_ref[...] = jnp.zeros_like(dg1_ref)

        dg1_ref[...] += jnp.sum(dh1 * x * r, axis=0, keepdims=True)

    return pl.pallas_call(
        body, name="first_bwd", grid=(SEQ // TR,),
        in_specs=[_rows(TR, D_MODEL)] * 3 + [_const((1, D_MODEL))],
        out_specs=[_rows(TR, D_MODEL), _const((1, D_MODEL))],
        out_shape=[jax.ShapeDtypeStruct((SEQ, D_MODEL), F32), jax.ShapeDtypeStruct((1, D_MODEL), F32)],
        compiler_params=_cp(("arbitrary",)),
    )(x, dx2, dh1, g1)


TC = 256
N_CB = D_FF // TC
GELU_C = math.sqrt(2.0 / math.pi)


def _shift_down(u, s):
    rolled = pltpu.roll(u, s, 0)
    row = lax.broadcasted_iota(jnp.int32, u.shape, 0)
    return jnp.where(row >= s, rolled, 0.0)


def _shift_up(u, s):
    n = u.shape[0]
    rolled = pltpu.roll(u, n - s, 0)
    row = lax.broadcasted_iota(jnp.int32, u.shape, 0)
    return jnp.where(row < n - s, rolled, 0.0)


def _conv3(u, w, b):
    return b + w[0:1] * _shift_down(u, 2) + w[1:2] * _shift_down(u, 1) + w[2:3] * u


def _gelu_and_grad(x):
    inner = GELU_C * (x + 0.044715 * (x * x * x))
    t = jnp.tanh(inner)
    gelu = 0.5 * x * (1.0 + t)
    dgelu = 0.5 * (1.0 + t) + 0.5 * x * (1.0 - t * t) * (GELU_C * (1.0 + 3 * 0.044715 * (x * x)))
    return gelu, dgelu


def _ffn_specs():
    col = lambda off: pl.BlockSpec((SEQ, TC), lambda *g: (0, g[-1] + off))
    w = lambda off: pl.BlockSpec((3, TC), lambda *g: (0, g[-1] + off))
    b = lambda off: pl.BlockSpec((1, TC), lambda *g: (0, g[-1] + off))
    return col, w, b


def _ffn_act(u, conv_w, conv_b):
    col, w, b = _ffn_specs()

    def body(ug_ref, uv_ref, wg_ref, wv_ref, bg_ref, bv_ref, act_ref):
        gate = _conv3(ug_ref[...], wg_ref[...], bg_ref[...])
        val = _conv3(uv_ref[...], wv_ref[...], bv_ref[...])
        act_ref[...] = (_gelu_and_grad(gate)[0] * val).astype(BF16)

    return pl.pallas_call(
        body, name="ffn_act", grid=(N_CB,),
        in_specs=[col(0), col(N_CB), w(0), w(N_CB), b(0), b(N_CB)],
        out_specs=col(0),
        out_shape=jax.ShapeDtypeStruct((SEQ, D_FF), BF16),
        compiler_params=_cp(("parallel",)),
    )(u, u, conv_w, conv_w, conv_b, conv_b)


def _ffn_act_bwd(u, dact, conv_w, conv_b):
    col, w, b = _ffn_specs()
    half = lambda shape: pl.BlockSpec(shape, lambda h, j: (0, h * N_CB + j))

    def body(ug_ref, uv_ref, da_ref, wg_ref, wv_ref, bg_ref, bv_ref, du_ref, dw_ref, db_ref):
        h = pl.program_id(0)
        is_gate = h == 0
        ug = ug_ref[...]
        uv = uv_ref[...]
        gate = _conv3(ug, wg_ref[...], bg_ref[...])
        val = _conv3(uv, wv_ref[...], bv_ref[...])
        gelu, dgelu = _gelu_and_grad(gate)
        da = da_ref[...]
        duc = jnp.where(is_gate, da * val * dgelu, da * gelu)
        usel = jnp.where(is_gate, ug, uv)
        wsel = jnp.where(is_gate, wg_ref[...], wv_ref[...])
        du = wsel[2:3] * duc + wsel[1:2] * _shift_up(duc, 1) + wsel[0:1] * _shift_up(duc, 2)
        du_ref[...] = du.astype(BF16)
        db_ref[...] = jnp.sum(duc, axis=0, keepdims=True)
        dw_ref[...] = jnp.concatenate(
            [jnp.sum(duc * _shift_down(usel, 2), axis=0, keepdims=True),
             jnp.sum(duc * _shift_down(usel, 1), axis=0, keepdims=True),
             jnp.sum(duc * usel, axis=0, keepdims=True)], axis=0)

    return pl.pallas_call(
        body, name="ffn_act_bwd", grid=(2, N_CB),
        in_specs=[col(0), col(N_CB), col(0), w(0), w(N_CB), b(0), b(N_CB)],
        out_specs=[half((SEQ, TC)), half((3, TC)), half((1, TC))],
        out_shape=[jax.ShapeDtypeStruct((SEQ, 2 * D_FF), BF16), jax.ShapeDtypeStruct((3, 2 * D_FF), F32),
                   jax.ShapeDtypeStruct((1, 2 * D_FF), F32)],
        compiler_params=_cp(("parallel", "parallel")),
    )(u, u, dact, conv_w, conv_w, conv_b, conv_b)


def _t5_onehot():
    rel = (np.arange(BLOCK)[:, None] + BLOCK) - np.arange(2 * BLOCK)[None, :]
    n = np.maximum(rel, 0)
    max_exact = N_BUCKETS // 2
    large = max_exact + (np.log(np.maximum(n, 1).astype(np.float32) / np.float32(max_exact))
                         / np.float32(math.log(MAX_DISTANCE / max_exact))
                         * np.float32(N_BUCKETS - max_exact)).astype(np.int32)
    large = np.minimum(large, N_BUCKETS - 1)
    bucket = np.where(n < max_exact, n, large).reshape(-1)
    return (bucket[None, :] == np.arange(N_BUCKETS)[:, None]).astype(np.float32)


N_REL = BLOCK * 2 * BLOCK


def _bias_table(rel_bias_t, onehot):
    def body(rb_ref, oh_ref, o_ref):
        o_ref[...] = _dot_ind(rb_ref[...], oh_ref[...])

    return pl.pallas_call(
        body, name="bias_table", grid=(1,),
        in_specs=[_const((N_Q_HEADS, N_BUCKETS)), _const((N_BUCKETS, N_REL))],
        out_specs=_const((N_Q_HEADS, N_REL)),
        out_shape=jax.ShapeDtypeStruct((N_Q_HEADS, N_REL), F32),
        compiler_params=_cp(("arbitrary",)),
    )(rel_bias_t, onehot)


def _bias_table_bwd(dbias, onehot):
    def body(db_ref, oh_ref, o_ref):
        acc = None
        for part in _split(db_ref[...], 3):
            t = _dot(part, oh_ref[...], NT)
            acc = t if acc is None else acc + t
        o_ref[...] = acc

    return pl.pallas_call(
        body, name="bias_table_bwd", grid=(1,),
        in_specs=[_const((N_Q_HEADS, N_REL)), _const((N_BUCKETS, N_REL))],
        out_specs=_const((N_Q_HEADS, N_BUCKETS)),
        out_shape=jax.ShapeDtypeStruct((N_Q_HEADS, N_BUCKETS), F32),
        compiler_params=_cp(("arbitrary",)),
    )(dbias, onehot)


def _attn_pieces(n, q, kvp, kvc, bias_ref, sinks_ref, hk):
    qi = lax.broadcasted_iota(jnp.int32, (BLOCK, 2 * BLOCK), 0)
    kj = lax.broadcasted_iota(jnp.int32, (BLOCK, 2 * BLOCK), 1)
    rel = qi + BLOCK - kj
    first_key = jnp.where(n > 0, 0, BLOCK)
    ok = jnp.where(rel >= 0, jnp.where(rel < BLOCK, jnp.where(kj >= first_key, 1.0, 0.0), 0.0), 0.0)
    ok4 = jnp.concatenate([ok] * Q_PER_KV, axis=0) > 0.5
    c0 = hk * HEAD_DIM
    kcat = jnp.concatenate([kvp[:, c0:c0 + HEAD_DIM], kvc[:, c0:c0 + HEAD_DIM]], axis=0).astype(BF16)
    vcat = jnp.concatenate([kvp[:, D_KV + c0:D_KV + c0 + HEAD_DIM], kvc[:, D_KV + c0:D_KV + c0 + HEAD_DIM]],
                           axis=0).astype(BF16)
    q0 = hk * Q_PER_KV * HEAD_DIM
    qs = jnp.concatenate([q[:, q0 + g * HEAD_DIM:q0 + (g + 1) * HEAD_DIM] for g in range(Q_PER_KV)],
                         axis=0).astype(BF16)
    s = _dot(qs, kcat, NT) * (HEAD_DIM ** -0.5) + bias_ref[hk]
    s = jnp.where(ok4, s, NEG_INF)
    row = lax.broadcasted_iota(jnp.int32, (Q_PER_KV * BLOCK, 1), 0)
    sink = jnp.zeros((Q_PER_KV * BLOCK, 1), F32)
    for g in range(Q_PER_KV):
        sink = jnp.where((row >> 7) == g, sinks_ref[hk * Q_PER_KV + g], sink)
    m = jnp.maximum(jnp.max(s, axis=-1, keepdims=True), sink)
    p = jnp.exp(s - m)
    es = jnp.exp(sink - m)
    inv = 1.0 / (jnp.sum(p, axis=-1, keepdims=True) + es)
    return qs, kcat, vcat, p * inv, es * inv


def _attn_in_specs():
    return [pl.BlockSpec((BLOCK, D_ATTN), lambda n: (n, 0)),
            pl.BlockSpec((BLOCK, 2 * D_KV), lambda n: (jnp.maximum(n - 1, 0), D_ATTN // (2 * D_KV))),
            pl.BlockSpec((BLOCK, 2 * D_KV), lambda n: (n, D_ATTN // (2 * D_KV))),
            _const((N_KV_HEADS, Q_PER_KV * BLOCK, 2 * BLOCK)),
            pl.BlockSpec(memory_space=pltpu.SMEM)]


def _unstack_heads(t):
    return jnp.concatenate([t[g * BLOCK:(g + 1) * BLOCK] for g in range(Q_PER_KV)], axis=1)


def _attn_fwd(proj, bias, sinks):
    def body(q_ref, kvp_ref, kvc_ref, bias_ref, sinks_ref, o_ref):
        n = pl.program_id(0)
        q, kvp, kvc = q_ref[...], kvp_ref[...], kvc_ref[...]
        outs = []
        for hk in range(N_KV_HEADS):
            _, _, vcat, probs, _ = _attn_pieces(n, q, kvp, kvc, bias_ref, sinks_ref, hk)
            outs.append(_unstack_heads(_dot(probs.astype(BF16), vcat)))
        o_ref[...] = jnp.concatenate(outs, axis=1)

    return pl.pallas_call(
        body, name="attn_fwd", grid=(SEQ // BLOCK,),
        in_specs=_attn_in_specs(),
        out_specs=pl.BlockSpec((BLOCK, D_ATTN), lambda n: (n, 0)),
        out_shape=jax.ShapeDtypeStruct((SEQ, D_ATTN), F32),
        compiler_params=_cp(("parallel",)),
    )(proj, proj, proj, bias, sinks)


def _attn_bwd(proj, bias, sinks, dcat):
    nb = SEQ // BLOCK

    def body(q_ref, kvp_ref, kvc_ref, bias_ref, sinks_ref, do_ref, dq_ref, dkv_ref, dbias_ref, dsink_ref, dsacc):
        n = pl.program_id(0)

        @pl.when(n == 0)
        def _():
            dkv_ref[...] = jnp.zeros_like(dkv_ref)
            dbias_ref[...] = jnp.zeros_like(dbias_ref)
            dsacc[...] = jnp.zeros_like(dsacc)

        q, kvp, kvc = q_ref[...], kvp_ref[...], kvc_ref[...]
        do_all = do_ref[...]
        dqs, dks, dvs = [], [], []
        for hk in range(N_KV_HEADS):
            qs, kcat, vcat, probs, psink = _attn_pieces(n, q, kvp, kvc, bias_ref, sinks_ref, hk)
            q0 = hk * Q_PER_KV * HEAD_DIM
            do = jnp.concatenate([do_all[:, q0 + g * HEAD_DIM:q0 + (g + 1) * HEAD_DIM] for g in range(Q_PER_KV)],
                                 axis=0).astype(BF16)
            dprobs = _dot(do, vcat, NT)
            dvs.append(_dot(probs.astype(BF16), do, TN))
            rowdot = jnp.sum(probs * dprobs, axis=-1, keepdims=True)
            ds = probs * (dprobs - rowdot)
            dsacc[hk] += -psink * rowdot
            dbias_ref[hk] += ds
            dsb = (ds * (HEAD_DIM ** -0.5)).astype(BF16)
            dqs.append(_unstack_heads(_dot(dsb, kcat)))
            dks.append(_dot(dsb, qs, TN))
        dq_ref[...] = jnp.concatenate(dqs, axis=1)
        upd = jnp.concatenate(dks + dvs, axis=1)
        cur = pl.multiple_of(n * BLOCK, BLOCK)
        dkv_ref[pl.ds(cur, BLOCK), :] += upd[BLOCK:]

        @pl.when(n > 0)
        def _():
            prev = pl.multiple_of((n - 1) * BLOCK, BLOCK)
            dkv_ref[pl.ds(prev, BLOCK), :] += upd[:BLOCK]

        @pl.when(n == nb - 1)
        def _():
            for hk in range(N_KV_HEADS):
                for g in range(Q_PER_KV):
                    tot = jnp.sum(dsacc[hk, g * BLOCK:(g + 1) * BLOCK, :], axis=0, keepdims=True)
                    h = hk * Q_PER_KV + g
                    dsink_ref[h:h + 1, :] = jnp.broadcast_to(tot, (1, LANES))

    return pl.pallas_call(
        body, name="attn_bwd", grid=(nb,),
        in_specs=_attn_in_specs() + [pl.BlockSpec((BLOCK, D_ATTN), lambda n: (n, 0))],
        out_specs=[pl.BlockSpec((BLOCK, D_ATTN), lambda n: (n, 0)), _const((SEQ, 2 * D_KV)),
                   _const((N_KV_HEADS, Q_PER_KV * BLOCK, 2 * BLOCK)), _const((N_Q_HEADS, LANES))],
        out_shape=[jax.ShapeDtypeStruct((SEQ, D_ATTN), F32), jax.ShapeDtypeStruct((SEQ, 2 * D_KV), F32),
                   jax.ShapeDtypeStruct((N_KV_HEADS, Q_PER_KV * BLOCK, 2 * BLOCK), F32),
                   jax.ShapeDtypeStruct((N_Q_HEADS, LANES), F32)],
        scratch_shapes=[pltpu.VMEM((N_KV_HEADS, Q_PER_KV * BLOCK, 1), F32)],
        compiler_params=_cp(("arbitrary",)),
    )(proj, proj, proj, bias, sinks, dcat)


@jax.custom_vjp
def _head_sum(x):
    return _dot_ind(x, _head_ones(x.shape[-1]))


_head_sum.defvjp(lambda x: (_head_sum(x), None), lambda _, ct: (_head_sum(ct),))


@jax.custom_vjp
def _bdot(a, w):
    return _dot(a.astype(BF16), w.astype(BF16))


def _bdot_bwd(res, ct):
    a, w = res
    ctb = ct.astype(BF16)
    return _dot(ctb, w.astype(BF16), NT), _dot(a.astype(BF16), ctb, TN)


_bdot.defvjp(lambda a, w: (_bdot(a, w), (a, w)), _bdot_bwd)


def _sigmoid(x):
    return 0.5 * (jnp.tanh(0.5 * x) + 1.0)


def _softplus(x):
    return jnp.maximum(x, 0.0) + jnp.log(1.0 + jnp.exp(-jnp.abs(x)))


def _rwkv_core(r, k, v, zwa, zg, w0, wdu, a0, wiu, wgu, k_k, k_a):
    w_log = -_softplus(-(w0 + _bdot(jnp.tanh(zwa), wdu))) - 0.5
    decay = jnp.exp(-jnp.exp(w_log))
    a = _sigmoid(a0 + _bdot(zwa, wiu))
    g = _bdot(_sigmoid(zg), wgu)
    kk = k * k_k
    kk = kk / jnp.maximum(jnp.sqrt(_head_sum(kk * kk)), 1e-12)
    k2 = k * (1.0 + (a - 1.0) * k_a)
    return r, decay, k2, v, -kk, kk * a, g


def _rwkv_out(o, r, k2, v, g, lng, lnb, rk):
    mu = _head_sum(o) * (1.0 / HEAD_DIM)
    d = o - mu
    var = _head_sum(d * d) * (1.0 / HEAD_DIM)
    on = d * lax.rsqrt(var + GN_EPS) * lng + lnb
    bonus = _head_sum(r * k2 * rk) * v
    return (on + bonus) * g


P_SPLITS = (0, 512, 1024, 1536, 1664, 1792)
N_PREP_PARAMS = 7
HALO = 8


def _shifted_pieces(i, p_ref, halo_ref, mix_ref):
    p = p_ref[:, P_OFF:]
    prev_row = halo_ref[HALO - 1:HALO, P_OFF:] * jnp.where(i > 0, 1.0, 0.0)
    row = lax.broadcasted_iota(jnp.int32, p.shape, 0)
    pprev = jnp.where(row == 0, prev_row, pltpu.roll(p, 1, 0))
    delta = pprev - p
    ps = p + delta * mix_ref[...]
    return [ps[:, a:b] for a, b in zip(P_SPLITS[:-1], P_SPLITS[1:])], delta


def _prep_in_specs():
    return [_rows(TR, D_IN),
            pl.BlockSpec((HALO, D_IN), lambda i: (jnp.maximum(i * (TR // HALO) - 1, 0), 0)),
            _const((1, RWKV_COLS)), _const((1, D_RWKV)), _const((LANES, D_RWKV)), _const((1, D_RWKV)),
            _const((LANES, D_RWKV)), _const((LANES, D_RWKV)), _const((1, D_RWKV)), _const((1, D_RWKV))]


def _rwkv_prep(proj, mix, prm):
    def body(p_ref, halo_ref, mix_ref, *refs):
        prm_refs, outs = refs[:N_PREP_PARAMS], refs[N_PREP_PARAMS:]
        pieces, _ = _shifted_pieces(pl.program_id(0), p_ref, halo_ref, mix_ref)
        vals = _rwkv_core(*pieces, *[t[...] for t in prm_refs])
        for ref, val in zip(outs, vals):
            ref[...] = val

    return pl.pallas_call(
        body, name="rwkv_prep", grid=(SEQ // TR,),
        in_specs=_prep_in_specs(),
        out_specs=[_rows(TR, D_RWKV)] * 7,
        out_shape=[jax.ShapeDtypeStruct((SEQ, D_RWKV), F32)] * 7,
        compiler_params=_cp(("parallel",)),
    )(proj, proj, mix, *prm)


def _rwkv_prep_bwd(proj, mix, prm, cts):
    def body(p_ref, halo_ref, mix_ref, *refs):
        i = pl.program_id(0)
        prm_refs = refs[:N_PREP_PARAMS]
        ct_refs = refs[N_PREP_PARAMS:N_PREP_PARAMS + 10]
        dps_ref, dmix_ref = refs[N_PREP_PARAMS + 10:N_PREP_PARAMS + 12]
        dprm_refs = refs[N_PREP_PARAMS + 12:]
        pieces, delta = _shifted_pieces(i, p_ref, halo_ref, mix_ref)
        _, vjp = jax.vjp(_rwkv_core, *pieces, *[t[...] for t in prm_refs])
        dr1, dr2, dw, dk1, dk2, dv1, dv2, dkkn, db, dg = [t[...] for t in ct_refs]
        grads = vjp((dr1 + dr2, dw, dk1 + dk2, dv1 + dv2, dkkn, db, dg))
        dps = jnp.concatenate(grads[:5], axis=1)
        dps_ref[...] = dps

        @pl.when(i == 0)
        def _():
            dmix_ref[...] = jnp.zeros_like(dmix_ref)
            for ref in dprm_refs:
                ref[...] = jnp.zeros_like(ref)

        dmix_ref[...] += jnp.sum(dps * delta, axis=0, keepdims=True)
        for ref, gval in zip(dprm_refs, grads[5:]):
            ref[...] += gval

    prm_shapes = [(1, D_RWKV), (LANES, D_RWKV), (1, D_RWKV), (LANES, D_RWKV), (LANES, D_RWKV), (1, D_RWKV), (1, D_RWKV)]
    return pl.pallas_call(
        body, name="rwkv_prep_bwd", grid=(SEQ // TR,),
        in_specs=_prep_in_specs() + [_rows(TR, D_RWKV)] * 10,
        out_specs=[_rows(TR, RWKV_COLS), _const((1, RWKV_COLS))] + [_const(s) for s in prm_shapes],
        out_shape=[jax.ShapeDtypeStruct((SEQ, RWKV_COLS), F32), jax.ShapeDtypeStruct((1, RWKV_COLS), F32)]
        + [jax.ShapeDtypeStruct(s, F32) for s in prm_shapes],
        compiler_params=_cp(("arbitrary",)),
    )(proj, proj, mix, *prm, *cts)


def _rwkv_post(o, r, k2, v, g, lng, lnb, rk, attn):
    def body(o_ref, r_ref, k_ref, v_ref, g_ref, lng_ref, lnb_ref, rk_ref, attn_ref, cat_ref):
        rw = _rwkv_out(*[t[...] for t in (o_ref, r_ref, k_ref, v_ref, g_ref, lng_ref, lnb_ref, rk_ref)])
        cat_ref[...] = jnp.concatenate([attn_ref[...], rw], axis=1).astype(BF16)

    return pl.pallas_call(
        body, name="rwkv_post", grid=(SEQ // TR,),
        in_specs=[_rows(TR, D_RWKV)] * 5 + [_const((1, D_RWKV))] * 3 + [_rows(TR, D_ATTN)],
        out_specs=_rows(TR, D_MODEL),
        out_shape=jax.ShapeDtypeStruct((SEQ, D_MODEL), BF16),
        compiler_params=_cp(("parallel",)),
    )(o, r, k2, v, g, lng, lnb, rk, attn)


def _rwkv_post_bwd(o, r, k2, v, g, lng, lnb, rk, dcat):
    def body(o_ref, r_ref, k_ref, v_ref, g_ref, lng_ref, lnb_ref, rk_ref, dcat_ref,
             do_ref, dr_ref, dk_ref, dv_ref, dg_ref, dlng_ref, dlnb_ref, drk_ref):
        i = pl.program_id(0)
        args = [t[...] for t in (o_ref, r_ref, k_ref, v_ref, g_ref, lng_ref, lnb_ref, rk_ref)]
        _, vjp = jax.vjp(_rwkv_out, *args)
        grads = vjp(dcat_ref[:, D_ATTN:])
        for ref, gval in zip((do_ref, dr_ref, dk_ref, dv_ref, dg_ref), grads[:5]):
            ref[...] = gval

        @pl.when(i == 0)
        def _():
            for ref in (dlng_ref, dlnb_ref, drk_ref):
                ref[...] = jnp.zeros_like(ref)

        for ref, gval in zip((dlng_ref, dlnb_ref, drk_ref), grads[5:]):
            ref[...] += gval

    return pl.pallas_call(
        body, name="rwkv_post_bwd", grid=(SEQ // TR,),
        in_specs=[_rows(TR, D_RWKV)] * 5 + [_const((1, D_RWKV))] * 3 + [_rows(TR, D_MODEL)],
        out_specs=[_rows(TR, D_RWKV)] * 5 + [_const((1, D_RWKV))] * 3,
        out_shape=[jax.ShapeDtypeStruct((SEQ, D_RWKV), F32)] * 5 + [jax.ShapeDtypeStruct((1, D_RWKV), F32)] * 3,
        compiler_params=_cp(("arbitrary",)),
    )(o, r, k2, v, g, lng, lnb, rk, dcat)


def _assemble_dproj(dq, dkv, dps, mix):
    last = SEQ // HALO - 1

    def body(dq_ref, dkv_ref, dps_ref, nxt_ref, mix_ref, o_ref):
        i = pl.program_id(0)
        dps = dps_ref[...]
        mixv = mix_ref[...]
        nxt_row = nxt_ref[0:1, :] * jnp.where(i < SEQ // TR - 1, 1.0, 0.0)
        row = lax.broadcasted_iota(jnp.int32, dps.shape, 0)
        up = jnp.where(row == TR - 1, nxt_row, pltpu.roll(dps, TR - 1, 0))
        dp = dps * (1.0 - mixv) + up * mixv
        o_ref[...] = jnp.concatenate([dq_ref[...], dkv_ref[...], dp], axis=1).astype(BF16)

    return pl.pallas_call(
        body, name="assemble_dproj", grid=(SEQ // TR,),
        in_specs=[_rows(TR, D_ATTN), _rows(TR, 2 * D_KV), _rows(TR, RWKV_COLS),
                  pl.BlockSpec((HALO, RWKV_COLS), lambda i: (jnp.minimum((i + 1) * (TR // HALO), last), 0)),
                  _const((1, RWKV_COLS))],
        out_specs=_rows(TR, D_IN),
        out_shape=jax.ShapeDtypeStruct((SEQ, D_IN), BF16),
        compiler_params=_cp(("parallel",)),
    )(dq, dkv, dps, dps, mix)


N_PAIR = D_RWKV // LANES
CHUNK = 2 * SCAN_T
N_CHUNK = SEQ // CHUNK
GROUP = 8


def _seg_sum(x, ones2):
    return _dot(jnp.concatenate(_split(x, 2), axis=1), ones2)


def _row_dot(s, vec_row, head_rows):
    a = jnp.broadcast_to(vec_row, head_rows.shape) * head_rows
    a_hi, a_lo = _split(a, 2)
    s_hi, s_lo = _split(s, 2)
    x = _dot(a_hi, s_hi, NT) + _dot(a_lo, s_hi, NT) + _dot(a_hi, s_lo, NT)
    return jnp.concatenate([x[0:1], x[1:2]], axis=1)


def _col_form(tile_t, t, lane_tile, lane_row):
    col = jnp.sum(jnp.where(lane_tile == t, tile_t, 0.0), axis=-1, keepdims=True)
    return jnp.where(lane_row < HEAD_DIM, col[:HEAD_DIM], col[HEAD_DIM:])


def _scan_consts():
    ones2 = jnp.concatenate([_head_ones(LANES)] * 2, axis=0)
    lane_tile = lax.broadcasted_iota(jnp.int32, (LANES, LANES), 1)
    lane_row = lax.broadcasted_iota(jnp.int32, (HEAD_DIM, LANES), 1)
    sub = lax.broadcasted_iota(jnp.int32, (16, LANES), 0)
    lane16 = lax.broadcasted_iota(jnp.int32, (16, LANES), 1)
    head_rows = jnp.where(sub == (lane16 >> 6), 1.0, 0.0)
    return ones2, lane_tile, lane_row, head_rows


def _pair(j):
    return slice(j * LANES, (j + 1) * LANES)


def _scan_fwd(r, w, k, v, kkn, b):
    def body(r_ref, w_ref, k_ref, v_ref, kkn_ref, b_ref, o_ref, ckpt_ref, s_scr, vt_scr):
        c = pl.program_id(0)
        ones2, lane_tile, lane_row, head_rows = _scan_consts()

        @pl.when(c == 0)
        def _():
            s_scr[...] = jnp.zeros_like(s_scr)

        for j in range(N_PAIR):
            vt_scr[j] = v_ref[:, _pair(j)].T

        for sub in range(CHUNK // SCAN_T):
            ckpt_ref[sub] = s_scr[...]

            def group(gi, states, sub=sub):
                row0 = pl.multiple_of(sub * SCAN_T + gi * GROUP, GROUP)
                states = list(states)
                for j in range(N_PAIR):
                    tiles = [t[pl.ds(row0, GROUP), _pair(j)] for t in (r_ref, w_ref, k_ref, kkn_ref, b_ref)]
                    s = states[j]
                    rows = []
                    for u in range(GROUP):
                        r_t, w_t, k_t, kkn_t, b_t = [t[u:u + 1] for t in tiles]
                        sa = _seg_sum(s * kkn_t, ones2)
                        vcol = _col_form(vt_scr[j], row0 + u, lane_tile, lane_row)
                        s = s * w_t + sa * b_t + vcol * k_t
                        rows.append(_row_dot(s, r_t, head_rows))
                    o_ref[pl.ds(row0, GROUP), _pair(j)] = jnp.concatenate(rows, axis=0)
                    states[j] = s
                return tuple(states)

            fin = lax.fori_loop(0, SCAN_T // GROUP, group, tuple(s_scr[j] for j in range(N_PAIR)))
            for j in range(N_PAIR):
                s_scr[j] = fin[j]

    blk = pl.BlockSpec((CHUNK, D_RWKV), lambda c: (c, 0))
    return pl.pallas_call(
        body, name="rwkv_scan_fwd", grid=(N_CHUNK,),
        in_specs=[blk] * 6,
        out_specs=[blk, pl.BlockSpec((CHUNK // SCAN_T, N_PAIR, HEAD_DIM, LANES), lambda c: (c, 0, 0, 0))],
        out_shape=[jax.ShapeDtypeStruct((SEQ, D_RWKV), F32),
                   jax.ShapeDtypeStruct((SEQ // SCAN_T, N_PAIR, HEAD_DIM, LANES), F32)],
        scratch_shapes=[pltpu.VMEM((N_PAIR, HEAD_DIM, LANES), F32), pltpu.VMEM((N_PAIR, LANES, LANES), F32)],
        compiler_params=_cp(("arbitrary",)),
    )(r, w, k, v, kkn, b)


def _scan_bwd(r, w, k, v, kkn, b, do, ckpt):
    def body(r_ref, w_ref, k_ref, v_ref, kkn_ref, b_ref, do_ref, ckpt_ref,
             dr_ref, dw_ref, dk_ref, dv_ref, dkkn_ref, db_ref, ds_scr, st_scr, sa_scr, vt_scr, dot_scr):
        i = pl.program_id(0)
        ones2, lane_tile, lane_row, head_rows = _scan_consts()

        @pl.when(i == 0)
        def _():
            ds_scr[...] = jnp.zeros_like(ds_scr)

        for j in range(N_PAIR):
            vt_scr[j] = v_ref[:, _pair(j)].T
            dot_scr[j] = do_ref[:, _pair(j)].T

        for sub in reversed(range(CHUNK // SCAN_T)):
            def recompute(gi, states, sub=sub):
                row0 = pl.multiple_of(sub * SCAN_T + gi * GROUP, GROUP)
                states = list(states)
                for j in range(N_PAIR):
                    tiles = [t[pl.ds(row0, GROUP), _pair(j)] for t in (w_ref, k_ref, kkn_ref, b_ref)]
                    s = states[j]
                    for u in range(GROUP):
                        w_t, k_t, kkn_t, b_t = [t[u:u + 1] for t in tiles]
                        sa = _seg_sum(s * kkn_t, ones2)
                        st_scr[gi * GROUP + u, j] = s
                        sa_scr[gi * GROUP + u, j] = sa
                        vcol = _col_form(vt_scr[j], row0 + u, lane_tile, lane_row)
                        s = s * w_t + sa * b_t + vcol * k_t
                    states[j] = s
                return tuple(states)

            fin = lax.fori_loop(0, SCAN_T // GROUP, recompute, tuple(ckpt_ref[sub, j] for j in range(N_PAIR)))
            for j in range(N_PAIR):
                st_scr[SCAN_T, j] = fin[j]

            def reverse(gr, dstates, sub=sub):
                gi = SCAN_T // GROUP - 1 - gr
                row0 = pl.multiple_of(sub * SCAN_T + gi * GROUP, GROUP)
                dstates = list(dstates)
                for j in range(N_PAIR):
                    tiles = [t[pl.ds(row0, GROUP), _pair(j)] for t in (r_ref, w_ref, k_ref, kkn_ref, b_ref)]
                    ds = dstates[j]
                    rows = [[None] * GROUP for _ in range(6)]
                    for u in reversed(range(GROUP)):
                        r_t, w_t, k_t, kkn_t, b_t = [t[u:u + 1] for t in tiles]
                        tl = gi * GROUP + u
                        s_prev = st_scr[tl, j]
                        s_next = st_scr[tl + 1, j]
                        sa = sa_scr[tl, j]
                        docol = _col_form(dot_scr[j], row0 + u, lane_tile, lane_row)
                        vcol = _col_form(vt_scr[j], row0 + u, lane_tile, lane_row)
                        ds = ds + docol * r_t
                        rows[0][u] = jnp.sum(s_next * docol, axis=0, keepdims=True)
                        rows[1][u] = jnp.sum(ds * s_prev, axis=0, keepdims=True)
                        rows[2][u] = jnp.sum(ds * vcol, axis=0, keepdims=True)
                        rows[3][u] = _row_dot(ds, k_t, head_rows)
                        dsa = _seg_sum(ds * b_t, ones2)
                        rows[4][u] = jnp.sum(s_prev * dsa, axis=0, keepdims=True)
                        rows[5][u] = jnp.sum(ds * sa, axis=0, keepdims=True)
                        ds = ds * w_t + dsa * kkn_t
                    for ref, rr in zip((dr_ref, dw_ref, dk_ref, dv_ref, dkkn_ref, db_ref), rows):
                        ref[pl.ds(row0, GROUP), _pair(j)] = jnp.concatenate(rr, axis=0)
                    dstates[j] = ds
                return tuple(dstates)

            dfin = lax.fori_loop(0, SCAN_T // GROUP, reverse, tuple(ds_scr[j] for j in range(N_PAIR)))
            for j in range(N_PAIR):
                ds_scr[j] = dfin[j]

    blk = pl.BlockSpec((CHUNK, D_RWKV), lambda i: (N_CHUNK - 1 - i, 0))
    state = (N_PAIR, HEAD_DIM, LANES)
    return pl.pallas_call(
        body, name="rwkv_scan_bwd", grid=(N_CHUNK,),
        in_specs=[blk] * 7 + [pl.BlockSpec((CHUNK // SCAN_T,) + state, lambda i: (N_CHUNK - 1 - i, 0, 0, 0))],
        out_specs=[blk] * 6,
        out_shape=[jax.ShapeDtypeStruct((SEQ, D_RWKV), F32)] * 6,
        scratch_shapes=[pltpu.VMEM(state, F32), pltpu.VMEM((SCAN_T + 1,) + state, F32),
                        pltpu.VMEM((SCAN_T,) + state, F32), pltpu.VMEM((N_PAIR, LANES, LANES), F32),
                        pltpu.VMEM((N_PAIR, LANES, LANES), F32)],
        compiler_params=_cp(("arbitrary",)),
    )(r, w, k, v, kkn, b, do, ckpt)


def _stacked(rows, cols, pick):
    return pl.BlockSpec((None, rows, cols), pick)


def _local_step(x, target, sm, win_st, wout, wup_st, wdown):
    zpad = jnp.zeros((LORA_DECAY, D_RWKV), F32)
    prm = [sm["w0"], jnp.concatenate([sm["w_decay_up"], zpad], axis=0), sm["a0"],
           jnp.concatenate([zpad, sm["w_iclr_up"]], axis=0), sm["w_gate_up"], sm["k_k"], sm["k_a"]]
    mix = sm["rwkv_shift_mix"]
    onehot = jnp.asarray(_t5_onehot(), BF16)
    sinks = sm["sinks"].reshape(N_Q_HEADS)
    lng, lnb, rk = sm["ln_x_g"], sm["ln_x_b"], sm["r_k"].reshape(1, D_RWKV)

    h1 = _norm_cast(x, sm["norm_mix_pre"], "norm_in")
    proj = _matmul(h1, win_st, "nn", "proj", m=SEQ, n=D_IN, k=D_MODEL, tm=SEQ, tn=640, tk=D_MODEL,
                   b_spec=_stacked(D_MODEL, 640, lambda i, j, kk: (j, 0, 0)))
    bias = _bias_table(sm["rel_bias"].T, onehot).reshape(N_KV_HEADS, Q_PER_KV * BLOCK, 2 * BLOCK)
    attn = _attn_fwd(proj, bias, sinks)
    r, w, k2, v, kkn, b, g = _rwkv_prep(proj, mix, prm)
    o, ckpt = _scan_fwd(r, w, k2, v, kkn, b)
    cat = _rwkv_post(o, r, k2, v, g, lng, lnb, rk, attn)
    mixo = _matmul(cat, wout, "nn", "out_proj", m=SEQ, n=D_MODEL, k=D_MODEL, tm=SEQ, tn=512, tk=D_MODEL)
    x2, h3 = _mix_norm(x, mixo, sm["norm_mix_post"], sm["norm_ffn_pre"])
    u = _matmul(h3, wup_st, "nn", "ffn_up", m=SEQ, n=2 * D_FF, k=D_MODEL, tm=SEQ, tn=512, tk=D_MODEL,
                b_spec=_stacked(D_MODEL, 512, lambda i, j, kk: (j // 4, 0, j % 4)))
    act = _ffn_act(u, sm["conv_w"], sm["conv_b"])
    f = _matmul(act, wdown, "nn", "ffn_down", m=SEQ, n=D_MODEL, k=D_FF, tm=1024, tn=512, tk=2048)
    loss, dy, df, d_g4 = _loss_head(x2, f, sm["norm_ffn_post"], target)

    dact = _matmul(df, wdown, "nt", "d_act", m=SEQ, n=D_FF, k=D_MODEL, tm=SEQ, tn=512, tk=D_MODEL)
    d_wdown = _matmul(act, df, "tn", "d_wdown", m=D_FF, n=D_MODEL, k=SEQ, tm=512, tn=D_MODEL, tk=SEQ)
    du, d_convw, d_convb = _ffn_act_bwd(u, dact, sm["conv_w"], sm["conv_b"])
    dh3 = _matmul(du, wup_st, "nt", "d_h3", m=SEQ, n=D_MODEL, k=2 * D_FF, tm=1024, tn=D_MODEL, tk=2048,
                  b_spec=_stacked(D_MODEL, 2048, lambda i, j, kk: (kk, j, 0)))
    d_wup = _matmul(h3, du, "tn", "d_wup", m=D_MODEL, n=2 * D_FF, k=SEQ, tm=D_MODEL, tn=512, tk=SEQ,
                    out=((N_CHIPS, D_MODEL, 2048), _stacked(D_MODEL, 512, lambda i, j, kk: (j // 4, 0, j % 4))))
    dx2, dmix, d_g2, d_g3 = _mid_bwd(x2, mixo, dy, dh3, sm["norm_mix_post"], sm["norm_ffn_pre"])
    dcat = _matmul(dmix, wout, "nt", "d_cat", m=SEQ, n=D_MODEL, k=D_MODEL, tm=SEQ, tn=512, tk=D_MODEL)
    d_wout = _matmul(cat, dmix, "tn", "d_wout", m=D_MODEL, n=D_MODEL, k=SEQ, tm=512, tn=D_MODEL, tk=SEQ)
    do, dr_p, dk_p, dv_p, dg, d_lng, d_lnb, d_rk = _rwkv_post_bwd(o, r, k2, v, g, lng, lnb, rk, dcat)
    dr_s, dw_s, dk_s, dv_s, dkkn_s, db_s = _scan_bwd(r, w, k2, v, kkn, b, do, ckpt)
    prep_grads = _rwkv_prep_bwd(proj, mix, prm, (dr_s, dr_p, dw_s, dk_s, dk_p, dv_s, dv_p, dkkn_s, db_s, dg))
    dps, d_mix, d_w0, d_wdu, d_a0, d_wiu, d_wgu, d_kk, d_ka = prep_grads
    dq, dkv, dbias, dsink = _attn_bwd(proj, bias, sinks, dcat)
    d_relb = _bias_table_bwd(dbias.reshape(N_Q_HEADS, N_REL), onehot).T
    dproj = _assemble_dproj(dq, dkv, dps, mix)
    dh1 = _matmul(dproj, win_st, "nt", "d_h1", m=SEQ, n=D_MODEL, k=D_IN, tm=1024, tn=D_MODEL, tk=640,
                  b_spec=_stacked(D_MODEL, 640, lambda i, j, kk: (kk, j, 0)))
    d_win = _matmul(h1, dproj, "tn", "d_win", m=D_MODEL, n=D_IN, k=SEQ, tm=D_MODEL, tn=640, tk=SEQ,
                    out=((N_CHIPS, D_MODEL, 640), _stacked(D_MODEL, 640, lambda i, j, kk: (j, 0, 0))))
    grad_x, d_g1 = _first_bwd(x, dx2, dh1, sm["norm_mix_pre"])

    grads = {
        "norm_mix_pre": d_g1, "norm_mix_post": d_g2, "norm_ffn_pre": d_g3, "norm_ffn_post": d_g4,
        "w_in": d_win, "rel_bias": d_relb, "sinks": dsink[:, 0].reshape(1, N_Q_HEADS),
        "rwkv_shift_mix": d_mix, "w0": d_w0, "w_decay_up": d_wdu[:LORA_DECAY], "a0": d_a0,
        "w_iclr_up": d_wiu[LORA_DECAY:], "w_gate_up": d_wgu, "k_k": d_kk, "k_a": d_ka,
        "r_k": d_rk.reshape(1, N_Q_HEADS, HEAD_DIM), "ln_x_g": d_lng, "ln_x_b": d_lnb,
        "w_out": d_wout, "w_ffn_up": d_wup, "conv_w": d_convw, "conv_b": d_convb, "w_ffn_down": d_wdown,
    }
    return loss, grad_x, grads


ANY = pl.BlockSpec(memory_space=pl.ANY)


def _place():
    x, y, c = lax.axis_index("x"), lax.axis_index("y"), lax.axis_index("c")
    chips = [(1 - x, y), (x, 1 - y), (1 - x, 1 - y)]
    return x, y, c, chips


def _remote(src, dst, sems, idx, to):
    return pltpu.make_async_remote_copy(src_ref=src, dst_ref=dst, send_sem=sems[0].at[idx], recv_sem=sems[1].at[idx],
                                        device_id=to, device_id_type=MESH)


def _half(c, rows):
    return pl.ds(pl.multiple_of(c * (rows // 2), 16), rows // 2)


def _gather_weights(big, small):
    nb, ns = len(big), len(small)

    def body(*refs):
        ins, outs = refs[:nb + ns], refs[nb + ns:2 * (nb + ns)]
        ici, d2d, sml, loc = refs[2 * (nb + ns):2 * (nb + ns) + 2], refs[-5:-3], refs[-3:-1], refs[-1]
        x, y, c, chips = _place()
        me = 2 * x + y
        sib = (x, y, 1 - c)
        local = [pltpu.make_async_copy(ins[a], outs[a].at[me], loc.at[a]) for a in range(nb + ns)]
        for cp in local:
            cp.start()
        sends = []
        for a in range(nb):
            rows = _half(c, big[a].shape[0])
            for kk, chip in enumerate(chips):
                sends.append(_remote(ins[a].at[rows], outs[a].at[me, rows], ici, a * 3 + kk, (*chip, c)))
        for a in range(ns):
            for kk, chip in enumerate(chips):
                sends.append(_remote(ins[nb + a], outs[nb + a].at[me], sml, a * 3 + kk, (*chip, c)))
        for cp in sends:
            cp.start()
        passed = []
        for a in range(nb):
            rows = _half(c, big[a].shape[0])
            for kk, (px, py) in enumerate(chips):
                got = outs[a].at[2 * px + py, rows]
                _remote(got, got, ici, a * 3 + kk, sib).wait_recv()
                fwd = _remote(got, got, d2d, a * 3 + kk, sib)
                fwd.start()
                passed.append(fwd)
        for a in range(nb):
            other = _half(1 - c, big[a].shape[0])
            for kk, (px, py) in enumerate(chips):
                land = outs[a].at[2 * px + py, other]
                _remote(land, land, d2d, a * 3 + kk, sib).wait_recv()
        for a in range(ns):
            for kk, (px, py) in enumerate(chips):
                land = outs[nb + a].at[2 * px + py]
                _remote(land, land, sml, a * 3 + kk, sib).wait_recv()
        for cp in sends + passed:
            cp.wait_send()
        for cp in local:
            cp.wait()

    arrs = list(big) + list(small)
    return pl.pallas_call(
        body, name="gather_weights",
        in_specs=[ANY] * len(arrs), out_specs=[ANY] * len(arrs),
        out_shape=[jax.ShapeDtypeStruct((N_CHIPS,) + t.shape, t.dtype) for t in arrs],
        scratch_shapes=[pltpu.SemaphoreType.DMA((3 * nb,)), pltpu.SemaphoreType.DMA((3 * nb,)),
                        pltpu.SemaphoreType.DMA((3 * nb,)), pltpu.SemaphoreType.DMA((3 * nb,)),
                        pltpu.SemaphoreType.DMA((3 * ns,)), pltpu.SemaphoreType.DMA((3 * ns,)),
                        pltpu.SemaphoreType.DMA((nb + ns,))],
        compiler_params=pltpu.CompilerParams(has_side_effects=True),
    )(*arrs)


def _allreduce_small(g):
    rows = g.shape[0]

    def body(g_ref, o_ref, buf, send, recv):
        x, y, c, _ = _place()
        me = 4 * x + 2 * y + c
        buf[me] = g_ref[...]
        sends = []
        for rel in range(1, N_DEV):
            px, py, pc = x ^ (rel >> 2), y ^ ((rel >> 1) & 1), c ^ (rel & 1)
            cp = _remote(g_ref, buf.at[me], (send, recv), rel - 1, (px, py, pc))
            cp.start()
            sends.append(cp)
        for rel in range(1, N_DEV):
            px, py, pc = x ^ (rel >> 2), y ^ ((rel >> 1) & 1), c ^ (rel & 1)
            land = buf.at[4 * px + 2 * py + pc]
            _remote(land, land, (send, recv), rel - 1, (px, py, pc)).wait_recv()
        acc = buf[0]
        for d in range(1, N_DEV):
            acc = acc + buf[d]
        o_ref[...] = acc
        for cp in sends:
            cp.wait_send()

    vm = pl.BlockSpec(memory_space=pltpu.VMEM)
    return pl.pallas_call(
        body, name="allreduce_small", in_specs=[vm], out_specs=vm,
        out_shape=jax.ShapeDtypeStruct((rows, LANES), F32),
        scratch_shapes=[pltpu.VMEM((N_DEV, rows, LANES), F32), pltpu.SemaphoreType.DMA((N_DEV - 1,)),
                        pltpu.SemaphoreType.DMA((N_DEV - 1,))],
        compiler_params=_cp(),
    )(g)


def _pair_exchange(gs):
    n = len(gs)

    def body(*refs):
        ins, got, mine, send, recv, loc = refs[:n], refs[n:2 * n], refs[2 * n:3 * n], refs[-3], refs[-2], refs[-1]
        x, y, c, _ = _place()
        sib = (x, y, 1 - c)
        cps, local = [], []
        for a in range(n):
            rows = gs[a].shape[1]
            cp = _remote(ins[a].at[:, _half(1 - c, rows)], got[a], (send, recv), a, sib)
            cp.start()
            cps.append(cp)
            lc = pltpu.make_async_copy(ins[a].at[:, _half(c, rows)], mine[a], loc.at[a])
            lc.start()
            local.append(lc)
        for a in range(n):
            cps[a].wait_recv()
        for a in range(n):
            cps[a].wait_send()
            local[a].wait()

    halves = [jax.ShapeDtypeStruct((N_CHIPS, t.shape[1] // 2, t.shape[2]), F32) for t in gs]
    outs = pl.pallas_call(
        body, name="grad_pair_exchange", in_specs=[ANY] * n, out_specs=[ANY] * (2 * n), out_shape=halves + halves,
        scratch_shapes=[pltpu.SemaphoreType.DMA((n,)), pltpu.SemaphoreType.DMA((n,)), pltpu.SemaphoreType.DMA((n,))],
        compiler_params=pltpu.CompilerParams(has_side_effects=True),
    )(*gs)
    return outs[:n], outs[n:]


def _chip_exchange(ps):
    n = len(ps)

    def body(*refs):
        ins, outs, send, recv, loc = refs[:n], refs[n:2 * n], refs[-3], refs[-2], refs[-1]
        x, y, c, chips = _place()
        me = 2 * x + y
        cps, local = [], []
        for a in range(n):
            lc = pltpu.make_async_copy(ins[a].at[me], outs[a].at[me], loc.at[a])
            lc.start()
            local.append(lc)
            for kk, (px, py) in enumerate(chips):
                cp = _remote(ins[a].at[2 * px + py], outs[a].at[me], (send, recv), a * 3 + kk, (px, py, c))
                cp.start()
                cps.append(cp)
        for a in range(n):
            for kk, (px, py) in enumerate(chips):
                land = outs[a].at[2 * px + py]
                _remote(land, land, (send, recv), a * 3 + kk, (px, py, c)).wait_recv()
        for cp in cps:
            cp.wait_send()
        for lc in local:
            lc.wait()

    return pl.pallas_call(
        body, name="grad_chip_exchange", in_specs=[ANY] * n, out_specs=[ANY] * n,
        out_shape=[jax.ShapeDtypeStruct(t.shape, F32) for t in ps],
        scratch_shapes=[pltpu.SemaphoreType.DMA((3 * n,)), pltpu.SemaphoreType.DMA((3 * n,)),
                        pltpu.SemaphoreType.DMA((n,))],
        compiler_params=pltpu.CompilerParams(has_side_effects=True),
    )(*ps)


def _pair_gather(hs):
    n = len(hs)

    def body(*refs):
        ins, outs, send, recv, loc = refs[:n], refs[n:2 * n], refs[-3], refs[-2], refs[-1]
        x, y, c, _ = _place()
        sib = (x, y, 1 - c)
        cps, local = [], []
        for a in range(n):
            rows = 2 * hs[a].shape[0]
            cp = _remote(ins[a], outs[a].at[_half(c, rows)], (send, recv), a, sib)
            cp.start()
            cps.append(cp)
            lc = pltpu.make_async_copy(ins[a], outs[a].at[_half(c, rows)], loc.at[a])
            lc.start()
            local.append(lc)
        for a in range(n):
            rows = 2 * hs[a].shape[0]
            land = outs[a].at[_half(1 - c, rows)]
            _remote(land, land, (send, recv), a, sib).wait_recv()
        for a in range(n):
            cps[a].wait_send()
            local[a].wait()

    return pl.pallas_call(
        body, name="grad_pair_gather", in_specs=[ANY] * n, out_specs=[ANY] * n,
        out_shape=[jax.ShapeDtypeStruct((2 * t.shape[0], t.shape[1]), F32) for t in hs],
        scratch_shapes=[pltpu.SemaphoreType.DMA((n,)), pltpu.SemaphoreType.DMA((n,)), pltpu.SemaphoreType.DMA((n,))],
        compiler_params=pltpu.CompilerParams(has_side_effects=True),
    )(*hs)


def _add2(a, b, name):
    r, cdim = a.shape
    tr = 256

    def body(a_ref, b_ref, o_ref):
        o_ref[...] = a_ref[...] + b_ref[...]

    return pl.pallas_call(
        body, name=name, grid=(r // tr,), in_specs=[_rows(tr, cdim)] * 2, out_specs=_rows(tr, cdim),
        out_shape=jax.ShapeDtypeStruct((r, cdim), F32), compiler_params=_cp(("parallel",)),
    )(a, b)


def _sum4(t, name):
    _, r, cdim = t.shape
    tr = 128

    def body(t_ref, o_ref):
        o_ref[...] = ((t_ref[0] + t_ref[1]) + t_ref[2]) + t_ref[3]

    return pl.pallas_call(
        body, name=name, grid=(r // tr,), in_specs=[pl.BlockSpec((N_CHIPS, tr, cdim), lambda i: (0, i, 0))],
        out_specs=_rows(tr, cdim), out_shape=jax.ShapeDtypeStruct((r, cdim), F32),
        compiler_params=_cp(("parallel",)),
    )(t)


def _reduce_big(gs):
    got, mine = _pair_exchange(gs)
    ps = [_add2(m.reshape(-1, m.shape[2]), g.reshape(-1, g.shape[2]), f"grad_pair_add_{i}").reshape(m.shape)
          for i, (m, g) in enumerate(zip(mine, got))]
    xs = _chip_exchange(ps)
    hs = [_sum4(t, f"grad_chip_sum_{i}") for i, t in enumerate(xs)]
    return _pair_gather(hs)


def _adamw(w, g, m, v, name, tr):
    r, cdim = w.shape

    def body(w_ref, g_ref, m_ref, v_ref, d_ref, nm_ref, nv_ref):
        g = g_ref[...]
        nm = ADAM_B1 * m_ref[...] + (1.0 - ADAM_B1) * g
        nv = ADAM_B2 * v_ref[...] + (1.0 - ADAM_B2) * (g * g)
        m_hat = nm / (1.0 - ADAM_B1 ** ADAM_STEP)
        v_hat = nv / (1.0 - ADAM_B2 ** ADAM_STEP)
        d_ref[...] = -ADAM_LR * (m_hat / (jnp.sqrt(v_hat) + ADAM_EPS) + ADAM_WD * w_ref[...])
        nm_ref[...] = nm
        nv_ref[...] = nv

    return pl.pallas_call(
        body, name=name, grid=(r // tr,), in_specs=[_rows(tr, cdim)] * 4, out_specs=[_rows(tr, cdim)] * 3,
        out_shape=[jax.ShapeDtypeStruct((r, cdim), F32)] * 3, compiler_params=_cp(("parallel",)),
    )(w, g, m, v)


REPLICATED = (("norm_mix_pre", 1024), ("norm_mix_post", 1024), ("norm_ffn_pre", 1024), ("norm_ffn_post", 1024),
              ("rel_bias", 256), ("sinks", 8), ("rwkv_shift_mix", 1792), ("w0", 512), ("a0", 512), ("k_k", 512),
              ("k_a", 512), ("r_k", 512), ("ln_x_g", 512), ("ln_x_b", 512), ("conv_b", 8192))
SMALL_SHARDED = (("w_decay_up", LORA_DECAY, D_RWKV), ("w_iclr_up", LORA_ICLR, D_RWKV),
                 ("w_gate_up", LORA_GATE, D_RWKV), ("conv_w", 3, 2 * D_FF))
BIG = (("w_in", D_MODEL, 640), ("w_out", 256, D_MODEL), ("w_ffn_up", D_MODEL, 2048), ("w_ffn_down", 1024, D_MODEL))
PACK_ALIGN = 8 * LANES


def _pack(pieces):
    flat = []
    for t in pieces:
        t = t.reshape(-1)
        pad = (-t.shape[0]) % LANES
        flat.append(jnp.pad(t, (0, pad)) if pad else t)
    flat = jnp.concatenate(flat)
    pad = (-flat.shape[0]) % PACK_ALIGN
    return jnp.pad(flat, (0, pad)).reshape(-1, LANES)


def _unpack(buf, sizes):
    flat, out, off = buf.reshape(-1), [], 0
    for n in sizes:
        out.append(flat[off:off + n])
        off += n + ((-n) % LANES)
    return out


def kernel(x, norm_mix_pre, norm_mix_post, norm_ffn_pre, norm_ffn_post, w_in, rel_bias, sinks, rwkv_shift_mix, w0, w_decay_up, a0, w_iclr_up, w_gate_up, k_k, k_a, r_k, ln_x_g, ln_x_b, w_out, w_ffn_up, conv_w, conv_b, w_ffn_down, loss_target, m_norm_mix_pre, m_norm_mix_post, m_norm_ffn_pre, m_norm_ffn_post, m_w_in, m_rel_bias, m_sinks, m_rwkv_shift_mix, m_w0, m_w_decay_up, m_a0, m_w_iclr_up, m_w_gate_up, m_k_k, m_k_a, m_r_k, m_ln_x_g, m_ln_x_b, m_w_out, m_w_ffn_up, m_conv_w, m_conv_b, m_w_ffn_down, v_norm_mix_pre, v_norm_mix_post, v_norm_ffn_pre, v_norm_ffn_post, v_w_in, v_rel_bias, v_sinks, v_rwkv_shift_mix, v_w0, v_w_decay_up, v_a0, v_w_iclr_up, v_w_gate_up, v_k_k, v_k_a, v_r_k, v_ln_x_g, v_ln_x_b, v_w_out, v_w_ffn_up, v_conv_w, v_conv_b, v_w_ffn_down):
    given = dict(locals())
    names = [n for n, _ in REPLICATED] + [n for n, _, _ in SMALL_SHARDED] + [n for n, _, _ in BIG]
    order = ["norm_mix_pre", "norm_mix_post", "norm_ffn_pre", "norm_ffn_post", "w_in", "rel_bias", "sinks",
             "rwkv_shift_mix", "w0", "w_decay_up", "a0", "w_iclr_up", "w_gate_up", "k_k", "k_a", "r_k", "ln_x_g",
             "ln_x_b", "w_out", "w_ffn_up", "conv_w", "conv_b", "w_ffn_down"]
    assert sorted(names) == sorted(order)
    shard = 2 * lax.axis_index("x") + lax.axis_index("y")

    big_sh = [given[n].reshape(a, b).astype(BF16) for n, a, b in BIG]
    small_sh = [given[n].reshape(r, c // N_CHIPS) for n, r, c in SMALL_SHARDED]
    gathered = _gather_weights(big_sh, small_sh)
    win_st, wout_st, wup_st, wdown_st = gathered[:4]
    sm = {n: given[n] for n, _ in REPLICATED}
    sm["r_k"] = r_k.reshape(N_Q_HEADS, HEAD_DIM)
    for (n, r, c), st in zip(SMALL_SHARDED, gathered[4:]):
        sm[n] = st.transpose(1, 0, 2).reshape(r, c)

    loss, grad_x, grads = _local_step(x[0], loss_target[0], sm, win_st, wout_st.reshape(D_MODEL, D_MODEL), wup_st,
                                      wdown_st.reshape(D_FF, D_MODEL))
    loss = lax.psum(loss[0, 0], ("x", "y", "c"))

    rep_sizes = [s for _, s in REPLICATED] + [r * c for _, r, c in SMALL_SHARDED]
    small_sum = _allreduce_small(_pack([grads[n] for n, _ in REPLICATED] + [grads[n] for n, _, _ in SMALL_SHARDED]))
    small_g = _unpack(small_sum, rep_sizes)
    g_out = {n: t.reshape(given[n].shape) for (n, _), t in zip(REPLICATED, small_g)}
    for (n, r, c), t in zip(SMALL_SHARDED, small_g[len(REPLICATED):]):
        g_out[n] = lax.dynamic_slice_in_dim(t.reshape(r, c), shard * (c // N_CHIPS), c // N_CHIPS, axis=1)
    big_g = _reduce_big([grads["w_in"], grads["w_out"].reshape(N_CHIPS, 256, D_MODEL), grads["w_ffn_up"],
                         grads["w_ffn_down"].reshape(N_CHIPS, 1024, D_MODEL)])
    for (n, _, _), t in zip(BIG, big_g):
        g_out[n] = t

    small_names = [n for n, _ in REPLICATED] + [n for n, _, _ in SMALL_SHARDED]
    packs = [_pack([src[n] for n in small_names]) for src in
             ({n: given[n] for n in small_names}, g_out, {n: given["m_" + n] for n in small_names},
              {n: given["v_" + n] for n in small_names})]
    small_sizes = [int(np.prod(given[n].shape)) for n in small_names]
    upd = [_unpack(t, small_sizes) for t in _adamw(*packs, "adamw_small", packs[0].shape[0])]
    delta, new_m, new_v = ({n: t.reshape(given[n].shape) for n, t in zip(small_names, u)} for u in upd)
    for n, a, b in BIG:
        d, nm, nv = _adamw(given[n].reshape(a, b), g_out[n], given["m_" + n].reshape(a, b),
                           given["v_" + n].reshape(a, b), "adamw_" + n, 128)
        delta[n], new_m[n], new_v[n] = d, nm, nv

    def shaped(d):
        return [d[n].reshape(given[n].shape) for n in order]

    return (loss, grad_x.reshape(x.shape), *shaped(g_out), *shaped(delta), *shaped(new_m), *shaped(new_v))
```

```python
import functools
import math

import numpy as np
import jax
import jax.numpy as jnp
from jax import lax
from jax.experimental import pallas as pl
from jax.experimental.pallas import tpu as pltpu

F32 = jnp.float32
BF16 = jnp.bfloat16
MESH = pl.DeviceIdType.MESH

SEQ = 2048
D_MODEL = 1024
HEAD_DIM = 64
D_ATTN = 512
D_RWKV = 512
D_KV = 128
N_Q_HEADS = 8
N_KV_HEADS = 2
Q_PER_KV = 4
BLOCK = 128
N_BUCKETS = 32
MAX_DISTANCE = 128
LORA_DECAY = 64
LORA_ICLR = 64
LORA_GATE = 128
RWKV_COLS = 3 * D_RWKV + LORA_DECAY + LORA_ICLR + LORA_GATE
P_OFF = D_ATTN + 2 * D_KV
D_IN = P_OFF + RWKV_COLS
D_FF = 4096
NORM_EPS = 1e-6
GN_EPS = 64e-5
NEG_INF = -1e30
N_CHIPS = 4
N_DEV = 8

ADAM_LR = 0.001
ADAM_B1 = 0.9
ADAM_B2 = 0.999
ADAM_EPS = 1e-08
ADAM_WD = 0.01
ADAM_STEP = 10

VMEM_LIMIT = 52 * 1024 * 1024
LANES = 128
SCAN_T = 64


def _cp(sem=None, vmem=VMEM_LIMIT):
    kw = dict(vmem_limit_bytes=vmem)
    if sem is not None:
        kw["dimension_semantics"] = sem
    return pltpu.CompilerParams(**kw)


def _rows(tr, nc):
    return pl.BlockSpec((tr, nc), lambda i: (i, 0))


def _const(shape):
    return pl.BlockSpec(shape, lambda *_: (0,) * len(shape))


def _split(x, n):
    parts = []
    for _ in range(n - 1):
        h = x.astype(BF16)
        parts.append(h)
        x = x - h.astype(F32)
    parts.append(x.astype(BF16))
    return parts


def _dot(a, b, dn=(((1,), (0,)), ((), ()))):
    return lax.dot_general(a, b, dn, preferred_element_type=F32)


NN = (((1,), (0,)), ((), ()))
NT = (((1,), (1,)), ((), ()))
TN = (((0,), (0,)), ((), ()))


def _dot_ind(x, ind_bf16, n=3):
    acc = None
    for part in _split(x, n):
        t = _dot(part, ind_bf16)
        acc = t if acc is None else acc + t
    return acc


def _head_ones(n, scale=1.0):
    r = lax.broadcasted_iota(jnp.int32, (n, n), 0) >> 6
    c = lax.broadcasted_iota(jnp.int32, (n, n), 1) >> 6
    return jnp.where(r == c, 1.0, 0.0).astype(BF16)


def _matmul(a, b, mode, name, *, m, n, k, tm, tn, tk, a_spec=None, b_spec=None, out=None, out_dtype=F32):
    nk = k // tk
    dn = {"nn": NN, "nt": NT, "tn": TN}[mode]

    def body(a_ref, b_ref, o_ref, *scratch):
        part = _dot(a_ref[...], b_ref[...], dn)
        if nk == 1:
            o_ref[...] = part.astype(out_dtype)
        else:
            acc_ref, = scratch
            kk = pl.program_id(2)

            @pl.when(kk == 0)
            def _():
                acc_ref[...] = part

            @pl.when(kk > 0)
            def _():
                acc_ref[...] += part

            @pl.when(kk == nk - 1)
            def _():
                o_ref[...] = acc_ref[...].astype(out_dtype)

    if a_spec is None:
        a_spec = (pl.BlockSpec((tk, tm), lambda i, j, kk: (kk, i)) if mode == "tn"
                  else pl.BlockSpec((tm, tk), lambda i, j, kk: (i, kk)))
    if b_spec is None:
        b_spec = (pl.BlockSpec((tn, tk), lambda i, j, kk: (j, kk)) if mode == "nt"
                  else pl.BlockSpec((tk, tn), lambda i, j, kk: (kk, j)))
    return pl.pallas_call(
        body, name=name, grid=(m // tm, n // tn, nk),
        in_specs=[a_spec, b_spec],
        out_specs=pl.BlockSpec((tm, tn), lambda i, j, kk: (i, j)) if out is None else out[1],
        out_shape=jax.ShapeDtypeStruct((m, n) if out is None else out[0], out_dtype),
        scratch_shapes=[] if nk == 1 else [pltpu.VMEM((tm, tn), F32)],
        compiler_params=_cp(("parallel", "parallel", "arbitrary")),
    )(a, b)


def _rstd(x):
    return lax.rsqrt(jnp.mean(x * x, axis=-1, keepdims=True) + NORM_EPS)


def _rms_bwd(x, r, g, dy):
    gy = dy * g
    return r * gy - x * ((r * r * r) * (jnp.sum(x * gy, axis=-1, keepdims=True) / x.shape[-1]))


TR = 256


def _norm_cast(x, g, name):
    def body(x_ref, g_ref, h_ref):
        x = x_ref[...]
        h_ref[...] = (x * _rstd(x) * g_ref[...]).astype(BF16)

    return pl.pallas_call(
        body, name=name, grid=(SEQ // TR,),
        in_specs=[_rows(TR, D_MODEL), _const((1, D_MODEL))],
        out_specs=_rows(TR, D_MODEL),
        out_shape=jax.ShapeDtypeStruct((SEQ, D_MODEL), BF16),
        compiler_params=_cp(("parallel",)),
    )(x, g)


def _mix_norm(x, mix, g2, g3):
    def body(x_ref, mix_ref, g2_ref, g3_ref, x2_ref, h3_ref):
        mixv = mix_ref[...]
        x2 = x_ref[...] + mixv * _rstd(mixv) * g2_ref[...]
        x2_ref[...] = x2
        h3_ref[...] = (x2 * _rstd(x2) * g3_ref[...]).astype(BF16)

    return pl.pallas_call(
        body, name="mix_norm", grid=(SEQ // TR,),
        in_specs=[_rows(TR, D_MODEL), _rows(TR, D_MODEL), _const((1, D_MODEL)), _const((1, D_MODEL))],
        out_specs=[_rows(TR, D_MODEL), _rows(TR, D_MODEL)],
        out_shape=[jax.ShapeDtypeStruct((SEQ, D_MODEL), F32), jax.ShapeDtypeStruct((SEQ, D_MODEL), BF16)],
        compiler_params=_cp(("parallel",)),
    )(x, mix, g2, g3)


def _loss_head(x2, f, g4, target):
    def body(x2_ref, f_ref, g4_ref, t_ref, loss_ref, dy_ref, df_ref, dg_ref):
        i = pl.program_id(0)
        f = f_ref[...]
        g4 = g4_ref[...]
        r = _rstd(f)
        e = x2_ref[...] + f * r * g4 - t_ref[...]
        dy = e * (1.0 / D_MODEL)
        dy_ref[...] = dy
        df_ref[...] = _rms_bwd(f, r, g4, dy).astype(BF16)
        part = 0.5 * jnp.sum(jnp.sum(e * e, axis=-1, keepdims=True), axis=0, keepdims=True) * (1.0 / D_MODEL)
        dg = jnp.sum(dy * f * r, axis=0, keepdims=True)

        @pl.when(i == 0)
        def _():
            loss_ref[...] = jnp.zeros_like(loss_ref)
            dg_ref[...] = jnp.zeros_like(dg_ref)

        loss_ref[...] += jnp.broadcast_to(part, loss_ref.shape)
        dg_ref[...] += dg

    return pl.pallas_call(
        body, name="loss_head", grid=(SEQ // TR,),
        in_specs=[_rows(TR, D_MODEL), _rows(TR, D_MODEL), _const((1, D_MODEL)), _rows(TR, D_MODEL)],
        out_specs=[_const((8, LANES)), _rows(TR, D_MODEL), _rows(TR, D_MODEL), _const((1, D_MODEL))],
        out_shape=[jax.ShapeDtypeStruct((8, LANES), F32), jax.ShapeDtypeStruct((SEQ, D_MODEL), F32),
                   jax.ShapeDtypeStruct((SEQ, D_MODEL), BF16), jax.ShapeDtypeStruct((1, D_MODEL), F32)],
        compiler_params=_cp(("arbitrary",)),
    )(x2, f, g4, target)


def _mid_bwd(x2, mix, dy, dh3, g2, g3):
    def body(x2_ref, mix_ref, dy_ref, dh3_ref, g2_ref, g3_ref, dx2_ref, dmix_ref, dg2_ref, dg3_ref):
        i = pl.program_id(0)
        x2 = x2_ref[...]
        mixv = mix_ref[...]
        dh3 = dh3_ref[...]
        r3 = _rstd(x2)
        dx2 = dy_ref[...] + _rms_bwd(x2, r3, g3_ref[...], dh3)
        dx2_ref[...] = dx2
        r2 = _rstd(mixv)
        dmix_ref[...] = _rms_bwd(mixv, r2, g2_ref[...], dx2).astype(BF16)

        @pl.when(i == 0)
        def _():
            dg2_ref[...] = jnp.zeros_like(dg2_ref)
            dg3_ref[...] = jnp.zeros_like(dg3_ref)

        dg3_ref[...] += jnp.sum(dh3 * x2 * r3, axis=0, keepdims=True)
        dg2_ref[...] += jnp.sum(dx2 * mixv * r2, axis=0, keepdims=True)

    return pl.pallas_call(
        body, name="mid_bwd", grid=(SEQ // TR,),
        in_specs=[_rows(TR, D_MODEL)] * 4 + [_const((1, D_MODEL))] * 2,
        out_specs=[_rows(TR, D_MODEL), _rows(TR, D_MODEL), _const((1, D_MODEL)), _const((1, D_MODEL))],
        out_shape=[jax.ShapeDtypeStruct((SEQ, D_MODEL), F32), jax.ShapeDtypeStruct((SEQ, D_MODEL), BF16),
                   jax.ShapeDtypeStruct((1, D_MODEL), F32), jax.ShapeDtypeStruct((1, D_MODEL), F32)],
        compiler_params=_cp(("arbitrary",)),
    )(x2, mix, dy, dh3, g2, g3)


def _first_bwd(x, dx2, dh1, g1):
    def body(x_ref, dx2_ref, dh1_ref, g1_ref, dx_ref, dg1_ref):
        i = pl.program_id(0)
        x = x_ref[...]
        dh1 = dh1_ref[...]
        r = _rstd(x)
        dx_ref[...] = dx2_ref[...] + _rms_bwd(x, r, g1_ref[...], dh1)

        @pl.when(i == 0)
        def _():
            dg1_ref[...] = jnp.zeros_like(dg1_ref)

        dg1_ref[...] += jnp.sum(dh1 * x * r, axis=0, keepdims=True)

    return pl.pallas_call(
        body, name="first_bwd", grid=(SEQ // TR,),
        in_specs=[_rows(TR, D_MODEL)] * 3 + [_const((1, D_MODEL))],
        out_specs=[_rows(TR, D_MODEL), _const((1, D_MODEL))],
        out_shape=[jax.ShapeDtypeStruct((SEQ, D_MODEL), F32), jax.ShapeDtypeStruct((1, D_MODEL), F32)],
        compiler_params=_cp(("arbitrary",)),
    )(x, dx2, dh1, g1)


TC = 256
N_CB = D_FF // TC
GELU_C = math.sqrt(2.0 / math.pi)


def _shift_down(u, s):
    rolled = pltpu.roll(u, s, 0)
    row = lax.broadcasted_iota(jnp.int32, u.shape, 0)
    return jnp.where(row >= s, rolled, 0.0)


def _shift_up(u, s):
    n = u.shape[0]
    rolled = pltpu.roll(u, n - s, 0)
    row = lax.broadcasted_iota(jnp.int32, u.shape, 0)
    return jnp.where(row < n - s, rolled, 0.0)


def _conv3(u, w, b):
    return b + w[0:1] * _shift_down(u, 2) + w[1:2] * _shift_down(u, 1) + w[2:3] * u


def _gelu_and_grad(x):
    inner = GELU_C * (x + 0.044715 * (x * x * x))
    t = jnp.tanh(inner)
    gelu = 0.5 * x * (1.0 + t)
    dgelu = 0.5 * (1.0 + t) + 0.5 * x * (1.0 - t * t) * (GELU_C * (1.0 + 3 * 0.044715 * (x * x)))
    return gelu, dgelu


def _ffn_specs():
    col = lambda off: pl.BlockSpec((SEQ, TC), lambda *g: (0, g[-1] + off))
    w = lambda off: pl.BlockSpec((3, TC), lambda *g: (0, g[-1] + off))
    b = lambda off: pl.BlockSpec((1, TC), lambda *g: (0, g[-1] + off))
    return col, w, b


def _ffn_act(u, conv_w, conv_b):
    col, w, b = _ffn_specs()

    def body(ug_ref, uv_ref, wg_ref, wv_ref, bg_ref, bv_ref, act_ref):
        gate = _conv3(ug_ref[...], wg_ref[...], bg_ref[...])
        val = _conv3(uv_ref[...], wv_ref[...], bv_ref[...])
        act_ref[...] = (_gelu_and_grad(gate)[0] * val).astype(BF16)

    return pl.pallas_call(
        body, name="ffn_act", grid=(N_CB,),
        in_specs=[col(0), col(N_CB), w(0), w(N_CB), b(0), b(N_CB)],
        out_specs=col(0),
        out_shape=jax.ShapeDtypeStruct((SEQ, D_FF), BF16),
        compiler_params=_cp(("parallel",)),
    )(u, u, conv_w, conv_w, conv_b, conv_b)


def _ffn_act_bwd(u, dact, conv_w, conv_b):
    col, w, b = _ffn_specs()
    half = lambda shape: pl.BlockSpec(shape, lambda h, j: (0, h * N_CB + j))

    def body(ug_ref, uv_ref, da_ref, wg_ref, wv_ref, bg_ref, bv_ref, du_ref, dw_ref, db_ref):
        h = pl.program_id(0)
        is_gate = h == 0
        ug = ug_ref[...]
        uv = uv_ref[...]
        gate = _conv3(ug, wg_ref[...], bg_ref[...])
        val = _conv3(uv, wv_ref[...], bv_ref[...])
        gelu, dgelu = _gelu_and_grad(gate)
        da = da_ref[...]
        duc = jnp.where(is_gate, da * val * dgelu, da * gelu)
        usel = jnp.where(is_gate, ug, uv)
        wsel = jnp.where(is_gate, wg_ref[...], wv_ref[...])
        du = wsel[2:3] * duc + wsel[1:2] * _shift_up(duc, 1) + wsel[0:1] * _shift_up(duc, 2)
        du_ref[...] = du.astype(BF16)
        db_ref[...] = jnp.sum(duc, axis=0, keepdims=True)
        dw_ref[...] = jnp.concatenate(
            [jnp.sum(duc * _shift_down(usel, 2), axis=0, keepdims=True),
             jnp.sum(duc * _shift_down(usel, 1), axis=0, keepdims=True),
             jnp.sum(duc * usel, axis=0, keepdims=True)], axis=0)

    return pl.pallas_call(
        body, name="ffn_act_bwd", grid=(2, N_CB),
        in_specs=[col(0), col(N_CB), col(0), w(0), w(N_CB), b(0), b(N_CB)],
        out_specs=[half((SEQ, TC)), half((3, TC)), half((1, TC))],
        out_shape=[jax.ShapeDtypeStruct((SEQ, 2 * D_FF), BF16), jax.ShapeDtypeStruct((3, 2 * D_FF), F32),
                   jax.ShapeDtypeStruct((1, 2 * D_FF), F32)],
        compiler_params=_cp(("parallel", "parallel")),
    )(u, u, dact, conv_w, conv_w, conv_b, conv_b)


def _t5_onehot():
    rel = (np.arange(BLOCK)[:, None] + BLOCK) - np.arange(2 * BLOCK)[None, :]
    n = np.maximum(rel, 0)
    max_exact = N_BUCKETS // 2
    large = max_exact + (np.log(np.maximum(n, 1).astype(np.float32) / np.float32(max_exact))
                         / np.float32(math.log(MAX_DISTANCE / max_exact))
                         * np.float32(N_BUCKETS - max_exact)).astype(np.int32)
    large = np.minimum(large, N_BUCKETS - 1)
    bucket = np.where(n < max_exact, n, large).reshape(-1)
    return (bucket[None, :] == np.arange(N_BUCKETS)[:, None]).astype(np.float32)


N_REL = BLOCK * 2 * BLOCK


def _bias_table(rel_bias_t, onehot):
    def body(rb_ref, oh_ref, o_ref):
        o_ref[...] = _dot_ind(rb_ref[...], oh_ref[...])

    return pl.pallas_call(
        body, name="bias_table", grid=(1,),
        in_specs=[_const((N_Q_HEADS, N_BUCKETS)), _const((N_BUCKETS, N_REL))],
        out_specs=_const((N_Q_HEADS, N_REL)),
        out_shape=jax.ShapeDtypeStruct((N_Q_HEADS, N_REL), F32),
        compiler_params=_cp(("arbitrary",)),
    )(rel_bias_t, onehot)


def _bias_table_bwd(dbias, onehot):
    def body(db_ref, oh_ref, o_ref):
        acc = None
        for part in _split(db_ref[...], 3):
            t = _dot(part, oh_ref[...], NT)
            acc = t if acc is None else acc + t
        o_ref[...] = acc

    return pl.pallas_call(
        body, name="bias_table_bwd", grid=(1,),
        in_specs=[_const((N_Q_HEADS, N_REL)), _const((N_BUCKETS, N_REL))],
        out_specs=_const((N_Q_HEADS, N_BUCKETS)),
        out_shape=jax.ShapeDtypeStruct((N_Q_HEADS, N_BUCKETS), F32),
        compiler_params=_cp(("arbitrary",)),
    )(dbias, onehot)


def _attn_pieces(n, q, kvp, kvc, bias_ref, sinks_ref, hk):
    qi = lax.broadcasted_iota(jnp.int32, (BLOCK, 2 * BLOCK), 0)
    kj = lax.broadcasted_iota(jnp.int32, (BLOCK, 2 * BLOCK), 1)
    rel = qi + BLOCK - kj
    first_key = jnp.where(n > 0, 0, BLOCK)
    ok = jnp.where(rel >= 0, jnp.where(rel < BLOCK, jnp.where(kj >= first_key, 1.0, 0.0), 0.0), 0.0)
    ok4 = jnp.concatenate([ok] * Q_PER_KV, axis=0) > 0.5
    c0 = hk * HEAD_DIM
    kcat = jnp.concatenate([kvp[:, c0:c0 + HEAD_DIM], kvc[:, c0:c0 + HEAD_DIM]], axis=0).astype(BF16)
    vcat = jnp.concatenate([kvp[:, D_KV + c0:D_KV + c0 + HEAD_DIM], kvc[:, D_KV + c0:D_KV + c0 + HEAD_DIM]],
                           axis=0).astype(BF16)
    q0 = hk * Q_PER_KV * HEAD_DIM
    qs = jnp.concatenate([q[:, q0 + g * HEAD_DIM:q0 + (g + 1) * HEAD_DIM] for g in range(Q_PER_KV)],
                         axis=0).astype(BF16)
    s = _dot(qs, kcat, NT) * (HEAD_DIM ** -0.5) + bias_ref[hk]
    s = jnp.where(ok4, s, NEG_INF)
    row = lax.broadcasted_iota(jnp.int32, (Q_PER_KV * BLOCK, 1), 0)
    sink = jnp.zeros((Q_PER_KV * BLOCK, 1), F32)
    for g in range(Q_PER_KV):
        sink = jnp.where((row >> 7) == g, sinks_ref[hk * Q_PER_KV + g], sink)
    m = jnp.maximum(jnp.max(s, axis=-1, keepdims=True), sink)
    p = jnp.exp(s - m)
    es = jnp.exp(sink - m)
    inv = 1.0 / (jnp.sum(p, axis=-1, keepdims=True) + es)
    return qs, kcat, vcat, p * inv, es * inv


def _attn_in_specs():
    return [pl.BlockSpec((BLOCK, D_ATTN), lambda n: (n, 0)),
            pl.BlockSpec((BLOCK, 2 * D_KV), lambda n: (jnp.maximum(n - 1, 0), D_ATTN // (2 * D_KV))),
            pl.BlockSpec((BLOCK, 2 * D_KV), lambda n: (n, D_ATTN // (2 * D_KV))),
            _const((N_KV_HEADS, Q_PER_KV * BLOCK, 2 * BLOCK)),
            pl.BlockSpec(memory_space=pltpu.SMEM)]


def _unstack_heads(t):
    return jnp.concatenate([t[g * BLOCK:(g + 1) * BLOCK] for g in range(Q_PER_KV)], axis=1)


def _attn_fwd(proj, bias, sinks):
    def body(q_ref, kvp_ref, kvc_ref, bias_ref, sinks_ref, o_ref):
        n = pl.program_id(0)
        q, kvp, kvc = q_ref[...], kvp_ref[...], kvc_ref[...]
        outs = []
        for hk in range(N_KV_HEADS):
            _, _, vcat, probs, _ = _attn_pieces(n, q, kvp, kvc, bias_ref, sinks_ref, hk)
            outs.append(_unstack_heads(_dot(probs.astype(BF16), vcat)))
        o_ref[...] = jnp.concatenate(outs, axis=1)

    return pl.pallas_call(
        body, name="attn_fwd", grid=(SEQ // BLOCK,),
        in_specs=_attn_in_specs(),
        out_specs=pl.BlockSpec((BLOCK, D_ATTN), lambda n: (n, 0)),
        out_shape=jax.ShapeDtypeStruct((SEQ, D_ATTN), F32),
        compiler_params=_cp(("parallel",)),
    )(proj, proj, proj, bias, sinks)


def _attn_bwd(proj, bias, sinks, dcat):
    nb = SEQ // BLOCK

    def body(q_ref, kvp_ref, kvc_ref, bias_ref, sinks_ref, do_ref, dq_ref, dkv_ref, dbias_ref, dsink_ref, dsacc):
        n = pl.program_id(0)

        @pl.when(n == 0)
        def _():
            dkv_ref[...] = jnp.zeros_like(dkv_ref)
            dbias_ref[...] = jnp.zeros_like(dbias_ref)
            dsacc[...] = jnp.zeros_like(dsacc)

        q, kvp, kvc = q_ref[...], kvp_ref[...], kvc_ref[...]
        do_all = do_ref[...]
        dqs, dks, dvs = [], [], []
        for hk in range(N_KV_HEADS):
            qs, kcat, vcat, probs, psink = _attn_pieces(n, q, kvp, kvc, bias_ref, sinks_ref, hk)
            q0 = hk * Q_PER_KV * HEAD_DIM
            do = jnp.concatenate([do_all[:, q0 + g * HEAD_DIM:q0 + (g + 1) * HEAD_DIM] for g in range(Q_PER_KV)],
                                 axis=0).astype(BF16)
            dprobs = _dot(do, vcat, NT)
            dvs.append(_dot(probs.astype(BF16), do, TN))
            rowdot = jnp.sum(probs * dprobs, axis=-1, keepdims=True)
            ds = probs * (dprobs - rowdot)
            dsacc[hk] += -psink * rowdot
            dbias_ref[hk] += ds
            dsb = (ds * (HEAD_DIM ** -0.5)).astype(BF16)
            dqs.append(_unstack_heads(_dot(dsb, kcat)))
            dks.append(_dot(dsb, qs, TN))
        dq_ref[...] = jnp.concatenate(dqs, axis=1)
        upd = jnp.concatenate(dks + dvs, axis=1)
        cur = pl.multiple_of(n * BLOCK, BLOCK)
        dkv_ref[pl.ds(cur, BLOCK), :] += upd[BLOCK:]

        @pl.when(n > 0)
        def _():
            prev = pl.multiple_of((n - 1) * BLOCK, BLOCK)
            dkv_ref[pl.ds(prev, BLOCK), :] += upd[:BLOCK]

        @pl.when(n == nb - 1)
        def _():
            for hk in range(N_KV_HEADS):
                for g in range(Q_PER_KV):
                    tot = jnp.sum(dsacc[hk, g * BLOCK:(g + 1) * BLOCK, :], axis=0, keepdims=True)
                    h = hk * Q_PER_KV + g
                    dsink_ref[h:h + 1, :] = jnp.broadcast_to(tot, (1, LANES))

    return pl.pallas_call(
        body, name="attn_bwd", grid=(nb,),
        in_specs=_attn_in_specs() + [pl.BlockSpec((BLOCK, D_ATTN), lambda n: (n, 0))],
        out_specs=[pl.BlockSpec((BLOCK, D_ATTN), lambda n: (n, 0)), _const((SEQ, 2 * D_KV)),
                   _const((N_KV_HEADS, Q_PER_KV * BLOCK, 2 * BLOCK)), _const((N_Q_HEADS, LANES))],
        out_shape=[jax.ShapeDtypeStruct((SEQ, D_ATTN), F32), jax.ShapeDtypeStruct((SEQ, 2 * D_KV), F32),
                   jax.ShapeDtypeStruct((N_KV_HEADS, Q_PER_KV * BLOCK, 2 * BLOCK), F32),
                   jax.ShapeDtypeStruct((N_Q_HEADS, LANES), F32)],
        scratch_shapes=[pltpu.VMEM((N_KV_HEADS, Q_PER_KV * BLOCK, 1), F32)],
        compiler_params=_cp(("arbitrary",)),
    )(proj, proj, proj, bias, sinks, dcat)


@jax.custom_vjp
def _head_sum(x):
    return _dot_ind(x, _head_ones(x.shape[-1]))


_head_sum.defvjp(lambda x: (_head_sum(x), None), lambda _, ct: (_head_sum(ct),))


@jax.custom_vjp
def _bdot(a, w):
    return _dot(a.astype(BF16), w.astype(BF16))


def _bdot_bwd(res, ct):
    a, w = res
    ctb = ct.astype(BF16)
    return _dot(ctb, w.astype(BF16), NT), _dot(a.astype(BF16), ctb, TN)


_bdot.defvjp(lambda a, w: (_bdot(a, w), (a, w)), _bdot_bwd)


def _sigmoid(x):
    return 0.5 * (jnp.tanh(0.5 * x) + 1.0)


def _softplus(x):
    return jnp.maximum(x, 0.0) + jnp.log(1.0 + jnp.exp(-jnp.abs(x)))


def _rwkv_core(r, k, v, zwa, zg, w0, wdu, a0, wiu, wgu, k_k, k_a):
    w_log = -_softplus(-(w0 + _bdot(jnp.tanh(zwa), wdu))) - 0.5
    decay = jnp.exp(-jnp.exp(w_log))
    a = _sigmoid(a0 + _bdot(zwa, wiu))
    g = _bdot(_sigmoid(zg), wgu)
    kk = k * k_k
    kk = kk / jnp.maximum(jnp.sqrt(_head_sum(kk * kk)), 1e-12)
    k2 = k * (1.0 + (a - 1.0) * k_a)
    return r, decay, k2, v, -kk, kk * a, g


def _rwkv_out(o, r, k2, v, g, lng, lnb, rk):
    mu = _head_sum(o) * (1.0 / HEAD_DIM)
    d = o - mu
    var = _head_sum(d * d) * (1.0 / HEAD_DIM)
    on = d * lax.rsqrt(var + GN_EPS) * lng + lnb
    bonus = _head_sum(r * k2 * rk) * v
    return (on + bonus) * g


P_SPLITS = (0, 512, 1024, 1536, 1664, 1792)
N_PREP_PARAMS = 7
HALO = 8


def _shifted_pieces(i, p_ref, halo_ref, mix_ref):
    p = p_ref[:, P_OFF:]
    prev_row = halo_ref[HALO - 1:HALO, P_OFF:] * jnp.where(i > 0, 1.0, 0.0)
    row = lax.broadcasted_iota(jnp.int32, p.shape, 0)
    pprev = jnp.where(row == 0, prev_row, pltpu.roll(p, 1, 0))
    delta = pprev - p
    ps = p + delta * mix_ref[...]
    return [ps[:, a:b] for a, b in zip(P_SPLITS[:-1], P_SPLITS[1:])], delta


def _prep_in_specs():
    return [_rows(TR, D_IN),
            pl.BlockSpec((HALO, D_IN), lambda i: (jnp.maximum(i * (TR // HALO) - 1, 0), 0)),
            _const((1, RWKV_COLS)), _const((1, D_RWKV)), _const((LANES, D_RWKV)), _const((1, D_RWKV)),
            _const((LANES, D_RWKV)), _const((LANES, D_RWKV)), _const((1, D_RWKV)), _const((1, D_RWKV))]


def _rwkv_prep(proj, mix, prm):
    def body(p_ref, halo_ref, mix_ref, *refs):
        prm_refs, outs = refs[:N_PREP_PARAMS], refs[N_PREP_PARAMS:]
        pieces, _ = _shifted_pieces(pl.program_id(0), p_ref, halo_ref, mix_ref)
        vals = _rwkv_core(*pieces, *[t[...] for t in prm_refs])
        for ref, val in zip(outs, vals):
            ref[...] = val

    return pl.pallas_call(
        body, name="rwkv_prep", grid=(SEQ // TR,),
        in_specs=_prep_in_specs(),
        out_specs=[_rows(TR, D_RWKV)] * 7,
        out_shape=[jax.ShapeDtypeStruct((SEQ, D_RWKV), F32)] * 7,
        compiler_params=_cp(("parallel",)),
    )(proj, proj, mix, *prm)


def _rwkv_prep_bwd(proj, mix, prm, cts):
    def body(p_ref, halo_ref, mix_ref, *refs):
        i = pl.program_id(0)
        prm_refs = refs[:N_PREP_PARAMS]
        ct_refs = refs[N_PREP_PARAMS:N_PREP_PARAMS + 10]
        dps_ref, dmix_ref = refs[N_PREP_PARAMS + 10:N_PREP_PARAMS + 12]
        dprm_refs = refs[N_PREP_PARAMS + 12:]
        pieces, delta = _shifted_pieces(i, p_ref, halo_ref, mix_ref)
        _, vjp = jax.vjp(_rwkv_core, *pieces, *[t[...] for t in prm_refs])
        dr1, dr2, dw, dk1, dk2, dv1, dv2, dkkn, db, dg = [t[...] for t in ct_refs]
        grads = vjp((dr1 + dr2, dw, dk1 + dk2, dv1 + dv2, dkkn, db, dg))
        dps = jnp.concatenate(grads[:5], axis=1)
        dps_ref[...] = dps

        @pl.when(i == 0)
        def _():
            dmix_ref[...] = jnp.zeros_like(dmix_ref)
            for ref in dprm_refs:
                ref[...] = jnp.zeros_like(ref)

        dmix_ref[...] += jnp.sum(dps * delta, axis=0, keepdims=True)
        for ref, gval in zip(dprm_refs, grads[5:]):
            ref[...] += gval

    prm_shapes = [(1, D_RWKV), (LANES, D_RWKV), (1, D_RWKV), (LANES, D_RWKV), (LANES, D_RWKV), (1, D_RWKV), (1, D_RWKV)]
    return pl.pallas_call(
        body, name="rwkv_prep_bwd", grid=(SEQ // TR,),
        in_specs=_prep_in_specs() + [_rows(TR, D_RWKV)] * 10,
        out_specs=[_rows(TR, RWKV_COLS), _const((1, RWKV_COLS))] + [_const(s) for s in prm_shapes],
        out_shape=[jax.ShapeDtypeStruct((SEQ, RWKV_COLS), F32), jax.ShapeDtypeStruct((1, RWKV_COLS), F32)]
        + [jax.ShapeDtypeStruct(s, F32) for s in prm_shapes],
        compiler_params=_cp(("arbitrary",)),
    )(proj, proj, mix, *prm, *cts)


def _rwkv_post(o, r, k2, v, g, lng, lnb, rk, attn):
    def body(o_ref, r_ref, k_ref, v_ref, g_ref, lng_ref, lnb_ref, rk_ref, attn_ref, cat_ref):
        rw = _rwkv_out(*[t[...] for t in (o_ref, r_ref, k_ref, v_ref, g_ref, lng_ref, lnb_ref, rk_ref)])
        cat_ref[...] = jnp.concatenate([attn_ref[...], rw], axis=1).astype(BF16)

    return pl.pallas_call(
        body, name="rwkv_post", grid=(SEQ // TR,),
        in_specs=[_rows(TR, D_RWKV)] * 5 + [_const((1, D_RWKV))] * 3 + [_rows(TR, D_ATTN)],
        out_specs=_rows(TR, D_MODEL),
        out_shape=jax.ShapeDtypeStruct((SEQ, D_MODEL), BF16),
        compiler_params=_cp(("parallel",)),
    )(o, r, k2, v, g, lng, lnb, rk, attn)


def _rwkv_post_bwd(o, r, k2, v, g, lng, lnb, rk, dcat):
    def body(o_ref, r_ref, k_ref, v_ref, g_ref, lng_ref, lnb_ref, rk_ref, dcat_ref,
             do_ref, dr_ref, dk_ref, dv_ref, dg_ref, dlng_ref, dlnb_ref, drk_ref):
        i = pl.program_id(0)
        args = [t[...] for t in (o_ref, r_ref, k_ref, v_ref, g_ref, lng_ref, lnb_ref, rk_ref)]
        _, vjp = jax.vjp(_rwkv_out, *args)
        grads = vjp(dcat_ref[:, D_ATTN:])
        for ref, gval in zip((do_ref, dr_ref, dk_ref, dv_ref, dg_ref), grads[:5]):
            ref[...] = gval

        @pl.when(i == 0)
        def _():
            for ref in (dlng_ref, dlnb_ref, drk_ref):
                ref[...] = jnp.zeros_like(ref)

        for ref, gval in zip((dlng_ref, dlnb_ref, drk_ref), grads[5:]):
            ref[...] += gval

    return pl.pallas_call(
        body, name="rwkv_post_bwd", grid=(SEQ // TR,),
        in_specs=[_rows(TR, D_RWKV)] * 5 + [_const((1, D_RWKV))] * 3 + [_rows(TR, D_MODEL)],
        out_specs=[_rows(TR, D_RWKV)] * 5 + [_const((1, D_RWKV))] * 3,
        out_shape=[jax.ShapeDtypeStruct((SEQ, D_RWKV), F32)] * 5 + [jax.ShapeDtypeStruct((1, D_RWKV), F32)] * 3,
        compiler_params=_cp(("arbitrary",)),
    )(o, r, k2, v, g, lng, lnb, rk, dcat)


def _assemble_dproj(dq, dkv, dps, mix):
    last = SEQ // HALO - 1

    def body(dq_ref, dkv_ref, dps_ref, nxt_ref, mix_ref, o_ref):
        i = pl.program_id(0)
        dps = dps_ref[...]
        mixv = mix_ref[...]
        nxt_row = nxt_ref[0:1, :] * jnp.where(i < SEQ // TR - 1, 1.0, 0.0)
        row = lax.broadcasted_iota(jnp.int32, dps.shape, 0)
        up = jnp.where(row == TR - 1, nxt_row, pltpu.roll(dps, TR - 1, 0))
        dp = dps * (1.0 - mixv) + up * mixv
        o_ref[...] = jnp.concatenate([dq_ref[...], dkv_ref[...], dp], axis=1).astype(BF16)

    return pl.pallas_call(
        body, name="assemble_dproj", grid=(SEQ // TR,),
        in_specs=[_rows(TR, D_ATTN), _rows(TR, 2 * D_KV), _rows(TR, RWKV_COLS),
                  pl.BlockSpec((HALO, RWKV_COLS), lambda i: (jnp.minimum((i + 1) * (TR // HALO), last), 0)),
                  _const((1, RWKV_COLS))],
        out_specs=_rows(TR, D_IN),
        out_shape=jax.ShapeDtypeStruct((SEQ, D_IN), BF16),
        compiler_params=_cp(("parallel",)),
    )(dq, dkv, dps, dps, mix)


N_PAIR = D_RWKV // LANES
CHUNK = 2 * SCAN_T
N_CHUNK = SEQ // CHUNK
GROUP = 8


def _lane_sums(lhs_tiles, ones2):
    out = _dot(jnp.concatenate(lhs_tiles, axis=0), ones2)
    return [out[i * HEAD_DIM:(i + 1) * HEAD_DIM] for i in range(len(lhs_tiles))]


def _seg_sum(xs, ones2):
    return _lane_sums([jnp.concatenate(_split(x, 2), axis=1) for x in xs], ones2)


def _col_form(rows, diag, ones2):
    zero = jnp.zeros((HEAD_DIM, LANES), BF16)
    tiles = []
    for row in rows:
        hi = row.astype(BF16)
        lo = (row - hi.astype(F32)).astype(BF16)
        tiles.append(jnp.concatenate(
            [jnp.where(diag, jnp.broadcast_to(part, (HEAD_DIM, LANES)), zero) for part in (hi, lo)], axis=1))
    return _lane_sums(tiles, ones2)


def _scan_consts():
    ones2 = jnp.concatenate([_head_ones(LANES)] * 2, axis=0)
    sub = lax.broadcasted_iota(jnp.int32, (HEAD_DIM, LANES), 0)
    lane_in_head = lax.broadcasted_iota(jnp.int32, (HEAD_DIM, LANES), 1) & (HEAD_DIM - 1)
    return ones2, lane_in_head == sub, lane_in_head


def _rows_of_columns(tile):
    t = tile.T
    return jnp.concatenate([t[:HEAD_DIM], t[HEAD_DIM:]], axis=1)


def _pair(j):
    return slice(j * LANES, (j + 1) * LANES)


def _scan_fwd(r, w, k, v, kkn, b):
    def body(r_ref, w_ref, k_ref, v_ref, kkn_ref, b_ref, o_ref, ckpt_ref, s_scr):
        c = pl.program_id(0)
        ones2, diag, lane_in_head = _scan_consts()

        @pl.when(c == 0)
        def _():
            s_scr[...] = jnp.zeros_like(s_scr)

        for sub in range(CHUNK // SCAN_T):
            ckpt_ref[sub] = s_scr[...]

            def group(gi, carry, sub=sub):
                row0 = pl.multiple_of(sub * SCAN_T + gi * GROUP, GROUP)
                states, ocols = list(carry[:N_PAIR]), list(carry[N_PAIR:])
                tiles = [[t[pl.ds(row0, GROUP), _pair(j)] for t in (r_ref, w_ref, k_ref, v_ref, kkn_ref, b_ref)]
                         for j in range(N_PAIR)]
                def row(j, name, u):
                    return tiles[j]["rwkvnb".index(name)][u:u + 1]

                def emit_out(u, after):
                    here = lane_in_head == gi * GROUP + u
                    outs = _seg_sum([after[j] * row(j, "r", u) for j in range(N_PAIR)], ones2)
                    for j in range(N_PAIR):
                        ocols[j] = jnp.where(here, outs[j], ocols[j])

                vcols = _col_form([row(j, "v", 0) for j in range(N_PAIR)], diag, ones2)
                after = None
                for u in range(GROUP):
                    sas = _seg_sum([states[j] * row(j, "n", u) for j in range(N_PAIR)], ones2)
                    if after is not None:
                        emit_out(u - 1, after)
                    nxt = (_col_form([row(j, "v", u + 1) for j in range(N_PAIR)], diag, ones2)
                           if u + 1 < GROUP else None)
                    for j in range(N_PAIR):
                        states[j] = states[j] * row(j, "w", u) + sas[j] * row(j, "b", u) + vcols[j] * row(j, "k", u)
                    after, vcols = list(states), nxt
                emit_out(GROUP - 1, after)
                return tuple(states + ocols)

            zero = jnp.zeros((HEAD_DIM, LANES), F32)
            fin = lax.fori_loop(0, SCAN_T // GROUP, group,
                                tuple(s_scr[j] for j in range(N_PAIR)) + (zero,) * N_PAIR)
            for j in range(N_PAIR):
                s_scr[j] = fin[j]
                o_ref[sub * SCAN_T:(sub + 1) * SCAN_T, _pair(j)] = _rows_of_columns(fin[N_PAIR + j])

    blk = pl.BlockSpec((CHUNK, D_RWKV), lambda c: (c, 0))
    return pl.pallas_call(
        body, name="rwkv_scan_fwd", grid=(N_CHUNK,),
        in_specs=[blk] * 6,
        out_specs=[blk, pl.BlockSpec((CHUNK // SCAN_T, N_PAIR, HEAD_DIM, LANES), lambda c: (c, 0, 0, 0))],
        out_shape=[jax.ShapeDtypeStruct((SEQ, D_RWKV), F32),
                   jax.ShapeDtypeStruct((SEQ // SCAN_T, N_PAIR, HEAD_DIM, LANES), F32)],
        scratch_shapes=[pltpu.VMEM((N_PAIR, HEAD_DIM, LANES), F32)],
        compiler_params=_cp(("arbitrary",)),
    )(r, w, k, v, kkn, b)


def _scan_bwd(r, w, k, v, kkn, b, do, ckpt):
    def body(r_ref, w_ref, k_ref, v_ref, kkn_ref, b_ref, do_ref, ckpt_ref,
             dr_ref, dw_ref, dk_ref, dv_ref, dkkn_ref, db_ref, ds_scr, st_scr, sa_scr):
        i = pl.program_id(0)
        ones2, diag, lane_in_head = _scan_consts()

        @pl.when(i == 0)
        def _():
            ds_scr[...] = jnp.zeros_like(ds_scr)

        for sub in reversed(range(CHUNK // SCAN_T)):
            def recompute(gi, states, sub=sub):
                row0 = pl.multiple_of(sub * SCAN_T + gi * GROUP, GROUP)
                states = list(states)
                tiles = [[t[pl.ds(row0, GROUP), _pair(j)] for t in (w_ref, k_ref, v_ref, kkn_ref, b_ref)]
                         for j in range(N_PAIR)]
                for u in range(GROUP):
                    rows = [[t[u:u + 1] for t in tiles[j]] for j in range(N_PAIR)]
                    sas = _seg_sum([states[j] * rows[j][3] for j in range(N_PAIR)], ones2)
                    vcols = _col_form([rows[j][2] for j in range(N_PAIR)], diag, ones2)
                    for j in range(N_PAIR):
                        w_t, k_t, _, _, b_t = rows[j]
                        st_scr[gi * GROUP + u, j] = states[j]
                        sa_scr[gi * GROUP + u, j] = sas[j]
                        states[j] = states[j] * w_t + sas[j] * b_t + vcols[j] * k_t
                return tuple(states)

            fin = lax.fori_loop(0, SCAN_T // GROUP, recompute, tuple(ckpt_ref[sub, j] for j in range(N_PAIR)))
            for j in range(N_PAIR):
                st_scr[SCAN_T, j] = fin[j]

            def reverse(gr, carry, sub=sub):
                gi = SCAN_T // GROUP - 1 - gr
                row0 = pl.multiple_of(sub * SCAN_T + gi * GROUP, GROUP)
                dstates, dvcols = list(carry[:N_PAIR]), list(carry[N_PAIR:])
                tiles = [[t[pl.ds(row0, GROUP), _pair(j)]
                          for t in (r_ref, w_ref, k_ref, v_ref, kkn_ref, b_ref, do_ref)] for j in range(N_PAIR)]
                rows = [[[None] * GROUP for _ in range(5)] for _ in range(N_PAIR)]

                def row(j, name, u):
                    return tiles[j]["rwkvnbd".index(name)][u:u + 1]

                def cols_of(u):
                    both = _col_form([row(j, "d", u) for j in range(N_PAIR)] + [row(j, "v", u) for j in range(N_PAIR)],
                                     diag, ones2)
                    return [(both[j], both[N_PAIR + j]) for j in range(N_PAIR)]

                def emit_dv(u, dsp):
                    here = lane_in_head == gi * GROUP + u
                    outs = _seg_sum([dsp[j] * row(j, "k", u) for j in range(N_PAIR)], ones2)
                    for j in range(N_PAIR):
                        dvcols[j] = jnp.where(here, outs[j], dvcols[j])

                cols = cols_of(GROUP - 1)
                before = None
                for u in reversed(range(GROUP)):
                    tl = gi * GROUP + u
                    dsp = [dstates[j] + cols[j][0] * row(j, "r", u) for j in range(N_PAIR)]
                    dsas = _seg_sum([dsp[j] * row(j, "b", u) for j in range(N_PAIR)], ones2)
                    if before is not None:
                        emit_dv(u + 1, before)
                    nxt = cols_of(u - 1) if u > 0 else None
                    for j in range(N_PAIR):
                        s_prev = st_scr[tl, j]
                        docol, vcol = cols[j]
                        rows[j][0][u] = jnp.sum(st_scr[tl + 1, j] * docol, axis=0, keepdims=True)
                        rows[j][1][u] = jnp.sum(dsp[j] * s_prev, axis=0, keepdims=True)
                        rows[j][2][u] = jnp.sum(dsp[j] * vcol, axis=0, keepdims=True)
                        rows[j][3][u] = jnp.sum(s_prev * dsas[j], axis=0, keepdims=True)
                        rows[j][4][u] = jnp.sum(dsp[j] * sa_scr[tl, j], axis=0, keepdims=True)
                        dstates[j] = dsp[j] * row(j, "w", u) + dsas[j] * row(j, "n", u)
                    before, cols = dsp, nxt
                emit_dv(0, before)
                for j in range(N_PAIR):
                    for ref, rr in zip((dr_ref, dw_ref, dk_ref, dkkn_ref, db_ref), rows[j]):
                        ref[pl.ds(row0, GROUP), _pair(j)] = jnp.concatenate(rr, axis=0)
                return tuple(dstates + dvcols)

            zero = jnp.zeros((HEAD_DIM, LANES), F32)
            dfin = lax.fori_loop(0, SCAN_T // GROUP, reverse,
                                 tuple(ds_scr[j] for j in range(N_PAIR)) + (zero,) * N_PAIR)
            for j in range(N_PAIR):
                ds_scr[j] = dfin[j]
                dv_ref[sub * SCAN_T:(sub + 1) * SCAN_T, _pair(j)] = _rows_of_columns(dfin[N_PAIR + j])

    blk = pl.BlockSpec((CHUNK, D_RWKV), lambda i: (N_CHUNK - 1 - i, 0))
    state = (N_PAIR, HEAD_DIM, LANES)
    return pl.pallas_call(
        body, name="rwkv_scan_bwd", grid=(N_CHUNK,),
        in_specs=[blk] * 7 + [pl.BlockSpec((CHUNK // SCAN_T,) + state, lambda i: (N_CHUNK - 1 - i, 0, 0, 0))],
        out_specs=[blk] * 6,
        out_shape=[jax.ShapeDtypeStruct((SEQ, D_RWKV), F32)] * 6,
        scratch_shapes=[pltpu.VMEM(state, F32), pltpu.VMEM((SCAN_T + 1,) + state, F32),
                        pltpu.VMEM((SCAN_T,) + state, F32)],
        compiler_params=_cp(("arbitrary",)),
    )(r, w, k, v, kkn, b, do, ckpt)


def _stacked(rows, cols, pick):
    return pl.BlockSpec((None, rows, cols), pick)


def _local_step(x, target, sm, win_st, wout, wup_st, wdown):
    zpad = jnp.zeros((LORA_DECAY, D_RWKV), F32)
    prm = [sm["w0"], jnp.concatenate([sm["w_decay_up"], zpad], axis=0), sm["a0"],
           jnp.concatenate([zpad, sm["w_iclr_up"]], axis=0), sm["w_gate_up"], sm["k_k"], sm["k_a"]]
    mix = sm["rwkv_shift_mix"]
    onehot = jnp.asarray(_t5_onehot(), BF16)
    sinks = sm["sinks"].reshape(N_Q_HEADS)
    lng, lnb, rk = sm["ln_x_g"], sm["ln_x_b"], sm["r_k"].reshape(1, D_RWKV)

    h1 = _norm_cast(x, sm["norm_mix_pre"], "norm_in")
    proj = _matmul(h1, win_st, "nn", "proj", m=SEQ, n=D_IN, k=D_MODEL, tm=SEQ, tn=640, tk=D_MODEL,
                   b_spec=_stacked(D_MODEL, 640, lambda i, j, kk: (j, 0, 0)))
    bias = _bias_table(sm["rel_bias"].T, onehot).reshape(N_KV_HEADS, Q_PER_KV * BLOCK, 2 * BLOCK)
    attn = _attn_fwd(proj, bias, sinks)
    r, w, k2, v, kkn, b, g = _rwkv_prep(proj, mix, prm)
    o, ckpt = _scan_fwd(r, w, k2, v, kkn, b)
    cat = _rwkv_post(o, r, k2, v, g, lng, lnb, rk, attn)
    mixo = _matmul(cat, wout, "nn", "out_proj", m=SEQ, n=D_MODEL, k=D_MODEL, tm=SEQ, tn=512, tk=D_MODEL)
    x2, h3 = _mix_norm(x, mixo, sm["norm_mix_post"], sm["norm_ffn_pre"])
    u = _matmul(h3, wup_st, "nn", "ffn_up", m=SEQ, n=2 * D_FF, k=D_MODEL, tm=SEQ, tn=512, tk=D_MODEL,
                b_spec=_stacked(D_MODEL, 512, lambda i, j, kk: (j // 4, 0, j % 4)))
    act = _ffn_act(u, sm["conv_w"], sm["conv_b"])
    f = _matmul(act, wdown, "nn", "ffn_down", m=SEQ, n=D_MODEL, k=D_FF, tm=1024, tn=512, tk=2048)
    loss, dy, df, d_g4 = _loss_head(x2, f, sm["norm_ffn_post"], target)

    dact = _matmul(df, wdown, "nt", "d_act", m=SEQ, n=D_FF, k=D_MODEL, tm=SEQ, tn=512, tk=D_MODEL)
    d_wdown = _matmul(act, df, "tn", "d_wdown", m=D_FF, n=D_MODEL, k=SEQ, tm=512, tn=D_MODEL, tk=SEQ)
    du, d_convw, d_convb = _ffn_act_bwd(u, dact, sm["conv_w"], sm["conv_b"])
    dh3 = _matmul(du, wup_st, "nt", "d_h3", m=SEQ, n=D_MODEL, k=2 * D_FF, tm=1024, tn=D_MODEL, tk=2048,
                  b_spec=_stacked(D_MODEL, 2048, lambda i, j, kk: (kk, j, 0)))
    d_wup = _matmul(h3, du, "tn", "d_wup", m=D_MODEL, n=2 * D_FF, k=SEQ, tm=D_MODEL, tn=512, tk=SEQ,
                    out=((N_CHIPS, D_MODEL, 2048), _stacked(D_MODEL, 512, lambda i, j, kk: (j // 4, 0, j % 4))))
    dx2, dmix, d_g2, d_g3 = _mid_bwd(x2, mixo, dy, dh3, sm["norm_mix_post"], sm["norm_ffn_pre"])
    dcat = _matmul(dmix, wout, "nt", "d_cat", m=SEQ, n=D_MODEL, k=D_MODEL, tm=SEQ, tn=512, tk=D_MODEL)
    d_wout = _matmul(cat, dmix, "tn", "d_wout", m=D_MODEL, n=D_MODEL, k=SEQ, tm=512, tn=D_MODEL, tk=SEQ)
    do, dr_p, dk_p, dv_p, dg, d_lng, d_lnb, d_rk = _rwkv_post_bwd(o, r, k2, v, g, lng, lnb, rk, dcat)
    dr_s, dw_s, dk_s, dv_s, dkkn_s, db_s = _scan_bwd(r, w, k2, v, kkn, b, do, ckpt)
    prep_grads = _rwkv_prep_bwd(proj, mix, prm, (dr_s, dr_p, dw_s, dk_s, dk_p, dv_s, dv_p, dkkn_s, db_s, dg))
    dps, d_mix, d_w0, d_wdu, d_a0, d_wiu, d_wgu, d_kk, d_ka = prep_grads
    dq, dkv, dbias, dsink = _attn_bwd(proj, bias, sinks, dcat)
    d_relb = _bias_table_bwd(dbias.reshape(N_Q_HEADS, N_REL), onehot).T
    dproj = _assemble_dproj(dq, dkv, dps, mix)
    dh1 = _matmul(dproj, win_st, "nt", "d_h1", m=SEQ, n=D_MODEL, k=D_IN, tm=1024, tn=D_MODEL, tk=640,
                  b_spec=_stacked(D_MODEL, 640, lambda i, j, kk: (kk, j, 0)))
    d_win = _matmul(h1, dproj, "tn", "d_win", m=D_MODEL, n=D_IN, k=SEQ, tm=D_MODEL, tn=640, tk=SEQ,
                    out=((N_CHIPS, D_MODEL, 640), _stacked(D_MODEL, 640, lambda i, j, kk: (j, 0, 0))))
    grad_x, d_g1 = _first_bwd(x, dx2, dh1, sm["norm_mix_pre"])

    grads = {
        "norm_mix_pre": d_g1, "norm_mix_post": d_g2, "norm_ffn_pre": d_g3, "norm_ffn_post": d_g4,
        "w_in": d_win, "rel_bias": d_relb, "sinks": dsink[:, 0].reshape(1, N_Q_HEADS),
        "rwkv_shift_mix": d_mix, "w0": d_w0, "w_decay_up": d_wdu[:LORA_DECAY], "a0": d_a0,
        "w_iclr_up": d_wiu[LORA_DECAY:], "w_gate_up": d_wgu, "k_k": d_kk, "k_a": d_ka,
        "r_k": d_rk.reshape(1, N_Q_HEADS, HEAD_DIM), "ln_x_g": d_lng, "ln_x_b": d_lnb,
        "w_out": d_wout, "w_ffn_up": d_wup, "conv_w": d_convw, "conv_b": d_convb, "w_ffn_down": d_wdown,
    }
    return loss, grad_x, grads


ANY = pl.BlockSpec(memory_space=pl.ANY)


def _place():
    x, y, c = lax.axis_index("x"), lax.axis_index("y"), lax.axis_index("c")
    chips = [(1 - x, y), (x, 1 - y), (1 - x, 1 - y)]
    return x, y, c, chips


def _remote(src, dst, sems, idx, to):
    return pltpu.make_async_remote_copy(src_ref=src, dst_ref=dst, send_sem=sems[0].at[idx], recv_sem=sems[1].at[idx],
                                        device_id=to, device_id_type=MESH)


def _half(c, rows):
    return pl.ds(pl.multiple_of(c * (rows // 2), 16), rows // 2)


def _gather_weights(big, small):
    nb, ns = len(big), len(small)

    def body(*refs):
        ins, outs = refs[:nb + ns], refs[nb + ns:2 * (nb + ns)]
        ici, d2d, sml, loc = refs[2 * (nb + ns):2 * (nb + ns) + 2], refs[-5:-3], refs[-3:-1], refs[-1]
        x, y, c, chips = _place()
        me = 2 * x + y
        sib = (x, y, 1 - c)
        local = [pltpu.make_async_copy(ins[a], outs[a].at[me], loc.at[a]) for a in range(nb + ns)]
        for cp in local:
            cp.start()
        sends = []
        for a in range(nb):
            rows = _half(c, big[a].shape[0])
            for kk, chip in enumerate(chips):
                sends.append(_remote(ins[a].at[rows], outs[a].at[me, rows], ici, a * 3 + kk, (*chip, c)))
        for a in range(ns):
            for kk, chip in enumerate(chips):
                sends.append(_remote(ins[nb + a], outs[nb + a].at[me], sml, a * 3 + kk, (*chip, c)))
        for cp in sends:
            cp.start()
        passed = []
        for a in range(nb):
            rows = _half(c, big[a].shape[0])
            for kk, (px, py) in enumerate(chips):
                got = outs[a].at[2 * px + py, rows]
                _remote(got, got, ici, a * 3 + kk, sib).wait_recv()
                fwd = _remote(got, got, d2d, a * 3 + kk, sib)
                fwd.start()
                passed.append(fwd)
        for a in range(nb):
            other = _half(1 - c, big[a].shape[0])
            for kk, (px, py) in enumerate(chips):
                land = outs[a].at[2 * px + py, other]
                _remote(land, land, d2d, a * 3 + kk, sib).wait_recv()
        for a in range(ns):
            for kk, (px, py) in enumerate(chips):
                land = outs[nb + a].at[2 * px + py]
                _remote(land, land, sml, a * 3 + kk, sib).wait_recv()
        for cp in sends + passed:
            cp.wait_send()
        for cp in local:
            cp.wait()

    arrs = list(big) + list(small)
    return pl.pallas_call(
        body, name="gather_weights",
        in_specs=[ANY] * len(arrs), out_specs=[ANY] * len(arrs),
        out_shape=[jax.ShapeDtypeStruct((N_CHIPS,) + t.shape, t.dtype) for t in arrs],
        scratch_shapes=[pltpu.SemaphoreType.DMA((3 * nb,)), pltpu.SemaphoreType.DMA((3 * nb,)),
                        pltpu.SemaphoreType.DMA((3 * nb,)), pltpu.SemaphoreType.DMA((3 * nb,)),
                        pltpu.SemaphoreType.DMA((3 * ns,)), pltpu.SemaphoreType.DMA((3 * ns,)),
                        pltpu.SemaphoreType.DMA((nb + ns,))],
        compiler_params=pltpu.CompilerParams(has_side_effects=True),
    )(*arrs)


def _allreduce_small(g):
    rows = g.shape[0]

    def body(g_ref, o_ref, buf, send, recv):
        x, y, c, _ = _place()
        me = 4 * x + 2 * y + c
        buf[me] = g_ref[...]
        sends = []
        for rel in range(1, N_DEV):
            px, py, pc = x ^ (rel >> 2), y ^ ((rel >> 1) & 1), c ^ (rel & 1)
            cp = _remote(g_ref, buf.at[me], (send, recv), rel - 1, (px, py, pc))
            cp.start()
            sends.append(cp)
        for rel in range(1, N_DEV):
            px, py, pc = x ^ (rel >> 2), y ^ ((rel >> 1) & 1), c ^ (rel & 1)
            land = buf.at[4 * px + 2 * py + pc]
            _remote(land, land, (send, recv), rel - 1, (px, py, pc)).wait_recv()
        acc = buf[0]
        for d in range(1, N_DEV):
            acc = acc + buf[d]
        o_ref[...] = acc
        for cp in sends:
            cp.wait_send()

    vm = pl.BlockSpec(memory_space=pltpu.VMEM)
    return pl.pallas_call(
        body, name="allreduce_small", in_specs=[vm], out_specs=vm,
        out_shape=jax.ShapeDtypeStruct((rows, LANES), F32),
        scratch_shapes=[pltpu.VMEM((N_DEV, rows, LANES), F32), pltpu.SemaphoreType.DMA((N_DEV - 1,)),
                        pltpu.SemaphoreType.DMA((N_DEV - 1,))],
        compiler_params=_cp(),
    )(g)


def _pair_exchange(gs):
    n = len(gs)

    def body(*refs):
        ins, got, mine, send, recv, loc = refs[:n], refs[n:2 * n], refs[2 * n:3 * n], refs[-3], refs[-2], refs[-1]
        x, y, c, _ = _place()
        sib = (x, y, 1 - c)
        cps, local = [], []
        for a in range(n):
            rows = gs[a].shape[1]
            cp = _remote(ins[a].at[:, _half(1 - c, rows)], got[a], (send, recv), a, sib)
            cp.start()
            cps.append(cp)
            lc = pltpu.make_async_copy(ins[a].at[:, _half(c, rows)], mine[a], loc.at[a])
            lc.start()
            local.append(lc)
        for a in range(n):
            cps[a].wait_recv()
        for a in range(n):
            cps[a].wait_send()
            local[a].wait()

    halves = [jax.ShapeDtypeStruct((N_CHIPS, t.shape[1] // 2, t.shape[2]), F32) for t in gs]
    outs = pl.pallas_call(
        body, name="grad_pair_exchange", in_specs=[ANY] * n, out_specs=[ANY] * (2 * n), out_shape=halves + halves,
        scratch_shapes=[pltpu.SemaphoreType.DMA((n,)), pltpu.SemaphoreType.DMA((n,)), pltpu.SemaphoreType.DMA((n,))],
        compiler_params=pltpu.CompilerParams(has_side_effects=True),
    )(*gs)
    return outs[:n], outs[n:]


def _chip_exchange(ps):
    n = len(ps)

    def body(*refs):
        ins, outs, send, recv, loc = refs[:n], refs[n:2 * n], refs[-3], refs[-2], refs[-1]
        x, y, c, chips = _place()
        me = 2 * x + y
        cps, local = [], []
        for a in range(n):
            lc = pltpu.make_async_copy(ins[a].at[me], outs[a].at[me], loc.at[a])
            lc.start()
            local.append(lc)
            for kk, (px, py) in enumerate(chips):
                cp = _remote(ins[a].at[2 * px + py], outs[a].at[me], (send, recv), a * 3 + kk, (px, py, c))
                cp.start()
                cps.append(cp)
        for a in range(n):
            for kk, (px, py) in enumerate(chips):
                land = outs[a].at[2 * px + py]
                _remote(land, land, (send, recv), a * 3 + kk, (px, py, c)).wait_recv()
        for cp in cps:
            cp.wait_send()
        for lc in local:
            lc.wait()

    return pl.pallas_call(
        body, name="grad_chip_exchange", in_specs=[ANY] * n, out_specs=[ANY] * n,
        out_shape=[jax.ShapeDtypeStruct(t.shape, F32) for t in ps],
        scratch_shapes=[pltpu.SemaphoreType.DMA((3 * n,)), pltpu.SemaphoreType.DMA((3 * n,)),
                        pltpu.SemaphoreType.DMA((n,))],
        compiler_params=pltpu.CompilerParams(has_side_effects=True),
    )(*ps)


def _pair_gather(hs):
    n = len(hs)

    def body(*refs):
        ins, outs, send, recv, loc = refs[:n], refs[n:2 * n], refs[-3], refs[-2], refs[-1]
        x, y, c, _ = _place()
        sib = (x, y, 1 - c)
        cps, local = [], []
        for a in range(n):
            rows = 2 * hs[a].shape[0]
            cp = _remote(ins[a], outs[a].at[_half(c, rows)], (send, recv), a, sib)
            cp.start()
            cps.append(cp)
            lc = pltpu.make_async_copy(ins[a], outs[a].at[_half(c, rows)], loc.at[a])
            lc.start()
            local.append(lc)
        for a in range(n):
            rows = 2 * hs[a].shape[0]
            land = outs[a].at[_half(1 - c, rows)]
            _remote(land, land, (send, recv), a, sib).wait_recv()
        for a in range(n):
            cps[a].wait_send()
            local[a].wait()

    return pl.pallas_call(
        body, name="grad_pair_gather", in_specs=[ANY] * n, out_specs=[ANY] * n,
        out_shape=[jax.ShapeDtypeStruct((2 * t.shape[0], t.shape[1]), F32) for t in hs],
        scratch_shapes=[pltpu.SemaphoreType.DMA((n,)), pltpu.SemaphoreType.DMA((n,)), pltpu.SemaphoreType.DMA((n,))],
        compiler_params=pltpu.CompilerParams(has_side_effects=True),
    )(*hs)


def _add2(a, b, name):
    r, cdim = a.shape
    tr = 256

    def body(a_ref, b_ref, o_ref):
        o_ref[...] = a_ref[...] + b_ref[...]

    return pl.pallas_call(
        body, name=name, grid=(r // tr,), in_specs=[_rows(tr, cdim)] * 2, out_specs=_rows(tr, cdim),
        out_shape=jax.ShapeDtypeStruct((r, cdim), F32), compiler_params=_cp(("parallel",)),
    )(a, b)


def _sum4(t, name):
    _, r, cdim = t.shape
    tr = 128

    def body(t_ref, o_ref):
        o_ref[...] = ((t_ref[0] + t_ref[1]) + t_ref[2]) + t_ref[3]

    return pl.pallas_call(
        body, name=name, grid=(r // tr,), in_specs=[pl.BlockSpec((N_CHIPS, tr, cdim), lambda i: (0, i, 0))],
        out_specs=_rows(tr, cdim), out_shape=jax.ShapeDtypeStruct((r, cdim), F32),
        compiler_params=_cp(("parallel",)),
    )(t)


def _reduce_big(gs):
    got, mine = _pair_exchange(gs)
    ps = [_add2(m.reshape(-1, m.shape[2]), g.reshape(-1, g.shape[2]), f"grad_pair_add_{i}").reshape(m.shape)
          for i, (m, g) in enumerate(zip(mine, got))]
    xs = _chip_exchange(ps)
    hs = [_sum4(t, f"grad_chip_sum_{i}") for i, t in enumerate(xs)]
    return _pair_gather(hs)


def _adamw(w, g, m, v, name, tr):
    r, cdim = w.shape

    def body(w_ref, g_ref, m_ref, v_ref, d_ref, nm_ref, nv_ref):
        g = g_ref[...]
        nm = ADAM_B1 * m_ref[...] + (1.0 - ADAM_B1) * g
        nv = ADAM_B2 * v_ref[...] + (1.0 - ADAM_B2) * (g * g)
        m_hat = nm / (1.0 - ADAM_B1 ** ADAM_STEP)
        v_hat = nv / (1.0 - ADAM_B2 ** ADAM_STEP)
        d_ref[...] = -ADAM_LR * (m_hat / (jnp.sqrt(v_hat) + ADAM_EPS) + ADAM_WD * w_ref[...])
        nm_ref[...] = nm
        nv_ref[...] = nv

    return pl.pallas_call(
        body, name=name, grid=(r // tr,), in_specs=[_rows(tr, cdim)] * 4, out_specs=[_rows(tr, cdim)] * 3,
        out_shape=[jax.ShapeDtypeStruct((r, cdim), F32)] * 3, compiler_params=_cp(("parallel",)),
    )(w, g, m, v)


REPLICATED = (("norm_mix_pre", 1024), ("norm_mix_post", 1024), ("norm_ffn_pre", 1024), ("norm_ffn_post", 1024),
              ("rel_bias", 256), ("sinks", 8), ("rwkv_shift_mix", 1792), ("w0", 512), ("a0", 512), ("k_k", 512),
              ("k_a", 512), ("r_k", 512), ("ln_x_g", 512), ("ln_x_b", 512), ("conv_b", 8192))
SMALL_SHARDED = (("w_decay_up", LORA_DECAY, D_RWKV), ("w_iclr_up", LORA_ICLR, D_RWKV),
                 ("w_gate_up", LORA_GATE, D_RWKV), ("conv_w", 3, 2 * D_FF))
BIG = (("w_in", D_MODEL, 640), ("w_out", 256, D_MODEL), ("w_ffn_up", D_MODEL, 2048), ("w_ffn_down", 1024, D_MODEL))
PACK_ALIGN = 8 * LANES


def _pack(pieces):
    flat = []
    for t in pieces:
        t = t.reshape(-1)
        pad = (-t.shape[0]) % LANES
        flat.append(jnp.pad(t, (0, pad)) if pad else t)
    flat = jnp.concatenate(flat)
    pad = (-flat.shape[0]) % PACK_ALIGN
    return jnp.pad(flat, (0, pad)).reshape(-1, LANES)


def _unpack(buf, sizes):
    flat, out, off = buf.reshape(-1), [], 0
    for n in sizes:
        out.append(flat[off:off + n])
        off += n + ((-n) % LANES)
    return out


def kernel(x, norm_mix_pre, norm_mix_post, norm_ffn_pre, norm_ffn_post, w_in, rel_bias, sinks, rwkv_shift_mix, w0, w_decay_up, a0, w_iclr_up, w_gate_up, k_k, k_a, r_k, ln_x_g, ln_x_b, w_out, w_ffn_up, conv_w, conv_b, w_ffn_down, loss_target, m_norm_mix_pre, m_norm_mix_post, m_norm_ffn_pre, m_norm_ffn_post, m_w_in, m_rel_bias, m_sinks, m_rwkv_shift_mix, m_w0, m_w_decay_up, m_a0, m_w_iclr_up, m_w_gate_up, m_k_k, m_k_a, m_r_k, m_ln_x_g, m_ln_x_b, m_w_out, m_w_ffn_up, m_conv_w, m_conv_b, m_w_ffn_down, v_norm_mix_pre, v_norm_mix_post, v_norm_ffn_pre, v_norm_ffn_post, v_w_in, v_rel_bias, v_sinks, v_rwkv_shift_mix, v_w0, v_w_decay_up, v_a0, v_w_iclr_up, v_w_gate_up, v_k_k, v_k_a, v_r_k, v_ln_x_g, v_ln_x_b, v_w_out, v_w_ffn_up, v_conv_w, v_conv_b, v_w_ffn_down):
    given = dict(locals())
    names = [n for n, _ in REPLICATED] + [n for n, _, _ in SMALL_SHARDED] + [n for n, _, _ in BIG]
    order = ["norm_mix_pre", "norm_mix_post", "norm_ffn_pre", "norm_ffn_post", "w_in", "rel_bias", "sinks",
             "rwkv_shift_mix", "w0", "w_decay_up", "a0", "w_iclr_up", "w_gate_up", "k_k", "k_a", "r_k", "ln_x_g",
             "ln_x_b", "w_out", "w_ffn_up", "conv_w", "conv_b", "w_ffn_down"]
    assert sorted(names) == sorted(order)
    shard = 2 * lax.axis_index("x") + lax.axis_index("y")

    big_sh = [given[n].reshape(a, b).astype(BF16) for n, a, b in BIG]
    small_sh = [given[n].reshape(r, c // N_CHIPS) for n, r, c in SMALL_SHARDED]
    gathered = _gather_weights(big_sh, small_sh)
    win_st, wout_st, wup_st, wdown_st = gathered[:4]
    sm = {n: given[n] for n, _ in REPLICATED}
    sm["r_k"] = r_k.reshape(N_Q_HEADS, HEAD_DIM)
    for (n, r, c), st in zip(SMALL_SHARDED, gathered[4:]):
        sm[n] = st.transpose(1, 0, 2).reshape(r, c)

    loss, grad_x, grads = _local_step(x[0], loss_target[0], sm, win_st, wout_st.reshape(D_MODEL, D_MODEL), wup_st,
                                      wdown_st.reshape(D_FF, D_MODEL))
    loss = lax.psum(loss[0, 0], ("x", "y", "c"))

    rep_sizes = [s for _, s in REPLICATED] + [r * c for _, r, c in SMALL_SHARDED]
    small_sum = _allreduce_small(_pack([grads[n] for n, _ in REPLICATED] + [grads[n] for n, _, _ in SMALL_SHARDED]))
    small_g = _unpack(small_sum, rep_sizes)
    g_out = {n: t.reshape(given[n].shape) for (n, _), t in zip(REPLICATED, small_g)}
    for (n, r, c), t in zip(SMALL_SHARDED, small_g[len(REPLICATED):]):
        g_out[n] = lax.dynamic_slice_in_dim(t.reshape(r, c), shard * (c // N_CHIPS), c // N_CHIPS, axis=1)
    big_g = _reduce_big([grads["w_in"], grads["w_out"].reshape(N_CHIPS, 256, D_MODEL), grads["w_ffn_up"],
                         grads["w_ffn_down"].reshape(N_CHIPS, 1024, D_MODEL)])
    for (n, _, _), t in zip(BIG, big_g):
        g_out[n] = t

    small_names = [n for n, _ in REPLICATED] + [n for n, _, _ in SMALL_SHARDED]
    packs = [_pack([src[n] for n in small_names]) for src in
             ({n: given[n] for n in small_names}, g_out, {n: given["m_" + n] for n in small_names},
              {n: given["v_" + n] for n in small_names})]
    small_sizes = [int(np.prod(given[n].shape)) for n in small_names]
    upd = [_unpack(t, small_sizes) for t in _adamw(*packs, "adamw_small", packs[0].shape[0])]
    delta, new_m, new_v = ({n: t.reshape(given[n].shape) for n, t in zip(small_names, u)} for u in upd)
    for n, a, b in BIG:
        d, nm, nv = _adamw(given[n].reshape(a, b), g_out[n], given["m_" + n].reshape(a, b),
                           given["v_" + n].reshape(a, b), "adamw_" + n, 128)
        delta[n], new_m[n], new_v[n] = d, nm, nv

    def shaped(d):
        return [d[n].reshape(given[n].shape) for n in order]

    return (loss, grad_x.reshape(x.shape), *shaped(g_out), *shaped(delta), *shaped(new_m), *shaped(new_v))
```

```python
import functools
import math

import numpy as np
import jax
import jax.numpy as jnp
from jax import lax
from jax.experimental import pallas as pl
from jax.experimental.pallas import tpu as pltpu

F32 = jnp.float32
BF16 = jnp.bfloat16
MESH = pl.DeviceIdType.MESH

SEQ = 2048
D_MODEL = 1024
HEAD_DIM = 64
D_ATTN = 512
D_RWKV = 512
D_KV = 128
N_Q_HEADS = 8
N_KV_HEADS = 2
Q_PER_KV = 4
BLOCK = 128
N_BUCKETS = 32
MAX_DISTANCE = 128
LORA_DECAY = 64
LORA_ICLR = 64
LORA_GATE = 128
RWKV_COLS = 3 * D_RWKV + LORA_DECAY + LORA_ICLR + LORA_GATE
P_OFF = D_ATTN + 2 * D_KV
D_IN = P_OFF + RWKV_COLS
D_FF = 4096
NORM_EPS = 1e-6
GN_EPS = 64e-5
NEG_INF = -1e30
N_CHIPS = 4
N_DEV = 8

ADAM_LR = 0.001
ADAM_B1 = 0.9
ADAM_B2 = 0.999
ADAM_EPS = 1e-08
ADAM_WD = 0.01
ADAM_STEP = 10

VMEM_LIMIT = 52 * 1024 * 1024
LANES = 128
SCAN_T = 64


def _cp(sem=None, vmem=VMEM_LIMIT):
    kw = dict(vmem_limit_bytes=vmem)
    if sem is not None:
        kw["dimension_semantics"] = sem
    return pltpu.CompilerParams(**kw)


def _rows(tr, nc):
    return pl.BlockSpec((tr, nc), lambda i: (i, 0))


def _const(shape):
    return pl.BlockSpec(shape, lambda *_: (0,) * len(shape))


ANY = pl.BlockSpec(memory_space=pl.ANY)


def _split(x, n):
    parts = []
    for _ in range(n - 1):
        h = x.astype(BF16)
        parts.append(h)
        x = x - h.astype(F32)
    parts.append(x.astype(BF16))
    return parts


def _dot(a, b, dn=(((1,), (0,)), ((), ()))):
    return lax.dot_general(a, b, dn, preferred_element_type=F32)


NN = (((1,), (0,)), ((), ()))
NT = (((1,), (1,)), ((), ()))
TN = (((0,), (0,)), ((), ()))


def _dot_ind(x, ind_bf16, n=3):
    acc = None
    for part in _split(x, n):
        t = _dot(part, ind_bf16)
        acc = t if acc is None else acc + t
    return acc


def _head_ones(n, scale=1.0):
    r = lax.broadcasted_iota(jnp.int32, (n, n), 0) >> 6
    c = lax.broadcasted_iota(jnp.int32, (n, n), 1) >> 6
    return jnp.where(r == c, 1.0, 0.0).astype(BF16)


def _matmul(a, b, mode, name, *, m, n, k, tm, tn, tk, a_spec=None, b_spec=None, out=None, out_dtype=F32):
    nk = k // tk
    dn = {"nn": NN, "nt": NT, "tn": TN}[mode]

    def body(a_ref, b_ref, o_ref, *scratch):
        part = _dot(a_ref[...], b_ref[...], dn)
        if nk == 1:
            o_ref[...] = part.astype(out_dtype)
        else:
            acc_ref, = scratch
            kk = pl.program_id(2)

            @pl.when(kk == 0)
            def _():
                acc_ref[...] = part

            @pl.when(kk > 0)
            def _():
                acc_ref[...] += part

            @pl.when(kk == nk - 1)
            def _():
                o_ref[...] = acc_ref[...].astype(out_dtype)

    if a_spec is None:
        a_spec = (pl.BlockSpec((tk, tm), lambda i, j, kk: (kk, i)) if mode == "tn"
                  else pl.BlockSpec((tm, tk), lambda i, j, kk: (i, kk)))
    if b_spec is None:
        b_spec = (pl.BlockSpec((tn, tk), lambda i, j, kk: (j, kk)) if mode == "nt"
                  else pl.BlockSpec((tk, tn), lambda i, j, kk: (kk, j)))
    return pl.pallas_call(
        body, name=name, grid=(m // tm, n // tn, nk),
        in_specs=[a_spec, b_spec],
        out_specs=pl.BlockSpec((tm, tn), lambda i, j, kk: (i, j)) if out is None else out[1],
        out_shape=jax.ShapeDtypeStruct((m, n) if out is None else out[0], out_dtype),
        scratch_shapes=[] if nk == 1 else [pltpu.VMEM((tm, tn), F32)],
        compiler_params=_cp(("parallel", "parallel", "arbitrary")),
    )(a, b)


def _rstd(x):
    return lax.rsqrt(jnp.mean(x * x, axis=-1, keepdims=True) + NORM_EPS)


def _rms_bwd(x, r, g, dy):
    gy = dy * g
    return r * gy - x * ((r * r * r) * (jnp.sum(x * gy, axis=-1, keepdims=True) / x.shape[-1]))


TR = 256


def _norm_cast(x, g, name):
    def body(x_ref, g_ref, h_ref):
        x = x_ref[...]
        h_ref[...] = (x * _rstd(x) * g_ref[...]).astype(BF16)

    return pl.pallas_call(
        body, name=name, grid=(SEQ // TR,),
        in_specs=[_rows(TR, D_MODEL), _const((1, D_MODEL))],
        out_specs=_rows(TR, D_MODEL),
        out_shape=jax.ShapeDtypeStruct((SEQ, D_MODEL), BF16),
        compiler_params=_cp(("parallel",)),
    )(x, g)


def _mix_norm(x, mix, g2, g3):
    def body(x_ref, mix_ref, g2_ref, g3_ref, x2_ref, h3_ref):
        mixv = mix_ref[...]
        x2 = x_ref[...] + mixv * _rstd(mixv) * g2_ref[...]
        x2_ref[...] = x2
        h3_ref[...] = (x2 * _rstd(x2) * g3_ref[...]).astype(BF16)

    return pl.pallas_call(
        body, name="mix_norm", grid=(SEQ // TR,),
        in_specs=[_rows(TR, D_MODEL), _rows(TR, D_MODEL), _const((1, D_MODEL)), _const((1, D_MODEL))],
        out_specs=[_rows(TR, D_MODEL), _rows(TR, D_MODEL)],
        out_shape=[jax.ShapeDtypeStruct((SEQ, D_MODEL), F32), jax.ShapeDtypeStruct((SEQ, D_MODEL), BF16)],
        compiler_params=_cp(("parallel",)),
    )(x, mix, g2, g3)


def _loss_head(x2, f, g4, target):
    def body(x2_ref, f_ref, g4_ref, t_ref, loss_ref, dy_ref, df_ref, dg_ref):
        i = pl.program_id(0)
        f = f_ref[...]
        g4 = g4_ref[...]
        r = _rstd(f)
        e = x2_ref[...] + f * r * g4 - t_ref[...]
        dy = e * (1.0 / D_MODEL)
        dy_ref[...] = dy
        df_ref[...] = _rms_bwd(f, r, g4, dy).astype(BF16)
        part = 0.5 * jnp.sum(jnp.sum(e * e, axis=-1, keepdims=True), axis=0, keepdims=True) * (1.0 / D_MODEL)
        dg = jnp.sum(dy * f * r, axis=0, keepdims=True)

        @pl.when(i == 0)
        def _():
            loss_ref[...] = jnp.zeros_like(loss_ref)
            dg_ref[...] = jnp.zeros_like(dg_ref)

        loss_ref[...] += jnp.broadcast_to(part, loss_ref.shape)
        dg_ref[...] += dg

    return pl.pallas_call(
        body, name="loss_head", grid=(SEQ // TR,),
        in_specs=[_rows(TR, D_MODEL), _rows(TR, D_MODEL), _const((1, D_MODEL)), _rows(TR, D_MODEL)],
        out_specs=[_const((8, LANES)), _rows(TR, D_MODEL), _rows(TR, D_MODEL), _const((1, D_MODEL))],
        out_shape=[jax.ShapeDtypeStruct((8, LANES), F32), jax.ShapeDtypeStruct((SEQ, D_MODEL), F32),
                   jax.ShapeDtypeStruct((SEQ, D_MODEL), BF16), jax.ShapeDtypeStruct((1, D_MODEL), F32)],
        compiler_params=_cp(("arbitrary",)),
    )(x2, f, g4, target)


def _mid_bwd(x2, mix, dy, dh3, g2, g3):
    def body(x2_ref, mix_ref, dy_ref, dh3_ref, g2_ref, g3_ref, dx2_ref, dmix_ref, dg2_ref, dg3_ref):
        i = pl.program_id(0)
        x2 = x2_ref[...]
        mixv = mix_ref[...]
        dh3 = dh3_ref[...]
        r3 = _rstd(x2)
        dx2 = dy_ref[...] + _rms_bwd(x2, r3, g3_ref[...], dh3)
        dx2_ref[...] = dx2
        r2 = _rstd(mixv)
        dmix_ref[...] = _rms_bwd(mixv, r2, g2_ref[...], dx2).astype(BF16)

        @pl.when(i == 0)
        def _():
            dg2_ref[...] = jnp.zeros_like(dg2_ref)
            dg3_ref[...] = jnp.zeros_like(dg3_ref)

        dg3_ref[...] += jnp.sum(dh3 * x2 * r3, axis=0, keepdims=True)
        dg2_ref[...] += jnp.sum(dx2 * mixv * r2, axis=0, keepdims=True)

    return pl.pallas_call(
        body, name="mid_bwd", grid=(SEQ // TR,),
        in_specs=[_rows(TR, D_MODEL)] * 4 + [_const((1, D_MODEL))] * 2,
        out_specs=[_rows(TR, D_MODEL), _rows(TR, D_MODEL), _const((1, D_MODEL)), _const((1, D_MODEL))],
        out_shape=[jax.ShapeDtypeStruct((SEQ, D_MODEL), F32), jax.ShapeDtypeStruct((SEQ, D_MODEL), BF16),
                   jax.ShapeDtypeStruct((1, D_MODEL), F32), jax.ShapeDtypeStruct((1, D_MODEL), F32)],
        compiler_params=_cp(("arbitrary",)),
    )(x2, mix, dy, dh3, g2, g3)


def _first_bwd(x, dx2, dh1, g1):
    def body(x_ref, dx2_ref, dh1_ref, g1_ref, dx_ref, dg1_ref):
        i = pl.program_id(0)
        x = x_ref[...]
        dh1 = dh1_ref[...]
        r = _rstd(x)
        dx_ref[...] = dx2_ref[...] + _rms_bwd(x, r, g1_ref[...], dh1)

        @pl.when(i == 0)
        def _():
            dg1_ref[...] = jnp.zeros_like(dg1_ref)

        dg1_ref[...] += jnp.sum(dh1 * x * r, axis=0, keepdims=True)

    return pl.pallas_call(
        body, name="first_bwd", grid=(SEQ // TR,),
        in_specs=[_rows(TR, D_MODEL)] * 3 + [_const((1, D_MODEL))],
        out_specs=[_rows(TR, D_MODEL), _const((1, D_MODEL))],
        out_shape=[jax.ShapeDtypeStruct((SEQ, D_MODEL), F32), jax.ShapeDtypeStruct((1, D_MODEL), F32)],
        compiler_params=_cp(("arbitrary",)),
    )(x, dx2, dh1, g1)


TC = 256
N_CB = D_FF // TC
GELU_C = math.sqrt(2.0 / math.pi)


def _shift_down(u, s):
    rolled = pltpu.roll(u, s, 0)
    row = lax.broadcasted_iota(jnp.int32, u.shape, 0)
    return jnp.where(row >= s, rolled, 0.0)


def _shift_up(u, s):
    n = u.shape[0]
    rolled = pltpu.roll(u, n - s, 0)
    row = lax.broadcasted_iota(jnp.int32, u.shape, 0)
    return jnp.where(row < n - s, rolled, 0.0)


def _conv3(u, w, b):
    return b + w[0:1] * _shift_down(u, 2) + w[1:2] * _shift_down(u, 1) + w[2:3] * u


def _gelu_and_grad(x):
    inner = GELU_C * (x + 0.044715 * (x * x * x))
    t = jnp.tanh(inner)
    gelu = 0.5 * x * (1.0 + t)
    dgelu = 0.5 * (1.0 + t) + 0.5 * x * (1.0 - t * t) * (GELU_C * (1.0 + 3 * 0.044715 * (x * x)))
    return gelu, dgelu


def _ffn_specs():
    col = lambda off: pl.BlockSpec((SEQ, TC), lambda *g: (0, g[-1] + off))
    w = lambda off: pl.BlockSpec((3, TC), lambda *g: (0, g[-1] + off))
    b = lambda off: pl.BlockSpec((1, TC), lambda *g: (0, g[-1] + off))
    return col, w, b


def _ffn_act(u, conv_w, conv_b):
    col, w, b = _ffn_specs()

    def body(ug_ref, uv_ref, wg_ref, wv_ref, bg_ref, bv_ref, act_ref):
        gate = _conv3(ug_ref[...], wg_ref[...], bg_ref[...])
        val = _conv3(uv_ref[...], wv_ref[...], bv_ref[...])
        act_ref[...] = (_gelu_and_grad(gate)[0] * val).astype(BF16)

    return pl.pallas_call(
        body, name="ffn_act", grid=(N_CB,),
        in_specs=[col(0), col(N_CB), w(0), w(N_CB), b(0), b(N_CB)],
        out_specs=col(0),
        out_shape=jax.ShapeDtypeStruct((SEQ, D_FF), BF16),
        compiler_params=_cp(("parallel",)),
    )(u, u, conv_w, conv_w, conv_b, conv_b)


def _ffn_act_bwd(u, dact, conv_w, conv_b):
    col, w, b = _ffn_specs()
    half = lambda shape: pl.BlockSpec(shape, lambda h, j: (0, h * N_CB + j))

    def body(ug_ref, uv_ref, da_ref, wg_ref, wv_ref, bg_ref, bv_ref, du_ref, dw_ref, db_ref):
        h = pl.program_id(0)
        is_gate = h == 0
        ug = ug_ref[...]
        uv = uv_ref[...]
        gate = _conv3(ug, wg_ref[...], bg_ref[...])
        val = _conv3(uv, wv_ref[...], bv_ref[...])
        gelu, dgelu = _gelu_and_grad(gate)
        da = da_ref[...]
        duc = jnp.where(is_gate, da * val * dgelu, da * gelu)
        usel = jnp.where(is_gate, ug, uv)
        wsel = jnp.where(is_gate, wg_ref[...], wv_ref[...])
        du = wsel[2:3] * duc + wsel[1:2] * _shift_up(duc, 1) + wsel[0:1] * _shift_up(duc, 2)
        du_ref[...] = du.astype(BF16)
        db_ref[...] = jnp.sum(duc, axis=0, keepdims=True)
        dw_ref[...] = jnp.concatenate(
            [jnp.sum(duc * _shift_down(usel, 2), axis=0, keepdims=True),
             jnp.sum(duc * _shift_down(usel, 1), axis=0, keepdims=True),
             jnp.sum(duc * usel, axis=0, keepdims=True)], axis=0)

    return pl.pallas_call(
        body, name="ffn_act_bwd", grid=(2, N_CB),
        in_specs=[col(0), col(N_CB), col(0), w(0), w(N_CB), b(0), b(N_CB)],
        out_specs=[half((SEQ, TC)), half((3, TC)), half((1, TC))],
        out_shape=[jax.ShapeDtypeStruct((SEQ, 2 * D_FF), BF16), jax.ShapeDtypeStruct((3, 2 * D_FF), F32),
                   jax.ShapeDtypeStruct((1, 2 * D_FF), F32)],
        compiler_params=_cp(("parallel", "parallel")),
    )(u, u, dact, conv_w, conv_w, conv_b, conv_b)


def _t5_onehot():
    rel = (np.arange(BLOCK)[:, None] + BLOCK) - np.arange(2 * BLOCK)[None, :]
    n = np.maximum(rel, 0)
    max_exact = N_BUCKETS // 2
    large = max_exact + (np.log(np.maximum(n, 1).astype(np.float32) / np.float32(max_exact))
                         / np.float32(math.log(MAX_DISTANCE / max_exact))
                         * np.float32(N_BUCKETS - max_exact)).astype(np.int32)
    large = np.minimum(large, N_BUCKETS - 1)
    bucket = np.where(n < max_exact, n, large).reshape(-1)
    return (bucket[None, :] == np.arange(N_BUCKETS)[:, None]).astype(np.float32)


N_REL = BLOCK * 2 * BLOCK


def _bias_table(rel_bias_t, onehot):
    def body(rb_ref, oh_ref, o_ref):
        o_ref[...] = _dot_ind(rb_ref[...], oh_ref[...])

    return pl.pallas_call(
        body, name="bias_table", grid=(1,),
        in_specs=[_const((N_Q_HEADS, N_BUCKETS)), _const((N_BUCKETS, N_REL))],
        out_specs=_const((N_Q_HEADS, N_REL)),
        out_shape=jax.ShapeDtypeStruct((N_Q_HEADS, N_REL), F32),
        compiler_params=_cp(("arbitrary",)),
    )(rel_bias_t, onehot)


def _bias_table_bwd(dbias, onehot):
    def body(db_ref, oh_ref, o_ref):
        acc = None
        for part in _split(db_ref[...], 3):
            t = _dot(part, oh_ref[...], NT)
            acc = t if acc is None else acc + t
        o_ref[...] = acc

    return pl.pallas_call(
        body, name="bias_table_bwd", grid=(1,),
        in_specs=[_const((N_Q_HEADS, N_REL)), _const((N_BUCKETS, N_REL))],
        out_specs=_const((N_Q_HEADS, N_BUCKETS)),
        out_shape=jax.ShapeDtypeStruct((N_Q_HEADS, N_BUCKETS), F32),
        compiler_params=_cp(("arbitrary",)),
    )(dbias, onehot)


def _attn_pieces(n, q, kvp, kvc, bias_ref, sinks_ref, hk):
    qi = lax.broadcasted_iota(jnp.int32, (BLOCK, 2 * BLOCK), 0)
    kj = lax.broadcasted_iota(jnp.int32, (BLOCK, 2 * BLOCK), 1)
    rel = qi + BLOCK - kj
    first_key = jnp.where(n > 0, 0, BLOCK)
    ok = jnp.where(rel >= 0, jnp.where(rel < BLOCK, jnp.where(kj >= first_key, 1.0, 0.0), 0.0), 0.0)
    ok4 = jnp.concatenate([ok] * Q_PER_KV, axis=0) > 0.5
    c0 = hk * HEAD_DIM
    kcat = jnp.concatenate([kvp[:, c0:c0 + HEAD_DIM], kvc[:, c0:c0 + HEAD_DIM]], axis=0).astype(BF16)
    vcat = jnp.concatenate([kvp[:, D_KV + c0:D_KV + c0 + HEAD_DIM], kvc[:, D_KV + c0:D_KV + c0 + HEAD_DIM]],
                           axis=0).astype(BF16)
    q0 = hk * Q_PER_KV * HEAD_DIM
    qs = jnp.concatenate([q[:, q0 + g * HEAD_DIM:q0 + (g + 1) * HEAD_DIM] for g in range(Q_PER_KV)],
                         axis=0).astype(BF16)
    s = _dot(qs, kcat, NT) * (HEAD_DIM ** -0.5) + bias_ref[hk]
    s = jnp.where(ok4, s, NEG_INF)
    row = lax.broadcasted_iota(jnp.int32, (Q_PER_KV * BLOCK, 1), 0)
    sink = jnp.zeros((Q_PER_KV * BLOCK, 1), F32)
    for g in range(Q_PER_KV):
        sink = jnp.where((row >> 7) == g, sinks_ref[hk * Q_PER_KV + g], sink)
    m = jnp.maximum(jnp.max(s, axis=-1, keepdims=True), sink)
    p = jnp.exp(s - m)
    es = jnp.exp(sink - m)
    inv = 1.0 / (jnp.sum(p, axis=-1, keepdims=True) + es)
    return qs, kcat, vcat, p * inv, es * inv


def _attn_in_specs():
    return [pl.BlockSpec((BLOCK, D_ATTN), lambda n: (n, 0)),
            pl.BlockSpec((BLOCK, 2 * D_KV), lambda n: (jnp.maximum(n - 1, 0), D_ATTN // (2 * D_KV))),
            pl.BlockSpec((BLOCK, 2 * D_KV), lambda n: (n, D_ATTN // (2 * D_KV))),
            _const((N_KV_HEADS, Q_PER_KV * BLOCK, 2 * BLOCK)),
            pl.BlockSpec(memory_space=pltpu.SMEM)]


def _unstack_heads(t):
    return jnp.concatenate([t[g * BLOCK:(g + 1) * BLOCK] for g in range(Q_PER_KV)], axis=1)


def _attn_fwd(proj, bias, sinks):
    def body(q_ref, kvp_ref, kvc_ref, bias_ref, sinks_ref, o_ref):
        n = pl.program_id(0)
        q, kvp, kvc = q_ref[...], kvp_ref[...], kvc_ref[...]
        outs = []
        for hk in range(N_KV_HEADS):
            _, _, vcat, probs, _ = _attn_pieces(n, q, kvp, kvc, bias_ref, sinks_ref, hk)
            outs.append(_unstack_heads(_dot(probs.astype(BF16), vcat)))
        o_ref[...] = jnp.concatenate(outs, axis=1)

    return pl.pallas_call(
        body, name="attn_fwd", grid=(SEQ // BLOCK,),
        in_specs=_attn_in_specs(),
        out_specs=pl.BlockSpec((BLOCK, D_ATTN), lambda n: (n, 0)),
        out_shape=jax.ShapeDtypeStruct((SEQ, D_ATTN), F32),
        compiler_params=_cp(("parallel",)),
    )(proj, proj, proj, bias, sinks)


def _attn_bwd(proj, bias, sinks, dcat):
    nb = SEQ // BLOCK

    def body(q_ref, kvp_ref, kvc_ref, bias_ref, sinks_ref, do_ref, dq_ref, dkv_ref, dbias_ref, dsink_ref, dsacc):
        n = pl.program_id(0)

        @pl.when(n == 0)
        def _():
            dkv_ref[...] = jnp.zeros_like(dkv_ref)
            dbias_ref[...] = jnp.zeros_like(dbias_ref)
            dsacc[...] = jnp.zeros_like(dsacc)

        q, kvp, kvc = q_ref[...], kvp_ref[...], kvc_ref[...]
        do_all = do_ref[...]
        dqs, dks, dvs = [], [], []
        for hk in range(N_KV_HEADS):
            qs, kcat, vcat, probs, psink = _attn_pieces(n, q, kvp, kvc, bias_ref, sinks_ref, hk)
            q0 = hk * Q_PER_KV * HEAD_DIM
            do = jnp.concatenate([do_all[:, q0 + g * HEAD_DIM:q0 + (g + 1) * HEAD_DIM] for g in range(Q_PER_KV)],
                                 axis=0).astype(BF16)
            dprobs = _dot(do, vcat, NT)
            dvs.append(_dot(probs.astype(BF16), do, TN))
            rowdot = jnp.sum(probs * dprobs, axis=-1, keepdims=True)
            ds = probs * (dprobs - rowdot)
            dsacc[hk] += -psink * rowdot
            dbias_ref[hk] += ds
            dsb = (ds * (HEAD_DIM ** -0.5)).astype(BF16)
            dqs.append(_unstack_heads(_dot(dsb, kcat)))
            dks.append(_dot(dsb, qs, TN))
        dq_ref[...] = jnp.concatenate(dqs, axis=1)
        upd = jnp.concatenate(dks + dvs, axis=1)
        cur = pl.multiple_of(n * BLOCK, BLOCK)
        dkv_ref[pl.ds(cur, BLOCK), :] += upd[BLOCK:]

        @pl.when(n > 0)
        def _():
            prev = pl.multiple_of((n - 1) * BLOCK, BLOCK)
            dkv_ref[pl.ds(prev, BLOCK), :] += upd[:BLOCK]

        @pl.when(n == nb - 1)
        def _():
            for hk in range(N_KV_HEADS):
                for g in range(Q_PER_KV):
                    tot = jnp.sum(dsacc[hk, g * BLOCK:(g + 1) * BLOCK, :], axis=0, keepdims=True)
                    h = hk * Q_PER_KV + g
                    dsink_ref[h:h + 1, :] = jnp.broadcast_to(tot, (1, LANES))

    return pl.pallas_call(
        body, name="attn_bwd", grid=(nb,),
        in_specs=_attn_in_specs() + [pl.BlockSpec((BLOCK, D_ATTN), lambda n: (n, 0))],
        out_specs=[pl.BlockSpec((BLOCK, D_ATTN), lambda n: (n, 0)), _const((SEQ, 2 * D_KV)),
                   _const((N_KV_HEADS, Q_PER_KV * BLOCK, 2 * BLOCK)), _const((N_Q_HEADS, LANES))],
        out_shape=[jax.ShapeDtypeStruct((SEQ, D_ATTN), F32), jax.ShapeDtypeStruct((SEQ, 2 * D_KV), F32),
                   jax.ShapeDtypeStruct((N_KV_HEADS, Q_PER_KV * BLOCK, 2 * BLOCK), F32),
                   jax.ShapeDtypeStruct((N_Q_HEADS, LANES), F32)],
        scratch_shapes=[pltpu.VMEM((N_KV_HEADS, Q_PER_KV * BLOCK, 1), F32)],
        compiler_params=_cp(("arbitrary",)),
    )(proj, proj, proj, bias, sinks, dcat)


@jax.custom_vjp
def _head_sum(x):
    return _dot_ind(x, _head_ones(x.shape[-1]))


_head_sum.defvjp(lambda x: (_head_sum(x), None), lambda _, ct: (_head_sum(ct),))


@jax.custom_vjp
def _bdot(a, w):
    return _dot(a.astype(BF16), w.astype(BF16))


def _bdot_bwd(res, ct):
    a, w = res
    ctb = ct.astype(BF16)
    return _dot(ctb, w.astype(BF16), NT), _dot(a.astype(BF16), ctb, TN)


_bdot.defvjp(lambda a, w: (_bdot(a, w), (a, w)), _bdot_bwd)


def _sigmoid(x):
    return 0.5 * (jnp.tanh(0.5 * x) + 1.0)


def _softplus(x):
    return jnp.maximum(x, 0.0) + jnp.log(1.0 + jnp.exp(-jnp.abs(x)))


def _rwkv_core(r, k, v, zwa, zg, w0, wdu, a0, wiu, wgu, k_k, k_a):
    w_log = -_softplus(-(w0 + _bdot(jnp.tanh(zwa), wdu))) - 0.5
    decay = jnp.exp(-jnp.exp(w_log))
    a = _sigmoid(a0 + _bdot(zwa, wiu))
    g = _bdot(_sigmoid(zg), wgu)
    kk = k * k_k
    kk = kk / jnp.maximum(jnp.sqrt(_head_sum(kk * kk)), 1e-12)
    k2 = k * (1.0 + (a - 1.0) * k_a)
    return r, decay, k2, v, -kk, kk * a, g


def _rwkv_out(o, r, k2, v, g, lng, lnb, rk):
    mu = _head_sum(o) * (1.0 / HEAD_DIM)
    d = o - mu
    var = _head_sum(d * d) * (1.0 / HEAD_DIM)
    on = d * lax.rsqrt(var + GN_EPS) * lng + lnb
    bonus = _head_sum(r * k2 * rk) * v
    return (on + bonus) * g


P_SPLITS = (0, 512, 1024, 1536, 1664, 1792)
N_PREP_PARAMS = 7
HALO = 8


def _shifted_pieces(i, p_ref, halo_ref, mix_ref):
    p = p_ref[:, P_OFF:]
    prev_row = halo_ref[HALO - 1:HALO, P_OFF:] * jnp.where(i > 0, 1.0, 0.0)
    row = lax.broadcasted_iota(jnp.int32, p.shape, 0)
    pprev = jnp.where(row == 0, prev_row, pltpu.roll(p, 1, 0))
    delta = pprev - p
    ps = p + delta * mix_ref[...]
    return [ps[:, a:b] for a, b in zip(P_SPLITS[:-1], P_SPLITS[1:])], delta


def _prep_in_specs():
    return [_rows(TR, D_IN),
            pl.BlockSpec((HALO, D_IN), lambda i: (jnp.maximum(i * (TR // HALO) - 1, 0), 0)),
            _const((1, RWKV_COLS)), _const((1, D_RWKV)), _const((LANES, D_RWKV)), _const((1, D_RWKV)),
            _const((LANES, D_RWKV)), _const((LANES, D_RWKV)), _const((1, D_RWKV)), _const((1, D_RWKV))]


def _rwkv_prep(proj, mix, prm):
    def body(p_ref, halo_ref, mix_ref, *refs):
        prm_refs, outs = refs[:N_PREP_PARAMS], refs[N_PREP_PARAMS:]
        pieces, _ = _shifted_pieces(pl.program_id(0), p_ref, halo_ref, mix_ref)
        vals = _rwkv_core(*pieces, *[t[...] for t in prm_refs])
        for ref, val in zip(outs, vals):
            ref[...] = val

    return pl.pallas_call(
        body, name="rwkv_prep", grid=(SEQ // TR,),
        in_specs=_prep_in_specs(),
        out_specs=[_rows(TR, D_RWKV)] * 7,
        out_shape=[jax.ShapeDtypeStruct((SEQ, D_RWKV), F32)] * 7,
        compiler_params=_cp(("parallel",)),
    )(proj, proj, mix, *prm)


def _rwkv_prep_bwd(proj, mix, prm, cts):
    def body(p_ref, halo_ref, mix_ref, *refs):
        i = pl.program_id(0)
        prm_refs = refs[:N_PREP_PARAMS]
        ct_refs = refs[N_PREP_PARAMS:N_PREP_PARAMS + 10]
        dps_ref, dmix_ref = refs[N_PREP_PARAMS + 10:N_PREP_PARAMS + 12]
        dprm_refs = refs[N_PREP_PARAMS + 12:]
        pieces, delta = _shifted_pieces(i, p_ref, halo_ref, mix_ref)
        _, vjp = jax.vjp(_rwkv_core, *pieces, *[t[...] for t in prm_refs])
        dr1, dr2, dw, dk1, dk2, dv1, dv2, dkkn, db, dg = [t[...] for t in ct_refs]
        grads = vjp((dr1 + dr2, dw, dk1 + dk2, dv1 + dv2, dkkn, db, dg))
        dps = jnp.concatenate(grads[:5], axis=1)
        dps_ref[...] = dps

        @pl.when(i == 0)
        def _():
            dmix_ref[...] = jnp.zeros_like(dmix_ref)
            for ref in dprm_refs:
                ref[...] = jnp.zeros_like(ref)

        dmix_ref[...] += jnp.sum(dps * delta, axis=0, keepdims=True)
        for ref, gval in zip(dprm_refs, grads[5:]):
            ref[...] += gval

    prm_shapes = [(1, D_RWKV), (LANES, D_RWKV), (1, D_RWKV), (LANES, D_RWKV), (LANES, D_RWKV), (1, D_RWKV), (1, D_RWKV)]
    return pl.pallas_call(
        body, name="rwkv_prep_bwd", grid=(SEQ // TR,),
        in_specs=_prep_in_specs() + [_rows(TR, D_RWKV)] * 10,
        out_specs=[_rows(TR, RWKV_COLS), _const((1, RWKV_COLS))] + [_const(s) for s in prm_shapes],
        out_shape=[jax.ShapeDtypeStruct((SEQ, RWKV_COLS), F32), jax.ShapeDtypeStruct((1, RWKV_COLS), F32)]
        + [jax.ShapeDtypeStruct(s, F32) for s in prm_shapes],
        compiler_params=_cp(("arbitrary",)),
    )(proj, proj, mix, *prm, *cts)


def _rwkv_post(o, r, k2, v, g, lng, lnb, rk, attn):
    def body(o_ref, r_ref, k_ref, v_ref, g_ref, lng_ref, lnb_ref, rk_ref, attn_ref, cat_ref):
        rw = _rwkv_out(*[t[...] for t in (o_ref, r_ref, k_ref, v_ref, g_ref, lng_ref, lnb_ref, rk_ref)])
        cat_ref[...] = jnp.concatenate([attn_ref[...], rw], axis=1).astype(BF16)

    return pl.pallas_call(
        body, name="rwkv_post", grid=(SEQ // TR,),
        in_specs=[_rows(TR, D_RWKV)] * 5 + [_const((1, D_RWKV))] * 3 + [_rows(TR, D_ATTN)],
        out_specs=_rows(TR, D_MODEL),
        out_shape=jax.ShapeDtypeStruct((SEQ, D_MODEL), BF16),
        compiler_params=_cp(("parallel",)),
    )(o, r, k2, v, g, lng, lnb, rk, attn)


def _rwkv_post_bwd(o, r, k2, v, g, lng, lnb, rk, dcat):
    def body(o_ref, r_ref, k_ref, v_ref, g_ref, lng_ref, lnb_ref, rk_ref, dcat_ref,
             do_ref, dr_ref, dk_ref, dv_ref, dg_ref, dlng_ref, dlnb_ref, drk_ref):
        i = pl.program_id(0)
        args = [t[...] for t in (o_ref, r_ref, k_ref, v_ref, g_ref, lng_ref, lnb_ref, rk_ref)]
        _, vjp = jax.vjp(_rwkv_out, *args)
        grads = vjp(dcat_ref[:, D_ATTN:])
        for ref, gval in zip((do_ref, dr_ref, dk_ref, dv_ref, dg_ref), grads[:5]):
            ref[...] = gval

        @pl.when(i == 0)
        def _():
            for ref in (dlng_ref, dlnb_ref, drk_ref):
                ref[...] = jnp.zeros_like(ref)

        for ref, gval in zip((dlng_ref, dlnb_ref, drk_ref), grads[5:]):
            ref[...] += gval

    return pl.pallas_call(
        body, name="rwkv_post_bwd", grid=(SEQ // TR,),
        in_specs=[_rows(TR, D_RWKV)] * 5 + [_const((1, D_RWKV))] * 3 + [_rows(TR, D_MODEL)],
        out_specs=[_rows(TR, D_RWKV)] * 5 + [_const((1, D_RWKV))] * 3,
        out_shape=[jax.ShapeDtypeStruct((SEQ, D_RWKV), F32)] * 5 + [jax.ShapeDtypeStruct((1, D_RWKV), F32)] * 3,
        compiler_params=_cp(("arbitrary",)),
    )(o, r, k2, v, g, lng, lnb, rk, dcat)


def _assemble_dproj(dq, dkv, dps, mix):
    last = SEQ // HALO - 1

    def body(dq_ref, dkv_ref, dps_ref, nxt_ref, mix_ref, o_ref):
        i = pl.program_id(0)
        dps = dps_ref[...]
        mixv = mix_ref[...]
        nxt_row = nxt_ref[0:1, :] * jnp.where(i < SEQ // TR - 1, 1.0, 0.0)
        row = lax.broadcasted_iota(jnp.int32, dps.shape, 0)
        up = jnp.where(row == TR - 1, nxt_row, pltpu.roll(dps, TR - 1, 0))
        dp = dps * (1.0 - mixv) + up * mixv
        o_ref[...] = jnp.concatenate([dq_ref[...], dkv_ref[...], dp], axis=1).astype(BF16)

    return pl.pallas_call(
        body, name="assemble_dproj", grid=(SEQ // TR,),
        in_specs=[_rows(TR, D_ATTN), _rows(TR, 2 * D_KV), _rows(TR, RWKV_COLS),
                  pl.BlockSpec((HALO, RWKV_COLS), lambda i: (jnp.minimum((i + 1) * (TR // HALO), last), 0)),
                  _const((1, RWKV_COLS))],
        out_specs=_rows(TR, D_IN),
        out_shape=jax.ShapeDtypeStruct((SEQ, D_IN), BF16),
        compiler_params=_cp(("parallel",)),
    )(dq, dkv, dps, dps, mix)


N_PAIR = D_RWKV // LANES
CHUNK = 2 * SCAN_T
N_CHUNK = SEQ // CHUNK
GROUP = 8


def _lane_sums(lhs_tiles, ones2):
    out = _dot(jnp.concatenate(lhs_tiles, axis=0), ones2)
    return [out[i * HEAD_DIM:(i + 1) * HEAD_DIM] for i in range(len(lhs_tiles))]


def _seg_sum(xs, ones2):
    return _lane_sums([jnp.concatenate(_split(x, 2), axis=1) for x in xs], ones2)


def _col_form(rows, diag, ones2):
    zero = jnp.zeros((HEAD_DIM, LANES), BF16)
    tiles = []
    for row in rows:
        hi = row.astype(BF16)
        lo = (row - hi.astype(F32)).astype(BF16)
        tiles.append(jnp.concatenate(
            [jnp.where(diag, jnp.broadcast_to(part, (HEAD_DIM, LANES)), zero) for part in (hi, lo)], axis=1))
    return _lane_sums(tiles, ones2)


def _scan_consts():
    ones2 = jnp.concatenate([_head_ones(LANES)] * 2, axis=0)
    sub = lax.broadcasted_iota(jnp.int32, (HEAD_DIM, LANES), 0)
    lane_in_head = lax.broadcasted_iota(jnp.int32, (HEAD_DIM, LANES), 1) & (HEAD_DIM - 1)
    return ones2, lane_in_head == sub, lane_in_head


def _rows_of_columns(tile):
    t = tile.T
    return jnp.concatenate([t[:HEAD_DIM], t[HEAD_DIM:]], axis=1)


def _pair(j):
    return slice(j * LANES, (j + 1) * LANES)


def _scan_fwd(r, w, k, v, kkn, b):
    def body(r_ref, w_ref, k_ref, v_ref, kkn_ref, b_ref, o_ref, ckpt_ref, s_scr):
        c = pl.program_id(0)
        ones2, diag, lane_in_head = _scan_consts()

        @pl.when(c == 0)
        def _():
            s_scr[...] = jnp.zeros_like(s_scr)

        for sub in range(CHUNK // SCAN_T):
            ckpt_ref[sub] = s_scr[...]

            def group(gi, carry, sub=sub):
                row0 = pl.multiple_of(sub * SCAN_T + gi * GROUP, GROUP)
                states, ocols = list(carry[:N_PAIR]), list(carry[N_PAIR:])
                tiles = [[t[pl.ds(row0, GROUP), _pair(j)] for t in (r_ref, w_ref, k_ref, v_ref, kkn_ref, b_ref)]
                         for j in range(N_PAIR)]
                def row(j, name, u):
                    return tiles[j]["rwkvnb".index(name)][u:u + 1]

                def emit_out(u, after):
                    here = lane_in_head == gi * GROUP + u
                    outs = _seg_sum([after[j] * row(j, "r", u) for j in range(N_PAIR)], ones2)
                    for j in range(N_PAIR):
                        ocols[j] = jnp.where(here, outs[j], ocols[j])

                vcols = _col_form([row(j, "v", 0) for j in range(N_PAIR)], diag, ones2)
                after = None
                for u in range(GROUP):
                    sas = _seg_sum([states[j] * row(j, "n", u) for j in range(N_PAIR)], ones2)
                    if after is not None:
                        emit_out(u - 1, after)
                    nxt = (_col_form([row(j, "v", u + 1) for j in range(N_PAIR)], diag, ones2)
                           if u + 1 < GROUP else None)
                    for j in range(N_PAIR):
                        states[j] = states[j] * row(j, "w", u) + sas[j] * row(j, "b", u) + vcols[j] * row(j, "k", u)
                    after, vcols = list(states), nxt
                emit_out(GROUP - 1, after)
                return tuple(states + ocols)

            zero = jnp.zeros((HEAD_DIM, LANES), F32)
            fin = lax.fori_loop(0, SCAN_T // GROUP, group,
                                tuple(s_scr[j] for j in range(N_PAIR)) + (zero,) * N_PAIR)
            for j in range(N_PAIR):
                s_scr[j] = fin[j]
                o_ref[sub * SCAN_T:(sub + 1) * SCAN_T, _pair(j)] = _rows_of_columns(fin[N_PAIR + j])

    blk = pl.BlockSpec((CHUNK, D_RWKV), lambda c: (c, 0))
    return pl.pallas_call(
        body, name="rwkv_scan_fwd", grid=(N_CHUNK,),
        in_specs=[blk] * 6,
        out_specs=[blk, pl.BlockSpec((CHUNK // SCAN_T, N_PAIR, HEAD_DIM, LANES), lambda c: (c, 0, 0, 0))],
        out_shape=[jax.ShapeDtypeStruct((SEQ, D_RWKV), F32),
                   jax.ShapeDtypeStruct((SEQ // SCAN_T, N_PAIR, HEAD_DIM, LANES), F32)],
        scratch_shapes=[pltpu.VMEM((N_PAIR, HEAD_DIM, LANES), F32)],
        compiler_params=_cp(("arbitrary",)),
    )(r, w, k, v, kkn, b)


def _scan_bwd(r, w, k, v, kkn, b, do, ckpt, ds_in, prev, name, first_chunk, n_chunks):
    top = first_chunk + n_chunks - 1

    def body(r_ref, w_ref, k_ref, v_ref, kkn_ref, b_ref, do_ref, ckpt_ref, ds_in_ref, *rest):
        dr_ref, dw_ref, dk_ref, dv_ref, dkkn_ref, db_ref, ds_out_ref, ds_scr, st_scr, sa_scr = rest[-10:]
        i = pl.program_id(0)
        ones2, diag, lane_in_head = _scan_consts()

        @pl.when(i == 0)
        def _():
            ds_scr[...] = ds_in_ref[...]

        for sub in reversed(range(CHUNK // SCAN_T)):
            def recompute(gi, states, sub=sub):
                row0 = pl.multiple_of(sub * SCAN_T + gi * GROUP, GROUP)
                states = list(states)
                tiles = [[t[pl.ds(row0, GROUP), _pair(j)] for t in (w_ref, k_ref, v_ref, kkn_ref, b_ref)]
                         for j in range(N_PAIR)]
                for u in range(GROUP):
                    rows = [[t[u:u + 1] for t in tiles[j]] for j in range(N_PAIR)]
                    sas = _seg_sum([states[j] * rows[j][3] for j in range(N_PAIR)], ones2)
                    vcols = _col_form([rows[j][2] for j in range(N_PAIR)], diag, ones2)
                    for j in range(N_PAIR):
                        w_t, k_t, _, _, b_t = rows[j]
                        st_scr[gi * GROUP + u, j] = states[j]
                        sa_scr[gi * GROUP + u, j] = sas[j]
                        states[j] = states[j] * w_t + sas[j] * b_t + vcols[j] * k_t
                return tuple(states)

            fin = lax.fori_loop(0, SCAN_T // GROUP, recompute, tuple(ckpt_ref[sub, j] for j in range(N_PAIR)))
            for j in range(N_PAIR):
                st_scr[SCAN_T, j] = fin[j]

            def reverse(gr, carry, sub=sub):
                gi = SCAN_T // GROUP - 1 - gr
                row0 = pl.multiple_of(sub * SCAN_T + gi * GROUP, GROUP)
                dstates, dvcols = list(carry[:N_PAIR]), list(carry[N_PAIR:])
                tiles = [[t[pl.ds(row0, GROUP), _pair(j)]
                          for t in (r_ref, w_ref, k_ref, v_ref, kkn_ref, b_ref, do_ref)] for j in range(N_PAIR)]
                rows = [[[None] * GROUP for _ in range(5)] for _ in range(N_PAIR)]

                def row(j, name, u):
                    return tiles[j]["rwkvnbd".index(name)][u:u + 1]

                def cols_of(u):
                    both = _col_form([row(j, "d", u) for j in range(N_PAIR)] + [row(j, "v", u) for j in range(N_PAIR)],
                                     diag, ones2)
                    return [(both[j], both[N_PAIR + j]) for j in range(N_PAIR)]

                def emit_dv(u, dsp):
                    here = lane_in_head == gi * GROUP + u
                    outs = _seg_sum([dsp[j] * row(j, "k", u) for j in range(N_PAIR)], ones2)
                    for j in range(N_PAIR):
                        dvcols[j] = jnp.where(here, outs[j], dvcols[j])

                cols = cols_of(GROUP - 1)
                before = None
                for u in reversed(range(GROUP)):
                    tl = gi * GROUP + u
                    dsp = [dstates[j] + cols[j][0] * row(j, "r", u) for j in range(N_PAIR)]
                    dsas = _seg_sum([dsp[j] * row(j, "b", u) for j in range(N_PAIR)], ones2)
                    if before is not None:
                        emit_dv(u + 1, before)
                    nxt = cols_of(u - 1) if u > 0 else None
                    for j in range(N_PAIR):
                        s_prev = st_scr[tl, j]
                        docol, vcol = cols[j]
                        rows[j][0][u] = jnp.sum(st_scr[tl + 1, j] * docol, axis=0, keepdims=True)
                        rows[j][1][u] = jnp.sum(dsp[j] * s_prev, axis=0, keepdims=True)
                        rows[j][2][u] = jnp.sum(dsp[j] * vcol, axis=0, keepdims=True)
                        rows[j][3][u] = jnp.sum(s_prev * dsas[j], axis=0, keepdims=True)
                        rows[j][4][u] = jnp.sum(dsp[j] * sa_scr[tl, j], axis=0, keepdims=True)
                        dstates[j] = dsp[j] * row(j, "w", u) + dsas[j] * row(j, "n", u)
                    before, cols = dsp, nxt
                emit_dv(0, before)
                for j in range(N_PAIR):
                    for ref, rr in zip((dr_ref, dw_ref, dk_ref, dkkn_ref, db_ref), rows[j]):
                        ref[pl.ds(row0, GROUP), _pair(j)] = jnp.concatenate(rr, axis=0)
                return tuple(dstates + dvcols)

            zero = jnp.zeros((HEAD_DIM, LANES), F32)
            dfin = lax.fori_loop(0, SCAN_T // GROUP, reverse,
                                 tuple(ds_scr[j] for j in range(N_PAIR)) + (zero,) * N_PAIR)
            for j in range(N_PAIR):
                ds_scr[j] = dfin[j]
                dv_ref[sub * SCAN_T:(sub + 1) * SCAN_T, _pair(j)] = _rows_of_columns(dfin[N_PAIR + j])

        @pl.when(i == n_chunks - 1)
        def _():
            ds_out_ref[...] = ds_scr[...]

    blk = pl.BlockSpec((CHUNK, D_RWKV), lambda i: (top - i, 0))
    state = (N_PAIR, HEAD_DIM, LANES)
    prev = [] if prev is None else list(prev)
    outs = pl.pallas_call(
        body, name=name, grid=(n_chunks,),
        in_specs=[blk] * 7 + [pl.BlockSpec((CHUNK // SCAN_T,) + state, lambda i: (top - i, 0, 0, 0)), _const(state)]
        + [ANY] * len(prev),
        out_specs=[blk] * 6 + [_const(state)],
        out_shape=[jax.ShapeDtypeStruct((SEQ, D_RWKV), F32)] * 6 + [jax.ShapeDtypeStruct(state, F32)],
        scratch_shapes=[pltpu.VMEM(state, F32), pltpu.VMEM((SCAN_T + 1,) + state, F32),
                        pltpu.VMEM((SCAN_T,) + state, F32)],
        input_output_aliases={9 + t: t for t in range(len(prev))},
        compiler_params=_cp(("arbitrary",)),
    )(r, w, k, v, kkn, b, do, ckpt, ds_in, *prev)
    return outs[:6], outs[6]


def _stacked(rows, cols, pick):
    return pl.BlockSpec((None, rows, cols), pick)


def _local_step(x, target, sm, win_st):
    def tied(t, token):
        return t if token is None else t + token[0:1, 0:1].reshape((1,) * t.ndim)

    zpad = jnp.zeros((LORA_DECAY, D_RWKV), F32)
    prm = [sm["w0"], jnp.concatenate([sm["w_decay_up"], zpad], axis=0), sm["a0"],
           jnp.concatenate([zpad, sm["w_iclr_up"]], axis=0), sm["w_gate_up"], sm["k_k"], sm["k_a"]]
    mix = sm["rwkv_shift_mix"]
    onehot = jnp.asarray(_t5_onehot(), BF16)
    sinks = sm["sinks"].reshape(N_Q_HEADS)
    lng, lnb, rk = sm["ln_x_g"], sm["ln_x_b"], sm["r_k"].reshape(1, D_RWKV)

    h1 = _norm_cast(x, sm["norm_mix_pre"], "norm_in")
    proj = _matmul(h1, win_st, "nn", "proj", m=SEQ, n=D_IN, k=D_MODEL, tm=SEQ, tn=640, tk=D_MODEL,
                   b_spec=_stacked(D_MODEL, 640, lambda i, j, kk: (j, 0, 0)))
    bias = _bias_table(sm["rel_bias"].T, onehot).reshape(N_KV_HEADS, Q_PER_KV * BLOCK, 2 * BLOCK)
    attn = _attn_fwd(proj, bias, sinks)
    r, w, k2, v, kkn, b, g = _rwkv_prep(proj, mix, prm)
    o, ckpt = _scan_fwd(r, w, k2, v, kkn, b)
    wout, wup_st, wdown = yield ("rest_weights", o)
    cat = _rwkv_post(o, r, k2, v, g, lng, lnb, rk, attn)
    mixo = _matmul(cat, wout, "nn", "out_proj", m=SEQ, n=D_MODEL, k=D_MODEL, tm=SEQ, tn=512, tk=D_MODEL)
    x2, h3 = _mix_norm(x, mixo, sm["norm_mix_post"], sm["norm_ffn_pre"])
    u = _matmul(h3, wup_st, "nn", "ffn_up", m=SEQ, n=2 * D_FF, k=D_MODEL, tm=SEQ, tn=512, tk=D_MODEL,
                b_spec=_stacked(D_MODEL, 512, lambda i, j, kk: (j // 4, 0, j % 4)))
    act = _ffn_act(u, sm["conv_w"], sm["conv_b"])
    f = _matmul(act, wdown, "nn", "ffn_down", m=SEQ, n=D_MODEL, k=D_FF, tm=1024, tn=512, tk=2048)
    loss, dy, df, d_g4 = _loss_head(x2, f, sm["norm_ffn_post"], target)

    dact = _matmul(df, wdown, "nt", "d_act", m=SEQ, n=D_FF, k=D_MODEL, tm=SEQ, tn=512, tk=D_MODEL)
    d_wdown = _matmul(act, df, "tn", "d_wdown", m=D_FF, n=D_MODEL, k=SEQ, tm=512, tn=D_MODEL, tk=SEQ)
    du, d_convw, d_convb = _ffn_act_bwd(u, dact, sm["conv_w"], sm["conv_b"])
    dh3 = _matmul(du, wup_st, "nt", "d_h3", m=SEQ, n=D_MODEL, k=2 * D_FF, tm=1024, tn=D_MODEL, tk=2048,
                  b_spec=_stacked(D_MODEL, 2048, lambda i, j, kk: (kk, j, 0)))
    d_wup = _matmul(h3, du, "tn", "d_wup", m=D_MODEL, n=2 * D_FF, k=SEQ, tm=D_MODEL, tn=512, tk=SEQ,
                    out=((N_CHIPS, D_MODEL, 2048), _stacked(D_MODEL, 512, lambda i, j, kk: (j // 4, 0, j % 4))))
    dx2, dmix, d_g2, d_g3 = _mid_bwd(x2, mixo, dy, dh3, sm["norm_mix_post"], sm["norm_ffn_pre"])
    dcat = _matmul(dmix, wout, "nt", "d_cat", m=SEQ, n=D_MODEL, k=D_MODEL, tm=SEQ, tn=512, tk=D_MODEL)
    d_wout = _matmul(cat, dmix, "tn", "d_wout", m=D_MODEL, n=D_MODEL, k=SEQ, tm=512, tn=D_MODEL, tk=SEQ)
    token = yield ("grads_a", (d_wdown, d_wup, d_wout))
    do, dr_p, dk_p, dv_p, dg, d_lng, d_lnb, d_rk = _rwkv_post_bwd(o, r, k2, v, g, lng, tied(lnb, token), rk, dcat)
    half = N_CHUNK // 2
    ds_end = jnp.zeros((N_PAIR, HEAD_DIM, LANES), F32)
    late, ds_mid = _scan_bwd(r, w, k2, v, kkn, b, do, ckpt, ds_end, None, "rwkv_scan_bwd_late", half, half)
    token = yield ("seam_1", ds_mid)
    scan_cts, ds_first = _scan_bwd(r, w, k2, v, kkn, b, do, ckpt, tied(ds_mid, token), late,
                                   "rwkv_scan_bwd_early", 0, half)
    dr_s, dw_s, dk_s, dv_s, dkkn_s, db_s = scan_cts
    token = yield ("seam_2", ds_first)
    prep_grads = _rwkv_prep_bwd(proj, tied(mix, token), prm,
                                (dr_s, dr_p, dw_s, dk_s, dk_p, dv_s, dv_p, dkkn_s, db_s, dg))
    dps, d_mix, d_w0, d_wdu, d_a0, d_wiu, d_wgu, d_kk, d_ka = prep_grads
    dq, dkv, dbias, dsink = _attn_bwd(proj, bias, sinks, dcat)
    d_relb = _bias_table_bwd(dbias.reshape(N_Q_HEADS, N_REL), onehot).T
    dproj = _assemble_dproj(dq, dkv, dps, mix)
    dh1 = _matmul(dproj, win_st, "nt", "d_h1", m=SEQ, n=D_MODEL, k=D_IN, tm=1024, tn=D_MODEL, tk=640,
                  b_spec=_stacked(D_MODEL, 640, lambda i, j, kk: (kk, j, 0)))
    d_win = _matmul(h1, dproj, "tn", "d_win", m=D_MODEL, n=D_IN, k=SEQ, tm=D_MODEL, tn=640, tk=SEQ,
                    out=((N_CHIPS, D_MODEL, 640), _stacked(D_MODEL, 640, lambda i, j, kk: (j, 0, 0))))
    grad_x, d_g1 = _first_bwd(x, dx2, dh1, sm["norm_mix_pre"])

    grads = {
        "norm_mix_pre": d_g1, "norm_mix_post": d_g2, "norm_ffn_pre": d_g3, "norm_ffn_post": d_g4,
        "w_in": d_win, "rel_bias": d_relb, "sinks": dsink[:, 0].reshape(1, N_Q_HEADS),
        "rwkv_shift_mix": d_mix, "w0": d_w0, "w_decay_up": d_wdu[:LORA_DECAY], "a0": d_a0,
        "w_iclr_up": d_wiu[LORA_DECAY:], "w_gate_up": d_wgu, "k_k": d_kk, "k_a": d_ka,
        "r_k": d_rk.reshape(1, N_Q_HEADS, HEAD_DIM), "ln_x_g": d_lng, "ln_x_b": d_lnb,
        "w_out": d_wout, "w_ffn_up": d_wup, "conv_w": d_convw, "conv_b": d_convb, "w_ffn_down": d_wdown,
    }
    return loss, grad_x, grads


def _place():
    x, y, c = lax.axis_index("x"), lax.axis_index("y"), lax.axis_index("c")
    chips = [(1 - x, y), (x, 1 - y), (1 - x, 1 - y)]
    return x, y, c, chips


def _remote(src, dst, sems, idx, to):
    return pltpu.make_async_remote_copy(src_ref=src, dst_ref=dst, send_sem=sems[0].at[idx], recv_sem=sems[1].at[idx],
                                        device_id=to, device_id_type=MESH)


def _half(c, rows):
    return pl.ds(pl.multiple_of(c * (rows // 2), 16), rows // 2)


def _gather_weights(big, small):
    nb, ns = len(big), len(small)

    def body(*refs):
        ins, outs = refs[:nb + ns], refs[nb + ns:2 * (nb + ns)]
        ici, d2d, sml, loc = refs[2 * (nb + ns):2 * (nb + ns) + 2], refs[-5:-3], refs[-3:-1], refs[-1]
        x, y, c, chips = _place()
        me = 2 * x + y
        sib = (x, y, 1 - c)
        local = [pltpu.make_async_copy(ins[a], outs[a].at[me], loc.at[a]) for a in range(nb + ns)]
        for cp in local:
            cp.start()
        sends = []
        for a in range(nb):
            rows = _half(c, big[a].shape[0])
            for kk, chip in enumerate(chips):
                sends.append(_remote(ins[a].at[rows], outs[a].at[me, rows], ici, a * 3 + kk, (*chip, c)))
        for a in range(ns):
            for kk, chip in enumerate(chips):
                sends.append(_remote(ins[nb + a], outs[nb + a].at[me], sml, a * 3 + kk, (*chip, c)))
        for cp in sends:
            cp.start()
        passed = []
        for a in range(nb):
            rows = _half(c, big[a].shape[0])
            for kk, (px, py) in enumerate(chips):
                got = outs[a].at[2 * px + py, rows]
                _remote(got, got, ici, a * 3 + kk, sib).wait_recv()
                fwd = _remote(got, got, d2d, a * 3 + kk, sib)
                fwd.start()
                passed.append(fwd)
        for a in range(nb):
            other = _half(1 - c, big[a].shape[0])
            for kk, (px, py) in enumerate(chips):
                land = outs[a].at[2 * px + py, other]
                _remote(land, land, d2d, a * 3 + kk, sib).wait_recv()
        for a in range(ns):
            for kk, (px, py) in enumerate(chips):
                land = outs[nb + a].at[2 * px + py]
                _remote(land, land, sml, a * 3 + kk, sib).wait_recv()
        for cp in sends + passed:
            cp.wait_send()
        for cp in local:
            cp.wait()

    arrs = list(big) + list(small)
    return pl.pallas_call(
        body, name="gather_weights",
        in_specs=[ANY] * len(arrs), out_specs=[ANY] * len(arrs),
        out_shape=[jax.ShapeDtypeStruct((N_CHIPS,) + t.shape, t.dtype) for t in arrs],
        scratch_shapes=[pltpu.SemaphoreType.DMA((3 * nb,)), pltpu.SemaphoreType.DMA((3 * nb,)),
                        pltpu.SemaphoreType.DMA((3 * nb,)), pltpu.SemaphoreType.DMA((3 * nb,)),
                        pltpu.SemaphoreType.DMA((3 * ns,)), pltpu.SemaphoreType.DMA((3 * ns,)),
                        pltpu.SemaphoreType.DMA((nb + ns,))],
        compiler_params=pltpu.CompilerParams(has_side_effects=True),
    )(*arrs)


def _allreduce_small(g):
    rows = g.shape[0]

    def body(g_ref, o_ref, buf, send, recv):
        x, y, c, _ = _place()
        me = 4 * x + 2 * y + c
        buf[me] = g_ref[...]
        sends = []
        for rel in range(1, N_DEV):
            px, py, pc = x ^ (rel >> 2), y ^ ((rel >> 1) & 1), c ^ (rel & 1)
            cp = _remote(g_ref, buf.at[me], (send, recv), rel - 1, (px, py, pc))
            cp.start()
            sends.append(cp)
        for rel in range(1, N_DEV):
            px, py, pc = x ^ (rel >> 2), y ^ ((rel >> 1) & 1), c ^ (rel & 1)
            land = buf.at[4 * px + 2 * py + pc]
            _remote(land, land, (send, recv), rel - 1, (px, py, pc)).wait_recv()
        acc = buf[0]
        for d in range(1, N_DEV):
            acc = acc + buf[d]
        o_ref[...] = acc
        for cp in sends:
            cp.wait_send()

    vm = pl.BlockSpec(memory_space=pltpu.VMEM)
    return pl.pallas_call(
        body, name="allreduce_small", in_specs=[vm], out_specs=vm,
        out_shape=jax.ShapeDtypeStruct((rows, LANES), F32),
        scratch_shapes=[pltpu.VMEM((N_DEV, rows, LANES), F32), pltpu.SemaphoreType.DMA((N_DEV - 1,)),
                        pltpu.SemaphoreType.DMA((N_DEV - 1,))],
        compiler_params=_cp(),
    )(g)


def _pair_exchange(gs):
    n = len(gs)

    def body(*refs):
        ins, got, mine, send, recv, loc = refs[:n], refs[n:2 * n], refs[2 * n:3 * n], refs[-3], refs[-2], refs[-1]
        x, y, c, _ = _place()
        sib = (x, y, 1 - c)
        cps, local = [], []
        for a in range(n):
            rows = gs[a].shape[1]
            cp = _remote(ins[a].at[:, _half(1 - c, rows)], got[a], (send, recv), a, sib)
            cp.start()
            cps.append(cp)
            lc = pltpu.make_async_copy(ins[a].at[:, _half(c, rows)], mine[a], loc.at[a])
            lc.start()
            local.append(lc)
        for a in range(n):
            cps[a].wait_recv()
        for a in range(n):
            cps[a].wait_send()
            local[a].wait()

    halves = [jax.ShapeDtypeStruct((N_CHIPS, t.shape[1] // 2, t.shape[2]), F32) for t in gs]
    outs = pl.pallas_call(
        body, name="grad_pair_exchange", in_specs=[ANY] * n, out_specs=[ANY] * (2 * n), out_shape=halves + halves,
        scratch_shapes=[pltpu.SemaphoreType.DMA((n,)), pltpu.SemaphoreType.DMA((n,)), pltpu.SemaphoreType.DMA((n,))],
        compiler_params=pltpu.CompilerParams(has_side_effects=True),
    )(*gs)
    return outs[:n], outs[n:]


def _chip_exchange(ps):
    n = len(ps)

    def body(*refs):
        ins, outs, send, recv, loc = refs[:n], refs[n:2 * n], refs[-3], refs[-2], refs[-1]
        x, y, c, chips = _place()
        me = 2 * x + y
        cps, local = [], []
        for a in range(n):
            lc = pltpu.make_async_copy(ins[a].at[me], outs[a].at[me], loc.at[a])
            lc.start()
            local.append(lc)
            for kk, (px, py) in enumerate(chips):
                cp = _remote(ins[a].at[2 * px + py], outs[a].at[me], (send, recv), a * 3 + kk, (px, py, c))
                cp.start()
                cps.append(cp)
        for a in range(n):
            for kk, (px, py) in enumerate(chips):
                land = outs[a].at[2 * px + py]
                _remote(land, land, (send, recv), a * 3 + kk, (px, py, c)).wait_recv()
        for cp in cps:
            cp.wait_send()
        for lc in local:
            lc.wait()

    return pl.pallas_call(
        body, name="grad_chip_exchange", in_specs=[ANY] * n, out_specs=[ANY] * n,
        out_shape=[jax.ShapeDtypeStruct(t.shape, F32) for t in ps],
        scratch_shapes=[pltpu.SemaphoreType.DMA((3 * n,)), pltpu.SemaphoreType.DMA((3 * n,)),
                        pltpu.SemaphoreType.DMA((n,))],
        compiler_params=pltpu.CompilerParams(has_side_effects=True),
    )(*ps)


def _pair_gather(hs):
    n = len(hs)

    def body(*refs):
        ins, outs, send, recv, loc = refs[:n], refs[n:2 * n], refs[-3], refs[-2], refs[-1]
        x, y, c, _ = _place()
        sib = (x, y, 1 - c)
        cps, local = [], []
        for a in range(n):
            rows = 2 * hs[a].shape[0]
            cp = _remote(ins[a], outs[a].at[_half(c, rows)], (send, recv), a, sib)
            cp.start()
            cps.append(cp)
            lc = pltpu.make_async_copy(ins[a], outs[a].at[_half(c, rows)], loc.at[a])
            lc.start()
            local.append(lc)
        for a in range(n):
            rows = 2 * hs[a].shape[0]
            land = outs[a].at[_half(1 - c, rows)]
            _remote(land, land, (send, recv), a, sib).wait_recv()
        for a in range(n):
            cps[a].wait_send()
            local[a].wait()

    return pl.pallas_call(
        body, name="grad_pair_gather", in_specs=[ANY] * n, out_specs=[ANY] * n,
        out_shape=[jax.ShapeDtypeStruct((2 * t.shape[0], t.shape[1]), F32) for t in hs],
        scratch_shapes=[pltpu.SemaphoreType.DMA((n,)), pltpu.SemaphoreType.DMA((n,)), pltpu.SemaphoreType.DMA((n,))],
        compiler_params=pltpu.CompilerParams(has_side_effects=True),
    )(*hs)


def _add2(a, b, name):
    r, cdim = a.shape
    tr = 256

    def body(a_ref, b_ref, o_ref):
        o_ref[...] = a_ref[...] + b_ref[...]

    return pl.pallas_call(
        body, name=name, grid=(r // tr,), in_specs=[_rows(tr, cdim)] * 2, out_specs=_rows(tr, cdim),
        out_shape=jax.ShapeDtypeStruct((r, cdim), F32), compiler_params=_cp(("parallel",)),
    )(a, b)


def _sum4(t, name):
    _, r, cdim = t.shape
    tr = 128

    def body(t_ref, o_ref):
        o_ref[...] = ((t_ref[0] + t_ref[1]) + t_ref[2]) + t_ref[3]

    return pl.pallas_call(
        body, name=name, grid=(r // tr,), in_specs=[pl.BlockSpec((N_CHIPS, tr, cdim), lambda i: (0, i, 0))],
        out_specs=_rows(tr, cdim), out_shape=jax.ShapeDtypeStruct((r, cdim), F32),
        compiler_params=_cp(("parallel",)),
    )(t)


def _reduce_big(gs):
    got, mine = _pair_exchange(gs)
    ps = [_add2(m.reshape(-1, m.shape[2]), g.reshape(-1, g.shape[2]), f"grad_pair_add_{i}").reshape(m.shape)
          for i, (m, g) in enumerate(zip(mine, got))]
    xs = _chip_exchange(ps)
    hs = [_sum4(t, f"grad_chip_sum_{i}") for i, t in enumerate(xs)]
    return _pair_gather(hs)


HBM = pl.BlockSpec(memory_space=pltpu.HBM)
SEM = pl.BlockSpec(memory_space=pltpu.SEMAPHORE)
EFFECT = pltpu.SideEffectType.DATAFLOW_SIDE_EFFECTING


def _copies_start(name, bufs, plan, n):
    nb = len(bufs)

    def body(*refs):
        ins, sems, token = refs[:nb], refs[nb:nb + 2 * n], refs[-1]
        for kk, (src, dst, dev) in enumerate(plan(ins)):
            pltpu.make_async_remote_copy(src_ref=src, dst_ref=dst, send_sem=sems[2 * kk], recv_sem=sems[2 * kk + 1],
                                         device_id=dev, device_id_type=MESH).start()
        token[...] = jnp.zeros_like(token)

    outs = pl.pallas_call(
        body, name=name,
        out_shape=tuple([pltpu.SemaphoreType.DMA(())] * (2 * n) + [pltpu.HBM(t.shape, t.dtype) for t in bufs]
                        + [jax.ShapeDtypeStruct((8, LANES), F32)]),
        in_specs=[HBM] * nb,
        out_specs=tuple([SEM] * (2 * n) + [HBM] * nb + [pl.BlockSpec(memory_space=pltpu.VMEM)]),
        input_output_aliases={t: 2 * n + t for t in range(nb)},
        compiler_params=pltpu.CompilerParams(has_side_effects=EFFECT),
    )(*[pltpu.with_memory_space_constraint(t, pltpu.HBM) for t in bufs])
    return outs[:2 * n], outs[2 * n:2 * n + nb], outs[-1]


def _copies_wait(name, sems, bufs, plan, n, after):
    nb = len(bufs)

    def body(*refs):
        ins, sem_refs = refs[:nb], refs[nb:nb + 2 * n]
        for kk, (src, dst, dev) in enumerate(plan(ins)):
            cp = pltpu.make_async_remote_copy(src_ref=src, dst_ref=dst, send_sem=sem_refs[2 * kk],
                                              recv_sem=sem_refs[2 * kk + 1], device_id=dev, device_id_type=MESH)
            cp.wait_send()
            cp.wait_recv()

    return pl.pallas_call(
        body, name=name,
        out_shape=tuple(pltpu.HBM(t.shape, t.dtype) for t in bufs),
        in_specs=[HBM] * nb + [SEM] * (2 * n) + [ANY],
        out_specs=tuple([HBM] * nb),
        input_output_aliases={t: t for t in range(nb)},
        compiler_params=pltpu.CompilerParams(has_side_effects=EFFECT),
    )(*bufs, *sems, after)


def _plan_gather(n_w):
    def plan(refs):
        x, y, c, chips = _place()
        me = 2 * x + y
        return [(refs[a], refs[n_w + a].at[me], (*chip, c)) for a in range(n_w) for chip in chips]
    return plan


def _plan_pair_halves(n_g, rows):
    def plan(refs):
        x, y, c, _ = _place()
        return [(refs[a].at[:, _half(1 - c, rows[a])], refs[n_g + a], (x, y, 1 - c)) for a in range(n_g)]
    return plan


def _plan_chip_parts(n_g):
    def plan(refs):
        x, y, c, chips = _place()
        me = 2 * x + y
        return [(refs[a].at[2 * px + py], refs[n_g + a].at[me], (px, py, c))
                for a in range(n_g) for (px, py) in chips]
    return plan


def _plan_pair_fill(n_g, rows):
    def plan(refs):
        x, y, c, _ = _place()
        return [(refs[a].at[_half(c, rows[a])], refs[a].at[_half(c, rows[a])], (x, y, 1 - c)) for a in range(n_g)]
    return plan


def _pair_add(g, got, name):
    _, rows, cols = g.shape
    hr = rows // 2
    tr = min(hr, 256)
    nb = hr // tr

    def body(g_ref, got_ref, p_ref, own_ref):
        val = g_ref[...] + got_ref[...]
        p_ref[...] = val

        @pl.when(pl.program_id(1) == 2 * lax.axis_index("x") + lax.axis_index("y"))
        def _():
            own_ref[...] = val

    def mine(i, s):
        return (2 * lax.axis_index("x") + lax.axis_index("y"), i, 0)

    return pl.pallas_call(
        body, name=name, grid=(nb, N_CHIPS),
        in_specs=[pl.BlockSpec((None, tr, cols), lambda i, s: (s, lax.axis_index("c") * nb + i, 0)),
                  pl.BlockSpec((None, tr, cols), lambda i, s: (s, i, 0))],
        out_specs=[pl.BlockSpec((None, tr, cols), lambda i, s: (s, i, 0)), pl.BlockSpec((None, tr, cols), mine)],
        out_shape=[jax.ShapeDtypeStruct((N_CHIPS, hr, cols), F32)] * 2,
        compiler_params=_cp(("parallel", "arbitrary")),
    )(g, got)


def _chip_sum(parts, name):
    _, hr, cols = parts.shape
    tr = min(hr, 128)
    nb = hr // tr

    def body(t_ref, o_ref):
        o_ref[...] = ((t_ref[0] + t_ref[1]) + t_ref[2]) + t_ref[3]

    return pl.pallas_call(
        body, name=name, grid=(nb,),
        in_specs=[pl.BlockSpec((N_CHIPS, tr, cols), lambda i: (0, i, 0))],
        out_specs=pl.BlockSpec((tr, cols), lambda i: (lax.axis_index("c") * nb + i, 0)),
        out_shape=jax.ShapeDtypeStruct((2 * hr, cols), F32),
        compiler_params=_cp(("parallel",)),
    )(parts)


def _adamw(w, g, m, v, name, tr):
    r, cdim = w.shape

    def body(w_ref, g_ref, m_ref, v_ref, d_ref, nm_ref, nv_ref):
        g = g_ref[...]
        nm = ADAM_B1 * m_ref[...] + (1.0 - ADAM_B1) * g
        nv = ADAM_B2 * v_ref[...] + (1.0 - ADAM_B2) * (g * g)
        m_hat = nm / (1.0 - ADAM_B1 ** ADAM_STEP)
        v_hat = nv / (1.0 - ADAM_B2 ** ADAM_STEP)
        d_ref[...] = -ADAM_LR * (m_hat / (jnp.sqrt(v_hat) + ADAM_EPS) + ADAM_WD * w_ref[...])
        nm_ref[...] = nm
        nv_ref[...] = nv

    return pl.pallas_call(
        body, name=name, grid=(r // tr,), in_specs=[_rows(tr, cdim)] * 4, out_specs=[_rows(tr, cdim)] * 3,
        out_shape=[jax.ShapeDtypeStruct((r, cdim), F32)] * 3, compiler_params=_cp(("parallel",)),
    )(w, g, m, v)


REPLICATED = (("norm_mix_pre", 1024), ("norm_mix_post", 1024), ("norm_ffn_pre", 1024), ("norm_ffn_post", 1024),
              ("rel_bias", 256), ("sinks", 8), ("rwkv_shift_mix", 1792), ("w0", 512), ("a0", 512), ("k_k", 512),
              ("k_a", 512), ("r_k", 512), ("ln_x_g", 512), ("ln_x_b", 512), ("conv_b", 8192))
SMALL_SHARDED = (("w_decay_up", LORA_DECAY, D_RWKV), ("w_iclr_up", LORA_ICLR, D_RWKV),
                 ("w_gate_up", LORA_GATE, D_RWKV), ("conv_w", 3, 2 * D_FF))
BIG = (("w_in", D_MODEL, 640), ("w_out", 256, D_MODEL), ("w_ffn_up", D_MODEL, 2048), ("w_ffn_down", 1024, D_MODEL))
PACK_ALIGN = 8 * LANES


def _pack(pieces):
    flat = []
    for t in pieces:
        t = t.reshape(-1)
        pad = (-t.shape[0]) % LANES
        flat.append(jnp.pad(t, (0, pad)) if pad else t)
    flat = jnp.concatenate(flat)
    pad = (-flat.shape[0]) % PACK_ALIGN
    return jnp.pad(flat, (0, pad)).reshape(-1, LANES)


def _unpack(buf, sizes):
    flat, out, off = buf.reshape(-1), [], 0
    for n in sizes:
        out.append(flat[off:off + n])
        off += n + ((-n) % LANES)
    return out


def kernel(x, norm_mix_pre, norm_mix_post, norm_ffn_pre, norm_ffn_post, w_in, rel_bias, sinks, rwkv_shift_mix, w0, w_decay_up, a0, w_iclr_up, w_gate_up, k_k, k_a, r_k, ln_x_g, ln_x_b, w_out, w_ffn_up, conv_w, conv_b, w_ffn_down, loss_target, m_norm_mix_pre, m_norm_mix_post, m_norm_ffn_pre, m_norm_ffn_post, m_w_in, m_rel_bias, m_sinks, m_rwkv_shift_mix, m_w0, m_w_decay_up, m_a0, m_w_iclr_up, m_w_gate_up, m_k_k, m_k_a, m_r_k, m_ln_x_g, m_ln_x_b, m_w_out, m_w_ffn_up, m_conv_w, m_conv_b, m_w_ffn_down, v_norm_mix_pre, v_norm_mix_post, v_norm_ffn_pre, v_norm_ffn_post, v_w_in, v_rel_bias, v_sinks, v_rwkv_shift_mix, v_w0, v_w_decay_up, v_a0, v_w_iclr_up, v_w_gate_up, v_k_k, v_k_a, v_r_k, v_ln_x_g, v_ln_x_b, v_w_out, v_w_ffn_up, v_conv_w, v_conv_b, v_w_ffn_down):
    given = dict(locals())
    names = [n for n, _ in REPLICATED] + [n for n, _, _ in SMALL_SHARDED] + [n for n, _, _ in BIG]
    order = ["norm_mix_pre", "norm_mix_post", "norm_ffn_pre", "norm_ffn_post", "w_in", "rel_bias", "sinks",
             "rwkv_shift_mix", "w0", "w_decay_up", "a0", "w_iclr_up", "w_gate_up", "k_k", "k_a", "r_k", "ln_x_g",
             "ln_x_b", "w_out", "w_ffn_up", "conv_w", "conv_b", "w_ffn_down"]
    assert sorted(names) == sorted(order)
    shard = 2 * lax.axis_index("x") + lax.axis_index("y")

    big_sh = {n: given[n].reshape(a, b).astype(BF16) for n, a, b in BIG}
    small_sh = [given[n].reshape(r, c // N_CHIPS) for n, r, c in SMALL_SHARDED]
    gathered = _gather_weights([big_sh["w_in"]], small_sh)
    win_st = gathered[0]
    sm = {n: given[n] for n, _ in REPLICATED}
    sm["r_k"] = r_k.reshape(N_Q_HEADS, HEAD_DIM)
    for (n, r, c), st in zip(SMALL_SHARDED, gathered[1:]):
        sm[n] = st.transpose(1, 0, 2).reshape(r, c)

    rest = ("w_out", "w_ffn_up", "w_ffn_down")
    lands = [lax.dynamic_update_slice(jnp.zeros((N_CHIPS,) + big_sh[n].shape, BF16), big_sh[n][None], (shard, 0, 0))
             for n in rest]
    plan_w = _plan_gather(len(rest))
    w_sems, w_bufs, token = _copies_start("gather_rest_start", [big_sh[n] for n in rest] + lands, plan_w, 9)
    sm["norm_mix_pre"] = norm_mix_pre + token[0:1, 0:1]

    rows_a = (1024, D_MODEL, 256)
    plan_1, plan_2, plan_3 = _plan_pair_halves(3, rows_a), _plan_chip_parts(3), _plan_pair_fill(3, rows_a)
    state = {}

    def on_rest_weights(after):
        out = _copies_wait("gather_rest_wait", w_sems, w_bufs, plan_w, 9, after)
        wout_st, wup_st, wdown_st = out[3:]
        return wout_st.reshape(D_MODEL, D_MODEL), wup_st, wdown_st.reshape(D_FF, D_MODEL)

    def on_grads_a(gs):
        d_wdown, d_wup, d_wout = gs
        gs = [d_wdown.reshape(N_CHIPS, 1024, D_MODEL), d_wup, d_wout.reshape(N_CHIPS, 256, D_MODEL)]
        gots = [jnp.zeros((N_CHIPS, t.shape[1] // 2, t.shape[2]), F32) for t in gs]
        state["g"] = gs
        state["p1"] = _copies_start("grad_pair_start", gs + gots, plan_1, 3)
        return state["p1"][2]

    def on_seam_1(after):
        sems, bufs, _ = state["p1"]
        out = _copies_wait("grad_pair_wait", sems, bufs, plan_1, 3, after)
        sums = [_pair_add(g, got, f"grad_pair_add_{i}") for i, (g, got) in enumerate(zip(out[:3], out[3:]))]
        state["p2"] = _copies_start("grad_chip_start", [p for p, _ in sums] + [own for _, own in sums], plan_2, 9)
        return state["p2"][2]

    def on_seam_2(after):
        sems, bufs, _ = state["p2"]
        out = _copies_wait("grad_chip_wait", sems, bufs, plan_2, 9, after)
        fulls = [_chip_sum(t, f"grad_chip_sum_{i}") for i, t in enumerate(out[3:])]
        state["p3"] = _copies_start("grad_fill_start", fulls, plan_3, 3)
        return state["p3"][2]

    handlers = {"rest_weights": on_rest_weights, "grads_a": on_grads_a, "seam_1": on_seam_1, "seam_2": on_seam_2}
    steps = _local_step(x[0], loss_target[0], sm, win_st)
    kind, payload = next(steps)
    while True:
        try:
            kind, payload = steps.send(handlers[kind](payload))
        except StopIteration as done:
            loss, grad_x, grads = done.value
            break
    loss = lax.psum(loss[0, 0], ("x", "y", "c"))

    rep_sizes = [s for _, s in REPLICATED] + [r * c for _, r, c in SMALL_SHARDED]
    small_sum = _allreduce_small(_pack([grads[n] for n, _ in REPLICATED] + [grads[n] for n, _, _ in SMALL_SHARDED]))
    small_g = _unpack(small_sum, rep_sizes)
    g_out = {n: t.reshape(given[n].shape) for (n, _), t in zip(REPLICATED, small_g)}
    for (n, r, c), t in zip(SMALL_SHARDED, small_g[len(REPLICATED):]):
        g_out[n] = lax.dynamic_slice_in_dim(t.reshape(r, c), shard * (c // N_CHIPS), c // N_CHIPS, axis=1)
    g_out["w_in"] = _reduce_big([grads["w_in"]])[0]
    sems, bufs, _ = state["p3"]
    filled = _copies_wait("grad_fill_wait", sems, bufs, plan_3, 3, g_out["w_in"])
    g_out["w_ffn_down"], g_out["w_ffn_up"], g_out["w_out"] = filled

    small_names = [n for n, _ in REPLICATED] + [n for n, _, _ in SMALL_SHARDED]
    packs = [_pack([src[n] for n in small_names]) for src in
             ({n: given[n] for n in small_names}, g_out, {n: given["m_" + n] for n in small_names},
              {n: given["v_" + n] for n in small_names})]
    small_sizes = [int(np.prod(given[n].shape)) for n in small_names]
    upd = [_unpack(t, small_sizes) for t in _adamw(*packs, "adamw_small", packs[0].shape[0])]
    delta, new_m, new_v = ({n: t.reshape(given[n].shape) for n, t in zip(small_names, u)} for u in upd)
    for n, a, b in BIG:
        d, nm, nv = _adamw(given[n].reshape(a, b), g_out[n], given["m_" + n].reshape(a, b),
                           given["v_" + n].reshape(a, b), "adamw_" + n, 128)
        delta[n], new_m[n], new_v[n] = d, nm, nv

    def shaped(d):
        return [d[n].reshape(given[n].shape) for n in order]

    return (loss, grad_x.reshape(x.shape), *shaped(g_out), *shaped(delta), *shaped(new_m), *shaped(new_v))
```

```python
import functools
import math

import numpy as np
import jax
import jax.numpy as jnp
from jax import lax
from jax.experimental import pallas as pl
from jax.experimental.pallas import tpu as pltpu

F32 = jnp.float32
BF16 = jnp.bfloat16
MESH = pl.DeviceIdType.MESH

SEQ = 2048
D_MODEL = 1024
HEAD_DIM = 64
D_ATTN = 512
D_RWKV = 512
D_KV = 128
N_Q_HEADS = 8
N_KV_HEADS = 2
Q_PER_KV = 4
BLOCK = 128
N_BUCKETS = 32
MAX_DISTANCE = 128
LORA_DECAY = 64
LORA_ICLR = 64
LORA_GATE = 128
RWKV_COLS = 3 * D_RWKV + LORA_DECAY + LORA_ICLR + LORA_GATE
P_OFF = D_ATTN + 2 * D_KV
D_IN = P_OFF + RWKV_COLS
D_FF = 4096
NORM_EPS = 1e-6
GN_EPS = 64e-5
NEG_INF = -1e30
N_CHIPS = 4
N_DEV = 8

ADAM_LR = 0.001
ADAM_B1 = 0.9
ADAM_B2 = 0.999
ADAM_EPS = 1e-08
ADAM_WD = 0.01
ADAM_STEP = 10

VMEM_LIMIT = 52 * 1024 * 1024
LANES = 128
SCAN_T = 64


def _cp(sem=None, vmem=VMEM_LIMIT):
    kw = dict(vmem_limit_bytes=vmem)
    if sem is not None:
        kw["dimension_semantics"] = sem
    return pltpu.CompilerParams(**kw)


def _rows(tr, nc):
    return pl.BlockSpec((tr, nc), lambda i: (i, 0))


def _const(shape):
    return pl.BlockSpec(shape, lambda *_: (0,) * len(shape))


ANY = pl.BlockSpec(memory_space=pl.ANY)


def _split(x, n):
    parts = []
    for _ in range(n - 1):
        h = x.astype(BF16)
        parts.append(h)
        x = x - h.astype(F32)
    parts.append(x.astype(BF16))
    return parts


def _dot(a, b, dn=(((1,), (0,)), ((), ()))):
    return lax.dot_general(a, b, dn, preferred_element_type=F32)


NN = (((1,), (0,)), ((), ()))
NT = (((1,), (1,)), ((), ()))
TN = (((0,), (0,)), ((), ()))


def _dot_ind(x, ind_bf16, n=3):
    acc = None
    for part in _split(x, n):
        t = _dot(part, ind_bf16)
        acc = t if acc is None else acc + t
    return acc


def _head_ones(n, scale=1.0):
    r = lax.broadcasted_iota(jnp.int32, (n, n), 0) >> 6
    c = lax.broadcasted_iota(jnp.int32, (n, n), 1) >> 6
    return jnp.where(r == c, 1.0, 0.0).astype(BF16)


def _matmul(a, b, mode, name, *, m, n, k, tm, tn, tk, a_spec=None, b_spec=None, out=None, out_dtype=F32):
    nk = k // tk
    dn = {"nn": NN, "nt": NT, "tn": TN}[mode]

    def body(a_ref, b_ref, o_ref, *scratch):
        part = _dot(a_ref[...], b_ref[...], dn)
        if nk == 1:
            o_ref[...] = part.astype(out_dtype)
        else:
            acc_ref, = scratch
            kk = pl.program_id(2)

            @pl.when(kk == 0)
            def _():
                acc_ref[...] = part

            @pl.when(kk > 0)
            def _():
                acc_ref[...] += part

            @pl.when(kk == nk - 1)
            def _():
                o_ref[...] = acc_ref[...].astype(out_dtype)

    if a_spec is None:
        a_spec = (pl.BlockSpec((tk, tm), lambda i, j, kk: (kk, i)) if mode == "tn"
                  else pl.BlockSpec((tm, tk), lambda i, j, kk: (i, kk)))
    if b_spec is None:
        b_spec = (pl.BlockSpec((tn, tk), lambda i, j, kk: (j, kk)) if mode == "nt"
                  else pl.BlockSpec((tk, tn), lambda i, j, kk: (kk, j)))
    return pl.pallas_call(
        body, name=name, grid=(m // tm, n // tn, nk),
        in_specs=[a_spec, b_spec],
        out_specs=pl.BlockSpec((tm, tn), lambda i, j, kk: (i, j)) if out is None else out[1],
        out_shape=jax.ShapeDtypeStruct((m, n) if out is None else out[0], out_dtype),
        scratch_shapes=[] if nk == 1 else [pltpu.VMEM((tm, tn), F32)],
        compiler_params=_cp(("parallel", "parallel", "arbitrary")),
    )(a, b)


def _rstd(x):
    return lax.rsqrt(jnp.mean(x * x, axis=-1, keepdims=True) + NORM_EPS)


def _rms_bwd(x, r, g, dy):
    gy = dy * g
    return r * gy - x * ((r * r * r) * (jnp.sum(x * gy, axis=-1, keepdims=True) / x.shape[-1]))


TR = 256


def _norm_cast(x, g, name):
    def body(x_ref, g_ref, h_ref):
        x = x_ref[...]
        h_ref[...] = (x * _rstd(x) * g_ref[...]).astype(BF16)

    return pl.pallas_call(
        body, name=name, grid=(SEQ // TR,),
        in_specs=[_rows(TR, D_MODEL), _const((1, D_MODEL))],
        out_specs=_rows(TR, D_MODEL),
        out_shape=jax.ShapeDtypeStruct((SEQ, D_MODEL), BF16),
        compiler_params=_cp(("parallel",)),
    )(x, g)


def _mix_norm(x, mix, g2, g3):
    def body(x_ref, mix_ref, g2_ref, g3_ref, x2_ref, h3_ref):
        mixv = mix_ref[...]
        x2 = x_ref[...] + mixv * _rstd(mixv) * g2_ref[...]
        x2_ref[...] = x2
        h3_ref[...] = (x2 * _rstd(x2) * g3_ref[...]).astype(BF16)

    return pl.pallas_call(
        body, name="mix_norm", grid=(SEQ // TR,),
        in_specs=[_rows(TR, D_MODEL), _rows(TR, D_MODEL), _const((1, D_MODEL)), _const((1, D_MODEL))],
        out_specs=[_rows(TR, D_MODEL), _rows(TR, D_MODEL)],
        out_shape=[jax.ShapeDtypeStruct((SEQ, D_MODEL), F32), jax.ShapeDtypeStruct((SEQ, D_MODEL), BF16)],
        compiler_params=_cp(("parallel",)),
    )(x, mix, g2, g3)


def _loss_head(x2, f, g4, target):
    def body(x2_ref, f_ref, g4_ref, t_ref, loss_ref, dy_ref, df_ref, dg_ref):
        i = pl.program_id(0)
        f = f_ref[...]
        g4 = g4_ref[...]
        r = _rstd(f)
        e = x2_ref[...] + f * r * g4 - t_ref[...]
        dy = e * (1.0 / D_MODEL)
        dy_ref[...] = dy
        df_ref[...] = _rms_bwd(f, r, g4, dy).astype(BF16)
        part = 0.5 * jnp.sum(jnp.sum(e * e, axis=-1, keepdims=True), axis=0, keepdims=True) * (1.0 / D_MODEL)
        dg = jnp.sum(dy * f * r, axis=0, keepdims=True)

        @pl.when(i == 0)
        def _():
            loss_ref[...] = jnp.zeros_like(loss_ref)
            dg_ref[...] = jnp.zeros_like(dg_ref)

        loss_ref[...] += jnp.broadcast_to(part, loss_ref.shape)
        dg_ref[...] += dg

    return pl.pallas_call(
        body, name="loss_head", grid=(SEQ // TR,),
        in_specs=[_rows(TR, D_MODEL), _rows(TR, D_MODEL), _const((1, D_MODEL)), _rows(TR, D_MODEL)],
        out_specs=[_const((8, LANES)), _rows(TR, D_MODEL), _rows(TR, D_MODEL), _const((1, D_MODEL))],
        out_shape=[jax.ShapeDtypeStruct((8, LANES), F32), jax.ShapeDtypeStruct((SEQ, D_MODEL), F32),
                   jax.ShapeDtypeStruct((SEQ, D_MODEL), BF16), jax.ShapeDtypeStruct((1, D_MODEL), F32)],
        compiler_params=_cp(("arbitrary",)),
    )(x2, f, g4, target)


def _mid_bwd(x2, mix, dy, dh3, g2, g3):
    def body(x2_ref, mix_ref, dy_ref, dh3_ref, g2_ref, g3_ref, dx2_ref, dmix_ref, dg2_ref, dg3_ref):
        i = pl.program_id(0)
        x2 = x2_ref[...]
        mixv = mix_ref[...]
        dh3 = dh3_ref[...]
        r3 = _rstd(x2)
        dx2 = dy_ref[...] + _rms_bwd(x2, r3, g3_ref[...], dh3)
        dx2_ref[...] = dx2
        r2 = _rstd(mixv)
        dmix_ref[...] = _rms_bwd(mixv, r2, g2_ref[...], dx2).astype(BF16)

        @pl.when(i == 0)
        def _():
            dg2_ref[...] = jnp.zeros_like(dg2_ref)
            dg3_ref[...] = jnp.zeros_like(dg3_ref)

        dg3_ref[...] += jnp.sum(dh3 * x2 * r3, axis=0, keepdims=True)
        dg2_ref[...] += jnp.sum(dx2 * mixv * r2, axis=0, keepdims=True)

    return pl.pallas_call(
        body, name="mid_bwd", grid=(SEQ // TR,),
        in_specs=[_rows(TR, D_MODEL)] * 4 + [_const((1, D_MODEL))] * 2,
        out_specs=[_rows(TR, D_MODEL), _rows(TR, D_MODEL), _const((1, D_MODEL)), _const((1, D_MODEL))],
        out_shape=[jax.ShapeDtypeStruct((SEQ, D_MODEL), F32), jax.ShapeDtypeStruct((SEQ, D_MODEL), BF16),
                   jax.ShapeDtypeStruct((1, D_MODEL), F32), jax.ShapeDtypeStruct((1, D_MODEL), F32)],
        compiler_params=_cp(("arbitrary",)),
    )(x2, mix, dy, dh3, g2, g3)


def _first_bwd(x, dx2, dh1, g1):
    def body(x_ref, dx2_ref, dh1_ref, g1_ref, dx_ref, dg1_ref):
        i = pl.program_id(0)
        x = x_ref[...]
        dh1 = dh1_ref[...]
        r = _rstd(x)
        dx_ref[...] = dx2_ref[...] + _rms_bwd(x, r, g1_ref[...], dh1)

        @pl.when(i == 0)
        def _():
            dg1_ref[...] = jnp.zeros_like(dg1_ref)

        dg1_ref[...] += jnp.sum(dh1 * x * r, axis=0, keepdims=True)

    return pl.pallas_call(
        body, name="first_bwd", grid=(SEQ // TR,),
        in_specs=[_rows(TR, D_MODEL)] * 3 + [_const((1, D_MODEL))],
        out_specs=[_rows(TR, D_MODEL), _const((1, D_MODEL))],
        out_shape=[jax.ShapeDtypeStruct((SEQ, D_MODEL), F32), jax.ShapeDtypeStruct((1, D_MODEL), F32)],
        compiler_params=_cp(("arbitrary",)),
    )(x, dx2, dh1, g1)


TC = 256
N_CB = D_FF // TC
GELU_C = math.sqrt(2.0 / math.pi)


def _shift_down(u, s):
    rolled = pltpu.roll(u, s, 0)
    row = lax.broadcasted_iota(jnp.int32, u.shape, 0)
    return jnp.where(row >= s, rolled, 0.0)


def _shift_up(u, s):
    n = u.shape[0]
    rolled = pltpu.roll(u, n - s, 0)
    row = lax.broadcasted_iota(jnp.int32, u.shape, 0)
    return jnp.where(row < n - s, rolled, 0.0)


def _conv3(u, w, b):
    return b + w[0:1] * _shift_down(u, 2) + w[1:2] * _shift_down(u, 1) + w[2:3] * u


def _gelu_and_grad(x):
    inner = GELU_C * (x + 0.044715 * (x * x * x))
    t = jnp.tanh(inner)
    gelu = 0.5 * x * (1.0 + t)
    dgelu = 0.5 * (1.0 + t) + 0.5 * x * (1.0 - t * t) * (GELU_C * (1.0 + 3 * 0.044715 * (x * x)))
    return gelu, dgelu


def _ffn_specs():
    col = lambda off: pl.BlockSpec((SEQ, TC), lambda *g: (0, g[-1] + off))
    w = lambda off: pl.BlockSpec((3, TC), lambda *g: (0, g[-1] + off))
    b = lambda off: pl.BlockSpec((1, TC), lambda *g: (0, g[-1] + off))
    return col, w, b


def _ffn_act(u, conv_w, conv_b):
    col, w, b = _ffn_specs()

    def body(ug_ref, uv_ref, wg_ref, wv_ref, bg_ref, bv_ref, act_ref):
        gate = _conv3(ug_ref[...], wg_ref[...], bg_ref[...])
        val = _conv3(uv_ref[...], wv_ref[...], bv_ref[...])
        act_ref[...] = (_gelu_and_grad(gate)[0] * val).astype(BF16)

    return pl.pallas_call(
        body, name="ffn_act", grid=(N_CB,),
        in_specs=[col(0), col(N_CB), w(0), w(N_CB), b(0), b(N_CB)],
        out_specs=col(0),
        out_shape=jax.ShapeDtypeStruct((SEQ, D_FF), BF16),
        compiler_params=_cp(("parallel",)),
    )(u, u, conv_w, conv_w, conv_b, conv_b)


def _ffn_act_bwd(u, dact, conv_w, conv_b):
    col, w, b = _ffn_specs()
    both = lambda rows: pl.BlockSpec((2, rows, TC), lambda j: (0, 0, j))

    def body(ug_ref, uv_ref, da_ref, wg_ref, wv_ref, bg_ref, bv_ref, du_ref, dw_ref, db_ref):
        ug, uv = ug_ref[...], uv_ref[...]
        wg, wv = wg_ref[...], wv_ref[...]
        gate = _conv3(ug, wg, bg_ref[...])
        val = _conv3(uv, wv, bv_ref[...])
        gelu, dgelu = _gelu_and_grad(gate)
        da = da_ref[...]
        for h, (duc, uh, wh) in enumerate(((da * val * dgelu, ug, wg), (da * gelu, uv, wv))):
            du = wh[2:3] * duc + wh[1:2] * _shift_up(duc, 1) + wh[0:1] * _shift_up(duc, 2)
            du_ref[h] = du.astype(BF16)
            db_ref[h] = jnp.sum(duc, axis=0, keepdims=True)
            dw_ref[h] = jnp.concatenate(
                [jnp.sum(duc * _shift_down(uh, 2), axis=0, keepdims=True),
                 jnp.sum(duc * _shift_down(uh, 1), axis=0, keepdims=True),
                 jnp.sum(duc * uh, axis=0, keepdims=True)], axis=0)

    return pl.pallas_call(
        body, name="ffn_act_bwd", grid=(N_CB,),
        in_specs=[col(0), col(N_CB), col(0), w(0), w(N_CB), b(0), b(N_CB)],
        out_specs=[both(SEQ), both(3), both(1)],
        out_shape=[jax.ShapeDtypeStruct((2, SEQ, D_FF), BF16), jax.ShapeDtypeStruct((2, 3, D_FF), F32),
                   jax.ShapeDtypeStruct((2, 1, D_FF), F32)],
        compiler_params=_cp(("parallel",)),
    )(u, u, dact, conv_w, conv_w, conv_b, conv_b)


def _t5_onehot():
    rel = (np.arange(BLOCK)[:, None] + BLOCK) - np.arange(2 * BLOCK)[None, :]
    n = np.maximum(rel, 0)
    max_exact = N_BUCKETS // 2
    large = max_exact + (np.log(np.maximum(n, 1).astype(np.float32) / np.float32(max_exact))
                         / np.float32(math.log(MAX_DISTANCE / max_exact))
                         * np.float32(N_BUCKETS - max_exact)).astype(np.int32)
    large = np.minimum(large, N_BUCKETS - 1)
    bucket = np.where(n < max_exact, n, large).reshape(-1)
    return (bucket[None, :] == np.arange(N_BUCKETS)[:, None]).astype(np.float32)


N_REL = BLOCK * 2 * BLOCK


def _bias_table(rel_bias_t, onehot):
    def body(rb_ref, oh_ref, o_ref):
        o_ref[...] = _dot_ind(rb_ref[...], oh_ref[...])

    return pl.pallas_call(
        body, name="bias_table", grid=(1,),
        in_specs=[_const((N_Q_HEADS, N_BUCKETS)), _const((N_BUCKETS, N_REL))],
        out_specs=_const((N_Q_HEADS, N_REL)),
        out_shape=jax.ShapeDtypeStruct((N_Q_HEADS, N_REL), F32),
        compiler_params=_cp(("arbitrary",)),
    )(rel_bias_t, onehot)


def _bias_table_bwd(dbias, onehot):
    def body(db_ref, oh_ref, o_ref):
        acc = None
        for part in _split(db_ref[...], 3):
            t = _dot(part, oh_ref[...], NT)
            acc = t if acc is None else acc + t
        o_ref[...] = acc

    return pl.pallas_call(
        body, name="bias_table_bwd", grid=(1,),
        in_specs=[_const((N_Q_HEADS, N_REL)), _const((N_BUCKETS, N_REL))],
        out_specs=_const((N_Q_HEADS, N_BUCKETS)),
        out_shape=jax.ShapeDtypeStruct((N_Q_HEADS, N_BUCKETS), F32),
        compiler_params=_cp(("arbitrary",)),
    )(dbias, onehot)


def _attn_pieces(n, q, kvp, kvc, bias_ref, sinks_ref, hk):
    qi = lax.broadcasted_iota(jnp.int32, (BLOCK, 2 * BLOCK), 0)
    kj = lax.broadcasted_iota(jnp.int32, (BLOCK, 2 * BLOCK), 1)
    rel = qi + BLOCK - kj
    first_key = jnp.where(n > 0, 0, BLOCK)
    ok = jnp.where(rel >= 0, jnp.where(rel < BLOCK, jnp.where(kj >= first_key, 1.0, 0.0), 0.0), 0.0)
    ok4 = jnp.concatenate([ok] * Q_PER_KV, axis=0) > 0.5
    c0 = hk * HEAD_DIM
    kcat = jnp.concatenate([kvp[:, c0:c0 + HEAD_DIM], kvc[:, c0:c0 + HEAD_DIM]], axis=0).astype(BF16)
    vcat = jnp.concatenate([kvp[:, D_KV + c0:D_KV + c0 + HEAD_DIM], kvc[:, D_KV + c0:D_KV + c0 + HEAD_DIM]],
                           axis=0).astype(BF16)
    q0 = hk * Q_PER_KV * HEAD_DIM
    qs = jnp.concatenate([q[:, q0 + g * HEAD_DIM:q0 + (g + 1) * HEAD_DIM] for g in range(Q_PER_KV)],
                         axis=0).astype(BF16)
    s = _dot(qs, kcat, NT) * (HEAD_DIM ** -0.5) + bias_ref[hk]
    s = jnp.where(ok4, s, NEG_INF)
    row = lax.broadcasted_iota(jnp.int32, (Q_PER_KV * BLOCK, 1), 0)
    sink = jnp.zeros((Q_PER_KV * BLOCK, 1), F32)
    for g in range(Q_PER_KV):
        sink = jnp.where((row >> 7) == g, sinks_ref[hk * Q_PER_KV + g], sink)
    m = jnp.maximum(jnp.max(s, axis=-1, keepdims=True), sink)
    p = jnp.exp(s - m)
    es = jnp.exp(sink - m)
    inv = 1.0 / (jnp.sum(p, axis=-1, keepdims=True) + es)
    return qs, kcat, vcat, p * inv, es * inv


def _attn_in_specs():
    return [pl.BlockSpec((BLOCK, D_ATTN), lambda n: (n, 0)),
            pl.BlockSpec((BLOCK, 2 * D_KV), lambda n: (jnp.maximum(n - 1, 0), D_ATTN // (2 * D_KV))),
            pl.BlockSpec((BLOCK, 2 * D_KV), lambda n: (n, D_ATTN // (2 * D_KV))),
            _const((N_KV_HEADS, Q_PER_KV * BLOCK, 2 * BLOCK)),
            pl.BlockSpec(memory_space=pltpu.SMEM)]


def _unstack_heads(t):
    return jnp.concatenate([t[g * BLOCK:(g + 1) * BLOCK] for g in range(Q_PER_KV)], axis=1)


def _attn_fwd(proj, bias, sinks):
    def body(q_ref, kvp_ref, kvc_ref, bias_ref, sinks_ref, o_ref):
        n = pl.program_id(0)
        q, kvp, kvc = q_ref[...], kvp_ref[...], kvc_ref[...]
        outs = []
        for hk in range(N_KV_HEADS):
            _, _, vcat, probs, _ = _attn_pieces(n, q, kvp, kvc, bias_ref, sinks_ref, hk)
            outs.append(_unstack_heads(_dot(probs.astype(BF16), vcat)))
        o_ref[...] = jnp.concatenate(outs, axis=1)

    return pl.pallas_call(
        body, name="attn_fwd", grid=(SEQ // BLOCK,),
        in_specs=_attn_in_specs(),
        out_specs=pl.BlockSpec((BLOCK, D_ATTN), lambda n: (n, 0)),
        out_shape=jax.ShapeDtypeStruct((SEQ, D_ATTN), F32),
        compiler_params=_cp(("parallel",)),
    )(proj, proj, proj, bias, sinks)


def _attn_bwd(proj, bias, sinks, dcat):
    nb = SEQ // BLOCK

    def body(q_ref, kvp_ref, kvc_ref, bias_ref, sinks_ref, do_ref, dq_ref, dkv_ref, dbias_ref, dsink_ref, dsacc):
        n = pl.program_id(0)

        @pl.when(n == 0)
        def _():
            dkv_ref[...] = jnp.zeros_like(dkv_ref)
            dbias_ref[...] = jnp.zeros_like(dbias_ref)
            dsacc[...] = jnp.zeros_like(dsacc)

        q, kvp, kvc = q_ref[...], kvp_ref[...], kvc_ref[...]
        do_all = do_ref[...]
        dqs, dks, dvs = [], [], []
        for hk in range(N_KV_HEADS):
            qs, kcat, vcat, probs, psink = _attn_pieces(n, q, kvp, kvc, bias_ref, sinks_ref, hk)
            q0 = hk * Q_PER_KV * HEAD_DIM
            do = jnp.concatenate([do_all[:, q0 + g * HEAD_DIM:q0 + (g + 1) * HEAD_DIM] for g in range(Q_PER_KV)],
                                 axis=0).astype(BF16)
            dprobs = _dot(do, vcat, NT)
            dvs.append(_dot(probs.astype(BF16), do, TN))
            rowdot = jnp.sum(probs * dprobs, axis=-1, keepdims=True)
            ds = probs * (dprobs - rowdot)
            dsacc[hk] += -psink * rowdot
            dbias_ref[hk] += ds
            dsb = (ds * (HEAD_DIM ** -0.5)).astype(BF16)
            dqs.append(_unstack_heads(_dot(dsb, kcat)))
            dks.append(_dot(dsb, qs, TN))
        dq_ref[...] = jnp.concatenate(dqs, axis=1)
        upd = jnp.concatenate(dks + dvs, axis=1)
        cur = pl.multiple_of(n * BLOCK, BLOCK)
        dkv_ref[pl.ds(cur, BLOCK), :] += upd[BLOCK:]

        @pl.when(n > 0)
        def _():
            prev = pl.multiple_of((n - 1) * BLOCK, BLOCK)
            dkv_ref[pl.ds(prev, BLOCK), :] += upd[:BLOCK]

        @pl.when(n == nb - 1)
        def _():
            for hk in range(N_KV_HEADS):
                for g in range(Q_PER_KV):
                    tot = jnp.sum(dsacc[hk, g * BLOCK:(g + 1) * BLOCK, :], axis=0, keepdims=True)
                    h = hk * Q_PER_KV + g
                    dsink_ref[h:h + 1, :] = jnp.broadcast_to(tot, (1, LANES))

    return pl.pallas_call(
        body, name="attn_bwd", grid=(nb,),
        in_specs=_attn_in_specs() + [pl.BlockSpec((BLOCK, D_ATTN), lambda n: (n, 0))],
        out_specs=[pl.BlockSpec((BLOCK, D_ATTN), lambda n: (n, 0)), _const((SEQ, 2 * D_KV)),
                   _const((N_KV_HEADS, Q_PER_KV * BLOCK, 2 * BLOCK)), _const((N_Q_HEADS, LANES))],
        out_shape=[jax.ShapeDtypeStruct((SEQ, D_ATTN), F32), jax.ShapeDtypeStruct((SEQ, 2 * D_KV), F32),
                   jax.ShapeDtypeStruct((N_KV_HEADS, Q_PER_KV * BLOCK, 2 * BLOCK), F32),
                   jax.ShapeDtypeStruct((N_Q_HEADS, LANES), F32)],
        scratch_shapes=[pltpu.VMEM((N_KV_HEADS, Q_PER_KV * BLOCK, 1), F32)],
        compiler_params=_cp(("arbitrary",)),
    )(proj, proj, proj, bias, sinks, dcat)


@jax.custom_vjp
def _head_sum(x):
    return _dot_ind(x, _head_ones(x.shape[-1]))


_head_sum.defvjp(lambda x: (_head_sum(x), None), lambda _, ct: (_head_sum(ct),))


@jax.custom_vjp
def _bdot(a, w):
    return _dot(a.astype(BF16), w.astype(BF16))


def _bdot_bwd(res, ct):
    a, w = res
    ctb = ct.astype(BF16)
    return _dot(ctb, w.astype(BF16), NT), _dot(a.astype(BF16), ctb, TN)


_bdot.defvjp(lambda a, w: (_bdot(a, w), (a, w)), _bdot_bwd)


def _sigmoid(x):
    return 0.5 * (jnp.tanh(0.5 * x) + 1.0)


def _softplus(x):
    return jnp.maximum(x, 0.0) + jnp.log(1.0 + jnp.exp(-jnp.abs(x)))


def _rwkv_core(r, k, v, zwa, zg, w0, wdu, a0, wiu, wgu, k_k, k_a):
    w_log = -_softplus(-(w0 + _bdot(jnp.tanh(zwa), wdu))) - 0.5
    decay = jnp.exp(-jnp.exp(w_log))
    a = _sigmoid(a0 + _bdot(zwa, wiu))
    g = _bdot(_sigmoid(zg), wgu)
    kk = k * k_k
    kk = kk / jnp.maximum(jnp.sqrt(_head_sum(kk * kk)), 1e-12)
    k2 = k * (1.0 + (a - 1.0) * k_a)
    return r, decay, k2, v, -kk, kk * a, g


def _rwkv_out(o, r, k2, v, g, lng, lnb, rk):
    mu = _head_sum(o) * (1.0 / HEAD_DIM)
    d = o - mu
    var = _head_sum(d * d) * (1.0 / HEAD_DIM)
    on = d * lax.rsqrt(var + GN_EPS) * lng + lnb
    bonus = _head_sum(r * k2 * rk) * v
    return (on + bonus) * g


P_SPLITS = (0, 512, 1024, 1536, 1664, 1792)
N_PREP_PARAMS = 7
HALO = 8


def _shifted_pieces(i, p_ref, halo_ref, mix_ref):
    p = p_ref[:, P_OFF:]
    prev_row = halo_ref[HALO - 1:HALO, P_OFF:] * jnp.where(i > 0, 1.0, 0.0)
    row = lax.broadcasted_iota(jnp.int32, p.shape, 0)
    pprev = jnp.where(row == 0, prev_row, pltpu.roll(p, 1, 0))
    delta = pprev - p
    ps = p + delta * mix_ref[...]
    return [ps[:, a:b] for a, b in zip(P_SPLITS[:-1], P_SPLITS[1:])], delta


def _prep_in_specs():
    return [_rows(TR, D_IN),
            pl.BlockSpec((HALO, D_IN), lambda i: (jnp.maximum(i * (TR // HALO) - 1, 0), 0)),
            _const((1, RWKV_COLS)), _const((1, D_RWKV)), _const((LANES, D_RWKV)), _const((1, D_RWKV)),
            _const((LANES, D_RWKV)), _const((LANES, D_RWKV)), _const((1, D_RWKV)), _const((1, D_RWKV))]


def _rwkv_prep(proj, mix, prm):
    def body(p_ref, halo_ref, mix_ref, *refs):
        prm_refs, outs = refs[:N_PREP_PARAMS], refs[N_PREP_PARAMS:]
        pieces, _ = _shifted_pieces(pl.program_id(0), p_ref, halo_ref, mix_ref)
        vals = _rwkv_core(*pieces, *[t[...] for t in prm_refs])
        for ref, val in zip(outs, vals):
            ref[...] = val

    return pl.pallas_call(
        body, name="rwkv_prep", grid=(SEQ // TR,),
        in_specs=_prep_in_specs(),
        out_specs=[_rows(TR, D_RWKV)] * 7,
        out_shape=[jax.ShapeDtypeStruct((SEQ, D_RWKV), F32)] * 7,
        compiler_params=_cp(("parallel",)),
    )(proj, proj, mix, *prm)


def _rwkv_prep_bwd(proj, mix, prm, cts):
    def body(p_ref, halo_ref, mix_ref, *refs):
        i = pl.program_id(0)
        prm_refs = refs[:N_PREP_PARAMS]
        ct_refs = refs[N_PREP_PARAMS:N_PREP_PARAMS + 10]
        dps_ref, dmix_ref = refs[N_PREP_PARAMS + 10:N_PREP_PARAMS + 12]
        dprm_refs = refs[N_PREP_PARAMS + 12:]
        pieces, delta = _shifted_pieces(i, p_ref, halo_ref, mix_ref)
        _, vjp = jax.vjp(_rwkv_core, *pieces, *[t[...] for t in prm_refs])
        dr1, dr2, dw, dk1, dk2, dv1, dv2, dkkn, db, dg = [t[...] for t in ct_refs]
        grads = vjp((dr1 + dr2, dw, dk1 + dk2, dv1 + dv2, dkkn, db, dg))
        dps = jnp.concatenate(grads[:5], axis=1)
        dps_ref[...] = dps

        @pl.when(i == 0)
        def _():
            dmix_ref[...] = jnp.zeros_like(dmix_ref)
            for ref in dprm_refs:
                ref[...] = jnp.zeros_like(ref)

        dmix_ref[...] += jnp.sum(dps * delta, axis=0, keepdims=True)
        for ref, gval in zip(dprm_refs, grads[5:]):
            ref[...] += gval

    prm_shapes = [(1, D_RWKV), (LANES, D_RWKV), (1, D_RWKV), (LANES, D_RWKV), (LANES, D_RWKV), (1, D_RWKV), (1, D_RWKV)]
    return pl.pallas_call(
        body, name="rwkv_prep_bwd", grid=(SEQ // TR,),
        in_specs=_prep_in_specs() + [_rows(TR, D_RWKV)] * 10,
        out_specs=[_rows(TR, RWKV_COLS), _const((1, RWKV_COLS))] + [_const(s) for s in prm_shapes],
        out_shape=[jax.ShapeDtypeStruct((SEQ, RWKV_COLS), F32), jax.ShapeDtypeStruct((1, RWKV_COLS), F32)]
        + [jax.ShapeDtypeStruct(s, F32) for s in prm_shapes],
        compiler_params=_cp(("arbitrary",)),
    )(proj, proj, mix, *prm, *cts)


def _rwkv_post(o, r, k2, v, g, lng, lnb, rk, attn):
    def body(o_ref, r_ref, k_ref, v_ref, g_ref, lng_ref, lnb_ref, rk_ref, attn_ref, cat_ref):
        rw = _rwkv_out(*[t[...] for t in (o_ref, r_ref, k_ref, v_ref, g_ref, lng_ref, lnb_ref, rk_ref)])
        cat_ref[...] = jnp.concatenate([attn_ref[...], rw], axis=1).astype(BF16)

    return pl.pallas_call(
        body, name="rwkv_post", grid=(SEQ // TR,),
        in_specs=[_rows(TR, D_RWKV)] * 5 + [_const((1, D_RWKV))] * 3 + [_rows(TR, D_ATTN)],
        out_specs=_rows(TR, D_MODEL),
        out_shape=jax.ShapeDtypeStruct((SEQ, D_MODEL), BF16),
        compiler_params=_cp(("parallel",)),
    )(o, r, k2, v, g, lng, lnb, rk, attn)


def _rwkv_post_bwd(o, r, k2, v, g, lng, lnb, rk, dcat):
    def body(o_ref, r_ref, k_ref, v_ref, g_ref, lng_ref, lnb_ref, rk_ref, dcat_ref,
             do_ref, dr_ref, dk_ref, dv_ref, dg_ref, dlng_ref, dlnb_ref, drk_ref):
        i = pl.program_id(0)
        args = [t[...] for t in (o_ref, r_ref, k_ref, v_ref, g_ref, lng_ref, lnb_ref, rk_ref)]
        _, vjp = jax.vjp(_rwkv_out, *args)
        grads = vjp(dcat_ref[:, D_ATTN:])
        for ref, gval in zip((do_ref, dr_ref, dk_ref, dv_ref, dg_ref), grads[:5]):
            ref[...] = gval

        @pl.when(i == 0)
        def _():
            for ref in (dlng_ref, dlnb_ref, drk_ref):
                ref[...] = jnp.zeros_like(ref)

        for ref, gval in zip((dlng_ref, dlnb_ref, drk_ref), grads[5:]):
            ref[...] += gval

    return pl.pallas_call(
        body, name="rwkv_post_bwd", grid=(SEQ // TR,),
        in_specs=[_rows(TR, D_RWKV)] * 5 + [_const((1, D_RWKV))] * 3 + [_rows(TR, D_MODEL)],
        out_specs=[_rows(TR, D_RWKV)] * 5 + [_const((1, D_RWKV))] * 3,
        out_shape=[jax.ShapeDtypeStruct((SEQ, D_RWKV), F32)] * 5 + [jax.ShapeDtypeStruct((1, D_RWKV), F32)] * 3,
        compiler_params=_cp(("arbitrary",)),
    )(o, r, k2, v, g, lng, lnb, rk, dcat)


def _assemble_dproj(dq, dkv, dps, mix):
    last = SEQ // HALO - 1

    def body(dq_ref, dkv_ref, dps_ref, nxt_ref, mix_ref, o_ref):
        i = pl.program_id(0)
        dps = dps_ref[...]
        mixv = mix_ref[...]
        nxt_row = nxt_ref[0:1, :] * jnp.where(i < SEQ // TR - 1, 1.0, 0.0)
        row = lax.broadcasted_iota(jnp.int32, dps.shape, 0)
        up = jnp.where(row == TR - 1, nxt_row, pltpu.roll(dps, TR - 1, 0))
        dp = dps * (1.0 - mixv) + up * mixv
        o_ref[...] = jnp.concatenate([dq_ref[...], dkv_ref[...], dp], axis=1).astype(BF16)

    return pl.pallas_call(
        body, name="assemble_dproj", grid=(SEQ // TR,),
        in_specs=[_rows(TR, D_ATTN), _rows(TR, 2 * D_KV), _rows(TR, RWKV_COLS),
                  pl.BlockSpec((HALO, RWKV_COLS), lambda i: (jnp.minimum((i + 1) * (TR // HALO), last), 0)),
                  _const((1, RWKV_COLS))],
        out_specs=_rows(TR, D_IN),
        out_shape=jax.ShapeDtypeStruct((SEQ, D_IN), BF16),
        compiler_params=_cp(("parallel",)),
    )(dq, dkv, dps, dps, mix)


N_PAIR = D_RWKV // LANES
CHUNK = 2 * SCAN_T
N_CHUNK = SEQ // CHUNK
GROUP = 8


def _lane_sums(lhs_tiles, ones2):
    out = _dot(jnp.concatenate(lhs_tiles, axis=0), ones2)
    return [out[i * HEAD_DIM:(i + 1) * HEAD_DIM] for i in range(len(lhs_tiles))]


def _seg_sum(xs, ones2):
    return _lane_sums([jnp.concatenate(_split(x, 2), axis=1) for x in xs], ones2)


def _col_form(rows, diag, ones2):
    zero = jnp.zeros((HEAD_DIM, LANES), BF16)
    tiles = []
    for row in rows:
        hi = row.astype(BF16)
        lo = (row - hi.astype(F32)).astype(BF16)
        tiles.append(jnp.concatenate(
            [jnp.where(diag, jnp.broadcast_to(part, (HEAD_DIM, LANES)), zero) for part in (hi, lo)], axis=1))
    return _lane_sums(tiles, ones2)


def _scan_consts():
    ones2 = jnp.concatenate([_head_ones(LANES)] * 2, axis=0)
    sub = lax.broadcasted_iota(jnp.int32, (HEAD_DIM, LANES), 0)
    lane_in_head = lax.broadcasted_iota(jnp.int32, (HEAD_DIM, LANES), 1) & (HEAD_DIM - 1)
    return ones2, lane_in_head == sub, lane_in_head


def _rows_of_columns(tile):
    t = tile.T
    return jnp.concatenate([t[:HEAD_DIM], t[HEAD_DIM:]], axis=1)


def _pair(j):
    return slice(j * LANES, (j + 1) * LANES)


def _scan_fwd(r, w, k, v, kkn, b):
    def body(r_ref, w_ref, k_ref, v_ref, kkn_ref, b_ref, o_ref, ckpt_ref, s_scr):
        c = pl.program_id(0)
        ones2, diag, lane_in_head = _scan_consts()

        @pl.when(c == 0)
        def _():
            s_scr[...] = jnp.zeros_like(s_scr)

        for sub in range(CHUNK // SCAN_T):
            ckpt_ref[sub] = s_scr[...]

            def group(gi, carry, sub=sub):
                row0 = pl.multiple_of(sub * SCAN_T + gi * GROUP, GROUP)
                states, ocols = list(carry[:N_PAIR]), list(carry[N_PAIR:])
                tiles = [[t[pl.ds(row0, GROUP), _pair(j)] for t in (r_ref, w_ref, k_ref, v_ref, kkn_ref, b_ref)]
                         for j in range(N_PAIR)]
                def row(j, name, u):
                    return tiles[j]["rwkvnb".index(name)][u:u + 1]

                def emit_out(u, after):
                    here = lane_in_head == gi * GROUP + u
                    outs = _seg_sum([after[j] * row(j, "r", u) for j in range(N_PAIR)], ones2)
                    for j in range(N_PAIR):
                        ocols[j] = jnp.where(here, outs[j], ocols[j])

                vcols = _col_form([row(j, "v", 0) for j in range(N_PAIR)], diag, ones2)
                after = None
                for u in range(GROUP):
                    sas = _seg_sum([states[j] * row(j, "n", u) for j in range(N_PAIR)], ones2)
                    if after is not None:
                        emit_out(u - 1, after)
                    nxt = (_col_form([row(j, "v", u + 1) for j in range(N_PAIR)], diag, ones2)
                           if u + 1 < GROUP else None)
                    for j in range(N_PAIR):
                        states[j] = states[j] * row(j, "w", u) + sas[j] * row(j, "b", u) + vcols[j] * row(j, "k", u)
                    after, vcols = list(states), nxt
                emit_out(GROUP - 1, after)
                return tuple(states + ocols)

            zero = jnp.zeros((HEAD_DIM, LANES), F32)
            fin = lax.fori_loop(0, SCAN_T // GROUP, group,
                                tuple(s_scr[j] for j in range(N_PAIR)) + (zero,) * N_PAIR)
            for j in range(N_PAIR):
                s_scr[j] = fin[j]
                o_ref[sub * SCAN_T:(sub + 1) * SCAN_T, _pair(j)] = _rows_of_columns(fin[N_PAIR + j])

    blk = pl.BlockSpec((CHUNK, D_RWKV), lambda c: (c, 0))
    return pl.pallas_call(
        body, name="rwkv_scan_fwd", grid=(N_CHUNK,),
        in_specs=[blk] * 6,
        out_specs=[blk, pl.BlockSpec((CHUNK // SCAN_T, N_PAIR, HEAD_DIM, LANES), lambda c: (c, 0, 0, 0))],
        out_shape=[jax.ShapeDtypeStruct((SEQ, D_RWKV), F32),
                   jax.ShapeDtypeStruct((SEQ // SCAN_T, N_PAIR, HEAD_DIM, LANES), F32)],
        scratch_shapes=[pltpu.VMEM((N_PAIR, HEAD_DIM, LANES), F32)],
        compiler_params=_cp(("arbitrary",)),
    )(r, w, k, v, kkn, b)


def _scan_bwd(r, w, k, v, kkn, b, do, ckpt, ds_in, prev, name, first_chunk, n_chunks):
    top = first_chunk + n_chunks - 1

    def body(r_ref, w_ref, k_ref, v_ref, kkn_ref, b_ref, do_ref, ckpt_ref, ds_in_ref, *rest):
        dr_ref, dw_ref, dk_ref, dv_ref, dkkn_ref, db_ref, ds_out_ref, ds_scr, st_scr, sa_scr = rest[-10:]
        i = pl.program_id(0)
        ones2, diag, lane_in_head = _scan_consts()

        @pl.when(i == 0)
        def _():
            ds_scr[...] = ds_in_ref[...]

        for sub in reversed(range(CHUNK // SCAN_T)):
            def recompute(gi, states, sub=sub):
                row0 = pl.multiple_of(sub * SCAN_T + gi * GROUP, GROUP)
                states = list(states)
                tiles = [[t[pl.ds(row0, GROUP), _pair(j)] for t in (w_ref, k_ref, v_ref, kkn_ref, b_ref)]
                         for j in range(N_PAIR)]
                for u in range(GROUP):
                    rows = [[t[u:u + 1] for t in tiles[j]] for j in range(N_PAIR)]
                    sas = _seg_sum([states[j] * rows[j][3] for j in range(N_PAIR)], ones2)
                    vcols = _col_form([rows[j][2] for j in range(N_PAIR)], diag, ones2)
                    for j in range(N_PAIR):
                        w_t, k_t, _, _, b_t = rows[j]
                        st_scr[gi * GROUP + u, j] = states[j]
                        sa_scr[gi * GROUP + u, j] = sas[j]
                        states[j] = states[j] * w_t + sas[j] * b_t + vcols[j] * k_t
                return tuple(states)

            fin = lax.fori_loop(0, SCAN_T // GROUP, recompute, tuple(ckpt_ref[sub, j] for j in range(N_PAIR)))
            for j in range(N_PAIR):
                st_scr[SCAN_T, j] = fin[j]

            def reverse(gr, carry, sub=sub):
                gi = SCAN_T // GROUP - 1 - gr
                row0 = pl.multiple_of(sub * SCAN_T + gi * GROUP, GROUP)
                dstates, dvcols = list(carry[:N_PAIR]), list(carry[N_PAIR:])
                tiles = [[t[pl.ds(row0, GROUP), _pair(j)]
                          for t in (r_ref, w_ref, k_ref, v_ref, kkn_ref, b_ref, do_ref)] for j in range(N_PAIR)]
                rows = [[[None] * GROUP for _ in range(5)] for _ in range(N_PAIR)]

                def row(j, name, u):
                    return tiles[j]["rwkvnbd".index(name)][u:u + 1]

                def cols_of(u):
                    both = _col_form([row(j, "d", u) for j in range(N_PAIR)] + [row(j, "v", u) for j in range(N_PAIR)],
                                     diag, ones2)
                    return [(both[j], both[N_PAIR + j]) for j in range(N_PAIR)]

                def emit_dv(u, dsp):
                    here = lane_in_head == gi * GROUP + u
                    outs = _seg_sum([dsp[j] * row(j, "k", u) for j in range(N_PAIR)], ones2)
                    for j in range(N_PAIR):
                        dvcols[j] = jnp.where(here, outs[j], dvcols[j])

                cols = cols_of(GROUP - 1)
                before = None
                for u in reversed(range(GROUP)):
                    tl = gi * GROUP + u
                    dsp = [dstates[j] + cols[j][0] * row(j, "r", u) for j in range(N_PAIR)]
                    dsas = _seg_sum([dsp[j] * row(j, "b", u) for j in range(N_PAIR)], ones2)
                    if before is not None:
                        emit_dv(u + 1, before)
                    nxt = cols_of(u - 1) if u > 0 else None
                    for j in range(N_PAIR):
                        s_prev = st_scr[tl, j]
                        docol, vcol = cols[j]
                        rows[j][0][u] = jnp.sum(st_scr[tl + 1, j] * docol, axis=0, keepdims=True)
                        rows[j][1][u] = jnp.sum(dsp[j] * s_prev, axis=0, keepdims=True)
                        rows[j][2][u] = jnp.sum(dsp[j] * vcol, axis=0, keepdims=True)
                        rows[j][3][u] = jnp.sum(s_prev * dsas[j], axis=0, keepdims=True)
                        rows[j][4][u] = jnp.sum(dsp[j] * sa_scr[tl, j], axis=0, keepdims=True)
                        dstates[j] = dsp[j] * row(j, "w", u) + dsas[j] * row(j, "n", u)
                    before, cols = dsp, nxt
                emit_dv(0, before)
                for j in range(N_PAIR):
                    for ref, rr in zip((dr_ref, dw_ref, dk_ref, dkkn_ref, db_ref), rows[j]):
                        ref[pl.ds(row0, GROUP), _pair(j)] = jnp.concatenate(rr, axis=0)
                return tuple(dstates + dvcols)

            zero = jnp.zeros((HEAD_DIM, LANES), F32)
            dfin = lax.fori_loop(0, SCAN_T // GROUP, reverse,
                                 tuple(ds_scr[j] for j in range(N_PAIR)) + (zero,) * N_PAIR)
            for j in range(N_PAIR):
                ds_scr[j] = dfin[j]
                dv_ref[sub * SCAN_T:(sub + 1) * SCAN_T, _pair(j)] = _rows_of_columns(dfin[N_PAIR + j])

        @pl.when(i == n_chunks - 1)
        def _():
            ds_out_ref[...] = ds_scr[...]

    blk = pl.BlockSpec((CHUNK, D_RWKV), lambda i: (top - i, 0))
    state = (N_PAIR, HEAD_DIM, LANES)
    prev = [] if prev is None else list(prev)
    outs = pl.pallas_call(
        body, name=name, grid=(n_chunks,),
        in_specs=[blk] * 7 + [pl.BlockSpec((CHUNK // SCAN_T,) + state, lambda i: (top - i, 0, 0, 0)), _const(state)]
        + [ANY] * len(prev),
        out_specs=[blk] * 6 + [_const(state)],
        out_shape=[jax.ShapeDtypeStruct((SEQ, D_RWKV), F32)] * 6 + [jax.ShapeDtypeStruct(state, F32)],
        scratch_shapes=[pltpu.VMEM(state, F32), pltpu.VMEM((SCAN_T + 1,) + state, F32),
                        pltpu.VMEM((SCAN_T,) + state, F32)],
        input_output_aliases={9 + t: t for t in range(len(prev))},
        compiler_params=_cp(("arbitrary",)),
    )(r, w, k, v, kkn, b, do, ckpt, ds_in, *prev)
    return outs[:6], outs[6]


def _stacked(rows, cols, pick):
    return pl.BlockSpec((None, rows, cols), pick)


def _local_step(x, target, sm, win_st):
    def tied(t, token):
        return t if token is None else t + token[0:1, 0:1].reshape((1,) * t.ndim)

    zpad = jnp.zeros((LORA_DECAY, D_RWKV), F32)
    prm = [sm["w0"], jnp.concatenate([sm["w_decay_up"], zpad], axis=0), sm["a0"],
           jnp.concatenate([zpad, sm["w_iclr_up"]], axis=0), sm["w_gate_up"], sm["k_k"], sm["k_a"]]
    mix = sm["rwkv_shift_mix"]
    onehot = jnp.asarray(_t5_onehot(), BF16)
    sinks = sm["sinks"].reshape(N_Q_HEADS)
    lng, lnb, rk = sm["ln_x_g"], sm["ln_x_b"], sm["r_k"].reshape(1, D_RWKV)

    h1 = _norm_cast(x, sm["norm_mix_pre"], "norm_in")
    proj = _matmul(h1, win_st, "nn", "proj", m=SEQ, n=D_IN, k=D_MODEL, tm=SEQ, tn=640, tk=D_MODEL,
                   b_spec=_stacked(D_MODEL, 640, lambda i, j, kk: (j, 0, 0)))
    bias = _bias_table(sm["rel_bias"].T, onehot).reshape(N_KV_HEADS, Q_PER_KV * BLOCK, 2 * BLOCK)
    attn = _attn_fwd(proj, bias, sinks)
    r, w, k2, v, kkn, b, g = _rwkv_prep(proj, mix, prm)
    o, ckpt = _scan_fwd(r, w, k2, v, kkn, b)
    wout, wup_st, wdown = yield ("rest_weights", o)
    cat = _rwkv_post(o, r, k2, v, g, lng, lnb, rk, attn)
    mixo = _matmul(cat, wout, "nn", "out_proj", m=SEQ, n=D_MODEL, k=D_MODEL, tm=SEQ, tn=512, tk=D_MODEL)
    x2, h3 = _mix_norm(x, mixo, sm["norm_mix_post"], sm["norm_ffn_pre"])
    u = _matmul(h3, wup_st, "nn", "ffn_up", m=SEQ, n=2 * D_FF, k=D_MODEL, tm=SEQ, tn=512, tk=D_MODEL,
                b_spec=_stacked(D_MODEL, 512, lambda i, j, kk: (j // 4, 0, j % 4)))
    act = _ffn_act(u, sm["conv_w"], sm["conv_b"])
    f = _matmul(act, wdown, "nn", "ffn_down", m=SEQ, n=D_MODEL, k=D_FF, tm=1024, tn=512, tk=2048)
    loss, dy, df, d_g4 = _loss_head(x2, f, sm["norm_ffn_post"], target)

    dact = _matmul(df, wdown, "nt", "d_act", m=SEQ, n=D_FF, k=D_MODEL, tm=SEQ, tn=512, tk=D_MODEL)
    d_wdown = _matmul(act, df, "tn", "d_wdown", m=D_FF, n=D_MODEL, k=SEQ, tm=512, tn=D_MODEL, tk=SEQ)
    du, d_convw, d_convb = _ffn_act_bwd(u, dact, sm["conv_w"], sm["conv_b"])
    d_convw = d_convw.transpose(1, 0, 2).reshape(3, 2 * D_FF)
    d_convb = d_convb.reshape(1, 2 * D_FF)
    dh3 = _matmul(du, wup_st, "nt", "d_h3", m=SEQ, n=D_MODEL, k=2 * D_FF, tm=1024, tn=D_MODEL, tk=2048,
                  a_spec=pl.BlockSpec((None, 1024, 2048), lambda i, j, kk: (kk // 2, i, kk % 2)),
                  b_spec=_stacked(D_MODEL, 2048, lambda i, j, kk: (kk, j, 0)))
    d_wup = _matmul(h3, du, "tn", "d_wup", m=D_MODEL, n=2 * D_FF, k=SEQ, tm=D_MODEL, tn=512, tk=SEQ,
                    b_spec=pl.BlockSpec((None, SEQ, 512), lambda i, j, kk: (j // 8, 0, j % 8)),
                    out=((N_CHIPS, D_MODEL, 2048), _stacked(D_MODEL, 512, lambda i, j, kk: (j // 4, 0, j % 4))))
    dx2, dmix, d_g2, d_g3 = _mid_bwd(x2, mixo, dy, dh3, sm["norm_mix_post"], sm["norm_ffn_pre"])
    dcat = _matmul(dmix, wout, "nt", "d_cat", m=SEQ, n=D_MODEL, k=D_MODEL, tm=SEQ, tn=512, tk=D_MODEL)
    d_wout = _matmul(cat, dmix, "tn", "d_wout", m=D_MODEL, n=D_MODEL, k=SEQ, tm=512, tn=D_MODEL, tk=SEQ)
    token = yield ("grads_a", (d_wdown, d_wup, d_wout))
    do, dr_p, dk_p, dv_p, dg, d_lng, d_lnb, d_rk = _rwkv_post_bwd(o, r, k2, v, g, lng, tied(lnb, token), rk, dcat)
    half = N_CHUNK // 2
    ds_end = jnp.zeros((N_PAIR, HEAD_DIM, LANES), F32)
    late, ds_mid = _scan_bwd(r, w, k2, v, kkn, b, do, ckpt, ds_end, None, "rwkv_scan_bwd_late", half, half)
    token = yield ("seam_1", ds_mid)
    scan_cts, ds_first = _scan_bwd(r, w, k2, v, kkn, b, do, ckpt, tied(ds_mid, token), late,
                                   "rwkv_scan_bwd_early", 0, half)
    dr_s, dw_s, dk_s, dv_s, dkkn_s, db_s = scan_cts
    token = yield ("seam_2", ds_first)
    prep_grads = _rwkv_prep_bwd(proj, tied(mix, token), prm,
                                (dr_s, dr_p, dw_s, dk_s, dk_p, dv_s, dv_p, dkkn_s, db_s, dg))
    dps, d_mix, d_w0, d_wdu, d_a0, d_wiu, d_wgu, d_kk, d_ka = prep_grads
    dq, dkv, dbias, dsink = _attn_bwd(proj, bias, sinks, dcat)
    d_relb = _bias_table_bwd(dbias.reshape(N_Q_HEADS, N_REL), onehot).T
    dproj = _assemble_dproj(dq, dkv, dps, mix)
    d_win = _matmul(h1, dproj, "tn", "d_win", m=D_MODEL, n=D_IN, k=SEQ, tm=D_MODEL, tn=640, tk=SEQ,
                    out=((N_CHIPS, D_MODEL, 640), _stacked(D_MODEL, 640, lambda i, j, kk: (j, 0, 0))))
    token = yield ("grads_b", d_win)
    dh1 = _matmul(dproj, win_st, "nt", "d_h1", m=SEQ, n=D_MODEL, k=D_IN, tm=1024, tn=D_MODEL, tk=640,
                  b_spec=_stacked(D_MODEL, 640, lambda i, j, kk: (kk, j, 0)))
    grad_x, d_g1 = _first_bwd(x, dx2, dh1, tied(sm["norm_mix_pre"], token))

    grads = {
        "norm_mix_pre": d_g1, "norm_mix_post": d_g2, "norm_ffn_pre": d_g3, "norm_ffn_post": d_g4,
        "w_in": d_win, "rel_bias": d_relb, "sinks": dsink[:, 0].reshape(1, N_Q_HEADS),
        "rwkv_shift_mix": d_mix, "w0": d_w0, "w_decay_up": d_wdu[:LORA_DECAY], "a0": d_a0,
        "w_iclr_up": d_wiu[LORA_DECAY:], "w_gate_up": d_wgu, "k_k": d_kk, "k_a": d_ka,
        "r_k": d_rk.reshape(1, N_Q_HEADS, HEAD_DIM), "ln_x_g": d_lng, "ln_x_b": d_lnb,
        "w_out": d_wout, "w_ffn_up": d_wup, "conv_w": d_convw, "conv_b": d_convb, "w_ffn_down": d_wdown,
    }
    return loss, grad_x, grads


def _place():
    x, y, c = lax.axis_index("x"), lax.axis_index("y"), lax.axis_index("c")
    chips = [(1 - x, y), (x, 1 - y), (1 - x, 1 - y)]
    return x, y, c, chips


def _remote(src, dst, sems, idx, to):
    return pltpu.make_async_remote_copy(src_ref=src, dst_ref=dst, send_sem=sems[0].at[idx], recv_sem=sems[1].at[idx],
                                        device_id=to, device_id_type=MESH)


def _half(c, rows):
    return pl.ds(pl.multiple_of(c * (rows // 2), 16), rows // 2)


def _gather_weights(big, small):
    nb, ns = len(big), len(small)

    def body(*refs):
        ins, outs = refs[:nb + ns], refs[nb + ns:2 * (nb + ns)]
        ici, d2d, sml, loc = refs[2 * (nb + ns):2 * (nb + ns) + 2], refs[-5:-3], refs[-3:-1], refs[-1]
        x, y, c, chips = _place()
        me = 2 * x + y
        sib = (x, y, 1 - c)
        local = [pltpu.make_async_copy(ins[a], outs[a].at[me], loc.at[a]) for a in range(nb + ns)]
        for cp in local:
            cp.start()
        sends = []
        for a in range(nb):
            rows = _half(c, big[a].shape[0])
            for kk, chip in enumerate(chips):
                sends.append(_remote(ins[a].at[rows], outs[a].at[me, rows], ici, a * 3 + kk, (*chip, c)))
        for a in range(ns):
            for kk, chip in enumerate(chips):
                sends.append(_remote(ins[nb + a], outs[nb + a].at[me], sml, a * 3 + kk, (*chip, c)))
        for cp in sends:
            cp.start()
        passed = []
        for a in range(nb):
            rows = _half(c, big[a].shape[0])
            for kk, (px, py) in enumerate(chips):
                got = outs[a].at[2 * px + py, rows]
                _remote(got, got, ici, a * 3 + kk, sib).wait_recv()
                fwd = _remote(got, got, d2d, a * 3 + kk, sib)
                fwd.start()
                passed.append(fwd)
        for a in range(nb):
            other = _half(1 - c, big[a].shape[0])
            for kk, (px, py) in enumerate(chips):
                land = outs[a].at[2 * px + py, other]
                _remote(land, land, d2d, a * 3 + kk, sib).wait_recv()
        for a in range(ns):
            for kk, (px, py) in enumerate(chips):
                land = outs[nb + a].at[2 * px + py]
                _remote(land, land, sml, a * 3 + kk, sib).wait_recv()
        for cp in sends + passed:
            cp.wait_send()
        for cp in local:
            cp.wait()

    arrs = list(big) + list(small)
    return pl.pallas_call(
        body, name="gather_weights",
        in_specs=[ANY] * len(arrs), out_specs=[ANY] * len(arrs),
        out_shape=[jax.ShapeDtypeStruct((N_CHIPS,) + t.shape, t.dtype) for t in arrs],
        scratch_shapes=[pltpu.SemaphoreType.DMA((3 * nb,)), pltpu.SemaphoreType.DMA((3 * nb,)),
                        pltpu.SemaphoreType.DMA((3 * nb,)), pltpu.SemaphoreType.DMA((3 * nb,)),
                        pltpu.SemaphoreType.DMA((3 * ns,)), pltpu.SemaphoreType.DMA((3 * ns,)),
                        pltpu.SemaphoreType.DMA((nb + ns,))],
        compiler_params=pltpu.CompilerParams(has_side_effects=True),
    )(*arrs)


def _allreduce_small(g):
    rows = g.shape[0]

    def body(g_ref, o_ref, buf, send, recv):
        x, y, c, _ = _place()
        me = 4 * x + 2 * y + c
        buf[me] = g_ref[...]
        sends = []
        for rel in range(1, N_DEV):
            px, py, pc = x ^ (rel >> 2), y ^ ((rel >> 1) & 1), c ^ (rel & 1)
            cp = _remote(g_ref, buf.at[me], (send, recv), rel - 1, (px, py, pc))
            cp.start()
            sends.append(cp)
        for rel in range(1, N_DEV):
            px, py, pc = x ^ (rel >> 2), y ^ ((rel >> 1) & 1), c ^ (rel & 1)
            land = buf.at[4 * px + 2 * py + pc]
            _remote(land, land, (send, recv), rel - 1, (px, py, pc)).wait_recv()
        acc = buf[0]
        for d in range(1, N_DEV):
            acc = acc + buf[d]
        o_ref[...] = acc
        for cp in sends:
            cp.wait_send()

    vm = pl.BlockSpec(memory_space=pltpu.VMEM)
    return pl.pallas_call(
        body, name="allreduce_small", in_specs=[vm], out_specs=vm,
        out_shape=jax.ShapeDtypeStruct((rows, LANES), F32),
        scratch_shapes=[pltpu.VMEM((N_DEV, rows, LANES), F32), pltpu.SemaphoreType.DMA((N_DEV - 1,)),
                        pltpu.SemaphoreType.DMA((N_DEV - 1,))],
        compiler_params=_cp(),
    )(g)


def _pair_exchange(gs):
    n = len(gs)

    def body(*refs):
        ins, got, mine, send, recv, loc = refs[:n], refs[n:2 * n], refs[2 * n:3 * n], refs[-3], refs[-2], refs[-1]
        x, y, c, _ = _place()
        sib = (x, y, 1 - c)
        cps, local = [], []
        for a in range(n):
            rows = gs[a].shape[1]
            cp = _remote(ins[a].at[:, _half(1 - c, rows)], got[a], (send, recv), a, sib)
            cp.start()
            cps.append(cp)
            lc = pltpu.make_async_copy(ins[a].at[:, _half(c, rows)], mine[a], loc.at[a])
            lc.start()
            local.append(lc)
        for a in range(n):
            cps[a].wait_recv()
        for a in range(n):
            cps[a].wait_send()
            local[a].wait()

    halves = [jax.ShapeDtypeStruct((N_CHIPS, t.shape[1] // 2, t.shape[2]), F32) for t in gs]
    outs = pl.pallas_call(
        body, name="grad_pair_exchange", in_specs=[ANY] * n, out_specs=[ANY] * (2 * n), out_shape=halves + halves,
        scratch_shapes=[pltpu.SemaphoreType.DMA((n,)), pltpu.SemaphoreType.DMA((n,)), pltpu.SemaphoreType.DMA((n,))],
        compiler_params=pltpu.CompilerParams(has_side_effects=True),
    )(*gs)
    return outs[:n], outs[n:]


def _chip_exchange(ps):
    n = len(ps)

    def body(*refs):
        ins, outs, send, recv, loc = refs[:n], refs[n:2 * n], refs[-3], refs[-2], refs[-1]
        x, y, c, chips = _place()
        me = 2 * x + y
        cps, local = [], []
        for a in range(n):
            lc = pltpu.make_async_copy(ins[a].at[me], outs[a].at[me], loc.at[a])
            lc.start()
            local.append(lc)
            for kk, (px, py) in enumerate(chips):
                cp = _remote(ins[a].at[2 * px + py], outs[a].at[me], (send, recv), a * 3 + kk, (px, py, c))
                cp.start()
                cps.append(cp)
        for a in range(n):
            for kk, (px, py) in enumerate(chips):
                land = outs[a].at[2 * px + py]
                _remote(land, land, (send, recv), a * 3 + kk, (px, py, c)).wait_recv()
        for cp in cps:
            cp.wait_send()
        for lc in local:
            lc.wait()

    return pl.pallas_call(
        body, name="grad_chip_exchange", in_specs=[ANY] * n, out_specs=[ANY] * n,
        out_shape=[jax.ShapeDtypeStruct(t.shape, F32) for t in ps],
        scratch_shapes=[pltpu.SemaphoreType.DMA((3 * n,)), pltpu.SemaphoreType.DMA((3 * n,)),
                        pltpu.SemaphoreType.DMA((n,))],
        compiler_params=pltpu.CompilerParams(has_side_effects=True),
    )(*ps)


def _pair_gather(hs):
    n = len(hs)

    def body(*refs):
        ins, outs, send, recv, loc = refs[:n], refs[n:2 * n], refs[-3], refs[-2], refs[-1]
        x, y, c, _ = _place()
        sib = (x, y, 1 - c)
        cps, local = [], []
        for a in range(n):
            rows = 2 * hs[a].shape[0]
            cp = _remote(ins[a], outs[a].at[_half(c, rows)], (send, recv), a, sib)
            cp.start()
            cps.append(cp)
            lc = pltpu.make_async_copy(ins[a], outs[a].at[_half(c, rows)], loc.at[a])
            lc.start()
            local.append(lc)
        for a in range(n):
            rows = 2 * hs[a].shape[0]
            land = outs[a].at[_half(1 - c, rows)]
            _remote(land, land, (send, recv), a, sib).wait_recv()
        for a in range(n):
            cps[a].wait_send()
            local[a].wait()

    return pl.pallas_call(
        body, name="grad_pair_gather", in_specs=[ANY] * n, out_specs=[ANY] * n,
        out_shape=[jax.ShapeDtypeStruct((2 * t.shape[0], t.shape[1]), F32) for t in hs],
        scratch_shapes=[pltpu.SemaphoreType.DMA((n,)), pltpu.SemaphoreType.DMA((n,)), pltpu.SemaphoreType.DMA((n,))],
        compiler_params=pltpu.CompilerParams(has_side_effects=True),
    )(*hs)


def _add2(a, b, name):
    r, cdim = a.shape
    tr = 256

    def body(a_ref, b_ref, o_ref):
        o_ref[...] = a_ref[...] + b_ref[...]

    return pl.pallas_call(
        body, name=name, grid=(r // tr,), in_specs=[_rows(tr, cdim)] * 2, out_specs=_rows(tr, cdim),
        out_shape=jax.ShapeDtypeStruct((r, cdim), F32), compiler_params=_cp(("parallel",)),
    )(a, b)


def _sum4(t, name):
    _, r, cdim = t.shape
    tr = 128

    def body(t_ref, o_ref):
        o_ref[...] = ((t_ref[0] + t_ref[1]) + t_ref[2]) + t_ref[3]

    return pl.pallas_call(
        body, name=name, grid=(r // tr,), in_specs=[pl.BlockSpec((N_CHIPS, tr, cdim), lambda i: (0, i, 0))],
        out_specs=_rows(tr, cdim), out_shape=jax.ShapeDtypeStruct((r, cdim), F32),
        compiler_params=_cp(("parallel",)),
    )(t)


def _reduce_big(gs):
    got, mine = _pair_exchange(gs)
    ps = [_add2(m.reshape(-1, m.shape[2]), g.reshape(-1, g.shape[2]), f"grad_pair_add_{i}").reshape(m.shape)
          for i, (m, g) in enumerate(zip(mine, got))]
    xs = _chip_exchange(ps)
    hs = [_sum4(t, f"grad_chip_sum_{i}") for i, t in enumerate(xs)]
    return _pair_gather(hs)


HBM = pl.BlockSpec(memory_space=pltpu.HBM)
SEM = pl.BlockSpec(memory_space=pltpu.SEMAPHORE)
EFFECT = pltpu.SideEffectType.DATAFLOW_SIDE_EFFECTING


def _copies_start(name, bufs, plan, n):
    nb = len(bufs)

    def body(*refs):
        ins, sems, token = refs[:nb], refs[nb:nb + 2 * n], refs[-1]
        for kk, (src, dst, dev) in enumerate(plan(ins)):
            pltpu.make_async_remote_copy(src_ref=src, dst_ref=dst, send_sem=sems[2 * kk], recv_sem=sems[2 * kk + 1],
                                         device_id=dev, device_id_type=MESH).start()
        token[...] = jnp.zeros_like(token)

    outs = pl.pallas_call(
        body, name=name,
        out_shape=tuple([pltpu.SemaphoreType.DMA(())] * (2 * n) + [pltpu.HBM(t.shape, t.dtype) for t in bufs]
                        + [jax.ShapeDtypeStruct((8, LANES), F32)]),
        in_specs=[HBM] * nb,
        out_specs=tuple([SEM] * (2 * n) + [HBM] * nb + [pl.BlockSpec(memory_space=pltpu.VMEM)]),
        input_output_aliases={t: 2 * n + t for t in range(nb)},
        compiler_params=pltpu.CompilerParams(has_side_effects=EFFECT),
    )(*[pltpu.with_memory_space_constraint(t, pltpu.HBM) for t in bufs])
    return outs[:2 * n], outs[2 * n:2 * n + nb], outs[-1]


def _copies_wait(name, sems, bufs, plan, n, after):
    nb = len(bufs)

    def body(*refs):
        ins, sem_refs = refs[:nb], refs[nb:nb + 2 * n]
        for kk, (src, dst, dev) in enumerate(plan(ins)):
            cp = pltpu.make_async_remote_copy(src_ref=src, dst_ref=dst, send_sem=sem_refs[2 * kk],
                                              recv_sem=sem_refs[2 * kk + 1], device_id=dev, device_id_type=MESH)
            cp.wait_send()
            cp.wait_recv()

    return pl.pallas_call(
        body, name=name,
        out_shape=tuple(pltpu.HBM(t.shape, t.dtype) for t in bufs),
        in_specs=[HBM] * nb + [SEM] * (2 * n) + [ANY],
        out_specs=tuple([HBM] * nb),
        input_output_aliases={t: t for t in range(nb)},
        compiler_params=pltpu.CompilerParams(has_side_effects=EFFECT),
    )(*bufs, *sems, after)


def _plan_gather(n_w):
    def plan(refs):
        x, y, c, chips = _place()
        me = 2 * x + y
        return [(refs[a], refs[n_w + a].at[me], (*chip, c)) for a in range(n_w) for chip in chips]
    return plan


def _plan_pair_halves(n_g, rows):
    def plan(refs):
        x, y, c, _ = _place()
        return [(refs[a].at[:, _half(1 - c, rows[a])], refs[n_g + a], (x, y, 1 - c)) for a in range(n_g)]
    return plan


def _plan_chip_parts(n_g):
    def plan(refs):
        x, y, c, chips = _place()
        me = 2 * x + y
        return [(refs[a].at[2 * px + py], refs[n_g + a].at[me], (px, py, c))
                for a in range(n_g) for (px, py) in chips]
    return plan


def _plan_pair_fill(n_g, rows):
    def plan(refs):
        x, y, c, _ = _place()
        return [(refs[a].at[_half(c, rows[a])], refs[a].at[_half(c, rows[a])], (x, y, 1 - c)) for a in range(n_g)]
    return plan


def _pair_add(g, got, name):
    _, rows, cols = g.shape
    hr = rows // 2
    tr = min(hr, 256)
    nb = hr // tr

    def body(g_ref, got_ref, p_ref, own_ref):
        val = g_ref[...] + got_ref[...]
        p_ref[...] = val

        @pl.when(pl.program_id(1) == 2 * lax.axis_index("x") + lax.axis_index("y"))
        def _():
            own_ref[...] = val

    def mine(i, s):
        return (2 * lax.axis_index("x") + lax.axis_index("y"), i, 0)

    return pl.pallas_call(
        body, name=name, grid=(nb, N_CHIPS),
        in_specs=[pl.BlockSpec((None, tr, cols), lambda i, s: (s, lax.axis_index("c") * nb + i, 0)),
                  pl.BlockSpec((None, tr, cols), lambda i, s: (s, i, 0))],
        out_specs=[pl.BlockSpec((None, tr, cols), lambda i, s: (s, i, 0)), pl.BlockSpec((None, tr, cols), mine)],
        out_shape=[jax.ShapeDtypeStruct((N_CHIPS, hr, cols), F32)] * 2,
        compiler_params=_cp(("parallel", "arbitrary")),
    )(g, got)


def _chip_sum(parts, name):
    _, hr, cols = parts.shape
    tr = min(hr, 128)
    nb = hr // tr

    def body(t_ref, o_ref):
        o_ref[...] = ((t_ref[0] + t_ref[1]) + t_ref[2]) + t_ref[3]

    return pl.pallas_call(
        body, name=name, grid=(nb,),
        in_specs=[pl.BlockSpec((N_CHIPS, tr, cols), lambda i: (0, i, 0))],
        out_specs=pl.BlockSpec((tr, cols), lambda i: (lax.axis_index("c") * nb + i, 0)),
        out_shape=jax.ShapeDtypeStruct((2 * hr, cols), F32),
        compiler_params=_cp(("parallel",)),
    )(parts)


class _Reduction:
    def __init__(self, tag, rows):
        self.tag, self.n, self.rows = tag, len(rows), rows
        self.plans = (_plan_pair_halves(self.n, rows), _plan_chip_parts(self.n), _plan_pair_fill(self.n, rows))
        self.flight = None

    def _name(self, what):
        return f"grad_{self.tag}_{what}"

    def start(self, gs):
        gots = [lax.empty((N_CHIPS, t.shape[1] // 2, t.shape[2]), F32) for t in gs]
        self.flight = _copies_start(self._name("pair_start"), list(gs) + gots, self.plans[0], self.n)
        return self.flight[2]

    def after_pair(self, after):
        sems, bufs, _ = self.flight
        out = _copies_wait(self._name("pair_wait"), sems, bufs, self.plans[0], self.n, after)
        sums = [_pair_add(g, got, self._name(f"pair_add_{i}"))
                for i, (g, got) in enumerate(zip(out[:self.n], out[self.n:]))]
        self.flight = _copies_start(self._name("chip_start"), [p for p, _ in sums] + [own for _, own in sums],
                                    self.plans[1], 3 * self.n)
        return self.flight[2]

    def after_chips(self, after):
        sems, bufs, _ = self.flight
        out = _copies_wait(self._name("chip_wait"), sems, bufs, self.plans[1], 3 * self.n, after)
        fulls = [_chip_sum(t, self._name(f"chip_sum_{i}")) for i, t in enumerate(out[self.n:])]
        self.flight = _copies_start(self._name("fill_start"), fulls, self.plans[2], self.n)
        return self.flight[2]

    def finish(self, after):
        sems, bufs, _ = self.flight
        return _copies_wait(self._name("fill_wait"), sems, bufs, self.plans[2], self.n, after)


def _adamw(w, g, m, v, name, tr):
    r, cdim = w.shape

    def body(w_ref, g_ref, m_ref, v_ref, d_ref, nm_ref, nv_ref):
        g = g_ref[...]
        nm = ADAM_B1 * m_ref[...] + (1.0 - ADAM_B1) * g
        nv = ADAM_B2 * v_ref[...] + (1.0 - ADAM_B2) * (g * g)
        m_hat = nm / (1.0 - ADAM_B1 ** ADAM_STEP)
        v_hat = nv / (1.0 - ADAM_B2 ** ADAM_STEP)
        d_ref[...] = -ADAM_LR * (m_hat / (jnp.sqrt(v_hat) + ADAM_EPS) + ADAM_WD * w_ref[...])
        nm_ref[...] = nm
        nv_ref[...] = nv

    return pl.pallas_call(
        body, name=name, grid=(r // tr,), in_specs=[_rows(tr, cdim)] * 4, out_specs=[_rows(tr, cdim)] * 3,
        out_shape=[jax.ShapeDtypeStruct((r, cdim), F32)] * 3, compiler_params=_cp(("parallel",)),
    )(w, g, m, v)


REPLICATED = (("norm_mix_pre", 1024), ("norm_mix_post", 1024), ("norm_ffn_pre", 1024), ("norm_ffn_post", 1024),
              ("rel_bias", 256), ("sinks", 8), ("rwkv_shift_mix", 1792), ("w0", 512), ("a0", 512), ("k_k", 512),
              ("k_a", 512), ("r_k", 512), ("ln_x_g", 512), ("ln_x_b", 512), ("conv_b", 8192))
SMALL_SHARDED = (("w_decay_up", LORA_DECAY, D_RWKV), ("w_iclr_up", LORA_ICLR, D_RWKV),
                 ("w_gate_up", LORA_GATE, D_RWKV), ("conv_w", 3, 2 * D_FF))
BIG = (("w_in", D_MODEL, 640), ("w_out", 256, D_MODEL), ("w_ffn_up", D_MODEL, 2048), ("w_ffn_down", 1024, D_MODEL))
PACK_ALIGN = 8 * LANES


def _pack(pieces):
    flat = []
    for t in pieces:
        t = t.reshape(-1)
        pad = (-t.shape[0]) % LANES
        flat.append(jnp.pad(t, (0, pad)) if pad else t)
    flat = jnp.concatenate(flat)
    pad = (-flat.shape[0]) % PACK_ALIGN
    return jnp.pad(flat, (0, pad)).reshape(-1, LANES)


def _unpack(buf, sizes):
    flat, out, off = buf.reshape(-1), [], 0
    for n in sizes:
        out.append(flat[off:off + n])
        off += n + ((-n) % LANES)
    return out


def kernel(x, norm_mix_pre, norm_mix_post, norm_ffn_pre, norm_ffn_post, w_in, rel_bias, sinks, rwkv_shift_mix, w0, w_decay_up, a0, w_iclr_up, w_gate_up, k_k, k_a, r_k, ln_x_g, ln_x_b, w_out, w_ffn_up, conv_w, conv_b, w_ffn_down, loss_target, m_norm_mix_pre, m_norm_mix_post, m_norm_ffn_pre, m_norm_ffn_post, m_w_in, m_rel_bias, m_sinks, m_rwkv_shift_mix, m_w0, m_w_decay_up, m_a0, m_w_iclr_up, m_w_gate_up, m_k_k, m_k_a, m_r_k, m_ln_x_g, m_ln_x_b, m_w_out, m_w_ffn_up, m_conv_w, m_conv_b, m_w_ffn_down, v_norm_mix_pre, v_norm_mix_post, v_norm_ffn_pre, v_norm_ffn_post, v_w_in, v_rel_bias, v_sinks, v_rwkv_shift_mix, v_w0, v_w_decay_up, v_a0, v_w_iclr_up, v_w_gate_up, v_k_k, v_k_a, v_r_k, v_ln_x_g, v_ln_x_b, v_w_out, v_w_ffn_up, v_conv_w, v_conv_b, v_w_ffn_down):
    given = dict(locals())
    names = [n for n, _ in REPLICATED] + [n for n, _, _ in SMALL_SHARDED] + [n for n, _, _ in BIG]
    order = ["norm_mix_pre", "norm_mix_post", "norm_ffn_pre", "norm_ffn_post", "w_in", "rel_bias", "sinks",
             "rwkv_shift_mix", "w0", "w_decay_up", "a0", "w_iclr_up", "w_gate_up", "k_k", "k_a", "r_k", "ln_x_g",
             "ln_x_b", "w_out", "w_ffn_up", "conv_w", "conv_b", "w_ffn_down"]
    assert sorted(names) == sorted(order)
    shard = 2 * lax.axis_index("x") + lax.axis_index("y")

    big_sh = {n: given[n].reshape(a, b).astype(BF16) for n, a, b in BIG}
    small_sh = [given[n].reshape(r, c // N_CHIPS) for n, r, c in SMALL_SHARDED]
    gathered = _gather_weights([big_sh["w_in"]], small_sh)
    rest = ("w_out", "w_ffn_up", "w_ffn_down")
    win_st, rest_sh = lax.optimization_barrier((gathered[0], [big_sh[n] for n in rest]))
    sm = {n: given[n] for n, _ in REPLICATED}
    sm["r_k"] = r_k.reshape(N_Q_HEADS, HEAD_DIM)
    for (n, r, c), st in zip(SMALL_SHARDED, gathered[1:]):
        sm[n] = st.transpose(1, 0, 2).reshape(r, c)

    lands = [lax.dynamic_update_slice(lax.empty((N_CHIPS,) + t.shape, BF16), t[None], (shard, 0, 0)) for t in rest_sh]
    plan_w = _plan_gather(len(rest))
    w_sems, w_bufs, token = _copies_start("gather_rest_start", rest_sh + lands, plan_w, 9)
    sm["norm_mix_pre"] = norm_mix_pre + token[0:1, 0:1]

    def on_rest_weights(after):
        out = _copies_wait("gather_rest_wait", w_sems, w_bufs, plan_w, 9, after)
        wout_st, wup_st, wdown_st = out[3:]
        return wout_st.reshape(D_MODEL, D_MODEL), wup_st, wdown_st.reshape(D_FF, D_MODEL)

    red_a = _Reduction("a", (1024, D_MODEL, 256))
    red_b = _Reduction("b", (D_MODEL,))

    def on_grads_a(gs):
        d_wdown, d_wup, d_wout = gs
        return red_a.start([d_wdown.reshape(N_CHIPS, 1024, D_MODEL), d_wup, d_wout.reshape(N_CHIPS, 256, D_MODEL)])

    handlers = {"rest_weights": on_rest_weights, "grads_a": on_grads_a, "seam_1": red_a.after_pair,
                "seam_2": red_a.after_chips, "grads_b": lambda g: red_b.start([g])}
    steps = _local_step(x[0], loss_target[0], sm, win_st)
    kind, payload = next(steps)
    while True:
        try:
            kind, payload = steps.send(handlers[kind](payload))
        except StopIteration as done:
            loss, grad_x, grads = done.value
            break
    loss = lax.psum(loss[0, 0], ("x", "y", "c"))

    rep_sizes = [s for _, s in REPLICATED] + [r * c for _, r, c in SMALL_SHARDED]
    small_sum = _allreduce_small(_pack([grads[n] for n, _ in REPLICATED] + [grads[n] for n, _, _ in SMALL_SHARDED]))
    small_g = _unpack(small_sum, rep_sizes)
    g_out = {n: t.reshape(given[n].shape) for (n, _), t in zip(REPLICATED, small_g)}
    for (n, r, c), t in zip(SMALL_SHARDED, small_g[len(REPLICATED):]):
        g_out[n] = lax.dynamic_slice_in_dim(t.reshape(r, c), shard * (c // N_CHIPS), c // N_CHIPS, axis=1)
    red_b.after_pair(small_sum)
    g_out["w_ffn_down"], g_out["w_ffn_up"], g_out["w_out"] = red_a.finish(small_sum)

    small_names = [n for n, _ in REPLICATED] + [n for n, _, _ in SMALL_SHARDED]
    packs = [_pack([src[n] for n in small_names]) for src in
             ({n: given[n] for n in small_names}, g_out, {n: given["m_" + n] for n in small_names},
              {n: given["v_" + n] for n in small_names})]
    small_sizes = [int(np.prod(given[n].shape)) for n in small_names]
    upd = [_unpack(t, small_sizes) for t in _adamw(*packs, "adamw_small", packs[0].shape[0])]
    delta, new_m, new_v = ({n: t.reshape(given[n].shape) for n, t in zip(small_names, u)} for u in upd)
    for n, a, b in reversed(BIG):
        if n == "w_out":
            red_b.after_chips(delta["w_ffn_up"])
        if n == "w_in":
            g_out[n], = red_b.finish(delta["w_out"])
        d, nm, nv = _adamw(given[n].reshape(a, b), g_out[n], given["m_" + n].reshape(a, b),
                           given["v_" + n].reshape(a, b), "adamw_" + n, 128)
        delta[n], new_m[n], new_v[n] = d, nm, nv

    def shaped(d):
        return [d[n].reshape(given[n].shape) for n in order]

    return (loss, grad_x.reshape(x.shape), *shaped(g_out), *shaped(delta), *shaped(new_m), *shaped(new_v))
```

```python
import functools
import math

import numpy as np
import jax
import jax.numpy as jnp
from jax import lax
from jax.experimental import pallas as pl
from jax.experimental.pallas import tpu as pltpu

F32 = jnp.float32
BF16 = jnp.bfloat16
MESH = pl.DeviceIdType.MESH

SEQ = 2048
D_MODEL = 1024
HEAD_DIM = 64
D_ATTN = 512
D_RWKV = 512
D_KV = 128
N_Q_HEADS = 8
N_KV_HEADS = 2
Q_PER_KV = 4
BLOCK = 128
N_BUCKETS = 32
MAX_DISTANCE = 128
LORA_DECAY = 64
LORA_ICLR = 64
LORA_GATE = 128
RWKV_COLS = 3 * D_RWKV + LORA_DECAY + LORA_ICLR + LORA_GATE
P_OFF = D_ATTN + 2 * D_KV
D_IN = P_OFF + RWKV_COLS
D_FF = 4096
NORM_EPS = 1e-6
GN_EPS = 64e-5
NEG_INF = -1e30
N_CHIPS = 4
N_DEV = 8

ADAM_LR = 0.001
ADAM_B1 = 0.9
ADAM_B2 = 0.999
ADAM_EPS = 1e-08
ADAM_WD = 0.01
ADAM_STEP = 10

VMEM_LIMIT = 52 * 1024 * 1024
LANES = 128


def _cp(sem=None, vmem=VMEM_LIMIT):
    kw = dict(vmem_limit_bytes=vmem)
    if sem is not None:
        kw["dimension_semantics"] = sem
    return pltpu.CompilerParams(**kw)


def _rows(tr, nc):
    return pl.BlockSpec((tr, nc), lambda i: (i, 0))


def _const(shape):
    return pl.BlockSpec(shape, lambda *_: (0,) * len(shape))


ANY = pl.BlockSpec(memory_space=pl.ANY)


def _split(x, n):
    parts = []
    for _ in range(n - 1):
        h = x.astype(BF16)
        parts.append(h)
        x = x - h.astype(F32)
    parts.append(x.astype(BF16))
    return parts


def _dot(a, b, dn=(((1,), (0,)), ((), ()))):
    return lax.dot_general(a, b, dn, preferred_element_type=F32)


NN = (((1,), (0,)), ((), ()))
NT = (((1,), (1,)), ((), ()))
TN = (((0,), (0,)), ((), ()))


def _dot_ind(x, ind_bf16, n=3):
    acc = None
    for part in _split(x, n):
        t = _dot(part, ind_bf16)
        acc = t if acc is None else acc + t
    return acc


def _head_ones(n, scale=1.0):
    r = lax.broadcasted_iota(jnp.int32, (n, n), 0) >> 6
    c = lax.broadcasted_iota(jnp.int32, (n, n), 1) >> 6
    return jnp.where(r == c, 1.0, 0.0).astype(BF16)


def _matmul(a, b, mode, name, *, m, n, k, tm, tn, tk, a_spec=None, b_spec=None, out=None, out_dtype=F32):
    nk = k // tk
    dn = {"nn": NN, "nt": NT, "tn": TN}[mode]

    def body(a_ref, b_ref, o_ref, *scratch):
        part = _dot(a_ref[...], b_ref[...], dn)
        if nk == 1:
            o_ref[...] = part.astype(out_dtype)
        else:
            acc_ref, = scratch
            kk = pl.program_id(2)

            @pl.when(kk == 0)
            def _():
                acc_ref[...] = part

            @pl.when(kk > 0)
            def _():
                acc_ref[...] += part

            @pl.when(kk == nk - 1)
            def _():
                o_ref[...] = acc_ref[...].astype(out_dtype)

    if a_spec is None:
        a_spec = (pl.BlockSpec((tk, tm), lambda i, j, kk: (kk, i)) if mode == "tn"
                  else pl.BlockSpec((tm, tk), lambda i, j, kk: (i, kk)))
    if b_spec is None:
        b_spec = (pl.BlockSpec((tn, tk), lambda i, j, kk: (j, kk)) if mode == "nt"
                  else pl.BlockSpec((tk, tn), lambda i, j, kk: (kk, j)))
    return pl.pallas_call(
        body, name=name, grid=(m // tm, n // tn, nk),
        in_specs=[a_spec, b_spec],
        out_specs=pl.BlockSpec((tm, tn), lambda i, j, kk: (i, j)) if out is None else out[1],
        out_shape=jax.ShapeDtypeStruct((m, n) if out is None else out[0], out_dtype),
        scratch_shapes=[] if nk == 1 else [pltpu.VMEM((tm, tn), F32)],
        compiler_params=_cp(("parallel", "parallel", "arbitrary")),
    )(a, b)


def _rstd(x):
    return lax.rsqrt(jnp.mean(x * x, axis=-1, keepdims=True) + NORM_EPS)


def _rms_bwd(x, r, g, dy):
    gy = dy * g
    return r * gy - x * ((r * r * r) * (jnp.sum(x * gy, axis=-1, keepdims=True) / x.shape[-1]))


TR = 256


def _norm_cast(x, g, name):
    def body(x_ref, g_ref, h_ref):
        x = x_ref[...]
        h_ref[...] = (x * _rstd(x) * g_ref[...]).astype(BF16)

    return pl.pallas_call(
        body, name=name, grid=(SEQ // TR,),
        in_specs=[_rows(TR, D_MODEL), _const((1, D_MODEL))],
        out_specs=_rows(TR, D_MODEL),
        out_shape=jax.ShapeDtypeStruct((SEQ, D_MODEL), BF16),
        compiler_params=_cp(("parallel",)),
    )(x, g)


def _mix_norm(x, mix, g2, g3):
    def body(x_ref, mix_ref, g2_ref, g3_ref, x2_ref, h3_ref):
        mixv = mix_ref[...]
        x2 = x_ref[...] + mixv * _rstd(mixv) * g2_ref[...]
        x2_ref[...] = x2
        h3_ref[...] = (x2 * _rstd(x2) * g3_ref[...]).astype(BF16)

    return pl.pallas_call(
        body, name="mix_norm", grid=(SEQ // TR,),
        in_specs=[_rows(TR, D_MODEL), _rows(TR, D_MODEL), _const((1, D_MODEL)), _const((1, D_MODEL))],
        out_specs=[_rows(TR, D_MODEL), _rows(TR, D_MODEL)],
        out_shape=[jax.ShapeDtypeStruct((SEQ, D_MODEL), F32), jax.ShapeDtypeStruct((SEQ, D_MODEL), BF16)],
        compiler_params=_cp(("parallel",)),
    )(x, mix, g2, g3)


def _loss_head(x2, f, g4, target):
    def body(x2_ref, f_ref, g4_ref, t_ref, loss_ref, dy_ref, df_ref, dg_ref):
        i = pl.program_id(0)
        f = f_ref[...]
        g4 = g4_ref[...]
        r = _rstd(f)
        e = x2_ref[...] + f * r * g4 - t_ref[...]
        dy = e * (1.0 / D_MODEL)
        dy_ref[...] = dy
        df_ref[...] = _rms_bwd(f, r, g4, dy).astype(BF16)
        part = 0.5 * jnp.sum(jnp.sum(e * e, axis=-1, keepdims=True), axis=0, keepdims=True) * (1.0 / D_MODEL)
        dg = jnp.sum(dy * f * r, axis=0, keepdims=True)

        @pl.when(i == 0)
        def _():
            loss_ref[...] = jnp.zeros_like(loss_ref)
            dg_ref[...] = jnp.zeros_like(dg_ref)

        loss_ref[...] += jnp.broadcast_to(part, loss_ref.shape)
        dg_ref[...] += dg

    return pl.pallas_call(
        body, name="loss_head", grid=(SEQ // TR,),
        in_specs=[_rows(TR, D_MODEL), _rows(TR, D_MODEL), _const((1, D_MODEL)), _rows(TR, D_MODEL)],
        out_specs=[_const((8, LANES)), _rows(TR, D_MODEL), _rows(TR, D_MODEL), _const((1, D_MODEL))],
        out_shape=[jax.ShapeDtypeStruct((8, LANES), F32), jax.ShapeDtypeStruct((SEQ, D_MODEL), F32),
                   jax.ShapeDtypeStruct((SEQ, D_MODEL), BF16), jax.ShapeDtypeStruct((1, D_MODEL), F32)],
        compiler_params=_cp(("arbitrary",)),
    )(x2, f, g4, target)


def _mid_bwd(x2, mix, dy, dh3, g2, g3):
    def body(x2_ref, mix_ref, dy_ref, dh3_ref, g2_ref, g3_ref, dx2_ref, dmix_ref, dg2_ref, dg3_ref):
        i = pl.program_id(0)
        x2 = x2_ref[...]
        mixv = mix_ref[...]
        dh3 = dh3_ref[...]
        r3 = _rstd(x2)
        dx2 = dy_ref[...] + _rms_bwd(x2, r3, g3_ref[...], dh3)
        dx2_ref[...] = dx2
        r2 = _rstd(mixv)
        dmix_ref[...] = _rms_bwd(mixv, r2, g2_ref[...], dx2).astype(BF16)

        @pl.when(i == 0)
        def _():
            dg2_ref[...] = jnp.zeros_like(dg2_ref)
            dg3_ref[...] = jnp.zeros_like(dg3_ref)

        dg3_ref[...] += jnp.sum(dh3 * x2 * r3, axis=0, keepdims=True)
        dg2_ref[...] += jnp.sum(dx2 * mixv * r2, axis=0, keepdims=True)

    return pl.pallas_call(
        body, name="mid_bwd", grid=(SEQ // TR,),
        in_specs=[_rows(TR, D_MODEL)] * 4 + [_const((1, D_MODEL))] * 2,
        out_specs=[_rows(TR, D_MODEL), _rows(TR, D_MODEL), _const((1, D_MODEL)), _const((1, D_MODEL))],
        out_shape=[jax.ShapeDtypeStruct((SEQ, D_MODEL), F32), jax.ShapeDtypeStruct((SEQ, D_MODEL), BF16),
                   jax.ShapeDtypeStruct((1, D_MODEL), F32), jax.ShapeDtypeStruct((1, D_MODEL), F32)],
        compiler_params=_cp(("arbitrary",)),
    )(x2, mix, dy, dh3, g2, g3)


def _first_bwd(x, dx2, dh1, g1):
    def body(x_ref, dx2_ref, dh1_ref, g1_ref, dx_ref, dg1_ref):
        i = pl.program_id(0)
        x = x_ref[...]
        dh1 = dh1_ref[...]
        r = _rstd(x)
        dx_ref[...] = dx2_ref[...] + _rms_bwd(x, r, g1_ref[...], dh1)

        @pl.when(i == 0)
        def _():
            dg1_ref[...] = jnp.zeros_like(dg1_ref)

        dg1_ref[...] += jnp.sum(dh1 * x * r, axis=0, keepdims=True)

    return pl.pallas_call(
        body, name="first_bwd", grid=(SEQ // TR,),
        in_specs=[_rows(TR, D_MODEL)] * 3 + [_const((1, D_MODEL))],
        out_specs=[_rows(TR, D_MODEL), _const((1, D_MODEL))],
        out_shape=[jax.ShapeDtypeStruct((SEQ, D_MODEL), F32), jax.ShapeDtypeStruct((1, D_MODEL), F32)],
        compiler_params=_cp(("arbitrary",)),
    )(x, dx2, dh1, g1)


TC = 256
N_CB = D_FF // TC
GELU_C = math.sqrt(2.0 / math.pi)


def _shift_down(u, s):
    rolled = pltpu.roll(u, s, 0)
    row = lax.broadcasted_iota(jnp.int32, u.shape, 0)
    return jnp.where(row >= s, rolled, 0.0)


def _shift_up(u, s):
    n = u.shape[0]
    rolled = pltpu.roll(u, n - s, 0)
    row = lax.broadcasted_iota(jnp.int32, u.shape, 0)
    return jnp.where(row < n - s, rolled, 0.0)


def _conv3(u, w, b):
    return b + w[0:1] * _shift_down(u, 2) + w[1:2] * _shift_down(u, 1) + w[2:3] * u


def _gelu_and_grad(x):
    inner = GELU_C * (x + 0.044715 * (x * x * x))
    t = jnp.tanh(inner)
    gelu = 0.5 * x * (1.0 + t)
    dgelu = 0.5 * (1.0 + t) + 0.5 * x * (1.0 - t * t) * (GELU_C * (1.0 + 3 * 0.044715 * (x * x)))
    return gelu, dgelu


def _ffn_specs():
    col = lambda off: pl.BlockSpec((SEQ, TC), lambda *g: (0, g[-1] + off))
    w = lambda off: pl.BlockSpec((3, TC), lambda *g: (0, g[-1] + off))
    b = lambda off: pl.BlockSpec((1, TC), lambda *g: (0, g[-1] + off))
    return col, w, b


def _ffn_act(u, conv_w, conv_b):
    col, w, b = _ffn_specs()

    def body(ug_ref, uv_ref, wg_ref, wv_ref, bg_ref, bv_ref, act_ref):
        gate = _conv3(ug_ref[...], wg_ref[...], bg_ref[...])
        val = _conv3(uv_ref[...], wv_ref[...], bv_ref[...])
        act_ref[...] = (_gelu_and_grad(gate)[0] * val).astype(BF16)

    return pl.pallas_call(
        body, name="ffn_act", grid=(N_CB,),
        in_specs=[col(0), col(N_CB), w(0), w(N_CB), b(0), b(N_CB)],
        out_specs=col(0),
        out_shape=jax.ShapeDtypeStruct((SEQ, D_FF), BF16),
        compiler_params=_cp(("parallel",)),
    )(u, u, conv_w, conv_w, conv_b, conv_b)


def _ffn_act_bwd(u, dact, conv_w, conv_b):
    col, w, b = _ffn_specs()
    both = lambda rows: pl.BlockSpec((2, rows, TC), lambda j: (0, 0, j))

    def body(ug_ref, uv_ref, da_ref, wg_ref, wv_ref, bg_ref, bv_ref, du_ref, dw_ref, db_ref):
        ug, uv = ug_ref[...], uv_ref[...]
        wg, wv = wg_ref[...], wv_ref[...]
        gate = _conv3(ug, wg, bg_ref[...])
        val = _conv3(uv, wv, bv_ref[...])
        gelu, dgelu = _gelu_and_grad(gate)
        da = da_ref[...]
        for h, (duc, uh, wh) in enumerate(((da * val * dgelu, ug, wg), (da * gelu, uv, wv))):
            du = wh[2:3] * duc + wh[1:2] * _shift_up(duc, 1) + wh[0:1] * _shift_up(duc, 2)
            du_ref[h] = du.astype(BF16)
            db_ref[h] = jnp.sum(duc, axis=0, keepdims=True)
            dw_ref[h] = jnp.concatenate(
                [jnp.sum(duc * _shift_down(uh, 2), axis=0, keepdims=True),
                 jnp.sum(duc * _shift_down(uh, 1), axis=0, keepdims=True),
                 jnp.sum(duc * uh, axis=0, keepdims=True)], axis=0)

    return pl.pallas_call(
        body, name="ffn_act_bwd", grid=(N_CB,),
        in_specs=[col(0), col(N_CB), col(0), w(0), w(N_CB), b(0), b(N_CB)],
        out_specs=[both(SEQ), both(3), both(1)],
        out_shape=[jax.ShapeDtypeStruct((2, SEQ, D_FF), BF16), jax.ShapeDtypeStruct((2, 3, D_FF), F32),
                   jax.ShapeDtypeStruct((2, 1, D_FF), F32)],
        compiler_params=_cp(("parallel",)),
    )(u, u, dact, conv_w, conv_w, conv_b, conv_b)


def _t5_onehot():
    rel = (np.arange(BLOCK)[:, None] + BLOCK) - np.arange(2 * BLOCK)[None, :]
    n = np.maximum(rel, 0)
    max_exact = N_BUCKETS // 2
    large = max_exact + (np.log(np.maximum(n, 1).astype(np.float32) / np.float32(max_exact))
                         / np.float32(math.log(MAX_DISTANCE / max_exact))
                         * np.float32(N_BUCKETS - max_exact)).astype(np.int32)
    large = np.minimum(large, N_BUCKETS - 1)
    bucket = np.where(n < max_exact, n, large).reshape(-1)
    return (bucket[None, :] == np.arange(N_BUCKETS)[:, None]).astype(np.float32)


N_REL = BLOCK * 2 * BLOCK


def _bias_table(rel_bias_t, onehot):
    def body(rb_ref, oh_ref, o_ref):
        o_ref[...] = _dot_ind(rb_ref[...], oh_ref[...])

    return pl.pallas_call(
        body, name="bias_table", grid=(1,),
        in_specs=[_const((N_Q_HEADS, N_BUCKETS)), _const((N_BUCKETS, N_REL))],
        out_specs=_const((N_Q_HEADS, N_REL)),
        out_shape=jax.ShapeDtypeStruct((N_Q_HEADS, N_REL), F32),
        compiler_params=_cp(("arbitrary",)),
    )(rel_bias_t, onehot)


def _bias_table_bwd(dbias, onehot):
    def body(db_ref, oh_ref, o_ref):
        acc = None
        for part in _split(db_ref[...], 3):
            t = _dot(part, oh_ref[...], NT)
            acc = t if acc is None else acc + t
        o_ref[...] = acc

    return pl.pallas_call(
        body, name="bias_table_bwd", grid=(1,),
        in_specs=[_const((N_Q_HEADS, N_REL)), _const((N_BUCKETS, N_REL))],
        out_specs=_const((N_Q_HEADS, N_BUCKETS)),
        out_shape=jax.ShapeDtypeStruct((N_Q_HEADS, N_BUCKETS), F32),
        compiler_params=_cp(("arbitrary",)),
    )(dbias, onehot)


def _attn_pieces(n, q, kvp, kvc, bias_ref, sinks_ref, hk):
    qi = lax.broadcasted_iota(jnp.int32, (BLOCK, 2 * BLOCK), 0)
    kj = lax.broadcasted_iota(jnp.int32, (BLOCK, 2 * BLOCK), 1)
    rel = qi + BLOCK - kj
    first_key = jnp.where(n > 0, 0, BLOCK)
    ok = jnp.where(rel >= 0, jnp.where(rel < BLOCK, jnp.where(kj >= first_key, 1.0, 0.0), 0.0), 0.0)
    ok4 = jnp.concatenate([ok] * Q_PER_KV, axis=0) > 0.5
    c0 = hk * HEAD_DIM
    kcat = jnp.concatenate([kvp[:, c0:c0 + HEAD_DIM], kvc[:, c0:c0 + HEAD_DIM]], axis=0).astype(BF16)
    vcat = jnp.concatenate([kvp[:, D_KV + c0:D_KV + c0 + HEAD_DIM], kvc[:, D_KV + c0:D_KV + c0 + HEAD_DIM]],
                           axis=0).astype(BF16)
    q0 = hk * Q_PER_KV * HEAD_DIM
    qs = jnp.concatenate([q[:, q0 + g * HEAD_DIM:q0 + (g + 1) * HEAD_DIM] for g in range(Q_PER_KV)],
                         axis=0).astype(BF16)
    s = _dot(qs, kcat, NT) * (HEAD_DIM ** -0.5) + bias_ref[hk]
    s = jnp.where(ok4, s, NEG_INF)
    row = lax.broadcasted_iota(jnp.int32, (Q_PER_KV * BLOCK, 1), 0)
    sink = jnp.zeros((Q_PER_KV * BLOCK, 1), F32)
    for g in range(Q_PER_KV):
        sink = jnp.where((row >> 7) == g, sinks_ref[hk * Q_PER_KV + g], sink)
    m = jnp.maximum(jnp.max(s, axis=-1, keepdims=True), sink)
    p = jnp.exp(s - m)
    es = jnp.exp(sink - m)
    inv = 1.0 / (jnp.sum(p, axis=-1, keepdims=True) + es)
    return qs, kcat, vcat, p * inv, es * inv


def _attn_in_specs():
    return [pl.BlockSpec((BLOCK, D_ATTN), lambda n: (n, 0)),
            pl.BlockSpec((BLOCK, 2 * D_KV), lambda n: (jnp.maximum(n - 1, 0), D_ATTN // (2 * D_KV))),
            pl.BlockSpec((BLOCK, 2 * D_KV), lambda n: (n, D_ATTN // (2 * D_KV))),
            _const((N_KV_HEADS, Q_PER_KV * BLOCK, 2 * BLOCK)),
            pl.BlockSpec(memory_space=pltpu.SMEM)]


def _unstack_heads(t):
    return jnp.concatenate([t[g * BLOCK:(g + 1) * BLOCK] for g in range(Q_PER_KV)], axis=1)


def _attn_fwd(proj, bias, sinks):
    def body(q_ref, kvp_ref, kvc_ref, bias_ref, sinks_ref, o_ref):
        n = pl.program_id(0)
        q, kvp, kvc = q_ref[...], kvp_ref[...], kvc_ref[...]
        outs = []
        for hk in range(N_KV_HEADS):
            _, _, vcat, probs, _ = _attn_pieces(n, q, kvp, kvc, bias_ref, sinks_ref, hk)
            outs.append(_unstack_heads(_dot(probs.astype(BF16), vcat)))
        o_ref[...] = jnp.concatenate(outs, axis=1)

    return pl.pallas_call(
        body, name="attn_fwd", grid=(SEQ // BLOCK,),
        in_specs=_attn_in_specs(),
        out_specs=pl.BlockSpec((BLOCK, D_ATTN), lambda n: (n, 0)),
        out_shape=jax.ShapeDtypeStruct((SEQ, D_ATTN), F32),
        compiler_params=_cp(("parallel",)),
    )(proj, proj, proj, bias, sinks)


def _attn_bwd(proj, bias, sinks, dcat):
    nb = SEQ // BLOCK

    def body(q_ref, kvp_ref, kvc_ref, bias_ref, sinks_ref, do_ref, dq_ref, dkv_ref, dbias_ref, dsink_ref, dsacc):
        n = pl.program_id(0)

        @pl.when(n == 0)
        def _():
            dkv_ref[...] = jnp.zeros_like(dkv_ref)
            dbias_ref[...] = jnp.zeros_like(dbias_ref)
            dsacc[...] = jnp.zeros_like(dsacc)

        q, kvp, kvc = q_ref[...], kvp_ref[...], kvc_ref[...]
        do_all = do_ref[...]
        dqs, dks, dvs = [], [], []
        for hk in range(N_KV_HEADS):
            qs, kcat, vcat, probs, psink = _attn_pieces(n, q, kvp, kvc, bias_ref, sinks_ref, hk)
            q0 = hk * Q_PER_KV * HEAD_DIM
            do = jnp.concatenate([do_all[:, q0 + g * HEAD_DIM:q0 + (g + 1) * HEAD_DIM] for g in range(Q_PER_KV)],
                                 axis=0).astype(BF16)
            dprobs = _dot(do, vcat, NT)
            dvs.append(_dot(probs.astype(BF16), do, TN))
            rowdot = jnp.sum(probs * dprobs, axis=-1, keepdims=True)
            ds = probs * (dprobs - rowdot)
            dsacc[hk] += -psink * rowdot
            dbias_ref[hk] += ds
            dsb = (ds * (HEAD_DIM ** -0.5)).astype(BF16)
            dqs.append(_unstack_heads(_dot(dsb, kcat)))
            dks.append(_dot(dsb, qs, TN))
        dq_ref[...] = jnp.concatenate(dqs, axis=1)
        upd = jnp.concatenate(dks + dvs, axis=1)
        cur = pl.multiple_of(n * BLOCK, BLOCK)
        dkv_ref[pl.ds(cur, BLOCK), :] += upd[BLOCK:]

        @pl.when(n > 0)
        def _():
            prev = pl.multiple_of((n - 1) * BLOCK, BLOCK)
            dkv_ref[pl.ds(prev, BLOCK), :] += upd[:BLOCK]

        @pl.when(n == nb - 1)
        def _():
            for hk in range(N_KV_HEADS):
                for g in range(Q_PER_KV):
                    tot = jnp.sum(dsacc[hk, g * BLOCK:(g + 1) * BLOCK, :], axis=0, keepdims=True)
                    h = hk * Q_PER_KV + g
                    dsink_ref[h:h + 1, :] = jnp.broadcast_to(tot, (1, LANES))

    return pl.pallas_call(
        body, name="attn_bwd", grid=(nb,),
        in_specs=_attn_in_specs() + [pl.BlockSpec((BLOCK, D_ATTN), lambda n: (n, 0))],
        out_specs=[pl.BlockSpec((BLOCK, D_ATTN), lambda n: (n, 0)), _const((SEQ, 2 * D_KV)),
                   _const((N_KV_HEADS, Q_PER_KV * BLOCK, 2 * BLOCK)), _const((N_Q_HEADS, LANES))],
        out_shape=[jax.ShapeDtypeStruct((SEQ, D_ATTN), F32), jax.ShapeDtypeStruct((SEQ, 2 * D_KV), F32),
                   jax.ShapeDtypeStruct((N_KV_HEADS, Q_PER_KV * BLOCK, 2 * BLOCK), F32),
                   jax.ShapeDtypeStruct((N_Q_HEADS, LANES), F32)],
        scratch_shapes=[pltpu.VMEM((N_KV_HEADS, Q_PER_KV * BLOCK, 1), F32)],
        compiler_params=_cp(("arbitrary",)),
    )(proj, proj, proj, bias, sinks, dcat)


@jax.custom_vjp
def _head_sum(x):
    return _dot_ind(x, _head_ones(x.shape[-1]))


_head_sum.defvjp(lambda x: (_head_sum(x), None), lambda _, ct: (_head_sum(ct),))


@jax.custom_vjp
def _bdot(a, w):
    return _dot(a.astype(BF16), w.astype(BF16))


def _bdot_bwd(res, ct):
    a, w = res
    ctb = ct.astype(BF16)
    return _dot(ctb, w.astype(BF16), NT), _dot(a.astype(BF16), ctb, TN)


_bdot.defvjp(lambda a, w: (_bdot(a, w), (a, w)), _bdot_bwd)


def _sigmoid(x):
    return 0.5 * (jnp.tanh(0.5 * x) + 1.0)


def _softplus(x):
    return jnp.maximum(x, 0.0) + jnp.log(1.0 + jnp.exp(-jnp.abs(x)))


def _rwkv_core(r, k, v, zwa, zg, w0, wdu, a0, wiu, wgu, k_k, k_a):
    w_log = -_softplus(-(w0 + _bdot(jnp.tanh(zwa), wdu))) - 0.5
    decay = jnp.exp(-jnp.exp(w_log))
    a = _sigmoid(a0 + _bdot(zwa, wiu))
    g = _bdot(_sigmoid(zg), wgu)
    kk = k * k_k
    kk = kk / jnp.maximum(jnp.sqrt(_head_sum(kk * kk)), 1e-12)
    k2 = k * (1.0 + (a - 1.0) * k_a)
    return r, decay, k2, v, -kk, kk * a, g


def _rwkv_out(o, r, k2, v, g, lng, lnb, rk):
    mu = _head_sum(o) * (1.0 / HEAD_DIM)
    d = o - mu
    var = _head_sum(d * d) * (1.0 / HEAD_DIM)
    on = d * lax.rsqrt(var + GN_EPS) * lng + lnb
    bonus = _head_sum(r * k2 * rk) * v
    return (on + bonus) * g


P_SPLITS = (0, 512, 1024, 1536, 1664, 1792)
N_PREP_PARAMS = 7
HALO = 8


def _shifted_pieces(i, p_ref, halo_ref, mix_ref):
    p = p_ref[:, P_OFF:]
    prev_row = halo_ref[HALO - 1:HALO, P_OFF:] * jnp.where(i > 0, 1.0, 0.0)
    row = lax.broadcasted_iota(jnp.int32, p.shape, 0)
    pprev = jnp.where(row == 0, prev_row, pltpu.roll(p, 1, 0))
    delta = pprev - p
    ps = p + delta * mix_ref[...]
    return [ps[:, a:b] for a, b in zip(P_SPLITS[:-1], P_SPLITS[1:])], delta


def _prep_in_specs():
    return [_rows(TR, D_IN),
            pl.BlockSpec((HALO, D_IN), lambda i: (jnp.maximum(i * (TR // HALO) - 1, 0), 0)),
            _const((1, RWKV_COLS)), _const((1, D_RWKV)), _const((LANES, D_RWKV)), _const((1, D_RWKV)),
            _const((LANES, D_RWKV)), _const((LANES, D_RWKV)), _const((1, D_RWKV)), _const((1, D_RWKV))]


def _rwkv_prep(proj, mix, prm):
    def body(p_ref, halo_ref, mix_ref, *refs):
        prm_refs, outs = refs[:N_PREP_PARAMS], refs[N_PREP_PARAMS:]
        pieces, _ = _shifted_pieces(pl.program_id(0), p_ref, halo_ref, mix_ref)
        vals = _rwkv_core(*pieces, *[t[...] for t in prm_refs])
        for ref, val in zip(outs, vals):
            ref[...] = val

    return pl.pallas_call(
        body, name="rwkv_prep", grid=(SEQ // TR,),
        in_specs=_prep_in_specs(),
        out_specs=[_rows(TR, D_RWKV)] * 7,
        out_shape=[jax.ShapeDtypeStruct((SEQ, D_RWKV), F32)] * 7,
        compiler_params=_cp(("parallel",)),
    )(proj, proj, mix, *prm)


def _rwkv_prep_bwd(proj, mix, prm, cts):
    def body(p_ref, halo_ref, mix_ref, *refs):
        i = pl.program_id(0)
        prm_refs = refs[:N_PREP_PARAMS]
        ct_refs = refs[N_PREP_PARAMS:N_PREP_PARAMS + 10]
        dps_ref, dmix_ref = refs[N_PREP_PARAMS + 10:N_PREP_PARAMS + 12]
        dprm_refs = refs[N_PREP_PARAMS + 12:]
        pieces, delta = _shifted_pieces(i, p_ref, halo_ref, mix_ref)
        _, vjp = jax.vjp(_rwkv_core, *pieces, *[t[...] for t in prm_refs])
        dr1, dr2, dw, dk1, dk2, dv1, dv2, dkkn, db, dg = [t[...] for t in ct_refs]
        grads = vjp((dr1 + dr2, dw, dk1 + dk2, dv1 + dv2, dkkn, db, dg))
        dps = jnp.concatenate(grads[:5], axis=1)
        dps_ref[...] = dps

        @pl.when(i == 0)
        def _():
            dmix_ref[...] = jnp.zeros_like(dmix_ref)
            for ref in dprm_refs:
                ref[...] = jnp.zeros_like(ref)

        dmix_ref[...] += jnp.sum(dps * delta, axis=0, keepdims=True)
        for ref, gval in zip(dprm_refs, grads[5:]):
            ref[...] += gval

    prm_shapes = [(1, D_RWKV), (LANES, D_RWKV), (1, D_RWKV), (LANES, D_RWKV), (LANES, D_RWKV), (1, D_RWKV), (1, D_RWKV)]
    return pl.pallas_call(
        body, name="rwkv_prep_bwd", grid=(SEQ // TR,),
        in_specs=_prep_in_specs() + [_rows(TR, D_RWKV)] * 10,
        out_specs=[_rows(TR, RWKV_COLS), _const((1, RWKV_COLS))] + [_const(s) for s in prm_shapes],
        out_shape=[jax.ShapeDtypeStruct((SEQ, RWKV_COLS), F32), jax.ShapeDtypeStruct((1, RWKV_COLS), F32)]
        + [jax.ShapeDtypeStruct(s, F32) for s in prm_shapes],
        compiler_params=_cp(("arbitrary",)),
    )(proj, proj, mix, *prm, *cts)


def _rwkv_post(o, r, k2, v, g, lng, lnb, rk, attn):
    def body(o_ref, r_ref, k_ref, v_ref, g_ref, lng_ref, lnb_ref, rk_ref, attn_ref, cat_ref):
        rw = _rwkv_out(*[t[...] for t in (o_ref, r_ref, k_ref, v_ref, g_ref, lng_ref, lnb_ref, rk_ref)])
        cat_ref[...] = jnp.concatenate([attn_ref[...], rw], axis=1).astype(BF16)

    return pl.pallas_call(
        body, name="rwkv_post", grid=(SEQ // TR,),
        in_specs=[_rows(TR, D_RWKV)] * 5 + [_const((1, D_RWKV))] * 3 + [_rows(TR, D_ATTN)],
        out_specs=_rows(TR, D_MODEL),
        out_shape=jax.ShapeDtypeStruct((SEQ, D_MODEL), BF16),
        compiler_params=_cp(("parallel",)),
    )(o, r, k2, v, g, lng, lnb, rk, attn)


def _rwkv_post_bwd(o, r, k2, v, g, lng, lnb, rk, dcat):
    def body(o_ref, r_ref, k_ref, v_ref, g_ref, lng_ref, lnb_ref, rk_ref, dcat_ref,
             do_ref, dr_ref, dk_ref, dv_ref, dg_ref, dlng_ref, dlnb_ref, drk_ref):
        i = pl.program_id(0)
        args = [t[...] for t in (o_ref, r_ref, k_ref, v_ref, g_ref, lng_ref, lnb_ref, rk_ref)]
        _, vjp = jax.vjp(_rwkv_out, *args)
        grads = vjp(dcat_ref[:, D_ATTN:])
        for ref, gval in zip((do_ref, dr_ref, dk_ref, dv_ref, dg_ref), grads[:5]):
            ref[...] = gval

        @pl.when(i == 0)
        def _():
            for ref in (dlng_ref, dlnb_ref, drk_ref):
                ref[...] = jnp.zeros_like(ref)

        for ref, gval in zip((dlng_ref, dlnb_ref, drk_ref), grads[5:]):
            ref[...] += gval

    return pl.pallas_call(
        body, name="rwkv_post_bwd", grid=(SEQ // TR,),
        in_specs=[_rows(TR, D_RWKV)] * 5 + [_const((1, D_RWKV))] * 3 + [_rows(TR, D_MODEL)],
        out_specs=[_rows(TR, D_RWKV)] * 5 + [_const((1, D_RWKV))] * 3,
        out_shape=[jax.ShapeDtypeStruct((SEQ, D_RWKV), F32)] * 5 + [jax.ShapeDtypeStruct((1, D_RWKV), F32)] * 3,
        compiler_params=_cp(("arbitrary",)),
    )(o, r, k2, v, g, lng, lnb, rk, dcat)


def _assemble_dproj(dq, dkv, dps, mix):
    last = SEQ // HALO - 1

    def body(dq_ref, dkv_ref, dps_ref, nxt_ref, mix_ref, o_ref):
        i = pl.program_id(0)
        dps = dps_ref[...]
        mixv = mix_ref[...]
        nxt_row = nxt_ref[0:1, :] * jnp.where(i < SEQ // TR - 1, 1.0, 0.0)
        row = lax.broadcasted_iota(jnp.int32, dps.shape, 0)
        up = jnp.where(row == TR - 1, nxt_row, pltpu.roll(dps, TR - 1, 0))
        dp = dps * (1.0 - mixv) + up * mixv
        o_ref[...] = jnp.concatenate([dq_ref[...], dkv_ref[...], dp], axis=1).astype(BF16)

    return pl.pallas_call(
        body, name="assemble_dproj", grid=(SEQ // TR,),
        in_specs=[_rows(TR, D_ATTN), _rows(TR, 2 * D_KV), _rows(TR, RWKV_COLS),
                  pl.BlockSpec((HALO, RWKV_COLS), lambda i: (jnp.minimum((i + 1) * (TR // HALO), last), 0)),
                  _const((1, RWKV_COLS))],
        out_specs=_rows(TR, D_IN),
        out_shape=jax.ShapeDtypeStruct((SEQ, D_IN), BF16),
        compiler_params=_cp(("parallel",)),
    )(dq, dkv, dps, dps, mix)


N_PAIR = D_RWKV // LANES
CHUNK = 32
N_CHUNK = SEQ // CHUNK
GROUP = 8
STATE = (N_PAIR, HEAD_DIM, LANES)


def _lane_sums(lhs_tiles, ones2):
    out = _dot(jnp.concatenate(lhs_tiles, axis=0), ones2)
    return [out[i * HEAD_DIM:(i + 1) * HEAD_DIM] for i in range(len(lhs_tiles))]


def _seg_sum(xs, ones2):
    return _lane_sums([jnp.concatenate(_split(x, 2), axis=1) for x in xs], ones2)


def _col_form(rows, diag, ones2):
    zero = jnp.zeros((HEAD_DIM, LANES), BF16)
    tiles = []
    for row in rows:
        hi = row.astype(BF16)
        lo = (row - hi.astype(F32)).astype(BF16)
        tiles.append(jnp.concatenate(
            [jnp.where(diag, jnp.broadcast_to(part, (HEAD_DIM, LANES)), zero) for part in (hi, lo)], axis=1))
    return _lane_sums(tiles, ones2)


def _scan_consts():
    ones2 = jnp.concatenate([_head_ones(LANES)] * 2, axis=0)
    sub = lax.broadcasted_iota(jnp.int32, (HEAD_DIM, LANES), 0)
    lane_in_head = lax.broadcasted_iota(jnp.int32, (HEAD_DIM, LANES), 1) & (HEAD_DIM - 1)
    return ones2, lane_in_head == sub, lane_in_head


def _rows_of_columns(tile):
    t = tile.T
    return jnp.concatenate([t[:CHUNK], t[HEAD_DIM:HEAD_DIM + CHUNK]], axis=1)


def _pair(j):
    return slice(j * LANES, (j + 1) * LANES)


def _scan_fwd(r, w, k, v, kkn, b):
    def body(r_ref, w_ref, k_ref, v_ref, kkn_ref, b_ref, o_ref, st_ref, sa_ref, s_scr):
        c = pl.program_id(0)
        ones2, diag, lane_in_head = _scan_consts()

        @pl.when(c == 0)
        def _():
            s_scr[...] = jnp.zeros_like(s_scr)

        def group(gi, carry):
            row0 = pl.multiple_of(gi * GROUP, GROUP)
            states, ocols = list(carry[:N_PAIR]), list(carry[N_PAIR:])
            tiles = [[t[pl.ds(row0, GROUP), _pair(j)] for t in (r_ref, w_ref, k_ref, v_ref, kkn_ref, b_ref)]
                     for j in range(N_PAIR)]
            def row(j, name, u):
                return tiles[j]["rwkvnb".index(name)][u:u + 1]

            def emit_out(u, after):
                here = lane_in_head == gi * GROUP + u
                outs = _seg_sum([after[j] * row(j, "r", u) for j in range(N_PAIR)], ones2)
                for j in range(N_PAIR):
                    ocols[j] = jnp.where(here, outs[j], ocols[j])

            vcols = _col_form([row(j, "v", 0) for j in range(N_PAIR)], diag, ones2)
            after = None
            for u in range(GROUP):
                sas = _seg_sum([states[j] * row(j, "n", u) for j in range(N_PAIR)], ones2)
                if after is not None:
                    emit_out(u - 1, after)
                nxt = (_col_form([row(j, "v", u + 1) for j in range(N_PAIR)], diag, ones2)
                       if u + 1 < GROUP else None)
                for j in range(N_PAIR):
                    states[j] = states[j] * row(j, "w", u) + sas[j] * row(j, "b", u) + vcols[j] * row(j, "k", u)
                    st_ref[row0 + u, j] = states[j]
                    sa_ref[row0 + u, j] = sas[j]
                after, vcols = list(states), nxt
            emit_out(GROUP - 1, after)
            return tuple(states + ocols)

        zero = jnp.zeros((HEAD_DIM, LANES), F32)
        fin = lax.fori_loop(0, CHUNK // GROUP, group, tuple(s_scr[j] for j in range(N_PAIR)) + (zero,) * N_PAIR)
        for j in range(N_PAIR):
            s_scr[j] = fin[j]
            o_ref[:, _pair(j)] = _rows_of_columns(fin[N_PAIR + j])

    blk = pl.BlockSpec((CHUNK, D_RWKV), lambda c: (c, 0))
    per_step = pl.BlockSpec((CHUNK,) + STATE, lambda c: (c, 0, 0, 0))
    return pl.pallas_call(
        body, name="rwkv_scan_fwd", grid=(N_CHUNK,),
        in_specs=[blk] * 6,
        out_specs=[blk, per_step, per_step],
        out_shape=[jax.ShapeDtypeStruct((SEQ, D_RWKV), F32)] + [jax.ShapeDtypeStruct((SEQ,) + STATE, F32)] * 2,
        scratch_shapes=[pltpu.VMEM(STATE, F32)],
        compiler_params=_cp(("arbitrary",)),
    )(r, w, k, v, kkn, b)


def _scan_bwd(r, w, k, v, kkn, b, do, states, sas, ds_in, prev, name, first_chunk, n_chunks):
    top = first_chunk + n_chunks - 1

    def body(r_ref, w_ref, k_ref, v_ref, kkn_ref, b_ref, do_ref, st_ref, before_ref, sa_ref, ds_in_ref, *rest):
        dr_ref, dw_ref, dk_ref, dv_ref, dkkn_ref, db_ref, ds_out_ref, ds_scr = rest[-8:]
        i = pl.program_id(0)
        ones2, diag, lane_in_head = _scan_consts()

        @pl.when(i == 0)
        def _():
            ds_scr[...] = ds_in_ref[...]

        entry = [before_ref[0, j] * jnp.where(i < top, 1.0, 0.0) for j in range(N_PAIR)]

        def reverse(gr, carry):
            gi = CHUNK // GROUP - 1 - gr
            row0 = pl.multiple_of(gi * GROUP, GROUP)
            dstates, dvcols = list(carry[:N_PAIR]), list(carry[N_PAIR:])
            tiles = [[t[pl.ds(row0, GROUP), _pair(j)]
                      for t in (r_ref, w_ref, k_ref, v_ref, kkn_ref, b_ref, do_ref)] for j in range(N_PAIR)]
            rows = [[[None] * GROUP for _ in range(5)] for _ in range(N_PAIR)]

            def row(j, name, u):
                return tiles[j]["rwkvnbd".index(name)][u:u + 1]

            def cols_of(u):
                both = _col_form([row(j, "d", u) for j in range(N_PAIR)] + [row(j, "v", u) for j in range(N_PAIR)],
                                 diag, ones2)
                return [(both[j], both[N_PAIR + j]) for j in range(N_PAIR)]

            def emit_dv(u, dsp):
                here = lane_in_head == gi * GROUP + u
                outs = _seg_sum([dsp[j] * row(j, "k", u) for j in range(N_PAIR)], ones2)
                for j in range(N_PAIR):
                    dvcols[j] = jnp.where(here, outs[j], dvcols[j])

            cols = cols_of(GROUP - 1)
            before = None
            for u in reversed(range(GROUP)):
                tl = gi * GROUP + u
                dsp = [dstates[j] + cols[j][0] * row(j, "r", u) for j in range(N_PAIR)]
                dsas = _seg_sum([dsp[j] * row(j, "b", u) for j in range(N_PAIR)], ones2)
                if before is not None:
                    emit_dv(u + 1, before)
                nxt = cols_of(u - 1) if u > 0 else None
                for j in range(N_PAIR):
                    if u > 0:
                        s_prev = st_ref[tl - 1, j]
                    else:
                        s_prev = jnp.where(gi == 0, entry[j], st_ref[jnp.maximum(tl - 1, 0), j])
                    docol, vcol = cols[j]
                    rows[j][0][u] = jnp.sum(st_ref[tl, j] * docol, axis=0, keepdims=True)
                    rows[j][1][u] = jnp.sum(dsp[j] * s_prev, axis=0, keepdims=True)
                    rows[j][2][u] = jnp.sum(dsp[j] * vcol, axis=0, keepdims=True)
                    rows[j][3][u] = jnp.sum(s_prev * dsas[j], axis=0, keepdims=True)
                    rows[j][4][u] = jnp.sum(dsp[j] * sa_ref[tl, j], axis=0, keepdims=True)
                    dstates[j] = dsp[j] * row(j, "w", u) + dsas[j] * row(j, "n", u)
                before, cols = dsp, nxt
            emit_dv(0, before)
            for j in range(N_PAIR):
                for ref, rr in zip((dr_ref, dw_ref, dk_ref, dkkn_ref, db_ref), rows[j]):
                    ref[pl.ds(row0, GROUP), _pair(j)] = jnp.concatenate(rr, axis=0)
            return tuple(dstates + dvcols)

        zero = jnp.zeros((HEAD_DIM, LANES), F32)
        dfin = lax.fori_loop(0, CHUNK // GROUP, reverse, tuple(ds_scr[j] for j in range(N_PAIR)) + (zero,) * N_PAIR)
        for j in range(N_PAIR):
            ds_scr[j] = dfin[j]
            dv_ref[:, _pair(j)] = _rows_of_columns(dfin[N_PAIR + j])

        @pl.when(i == n_chunks - 1)
        def _():
            ds_out_ref[...] = ds_scr[...]

    blk = pl.BlockSpec((CHUNK, D_RWKV), lambda i: (top - i, 0))
    per_step = pl.BlockSpec((CHUNK,) + STATE, lambda i: (top - i, 0, 0, 0))
    step_before = pl.BlockSpec((1,) + STATE, lambda i: (jnp.maximum((top - i) * CHUNK - 1, 0), 0, 0, 0))
    prev = [] if prev is None else list(prev)
    outs = pl.pallas_call(
        body, name=name, grid=(n_chunks,),
        in_specs=[blk] * 7 + [per_step, step_before, per_step, _const(STATE)] + [ANY] * len(prev),
        out_specs=[blk] * 6 + [_const(STATE)],
        out_shape=[jax.ShapeDtypeStruct((SEQ, D_RWKV), F32)] * 6 + [jax.ShapeDtypeStruct(STATE, F32)],
        scratch_shapes=[pltpu.VMEM(STATE, F32)],
        input_output_aliases={11 + t: t for t in range(len(prev))},
        compiler_params=_cp(("arbitrary",)),
    )(r, w, k, v, kkn, b, do, states, states, sas, ds_in, *prev)
    return outs[:6], outs[6]


def _stacked(rows, cols, pick):
    return pl.BlockSpec((None, rows, cols), pick)


def _local_step(x, target, sm, win_st):
    def tied(t, token):
        return t if token is None else t + token[0:1, 0:1].reshape((1,) * t.ndim)

    zpad = jnp.zeros((LORA_DECAY, D_RWKV), F32)
    prm = [sm["w0"], jnp.concatenate([sm["w_decay_up"], zpad], axis=0), sm["a0"],
           jnp.concatenate([zpad, sm["w_iclr_up"]], axis=0), sm["w_gate_up"], sm["k_k"], sm["k_a"]]
    mix = sm["rwkv_shift_mix"]
    onehot = jnp.asarray(_t5_onehot(), BF16)
    sinks = sm["sinks"].reshape(N_Q_HEADS)
    lng, lnb, rk = sm["ln_x_g"], sm["ln_x_b"], sm["r_k"].reshape(1, D_RWKV)

    h1 = _norm_cast(x, sm["norm_mix_pre"], "norm_in")
    proj = _matmul(h1, win_st, "nn", "proj", m=SEQ, n=D_IN, k=D_MODEL, tm=SEQ, tn=640, tk=D_MODEL,
                   b_spec=_stacked(D_MODEL, 640, lambda i, j, kk: (j, 0, 0)))
    bias = _bias_table(sm["rel_bias"].T, onehot).reshape(N_KV_HEADS, Q_PER_KV * BLOCK, 2 * BLOCK)
    attn = _attn_fwd(proj, bias, sinks)
    r, w, k2, v, kkn, b, g = _rwkv_prep(proj, mix, prm)
    o, states, sas = _scan_fwd(r, w, k2, v, kkn, b)
    wout, wup_st, wdown = yield ("rest_weights", o)
    cat = _rwkv_post(o, r, k2, v, g, lng, lnb, rk, attn)
    mixo = _matmul(cat, wout, "nn", "out_proj", m=SEQ, n=D_MODEL, k=D_MODEL, tm=SEQ, tn=512, tk=D_MODEL)
    x2, h3 = _mix_norm(x, mixo, sm["norm_mix_post"], sm["norm_ffn_pre"])
    u = _matmul(h3, wup_st, "nn", "ffn_up", m=SEQ, n=2 * D_FF, k=D_MODEL, tm=SEQ, tn=512, tk=D_MODEL,
                b_spec=_stacked(D_MODEL, 512, lambda i, j, kk: (j // 4, 0, j % 4)))
    act = _ffn_act(u, sm["conv_w"], sm["conv_b"])
    f = _matmul(act, wdown, "nn", "ffn_down", m=SEQ, n=D_MODEL, k=D_FF, tm=1024, tn=512, tk=2048)
    loss, dy, df, d_g4 = _loss_head(x2, f, sm["norm_ffn_post"], target)

    dact = _matmul(df, wdown, "nt", "d_act", m=SEQ, n=D_FF, k=D_MODEL, tm=SEQ, tn=512, tk=D_MODEL)
    d_wdown = _matmul(act, df, "tn", "d_wdown", m=D_FF, n=D_MODEL, k=SEQ, tm=512, tn=D_MODEL, tk=SEQ)
    du, d_convw, d_convb = _ffn_act_bwd(u, dact, sm["conv_w"], sm["conv_b"])
    d_convw = d_convw.transpose(1, 0, 2).reshape(3, 2 * D_FF)
    d_convb = d_convb.reshape(1, 2 * D_FF)
    dh3 = _matmul(du, wup_st, "nt", "d_h3", m=SEQ, n=D_MODEL, k=2 * D_FF, tm=1024, tn=D_MODEL, tk=2048,
                  a_spec=pl.BlockSpec((None, 1024, 2048), lambda i, j, kk: (kk // 2, i, kk % 2)),
                  b_spec=_stacked(D_MODEL, 2048, lambda i, j, kk: (kk, j, 0)))
    d_wup = _matmul(h3, du, "tn", "d_wup", m=D_MODEL, n=2 * D_FF, k=SEQ, tm=D_MODEL, tn=512, tk=SEQ,
                    b_spec=pl.BlockSpec((None, SEQ, 512), lambda i, j, kk: (j // 8, 0, j % 8)),
                    out=((N_CHIPS, D_MODEL, 2048), _stacked(D_MODEL, 512, lambda i, j, kk: (j // 4, 0, j % 4))))
    dx2, dmix, d_g2, d_g3 = _mid_bwd(x2, mixo, dy, dh3, sm["norm_mix_post"], sm["norm_ffn_pre"])
    dcat = _matmul(dmix, wout, "nt", "d_cat", m=SEQ, n=D_MODEL, k=D_MODEL, tm=SEQ, tn=512, tk=D_MODEL)
    d_wout = _matmul(cat, dmix, "tn", "d_wout", m=D_MODEL, n=D_MODEL, k=SEQ, tm=512, tn=D_MODEL, tk=SEQ)
    token = yield ("grads_a", (d_wdown, d_wup, d_wout))
    do, dr_p, dk_p, dv_p, dg, d_lng, d_lnb, d_rk = _rwkv_post_bwd(o, r, k2, v, g, lng, tied(lnb, token), rk, dcat)
    half = N_CHUNK // 2
    ds_end = jnp.zeros(STATE, F32)
    late, ds_mid = _scan_bwd(r, w, k2, v, kkn, b, do, states, sas, ds_end, None, "rwkv_scan_bwd_late", half, half)
    token = yield ("seam_1", ds_mid)
    scan_cts, ds_first = _scan_bwd(r, w, k2, v, kkn, b, do, states, sas, tied(ds_mid, token), late,
                                   "rwkv_scan_bwd_early", 0, half)
    dr_s, dw_s, dk_s, dv_s, dkkn_s, db_s = scan_cts
    token = yield ("seam_2", ds_first)
    prep_grads = _rwkv_prep_bwd(proj, tied(mix, token), prm,
                                (dr_s, dr_p, dw_s, dk_s, dk_p, dv_s, dv_p, dkkn_s, db_s, dg))
    dps, d_mix, d_w0, d_wdu, d_a0, d_wiu, d_wgu, d_kk, d_ka = prep_grads
    dq, dkv, dbias, dsink = _attn_bwd(proj, bias, sinks, dcat)
    d_relb = _bias_table_bwd(dbias.reshape(N_Q_HEADS, N_REL), onehot).T
    dproj = _assemble_dproj(dq, dkv, dps, mix)
    d_win = _matmul(h1, dproj, "tn", "d_win", m=D_MODEL, n=D_IN, k=SEQ, tm=D_MODEL, tn=640, tk=SEQ,
                    out=((N_CHIPS, D_MODEL, 640), _stacked(D_MODEL, 640, lambda i, j, kk: (j, 0, 0))))
    token = yield ("grads_b", d_win)
    dh1 = _matmul(dproj, win_st, "nt", "d_h1", m=SEQ, n=D_MODEL, k=D_IN, tm=1024, tn=D_MODEL, tk=640,
                  b_spec=_stacked(D_MODEL, 640, lambda i, j, kk: (kk, j, 0)))
    grad_x, d_g1 = _first_bwd(x, dx2, dh1, tied(sm["norm_mix_pre"], token))

    grads = {
        "norm_mix_pre": d_g1, "norm_mix_post": d_g2, "norm_ffn_pre": d_g3, "norm_ffn_post": d_g4,
        "w_in": d_win, "rel_bias": d_relb, "sinks": dsink[:, 0].reshape(1, N_Q_HEADS),
        "rwkv_shift_mix": d_mix, "w0": d_w0, "w_decay_up": d_wdu[:LORA_DECAY], "a0": d_a0,
        "w_iclr_up": d_wiu[LORA_DECAY:], "w_gate_up": d_wgu, "k_k": d_kk, "k_a": d_ka,
        "r_k": d_rk.reshape(1, N_Q_HEADS, HEAD_DIM), "ln_x_g": d_lng, "ln_x_b": d_lnb,
        "w_out": d_wout, "w_ffn_up": d_wup, "conv_w": d_convw, "conv_b": d_convb, "w_ffn_down": d_wdown,
    }
    return loss, grad_x, grads


def _place():
    x, y, c = lax.axis_index("x"), lax.axis_index("y"), lax.axis_index("c")
    chips = [(1 - x, y), (x, 1 - y), (1 - x, 1 - y)]
    return x, y, c, chips


def _remote(src, dst, sems, idx, to):
    return pltpu.make_async_remote_copy(src_ref=src, dst_ref=dst, send_sem=sems[0].at[idx], recv_sem=sems[1].at[idx],
                                        device_id=to, device_id_type=MESH)


def _half(c, rows):
    return pl.ds(pl.multiple_of(c * (rows // 2), 16), rows // 2)


def _gather_weights(big, small):
    nb, ns = len(big), len(small)

    def body(*refs):
        ins, outs = refs[:nb + ns], refs[nb + ns:2 * (nb + ns)]
        ici, d2d, sml, loc = refs[2 * (nb + ns):2 * (nb + ns) + 2], refs[-5:-3], refs[-3:-1], refs[-1]
        x, y, c, chips = _place()
        me = 2 * x + y
        sib = (x, y, 1 - c)
        local = [pltpu.make_async_copy(ins[a], outs[a].at[me], loc.at[a]) for a in range(nb + ns)]
        for cp in local:
            cp.start()
        sends = []
        for a in range(nb):
            rows = _half(c, big[a].shape[0])
            for kk, chip in enumerate(chips):
                sends.append(_remote(ins[a].at[rows], outs[a].at[me, rows], ici, a * 3 + kk, (*chip, c)))
        for a in range(ns):
            for kk, chip in enumerate(chips):
                sends.append(_remote(ins[nb + a], outs[nb + a].at[me], sml, a * 3 + kk, (*chip, c)))
        for cp in sends:
            cp.start()
        passed = []
        for a in range(nb):
            rows = _half(c, big[a].shape[0])
            for kk, (px, py) in enumerate(chips):
                got = outs[a].at[2 * px + py, rows]
                _remote(got, got, ici, a * 3 + kk, sib).wait_recv()
                fwd = _remote(got, got, d2d, a * 3 + kk, sib)
                fwd.start()
                passed.append(fwd)
        for a in range(nb):
            other = _half(1 - c, big[a].shape[0])
            for kk, (px, py) in enumerate(chips):
                land = outs[a].at[2 * px + py, other]
                _remote(land, land, d2d, a * 3 + kk, sib).wait_recv()
        for a in range(ns):
            for kk, (px, py) in enumerate(chips):
                land = outs[nb + a].at[2 * px + py]
                _remote(land, land, sml, a * 3 + kk, sib).wait_recv()
        for cp in sends + passed:
            cp.wait_send()
        for cp in local:
            cp.wait()

    arrs = list(big) + list(small)
    return pl.pallas_call(
        body, name="gather_weights",
        in_specs=[ANY] * len(arrs), out_specs=[ANY] * len(arrs),
        out_shape=[jax.ShapeDtypeStruct((N_CHIPS,) + t.shape, t.dtype) for t in arrs],
        scratch_shapes=[pltpu.SemaphoreType.DMA((3 * nb,)), pltpu.SemaphoreType.DMA((3 * nb,)),
                        pltpu.SemaphoreType.DMA((3 * nb,)), pltpu.SemaphoreType.DMA((3 * nb,)),
                        pltpu.SemaphoreType.DMA((3 * ns,)), pltpu.SemaphoreType.DMA((3 * ns,)),
                        pltpu.SemaphoreType.DMA((nb + ns,))],
        compiler_params=pltpu.CompilerParams(has_side_effects=True),
    )(*arrs)


def _allreduce_small(g):
    rows = g.shape[0]

    def body(g_ref, o_ref, buf, send, recv):
        x, y, c, _ = _place()
        me = 4 * x + 2 * y + c
        buf[me] = g_ref[...]
        sends = []
        for rel in range(1, N_DEV):
            px, py, pc = x ^ (rel >> 2), y ^ ((rel >> 1) & 1), c ^ (rel & 1)
            cp = _remote(g_ref, buf.at[me], (send, recv), rel - 1, (px, py, pc))
            cp.start()
            sends.append(cp)
        for rel in range(1, N_DEV):
            px, py, pc = x ^ (rel >> 2), y ^ ((rel >> 1) & 1), c ^ (rel & 1)
            land = buf.at[4 * px + 2 * py + pc]
            _remote(land, land, (send, recv), rel - 1, (px, py, pc)).wait_recv()
        acc = buf[0]
        for d in range(1, N_DEV):
            acc = acc + buf[d]
        o_ref[...] = acc
        for cp in sends:
            cp.wait_send()

    vm = pl.BlockSpec(memory_space=pltpu.VMEM)
    return pl.pallas_call(
        body, name="allreduce_small", in_specs=[vm], out_specs=vm,
        out_shape=jax.ShapeDtypeStruct((rows, LANES), F32),
        scratch_shapes=[pltpu.VMEM((N_DEV, rows, LANES), F32), pltpu.SemaphoreType.DMA((N_DEV - 1,)),
                        pltpu.SemaphoreType.DMA((N_DEV - 1,))],
        compiler_params=_cp(),
    )(g)


def _pair_exchange(gs):
    n = len(gs)

    def body(*refs):
        ins, got, mine, send, recv, loc = refs[:n], refs[n:2 * n], refs[2 * n:3 * n], refs[-3], refs[-2], refs[-1]
        x, y, c, _ = _place()
        sib = (x, y, 1 - c)
        cps, local = [], []
        for a in range(n):
            rows = gs[a].shape[1]
            cp = _remote(ins[a].at[:, _half(1 - c, rows)], got[a], (send, recv), a, sib)
            cp.start()
            cps.append(cp)
            lc = pltpu.make_async_copy(ins[a].at[:, _half(c, rows)], mine[a], loc.at[a])
            lc.start()
            local.append(lc)
        for a in range(n):
            cps[a].wait_recv()
        for a in range(n):
            cps[a].wait_send()
            local[a].wait()

    halves = [jax.ShapeDtypeStruct((N_CHIPS, t.shape[1] // 2, t.shape[2]), F32) for t in gs]
    outs = pl.pallas_call(
        body, name="grad_pair_exchange", in_specs=[ANY] * n, out_specs=[ANY] * (2 * n), out_shape=halves + halves,
        scratch_shapes=[pltpu.SemaphoreType.DMA((n,)), pltpu.SemaphoreType.DMA((n,)), pltpu.SemaphoreType.DMA((n,))],
        compiler_params=pltpu.CompilerParams(has_side_effects=True),
    )(*gs)
    return outs[:n], outs[n:]


def _chip_exchange(ps):
    n = len(ps)

    def body(*refs):
        ins, outs, send, recv, loc = refs[:n], refs[n:2 * n], refs[-3], refs[-2], refs[-1]
        x, y, c, chips = _place()
        me = 2 * x + y
        cps, local = [], []
        for a in range(n):
            lc = pltpu.make_async_copy(ins[a].at[me], outs[a].at[me], loc.at[a])
            lc.start()
            local.append(lc)
            for kk, (px, py) in enumerate(chips):
                cp = _remote(ins[a].at[2 * px + py], outs[a].at[me], (send, recv), a * 3 + kk, (px, py, c))
                cp.start()
                cps.append(cp)
        for a in range(n):
            for kk, (px, py) in enumerate(chips):
                land = outs[a].at[2 * px + py]
                _remote(land, land, (send, recv), a * 3 + kk, (px, py, c)).wait_recv()
        for cp in cps:
            cp.wait_send()
        for lc in local:
            lc.wait()

    return pl.pallas_call(
        body, name="grad_chip_exchange", in_specs=[ANY] * n, out_specs=[ANY] * n,
        out_shape=[jax.ShapeDtypeStruct(t.shape, F32) for t in ps],
        scratch_shapes=[pltpu.SemaphoreType.DMA((3 * n,)), pltpu.SemaphoreType.DMA((3 * n,)),
                        pltpu.SemaphoreType.DMA((n,))],
        compiler_params=pltpu.CompilerParams(has_side_effects=True),
    )(*ps)


def _pair_gather(hs):
    n = len(hs)

    def body(*refs):
        ins, outs, send, recv, loc = refs[:n], refs[n:2 * n], refs[-3], refs[-2], refs[-1]
        x, y, c, _ = _place()
        sib = (x, y, 1 - c)
        cps, local = [], []
        for a in range(n):
            rows = 2 * hs[a].shape[0]
            cp = _remote(ins[a], outs[a].at[_half(c, rows)], (send, recv), a, sib)
            cp.start()
            cps.append(cp)
            lc = pltpu.make_async_copy(ins[a], outs[a].at[_half(c, rows)], loc.at[a])
            lc.start()
            local.append(lc)
        for a in range(n):
            rows = 2 * hs[a].shape[0]
            land = outs[a].at[_half(1 - c, rows)]
            _remote(land, land, (send, recv), a, sib).wait_recv()
        for a in range(n):
            cps[a].wait_send()
            local[a].wait()

    return pl.pallas_call(
        body, name="grad_pair_gather", in_specs=[ANY] * n, out_specs=[ANY] * n,
        out_shape=[jax.ShapeDtypeStruct((2 * t.shape[0], t.shape[1]), F32) for t in hs],
        scratch_shapes=[pltpu.SemaphoreType.DMA((n,)), pltpu.SemaphoreType.DMA((n,)), pltpu.SemaphoreType.DMA((n,))],
        compiler_params=pltpu.CompilerParams(has_side_effects=True),
    )(*hs)


def _add2(a, b, name):
    r, cdim = a.shape
    tr = 256

    def body(a_ref, b_ref, o_ref):
        o_ref[...] = a_ref[...] + b_ref[...]

    return pl.pallas_call(
        body, name=name, grid=(r // tr,), in_specs=[_rows(tr, cdim)] * 2, out_specs=_rows(tr, cdim),
        out_shape=jax.ShapeDtypeStruct((r, cdim), F32), compiler_params=_cp(("parallel",)),
    )(a, b)


def _sum4(t, name):
    _, r, cdim = t.shape
    tr = 128

    def body(t_ref, o_ref):
        o_ref[...] = ((t_ref[0] + t_ref[1]) + t_ref[2]) + t_ref[3]

    return pl.pallas_call(
        body, name=name, grid=(r // tr,), in_specs=[pl.BlockSpec((N_CHIPS, tr, cdim), lambda i: (0, i, 0))],
        out_specs=_rows(tr, cdim), out_shape=jax.ShapeDtypeStruct((r, cdim), F32),
        compiler_params=_cp(("parallel",)),
    )(t)


def _reduce_big(gs):
    got, mine = _pair_exchange(gs)
    ps = [_add2(m.reshape(-1, m.shape[2]), g.reshape(-1, g.shape[2]), f"grad_pair_add_{i}").reshape(m.shape)
          for i, (m, g) in enumerate(zip(mine, got))]
    xs = _chip_exchange(ps)
    hs = [_sum4(t, f"grad_chip_sum_{i}") for i, t in enumerate(xs)]
    return _pair_gather(hs)


HBM = pl.BlockSpec(memory_space=pltpu.HBM)
SEM = pl.BlockSpec(memory_space=pltpu.SEMAPHORE)
EFFECT = pltpu.SideEffectType.DATAFLOW_SIDE_EFFECTING


def _copies_start(name, bufs, plan, n):
    nb = len(bufs)

    def body(*refs):
        ins, sems, token = refs[:nb], refs[nb:nb + 2 * n], refs[-1]
        for kk, (src, dst, dev) in enumerate(plan(ins)):
            pltpu.make_async_remote_copy(src_ref=src, dst_ref=dst, send_sem=sems[2 * kk], recv_sem=sems[2 * kk + 1],
                                         device_id=dev, device_id_type=MESH).start()
        token[...] = jnp.zeros_like(token)

    outs = pl.pallas_call(
        body, name=name,
        out_shape=tuple([pltpu.SemaphoreType.DMA(())] * (2 * n) + [pltpu.HBM(t.shape, t.dtype) for t in bufs]
                        + [jax.ShapeDtypeStruct((8, LANES), F32)]),
        in_specs=[HBM] * nb,
        out_specs=tuple([SEM] * (2 * n) + [HBM] * nb + [pl.BlockSpec(memory_space=pltpu.VMEM)]),
        input_output_aliases={t: 2 * n + t for t in range(nb)},
        compiler_params=pltpu.CompilerParams(has_side_effects=EFFECT),
    )(*[pltpu.with_memory_space_constraint(t, pltpu.HBM) for t in bufs])
    return outs[:2 * n], outs[2 * n:2 * n + nb], outs[-1]


def _copies_wait(name, sems, bufs, plan, n, after):
    nb = len(bufs)

    def body(*refs):
        ins, sem_refs = refs[:nb], refs[nb:nb + 2 * n]
        for kk, (src, dst, dev) in enumerate(plan(ins)):
            cp = pltpu.make_async_remote_copy(src_ref=src, dst_ref=dst, send_sem=sem_refs[2 * kk],
                                              recv_sem=sem_refs[2 * kk + 1], device_id=dev, device_id_type=MESH)
            cp.wait_send()
            cp.wait_recv()

    return pl.pallas_call(
        body, name=name,
        out_shape=tuple(pltpu.HBM(t.shape, t.dtype) for t in bufs),
        in_specs=[HBM] * nb + [SEM] * (2 * n) + [ANY],
        out_specs=tuple([HBM] * nb),
        input_output_aliases={t: t for t in range(nb)},
        compiler_params=pltpu.CompilerParams(has_side_effects=EFFECT),
    )(*bufs, *sems, after)


def _plan_gather(n_w):
    def plan(refs):
        x, y, c, chips = _place()
        me = 2 * x + y
        return [(refs[a], refs[n_w + a].at[me], (*chip, c)) for a in range(n_w) for chip in chips]
    return plan


def _plan_pair_halves(n_g, rows):
    def plan(refs):
        x, y, c, _ = _place()
        return [(refs[a].at[:, _half(1 - c, rows[a])], refs[n_g + a], (x, y, 1 - c)) for a in range(n_g)]
    return plan


def _plan_chip_parts(n_g):
    def plan(refs):
        x, y, c, chips = _place()
        me = 2 * x + y
        return [(refs[a].at[2 * px + py], refs[n_g + a].at[me], (px, py, c))
                for a in range(n_g) for (px, py) in chips]
    return plan


def _plan_pair_fill(n_g, rows):
    def plan(refs):
        x, y, c, _ = _place()
        return [(refs[a].at[_half(c, rows[a])], refs[a].at[_half(c, rows[a])], (x, y, 1 - c)) for a in range(n_g)]
    return plan


def _pair_add(g, got, name):
    _, rows, cols = g.shape
    hr = rows // 2
    tr = min(hr, 256)
    nb = hr // tr

    def body(g_ref, got_ref, p_ref, own_ref):
        val = g_ref[...] + got_ref[...]
        p_ref[...] = val

        @pl.when(pl.program_id(1) == 2 * lax.axis_index("x") + lax.axis_index("y"))
        def _():
            own_ref[...] = val

    def mine(i, s):
        return (2 * lax.axis_index("x") + lax.axis_index("y"), i, 0)

    return pl.pallas_call(
        body, name=name, grid=(nb, N_CHIPS),
        in_specs=[pl.BlockSpec((None, tr, cols), lambda i, s: (s, lax.axis_index("c") * nb + i, 0)),
                  pl.BlockSpec((None, tr, cols), lambda i, s: (s, i, 0))],
        out_specs=[pl.BlockSpec((None, tr, cols), lambda i, s: (s, i, 0)), pl.BlockSpec((None, tr, cols), mine)],
        out_shape=[jax.ShapeDtypeStruct((N_CHIPS, hr, cols), F32)] * 2,
        compiler_params=_cp(("parallel", "arbitrary")),
    )(g, got)


def _chip_sum(parts, name):
    _, hr, cols = parts.shape
    tr = min(hr, 128)
    nb = hr // tr

    def body(t_ref, o_ref):
        o_ref[...] = ((t_ref[0] + t_ref[1]) + t_ref[2]) + t_ref[3]

    return pl.pallas_call(
        body, name=name, grid=(nb,),
        in_specs=[pl.BlockSpec((N_CHIPS, tr, cols), lambda i: (0, i, 0))],
        out_specs=pl.BlockSpec((tr, cols), lambda i: (lax.axis_index("c") * nb + i, 0)),
        out_shape=jax.ShapeDtypeStruct((2 * hr, cols), F32),
        compiler_params=_cp(("parallel",)),
    )(parts)


class _Reduction:
    def __init__(self, tag, rows):
        self.tag, self.n, self.rows = tag, len(rows), rows
        self.plans = (_plan_pair_halves(self.n, rows), _plan_chip_parts(self.n), _plan_pair_fill(self.n, rows))
        self.flight = None

    def _name(self, what):
        return f"grad_{self.tag}_{what}"

    def start(self, gs):
        gots = [lax.empty((N_CHIPS, t.shape[1] // 2, t.shape[2]), F32) for t in gs]
        self.flight = _copies_start(self._name("pair_start"), list(gs) + gots, self.plans[0], self.n)
        return self.flight[2]

    def after_pair(self, after):
        sems, bufs, _ = self.flight
        out = _copies_wait(self._name("pair_wait"), sems, bufs, self.plans[0], self.n, after)
        sums = [_pair_add(g, got, self._name(f"pair_add_{i}"))
                for i, (g, got) in enumerate(zip(out[:self.n], out[self.n:]))]
        self.flight = _copies_start(self._name("chip_start"), [p for p, _ in sums] + [own for _, own in sums],
                                    self.plans[1], 3 * self.n)
        return self.flight[2]

    def after_chips(self, after):
        sems, bufs, _ = self.flight
        out = _copies_wait(self._name("chip_wait"), sems, bufs, self.plans[1], 3 * self.n, after)
        fulls = [_chip_sum(t, self._name(f"chip_sum_{i}")) for i, t in enumerate(out[self.n:])]
        self.flight = _copies_start(self._name("fill_start"), fulls, self.plans[2], self.n)
        return self.flight[2]

    def finish(self, after):
        sems, bufs, _ = self.flight
        return _copies_wait(self._name("fill_wait"), sems, bufs, self.plans[2], self.n, after)


def _adamw(w, g, m, v, name, tr):
    r, cdim = w.shape

    def body(w_ref, g_ref, m_ref, v_ref, d_ref, nm_ref, nv_ref):
        g = g_ref[...]
        nm = ADAM_B1 * m_ref[...] + (1.0 - ADAM_B1) * g
        nv = ADAM_B2 * v_ref[...] + (1.0 - ADAM_B2) * (g * g)
        m_hat = nm / (1.0 - ADAM_B1 ** ADAM_STEP)
        v_hat = nv / (1.0 - ADAM_B2 ** ADAM_STEP)
        d_ref[...] = -ADAM_LR * (m_hat / (jnp.sqrt(v_hat) + ADAM_EPS) + ADAM_WD * w_ref[...])
        nm_ref[...] = nm
        nv_ref[...] = nv

    return pl.pallas_call(
        body, name=name, grid=(r // tr,), in_specs=[_rows(tr, cdim)] * 4, out_specs=[_rows(tr, cdim)] * 3,
        out_shape=[jax.ShapeDtypeStruct((r, cdim), F32)] * 3, compiler_params=_cp(("parallel",)),
    )(w, g, m, v)


REPLICATED = (("norm_mix_pre", 1024), ("norm_mix_post", 1024), ("norm_ffn_pre", 1024), ("norm_ffn_post", 1024),
              ("rel_bias", 256), ("sinks", 8), ("rwkv_shift_mix", 1792), ("w0", 512), ("a0", 512), ("k_k", 512),
              ("k_a", 512), ("r_k", 512), ("ln_x_g", 512), ("ln_x_b", 512), ("conv_b", 8192))
SMALL_SHARDED = (("w_decay_up", LORA_DECAY, D_RWKV), ("w_iclr_up", LORA_ICLR, D_RWKV),
                 ("w_gate_up", LORA_GATE, D_RWKV), ("conv_w", 3, 2 * D_FF))
BIG = (("w_in", D_MODEL, 640), ("w_out", 256, D_MODEL), ("w_ffn_up", D_MODEL, 2048), ("w_ffn_down", 1024, D_MODEL))
PACK_ALIGN = 8 * LANES


def _pack(pieces):
    flat = []
    for t in pieces:
        t = t.reshape(-1)
        pad = (-t.shape[0]) % LANES
        flat.append(jnp.pad(t, (0, pad)) if pad else t)
    flat = jnp.concatenate(flat)
    pad = (-flat.shape[0]) % PACK_ALIGN
    return jnp.pad(flat, (0, pad)).reshape(-1, LANES)


def _unpack(buf, sizes):
    flat, out, off = buf.reshape(-1), [], 0
    for n in sizes:
        out.append(flat[off:off + n])
        off += n + ((-n) % LANES)
    return out


def kernel(x, norm_mix_pre, norm_mix_post, norm_ffn_pre, norm_ffn_post, w_in, rel_bias, sinks, rwkv_shift_mix, w0, w_decay_up, a0, w_iclr_up, w_gate_up, k_k, k_a, r_k, ln_x_g, ln_x_b, w_out, w_ffn_up, conv_w, conv_b, w_ffn_down, loss_target, m_norm_mix_pre, m_norm_mix_post, m_norm_ffn_pre, m_norm_ffn_post, m_w_in, m_rel_bias, m_sinks, m_rwkv_shift_mix, m_w0, m_w_decay_up, m_a0, m_w_iclr_up, m_w_gate_up, m_k_k, m_k_a, m_r_k, m_ln_x_g, m_ln_x_b, m_w_out, m_w_ffn_up, m_conv_w, m_conv_b, m_w_ffn_down, v_norm_mix_pre, v_norm_mix_post, v_norm_ffn_pre, v_norm_ffn_post, v_w_in, v_rel_bias, v_sinks, v_rwkv_shift_mix, v_w0, v_w_decay_up, v_a0, v_w_iclr_up, v_w_gate_up, v_k_k, v_k_a, v_r_k, v_ln_x_g, v_ln_x_b, v_w_out, v_w_ffn_up, v_conv_w, v_conv_b, v_w_ffn_down):
    given = dict(locals())
    names = [n for n, _ in REPLICATED] + [n for n, _, _ in SMALL_SHARDED] + [n for n, _, _ in BIG]
    order = ["norm_mix_pre", "norm_mix_post", "norm_ffn_pre", "norm_ffn_post", "w_in", "rel_bias", "sinks",
             "rwkv_shift_mix", "w0", "w_decay_up", "a0", "w_iclr_up", "w_gate_up", "k_k", "k_a", "r_k", "ln_x_g",
             "ln_x_b", "w_out", "w_ffn_up", "conv_w", "conv_b", "w_ffn_down"]
    assert sorted(names) == sorted(order)
    shard = 2 * lax.axis_index("x") + lax.axis_index("y")

    big_sh = {n: given[n].reshape(a, b).astype(BF16) for n, a, b in BIG}
    small_sh = [given[n].reshape(r, c // N_CHIPS) for n, r, c in SMALL_SHARDED]
    gathered = _gather_weights([big_sh["w_in"]], small_sh)
    rest = ("w_out", "w_ffn_up", "w_ffn_down")
    win_st, rest_sh = lax.optimization_barrier((gathered[0], [big_sh[n] for n in rest]))
    sm = {n: given[n] for n, _ in REPLICATED}
    sm["r_k"] = r_k.reshape(N_Q_HEADS, HEAD_DIM)
    for (n, r, c), st in zip(SMALL_SHARDED, gathered[1:]):
        sm[n] = st.transpose(1, 0, 2).reshape(r, c)

    lands = [lax.dynamic_update_slice(lax.empty((N_CHIPS,) + t.shape, BF16), t[None], (shard, 0, 0)) for t in rest_sh]
    plan_w = _plan_gather(len(rest))
    w_sems, w_bufs, token = _copies_start("gather_rest_start", rest_sh + lands, plan_w, 9)
    sm["norm_mix_pre"] = norm_mix_pre + token[0:1, 0:1]

    def on_rest_weights(after):
        out = _copies_wait("gather_rest_wait", w_sems, w_bufs, plan_w, 9, after)
        wout_st, wup_st, wdown_st = out[3:]
        return wout_st.reshape(D_MODEL, D_MODEL), wup_st, wdown_st.reshape(D_FF, D_MODEL)

    red_a = _Reduction("a", (1024, D_MODEL, 256))
    red_b = _Reduction("b", (D_MODEL,))

    def on_grads_a(gs):
        d_wdown, d_wup, d_wout = gs
        return red_a.start([d_wdown.reshape(N_CHIPS, 1024, D_MODEL), d_wup, d_wout.reshape(N_CHIPS, 256, D_MODEL)])

    handlers = {"rest_weights": on_rest_weights, "grads_a": on_grads_a, "seam_1": red_a.after_pair,
                "seam_2": red_a.after_chips, "grads_b": lambda g: red_b.start([g])}
    steps = _local_step(x[0], loss_target[0], sm, win_st)
    kind, payload = next(steps)
    while True:
        try:
            kind, payload = steps.send(handlers[kind](payload))
        except StopIteration as done:
            loss, grad_x, grads = done.value
            break
    loss = lax.psum(loss[0, 0], ("x", "y", "c"))

    rep_sizes = [s for _, s in REPLICATED] + [r * c for _, r, c in SMALL_SHARDED]
    small_sum = _allreduce_small(_pack([grads[n] for n, _ in REPLICATED] + [grads[n] for n, _, _ in SMALL_SHARDED]))
    small_g = _unpack(small_sum, rep_sizes)
    g_out = {n: t.reshape(given[n].shape) for (n, _), t in zip(REPLICATED, small_g)}
    for (n, r, c), t in zip(SMALL_SHARDED, small_g[len(REPLICATED):]):
        g_out[n] = lax.dynamic_slice_in_dim(t.reshape(r, c), shard * (c // N_CHIPS), c // N_CHIPS, axis=1)
    red_b.after_pair(small_sum)
    g_out["w_ffn_down"], g_out["w_ffn_up"], g_out["w_out"] = red_a.finish(small_sum)

    small_names = [n for n, _ in REPLICATED] + [n for n, _, _ in SMALL_SHARDED]
    packs = [_pack([src[n] for n in small_names]) for src in
             ({n: given[n] for n in small_names}, g_out, {n: given["m_" + n] for n in small_names},
              {n: given["v_" + n] for n in small_names})]
    small_sizes = [int(np.prod(given[n].shape)) for n in small_names]
    upd = [_unpack(t, small_sizes) for t in _adamw(*packs, "adamw_small", packs[0].shape[0])]
    delta, new_m, new_v = ({n: t.reshape(given[n].shape) for n, t in zip(small_names, u)} for u in upd)
    for n, a, b in reversed(BIG):
        if n == "w_out":
            red_b.after_chips(delta["w_ffn_up"])
        if n == "w_in":
            g_out[n], = red_b.finish(delta["w_out"])
        d, nm, nv = _adamw(given[n].reshape(a, b), g_out[n], given["m_" + n].reshape(a, b),
                           given["v_" + n].reshape(a, b), "adamw_" + n, 128)
        delta[n], new_m[n], new_v[n] = d, nm, nv

    def shaped(d):
        return [d[n].reshape(given[n].shape) for n in order]

    return (loss, grad_x.reshape(x.shape), *shaped(g_out), *shaped(delta), *shaped(new_m), *shaped(new_v))
```

```python
import functools
import math

import numpy as np
import jax
import jax.numpy as jnp
from jax import lax
from jax.experimental import pallas as pl
from jax.experimental.pallas import tpu as pltpu

F32 = jnp.float32
BF16 = jnp.bfloat16
MESH = pl.DeviceIdType.MESH

SEQ = 2048
D_MODEL = 1024
HEAD_DIM = 64
D_ATTN = 512
D_RWKV = 512
D_KV = 128
N_Q_HEADS = 8
N_KV_HEADS = 2
Q_PER_KV = 4
BLOCK = 128
N_BUCKETS = 32
MAX_DISTANCE = 128
LORA_DECAY = 64
LORA_ICLR = 64
LORA_GATE = 128
RWKV_COLS = 3 * D_RWKV + LORA_DECAY + LORA_ICLR + LORA_GATE
P_OFF = D_ATTN + 2 * D_KV
D_IN = P_OFF + RWKV_COLS
D_FF = 4096
NORM_EPS = 1e-6
GN_EPS = 64e-5
NEG_INF = -1e30
N_CHIPS = 4
N_DEV = 8

ADAM_LR = 0.001
ADAM_B1 = 0.9
ADAM_B2 = 0.999
ADAM_EPS = 1e-08
ADAM_WD = 0.01
ADAM_STEP = 10

VMEM_LIMIT = 52 * 1024 * 1024
LANES = 128


def _cp(sem=None, vmem=VMEM_LIMIT):
    kw = dict(vmem_limit_bytes=vmem)
    if sem is not None:
        kw["dimension_semantics"] = sem
    return pltpu.CompilerParams(**kw)


def _rows(tr, nc):
    return pl.BlockSpec((tr, nc), lambda i: (i, 0))


def _const(shape):
    return pl.BlockSpec(shape, lambda *_: (0,) * len(shape))


ANY = pl.BlockSpec(memory_space=pl.ANY)


def _split(x, n):
    parts = []
    for _ in range(n - 1):
        h = x.astype(BF16)
        parts.append(h)
        x = x - h.astype(F32)
    parts.append(x.astype(BF16))
    return parts


def _dot(a, b, dn=(((1,), (0,)), ((), ()))):
    return lax.dot_general(a, b, dn, preferred_element_type=F32)


NN = (((1,), (0,)), ((), ()))
NT = (((1,), (1,)), ((), ()))
TN = (((0,), (0,)), ((), ()))


def _dot_ind(x, ind_bf16, n=3):
    acc = None
    for part in _split(x, n):
        t = _dot(part, ind_bf16)
        acc = t if acc is None else acc + t
    return acc


def _head_ones(n, scale=1.0):
    r = lax.broadcasted_iota(jnp.int32, (n, n), 0) >> 6
    c = lax.broadcasted_iota(jnp.int32, (n, n), 1) >> 6
    return jnp.where(r == c, 1.0, 0.0).astype(BF16)


def _matmul(a, b, mode, name, *, m, n, k, tm, tn, tk, a_spec=None, b_spec=None, out=None, out_dtype=F32):
    nk = k // tk
    dn = {"nn": NN, "nt": NT, "tn": TN}[mode]

    def body(a_ref, b_ref, o_ref, *scratch):
        part = _dot(a_ref[...], b_ref[...], dn)
        if nk == 1:
            o_ref[...] = part.astype(out_dtype)
        else:
            acc_ref, = scratch
            kk = pl.program_id(2)

            @pl.when(kk == 0)
            def _():
                acc_ref[...] = part

            @pl.when(kk > 0)
            def _():
                acc_ref[...] += part

            @pl.when(kk == nk - 1)
            def _():
                o_ref[...] = acc_ref[...].astype(out_dtype)

    if a_spec is None:
        a_spec = (pl.BlockSpec((tk, tm), lambda i, j, kk: (kk, i)) if mode == "tn"
                  else pl.BlockSpec((tm, tk), lambda i, j, kk: (i, kk)))
    if b_spec is None:
        b_spec = (pl.BlockSpec((tn, tk), lambda i, j, kk: (j, kk)) if mode == "nt"
                  else pl.BlockSpec((tk, tn), lambda i, j, kk: (kk, j)))
    return pl.pallas_call(
        body, name=name, grid=(m // tm, n // tn, nk),
        in_specs=[a_spec, b_spec],
        out_specs=pl.BlockSpec((tm, tn), lambda i, j, kk: (i, j)) if out is None else out[1],
        out_shape=jax.ShapeDtypeStruct((m, n) if out is None else out[0], out_dtype),
        scratch_shapes=[] if nk == 1 else [pltpu.VMEM((tm, tn), F32)],
        compiler_params=_cp(("parallel", "parallel", "arbitrary")),
    )(a, b)


def _rstd(x):
    return lax.rsqrt(jnp.mean(x * x, axis=-1, keepdims=True) + NORM_EPS)


def _rms_bwd(x, r, g, dy):
    gy = dy * g
    return r * gy - x * ((r * r * r) * (jnp.sum(x * gy, axis=-1, keepdims=True) / x.shape[-1]))


TR = 256


def _norm_cast(x, g, name):
    def body(x_ref, g_ref, h_ref):
        x = x_ref[...]
        h_ref[...] = (x * _rstd(x) * g_ref[...]).astype(BF16)

    return pl.pallas_call(
        body, name=name, grid=(SEQ // TR,),
        in_specs=[_rows(TR, D_MODEL), _const((1, D_MODEL))],
        out_specs=_rows(TR, D_MODEL),
        out_shape=jax.ShapeDtypeStruct((SEQ, D_MODEL), BF16),
        compiler_params=_cp(("parallel",)),
    )(x, g)


def _mix_norm(x, mix, g2, g3):
    def body(x_ref, mix_ref, g2_ref, g3_ref, x2_ref, h3_ref):
        mixv = mix_ref[...]
        x2 = x_ref[...] + mixv * _rstd(mixv) * g2_ref[...]
        x2_ref[...] = x2
        h3_ref[...] = (x2 * _rstd(x2) * g3_ref[...]).astype(BF16)

    return pl.pallas_call(
        body, name="mix_norm", grid=(SEQ // TR,),
        in_specs=[_rows(TR, D_MODEL), _rows(TR, D_MODEL), _const((1, D_MODEL)), _const((1, D_MODEL))],
        out_specs=[_rows(TR, D_MODEL), _rows(TR, D_MODEL)],
        out_shape=[jax.ShapeDtypeStruct((SEQ, D_MODEL), F32), jax.ShapeDtypeStruct((SEQ, D_MODEL), BF16)],
        compiler_params=_cp(("parallel",)),
    )(x, mix, g2, g3)


def _loss_head(x2, f, g4, target):
    def body(x2_ref, f_ref, g4_ref, t_ref, loss_ref, dy_ref, df_ref, dg_ref):
        i = pl.program_id(0)
        f = f_ref[...]
        g4 = g4_ref[...]
        r = _rstd(f)
        e = x2_ref[...] + f * r * g4 - t_ref[...]
        dy = e * (1.0 / D_MODEL)
        dy_ref[...] = dy
        df_ref[...] = _rms_bwd(f, r, g4, dy).astype(BF16)
        part = 0.5 * jnp.sum(jnp.sum(e * e, axis=-1, keepdims=True), axis=0, keepdims=True) * (1.0 / D_MODEL)
        dg = jnp.sum(dy * f * r, axis=0, keepdims=True)

        @pl.when(i == 0)
        def _():
            loss_ref[...] = jnp.zeros_like(loss_ref)
            dg_ref[...] = jnp.zeros_like(dg_ref)

        loss_ref[...] += jnp.broadcast_to(part, loss_ref.shape)
        dg_ref[...] += dg

    return pl.pallas_call(
        body, name="loss_head", grid=(SEQ // TR,),
        in_specs=[_rows(TR, D_MODEL), _rows(TR, D_MODEL), _const((1, D_MODEL)), _rows(TR, D_MODEL)],
        out_specs=[_const((8, LANES)), _rows(TR, D_MODEL), _rows(TR, D_MODEL), _const((1, D_MODEL))],
        out_shape=[jax.ShapeDtypeStruct((8, LANES), F32), jax.ShapeDtypeStruct((SEQ, D_MODEL), F32),
                   jax.ShapeDtypeStruct((SEQ, D_MODEL), BF16), jax.ShapeDtypeStruct((1, D_MODEL), F32)],
        compiler_params=_cp(("arbitrary",)),
    )(x2, f, g4, target)


def _mid_bwd(x2, mix, dy, dh3, g2, g3):
    def body(x2_ref, mix_ref, dy_ref, dh3_ref, g2_ref, g3_ref, dx2_ref, dmix_ref, dg2_ref, dg3_ref):
        i = pl.program_id(0)
        x2 = x2_ref[...]
        mixv = mix_ref[...]
        dh3 = dh3_ref[...]
        r3 = _rstd(x2)
        dx2 = dy_ref[...] + _rms_bwd(x2, r3, g3_ref[...], dh3)
        dx2_ref[...] = dx2
        r2 = _rstd(mixv)
        dmix_ref[...] = _rms_bwd(mixv, r2, g2_ref[...], dx2).astype(BF16)

        @pl.when(i == 0)
        def _():
            dg2_ref[...] = jnp.zeros_like(dg2_ref)
            dg3_ref[...] = jnp.zeros_like(dg3_ref)

        dg3_ref[...] += jnp.sum(dh3 * x2 * r3, axis=0, keepdims=True)
        dg2_ref[...] += jnp.sum(dx2 * mixv * r2, axis=0, keepdims=True)

    return pl.pallas_call(
        body, name="mid_bwd", grid=(SEQ // TR,),
        in_specs=[_rows(TR, D_MODEL)] * 4 + [_const((1, D_MODEL))] * 2,
        out_specs=[_rows(TR, D_MODEL), _rows(TR, D_MODEL), _const((1, D_MODEL)), _const((1, D_MODEL))],
        out_shape=[jax.ShapeDtypeStruct((SEQ, D_MODEL), F32), jax.ShapeDtypeStruct((SEQ, D_MODEL), BF16),
                   jax.ShapeDtypeStruct((1, D_MODEL), F32), jax.ShapeDtypeStruct((1, D_MODEL), F32)],
        compiler_params=_cp(("arbitrary",)),
    )(x2, mix, dy, dh3, g2, g3)


def _first_bwd(x, dx2, dh1, g1):
    def body(x_ref, dx2_ref, dh1_ref, g1_ref, dx_ref, dg1_ref):
        i = pl.program_id(0)
        x = x_ref[...]
        dh1 = dh1_ref[...]
        r = _rstd(x)
        dx_ref[...] = dx2_ref[...] + _rms_bwd(x, r, g1_ref[...], dh1)

        @pl.when(i == 0)
        def _():
            dg1_ref[...] = jnp.zeros_like(dg1_ref)

        dg1_ref[...] += jnp.sum(dh1 * x * r, axis=0, keepdims=True)

    return pl.pallas_call(
        body, name="first_bwd", grid=(SEQ // TR,),
        in_specs=[_rows(TR, D_MODEL)] * 3 + [_const((1, D_MODEL))],
        out_specs=[_rows(TR, D_MODEL), _const((1, D_MODEL))],
        out_shape=[jax.ShapeDtypeStruct((SEQ, D_MODEL), F32), jax.ShapeDtypeStruct((1, D_MODEL), F32)],
        compiler_params=_cp(("arbitrary",)),
    )(x, dx2, dh1, g1)


TC = 256
N_CB = D_FF // TC
GELU_C = math.sqrt(2.0 / math.pi)


def _shift_down(u, s):
    rolled = pltpu.roll(u, s, 0)
    row = lax.broadcasted_iota(jnp.int32, u.shape, 0)
    return jnp.where(row >= s, rolled, 0.0)


def _shift_up(u, s):
    n = u.shape[0]
    rolled = pltpu.roll(u, n - s, 0)
    row = lax.broadcasted_iota(jnp.int32, u.shape, 0)
    return jnp.where(row < n - s, rolled, 0.0)


def _conv3(u, w, b):
    return b + w[0:1] * _shift_down(u, 2) + w[1:2] * _shift_down(u, 1) + w[2:3] * u


def _gelu_and_grad(x):
    inner = GELU_C * (x + 0.044715 * (x * x * x))
    t = jnp.tanh(inner)
    gelu = 0.5 * x * (1.0 + t)
    dgelu = 0.5 * (1.0 + t) + 0.5 * x * (1.0 - t * t) * (GELU_C * (1.0 + 3 * 0.044715 * (x * x)))
    return gelu, dgelu


def _ffn_specs():
    col = lambda off: pl.BlockSpec((SEQ, TC), lambda *g: (0, g[-1] + off))
    w = lambda off: pl.BlockSpec((3, TC), lambda *g: (0, g[-1] + off))
    b = lambda off: pl.BlockSpec((1, TC), lambda *g: (0, g[-1] + off))
    return col, w, b


def _ffn_act(u, conv_w, conv_b):
    col, w, b = _ffn_specs()

    def body(ug_ref, uv_ref, wg_ref, wv_ref, bg_ref, bv_ref, act_ref):
        gate = _conv3(ug_ref[...], wg_ref[...], bg_ref[...])
        val = _conv3(uv_ref[...], wv_ref[...], bv_ref[...])
        act_ref[...] = (_gelu_and_grad(gate)[0] * val).astype(BF16)

    return pl.pallas_call(
        body, name="ffn_act", grid=(N_CB,),
        in_specs=[col(0), col(N_CB), w(0), w(N_CB), b(0), b(N_CB)],
        out_specs=col(0),
        out_shape=jax.ShapeDtypeStruct((SEQ, D_FF), BF16),
        compiler_params=_cp(("parallel",)),
    )(u, u, conv_w, conv_w, conv_b, conv_b)


def _ffn_act_bwd(u, dact, conv_w, conv_b):
    col, w, b = _ffn_specs()
    both = lambda rows: pl.BlockSpec((2, rows, TC), lambda j: (0, 0, j))

    def body(ug_ref, uv_ref, da_ref, wg_ref, wv_ref, bg_ref, bv_ref, du_ref, dw_ref, db_ref):
        ug, uv = ug_ref[...], uv_ref[...]
        wg, wv = wg_ref[...], wv_ref[...]
        gate = _conv3(ug, wg, bg_ref[...])
        val = _conv3(uv, wv, bv_ref[...])
        gelu, dgelu = _gelu_and_grad(gate)
        da = da_ref[...]
        for h, (duc, uh, wh) in enumerate(((da * val * dgelu, ug, wg), (da * gelu, uv, wv))):
            du = wh[2:3] * duc + wh[1:2] * _shift_up(duc, 1) + wh[0:1] * _shift_up(duc, 2)
            du_ref[h] = du.astype(BF16)
            db_ref[h] = jnp.sum(duc, axis=0, keepdims=True)
            dw_ref[h] = jnp.concatenate(
                [jnp.sum(duc * _shift_down(uh, 2), axis=0, keepdims=True),
                 jnp.sum(duc * _shift_down(uh, 1), axis=0, keepdims=True),
                 jnp.sum(duc * uh, axis=0, keepdims=True)], axis=0)

    return pl.pallas_call(
        body, name="ffn_act_bwd", grid=(N_CB,),
        in_specs=[col(0), col(N_CB), col(0), w(0), w(N_CB), b(0), b(N_CB)],
        out_specs=[both(SEQ), both(3), both(1)],
        out_shape=[jax.ShapeDtypeStruct((2, SEQ, D_FF), BF16), jax.ShapeDtypeStruct((2, 3, D_FF), F32),
                   jax.ShapeDtypeStruct((2, 1, D_FF), F32)],
        compiler_params=_cp(("parallel",)),
    )(u, u, dact, conv_w, conv_w, conv_b, conv_b)


def _t5_onehot():
    rel = (np.arange(BLOCK)[:, None] + BLOCK) - np.arange(2 * BLOCK)[None, :]
    n = np.maximum(rel, 0)
    max_exact = N_BUCKETS // 2
    large = max_exact + (np.log(np.maximum(n, 1).astype(np.float32) / np.float32(max_exact))
                         / np.float32(math.log(MAX_DISTANCE / max_exact))
                         * np.float32(N_BUCKETS - max_exact)).astype(np.int32)
    large = np.minimum(large, N_BUCKETS - 1)
    bucket = np.where(n < max_exact, n, large).reshape(-1)
    return (bucket[None, :] == np.arange(N_BUCKETS)[:, None]).astype(np.float32)


N_REL = BLOCK * 2 * BLOCK


def _bias_table(rel_bias_t, onehot):
    def body(rb_ref, oh_ref, o_ref):
        o_ref[...] = _dot_ind(rb_ref[...], oh_ref[...])

    return pl.pallas_call(
        body, name="bias_table", grid=(1,),
        in_specs=[_const((N_Q_HEADS, N_BUCKETS)), _const((N_BUCKETS, N_REL))],
        out_specs=_const((N_Q_HEADS, N_REL)),
        out_shape=jax.ShapeDtypeStruct((N_Q_HEADS, N_REL), F32),
        compiler_params=_cp(("arbitrary",)),
    )(rel_bias_t, onehot)


def _bias_table_bwd(dbias, onehot):
    def body(db_ref, oh_ref, o_ref):
        acc = None
        for part in _split(db_ref[...], 3):
            t = _dot(part, oh_ref[...], NT)
            acc = t if acc is None else acc + t
        o_ref[...] = acc

    return pl.pallas_call(
        body, name="bias_table_bwd", grid=(1,),
        in_specs=[_const((N_Q_HEADS, N_REL)), _const((N_BUCKETS, N_REL))],
        out_specs=_const((N_Q_HEADS, N_BUCKETS)),
        out_shape=jax.ShapeDtypeStruct((N_Q_HEADS, N_BUCKETS), F32),
        compiler_params=_cp(("arbitrary",)),
    )(dbias, onehot)


def _attn_pieces(n, q, kvp, kvc, bias_ref, sinks_ref, hk):
    qi = lax.broadcasted_iota(jnp.int32, (BLOCK, 2 * BLOCK), 0)
    kj = lax.broadcasted_iota(jnp.int32, (BLOCK, 2 * BLOCK), 1)
    rel = qi + BLOCK - kj
    first_key = jnp.where(n > 0, 0, BLOCK)
    ok = jnp.where(rel >= 0, jnp.where(rel < BLOCK, jnp.where(kj >= first_key, 1.0, 0.0), 0.0), 0.0)
    ok4 = jnp.concatenate([ok] * Q_PER_KV, axis=0) > 0.5
    c0 = hk * HEAD_DIM
    kcat = jnp.concatenate([kvp[:, c0:c0 + HEAD_DIM], kvc[:, c0:c0 + HEAD_DIM]], axis=0).astype(BF16)
    vcat = jnp.concatenate([kvp[:, D_KV + c0:D_KV + c0 + HEAD_DIM], kvc[:, D_KV + c0:D_KV + c0 + HEAD_DIM]],
                           axis=0).astype(BF16)
    q0 = hk * Q_PER_KV * HEAD_DIM
    qs = jnp.concatenate([q[:, q0 + g * HEAD_DIM:q0 + (g + 1) * HEAD_DIM] for g in range(Q_PER_KV)],
                         axis=0).astype(BF16)
    s = _dot(qs, kcat, NT) * (HEAD_DIM ** -0.5) + bias_ref[hk]
    s = jnp.where(ok4, s, NEG_INF)
    row = lax.broadcasted_iota(jnp.int32, (Q_PER_KV * BLOCK, 1), 0)
    sink = jnp.zeros((Q_PER_KV * BLOCK, 1), F32)
    for g in range(Q_PER_KV):
        sink = jnp.where((row >> 7) == g, sinks_ref[hk * Q_PER_KV + g], sink)
    m = jnp.maximum(jnp.max(s, axis=-1, keepdims=True), sink)
    p = jnp.exp(s - m)
    es = jnp.exp(sink - m)
    inv = 1.0 / (jnp.sum(p, axis=-1, keepdims=True) + es)
    return qs, kcat, vcat, p * inv, es * inv


def _attn_in_specs():
    return [pl.BlockSpec((BLOCK, D_ATTN), lambda n: (n, 0)),
            pl.BlockSpec((BLOCK, 2 * D_KV), lambda n: (jnp.maximum(n - 1, 0), D_ATTN // (2 * D_KV))),
            pl.BlockSpec((BLOCK, 2 * D_KV), lambda n: (n, D_ATTN // (2 * D_KV))),
            _const((N_KV_HEADS, Q_PER_KV * BLOCK, 2 * BLOCK)),
            pl.BlockSpec(memory_space=pltpu.SMEM)]


def _unstack_heads(t):
    return jnp.concatenate([t[g * BLOCK:(g + 1) * BLOCK] for g in range(Q_PER_KV)], axis=1)


def _attn_fwd(proj, bias, sinks):
    def body(q_ref, kvp_ref, kvc_ref, bias_ref, sinks_ref, o_ref):
        n = pl.program_id(0)
        q, kvp, kvc = q_ref[...], kvp_ref[...], kvc_ref[...]
        outs = []
        for hk in range(N_KV_HEADS):
            _, _, vcat, probs, _ = _attn_pieces(n, q, kvp, kvc, bias_ref, sinks_ref, hk)
            outs.append(_unstack_heads(_dot(probs.astype(BF16), vcat)))
        o_ref[...] = jnp.concatenate(outs, axis=1)

    return pl.pallas_call(
        body, name="attn_fwd", grid=(SEQ // BLOCK,),
        in_specs=_attn_in_specs(),
        out_specs=pl.BlockSpec((BLOCK, D_ATTN), lambda n: (n, 0)),
        out_shape=jax.ShapeDtypeStruct((SEQ, D_ATTN), F32),
        compiler_params=_cp(("parallel",)),
    )(proj, proj, proj, bias, sinks)


def _attn_bwd(proj, bias, sinks, dcat):
    nb = SEQ // BLOCK

    def body(q_ref, kvp_ref, kvc_ref, bias_ref, sinks_ref, do_ref, dq_ref, dkv_ref, dbias_ref, dsink_ref, dsacc):
        n = pl.program_id(0)

        @pl.when(n == 0)
        def _():
            dkv_ref[...] = jnp.zeros_like(dkv_ref)
            dbias_ref[...] = jnp.zeros_like(dbias_ref)
            dsacc[...] = jnp.zeros_like(dsacc)

        q, kvp, kvc = q_ref[...], kvp_ref[...], kvc_ref[...]
        do_all = do_ref[...]
        dqs, dks, dvs = [], [], []
        for hk in range(N_KV_HEADS):
            qs, kcat, vcat, probs, psink = _attn_pieces(n, q, kvp, kvc, bias_ref, sinks_ref, hk)
            q0 = hk * Q_PER_KV * HEAD_DIM
            do = jnp.concatenate([do_all[:, q0 + g * HEAD_DIM:q0 + (g + 1) * HEAD_DIM] for g in range(Q_PER_KV)],
                                 axis=0).astype(BF16)
            dprobs = _dot(do, vcat, NT)
            dvs.append(_dot(probs.astype(BF16), do, TN))
            rowdot = jnp.sum(probs * dprobs, axis=-1, keepdims=True)
            ds = probs * (dprobs - rowdot)
            dsacc[hk] += -psink * rowdot
            dbias_ref[hk] += ds
            dsb = (ds * (HEAD_DIM ** -0.5)).astype(BF16)
            dqs.append(_unstack_heads(_dot(dsb, kcat)))
            dks.append(_dot(dsb, qs, TN))
        dq_ref[...] = jnp.concatenate(dqs, axis=1)
        upd = jnp.concatenate(dks + dvs, axis=1)
        cur = pl.multiple_of(n * BLOCK, BLOCK)
        dkv_ref[pl.ds(cur, BLOCK), :] += upd[BLOCK:]

        @pl.when(n > 0)
        def _():
            prev = pl.multiple_of((n - 1) * BLOCK, BLOCK)
            dkv_ref[pl.ds(prev, BLOCK), :] += upd[:BLOCK]

        @pl.when(n == nb - 1)
        def _():
            for hk in range(N_KV_HEADS):
                for g in range(Q_PER_KV):
                    tot = jnp.sum(dsacc[hk, g * BLOCK:(g + 1) * BLOCK, :], axis=0, keepdims=True)
                    h = hk * Q_PER_KV + g
                    dsink_ref[h:h + 1, :] = jnp.broadcast_to(tot, (1, LANES))

    return pl.pallas_call(
        body, name="attn_bwd", grid=(nb,),
        in_specs=_attn_in_specs() + [pl.BlockSpec((BLOCK, D_ATTN), lambda n: (n, 0))],
        out_specs=[pl.BlockSpec((BLOCK, D_ATTN), lambda n: (n, 0)), _const((SEQ, 2 * D_KV)),
                   _const((N_KV_HEADS, Q_PER_KV * BLOCK, 2 * BLOCK)), _const((N_Q_HEADS, LANES))],
        out_shape=[jax.ShapeDtypeStruct((SEQ, D_ATTN), F32), jax.ShapeDtypeStruct((SEQ, 2 * D_KV), F32),
                   jax.ShapeDtypeStruct((N_KV_HEADS, Q_PER_KV * BLOCK, 2 * BLOCK), F32),
                   jax.ShapeDtypeStruct((N_Q_HEADS, LANES), F32)],
        scratch_shapes=[pltpu.VMEM((N_KV_HEADS, Q_PER_KV * BLOCK, 1), F32)],
        compiler_params=_cp(("arbitrary",)),
    )(proj, proj, proj, bias, sinks, dcat)


@jax.custom_vjp
def _head_sum(x):
    return _dot_ind(x, _head_ones(x.shape[-1]))


_head_sum.defvjp(lambda x: (_head_sum(x), None), lambda _, ct: (_head_sum(ct),))


@jax.custom_vjp
def _bdot(a, w):
    return _dot(a.astype(BF16), w.astype(BF16))


def _bdot_bwd(res, ct):
    a, w = res
    ctb = ct.astype(BF16)
    return _dot(ctb, w.astype(BF16), NT), _dot(a.astype(BF16), ctb, TN)


_bdot.defvjp(lambda a, w: (_bdot(a, w), (a, w)), _bdot_bwd)


def _sigmoid(x):
    return 0.5 * (jnp.tanh(0.5 * x) + 1.0)


def _softplus(x):
    return jnp.maximum(x, 0.0) + jnp.log(1.0 + jnp.exp(-jnp.abs(x)))


def _rwkv_core(r, k, v, zwa, zg, w0, wdu, a0, wiu, wgu, k_k, k_a):
    w_log = -_softplus(-(w0 + _bdot(jnp.tanh(zwa), wdu))) - 0.5
    decay = jnp.exp(-jnp.exp(w_log))
    a = _sigmoid(a0 + _bdot(zwa, wiu))
    g = _bdot(_sigmoid(zg), wgu)
    kk = k * k_k
    kk = kk / jnp.maximum(jnp.sqrt(_head_sum(kk * kk)), 1e-12)
    k2 = k * (1.0 + (a - 1.0) * k_a)
    return r, decay, k2, v, -kk, kk * a, g


def _rwkv_out(o, r, k2, v, g, lng, lnb, rk):
    mu = _head_sum(o) * (1.0 / HEAD_DIM)
    d = o - mu
    var = _head_sum(d * d) * (1.0 / HEAD_DIM)
    on = d * lax.rsqrt(var + GN_EPS) * lng + lnb
    bonus = _head_sum(r * k2 * rk) * v
    return (on + bonus) * g


P_SPLITS = (0, 512, 1024, 1536, 1664, 1792)
N_PREP_PARAMS = 7
HALO = 8


def _shifted_pieces(i, p_ref, halo_ref, mix_ref):
    p = p_ref[:, P_OFF:]
    prev_row = halo_ref[HALO - 1:HALO, P_OFF:] * jnp.where(i > 0, 1.0, 0.0)
    row = lax.broadcasted_iota(jnp.int32, p.shape, 0)
    pprev = jnp.where(row == 0, prev_row, pltpu.roll(p, 1, 0))
    delta = pprev - p
    ps = p + delta * mix_ref[...]
    return [ps[:, a:b] for a, b in zip(P_SPLITS[:-1], P_SPLITS[1:])], delta


def _prep_in_specs():
    return [_rows(TR, D_IN),
            pl.BlockSpec((HALO, D_IN), lambda i: (jnp.maximum(i * (TR // HALO) - 1, 0), 0)),
            _const((1, RWKV_COLS)), _const((1, D_RWKV)), _const((LANES, D_RWKV)), _const((1, D_RWKV)),
            _const((LANES, D_RWKV)), _const((LANES, D_RWKV)), _const((1, D_RWKV)), _const((1, D_RWKV))]


def _rwkv_prep(proj, mix, prm):
    def body(p_ref, halo_ref, mix_ref, *refs):
        prm_refs, outs = refs[:N_PREP_PARAMS], refs[N_PREP_PARAMS:]
        pieces, _ = _shifted_pieces(pl.program_id(0), p_ref, halo_ref, mix_ref)
        vals = _rwkv_core(*pieces, *[t[...] for t in prm_refs])
        for ref, val in zip(outs, vals):
            ref[...] = val

    return pl.pallas_call(
        body, name="rwkv_prep", grid=(SEQ // TR,),
        in_specs=_prep_in_specs(),
        out_specs=[_rows(TR, D_RWKV)] * 7,
        out_shape=[jax.ShapeDtypeStruct((SEQ, D_RWKV), F32)] * 7,
        compiler_params=_cp(("parallel",)),
    )(proj, proj, mix, *prm)


def _rwkv_prep_bwd(proj, mix, prm, cts):
    def body(p_ref, halo_ref, mix_ref, *refs):
        i = pl.program_id(0)
        prm_refs = refs[:N_PREP_PARAMS]
        ct_refs = refs[N_PREP_PARAMS:N_PREP_PARAMS + 10]
        dps_ref, dmix_ref = refs[N_PREP_PARAMS + 10:N_PREP_PARAMS + 12]
        dprm_refs = refs[N_PREP_PARAMS + 12:]
        pieces, delta = _shifted_pieces(i, p_ref, halo_ref, mix_ref)
        _, vjp = jax.vjp(_rwkv_core, *pieces, *[t[...] for t in prm_refs])
        dr1, dr2, dw, dk1, dk2, dv1, dv2, dkkn, db, dg = [t[...] for t in ct_refs]
        grads = vjp((dr1 + dr2, dw, dk1 + dk2, dv1 + dv2, dkkn, db, dg))
        dps = jnp.concatenate(grads[:5], axis=1)
        dps_ref[...] = dps

        @pl.when(i == 0)
        def _():
            dmix_ref[...] = jnp.zeros_like(dmix_ref)
            for ref in dprm_refs:
                ref[...] = jnp.zeros_like(ref)

        dmix_ref[...] += jnp.sum(dps * delta, axis=0, keepdims=True)
        for ref, gval in zip(dprm_refs, grads[5:]):
            ref[...] += gval

    prm_shapes = [(1, D_RWKV), (LANES, D_RWKV), (1, D_RWKV), (LANES, D_RWKV), (LANES, D_RWKV), (1, D_RWKV), (1, D_RWKV)]
    return pl.pallas_call(
        body, name="rwkv_prep_bwd", grid=(SEQ // TR,),
        in_specs=_prep_in_specs() + [_rows(TR, D_RWKV)] * 10,
        out_specs=[_rows(TR, RWKV_COLS), _const((1, RWKV_COLS))] + [_const(s) for s in prm_shapes],
        out_shape=[jax.ShapeDtypeStruct((SEQ, RWKV_COLS), F32), jax.ShapeDtypeStruct((1, RWKV_COLS), F32)]
        + [jax.ShapeDtypeStruct(s, F32) for s in prm_shapes],
        compiler_params=_cp(("arbitrary",)),
    )(proj, proj, mix, *prm, *cts)


def _rwkv_post(o, r, k2, v, g, lng, lnb, rk, attn):
    def body(o_ref, r_ref, k_ref, v_ref, g_ref, lng_ref, lnb_ref, rk_ref, attn_ref, cat_ref):
        rw = _rwkv_out(*[t[...] for t in (o_ref, r_ref, k_ref, v_ref, g_ref, lng_ref, lnb_ref, rk_ref)])
        cat_ref[...] = jnp.concatenate([attn_ref[...], rw], axis=1).astype(BF16)

    return pl.pallas_call(
        body, name="rwkv_post", grid=(SEQ // TR,),
        in_specs=[_rows(TR, D_RWKV)] * 5 + [_const((1, D_RWKV))] * 3 + [_rows(TR, D_ATTN)],
        out_specs=_rows(TR, D_MODEL),
        out_shape=jax.ShapeDtypeStruct((SEQ, D_MODEL), BF16),
        compiler_params=_cp(("parallel",)),
    )(o, r, k2, v, g, lng, lnb, rk, attn)


def _rwkv_post_bwd(o, r, k2, v, g, lng, lnb, rk, dcat):
    def body(o_ref, r_ref, k_ref, v_ref, g_ref, lng_ref, lnb_ref, rk_ref, dcat_ref,
             do_ref, dr_ref, dk_ref, dv_ref, dg_ref, dlng_ref, dlnb_ref, drk_ref):
        i = pl.program_id(0)
        args = [t[...] for t in (o_ref, r_ref, k_ref, v_ref, g_ref, lng_ref, lnb_ref, rk_ref)]
        _, vjp = jax.vjp(_rwkv_out, *args)
        grads = vjp(dcat_ref[:, D_ATTN:])
        for ref, gval in zip((do_ref, dr_ref, dk_ref, dv_ref, dg_ref), grads[:5]):
            ref[...] = gval

        @pl.when(i == 0)
        def _():
            for ref in (dlng_ref, dlnb_ref, drk_ref):
                ref[...] = jnp.zeros_like(ref)

        for ref, gval in zip((dlng_ref, dlnb_ref, drk_ref), grads[5:]):
            ref[...] += gval

    return pl.pallas_call(
        body, name="rwkv_post_bwd", grid=(SEQ // TR,),
        in_specs=[_rows(TR, D_RWKV)] * 5 + [_const((1, D_RWKV))] * 3 + [_rows(TR, D_MODEL)],
        out_specs=[_rows(TR, D_RWKV)] * 5 + [_const((1, D_RWKV))] * 3,
        out_shape=[jax.ShapeDtypeStruct((SEQ, D_RWKV), F32)] * 5 + [jax.ShapeDtypeStruct((1, D_RWKV), F32)] * 3,
        compiler_params=_cp(("arbitrary",)),
    )(o, r, k2, v, g, lng, lnb, rk, dcat)


def _assemble_dproj(dq, dkv, dps, mix):
    last = SEQ // HALO - 1

    def body(dq_ref, dkv_ref, dps_ref, nxt_ref, mix_ref, o_ref):
        i = pl.program_id(0)
        dps = dps_ref[...]
        mixv = mix_ref[...]
        nxt_row = nxt_ref[0:1, :] * jnp.where(i < SEQ // TR - 1, 1.0, 0.0)
        row = lax.broadcasted_iota(jnp.int32, dps.shape, 0)
        up = jnp.where(row == TR - 1, nxt_row, pltpu.roll(dps, TR - 1, 0))
        dp = dps * (1.0 - mixv) + up * mixv
        o_ref[...] = jnp.concatenate([dq_ref[...], dkv_ref[...], dp], axis=1).astype(BF16)

    return pl.pallas_call(
        body, name="assemble_dproj", grid=(SEQ // TR,),
        in_specs=[_rows(TR, D_ATTN), _rows(TR, 2 * D_KV), _rows(TR, RWKV_COLS),
                  pl.BlockSpec((HALO, RWKV_COLS), lambda i: (jnp.minimum((i + 1) * (TR // HALO), last), 0)),
                  _const((1, RWKV_COLS))],
        out_specs=_rows(TR, D_IN),
        out_shape=jax.ShapeDtypeStruct((SEQ, D_IN), BF16),
        compiler_params=_cp(("parallel",)),
    )(dq, dkv, dps, dps, mix)


N_PAIR = D_RWKV // LANES
CHUNK = 32
N_CHUNK = SEQ // CHUNK
GROUP = 8
STATE = (N_PAIR, HEAD_DIM, LANES)


def _lane_sums(lhs_tiles, ones2):
    out = _dot(jnp.concatenate(lhs_tiles, axis=0), ones2)
    return [out[i * HEAD_DIM:(i + 1) * HEAD_DIM] for i in range(len(lhs_tiles))]


def _seg_sum(xs, ones2):
    return _lane_sums([jnp.concatenate(_split(x, 2), axis=1) for x in xs], ones2)


def _col_form(rows, diag, ones2):
    zero = jnp.zeros((HEAD_DIM, LANES), BF16)
    tiles = []
    for row in rows:
        hi = row.astype(BF16)
        lo = (row - hi.astype(F32)).astype(BF16)
        tiles.append(jnp.concatenate(
            [jnp.where(diag, jnp.broadcast_to(part, (HEAD_DIM, LANES)), zero) for part in (hi, lo)], axis=1))
    return _lane_sums(tiles, ones2)


def _scan_consts():
    ones2 = jnp.concatenate([_head_ones(LANES)] * 2, axis=0)
    sub = lax.broadcasted_iota(jnp.int32, (HEAD_DIM, LANES), 0)
    lane_in_head = lax.broadcasted_iota(jnp.int32, (HEAD_DIM, LANES), 1) & (HEAD_DIM - 1)
    return ones2, lane_in_head == sub, lane_in_head


def _rows_of_columns(tile):
    t = tile.T
    return jnp.concatenate([t[:CHUNK], t[HEAD_DIM:HEAD_DIM + CHUNK]], axis=1)


def _pair(j):
    return slice(j * LANES, (j + 1) * LANES)


def _scan_fwd(r, w, k, v, kkn, b):
    def body(r_ref, w_ref, k_ref, v_ref, kkn_ref, b_ref, o_ref, st_ref, sa_ref, s_scr):
        c = pl.program_id(0)
        ones2, diag, lane_in_head = _scan_consts()

        @pl.when(c == 0)
        def _():
            s_scr[...] = jnp.zeros_like(s_scr)

        def group(gi, carry):
            row0 = pl.multiple_of(gi * GROUP, GROUP)
            states, ocols = list(carry[:N_PAIR]), list(carry[N_PAIR:])
            tiles = [[t[pl.ds(row0, GROUP), _pair(j)] for t in (r_ref, w_ref, k_ref, v_ref, kkn_ref, b_ref)]
                     for j in range(N_PAIR)]
            def row(j, name, u):
                return tiles[j]["rwkvnb".index(name)][u:u + 1]

            def emit_out(u, after):
                here = lane_in_head == gi * GROUP + u
                outs = _seg_sum([after[j] * row(j, "r", u) for j in range(N_PAIR)], ones2)
                for j in range(N_PAIR):
                    ocols[j] = jnp.where(here, outs[j], ocols[j])

            vcols = _col_form([row(j, "v", 0) for j in range(N_PAIR)], diag, ones2)
            after = None
            for u in range(GROUP):
                sas = _seg_sum([states[j] * row(j, "n", u) for j in range(N_PAIR)], ones2)
                if after is not None:
                    emit_out(u - 1, after)
                nxt = (_col_form([row(j, "v", u + 1) for j in range(N_PAIR)], diag, ones2)
                       if u + 1 < GROUP else None)
                for j in range(N_PAIR):
                    states[j] = states[j] * row(j, "w", u) + sas[j] * row(j, "b", u) + vcols[j] * row(j, "k", u)
                    st_ref[row0 + u, j] = states[j]
                    sa_ref[row0 + u, j] = sas[j]
                after, vcols = list(states), nxt
            emit_out(GROUP - 1, after)
            return tuple(states + ocols)

        zero = jnp.zeros((HEAD_DIM, LANES), F32)
        fin = lax.fori_loop(0, CHUNK // GROUP, group, tuple(s_scr[j] for j in range(N_PAIR)) + (zero,) * N_PAIR)
        for j in range(N_PAIR):
            s_scr[j] = fin[j]
            o_ref[:, _pair(j)] = _rows_of_columns(fin[N_PAIR + j])

    blk = pl.BlockSpec((CHUNK, D_RWKV), lambda c: (c, 0))
    per_step = pl.BlockSpec((CHUNK,) + STATE, lambda c: (c, 0, 0, 0))
    return pl.pallas_call(
        body, name="rwkv_scan_fwd", grid=(N_CHUNK,),
        in_specs=[blk] * 6,
        out_specs=[blk, per_step, per_step],
        out_shape=[jax.ShapeDtypeStruct((SEQ, D_RWKV), F32)] + [jax.ShapeDtypeStruct((SEQ,) + STATE, F32)] * 2,
        scratch_shapes=[pltpu.VMEM(STATE, F32)],
        compiler_params=_cp(("arbitrary",)),
    )(r, w, k, v, kkn, b)


def _scan_bwd(r, w, k, v, kkn, b, do, states, sas, ds_in, prev, name, first_chunk, n_chunks):
    top = first_chunk + n_chunks - 1

    def body(r_ref, w_ref, k_ref, v_ref, kkn_ref, b_ref, do_ref, st_ref, before_ref, sa_ref, ds_in_ref, *rest):
        dr_ref, dw_ref, dk_ref, dv_ref, dkkn_ref, db_ref, ds_out_ref, ds_scr = rest[-8:]
        i = pl.program_id(0)
        ones2, diag, lane_in_head = _scan_consts()

        @pl.when(i == 0)
        def _():
            ds_scr[...] = ds_in_ref[...]

        entry = [before_ref[0, j] * jnp.where(i < top, 1.0, 0.0) for j in range(N_PAIR)]

        def reverse(gr, carry):
            gi = CHUNK // GROUP - 1 - gr
            row0 = pl.multiple_of(gi * GROUP, GROUP)
            dstates, dvcols = list(carry[:N_PAIR]), list(carry[N_PAIR:])
            tiles = [[t[pl.ds(row0, GROUP), _pair(j)]
                      for t in (r_ref, w_ref, k_ref, v_ref, kkn_ref, b_ref, do_ref)] for j in range(N_PAIR)]
            rows = [[[None] * GROUP for _ in range(5)] for _ in range(N_PAIR)]

            def row(j, name, u):
                return tiles[j]["rwkvnbd".index(name)][u:u + 1]

            def cols_of(u):
                both = _col_form([row(j, "d", u) for j in range(N_PAIR)] + [row(j, "v", u) for j in range(N_PAIR)],
                                 diag, ones2)
                return [(both[j], both[N_PAIR + j]) for j in range(N_PAIR)]

            def emit_dv(u, dsp):
                here = lane_in_head == gi * GROUP + u
                outs = _seg_sum([dsp[j] * row(j, "k", u) for j in range(N_PAIR)], ones2)
                for j in range(N_PAIR):
                    dvcols[j] = jnp.where(here, outs[j], dvcols[j])

            cols = cols_of(GROUP - 1)
            before = None
            for u in reversed(range(GROUP)):
                tl = gi * GROUP + u
                dsp = [dstates[j] + cols[j][0] * row(j, "r", u) for j in range(N_PAIR)]
                dsas = _seg_sum([dsp[j] * row(j, "b", u) for j in range(N_PAIR)], ones2)
                if before is not None:
                    emit_dv(u + 1, before)
                nxt = cols_of(u - 1) if u > 0 else None
                for j in range(N_PAIR):
                    if u > 0:
                        s_prev = st_ref[tl - 1, j]
                    else:
                        s_prev = jnp.where(gi == 0, entry[j], st_ref[jnp.maximum(tl - 1, 0), j])
                    docol, vcol = cols[j]
                    rows[j][0][u] = jnp.sum(st_ref[tl, j] * docol, axis=0, keepdims=True)
                    rows[j][1][u] = jnp.sum(dsp[j] * s_prev, axis=0, keepdims=True)
                    rows[j][2][u] = jnp.sum(dsp[j] * vcol, axis=0, keepdims=True)
                    rows[j][3][u] = jnp.sum(s_prev * dsas[j], axis=0, keepdims=True)
                    rows[j][4][u] = jnp.sum(dsp[j] * sa_ref[tl, j], axis=0, keepdims=True)
                    dstates[j] = dsp[j] * row(j, "w", u) + dsas[j] * row(j, "n", u)
                before, cols = dsp, nxt
            emit_dv(0, before)
            for j in range(N_PAIR):
                for ref, rr in zip((dr_ref, dw_ref, dk_ref, dkkn_ref, db_ref), rows[j]):
                    ref[pl.ds(row0, GROUP), _pair(j)] = jnp.concatenate(rr, axis=0)
            return tuple(dstates + dvcols)

        zero = jnp.zeros((HEAD_DIM, LANES), F32)
        dfin = lax.fori_loop(0, CHUNK // GROUP, reverse, tuple(ds_scr[j] for j in range(N_PAIR)) + (zero,) * N_PAIR)
        for j in range(N_PAIR):
            ds_scr[j] = dfin[j]
            dv_ref[:, _pair(j)] = _rows_of_columns(dfin[N_PAIR + j])

        @pl.when(i == n_chunks - 1)
        def _():
            ds_out_ref[...] = ds_scr[...]

    blk = pl.BlockSpec((CHUNK, D_RWKV), lambda i: (top - i, 0))
    per_step = pl.BlockSpec((CHUNK,) + STATE, lambda i: (top - i, 0, 0, 0))
    step_before = pl.BlockSpec((1,) + STATE, lambda i: (jnp.maximum((top - i) * CHUNK - 1, 0), 0, 0, 0))
    prev = [] if prev is None else list(prev)
    outs = pl.pallas_call(
        body, name=name, grid=(n_chunks,),
        in_specs=[blk] * 7 + [per_step, step_before, per_step, _const(STATE)] + [ANY] * len(prev),
        out_specs=[blk] * 6 + [_const(STATE)],
        out_shape=[jax.ShapeDtypeStruct((SEQ, D_RWKV), F32)] * 6 + [jax.ShapeDtypeStruct(STATE, F32)],
        scratch_shapes=[pltpu.VMEM(STATE, F32)],
        input_output_aliases={11 + t: t for t in range(len(prev))},
        compiler_params=_cp(("arbitrary",)),
    )(r, w, k, v, kkn, b, do, states, states, sas, ds_in, *prev)
    return outs[:6], outs[6]


def _stacked(rows, cols, pick):
    return pl.BlockSpec((None, rows, cols), pick)


def _local_step(x, target, sm, win_st):
    def tied(t, token):
        return t if token is None else t + token[0:1, 0:1].reshape((1,) * t.ndim)

    zpad = jnp.zeros((LORA_DECAY, D_RWKV), F32)
    prm = [sm["w0"], jnp.concatenate([sm["w_decay_up"], zpad], axis=0), sm["a0"],
           jnp.concatenate([zpad, sm["w_iclr_up"]], axis=0), sm["w_gate_up"], sm["k_k"], sm["k_a"]]
    mix = sm["rwkv_shift_mix"]
    onehot = jnp.asarray(_t5_onehot(), BF16)
    sinks = sm["sinks"].reshape(N_Q_HEADS)
    lng, lnb, rk = sm["ln_x_g"], sm["ln_x_b"], sm["r_k"].reshape(1, D_RWKV)

    h1 = _norm_cast(x, sm["norm_mix_pre"], "norm_in")
    proj = _matmul(h1, win_st, "nn", "proj", m=SEQ, n=D_IN, k=D_MODEL, tm=SEQ, tn=640, tk=D_MODEL,
                   b_spec=_stacked(D_MODEL, 640, lambda i, j, kk: (j, 0, 0)))
    bias = _bias_table(sm["rel_bias"].T, onehot).reshape(N_KV_HEADS, Q_PER_KV * BLOCK, 2 * BLOCK)
    attn = _attn_fwd(proj, bias, sinks)
    r, w, k2, v, kkn, b, g = _rwkv_prep(proj, mix, prm)
    o, states, sas = _scan_fwd(r, w, k2, v, kkn, b)
    wout, wup_st, wdown = yield ("rest_weights", o)
    cat = _rwkv_post(o, r, k2, v, g, lng, lnb, rk, attn)
    mixo = _matmul(cat, wout, "nn", "out_proj", m=SEQ, n=D_MODEL, k=D_MODEL, tm=SEQ, tn=512, tk=D_MODEL)
    x2, h3 = _mix_norm(x, mixo, sm["norm_mix_post"], sm["norm_ffn_pre"])
    u = _matmul(h3, wup_st, "nn", "ffn_up", m=SEQ, n=2 * D_FF, k=D_MODEL, tm=SEQ, tn=512, tk=D_MODEL,
                b_spec=_stacked(D_MODEL, 512, lambda i, j, kk: (j // 4, 0, j % 4)))
    act = _ffn_act(u, sm["conv_w"], sm["conv_b"])
    f = _matmul(act, wdown, "nn", "ffn_down", m=SEQ, n=D_MODEL, k=D_FF, tm=1024, tn=512, tk=2048)
    loss, dy, df, d_g4 = _loss_head(x2, f, sm["norm_ffn_post"], target)

    dact = _matmul(df, wdown, "nt", "d_act", m=SEQ, n=D_FF, k=D_MODEL, tm=SEQ, tn=512, tk=D_MODEL)
    d_wdown = _matmul(act, df, "tn", "d_wdown", m=D_FF, n=D_MODEL, k=SEQ, tm=512, tn=D_MODEL, tk=SEQ)
    du, d_convw, d_convb = _ffn_act_bwd(u, dact, sm["conv_w"], sm["conv_b"])
    d_convw = d_convw.transpose(1, 0, 2).reshape(3, 2 * D_FF)
    d_convb = d_convb.reshape(1, 2 * D_FF)
    dh3 = _matmul(du, wup_st, "nt", "d_h3", m=SEQ, n=D_MODEL, k=2 * D_FF, tm=1024, tn=D_MODEL, tk=2048,
                  a_spec=pl.BlockSpec((None, 1024, 2048), lambda i, j, kk: (kk // 2, i, kk % 2)),
                  b_spec=_stacked(D_MODEL, 2048, lambda i, j, kk: (kk, j, 0)))
    d_wup = _matmul(h3, du, "tn", "d_wup", m=D_MODEL, n=2 * D_FF, k=SEQ, tm=D_MODEL, tn=512, tk=SEQ,
                    b_spec=pl.BlockSpec((None, SEQ, 512), lambda i, j, kk: (j // 8, 0, j % 8)),
                    out=((N_CHIPS, D_MODEL, 2048), _stacked(D_MODEL, 512, lambda i, j, kk: (j // 4, 0, j % 4))))
    dx2, dmix, d_g2, d_g3 = _mid_bwd(x2, mixo, dy, dh3, sm["norm_mix_post"], sm["norm_ffn_pre"])
    dcat = _matmul(dmix, wout, "nt", "d_cat", m=SEQ, n=D_MODEL, k=D_MODEL, tm=SEQ, tn=512, tk=D_MODEL)
    d_wout = _matmul(cat, dmix, "tn", "d_wout", m=D_MODEL, n=D_MODEL, k=SEQ, tm=512, tn=D_MODEL, tk=SEQ)
    token = yield ("grads_a", (d_wdown, d_wup, d_wout))
    do, dr_p, dk_p, dv_p, dg, d_lng, d_lnb, d_rk = _rwkv_post_bwd(o, r, k2, v, g, lng, tied(lnb, token), rk, dcat)
    half = N_CHUNK // 2
    ds_end = jnp.zeros(STATE, F32)
    late, ds_mid = _scan_bwd(r, w, k2, v, kkn, b, do, states, sas, ds_end, None, "rwkv_scan_bwd_late", half, half)
    token = yield ("seam_1", ds_mid)
    scan_cts, ds_first = _scan_bwd(r, w, k2, v, kkn, b, do, states, sas, tied(ds_mid, token), late,
                                   "rwkv_scan_bwd_early", 0, half)
    dr_s, dw_s, dk_s, dv_s, dkkn_s, db_s = scan_cts
    token = yield ("seam_2", ds_first)
    prep_grads = _rwkv_prep_bwd(proj, tied(mix, token), prm,
                                (dr_s, dr_p, dw_s, dk_s, dk_p, dv_s, dv_p, dkkn_s, db_s, dg))
    dps, d_mix, d_w0, d_wdu, d_a0, d_wiu, d_wgu, d_kk, d_ka = prep_grads
    dq, dkv, dbias, dsink = _attn_bwd(proj, bias, sinks, dcat)
    d_relb = _bias_table_bwd(dbias.reshape(N_Q_HEADS, N_REL), onehot).T
    dproj = _assemble_dproj(dq, dkv, dps, mix)
    d_win = _matmul(h1, dproj, "tn", "d_win", m=D_MODEL, n=D_IN, k=SEQ, tm=D_MODEL, tn=640, tk=SEQ,
                    out=((N_CHIPS, D_MODEL, 640), _stacked(D_MODEL, 640, lambda i, j, kk: (j, 0, 0))))
    token = yield ("grads_b", d_win)
    dh1 = _matmul(dproj, win_st, "nt", "d_h1", m=SEQ, n=D_MODEL, k=D_IN, tm=1024, tn=D_MODEL, tk=640,
                  b_spec=_stacked(D_MODEL, 640, lambda i, j, kk: (kk, j, 0)))
    grad_x, d_g1 = _first_bwd(x, dx2, dh1, tied(sm["norm_mix_pre"], token))

    grads = {
        "norm_mix_pre": d_g1, "norm_mix_post": d_g2, "norm_ffn_pre": d_g3, "norm_ffn_post": d_g4,
        "w_in": d_win, "rel_bias": d_relb, "sinks": dsink[:, 0].reshape(1, N_Q_HEADS),
        "rwkv_shift_mix": d_mix, "w0": d_w0, "w_decay_up": d_wdu[:LORA_DECAY], "a0": d_a0,
        "w_iclr_up": d_wiu[LORA_DECAY:], "w_gate_up": d_wgu, "k_k": d_kk, "k_a": d_ka,
        "r_k": d_rk.reshape(1, N_Q_HEADS, HEAD_DIM), "ln_x_g": d_lng, "ln_x_b": d_lnb,
        "w_out": d_wout, "w_ffn_up": d_wup, "conv_w": d_convw, "conv_b": d_convb, "w_ffn_down": d_wdown,
    }
    return loss, grad_x, grads


def _place():
    x, y, c = lax.axis_index("x"), lax.axis_index("y"), lax.axis_index("c")
    chips = [(1 - x, y), (x, 1 - y), (1 - x, 1 - y)]
    return x, y, c, chips


def _remote(src, dst, sems, idx, to):
    return pltpu.make_async_remote_copy(src_ref=src, dst_ref=dst, send_sem=sems[0].at[idx], recv_sem=sems[1].at[idx],
                                        device_id=to, device_id_type=MESH)


def _half(c, rows):
    return pl.ds(pl.multiple_of(c * (rows // 2), 16), rows // 2)


def _gather_weights(big, small):
    nb, ns = len(big), len(small)

    def body(*refs):
        ins, outs = refs[:nb + ns], refs[nb + ns:2 * (nb + ns)]
        ici, d2d, sml, loc = refs[2 * (nb + ns):2 * (nb + ns) + 2], refs[-5:-3], refs[-3:-1], refs[-1]
        x, y, c, chips = _place()
        me = 2 * x + y
        sib = (x, y, 1 - c)
        local = [pltpu.make_async_copy(ins[a], outs[a].at[me], loc.at[a]) for a in range(nb + ns)]
        for cp in local:
            cp.start()
        sends = []
        for a in range(nb):
            rows = _half(c, big[a].shape[0])
            for kk, chip in enumerate(chips):
                sends.append(_remote(ins[a].at[rows], outs[a].at[me, rows], ici, a * 3 + kk, (*chip, c)))
        for a in range(ns):
            for kk, chip in enumerate(chips):
                sends.append(_remote(ins[nb + a], outs[nb + a].at[me], sml, a * 3 + kk, (*chip, c)))
        for cp in sends:
            cp.start()
        passed = []
        for a in range(nb):
            rows = _half(c, big[a].shape[0])
            for kk, (px, py) in enumerate(chips):
                got = outs[a].at[2 * px + py, rows]
                _remote(got, got, ici, a * 3 + kk, sib).wait_recv()
                fwd = _remote(got, got, d2d, a * 3 + kk, sib)
                fwd.start()
                passed.append(fwd)
        for a in range(nb):
            other = _half(1 - c, big[a].shape[0])
            for kk, (px, py) in enumerate(chips):
                land = outs[a].at[2 * px + py, other]
                _remote(land, land, d2d, a * 3 + kk, sib).wait_recv()
        for a in range(ns):
            for kk, (px, py) in enumerate(chips):
                land = outs[nb + a].at[2 * px + py]
                _remote(land, land, sml, a * 3 + kk, sib).wait_recv()
        for cp in sends + passed:
            cp.wait_send()
        for cp in local:
            cp.wait()

    arrs = list(big) + list(small)
    return pl.pallas_call(
        body, name="gather_weights",
        in_specs=[ANY] * len(arrs), out_specs=[ANY] * len(arrs),
        out_shape=[jax.ShapeDtypeStruct((N_CHIPS,) + t.shape, t.dtype) for t in arrs],
        scratch_shapes=[pltpu.SemaphoreType.DMA((3 * nb,)), pltpu.SemaphoreType.DMA((3 * nb,)),
                        pltpu.SemaphoreType.DMA((3 * nb,)), pltpu.SemaphoreType.DMA((3 * nb,)),
                        pltpu.SemaphoreType.DMA((3 * ns,)), pltpu.SemaphoreType.DMA((3 * ns,)),
                        pltpu.SemaphoreType.DMA((nb + ns,))],
        compiler_params=pltpu.CompilerParams(has_side_effects=True),
    )(*arrs)


def _allreduce_small(g):
    rows = g.shape[0]

    def body(g_ref, o_ref, buf, send, recv):
        x, y, c, _ = _place()
        me = 4 * x + 2 * y + c
        buf[me] = g_ref[...]
        sends = []
        for rel in range(1, N_DEV):
            px, py, pc = x ^ (rel >> 2), y ^ ((rel >> 1) & 1), c ^ (rel & 1)
            cp = _remote(g_ref, buf.at[me], (send, recv), rel - 1, (px, py, pc))
            cp.start()
            sends.append(cp)
        for rel in range(1, N_DEV):
            px, py, pc = x ^ (rel >> 2), y ^ ((rel >> 1) & 1), c ^ (rel & 1)
            land = buf.at[4 * px + 2 * py + pc]
            _remote(land, land, (send, recv), rel - 1, (px, py, pc)).wait_recv()
        acc = buf[0]
        for d in range(1, N_DEV):
            acc = acc + buf[d]
        o_ref[...] = acc
        for cp in sends:
            cp.wait_send()

    vm = pl.BlockSpec(memory_space=pltpu.VMEM)
    return pl.pallas_call(
        body, name="allreduce_small", in_specs=[vm], out_specs=vm,
        out_shape=jax.ShapeDtypeStruct((rows, LANES), F32),
        scratch_shapes=[pltpu.VMEM((N_DEV, rows, LANES), F32), pltpu.SemaphoreType.DMA((N_DEV - 1,)),
                        pltpu.SemaphoreType.DMA((N_DEV - 1,))],
        compiler_params=_cp(),
    )(g)


def _pair_exchange(gs):
    n = len(gs)

    def body(*refs):
        ins, got, mine, send, recv, loc = refs[:n], refs[n:2 * n], refs[2 * n:3 * n], refs[-3], refs[-2], refs[-1]
        x, y, c, _ = _place()
        sib = (x, y, 1 - c)
        cps, local = [], []
        for a in range(n):
            rows = gs[a].shape[1]
            cp = _remote(ins[a].at[:, _half(1 - c, rows)], got[a], (send, recv), a, sib)
            cp.start()
            cps.append(cp)
            lc = pltpu.make_async_copy(ins[a].at[:, _half(c, rows)], mine[a], loc.at[a])
            lc.start()
            local.append(lc)
        for a in range(n):
            cps[a].wait_recv()
        for a in range(n):
            cps[a].wait_send()
            local[a].wait()

    halves = [jax.ShapeDtypeStruct((N_CHIPS, t.shape[1] // 2, t.shape[2]), F32) for t in gs]
    outs = pl.pallas_call(
        body, name="grad_pair_exchange", in_specs=[ANY] * n, out_specs=[ANY] * (2 * n), out_shape=halves + halves,
        scratch_shapes=[pltpu.SemaphoreType.DMA((n,)), pltpu.SemaphoreType.DMA((n,)), pltpu.SemaphoreType.DMA((n,))],
        compiler_params=pltpu.CompilerParams(has_side_effects=True),
    )(*gs)
    return outs[:n], outs[n:]


def _chip_exchange(ps):
    n = len(ps)

    def body(*refs):
        ins, outs, send, recv, loc = refs[:n], refs[n:2 * n], refs[-3], refs[-2], refs[-1]
        x, y, c, chips = _place()
        me = 2 * x + y
        cps, local = [], []
        for a in range(n):
            lc = pltpu.make_async_copy(ins[a].at[me], outs[a].at[me], loc.at[a])
            lc.start()
            local.append(lc)
            for kk, (px, py) in enumerate(chips):
                cp = _remote(ins[a].at[2 * px + py], outs[a].at[me], (send, recv), a * 3 + kk, (px, py, c))
                cp.start()
                cps.append(cp)
        for a in range(n):
            for kk, (px, py) in enumerate(chips):
                land = outs[a].at[2 * px + py]
                _remote(land, land, (send, recv), a * 3 + kk, (px, py, c)).wait_recv()
        for cp in cps:
            cp.wait_send()
        for lc in local:
            lc.wait()

    return pl.pallas_call(
        body, name="grad_chip_exchange", in_specs=[ANY] * n, out_specs=[ANY] * n,
        out_shape=[jax.ShapeDtypeStruct(t.shape, F32) for t in ps],
        scratch_shapes=[pltpu.SemaphoreType.DMA((3 * n,)), pltpu.SemaphoreType.DMA((3 * n,)),
                        pltpu.SemaphoreType.DMA((n,))],
        compiler_params=pltpu.CompilerParams(has_side_effects=True),
    )(*ps)


def _pair_gather(hs):
    n = len(hs)

    def body(*refs):
        ins, outs, send, recv, loc = refs[:n], refs[n:2 * n], refs[-3], refs[-2], refs[-1]
        x, y, c, _ = _place()
        sib = (x, y, 1 - c)
        cps, local = [], []
        for a in range(n):
            rows = 2 * hs[a].shape[0]
            cp = _remote(ins[a], outs[a].at[_half(c, rows)], (send, recv), a, sib)
            cp.start()
            cps.append(cp)
            lc = pltpu.make_async_copy(ins[a], outs[a].at[_half(c, rows)], loc.at[a])
            lc.start()
            local.append(lc)
        for a in range(n):
            rows = 2 * hs[a].shape[0]
            land = outs[a].at[_half(1 - c, rows)]
            _remote(land, land, (send, recv), a, sib).wait_recv()
        for a in range(n):
            cps[a].wait_send()
            local[a].wait()

    return pl.pallas_call(
        body, name="grad_pair_gather", in_specs=[ANY] * n, out_specs=[ANY] * n,
        out_shape=[jax.ShapeDtypeStruct((2 * t.shape[0], t.shape[1]), F32) for t in hs],
        scratch_shapes=[pltpu.SemaphoreType.DMA((n,)), pltpu.SemaphoreType.DMA((n,)), pltpu.SemaphoreType.DMA((n,))],
        compiler_params=pltpu.CompilerParams(has_side_effects=True),
    )(*hs)


def _add2(a, b, name):
    r, cdim = a.shape
    tr = 256

    def body(a_ref, b_ref, o_ref):
        o_ref[...] = a_ref[...] + b_ref[...]

    return pl.pallas_call(
        body, name=name, grid=(r // tr,), in_specs=[_rows(tr, cdim)] * 2, out_specs=_rows(tr, cdim),
        out_shape=jax.ShapeDtypeStruct((r, cdim), F32), compiler_params=_cp(("parallel",)),
    )(a, b)


def _sum4(t, name):
    _, r, cdim = t.shape
    tr = 128

    def body(t_ref, o_ref):
        o_ref[...] = ((t_ref[0] + t_ref[1]) + t_ref[2]) + t_ref[3]

    return pl.pallas_call(
        body, name=name, grid=(r // tr,), in_specs=[pl.BlockSpec((N_CHIPS, tr, cdim), lambda i: (0, i, 0))],
        out_specs=_rows(tr, cdim), out_shape=jax.ShapeDtypeStruct((r, cdim), F32),
        compiler_params=_cp(("parallel",)),
    )(t)


def _reduce_big(gs):
    got, mine = _pair_exchange(gs)
    ps = [_add2(m.reshape(-1, m.shape[2]), g.reshape(-1, g.shape[2]), f"grad_pair_add_{i}").reshape(m.shape)
          for i, (m, g) in enumerate(zip(mine, got))]
    xs = _chip_exchange(ps)
    hs = [_sum4(t, f"grad_chip_sum_{i}") for i, t in enumerate(xs)]
    return _pair_gather(hs)


HBM = pl.BlockSpec(memory_space=pltpu.HBM)
SEM = pl.BlockSpec(memory_space=pltpu.SEMAPHORE)
EFFECT = pltpu.SideEffectType.DATAFLOW_SIDE_EFFECTING


def _copies_start(name, bufs, plan, n):
    nb = len(bufs)

    def body(*refs):
        ins, sems, token = refs[:nb], refs[nb:nb + 2 * n], refs[-1]
        for kk, (src, dst, dev) in enumerate(plan(ins)):
            pltpu.make_async_remote_copy(src_ref=src, dst_ref=dst, send_sem=sems[2 * kk], recv_sem=sems[2 * kk + 1],
                                         device_id=dev, device_id_type=MESH).start()
        token[...] = jnp.zeros_like(token)

    outs = pl.pallas_call(
        body, name=name,
        out_shape=tuple([pltpu.SemaphoreType.DMA(())] * (2 * n) + [pltpu.HBM(t.shape, t.dtype) for t in bufs]
                        + [jax.ShapeDtypeStruct((8, LANES), F32)]),
        in_specs=[HBM] * nb,
        out_specs=tuple([SEM] * (2 * n) + [HBM] * nb + [pl.BlockSpec(memory_space=pltpu.VMEM)]),
        input_output_aliases={t: 2 * n + t for t in range(nb)},
        compiler_params=pltpu.CompilerParams(has_side_effects=EFFECT),
    )(*[pltpu.with_memory_space_constraint(t, pltpu.HBM) for t in bufs])
    return outs[:2 * n], outs[2 * n:2 * n + nb], outs[-1]


def _copies_wait(name, sems, bufs, plan, n, after):
    nb = len(bufs)

    def body(*refs):
        ins, sem_refs = refs[:nb], refs[nb:nb + 2 * n]
        for kk, (src, dst, dev) in enumerate(plan(ins)):
            cp = pltpu.make_async_remote_copy(src_ref=src, dst_ref=dst, send_sem=sem_refs[2 * kk],
                                              recv_sem=sem_refs[2 * kk + 1], device_id=dev, device_id_type=MESH)
            cp.wait_send()
            cp.wait_recv()

    return pl.pallas_call(
        body, name=name,
        out_shape=tuple(pltpu.HBM(t.shape, t.dtype) for t in bufs),
        in_specs=[HBM] * nb + [SEM] * (2 * n) + [ANY],
        out_specs=tuple([HBM] * nb),
        input_output_aliases={t: t for t in range(nb)},
        compiler_params=pltpu.CompilerParams(has_side_effects=EFFECT),
    )(*bufs, *sems, after)


def _plan_gather(n_w):
    def plan(refs):
        x, y, c, chips = _place()
        me = 2 * x + y
        return [(refs[a], refs[n_w + a].at[me], (*chip, c)) for a in range(n_w) for chip in chips]
    return plan


def _plan_pair_halves(n_g, rows):
    def plan(refs):
        x, y, c, _ = _place()
        return [(refs[a].at[:, _half(1 - c, rows[a])], refs[n_g + a], (x, y, 1 - c)) for a in range(n_g)]
    return plan


def _plan_chip_parts(n_g):
    def plan(refs):
        x, y, c, chips = _place()
        me = 2 * x + y
        return [(refs[a].at[2 * px + py], refs[n_g + a].at[me], (px, py, c))
                for a in range(n_g) for (px, py) in chips]
    return plan


def _plan_pair_fill(n_g, rows):
    def plan(refs):
        x, y, c, _ = _place()
        return [(refs[a].at[_half(c, rows[a])], refs[a].at[_half(c, rows[a])], (x, y, 1 - c)) for a in range(n_g)]
    return plan


def _pair_add(g, got, name):
    _, rows, cols = g.shape
    hr = rows // 2
    tr = min(hr, 256)
    nb = hr // tr

    def body(g_ref, got_ref, p_ref, own_ref):
        val = (g_ref[...] + got_ref[...]).astype(BF16)
        p_ref[...] = val

        @pl.when(pl.program_id(1) == 2 * lax.axis_index("x") + lax.axis_index("y"))
        def _():
            own_ref[...] = val

    def mine(i, s):
        return (2 * lax.axis_index("x") + lax.axis_index("y"), i, 0)

    return pl.pallas_call(
        body, name=name, grid=(nb, N_CHIPS),
        in_specs=[pl.BlockSpec((None, tr, cols), lambda i, s: (s, lax.axis_index("c") * nb + i, 0)),
                  pl.BlockSpec((None, tr, cols), lambda i, s: (s, i, 0))],
        out_specs=[pl.BlockSpec((None, tr, cols), lambda i, s: (s, i, 0)), pl.BlockSpec((None, tr, cols), mine)],
        out_shape=[jax.ShapeDtypeStruct((N_CHIPS, hr, cols), BF16)] * 2,
        compiler_params=_cp(("parallel", "arbitrary")),
    )(g, got)


def _chip_sum(parts, name):
    _, hr, cols = parts.shape
    tr = min(hr, 128)
    nb = hr // tr

    def body(t_ref, o_ref):
        part = [t_ref[s].astype(F32) for s in range(N_CHIPS)]
        o_ref[...] = ((part[0] + part[1]) + part[2]) + part[3]

    return pl.pallas_call(
        body, name=name, grid=(nb,),
        in_specs=[pl.BlockSpec((N_CHIPS, tr, cols), lambda i: (0, i, 0))],
        out_specs=pl.BlockSpec((tr, cols), lambda i: (lax.axis_index("c") * nb + i, 0)),
        out_shape=jax.ShapeDtypeStruct((2 * hr, cols), F32),
        compiler_params=_cp(("parallel",)),
    )(parts)


class _Reduction:
    def __init__(self, tag, rows):
        self.tag, self.n, self.rows = tag, len(rows), rows
        self.plans = (_plan_pair_halves(self.n, rows), _plan_chip_parts(self.n), _plan_pair_fill(self.n, rows))
        self.flight = None

    def _name(self, what):
        return f"grad_{self.tag}_{what}"

    def start(self, gs):
        gots = [lax.empty((N_CHIPS, t.shape[1] // 2, t.shape[2]), F32) for t in gs]
        self.flight = _copies_start(self._name("pair_start"), list(gs) + gots, self.plans[0], self.n)
        return self.flight[2]

    def after_pair(self, after):
        sems, bufs, _ = self.flight
        out = _copies_wait(self._name("pair_wait"), sems, bufs, self.plans[0], self.n, after)
        sums = [_pair_add(g, got, self._name(f"pair_add_{i}"))
                for i, (g, got) in enumerate(zip(out[:self.n], out[self.n:]))]
        self.flight = _copies_start(self._name("chip_start"), [p for p, _ in sums] + [own for _, own in sums],
                                    self.plans[1], 3 * self.n)
        return self.flight[2]

    def after_chips(self, after):
        sems, bufs, _ = self.flight
        out = _copies_wait(self._name("chip_wait"), sems, bufs, self.plans[1], 3 * self.n, after)
        fulls = [_chip_sum(t, self._name(f"chip_sum_{i}")) for i, t in enumerate(out[self.n:])]
        self.flight = _copies_start(self._name("fill_start"), fulls, self.plans[2], self.n)
        return self.flight[2]

    def finish(self, after):
        sems, bufs, _ = self.flight
        return _copies_wait(self._name("fill_wait"), sems, bufs, self.plans[2], self.n, after)


def _adamw_math(w, g, m, v):
    nm = ADAM_B1 * m + (1.0 - ADAM_B1) * g
    nv = ADAM_B2 * v + (1.0 - ADAM_B2) * (g * g)
    m_hat = nm / (1.0 - ADAM_B1 ** ADAM_STEP)
    v_hat = nv / (1.0 - ADAM_B2 ** ADAM_STEP)
    return -ADAM_LR * (m_hat / (jnp.sqrt(v_hat) + ADAM_EPS) + ADAM_WD * w), nm, nv


def _adamw(w, g, m, v, name, tr):
    r, cdim = w.shape

    def body(w_ref, g_ref, m_ref, v_ref, d_ref, nm_ref, nv_ref):
        d_ref[...], nm_ref[...], nv_ref[...] = _adamw_math(w_ref[...], g_ref[...], m_ref[...], v_ref[...])

    return pl.pallas_call(
        body, name=name, grid=(r // tr,), in_specs=[_rows(tr, cdim)] * 4, out_specs=[_rows(tr, cdim)] * 3,
        out_shape=[jax.ShapeDtypeStruct((r, cdim), F32)] * 3, compiler_params=_cp(("parallel",)),
    )(w, g, m, v)


def _adamw_small(w, parts, m, v):
    def body(w_ref, p_ref, m_ref, v_ref, d_ref, nm_ref, nv_ref, g_ref):
        g = p_ref[0]
        for dev in range(1, N_DEV):
            g = g + p_ref[dev]
        g_ref[...] = g
        d_ref[...], nm_ref[...], nv_ref[...] = _adamw_math(w_ref[...], g, m_ref[...], v_ref[...])

    return pl.pallas_call(
        body, name="adamw_small", grid=(1,),
        in_specs=[_const(w.shape), _const(parts.shape), _const(w.shape), _const(w.shape)],
        out_specs=[_const(w.shape)] * 4, out_shape=[jax.ShapeDtypeStruct(w.shape, F32)] * 4,
        compiler_params=_cp(("arbitrary",)),
    )(w, parts, m, v)


REPLICATED = (("norm_mix_pre", 1024), ("norm_mix_post", 1024), ("norm_ffn_pre", 1024), ("norm_ffn_post", 1024),
              ("rel_bias", 256), ("sinks", 8), ("rwkv_shift_mix", 1792), ("w0", 512), ("a0", 512), ("k_k", 512),
              ("k_a", 512), ("r_k", 512), ("ln_x_g", 512), ("ln_x_b", 512), ("conv_b", 8192))
SMALL_SHARDED = (("w_decay_up", LORA_DECAY, D_RWKV), ("w_iclr_up", LORA_ICLR, D_RWKV),
                 ("w_gate_up", LORA_GATE, D_RWKV), ("conv_w", 3, 2 * D_FF))
BIG = (("w_in", D_MODEL, 640), ("w_out", 256, D_MODEL), ("w_ffn_up", D_MODEL, 2048), ("w_ffn_down", 1024, D_MODEL))
PACK_ALIGN = 8 * LANES


def _pack(pieces):
    flat = []
    for t in pieces:
        t = t.reshape(-1)
        pad = (-t.shape[0]) % LANES
        flat.append(jnp.pad(t, (0, pad)) if pad else t)
    flat = jnp.concatenate(flat)
    pad = (-flat.shape[0]) % PACK_ALIGN
    return jnp.pad(flat, (0, pad)).reshape(-1, LANES)


def _unpack(buf, sizes):
    flat, out, off = buf.reshape(-1), [], 0
    for n in sizes:
        out.append(flat[off:off + n])
        off += n + ((-n) % LANES)
    return out


def kernel(x, norm_mix_pre, norm_mix_post, norm_ffn_pre, norm_ffn_post, w_in, rel_bias, sinks, rwkv_shift_mix, w0, w_decay_up, a0, w_iclr_up, w_gate_up, k_k, k_a, r_k, ln_x_g, ln_x_b, w_out, w_ffn_up, conv_w, conv_b, w_ffn_down, loss_target, m_norm_mix_pre, m_norm_mix_post, m_norm_ffn_pre, m_norm_ffn_post, m_w_in, m_rel_bias, m_sinks, m_rwkv_shift_mix, m_w0, m_w_decay_up, m_a0, m_w_iclr_up, m_w_gate_up, m_k_k, m_k_a, m_r_k, m_ln_x_g, m_ln_x_b, m_w_out, m_w_ffn_up, m_conv_w, m_conv_b, m_w_ffn_down, v_norm_mix_pre, v_norm_mix_post, v_norm_ffn_pre, v_norm_ffn_post, v_w_in, v_rel_bias, v_sinks, v_rwkv_shift_mix, v_w0, v_w_decay_up, v_a0, v_w_iclr_up, v_w_gate_up, v_k_k, v_k_a, v_r_k, v_ln_x_g, v_ln_x_b, v_w_out, v_w_ffn_up, v_conv_w, v_conv_b, v_w_ffn_down):
    given = dict(locals())
    names = [n for n, _ in REPLICATED] + [n for n, _, _ in SMALL_SHARDED] + [n for n, _, _ in BIG]
    order = ["norm_mix_pre", "norm_mix_post", "norm_ffn_pre", "norm_ffn_post", "w_in", "rel_bias", "sinks",
             "rwkv_shift_mix", "w0", "w_decay_up", "a0", "w_iclr_up", "w_gate_up", "k_k", "k_a", "r_k", "ln_x_g",
             "ln_x_b", "w_out", "w_ffn_up", "conv_w", "conv_b", "w_ffn_down"]
    assert sorted(names) == sorted(order)
    shard = 2 * lax.axis_index("x") + lax.axis_index("y")

    big_sh = {n: given[n].reshape(a, b).astype(BF16) for n, a, b in BIG}
    small_sh = [given[n].reshape(r, c // N_CHIPS) for n, r, c in SMALL_SHARDED]
    gathered = _gather_weights([big_sh["w_in"]], small_sh)
    rest = ("w_out", "w_ffn_up", "w_ffn_down")
    win_st, rest_sh = lax.optimization_barrier((gathered[0], [big_sh[n] for n in rest]))
    sm = {n: given[n] for n, _ in REPLICATED}
    sm["r_k"] = r_k.reshape(N_Q_HEADS, HEAD_DIM)
    for (n, r, c), st in zip(SMALL_SHARDED, gathered[1:]):
        sm[n] = st.transpose(1, 0, 2).reshape(r, c)

    lands = [lax.dynamic_update_slice(lax.empty((N_CHIPS,) + t.shape, BF16), t[None], (shard, 0, 0)) for t in rest_sh]
    plan_w = _plan_gather(len(rest))
    w_sems, w_bufs, token = _copies_start("gather_rest_start", rest_sh + lands, plan_w, 9)
    sm["norm_mix_pre"] = norm_mix_pre + token[0:1, 0:1]

    def on_rest_weights(after):
        out = _copies_wait("gather_rest_wait", w_sems, w_bufs, plan_w, 9, after)
        wout_st, wup_st, wdown_st = out[3:]
        return wout_st.reshape(D_MODEL, D_MODEL), wup_st, wdown_st.reshape(D_FF, D_MODEL)

    red_a = _Reduction("a", (1024, D_MODEL, 256))
    red_b = _Reduction("b", (D_MODEL,))

    def on_grads_a(gs):
        d_wdown, d_wup, d_wout = gs
        return red_a.start([d_wdown.reshape(N_CHIPS, 1024, D_MODEL), d_wup, d_wout.reshape(N_CHIPS, 256, D_MODEL)])

    handlers = {"rest_weights": on_rest_weights, "grads_a": on_grads_a, "seam_1": red_a.after_pair,
                "seam_2": red_a.after_chips, "grads_b": lambda g: red_b.start([g])}
    steps = _local_step(x[0], loss_target[0], sm, win_st)
    kind, payload = next(steps)
    while True:
        try:
            kind, payload = steps.send(handlers[kind](payload))
        except StopIteration as done:
            loss, grad_x, grads = done.value
            break
    loss = lax.psum(loss[0, 0], ("x", "y", "c"))

    small_names = [n for n, _ in REPLICATED] + [n for n, _, _ in SMALL_SHARDED]

    def shard_cols(t, s):
        return t[:, s * (t.shape[1] // N_CHIPS):(s + 1) * (t.shape[1] // N_CHIPS)]

    for_chip = jnp.stack([_pack([grads[n] for n, _ in REPLICATED]
                                + [shard_cols(grads[n], s) for n, _, _ in SMALL_SHARDED]) for s in range(N_CHIPS)])
    me = 2 * shard + lax.axis_index("c")
    mine = lax.dynamic_index_in_dim(for_chip, shard, 0, keepdims=True)
    land = lax.dynamic_update_slice(lax.empty((N_DEV,) + for_chip.shape[1:], F32), mine, (me, 0, 0))

    def plan_small(refs):
        x, y, c, _ = _place()
        out = []
        for rel in range(1, N_DEV):
            px, py, pc = x ^ (rel >> 2), y ^ ((rel >> 1) & 1), c ^ (rel & 1)
            out.append((refs[0].at[2 * px + py], refs[1].at[4 * x + 2 * y + c], (px, py, pc)))
        return out

    s_sems, s_bufs, _ = _copies_start("grad_small_start", [for_chip, land], plan_small, N_DEV - 1)

    red_b.after_pair(grad_x)
    g_out = {}
    g_out["w_ffn_down"], g_out["w_ffn_up"], g_out["w_out"] = red_a.finish(grad_x)

    delta, new_m, new_v = {}, {}, {}
    for n, a, b in reversed(BIG):
        if n == "w_out":
            red_b.after_chips(delta["w_ffn_up"])
        if n == "w_in":
            parts = _copies_wait("grad_small_wait", s_sems, s_bufs, plan_small, N_DEV - 1, delta["w_out"])[1]
            packs = [_pack([given[pre + n2] for n2 in small_names]) for pre in ("", "m_", "v_")]
            small_sizes = [int(np.prod(given[n2].shape)) for n2 in small_names]
            upd = [_unpack(t, small_sizes) for t in _adamw_small(packs[0], parts, packs[1], packs[2])]
            for n2, d, nm, nv, g in zip(small_names, *upd):
                shape = given[n2].shape
                delta[n2], new_m[n2], new_v[n2], g_out[n2] = (t.reshape(shape) for t in (d, nm, nv, g))
            g_out[n], = red_b.finish(delta["w_out"])
        d, nm, nv = _adamw(given[n].reshape(a, b), g_out[n], given["m_" + n].reshape(a, b),
                           given["v_" + n].reshape(a, b), "adamw_" + n, 128)
        delta[n], new_m[n], new_v[n] = d, nm, nv

    def shaped(d):
        return [d[n].reshape(given[n].shape) for n in order]

    return (loss, grad_x.reshape(x.shape), *shaped(g_out), *shaped(delta), *shaped(new_m), *shaped(new_v))
```

```python
import functools
import math

import numpy as np
import jax
import jax.numpy as jnp
from jax import lax
from jax.experimental import pallas as pl
from jax.experimental.pallas import tpu as pltpu

F32 = jnp.float32
BF16 = jnp.bfloat16
MESH = pl.DeviceIdType.MESH

SEQ = 2048
D_MODEL = 1024
HEAD_DIM = 64
D_ATTN = 512
D_RWKV = 512
D_KV = 128
N_Q_HEADS = 8
N_KV_HEADS = 2
Q_PER_KV = 4
BLOCK = 128
N_BUCKETS = 32
MAX_DISTANCE = 128
LORA_DECAY = 64
LORA_ICLR = 64
LORA_GATE = 128
RWKV_COLS = 3 * D_RWKV + LORA_DECAY + LORA_ICLR + LORA_GATE
P_OFF = D_ATTN + 2 * D_KV
D_IN = P_OFF + RWKV_COLS
D_FF = 4096
NORM_EPS = 1e-6
GN_EPS = 64e-5
NEG_INF = -1e30
N_CHIPS = 4
N_DEV = 8

ADAM_LR = 0.001
ADAM_B1 = 0.9
ADAM_B2 = 0.999
ADAM_EPS = 1e-08
ADAM_WD = 0.01
ADAM_STEP = 10

VMEM_LIMIT = 52 * 1024 * 1024
LANES = 128


def _cp(sem=None, vmem=VMEM_LIMIT):
    kw = dict(vmem_limit_bytes=vmem)
    if sem is not None:
        kw["dimension_semantics"] = sem
    return pltpu.CompilerParams(**kw)


def _rows(tr, nc):
    return pl.BlockSpec((tr, nc), lambda i: (i, 0))


def _const(shape):
    return pl.BlockSpec(shape, lambda *_: (0,) * len(shape))


ANY = pl.BlockSpec(memory_space=pl.ANY)


def _split(x, n):
    parts = []
    for _ in range(n - 1):
        h = x.astype(BF16)
        parts.append(h)
        x = x - h.astype(F32)
    parts.append(x.astype(BF16))
    return parts


def _dot(a, b, dn=(((1,), (0,)), ((), ()))):
    return lax.dot_general(a, b, dn, preferred_element_type=F32)


NN = (((1,), (0,)), ((), ()))
NT = (((1,), (1,)), ((), ()))
TN = (((0,), (0,)), ((), ()))


def _dot_ind(x, ind_bf16, n=3):
    acc = None
    for part in _split(x, n):
        t = _dot(part, ind_bf16)
        acc = t if acc is None else acc + t
    return acc


def _head_ones(n, scale=1.0):
    r = lax.broadcasted_iota(jnp.int32, (n, n), 0) >> 6
    c = lax.broadcasted_iota(jnp.int32, (n, n), 1) >> 6
    return jnp.where(r == c, 1.0, 0.0).astype(BF16)


def _matmul(a, b, mode, name, *, m, n, k, tm, tn, tk, a_spec=None, b_spec=None, out=None, out_dtype=F32):
    nk = k // tk
    dn = {"nn": NN, "nt": NT, "tn": TN}[mode]

    def body(a_ref, b_ref, o_ref, *scratch):
        part = _dot(a_ref[...], b_ref[...], dn)
        if nk == 1:
            o_ref[...] = part.astype(out_dtype)
        else:
            acc_ref, = scratch
            kk = pl.program_id(2)

            @pl.when(kk == 0)
            def _():
                acc_ref[...] = part

            @pl.when(kk > 0)
            def _():
                acc_ref[...] += part

            @pl.when(kk == nk - 1)
            def _():
                o_ref[...] = acc_ref[...].astype(out_dtype)

    if a_spec is None:
        a_spec = (pl.BlockSpec((tk, tm), lambda i, j, kk: (kk, i)) if mode == "tn"
                  else pl.BlockSpec((tm, tk), lambda i, j, kk: (i, kk)))
    if b_spec is None:
        b_spec = (pl.BlockSpec((tn, tk), lambda i, j, kk: (j, kk)) if mode == "nt"
                  else pl.BlockSpec((tk, tn), lambda i, j, kk: (kk, j)))
    return pl.pallas_call(
        body, name=name, grid=(m // tm, n // tn, nk),
        in_specs=[a_spec, b_spec],
        out_specs=pl.BlockSpec((tm, tn), lambda i, j, kk: (i, j)) if out is None else out[1],
        out_shape=jax.ShapeDtypeStruct((m, n) if out is None else out[0], out_dtype),
        scratch_shapes=[] if nk == 1 else [pltpu.VMEM((tm, tn), F32)],
        compiler_params=_cp(("parallel", "parallel", "arbitrary")),
    )(a, b)


def _rstd(x):
    return lax.rsqrt(jnp.mean(x * x, axis=-1, keepdims=True) + NORM_EPS)


def _rms_bwd(x, r, g, dy):
    gy = dy * g
    return r * gy - x * ((r * r * r) * (jnp.sum(x * gy, axis=-1, keepdims=True) / x.shape[-1]))


TR = 256


def _norm_cast(x, g, name):
    def body(x_ref, g_ref, h_ref):
        x = x_ref[...]
        h_ref[...] = (x * _rstd(x) * g_ref[...]).astype(BF16)

    return pl.pallas_call(
        body, name=name, grid=(SEQ // TR,),
        in_specs=[_rows(TR, D_MODEL), _const((1, D_MODEL))],
        out_specs=_rows(TR, D_MODEL),
        out_shape=jax.ShapeDtypeStruct((SEQ, D_MODEL), BF16),
        compiler_params=_cp(("parallel",)),
    )(x, g)


def _mix_norm(x, mix, g2, g3):
    def body(x_ref, mix_ref, g2_ref, g3_ref, x2_ref, h3_ref):
        mixv = mix_ref[...]
        x2 = x_ref[...] + mixv * _rstd(mixv) * g2_ref[...]
        x2_ref[...] = x2
        h3_ref[...] = (x2 * _rstd(x2) * g3_ref[...]).astype(BF16)

    return pl.pallas_call(
        body, name="mix_norm", grid=(SEQ // TR,),
        in_specs=[_rows(TR, D_MODEL), _rows(TR, D_MODEL), _const((1, D_MODEL)), _const((1, D_MODEL))],
        out_specs=[_rows(TR, D_MODEL), _rows(TR, D_MODEL)],
        out_shape=[jax.ShapeDtypeStruct((SEQ, D_MODEL), F32), jax.ShapeDtypeStruct((SEQ, D_MODEL), BF16)],
        compiler_params=_cp(("parallel",)),
    )(x, mix, g2, g3)


def _loss_head(x2, f, g4, target):
    def body(x2_ref, f_ref, g4_ref, t_ref, loss_ref, dy_ref, df_ref, dg_ref):
        i = pl.program_id(0)
        f = f_ref[...]
        g4 = g4_ref[...]
        r = _rstd(f)
        e = x2_ref[...] + f * r * g4 - t_ref[...]
        dy = e * (1.0 / D_MODEL)
        dy_ref[...] = dy
        df_ref[...] = _rms_bwd(f, r, g4, dy).astype(BF16)
        part = 0.5 * jnp.sum(jnp.sum(e * e, axis=-1, keepdims=True), axis=0, keepdims=True) * (1.0 / D_MODEL)
        dg = jnp.sum(dy * f * r, axis=0, keepdims=True)

        @pl.when(i == 0)
        def _():
            loss_ref[...] = jnp.zeros_like(loss_ref)
            dg_ref[...] = jnp.zeros_like(dg_ref)

        loss_ref[...] += jnp.broadcast_to(part, loss_ref.shape)
        dg_ref[...] += dg

    return pl.pallas_call(
        body, name="loss_head", grid=(SEQ // TR,),
        in_specs=[_rows(TR, D_MODEL), _rows(TR, D_MODEL), _const((1, D_MODEL)), _rows(TR, D_MODEL)],
        out_specs=[_const((8, LANES)), _rows(TR, D_MODEL), _rows(TR, D_MODEL), _const((1, D_MODEL))],
        out_shape=[jax.ShapeDtypeStruct((8, LANES), F32), jax.ShapeDtypeStruct((SEQ, D_MODEL), F32),
                   jax.ShapeDtypeStruct((SEQ, D_MODEL), BF16), jax.ShapeDtypeStruct((1, D_MODEL), F32)],
        compiler_params=_cp(("arbitrary",)),
    )(x2, f, g4, target)


def _mid_bwd(x2, mix, dy, dh3, g2, g3):
    def body(x2_ref, mix_ref, dy_ref, dh3_ref, g2_ref, g3_ref, dx2_ref, dmix_ref, dg2_ref, dg3_ref):
        i = pl.program_id(0)
        x2 = x2_ref[...]
        mixv = mix_ref[...]
        dh3 = dh3_ref[...]
        r3 = _rstd(x2)
        dx2 = dy_ref[...] + _rms_bwd(x2, r3, g3_ref[...], dh3)
        dx2_ref[...] = dx2
        r2 = _rstd(mixv)
        dmix_ref[...] = _rms_bwd(mixv, r2, g2_ref[...], dx2).astype(BF16)

        @pl.when(i == 0)
        def _():
            dg2_ref[...] = jnp.zeros_like(dg2_ref)
            dg3_ref[...] = jnp.zeros_like(dg3_ref)

        dg3_ref[...] += jnp.sum(dh3 * x2 * r3, axis=0, keepdims=True)
        dg2_ref[...] += jnp.sum(dx2 * mixv * r2, axis=0, keepdims=True)

    return pl.pallas_call(
        body, name="mid_bwd", grid=(SEQ // TR,),
        in_specs=[_rows(TR, D_MODEL)] * 4 + [_const((1, D_MODEL))] * 2,
        out_specs=[_rows(TR, D_MODEL), _rows(TR, D_MODEL), _const((1, D_MODEL)), _const((1, D_MODEL))],
        out_shape=[jax.ShapeDtypeStruct((SEQ, D_MODEL), F32), jax.ShapeDtypeStruct((SEQ, D_MODEL), BF16),
                   jax.ShapeDtypeStruct((1, D_MODEL), F32), jax.ShapeDtypeStruct((1, D_MODEL), F32)],
        compiler_params=_cp(("arbitrary",)),
    )(x2, mix, dy, dh3, g2, g3)


def _first_bwd(x, dx2, dh1, g1):
    def body(x_ref, dx2_ref, dh1_ref, g1_ref, dx_ref, dg1_ref):
        i = pl.program_id(0)
        x = x_ref[...]
        dh1 = dh1_ref[...]
        r = _rstd(x)
        dx_ref[...] = dx2_ref[...] + _rms_bwd(x, r, g1_ref[...], dh1)

        @pl.when(i == 0)
        def _():
            dg1_ref[...] = jnp.zeros_like(dg1_ref)

        dg1_ref[...] += jnp.sum(dh1 * x * r, axis=0, keepdims=True)

    return pl.pallas_call(
        body, name="first_bwd", grid=(SEQ // TR,),
        in_specs=[_rows(TR, D_MODEL)] * 3 + [_const((1, D_MODEL))],
        out_specs=[_rows(TR, D_MODEL), _const((1, D_MODEL))],
        out_shape=[jax.ShapeDtypeStruct((SEQ, D_MODEL), F32), jax.ShapeDtypeStruct((1, D_MODEL), F32)],
        compiler_params=_cp(("arbitrary",)),
    )(x, dx2, dh1, g1)


TC = 256
N_CB = D_FF // TC
GELU_C = math.sqrt(2.0 / math.pi)


def _shift_down(u, s):
    rolled = pltpu.roll(u, s, 0)
    row = lax.broadcasted_iota(jnp.int32, u.shape, 0)
    return jnp.where(row >= s, rolled, 0.0)


def _shift_up(u, s):
    n = u.shape[0]
    rolled = pltpu.roll(u, n - s, 0)
    row = lax.broadcasted_iota(jnp.int32, u.shape, 0)
    return jnp.where(row < n - s, rolled, 0.0)


def _conv3(u, w, b):
    return b + w[0:1] * _shift_down(u, 2) + w[1:2] * _shift_down(u, 1) + w[2:3] * u


def _gelu_and_grad(x):
    inner = GELU_C * (x + 0.044715 * (x * x * x))
    t = jnp.tanh(inner)
    gelu = 0.5 * x * (1.0 + t)
    dgelu = 0.5 * (1.0 + t) + 0.5 * x * (1.0 - t * t) * (GELU_C * (1.0 + 3 * 0.044715 * (x * x)))
    return gelu, dgelu


def _ffn_specs():
    col = lambda off: pl.BlockSpec((SEQ, TC), lambda *g: (0, g[-1] + off))
    w = lambda off: pl.BlockSpec((3, TC), lambda *g: (0, g[-1] + off))
    b = lambda off: pl.BlockSpec((1, TC), lambda *g: (0, g[-1] + off))
    return col, w, b


def _ffn_act(u, conv_w, conv_b):
    col, w, b = _ffn_specs()

    def body(ug_ref, uv_ref, wg_ref, wv_ref, bg_ref, bv_ref, act_ref):
        gate = _conv3(ug_ref[...], wg_ref[...], bg_ref[...])
        val = _conv3(uv_ref[...], wv_ref[...], bv_ref[...])
        act_ref[...] = (_gelu_and_grad(gate)[0] * val).astype(BF16)

    return pl.pallas_call(
        body, name="ffn_act", grid=(N_CB,),
        in_specs=[col(0), col(N_CB), w(0), w(N_CB), b(0), b(N_CB)],
        out_specs=col(0),
        out_shape=jax.ShapeDtypeStruct((SEQ, D_FF), BF16),
        compiler_params=_cp(("parallel",)),
    )(u, u, conv_w, conv_w, conv_b, conv_b)


def _ffn_act_bwd(u, dact, conv_w, conv_b):
    col, w, b = _ffn_specs()
    both = lambda rows: pl.BlockSpec((2, rows, TC), lambda j: (0, 0, j))

    def body(ug_ref, uv_ref, da_ref, wg_ref, wv_ref, bg_ref, bv_ref, du_ref, dw_ref, db_ref):
        ug, uv = ug_ref[...], uv_ref[...]
        wg, wv = wg_ref[...], wv_ref[...]
        gate = _conv3(ug, wg, bg_ref[...])
        val = _conv3(uv, wv, bv_ref[...])
        gelu, dgelu = _gelu_and_grad(gate)
        da = da_ref[...]
        for h, (duc, uh, wh) in enumerate(((da * val * dgelu, ug, wg), (da * gelu, uv, wv))):
            up1, up2 = _shift_up(duc, 1), _shift_up(duc, 2)
            du_ref[h] = (wh[2:3] * duc + wh[1:2] * up1 + wh[0:1] * up2).astype(BF16)
            db_ref[h] = jnp.sum(duc, axis=0, keepdims=True)
            dw_ref[h] = jnp.concatenate(
                [jnp.sum(up2 * uh, axis=0, keepdims=True), jnp.sum(up1 * uh, axis=0, keepdims=True),
                 jnp.sum(duc * uh, axis=0, keepdims=True)], axis=0)

    return pl.pallas_call(
        body, name="ffn_act_bwd", grid=(N_CB,),
        in_specs=[col(0), col(N_CB), col(0), w(0), w(N_CB), b(0), b(N_CB)],
        out_specs=[both(SEQ), both(3), both(1)],
        out_shape=[jax.ShapeDtypeStruct((2, SEQ, D_FF), BF16), jax.ShapeDtypeStruct((2, 3, D_FF), F32),
                   jax.ShapeDtypeStruct((2, 1, D_FF), F32)],
        compiler_params=_cp(("parallel",)),
    )(u, u, dact, conv_w, conv_w, conv_b, conv_b)


def _t5_onehot():
    rel = (np.arange(BLOCK)[:, None] + BLOCK) - np.arange(2 * BLOCK)[None, :]
    n = np.maximum(rel, 0)
    max_exact = N_BUCKETS // 2
    large = max_exact + (np.log(np.maximum(n, 1).astype(np.float32) / np.float32(max_exact))
                         / np.float32(math.log(MAX_DISTANCE / max_exact))
                         * np.float32(N_BUCKETS - max_exact)).astype(np.int32)
    large = np.minimum(large, N_BUCKETS - 1)
    bucket = np.where(n < max_exact, n, large).reshape(-1)
    return (bucket[None, :] == np.arange(N_BUCKETS)[:, None]).astype(np.float32)


N_REL = BLOCK * 2 * BLOCK


def _bias_table(rel_bias_t, onehot):
    def body(rb_ref, oh_ref, o_ref):
        o_ref[...] = _dot_ind(rb_ref[...], oh_ref[...])

    return pl.pallas_call(
        body, name="bias_table", grid=(1,),
        in_specs=[_const((N_Q_HEADS, N_BUCKETS)), _const((N_BUCKETS, N_REL))],
        out_specs=_const((N_Q_HEADS, N_REL)),
        out_shape=jax.ShapeDtypeStruct((N_Q_HEADS, N_REL), F32),
        compiler_params=_cp(("arbitrary",)),
    )(rel_bias_t, onehot)


def _bias_table_bwd(dbias, onehot):
    def body(db_ref, oh_ref, o_ref):
        acc = None
        for part in _split(db_ref[...], 3):
            t = _dot(part, oh_ref[...], NT)
            acc = t if acc is None else acc + t
        o_ref[...] = acc

    return pl.pallas_call(
        body, name="bias_table_bwd", grid=(1,),
        in_specs=[_const((N_Q_HEADS, N_REL)), _const((N_BUCKETS, N_REL))],
        out_specs=_const((N_Q_HEADS, N_BUCKETS)),
        out_shape=jax.ShapeDtypeStruct((N_Q_HEADS, N_BUCKETS), F32),
        compiler_params=_cp(("arbitrary",)),
    )(dbias, onehot)


def _attn_pieces(n, q, kvp, kvc, bias_ref, sinks_ref, hk):
    qi = lax.broadcasted_iota(jnp.int32, (BLOCK, 2 * BLOCK), 0)
    kj = lax.broadcasted_iota(jnp.int32, (BLOCK, 2 * BLOCK), 1)
    rel = qi + BLOCK - kj
    first_key = jnp.where(n > 0, 0, BLOCK)
    ok = jnp.where(rel >= 0, jnp.where(rel < BLOCK, jnp.where(kj >= first_key, 1.0, 0.0), 0.0), 0.0)
    ok4 = jnp.concatenate([ok] * Q_PER_KV, axis=0) > 0.5
    c0 = hk * HEAD_DIM
    kcat = jnp.concatenate([kvp[:, c0:c0 + HEAD_DIM], kvc[:, c0:c0 + HEAD_DIM]], axis=0).astype(BF16)
    vcat = jnp.concatenate([kvp[:, D_KV + c0:D_KV + c0 + HEAD_DIM], kvc[:, D_KV + c0:D_KV + c0 + HEAD_DIM]],
                           axis=0).astype(BF16)
    q0 = hk * Q_PER_KV * HEAD_DIM
    qs = jnp.concatenate([q[:, q0 + g * HEAD_DIM:q0 + (g + 1) * HEAD_DIM] for g in range(Q_PER_KV)],
                         axis=0).astype(BF16)
    s = _dot(qs, kcat, NT) * (HEAD_DIM ** -0.5) + bias_ref[hk]
    s = jnp.where(ok4, s, NEG_INF)
    row = lax.broadcasted_iota(jnp.int32, (Q_PER_KV * BLOCK, 1), 0)
    sink = jnp.zeros((Q_PER_KV * BLOCK, 1), F32)
    for g in range(Q_PER_KV):
        sink = jnp.where((row >> 7) == g, sinks_ref[hk * Q_PER_KV + g], sink)
    m = jnp.maximum(jnp.max(s, axis=-1, keepdims=True), sink)
    p = jnp.exp(s - m)
    es = jnp.exp(sink - m)
    inv = 1.0 / (jnp.sum(p, axis=-1, keepdims=True) + es)
    return qs, kcat, vcat, p * inv, es * inv


def _attn_in_specs():
    return [pl.BlockSpec((BLOCK, D_ATTN), lambda n: (n, 0)),
            pl.BlockSpec((BLOCK, 2 * D_KV), lambda n: (jnp.maximum(n - 1, 0), D_ATTN // (2 * D_KV))),
            pl.BlockSpec((BLOCK, 2 * D_KV), lambda n: (n, D_ATTN // (2 * D_KV))),
            _const((N_KV_HEADS, Q_PER_KV * BLOCK, 2 * BLOCK)),
            pl.BlockSpec(memory_space=pltpu.SMEM)]


def _unstack_heads(t):
    return jnp.concatenate([t[g * BLOCK:(g + 1) * BLOCK] for g in range(Q_PER_KV)], axis=1)


def _attn_fwd(proj, bias, sinks):
    def body(q_ref, kvp_ref, kvc_ref, bias_ref, sinks_ref, o_ref):
        n = pl.program_id(0)
        q, kvp, kvc = q_ref[...], kvp_ref[...], kvc_ref[...]
        outs = []
        for hk in range(N_KV_HEADS):
            _, _, vcat, probs, _ = _attn_pieces(n, q, kvp, kvc, bias_ref, sinks_ref, hk)
            outs.append(_unstack_heads(_dot(probs.astype(BF16), vcat)))
        o_ref[...] = jnp.concatenate(outs, axis=1)

    return pl.pallas_call(
        body, name="attn_fwd", grid=(SEQ // BLOCK,),
        in_specs=_attn_in_specs(),
        out_specs=pl.BlockSpec((BLOCK, D_ATTN), lambda n: (n, 0)),
        out_shape=jax.ShapeDtypeStruct((SEQ, D_ATTN), F32),
        compiler_params=_cp(("parallel",)),
    )(proj, proj, proj, bias, sinks)


def _attn_bwd(proj, bias, sinks, dcat):
    nb = SEQ // BLOCK

    def body(q_ref, kvp_ref, kvc_ref, bias_ref, sinks_ref, do_ref, dq_ref, dkv_ref, dbias_ref, dsink_ref, dsacc):
        n = pl.program_id(0)

        @pl.when(n == 0)
        def _():
            dkv_ref[...] = jnp.zeros_like(dkv_ref)
            dbias_ref[...] = jnp.zeros_like(dbias_ref)
            dsacc[...] = jnp.zeros_like(dsacc)

        q, kvp, kvc = q_ref[...], kvp_ref[...], kvc_ref[...]
        do_all = do_ref[...]
        dqs, dks, dvs = [], [], []
        for hk in range(N_KV_HEADS):
            qs, kcat, vcat, probs, psink = _attn_pieces(n, q, kvp, kvc, bias_ref, sinks_ref, hk)
            q0 = hk * Q_PER_KV * HEAD_DIM
            do = jnp.concatenate([do_all[:, q0 + g * HEAD_DIM:q0 + (g + 1) * HEAD_DIM] for g in range(Q_PER_KV)],
                                 axis=0).astype(BF16)
            dprobs = _dot(do, vcat, NT)
            dvs.append(_dot(probs.astype(BF16), do, TN))
            rowdot = jnp.sum(probs * dprobs, axis=-1, keepdims=True)
            ds = probs * (dprobs - rowdot)
            dsacc[hk] += -psink * rowdot
            dbias_ref[hk] += ds
            dsb = (ds * (HEAD_DIM ** -0.5)).astype(BF16)
            dqs.append(_unstack_heads(_dot(dsb, kcat)))
            dks.append(_dot(dsb, qs, TN))
        dq_ref[...] = jnp.concatenate(dqs, axis=1)
        upd = jnp.concatenate(dks + dvs, axis=1)
        cur = pl.multiple_of(n * BLOCK, BLOCK)
        dkv_ref[pl.ds(cur, BLOCK), :] += upd[BLOCK:]

        @pl.when(n > 0)
        def _():
            prev = pl.multiple_of((n - 1) * BLOCK, BLOCK)
            dkv_ref[pl.ds(prev, BLOCK), :] += upd[:BLOCK]

        @pl.when(n == nb - 1)
        def _():
            for hk in range(N_KV_HEADS):
                for g in range(Q_PER_KV):
                    tot = jnp.sum(dsacc[hk, g * BLOCK:(g + 1) * BLOCK, :], axis=0, keepdims=True)
                    h = hk * Q_PER_KV + g
                    dsink_ref[h:h + 1, :] = jnp.broadcast_to(tot, (1, LANES))

    return pl.pallas_call(
        body, name="attn_bwd", grid=(nb,),
        in_specs=_attn_in_specs() + [pl.BlockSpec((BLOCK, D_ATTN), lambda n: (n, 0))],
        out_specs=[pl.BlockSpec((BLOCK, D_ATTN), lambda n: (n, 0)), _const((SEQ, 2 * D_KV)),
                   _const((N_KV_HEADS, Q_PER_KV * BLOCK, 2 * BLOCK)), _const((N_Q_HEADS, LANES))],
        out_shape=[jax.ShapeDtypeStruct((SEQ, D_ATTN), F32), jax.ShapeDtypeStruct((SEQ, 2 * D_KV), F32),
                   jax.ShapeDtypeStruct((N_KV_HEADS, Q_PER_KV * BLOCK, 2 * BLOCK), F32),
                   jax.ShapeDtypeStruct((N_Q_HEADS, LANES), F32)],
        scratch_shapes=[pltpu.VMEM((N_KV_HEADS, Q_PER_KV * BLOCK, 1), F32)],
        compiler_params=_cp(("arbitrary",)),
    )(proj, proj, proj, bias, sinks, dcat)


@jax.custom_vjp
def _head_sum(x):
    return _dot_ind(x, _head_ones(x.shape[-1]))


_head_sum.defvjp(lambda x: (_head_sum(x), None), lambda _, ct: (_head_sum(ct),))


@jax.custom_vjp
def _bdot(a, w):
    return _dot(a.astype(BF16), w.astype(BF16))


def _bdot_bwd(res, ct):
    a, w = res
    ctb = ct.astype(BF16)
    return _dot(ctb, w.astype(BF16), NT), _dot(a.astype(BF16), ctb, TN)


_bdot.defvjp(lambda a, w: (_bdot(a, w), (a, w)), _bdot_bwd)


def _sigmoid(x):
    return 0.5 * (jnp.tanh(0.5 * x) + 1.0)


def _softplus(x):
    return jnp.maximum(x, 0.0) + jnp.log(1.0 + jnp.exp(-jnp.abs(x)))


def _rwkv_core(r, k, v, zwa, zg, w0, wdu, a0, wiu, wgu, k_k, k_a):
    w_log = -_softplus(-(w0 + _bdot(jnp.tanh(zwa), wdu))) - 0.5
    decay = jnp.exp(-jnp.exp(w_log))
    a = _sigmoid(a0 + _bdot(zwa, wiu))
    g = _bdot(_sigmoid(zg), wgu)
    kk = k * k_k
    kk = kk / jnp.maximum(jnp.sqrt(_head_sum(kk * kk)), 1e-12)
    k2 = k * (1.0 + (a - 1.0) * k_a)
    return r, decay, k2, v, -kk, kk * a, g


def _rwkv_out(o, r, k2, v, g, lng, lnb, rk):
    mu = _head_sum(o) * (1.0 / HEAD_DIM)
    d = o - mu
    var = _head_sum(d * d) * (1.0 / HEAD_DIM)
    on = d * lax.rsqrt(var + GN_EPS) * lng + lnb
    bonus = _head_sum(r * k2 * rk) * v
    return (on + bonus) * g


P_SPLITS = (0, 512, 1024, 1536, 1664, 1792)
N_PREP_PARAMS = 7
HALO = 8


def _shifted_pieces(i, p_ref, halo_ref, mix_ref):
    p = p_ref[:, P_OFF:]
    prev_row = halo_ref[HALO - 1:HALO, P_OFF:] * jnp.where(i > 0, 1.0, 0.0)
    row = lax.broadcasted_iota(jnp.int32, p.shape, 0)
    pprev = jnp.where(row == 0, prev_row, pltpu.roll(p, 1, 0))
    delta = pprev - p
    ps = p + delta * mix_ref[...]
    return [ps[:, a:b] for a, b in zip(P_SPLITS[:-1], P_SPLITS[1:])], delta


def _prep_in_specs():
    return [_rows(TR, D_IN),
            pl.BlockSpec((HALO, D_IN), lambda i: (jnp.maximum(i * (TR // HALO) - 1, 0), 0)),
            _const((1, RWKV_COLS)), _const((1, D_RWKV)), _const((LANES, D_RWKV)), _const((1, D_RWKV)),
            _const((LANES, D_RWKV)), _const((LANES, D_RWKV)), _const((1, D_RWKV)), _const((1, D_RWKV))]


def _rwkv_prep(proj, mix, prm):
    def body(p_ref, halo_ref, mix_ref, *refs):
        prm_refs, outs = refs[:N_PREP_PARAMS], refs[N_PREP_PARAMS:]
        pieces, _ = _shifted_pieces(pl.program_id(0), p_ref, halo_ref, mix_ref)
        vals = _rwkv_core(*pieces, *[t[...] for t in prm_refs])
        for ref, val in zip(outs, vals):
            ref[...] = val

    return pl.pallas_call(
        body, name="rwkv_prep", grid=(SEQ // TR,),
        in_specs=_prep_in_specs(),
        out_specs=[_rows(TR, D_RWKV)] * 7,
        out_shape=[jax.ShapeDtypeStruct((SEQ, D_RWKV), F32)] * 7,
        compiler_params=_cp(("parallel",)),
    )(proj, proj, mix, *prm)


def _rwkv_prep_bwd(proj, mix, prm, cts):
    def body(p_ref, halo_ref, mix_ref, *refs):
        i = pl.program_id(0)
        prm_refs = refs[:N_PREP_PARAMS]
        ct_refs = refs[N_PREP_PARAMS:N_PREP_PARAMS + 10]
        dps_ref, dmix_ref = refs[N_PREP_PARAMS + 10:N_PREP_PARAMS + 12]
        dprm_refs = refs[N_PREP_PARAMS + 12:]
        pieces, delta = _shifted_pieces(i, p_ref, halo_ref, mix_ref)
        _, vjp = jax.vjp(_rwkv_core, *pieces, *[t[...] for t in prm_refs])
        dr1, dr2, dw, dk1, dk2, dv1, dv2, dkkn, db, dg = [t[...] for t in ct_refs]
        grads = vjp((dr1 + dr2, dw, dk1 + dk2, dv1 + dv2, dkkn, db, dg))
        dps = jnp.concatenate(grads[:5], axis=1)
        dps_ref[...] = dps

        @pl.when(i == 0)
        def _():
            dmix_ref[...] = jnp.zeros_like(dmix_ref)
            for ref in dprm_refs:
                ref[...] = jnp.zeros_like(ref)

        dmix_ref[...] += jnp.sum(dps * delta, axis=0, keepdims=True)
        for ref, gval in zip(dprm_refs, grads[5:]):
            ref[...] += gval

    prm_shapes = [(1, D_RWKV), (LANES, D_RWKV), (1, D_RWKV), (LANES, D_RWKV), (LANES, D_RWKV), (1, D_RWKV), (1, D_RWKV)]
    return pl.pallas_call(
        body, name="rwkv_prep_bwd", grid=(SEQ // TR,),
        in_specs=_prep_in_specs() + [_rows(TR, D_RWKV)] * 10,
        out_specs=[_rows(TR, RWKV_COLS), _const((1, RWKV_COLS))] + [_const(s) for s in prm_shapes],
        out_shape=[jax.ShapeDtypeStruct((SEQ, RWKV_COLS), F32), jax.ShapeDtypeStruct((1, RWKV_COLS), F32)]
        + [jax.ShapeDtypeStruct(s, F32) for s in prm_shapes],
        compiler_params=_cp(("arbitrary",)),
    )(proj, proj, mix, *prm, *cts)


def _rwkv_post(o, r, k2, v, g, lng, lnb, rk, attn):
    def body(o_ref, r_ref, k_ref, v_ref, g_ref, lng_ref, lnb_ref, rk_ref, attn_ref, cat_ref):
        rw = _rwkv_out(*[t[...] for t in (o_ref, r_ref, k_ref, v_ref, g_ref, lng_ref, lnb_ref, rk_ref)])
        cat_ref[...] = jnp.concatenate([attn_ref[...], rw], axis=1).astype(BF16)

    return pl.pallas_call(
        body, name="rwkv_post", grid=(SEQ // TR,),
        in_specs=[_rows(TR, D_RWKV)] * 5 + [_const((1, D_RWKV))] * 3 + [_rows(TR, D_ATTN)],
        out_specs=_rows(TR, D_MODEL),
        out_shape=jax.ShapeDtypeStruct((SEQ, D_MODEL), BF16),
        compiler_params=_cp(("parallel",)),
    )(o, r, k2, v, g, lng, lnb, rk, attn)


def _rwkv_post_bwd(o, r, k2, v, g, lng, lnb, rk, dcat):
    def body(o_ref, r_ref, k_ref, v_ref, g_ref, lng_ref, lnb_ref, rk_ref, dcat_ref,
             do_ref, dr_ref, dk_ref, dv_ref, dg_ref, dlng_ref, dlnb_ref, drk_ref):
        i = pl.program_id(0)
        args = [t[...] for t in (o_ref, r_ref, k_ref, v_ref, g_ref, lng_ref, lnb_ref, rk_ref)]
        _, vjp = jax.vjp(_rwkv_out, *args)
        grads = vjp(dcat_ref[:, D_ATTN:])
        for ref, gval in zip((do_ref, dr_ref, dk_ref, dv_ref, dg_ref), grads[:5]):
            ref[...] = gval

        @pl.when(i == 0)
        def _():
            for ref in (dlng_ref, dlnb_ref, drk_ref):
                ref[...] = jnp.zeros_like(ref)

        for ref, gval in zip((dlng_ref, dlnb_ref, drk_ref), grads[5:]):
            ref[...] += gval

    return pl.pallas_call(
        body, name="rwkv_post_bwd", grid=(SEQ // TR,),
        in_specs=[_rows(TR, D_RWKV)] * 5 + [_const((1, D_RWKV))] * 3 + [_rows(TR, D_MODEL)],
        out_specs=[_rows(TR, D_RWKV)] * 5 + [_const((1, D_RWKV))] * 3,
        out_shape=[jax.ShapeDtypeStruct((SEQ, D_RWKV), F32)] * 5 + [jax.ShapeDtypeStruct((1, D_RWKV), F32)] * 3,
        compiler_params=_cp(("arbitrary",)),
    )(o, r, k2, v, g, lng, lnb, rk, dcat)


def _assemble_dproj(dq, dkv, dps, mix):
    last = SEQ // HALO - 1

    def body(dq_ref, dkv_ref, dps_ref, nxt_ref, mix_ref, o_ref):
        i = pl.program_id(0)
        dps = dps_ref[...]
        mixv = mix_ref[...]
        nxt_row = nxt_ref[0:1, :] * jnp.where(i < SEQ // TR - 1, 1.0, 0.0)
        row = lax.broadcasted_iota(jnp.int32, dps.shape, 0)
        up = jnp.where(row == TR - 1, nxt_row, pltpu.roll(dps, TR - 1, 0))
        dp = dps * (1.0 - mixv) + up * mixv
        o_ref[...] = jnp.concatenate([dq_ref[...], dkv_ref[...], dp], axis=1).astype(BF16)

    return pl.pallas_call(
        body, name="assemble_dproj", grid=(SEQ // TR,),
        in_specs=[_rows(TR, D_ATTN), _rows(TR, 2 * D_KV), _rows(TR, RWKV_COLS),
                  pl.BlockSpec((HALO, RWKV_COLS), lambda i: (jnp.minimum((i + 1) * (TR // HALO), last), 0)),
                  _const((1, RWKV_COLS))],
        out_specs=_rows(TR, D_IN),
        out_shape=jax.ShapeDtypeStruct((SEQ, D_IN), BF16),
        compiler_params=_cp(("parallel",)),
    )(dq, dkv, dps, dps, mix)


N_PAIR = D_RWKV // LANES
CHUNK = 64
N_CHUNK = SEQ // CHUNK
GROUP = 8
STATE = (N_PAIR, HEAD_DIM, LANES)


def _lane_sums(lhs_tiles, ones2):
    out = _dot(jnp.concatenate(lhs_tiles, axis=0), ones2)
    return [out[i * HEAD_DIM:(i + 1) * HEAD_DIM] for i in range(len(lhs_tiles))]


def _seg_sum(xs, ones2):
    return _lane_sums([jnp.concatenate(_split(x, 2), axis=1) for x in xs], ones2)


def _seg_sum_rows(xs, ones2):
    out = _dot(jnp.concatenate(_split(jnp.concatenate(xs, axis=0), 2), axis=1), ones2)
    return [out[i * GROUP:(i + 1) * GROUP] for i in range(len(xs))]


def _col_form(rows, diag, ones2):
    zero = jnp.zeros((HEAD_DIM, LANES), BF16)
    tiles = []
    for row in rows:
        hi = row.astype(BF16)
        lo = (row - hi.astype(F32)).astype(BF16)
        tiles.append(jnp.concatenate(
            [jnp.where(diag, jnp.broadcast_to(part, (HEAD_DIM, LANES)), zero) for part in (hi, lo)], axis=1))
    return _lane_sums(tiles, ones2)


def _scan_consts():
    ones2 = jnp.concatenate([_head_ones(LANES)] * 2, axis=0)
    sub = lax.broadcasted_iota(jnp.int32, (HEAD_DIM, LANES), 0)
    lane_in_head = lax.broadcasted_iota(jnp.int32, (HEAD_DIM, LANES), 1) & (HEAD_DIM - 1)
    return ones2, lane_in_head == sub, lane_in_head


def _rows_of_columns(tile):
    t = tile.T
    return jnp.concatenate([t[:CHUNK], t[HEAD_DIM:HEAD_DIM + CHUNK]], axis=1)


def _pair(j):
    return slice(j * LANES, (j + 1) * LANES)


def _scan_fwd(r, w, k, v, kkn, b):
    def body(r_ref, w_ref, k_ref, v_ref, kkn_ref, b_ref, o_ref, st_ref, sa_ref, s_scr):
        c = pl.program_id(0)
        ones2, diag, lane_in_head = _scan_consts()

        @pl.when(c == 0)
        def _():
            s_scr[...] = jnp.zeros_like(s_scr)

        def group(gi, carry):
            row0 = pl.multiple_of(gi * GROUP, GROUP)
            states, ocols = list(carry[:N_PAIR]), list(carry[N_PAIR:])
            tiles = [[t[pl.ds(row0, GROUP), _pair(j)] for t in (r_ref, w_ref, k_ref, v_ref, kkn_ref, b_ref)]
                     for j in range(N_PAIR)]
            def row(j, name, u):
                return tiles[j]["rwkvnb".index(name)][u:u + 1]

            def emit_out(u, after):
                outs = _seg_sum([s[j] * row(j, "r", u + d) for d, s in enumerate(after) for j in range(N_PAIR)], ones2)
                for d in range(2):
                    here = lane_in_head == gi * GROUP + u + d
                    for j in range(N_PAIR):
                        ocols[j] = jnp.where(here, outs[d * N_PAIR + j], ocols[j])

            def vcols_of(u):
                cols = _col_form([row(j, "v", u + d) for d in range(2) for j in range(N_PAIR)], diag, ones2)
                return cols[:N_PAIR], cols[N_PAIR:]

            n_next = [pltpu.roll(tiles[j][4], GROUP - 1, 0) for j in range(N_PAIR)]
            dots = _seg_sum_rows([tiles[j][5] * n_next[j] for j in range(N_PAIR)]
                                 + [tiles[j][2] * n_next[j] for j in range(N_PAIR)], ones2)
            b_n, k_n = dots[:N_PAIR], dots[N_PAIR:]
            w_n = [tiles[j][1] * n_next[j] for j in range(N_PAIR)]

            vcols = vcols_of(0)
            after = None
            for u in range(0, GROUP, 2):
                prods = _seg_sum([states[j] * row(j, "n", u) for j in range(N_PAIR)]
                                 + [states[j] * w_n[j][u:u + 1] for j in range(N_PAIR)], ones2)
                if after is not None:
                    emit_out(u - 2, after)
                nxt = vcols_of(u + 2) if u + 2 < GROUP else None
                first, second = [], []
                for j in range(N_PAIR):
                    sa1 = prods[j]
                    sa2 = prods[N_PAIR + j] + sa1 * b_n[j][u:u + 1] + vcols[0][j] * k_n[j][u:u + 1]
                    s1 = states[j] * row(j, "w", u) + sa1 * row(j, "b", u) + vcols[0][j] * row(j, "k", u)
                    s2 = s1 * row(j, "w", u + 1) + sa2 * row(j, "b", u + 1) + vcols[1][j] * row(j, "k", u + 1)
                    st_ref[row0 + u, j] = s1
                    sa_ref[row0 + u, j] = sa1
                    st_ref[row0 + u + 1, j] = s2
                    sa_ref[row0 + u + 1, j] = sa2
                    first.append(s1)
                    second.append(s2)
                    states[j] = s2
                after, vcols = (first, second), nxt
            emit_out(GROUP - 2, after)
            return tuple(states + ocols)

        zero = jnp.zeros((HEAD_DIM, LANES), F32)
        fin = lax.fori_loop(0, CHUNK // GROUP, group, tuple(s_scr[j] for j in range(N_PAIR)) + (zero,) * N_PAIR)
        for j in range(N_PAIR):
            s_scr[j] = fin[j]
            o_ref[:, _pair(j)] = _rows_of_columns(fin[N_PAIR + j])

    blk = pl.BlockSpec((CHUNK, D_RWKV), lambda c: (c, 0))
    per_step = pl.BlockSpec((CHUNK,) + STATE, lambda c: (c, 0, 0, 0))
    return pl.pallas_call(
        body, name="rwkv_scan_fwd", grid=(N_CHUNK,),
        in_specs=[blk] * 6,
        out_specs=[blk, per_step, per_step],
        out_shape=[jax.ShapeDtypeStruct((SEQ, D_RWKV), F32)] + [jax.ShapeDtypeStruct((SEQ,) + STATE, F32)] * 2,
        scratch_shapes=[pltpu.VMEM(STATE, F32)],
        compiler_params=_cp(("arbitrary",)),
    )(r, w, k, v, kkn, b)


def _scan_bwd(r, w, k, v, kkn, b, do, states, sas, ds_in, prev, name, first_chunk, n_chunks):
    top = first_chunk + n_chunks - 1

    def body(r_ref, w_ref, k_ref, v_ref, kkn_ref, b_ref, do_ref, st_ref, before_ref, sa_ref, ds_in_ref, *rest):
        dr_ref, dw_ref, dk_ref, dv_ref, dkkn_ref, db_ref, ds_out_ref, ds_scr = rest[-8:]
        i = pl.program_id(0)
        ones2, diag, lane_in_head = _scan_consts()

        @pl.when(i == 0)
        def _():
            ds_scr[...] = ds_in_ref[...]

        entry = [before_ref[0, j] * jnp.where(i < top, 1.0, 0.0) for j in range(N_PAIR)]

        def reverse(gr, carry):
            gi = CHUNK // GROUP - 1 - gr
            row0 = pl.multiple_of(gi * GROUP, GROUP)
            dstates, dvcols = list(carry[:N_PAIR]), list(carry[N_PAIR:])
            tiles = [[t[pl.ds(row0, GROUP), _pair(j)]
                      for t in (r_ref, w_ref, k_ref, v_ref, kkn_ref, b_ref, do_ref)] for j in range(N_PAIR)]
            rows = [[[None] * GROUP for _ in range(5)] for _ in range(N_PAIR)]

            def row(j, name, u):
                return tiles[j]["rwkvnbd".index(name)][u:u + 1]

            def cols_of(u):
                both = _col_form([row(j, "d", u) for j in range(N_PAIR)] + [row(j, "v", u) for j in range(N_PAIR)],
                                 diag, ones2)
                return [(both[j], both[N_PAIR + j]) for j in range(N_PAIR)]

            def emit_dv(u, dsp):
                here = lane_in_head == gi * GROUP + u
                outs = _seg_sum([dsp[j] * row(j, "k", u) for j in range(N_PAIR)], ones2)
                for j in range(N_PAIR):
                    dvcols[j] = jnp.where(here, outs[j], dvcols[j])

            cols = cols_of(GROUP - 1)
            before = None
            for u in reversed(range(GROUP)):
                tl = gi * GROUP + u
                dsp = [dstates[j] + cols[j][0] * row(j, "r", u) for j in range(N_PAIR)]
                dsas = _seg_sum([dsp[j] * row(j, "b", u) for j in range(N_PAIR)], ones2)
                if before is not None:
                    emit_dv(u + 1, before)
                nxt = cols_of(u - 1) if u > 0 else None
                for j in range(N_PAIR):
                    if u > 0:
                        s_prev = st_ref[tl - 1, j]
                    else:
                        s_prev = jnp.where(gi == 0, entry[j], st_ref[jnp.maximum(tl - 1, 0), j])
                    docol, vcol = cols[j]
                    rows[j][0][u] = jnp.sum(st_ref[tl, j] * docol, axis=0, keepdims=True)
                    rows[j][1][u] = jnp.sum(dsp[j] * s_prev, axis=0, keepdims=True)
                    rows[j][2][u] = jnp.sum(dsp[j] * vcol, axis=0, keepdims=True)
                    rows[j][3][u] = jnp.sum(s_prev * dsas[j], axis=0, keepdims=True)
                    rows[j][4][u] = jnp.sum(dsp[j] * sa_ref[tl, j], axis=0, keepdims=True)
                    dstates[j] = dsp[j] * row(j, "w", u) + dsas[j] * row(j, "n", u)
                before, cols = dsp, nxt
            emit_dv(0, before)
            for j in range(N_PAIR):
                for ref, rr in zip((dr_ref, dw_ref, dk_ref, dkkn_ref, db_ref), rows[j]):
                    ref[pl.ds(row0, GROUP), _pair(j)] = jnp.concatenate(rr, axis=0)
            return tuple(dstates + dvcols)

        zero = jnp.zeros((HEAD_DIM, LANES), F32)
        dfin = lax.fori_loop(0, CHUNK // GROUP, reverse, tuple(ds_scr[j] for j in range(N_PAIR)) + (zero,) * N_PAIR)
        for j in range(N_PAIR):
            ds_scr[j] = dfin[j]
            dv_ref[:, _pair(j)] = _rows_of_columns(dfin[N_PAIR + j])

        @pl.when(i == n_chunks - 1)
        def _():
            ds_out_ref[...] = ds_scr[...]

    blk = pl.BlockSpec((CHUNK, D_RWKV), lambda i: (top - i, 0))
    per_step = pl.BlockSpec((CHUNK,) + STATE, lambda i: (top - i, 0, 0, 0))
    step_before = pl.BlockSpec((1,) + STATE, lambda i: (jnp.maximum((top - i) * CHUNK - 1, 0), 0, 0, 0))
    prev = [] if prev is None else list(prev)
    outs = pl.pallas_call(
        body, name=name, grid=(n_chunks,),
        in_specs=[blk] * 7 + [per_step, step_before, per_step, _const(STATE)] + [ANY] * len(prev),
        out_specs=[blk] * 6 + [_const(STATE)],
        out_shape=[jax.ShapeDtypeStruct((SEQ, D_RWKV), F32)] * 6 + [jax.ShapeDtypeStruct(STATE, F32)],
        scratch_shapes=[pltpu.VMEM(STATE, F32)],
        input_output_aliases={11 + t: t for t in range(len(prev))},
        compiler_params=_cp(("arbitrary",)),
    )(r, w, k, v, kkn, b, do, states, states, sas, ds_in, *prev)
    return outs[:6], outs[6]


def _stacked(rows, cols, pick):
    return pl.BlockSpec((None, rows, cols), pick)


def _local_step(x, target, sm, win_st):
    def tied(t, token):
        return t if token is None else t + token[0:1, 0:1].reshape((1,) * t.ndim)

    zpad = jnp.zeros((LORA_DECAY, D_RWKV), F32)
    prm = [sm["w0"], jnp.concatenate([sm["w_decay_up"], zpad], axis=0), sm["a0"],
           jnp.concatenate([zpad, sm["w_iclr_up"]], axis=0), sm["w_gate_up"], sm["k_k"], sm["k_a"]]
    mix = sm["rwkv_shift_mix"]
    onehot = jnp.asarray(_t5_onehot(), BF16)
    sinks = sm["sinks"].reshape(N_Q_HEADS)
    lng, lnb, rk = sm["ln_x_g"], sm["ln_x_b"], sm["r_k"].reshape(1, D_RWKV)

    h1 = _norm_cast(x, sm["norm_mix_pre"], "norm_in")
    proj = _matmul(h1, win_st, "nn", "proj", m=SEQ, n=D_IN, k=D_MODEL, tm=SEQ, tn=640, tk=D_MODEL,
                   b_spec=_stacked(D_MODEL, 640, lambda i, j, kk: (j, 0, 0)))
    bias = _bias_table(sm["rel_bias"].T, onehot).reshape(N_KV_HEADS, Q_PER_KV * BLOCK, 2 * BLOCK)
    attn = _attn_fwd(proj, bias, sinks)
    r, w, k2, v, kkn, b, g = _rwkv_prep(proj, mix, prm)
    o, states, sas = _scan_fwd(r, w, k2, v, kkn, b)
    wout, wup_st, wdown = yield ("rest_weights", o)
    cat = _rwkv_post(o, r, k2, v, g, lng, lnb, rk, attn)
    mixo = _matmul(cat, wout, "nn", "out_proj", m=SEQ, n=D_MODEL, k=D_MODEL, tm=SEQ, tn=512, tk=D_MODEL)
    x2, h3 = _mix_norm(x, mixo, sm["norm_mix_post"], sm["norm_ffn_pre"])
    u = _matmul(h3, wup_st, "nn", "ffn_up", m=SEQ, n=2 * D_FF, k=D_MODEL, tm=SEQ, tn=512, tk=D_MODEL,
                b_spec=_stacked(D_MODEL, 512, lambda i, j, kk: (j // 4, 0, j % 4)))
    act = _ffn_act(u, sm["conv_w"], sm["conv_b"])
    f = _matmul(act, wdown, "nn", "ffn_down", m=SEQ, n=D_MODEL, k=D_FF, tm=1024, tn=512, tk=2048)
    loss, dy, df, d_g4 = _loss_head(x2, f, sm["norm_ffn_post"], target)

    dact = _matmul(df, wdown, "nt", "d_act", m=SEQ, n=D_FF, k=D_MODEL, tm=SEQ, tn=512, tk=D_MODEL)
    d_wdown = _matmul(act, df, "tn", "d_wdown", m=D_FF, n=D_MODEL, k=SEQ, tm=512, tn=D_MODEL, tk=SEQ)
    du, d_convw, d_convb = _ffn_act_bwd(u, dact, sm["conv_w"], sm["conv_b"])
    d_convw = d_convw.transpose(1, 0, 2).reshape(3, 2 * D_FF)
    d_convb = d_convb.reshape(1, 2 * D_FF)
    dh3 = _matmul(du, wup_st, "nt", "d_h3", m=SEQ, n=D_MODEL, k=2 * D_FF, tm=1024, tn=D_MODEL, tk=2048,
                  a_spec=pl.BlockSpec((None, 1024, 2048), lambda i, j, kk: (kk // 2, i, kk % 2)),
                  b_spec=_stacked(D_MODEL, 2048, lambda i, j, kk: (kk, j, 0)))
    d_wup = _matmul(h3, du, "tn", "d_wup", m=D_MODEL, n=2 * D_FF, k=SEQ, tm=D_MODEL, tn=512, tk=SEQ,
                    b_spec=pl.BlockSpec((None, SEQ, 512), lambda i, j, kk: (j // 8, 0, j % 8)),
                    out=((N_CHIPS, D_MODEL, 2048), _stacked(D_MODEL, 512, lambda i, j, kk: (j // 4, 0, j % 4))))
    dx2, dmix, d_g2, d_g3 = _mid_bwd(x2, mixo, dy, dh3, sm["norm_mix_post"], sm["norm_ffn_pre"])
    dcat = _matmul(dmix, wout, "nt", "d_cat", m=SEQ, n=D_MODEL, k=D_MODEL, tm=SEQ, tn=512, tk=D_MODEL)
    d_wout = _matmul(cat, dmix, "tn", "d_wout", m=D_MODEL, n=D_MODEL, k=SEQ, tm=512, tn=D_MODEL, tk=SEQ)
    token = yield ("grads_a", (d_wdown, d_wup, d_wout))
    do, dr_p, dk_p, dv_p, dg, d_lng, d_lnb, d_rk = _rwkv_post_bwd(o, r, k2, v, g, lng, tied(lnb, token), rk, dcat)
    half = N_CHUNK // 2
    ds_end = jnp.zeros(STATE, F32)
    late, ds_mid = _scan_bwd(r, w, k2, v, kkn, b, do, states, sas, ds_end, None, "rwkv_scan_bwd_late", half, half)
    token = yield ("seam_1", ds_mid)
    scan_cts, ds_first = _scan_bwd(r, w, k2, v, kkn, b, do, states, sas, tied(ds_mid, token), late,
                                   "rwkv_scan_bwd_early", 0, half)
    dr_s, dw_s, dk_s, dv_s, dkkn_s, db_s = scan_cts
    token = yield ("seam_2", ds_first)
    prep_grads = _rwkv_prep_bwd(proj, tied(mix, token), prm,
                                (dr_s, dr_p, dw_s, dk_s, dk_p, dv_s, dv_p, dkkn_s, db_s, dg))
    dps, d_mix, d_w0, d_wdu, d_a0, d_wiu, d_wgu, d_kk, d_ka = prep_grads
    dq, dkv, dbias, dsink = _attn_bwd(proj, bias, sinks, dcat)
    d_relb = _bias_table_bwd(dbias.reshape(N_Q_HEADS, N_REL), onehot).T
    dproj = _assemble_dproj(dq, dkv, dps, mix)
    d_win = _matmul(h1, dproj, "tn", "d_win", m=D_MODEL, n=D_IN, k=SEQ, tm=D_MODEL, tn=640, tk=SEQ,
                    out=((N_CHIPS, D_MODEL, 640), _stacked(D_MODEL, 640, lambda i, j, kk: (j, 0, 0))))
    token = yield ("grads_b", d_win)
    dh1 = _matmul(dproj, win_st, "nt", "d_h1", m=SEQ, n=D_MODEL, k=D_IN, tm=1024, tn=D_MODEL, tk=640,
                  b_spec=_stacked(D_MODEL, 640, lambda i, j, kk: (kk, j, 0)))
    grad_x, d_g1 = _first_bwd(x, dx2, dh1, tied(sm["norm_mix_pre"], token))

    grads = {
        "norm_mix_pre": d_g1, "norm_mix_post": d_g2, "norm_ffn_pre": d_g3, "norm_ffn_post": d_g4,
        "w_in": d_win, "rel_bias": d_relb, "sinks": dsink[:, 0].reshape(1, N_Q_HEADS),
        "rwkv_shift_mix": d_mix, "w0": d_w0, "w_decay_up": d_wdu[:LORA_DECAY], "a0": d_a0,
        "w_iclr_up": d_wiu[LORA_DECAY:], "w_gate_up": d_wgu, "k_k": d_kk, "k_a": d_ka,
        "r_k": d_rk.reshape(1, N_Q_HEADS, HEAD_DIM), "ln_x_g": d_lng, "ln_x_b": d_lnb,
        "w_out": d_wout, "w_ffn_up": d_wup, "conv_w": d_convw, "conv_b": d_convb, "w_ffn_down": d_wdown,
    }
    return loss, grad_x, grads


def _place():
    x, y, c = lax.axis_index("x"), lax.axis_index("y"), lax.axis_index("c")
    chips = [(1 - x, y), (x, 1 - y), (1 - x, 1 - y)]
    return x, y, c, chips


def _remote(src, dst, sems, idx, to):
    return pltpu.make_async_remote_copy(src_ref=src, dst_ref=dst, send_sem=sems[0].at[idx], recv_sem=sems[1].at[idx],
                                        device_id=to, device_id_type=MESH)


def _half(c, rows):
    return pl.ds(pl.multiple_of(c * (rows // 2), 16), rows // 2)


def _gather_weights(big, small):
    nb, ns = len(big), len(small)

    def body(*refs):
        ins, outs = refs[:nb + ns], refs[nb + ns:2 * (nb + ns)]
        ici, d2d, sml, loc = refs[2 * (nb + ns):2 * (nb + ns) + 2], refs[-5:-3], refs[-3:-1], refs[-1]
        x, y, c, chips = _place()
        me = 2 * x + y
        sib = (x, y, 1 - c)
        local = [pltpu.make_async_copy(ins[a], outs[a].at[me], loc.at[a]) for a in range(nb + ns)]
        for cp in local:
            cp.start()
        sends = []
        for a in range(nb):
            rows = _half(c, big[a].shape[0])
            for kk, chip in enumerate(chips):
                sends.append(_remote(ins[a].at[rows], outs[a].at[me, rows], ici, a * 3 + kk, (*chip, c)))
        for a in range(ns):
            for kk, chip in enumerate(chips):
                sends.append(_remote(ins[nb + a], outs[nb + a].at[me], sml, a * 3 + kk, (*chip, c)))
        for cp in sends:
            cp.start()
        passed = []
        for a in range(nb):
            rows = _half(c, big[a].shape[0])
            for kk, (px, py) in enumerate(chips):
                got = outs[a].at[2 * px + py, rows]
                _remote(got, got, ici, a * 3 + kk, sib).wait_recv()
                fwd = _remote(got, got, d2d, a * 3 + kk, sib)
                fwd.start()
                passed.append(fwd)
        for a in range(nb):
            other = _half(1 - c, big[a].shape[0])
            for kk, (px, py) in enumerate(chips):
                land = outs[a].at[2 * px + py, other]
                _remote(land, land, d2d, a * 3 + kk, sib).wait_recv()
        for a in range(ns):
            for kk, (px, py) in enumerate(chips):
                land = outs[nb + a].at[2 * px + py]
                _remote(land, land, sml, a * 3 + kk, sib).wait_recv()
        for cp in sends + passed:
            cp.wait_send()
        for cp in local:
            cp.wait()

    arrs = list(big) + list(small)
    return pl.pallas_call(
        body, name="gather_weights",
        in_specs=[ANY] * len(arrs), out_specs=[ANY] * len(arrs),
        out_shape=[jax.ShapeDtypeStruct((N_CHIPS,) + t.shape, t.dtype) for t in arrs],
        scratch_shapes=[pltpu.SemaphoreType.DMA((3 * nb,)), pltpu.SemaphoreType.DMA((3 * nb,)),
                        pltpu.SemaphoreType.DMA((3 * nb,)), pltpu.SemaphoreType.DMA((3 * nb,)),
                        pltpu.SemaphoreType.DMA((3 * ns,)), pltpu.SemaphoreType.DMA((3 * ns,)),
                        pltpu.SemaphoreType.DMA((nb + ns,))],
        compiler_params=pltpu.CompilerParams(has_side_effects=True),
    )(*arrs)


def _allreduce_small(g):
    rows = g.shape[0]

    def body(g_ref, o_ref, buf, send, recv):
        x, y, c, _ = _place()
        me = 4 * x + 2 * y + c
        buf[me] = g_ref[...]
        sends = []
        for rel in range(1, N_DEV):
            px, py, pc = x ^ (rel >> 2), y ^ ((rel >> 1) & 1), c ^ (rel & 1)
            cp = _remote(g_ref, buf.at[me], (send, recv), rel - 1, (px, py, pc))
            cp.start()
            sends.append(cp)
        for rel in range(1, N_DEV):
            px, py, pc = x ^ (rel >> 2), y ^ ((rel >> 1) & 1), c ^ (rel & 1)
            land = buf.at[4 * px + 2 * py + pc]
            _remote(land, land, (send, recv), rel - 1, (px, py, pc)).wait_recv()
        acc = buf[0]
        for d in range(1, N_DEV):
            acc = acc + buf[d]
        o_ref[...] = acc
        for cp in sends:
            cp.wait_send()

    vm = pl.BlockSpec(memory_space=pltpu.VMEM)
    return pl.pallas_call(
        body, name="allreduce_small", in_specs=[vm], out_specs=vm,
        out_shape=jax.ShapeDtypeStruct((rows, LANES), F32),
        scratch_shapes=[pltpu.VMEM((N_DEV, rows, LANES), F32), pltpu.SemaphoreType.DMA((N_DEV - 1,)),
                        pltpu.SemaphoreType.DMA((N_DEV - 1,))],
        compiler_params=_cp(),
    )(g)


def _pair_exchange(gs):
    n = len(gs)

    def body(*refs):
        ins, got, mine, send, recv, loc = refs[:n], refs[n:2 * n], refs[2 * n:3 * n], refs[-3], refs[-2], refs[-1]
        x, y, c, _ = _place()
        sib = (x, y, 1 - c)
        cps, local = [], []
        for a in range(n):
            rows = gs[a].shape[1]
            cp = _remote(ins[a].at[:, _half(1 - c, rows)], got[a], (send, recv), a, sib)
            cp.start()
            cps.append(cp)
            lc = pltpu.make_async_copy(ins[a].at[:, _half(c, rows)], mine[a], loc.at[a])
            lc.start()
            local.append(lc)
        for a in range(n):
            cps[a].wait_recv()
        for a in range(n):
            cps[a].wait_send()
            local[a].wait()

    halves = [jax.ShapeDtypeStruct((N_CHIPS, t.shape[1] // 2, t.shape[2]), F32) for t in gs]
    outs = pl.pallas_call(
        body, name="grad_pair_exchange", in_specs=[ANY] * n, out_specs=[ANY] * (2 * n), out_shape=halves + halves,
        scratch_shapes=[pltpu.SemaphoreType.DMA((n,)), pltpu.SemaphoreType.DMA((n,)), pltpu.SemaphoreType.DMA((n,))],
        compiler_params=pltpu.CompilerParams(has_side_effects=True),
    )(*gs)
    return outs[:n], outs[n:]


def _chip_exchange(ps):
    n = len(ps)

    def body(*refs):
        ins, outs, send, recv, loc = refs[:n], refs[n:2 * n], refs[-3], refs[-2], refs[-1]
        x, y, c, chips = _place()
        me = 2 * x + y
        cps, local = [], []
        for a in range(n):
            lc = pltpu.make_async_copy(ins[a].at[me], outs[a].at[me], loc.at[a])
            lc.start()
            local.append(lc)
            for kk, (px, py) in enumerate(chips):
                cp = _remote(ins[a].at[2 * px + py], outs[a].at[me], (send, recv), a * 3 + kk, (px, py, c))
                cp.start()
                cps.append(cp)
        for a in range(n):
            for kk, (px, py) in enumerate(chips):
                land = outs[a].at[2 * px + py]
                _remote(land, land, (send, recv), a * 3 + kk, (px, py, c)).wait_recv()
        for cp in cps:
            cp.wait_send()
        for lc in local:
            lc.wait()

    return pl.pallas_call(
        body, name="grad_chip_exchange", in_specs=[ANY] * n, out_specs=[ANY] * n,
        out_shape=[jax.ShapeDtypeStruct(t.shape, F32) for t in ps],
        scratch_shapes=[pltpu.SemaphoreType.DMA((3 * n,)), pltpu.SemaphoreType.DMA((3 * n,)),
                        pltpu.SemaphoreType.DMA((n,))],
        compiler_params=pltpu.CompilerParams(has_side_effects=True),
    )(*ps)


def _pair_gather(hs):
    n = len(hs)

    def body(*refs):
        ins, outs, send, recv, loc = refs[:n], refs[n:2 * n], refs[-3], refs[-2], refs[-1]
        x, y, c, _ = _place()
        sib = (x, y, 1 - c)
        cps, local = [], []
        for a in range(n):
            rows = 2 * hs[a].shape[0]
            cp = _remote(ins[a], outs[a].at[_half(c, rows)], (send, recv), a, sib)
            cp.start()
            cps.append(cp)
            lc = pltpu.make_async_copy(ins[a], outs[a].at[_half(c, rows)], loc.at[a])
            lc.start()
            local.append(lc)
        for a in range(n):
            rows = 2 * hs[a].shape[0]
            land = outs[a].at[_half(1 - c, rows)]
            _remote(land, land, (send, recv), a, sib).wait_recv()
        for a in range(n):
            cps[a].wait_send()
            local[a].wait()

    return pl.pallas_call(
        body, name="grad_pair_gather", in_specs=[ANY] * n, out_specs=[ANY] * n,
        out_shape=[jax.ShapeDtypeStruct((2 * t.shape[0], t.shape[1]), F32) for t in hs],
        scratch_shapes=[pltpu.SemaphoreType.DMA((n,)), pltpu.SemaphoreType.DMA((n,)), pltpu.SemaphoreType.DMA((n,))],
        compiler_params=pltpu.CompilerParams(has_side_effects=True),
    )(*hs)


def _add2(a, b, name):
    r, cdim = a.shape
    tr = 256

    def body(a_ref, b_ref, o_ref):
        o_ref[...] = a_ref[...] + b_ref[...]

    return pl.pallas_call(
        body, name=name, grid=(r // tr,), in_specs=[_rows(tr, cdim)] * 2, out_specs=_rows(tr, cdim),
        out_shape=jax.ShapeDtypeStruct((r, cdim), F32), compiler_params=_cp(("parallel",)),
    )(a, b)


def _sum4(t, name):
    _, r, cdim = t.shape
    tr = 128

    def body(t_ref, o_ref):
        o_ref[...] = ((t_ref[0] + t_ref[1]) + t_ref[2]) + t_ref[3]

    return pl.pallas_call(
        body, name=name, grid=(r // tr,), in_specs=[pl.BlockSpec((N_CHIPS, tr, cdim), lambda i: (0, i, 0))],
        out_specs=_rows(tr, cdim), out_shape=jax.ShapeDtypeStruct((r, cdim), F32),
        compiler_params=_cp(("parallel",)),
    )(t)


def _reduce_big(gs):
    got, mine = _pair_exchange(gs)
    ps = [_add2(m.reshape(-1, m.shape[2]), g.reshape(-1, g.shape[2]), f"grad_pair_add_{i}").reshape(m.shape)
          for i, (m, g) in enumerate(zip(mine, got))]
    xs = _chip_exchange(ps)
    hs = [_sum4(t, f"grad_chip_sum_{i}") for i, t in enumerate(xs)]
    return _pair_gather(hs)


HBM = pl.BlockSpec(memory_space=pltpu.HBM)
SEM = pl.BlockSpec(memory_space=pltpu.SEMAPHORE)
EFFECT = pltpu.SideEffectType.DATAFLOW_SIDE_EFFECTING


def _copies_start(name, bufs, plan, n):
    nb = len(bufs)

    def body(*refs):
        ins, sems, token = refs[:nb], refs[nb:nb + 2 * n], refs[-1]
        for kk, (src, dst, dev) in enumerate(plan(ins)):
            pltpu.make_async_remote_copy(src_ref=src, dst_ref=dst, send_sem=sems[2 * kk], recv_sem=sems[2 * kk + 1],
                                         device_id=dev, device_id_type=MESH).start()
        token[...] = jnp.zeros_like(token)

    outs = pl.pallas_call(
        body, name=name,
        out_shape=tuple([pltpu.SemaphoreType.DMA(())] * (2 * n) + [pltpu.HBM(t.shape, t.dtype) for t in bufs]
                        + [jax.ShapeDtypeStruct((8, LANES), F32)]),
        in_specs=[HBM] * nb,
        out_specs=tuple([SEM] * (2 * n) + [HBM] * nb + [pl.BlockSpec(memory_space=pltpu.VMEM)]),
        input_output_aliases={t: 2 * n + t for t in range(nb)},
        compiler_params=pltpu.CompilerParams(has_side_effects=EFFECT),
    )(*[pltpu.with_memory_space_constraint(t, pltpu.HBM) for t in bufs])
    return outs[:2 * n], outs[2 * n:2 * n + nb], outs[-1]


def _copies_wait(name, sems, bufs, plan, n, after):
    nb = len(bufs)

    def body(*refs):
        ins, sem_refs = refs[:nb], refs[nb:nb + 2 * n]
        for kk, (src, dst, dev) in enumerate(plan(ins)):
            cp = pltpu.make_async_remote_copy(src_ref=src, dst_ref=dst, send_sem=sem_refs[2 * kk],
                                              recv_sem=sem_refs[2 * kk + 1], device_id=dev, device_id_type=MESH)
            cp.wait_send()
            cp.wait_recv()

    return pl.pallas_call(
        body, name=name,
        out_shape=tuple(pltpu.HBM(t.shape, t.dtype) for t in bufs),
        in_specs=[HBM] * nb + [SEM] * (2 * n) + [ANY],
        out_specs=tuple([HBM] * nb),
        input_output_aliases={t: t for t in range(nb)},
        compiler_params=pltpu.CompilerParams(has_side_effects=EFFECT),
    )(*bufs, *sems, after)


def _plan_gather(n_w):
    def plan(refs):
        x, y, c, chips = _place()
        me = 2 * x + y
        return [(refs[a], refs[n_w + a].at[me], (*chip, c)) for a in range(n_w) for chip in chips]
    return plan


def _plan_pair_halves(n_g, rows):
    def plan(refs):
        x, y, c, _ = _place()
        return [(refs[a].at[:, _half(1 - c, rows[a])], refs[n_g + a], (x, y, 1 - c)) for a in range(n_g)]
    return plan


def _plan_chip_parts(n_g):
    def plan(refs):
        x, y, c, chips = _place()
        me = 2 * x + y
        return [(refs[a].at[2 * px + py], refs[n_g + a].at[me], (px, py, c))
                for a in range(n_g) for (px, py) in chips]
    return plan


def _plan_pair_fill(n_g, rows):
    def plan(refs):
        x, y, c, _ = _place()
        return [(refs[a].at[_half(c, rows[a])], refs[a].at[_half(c, rows[a])], (x, y, 1 - c)) for a in range(n_g)]
    return plan


def _pair_add(g, got, name):
    _, rows, cols = g.shape
    hr = rows // 2
    tr = min(hr, 256)
    nb = hr // tr

    def body(g_ref, got_ref, p_ref, own_ref):
        val = (g_ref[...] + got_ref[...]).astype(BF16)
        p_ref[...] = val

        @pl.when(pl.program_id(1) == 2 * lax.axis_index("x") + lax.axis_index("y"))
        def _():
            own_ref[...] = val

    def mine(i, s):
        return (2 * lax.axis_index("x") + lax.axis_index("y"), i, 0)

    return pl.pallas_call(
        body, name=name, grid=(nb, N_CHIPS),
        in_specs=[pl.BlockSpec((None, tr, cols), lambda i, s: (s, lax.axis_index("c") * nb + i, 0)),
                  pl.BlockSpec((None, tr, cols), lambda i, s: (s, i, 0))],
        out_specs=[pl.BlockSpec((None, tr, cols), lambda i, s: (s, i, 0)), pl.BlockSpec((None, tr, cols), mine)],
        out_shape=[jax.ShapeDtypeStruct((N_CHIPS, hr, cols), BF16)] * 2,
        compiler_params=_cp(("parallel", "arbitrary")),
    )(g, got)


def _chip_sum(parts, name):
    _, hr, cols = parts.shape
    tr = min(hr, 128)
    nb = hr // tr

    def body(t_ref, o_ref):
        part = [t_ref[s].astype(F32) for s in range(N_CHIPS)]
        o_ref[...] = ((part[0] + part[1]) + part[2]) + part[3]

    return pl.pallas_call(
        body, name=name, grid=(nb,),
        in_specs=[pl.BlockSpec((N_CHIPS, tr, cols), lambda i: (0, i, 0))],
        out_specs=pl.BlockSpec((tr, cols), lambda i: (lax.axis_index("c") * nb + i, 0)),
        out_shape=jax.ShapeDtypeStruct((2 * hr, cols), F32),
        compiler_params=_cp(("parallel",)),
    )(parts)


class _Reduction:
    def __init__(self, tag, rows):
        self.tag, self.n, self.rows = tag, len(rows), rows
        self.plans = (_plan_pair_halves(self.n, rows), _plan_chip_parts(self.n), _plan_pair_fill(self.n, rows))
        self.flight = None

    def _name(self, what):
        return f"grad_{self.tag}_{what}"

    def start(self, gs):
        gots = [lax.empty((N_CHIPS, t.shape[1] // 2, t.shape[2]), F32) for t in gs]
        self.flight = _copies_start(self._name("pair_start"), list(gs) + gots, self.plans[0], self.n)
        return self.flight[2]

    def after_pair(self, after):
        sems, bufs, _ = self.flight
        out = _copies_wait(self._name("pair_wait"), sems, bufs, self.plans[0], self.n, after)
        sums = [_pair_add(g, got, self._name(f"pair_add_{i}"))
                for i, (g, got) in enumerate(zip(out[:self.n], out[self.n:]))]
        self.flight = _copies_start(self._name("chip_start"), [p for p, _ in sums] + [own for _, own in sums],
                                    self.plans[1], 3 * self.n)
        return self.flight[2]

    def after_chips(self, after):
        sems, bufs, _ = self.flight
        out = _copies_wait(self._name("chip_wait"), sems, bufs, self.plans[1], 3 * self.n, after)
        fulls = [_chip_sum(t, self._name(f"chip_sum_{i}")) for i, t in enumerate(out[self.n:])]
        self.flight = _copies_start(self._name("fill_start"), fulls, self.plans[2], self.n)
        return self.flight[2]

    def finish(self, after):
        sems, bufs, _ = self.flight
        return _copies_wait(self._name("fill_wait"), sems, bufs, self.plans[2], self.n, after)


def _adamw_math(w, g, m, v):
    nm = ADAM_B1 * m + (1.0 - ADAM_B1) * g
    nv = ADAM_B2 * v + (1.0 - ADAM_B2) * (g * g)
    m_hat = nm / (1.0 - ADAM_B1 ** ADAM_STEP)
    v_hat = nv / (1.0 - ADAM_B2 ** ADAM_STEP)
    return -ADAM_LR * (m_hat / (jnp.sqrt(v_hat) + ADAM_EPS) + ADAM_WD * w), nm, nv


def _adamw(w, g, m, v, name, tr):
    r, cdim = w.shape

    def body(w_ref, g_ref, m_ref, v_ref, d_ref, nm_ref, nv_ref):
        d_ref[...], nm_ref[...], nv_ref[...] = _adamw_math(w_ref[...], g_ref[...], m_ref[...], v_ref[...])

    return pl.pallas_call(
        body, name=name, grid=(r // tr,), in_specs=[_rows(tr, cdim)] * 4, out_specs=[_rows(tr, cdim)] * 3,
        out_shape=[jax.ShapeDtypeStruct((r, cdim), F32)] * 3, compiler_params=_cp(("parallel",)),
    )(w, g, m, v)


def _adamw_small(w, parts, m, v):
    def body(w_ref, p_ref, m_ref, v_ref, d_ref, nm_ref, nv_ref, g_ref):
        g = p_ref[0]
        for dev in range(1, N_DEV):
            g = g + p_ref[dev]
        g_ref[...] = g
        d_ref[...], nm_ref[...], nv_ref[...] = _adamw_math(w_ref[...], g, m_ref[...], v_ref[...])

    return pl.pallas_call(
        body, name="adamw_small", grid=(1,),
        in_specs=[_const(w.shape), _const(parts.shape), _const(w.shape), _const(w.shape)],
        out_specs=[_const(w.shape)] * 4, out_shape=[jax.ShapeDtypeStruct(w.shape, F32)] * 4,
        compiler_params=_cp(("arbitrary",)),
    )(w, parts, m, v)


REPLICATED = (("norm_mix_pre", 1024), ("norm_mix_post", 1024), ("norm_ffn_pre", 1024), ("norm_ffn_post", 1024),
              ("rel_bias", 256), ("sinks", 8), ("rwkv_shift_mix", 1792), ("w0", 512), ("a0", 512), ("k_k", 512),
              ("k_a", 512), ("r_k", 512), ("ln_x_g", 512), ("ln_x_b", 512), ("conv_b", 8192))
SMALL_SHARDED = (("w_decay_up", LORA_DECAY, D_RWKV), ("w_iclr_up", LORA_ICLR, D_RWKV),
                 ("w_gate_up", LORA_GATE, D_RWKV), ("conv_w", 3, 2 * D_FF))
BIG = (("w_in", D_MODEL, 640), ("w_out", 256, D_MODEL), ("w_ffn_up", D_MODEL, 2048), ("w_ffn_down", 1024, D_MODEL))
PACK_ALIGN = 8 * LANES


def _pack(pieces):
    flat = []
    for t in pieces:
        t = t.reshape(-1)
        pad = (-t.shape[0]) % LANES
        flat.append(jnp.pad(t, (0, pad)) if pad else t)
    flat = jnp.concatenate(flat)
    pad = (-flat.shape[0]) % PACK_ALIGN
    return jnp.pad(flat, (0, pad)).reshape(-1, LANES)


def _unpack(buf, sizes):
    flat, out, off = buf.reshape(-1), [], 0
    for n in sizes:
        out.append(flat[off:off + n])
        off += n + ((-n) % LANES)
    return out


def kernel(x, norm_mix_pre, norm_mix_post, norm_ffn_pre, norm_ffn_post, w_in, rel_bias, sinks, rwkv_shift_mix, w0, w_decay_up, a0, w_iclr_up, w_gate_up, k_k, k_a, r_k, ln_x_g, ln_x_b, w_out, w_ffn_up, conv_w, conv_b, w_ffn_down, loss_target, m_norm_mix_pre, m_norm_mix_post, m_norm_ffn_pre, m_norm_ffn_post, m_w_in, m_rel_bias, m_sinks, m_rwkv_shift_mix, m_w0, m_w_decay_up, m_a0, m_w_iclr_up, m_w_gate_up, m_k_k, m_k_a, m_r_k, m_ln_x_g, m_ln_x_b, m_w_out, m_w_ffn_up, m_conv_w, m_conv_b, m_w_ffn_down, v_norm_mix_pre, v_norm_mix_post, v_norm_ffn_pre, v_norm_ffn_post, v_w_in, v_rel_bias, v_sinks, v_rwkv_shift_mix, v_w0, v_w_decay_up, v_a0, v_w_iclr_up, v_w_gate_up, v_k_k, v_k_a, v_r_k, v_ln_x_g, v_ln_x_b, v_w_out, v_w_ffn_up, v_conv_w, v_conv_b, v_w_ffn_down):
    given = dict(locals())
    names = [n for n, _ in REPLICATED] + [n for n, _, _ in SMALL_SHARDED] + [n for n, _, _ in BIG]
    order = ["norm_mix_pre", "norm_mix_post", "norm_ffn_pre", "norm_ffn_post", "w_in", "rel_bias", "sinks",
             "rwkv_shift_mix", "w0", "w_decay_up", "a0", "w_iclr_up", "w_gate_up", "k_k", "k_a", "r_k", "ln_x_g",
             "ln_x_b", "w_out", "w_ffn_up", "conv_w", "conv_b", "w_ffn_down"]
    assert sorted(names) == sorted(order)
    shard = 2 * lax.axis_index("x") + lax.axis_index("y")

    big_sh = {n: given[n].reshape(a, b).astype(BF16) for n, a, b in BIG}
    small_sh = [given[n].reshape(r, c // N_CHIPS) for n, r, c in SMALL_SHARDED]
    gathered = _gather_weights([big_sh["w_in"]], small_sh)
    rest = ("w_out", "w_ffn_up", "w_ffn_down")
    win_st, rest_sh = lax.optimization_barrier((gathered[0], [big_sh[n] for n in rest]))
    sm = {n: given[n] for n, _ in REPLICATED}
    sm["r_k"] = r_k.reshape(N_Q_HEADS, HEAD_DIM)
    for (n, r, c), st in zip(SMALL_SHARDED, gathered[1:]):
        sm[n] = st.transpose(1, 0, 2).reshape(r, c)

    lands = [lax.dynamic_update_slice(lax.empty((N_CHIPS,) + t.shape, BF16), t[None], (shard, 0, 0)) for t in rest_sh]
    plan_w = _plan_gather(len(rest))
    w_sems, w_bufs, token = _copies_start("gather_rest_start", rest_sh + lands, plan_w, 9)
    sm["norm_mix_pre"] = norm_mix_pre + token[0:1, 0:1]

    def on_rest_weights(after):
        out = _copies_wait("gather_rest_wait", w_sems, w_bufs, plan_w, 9, after)
        wout_st, wup_st, wdown_st = out[3:]
        return wout_st.reshape(D_MODEL, D_MODEL), wup_st, wdown_st.reshape(D_FF, D_MODEL)

    red_a = _Reduction("a", (1024, D_MODEL, 256))
    red_b = _Reduction("b", (D_MODEL,))

    def on_grads_a(gs):
        d_wdown, d_wup, d_wout = gs
        return red_a.start([d_wdown.reshape(N_CHIPS, 1024, D_MODEL), d_wup, d_wout.reshape(N_CHIPS, 256, D_MODEL)])

    handlers = {"rest_weights": on_rest_weights, "grads_a": on_grads_a, "seam_1": red_a.after_pair,
                "seam_2": red_a.after_chips, "grads_b": lambda g: red_b.start([g])}
    steps = _local_step(x[0], loss_target[0], sm, win_st)
    kind, payload = next(steps)
    while True:
        try:
            kind, payload = steps.send(handlers[kind](payload))
        except StopIteration as done:
            loss, grad_x, grads = done.value
            break
    loss = lax.psum(loss[0, 0], ("x", "y", "c"))

    small_names = [n for n, _ in REPLICATED] + [n for n, _, _ in SMALL_SHARDED]

    def shard_cols(t, s):
        return t[:, s * (t.shape[1] // N_CHIPS):(s + 1) * (t.shape[1] // N_CHIPS)]

    for_chip = jnp.stack([_pack([grads[n] for n, _ in REPLICATED]
                                + [shard_cols(grads[n], s) for n, _, _ in SMALL_SHARDED]) for s in range(N_CHIPS)])
    me = 2 * shard + lax.axis_index("c")
    mine = lax.dynamic_index_in_dim(for_chip, shard, 0, keepdims=True)
    land = lax.dynamic_update_slice(lax.empty((N_DEV,) + for_chip.shape[1:], F32), mine, (me, 0, 0))

    def plan_small(refs):
        x, y, c, _ = _place()
        out = []
        for rel in range(1, N_DEV):
            px, py, pc = x ^ (rel >> 2), y ^ ((rel >> 1) & 1), c ^ (rel & 1)
            out.append((refs[0].at[2 * px + py], refs[1].at[4 * x + 2 * y + c], (px, py, pc)))
        return out

    s_sems, s_bufs, _ = _copies_start("grad_small_start", [for_chip, land], plan_small, N_DEV - 1)

    red_b.after_pair(grad_x)
    g_out = {}
    g_out["w_ffn_down"], g_out["w_ffn_up"], g_out["w_out"] = red_a.finish(grad_x)

    delta, new_m, new_v = {}, {}, {}
    for n, a, b in reversed(BIG):
        if n == "w_out":
            red_b.after_chips(delta["w_ffn_up"])
        if n == "w_in":
            parts = _copies_wait("grad_small_wait", s_sems, s_bufs, plan_small, N_DEV - 1, delta["w_out"])[1]
            packs = [_pack([given[pre + n2] for n2 in small_names]) for pre in ("", "m_", "v_")]
            small_sizes = [int(np.prod(given[n2].shape)) for n2 in small_names]
            upd = [_unpack(t, small_sizes) for t in _adamw_small(packs[0], parts, packs[1], packs[2])]
            for n2, d, nm, nv, g in zip(small_names, *upd):
                shape = given[n2].shape
                delta[n2], new_m[n2], new_v[n2], g_out[n2] = (t.reshape(shape) for t in (d, nm, nv, g))
            g_out[n], = red_b.finish(delta["w_out"])
        d, nm, nv = _adamw(given[n].reshape(a, b), g_out[n], given["m_" + n].reshape(a, b),
                           given["v_" + n].reshape(a, b), "adamw_" + n, 128)
        delta[n], new_m[n], new_v[n] = d, nm, nv

    def shaped(d):
        return [d[n].reshape(given[n].shape) for n in order]

    return (loss, grad_x.reshape(x.shape), *shaped(g_out), *shaped(delta), *shaped(new_m), *shaped(new_v))
```

```python
import functools
import math

import numpy as np
import jax
import jax.numpy as jnp
from jax import lax
from jax.experimental import pallas as pl
from jax.experimental.pallas import tpu as pltpu

F32 = jnp.float32
BF16 = jnp.bfloat16
MESH = pl.DeviceIdType.MESH

SEQ = 2048
D_MODEL = 1024
HEAD_DIM = 64
D_ATTN = 512
D_RWKV = 512
D_KV = 128
N_Q_HEADS = 8
N_KV_HEADS = 2
Q_PER_KV = 4
BLOCK = 128
N_BUCKETS = 32
MAX_DISTANCE = 128
LORA_DECAY = 64
LORA_ICLR = 64
LORA_GATE = 128
RWKV_COLS = 3 * D_RWKV + LORA_DECAY + LORA_ICLR + LORA_GATE
P_OFF = D_ATTN + 2 * D_KV
D_IN = P_OFF + RWKV_COLS
D_FF = 4096
NORM_EPS = 1e-6
GN_EPS = 64e-5
NEG_INF = -1e30
N_CHIPS = 4
N_DEV = 8

ADAM_LR = 0.001
ADAM_B1 = 0.9
ADAM_B2 = 0.999
ADAM_EPS = 1e-08
ADAM_WD = 0.01
ADAM_STEP = 10

VMEM_LIMIT = 52 * 1024 * 1024
LANES = 128


def _cp(sem=None, vmem=VMEM_LIMIT):
    kw = dict(vmem_limit_bytes=vmem)
    if sem is not None:
        kw["dimension_semantics"] = sem
    return pltpu.CompilerParams(**kw)


def _rows(tr, nc):
    return pl.BlockSpec((tr, nc), lambda i: (i, 0))


def _const(shape):
    return pl.BlockSpec(shape, lambda *_: (0,) * len(shape))


ANY = pl.BlockSpec(memory_space=pl.ANY)


def _split(x, n):
    parts = []
    for _ in range(n - 1):
        h = x.astype(BF16)
        parts.append(h)
        x = x - h.astype(F32)
    parts.append(x.astype(BF16))
    return parts


def _dot(a, b, dn=(((1,), (0,)), ((), ()))):
    return lax.dot_general(a, b, dn, preferred_element_type=F32)


NN = (((1,), (0,)), ((), ()))
NT = (((1,), (1,)), ((), ()))
TN = (((0,), (0,)), ((), ()))


def _dot_ind(x, ind_bf16, n=3):
    acc = None
    for part in _split(x, n):
        t = _dot(part, ind_bf16)
        acc = t if acc is None else acc + t
    return acc


def _head_ones(n, scale=1.0):
    r = lax.broadcasted_iota(jnp.int32, (n, n), 0) >> 6
    c = lax.broadcasted_iota(jnp.int32, (n, n), 1) >> 6
    return jnp.where(r == c, 1.0, 0.0).astype(BF16)


def _matmul(a, b, mode, name, *, m, n, k, tm, tn, tk, a_spec=None, b_spec=None, out=None, out_dtype=F32):
    nk = k // tk
    dn = {"nn": NN, "nt": NT, "tn": TN}[mode]

    def body(a_ref, b_ref, o_ref, *scratch):
        part = _dot(a_ref[...], b_ref[...], dn)
        if nk == 1:
            o_ref[...] = part.astype(out_dtype)
        else:
            acc_ref, = scratch
            kk = pl.program_id(2)

            @pl.when(kk == 0)
            def _():
                acc_ref[...] = part

            @pl.when(kk > 0)
            def _():
                acc_ref[...] += part

            @pl.when(kk == nk - 1)
            def _():
                o_ref[...] = acc_ref[...].astype(out_dtype)

    if a_spec is None:
        a_spec = (pl.BlockSpec((tk, tm), lambda i, j, kk: (kk, i)) if mode == "tn"
                  else pl.BlockSpec((tm, tk), lambda i, j, kk: (i, kk)))
    if b_spec is None:
        b_spec = (pl.BlockSpec((tn, tk), lambda i, j, kk: (j, kk)) if mode == "nt"
                  else pl.BlockSpec((tk, tn), lambda i, j, kk: (kk, j)))
    return pl.pallas_call(
        body, name=name, grid=(m // tm, n // tn, nk),
        in_specs=[a_spec, b_spec],
        out_specs=pl.BlockSpec((tm, tn), lambda i, j, kk: (i, j)) if out is None else out[1],
        out_shape=jax.ShapeDtypeStruct((m, n) if out is None else out[0], out_dtype),
        scratch_shapes=[] if nk == 1 else [pltpu.VMEM((tm, tn), F32)],
        compiler_params=_cp(("parallel", "parallel", "arbitrary")),
    )(a, b)


def _rstd(x):
    return lax.rsqrt(jnp.mean(x * x, axis=-1, keepdims=True) + NORM_EPS)


def _rms_bwd(x, r, g, dy):
    gy = dy * g
    return r * gy - x * ((r * r * r) * (jnp.sum(x * gy, axis=-1, keepdims=True) / x.shape[-1]))


TR = 256


def _norm_cast(x, g, name):
    def body(x_ref, g_ref, h_ref):
        x = x_ref[...]
        h_ref[...] = (x * _rstd(x) * g_ref[...]).astype(BF16)

    return pl.pallas_call(
        body, name=name, grid=(SEQ // TR,),
        in_specs=[_rows(TR, D_MODEL), _const((1, D_MODEL))],
        out_specs=_rows(TR, D_MODEL),
        out_shape=jax.ShapeDtypeStruct((SEQ, D_MODEL), BF16),
        compiler_params=_cp(("parallel",)),
    )(x, g)


def _mix_norm(x, mix, g2, g3):
    def body(x_ref, mix_ref, g2_ref, g3_ref, x2_ref, h3_ref):
        mixv = mix_ref[...]
        x2 = x_ref[...] + mixv * _rstd(mixv) * g2_ref[...]
        x2_ref[...] = x2
        h3_ref[...] = (x2 * _rstd(x2) * g3_ref[...]).astype(BF16)

    return pl.pallas_call(
        body, name="mix_norm", grid=(SEQ // TR,),
        in_specs=[_rows(TR, D_MODEL), _rows(TR, D_MODEL), _const((1, D_MODEL)), _const((1, D_MODEL))],
        out_specs=[_rows(TR, D_MODEL), _rows(TR, D_MODEL)],
        out_shape=[jax.ShapeDtypeStruct((SEQ, D_MODEL), F32), jax.ShapeDtypeStruct((SEQ, D_MODEL), BF16)],
        compiler_params=_cp(("parallel",)),
    )(x, mix, g2, g3)


def _loss_head(x2, f, g4, target):
    def body(x2_ref, f_ref, g4_ref, t_ref, loss_ref, dy_ref, df_ref, dg_ref):
        i = pl.program_id(0)
        f = f_ref[...]
        g4 = g4_ref[...]
        r = _rstd(f)
        e = x2_ref[...] + f * r * g4 - t_ref[...]
        dy = e * (1.0 / D_MODEL)
        dy_ref[...] = dy
        df_ref[...] = _rms_bwd(f, r, g4, dy).astype(BF16)
        part = 0.5 * jnp.sum(jnp.sum(e * e, axis=-1, keepdims=True), axis=0, keepdims=True) * (1.0 / D_MODEL)
        dg = jnp.sum(dy * f * r, axis=0, keepdims=True)

        @pl.when(i == 0)
        def _():
            loss_ref[...] = jnp.zeros_like(loss_ref)
            dg_ref[...] = jnp.zeros_like(dg_ref)

        loss_ref[...] += jnp.broadcast_to(part, loss_ref.shape)
        dg_ref[...] += dg

    return pl.pallas_call(
        body, name="loss_head", grid=(SEQ // TR,),
        in_specs=[_rows(TR, D_MODEL), _rows(TR, D_MODEL), _const((1, D_MODEL)), _rows(TR, D_MODEL)],
        out_specs=[_const((8, LANES)), _rows(TR, D_MODEL), _rows(TR, D_MODEL), _const((1, D_MODEL))],
        out_shape=[jax.ShapeDtypeStruct((8, LANES), F32), jax.ShapeDtypeStruct((SEQ, D_MODEL), F32),
                   jax.ShapeDtypeStruct((SEQ, D_MODEL), BF16), jax.ShapeDtypeStruct((1, D_MODEL), F32)],
        compiler_params=_cp(("arbitrary",)),
    )(x2, f, g4, target)


def _mid_bwd(x2, mix, dy, dh3, g2, g3):
    def body(x2_ref, mix_ref, dy_ref, dh3_ref, g2_ref, g3_ref, dx2_ref, dmix_ref, dg2_ref, dg3_ref):
        i = pl.program_id(0)
        x2 = x2_ref[...]
        mixv = mix_ref[...]
        dh3 = dh3_ref[...]
        r3 = _rstd(x2)
        dx2 = dy_ref[...] + _rms_bwd(x2, r3, g3_ref[...], dh3)
        dx2_ref[...] = dx2
        r2 = _rstd(mixv)
        dmix_ref[...] = _rms_bwd(mixv, r2, g2_ref[...], dx2).astype(BF16)

        @pl.when(i == 0)
        def _():
            dg2_ref[...] = jnp.zeros_like(dg2_ref)
            dg3_ref[...] = jnp.zeros_like(dg3_ref)

        dg3_ref[...] += jnp.sum(dh3 * x2 * r3, axis=0, keepdims=True)
        dg2_ref[...] += jnp.sum(dx2 * mixv * r2, axis=0, keepdims=True)

    return pl.pallas_call(
        body, name="mid_bwd", grid=(SEQ // TR,),
        in_specs=[_rows(TR, D_MODEL)] * 4 + [_const((1, D_MODEL))] * 2,
        out_specs=[_rows(TR, D_MODEL), _rows(TR, D_MODEL), _const((1, D_MODEL)), _const((1, D_MODEL))],
        out_shape=[jax.ShapeDtypeStruct((SEQ, D_MODEL), F32), jax.ShapeDtypeStruct((SEQ, D_MODEL), BF16),
                   jax.ShapeDtypeStruct((1, D_MODEL), F32), jax.ShapeDtypeStruct((1, D_MODEL), F32)],
        compiler_params=_cp(("arbitrary",)),
    )(x2, mix, dy, dh3, g2, g3)


def _first_bwd(x, dx2, dh1, g1):
    def body(x_ref, dx2_ref, dh1_ref, g1_ref, dx_ref, dg1_ref):
        i = pl.program_id(0)
        x = x_ref[...]
        dh1 = dh1_ref[...]
        r = _rstd(x)
        dx_ref[...] = dx2_ref[...] + _rms_bwd(x, r, g1_ref[...], dh1)

        @pl.when(i == 0)
        def _():
            dg1_ref[...] = jnp.zeros_like(dg1_ref)

        dg1_ref[...] += jnp.sum(dh1 * x * r, axis=0, keepdims=True)

    return pl.pallas_call(
        body, name="first_bwd", grid=(SEQ // TR,),
        in_specs=[_rows(TR, D_MODEL)] * 3 + [_const((1, D_MODEL))],
        out_specs=[_rows(TR, D_MODEL), _const((1, D_MODEL))],
        out_shape=[jax.ShapeDtypeStruct((SEQ, D_MODEL), F32), jax.ShapeDtypeStruct((1, D_MODEL), F32)],
        compiler_params=_cp(("arbitrary",)),
    )(x, dx2, dh1, g1)


TC = 256
N_CB = D_FF // TC
GELU_C = math.sqrt(2.0 / math.pi)


def _shift_down(u, s):
    rolled = pltpu.roll(u, s, 0)
    row = lax.broadcasted_iota(jnp.int32, u.shape, 0)
    return jnp.where(row >= s, rolled, 0.0)


def _shift_up(u, s):
    n = u.shape[0]
    rolled = pltpu.roll(u, n - s, 0)
    row = lax.broadcasted_iota(jnp.int32, u.shape, 0)
    return jnp.where(row < n - s, rolled, 0.0)


def _conv3(u, w, b):
    return b + w[0:1] * _shift_down(u, 2) + w[1:2] * _shift_down(u, 1) + w[2:3] * u


def _gelu_and_grad(x):
    inner = GELU_C * (x + 0.044715 * (x * x * x))
    t = jnp.tanh(inner)
    gelu = 0.5 * x * (1.0 + t)
    dgelu = 0.5 * (1.0 + t) + 0.5 * x * (1.0 - t * t) * (GELU_C * (1.0 + 3 * 0.044715 * (x * x)))
    return gelu, dgelu


def _ffn_specs():
    col = lambda off: pl.BlockSpec((SEQ, TC), lambda *g: (0, g[-1] + off))
    w = lambda off: pl.BlockSpec((3, TC), lambda *g: (0, g[-1] + off))
    b = lambda off: pl.BlockSpec((1, TC), lambda *g: (0, g[-1] + off))
    return col, w, b


def _ffn_act(u, conv_w, conv_b):
    col, w, b = _ffn_specs()

    def body(ug_ref, uv_ref, wg_ref, wv_ref, bg_ref, bv_ref, act_ref):
        gate = _conv3(ug_ref[...], wg_ref[...], bg_ref[...])
        val = _conv3(uv_ref[...], wv_ref[...], bv_ref[...])
        act_ref[...] = (_gelu_and_grad(gate)[0] * val).astype(BF16)

    return pl.pallas_call(
        body, name="ffn_act", grid=(N_CB,),
        in_specs=[col(0), col(N_CB), w(0), w(N_CB), b(0), b(N_CB)],
        out_specs=col(0),
        out_shape=jax.ShapeDtypeStruct((SEQ, D_FF), BF16),
        compiler_params=_cp(("parallel",)),
    )(u, u, conv_w, conv_w, conv_b, conv_b)


def _ffn_act_bwd(u, dact, conv_w, conv_b):
    col, w, b = _ffn_specs()
    both = lambda rows: pl.BlockSpec((2, rows, TC), lambda j: (0, 0, j))

    def body(ug_ref, uv_ref, da_ref, wg_ref, wv_ref, bg_ref, bv_ref, du_ref, dw_ref, db_ref):
        ug, uv = ug_ref[...], uv_ref[...]
        wg, wv = wg_ref[...], wv_ref[...]
        gate = _conv3(ug, wg, bg_ref[...])
        val = _conv3(uv, wv, bv_ref[...])
        gelu, dgelu = _gelu_and_grad(gate)
        da = da_ref[...]
        for h, (duc, uh, wh) in enumerate(((da * val * dgelu, ug, wg), (da * gelu, uv, wv))):
            up1, up2 = _shift_up(duc, 1), _shift_up(duc, 2)
            du_ref[h] = (wh[2:3] * duc + wh[1:2] * up1 + wh[0:1] * up2).astype(BF16)
            db_ref[h] = jnp.sum(duc, axis=0, keepdims=True)
            dw_ref[h] = jnp.concatenate(
                [jnp.sum(up2 * uh, axis=0, keepdims=True), jnp.sum(up1 * uh, axis=0, keepdims=True),
                 jnp.sum(duc * uh, axis=0, keepdims=True)], axis=0)

    return pl.pallas_call(
        body, name="ffn_act_bwd", grid=(N_CB,),
        in_specs=[col(0), col(N_CB), col(0), w(0), w(N_CB), b(0), b(N_CB)],
        out_specs=[both(SEQ), both(3), both(1)],
        out_shape=[jax.ShapeDtypeStruct((2, SEQ, D_FF), BF16), jax.ShapeDtypeStruct((2, 3, D_FF), F32),
                   jax.ShapeDtypeStruct((2, 1, D_FF), F32)],
        compiler_params=_cp(("parallel",)),
    )(u, u, dact, conv_w, conv_w, conv_b, conv_b)


def _t5_onehot():
    rel = (np.arange(BLOCK)[:, None] + BLOCK) - np.arange(2 * BLOCK)[None, :]
    n = np.maximum(rel, 0)
    max_exact = N_BUCKETS // 2
    large = max_exact + (np.log(np.maximum(n, 1).astype(np.float32) / np.float32(max_exact))
                         / np.float32(math.log(MAX_DISTANCE / max_exact))
                         * np.float32(N_BUCKETS - max_exact)).astype(np.int32)
    large = np.minimum(large, N_BUCKETS - 1)
    bucket = np.where(n < max_exact, n, large).reshape(-1)
    return (bucket[None, :] == np.arange(N_BUCKETS)[:, None]).astype(np.float32)


N_REL = BLOCK * 2 * BLOCK


def _bias_table(rel_bias_t, onehot):
    def body(rb_ref, oh_ref, o_ref):
        o_ref[...] = _dot_ind(rb_ref[...], oh_ref[...])

    return pl.pallas_call(
        body, name="bias_table", grid=(1,),
        in_specs=[_const((N_Q_HEADS, N_BUCKETS)), _const((N_BUCKETS, N_REL))],
        out_specs=_const((N_Q_HEADS, N_REL)),
        out_shape=jax.ShapeDtypeStruct((N_Q_HEADS, N_REL), F32),
        compiler_params=_cp(("arbitrary",)),
    )(rel_bias_t, onehot)


def _bias_table_bwd(dbias, onehot):
    def body(db_ref, oh_ref, o_ref):
        acc = None
        for part in _split(db_ref[...], 3):
            t = _dot(part, oh_ref[...], NT)
            acc = t if acc is None else acc + t
        o_ref[...] = acc

    return pl.pallas_call(
        body, name="bias_table_bwd", grid=(1,),
        in_specs=[_const((N_Q_HEADS, N_REL)), _const((N_BUCKETS, N_REL))],
        out_specs=_const((N_Q_HEADS, N_BUCKETS)),
        out_shape=jax.ShapeDtypeStruct((N_Q_HEADS, N_BUCKETS), F32),
        compiler_params=_cp(("arbitrary",)),
    )(dbias, onehot)


def _attn_pieces(n, q, kvp, kvc, bias_ref, sinks_ref, hk):
    qi = lax.broadcasted_iota(jnp.int32, (BLOCK, 2 * BLOCK), 0)
    kj = lax.broadcasted_iota(jnp.int32, (BLOCK, 2 * BLOCK), 1)
    rel = qi + BLOCK - kj
    first_key = jnp.where(n > 0, 0, BLOCK)
    ok = jnp.where(rel >= 0, jnp.where(rel < BLOCK, jnp.where(kj >= first_key, 1.0, 0.0), 0.0), 0.0)
    ok4 = jnp.concatenate([ok] * Q_PER_KV, axis=0) > 0.5
    c0 = hk * HEAD_DIM
    kcat = jnp.concatenate([kvp[:, c0:c0 + HEAD_DIM], kvc[:, c0:c0 + HEAD_DIM]], axis=0).astype(BF16)
    vcat = jnp.concatenate([kvp[:, D_KV + c0:D_KV + c0 + HEAD_DIM], kvc[:, D_KV + c0:D_KV + c0 + HEAD_DIM]],
                           axis=0).astype(BF16)
    q0 = hk * Q_PER_KV * HEAD_DIM
    qs = jnp.concatenate([q[:, q0 + g * HEAD_DIM:q0 + (g + 1) * HEAD_DIM] for g in range(Q_PER_KV)],
                         axis=0).astype(BF16)
    s = _dot(qs, kcat, NT) * (HEAD_DIM ** -0.5) + bias_ref[hk]
    s = jnp.where(ok4, s, NEG_INF)
    row = lax.broadcasted_iota(jnp.int32, (Q_PER_KV * BLOCK, 1), 0)
    sink = jnp.zeros((Q_PER_KV * BLOCK, 1), F32)
    for g in range(Q_PER_KV):
        sink = jnp.where((row >> 7) == g, sinks_ref[hk * Q_PER_KV + g], sink)
    m = jnp.maximum(jnp.max(s, axis=-1, keepdims=True), sink)
    p = jnp.exp(s - m)
    es = jnp.exp(sink - m)
    inv = 1.0 / (jnp.sum(p, axis=-1, keepdims=True) + es)
    return qs, kcat, vcat, p * inv, es * inv


def _attn_in_specs():
    return [pl.BlockSpec((BLOCK, D_ATTN), lambda n: (n, 0)),
            pl.BlockSpec((BLOCK, 2 * D_KV), lambda n: (jnp.maximum(n - 1, 0), D_ATTN // (2 * D_KV))),
            pl.BlockSpec((BLOCK, 2 * D_KV), lambda n: (n, D_ATTN // (2 * D_KV))),
            _const((N_KV_HEADS, Q_PER_KV * BLOCK, 2 * BLOCK)),
            pl.BlockSpec(memory_space=pltpu.SMEM)]


def _unstack_heads(t):
    return jnp.concatenate([t[g * BLOCK:(g + 1) * BLOCK] for g in range(Q_PER_KV)], axis=1)


def _attn_fwd(proj, bias, sinks):
    def body(q_ref, kvp_ref, kvc_ref, bias_ref, sinks_ref, o_ref):
        n = pl.program_id(0)
        q, kvp, kvc = q_ref[...], kvp_ref[...], kvc_ref[...]
        outs = []
        for hk in range(N_KV_HEADS):
            _, _, vcat, probs, _ = _attn_pieces(n, q, kvp, kvc, bias_ref, sinks_ref, hk)
            outs.append(_unstack_heads(_dot(probs.astype(BF16), vcat)))
        o_ref[...] = jnp.concatenate(outs, axis=1)

    return pl.pallas_call(
        body, name="attn_fwd", grid=(SEQ // BLOCK,),
        in_specs=_attn_in_specs(),
        out_specs=pl.BlockSpec((BLOCK, D_ATTN), lambda n: (n, 0)),
        out_shape=jax.ShapeDtypeStruct((SEQ, D_ATTN), F32),
        compiler_params=_cp(("parallel",)),
    )(proj, proj, proj, bias, sinks)


def _attn_bwd(proj, bias, sinks, dcat):
    nb = SEQ // BLOCK

    def body(q_ref, kvp_ref, kvc_ref, bias_ref, sinks_ref, do_ref, dq_ref, dkv_ref, dbias_ref, dsink_ref, dsacc):
        n = pl.program_id(0)

        @pl.when(n == 0)
        def _():
            dkv_ref[...] = jnp.zeros_like(dkv_ref)
            dbias_ref[...] = jnp.zeros_like(dbias_ref)
            dsacc[...] = jnp.zeros_like(dsacc)

        q, kvp, kvc = q_ref[...], kvp_ref[...], kvc_ref[...]
        do_all = do_ref[...]
        dqs, dks, dvs = [], [], []
        for hk in range(N_KV_HEADS):
            qs, kcat, vcat, probs, psink = _attn_pieces(n, q, kvp, kvc, bias_ref, sinks_ref, hk)
            q0 = hk * Q_PER_KV * HEAD_DIM
            do = jnp.concatenate([do_all[:, q0 + g * HEAD_DIM:q0 + (g + 1) * HEAD_DIM] for g in range(Q_PER_KV)],
                                 axis=0).astype(BF16)
            dprobs = _dot(do, vcat, NT)
            dvs.append(_dot(probs.astype(BF16), do, TN))
            rowdot = jnp.sum(probs * dprobs, axis=-1, keepdims=True)
            ds = probs * (dprobs - rowdot)
            dsacc[hk] += -psink * rowdot
            dbias_ref[hk] += ds
            dsb = (ds * (HEAD_DIM ** -0.5)).astype(BF16)
            dqs.append(_unstack_heads(_dot(dsb, kcat)))
            dks.append(_dot(dsb, qs, TN))
        dq_ref[...] = jnp.concatenate(dqs, axis=1)
        upd = jnp.concatenate(dks + dvs, axis=1)
        cur = pl.multiple_of(n * BLOCK, BLOCK)
        dkv_ref[pl.ds(cur, BLOCK), :] += upd[BLOCK:]

        @pl.when(n > 0)
        def _():
            prev = pl.multiple_of((n - 1) * BLOCK, BLOCK)
            dkv_ref[pl.ds(prev, BLOCK), :] += upd[:BLOCK]

        @pl.when(n == nb - 1)
        def _():
            for hk in range(N_KV_HEADS):
                for g in range(Q_PER_KV):
                    tot = jnp.sum(dsacc[hk, g * BLOCK:(g + 1) * BLOCK, :], axis=0, keepdims=True)
                    h = hk * Q_PER_KV + g
                    dsink_ref[h:h + 1, :] = jnp.broadcast_to(tot, (1, LANES))

    return pl.pallas_call(
        body, name="attn_bwd", grid=(nb,),
        in_specs=_attn_in_specs() + [pl.BlockSpec((BLOCK, D_ATTN), lambda n: (n, 0))],
        out_specs=[pl.BlockSpec((BLOCK, D_ATTN), lambda n: (n, 0)), _const((SEQ, 2 * D_KV)),
                   _const((N_KV_HEADS, Q_PER_KV * BLOCK, 2 * BLOCK)), _const((N_Q_HEADS, LANES))],
        out_shape=[jax.ShapeDtypeStruct((SEQ, D_ATTN), F32), jax.ShapeDtypeStruct((SEQ, 2 * D_KV), F32),
                   jax.ShapeDtypeStruct((N_KV_HEADS, Q_PER_KV * BLOCK, 2 * BLOCK), F32),
                   jax.ShapeDtypeStruct((N_Q_HEADS, LANES), F32)],
        scratch_shapes=[pltpu.VMEM((N_KV_HEADS, Q_PER_KV * BLOCK, 1), F32)],
        compiler_params=_cp(("arbitrary",)),
    )(proj, proj, proj, bias, sinks, dcat)


@jax.custom_vjp
def _head_sum(x):
    return _dot_ind(x, _head_ones(x.shape[-1]))


_head_sum.defvjp(lambda x: (_head_sum(x), None), lambda _, ct: (_head_sum(ct),))


@jax.custom_vjp
def _bdot(a, w):
    return _dot(a.astype(BF16), w.astype(BF16))


def _bdot_bwd(res, ct):
    a, w = res
    ctb = ct.astype(BF16)
    return _dot(ctb, w.astype(BF16), NT), _dot(a.astype(BF16), ctb, TN)


_bdot.defvjp(lambda a, w: (_bdot(a, w), (a, w)), _bdot_bwd)


def _sigmoid(x):
    return 0.5 * (jnp.tanh(0.5 * x) + 1.0)


def _softplus(x):
    return jnp.maximum(x, 0.0) + jnp.log(1.0 + jnp.exp(-jnp.abs(x)))


def _rwkv_core(r, k, v, zwa, zg, w0, wdu, a0, wiu, wgu, k_k, k_a):
    w_log = -_softplus(-(w0 + _bdot(jnp.tanh(zwa), wdu))) - 0.5
    decay = jnp.exp(-jnp.exp(w_log))
    a = _sigmoid(a0 + _bdot(zwa, wiu))
    g = _bdot(_sigmoid(zg), wgu)
    kk = k * k_k
    kk = kk / jnp.maximum(jnp.sqrt(_head_sum(kk * kk)), 1e-12)
    k2 = k * (1.0 + (a - 1.0) * k_a)
    return r, decay, k2, v, -kk, kk * a, g


def _rwkv_out(o, r, k2, v, g, lng, lnb, rk):
    mu = _head_sum(o) * (1.0 / HEAD_DIM)
    d = o - mu
    var = _head_sum(d * d) * (1.0 / HEAD_DIM)
    on = d * lax.rsqrt(var + GN_EPS) * lng + lnb
    bonus = _head_sum(r * k2 * rk) * v
    return (on + bonus) * g


P_SPLITS = (0, 512, 1024, 1536, 1664, 1792)
N_PREP_PARAMS = 7
HALO = 8


def _shifted_pieces(i, p_ref, halo_ref, mix_ref):
    p = p_ref[:, P_OFF:]
    prev_row = halo_ref[HALO - 1:HALO, P_OFF:] * jnp.where(i > 0, 1.0, 0.0)
    row = lax.broadcasted_iota(jnp.int32, p.shape, 0)
    pprev = jnp.where(row == 0, prev_row, pltpu.roll(p, 1, 0))
    delta = pprev - p
    ps = p + delta * mix_ref[...]
    return [ps[:, a:b] for a, b in zip(P_SPLITS[:-1], P_SPLITS[1:])], delta


def _prep_in_specs():
    return [_rows(TR, D_IN),
            pl.BlockSpec((HALO, D_IN), lambda i: (jnp.maximum(i * (TR // HALO) - 1, 0), 0)),
            _const((1, RWKV_COLS)), _const((1, D_RWKV)), _const((LANES, D_RWKV)), _const((1, D_RWKV)),
            _const((LANES, D_RWKV)), _const((LANES, D_RWKV)), _const((1, D_RWKV)), _const((1, D_RWKV))]


def _rwkv_prep(proj, mix, prm):
    def body(p_ref, halo_ref, mix_ref, *refs):
        prm_refs, outs = refs[:N_PREP_PARAMS], refs[N_PREP_PARAMS:]
        pieces, _ = _shifted_pieces(pl.program_id(0), p_ref, halo_ref, mix_ref)
        vals = _rwkv_core(*pieces, *[t[...] for t in prm_refs])
        for ref, val in zip(outs, vals):
            ref[...] = val

    return pl.pallas_call(
        body, name="rwkv_prep", grid=(SEQ // TR,),
        in_specs=_prep_in_specs(),
        out_specs=[_rows(TR, D_RWKV)] * 7,
        out_shape=[jax.ShapeDtypeStruct((SEQ, D_RWKV), F32)] * 7,
        compiler_params=_cp(("parallel",)),
    )(proj, proj, mix, *prm)


def _rwkv_prep_bwd(proj, mix, prm, cts):
    def body(p_ref, halo_ref, mix_ref, *refs):
        i = pl.program_id(0)
        prm_refs = refs[:N_PREP_PARAMS]
        ct_refs = refs[N_PREP_PARAMS:N_PREP_PARAMS + 10]
        dps_ref, dmix_ref = refs[N_PREP_PARAMS + 10:N_PREP_PARAMS + 12]
        dprm_refs = refs[N_PREP_PARAMS + 12:]
        pieces, delta = _shifted_pieces(i, p_ref, halo_ref, mix_ref)
        _, vjp = jax.vjp(_rwkv_core, *pieces, *[t[...] for t in prm_refs])
        dr1, dr2, dw, dk1, dk2, dv1, dv2, dkkn, db, dg = [t[...] for t in ct_refs]
        grads = vjp((dr1 + dr2, dw, dk1 + dk2, dv1 + dv2, dkkn, db, dg))
        dps = jnp.concatenate(grads[:5], axis=1)
        dps_ref[...] = dps

        @pl.when(i == 0)
        def _():
            dmix_ref[...] = jnp.zeros_like(dmix_ref)
            for ref in dprm_refs:
                ref[...] = jnp.zeros_like(ref)

        dmix_ref[...] += jnp.sum(dps * delta, axis=0, keepdims=True)
        for ref, gval in zip(dprm_refs, grads[5:]):
            ref[...] += gval

    prm_shapes = [(1, D_RWKV), (LANES, D_RWKV), (1, D_RWKV), (LANES, D_RWKV), (LANES, D_RWKV), (1, D_RWKV), (1, D_RWKV)]
    return pl.pallas_call(
        body, name="rwkv_prep_bwd", grid=(SEQ // TR,),
        in_specs=_prep_in_specs() + [_rows(TR, D_RWKV)] * 10,
        out_specs=[_rows(TR, RWKV_COLS), _const((1, RWKV_COLS))] + [_const(s) for s in prm_shapes],
        out_shape=[jax.ShapeDtypeStruct((SEQ, RWKV_COLS), F32), jax.ShapeDtypeStruct((1, RWKV_COLS), F32)]
        + [jax.ShapeDtypeStruct(s, F32) for s in prm_shapes],
        compiler_params=_cp(("arbitrary",)),
    )(proj, proj, mix, *prm, *cts)


def _rwkv_post(o, r, k2, v, g, lng, lnb, rk, attn):
    def body(o_ref, r_ref, k_ref, v_ref, g_ref, lng_ref, lnb_ref, rk_ref, attn_ref, cat_ref):
        rw = _rwkv_out(*[t[...] for t in (o_ref, r_ref, k_ref, v_ref, g_ref, lng_ref, lnb_ref, rk_ref)])
        cat_ref[...] = jnp.concatenate([attn_ref[...], rw], axis=1).astype(BF16)

    return pl.pallas_call(
        body, name="rwkv_post", grid=(SEQ // TR,),
        in_specs=[_rows(TR, D_RWKV)] * 5 + [_const((1, D_RWKV))] * 3 + [_rows(TR, D_ATTN)],
        out_specs=_rows(TR, D_MODEL),
        out_shape=jax.ShapeDtypeStruct((SEQ, D_MODEL), BF16),
        compiler_params=_cp(("parallel",)),
    )(o, r, k2, v, g, lng, lnb, rk, attn)


def _rwkv_post_bwd(o, r, k2, v, g, lng, lnb, rk, dcat):
    def body(o_ref, r_ref, k_ref, v_ref, g_ref, lng_ref, lnb_ref, rk_ref, dcat_ref,
             do_ref, dr_ref, dk_ref, dv_ref, dg_ref, dlng_ref, dlnb_ref, drk_ref):
        i = pl.program_id(0)
        args = [t[...] for t in (o_ref, r_ref, k_ref, v_ref, g_ref, lng_ref, lnb_ref, rk_ref)]
        _, vjp = jax.vjp(_rwkv_out, *args)
        grads = vjp(dcat_ref[:, D_ATTN:])
        for ref, gval in zip((do_ref, dr_ref, dk_ref, dv_ref, dg_ref), grads[:5]):
            ref[...] = gval

        @pl.when(i == 0)
        def _():
            for ref in (dlng_ref, dlnb_ref, drk_ref):
                ref[...] = jnp.zeros_like(ref)

        for ref, gval in zip((dlng_ref, dlnb_ref, drk_ref), grads[5:]):
            ref[...] += gval

    return pl.pallas_call(
        body, name="rwkv_post_bwd", grid=(SEQ // TR,),
        in_specs=[_rows(TR, D_RWKV)] * 5 + [_const((1, D_RWKV))] * 3 + [_rows(TR, D_MODEL)],
        out_specs=[_rows(TR, D_RWKV)] * 5 + [_const((1, D_RWKV))] * 3,
        out_shape=[jax.ShapeDtypeStruct((SEQ, D_RWKV), F32)] * 5 + [jax.ShapeDtypeStruct((1, D_RWKV), F32)] * 3,
        compiler_params=_cp(("arbitrary",)),
    )(o, r, k2, v, g, lng, lnb, rk, dcat)


def _assemble_dproj(dq, dkv, dps, mix):
    last = SEQ // HALO - 1

    def body(dq_ref, dkv_ref, dps_ref, nxt_ref, mix_ref, o_ref):
        i = pl.program_id(0)
        dps = dps_ref[...]
        mixv = mix_ref[...]
        nxt_row = nxt_ref[0:1, :] * jnp.where(i < SEQ // TR - 1, 1.0, 0.0)
        row = lax.broadcasted_iota(jnp.int32, dps.shape, 0)
        up = jnp.where(row == TR - 1, nxt_row, pltpu.roll(dps, TR - 1, 0))
        dp = dps * (1.0 - mixv) + up * mixv
        o_ref[...] = jnp.concatenate([dq_ref[...], dkv_ref[...], dp], axis=1).astype(BF16)

    return pl.pallas_call(
        body, name="assemble_dproj", grid=(SEQ // TR,),
        in_specs=[_rows(TR, D_ATTN), _rows(TR, 2 * D_KV), _rows(TR, RWKV_COLS),
                  pl.BlockSpec((HALO, RWKV_COLS), lambda i: (jnp.minimum((i + 1) * (TR // HALO), last), 0)),
                  _const((1, RWKV_COLS))],
        out_specs=_rows(TR, D_IN),
        out_shape=jax.ShapeDtypeStruct((SEQ, D_IN), BF16),
        compiler_params=_cp(("parallel",)),
    )(dq, dkv, dps, dps, mix)


N_PAIR = D_RWKV // LANES
CHUNK = 64
N_CHUNK = SEQ // CHUNK
GROUP = 8
STATE = (N_PAIR, HEAD_DIM, LANES)


def _lane_sums(lhs_tiles, ones2):
    out = _dot(jnp.concatenate(lhs_tiles, axis=0), ones2)
    return [out[i * HEAD_DIM:(i + 1) * HEAD_DIM] for i in range(len(lhs_tiles))]


def _seg_sum(xs, ones2):
    return _lane_sums([jnp.concatenate(_split(x, 2), axis=1) for x in xs], ones2)


def _seg_sum_rows(xs, ones2):
    out = _dot(jnp.concatenate(_split(jnp.concatenate(xs, axis=0), 2), axis=1), ones2)
    return [out[i * GROUP:(i + 1) * GROUP] for i in range(len(xs))]


def _col_form(rows, diag, ones2):
    zero = jnp.zeros((HEAD_DIM, LANES), BF16)
    tiles = []
    for row in rows:
        hi = row.astype(BF16)
        lo = (row - hi.astype(F32)).astype(BF16)
        tiles.append(jnp.concatenate(
            [jnp.where(diag, jnp.broadcast_to(part, (HEAD_DIM, LANES)), zero) for part in (hi, lo)], axis=1))
    return _lane_sums(tiles, ones2)


def _scan_consts():
    ones2 = jnp.concatenate([_head_ones(LANES)] * 2, axis=0)
    sub = lax.broadcasted_iota(jnp.int32, (HEAD_DIM, LANES), 0)
    lane_in_head = lax.broadcasted_iota(jnp.int32, (HEAD_DIM, LANES), 1) & (HEAD_DIM - 1)
    return ones2, lane_in_head == sub, lane_in_head


def _rows_of_columns(tile):
    t = tile.T
    return jnp.concatenate([t[:CHUNK], t[HEAD_DIM:HEAD_DIM + CHUNK]], axis=1)


def _pair(j):
    return slice(j * LANES, (j + 1) * LANES)


def _scan_fwd(r, w, k, v, kkn, b):
    def body(r_ref, w_ref, k_ref, v_ref, kkn_ref, b_ref, o_ref, st_ref, sa_ref, s_scr):
        c = pl.program_id(0)
        ones2, diag, lane_in_head = _scan_consts()

        @pl.when(c == 0)
        def _():
            s_scr[...] = jnp.zeros_like(s_scr)

        def group(gi, carry):
            row0 = pl.multiple_of(gi * GROUP, GROUP)
            states, ocols = list(carry[:N_PAIR]), list(carry[N_PAIR:])
            tiles = [[t[pl.ds(row0, GROUP), _pair(j)] for t in (r_ref, w_ref, k_ref, v_ref, kkn_ref, b_ref)]
                     for j in range(N_PAIR)]
            def row(j, name, u):
                return tiles[j]["rwkvnb".index(name)][u:u + 1]

            def emit_out(u, after):
                outs = _seg_sum([s[j] * row(j, "r", u + d) for d, s in enumerate(after) for j in range(N_PAIR)], ones2)
                for d in range(2):
                    here = lane_in_head == gi * GROUP + u + d
                    for j in range(N_PAIR):
                        ocols[j] = jnp.where(here, outs[d * N_PAIR + j], ocols[j])

            def vcols_of(u):
                cols = _col_form([row(j, "v", u + d) for d in range(2) for j in range(N_PAIR)], diag, ones2)
                return cols[:N_PAIR], cols[N_PAIR:]

            n_next = [pltpu.roll(tiles[j][4], GROUP - 1, 0) for j in range(N_PAIR)]
            dots = _seg_sum_rows([tiles[j][5] * n_next[j] for j in range(N_PAIR)]
                                 + [tiles[j][2] * n_next[j] for j in range(N_PAIR)], ones2)
            b_n, k_n = dots[:N_PAIR], dots[N_PAIR:]
            w_n = [tiles[j][1] * n_next[j] for j in range(N_PAIR)]

            vcols = vcols_of(0)
            after = None
            for u in range(0, GROUP, 2):
                prods = _seg_sum([states[j] * row(j, "n", u) for j in range(N_PAIR)]
                                 + [states[j] * w_n[j][u:u + 1] for j in range(N_PAIR)], ones2)
                if after is not None:
                    emit_out(u - 2, after)
                nxt = vcols_of(u + 2) if u + 2 < GROUP else None
                first, second = [], []
                for j in range(N_PAIR):
                    sa1 = prods[j]
                    sa2 = prods[N_PAIR + j] + sa1 * b_n[j][u:u + 1] + vcols[0][j] * k_n[j][u:u + 1]
                    s1 = states[j] * row(j, "w", u) + sa1 * row(j, "b", u) + vcols[0][j] * row(j, "k", u)
                    s2 = s1 * row(j, "w", u + 1) + sa2 * row(j, "b", u + 1) + vcols[1][j] * row(j, "k", u + 1)
                    st_ref[row0 + u, j] = s1
                    sa_ref[row0 + u, j] = sa1
                    st_ref[row0 + u + 1, j] = s2
                    sa_ref[row0 + u + 1, j] = sa2
                    first.append(s1)
                    second.append(s2)
                    states[j] = s2
                after, vcols = (first, second), nxt
            emit_out(GROUP - 2, after)
            return tuple(states + ocols)

        zero = jnp.zeros((HEAD_DIM, LANES), F32)
        fin = lax.fori_loop(0, CHUNK // GROUP, group, tuple(s_scr[j] for j in range(N_PAIR)) + (zero,) * N_PAIR)
        for j in range(N_PAIR):
            s_scr[j] = fin[j]
            o_ref[:, _pair(j)] = _rows_of_columns(fin[N_PAIR + j])

    blk = pl.BlockSpec((CHUNK, D_RWKV), lambda c: (c, 0))
    per_step = pl.BlockSpec((CHUNK,) + STATE, lambda c: (c, 0, 0, 0))
    return pl.pallas_call(
        body, name="rwkv_scan_fwd", grid=(N_CHUNK,),
        in_specs=[blk] * 6,
        out_specs=[blk, per_step, per_step],
        out_shape=[jax.ShapeDtypeStruct((SEQ, D_RWKV), F32)] + [jax.ShapeDtypeStruct((SEQ,) + STATE, F32)] * 2,
        scratch_shapes=[pltpu.VMEM(STATE, F32)],
        compiler_params=_cp(("arbitrary",)),
    )(r, w, k, v, kkn, b)


def _scan_bwd(r, w, k, v, kkn, b, do, states, sas, ds_in, prev, name, first_chunk, n_chunks):
    top = first_chunk + n_chunks - 1

    def body(r_ref, w_ref, k_ref, v_ref, kkn_ref, b_ref, do_ref, st_ref, before_ref, sa_ref, ds_in_ref, *rest):
        dr_ref, dw_ref, dk_ref, dv_ref, dkkn_ref, db_ref, ds_out_ref, ds_scr = rest[-8:]
        i = pl.program_id(0)
        ones2, diag, lane_in_head = _scan_consts()

        @pl.when(i == 0)
        def _():
            ds_scr[...] = ds_in_ref[...]

        entry = [before_ref[0, j] * jnp.where(i < top, 1.0, 0.0) for j in range(N_PAIR)]

        def reverse(gr, carry):
            gi = CHUNK // GROUP - 1 - gr
            row0 = pl.multiple_of(gi * GROUP, GROUP)
            dstates, dvcols = list(carry[:N_PAIR]), list(carry[N_PAIR:])
            tiles = [[t[pl.ds(row0, GROUP), _pair(j)]
                      for t in (r_ref, w_ref, k_ref, v_ref, kkn_ref, b_ref, do_ref)] for j in range(N_PAIR)]
            rows = [[[None] * GROUP for _ in range(5)] for _ in range(N_PAIR)]

            def row(j, name, u):
                return tiles[j]["rwkvnbd".index(name)][u:u + 1]

            def cols_of(u):
                cols = _col_form([row(j, name, u - d) for d in range(2) for name in "dv" for j in range(N_PAIR)],
                                 diag, ones2)
                return [[(cols[(2 * d) * N_PAIR + j], cols[(2 * d + 1) * N_PAIR + j]) for j in range(N_PAIR)]
                        for d in range(2)]

            def emit_dv(u, dsps):
                outs = _seg_sum([dsp[j] * row(j, "k", u - d) for d, dsp in enumerate(dsps) for j in range(N_PAIR)], ones2)
                for d in range(2):
                    here = lane_in_head == gi * GROUP + u - d
                    for j in range(N_PAIR):
                        dvcols[j] = jnp.where(here, outs[d * N_PAIR + j], dvcols[j])

            b_prev = [pltpu.roll(tiles[j][5], 1, 0) for j in range(N_PAIR)]
            dots = _seg_sum_rows([tiles[j][4] * b_prev[j] for j in range(N_PAIR)]
                                 + [tiles[j][0] * tiles[j][5] for j in range(N_PAIR)], ones2)
            n_b, r_b = dots[:N_PAIR], dots[N_PAIR:]
            w_b = [tiles[j][1] * b_prev[j] for j in range(N_PAIR)]

            def outputs(u, j, dsp, dsa, docol, vcol):
                tl = gi * GROUP + u
                if u > 0:
                    s_prev = st_ref[tl - 1, j]
                else:
                    s_prev = jnp.where(gi == 0, entry[j], st_ref[jnp.maximum(tl - 1, 0), j])
                rows[j][0][u] = jnp.sum(st_ref[tl, j] * docol, axis=0, keepdims=True)
                rows[j][1][u] = jnp.sum(dsp * s_prev, axis=0, keepdims=True)
                rows[j][2][u] = jnp.sum(dsp * vcol, axis=0, keepdims=True)
                rows[j][3][u] = jnp.sum(s_prev * dsa, axis=0, keepdims=True)
                rows[j][4][u] = jnp.sum(dsp * sa_ref[tl, j], axis=0, keepdims=True)

            cols = cols_of(GROUP - 1)
            before = None
            for u in range(GROUP - 1, 0, -2):
                dsp1 = [dstates[j] + cols[0][j][0] * row(j, "r", u) for j in range(N_PAIR)]
                prods = _seg_sum([dsp1[j] * row(j, "b", u) for j in range(N_PAIR)]
                                 + [dsp1[j] * w_b[j][u:u + 1] for j in range(N_PAIR)], ones2)
                if before is not None:
                    emit_dv(u + 2, before)
                nxt = cols_of(u - 2) if u >= 2 else None
                dsp2 = []
                for j in range(N_PAIR):
                    dsa1 = prods[j]
                    dsa2 = prods[N_PAIR + j] + dsa1 * n_b[j][u:u + 1] + cols[1][j][0] * r_b[j][u - 1:u]
                    mid = dsp1[j] * row(j, "w", u) + dsa1 * row(j, "n", u) + cols[1][j][0] * row(j, "r", u - 1)
                    outputs(u, j, dsp1[j], dsa1, *cols[0][j])
                    outputs(u - 1, j, mid, dsa2, *cols[1][j])
                    dstates[j] = mid * row(j, "w", u - 1) + dsa2 * row(j, "n", u - 1)
                    dsp2.append(mid)
                before, cols = (dsp1, dsp2), nxt
            emit_dv(1, before)
            for j in range(N_PAIR):
                for ref, rr in zip((dr_ref, dw_ref, dk_ref, dkkn_ref, db_ref), rows[j]):
                    ref[pl.ds(row0, GROUP), _pair(j)] = jnp.concatenate(rr, axis=0)
            return tuple(dstates + dvcols)

        zero = jnp.zeros((HEAD_DIM, LANES), F32)
        dfin = lax.fori_loop(0, CHUNK // GROUP, reverse, tuple(ds_scr[j] for j in range(N_PAIR)) + (zero,) * N_PAIR)
        for j in range(N_PAIR):
            ds_scr[j] = dfin[j]
            dv_ref[:, _pair(j)] = _rows_of_columns(dfin[N_PAIR + j])

        @pl.when(i == n_chunks - 1)
        def _():
            ds_out_ref[...] = ds_scr[...]

    blk = pl.BlockSpec((CHUNK, D_RWKV), lambda i: (top - i, 0))
    per_step = pl.BlockSpec((CHUNK,) + STATE, lambda i: (top - i, 0, 0, 0))
    step_before = pl.BlockSpec((1,) + STATE, lambda i: (jnp.maximum((top - i) * CHUNK - 1, 0), 0, 0, 0))
    prev = [] if prev is None else list(prev)
    outs = pl.pallas_call(
        body, name=name, grid=(n_chunks,),
        in_specs=[blk] * 7 + [per_step, step_before, per_step, _const(STATE)] + [ANY] * len(prev),
        out_specs=[blk] * 6 + [_const(STATE)],
        out_shape=[jax.ShapeDtypeStruct((SEQ, D_RWKV), F32)] * 6 + [jax.ShapeDtypeStruct(STATE, F32)],
        scratch_shapes=[pltpu.VMEM(STATE, F32)],
        input_output_aliases={11 + t: t for t in range(len(prev))},
        compiler_params=_cp(("arbitrary",)),
    )(r, w, k, v, kkn, b, do, states, states, sas, ds_in, *prev)
    return outs[:6], outs[6]


def _stacked(rows, cols, pick):
    return pl.BlockSpec((None, rows, cols), pick)


def _local_step(x, target, sm, win_st):
    def tied(t, token):
        return t if token is None else t + token[0:1, 0:1].reshape((1,) * t.ndim)

    zpad = jnp.zeros((LORA_DECAY, D_RWKV), F32)
    prm = [sm["w0"], jnp.concatenate([sm["w_decay_up"], zpad], axis=0), sm["a0"],
           jnp.concatenate([zpad, sm["w_iclr_up"]], axis=0), sm["w_gate_up"], sm["k_k"], sm["k_a"]]
    mix = sm["rwkv_shift_mix"]
    onehot = jnp.asarray(_t5_onehot(), BF16)
    sinks = sm["sinks"].reshape(N_Q_HEADS)
    lng, lnb, rk = sm["ln_x_g"], sm["ln_x_b"], sm["r_k"].reshape(1, D_RWKV)

    h1 = _norm_cast(x, sm["norm_mix_pre"], "norm_in")
    proj = _matmul(h1, win_st, "nn", "proj", m=SEQ, n=D_IN, k=D_MODEL, tm=SEQ, tn=640, tk=D_MODEL,
                   b_spec=_stacked(D_MODEL, 640, lambda i, j, kk: (j, 0, 0)))
    bias = _bias_table(sm["rel_bias"].T, onehot).reshape(N_KV_HEADS, Q_PER_KV * BLOCK, 2 * BLOCK)
    attn = _attn_fwd(proj, bias, sinks)
    r, w, k2, v, kkn, b, g = _rwkv_prep(proj, mix, prm)
    o, states, sas = _scan_fwd(r, w, k2, v, kkn, b)
    wout, wup_st, wdown = yield ("rest_weights", o)
    cat = _rwkv_post(o, r, k2, v, g, lng, lnb, rk, attn)
    mixo = _matmul(cat, wout, "nn", "out_proj", m=SEQ, n=D_MODEL, k=D_MODEL, tm=SEQ, tn=512, tk=D_MODEL)
    x2, h3 = _mix_norm(x, mixo, sm["norm_mix_post"], sm["norm_ffn_pre"])
    u = _matmul(h3, wup_st, "nn", "ffn_up", m=SEQ, n=2 * D_FF, k=D_MODEL, tm=SEQ, tn=512, tk=D_MODEL,
                b_spec=_stacked(D_MODEL, 512, lambda i, j, kk: (j // 4, 0, j % 4)))
    act = _ffn_act(u, sm["conv_w"], sm["conv_b"])
    f = _matmul(act, wdown, "nn", "ffn_down", m=SEQ, n=D_MODEL, k=D_FF, tm=1024, tn=512, tk=2048)
    loss, dy, df, d_g4 = _loss_head(x2, f, sm["norm_ffn_post"], target)

    dact = _matmul(df, wdown, "nt", "d_act", m=SEQ, n=D_FF, k=D_MODEL, tm=SEQ, tn=512, tk=D_MODEL)
    d_wdown = _matmul(act, df, "tn", "d_wdown", m=D_FF, n=D_MODEL, k=SEQ, tm=512, tn=D_MODEL, tk=SEQ)
    du, d_convw, d_convb = _ffn_act_bwd(u, dact, sm["conv_w"], sm["conv_b"])
    d_convw = d_convw.transpose(1, 0, 2).reshape(3, 2 * D_FF)
    d_convb = d_convb.reshape(1, 2 * D_FF)
    dh3 = _matmul(du, wup_st, "nt", "d_h3", m=SEQ, n=D_MODEL, k=2 * D_FF, tm=1024, tn=D_MODEL, tk=2048,
                  a_spec=pl.BlockSpec((None, 1024, 2048), lambda i, j, kk: (kk // 2, i, kk % 2)),
                  b_spec=_stacked(D_MODEL, 2048, lambda i, j, kk: (kk, j, 0)))
    d_wup = _matmul(h3, du, "tn", "d_wup", m=D_MODEL, n=2 * D_FF, k=SEQ, tm=D_MODEL, tn=512, tk=SEQ,
                    b_spec=pl.BlockSpec((None, SEQ, 512), lambda i, j, kk: (j // 8, 0, j % 8)),
                    out=((N_CHIPS, D_MODEL, 2048), _stacked(D_MODEL, 512, lambda i, j, kk: (j // 4, 0, j % 4))))
    dx2, dmix, d_g2, d_g3 = _mid_bwd(x2, mixo, dy, dh3, sm["norm_mix_post"], sm["norm_ffn_pre"])
    dcat = _matmul(dmix, wout, "nt", "d_cat", m=SEQ, n=D_MODEL, k=D_MODEL, tm=SEQ, tn=512, tk=D_MODEL)
    d_wout = _matmul(cat, dmix, "tn", "d_wout", m=D_MODEL, n=D_MODEL, k=SEQ, tm=512, tn=D_MODEL, tk=SEQ)
    token = yield ("grads_a", (d_wdown, d_wup, d_wout))
    do, dr_p, dk_p, dv_p, dg, d_lng, d_lnb, d_rk = _rwkv_post_bwd(o, r, k2, v, g, lng, tied(lnb, token), rk, dcat)
    half = N_CHUNK // 2
    ds_end = jnp.zeros(STATE, F32)
    late, ds_mid = _scan_bwd(r, w, k2, v, kkn, b, do, states, sas, ds_end, None, "rwkv_scan_bwd_late", half, half)
    token = yield ("seam_1", ds_mid)
    scan_cts, ds_first = _scan_bwd(r, w, k2, v, kkn, b, do, states, sas, tied(ds_mid, token), late,
                                   "rwkv_scan_bwd_early", 0, half)
    dr_s, dw_s, dk_s, dv_s, dkkn_s, db_s = scan_cts
    token = yield ("seam_2", ds_first)
    prep_grads = _rwkv_prep_bwd(proj, tied(mix, token), prm,
                                (dr_s, dr_p, dw_s, dk_s, dk_p, dv_s, dv_p, dkkn_s, db_s, dg))
    dps, d_mix, d_w0, d_wdu, d_a0, d_wiu, d_wgu, d_kk, d_ka = prep_grads
    dq, dkv, dbias, dsink = _attn_bwd(proj, bias, sinks, dcat)
    d_relb = _bias_table_bwd(dbias.reshape(N_Q_HEADS, N_REL), onehot).T
    dproj = _assemble_dproj(dq, dkv, dps, mix)
    d_win = _matmul(h1, dproj, "tn", "d_win", m=D_MODEL, n=D_IN, k=SEQ, tm=D_MODEL, tn=640, tk=SEQ,
                    out=((N_CHIPS, D_MODEL, 640), _stacked(D_MODEL, 640, lambda i, j, kk: (j, 0, 0))))
    token = yield ("grads_b", d_win)
    dh1 = _matmul(dproj, win_st, "nt", "d_h1", m=SEQ, n=D_MODEL, k=D_IN, tm=1024, tn=D_MODEL, tk=640,
                  b_spec=_stacked(D_MODEL, 640, lambda i, j, kk: (kk, j, 0)))
    grad_x, d_g1 = _first_bwd(x, dx2, dh1, tied(sm["norm_mix_pre"], token))

    grads = {
        "norm_mix_pre": d_g1, "norm_mix_post": d_g2, "norm_ffn_pre": d_g3, "norm_ffn_post": d_g4,
        "w_in": d_win, "rel_bias": d_relb, "sinks": dsink[:, 0].reshape(1, N_Q_HEADS),
        "rwkv_shift_mix": d_mix, "w0": d_w0, "w_decay_up": d_wdu[:LORA_DECAY], "a0": d_a0,
        "w_iclr_up": d_wiu[LORA_DECAY:], "w_gate_up": d_wgu, "k_k": d_kk, "k_a": d_ka,
        "r_k": d_rk.reshape(1, N_Q_HEADS, HEAD_DIM), "ln_x_g": d_lng, "ln_x_b": d_lnb,
        "w_out": d_wout, "w_ffn_up": d_wup, "conv_w": d_convw, "conv_b": d_convb, "w_ffn_down": d_wdown,
    }
    return loss, grad_x, grads


def _place():
    x, y, c = lax.axis_index("x"), lax.axis_index("y"), lax.axis_index("c")
    chips = [(1 - x, y), (x, 1 - y), (1 - x, 1 - y)]
    return x, y, c, chips


def _remote(src, dst, sems, idx, to):
    return pltpu.make_async_remote_copy(src_ref=src, dst_ref=dst, send_sem=sems[0].at[idx], recv_sem=sems[1].at[idx],
                                        device_id=to, device_id_type=MESH)


def _half(c, rows):
    return pl.ds(pl.multiple_of(c * (rows // 2), 16), rows // 2)


def _gather_weights(big, small):
    nb, ns = len(big), len(small)

    def body(*refs):
        ins, outs = refs[:nb + ns], refs[nb + ns:2 * (nb + ns)]
        ici, d2d, sml, loc = refs[2 * (nb + ns):2 * (nb + ns) + 2], refs[-5:-3], refs[-3:-1], refs[-1]
        x, y, c, chips = _place()
        me = 2 * x + y
        sib = (x, y, 1 - c)
        local = [pltpu.make_async_copy(ins[a], outs[a].at[me], loc.at[a]) for a in range(nb + ns)]
        for cp in local:
            cp.start()
        sends = []
        for a in range(nb):
            rows = _half(c, big[a].shape[0])
            for kk, chip in enumerate(chips):
                sends.append(_remote(ins[a].at[rows], outs[a].at[me, rows], ici, a * 3 + kk, (*chip, c)))
        for a in range(ns):
            for kk, chip in enumerate(chips):
                sends.append(_remote(ins[nb + a], outs[nb + a].at[me], sml, a * 3 + kk, (*chip, c)))
        for cp in sends:
            cp.start()
        passed = []
        for a in range(nb):
            rows = _half(c, big[a].shape[0])
            for kk, (px, py) in enumerate(chips):
                got = outs[a].at[2 * px + py, rows]
                _remote(got, got, ici, a * 3 + kk, sib).wait_recv()
                fwd = _remote(got, got, d2d, a * 3 + kk, sib)
                fwd.start()
                passed.append(fwd)
        for a in range(nb):
            other = _half(1 - c, big[a].shape[0])
            for kk, (px, py) in enumerate(chips):
                land = outs[a].at[2 * px + py, other]
                _remote(land, land, d2d, a * 3 + kk, sib).wait_recv()
        for a in range(ns):
            for kk, (px, py) in enumerate(chips):
                land = outs[nb + a].at[2 * px + py]
                _remote(land, land, sml, a * 3 + kk, sib).wait_recv()
        for cp in sends + passed:
            cp.wait_send()
        for cp in local:
            cp.wait()

    arrs = list(big) + list(small)
    return pl.pallas_call(
        body, name="gather_weights",
        in_specs=[ANY] * len(arrs), out_specs=[ANY] * len(arrs),
        out_shape=[jax.ShapeDtypeStruct((N_CHIPS,) + t.shape, t.dtype) for t in arrs],
        scratch_shapes=[pltpu.SemaphoreType.DMA((3 * nb,)), pltpu.SemaphoreType.DMA((3 * nb,)),
                        pltpu.SemaphoreType.DMA((3 * nb,)), pltpu.SemaphoreType.DMA((3 * nb,)),
                        pltpu.SemaphoreType.DMA((3 * ns,)), pltpu.SemaphoreType.DMA((3 * ns,)),
                        pltpu.SemaphoreType.DMA((nb + ns,))],
        compiler_params=pltpu.CompilerParams(has_side_effects=True),
    )(*arrs)


def _allreduce_small(g):
    rows = g.shape[0]

    def body(g_ref, o_ref, buf, send, recv):
        x, y, c, _ = _place()
        me = 4 * x + 2 * y + c
        buf[me] = g_ref[...]
        sends = []
        for rel in range(1, N_DEV):
            px, py, pc = x ^ (rel >> 2), y ^ ((rel >> 1) & 1), c ^ (rel & 1)
            cp = _remote(g_ref, buf.at[me], (send, recv), rel - 1, (px, py, pc))
            cp.start()
            sends.append(cp)
        for rel in range(1, N_DEV):
            px, py, pc = x ^ (rel >> 2), y ^ ((rel >> 1) & 1), c ^ (rel & 1)
            land = buf.at[4 * px + 2 * py + pc]
            _remote(land, land, (send, recv), rel - 1, (px, py, pc)).wait_recv()
        acc = buf[0]
        for d in range(1, N_DEV):
            acc = acc + buf[d]
        o_ref[...] = acc
        for cp in sends:
            cp.wait_send()

    vm = pl.BlockSpec(memory_space=pltpu.VMEM)
    return pl.pallas_call(
        body, name="allreduce_small", in_specs=[vm], out_specs=vm,
        out_shape=jax.ShapeDtypeStruct((rows, LANES), F32),
        scratch_shapes=[pltpu.VMEM((N_DEV, rows, LANES), F32), pltpu.SemaphoreType.DMA((N_DEV - 1,)),
                        pltpu.SemaphoreType.DMA((N_DEV - 1,))],
        compiler_params=_cp(),
    )(g)


def _pair_exchange(gs):
    n = len(gs)

    def body(*refs):
        ins, got, mine, send, recv, loc = refs[:n], refs[n:2 * n], refs[2 * n:3 * n], refs[-3], refs[-2], refs[-1]
        x, y, c, _ = _place()
        sib = (x, y, 1 - c)
        cps, local = [], []
        for a in range(n):
            rows = gs[a].shape[1]
            cp = _remote(ins[a].at[:, _half(1 - c, rows)], got[a], (send, recv), a, sib)
            cp.start()
            cps.append(cp)
            lc = pltpu.make_async_copy(ins[a].at[:, _half(c, rows)], mine[a], loc.at[a])
            lc.start()
            local.append(lc)
        for a in range(n):
            cps[a].wait_recv()
        for a in range(n):
            cps[a].wait_send()
            local[a].wait()

    halves = [jax.ShapeDtypeStruct((N_CHIPS, t.shape[1] // 2, t.shape[2]), F32) for t in gs]
    outs = pl.pallas_call(
        body, name="grad_pair_exchange", in_specs=[ANY] * n, out_specs=[ANY] * (2 * n), out_shape=halves + halves,
        scratch_shapes=[pltpu.SemaphoreType.DMA((n,)), pltpu.SemaphoreType.DMA((n,)), pltpu.SemaphoreType.DMA((n,))],
        compiler_params=pltpu.CompilerParams(has_side_effects=True),
    )(*gs)
    return outs[:n], outs[n:]


def _chip_exchange(ps):
    n = len(ps)

    def body(*refs):
        ins, outs, send, recv, loc = refs[:n], refs[n:2 * n], refs[-3], refs[-2], refs[-1]
        x, y, c, chips = _place()
        me = 2 * x + y
        cps, local = [], []
        for a in range(n):
            lc = pltpu.make_async_copy(ins[a].at[me], outs[a].at[me], loc.at[a])
            lc.start()
            local.append(lc)
            for kk, (px, py) in enumerate(chips):
                cp = _remote(ins[a].at[2 * px + py], outs[a].at[me], (send, recv), a * 3 + kk, (px, py, c))
                cp.start()
                cps.append(cp)
        for a in range(n):
            for kk, (px, py) in enumerate(chips):
                land = outs[a].at[2 * px + py]
                _remote(land, land, (send, recv), a * 3 + kk, (px, py, c)).wait_recv()
        for cp in cps:
            cp.wait_send()
        for lc in local:
            lc.wait()

    return pl.pallas_call(
        body, name="grad_chip_exchange", in_specs=[ANY] * n, out_specs=[ANY] * n,
        out_shape=[jax.ShapeDtypeStruct(t.shape, F32) for t in ps],
        scratch_shapes=[pltpu.SemaphoreType.DMA((3 * n,)), pltpu.SemaphoreType.DMA((3 * n,)),
                        pltpu.SemaphoreType.DMA((n,))],
        compiler_params=pltpu.CompilerParams(has_side_effects=True),
    )(*ps)


def _pair_gather(hs):
    n = len(hs)

    def body(*refs):
        ins, outs, send, recv, loc = refs[:n], refs[n:2 * n], refs[-3], refs[-2], refs[-1]
        x, y, c, _ = _place()
        sib = (x, y, 1 - c)
        cps, local = [], []
        for a in range(n):
            rows = 2 * hs[a].shape[0]
            cp = _remote(ins[a], outs[a].at[_half(c, rows)], (send, recv), a, sib)
            cp.start()
            cps.append(cp)
            lc = pltpu.make_async_copy(ins[a], outs[a].at[_half(c, rows)], loc.at[a])
            lc.start()
            local.append(lc)
        for a in range(n):
            rows = 2 * hs[a].shape[0]
            land = outs[a].at[_half(1 - c, rows)]
            _remote(land, land, (send, recv), a, sib).wait_recv()
        for a in range(n):
            cps[a].wait_send()
            local[a].wait()

    return pl.pallas_call(
        body, name="grad_pair_gather", in_specs=[ANY] * n, out_specs=[ANY] * n,
        out_shape=[jax.ShapeDtypeStruct((2 * t.shape[0], t.shape[1]), F32) for t in hs],
        scratch_shapes=[pltpu.SemaphoreType.DMA((n,)), pltpu.SemaphoreType.DMA((n,)), pltpu.SemaphoreType.DMA((n,))],
        compiler_params=pltpu.CompilerParams(has_side_effects=True),
    )(*hs)


def _add2(a, b, name):
    r, cdim = a.shape
    tr = 256

    def body(a_ref, b_ref, o_ref):
        o_ref[...] = a_ref[...] + b_ref[...]

    return pl.pallas_call(
        body, name=name, grid=(r // tr,), in_specs=[_rows(tr, cdim)] * 2, out_specs=_rows(tr, cdim),
        out_shape=jax.ShapeDtypeStruct((r, cdim), F32), compiler_params=_cp(("parallel",)),
    )(a, b)


def _sum4(t, name):
    _, r, cdim = t.shape
    tr = 128

    def body(t_ref, o_ref):
        o_ref[...] = ((t_ref[0] + t_ref[1]) + t_ref[2]) + t_ref[3]

    return pl.pallas_call(
        body, name=name, grid=(r // tr,), in_specs=[pl.BlockSpec((N_CHIPS, tr, cdim), lambda i: (0, i, 0))],
        out_specs=_rows(tr, cdim), out_shape=jax.ShapeDtypeStruct((r, cdim), F32),
        compiler_params=_cp(("parallel",)),
    )(t)


def _reduce_big(gs):
    got, mine = _pair_exchange(gs)
    ps = [_add2(m.reshape(-1, m.shape[2]), g.reshape(-1, g.shape[2]), f"grad_pair_add_{i}").reshape(m.shape)
          for i, (m, g) in enumerate(zip(mine, got))]
    xs = _chip_exchange(ps)
    hs = [_sum4(t, f"grad_chip_sum_{i}") for i, t in enumerate(xs)]
    return _pair_gather(hs)


HBM = pl.BlockSpec(memory_space=pltpu.HBM)
SEM = pl.BlockSpec(memory_space=pltpu.SEMAPHORE)
EFFECT = pltpu.SideEffectType.DATAFLOW_SIDE_EFFECTING


def _copies_start(name, bufs, plan, n):
    nb = len(bufs)

    def body(*refs):
        ins, sems, token = refs[:nb], refs[nb:nb + 2 * n], refs[-1]
        for kk, (src, dst, dev) in enumerate(plan(ins)):
            pltpu.make_async_remote_copy(src_ref=src, dst_ref=dst, send_sem=sems[2 * kk], recv_sem=sems[2 * kk + 1],
                                         device_id=dev, device_id_type=MESH).start()
        token[...] = jnp.zeros_like(token)

    outs = pl.pallas_call(
        body, name=name,
        out_shape=tuple([pltpu.SemaphoreType.DMA(())] * (2 * n) + [pltpu.HBM(t.shape, t.dtype) for t in bufs]
                        + [jax.ShapeDtypeStruct((8, LANES), F32)]),
        in_specs=[HBM] * nb,
        out_specs=tuple([SEM] * (2 * n) + [HBM] * nb + [pl.BlockSpec(memory_space=pltpu.VMEM)]),
        input_output_aliases={t: 2 * n + t for t in range(nb)},
        compiler_params=pltpu.CompilerParams(has_side_effects=EFFECT),
    )(*[pltpu.with_memory_space_constraint(t, pltpu.HBM) for t in bufs])
    return outs[:2 * n], outs[2 * n:2 * n + nb], outs[-1]


def _copies_wait(name, sems, bufs, plan, n, after):
    nb = len(bufs)

    def body(*refs):
        ins, sem_refs = refs[:nb], refs[nb:nb + 2 * n]
        for kk, (src, dst, dev) in enumerate(plan(ins)):
            cp = pltpu.make_async_remote_copy(src_ref=src, dst_ref=dst, send_sem=sem_refs[2 * kk],
                                              recv_sem=sem_refs[2 * kk + 1], device_id=dev, device_id_type=MESH)
            cp.wait_send()
            cp.wait_recv()

    return pl.pallas_call(
        body, name=name,
        out_shape=tuple(pltpu.HBM(t.shape, t.dtype) for t in bufs),
        in_specs=[HBM] * nb + [SEM] * (2 * n) + [ANY],
        out_specs=tuple([HBM] * nb),
        input_output_aliases={t: t for t in range(nb)},
        compiler_params=pltpu.CompilerParams(has_side_effects=EFFECT),
    )(*bufs, *sems, after)


def _plan_gather(n_w):
    def plan(refs):
        x, y, c, chips = _place()
        me = 2 * x + y
        return [(refs[a], refs[n_w + a].at[me], (*chip, c)) for a in range(n_w) for chip in chips]
    return plan


def _plan_pair_halves(n_g, rows):
    def plan(refs):
        x, y, c, _ = _place()
        return [(refs[a].at[:, _half(1 - c, rows[a])], refs[n_g + a], (x, y, 1 - c)) for a in range(n_g)]
    return plan


def _plan_chip_parts(n_g):
    def plan(refs):
        x, y, c, chips = _place()
        me = 2 * x + y
        return [(refs[a].at[2 * px + py], refs[n_g + a].at[me], (px, py, c))
                for a in range(n_g) for (px, py) in chips]
    return plan


def _plan_pair_fill(n_g, rows):
    def plan(refs):
        x, y, c, _ = _place()
        return [(refs[a].at[_half(c, rows[a])], refs[a].at[_half(c, rows[a])], (x, y, 1 - c)) for a in range(n_g)]
    return plan


def _pair_add(g, got, name):
    _, rows, cols = g.shape
    hr = rows // 2
    tr = min(hr, 256)
    nb = hr // tr

    def body(g_ref, got_ref, p_ref, own_ref):
        val = (g_ref[...] + got_ref[...]).astype(BF16)
        p_ref[...] = val

        @pl.when(pl.program_id(1) == 2 * lax.axis_index("x") + lax.axis_index("y"))
        def _():
            own_ref[...] = val

    def mine(i, s):
        return (2 * lax.axis_index("x") + lax.axis_index("y"), i, 0)

    return pl.pallas_call(
        body, name=name, grid=(nb, N_CHIPS),
        in_specs=[pl.BlockSpec((None, tr, cols), lambda i, s: (s, lax.axis_index("c") * nb + i, 0)),
                  pl.BlockSpec((None, tr, cols), lambda i, s: (s, i, 0))],
        out_specs=[pl.BlockSpec((None, tr, cols), lambda i, s: (s, i, 0)), pl.BlockSpec((None, tr, cols), mine)],
        out_shape=[jax.ShapeDtypeStruct((N_CHIPS, hr, cols), BF16)] * 2,
        compiler_params=_cp(("parallel", "arbitrary")),
    )(g, got)


def _chip_sum(parts, name):
    _, hr, cols = parts.shape
    tr = min(hr, 128)
    nb = hr // tr

    def body(t_ref, o_ref):
        part = [t_ref[s].astype(F32) for s in range(N_CHIPS)]
        o_ref[...] = ((part[0] + part[1]) + part[2]) + part[3]

    return pl.pallas_call(
        body, name=name, grid=(nb,),
        in_specs=[pl.BlockSpec((N_CHIPS, tr, cols), lambda i: (0, i, 0))],
        out_specs=pl.BlockSpec((tr, cols), lambda i: (lax.axis_index("c") * nb + i, 0)),
        out_shape=jax.ShapeDtypeStruct((2 * hr, cols), F32),
        compiler_params=_cp(("parallel",)),
    )(parts)


class _Reduction:
    def __init__(self, tag, rows):
        self.tag, self.n, self.rows = tag, len(rows), rows
        self.plans = (_plan_pair_halves(self.n, rows), _plan_chip_parts(self.n), _plan_pair_fill(self.n, rows))
        self.flight = None

    def _name(self, what):
        return f"grad_{self.tag}_{what}"

    def start(self, gs):
        gots = [lax.empty((N_CHIPS, t.shape[1] // 2, t.shape[2]), F32) for t in gs]
        self.flight = _copies_start(self._name("pair_start"), list(gs) + gots, self.plans[0], self.n)
        return self.flight[2]

    def after_pair(self, after):
        sems, bufs, _ = self.flight
        out = _copies_wait(self._name("pair_wait"), sems, bufs, self.plans[0], self.n, after)
        sums = [_pair_add(g, got, self._name(f"pair_add_{i}"))
                for i, (g, got) in enumerate(zip(out[:self.n], out[self.n:]))]
        self.flight = _copies_start(self._name("chip_start"), [p for p, _ in sums] + [own for _, own in sums],
                                    self.plans[1], 3 * self.n)
        return self.flight[2]

    def after_chips(self, after):
        sems, bufs, _ = self.flight
        out = _copies_wait(self._name("chip_wait"), sems, bufs, self.plans[1], 3 * self.n, after)
        fulls = [_chip_sum(t, self._name(f"chip_sum_{i}")) for i, t in enumerate(out[self.n:])]
        self.flight = _copies_start(self._name("fill_start"), fulls, self.plans[2], self.n)
        return self.flight[2]

    def finish(self, after):
        sems, bufs, _ = self.flight
        return _copies_wait(self._name("fill_wait"), sems, bufs, self.plans[2], self.n, after)


def _adamw_math(w, g, m, v):
    nm = ADAM_B1 * m + (1.0 - ADAM_B1) * g
    nv = ADAM_B2 * v + (1.0 - ADAM_B2) * (g * g)
    m_hat = nm / (1.0 - ADAM_B1 ** ADAM_STEP)
    v_hat = nv / (1.0 - ADAM_B2 ** ADAM_STEP)
    return -ADAM_LR * (m_hat / (jnp.sqrt(v_hat) + ADAM_EPS) + ADAM_WD * w), nm, nv


def _adamw(w, g, m, v, name, tr):
    r, cdim = w.shape

    def body(w_ref, g_ref, m_ref, v_ref, d_ref, nm_ref, nv_ref):
        d_ref[...], nm_ref[...], nv_ref[...] = _adamw_math(w_ref[...], g_ref[...], m_ref[...], v_ref[...])

    return pl.pallas_call(
        body, name=name, grid=(r // tr,), in_specs=[_rows(tr, cdim)] * 4, out_specs=[_rows(tr, cdim)] * 3,
        out_shape=[jax.ShapeDtypeStruct((r, cdim), F32)] * 3, compiler_params=_cp(("parallel",)),
    )(w, g, m, v)


def _adamw_small(w, parts, m, v):
    def body(w_ref, p_ref, m_ref, v_ref, d_ref, nm_ref, nv_ref, g_ref):
        g = p_ref[0]
        for dev in range(1, N_DEV):
            g = g + p_ref[dev]
        g_ref[...] = g
        d_ref[...], nm_ref[...], nv_ref[...] = _adamw_math(w_ref[...], g, m_ref[...], v_ref[...])

    return pl.pallas_call(
        body, name="adamw_small", grid=(1,),
        in_specs=[_const(w.shape), _const(parts.shape), _const(w.shape), _const(w.shape)],
        out_specs=[_const(w.shape)] * 4, out_shape=[jax.ShapeDtypeStruct(w.shape, F32)] * 4,
        compiler_params=_cp(("arbitrary",)),
    )(w, parts, m, v)


REPLICATED = (("norm_mix_pre", 1024), ("norm_mix_post", 1024), ("norm_ffn_pre", 1024), ("norm_ffn_post", 1024),
              ("rel_bias", 256), ("sinks", 8), ("rwkv_shift_mix", 1792), ("w0", 512), ("a0", 512), ("k_k", 512),
              ("k_a", 512), ("r_k", 512), ("ln_x_g", 512), ("ln_x_b", 512), ("conv_b", 8192))
SMALL_SHARDED = (("w_decay_up", LORA_DECAY, D_RWKV), ("w_iclr_up", LORA_ICLR, D_RWKV),
                 ("w_gate_up", LORA_GATE, D_RWKV), ("conv_w", 3, 2 * D_FF))
BIG = (("w_in", D_MODEL, 640), ("w_out", 256, D_MODEL), ("w_ffn_up", D_MODEL, 2048), ("w_ffn_down", 1024, D_MODEL))
PACK_ALIGN = 8 * LANES


def _pack(pieces):
    flat = []
    for t in pieces:
        t = t.reshape(-1)
        pad = (-t.shape[0]) % LANES
        flat.append(jnp.pad(t, (0, pad)) if pad else t)
    flat = jnp.concatenate(flat)
    pad = (-flat.shape[0]) % PACK_ALIGN
    return jnp.pad(flat, (0, pad)).reshape(-1, LANES)


def _unpack(buf, sizes):
    flat, out, off = buf.reshape(-1), [], 0
    for n in sizes:
        out.append(flat[off:off + n])
        off += n + ((-n) % LANES)
    return out


def kernel(x, norm_mix_pre, norm_mix_post, norm_ffn_pre, norm_ffn_post, w_in, rel_bias, sinks, rwkv_shift_mix, w0, w_decay_up, a0, w_iclr_up, w_gate_up, k_k, k_a, r_k, ln_x_g, ln_x_b, w_out, w_ffn_up, conv_w, conv_b, w_ffn_down, loss_target, m_norm_mix_pre, m_norm_mix_post, m_norm_ffn_pre, m_norm_ffn_post, m_w_in, m_rel_bias, m_sinks, m_rwkv_shift_mix, m_w0, m_w_decay_up, m_a0, m_w_iclr_up, m_w_gate_up, m_k_k, m_k_a, m_r_k, m_ln_x_g, m_ln_x_b, m_w_out, m_w_ffn_up, m_conv_w, m_conv_b, m_w_ffn_down, v_norm_mix_pre, v_norm_mix_post, v_norm_ffn_pre, v_norm_ffn_post, v_w_in, v_rel_bias, v_sinks, v_rwkv_shift_mix, v_w0, v_w_decay_up, v_a0, v_w_iclr_up, v_w_gate_up, v_k_k, v_k_a, v_r_k, v_ln_x_g, v_ln_x_b, v_w_out, v_w_ffn_up, v_conv_w, v_conv_b, v_w_ffn_down):
    given = dict(locals())
    names = [n for n, _ in REPLICATED] + [n for n, _, _ in SMALL_SHARDED] + [n for n, _, _ in BIG]
    order = ["norm_mix_pre", "norm_mix_post", "norm_ffn_pre", "norm_ffn_post", "w_in", "rel_bias", "sinks",
             "rwkv_shift_mix", "w0", "w_decay_up", "a0", "w_iclr_up", "w_gate_up", "k_k", "k_a", "r_k", "ln_x_g",
             "ln_x_b", "w_out", "w_ffn_up", "conv_w", "conv_b", "w_ffn_down"]
    assert sorted(names) == sorted(order)
    shard = 2 * lax.axis_index("x") + lax.axis_index("y")

    big_sh = {n: given[n].reshape(a, b).astype(BF16) for n, a, b in BIG}
    small_sh = [given[n].reshape(r, c // N_CHIPS) for n, r, c in SMALL_SHARDED]
    gathered = _gather_weights([big_sh["w_in"]], small_sh)
    rest = ("w_out", "w_ffn_up", "w_ffn_down")
    win_st, rest_sh = lax.optimization_barrier((gathered[0], [big_sh[n] for n in rest]))
    sm = {n: given[n] for n, _ in REPLICATED}
    sm["r_k"] = r_k.reshape(N_Q_HEADS, HEAD_DIM)
    for (n, r, c), st in zip(SMALL_SHARDED, gathered[1:]):
        sm[n] = st.transpose(1, 0, 2).reshape(r, c)

    lands = [lax.dynamic_update_slice(lax.empty((N_CHIPS,) + t.shape, BF16), t[None], (shard, 0, 0)) for t in rest_sh]
    plan_w = _plan_gather(len(rest))
    w_sems, w_bufs, token = _copies_start("gather_rest_start", rest_sh + lands, plan_w, 9)
    sm["norm_mix_pre"] = norm_mix_pre + token[0:1, 0:1]

    def on_rest_weights(after):
        out = _copies_wait("gather_rest_wait", w_sems, w_bufs, plan_w, 9, after)
        wout_st, wup_st, wdown_st = out[3:]
        return wout_st.reshape(D_MODEL, D_MODEL), wup_st, wdown_st.reshape(D_FF, D_MODEL)

    red_a = _Reduction("a", (1024, D_MODEL, 256))
    red_b = _Reduction("b", (D_MODEL,))

    def on_grads_a(gs):
        d_wdown, d_wup, d_wout = gs
        return red_a.start([d_wdown.reshape(N_CHIPS, 1024, D_MODEL), d_wup, d_wout.reshape(N_CHIPS, 256, D_MODEL)])

    handlers = {"rest_weights": on_rest_weights, "grads_a": on_grads_a, "seam_1": red_a.after_pair,
                "seam_2": red_a.after_chips, "grads_b": lambda g: red_b.start([g])}
    steps = _local_step(x[0], loss_target[0], sm, win_st)
    kind, payload = next(steps)
    while True:
        try:
            kind, payload = steps.send(handlers[kind](payload))
        except StopIteration as done:
            loss, grad_x, grads = done.value
            break
    loss = lax.psum(loss[0, 0], ("x", "y", "c"))

    small_names = [n for n, _ in REPLICATED] + [n for n, _, _ in SMALL_SHARDED]

    def shard_cols(t, s):
        return t[:, s * (t.shape[1] // N_CHIPS):(s + 1) * (t.shape[1] // N_CHIPS)]

    for_chip = jnp.stack([_pack([grads[n] for n, _ in REPLICATED]
                                + [shard_cols(grads[n], s) for n, _, _ in SMALL_SHARDED]) for s in range(N_CHIPS)])
    me = 2 * shard + lax.axis_index("c")
    mine = lax.dynamic_index_in_dim(for_chip, shard, 0, keepdims=True)
    land = lax.dynamic_update_slice(lax.empty((N_DEV,) + for_chip.shape[1:], F32), mine, (me, 0, 0))

    def plan_small(refs):
        x, y, c, _ = _place()
        out = []
        for rel in range(1, N_DEV):
            px, py, pc = x ^ (rel >> 2), y ^ ((rel >> 1) & 1), c ^ (rel & 1)
            out.append((refs[0].at[2 * px + py], refs[1].at[4 * x + 2 * y + c], (px, py, pc)))
        return out

    s_sems, s_bufs, _ = _copies_start("grad_small_start", [for_chip, land], plan_small, N_DEV - 1)

    red_b.after_pair(grad_x)
    g_out = {}
    g_out["w_ffn_down"], g_out["w_ffn_up"], g_out["w_out"] = red_a.finish(grad_x)

    delta, new_m, new_v = {}, {}, {}
    for n, a, b in reversed(BIG):
        if n == "w_out":
            red_b.after_chips(delta["w_ffn_up"])
        if n == "w_in":
            parts = _copies_wait("grad_small_wait", s_sems, s_bufs, plan_small, N_DEV - 1, delta["w_out"])[1]
            packs = [_pack([given[pre + n2] for n2 in small_names]) for pre in ("", "m_", "v_")]
            small_sizes = [int(np.prod(given[n2].shape)) for n2 in small_names]
            upd = [_unpack(t, small_sizes) for t in _adamw_small(packs[0], parts, packs[1], packs[2])]
            for n2, d, nm, nv, g in zip(small_names, *upd):
                shape = given[n2].shape
                delta[n2], new_m[n2], new_v[n2], g_out[n2] = (t.reshape(shape) for t in (d, nm, nv, g))
            g_out[n], = red_b.finish(delta["w_out"])
        d, nm, nv = _adamw(given[n].reshape(a, b), g_out[n], given["m_" + n].reshape(a, b),
                           given["v_" + n].reshape(a, b), "adamw_" + n, 128)
        delta[n], new_m[n], new_v[n] = d, nm, nv

    def shaped(d):
        return [d[n].reshape(given[n].shape) for n in order]

    return (loss, grad_x.reshape(x.shape), *shaped(g_out), *shaped(delta), *shaped(new_m), *shaped(new_v))
```

```python
import math

import numpy as np
import jax
import jax.numpy as jnp
from jax import lax
from jax.experimental import pallas as pl
from jax.experimental.pallas import tpu as pltpu

F32 = jnp.float32
BF16 = jnp.bfloat16
MESH = pl.DeviceIdType.MESH

SEQ = 2048
D_MODEL = 1024
HEAD_DIM = 64
D_ATTN = 512
D_RWKV = 512
D_KV = 128
N_Q_HEADS = 8
N_KV_HEADS = 2
Q_PER_KV = 4
BLOCK = 128
N_BUCKETS = 32
MAX_DISTANCE = 128
LORA_DECAY = 64
LORA_ICLR = 64
LORA_GATE = 128
RWKV_COLS = 3 * D_RWKV + LORA_DECAY + LORA_ICLR + LORA_GATE
P_OFF = D_ATTN + 2 * D_KV
D_IN = P_OFF + RWKV_COLS
D_FF = 4096
NORM_EPS = 1e-6
GN_EPS = 64e-5
NEG_INF = -1e30
N_CHIPS = 4
N_DEV = 8

ADAM_LR = 0.001
ADAM_B1 = 0.9
ADAM_B2 = 0.999
ADAM_EPS = 1e-08
ADAM_WD = 0.01
ADAM_STEP = 10

VMEM_LIMIT = 52 * 1024 * 1024
LANES = 128


def _cp(sem=None, vmem=VMEM_LIMIT):
    kw = dict(vmem_limit_bytes=vmem)
    if sem is not None:
        kw["dimension_semantics"] = sem
    return pltpu.CompilerParams(**kw)


def _rows(tr, nc):
    return pl.BlockSpec((tr, nc), lambda i: (i, 0))


def _const(shape):
    return pl.BlockSpec(shape, lambda *_: (0,) * len(shape))


ANY = pl.BlockSpec(memory_space=pl.ANY)


def _split(x, n):
    parts = []
    for _ in range(n - 1):
        h = x.astype(BF16)
        parts.append(h)
        x = x - h.astype(F32)
    parts.append(x.astype(BF16))
    return parts


def _dot(a, b, dn=(((1,), (0,)), ((), ()))):
    return lax.dot_general(a, b, dn, preferred_element_type=F32)


NN = (((1,), (0,)), ((), ()))
NT = (((1,), (1,)), ((), ()))
TN = (((0,), (0,)), ((), ()))


def _dot_ind(x, ind_bf16, n=3):
    acc = None
    for part in _split(x, n):
        t = _dot(part, ind_bf16)
        acc = t if acc is None else acc + t
    return acc


def _head_ones(n):
    r = lax.broadcasted_iota(jnp.int32, (n, n), 0) >> 6
    c = lax.broadcasted_iota(jnp.int32, (n, n), 1) >> 6
    return jnp.where(r == c, 1.0, 0.0).astype(BF16)


def _matmul(a, b, mode, name, *, m, n, k, tm, tn, a_spec=None, b_spec=None, out=None, out_dtype=F32):
    keep_at = mode == "tn" and m == tm and n > tn

    def body(a_ref, b_ref, o_ref, *scratch):
        if keep_at:
            at_ref, = scratch

            @pl.when(pl.program_id(1) == 0)
            def _():
                at_ref[...] = a_ref[...].T

            o_ref[...] = _dot(at_ref[...], b_ref[...], NN).astype(out_dtype)
        else:
            o_ref[...] = _dot(a_ref[...], b_ref[...], {"nn": NN, "nt": NT, "tn": TN}[mode]).astype(out_dtype)

    if a_spec is None:
        a_spec = pl.BlockSpec((k, tm), lambda i, j: (0, i)) if mode == "tn" else pl.BlockSpec((tm, k), lambda i, j: (i, 0))
    if b_spec is None:
        b_spec = pl.BlockSpec((tn, k), lambda i, j: (j, 0)) if mode == "nt" else pl.BlockSpec((k, tn), lambda i, j: (0, j))
    return pl.pallas_call(
        body, name=name, grid=(m // tm, n // tn),
        in_specs=[a_spec, b_spec],
        out_specs=pl.BlockSpec((tm, tn), lambda i, j: (i, j)) if out is None else out[1],
        out_shape=jax.ShapeDtypeStruct((m, n) if out is None else out[0], out_dtype),
        scratch_shapes=[pltpu.VMEM((tm, k), a.dtype)] if keep_at else [],
        compiler_params=_cp(("parallel", "arbitrary" if keep_at else "parallel")),
    )(a, b)


def _matmul_nt_shards(a, b_st, name, *, m, n, tm, tn, a_spec, a_piece):
    ks = b_st.shape[2]

    def body(a_ref, b_ref, o_ref):
        acc = _dot(a_piece(a_ref, 0), b_ref[0], NT)
        for s in range(1, N_CHIPS):
            acc = acc + _dot(a_piece(a_ref, s), b_ref[s], NT)
        o_ref[...] = acc

    return pl.pallas_call(
        body, name=name, grid=(m // tm, n // tn),
        in_specs=[a_spec, pl.BlockSpec((N_CHIPS, tn, ks), lambda i, j: (0, j, 0))],
        out_specs=pl.BlockSpec((tm, tn), lambda i, j: (i, j)),
        out_shape=jax.ShapeDtypeStruct((m, n), F32),
        compiler_params=_cp(("parallel", "parallel")),
    )(a, b_st)


def _rstd(x):
    return lax.rsqrt(jnp.mean(x * x, axis=-1, keepdims=True) + NORM_EPS)


def _rms_bwd(x, r, g, dy):
    gy = dy * g
    return r * gy - x * ((r * r * r) * (jnp.sum(x * gy, axis=-1, keepdims=True) / x.shape[-1]))


TR = 256


def _norm_cast(x, g, name):
    def body(x_ref, g_ref, h_ref):
        x = x_ref[...]
        h_ref[...] = (x * _rstd(x) * g_ref[...]).astype(BF16)

    return pl.pallas_call(
        body, name=name, grid=(SEQ // TR,),
        in_specs=[_rows(TR, D_MODEL), _const((1, D_MODEL))],
        out_specs=_rows(TR, D_MODEL),
        out_shape=jax.ShapeDtypeStruct((SEQ, D_MODEL), BF16),
        compiler_params=_cp(("parallel",)),
    )(x, g)


def _mix_norm(x, mix, g2, g3):
    def body(x_ref, mix_ref, g2_ref, g3_ref, x2_ref, h3_ref):
        mixv = mix_ref[...]
        x2 = x_ref[...] + mixv * _rstd(mixv) * g2_ref[...]
        x2_ref[...] = x2
        h3_ref[...] = (x2 * _rstd(x2) * g3_ref[...]).astype(BF16)

    return pl.pallas_call(
        body, name="mix_norm", grid=(SEQ // TR,),
        in_specs=[_rows(TR, D_MODEL), _rows(TR, D_MODEL), _const((1, D_MODEL)), _const((1, D_MODEL))],
        out_specs=[_rows(TR, D_MODEL), _rows(TR, D_MODEL)],
        out_shape=[jax.ShapeDtypeStruct((SEQ, D_MODEL), F32), jax.ShapeDtypeStruct((SEQ, D_MODEL), BF16)],
        compiler_params=_cp(("parallel",)),
    )(x, mix, g2, g3)


def _loss_head(x2, f, g4, target):
    def body(x2_ref, f_ref, g4_ref, t_ref, loss_ref, dy_ref, df_ref, dg_ref):
        i = pl.program_id(0)
        f = f_ref[...]
        g4 = g4_ref[...]
        r = _rstd(f)
        e = x2_ref[...] + f * r * g4 - t_ref[...]
        dy = e * (1.0 / D_MODEL)
        dy_ref[...] = dy
        df_ref[...] = _rms_bwd(f, r, g4, dy).astype(BF16)
        part = 0.5 * jnp.sum(jnp.sum(e * e, axis=-1, keepdims=True), axis=0, keepdims=True) * (1.0 / D_MODEL)
        dg = jnp.sum(dy * f * r, axis=0, keepdims=True)

        @pl.when(i == 0)
        def _():
            loss_ref[...] = jnp.zeros_like(loss_ref)
            dg_ref[...] = jnp.zeros_like(dg_ref)

        loss_ref[...] += jnp.broadcast_to(part, loss_ref.shape)
        dg_ref[...] += dg

    return pl.pallas_call(
        body, name="loss_head", grid=(SEQ // TR,),
        in_specs=[_rows(TR, D_MODEL), _rows(TR, D_MODEL), _const((1, D_MODEL)), _rows(TR, D_MODEL)],
        out_specs=[_const((8, LANES)), _rows(TR, D_MODEL), _rows(TR, D_MODEL), _const((1, D_MODEL))],
        out_shape=[jax.ShapeDtypeStruct((8, LANES), F32), jax.ShapeDtypeStruct((SEQ, D_MODEL), F32),
                   jax.ShapeDtypeStruct((SEQ, D_MODEL), BF16), jax.ShapeDtypeStruct((1, D_MODEL), F32)],
        compiler_params=_cp(("arbitrary",)),
    )(x2, f, g4, target)


def _mid_bwd(x2, mix, dy, dh3, g2, g3):
    def body(x2_ref, mix_ref, dy_ref, dh3_ref, g2_ref, g3_ref, dx2_ref, dmix_ref, dg2_ref, dg3_ref):
        i = pl.program_id(0)
        x2 = x2_ref[...]
        mixv = mix_ref[...]
        dh3 = dh3_ref[...]
        r3 = _rstd(x2)
        dx2 = dy_ref[...] + _rms_bwd(x2, r3, g3_ref[...], dh3)
        dx2_ref[...] = dx2
        r2 = _rstd(mixv)
        dmix_ref[...] = _rms_bwd(mixv, r2, g2_ref[...], dx2).astype(BF16)

        @pl.when(i == 0)
        def _():
            dg2_ref[...] = jnp.zeros_like(dg2_ref)
            dg3_ref[...] = jnp.zeros_like(dg3_ref)

        dg3_ref[...] += jnp.sum(dh3 * x2 * r3, axis=0, keepdims=True)
        dg2_ref[...] += jnp.sum(dx2 * mixv * r2, axis=0, keepdims=True)

    return pl.pallas_call(
        body, name="mid_bwd", grid=(SEQ // TR,),
        in_specs=[_rows(TR, D_MODEL)] * 4 + [_const((1, D_MODEL))] * 2,
        out_specs=[_rows(TR, D_MODEL), _rows(TR, D_MODEL), _const((1, D_MODEL)), _const((1, D_MODEL))],
        out_shape=[jax.ShapeDtypeStruct((SEQ, D_MODEL), F32), jax.ShapeDtypeStruct((SEQ, D_MODEL), BF16),
                   jax.ShapeDtypeStruct((1, D_MODEL), F32), jax.ShapeDtypeStruct((1, D_MODEL), F32)],
        compiler_params=_cp(("arbitrary",)),
    )(x2, mix, dy, dh3, g2, g3)


def _first_bwd(x, dx2, dh1, g1):
    def body(x_ref, dx2_ref, dh1_ref, g1_ref, dx_ref, dg1_ref):
        i = pl.program_id(0)
        x = x_ref[...]
        dh1 = dh1_ref[...]
        r = _rstd(x)
        dx_ref[...] = dx2_ref[...] + _rms_bwd(x, r, g1_ref[...], dh1)

        @pl.when(i == 0)
        def _():
            dg1_ref[...] = jnp.zeros_like(dg1_ref)

        dg1_ref[...] += jnp.sum(dh1 * x * r, axis=0, keepdims=True)

    return pl.pallas_call(
        body, name="first_bwd", grid=(SEQ // TR,),
        in_specs=[_rows(TR, D_MODEL)] * 3 + [_const((1, D_MODEL))],
        out_specs=[_rows(TR, D_MODEL), _const((1, D_MODEL))],
        out_shape=[jax.ShapeDtypeStruct((SEQ, D_MODEL), F32), jax.ShapeDtypeStruct((1, D_MODEL), F32)],
        compiler_params=_cp(("arbitrary",)),
    )(x, dx2, dh1, g1)


TC = 256
N_CB = D_FF // TC
GELU_C = math.sqrt(2.0 / math.pi)


def _shift_down(u, s):
    rolled = pltpu.roll(u, s, 0)
    row = lax.broadcasted_iota(jnp.int32, u.shape, 0)
    return jnp.where(row >= s, rolled, 0.0)


def _shift_up(u, s):
    n = u.shape[0]
    rolled = pltpu.roll(u, n - s, 0)
    row = lax.broadcasted_iota(jnp.int32, u.shape, 0)
    return jnp.where(row < n - s, rolled, 0.0)


def _conv3(u, w, b):
    return b + w[0:1] * _shift_down(u, 2) + w[1:2] * _shift_down(u, 1) + w[2:3] * u


def _gelu_and_grad(x):
    inner = GELU_C * (x + 0.044715 * (x * x * x))
    t = jnp.tanh(inner)
    gelu = 0.5 * x * (1.0 + t)
    dgelu = 0.5 * (1.0 + t) + 0.5 * x * (1.0 - t * t) * (GELU_C * (1.0 + 3 * 0.044715 * (x * x)))
    return gelu, dgelu


def _ffn_specs():
    col = lambda off: pl.BlockSpec((SEQ, TC), lambda *g: (0, g[-1] + off))
    w = lambda off: pl.BlockSpec((3, TC), lambda *g: (0, g[-1] + off))
    b = lambda off: pl.BlockSpec((1, TC), lambda *g: (0, g[-1] + off))
    return col, w, b


def _ffn_act(u, conv_w, conv_b):
    col, w, b = _ffn_specs()

    def body(ug_ref, uv_ref, wg_ref, wv_ref, bg_ref, bv_ref, act_ref):
        gate = _conv3(ug_ref[...], wg_ref[...], bg_ref[...])
        val = _conv3(uv_ref[...], wv_ref[...], bv_ref[...])
        act_ref[...] = (_gelu_and_grad(gate)[0] * val).astype(BF16)

    return pl.pallas_call(
        body, name="ffn_act", grid=(N_CB,),
        in_specs=[col(0), col(N_CB), w(0), w(N_CB), b(0), b(N_CB)],
        out_specs=col(0),
        out_shape=jax.ShapeDtypeStruct((SEQ, D_FF), BF16),
        compiler_params=_cp(("parallel",)),
    )(u, u, conv_w, conv_w, conv_b, conv_b)


def _ffn_act_bwd(u, dact, conv_w, conv_b):
    col, w, b = _ffn_specs()
    both = lambda rows: pl.BlockSpec((2, rows, TC), lambda j: (0, 0, j))

    def body(ug_ref, uv_ref, da_ref, wg_ref, wv_ref, bg_ref, bv_ref, du_ref, dw_ref, db_ref):
        ug, uv = ug_ref[...], uv_ref[...]
        wg, wv = wg_ref[...], wv_ref[...]
        gate = _conv3(ug, wg, bg_ref[...])
        val = _conv3(uv, wv, bv_ref[...])
        gelu, dgelu = _gelu_and_grad(gate)
        da = da_ref[...]
        for h, (duc, uh, wh) in enumerate(((da * val * dgelu, ug, wg), (da * gelu, uv, wv))):
            up1, up2 = _shift_up(duc, 1), _shift_up(duc, 2)
            du_ref[h] = (wh[2:3] * duc + wh[1:2] * up1 + wh[0:1] * up2).astype(BF16)
            db_ref[h] = jnp.sum(duc, axis=0, keepdims=True)
            dw_ref[h] = jnp.concatenate(
                [jnp.sum(up2 * uh, axis=0, keepdims=True), jnp.sum(up1 * uh, axis=0, keepdims=True),
                 jnp.sum(duc * uh, axis=0, keepdims=True)], axis=0)

    return pl.pallas_call(
        body, name="ffn_act_bwd", grid=(N_CB,),
        in_specs=[col(0), col(N_CB), col(0), w(0), w(N_CB), b(0), b(N_CB)],
        out_specs=[both(SEQ), both(3), both(1)],
        out_shape=[jax.ShapeDtypeStruct((2, SEQ, D_FF), BF16), jax.ShapeDtypeStruct((2, 3, D_FF), F32),
                   jax.ShapeDtypeStruct((2, 1, D_FF), F32)],
        compiler_params=_cp(("parallel",)),
    )(u, u, dact, conv_w, conv_w, conv_b, conv_b)


def _t5_onehot():
    rel = (np.arange(BLOCK)[:, None] + BLOCK) - np.arange(2 * BLOCK)[None, :]
    n = np.maximum(rel, 0)
    max_exact = N_BUCKETS // 2
    large = max_exact + (np.log(np.maximum(n, 1).astype(np.float32) / np.float32(max_exact))
                         / np.float32(math.log(MAX_DISTANCE / max_exact))
                         * np.float32(N_BUCKETS - max_exact)).astype(np.int32)
    large = np.minimum(large, N_BUCKETS - 1)
    bucket = np.where(n < max_exact, n, large).reshape(-1)
    return (bucket[None, :] == np.arange(N_BUCKETS)[:, None]).astype(np.float32)


N_REL = BLOCK * 2 * BLOCK


def _bias_table(rel_bias_t, onehot):
    def body(rb_ref, oh_ref, o_ref):
        o_ref[...] = _dot_ind(rb_ref[...], oh_ref[...])

    return pl.pallas_call(
        body, name="bias_table", grid=(1,),
        in_specs=[_const((N_Q_HEADS, N_BUCKETS)), _const((N_BUCKETS, N_REL))],
        out_specs=_const((N_Q_HEADS, N_REL)),
        out_shape=jax.ShapeDtypeStruct((N_Q_HEADS, N_REL), F32),
        compiler_params=_cp(("arbitrary",)),
    )(rel_bias_t, onehot)


def _bias_table_bwd(dbias, onehot):
    def body(db_ref, oh_ref, o_ref):
        acc = None
        for part in _split(db_ref[...], 3):
            t = _dot(part, oh_ref[...], NT)
            acc = t if acc is None else acc + t
        o_ref[...] = acc

    return pl.pallas_call(
        body, name="bias_table_bwd", grid=(1,),
        in_specs=[_const((N_Q_HEADS, N_REL)), _const((N_BUCKETS, N_REL))],
        out_specs=_const((N_Q_HEADS, N_BUCKETS)),
        out_shape=jax.ShapeDtypeStruct((N_Q_HEADS, N_BUCKETS), F32),
        compiler_params=_cp(("arbitrary",)),
    )(dbias, onehot)


def _attn_pieces(n, q, kvp, kvc, bias_ref, sinks_ref, hk):
    qi = lax.broadcasted_iota(jnp.int32, (BLOCK, 2 * BLOCK), 0)
    kj = lax.broadcasted_iota(jnp.int32, (BLOCK, 2 * BLOCK), 1)
    rel = qi + BLOCK - kj
    first_key = jnp.where(n > 0, 0, BLOCK)
    ok = jnp.where(rel >= 0, jnp.where(rel < BLOCK, jnp.where(kj >= first_key, 1.0, 0.0), 0.0), 0.0)
    ok4 = jnp.concatenate([ok] * Q_PER_KV, axis=0) > 0.5
    c0 = hk * HEAD_DIM
    kcat = jnp.concatenate([kvp[:, c0:c0 + HEAD_DIM], kvc[:, c0:c0 + HEAD_DIM]], axis=0).astype(BF16)
    vcat = jnp.concatenate([kvp[:, D_KV + c0:D_KV + c0 + HEAD_DIM], kvc[:, D_KV + c0:D_KV + c0 + HEAD_DIM]],
                           axis=0).astype(BF16)
    q0 = hk * Q_PER_KV * HEAD_DIM
    qs = jnp.concatenate([q[:, q0 + g * HEAD_DIM:q0 + (g + 1) * HEAD_DIM] for g in range(Q_PER_KV)],
                         axis=0).astype(BF16)
    s = _dot(qs, kcat, NT) * (HEAD_DIM ** -0.5) + bias_ref[hk]
    s = jnp.where(ok4, s, NEG_INF)
    row = lax.broadcasted_iota(jnp.int32, (Q_PER_KV * BLOCK, 1), 0)
    sink = jnp.zeros((Q_PER_KV * BLOCK, 1), F32)
    for g in range(Q_PER_KV):
        sink = jnp.where((row >> 7) == g, sinks_ref[hk * Q_PER_KV + g], sink)
    m = jnp.maximum(jnp.max(s, axis=-1, keepdims=True), sink)
    p = jnp.exp(s - m)
    es = jnp.exp(sink - m)
    inv = 1.0 / (jnp.sum(p, axis=-1, keepdims=True) + es)
    return qs, kcat, vcat, p * inv, es * inv


def _attn_in_specs():
    return [pl.BlockSpec((BLOCK, D_ATTN), lambda n: (n, 0)),
            pl.BlockSpec((BLOCK, 2 * D_KV), lambda n: (jnp.maximum(n - 1, 0), D_ATTN // (2 * D_KV))),
            pl.BlockSpec((BLOCK, 2 * D_KV), lambda n: (n, D_ATTN // (2 * D_KV))),
            _const((N_KV_HEADS, Q_PER_KV * BLOCK, 2 * BLOCK)),
            pl.BlockSpec(memory_space=pltpu.SMEM)]


def _unstack_heads(t):
    return jnp.concatenate([t[g * BLOCK:(g + 1) * BLOCK] for g in range(Q_PER_KV)], axis=1)


def _attn_fwd(proj, bias, sinks):
    def body(q_ref, kvp_ref, kvc_ref, bias_ref, sinks_ref, o_ref):
        n = pl.program_id(0)
        q, kvp, kvc = q_ref[...], kvp_ref[...], kvc_ref[...]
        outs = []
        for hk in range(N_KV_HEADS):
            _, _, vcat, probs, _ = _attn_pieces(n, q, kvp, kvc, bias_ref, sinks_ref, hk)
            outs.append(_unstack_heads(_dot(probs.astype(BF16), vcat)))
        o_ref[...] = jnp.concatenate(outs, axis=1)

    return pl.pallas_call(
        body, name="attn_fwd", grid=(SEQ // BLOCK,),
        in_specs=_attn_in_specs(),
        out_specs=pl.BlockSpec((BLOCK, D_ATTN), lambda n: (n, 0)),
        out_shape=jax.ShapeDtypeStruct((SEQ, D_ATTN), F32),
        compiler_params=_cp(("parallel",)),
    )(proj, proj, proj, bias, sinks)


def _attn_bwd(proj, bias, sinks, dcat):
    nb = SEQ // BLOCK

    def body(q_ref, kvp_ref, kvc_ref, bias_ref, sinks_ref, do_ref, dq_ref, dkv_ref, dbias_ref, dsink_ref, dsacc):
        n = pl.program_id(0)

        @pl.when(n == 0)
        def _():
            dkv_ref[...] = jnp.zeros_like(dkv_ref)
            dbias_ref[...] = jnp.zeros_like(dbias_ref)
            dsacc[...] = jnp.zeros_like(dsacc)

        q, kvp, kvc = q_ref[...], kvp_ref[...], kvc_ref[...]
        do_all = do_ref[...]
        dqs, dks, dvs = [], [], []
        for hk in range(N_KV_HEADS):
            qs, kcat, vcat, probs, psink = _attn_pieces(n, q, kvp, kvc, bias_ref, sinks_ref, hk)
            q0 = hk * Q_PER_KV * HEAD_DIM
            do = jnp.concatenate([do_all[:, q0 + g * HEAD_DIM:q0 + (g + 1) * HEAD_DIM] for g in range(Q_PER_KV)],
                                 axis=0).astype(BF16)
            dprobs = _dot(do, vcat, NT)
            dvs.append(_dot(probs.astype(BF16), do, TN))
            rowdot = jnp.sum(probs * dprobs, axis=-1, keepdims=True)
            ds = probs * (dprobs - rowdot)
            dsacc[hk] += -psink * rowdot
            dbias_ref[hk] += ds
            dsb = (ds * (HEAD_DIM ** -0.5)).astype(BF16)
            dqs.append(_unstack_heads(_dot(dsb, kcat)))
            dks.append(_dot(dsb, qs, TN))
        dq_ref[...] = jnp.concatenate(dqs, axis=1)
        upd = jnp.concatenate(dks + dvs, axis=1)
        cur = pl.multiple_of(n * BLOCK, BLOCK)
        dkv_ref[pl.ds(cur, BLOCK), :] += upd[BLOCK:]

        @pl.when(n > 0)
        def _():
            prev = pl.multiple_of((n - 1) * BLOCK, BLOCK)
            dkv_ref[pl.ds(prev, BLOCK), :] += upd[:BLOCK]

        @pl.when(n == nb - 1)
        def _():
            for hk in range(N_KV_HEADS):
                for g in range(Q_PER_KV):
                    tot = jnp.sum(dsacc[hk, g * BLOCK:(g + 1) * BLOCK, :], axis=0, keepdims=True)
                    h = hk * Q_PER_KV + g
                    dsink_ref[h:h + 1, :] = jnp.broadcast_to(tot, (1, LANES))

    return pl.pallas_call(
        body, name="attn_bwd", grid=(nb,),
        in_specs=_attn_in_specs() + [pl.BlockSpec((BLOCK, D_ATTN), lambda n: (n, 0))],
        out_specs=[pl.BlockSpec((BLOCK, D_ATTN), lambda n: (n, 0)), _const((SEQ, 2 * D_KV)),
                   _const((N_KV_HEADS, Q_PER_KV * BLOCK, 2 * BLOCK)), _const((N_Q_HEADS, LANES))],
        out_shape=[jax.ShapeDtypeStruct((SEQ, D_ATTN), F32), jax.ShapeDtypeStruct((SEQ, 2 * D_KV), F32),
                   jax.ShapeDtypeStruct((N_KV_HEADS, Q_PER_KV * BLOCK, 2 * BLOCK), F32),
                   jax.ShapeDtypeStruct((N_Q_HEADS, LANES), F32)],
        scratch_shapes=[pltpu.VMEM((N_KV_HEADS, Q_PER_KV * BLOCK, 1), F32)],
        compiler_params=_cp(("arbitrary",)),
    )(proj, proj, proj, bias, sinks, dcat)


@jax.custom_vjp
def _head_sum(x):
    return _dot_ind(x, _head_ones(x.shape[-1]))


_head_sum.defvjp(lambda x: (_head_sum(x), None), lambda _, ct: (_head_sum(ct),))


@jax.custom_vjp
def _bdot(a, w):
    return _dot(a.astype(BF16), w.astype(BF16))


def _bdot_bwd(res, ct):
    a, w = res
    ctb = ct.astype(BF16)
    return _dot(ctb, w.astype(BF16), NT), _dot(a.astype(BF16), ctb, TN)


_bdot.defvjp(lambda a, w: (_bdot(a, w), (a, w)), _bdot_bwd)


def _sigmoid(x):
    return 0.5 * (jnp.tanh(0.5 * x) + 1.0)


def _softplus(x):
    return jnp.maximum(x, 0.0) + jnp.log(1.0 + jnp.exp(-jnp.abs(x)))


def _rwkv_core(r, k, v, zwa, zg, w0, wdu, a0, wiu, wgu, k_k, k_a):
    w_log = -_softplus(-(w0 + _bdot(jnp.tanh(zwa), wdu))) - 0.5
    decay = jnp.exp(-jnp.exp(w_log))
    a = _sigmoid(a0 + _bdot(zwa, wiu))
    g = _bdot(_sigmoid(zg), wgu)
    kk = k * k_k
    kk = kk / jnp.maximum(jnp.sqrt(_head_sum(kk * kk)), 1e-12)
    k2 = k * (1.0 + (a - 1.0) * k_a)
    return r, decay, k2, v, -kk, kk * a, g


def _rwkv_out(o, r, k2, v, g, lng, lnb, rk):
    mu = _head_sum(o) * (1.0 / HEAD_DIM)
    d = o - mu
    var = _head_sum(d * d) * (1.0 / HEAD_DIM)
    on = d * lax.rsqrt(var + GN_EPS) * lng + lnb
    bonus = _head_sum(r * k2 * rk) * v
    return (on + bonus) * g


P_SPLITS = (0, 512, 1024, 1536, 1664, 1792)
N_PREP_PARAMS = 7
HALO = 8


def _shifted_pieces(i, p_ref, halo_ref, mix_ref):
    p = p_ref[:, P_OFF:]
    prev_row = halo_ref[HALO - 1:HALO, P_OFF:] * jnp.where(i > 0, 1.0, 0.0)
    row = lax.broadcasted_iota(jnp.int32, p.shape, 0)
    pprev = jnp.where(row == 0, prev_row, pltpu.roll(p, 1, 0))
    delta = pprev - p
    ps = p + delta * mix_ref[...]
    return [ps[:, a:b] for a, b in zip(P_SPLITS[:-1], P_SPLITS[1:])], delta


def _prep_in_specs():
    return [_rows(TR, D_IN),
            pl.BlockSpec((HALO, D_IN), lambda i: (jnp.maximum(i * (TR // HALO) - 1, 0), 0)),
            _const((1, RWKV_COLS)), _const((1, D_RWKV)), _const((LANES, D_RWKV)), _const((1, D_RWKV)),
            _const((LANES, D_RWKV)), _const((LANES, D_RWKV)), _const((1, D_RWKV)), _const((1, D_RWKV))]


def _rwkv_prep(proj, mix, prm):
    def body(p_ref, halo_ref, mix_ref, *refs):
        prm_refs, outs = refs[:N_PREP_PARAMS], refs[N_PREP_PARAMS:]
        pieces, _ = _shifted_pieces(pl.program_id(0), p_ref, halo_ref, mix_ref)
        vals = _rwkv_core(*pieces, *[t[...] for t in prm_refs])
        for ref, val in zip(outs, vals):
            ref[...] = val

    return pl.pallas_call(
        body, name="rwkv_prep", grid=(SEQ // TR,),
        in_specs=_prep_in_specs(),
        out_specs=[_rows(TR, D_RWKV)] * 7,
        out_shape=[jax.ShapeDtypeStruct((SEQ, D_RWKV), F32)] * 7,
        compiler_params=_cp(("parallel",)),
    )(proj, proj, mix, *prm)


def _rwkv_prep_bwd(proj, mix, prm, cts):
    def body(p_ref, halo_ref, mix_ref, *refs):
        i = pl.program_id(0)
        prm_refs = refs[:N_PREP_PARAMS]
        ct_refs = refs[N_PREP_PARAMS:N_PREP_PARAMS + 10]
        dps_ref, dmix_ref = refs[N_PREP_PARAMS + 10:N_PREP_PARAMS + 12]
        dprm_refs = refs[N_PREP_PARAMS + 12:]
        pieces, delta = _shifted_pieces(i, p_ref, halo_ref, mix_ref)
        _, vjp = jax.vjp(_rwkv_core, *pieces, *[t[...] for t in prm_refs])
        dr1, dr2, dw, dk1, dk2, dv1, dv2, dkkn, db, dg = [t[...] for t in ct_refs]
        grads = vjp((dr1 + dr2, dw, dk1 + dk2, dv1 + dv2, dkkn, db, dg))
        dps = jnp.concatenate(grads[:5], axis=1)
        dps_ref[...] = dps

        @pl.when(i == 0)
        def _():
            dmix_ref[...] = jnp.zeros_like(dmix_ref)
            for ref in dprm_refs:
                ref[...] = jnp.zeros_like(ref)

        dmix_ref[...] += jnp.sum(dps * delta, axis=0, keepdims=True)
        for ref, gval in zip(dprm_refs, grads[5:]):
            ref[...] += gval

    prm_shapes = [(1, D_RWKV), (LANES, D_RWKV), (1, D_RWKV), (LANES, D_RWKV), (LANES, D_RWKV), (1, D_RWKV), (1, D_RWKV)]
    return pl.pallas_call(
        body, name="rwkv_prep_bwd", grid=(SEQ // TR,),
        in_specs=_prep_in_specs() + [_rows(TR, D_RWKV)] * 10,
        out_specs=[_rows(TR, RWKV_COLS), _const((1, RWKV_COLS))] + [_const(s) for s in prm_shapes],
        out_shape=[jax.ShapeDtypeStruct((SEQ, RWKV_COLS), F32), jax.ShapeDtypeStruct((1, RWKV_COLS), F32)]
        + [jax.ShapeDtypeStruct(s, F32) for s in prm_shapes],
        compiler_params=_cp(("arbitrary",)),
    )(proj, proj, mix, *prm, *cts)


def _rwkv_post(o, r, k2, v, g, lng, lnb, rk, attn):
    def body(o_ref, r_ref, k_ref, v_ref, g_ref, lng_ref, lnb_ref, rk_ref, attn_ref, cat_ref):
        rw = _rwkv_out(*[t[...] for t in (o_ref, r_ref, k_ref, v_ref, g_ref, lng_ref, lnb_ref, rk_ref)])
        cat_ref[...] = jnp.concatenate([attn_ref[...], rw], axis=1).astype(BF16)

    return pl.pallas_call(
        body, name="rwkv_post", grid=(SEQ // TR,),
        in_specs=[_rows(TR, D_RWKV)] * 5 + [_const((1, D_RWKV))] * 3 + [_rows(TR, D_ATTN)],
        out_specs=_rows(TR, D_MODEL),
        out_shape=jax.ShapeDtypeStruct((SEQ, D_MODEL), BF16),
        compiler_params=_cp(("parallel",)),
    )(o, r, k2, v, g, lng, lnb, rk, attn)


def _rwkv_post_bwd(o, r, k2, v, g, lng, lnb, rk, dcat):
    def body(o_ref, r_ref, k_ref, v_ref, g_ref, lng_ref, lnb_ref, rk_ref, dcat_ref,
             do_ref, dr_ref, dk_ref, dv_ref, dg_ref, dlng_ref, dlnb_ref, drk_ref):
        i = pl.program_id(0)
        args = [t[...] for t in (o_ref, r_ref, k_ref, v_ref, g_ref, lng_ref, lnb_ref, rk_ref)]
        _, vjp = jax.vjp(_rwkv_out, *args)
        grads = vjp(dcat_ref[:, D_ATTN:])
        for ref, gval in zip((do_ref, dr_ref, dk_ref, dv_ref, dg_ref), grads[:5]):
            ref[...] = gval

        @pl.when(i == 0)
        def _():
            for ref in (dlng_ref, dlnb_ref, drk_ref):
                ref[...] = jnp.zeros_like(ref)

        for ref, gval in zip((dlng_ref, dlnb_ref, drk_ref), grads[5:]):
            ref[...] += gval

    return pl.pallas_call(
        body, name="rwkv_post_bwd", grid=(SEQ // TR,),
        in_specs=[_rows(TR, D_RWKV)] * 5 + [_const((1, D_RWKV))] * 3 + [_rows(TR, D_MODEL)],
        out_specs=[_rows(TR, D_RWKV)] * 5 + [_const((1, D_RWKV))] * 3,
        out_shape=[jax.ShapeDtypeStruct((SEQ, D_RWKV), F32)] * 5 + [jax.ShapeDtypeStruct((1, D_RWKV), F32)] * 3,
        compiler_params=_cp(("arbitrary",)),
    )(o, r, k2, v, g, lng, lnb, rk, dcat)


def _assemble_dproj(dq, dkv, dps, mix):
    last = SEQ // HALO - 1

    def body(dq_ref, dkv_ref, dps_ref, nxt_ref, mix_ref, o_ref):
        i = pl.program_id(0)
        dps = dps_ref[...]
        mixv = mix_ref[...]
        nxt_row = nxt_ref[0:1, :] * jnp.where(i < SEQ // TR - 1, 1.0, 0.0)
        row = lax.broadcasted_iota(jnp.int32, dps.shape, 0)
        up = jnp.where(row == TR - 1, nxt_row, pltpu.roll(dps, TR - 1, 0))
        dp = dps * (1.0 - mixv) + up * mixv
        o_ref[...] = jnp.concatenate([dq_ref[...], dkv_ref[...], dp], axis=1).astype(BF16)

    return pl.pallas_call(
        body, name="assemble_dproj", grid=(SEQ // TR,),
        in_specs=[_rows(TR, D_ATTN), _rows(TR, 2 * D_KV), _rows(TR, RWKV_COLS),
                  pl.BlockSpec((HALO, RWKV_COLS), lambda i: (jnp.minimum((i + 1) * (TR // HALO), last), 0)),
                  _const((1, RWKV_COLS))],
        out_specs=_rows(TR, D_IN),
        out_shape=jax.ShapeDtypeStruct((SEQ, D_IN), BF16),
        compiler_params=_cp(("parallel",)),
    )(dq, dkv, dps, dps, mix)


N_PAIR = D_RWKV // LANES
CHUNK = 64
N_CHUNK = SEQ // CHUNK
GROUP = 8
STATE = (N_PAIR, HEAD_DIM, LANES)


def _lane_sums(lhs_tiles, ones2):
    out = _dot(jnp.concatenate(lhs_tiles, axis=0), ones2)
    return [out[i * HEAD_DIM:(i + 1) * HEAD_DIM] for i in range(len(lhs_tiles))]


def _seg_sum(xs, ones2):
    return _lane_sums([jnp.concatenate(_split(x, 2), axis=1) for x in xs], ones2)


def _seg_sum_rows(xs, ones2):
    out = _dot(jnp.concatenate(_split(jnp.concatenate(xs, axis=0), 2), axis=1), ones2)
    return [out[i * GROUP:(i + 1) * GROUP] for i in range(len(xs))]


def _col_form(rows, diag, ones2):
    zero = jnp.zeros((HEAD_DIM, LANES), BF16)
    tiles = []
    for row in rows:
        hi = row.astype(BF16)
        lo = (row - hi.astype(F32)).astype(BF16)
        tiles.append(jnp.concatenate(
            [jnp.where(diag, jnp.broadcast_to(part, (HEAD_DIM, LANES)), zero) for part in (hi, lo)], axis=1))
    return _lane_sums(tiles, ones2)


def _scan_consts():
    ones2 = jnp.concatenate([_head_ones(LANES)] * 2, axis=0)
    sub = lax.broadcasted_iota(jnp.int32, (HEAD_DIM, LANES), 0)
    lane_in_head = lax.broadcasted_iota(jnp.int32, (HEAD_DIM, LANES), 1) & (HEAD_DIM - 1)
    return ones2, lane_in_head == sub, lane_in_head


def _rows_of_columns(tile):
    t = tile.T
    return jnp.concatenate([t[:CHUNK], t[HEAD_DIM:HEAD_DIM + CHUNK]], axis=1)


def _pair(j):
    return slice(j * LANES, (j + 1) * LANES)


def _scan_fwd(r, w, k, v, kkn, b):
    def body(r_ref, w_ref, k_ref, v_ref, kkn_ref, b_ref, o_ref, st_ref, sa_ref, s_scr):
        c = pl.program_id(0)
        ones2, diag, lane_in_head = _scan_consts()

        @pl.when(c == 0)
        def _():
            s_scr[...] = jnp.zeros_like(s_scr)

        def group(gi, carry):
            row0 = pl.multiple_of(gi * GROUP, GROUP)
            states, ocols = list(carry[:N_PAIR]), list(carry[N_PAIR:])
            tiles = [[t[pl.ds(row0, GROUP), _pair(j)] for t in (r_ref, w_ref, k_ref, v_ref, kkn_ref, b_ref)]
                     for j in range(N_PAIR)]
            def row(j, name, u):
                return tiles[j]["rwkvnb".index(name)][u:u + 1]

            def emit_out(u, after):
                outs = _seg_sum([s[j] * row(j, "r", u + d) for d, s in enumerate(after) for j in range(N_PAIR)], ones2)
                for d in range(2):
                    here = lane_in_head == gi * GROUP + u + d
                    for j in range(N_PAIR):
                        ocols[j] = jnp.where(here, outs[d * N_PAIR + j], ocols[j])

            def vcols_of(u):
                cols = _col_form([row(j, "v", u + d) for d in range(2) for j in range(N_PAIR)], diag, ones2)
                return cols[:N_PAIR], cols[N_PAIR:]

            n_next = [pltpu.roll(tiles[j][4], GROUP - 1, 0) for j in range(N_PAIR)]
            dots = _seg_sum_rows([tiles[j][5] * n_next[j] for j in range(N_PAIR)]
                                 + [tiles[j][2] * n_next[j] for j in range(N_PAIR)], ones2)
            b_n, k_n = dots[:N_PAIR], dots[N_PAIR:]
            w_n = [tiles[j][1] * n_next[j] for j in range(N_PAIR)]

            vcols = vcols_of(0)
            after = None
            for u in range(0, GROUP, 2):
                prods = _seg_sum([states[j] * row(j, "n", u) for j in range(N_PAIR)]
                                 + [states[j] * w_n[j][u:u + 1] for j in range(N_PAIR)], ones2)
                if after is not None:
                    emit_out(u - 2, after)
                nxt = vcols_of(u + 2) if u + 2 < GROUP else None
                first, second = [], []
                for j in range(N_PAIR):
                    sa1 = prods[j]
                    sa2 = prods[N_PAIR + j] + sa1 * b_n[j][u:u + 1] + vcols[0][j] * k_n[j][u:u + 1]
                    s1 = states[j] * row(j, "w", u) + sa1 * row(j, "b", u) + vcols[0][j] * row(j, "k", u)
                    s2 = s1 * row(j, "w", u + 1) + sa2 * row(j, "b", u + 1) + vcols[1][j] * row(j, "k", u + 1)
                    st_ref[row0 + u, j] = s1
                    sa_ref[row0 + u, j] = sa1
                    st_ref[row0 + u + 1, j] = s2
                    sa_ref[row0 + u + 1, j] = sa2
                    first.append(s1)
                    second.append(s2)
                    states[j] = s2
                after, vcols = (first, second), nxt
            emit_out(GROUP - 2, after)
            return tuple(states + ocols)

        zero = jnp.zeros((HEAD_DIM, LANES), F32)
        fin = lax.fori_loop(0, CHUNK // GROUP, group, tuple(s_scr[j] for j in range(N_PAIR)) + (zero,) * N_PAIR)
        for j in range(N_PAIR):
            s_scr[j] = fin[j]
            o_ref[:, _pair(j)] = _rows_of_columns(fin[N_PAIR + j])

    blk = pl.BlockSpec((CHUNK, D_RWKV), lambda c: (c, 0))
    per_step = pl.BlockSpec((CHUNK,) + STATE, lambda c: (c, 0, 0, 0))
    return pl.pallas_call(
        body, name="rwkv_scan_fwd", grid=(N_CHUNK,),
        in_specs=[blk] * 6,
        out_specs=[blk, per_step, per_step],
        out_shape=[jax.ShapeDtypeStruct((SEQ, D_RWKV), F32)] + [jax.ShapeDtypeStruct((SEQ,) + STATE, F32)] * 2,
        scratch_shapes=[pltpu.VMEM(STATE, F32)],
        compiler_params=_cp(("arbitrary",)),
    )(r, w, k, v, kkn, b)


def _scan_bwd(r, w, k, v, kkn, b, do, states, sas, ds_in, prev, name, first_chunk, n_chunks):
    top = first_chunk + n_chunks - 1

    def body(r_ref, w_ref, k_ref, v_ref, kkn_ref, b_ref, do_ref, st_ref, before_ref, sa_ref, ds_in_ref, *rest):
        dr_ref, dw_ref, dk_ref, dv_ref, dkkn_ref, db_ref, ds_out_ref, ds_scr = rest[-8:]
        i = pl.program_id(0)
        ones2, diag, lane_in_head = _scan_consts()

        @pl.when(i == 0)
        def _():
            ds_scr[...] = ds_in_ref[...]

        entry = [before_ref[0, j] * jnp.where(i < top, 1.0, 0.0) for j in range(N_PAIR)]

        def reverse(gr, carry):
            gi = CHUNK // GROUP - 1 - gr
            row0 = pl.multiple_of(gi * GROUP, GROUP)
            dstates, dvcols = list(carry[:N_PAIR]), list(carry[N_PAIR:])
            tiles = [[t[pl.ds(row0, GROUP), _pair(j)]
                      for t in (r_ref, w_ref, k_ref, v_ref, kkn_ref, b_ref, do_ref)] for j in range(N_PAIR)]
            rows = [[[None] * GROUP for _ in range(5)] for _ in range(N_PAIR)]

            def row(j, name, u):
                return tiles[j]["rwkvnbd".index(name)][u:u + 1]

            def cols_of(u):
                cols = _col_form([row(j, name, u - d) for d in range(2) for name in "dv" for j in range(N_PAIR)],
                                 diag, ones2)
                return [[(cols[(2 * d) * N_PAIR + j], cols[(2 * d + 1) * N_PAIR + j]) for j in range(N_PAIR)]
                        for d in range(2)]

            def emit_dv(u, dsps):
                outs = _seg_sum([dsp[j] * row(j, "k", u - d) for d, dsp in enumerate(dsps) for j in range(N_PAIR)], ones2)
                for d in range(2):
                    here = lane_in_head == gi * GROUP + u - d
                    for j in range(N_PAIR):
                        dvcols[j] = jnp.where(here, outs[d * N_PAIR + j], dvcols[j])

            b_prev = [pltpu.roll(tiles[j][5], 1, 0) for j in range(N_PAIR)]
            dots = _seg_sum_rows([tiles[j][4] * b_prev[j] for j in range(N_PAIR)]
                                 + [tiles[j][0] * tiles[j][5] for j in range(N_PAIR)], ones2)
            n_b, r_b = dots[:N_PAIR], dots[N_PAIR:]
            w_b = [tiles[j][1] * b_prev[j] for j in range(N_PAIR)]

            def outputs(u, j, dsp, dsa, docol, vcol):
                tl = gi * GROUP + u
                if u > 0:
                    s_prev = st_ref[tl - 1, j]
                else:
                    s_prev = jnp.where(gi == 0, entry[j], st_ref[jnp.maximum(tl - 1, 0), j])
                rows[j][0][u] = jnp.sum(st_ref[tl, j] * docol, axis=0, keepdims=True)
                rows[j][1][u] = jnp.sum(dsp * s_prev, axis=0, keepdims=True)
                rows[j][2][u] = jnp.sum(dsp * vcol, axis=0, keepdims=True)
                rows[j][3][u] = jnp.sum(s_prev * dsa, axis=0, keepdims=True)
                rows[j][4][u] = jnp.sum(dsp * sa_ref[tl, j], axis=0, keepdims=True)

            cols = cols_of(GROUP - 1)
            before = None
            for u in range(GROUP - 1, 0, -2):
                dsp1 = [dstates[j] + cols[0][j][0] * row(j, "r", u) for j in range(N_PAIR)]
                prods = _seg_sum([dsp1[j] * row(j, "b", u) for j in range(N_PAIR)]
                                 + [dsp1[j] * w_b[j][u:u + 1] for j in range(N_PAIR)], ones2)
                if before is not None:
                    emit_dv(u + 2, before)
                nxt = cols_of(u - 2) if u >= 2 else None
                dsp2 = []
                for j in range(N_PAIR):
                    dsa1 = prods[j]
                    dsa2 = prods[N_PAIR + j] + dsa1 * n_b[j][u:u + 1] + cols[1][j][0] * r_b[j][u - 1:u]
                    mid = dsp1[j] * row(j, "w", u) + dsa1 * row(j, "n", u) + cols[1][j][0] * row(j, "r", u - 1)
                    outputs(u, j, dsp1[j], dsa1, *cols[0][j])
                    outputs(u - 1, j, mid, dsa2, *cols[1][j])
                    dstates[j] = mid * row(j, "w", u - 1) + dsa2 * row(j, "n", u - 1)
                    dsp2.append(mid)
                before, cols = (dsp1, dsp2), nxt
            emit_dv(1, before)
            for j in range(N_PAIR):
                for ref, rr in zip((dr_ref, dw_ref, dk_ref, dkkn_ref, db_ref), rows[j]):
                    ref[pl.ds(row0, GROUP), _pair(j)] = jnp.concatenate(rr, axis=0)
            return tuple(dstates + dvcols)

        zero = jnp.zeros((HEAD_DIM, LANES), F32)
        dfin = lax.fori_loop(0, CHUNK // GROUP, reverse, tuple(ds_scr[j] for j in range(N_PAIR)) + (zero,) * N_PAIR)
        for j in range(N_PAIR):
            ds_scr[j] = dfin[j]
            dv_ref[:, _pair(j)] = _rows_of_columns(dfin[N_PAIR + j])

        @pl.when(i == n_chunks - 1)
        def _():
            ds_out_ref[...] = ds_scr[...]

    blk = pl.BlockSpec((CHUNK, D_RWKV), lambda i: (top - i, 0))
    per_step = pl.BlockSpec((CHUNK,) + STATE, lambda i: (top - i, 0, 0, 0))
    step_before = pl.BlockSpec((1,) + STATE, lambda i: (jnp.maximum((top - i) * CHUNK - 1, 0), 0, 0, 0))
    prev = [] if prev is None else list(prev)
    outs = pl.pallas_call(
        body, name=name, grid=(n_chunks,),
        in_specs=[blk] * 7 + [per_step, step_before, per_step, _const(STATE)] + [ANY] * len(prev),
        out_specs=[blk] * 6 + [_const(STATE)],
        out_shape=[jax.ShapeDtypeStruct((SEQ, D_RWKV), F32)] * 6 + [jax.ShapeDtypeStruct(STATE, F32)],
        scratch_shapes=[pltpu.VMEM(STATE, F32)],
        input_output_aliases={11 + t: t for t in range(len(prev))},
        compiler_params=_cp(("arbitrary",)),
    )(r, w, k, v, kkn, b, do, states, states, sas, ds_in, *prev)
    return outs[:6], outs[6]


def _stacked(rows, cols, pick):
    return pl.BlockSpec((None, rows, cols), pick)


def _local_step(x, target, sm, win_st):
    def tied(t, token):
        return t if token is None else t + token[0:1, 0:1].reshape((1,) * t.ndim)

    zpad = jnp.zeros((LORA_DECAY, D_RWKV), F32)
    prm = [sm["w0"], jnp.concatenate([sm["w_decay_up"], zpad], axis=0), sm["a0"],
           jnp.concatenate([zpad, sm["w_iclr_up"]], axis=0), sm["w_gate_up"], sm["k_k"], sm["k_a"]]
    mix = sm["rwkv_shift_mix"]
    onehot = jnp.asarray(_t5_onehot(), BF16)
    sinks = sm["sinks"].reshape(N_Q_HEADS)
    lng, lnb, rk = sm["ln_x_g"], sm["ln_x_b"], sm["r_k"].reshape(1, D_RWKV)

    h1 = _norm_cast(x, sm["norm_mix_pre"], "norm_in")
    proj = _matmul(h1, win_st, "nn", "proj", m=SEQ, n=D_IN, k=D_MODEL, tm=SEQ, tn=640,
                   b_spec=_stacked(D_MODEL, 640, lambda i, j: (j, 0, 0)))
    bias = _bias_table(sm["rel_bias"].T, onehot).reshape(N_KV_HEADS, Q_PER_KV * BLOCK, 2 * BLOCK)
    attn = _attn_fwd(proj, bias, sinks)
    r, w, k2, v, kkn, b, g = _rwkv_prep(proj, mix, prm)
    o, states, sas = _scan_fwd(r, w, k2, v, kkn, b)
    wout, wup_st, wdown = yield ("rest_weights", o)
    cat = _rwkv_post(o, r, k2, v, g, lng, lnb, rk, attn)
    mixo = _matmul(cat, wout, "nn", "out_proj", m=SEQ, n=D_MODEL, k=D_MODEL, tm=SEQ, tn=512)
    x2, h3 = _mix_norm(x, mixo, sm["norm_mix_post"], sm["norm_ffn_pre"])
    u = _matmul(h3, wup_st, "nn", "ffn_up", m=SEQ, n=2 * D_FF, k=D_MODEL, tm=SEQ, tn=512,
                b_spec=_stacked(D_MODEL, 512, lambda i, j: (j // 4, 0, j % 4)))
    act = _ffn_act(u, sm["conv_w"], sm["conv_b"])
    f = _matmul(act, wdown, "nn", "ffn_down", m=SEQ, n=D_MODEL, k=D_FF, tm=1024, tn=512)
    loss, dy, df, d_g4 = _loss_head(x2, f, sm["norm_ffn_post"], target)

    dact = _matmul(df, wdown, "nt", "d_act", m=SEQ, n=D_FF, k=D_MODEL, tm=SEQ, tn=512)
    d_wdown = _matmul(act, df, "tn", "d_wdown", m=D_FF, n=D_MODEL, k=SEQ, tm=512, tn=D_MODEL)
    du, d_convw, d_convb = _ffn_act_bwd(u, dact, sm["conv_w"], sm["conv_b"])
    d_convw = d_convw.transpose(1, 0, 2).reshape(3, 2 * D_FF)
    d_convb = d_convb.reshape(1, 2 * D_FF)
    dh3 = _matmul_nt_shards(du, wup_st, "d_h3", m=SEQ, n=D_MODEL, tm=512, tn=512,
                            a_spec=pl.BlockSpec((2, 512, D_FF), lambda i, j: (0, i, 0)),
                            a_piece=lambda ref, s: ref[s // 2, :, (s % 2) * 2048:(s % 2 + 1) * 2048])
    d_wup = _matmul(h3, du, "tn", "d_wup", m=D_MODEL, n=2 * D_FF, k=SEQ, tm=D_MODEL, tn=512,
                    b_spec=pl.BlockSpec((None, SEQ, 512), lambda i, j: (j // 8, 0, j % 8)),
                    out=((N_CHIPS, D_MODEL, 2048), _stacked(D_MODEL, 512, lambda i, j: (j // 4, 0, j % 4))))
    dx2, dmix, d_g2, d_g3 = _mid_bwd(x2, mixo, dy, dh3, sm["norm_mix_post"], sm["norm_ffn_pre"])
    dcat = _matmul(dmix, wout, "nt", "d_cat", m=SEQ, n=D_MODEL, k=D_MODEL, tm=SEQ, tn=512)
    d_wout = _matmul(cat, dmix, "tn", "d_wout", m=D_MODEL, n=D_MODEL, k=SEQ, tm=512, tn=D_MODEL)
    token = yield ("grads_a", (d_wdown, d_wup, d_wout))
    do, dr_p, dk_p, dv_p, dg, d_lng, d_lnb, d_rk = _rwkv_post_bwd(o, r, k2, v, g, lng, tied(lnb, token), rk, dcat)
    half = N_CHUNK // 2
    ds_end = jnp.zeros(STATE, F32)
    late, ds_mid = _scan_bwd(r, w, k2, v, kkn, b, do, states, sas, ds_end, None, "rwkv_scan_bwd_late", half, half)
    token = yield ("seam_1", ds_mid)
    scan_cts, ds_first = _scan_bwd(r, w, k2, v, kkn, b, do, states, sas, tied(ds_mid, token), late,
                                   "rwkv_scan_bwd_early", 0, half)
    dr_s, dw_s, dk_s, dv_s, dkkn_s, db_s = scan_cts
    token = yield ("seam_2", ds_first)
    prep_grads = _rwkv_prep_bwd(proj, tied(mix, token), prm,
                                (dr_s, dr_p, dw_s, dk_s, dk_p, dv_s, dv_p, dkkn_s, db_s, dg))
    dps, d_mix, d_w0, d_wdu, d_a0, d_wiu, d_wgu, d_kk, d_ka = prep_grads
    dq, dkv, dbias, dsink = _attn_bwd(proj, bias, sinks, dcat)
    d_relb = _bias_table_bwd(dbias.reshape(N_Q_HEADS, N_REL), onehot).T
    dproj = _assemble_dproj(dq, dkv, dps, mix)
    d_win = _matmul(h1, dproj, "tn", "d_win", m=D_MODEL, n=D_IN, k=SEQ, tm=D_MODEL, tn=640,
                    out=((N_CHIPS, D_MODEL, 640), _stacked(D_MODEL, 640, lambda i, j: (j, 0, 0))))
    token = yield ("grads_b", d_win)
    dh1 = _matmul_nt_shards(dproj, win_st, "d_h1", m=SEQ, n=D_MODEL, tm=1024, tn=D_MODEL,
                            a_spec=pl.BlockSpec((1024, D_IN), lambda i, j: (i, 0)),
                            a_piece=lambda ref, s: ref[:, s * 640:(s + 1) * 640])
    grad_x, d_g1 = _first_bwd(x, dx2, dh1, tied(sm["norm_mix_pre"], token))

    grads = {
        "norm_mix_pre": d_g1, "norm_mix_post": d_g2, "norm_ffn_pre": d_g3, "norm_ffn_post": d_g4,
        "w_in": d_win, "rel_bias": d_relb, "sinks": dsink[:, 0].reshape(1, N_Q_HEADS),
        "rwkv_shift_mix": d_mix, "w0": d_w0, "w_decay_up": d_wdu[:LORA_DECAY], "a0": d_a0,
        "w_iclr_up": d_wiu[LORA_DECAY:], "w_gate_up": d_wgu, "k_k": d_kk, "k_a": d_ka,
        "r_k": d_rk.reshape(1, N_Q_HEADS, HEAD_DIM), "ln_x_g": d_lng, "ln_x_b": d_lnb,
        "w_out": d_wout, "w_ffn_up": d_wup, "conv_w": d_convw, "conv_b": d_convb, "w_ffn_down": d_wdown,
    }
    return loss, grad_x, grads


def _place():
    x, y, c = lax.axis_index("x"), lax.axis_index("y"), lax.axis_index("c")
    chips = [(1 - x, y), (x, 1 - y), (1 - x, 1 - y)]
    return x, y, c, chips


def _remote(src, dst, sems, idx, to):
    return pltpu.make_async_remote_copy(src_ref=src, dst_ref=dst, send_sem=sems[0].at[idx], recv_sem=sems[1].at[idx],
                                        device_id=to, device_id_type=MESH)


def _half(c, rows):
    return pl.ds(pl.multiple_of(c * (rows // 2), 16), rows // 2)


def _gather_weights(big, small):
    nb, ns = len(big), len(small)

    def body(*refs):
        ins, outs = refs[:nb + ns], refs[nb + ns:2 * (nb + ns)]
        ici, d2d, sml, loc = refs[2 * (nb + ns):2 * (nb + ns) + 2], refs[-5:-3], refs[-3:-1], refs[-1]
        x, y, c, chips = _place()
        me = 2 * x + y
        sib = (x, y, 1 - c)
        local = [pltpu.make_async_copy(ins[a], outs[a].at[me], loc.at[a]) for a in range(nb + ns)]
        for cp in local:
            cp.start()
        sends = []
        for a in range(nb):
            rows = _half(c, big[a].shape[0])
            for kk, chip in enumerate(chips):
                sends.append(_remote(ins[a].at[rows], outs[a].at[me, rows], ici, a * 3 + kk, (*chip, c)))
        for a in range(ns):
            for kk, chip in enumerate(chips):
                sends.append(_remote(ins[nb + a], outs[nb + a].at[me], sml, a * 3 + kk, (*chip, c)))
        for cp in sends:
            cp.start()
        passed = []
        for a in range(nb):
            rows = _half(c, big[a].shape[0])
            for kk, (px, py) in enumerate(chips):
                got = outs[a].at[2 * px + py, rows]
                _remote(got, got, ici, a * 3 + kk, sib).wait_recv()
                fwd = _remote(got, got, d2d, a * 3 + kk, sib)
                fwd.start()
                passed.append(fwd)
        for a in range(nb):
            other = _half(1 - c, big[a].shape[0])
            for kk, (px, py) in enumerate(chips):
                land = outs[a].at[2 * px + py, other]
                _remote(land, land, d2d, a * 3 + kk, sib).wait_recv()
        for a in range(ns):
            for kk, (px, py) in enumerate(chips):
                land = outs[nb + a].at[2 * px + py]
                _remote(land, land, sml, a * 3 + kk, sib).wait_recv()
        for cp in sends + passed:
            cp.wait_send()
        for cp in local:
            cp.wait()

    arrs = list(big) + list(small)
    return pl.pallas_call(
        body, name="gather_weights",
        in_specs=[ANY] * len(arrs), out_specs=[ANY] * len(arrs),
        out_shape=[jax.ShapeDtypeStruct((N_CHIPS,) + t.shape, t.dtype) for t in arrs],
        scratch_shapes=[pltpu.SemaphoreType.DMA((3 * nb,)), pltpu.SemaphoreType.DMA((3 * nb,)),
                        pltpu.SemaphoreType.DMA((3 * nb,)), pltpu.SemaphoreType.DMA((3 * nb,)),
                        pltpu.SemaphoreType.DMA((3 * ns,)), pltpu.SemaphoreType.DMA((3 * ns,)),
                        pltpu.SemaphoreType.DMA((nb + ns,))],
        compiler_params=pltpu.CompilerParams(has_side_effects=True),
    )(*arrs)


HBM = pl.BlockSpec(memory_space=pltpu.HBM)
SEM = pl.BlockSpec(memory_space=pltpu.SEMAPHORE)
EFFECT = pltpu.SideEffectType.DATAFLOW_SIDE_EFFECTING


def _copies_start(name, bufs, plan, n):
    nb = len(bufs)

    def body(*refs):
        ins, sems, token = refs[:nb], refs[nb:nb + 2 * n], refs[-1]
        for kk, (src, dst, dev) in enumerate(plan(ins)):
            pltpu.make_async_remote_copy(src_ref=src, dst_ref=dst, send_sem=sems[2 * kk], recv_sem=sems[2 * kk + 1],
                                         device_id=dev, device_id_type=MESH).start()
        token[...] = jnp.zeros_like(token)

    outs = pl.pallas_call(
        body, name=name,
        out_shape=tuple([pltpu.SemaphoreType.DMA(())] * (2 * n) + [pltpu.HBM(t.shape, t.dtype) for t in bufs]
                        + [jax.ShapeDtypeStruct((8, LANES), F32)]),
        in_specs=[HBM] * nb,
        out_specs=tuple([SEM] * (2 * n) + [HBM] * nb + [pl.BlockSpec(memory_space=pltpu.VMEM)]),
        input_output_aliases={t: 2 * n + t for t in range(nb)},
        compiler_params=pltpu.CompilerParams(has_side_effects=EFFECT),
    )(*[pltpu.with_memory_space_constraint(t, pltpu.HBM) for t in bufs])
    return outs[:2 * n], outs[2 * n:2 * n + nb], outs[-1]


def _copies_wait(name, sems, bufs, plan, n, after):
    nb = len(bufs)

    def body(*refs):
        ins, sem_refs = refs[:nb], refs[nb:nb + 2 * n]
        for kk, (src, dst, dev) in enumerate(plan(ins)):
            cp = pltpu.make_async_remote_copy(src_ref=src, dst_ref=dst, send_sem=sem_refs[2 * kk],
                                              recv_sem=sem_refs[2 * kk + 1], device_id=dev, device_id_type=MESH)
            cp.wait_send()
            cp.wait_recv()

    return pl.pallas_call(
        body, name=name,
        out_shape=tuple(pltpu.HBM(t.shape, t.dtype) for t in bufs),
        in_specs=[HBM] * nb + [SEM] * (2 * n) + [ANY],
        out_specs=tuple([HBM] * nb),
        input_output_aliases={t: t for t in range(nb)},
        compiler_params=pltpu.CompilerParams(has_side_effects=EFFECT),
    )(*bufs, *sems, after)


def _plan_gather(n_w):
    def plan(refs):
        x, y, c, chips = _place()
        me = 2 * x + y
        return [(refs[a], refs[n_w + a].at[me], (*chip, c)) for a in range(n_w) for chip in chips]
    return plan


def _plan_pair_halves(n_g, rows):
    def plan(refs):
        x, y, c, _ = _place()
        return [(refs[a].at[:, _half(1 - c, rows[a])], refs[n_g + a], (x, y, 1 - c)) for a in range(n_g)]
    return plan


def _plan_chip_parts(n_g):
    def plan(refs):
        x, y, c, chips = _place()
        me = 2 * x + y
        return [(refs[a].at[2 * px + py], refs[n_g + a].at[me], (px, py, c))
                for a in range(n_g) for (px, py) in chips]
    return plan


def _plan_pair_fill(n_g, rows):
    def plan(refs):
        x, y, c, _ = _place()
        return [(refs[a].at[_half(c, rows[a])], refs[a].at[_half(c, rows[a])], (x, y, 1 - c)) for a in range(n_g)]
    return plan


def _pair_add(g, got, name):
    _, rows, cols = g.shape
    hr = rows // 2
    tr = min(hr, 256)
    nb = hr // tr

    def body(g_ref, got_ref, p_ref, own_ref):
        val = (g_ref[...] + got_ref[...]).astype(BF16)
        p_ref[...] = val

        @pl.when(pl.program_id(1) == 2 * lax.axis_index("x") + lax.axis_index("y"))
        def _():
            own_ref[...] = val

    def mine(i, s):
        return (2 * lax.axis_index("x") + lax.axis_index("y"), i, 0)

    return pl.pallas_call(
        body, name=name, grid=(nb, N_CHIPS),
        in_specs=[pl.BlockSpec((None, tr, cols), lambda i, s: (s, lax.axis_index("c") * nb + i, 0)),
                  pl.BlockSpec((None, tr, cols), lambda i, s: (s, i, 0))],
        out_specs=[pl.BlockSpec((None, tr, cols), lambda i, s: (s, i, 0)), pl.BlockSpec((None, tr, cols), mine)],
        out_shape=[jax.ShapeDtypeStruct((N_CHIPS, hr, cols), BF16)] * 2,
        compiler_params=_cp(("parallel", "arbitrary")),
    )(g, got)


def _chip_sum(parts, name):
    _, hr, cols = parts.shape
    tr = min(hr, 128)
    nb = hr // tr

    def body(t_ref, o_ref):
        part = [t_ref[s].astype(F32) for s in range(N_CHIPS)]
        o_ref[...] = ((part[0] + part[1]) + part[2]) + part[3]

    return pl.pallas_call(
        body, name=name, grid=(nb,),
        in_specs=[pl.BlockSpec((N_CHIPS, tr, cols), lambda i: (0, i, 0))],
        out_specs=pl.BlockSpec((tr, cols), lambda i: (lax.axis_index("c") * nb + i, 0)),
        out_shape=jax.ShapeDtypeStruct((2 * hr, cols), F32),
        compiler_params=_cp(("parallel",)),
    )(parts)


class _Reduction:
    def __init__(self, tag, rows):
        self.tag, self.n, self.rows = tag, len(rows), rows
        self.plans = (_plan_pair_halves(self.n, rows), _plan_chip_parts(self.n), _plan_pair_fill(self.n, rows))
        self.flight = None

    def _name(self, what):
        return f"grad_{self.tag}_{what}"

    def start(self, gs):
        gots = [lax.empty((N_CHIPS, t.shape[1] // 2, t.shape[2]), F32) for t in gs]
        self.flight = _copies_start(self._name("pair_start"), list(gs) + gots, self.plans[0], self.n)
        return self.flight[2]

    def after_pair(self, after):
        sems, bufs, _ = self.flight
        out = _copies_wait(self._name("pair_wait"), sems, bufs, self.plans[0], self.n, after)
        sums = [_pair_add(g, got, self._name(f"pair_add_{i}"))
                for i, (g, got) in enumerate(zip(out[:self.n], out[self.n:]))]
        self.flight = _copies_start(self._name("chip_start"), [p for p, _ in sums] + [own for _, own in sums],
                                    self.plans[1], 3 * self.n)
        return self.flight[2]

    def after_chips(self, after):
        sems, bufs, _ = self.flight
        out = _copies_wait(self._name("chip_wait"), sems, bufs, self.plans[1], 3 * self.n, after)
        fulls = [_chip_sum(t, self._name(f"chip_sum_{i}")) for i, t in enumerate(out[self.n:])]
        self.flight = _copies_start(self._name("fill_start"), fulls, self.plans[2], self.n)
        return self.flight[2]

    def finish(self, after):
        sems, bufs, _ = self.flight
        return _copies_wait(self._name("fill_wait"), sems, bufs, self.plans[2], self.n, after)


def _adamw_math(w, g, m, v):
    nm = ADAM_B1 * m + (1.0 - ADAM_B1) * g
    nv = ADAM_B2 * v + (1.0 - ADAM_B2) * (g * g)
    m_hat = nm / (1.0 - ADAM_B1 ** ADAM_STEP)
    v_hat = nv / (1.0 - ADAM_B2 ** ADAM_STEP)
    return -ADAM_LR * (m_hat / (jnp.sqrt(v_hat) + ADAM_EPS) + ADAM_WD * w), nm, nv


def _adamw(w, g, m, v, name, tr):
    r, cdim = w.shape

    def body(w_ref, g_ref, m_ref, v_ref, d_ref, nm_ref, nv_ref):
        d_ref[...], nm_ref[...], nv_ref[...] = _adamw_math(w_ref[...], g_ref[...], m_ref[...], v_ref[...])

    return pl.pallas_call(
        body, name=name, grid=(r // tr,), in_specs=[_rows(tr, cdim)] * 4, out_specs=[_rows(tr, cdim)] * 3,
        out_shape=[jax.ShapeDtypeStruct((r, cdim), F32)] * 3, compiler_params=_cp(("parallel",)),
    )(w, g, m, v)


def _adamw_small(w, parts, m, v):
    def body(w_ref, p_ref, m_ref, v_ref, d_ref, nm_ref, nv_ref, g_ref):
        g = p_ref[0]
        for dev in range(1, N_DEV):
            g = g + p_ref[dev]
        g_ref[...] = g
        d_ref[...], nm_ref[...], nv_ref[...] = _adamw_math(w_ref[...], g, m_ref[...], v_ref[...])

    return pl.pallas_call(
        body, name="adamw_small", grid=(1,),
        in_specs=[_const(w.shape), _const(parts.shape), _const(w.shape), _const(w.shape)],
        out_specs=[_const(w.shape)] * 4, out_shape=[jax.ShapeDtypeStruct(w.shape, F32)] * 4,
        compiler_params=_cp(("arbitrary",)),
    )(w, parts, m, v)


REPLICATED = (("norm_mix_pre", 1024), ("norm_mix_post", 1024), ("norm_ffn_pre", 1024), ("norm_ffn_post", 1024),
              ("rel_bias", 256), ("sinks", 8), ("rwkv_shift_mix", 1792), ("w0", 512), ("a0", 512), ("k_k", 512),
              ("k_a", 512), ("r_k", 512), ("ln_x_g", 512), ("ln_x_b", 512), ("conv_b", 8192))
SMALL_SHARDED = (("w_decay_up", LORA_DECAY, D_RWKV), ("w_iclr_up", LORA_ICLR, D_RWKV),
                 ("w_gate_up", LORA_GATE, D_RWKV), ("conv_w", 3, 2 * D_FF))
BIG = (("w_in", D_MODEL, 640), ("w_out", 256, D_MODEL), ("w_ffn_up", D_MODEL, 2048), ("w_ffn_down", 1024, D_MODEL))
PACK_ALIGN = 8 * LANES


def _pack(pieces):
    flat = []
    for t in pieces:
        t = t.reshape(-1)
        pad = (-t.shape[0]) % LANES
        flat.append(jnp.pad(t, (0, pad)) if pad else t)
    flat = jnp.concatenate(flat)
    pad = (-flat.shape[0]) % PACK_ALIGN
    return jnp.pad(flat, (0, pad)).reshape(-1, LANES)


def _unpack(buf, sizes):
    flat, out, off = buf.reshape(-1), [], 0
    for n in sizes:
        out.append(flat[off:off + n])
        off += n + ((-n) % LANES)
    return out


def kernel(x, norm_mix_pre, norm_mix_post, norm_ffn_pre, norm_ffn_post, w_in, rel_bias, sinks, rwkv_shift_mix, w0, w_decay_up, a0, w_iclr_up, w_gate_up, k_k, k_a, r_k, ln_x_g, ln_x_b, w_out, w_ffn_up, conv_w, conv_b, w_ffn_down, loss_target, m_norm_mix_pre, m_norm_mix_post, m_norm_ffn_pre, m_norm_ffn_post, m_w_in, m_rel_bias, m_sinks, m_rwkv_shift_mix, m_w0, m_w_decay_up, m_a0, m_w_iclr_up, m_w_gate_up, m_k_k, m_k_a, m_r_k, m_ln_x_g, m_ln_x_b, m_w_out, m_w_ffn_up, m_conv_w, m_conv_b, m_w_ffn_down, v_norm_mix_pre, v_norm_mix_post, v_norm_ffn_pre, v_norm_ffn_post, v_w_in, v_rel_bias, v_sinks, v_rwkv_shift_mix, v_w0, v_w_decay_up, v_a0, v_w_iclr_up, v_w_gate_up, v_k_k, v_k_a, v_r_k, v_ln_x_g, v_ln_x_b, v_w_out, v_w_ffn_up, v_conv_w, v_conv_b, v_w_ffn_down):
    given = dict(locals())
    names = [n for n, _ in REPLICATED] + [n for n, _, _ in SMALL_SHARDED] + [n for n, _, _ in BIG]
    order = ["norm_mix_pre", "norm_mix_post", "norm_ffn_pre", "norm_ffn_post", "w_in", "rel_bias", "sinks",
             "rwkv_shift_mix", "w0", "w_decay_up", "a0", "w_iclr_up", "w_gate_up", "k_k", "k_a", "r_k", "ln_x_g",
             "ln_x_b", "w_out", "w_ffn_up", "conv_w", "conv_b", "w_ffn_down"]
    assert sorted(names) == sorted(order)
    shard = 2 * lax.axis_index("x") + lax.axis_index("y")

    big_sh = {n: given[n].reshape(a, b).astype(BF16) for n, a, b in BIG}
    small_sh = [given[n].reshape(r, c // N_CHIPS) for n, r, c in SMALL_SHARDED]
    gathered = _gather_weights([big_sh["w_in"]], small_sh)
    rest = ("w_out", "w_ffn_up", "w_ffn_down")
    win_st, rest_sh = lax.optimization_barrier((gathered[0], [big_sh[n] for n in rest]))
    sm = {n: given[n] for n, _ in REPLICATED}
    sm["r_k"] = r_k.reshape(N_Q_HEADS, HEAD_DIM)
    for (n, r, c), st in zip(SMALL_SHARDED, gathered[1:]):
        sm[n] = st.transpose(1, 0, 2).reshape(r, c)

    lands = [lax.dynamic_update_slice(lax.empty((N_CHIPS,) + t.shape, BF16), t[None], (shard, 0, 0)) for t in rest_sh]
    plan_w = _plan_gather(len(rest))
    w_sems, w_bufs, token = _copies_start("gather_rest_start", rest_sh + lands, plan_w, 9)
    sm["norm_mix_pre"] = norm_mix_pre + token[0:1, 0:1]

    def on_rest_weights(after):
        out = _copies_wait("gather_rest_wait", w_sems, w_bufs, plan_w, 9, after)
        wout_st, wup_st, wdown_st = out[3:]
        return wout_st.reshape(D_MODEL, D_MODEL), wup_st, wdown_st.reshape(D_FF, D_MODEL)

    red_a = _Reduction("a", (1024, D_MODEL, 256))
    red_b = _Reduction("b", (D_MODEL,))

    def on_grads_a(gs):
        d_wdown, d_wup, d_wout = gs
        return red_a.start([d_wdown.reshape(N_CHIPS, 1024, D_MODEL), d_wup, d_wout.reshape(N_CHIPS, 256, D_MODEL)])

    handlers = {"rest_weights": on_rest_weights, "grads_a": on_grads_a, "seam_1": red_a.after_pair,
                "seam_2": red_a.after_chips, "grads_b": lambda g: red_b.start([g])}
    steps = _local_step(x[0], loss_target[0], sm, win_st)
    kind, payload = next(steps)
    while True:
        try:
            kind, payload = steps.send(handlers[kind](payload))
        except StopIteration as done:
            loss, grad_x, grads = done.value
            break
    loss = lax.psum(loss[0, 0], ("x", "y", "c"))

    small_names = [n for n, _ in REPLICATED] + [n for n, _, _ in SMALL_SHARDED]

    def shard_cols(t, s):
        return t[:, s * (t.shape[1] // N_CHIPS):(s + 1) * (t.shape[1] // N_CHIPS)]

    for_chip = jnp.stack([_pack([grads[n] for n, _ in REPLICATED]
                                + [shard_cols(grads[n], s) for n, _, _ in SMALL_SHARDED]) for s in range(N_CHIPS)])
    me = 2 * shard + lax.axis_index("c")
    mine = lax.dynamic_index_in_dim(for_chip, shard, 0, keepdims=True)
    land = lax.dynamic_update_slice(lax.empty((N_DEV,) + for_chip.shape[1:], F32), mine, (me, 0, 0))

    def plan_small(refs):
        x, y, c, _ = _place()
        out = []
        for rel in range(1, N_DEV):
            px, py, pc = x ^ (rel >> 2), y ^ ((rel >> 1) & 1), c ^ (rel & 1)
            out.append((refs[0].at[2 * px + py], refs[1].at[4 * x + 2 * y + c], (px, py, pc)))
        return out

    s_sems, s_bufs, _ = _copies_start("grad_small_start", [for_chip, land], plan_small, N_DEV - 1)

    red_b.after_pair(grad_x)
    g_out = {}
    g_out["w_ffn_down"], g_out["w_ffn_up"], g_out["w_out"] = red_a.finish(grad_x)

    delta, new_m, new_v = {}, {}, {}
    for n, a, b in reversed(BIG):
        if n == "w_out":
            red_b.after_chips(delta["w_ffn_up"])
        if n == "w_in":
            parts = _copies_wait("grad_small_wait", s_sems, s_bufs, plan_small, N_DEV - 1, delta["w_out"])[1]
            packs = [_pack([given[pre + n2] for n2 in small_names]) for pre in ("", "m_", "v_")]
            small_sizes = [int(np.prod(given[n2].shape)) for n2 in small_names]
            upd = [_unpack(t, small_sizes) for t in _adamw_small(packs[0], parts, packs[1], packs[2])]
            for n2, d, nm, nv, g in zip(small_names, *upd):
                shape = given[n2].shape
                delta[n2], new_m[n2], new_v[n2], g_out[n2] = (t.reshape(shape) for t in (d, nm, nv, g))
            g_out[n], = red_b.finish(delta["w_out"])
        d, nm, nv = _adamw(given[n].reshape(a, b), g_out[n], given["m_" + n].reshape(a, b),
                           given["v_" + n].reshape(a, b), "adamw_" + n, 128)
        delta[n], new_m[n], new_v[n] = d, nm, nv

    def shaped(d):
        return [d[n].reshape(given[n].shape) for n in order]

    return (loss, grad_x.reshape(x.shape), *shaped(g_out), *shaped(delta), *shaped(new_m), *shaped(new_v))
```

```python
import math

import numpy as np
import jax
import jax.numpy as jnp
from jax import lax
from jax.experimental import pallas as pl
from jax.experimental.pallas import tpu as pltpu

F32 = jnp.float32
BF16 = jnp.bfloat16
MESH = pl.DeviceIdType.MESH

SEQ = 2048
D_MODEL = 1024
HEAD_DIM = 64
D_ATTN = 512
D_RWKV = 512
D_KV = 128
N_Q_HEADS = 8
N_KV_HEADS = 2
Q_PER_KV = 4
BLOCK = 128
N_BUCKETS = 32
MAX_DISTANCE = 128
LORA_DECAY = 64
LORA_ICLR = 64
LORA_GATE = 128
RWKV_COLS = 3 * D_RWKV + LORA_DECAY + LORA_ICLR + LORA_GATE
P_OFF = D_ATTN + 2 * D_KV
D_IN = P_OFF + RWKV_COLS
D_FF = 4096
NORM_EPS = 1e-6
GN_EPS = 64e-5
NEG_INF = -1e30
N_CHIPS = 4
N_DEV = 8

ADAM_LR = 0.001
ADAM_B1 = 0.9
ADAM_B2 = 0.999
ADAM_EPS = 1e-08
ADAM_WD = 0.01
ADAM_STEP = 10

VMEM_LIMIT = 52 * 1024 * 1024
LANES = 128


def _cp(sem=None, vmem=VMEM_LIMIT):
    kw = dict(vmem_limit_bytes=vmem)
    if sem is not None:
        kw["dimension_semantics"] = sem
    return pltpu.CompilerParams(**kw)


def _rows(tr, nc):
    return pl.BlockSpec((tr, nc), lambda i: (i, 0))


def _const(shape):
    return pl.BlockSpec(shape, lambda *_: (0,) * len(shape))


ANY = pl.BlockSpec(memory_space=pl.ANY)


def _split(x, n):
    parts = []
    for _ in range(n - 1):
        h = x.astype(BF16)
        parts.append(h)
        x = x - h.astype(F32)
    parts.append(x.astype(BF16))
    return parts


def _dot(a, b, dn=(((1,), (0,)), ((), ()))):
    return lax.dot_general(a, b, dn, preferred_element_type=F32)


NN = (((1,), (0,)), ((), ()))
NT = (((1,), (1,)), ((), ()))
TN = (((0,), (0,)), ((), ()))


def _dot_ind(x, ind_bf16, n=3):
    acc = None
    for part in _split(x, n):
        t = _dot(part, ind_bf16)
        acc = t if acc is None else acc + t
    return acc


def _head_ones(n):
    r = lax.broadcasted_iota(jnp.int32, (n, n), 0) >> 6
    c = lax.broadcasted_iota(jnp.int32, (n, n), 1) >> 6
    return jnp.where(r == c, 1.0, 0.0).astype(BF16)


def _matmul(a, b, mode, name, *, m, n, k, tm, tn, a_spec=None, b_spec=None, out=None, out_dtype=F32):
    keep_at = mode == "tn" and m == tm and n > tn

    def body(a_ref, b_ref, o_ref, *scratch):
        if keep_at:
            at_ref, = scratch

            @pl.when(pl.program_id(1) == 0)
            def _():
                at_ref[...] = a_ref[...].T

            o_ref[...] = _dot(at_ref[...], b_ref[...], NN).astype(out_dtype)
        else:
            o_ref[...] = _dot(a_ref[...], b_ref[...], {"nn": NN, "nt": NT, "tn": TN}[mode]).astype(out_dtype)

    if a_spec is None:
        a_spec = pl.BlockSpec((k, tm), lambda i, j: (0, i)) if mode == "tn" else pl.BlockSpec((tm, k), lambda i, j: (i, 0))
    if b_spec is None:
        b_spec = pl.BlockSpec((tn, k), lambda i, j: (j, 0)) if mode == "nt" else pl.BlockSpec((k, tn), lambda i, j: (0, j))
    return pl.pallas_call(
        body, name=name, grid=(m // tm, n // tn),
        in_specs=[a_spec, b_spec],
        out_specs=pl.BlockSpec((tm, tn), lambda i, j: (i, j)) if out is None else out[1],
        out_shape=jax.ShapeDtypeStruct((m, n) if out is None else out[0], out_dtype),
        scratch_shapes=[pltpu.VMEM((tm, k), a.dtype)] if keep_at else [],
        compiler_params=_cp(("parallel", "arbitrary" if keep_at else "parallel")),
    )(a, b)


def _matmul_nt_shards(a, b_st, name, *, m, n, tm, tn, a_spec, a_piece):
    ks = b_st.shape[2]

    def body(a_ref, b_ref, o_ref):
        acc = _dot(a_piece(a_ref, 0), b_ref[0], NT)
        for s in range(1, N_CHIPS):
            acc = acc + _dot(a_piece(a_ref, s), b_ref[s], NT)
        o_ref[...] = acc

    return pl.pallas_call(
        body, name=name, grid=(m // tm, n // tn),
        in_specs=[a_spec, pl.BlockSpec((N_CHIPS, tn, ks), lambda i, j: (0, j, 0))],
        out_specs=pl.BlockSpec((tm, tn), lambda i, j: (i, j)),
        out_shape=jax.ShapeDtypeStruct((m, n), F32),
        compiler_params=_cp(("parallel", "parallel")),
    )(a, b_st)


def _rstd(x):
    return lax.rsqrt(jnp.mean(x * x, axis=-1, keepdims=True) + NORM_EPS)


def _rms_bwd(x, r, g, dy):
    gy = dy * g
    return r * gy - x * ((r * r * r) * (jnp.sum(x * gy, axis=-1, keepdims=True) / x.shape[-1]))


TR = 256


def _norm_cast(x, g, name):
    def body(x_ref, g_ref, h_ref):
        x = x_ref[...]
        h_ref[...] = (x * _rstd(x) * g_ref[...]).astype(BF16)

    return pl.pallas_call(
        body, name=name, grid=(SEQ // TR,),
        in_specs=[_rows(TR, D_MODEL), _const((1, D_MODEL))],
        out_specs=_rows(TR, D_MODEL),
        out_shape=jax.ShapeDtypeStruct((SEQ, D_MODEL), BF16),
        compiler_params=_cp(("parallel",)),
    )(x, g)


def _mix_norm(x, mix, g2, g3):
    def body(x_ref, mix_ref, g2_ref, g3_ref, x2_ref, h3_ref):
        mixv = mix_ref[...]
        x2 = x_ref[...] + mixv * _rstd(mixv) * g2_ref[...]
        x2_ref[...] = x2
        h3_ref[...] = (x2 * _rstd(x2) * g3_ref[...]).astype(BF16)

    return pl.pallas_call(
        body, name="mix_norm", grid=(SEQ // TR,),
        in_specs=[_rows(TR, D_MODEL), _rows(TR, D_MODEL), _const((1, D_MODEL)), _const((1, D_MODEL))],
        out_specs=[_rows(TR, D_MODEL), _rows(TR, D_MODEL)],
        out_shape=[jax.ShapeDtypeStruct((SEQ, D_MODEL), F32), jax.ShapeDtypeStruct((SEQ, D_MODEL), BF16)],
        compiler_params=_cp(("parallel",)),
    )(x, mix, g2, g3)


def _loss_head(x2, f, g4, target):
    def body(x2_ref, f_ref, g4_ref, t_ref, loss_ref, dy_ref, df_ref, dg_ref):
        i = pl.program_id(0)
        f = f_ref[...]
        g4 = g4_ref[...]
        r = _rstd(f)
        e = x2_ref[...] + f * r * g4 - t_ref[...]
        dy = e * (1.0 / D_MODEL)
        dy_ref[...] = dy
        df_ref[...] = _rms_bwd(f, r, g4, dy).astype(BF16)
        part = 0.5 * jnp.sum(jnp.sum(e * e, axis=-1, keepdims=True), axis=0, keepdims=True) * (1.0 / D_MODEL)
        dg = jnp.sum(dy * f * r, axis=0, keepdims=True)

        @pl.when(i == 0)
        def _():
            loss_ref[...] = jnp.zeros_like(loss_ref)
            dg_ref[...] = jnp.zeros_like(dg_ref)

        loss_ref[...] += jnp.broadcast_to(part, loss_ref.shape)
        dg_ref[...] += dg

    return pl.pallas_call(
        body, name="loss_head", grid=(SEQ // TR,),
        in_specs=[_rows(TR, D_MODEL), _rows(TR, D_MODEL), _const((1, D_MODEL)), _rows(TR, D_MODEL)],
        out_specs=[_const((8, LANES)), _rows(TR, D_MODEL), _rows(TR, D_MODEL), _const((1, D_MODEL))],
        out_shape=[jax.ShapeDtypeStruct((8, LANES), F32), jax.ShapeDtypeStruct((SEQ, D_MODEL), F32),
                   jax.ShapeDtypeStruct((SEQ, D_MODEL), BF16), jax.ShapeDtypeStruct((1, D_MODEL), F32)],
        compiler_params=_cp(("arbitrary",)),
    )(x2, f, g4, target)


def _mid_bwd(x2, mix, dy, dh3, g2, g3):
    def body(x2_ref, mix_ref, dy_ref, dh3_ref, g2_ref, g3_ref, dx2_ref, dmix_ref, dg2_ref, dg3_ref):
        i = pl.program_id(0)
        x2 = x2_ref[...]
        mixv = mix_ref[...]
        dh3 = dh3_ref[...]
        r3 = _rstd(x2)
        dx2 = dy_ref[...] + _rms_bwd(x2, r3, g3_ref[...], dh3)
        dx2_ref[...] = dx2
        r2 = _rstd(mixv)
        dmix_ref[...] = _rms_bwd(mixv, r2, g2_ref[...], dx2).astype(BF16)

        @pl.when(i == 0)
        def _():
            dg2_ref[...] = jnp.zeros_like(dg2_ref)
            dg3_ref[...] = jnp.zeros_like(dg3_ref)

        dg3_ref[...] += jnp.sum(dh3 * x2 * r3, axis=0, keepdims=True)
        dg2_ref[...] += jnp.sum(dx2 * mixv * r2, axis=0, keepdims=True)

    return pl.pallas_call(
        body, name="mid_bwd", grid=(SEQ // TR,),
        in_specs=[_rows(TR, D_MODEL)] * 4 + [_const((1, D_MODEL))] * 2,
        out_specs=[_rows(TR, D_MODEL), _rows(TR, D_MODEL), _const((1, D_MODEL)), _const((1, D_MODEL))],
        out_shape=[jax.ShapeDtypeStruct((SEQ, D_MODEL), F32), jax.ShapeDtypeStruct((SEQ, D_MODEL), BF16),
                   jax.ShapeDtypeStruct((1, D_MODEL), F32), jax.ShapeDtypeStruct((1, D_MODEL), F32)],
        compiler_params=_cp(("arbitrary",)),
    )(x2, mix, dy, dh3, g2, g3)


def _first_bwd(x, dx2, dh1, g1):
    def body(x_ref, dx2_ref, dh1_ref, g1_ref, dx_ref, dg1_ref):
        i = pl.program_id(0)
        x = x_ref[...]
        dh1 = dh1_ref[...]
        r = _rstd(x)
        dx_ref[...] = dx2_ref[...] + _rms_bwd(x, r, g1_ref[...], dh1)

        @pl.when(i == 0)
        def _():
            dg1_ref[...] = jnp.zeros_like(dg1_ref)

        dg1_ref[...] += jnp.sum(dh1 * x * r, axis=0, keepdims=True)

    return pl.pallas_call(
        body, name="first_bwd", grid=(SEQ // TR,),
        in_specs=[_rows(TR, D_MODEL)] * 3 + [_const((1, D_MODEL))],
        out_specs=[_rows(TR, D_MODEL), _const((1, D_MODEL))],
        out_shape=[jax.ShapeDtypeStruct((SEQ, D_MODEL), F32), jax.ShapeDtypeStruct((1, D_MODEL), F32)],
        compiler_params=_cp(("arbitrary",)),
    )(x, dx2, dh1, g1)


TC = 256
N_CB = D_FF // TC
GELU_C = math.sqrt(2.0 / math.pi)


def _shift_down(u, s):
    rolled = pltpu.roll(u, s, 0)
    row = lax.broadcasted_iota(jnp.int32, u.shape, 0)
    return jnp.where(row >= s, rolled, 0.0)


def _shift_up(u, s):
    n = u.shape[0]
    rolled = pltpu.roll(u, n - s, 0)
    row = lax.broadcasted_iota(jnp.int32, u.shape, 0)
    return jnp.where(row < n - s, rolled, 0.0)


def _conv3(u, w, b):
    return b + w[0:1] * _shift_down(u, 2) + w[1:2] * _shift_down(u, 1) + w[2:3] * u


def _gelu_and_grad(x):
    inner = GELU_C * (x + 0.044715 * (x * x * x))
    t = jnp.tanh(inner)
    gelu = 0.5 * x * (1.0 + t)
    dgelu = 0.5 * (1.0 + t) + 0.5 * x * (1.0 - t * t) * (GELU_C * (1.0 + 3 * 0.044715 * (x * x)))
    return gelu, dgelu


def _ffn_specs():
    col = lambda off: pl.BlockSpec((SEQ, TC), lambda *g: (0, g[-1] + off))
    w = lambda off: pl.BlockSpec((3, TC), lambda *g: (0, g[-1] + off))
    b = lambda off: pl.BlockSpec((1, TC), lambda *g: (0, g[-1] + off))
    return col, w, b


def _ffn_act(u, conv_w, conv_b):
    col, w, b = _ffn_specs()

    def body(ug_ref, uv_ref, wg_ref, wv_ref, bg_ref, bv_ref, act_ref):
        gate = _conv3(ug_ref[...], wg_ref[...], bg_ref[...])
        val = _conv3(uv_ref[...], wv_ref[...], bv_ref[...])
        act_ref[...] = (_gelu_and_grad(gate)[0] * val).astype(BF16)

    return pl.pallas_call(
        body, name="ffn_act", grid=(N_CB,),
        in_specs=[col(0), col(N_CB), w(0), w(N_CB), b(0), b(N_CB)],
        out_specs=col(0),
        out_shape=jax.ShapeDtypeStruct((SEQ, D_FF), BF16),
        compiler_params=_cp(("parallel",)),
    )(u, u, conv_w, conv_w, conv_b, conv_b)


def _ffn_act_bwd(u, dact, conv_w, conv_b):
    col, w, b = _ffn_specs()
    both = lambda rows: pl.BlockSpec((2, rows, TC), lambda j: (0, 0, j))

    def body(ug_ref, uv_ref, da_ref, wg_ref, wv_ref, bg_ref, bv_ref, du_ref, dw_ref, db_ref):
        ug, uv = ug_ref[...], uv_ref[...]
        wg, wv = wg_ref[...], wv_ref[...]
        gate = _conv3(ug, wg, bg_ref[...])
        val = _conv3(uv, wv, bv_ref[...])
        gelu, dgelu = _gelu_and_grad(gate)
        da = da_ref[...]
        for h, (duc, uh, wh) in enumerate(((da * val * dgelu, ug, wg), (da * gelu, uv, wv))):
            up1, up2 = _shift_up(duc, 1), _shift_up(duc, 2)
            du_ref[h] = (wh[2:3] * duc + wh[1:2] * up1 + wh[0:1] * up2).astype(BF16)
            db_ref[h] = jnp.sum(duc, axis=0, keepdims=True)
            dw_ref[h] = jnp.concatenate(
                [jnp.sum(up2 * uh, axis=0, keepdims=True), jnp.sum(up1 * uh, axis=0, keepdims=True),
                 jnp.sum(duc * uh, axis=0, keepdims=True)], axis=0)

    return pl.pallas_call(
        body, name="ffn_act_bwd", grid=(N_CB,),
        in_specs=[col(0), col(N_CB), col(0), w(0), w(N_CB), b(0), b(N_CB)],
        out_specs=[both(SEQ), both(3), both(1)],
        out_shape=[jax.ShapeDtypeStruct((2, SEQ, D_FF), BF16), jax.ShapeDtypeStruct((2, 3, D_FF), F32),
                   jax.ShapeDtypeStruct((2, 1, D_FF), F32)],
        compiler_params=_cp(("parallel",)),
    )(u, u, dact, conv_w, conv_w, conv_b, conv_b)


def _t5_onehot():
    rel = (np.arange(BLOCK)[:, None] + BLOCK) - np.arange(2 * BLOCK)[None, :]
    n = np.maximum(rel, 0)
    max_exact = N_BUCKETS // 2
    large = max_exact + (np.log(np.maximum(n, 1).astype(np.float32) / np.float32(max_exact))
                         / np.float32(math.log(MAX_DISTANCE / max_exact))
                         * np.float32(N_BUCKETS - max_exact)).astype(np.int32)
    large = np.minimum(large, N_BUCKETS - 1)
    bucket = np.where(n < max_exact, n, large).reshape(-1)
    return (bucket[None, :] == np.arange(N_BUCKETS)[:, None]).astype(np.float32)


N_REL = BLOCK * 2 * BLOCK


def _bias_table(rel_bias_t, onehot):
    def body(rb_ref, oh_ref, o_ref):
        o_ref[...] = _dot_ind(rb_ref[...], oh_ref[...])

    return pl.pallas_call(
        body, name="bias_table", grid=(1,),
        in_specs=[_const((N_Q_HEADS, N_BUCKETS)), _const((N_BUCKETS, N_REL))],
        out_specs=_const((N_Q_HEADS, N_REL)),
        out_shape=jax.ShapeDtypeStruct((N_Q_HEADS, N_REL), F32),
        compiler_params=_cp(("arbitrary",)),
    )(rel_bias_t, onehot)


def _bias_table_bwd(dbias, onehot):
    def body(db_ref, oh_ref, o_ref):
        acc = None
        for part in _split(db_ref[...], 3):
            t = _dot(part, oh_ref[...], NT)
            acc = t if acc is None else acc + t
        o_ref[...] = acc

    return pl.pallas_call(
        body, name="bias_table_bwd", grid=(1,),
        in_specs=[_const((N_Q_HEADS, N_REL)), _const((N_BUCKETS, N_REL))],
        out_specs=_const((N_Q_HEADS, N_BUCKETS)),
        out_shape=jax.ShapeDtypeStruct((N_Q_HEADS, N_BUCKETS), F32),
        compiler_params=_cp(("arbitrary",)),
    )(dbias, onehot)


def _attn_pieces(n, q, kvp, kvc, bias_ref, sinks_ref, hk):
    qi = lax.broadcasted_iota(jnp.int32, (BLOCK, 2 * BLOCK), 0)
    kj = lax.broadcasted_iota(jnp.int32, (BLOCK, 2 * BLOCK), 1)
    rel = qi + BLOCK - kj
    first_key = jnp.where(n > 0, 0, BLOCK)
    ok = jnp.where(rel >= 0, jnp.where(rel < BLOCK, jnp.where(kj >= first_key, 1.0, 0.0), 0.0), 0.0)
    ok4 = jnp.concatenate([ok] * Q_PER_KV, axis=0) > 0.5
    c0 = hk * HEAD_DIM
    kcat = jnp.concatenate([kvp[:, c0:c0 + HEAD_DIM], kvc[:, c0:c0 + HEAD_DIM]], axis=0).astype(BF16)
    vcat = jnp.concatenate([kvp[:, D_KV + c0:D_KV + c0 + HEAD_DIM], kvc[:, D_KV + c0:D_KV + c0 + HEAD_DIM]],
                           axis=0).astype(BF16)
    q0 = hk * Q_PER_KV * HEAD_DIM
    qs = jnp.concatenate([q[:, q0 + g * HEAD_DIM:q0 + (g + 1) * HEAD_DIM] for g in range(Q_PER_KV)],
                         axis=0).astype(BF16)
    s = _dot(qs, kcat, NT) * (HEAD_DIM ** -0.5) + bias_ref[hk]
    s = jnp.where(ok4, s, NEG_INF)
    row = lax.broadcasted_iota(jnp.int32, (Q_PER_KV * BLOCK, 1), 0)
    sink = jnp.zeros((Q_PER_KV * BLOCK, 1), F32)
    for g in range(Q_PER_KV):
        sink = jnp.where((row >> 7) == g, sinks_ref[hk * Q_PER_KV + g], sink)
    m = jnp.maximum(jnp.max(s, axis=-1, keepdims=True), sink)
    p = jnp.exp(s - m)
    es = jnp.exp(sink - m)
    inv = 1.0 / (jnp.sum(p, axis=-1, keepdims=True) + es)
    return qs, kcat, vcat, p * inv, es * inv


def _attn_in_specs():
    return [pl.BlockSpec((BLOCK, D_ATTN), lambda n: (n, 0)),
            pl.BlockSpec((BLOCK, 2 * D_KV), lambda n: (jnp.maximum(n - 1, 0), D_ATTN // (2 * D_KV))),
            pl.BlockSpec((BLOCK, 2 * D_KV), lambda n: (n, D_ATTN // (2 * D_KV))),
            _const((N_KV_HEADS, Q_PER_KV * BLOCK, 2 * BLOCK)),
            pl.BlockSpec(memory_space=pltpu.SMEM)]


def _unstack_heads(t):
    return jnp.concatenate([t[g * BLOCK:(g + 1) * BLOCK] for g in range(Q_PER_KV)], axis=1)


def _attn_fwd(proj, bias, sinks):
    def body(q_ref, kvp_ref, kvc_ref, bias_ref, sinks_ref, o_ref):
        n = pl.program_id(0)
        q, kvp, kvc = q_ref[...], kvp_ref[...], kvc_ref[...]
        outs = []
        for hk in range(N_KV_HEADS):
            _, _, vcat, probs, _ = _attn_pieces(n, q, kvp, kvc, bias_ref, sinks_ref, hk)
            outs.append(_unstack_heads(_dot(probs.astype(BF16), vcat)))
        o_ref[...] = jnp.concatenate(outs, axis=1)

    return pl.pallas_call(
        body, name="attn_fwd", grid=(SEQ // BLOCK,),
        in_specs=_attn_in_specs(),
        out_specs=pl.BlockSpec((BLOCK, D_ATTN), lambda n: (n, 0)),
        out_shape=jax.ShapeDtypeStruct((SEQ, D_ATTN), F32),
        compiler_params=_cp(("parallel",)),
    )(proj, proj, proj, bias, sinks)


def _attn_bwd(proj, bias, sinks, dcat):
    nb = SEQ // BLOCK

    def body(q_ref, kvp_ref, kvc_ref, bias_ref, sinks_ref, do_ref, dq_ref, dkv_ref, dbias_ref, dsink_ref, dsacc):
        n = pl.program_id(0)

        @pl.when(n == 0)
        def _():
            dkv_ref[...] = jnp.zeros_like(dkv_ref)
            dbias_ref[...] = jnp.zeros_like(dbias_ref)
            dsacc[...] = jnp.zeros_like(dsacc)

        q, kvp, kvc = q_ref[...], kvp_ref[...], kvc_ref[...]
        do_all = do_ref[...]
        dqs, dks, dvs = [], [], []
        for hk in range(N_KV_HEADS):
            qs, kcat, vcat, probs, psink = _attn_pieces(n, q, kvp, kvc, bias_ref, sinks_ref, hk)
            q0 = hk * Q_PER_KV * HEAD_DIM
            do = jnp.concatenate([do_all[:, q0 + g * HEAD_DIM:q0 + (g + 1) * HEAD_DIM] for g in range(Q_PER_KV)],
                                 axis=0).astype(BF16)
            dprobs = _dot(do, vcat, NT)
            dvs.append(_dot(probs.astype(BF16), do, TN))
            rowdot = jnp.sum(probs * dprobs, axis=-1, keepdims=True)
            ds = probs * (dprobs - rowdot)
            dsacc[hk] += -psink * rowdot
            dbias_ref[hk] += ds
            dsb = (ds * (HEAD_DIM ** -0.5)).astype(BF16)
            dqs.append(_unstack_heads(_dot(dsb, kcat)))
            dks.append(_dot(dsb, qs, TN))
        dq_ref[...] = jnp.concatenate(dqs, axis=1)
        upd = jnp.concatenate(dks + dvs, axis=1)
        cur = pl.multiple_of(n * BLOCK, BLOCK)
        dkv_ref[pl.ds(cur, BLOCK), :] += upd[BLOCK:]

        @pl.when(n > 0)
        def _():
            prev = pl.multiple_of((n - 1) * BLOCK, BLOCK)
            dkv_ref[pl.ds(prev, BLOCK), :] += upd[:BLOCK]

        @pl.when(n == nb - 1)
        def _():
            for hk in range(N_KV_HEADS):
                for g in range(Q_PER_KV):
                    tot = jnp.sum(dsacc[hk, g * BLOCK:(g + 1) * BLOCK, :], axis=0, keepdims=True)
                    h = hk * Q_PER_KV + g
                    dsink_ref[h:h + 1, :] = jnp.broadcast_to(tot, (1, LANES))

    return pl.pallas_call(
        body, name="attn_bwd", grid=(nb,),
        in_specs=_attn_in_specs() + [pl.BlockSpec((BLOCK, D_ATTN), lambda n: (n, 0))],
        out_specs=[pl.BlockSpec((BLOCK, D_ATTN), lambda n: (n, 0)), _const((SEQ, 2 * D_KV)),
                   _const((N_KV_HEADS, Q_PER_KV * BLOCK, 2 * BLOCK)), _const((N_Q_HEADS, LANES))],
        out_shape=[jax.ShapeDtypeStruct((SEQ, D_ATTN), F32), jax.ShapeDtypeStruct((SEQ, 2 * D_KV), F32),
                   jax.ShapeDtypeStruct((N_KV_HEADS, Q_PER_KV * BLOCK, 2 * BLOCK), F32),
                   jax.ShapeDtypeStruct((N_Q_HEADS, LANES), F32)],
        scratch_shapes=[pltpu.VMEM((N_KV_HEADS, Q_PER_KV * BLOCK, 1), F32)],
        compiler_params=_cp(("arbitrary",)),
    )(proj, proj, proj, bias, sinks, dcat)


@jax.custom_vjp
def _head_sum(x):
    ones = _head_ones(LANES)
    return jnp.concatenate([_dot_ind(x[:, c:c + LANES], ones, 2) for c in range(0, x.shape[-1], LANES)], axis=1)


_head_sum.defvjp(lambda x: (_head_sum(x), None), lambda _, ct: (_head_sum(ct),))


@jax.custom_vjp
def _bdot(a, w):
    return _dot(a.astype(BF16), w.astype(BF16))


def _bdot_bwd(res, ct):
    a, w = res
    ctb = ct.astype(BF16)
    return _dot(ctb, w.astype(BF16), NT), _dot(a.astype(BF16), ctb, TN)


_bdot.defvjp(lambda a, w: (_bdot(a, w), (a, w)), _bdot_bwd)


def _sigmoid(x):
    return 0.5 * (jnp.tanh(0.5 * x) + 1.0)


def _softplus(x):
    return jnp.maximum(x, 0.0) + jnp.log(1.0 + jnp.exp(-jnp.abs(x)))


def _rwkv_core(r, k, v, zwa, zg, w0, wdu, a0, wiu, wgu, k_k, k_a):
    w_log = -_softplus(-(w0 + _bdot(jnp.tanh(zwa), wdu))) - 0.5
    decay = jnp.exp(-jnp.exp(w_log))
    a = _sigmoid(a0 + _bdot(zwa, wiu))
    g = _bdot(_sigmoid(zg), wgu)
    kk = k * k_k
    kk = kk / jnp.maximum(jnp.sqrt(_head_sum(kk * kk)), 1e-12)
    k2 = k * (1.0 + (a - 1.0) * k_a)
    return r, decay, k2, v, -kk, kk * a, g


def _rwkv_out(o, r, k2, v, g, lng, lnb, rk):
    mu = _head_sum(o) * (1.0 / HEAD_DIM)
    d = o - mu
    var = _head_sum(d * d) * (1.0 / HEAD_DIM)
    on = d * lax.rsqrt(var + GN_EPS) * lng + lnb
    bonus = _head_sum(r * k2 * rk) * v
    return (on + bonus) * g


P_SPLITS = (0, 512, 1024, 1536, 1664, 1792)
N_PREP_PARAMS = 7
HALO = 8


def _shifted_pieces(i, p_ref, halo_ref, mix_ref):
    p = p_ref[:, P_OFF:]
    prev_row = halo_ref[HALO - 1:HALO, P_OFF:] * jnp.where(i > 0, 1.0, 0.0)
    row = lax.broadcasted_iota(jnp.int32, p.shape, 0)
    pprev = jnp.where(row == 0, prev_row, pltpu.roll(p, 1, 0))
    delta = pprev - p
    ps = p + delta * mix_ref[...]
    return [ps[:, a:b] for a, b in zip(P_SPLITS[:-1], P_SPLITS[1:])], delta


def _prep_in_specs():
    return [_rows(TR, D_IN),
            pl.BlockSpec((HALO, D_IN), lambda i: (jnp.maximum(i * (TR // HALO) - 1, 0), 0)),
            _const((1, RWKV_COLS)), _const((1, D_RWKV)), _const((LANES, D_RWKV)), _const((1, D_RWKV)),
            _const((LANES, D_RWKV)), _const((LANES, D_RWKV)), _const((1, D_RWKV)), _const((1, D_RWKV))]


def _rwkv_prep(proj, mix, prm):
    def body(p_ref, halo_ref, mix_ref, *refs):
        prm_refs, outs = refs[:N_PREP_PARAMS], refs[N_PREP_PARAMS:]
        pieces, _ = _shifted_pieces(pl.program_id(0), p_ref, halo_ref, mix_ref)
        vals = _rwkv_core(*pieces, *[t[...] for t in prm_refs])
        for ref, val in zip(outs, vals):
            ref[...] = val

    return pl.pallas_call(
        body, name="rwkv_prep", grid=(SEQ // TR,),
        in_specs=_prep_in_specs(),
        out_specs=[_rows(TR, D_RWKV)] * 7,
        out_shape=[jax.ShapeDtypeStruct((SEQ, D_RWKV), F32)] * 7,
        compiler_params=_cp(("parallel",)),
    )(proj, proj, mix, *prm)


def _rwkv_prep_bwd(proj, mix, prm, cts):
    def body(p_ref, halo_ref, mix_ref, *refs):
        i = pl.program_id(0)
        prm_refs = refs[:N_PREP_PARAMS]
        ct_refs = refs[N_PREP_PARAMS:N_PREP_PARAMS + 10]
        dps_ref, dmix_ref = refs[N_PREP_PARAMS + 10:N_PREP_PARAMS + 12]
        dprm_refs = refs[N_PREP_PARAMS + 12:]
        pieces, delta = _shifted_pieces(i, p_ref, halo_ref, mix_ref)
        _, vjp = jax.vjp(_rwkv_core, *pieces, *[t[...] for t in prm_refs])
        dr1, dr2, dw, dk1, dk2, dv1, dv2, dkkn, db, dg = [t[...] for t in ct_refs]
        grads = vjp((dr1 + dr2, dw, dk1 + dk2, dv1 + dv2, dkkn, db, dg))
        dps = jnp.concatenate(grads[:5], axis=1)
        dps_ref[...] = dps

        @pl.when(i == 0)
        def _():
            dmix_ref[...] = jnp.zeros_like(dmix_ref)
            for ref in dprm_refs:
                ref[...] = jnp.zeros_like(ref)

        dmix_ref[...] += jnp.sum(dps * delta, axis=0, keepdims=True)
        for ref, gval in zip(dprm_refs, grads[5:]):
            ref[...] += gval

    prm_shapes = [(1, D_RWKV), (LANES, D_RWKV), (1, D_RWKV), (LANES, D_RWKV), (LANES, D_RWKV), (1, D_RWKV), (1, D_RWKV)]
    return pl.pallas_call(
        body, name="rwkv_prep_bwd", grid=(SEQ // TR,),
        in_specs=_prep_in_specs() + [_rows(TR, D_RWKV)] * 10,
        out_specs=[_rows(TR, RWKV_COLS), _const((1, RWKV_COLS))] + [_const(s) for s in prm_shapes],
        out_shape=[jax.ShapeDtypeStruct((SEQ, RWKV_COLS), F32), jax.ShapeDtypeStruct((1, RWKV_COLS), F32)]
        + [jax.ShapeDtypeStruct(s, F32) for s in prm_shapes],
        compiler_params=_cp(("arbitrary",)),
    )(proj, proj, mix, *prm, *cts)


def _rwkv_post(o, r, k2, v, g, lng, lnb, rk, attn):
    def body(o_ref, r_ref, k_ref, v_ref, g_ref, lng_ref, lnb_ref, rk_ref, attn_ref, cat_ref):
        rw = _rwkv_out(*[t[...] for t in (o_ref, r_ref, k_ref, v_ref, g_ref, lng_ref, lnb_ref, rk_ref)])
        cat_ref[...] = jnp.concatenate([attn_ref[...], rw], axis=1).astype(BF16)

    return pl.pallas_call(
        body, name="rwkv_post", grid=(SEQ // TR,),
        in_specs=[_rows(TR, D_RWKV)] * 5 + [_const((1, D_RWKV))] * 3 + [_rows(TR, D_ATTN)],
        out_specs=_rows(TR, D_MODEL),
        out_shape=jax.ShapeDtypeStruct((SEQ, D_MODEL), BF16),
        compiler_params=_cp(("parallel",)),
    )(o, r, k2, v, g, lng, lnb, rk, attn)


def _rwkv_post_bwd(o, r, k2, v, g, lng, lnb, rk, dcat):
    def body(o_ref, r_ref, k_ref, v_ref, g_ref, lng_ref, lnb_ref, rk_ref, dcat_ref,
             do_ref, dr_ref, dk_ref, dv_ref, dg_ref, dlng_ref, dlnb_ref, drk_ref):
        i = pl.program_id(0)
        args = [t[...] for t in (o_ref, r_ref, k_ref, v_ref, g_ref, lng_ref, lnb_ref, rk_ref)]
        _, vjp = jax.vjp(_rwkv_out, *args)
        grads = vjp(dcat_ref[:, D_ATTN:])
        for ref, gval in zip((do_ref, dr_ref, dk_ref, dv_ref, dg_ref), grads[:5]):
            ref[...] = gval

        @pl.when(i == 0)
        def _():
            for ref in (dlng_ref, dlnb_ref, drk_ref):
                ref[...] = jnp.zeros_like(ref)

        for ref, gval in zip((dlng_ref, dlnb_ref, drk_ref), grads[5:]):
            ref[...] += gval

    return pl.pallas_call(
        body, name="rwkv_post_bwd", grid=(SEQ // TR,),
        in_specs=[_rows(TR, D_RWKV)] * 5 + [_const((1, D_RWKV))] * 3 + [_rows(TR, D_MODEL)],
        out_specs=[_rows(TR, D_RWKV)] * 5 + [_const((1, D_RWKV))] * 3,
        out_shape=[jax.ShapeDtypeStruct((SEQ, D_RWKV), F32)] * 5 + [jax.ShapeDtypeStruct((1, D_RWKV), F32)] * 3,
        compiler_params=_cp(("arbitrary",)),
    )(o, r, k2, v, g, lng, lnb, rk, dcat)


def _assemble_dproj(dq, dkv, dps, mix):
    last = SEQ // HALO - 1

    def body(dq_ref, dkv_ref, dps_ref, nxt_ref, mix_ref, o_ref):
        i = pl.program_id(0)
        dps = dps_ref[...]
        mixv = mix_ref[...]
        nxt_row = nxt_ref[0:1, :] * jnp.where(i < SEQ // TR - 1, 1.0, 0.0)
        row = lax.broadcasted_iota(jnp.int32, dps.shape, 0)
        up = jnp.where(row == TR - 1, nxt_row, pltpu.roll(dps, TR - 1, 0))
        dp = dps * (1.0 - mixv) + up * mixv
        o_ref[...] = jnp.concatenate([dq_ref[...], dkv_ref[...], dp], axis=1).astype(BF16)

    return pl.pallas_call(
        body, name="assemble_dproj", grid=(SEQ // TR,),
        in_specs=[_rows(TR, D_ATTN), _rows(TR, 2 * D_KV), _rows(TR, RWKV_COLS),
                  pl.BlockSpec((HALO, RWKV_COLS), lambda i: (jnp.minimum((i + 1) * (TR // HALO), last), 0)),
                  _const((1, RWKV_COLS))],
        out_specs=_rows(TR, D_IN),
        out_shape=jax.ShapeDtypeStruct((SEQ, D_IN), BF16),
        compiler_params=_cp(("parallel",)),
    )(dq, dkv, dps, dps, mix)


N_PAIR = D_RWKV // LANES
CHUNK = 64
N_CHUNK = SEQ // CHUNK
GROUP = 8
STATE = (N_PAIR, HEAD_DIM, LANES)


def _lane_sums(lhs_tiles, ones2):
    out = _dot(jnp.concatenate(lhs_tiles, axis=0), ones2)
    return [out[i * HEAD_DIM:(i + 1) * HEAD_DIM] for i in range(len(lhs_tiles))]


def _seg_sum(xs, ones2):
    return _lane_sums([jnp.concatenate(_split(x, 2), axis=1) for x in xs], ones2)


def _seg_sum_rows(xs, ones2):
    out = _dot(jnp.concatenate(_split(jnp.concatenate(xs, axis=0), 2), axis=1), ones2)
    return [out[i * GROUP:(i + 1) * GROUP] for i in range(len(xs))]


def _col_form(rows, diag, ones2):
    zero = jnp.zeros((HEAD_DIM, LANES), BF16)
    tiles = []
    for row in rows:
        hi = row.astype(BF16)
        lo = (row - hi.astype(F32)).astype(BF16)
        tiles.append(jnp.concatenate(
            [jnp.where(diag, jnp.broadcast_to(part, (HEAD_DIM, LANES)), zero) for part in (hi, lo)], axis=1))
    return _lane_sums(tiles, ones2)


def _scan_consts():
    ones2 = jnp.concatenate([_head_ones(LANES)] * 2, axis=0)
    sub = lax.broadcasted_iota(jnp.int32, (HEAD_DIM, LANES), 0)
    lane_in_head = lax.broadcasted_iota(jnp.int32, (HEAD_DIM, LANES), 1) & (HEAD_DIM - 1)
    return ones2, lane_in_head == sub, lane_in_head


def _rows_of_columns(tile):
    t = tile.T
    return jnp.concatenate([t[:CHUNK], t[HEAD_DIM:HEAD_DIM + CHUNK]], axis=1)


def _pair(j):
    return slice(j * LANES, (j + 1) * LANES)


def _scan_fwd(r, w, k, v, kkn, b):
    def body(r_ref, w_ref, k_ref, v_ref, kkn_ref, b_ref, o_ref, st_ref, sa_ref, s_scr):
        c = pl.program_id(0)
        ones2, diag, lane_in_head = _scan_consts()

        @pl.when(c == 0)
        def _():
            s_scr[...] = jnp.zeros_like(s_scr)

        def group(gi, carry):
            row0 = pl.multiple_of(gi * GROUP, GROUP)
            states, ocols = list(carry[:N_PAIR]), list(carry[N_PAIR:])
            tiles = [[t[pl.ds(row0, GROUP), _pair(j)] for t in (r_ref, w_ref, k_ref, v_ref, kkn_ref, b_ref)]
                     for j in range(N_PAIR)]
            def row(j, name, u):
                return tiles[j]["rwkvnb".index(name)][u:u + 1]

            def emit_out(u, after):
                outs = _seg_sum([s[j] * row(j, "r", u + d) for d, s in enumerate(after) for j in range(N_PAIR)], ones2)
                for d in range(2):
                    here = lane_in_head == gi * GROUP + u + d
                    for j in range(N_PAIR):
                        ocols[j] = jnp.where(here, outs[d * N_PAIR + j], ocols[j])

            def vcols_of(u):
                cols = _col_form([row(j, "v", u + d) for d in range(2) for j in range(N_PAIR)], diag, ones2)
                return cols[:N_PAIR], cols[N_PAIR:]

            n_next = [pltpu.roll(tiles[j][4], GROUP - 1, 0) for j in range(N_PAIR)]
            dots = _seg_sum_rows([tiles[j][5] * n_next[j] for j in range(N_PAIR)]
                                 + [tiles[j][2] * n_next[j] for j in range(N_PAIR)], ones2)
            b_n, k_n = dots[:N_PAIR], dots[N_PAIR:]
            w_n = [tiles[j][1] * n_next[j] for j in range(N_PAIR)]

            vcols = vcols_of(0)
            after = None
            for u in range(0, GROUP, 2):
                prods = _seg_sum([states[j] * row(j, "n", u) for j in range(N_PAIR)]
                                 + [states[j] * w_n[j][u:u + 1] for j in range(N_PAIR)], ones2)
                if after is not None:
                    emit_out(u - 2, after)
                nxt = vcols_of(u + 2) if u + 2 < GROUP else None
                first, second = [], []
                for j in range(N_PAIR):
                    sa1 = prods[j]
                    sa2 = prods[N_PAIR + j] + sa1 * b_n[j][u:u + 1] + vcols[0][j] * k_n[j][u:u + 1]
                    s1 = states[j] * row(j, "w", u) + sa1 * row(j, "b", u) + vcols[0][j] * row(j, "k", u)
                    s2 = s1 * row(j, "w", u + 1) + sa2 * row(j, "b", u + 1) + vcols[1][j] * row(j, "k", u + 1)
                    st_ref[row0 + u, j] = s1
                    sa_ref[row0 + u, j] = sa1
                    st_ref[row0 + u + 1, j] = s2
                    sa_ref[row0 + u + 1, j] = sa2
                    first.append(s1)
                    second.append(s2)
                    states[j] = s2
                after, vcols = (first, second), nxt
            emit_out(GROUP - 2, after)
            return tuple(states + ocols)

        zero = jnp.zeros((HEAD_DIM, LANES), F32)
        fin = lax.fori_loop(0, CHUNK // GROUP, group, tuple(s_scr[j] for j in range(N_PAIR)) + (zero,) * N_PAIR)
        for j in range(N_PAIR):
            s_scr[j] = fin[j]
            o_ref[:, _pair(j)] = _rows_of_columns(fin[N_PAIR + j])

    blk = pl.BlockSpec((CHUNK, D_RWKV), lambda c: (c, 0))
    per_step = pl.BlockSpec((CHUNK,) + STATE, lambda c: (c, 0, 0, 0))
    return pl.pallas_call(
        body, name="rwkv_scan_fwd", grid=(N_CHUNK,),
        in_specs=[blk] * 6,
        out_specs=[blk, per_step, per_step],
        out_shape=[jax.ShapeDtypeStruct((SEQ, D_RWKV), F32)] + [jax.ShapeDtypeStruct((SEQ,) + STATE, F32)] * 2,
        scratch_shapes=[pltpu.VMEM(STATE, F32)],
        compiler_params=_cp(("arbitrary",)),
    )(r, w, k, v, kkn, b)


def _scan_bwd(r, w, k, v, kkn, b, do, states, sas, ds_in, prev, name, first_chunk, n_chunks):
    top = first_chunk + n_chunks - 1

    def body(r_ref, w_ref, k_ref, v_ref, kkn_ref, b_ref, do_ref, st_ref, before_ref, sa_ref, ds_in_ref, *rest):
        dr_ref, dw_ref, dk_ref, dv_ref, dkkn_ref, db_ref, ds_out_ref, ds_scr = rest[-8:]
        i = pl.program_id(0)
        ones2, diag, lane_in_head = _scan_consts()

        @pl.when(i == 0)
        def _():
            ds_scr[...] = ds_in_ref[...]

        entry = [before_ref[0, j] * jnp.where(i < top, 1.0, 0.0) for j in range(N_PAIR)]

        def reverse(gr, carry):
            gi = CHUNK // GROUP - 1 - gr
            row0 = pl.multiple_of(gi * GROUP, GROUP)
            dstates, dvcols = list(carry[:N_PAIR]), list(carry[N_PAIR:])
            tiles = [[t[pl.ds(row0, GROUP), _pair(j)]
                      for t in (r_ref, w_ref, k_ref, v_ref, kkn_ref, b_ref, do_ref)] for j in range(N_PAIR)]
            rows = [[[None] * GROUP for _ in range(5)] for _ in range(N_PAIR)]

            def row(j, name, u):
                return tiles[j]["rwkvnbd".index(name)][u:u + 1]

            def cols_of(u):
                cols = _col_form([row(j, name, u - d) for d in range(2) for name in "dv" for j in range(N_PAIR)],
                                 diag, ones2)
                return [[(cols[(2 * d) * N_PAIR + j], cols[(2 * d + 1) * N_PAIR + j]) for j in range(N_PAIR)]
                        for d in range(2)]

            def emit_dv(u, dsps):
                outs = _seg_sum([dsp[j] * row(j, "k", u - d) for d, dsp in enumerate(dsps) for j in range(N_PAIR)], ones2)
                for d in range(2):
                    here = lane_in_head == gi * GROUP + u - d
                    for j in range(N_PAIR):
                        dvcols[j] = jnp.where(here, outs[d * N_PAIR + j], dvcols[j])

            b_prev = [pltpu.roll(tiles[j][5], 1, 0) for j in range(N_PAIR)]
            dots = _seg_sum_rows([tiles[j][4] * b_prev[j] for j in range(N_PAIR)]
                                 + [tiles[j][0] * tiles[j][5] for j in range(N_PAIR)], ones2)
            n_b, r_b = dots[:N_PAIR], dots[N_PAIR:]
            w_b = [tiles[j][1] * b_prev[j] for j in range(N_PAIR)]

            def outputs(u, j, dsp, dsa, docol, vcol):
                tl = gi * GROUP + u
                if u > 0:
                    s_prev = st_ref[tl - 1, j]
                else:
                    s_prev = jnp.where(gi == 0, entry[j], st_ref[jnp.maximum(tl - 1, 0), j])
                rows[j][0][u] = jnp.sum(st_ref[tl, j] * docol, axis=0, keepdims=True)
                rows[j][1][u] = jnp.sum(dsp * s_prev, axis=0, keepdims=True)
                rows[j][2][u] = jnp.sum(dsp * vcol, axis=0, keepdims=True)
                rows[j][3][u] = jnp.sum(s_prev * dsa, axis=0, keepdims=True)
                rows[j][4][u] = jnp.sum(dsp * sa_ref[tl, j], axis=0, keepdims=True)

            cols = cols_of(GROUP - 1)
            before = None
            for u in range(GROUP - 1, 0, -2):
                dsp1 = [dstates[j] + cols[0][j][0] * row(j, "r", u) for j in range(N_PAIR)]
                prods = _seg_sum([dsp1[j] * row(j, "b", u) for j in range(N_PAIR)]
                                 + [dsp1[j] * w_b[j][u:u + 1] for j in range(N_PAIR)], ones2)
                if before is not None:
                    emit_dv(u + 2, before)
                nxt = cols_of(u - 2) if u >= 2 else None
                dsp2 = []
                for j in range(N_PAIR):
                    dsa1 = prods[j]
                    dsa2 = prods[N_PAIR + j] + dsa1 * n_b[j][u:u + 1] + cols[1][j][0] * r_b[j][u - 1:u]
                    mid = dsp1[j] * row(j, "w", u) + dsa1 * row(j, "n", u) + cols[1][j][0] * row(j, "r", u - 1)
                    outputs(u, j, dsp1[j], dsa1, *cols[0][j])
                    outputs(u - 1, j, mid, dsa2, *cols[1][j])
                    dstates[j] = mid * row(j, "w", u - 1) + dsa2 * row(j, "n", u - 1)
                    dsp2.append(mid)
                before, cols = (dsp1, dsp2), nxt
            emit_dv(1, before)
            for j in range(N_PAIR):
                for ref, rr in zip((dr_ref, dw_ref, dk_ref, dkkn_ref, db_ref), rows[j]):
                    ref[pl.ds(row0, GROUP), _pair(j)] = jnp.concatenate(rr, axis=0)
            return tuple(dstates + dvcols)

        zero = jnp.zeros((HEAD_DIM, LANES), F32)
        dfin = lax.fori_loop(0, CHUNK // GROUP, reverse, tuple(ds_scr[j] for j in range(N_PAIR)) + (zero,) * N_PAIR)
        for j in range(N_PAIR):
            ds_scr[j] = dfin[j]
            dv_ref[:, _pair(j)] = _rows_of_columns(dfin[N_PAIR + j])

        @pl.when(i == n_chunks - 1)
        def _():
            ds_out_ref[...] = ds_scr[...]

    blk = pl.BlockSpec((CHUNK, D_RWKV), lambda i: (top - i, 0))
    per_step = pl.BlockSpec((CHUNK,) + STATE, lambda i: (top - i, 0, 0, 0))
    step_before = pl.BlockSpec((1,) + STATE, lambda i: (jnp.maximum((top - i) * CHUNK - 1, 0), 0, 0, 0))
    prev = [] if prev is None else list(prev)
    outs = pl.pallas_call(
        body, name=name, grid=(n_chunks,),
        in_specs=[blk] * 7 + [per_step, step_before, per_step, _const(STATE)] + [ANY] * len(prev),
        out_specs=[blk] * 6 + [_const(STATE)],
        out_shape=[jax.ShapeDtypeStruct((SEQ, D_RWKV), F32)] * 6 + [jax.ShapeDtypeStruct(STATE, F32)],
        scratch_shapes=[pltpu.VMEM(STATE, F32)],
        input_output_aliases={11 + t: t for t in range(len(prev))},
        compiler_params=_cp(("arbitrary",)),
    )(r, w, k, v, kkn, b, do, states, states, sas, ds_in, *prev)
    return outs[:6], outs[6]


def _stacked(rows, cols, pick):
    return pl.BlockSpec((None, rows, cols), pick)


def _local_step(x, target, sm, win_st):
    def tied(t, token):
        return t if token is None else t + token[0:1, 0:1].reshape((1,) * t.ndim)

    zpad = jnp.zeros((LORA_DECAY, D_RWKV), F32)
    prm = [sm["w0"], jnp.concatenate([sm["w_decay_up"], zpad], axis=0), sm["a0"],
           jnp.concatenate([zpad, sm["w_iclr_up"]], axis=0), sm["w_gate_up"], sm["k_k"], sm["k_a"]]
    mix = sm["rwkv_shift_mix"]
    onehot = jnp.asarray(_t5_onehot(), BF16)
    sinks = sm["sinks"].reshape(N_Q_HEADS)
    lng, lnb, rk = sm["ln_x_g"], sm["ln_x_b"], sm["r_k"].reshape(1, D_RWKV)

    h1 = _norm_cast(x, sm["norm_mix_pre"], "norm_in")
    proj = _matmul(h1, win_st, "nn", "proj", m=SEQ, n=D_IN, k=D_MODEL, tm=SEQ, tn=640,
                   b_spec=_stacked(D_MODEL, 640, lambda i, j: (j, 0, 0)))
    bias = _bias_table(sm["rel_bias"].T, onehot).reshape(N_KV_HEADS, Q_PER_KV * BLOCK, 2 * BLOCK)
    attn = _attn_fwd(proj, bias, sinks)
    r, w, k2, v, kkn, b, g = _rwkv_prep(proj, mix, prm)
    o, states, sas = _scan_fwd(r, w, k2, v, kkn, b)
    wout, wup_st, wdown = yield ("rest_weights", o)
    cat = _rwkv_post(o, r, k2, v, g, lng, lnb, rk, attn)
    mixo = _matmul(cat, wout, "nn", "out_proj", m=SEQ, n=D_MODEL, k=D_MODEL, tm=SEQ, tn=512)
    x2, h3 = _mix_norm(x, mixo, sm["norm_mix_post"], sm["norm_ffn_pre"])
    u = _matmul(h3, wup_st, "nn", "ffn_up", m=SEQ, n=2 * D_FF, k=D_MODEL, tm=SEQ, tn=512,
                b_spec=_stacked(D_MODEL, 512, lambda i, j: (j // 4, 0, j % 4)))
    act = _ffn_act(u, sm["conv_w"], sm["conv_b"])
    f = _matmul(act, wdown, "nn", "ffn_down", m=SEQ, n=D_MODEL, k=D_FF, tm=1024, tn=512)
    loss, dy, df, d_g4 = _loss_head(x2, f, sm["norm_ffn_post"], target)

    dact = _matmul(df, wdown, "nt", "d_act", m=SEQ, n=D_FF, k=D_MODEL, tm=SEQ, tn=512)
    d_wdown = _matmul(act, df, "tn", "d_wdown", m=D_FF, n=D_MODEL, k=SEQ, tm=512, tn=D_MODEL)
    du, d_convw, d_convb = _ffn_act_bwd(u, dact, sm["conv_w"], sm["conv_b"])
    d_convw = d_convw.transpose(1, 0, 2).reshape(3, 2 * D_FF)
    d_convb = d_convb.reshape(1, 2 * D_FF)
    dh3 = _matmul_nt_shards(du, wup_st, "d_h3", m=SEQ, n=D_MODEL, tm=512, tn=512,
                            a_spec=pl.BlockSpec((2, 512, D_FF), lambda i, j: (0, i, 0)),
                            a_piece=lambda ref, s: ref[s // 2, :, (s % 2) * 2048:(s % 2 + 1) * 2048])
    d_wup = _matmul(h3, du, "tn", "d_wup", m=D_MODEL, n=2 * D_FF, k=SEQ, tm=D_MODEL, tn=512,
                    b_spec=pl.BlockSpec((None, SEQ, 512), lambda i, j: (j // 8, 0, j % 8)),
                    out=((N_CHIPS, D_MODEL, 2048), _stacked(D_MODEL, 512, lambda i, j: (j // 4, 0, j % 4))))
    dx2, dmix, d_g2, d_g3 = _mid_bwd(x2, mixo, dy, dh3, sm["norm_mix_post"], sm["norm_ffn_pre"])
    dcat = _matmul(dmix, wout, "nt", "d_cat", m=SEQ, n=D_MODEL, k=D_MODEL, tm=SEQ, tn=512)
    d_wout = _matmul(cat, dmix, "tn", "d_wout", m=D_MODEL, n=D_MODEL, k=SEQ, tm=512, tn=D_MODEL)
    token = yield ("grads_a", (d_wdown, d_wup, d_wout))
    do, dr_p, dk_p, dv_p, dg, d_lng, d_lnb, d_rk = _rwkv_post_bwd(o, r, k2, v, g, lng, tied(lnb, token), rk, dcat)
    half = N_CHUNK // 2
    ds_end = jnp.zeros(STATE, F32)
    late, ds_mid = _scan_bwd(r, w, k2, v, kkn, b, do, states, sas, ds_end, None, "rwkv_scan_bwd_late", half, half)
    token = yield ("seam_1", ds_mid)
    scan_cts, ds_first = _scan_bwd(r, w, k2, v, kkn, b, do, states, sas, tied(ds_mid, token), late,
                                   "rwkv_scan_bwd_early", 0, half)
    dr_s, dw_s, dk_s, dv_s, dkkn_s, db_s = scan_cts
    token = yield ("seam_2", ds_first)
    prep_grads = _rwkv_prep_bwd(proj, tied(mix, token), prm,
                                (dr_s, dr_p, dw_s, dk_s, dk_p, dv_s, dv_p, dkkn_s, db_s, dg))
    dps, d_mix, d_w0, d_wdu, d_a0, d_wiu, d_wgu, d_kk, d_ka = prep_grads
    dq, dkv, dbias, dsink = _attn_bwd(proj, bias, sinks, dcat)
    d_relb = _bias_table_bwd(dbias.reshape(N_Q_HEADS, N_REL), onehot).T
    dproj = _assemble_dproj(dq, dkv, dps, mix)
    d_win = _matmul(h1, dproj, "tn", "d_win", m=D_MODEL, n=D_IN, k=SEQ, tm=D_MODEL, tn=640,
                    out=((N_CHIPS, D_MODEL, 640), _stacked(D_MODEL, 640, lambda i, j: (j, 0, 0))))
    token = yield ("grads_b", d_win)
    dh1 = _matmul_nt_shards(dproj, win_st, "d_h1", m=SEQ, n=D_MODEL, tm=1024, tn=D_MODEL,
                            a_spec=pl.BlockSpec((1024, D_IN), lambda i, j: (i, 0)),
                            a_piece=lambda ref, s: ref[:, s * 640:(s + 1) * 640])
    grad_x, d_g1 = _first_bwd(x, dx2, dh1, tied(sm["norm_mix_pre"], token))

    grads = {
        "norm_mix_pre": d_g1, "norm_mix_post": d_g2, "norm_ffn_pre": d_g3, "norm_ffn_post": d_g4,
        "w_in": d_win, "rel_bias": d_relb, "sinks": dsink[:, 0].reshape(1, N_Q_HEADS),
        "rwkv_shift_mix": d_mix, "w0": d_w0, "w_decay_up": d_wdu[:LORA_DECAY], "a0": d_a0,
        "w_iclr_up": d_wiu[LORA_DECAY:], "w_gate_up": d_wgu, "k_k": d_kk, "k_a": d_ka,
        "r_k": d_rk.reshape(1, N_Q_HEADS, HEAD_DIM), "ln_x_g": d_lng, "ln_x_b": d_lnb,
        "w_out": d_wout, "w_ffn_up": d_wup, "conv_w": d_convw, "conv_b": d_convb, "w_ffn_down": d_wdown,
    }
    return loss, grad_x, grads


def _place():
    x, y, c = lax.axis_index("x"), lax.axis_index("y"), lax.axis_index("c")
    chips = [(1 - x, y), (x, 1 - y), (1 - x, 1 - y)]
    return x, y, c, chips


def _remote(src, dst, sems, idx, to):
    return pltpu.make_async_remote_copy(src_ref=src, dst_ref=dst, send_sem=sems[0].at[idx], recv_sem=sems[1].at[idx],
                                        device_id=to, device_id_type=MESH)


def _half(c, rows):
    return pl.ds(pl.multiple_of(c * (rows // 2), 16), rows // 2)


def _gather_weights(big, small):
    nb, ns = len(big), len(small)

    def body(*refs):
        ins, outs = refs[:nb + ns], refs[nb + ns:2 * (nb + ns)]
        ici, d2d, sml, loc = refs[2 * (nb + ns):2 * (nb + ns) + 2], refs[-5:-3], refs[-3:-1], refs[-1]
        x, y, c, chips = _place()
        me = 2 * x + y
        sib = (x, y, 1 - c)
        local = [pltpu.make_async_copy(ins[a], outs[a].at[me], loc.at[a]) for a in range(nb + ns)]
        for cp in local:
            cp.start()
        sends = []
        for a in range(nb):
            rows = _half(c, big[a].shape[0])
            for kk, chip in enumerate(chips):
                sends.append(_remote(ins[a].at[rows], outs[a].at[me, rows], ici, a * 3 + kk, (*chip, c)))
        for a in range(ns):
            for kk, chip in enumerate(chips):
                sends.append(_remote(ins[nb + a], outs[nb + a].at[me], sml, a * 3 + kk, (*chip, c)))
        for cp in sends:
            cp.start()
        passed = []
        for a in range(nb):
            rows = _half(c, big[a].shape[0])
            for kk, (px, py) in enumerate(chips):
                got = outs[a].at[2 * px + py, rows]
                _remote(got, got, ici, a * 3 + kk, sib).wait_recv()
                fwd = _remote(got, got, d2d, a * 3 + kk, sib)
                fwd.start()
                passed.append(fwd)
        for a in range(nb):
            other = _half(1 - c, big[a].shape[0])
            for kk, (px, py) in enumerate(chips):
                land = outs[a].at[2 * px + py, other]
                _remote(land, land, d2d, a * 3 + kk, sib).wait_recv()
        for a in range(ns):
            for kk, (px, py) in enumerate(chips):
                land = outs[nb + a].at[2 * px + py]
                _remote(land, land, sml, a * 3 + kk, sib).wait_recv()
        for cp in sends + passed:
            cp.wait_send()
        for cp in local:
            cp.wait()

    arrs = list(big) + list(small)
    return pl.pallas_call(
        body, name="gather_weights",
        in_specs=[ANY] * len(arrs), out_specs=[ANY] * len(arrs),
        out_shape=[jax.ShapeDtypeStruct((N_CHIPS,) + t.shape, t.dtype) for t in arrs],
        scratch_shapes=[pltpu.SemaphoreType.DMA((3 * nb,)), pltpu.SemaphoreType.DMA((3 * nb,)),
                        pltpu.SemaphoreType.DMA((3 * nb,)), pltpu.SemaphoreType.DMA((3 * nb,)),
                        pltpu.SemaphoreType.DMA((3 * ns,)), pltpu.SemaphoreType.DMA((3 * ns,)),
                        pltpu.SemaphoreType.DMA((nb + ns,))],
        compiler_params=pltpu.CompilerParams(has_side_effects=True),
    )(*arrs)


HBM = pl.BlockSpec(memory_space=pltpu.HBM)
SEM = pl.BlockSpec(memory_space=pltpu.SEMAPHORE)
EFFECT = pltpu.SideEffectType.DATAFLOW_SIDE_EFFECTING


def _copies_start(name, bufs, plan, n):
    nb = len(bufs)

    def body(*refs):
        ins, sems, token = refs[:nb], refs[nb:nb + 2 * n], refs[-1]
        for kk, (src, dst, dev) in enumerate(plan(ins)):
            pltpu.make_async_remote_copy(src_ref=src, dst_ref=dst, send_sem=sems[2 * kk], recv_sem=sems[2 * kk + 1],
                                         device_id=dev, device_id_type=MESH).start()
        token[...] = jnp.zeros_like(token)

    outs = pl.pallas_call(
        body, name=name,
        out_shape=tuple([pltpu.SemaphoreType.DMA(())] * (2 * n) + [pltpu.HBM(t.shape, t.dtype) for t in bufs]
                        + [jax.ShapeDtypeStruct((8, LANES), F32)]),
        in_specs=[HBM] * nb,
        out_specs=tuple([SEM] * (2 * n) + [HBM] * nb + [pl.BlockSpec(memory_space=pltpu.VMEM)]),
        input_output_aliases={t: 2 * n + t for t in range(nb)},
        compiler_params=pltpu.CompilerParams(has_side_effects=EFFECT),
    )(*[pltpu.with_memory_space_constraint(t, pltpu.HBM) for t in bufs])
    return outs[:2 * n], outs[2 * n:2 * n + nb], outs[-1]


def _copies_wait(name, sems, bufs, plan, n, after):
    nb = len(bufs)

    def body(*refs):
        ins, sem_refs = refs[:nb], refs[nb:nb + 2 * n]
        for kk, (src, dst, dev) in enumerate(plan(ins)):
            cp = pltpu.make_async_remote_copy(src_ref=src, dst_ref=dst, send_sem=sem_refs[2 * kk],
                                              recv_sem=sem_refs[2 * kk + 1], device_id=dev, device_id_type=MESH)
            cp.wait_send()
            cp.wait_recv()

    return pl.pallas_call(
        body, name=name,
        out_shape=tuple(pltpu.HBM(t.shape, t.dtype) for t in bufs),
        in_specs=[HBM] * nb + [SEM] * (2 * n) + [ANY],
        out_specs=tuple([HBM] * nb),
        input_output_aliases={t: t for t in range(nb)},
        compiler_params=pltpu.CompilerParams(has_side_effects=EFFECT),
    )(*bufs, *sems, after)


def _plan_gather(n_w):
    def plan(refs):
        x, y, c, chips = _place()
        me = 2 * x + y
        return [(refs[a], refs[n_w + a].at[me], (*chip, c)) for a in range(n_w) for chip in chips]
    return plan


def _plan_pair_halves(n_g, rows):
    def plan(refs):
        x, y, c, _ = _place()
        return [(refs[a].at[:, _half(1 - c, rows[a])], refs[n_g + a], (x, y, 1 - c)) for a in range(n_g)]
    return plan


def _plan_chip_parts(n_g):
    def plan(refs):
        x, y, c, chips = _place()
        me = 2 * x + y
        return [(refs[a].at[2 * px + py], refs[n_g + a].at[me], (px, py, c))
                for a in range(n_g) for (px, py) in chips]
    return plan


def _plan_pair_fill(n_g, rows):
    def plan(refs):
        x, y, c, _ = _place()
        return [(refs[a].at[_half(c, rows[a])], refs[a].at[_half(c, rows[a])], (x, y, 1 - c)) for a in range(n_g)]
    return plan


def _pair_add(g, got, name):
    _, rows, cols = g.shape
    hr = rows // 2
    tr = min(hr, 256)
    nb = hr // tr

    def body(g_ref, got_ref, p_ref, own_ref):
        val = (g_ref[...] + got_ref[...]).astype(BF16)
        p_ref[...] = val

        @pl.when(pl.program_id(1) == 2 * lax.axis_index("x") + lax.axis_index("y"))
        def _():
            own_ref[...] = val

    def mine(i, s):
        return (2 * lax.axis_index("x") + lax.axis_index("y"), i, 0)

    return pl.pallas_call(
        body, name=name, grid=(nb, N_CHIPS),
        in_specs=[pl.BlockSpec((None, tr, cols), lambda i, s: (s, lax.axis_index("c") * nb + i, 0)),
                  pl.BlockSpec((None, tr, cols), lambda i, s: (s, i, 0))],
        out_specs=[pl.BlockSpec((None, tr, cols), lambda i, s: (s, i, 0)), pl.BlockSpec((None, tr, cols), mine)],
        out_shape=[jax.ShapeDtypeStruct((N_CHIPS, hr, cols), BF16)] * 2,
        compiler_params=_cp(("parallel", "arbitrary")),
    )(g, got)


def _chip_sum(parts, name):
    _, hr, cols = parts.shape
    tr = min(hr, 128)
    nb = hr // tr

    def body(t_ref, o_ref):
        part = [t_ref[s].astype(F32) for s in range(N_CHIPS)]
        o_ref[...] = ((part[0] + part[1]) + part[2]) + part[3]

    return pl.pallas_call(
        body, name=name, grid=(nb,),
        in_specs=[pl.BlockSpec((N_CHIPS, tr, cols), lambda i: (0, i, 0))],
        out_specs=pl.BlockSpec((tr, cols), lambda i: (lax.axis_index("c") * nb + i, 0)),
        out_shape=jax.ShapeDtypeStruct((2 * hr, cols), F32),
        compiler_params=_cp(("parallel",)),
    )(parts)


class _Reduction:
    def __init__(self, tag, rows):
        self.tag, self.n, self.rows = tag, len(rows), rows
        self.plans = (_plan_pair_halves(self.n, rows), _plan_chip_parts(self.n), _plan_pair_fill(self.n, rows))
        self.flight = None

    def _name(self, what):
        return f"grad_{self.tag}_{what}"

    def start(self, gs):
        gots = [lax.empty((N_CHIPS, t.shape[1] // 2, t.shape[2]), F32) for t in gs]
        self.flight = _copies_start(self._name("pair_start"), list(gs) + gots, self.plans[0], self.n)
        return self.flight[2]

    def after_pair(self, after):
        sems, bufs, _ = self.flight
        out = _copies_wait(self._name("pair_wait"), sems, bufs, self.plans[0], self.n, after)
        sums = [_pair_add(g, got, self._name(f"pair_add_{i}"))
                for i, (g, got) in enumerate(zip(out[:self.n], out[self.n:]))]
        self.flight = _copies_start(self._name("chip_start"), [p for p, _ in sums] + [own for _, own in sums],
                                    self.plans[1], 3 * self.n)
        return self.flight[2]

    def after_chips(self, after):
        sems, bufs, _ = self.flight
        out = _copies_wait(self._name("chip_wait"), sems, bufs, self.plans[1], 3 * self.n, after)
        fulls = [_chip_sum(t, self._name(f"chip_sum_{i}")) for i, t in enumerate(out[self.n:])]
        self.flight = _copies_start(self._name("fill_start"), fulls, self.plans[2], self.n)
        return self.flight[2]

    def finish(self, after):
        sems, bufs, _ = self.flight
        return _copies_wait(self._name("fill_wait"), sems, bufs, self.plans[2], self.n, after)


def _adamw_math(w, g, m, v):
    nm = ADAM_B1 * m + (1.0 - ADAM_B1) * g
    nv = ADAM_B2 * v + (1.0 - ADAM_B2) * (g * g)
    m_hat = nm / (1.0 - ADAM_B1 ** ADAM_STEP)
    v_hat = nv / (1.0 - ADAM_B2 ** ADAM_STEP)
    return -ADAM_LR * (m_hat / (jnp.sqrt(v_hat) + ADAM_EPS) + ADAM_WD * w), nm, nv


def _adamw(w, g, m, v, name, tr):
    r, cdim = w.shape

    def body(w_ref, g_ref, m_ref, v_ref, d_ref, nm_ref, nv_ref):
        d_ref[...], nm_ref[...], nv_ref[...] = _adamw_math(w_ref[...], g_ref[...], m_ref[...], v_ref[...])

    return pl.pallas_call(
        body, name=name, grid=(r // tr,), in_specs=[_rows(tr, cdim)] * 4, out_specs=[_rows(tr, cdim)] * 3,
        out_shape=[jax.ShapeDtypeStruct((r, cdim), F32)] * 3, compiler_params=_cp(("parallel",)),
    )(w, g, m, v)


def _adamw_small(w, parts, m, v):
    def body(w_ref, p_ref, m_ref, v_ref, d_ref, nm_ref, nv_ref, g_ref):
        g = p_ref[0]
        for dev in range(1, N_DEV):
            g = g + p_ref[dev]
        g_ref[...] = g
        d_ref[...], nm_ref[...], nv_ref[...] = _adamw_math(w_ref[...], g, m_ref[...], v_ref[...])

    return pl.pallas_call(
        body, name="adamw_small", grid=(1,),
        in_specs=[_const(w.shape), _const(parts.shape), _const(w.shape), _const(w.shape)],
        out_specs=[_const(w.shape)] * 4, out_shape=[jax.ShapeDtypeStruct(w.shape, F32)] * 4,
        compiler_params=_cp(("arbitrary",)),
    )(w, parts, m, v)


REPLICATED = (("norm_mix_pre", 1024), ("norm_mix_post", 1024), ("norm_ffn_pre", 1024), ("norm_ffn_post", 1024),
              ("rel_bias", 256), ("sinks", 8), ("rwkv_shift_mix", 1792), ("w0", 512), ("a0", 512), ("k_k", 512),
              ("k_a", 512), ("r_k", 512), ("ln_x_g", 512), ("ln_x_b", 512), ("conv_b", 8192))
SMALL_SHARDED = (("w_decay_up", LORA_DECAY, D_RWKV), ("w_iclr_up", LORA_ICLR, D_RWKV),
                 ("w_gate_up", LORA_GATE, D_RWKV), ("conv_w", 3, 2 * D_FF))
BIG = (("w_in", D_MODEL, 640), ("w_out", 256, D_MODEL), ("w_ffn_up", D_MODEL, 2048), ("w_ffn_down", 1024, D_MODEL))
PACK_ALIGN = 8 * LANES


def _pack(pieces):
    flat = []
    for t in pieces:
        t = t.reshape(-1)
        pad = (-t.shape[0]) % LANES
        flat.append(jnp.pad(t, (0, pad)) if pad else t)
    flat = jnp.concatenate(flat)
    pad = (-flat.shape[0]) % PACK_ALIGN
    return jnp.pad(flat, (0, pad)).reshape(-1, LANES)


def _unpack(buf, sizes):
    flat, out, off = buf.reshape(-1), [], 0
    for n in sizes:
        out.append(flat[off:off + n])
        off += n + ((-n) % LANES)
    return out


def kernel(x, norm_mix_pre, norm_mix_post, norm_ffn_pre, norm_ffn_post, w_in, rel_bias, sinks, rwkv_shift_mix, w0, w_decay_up, a0, w_iclr_up, w_gate_up, k_k, k_a, r_k, ln_x_g, ln_x_b, w_out, w_ffn_up, conv_w, conv_b, w_ffn_down, loss_target, m_norm_mix_pre, m_norm_mix_post, m_norm_ffn_pre, m_norm_ffn_post, m_w_in, m_rel_bias, m_sinks, m_rwkv_shift_mix, m_w0, m_w_decay_up, m_a0, m_w_iclr_up, m_w_gate_up, m_k_k, m_k_a, m_r_k, m_ln_x_g, m_ln_x_b, m_w_out, m_w_ffn_up, m_conv_w, m_conv_b, m_w_ffn_down, v_norm_mix_pre, v_norm_mix_post, v_norm_ffn_pre, v_norm_ffn_post, v_w_in, v_rel_bias, v_sinks, v_rwkv_shift_mix, v_w0, v_w_decay_up, v_a0, v_w_iclr_up, v_w_gate_up, v_k_k, v_k_a, v_r_k, v_ln_x_g, v_ln_x_b, v_w_out, v_w_ffn_up, v_conv_w, v_conv_b, v_w_ffn_down):
    given = dict(locals())
    names = [n for n, _ in REPLICATED] + [n for n, _, _ in SMALL_SHARDED] + [n for n, _, _ in BIG]
    order = ["norm_mix_pre", "norm_mix_post", "norm_ffn_pre", "norm_ffn_post", "w_in", "rel_bias", "sinks",
             "rwkv_shift_mix", "w0", "w_decay_up", "a0", "w_iclr_up", "w_gate_up", "k_k", "k_a", "r_k", "ln_x_g",
             "ln_x_b", "w_out", "w_ffn_up", "conv_w", "conv_b", "w_ffn_down"]
    assert sorted(names) == sorted(order)
    shard = 2 * lax.axis_index("x") + lax.axis_index("y")

    big_sh = {n: given[n].reshape(a, b).astype(BF16) for n, a, b in BIG}
    small_sh = [given[n].reshape(r, c // N_CHIPS) for n, r, c in SMALL_SHARDED]
    gathered = _gather_weights([big_sh["w_in"]], small_sh)
    rest = ("w_out", "w_ffn_up", "w_ffn_down")
    win_st, rest_sh = lax.optimization_barrier((gathered[0], [big_sh[n] for n in rest]))
    sm = {n: given[n] for n, _ in REPLICATED}
    sm["r_k"] = r_k.reshape(N_Q_HEADS, HEAD_DIM)
    for (n, r, c), st in zip(SMALL_SHARDED, gathered[1:]):
        sm[n] = st.transpose(1, 0, 2).reshape(r, c)

    lands = [lax.dynamic_update_slice(lax.empty((N_CHIPS,) + t.shape, BF16), t[None], (shard, 0, 0)) for t in rest_sh]
    plan_w = _plan_gather(len(rest))
    w_sems, w_bufs, token = _copies_start("gather_rest_start", rest_sh + lands, plan_w, 9)
    sm["norm_mix_pre"] = norm_mix_pre + token[0:1, 0:1]

    def on_rest_weights(after):
        out = _copies_wait("gather_rest_wait", w_sems, w_bufs, plan_w, 9, after)
        wout_st, wup_st, wdown_st = out[3:]
        return wout_st.reshape(D_MODEL, D_MODEL), wup_st, wdown_st.reshape(D_FF, D_MODEL)

    red_a = _Reduction("a", (1024, D_MODEL, 256))
    red_b = _Reduction("b", (D_MODEL,))

    def on_grads_a(gs):
        d_wdown, d_wup, d_wout = gs
        return red_a.start([d_wdown.reshape(N_CHIPS, 1024, D_MODEL), d_wup, d_wout.reshape(N_CHIPS, 256, D_MODEL)])

    handlers = {"rest_weights": on_rest_weights, "grads_a": on_grads_a, "seam_1": red_a.after_pair,
                "seam_2": red_a.after_chips, "grads_b": lambda g: red_b.start([g])}
    steps = _local_step(x[0], loss_target[0], sm, win_st)
    kind, payload = next(steps)
    while True:
        try:
            kind, payload = steps.send(handlers[kind](payload))
        except StopIteration as done:
            loss, grad_x, grads = done.value
            break
    loss = lax.psum(loss[0, 0], ("x", "y", "c"))

    small_names = [n for n, _ in REPLICATED] + [n for n, _, _ in SMALL_SHARDED]

    def shard_cols(t, s):
        return t[:, s * (t.shape[1] // N_CHIPS):(s + 1) * (t.shape[1] // N_CHIPS)]

    for_chip = jnp.stack([_pack([grads[n] for n, _ in REPLICATED]
                                + [shard_cols(grads[n], s) for n, _, _ in SMALL_SHARDED]) for s in range(N_CHIPS)])
    me = 2 * shard + lax.axis_index("c")
    mine = lax.dynamic_index_in_dim(for_chip, shard, 0, keepdims=True)
    land = lax.dynamic_update_slice(lax.empty((N_DEV,) + for_chip.shape[1:], F32), mine, (me, 0, 0))

    def plan_small(refs):
        x, y, c, _ = _place()
        out = []
        for rel in range(1, N_DEV):
            px, py, pc = x ^ (rel >> 2), y ^ ((rel >> 1) & 1), c ^ (rel & 1)
            out.append((refs[0].at[2 * px + py], refs[1].at[4 * x + 2 * y + c], (px, py, pc)))
        return out

    s_sems, s_bufs, _ = _copies_start("grad_small_start", [for_chip, land], plan_small, N_DEV - 1)

    red_b.after_pair(grad_x)
    g_out = {}
    g_out["w_ffn_down"], g_out["w_ffn_up"], g_out["w_out"] = red_a.finish(grad_x)

    delta, new_m, new_v = {}, {}, {}
    for n, a, b in reversed(BIG):
        if n == "w_out":
            red_b.after_chips(delta["w_ffn_up"])
        if n == "w_in":
            parts = _copies_wait("grad_small_wait", s_sems, s_bufs, plan_small, N_DEV - 1, delta["w_out"])[1]
            packs = [_pack([given[pre + n2] for n2 in small_names]) for pre in ("", "m_", "v_")]
            small_sizes = [int(np.prod(given[n2].shape)) for n2 in small_names]
            upd = [_unpack(t, small_sizes) for t in _adamw_small(packs[0], parts, packs[1], packs[2])]
            for n2, d, nm, nv, g in zip(small_names, *upd):
                shape = given[n2].shape
                delta[n2], new_m[n2], new_v[n2], g_out[n2] = (t.reshape(shape) for t in (d, nm, nv, g))
            g_out[n], = red_b.finish(delta["w_out"])
        d, nm, nv = _adamw(given[n].reshape(a, b), g_out[n], given["m_" + n].reshape(a, b),
                           given["v_" + n].reshape(a, b), "adamw_" + n, 128)
        delta[n], new_m[n], new_v[n] = d, nm, nv

    def shaped(d):
        return [d[n].reshape(given[n].shape) for n in order]

    return (loss, grad_x.reshape(x.shape), *shaped(g_out), *shaped(delta), *shaped(new_m), *shaped(new_v))
```

```python
import math

import numpy as np
import jax
import jax.numpy as jnp
from jax import lax
from jax.experimental import pallas as pl
from jax.experimental.pallas import tpu as pltpu

F32 = jnp.float32
BF16 = jnp.bfloat16
MESH = pl.DeviceIdType.MESH

SEQ = 2048
D_MODEL = 1024
HEAD_DIM = 64
D_ATTN = 512
D_RWKV = 512
D_KV = 128
N_Q_HEADS = 8
N_KV_HEADS = 2
Q_PER_KV = 4
BLOCK = 128
N_BUCKETS = 32
MAX_DISTANCE = 128
LORA_DECAY = 64
LORA_ICLR = 64
LORA_GATE = 128
RWKV_COLS = 3 * D_RWKV + LORA_DECAY + LORA_ICLR + LORA_GATE
P_OFF = D_ATTN + 2 * D_KV
D_IN = P_OFF + RWKV_COLS
D_FF = 4096
NORM_EPS = 1e-6
GN_EPS = 64e-5
NEG_INF = -1e30
N_CHIPS = 4
N_DEV = 8

ADAM_LR = 0.001
ADAM_B1 = 0.9
ADAM_B2 = 0.999
ADAM_EPS = 1e-08
ADAM_WD = 0.01
ADAM_STEP = 10

VMEM_LIMIT = 52 * 1024 * 1024
LANES = 128


def _cp(sem=None, vmem=VMEM_LIMIT):
    kw = dict(vmem_limit_bytes=vmem)
    if sem is not None:
        kw["dimension_semantics"] = sem
    return pltpu.CompilerParams(**kw)


def _rows(tr, nc):
    return pl.BlockSpec((tr, nc), lambda i: (i, 0))


def _const(shape):
    return pl.BlockSpec(shape, lambda *_: (0,) * len(shape))


ANY = pl.BlockSpec(memory_space=pl.ANY)


def _split(x, n):
    parts = []
    for _ in range(n - 1):
        h = x.astype(BF16)
        parts.append(h)
        x = x - h.astype(F32)
    parts.append(x.astype(BF16))
    return parts


def _dot(a, b, dn=(((1,), (0,)), ((), ()))):
    return lax.dot_general(a, b, dn, preferred_element_type=F32)


NN = (((1,), (0,)), ((), ()))
NT = (((1,), (1,)), ((), ()))
TN = (((0,), (0,)), ((), ()))


def _dot_ind(x, ind_bf16, n=3):
    acc = None
    for part in _split(x, n):
        t = _dot(part, ind_bf16)
        acc = t if acc is None else acc + t
    return acc


def _head_ones(n):
    r = lax.broadcasted_iota(jnp.int32, (n, n), 0) >> 6
    c = lax.broadcasted_iota(jnp.int32, (n, n), 1) >> 6
    return jnp.where(r == c, 1.0, 0.0).astype(BF16)


def _matmul(a, b, mode, name, *, m, n, k, tm, tn, a_spec=None, b_spec=None, out=None, out_dtype=F32):
    keep_at = mode == "tn" and m == tm and n > tn

    def body(a_ref, b_ref, o_ref, *scratch):
        if keep_at:
            at_ref, = scratch

            @pl.when(pl.program_id(1) == 0)
            def _():
                at_ref[...] = a_ref[...].T

            o_ref[...] = _dot(at_ref[...], b_ref[...], NN).astype(out_dtype)
        else:
            o_ref[...] = _dot(a_ref[...], b_ref[...], {"nn": NN, "nt": NT, "tn": TN}[mode]).astype(out_dtype)

    if a_spec is None:
        a_spec = pl.BlockSpec((k, tm), lambda i, j: (0, i)) if mode == "tn" else pl.BlockSpec((tm, k), lambda i, j: (i, 0))
    if b_spec is None:
        b_spec = pl.BlockSpec((tn, k), lambda i, j: (j, 0)) if mode == "nt" else pl.BlockSpec((k, tn), lambda i, j: (0, j))
    return pl.pallas_call(
        body, name=name, grid=(m // tm, n // tn),
        in_specs=[a_spec, b_spec],
        out_specs=pl.BlockSpec((tm, tn), lambda i, j: (i, j)) if out is None else out[1],
        out_shape=jax.ShapeDtypeStruct((m, n) if out is None else out[0], out_dtype),
        scratch_shapes=[pltpu.VMEM((tm, k), a.dtype)] if keep_at else [],
        compiler_params=_cp(("parallel", "arbitrary" if keep_at else "parallel")),
    )(a, b)


def _matmul_nt_shards(a, b_st, name, *, m, n, tm, tn, a_spec, a_piece):
    ks = b_st.shape[2]

    def body(a_ref, b_ref, o_ref):
        acc = _dot(a_piece(a_ref, 0), b_ref[0], NT)
        for s in range(1, N_CHIPS):
            acc = acc + _dot(a_piece(a_ref, s), b_ref[s], NT)
        o_ref[...] = acc

    return pl.pallas_call(
        body, name=name, grid=(m // tm, n // tn),
        in_specs=[a_spec, pl.BlockSpec((N_CHIPS, tn, ks), lambda i, j: (0, j, 0))],
        out_specs=pl.BlockSpec((tm, tn), lambda i, j: (i, j)),
        out_shape=jax.ShapeDtypeStruct((m, n), F32),
        compiler_params=_cp(("parallel", "parallel")),
    )(a, b_st)


def _rstd(x):
    return lax.rsqrt(jnp.mean(x * x, axis=-1, keepdims=True) + NORM_EPS)


def _rms_bwd(x, r, g, dy):
    gy = dy * g
    return r * gy - x * ((r * r * r) * (jnp.sum(x * gy, axis=-1, keepdims=True) / x.shape[-1]))


TR = 256


def _norm_cast(x, g, name):
    def body(x_ref, g_ref, h_ref):
        x = x_ref[...]
        h_ref[...] = (x * _rstd(x) * g_ref[...]).astype(BF16)

    return pl.pallas_call(
        body, name=name, grid=(SEQ // TR,),
        in_specs=[_rows(TR, D_MODEL), _const((1, D_MODEL))],
        out_specs=_rows(TR, D_MODEL),
        out_shape=jax.ShapeDtypeStruct((SEQ, D_MODEL), BF16),
        compiler_params=_cp(("parallel",)),
    )(x, g)


def _mix_norm(x, mix, g2, g3):
    def body(x_ref, mix_ref, g2_ref, g3_ref, x2_ref, h3_ref):
        mixv = mix_ref[...]
        x2 = x_ref[...] + mixv * _rstd(mixv) * g2_ref[...]
        x2_ref[...] = x2
        h3_ref[...] = (x2 * _rstd(x2) * g3_ref[...]).astype(BF16)

    return pl.pallas_call(
        body, name="mix_norm", grid=(SEQ // TR,),
        in_specs=[_rows(TR, D_MODEL), _rows(TR, D_MODEL), _const((1, D_MODEL)), _const((1, D_MODEL))],
        out_specs=[_rows(TR, D_MODEL), _rows(TR, D_MODEL)],
        out_shape=[jax.ShapeDtypeStruct((SEQ, D_MODEL), F32), jax.ShapeDtypeStruct((SEQ, D_MODEL), BF16)],
        compiler_params=_cp(("parallel",)),
    )(x, mix, g2, g3)


def _loss_head(x2, f, g4, target):
    def body(x2_ref, f_ref, g4_ref, t_ref, loss_ref, dy_ref, df_ref, dg_ref):
        i = pl.program_id(0)
        f = f_ref[...]
        g4 = g4_ref[...]
        r = _rstd(f)
        e = x2_ref[...] + f * r * g4 - t_ref[...]
        dy = e * (1.0 / D_MODEL)
        dy_ref[...] = dy
        df_ref[...] = _rms_bwd(f, r, g4, dy).astype(BF16)
        part = 0.5 * jnp.sum(jnp.sum(e * e, axis=-1, keepdims=True), axis=0, keepdims=True) * (1.0 / D_MODEL)
        dg = jnp.sum(dy * f * r, axis=0, keepdims=True)

        @pl.when(i == 0)
        def _():
            loss_ref[...] = jnp.zeros_like(loss_ref)
            dg_ref[...] = jnp.zeros_like(dg_ref)

        loss_ref[...] += jnp.broadcast_to(part, loss_ref.shape)
        dg_ref[...] += dg

    return pl.pallas_call(
        body, name="loss_head", grid=(SEQ // TR,),
        in_specs=[_rows(TR, D_MODEL), _rows(TR, D_MODEL), _const((1, D_MODEL)), _rows(TR, D_MODEL)],
        out_specs=[_const((8, LANES)), _rows(TR, D_MODEL), _rows(TR, D_MODEL), _const((1, D_MODEL))],
        out_shape=[jax.ShapeDtypeStruct((8, LANES), F32), jax.ShapeDtypeStruct((SEQ, D_MODEL), F32),
                   jax.ShapeDtypeStruct((SEQ, D_MODEL), BF16), jax.ShapeDtypeStruct((1, D_MODEL), F32)],
        compiler_params=_cp(("arbitrary",)),
    )(x2, f, g4, target)


def _mid_bwd(x2, mix, dy, dh3, g2, g3):
    def body(x2_ref, mix_ref, dy_ref, dh3_ref, g2_ref, g3_ref, dx2_ref, dmix_ref, dg2_ref, dg3_ref):
        i = pl.program_id(0)
        x2 = x2_ref[...]
        mixv = mix_ref[...]
        dh3 = dh3_ref[...]
        r3 = _rstd(x2)
        dx2 = dy_ref[...] + _rms_bwd(x2, r3, g3_ref[...], dh3)
        dx2_ref[...] = dx2
        r2 = _rstd(mixv)
        dmix_ref[...] = _rms_bwd(mixv, r2, g2_ref[...], dx2).astype(BF16)

        @pl.when(i == 0)
        def _():
            dg2_ref[...] = jnp.zeros_like(dg2_ref)
            dg3_ref[...] = jnp.zeros_like(dg3_ref)

        dg3_ref[...] += jnp.sum(dh3 * x2 * r3, axis=0, keepdims=True)
        dg2_ref[...] += jnp.sum(dx2 * mixv * r2, axis=0, keepdims=True)

    return pl.pallas_call(
        body, name="mid_bwd", grid=(SEQ // TR,),
        in_specs=[_rows(TR, D_MODEL)] * 4 + [_const((1, D_MODEL))] * 2,
        out_specs=[_rows(TR, D_MODEL), _rows(TR, D_MODEL), _const((1, D_MODEL)), _const((1, D_MODEL))],
        out_shape=[jax.ShapeDtypeStruct((SEQ, D_MODEL), F32), jax.ShapeDtypeStruct((SEQ, D_MODEL), BF16),
                   jax.ShapeDtypeStruct((1, D_MODEL), F32), jax.ShapeDtypeStruct((1, D_MODEL), F32)],
        compiler_params=_cp(("arbitrary",)),
    )(x2, mix, dy, dh3, g2, g3)


def _first_bwd(x, dx2, dh1, g1):
    def body(x_ref, dx2_ref, dh1_ref, g1_ref, dx_ref, dg1_ref):
        i = pl.program_id(0)
        x = x_ref[...]
        dh1 = dh1_ref[...]
        r = _rstd(x)
        dx_ref[...] = dx2_ref[...] + _rms_bwd(x, r, g1_ref[...], dh1)

        @pl.when(i == 0)
        def _():
            dg1_ref[...] = jnp.zeros_like(dg1_ref)

        dg1_ref[...] += jnp.sum(dh1 * x * r, axis=0, keepdims=True)

    return pl.pallas_call(
        body, name="first_bwd", grid=(SEQ // TR,),
        in_specs=[_rows(TR, D_MODEL)] * 3 + [_const((1, D_MODEL))],
        out_specs=[_rows(TR, D_MODEL), _const((1, D_MODEL))],
        out_shape=[jax.ShapeDtypeStruct((SEQ, D_MODEL), F32), jax.ShapeDtypeStruct((1, D_MODEL), F32)],
        compiler_params=_cp(("arbitrary",)),
    )(x, dx2, dh1, g1)


TC = 256
N_CB = D_FF // TC
GELU_C = math.sqrt(2.0 / math.pi)


def _shift_down(u, s):
    rolled = pltpu.roll(u, s, 0)
    row = lax.broadcasted_iota(jnp.int32, u.shape, 0)
    return jnp.where(row >= s, rolled, 0.0)


def _shift_up(u, s):
    n = u.shape[0]
    rolled = pltpu.roll(u, n - s, 0)
    row = lax.broadcasted_iota(jnp.int32, u.shape, 0)
    return jnp.where(row < n - s, rolled, 0.0)


def _conv3(u, w, b):
    return b + w[0:1] * _shift_down(u, 2) + w[1:2] * _shift_down(u, 1) + w[2:3] * u


def _gelu_and_grad(x):
    inner = GELU_C * (x + 0.044715 * (x * x * x))
    t = jnp.tanh(inner)
    gelu = 0.5 * x * (1.0 + t)
    dgelu = 0.5 * (1.0 + t) + 0.5 * x * (1.0 - t * t) * (GELU_C * (1.0 + 3 * 0.044715 * (x * x)))
    return gelu, dgelu


def _ffn_specs():
    col = lambda off: pl.BlockSpec((SEQ, TC), lambda *g: (0, g[-1] + off))
    w = lambda off: pl.BlockSpec((3, TC), lambda *g: (0, g[-1] + off))
    b = lambda off: pl.BlockSpec((1, TC), lambda *g: (0, g[-1] + off))
    return col, w, b


def _ffn_up_act(h3, wup_st, conv_w, conv_b):
    col, w, b = _ffn_specs()
    per_shard = wup_st.shape[2] // TC

    def body(h_ref, upg_ref, upv_ref, wg_ref, wv_ref, bg_ref, bv_ref, ug_ref, uv_ref, act_ref):
        h = h_ref[...]
        ug = _dot(h, upg_ref[...])
        uv = _dot(h, upv_ref[...])
        ug_ref[...] = ug
        uv_ref[...] = uv
        gate = _conv3(ug, wg_ref[...], bg_ref[...])
        val = _conv3(uv, wv_ref[...], bv_ref[...])
        act_ref[...] = (_gelu_and_grad(gate)[0] * val).astype(BF16)

    return pl.pallas_call(
        body, name="ffn_up_act", grid=(N_CB,),
        in_specs=[_const((SEQ, D_MODEL)),
                  pl.BlockSpec((None, D_MODEL, TC), lambda j: (j // per_shard, 0, j % per_shard)),
                  pl.BlockSpec((None, D_MODEL, TC), lambda j: (2 + j // per_shard, 0, j % per_shard)),
                  w(0), w(N_CB), b(0), b(N_CB)],
        out_specs=[col(0)] * 3,
        out_shape=[jax.ShapeDtypeStruct((SEQ, D_FF), F32)] * 2 + [jax.ShapeDtypeStruct((SEQ, D_FF), BF16)],
        compiler_params=_cp(("parallel",)),
    )(h3, wup_st, wup_st, conv_w, conv_w, conv_b, conv_b)


def _ffn_act_bwd(u_gate, u_val, df, wdown, conv_w, conv_b):
    col, w, b = _ffn_specs()
    both = lambda rows: pl.BlockSpec((2, rows, TC), lambda j: (0, 0, j))

    def body(ug_ref, uv_ref, df_ref, wd_ref, wg_ref, wv_ref, bg_ref, bv_ref, du_ref, dw_ref, db_ref):
        ug, uv = ug_ref[...], uv_ref[...]
        wg, wv = wg_ref[...], wv_ref[...]
        gate = _conv3(ug, wg, bg_ref[...])
        val = _conv3(uv, wv, bv_ref[...])
        gelu, dgelu = _gelu_and_grad(gate)
        da = _dot(df_ref[...], wd_ref[...], NT)
        for h, (duc, uh, wh) in enumerate(((da * val * dgelu, ug, wg), (da * gelu, uv, wv))):
            up1, up2 = _shift_up(duc, 1), _shift_up(duc, 2)
            du_ref[h] = (wh[2:3] * duc + wh[1:2] * up1 + wh[0:1] * up2).astype(BF16)
            db_ref[h] = jnp.sum(duc, axis=0, keepdims=True)
            dw_ref[h] = jnp.concatenate(
                [jnp.sum(up2 * uh, axis=0, keepdims=True), jnp.sum(up1 * uh, axis=0, keepdims=True),
                 jnp.sum(duc * uh, axis=0, keepdims=True)], axis=0)

    return pl.pallas_call(
        body, name="ffn_act_bwd", grid=(N_CB,),
        in_specs=[col(0), col(0), _const((SEQ, D_MODEL)), pl.BlockSpec((TC, D_MODEL), lambda j: (j, 0)),
                  w(0), w(N_CB), b(0), b(N_CB)],
        out_specs=[both(SEQ), both(3), both(1)],
        out_shape=[jax.ShapeDtypeStruct((2, SEQ, D_FF), BF16), jax.ShapeDtypeStruct((2, 3, D_FF), F32),
                   jax.ShapeDtypeStruct((2, 1, D_FF), F32)],
        compiler_params=_cp(("parallel",)),
    )(u_gate, u_val, df, wdown, conv_w, conv_w, conv_b, conv_b)


def _t5_onehot():
    rel = (np.arange(BLOCK)[:, None] + BLOCK) - np.arange(2 * BLOCK)[None, :]
    n = np.maximum(rel, 0)
    max_exact = N_BUCKETS // 2
    large = max_exact + (np.log(np.maximum(n, 1).astype(np.float32) / np.float32(max_exact))
                         / np.float32(math.log(MAX_DISTANCE / max_exact))
                         * np.float32(N_BUCKETS - max_exact)).astype(np.int32)
    large = np.minimum(large, N_BUCKETS - 1)
    bucket = np.where(n < max_exact, n, large).reshape(-1)
    return (bucket[None, :] == np.arange(N_BUCKETS)[:, None]).astype(np.float32)


N_REL = BLOCK * 2 * BLOCK


def _bias_table(rel_bias_t, onehot):
    def body(rb_ref, oh_ref, o_ref):
        o_ref[...] = _dot_ind(rb_ref[...], oh_ref[...])

    return pl.pallas_call(
        body, name="bias_table", grid=(1,),
        in_specs=[_const((N_Q_HEADS, N_BUCKETS)), _const((N_BUCKETS, N_REL))],
        out_specs=_const((N_Q_HEADS, N_REL)),
        out_shape=jax.ShapeDtypeStruct((N_Q_HEADS, N_REL), F32),
        compiler_params=_cp(("arbitrary",)),
    )(rel_bias_t, onehot)


def _bias_table_bwd(dbias, onehot):
    def body(db_ref, oh_ref, o_ref):
        acc = None
        for part in _split(db_ref[...], 3):
            t = _dot(part, oh_ref[...], NT)
            acc = t if acc is None else acc + t
        o_ref[...] = acc

    return pl.pallas_call(
        body, name="bias_table_bwd", grid=(1,),
        in_specs=[_const((N_Q_HEADS, N_REL)), _const((N_BUCKETS, N_REL))],
        out_specs=_const((N_Q_HEADS, N_BUCKETS)),
        out_shape=jax.ShapeDtypeStruct((N_Q_HEADS, N_BUCKETS), F32),
        compiler_params=_cp(("arbitrary",)),
    )(dbias, onehot)


def _attn_pieces(n, q, kvp, kvc, bias_ref, sinks_ref, hk):
    qi = lax.broadcasted_iota(jnp.int32, (BLOCK, 2 * BLOCK), 0)
    kj = lax.broadcasted_iota(jnp.int32, (BLOCK, 2 * BLOCK), 1)
    rel = qi + BLOCK - kj
    first_key = jnp.where(n > 0, 0, BLOCK)
    ok = jnp.where(rel >= 0, jnp.where(rel < BLOCK, jnp.where(kj >= first_key, 1.0, 0.0), 0.0), 0.0)
    ok4 = jnp.concatenate([ok] * Q_PER_KV, axis=0) > 0.5
    c0 = hk * HEAD_DIM
    kcat = jnp.concatenate([kvp[:, c0:c0 + HEAD_DIM], kvc[:, c0:c0 + HEAD_DIM]], axis=0).astype(BF16)
    vcat = jnp.concatenate([kvp[:, D_KV + c0:D_KV + c0 + HEAD_DIM], kvc[:, D_KV + c0:D_KV + c0 + HEAD_DIM]],
                           axis=0).astype(BF16)
    q0 = hk * Q_PER_KV * HEAD_DIM
    qs = jnp.concatenate([q[:, q0 + g * HEAD_DIM:q0 + (g + 1) * HEAD_DIM] for g in range(Q_PER_KV)],
                         axis=0).astype(BF16)
    s = _dot(qs, kcat, NT) * (HEAD_DIM ** -0.5) + bias_ref[hk]
    s = jnp.where(ok4, s, NEG_INF)
    row = lax.broadcasted_iota(jnp.int32, (Q_PER_KV * BLOCK, 1), 0)
    sink = jnp.zeros((Q_PER_KV * BLOCK, 1), F32)
    for g in range(Q_PER_KV):
        sink = jnp.where((row >> 7) == g, sinks_ref[hk * Q_PER_KV + g], sink)
    m = jnp.maximum(jnp.max(s, axis=-1, keepdims=True), sink)
    p = jnp.exp(s - m)
    es = jnp.exp(sink - m)
    inv = 1.0 / (jnp.sum(p, axis=-1, keepdims=True) + es)
    return qs, kcat, vcat, p * inv, es * inv


def _attn_in_specs():
    return [pl.BlockSpec((BLOCK, D_ATTN), lambda n: (n, 0)),
            pl.BlockSpec((BLOCK, 2 * D_KV), lambda n: (jnp.maximum(n - 1, 0), D_ATTN // (2 * D_KV))),
            pl.BlockSpec((BLOCK, 2 * D_KV), lambda n: (n, D_ATTN // (2 * D_KV))),
            _const((N_KV_HEADS, Q_PER_KV * BLOCK, 2 * BLOCK)),
            pl.BlockSpec(memory_space=pltpu.SMEM)]


def _unstack_heads(t):
    return jnp.concatenate([t[g * BLOCK:(g + 1) * BLOCK] for g in range(Q_PER_KV)], axis=1)


def _attn_fwd(proj, bias, sinks):
    def body(q_ref, kvp_ref, kvc_ref, bias_ref, sinks_ref, o_ref):
        n = pl.program_id(0)
        q, kvp, kvc = q_ref[...], kvp_ref[...], kvc_ref[...]
        outs = []
        for hk in range(N_KV_HEADS):
            _, _, vcat, probs, _ = _attn_pieces(n, q, kvp, kvc, bias_ref, sinks_ref, hk)
            outs.append(_unstack_heads(_dot(probs.astype(BF16), vcat)))
        o_ref[...] = jnp.concatenate(outs, axis=1)

    return pl.pallas_call(
        body, name="attn_fwd", grid=(SEQ // BLOCK,),
        in_specs=_attn_in_specs(),
        out_specs=pl.BlockSpec((BLOCK, D_ATTN), lambda n: (n, 0)),
        out_shape=jax.ShapeDtypeStruct((SEQ, D_ATTN), F32),
        compiler_params=_cp(("parallel",)),
    )(proj, proj, proj, bias, sinks)


def _attn_bwd(proj, bias, sinks, dcat):
    nb = SEQ // BLOCK

    def body(q_ref, kvp_ref, kvc_ref, bias_ref, sinks_ref, do_ref, dq_ref, dkv_ref, dbias_ref, dsink_ref, dsacc):
        n = pl.program_id(0)

        @pl.when(n == 0)
        def _():
            dkv_ref[...] = jnp.zeros_like(dkv_ref)
            dbias_ref[...] = jnp.zeros_like(dbias_ref)
            dsacc[...] = jnp.zeros_like(dsacc)

        q, kvp, kvc = q_ref[...], kvp_ref[...], kvc_ref[...]
        do_all = do_ref[...]
        dqs, dks, dvs = [], [], []
        for hk in range(N_KV_HEADS):
            qs, kcat, vcat, probs, psink = _attn_pieces(n, q, kvp, kvc, bias_ref, sinks_ref, hk)
            q0 = hk * Q_PER_KV * HEAD_DIM
            do = jnp.concatenate([do_all[:, q0 + g * HEAD_DIM:q0 + (g + 1) * HEAD_DIM] for g in range(Q_PER_KV)],
                                 axis=0).astype(BF16)
            dprobs = _dot(do, vcat, NT)
            dvs.append(_dot(probs.astype(BF16), do, TN))
            rowdot = jnp.sum(probs * dprobs, axis=-1, keepdims=True)
            ds = probs * (dprobs - rowdot)
            dsacc[hk] += -psink * rowdot
            dbias_ref[hk] += ds
            dsb = (ds * (HEAD_DIM ** -0.5)).astype(BF16)
            dqs.append(_unstack_heads(_dot(dsb, kcat)))
            dks.append(_dot(dsb, qs, TN))
        dq_ref[...] = jnp.concatenate(dqs, axis=1)
        upd = jnp.concatenate(dks + dvs, axis=1)
        cur = pl.multiple_of(n * BLOCK, BLOCK)
        dkv_ref[pl.ds(cur, BLOCK), :] += upd[BLOCK:]

        @pl.when(n > 0)
        def _():
            prev = pl.multiple_of((n - 1) * BLOCK, BLOCK)
            dkv_ref[pl.ds(prev, BLOCK), :] += upd[:BLOCK]

        @pl.when(n == nb - 1)
        def _():
            for hk in range(N_KV_HEADS):
                for g in range(Q_PER_KV):
                    tot = jnp.sum(dsacc[hk, g * BLOCK:(g + 1) * BLOCK, :], axis=0, keepdims=True)
                    h = hk * Q_PER_KV + g
                    dsink_ref[h:h + 1, :] = jnp.broadcast_to(tot, (1, LANES))

    return pl.pallas_call(
        body, name="attn_bwd", grid=(nb,),
        in_specs=_attn_in_specs() + [pl.BlockSpec((BLOCK, D_ATTN), lambda n: (n, 0))],
        out_specs=[pl.BlockSpec((BLOCK, D_ATTN), lambda n: (n, 0)), _const((SEQ, 2 * D_KV)),
                   _const((N_KV_HEADS, Q_PER_KV * BLOCK, 2 * BLOCK)), _const((N_Q_HEADS, LANES))],
        out_shape=[jax.ShapeDtypeStruct((SEQ, D_ATTN), F32), jax.ShapeDtypeStruct((SEQ, 2 * D_KV), F32),
                   jax.ShapeDtypeStruct((N_KV_HEADS, Q_PER_KV * BLOCK, 2 * BLOCK), F32),
                   jax.ShapeDtypeStruct((N_Q_HEADS, LANES), F32)],
        scratch_shapes=[pltpu.VMEM((N_KV_HEADS, Q_PER_KV * BLOCK, 1), F32)],
        compiler_params=_cp(("arbitrary",)),
    )(proj, proj, proj, bias, sinks, dcat)


@jax.custom_vjp
def _head_sum(x):
    ones = _head_ones(LANES)
    return jnp.concatenate([_dot_ind(x[:, c:c + LANES], ones, 2) for c in range(0, x.shape[-1], LANES)], axis=1)


_head_sum.defvjp(lambda x: (_head_sum(x), None), lambda _, ct: (_head_sum(ct),))


@jax.custom_vjp
def _bdot(a, w):
    return _dot(a.astype(BF16), w.astype(BF16))


def _bdot_bwd(res, ct):
    a, w = res
    ctb = ct.astype(BF16)
    return _dot(ctb, w.astype(BF16), NT), _dot(a.astype(BF16), ctb, TN)


_bdot.defvjp(lambda a, w: (_bdot(a, w), (a, w)), _bdot_bwd)


def _sigmoid(x):
    return 0.5 * (jnp.tanh(0.5 * x) + 1.0)


def _softplus(x):
    return jnp.maximum(x, 0.0) + jnp.log(1.0 + jnp.exp(-jnp.abs(x)))


def _rwkv_core(r, k, v, zwa, zg, w0, wdu, a0, wiu, wgu, k_k, k_a):
    w_log = -_softplus(-(w0 + _bdot(jnp.tanh(zwa), wdu))) - 0.5
    decay = jnp.exp(-jnp.exp(w_log))
    a = _sigmoid(a0 + _bdot(zwa, wiu))
    g = _bdot(_sigmoid(zg), wgu)
    kk = k * k_k
    kk = kk / jnp.maximum(jnp.sqrt(_head_sum(kk * kk)), 1e-12)
    k2 = k * (1.0 + (a - 1.0) * k_a)
    return r, decay, k2, v, -kk, kk * a, g


def _rwkv_out(o, r, k2, v, g, lng, lnb, rk):
    mu = _head_sum(o) * (1.0 / HEAD_DIM)
    d = o - mu
    var = _head_sum(d * d) * (1.0 / HEAD_DIM)
    on = d * lax.rsqrt(var + GN_EPS) * lng + lnb
    bonus = _head_sum(r * k2 * rk) * v
    return (on + bonus) * g


P_SPLITS = (0, 512, 1024, 1536, 1664, 1792)
N_PREP_PARAMS = 7
HALO = 8


def _shifted_pieces(i, p_ref, halo_ref, mix_ref):
    p = p_ref[:, P_OFF:]
    prev_row = halo_ref[HALO - 1:HALO, P_OFF:] * jnp.where(i > 0, 1.0, 0.0)
    row = lax.broadcasted_iota(jnp.int32, p.shape, 0)
    pprev = jnp.where(row == 0, prev_row, pltpu.roll(p, 1, 0))
    delta = pprev - p
    ps = p + delta * mix_ref[...]
    return [ps[:, a:b] for a, b in zip(P_SPLITS[:-1], P_SPLITS[1:])], delta


def _prep_in_specs():
    return [_rows(TR, D_IN),
            pl.BlockSpec((HALO, D_IN), lambda i: (jnp.maximum(i * (TR // HALO) - 1, 0), 0)),
            _const((1, RWKV_COLS)), _const((1, D_RWKV)), _const((LANES, D_RWKV)), _const((1, D_RWKV)),
            _const((LANES, D_RWKV)), _const((LANES, D_RWKV)), _const((1, D_RWKV)), _const((1, D_RWKV))]


def _rwkv_prep(proj, mix, prm):
    def body(p_ref, halo_ref, mix_ref, *refs):
        prm_refs, outs = refs[:N_PREP_PARAMS], refs[N_PREP_PARAMS:]
        pieces, _ = _shifted_pieces(pl.program_id(0), p_ref, halo_ref, mix_ref)
        vals = _rwkv_core(*pieces, *[t[...] for t in prm_refs])
        for ref, val in zip(outs, vals):
            ref[...] = val

    return pl.pallas_call(
        body, name="rwkv_prep", grid=(SEQ // TR,),
        in_specs=_prep_in_specs(),
        out_specs=[_rows(TR, D_RWKV)] * 7,
        out_shape=[jax.ShapeDtypeStruct((SEQ, D_RWKV), F32)] * 7,
        compiler_params=_cp(("parallel",)),
    )(proj, proj, mix, *prm)


def _rwkv_prep_bwd(proj, mix, prm, cts):
    def body(p_ref, halo_ref, mix_ref, *refs):
        i = pl.program_id(0)
        prm_refs = refs[:N_PREP_PARAMS]
        ct_refs = refs[N_PREP_PARAMS:N_PREP_PARAMS + 10]
        dps_ref, dmix_ref = refs[N_PREP_PARAMS + 10:N_PREP_PARAMS + 12]
        dprm_refs = refs[N_PREP_PARAMS + 12:]
        pieces, delta = _shifted_pieces(i, p_ref, halo_ref, mix_ref)
        _, vjp = jax.vjp(_rwkv_core, *pieces, *[t[...] for t in prm_refs])
        dr1, dr2, dw, dk1, dk2, dv1, dv2, dkkn, db, dg = [t[...] for t in ct_refs]
        grads = vjp((dr1 + dr2, dw, dk1 + dk2, dv1 + dv2, dkkn, db, dg))
        dps = jnp.concatenate(grads[:5], axis=1)
        dps_ref[...] = dps

        @pl.when(i == 0)
        def _():
            dmix_ref[...] = jnp.zeros_like(dmix_ref)
            for ref in dprm_refs:
                ref[...] = jnp.zeros_like(ref)

        dmix_ref[...] += jnp.sum(dps * delta, axis=0, keepdims=True)
        for ref, gval in zip(dprm_refs, grads[5:]):
            ref[...] += gval

    prm_shapes = [(1, D_RWKV), (LANES, D_RWKV), (1, D_RWKV), (LANES, D_RWKV), (LANES, D_RWKV), (1, D_RWKV), (1, D_RWKV)]
    return pl.pallas_call(
        body, name="rwkv_prep_bwd", grid=(SEQ // TR,),
        in_specs=_prep_in_specs() + [_rows(TR, D_RWKV)] * 10,
        out_specs=[_rows(TR, RWKV_COLS), _const((1, RWKV_COLS))] + [_const(s) for s in prm_shapes],
        out_shape=[jax.ShapeDtypeStruct((SEQ, RWKV_COLS), F32), jax.ShapeDtypeStruct((1, RWKV_COLS), F32)]
        + [jax.ShapeDtypeStruct(s, F32) for s in prm_shapes],
        compiler_params=_cp(("arbitrary",)),
    )(proj, proj, mix, *prm, *cts)


def _rwkv_post(o, r, k2, v, g, lng, lnb, rk, attn):
    def body(o_ref, r_ref, k_ref, v_ref, g_ref, lng_ref, lnb_ref, rk_ref, attn_ref, cat_ref):
        rw = _rwkv_out(*[t[...] for t in (o_ref, r_ref, k_ref, v_ref, g_ref, lng_ref, lnb_ref, rk_ref)])
        cat_ref[...] = jnp.concatenate([attn_ref[...], rw], axis=1).astype(BF16)

    return pl.pallas_call(
        body, name="rwkv_post", grid=(SEQ // TR,),
        in_specs=[_rows(TR, D_RWKV)] * 5 + [_const((1, D_RWKV))] * 3 + [_rows(TR, D_ATTN)],
        out_specs=_rows(TR, D_MODEL),
        out_shape=jax.ShapeDtypeStruct((SEQ, D_MODEL), BF16),
        compiler_params=_cp(("parallel",)),
    )(o, r, k2, v, g, lng, lnb, rk, attn)


def _rwkv_post_bwd(o, r, k2, v, g, lng, lnb, rk, dcat):
    def body(o_ref, r_ref, k_ref, v_ref, g_ref, lng_ref, lnb_ref, rk_ref, dcat_ref,
             do_ref, dr_ref, dk_ref, dv_ref, dg_ref, dlng_ref, dlnb_ref, drk_ref):
        i = pl.program_id(0)
        args = [t[...] for t in (o_ref, r_ref, k_ref, v_ref, g_ref, lng_ref, lnb_ref, rk_ref)]
        _, vjp = jax.vjp(_rwkv_out, *args)
        grads = vjp(dcat_ref[:, D_ATTN:])
        for ref, gval in zip((do_ref, dr_ref, dk_ref, dv_ref, dg_ref), grads[:5]):
            ref[...] = gval

        @pl.when(i == 0)
        def _():
            for ref in (dlng_ref, dlnb_ref, drk_ref):
                ref[...] = jnp.zeros_like(ref)

        for ref, gval in zip((dlng_ref, dlnb_ref, drk_ref), grads[5:]):
            ref[...] += gval

    return pl.pallas_call(
        body, name="rwkv_post_bwd", grid=(SEQ // TR,),
        in_specs=[_rows(TR, D_RWKV)] * 5 + [_const((1, D_RWKV))] * 3 + [_rows(TR, D_MODEL)],
        out_specs=[_rows(TR, D_RWKV)] * 5 + [_const((1, D_RWKV))] * 3,
        out_shape=[jax.ShapeDtypeStruct((SEQ, D_RWKV), F32)] * 5 + [jax.ShapeDtypeStruct((1, D_RWKV), F32)] * 3,
        compiler_params=_cp(("arbitrary",)),
    )(o, r, k2, v, g, lng, lnb, rk, dcat)


def _assemble_dproj(dq, dkv, dps, mix):
    last = SEQ // HALO - 1

    def body(dq_ref, dkv_ref, dps_ref, nxt_ref, mix_ref, o_ref):
        i = pl.program_id(0)
        dps = dps_ref[...]
        mixv = mix_ref[...]
        nxt_row = nxt_ref[0:1, :] * jnp.where(i < SEQ // TR - 1, 1.0, 0.0)
        row = lax.broadcasted_iota(jnp.int32, dps.shape, 0)
        up = jnp.where(row == TR - 1, nxt_row, pltpu.roll(dps, TR - 1, 0))
        dp = dps * (1.0 - mixv) + up * mixv
        o_ref[...] = jnp.concatenate([dq_ref[...], dkv_ref[...], dp], axis=1).astype(BF16)

    return pl.pallas_call(
        body, name="assemble_dproj", grid=(SEQ // TR,),
        in_specs=[_rows(TR, D_ATTN), _rows(TR, 2 * D_KV), _rows(TR, RWKV_COLS),
                  pl.BlockSpec((HALO, RWKV_COLS), lambda i: (jnp.minimum((i + 1) * (TR // HALO), last), 0)),
                  _const((1, RWKV_COLS))],
        out_specs=_rows(TR, D_IN),
        out_shape=jax.ShapeDtypeStruct((SEQ, D_IN), BF16),
        compiler_params=_cp(("parallel",)),
    )(dq, dkv, dps, dps, mix)


N_PAIR = D_RWKV // LANES
CHUNK = 64
N_CHUNK = SEQ // CHUNK
GROUP = 8
STATE = (N_PAIR, HEAD_DIM, LANES)


def _lane_sums(lhs_tiles, ones2):
    out = _dot(jnp.concatenate(lhs_tiles, axis=0), ones2)
    return [out[i * HEAD_DIM:(i + 1) * HEAD_DIM] for i in range(len(lhs_tiles))]


def _seg_sum(xs, ones2):
    return _lane_sums([jnp.concatenate(_split(x, 2), axis=1) for x in xs], ones2)


def _seg_sum_rows(xs, ones2):
    out = _dot(jnp.concatenate(_split(jnp.concatenate(xs, axis=0), 2), axis=1), ones2)
    return [out[i * GROUP:(i + 1) * GROUP] for i in range(len(xs))]


def _col_form(rows, diag, ones2):
    zero = jnp.zeros((HEAD_DIM, LANES), BF16)
    tiles = []
    for row in rows:
        hi = row.astype(BF16)
        lo = (row - hi.astype(F32)).astype(BF16)
        tiles.append(jnp.concatenate(
            [jnp.where(diag, jnp.broadcast_to(part, (HEAD_DIM, LANES)), zero) for part in (hi, lo)], axis=1))
    return _lane_sums(tiles, ones2)


def _scan_consts():
    ones2 = jnp.concatenate([_head_ones(LANES)] * 2, axis=0)
    sub = lax.broadcasted_iota(jnp.int32, (HEAD_DIM, LANES), 0)
    lane_in_head = lax.broadcasted_iota(jnp.int32, (HEAD_DIM, LANES), 1) & (HEAD_DIM - 1)
    return ones2, lane_in_head == sub, lane_in_head


def _rows_of_columns(tile):
    t = tile.T
    return jnp.concatenate([t[:CHUNK], t[HEAD_DIM:HEAD_DIM + CHUNK]], axis=1)


def _pair(j):
    return slice(j * LANES, (j + 1) * LANES)


def _scan_fwd(r, w, k, v, kkn, b):
    def body(r_ref, w_ref, k_ref, v_ref, kkn_ref, b_ref, o_ref, st_ref, sa_ref, s_scr):
        c = pl.program_id(0)
        ones2, diag, lane_in_head = _scan_consts()

        @pl.when(c == 0)
        def _():
            s_scr[...] = jnp.zeros_like(s_scr)

        def group(gi, carry):
            row0 = pl.multiple_of(gi * GROUP, GROUP)
            states, ocols = list(carry[:N_PAIR]), list(carry[N_PAIR:])
            tiles = [[t[pl.ds(row0, GROUP), _pair(j)] for t in (r_ref, w_ref, k_ref, v_ref, kkn_ref, b_ref)]
                     for j in range(N_PAIR)]
            def row(j, name, u):
                return tiles[j]["rwkvnb".index(name)][u:u + 1]

            def emit_out(u, after):
                outs = _seg_sum([s[j] * row(j, "r", u + d) for d, s in enumerate(after) for j in range(N_PAIR)], ones2)
                for d in range(2):
                    here = lane_in_head == gi * GROUP + u + d
                    for j in range(N_PAIR):
                        ocols[j] = jnp.where(here, outs[d * N_PAIR + j], ocols[j])

            def vcols_of(u):
                cols = _col_form([row(j, "v", u + d) for d in range(2) for j in range(N_PAIR)], diag, ones2)
                return cols[:N_PAIR], cols[N_PAIR:]

            n_next = [pltpu.roll(tiles[j][4], GROUP - 1, 0) for j in range(N_PAIR)]
            dots = _seg_sum_rows([tiles[j][5] * n_next[j] for j in range(N_PAIR)]
                                 + [tiles[j][2] * n_next[j] for j in range(N_PAIR)], ones2)
            b_n, k_n = dots[:N_PAIR], dots[N_PAIR:]
            w_n = [tiles[j][1] * n_next[j] for j in range(N_PAIR)]

            vcols = vcols_of(0)
            after = None
            for u in range(0, GROUP, 2):
                prods = _seg_sum([states[j] * row(j, "n", u) for j in range(N_PAIR)]
                                 + [states[j] * w_n[j][u:u + 1] for j in range(N_PAIR)], ones2)
                if after is not None:
                    emit_out(u - 2, after)
                nxt = vcols_of(u + 2) if u + 2 < GROUP else None
                first, second = [], []
                for j in range(N_PAIR):
                    sa1 = prods[j]
                    sa2 = prods[N_PAIR + j] + sa1 * b_n[j][u:u + 1] + vcols[0][j] * k_n[j][u:u + 1]
                    s1 = states[j] * row(j, "w", u) + sa1 * row(j, "b", u) + vcols[0][j] * row(j, "k", u)
                    s2 = s1 * row(j, "w", u + 1) + sa2 * row(j, "b", u + 1) + vcols[1][j] * row(j, "k", u + 1)
                    st_ref[row0 + u, j] = s1
                    sa_ref[row0 + u, j] = sa1
                    st_ref[row0 + u + 1, j] = s2
                    sa_ref[row0 + u + 1, j] = sa2
                    first.append(s1)
                    second.append(s2)
                    states[j] = s2
                after, vcols = (first, second), nxt
            emit_out(GROUP - 2, after)
            return tuple(states + ocols)

        zero = jnp.zeros((HEAD_DIM, LANES), F32)
        fin = lax.fori_loop(0, CHUNK // GROUP, group, tuple(s_scr[j] for j in range(N_PAIR)) + (zero,) * N_PAIR)
        for j in range(N_PAIR):
            s_scr[j] = fin[j]
            o_ref[:, _pair(j)] = _rows_of_columns(fin[N_PAIR + j])

    blk = pl.BlockSpec((CHUNK, D_RWKV), lambda c: (c, 0))
    per_step = pl.BlockSpec((CHUNK,) + STATE, lambda c: (c, 0, 0, 0))
    return pl.pallas_call(
        body, name="rwkv_scan_fwd", grid=(N_CHUNK,),
        in_specs=[blk] * 6,
        out_specs=[blk, per_step, per_step],
        out_shape=[jax.ShapeDtypeStruct((SEQ, D_RWKV), F32)] + [jax.ShapeDtypeStruct((SEQ,) + STATE, F32)] * 2,
        scratch_shapes=[pltpu.VMEM(STATE, F32)],
        compiler_params=_cp(("arbitrary",)),
    )(r, w, k, v, kkn, b)


def _scan_bwd(r, w, k, v, kkn, b, do, states, sas, ds_in, prev, name, first_chunk, n_chunks):
    top = first_chunk + n_chunks - 1

    def body(r_ref, w_ref, k_ref, v_ref, kkn_ref, b_ref, do_ref, st_ref, before_ref, sa_ref, ds_in_ref, *rest):
        dr_ref, dw_ref, dk_ref, dv_ref, dkkn_ref, db_ref, ds_out_ref, ds_scr = rest[-8:]
        i = pl.program_id(0)
        ones2, diag, lane_in_head = _scan_consts()

        @pl.when(i == 0)
        def _():
            ds_scr[...] = ds_in_ref[...]

        entry = [before_ref[0, j] * jnp.where(i < top, 1.0, 0.0) for j in range(N_PAIR)]

        def reverse(gr, carry):
            gi = CHUNK // GROUP - 1 - gr
            row0 = pl.multiple_of(gi * GROUP, GROUP)
            dstates, dvcols = list(carry[:N_PAIR]), list(carry[N_PAIR:])
            tiles = [[t[pl.ds(row0, GROUP), _pair(j)]
                      for t in (r_ref, w_ref, k_ref, v_ref, kkn_ref, b_ref, do_ref)] for j in range(N_PAIR)]
            rows = [[[None] * GROUP for _ in range(5)] for _ in range(N_PAIR)]

            def row(j, name, u):
                return tiles[j]["rwkvnbd".index(name)][u:u + 1]

            def cols_of(u):
                cols = _col_form([row(j, name, u - d) for d in range(2) for name in "dv" for j in range(N_PAIR)],
                                 diag, ones2)
                return [[(cols[(2 * d) * N_PAIR + j], cols[(2 * d + 1) * N_PAIR + j]) for j in range(N_PAIR)]
                        for d in range(2)]

            def emit_dv(u, dsps):
                outs = _seg_sum([dsp[j] * row(j, "k", u - d) for d, dsp in enumerate(dsps) for j in range(N_PAIR)], ones2)
                for d in range(2):
                    here = lane_in_head == gi * GROUP + u - d
                    for j in range(N_PAIR):
                        dvcols[j] = jnp.where(here, outs[d * N_PAIR + j], dvcols[j])

            b_prev = [pltpu.roll(tiles[j][5], 1, 0) for j in range(N_PAIR)]
            dots = _seg_sum_rows([tiles[j][4] * b_prev[j] for j in range(N_PAIR)]
                                 + [tiles[j][0] * tiles[j][5] for j in range(N_PAIR)], ones2)
            n_b, r_b = dots[:N_PAIR], dots[N_PAIR:]
            w_b = [tiles[j][1] * b_prev[j] for j in range(N_PAIR)]

            def outputs(u, j, dsp, dsa, docol, vcol):
                tl = gi * GROUP + u
                if u > 0:
                    s_prev = st_ref[tl - 1, j]
                else:
                    s_prev = jnp.where(gi == 0, entry[j], st_ref[jnp.maximum(tl - 1, 0), j])
                rows[j][0][u] = jnp.sum(st_ref[tl, j] * docol, axis=0, keepdims=True)
                rows[j][1][u] = jnp.sum(dsp * s_prev, axis=0, keepdims=True)
                rows[j][2][u] = jnp.sum(dsp * vcol, axis=0, keepdims=True)
                rows[j][3][u] = jnp.sum(s_prev * dsa, axis=0, keepdims=True)
                rows[j][4][u] = jnp.sum(dsp * sa_ref[tl, j], axis=0, keepdims=True)

            cols = cols_of(GROUP - 1)
            before = None
            for u in range(GROUP - 1, 0, -2):
                dsp1 = [dstates[j] + cols[0][j][0] * row(j, "r", u) for j in range(N_PAIR)]
                prods = _seg_sum([dsp1[j] * row(j, "b", u) for j in range(N_PAIR)]
                                 + [dsp1[j] * w_b[j][u:u + 1] for j in range(N_PAIR)], ones2)
                if before is not None:
                    emit_dv(u + 2, before)
                nxt = cols_of(u - 2) if u >= 2 else None
                dsp2 = []
                for j in range(N_PAIR):
                    dsa1 = prods[j]
                    dsa2 = prods[N_PAIR + j] + dsa1 * n_b[j][u:u + 1] + cols[1][j][0] * r_b[j][u - 1:u]
                    mid = dsp1[j] * row(j, "w", u) + dsa1 * row(j, "n", u) + cols[1][j][0] * row(j, "r", u - 1)
                    outputs(u, j, dsp1[j], dsa1, *cols[0][j])
                    outputs(u - 1, j, mid, dsa2, *cols[1][j])
                    dstates[j] = mid * row(j, "w", u - 1) + dsa2 * row(j, "n", u - 1)
                    dsp2.append(mid)
                before, cols = (dsp1, dsp2), nxt
            emit_dv(1, before)
            for j in range(N_PAIR):
                for ref, rr in zip((dr_ref, dw_ref, dk_ref, dkkn_ref, db_ref), rows[j]):
                    ref[pl.ds(row0, GROUP), _pair(j)] = jnp.concatenate(rr, axis=0)
            return tuple(dstates + dvcols)

        zero = jnp.zeros((HEAD_DIM, LANES), F32)
        dfin = lax.fori_loop(0, CHUNK // GROUP, reverse, tuple(ds_scr[j] for j in range(N_PAIR)) + (zero,) * N_PAIR)
        for j in range(N_PAIR):
            ds_scr[j] = dfin[j]
            dv_ref[:, _pair(j)] = _rows_of_columns(dfin[N_PAIR + j])

        @pl.when(i == n_chunks - 1)
        def _():
            ds_out_ref[...] = ds_scr[...]

    blk = pl.BlockSpec((CHUNK, D_RWKV), lambda i: (top - i, 0))
    per_step = pl.BlockSpec((CHUNK,) + STATE, lambda i: (top - i, 0, 0, 0))
    step_before = pl.BlockSpec((1,) + STATE, lambda i: (jnp.maximum((top - i) * CHUNK - 1, 0), 0, 0, 0))
    prev = [] if prev is None else list(prev)
    outs = pl.pallas_call(
        body, name=name, grid=(n_chunks,),
        in_specs=[blk] * 7 + [per_step, step_before, per_step, _const(STATE)] + [ANY] * len(prev),
        out_specs=[blk] * 6 + [_const(STATE)],
        out_shape=[jax.ShapeDtypeStruct((SEQ, D_RWKV), F32)] * 6 + [jax.ShapeDtypeStruct(STATE, F32)],
        scratch_shapes=[pltpu.VMEM(STATE, F32)],
        input_output_aliases={11 + t: t for t in range(len(prev))},
        compiler_params=_cp(("arbitrary",)),
    )(r, w, k, v, kkn, b, do, states, states, sas, ds_in, *prev)
    return outs[:6], outs[6]


def _stacked(rows, cols, pick):
    return pl.BlockSpec((None, rows, cols), pick)


def _local_step(x, target, sm, win_st):
    def tied(t, token):
        return t if token is None else t + token[0:1, 0:1].reshape((1,) * t.ndim)

    zpad = jnp.zeros((LORA_DECAY, D_RWKV), F32)
    prm = [sm["w0"], jnp.concatenate([sm["w_decay_up"], zpad], axis=0), sm["a0"],
           jnp.concatenate([zpad, sm["w_iclr_up"]], axis=0), sm["w_gate_up"], sm["k_k"], sm["k_a"]]
    mix = sm["rwkv_shift_mix"]
    onehot = jnp.asarray(_t5_onehot(), BF16)
    sinks = sm["sinks"].reshape(N_Q_HEADS)
    lng, lnb, rk = sm["ln_x_g"], sm["ln_x_b"], sm["r_k"].reshape(1, D_RWKV)

    h1 = _norm_cast(x, sm["norm_mix_pre"], "norm_in")
    proj = _matmul(h1, win_st, "nn", "proj", m=SEQ, n=D_IN, k=D_MODEL, tm=SEQ, tn=640,
                   b_spec=_stacked(D_MODEL, 640, lambda i, j: (j, 0, 0)))
    bias = _bias_table(sm["rel_bias"].T, onehot).reshape(N_KV_HEADS, Q_PER_KV * BLOCK, 2 * BLOCK)
    attn = _attn_fwd(proj, bias, sinks)
    r, w, k2, v, kkn, b, g = _rwkv_prep(proj, mix, prm)
    o, states, sas = _scan_fwd(r, w, k2, v, kkn, b)
    wout, wup_st, wdown = yield ("rest_weights", o)
    cat = _rwkv_post(o, r, k2, v, g, lng, lnb, rk, attn)
    mixo = _matmul(cat, wout, "nn", "out_proj", m=SEQ, n=D_MODEL, k=D_MODEL, tm=SEQ, tn=512)
    x2, h3 = _mix_norm(x, mixo, sm["norm_mix_post"], sm["norm_ffn_pre"])
    u_gate, u_val, act = _ffn_up_act(h3, wup_st, sm["conv_w"], sm["conv_b"])
    f = _matmul(act, wdown, "nn", "ffn_down", m=SEQ, n=D_MODEL, k=D_FF, tm=1024, tn=512)
    loss, dy, df, d_g4 = _loss_head(x2, f, sm["norm_ffn_post"], target)

    d_wdown = _matmul(act, df, "tn", "d_wdown", m=D_FF, n=D_MODEL, k=SEQ, tm=512, tn=D_MODEL)
    du, d_convw, d_convb = _ffn_act_bwd(u_gate, u_val, df, wdown, sm["conv_w"], sm["conv_b"])
    d_convw = d_convw.transpose(1, 0, 2).reshape(3, 2 * D_FF)
    d_convb = d_convb.reshape(1, 2 * D_FF)
    dh3 = _matmul_nt_shards(du, wup_st, "d_h3", m=SEQ, n=D_MODEL, tm=512, tn=512,
                            a_spec=pl.BlockSpec((2, 512, D_FF), lambda i, j: (0, i, 0)),
                            a_piece=lambda ref, s: ref[s // 2, :, (s % 2) * 2048:(s % 2 + 1) * 2048])
    d_wup = _matmul(h3, du, "tn", "d_wup", m=D_MODEL, n=2 * D_FF, k=SEQ, tm=D_MODEL, tn=512,
                    b_spec=pl.BlockSpec((None, SEQ, 512), lambda i, j: (j // 8, 0, j % 8)),
                    out=((N_CHIPS, D_MODEL, 2048), _stacked(D_MODEL, 512, lambda i, j: (j // 4, 0, j % 4))))
    dx2, dmix, d_g2, d_g3 = _mid_bwd(x2, mixo, dy, dh3, sm["norm_mix_post"], sm["norm_ffn_pre"])
    dcat = _matmul(dmix, wout, "nt", "d_cat", m=SEQ, n=D_MODEL, k=D_MODEL, tm=SEQ, tn=512)
    d_wout = _matmul(cat, dmix, "tn", "d_wout", m=D_MODEL, n=D_MODEL, k=SEQ, tm=512, tn=D_MODEL)
    token = yield ("grads_a", (d_wdown, d_wup, d_wout))
    do, dr_p, dk_p, dv_p, dg, d_lng, d_lnb, d_rk = _rwkv_post_bwd(o, r, k2, v, g, lng, tied(lnb, token), rk, dcat)
    half = N_CHUNK // 2
    ds_end = jnp.zeros(STATE, F32)
    late, ds_mid = _scan_bwd(r, w, k2, v, kkn, b, do, states, sas, ds_end, None, "rwkv_scan_bwd_late", half, half)
    token = yield ("seam_1", ds_mid)
    scan_cts, ds_first = _scan_bwd(r, w, k2, v, kkn, b, do, states, sas, tied(ds_mid, token), late,
                                   "rwkv_scan_bwd_early", 0, half)
    dr_s, dw_s, dk_s, dv_s, dkkn_s, db_s = scan_cts
    token = yield ("seam_2", ds_first)
    prep_grads = _rwkv_prep_bwd(proj, tied(mix, token), prm,
                                (dr_s, dr_p, dw_s, dk_s, dk_p, dv_s, dv_p, dkkn_s, db_s, dg))
    dps, d_mix, d_w0, d_wdu, d_a0, d_wiu, d_wgu, d_kk, d_ka = prep_grads
    dq, dkv, dbias, dsink = _attn_bwd(proj, bias, sinks, dcat)
    d_relb = _bias_table_bwd(dbias.reshape(N_Q_HEADS, N_REL), onehot).T
    dproj = _assemble_dproj(dq, dkv, dps, mix)
    d_win = _matmul(h1, dproj, "tn", "d_win", m=D_MODEL, n=D_IN, k=SEQ, tm=D_MODEL, tn=640,
                    out=((N_CHIPS, D_MODEL, 640), _stacked(D_MODEL, 640, lambda i, j: (j, 0, 0))))
    token = yield ("grads_b", d_win)
    dh1 = _matmul_nt_shards(dproj, win_st, "d_h1", m=SEQ, n=D_MODEL, tm=1024, tn=D_MODEL,
                            a_spec=pl.BlockSpec((1024, D_IN), lambda i, j: (i, 0)),
                            a_piece=lambda ref, s: ref[:, s * 640:(s + 1) * 640])
    grad_x, d_g1 = _first_bwd(x, dx2, dh1, tied(sm["norm_mix_pre"], token))

    grads = {
        "norm_mix_pre": d_g1, "norm_mix_post": d_g2, "norm_ffn_pre": d_g3, "norm_ffn_post": d_g4,
        "w_in": d_win, "rel_bias": d_relb, "sinks": dsink[:, 0].reshape(1, N_Q_HEADS),
        "rwkv_shift_mix": d_mix, "w0": d_w0, "w_decay_up": d_wdu[:LORA_DECAY], "a0": d_a0,
        "w_iclr_up": d_wiu[LORA_DECAY:], "w_gate_up": d_wgu, "k_k": d_kk, "k_a": d_ka,
        "r_k": d_rk.reshape(1, N_Q_HEADS, HEAD_DIM), "ln_x_g": d_lng, "ln_x_b": d_lnb,
        "w_out": d_wout, "w_ffn_up": d_wup, "conv_w": d_convw, "conv_b": d_convb, "w_ffn_down": d_wdown,
    }
    return loss, grad_x, grads


def _place():
    x, y, c = lax.axis_index("x"), lax.axis_index("y"), lax.axis_index("c")
    chips = [(1 - x, y), (x, 1 - y), (1 - x, 1 - y)]
    return x, y, c, chips


def _remote(src, dst, sems, idx, to):
    return pltpu.make_async_remote_copy(src_ref=src, dst_ref=dst, send_sem=sems[0].at[idx], recv_sem=sems[1].at[idx],
                                        device_id=to, device_id_type=MESH)


def _half(c, rows):
    return pl.ds(pl.multiple_of(c * (rows // 2), 16), rows // 2)


def _gather_weights(big, small):
    nb, ns = len(big), len(small)

    def body(*refs):
        ins, outs = refs[:nb + ns], refs[nb + ns:2 * (nb + ns)]
        ici, d2d, sml, loc = refs[2 * (nb + ns):2 * (nb + ns) + 2], refs[-5:-3], refs[-3:-1], refs[-1]
        x, y, c, chips = _place()
        me = 2 * x + y
        sib = (x, y, 1 - c)
        local = [pltpu.make_async_copy(ins[a], outs[a].at[me], loc.at[a]) for a in range(nb + ns)]
        for cp in local:
            cp.start()
        sends = []
        for a in range(nb):
            rows = _half(c, big[a].shape[0])
            for kk, chip in enumerate(chips):
                sends.append(_remote(ins[a].at[rows], outs[a].at[me, rows], ici, a * 3 + kk, (*chip, c)))
        for a in range(ns):
            for kk, chip in enumerate(chips):
                sends.append(_remote(ins[nb + a], outs[nb + a].at[me], sml, a * 3 + kk, (*chip, c)))
        for cp in sends:
            cp.start()
        passed = []
        for a in range(nb):
            rows = _half(c, big[a].shape[0])
            for kk, (px, py) in enumerate(chips):
                got = outs[a].at[2 * px + py, rows]
                _remote(got, got, ici, a * 3 + kk, sib).wait_recv()
                fwd = _remote(got, got, d2d, a * 3 + kk, sib)
                fwd.start()
                passed.append(fwd)
        for a in range(nb):
            other = _half(1 - c, big[a].shape[0])
            for kk, (px, py) in enumerate(chips):
                land = outs[a].at[2 * px + py, other]
                _remote(land, land, d2d, a * 3 + kk, sib).wait_recv()
        for a in range(ns):
            for kk, (px, py) in enumerate(chips):
                land = outs[nb + a].at[2 * px + py]
                _remote(land, land, sml, a * 3 + kk, sib).wait_recv()
        for cp in sends + passed:
            cp.wait_send()
        for cp in local:
            cp.wait()

    arrs = list(big) + list(small)
    return pl.pallas_call(
        body, name="gather_weights",
        in_specs=[ANY] * len(arrs), out_specs=[ANY] * len(arrs),
        out_shape=[jax.ShapeDtypeStruct((N_CHIPS,) + t.shape, t.dtype) for t in arrs],
        scratch_shapes=[pltpu.SemaphoreType.DMA((3 * nb,)), pltpu.SemaphoreType.DMA((3 * nb,)),
                        pltpu.SemaphoreType.DMA((3 * nb,)), pltpu.SemaphoreType.DMA((3 * nb,)),
                        pltpu.SemaphoreType.DMA((3 * ns,)), pltpu.SemaphoreType.DMA((3 * ns,)),
                        pltpu.SemaphoreType.DMA((nb + ns,))],
        compiler_params=pltpu.CompilerParams(has_side_effects=True),
    )(*arrs)


HBM = pl.BlockSpec(memory_space=pltpu.HBM)
SEM = pl.BlockSpec(memory_space=pltpu.SEMAPHORE)
EFFECT = pltpu.SideEffectType.DATAFLOW_SIDE_EFFECTING


def _copies_start(name, bufs, plan, n):
    nb = len(bufs)

    def body(*refs):
        ins, sems, token = refs[:nb], refs[nb:nb + 2 * n], refs[-1]
        for kk, (src, dst, dev) in enumerate(plan(ins)):
            pltpu.make_async_remote_copy(src_ref=src, dst_ref=dst, send_sem=sems[2 * kk], recv_sem=sems[2 * kk + 1],
                                         device_id=dev, device_id_type=MESH).start()
        token[...] = jnp.zeros_like(token)

    outs = pl.pallas_call(
        body, name=name,
        out_shape=tuple([pltpu.SemaphoreType.DMA(())] * (2 * n) + [pltpu.HBM(t.shape, t.dtype) for t in bufs]
                        + [jax.ShapeDtypeStruct((8, LANES), F32)]),
        in_specs=[HBM] * nb,
        out_specs=tuple([SEM] * (2 * n) + [HBM] * nb + [pl.BlockSpec(memory_space=pltpu.VMEM)]),
        input_output_aliases={t: 2 * n + t for t in range(nb)},
        compiler_params=pltpu.CompilerParams(has_side_effects=EFFECT),
    )(*[pltpu.with_memory_space_constraint(t, pltpu.HBM) for t in bufs])
    return outs[:2 * n], outs[2 * n:2 * n + nb], outs[-1]


def _copies_wait(name, sems, bufs, plan, n, after):
    nb = len(bufs)

    def body(*refs):
        ins, sem_refs = refs[:nb], refs[nb:nb + 2 * n]
        for kk, (src, dst, dev) in enumerate(plan(ins)):
            cp = pltpu.make_async_remote_copy(src_ref=src, dst_ref=dst, send_sem=sem_refs[2 * kk],
                                              recv_sem=sem_refs[2 * kk + 1], device_id=dev, device_id_type=MESH)
            cp.wait_send()
            cp.wait_recv()

    return pl.pallas_call(
        body, name=name,
        out_shape=tuple(pltpu.HBM(t.shape, t.dtype) for t in bufs),
        in_specs=[HBM] * nb + [SEM] * (2 * n) + [ANY],
        out_specs=tuple([HBM] * nb),
        input_output_aliases={t: t for t in range(nb)},
        compiler_params=pltpu.CompilerParams(has_side_effects=EFFECT),
    )(*bufs, *sems, after)


def _plan_gather(n_w):
    def plan(refs):
        x, y, c, chips = _place()
        me = 2 * x + y
        return [(refs[a], refs[n_w + a].at[me], (*chip, c)) for a in range(n_w) for chip in chips]
    return plan


def _plan_pair_halves(n_g, rows):
    def plan(refs):
        x, y, c, _ = _place()
        return [(refs[a].at[:, _half(1 - c, rows[a])], refs[n_g + a], (x, y, 1 - c)) for a in range(n_g)]
    return plan


def _plan_chip_parts(n_g):
    def plan(refs):
        x, y, c, chips = _place()
        me = 2 * x + y
        return [(refs[a].at[2 * px + py], refs[n_g + a].at[me], (px, py, c))
                for a in range(n_g) for (px, py) in chips]
    return plan


def _plan_pair_fill(n_g, rows):
    def plan(refs):
        x, y, c, _ = _place()
        return [(refs[a].at[_half(c, rows[a])], refs[a].at[_half(c, rows[a])], (x, y, 1 - c)) for a in range(n_g)]
    return plan


def _pair_add(g, got, name):
    _, rows, cols = g.shape
    hr = rows // 2
    tr = min(hr, 256)
    nb = hr // tr

    def body(g_ref, got_ref, p_ref, own_ref):
        val = (g_ref[...] + got_ref[...]).astype(BF16)
        p_ref[...] = val

        @pl.when(pl.program_id(1) == 2 * lax.axis_index("x") + lax.axis_index("y"))
        def _():
            own_ref[...] = val

    def mine(i, s):
        return (2 * lax.axis_index("x") + lax.axis_index("y"), i, 0)

    return pl.pallas_call(
        body, name=name, grid=(nb, N_CHIPS),
        in_specs=[pl.BlockSpec((None, tr, cols), lambda i, s: (s, lax.axis_index("c") * nb + i, 0)),
                  pl.BlockSpec((None, tr, cols), lambda i, s: (s, i, 0))],
        out_specs=[pl.BlockSpec((None, tr, cols), lambda i, s: (s, i, 0)), pl.BlockSpec((None, tr, cols), mine)],
        out_shape=[jax.ShapeDtypeStruct((N_CHIPS, hr, cols), BF16)] * 2,
        compiler_params=_cp(("parallel", "arbitrary")),
    )(g, got)


def _chip_sum(parts, name):
    _, hr, cols = parts.shape
    tr = min(hr, 128)
    nb = hr // tr

    def body(t_ref, o_ref):
        part = [t_ref[s].astype(F32) for s in range(N_CHIPS)]
        o_ref[...] = ((part[0] + part[1]) + part[2]) + part[3]

    return pl.pallas_call(
        body, name=name, grid=(nb,),
        in_specs=[pl.BlockSpec((N_CHIPS, tr, cols), lambda i: (0, i, 0))],
        out_specs=pl.BlockSpec((tr, cols), lambda i: (lax.axis_index("c") * nb + i, 0)),
        out_shape=jax.ShapeDtypeStruct((2 * hr, cols), F32),
        compiler_params=_cp(("parallel",)),
    )(parts)


class _Reduction:
    def __init__(self, tag, rows):
        self.tag, self.n, self.rows = tag, len(rows), rows
        self.plans = (_plan_pair_halves(self.n, rows), _plan_chip_parts(self.n), _plan_pair_fill(self.n, rows))
        self.flight = None

    def _name(self, what):
        return f"grad_{self.tag}_{what}"

    def start(self, gs):
        gots = [lax.empty((N_CHIPS, t.shape[1] // 2, t.shape[2]), F32) for t in gs]
        self.flight = _copies_start(self._name("pair_start"), list(gs) + gots, self.plans[0], self.n)
        return self.flight[2]

    def after_pair(self, after):
        sems, bufs, _ = self.flight
        out = _copies_wait(self._name("pair_wait"), sems, bufs, self.plans[0], self.n, after)
        sums = [_pair_add(g, got, self._name(f"pair_add_{i}"))
                for i, (g, got) in enumerate(zip(out[:self.n], out[self.n:]))]
        self.flight = _copies_start(self._name("chip_start"), [p for p, _ in sums] + [own for _, own in sums],
                                    self.plans[1], 3 * self.n)
        return self.flight[2]

    def after_chips(self, after):
        sems, bufs, _ = self.flight
        out = _copies_wait(self._name("chip_wait"), sems, bufs, self.plans[1], 3 * self.n, after)
        fulls = [_chip_sum(t, self._name(f"chip_sum_{i}")) for i, t in enumerate(out[self.n:])]
        self.flight = _copies_start(self._name("fill_start"), fulls, self.plans[2], self.n)
        return self.flight[2]

    def finish(self, after):
        sems, bufs, _ = self.flight
        return _copies_wait(self._name("fill_wait"), sems, bufs, self.plans[2], self.n, after)


def _adamw_math(w, g, m, v):
    nm = ADAM_B1 * m + (1.0 - ADAM_B1) * g
    nv = ADAM_B2 * v + (1.0 - ADAM_B2) * (g * g)
    m_hat = nm / (1.0 - ADAM_B1 ** ADAM_STEP)
    v_hat = nv / (1.0 - ADAM_B2 ** ADAM_STEP)
    return -ADAM_LR * (m_hat / (jnp.sqrt(v_hat) + ADAM_EPS) + ADAM_WD * w), nm, nv


def _adamw(w, g, m, v, name, tr):
    r, cdim = w.shape

    def body(w_ref, g_ref, m_ref, v_ref, d_ref, nm_ref, nv_ref):
        d_ref[...], nm_ref[...], nv_ref[...] = _adamw_math(w_ref[...], g_ref[...], m_ref[...], v_ref[...])

    return pl.pallas_call(
        body, name=name, grid=(r // tr,), in_specs=[_rows(tr, cdim)] * 4, out_specs=[_rows(tr, cdim)] * 3,
        out_shape=[jax.ShapeDtypeStruct((r, cdim), F32)] * 3, compiler_params=_cp(("parallel",)),
    )(w, g, m, v)


def _adamw_small(w, parts, m, v):
    def body(w_ref, p_ref, m_ref, v_ref, d_ref, nm_ref, nv_ref, g_ref):
        g = p_ref[0]
        for dev in range(1, N_DEV):
            g = g + p_ref[dev]
        g_ref[...] = g
        d_ref[...], nm_ref[...], nv_ref[...] = _adamw_math(w_ref[...], g, m_ref[...], v_ref[...])

    return pl.pallas_call(
        body, name="adamw_small", grid=(1,),
        in_specs=[_const(w.shape), _const(parts.shape), _const(w.shape), _const(w.shape)],
        out_specs=[_const(w.shape)] * 4, out_shape=[jax.ShapeDtypeStruct(w.shape, F32)] * 4,
        compiler_params=_cp(("arbitrary",)),
    )(w, parts, m, v)


REPLICATED = (("norm_mix_pre", 1024), ("norm_mix_post", 1024), ("norm_ffn_pre", 1024), ("norm_ffn_post", 1024),
              ("rel_bias", 256), ("sinks", 8), ("rwkv_shift_mix", 1792), ("w0", 512), ("a0", 512), ("k_k", 512),
              ("k_a", 512), ("r_k", 512), ("ln_x_g", 512), ("ln_x_b", 512), ("conv_b", 8192))
SMALL_SHARDED = (("w_decay_up", LORA_DECAY, D_RWKV), ("w_iclr_up", LORA_ICLR, D_RWKV),
                 ("w_gate_up", LORA_GATE, D_RWKV), ("conv_w", 3, 2 * D_FF))
BIG = (("w_in", D_MODEL, 640), ("w_out", 256, D_MODEL), ("w_ffn_up", D_MODEL, 2048), ("w_ffn_down", 1024, D_MODEL))
PACK_ALIGN = 8 * LANES


def _pack(pieces):
    flat = []
    for t in pieces:
        t = t.reshape(-1)
        pad = (-t.shape[0]) % LANES
        flat.append(jnp.pad(t, (0, pad)) if pad else t)
    flat = jnp.concatenate(flat)
    pad = (-flat.shape[0]) % PACK_ALIGN
    return jnp.pad(flat, (0, pad)).reshape(-1, LANES)


def _unpack(buf, sizes):
    flat, out, off = buf.reshape(-1), [], 0
    for n in sizes:
        out.append(flat[off:off + n])
        off += n + ((-n) % LANES)
    return out


def kernel(x, norm_mix_pre, norm_mix_post, norm_ffn_pre, norm_ffn_post, w_in, rel_bias, sinks, rwkv_shift_mix, w0, w_decay_up, a0, w_iclr_up, w_gate_up, k_k, k_a, r_k, ln_x_g, ln_x_b, w_out, w_ffn_up, conv_w, conv_b, w_ffn_down, loss_target, m_norm_mix_pre, m_norm_mix_post, m_norm_ffn_pre, m_norm_ffn_post, m_w_in, m_rel_bias, m_sinks, m_rwkv_shift_mix, m_w0, m_w_decay_up, m_a0, m_w_iclr_up, m_w_gate_up, m_k_k, m_k_a, m_r_k, m_ln_x_g, m_ln_x_b, m_w_out, m_w_ffn_up, m_conv_w, m_conv_b, m_w_ffn_down, v_norm_mix_pre, v_norm_mix_post, v_norm_ffn_pre, v_norm_ffn_post, v_w_in, v_rel_bias, v_sinks, v_rwkv_shift_mix, v_w0, v_w_decay_up, v_a0, v_w_iclr_up, v_w_gate_up, v_k_k, v_k_a, v_r_k, v_ln_x_g, v_ln_x_b, v_w_out, v_w_ffn_up, v_conv_w, v_conv_b, v_w_ffn_down):
    given = dict(locals())
    names = [n for n, _ in REPLICATED] + [n for n, _, _ in SMALL_SHARDED] + [n for n, _, _ in BIG]
    order = ["norm_mix_pre", "norm_mix_post", "norm_ffn_pre", "norm_ffn_post", "w_in", "rel_bias", "sinks",
             "rwkv_shift_mix", "w0", "w_decay_up", "a0", "w_iclr_up", "w_gate_up", "k_k", "k_a", "r_k", "ln_x_g",
             "ln_x_b", "w_out", "w_ffn_up", "conv_w", "conv_b", "w_ffn_down"]
    assert sorted(names) == sorted(order)
    shard = 2 * lax.axis_index("x") + lax.axis_index("y")

    big_sh = {n: given[n].reshape(a, b).astype(BF16) for n, a, b in BIG}
    small_sh = [given[n].reshape(r, c // N_CHIPS) for n, r, c in SMALL_SHARDED]
    gathered = _gather_weights([big_sh["w_in"]], small_sh)
    rest = ("w_out", "w_ffn_up", "w_ffn_down")
    win_st, rest_sh = lax.optimization_barrier((gathered[0], [big_sh[n] for n in rest]))
    sm = {n: given[n] for n, _ in REPLICATED}
    sm["r_k"] = r_k.reshape(N_Q_HEADS, HEAD_DIM)
    for (n, r, c), st in zip(SMALL_SHARDED, gathered[1:]):
        sm[n] = st.transpose(1, 0, 2).reshape(r, c)

    lands = [lax.dynamic_update_slice(lax.empty((N_CHIPS,) + t.shape, BF16), t[None], (shard, 0, 0)) for t in rest_sh]
    plan_w = _plan_gather(len(rest))
    w_sems, w_bufs, token = _copies_start("gather_rest_start", rest_sh + lands, plan_w, 9)
    sm["norm_mix_pre"] = norm_mix_pre + token[0:1, 0:1]

    def on_rest_weights(after):
        out = _copies_wait("gather_rest_wait", w_sems, w_bufs, plan_w, 9, after)
        wout_st, wup_st, wdown_st = out[3:]
        return wout_st.reshape(D_MODEL, D_MODEL), wup_st, wdown_st.reshape(D_FF, D_MODEL)

    red_a = _Reduction("a", (1024, D_MODEL, 256))
    red_b = _Reduction("b", (D_MODEL,))

    def on_grads_a(gs):
        d_wdown, d_wup, d_wout = gs
        return red_a.start([d_wdown.reshape(N_CHIPS, 1024, D_MODEL), d_wup, d_wout.reshape(N_CHIPS, 256, D_MODEL)])

    handlers = {"rest_weights": on_rest_weights, "grads_a": on_grads_a, "seam_1": red_a.after_pair,
                "seam_2": red_a.after_chips, "grads_b": lambda g: red_b.start([g])}
    steps = _local_step(x[0], loss_target[0], sm, win_st)
    kind, payload = next(steps)
    while True:
        try:
            kind, payload = steps.send(handlers[kind](payload))
        except StopIteration as done:
            loss, grad_x, grads = done.value
            break

    small_names = [n for n, _ in REPLICATED] + [n for n, _, _ in SMALL_SHARDED]

    def shard_cols(t, s):
        return t[:, s * (t.shape[1] // N_CHIPS):(s + 1) * (t.shape[1] // N_CHIPS)]

    for_chip = jnp.stack([_pack([loss[0]] + [grads[n] for n, _ in REPLICATED]
                                + [shard_cols(grads[n], s) for n, _, _ in SMALL_SHARDED]) for s in range(N_CHIPS)])
    me = 2 * shard + lax.axis_index("c")
    mine = lax.dynamic_index_in_dim(for_chip, shard, 0, keepdims=True)
    land = lax.dynamic_update_slice(lax.empty((N_DEV,) + for_chip.shape[1:], F32), mine, (me, 0, 0))

    def plan_small(refs):
        x, y, c, _ = _place()
        out = []
        for rel in range(1, N_DEV):
            px, py, pc = x ^ (rel >> 2), y ^ ((rel >> 1) & 1), c ^ (rel & 1)
            out.append((refs[0].at[2 * px + py], refs[1].at[4 * x + 2 * y + c], (px, py, pc)))
        return out

    s_sems, s_bufs, _ = _copies_start("grad_small_start", [for_chip, land], plan_small, N_DEV - 1)

    red_b.after_pair(grad_x)
    g_out = {}
    g_out["w_ffn_down"], g_out["w_ffn_up"], g_out["w_out"] = red_a.finish(grad_x)

    delta, new_m, new_v = {}, {}, {}
    for n, a, b in reversed(BIG):
        if n == "w_out":
            red_b.after_chips(delta["w_ffn_up"])
        if n == "w_in":
            parts = _copies_wait("grad_small_wait", s_sems, s_bufs, plan_small, N_DEV - 1, delta["w_out"])[1]
            no_param = jnp.zeros((LANES,), F32)
            packs = [_pack([no_param] + [given[pre + n2] for n2 in small_names]) for pre in ("", "m_", "v_")]
            small_sizes = [LANES] + [int(np.prod(given[n2].shape)) for n2 in small_names]
            upd = [_unpack(t, small_sizes) for t in _adamw_small(packs[0], parts, packs[1], packs[2])]
            loss = upd[3][0][0]
            for n2, d, nm, nv, g in zip(small_names, *[u[1:] for u in upd]):
                shape = given[n2].shape
                delta[n2], new_m[n2], new_v[n2], g_out[n2] = (t.reshape(shape) for t in (d, nm, nv, g))
            g_out[n], = red_b.finish(delta["w_out"])
        d, nm, nv = _adamw(given[n].reshape(a, b), g_out[n], given["m_" + n].reshape(a, b),
                           given["v_" + n].reshape(a, b), "adamw_" + n, 128)
        delta[n], new_m[n], new_v[n] = d, nm, nv

    def shaped(d):
        return [d[n].reshape(given[n].shape) for n in order]

    return (loss, grad_x.reshape(x.shape), *shaped(g_out), *shaped(delta), *shaped(new_m), *shaped(new_v))
```

```python
import math

import numpy as np
import jax
import jax.numpy as jnp
from jax import lax
from jax.experimental import pallas as pl
from jax.experimental.pallas import tpu as pltpu

F32 = jnp.float32
BF16 = jnp.bfloat16
MESH = pl.DeviceIdType.MESH

SEQ = 2048
D_MODEL = 1024
HEAD_DIM = 64
D_ATTN = 512
D_RWKV = 512
D_KV = 128
N_Q_HEADS = 8
N_KV_HEADS = 2
Q_PER_KV = 4
BLOCK = 128
N_BUCKETS = 32
MAX_DISTANCE = 128
LORA_DECAY = 64
LORA_ICLR = 64
LORA_GATE = 128
RWKV_COLS = 3 * D_RWKV + LORA_DECAY + LORA_ICLR + LORA_GATE
P_OFF = D_ATTN + 2 * D_KV
D_IN = P_OFF + RWKV_COLS
D_FF = 4096
NORM_EPS = 1e-6
GN_EPS = 64e-5
NEG_INF = -1e30
N_CHIPS = 4
N_DEV = 8

ADAM_LR = 0.001
ADAM_B1 = 0.9
ADAM_B2 = 0.999
ADAM_EPS = 1e-08
ADAM_WD = 0.01
ADAM_STEP = 10

VMEM_LIMIT = 52 * 1024 * 1024
LANES = 128


def _cp(sem=None, vmem=VMEM_LIMIT):
    kw = dict(vmem_limit_bytes=vmem)
    if sem is not None:
        kw["dimension_semantics"] = sem
    return pltpu.CompilerParams(**kw)


def _rows(tr, nc):
    return pl.BlockSpec((tr, nc), lambda i: (i, 0))


def _const(shape):
    return pl.BlockSpec(shape, lambda *_: (0,) * len(shape))


ANY = pl.BlockSpec(memory_space=pl.ANY)


def _split(x, n):
    parts = []
    for _ in range(n - 1):
        h = x.astype(BF16)
        parts.append(h)
        x = x - h.astype(F32)
    parts.append(x.astype(BF16))
    return parts


def _dot(a, b, dn=(((1,), (0,)), ((), ()))):
    return lax.dot_general(a, b, dn, preferred_element_type=F32)


NN = (((1,), (0,)), ((), ()))
NT = (((1,), (1,)), ((), ()))
TN = (((0,), (0,)), ((), ()))


def _dot_ind(x, ind_bf16, n=3):
    acc = None
    for part in _split(x, n):
        t = _dot(part, ind_bf16)
        acc = t if acc is None else acc + t
    return acc


def _head_ones(n):
    r = lax.broadcasted_iota(jnp.int32, (n, n), 0) >> 6
    c = lax.broadcasted_iota(jnp.int32, (n, n), 1) >> 6
    return jnp.where(r == c, 1.0, 0.0).astype(BF16)


def _matmul(a, b, mode, name, *, m, n, k, tm, tn, a_spec=None, b_spec=None, out=None, out_dtype=F32):
    keep_at = mode == "tn" and m == tm and n > tn

    def body(a_ref, b_ref, o_ref, *scratch):
        if keep_at:
            at_ref, = scratch

            @pl.when(pl.program_id(1) == 0)
            def _():
                at_ref[...] = a_ref[...].T

            o_ref[...] = _dot(at_ref[...], b_ref[...], NN).astype(out_dtype)
        else:
            o_ref[...] = _dot(a_ref[...], b_ref[...], {"nn": NN, "nt": NT, "tn": TN}[mode]).astype(out_dtype)

    if a_spec is None:
        a_spec = pl.BlockSpec((k, tm), lambda i, j: (0, i)) if mode == "tn" else pl.BlockSpec((tm, k), lambda i, j: (i, 0))
    if b_spec is None:
        b_spec = pl.BlockSpec((tn, k), lambda i, j: (j, 0)) if mode == "nt" else pl.BlockSpec((k, tn), lambda i, j: (0, j))
    return pl.pallas_call(
        body, name=name, grid=(m // tm, n // tn),
        in_specs=[a_spec, b_spec],
        out_specs=pl.BlockSpec((tm, tn), lambda i, j: (i, j)) if out is None else out[1],
        out_shape=jax.ShapeDtypeStruct((m, n) if out is None else out[0], out_dtype),
        scratch_shapes=[pltpu.VMEM((tm, k), a.dtype)] if keep_at else [],
        compiler_params=_cp(("parallel", "arbitrary" if keep_at else "parallel")),
    )(a, b)


def _matmul_nt_shards(a, b_st, name, *, m, n, tm, tn, a_spec, a_piece):
    ks = b_st.shape[2]

    def body(a_ref, b_ref, o_ref):
        acc = _dot(a_piece(a_ref, 0), b_ref[0], NT)
        for s in range(1, N_CHIPS):
            acc = acc + _dot(a_piece(a_ref, s), b_ref[s], NT)
        o_ref[...] = acc

    return pl.pallas_call(
        body, name=name, grid=(m // tm, n // tn),
        in_specs=[a_spec, pl.BlockSpec((N_CHIPS, tn, ks), lambda i, j: (0, j, 0))],
        out_specs=pl.BlockSpec((tm, tn), lambda i, j: (i, j)),
        out_shape=jax.ShapeDtypeStruct((m, n), F32),
        compiler_params=_cp(("parallel", "parallel")),
    )(a, b_st)


def _rstd(x):
    return lax.rsqrt(jnp.mean(x * x, axis=-1, keepdims=True) + NORM_EPS)


def _rms_bwd(x, r, g, dy):
    gy = dy * g
    return r * gy - x * ((r * r * r) * (jnp.sum(x * gy, axis=-1, keepdims=True) / x.shape[-1]))


TR = 256


def _norm_cast(x, g, name):
    def body(x_ref, g_ref, h_ref):
        x = x_ref[...]
        h_ref[...] = (x * _rstd(x) * g_ref[...]).astype(BF16)

    return pl.pallas_call(
        body, name=name, grid=(SEQ // TR,),
        in_specs=[_rows(TR, D_MODEL), _const((1, D_MODEL))],
        out_specs=_rows(TR, D_MODEL),
        out_shape=jax.ShapeDtypeStruct((SEQ, D_MODEL), BF16),
        compiler_params=_cp(("parallel",)),
    )(x, g)


def _mix_norm(x, mix, g2, g3):
    def body(x_ref, mix_ref, g2_ref, g3_ref, x2_ref, h3_ref):
        mixv = mix_ref[...]
        x2 = x_ref[...] + mixv * _rstd(mixv) * g2_ref[...]
        x2_ref[...] = x2
        h3_ref[...] = (x2 * _rstd(x2) * g3_ref[...]).astype(BF16)

    return pl.pallas_call(
        body, name="mix_norm", grid=(SEQ // TR,),
        in_specs=[_rows(TR, D_MODEL), _rows(TR, D_MODEL), _const((1, D_MODEL)), _const((1, D_MODEL))],
        out_specs=[_rows(TR, D_MODEL), _rows(TR, D_MODEL)],
        out_shape=[jax.ShapeDtypeStruct((SEQ, D_MODEL), F32), jax.ShapeDtypeStruct((SEQ, D_MODEL), BF16)],
        compiler_params=_cp(("parallel",)),
    )(x, mix, g2, g3)


def _loss_head(x2, f, g4, target):
    def body(x2_ref, f_ref, g4_ref, t_ref, loss_ref, dy_ref, df_ref, dg_ref):
        i = pl.program_id(0)
        f = f_ref[...]
        g4 = g4_ref[...]
        r = _rstd(f)
        e = x2_ref[...] + f * r * g4 - t_ref[...]
        dy = e * (1.0 / D_MODEL)
        dy_ref[...] = dy
        df_ref[...] = _rms_bwd(f, r, g4, dy).astype(BF16)
        part = 0.5 * jnp.sum(jnp.sum(e * e, axis=-1, keepdims=True), axis=0, keepdims=True) * (1.0 / D_MODEL)
        dg = jnp.sum(dy * f * r, axis=0, keepdims=True)

        @pl.when(i == 0)
        def _():
            loss_ref[...] = jnp.zeros_like(loss_ref)
            dg_ref[...] = jnp.zeros_like(dg_ref)

        loss_ref[...] += jnp.broadcast_to(part, loss_ref.shape)
        dg_ref[...] += dg

    return pl.pallas_call(
        body, name="loss_head", grid=(SEQ // TR,),
        in_specs=[_rows(TR, D_MODEL), _rows(TR, D_MODEL), _const((1, D_MODEL)), _rows(TR, D_MODEL)],
        out_specs=[_const((8, LANES)), _rows(TR, D_MODEL), _rows(TR, D_MODEL), _const((1, D_MODEL))],
        out_shape=[jax.ShapeDtypeStruct((8, LANES), F32), jax.ShapeDtypeStruct((SEQ, D_MODEL), F32),
                   jax.ShapeDtypeStruct((SEQ, D_MODEL), BF16), jax.ShapeDtypeStruct((1, D_MODEL), F32)],
        compiler_params=_cp(("arbitrary",)),
    )(x2, f, g4, target)


def _mid_bwd(x2, mix, dy, dh3, g2, g3):
    def body(x2_ref, mix_ref, dy_ref, dh3_ref, g2_ref, g3_ref, dx2_ref, dmix_ref, dg2_ref, dg3_ref):
        i = pl.program_id(0)
        x2 = x2_ref[...]
        mixv = mix_ref[...]
        dh3 = dh3_ref[...]
        r3 = _rstd(x2)
        dx2 = dy_ref[...] + _rms_bwd(x2, r3, g3_ref[...], dh3)
        dx2_ref[...] = dx2
        r2 = _rstd(mixv)
        dmix_ref[...] = _rms_bwd(mixv, r2, g2_ref[...], dx2).astype(BF16)

        @pl.when(i == 0)
        def _():
            dg2_ref[...] = jnp.zeros_like(dg2_ref)
            dg3_ref[...] = jnp.zeros_like(dg3_ref)

        dg3_ref[...] += jnp.sum(dh3 * x2 * r3, axis=0, keepdims=True)
        dg2_ref[...] += jnp.sum(dx2 * mixv * r2, axis=0, keepdims=True)

    return pl.pallas_call(
        body, name="mid_bwd", grid=(SEQ // TR,),
        in_specs=[_rows(TR, D_MODEL)] * 4 + [_const((1, D_MODEL))] * 2,
        out_specs=[_rows(TR, D_MODEL), _rows(TR, D_MODEL), _const((1, D_MODEL)), _const((1, D_MODEL))],
        out_shape=[jax.ShapeDtypeStruct((SEQ, D_MODEL), F32), jax.ShapeDtypeStruct((SEQ, D_MODEL), BF16),
                   jax.ShapeDtypeStruct((1, D_MODEL), F32), jax.ShapeDtypeStruct((1, D_MODEL), F32)],
        compiler_params=_cp(("arbitrary",)),
    )(x2, mix, dy, dh3, g2, g3)


def _first_bwd(x, dx2, dh1, g1):
    def body(x_ref, dx2_ref, dh1_ref, g1_ref, dx_ref, dg1_ref):
        i = pl.program_id(0)
        x = x_ref[...]
        dh1 = dh1_ref[...]
        r = _rstd(x)
        dx_ref[...] = dx2_ref[...] + _rms_bwd(x, r, g1_ref[...], dh1)

        @pl.when(i == 0)
        def _():
            dg1_ref[...] = jnp.zeros_like(dg1_ref)

        dg1_ref[...] += jnp.sum(dh1 * x * r, axis=0, keepdims=True)

    return pl.pallas_call(
        body, name="first_bwd", grid=(SEQ // TR,),
        in_specs=[_rows(TR, D_MODEL)] * 3 + [_const((1, D_MODEL))],
        out_specs=[_rows(TR, D_MODEL), _const((1, D_MODEL))],
        out_shape=[jax.ShapeDtypeStruct((SEQ, D_MODEL), F32), jax.ShapeDtypeStruct((1, D_MODEL), F32)],
        compiler_params=_cp(("arbitrary",)),
    )(x, dx2, dh1, g1)


TC = 256
N_CB = D_FF // TC
GELU_C = math.sqrt(2.0 / math.pi)


def _shift_down(u, s):
    rolled = pltpu.roll(u, s, 0)
    row = lax.broadcasted_iota(jnp.int32, u.shape, 0)
    return jnp.where(row >= s, rolled, 0.0)


def _shift_up(u, s):
    n = u.shape[0]
    rolled = pltpu.roll(u, n - s, 0)
    row = lax.broadcasted_iota(jnp.int32, u.shape, 0)
    return jnp.where(row < n - s, rolled, 0.0)


def _conv3(u, w, b):
    return b + w[0:1] * _shift_down(u, 2) + w[1:2] * _shift_down(u, 1) + w[2:3] * u


def _gelu_and_grad(x):
    inner = GELU_C * (x + 0.044715 * (x * x * x))
    t = jnp.tanh(inner)
    gelu = 0.5 * x * (1.0 + t)
    dgelu = 0.5 * (1.0 + t) + 0.5 * x * (1.0 - t * t) * (GELU_C * (1.0 + 3 * 0.044715 * (x * x)))
    return gelu, dgelu


def _ffn_specs():
    col = lambda off: pl.BlockSpec((SEQ, TC), lambda *g: (0, g[-1] + off))
    w = lambda off: pl.BlockSpec((3, TC), lambda *g: (0, g[-1] + off))
    b = lambda off: pl.BlockSpec((1, TC), lambda *g: (0, g[-1] + off))
    return col, w, b


def _ffn_up_act(h3, wup_st, conv_w, conv_b):
    col, w, b = _ffn_specs()
    per_shard = wup_st.shape[2] // TC

    def body(h_ref, upg_ref, upv_ref, wg_ref, wv_ref, bg_ref, bv_ref, ug_ref, uv_ref, act_ref):
        h = h_ref[...]
        for c in range(0, TC, LANES):
            cols = slice(c, c + LANES)
            ug = _dot(h, upg_ref[:, cols])
            uv = _dot(h, upv_ref[:, cols])
            ug_ref[:, cols] = ug
            uv_ref[:, cols] = uv
            gate = _conv3(ug, wg_ref[:, cols], bg_ref[:, cols])
            val = _conv3(uv, wv_ref[:, cols], bv_ref[:, cols])
            act_ref[:, cols] = (_gelu_and_grad(gate)[0] * val).astype(BF16)

    return pl.pallas_call(
        body, name="ffn_up_act", grid=(N_CB,),
        in_specs=[_const((SEQ, D_MODEL)),
                  pl.BlockSpec((None, D_MODEL, TC), lambda j: (j // per_shard, 0, j % per_shard)),
                  pl.BlockSpec((None, D_MODEL, TC), lambda j: (2 + j // per_shard, 0, j % per_shard)),
                  w(0), w(N_CB), b(0), b(N_CB)],
        out_specs=[col(0)] * 3,
        out_shape=[jax.ShapeDtypeStruct((SEQ, D_FF), F32)] * 2 + [jax.ShapeDtypeStruct((SEQ, D_FF), BF16)],
        compiler_params=_cp(("parallel",)),
    )(h3, wup_st, wup_st, conv_w, conv_w, conv_b, conv_b)


def _ffn_act_bwd(u_gate, u_val, df, wdown, conv_w, conv_b):
    col, w, b = _ffn_specs()
    both = lambda rows: pl.BlockSpec((2, rows, TC), lambda j: (0, 0, j))

    def body(ug_ref, uv_ref, df_ref, wd_ref, wg_ref, wv_ref, bg_ref, bv_ref, du_ref, dw_ref, db_ref):
        df = df_ref[...]
        for c in range(0, TC, LANES):
            cols = slice(c, c + LANES)
            da = _dot(df, wd_ref[cols, :], NT)
            ug, uv = ug_ref[:, cols], uv_ref[:, cols]
            wg, wv = wg_ref[:, cols], wv_ref[:, cols]
            gate = _conv3(ug, wg, bg_ref[:, cols])
            val = _conv3(uv, wv, bv_ref[:, cols])
            gelu, dgelu = _gelu_and_grad(gate)
            for h, (duc, uh, wh) in enumerate(((da * val * dgelu, ug, wg), (da * gelu, uv, wv))):
                up1, up2 = _shift_up(duc, 1), _shift_up(duc, 2)
                du_ref[h, :, cols] = (wh[2:3] * duc + wh[1:2] * up1 + wh[0:1] * up2).astype(BF16)
                db_ref[h, :, cols] = jnp.sum(duc, axis=0, keepdims=True)
                dw_ref[h, :, cols] = jnp.concatenate(
                    [jnp.sum(up2 * uh, axis=0, keepdims=True), jnp.sum(up1 * uh, axis=0, keepdims=True),
                     jnp.sum(duc * uh, axis=0, keepdims=True)], axis=0)

    return pl.pallas_call(
        body, name="ffn_act_bwd", grid=(N_CB,),
        in_specs=[col(0), col(0), _const((SEQ, D_MODEL)), pl.BlockSpec((TC, D_MODEL), lambda j: (j, 0)),
                  w(0), w(N_CB), b(0), b(N_CB)],
        out_specs=[both(SEQ), both(3), both(1)],
        out_shape=[jax.ShapeDtypeStruct((2, SEQ, D_FF), BF16), jax.ShapeDtypeStruct((2, 3, D_FF), F32),
                   jax.ShapeDtypeStruct((2, 1, D_FF), F32)],
        compiler_params=_cp(("parallel",)),
    )(u_gate, u_val, df, wdown, conv_w, conv_w, conv_b, conv_b)


def _t5_onehot():
    rel = (np.arange(BLOCK)[:, None] + BLOCK) - np.arange(2 * BLOCK)[None, :]
    n = np.maximum(rel, 0)
    max_exact = N_BUCKETS // 2
    large = max_exact + (np.log(np.maximum(n, 1).astype(np.float32) / np.float32(max_exact))
                         / np.float32(math.log(MAX_DISTANCE / max_exact))
                         * np.float32(N_BUCKETS - max_exact)).astype(np.int32)
    large = np.minimum(large, N_BUCKETS - 1)
    bucket = np.where(n < max_exact, n, large).reshape(-1)
    return (bucket[None, :] == np.arange(N_BUCKETS)[:, None]).astype(np.float32)


N_REL = BLOCK * 2 * BLOCK


def _bias_table(rel_bias_t, onehot):
    def body(rb_ref, oh_ref, o_ref):
        o_ref[...] = _dot_ind(rb_ref[...], oh_ref[...])

    return pl.pallas_call(
        body, name="bias_table", grid=(1,),
        in_specs=[_const((N_Q_HEADS, N_BUCKETS)), _const((N_BUCKETS, N_REL))],
        out_specs=_const((N_Q_HEADS, N_REL)),
        out_shape=jax.ShapeDtypeStruct((N_Q_HEADS, N_REL), F32),
        compiler_params=_cp(("arbitrary",)),
    )(rel_bias_t, onehot)


def _bias_table_bwd(dbias, onehot):
    def body(db_ref, oh_ref, o_ref):
        acc = None
        for part in _split(db_ref[...], 3):
            t = _dot(part, oh_ref[...], NT)
            acc = t if acc is None else acc + t
        o_ref[...] = acc

    return pl.pallas_call(
        body, name="bias_table_bwd", grid=(1,),
        in_specs=[_const((N_Q_HEADS, N_REL)), _const((N_BUCKETS, N_REL))],
        out_specs=_const((N_Q_HEADS, N_BUCKETS)),
        out_shape=jax.ShapeDtypeStruct((N_Q_HEADS, N_BUCKETS), F32),
        compiler_params=_cp(("arbitrary",)),
    )(dbias, onehot)


def _attn_pieces(n, q, kvp, kvc, bias_ref, sinks_ref, hk):
    qi = lax.broadcasted_iota(jnp.int32, (BLOCK, 2 * BLOCK), 0)
    kj = lax.broadcasted_iota(jnp.int32, (BLOCK, 2 * BLOCK), 1)
    rel = qi + BLOCK - kj
    first_key = jnp.where(n > 0, 0, BLOCK)
    ok = jnp.where(rel >= 0, jnp.where(rel < BLOCK, jnp.where(kj >= first_key, 1.0, 0.0), 0.0), 0.0)
    ok4 = jnp.concatenate([ok] * Q_PER_KV, axis=0) > 0.5
    c0 = hk * HEAD_DIM
    kcat = jnp.concatenate([kvp[:, c0:c0 + HEAD_DIM], kvc[:, c0:c0 + HEAD_DIM]], axis=0).astype(BF16)
    vcat = jnp.concatenate([kvp[:, D_KV + c0:D_KV + c0 + HEAD_DIM], kvc[:, D_KV + c0:D_KV + c0 + HEAD_DIM]],
                           axis=0).astype(BF16)
    q0 = hk * Q_PER_KV * HEAD_DIM
    qs = jnp.concatenate([q[:, q0 + g * HEAD_DIM:q0 + (g + 1) * HEAD_DIM] for g in range(Q_PER_KV)],
                         axis=0).astype(BF16)
    s = _dot(qs, kcat, NT) * (HEAD_DIM ** -0.5) + bias_ref[hk]
    s = jnp.where(ok4, s, NEG_INF)
    row = lax.broadcasted_iota(jnp.int32, (Q_PER_KV * BLOCK, 1), 0)
    sink = jnp.zeros((Q_PER_KV * BLOCK, 1), F32)
    for g in range(Q_PER_KV):
        sink = jnp.where((row >> 7) == g, sinks_ref[hk * Q_PER_KV + g], sink)
    m = jnp.maximum(jnp.max(s, axis=-1, keepdims=True), sink)
    p = jnp.exp(s - m)
    es = jnp.exp(sink - m)
    inv = 1.0 / (jnp.sum(p, axis=-1, keepdims=True) + es)
    return qs, kcat, vcat, p * inv, es * inv


def _attn_in_specs():
    return [pl.BlockSpec((BLOCK, D_ATTN), lambda n: (n, 0)),
            pl.BlockSpec((BLOCK, 2 * D_KV), lambda n: (jnp.maximum(n - 1, 0), D_ATTN // (2 * D_KV))),
            pl.BlockSpec((BLOCK, 2 * D_KV), lambda n: (n, D_ATTN // (2 * D_KV))),
            _const((N_KV_HEADS, Q_PER_KV * BLOCK, 2 * BLOCK)),
            pl.BlockSpec(memory_space=pltpu.SMEM)]


def _unstack_heads(t):
    return jnp.concatenate([t[g * BLOCK:(g + 1) * BLOCK] for g in range(Q_PER_KV)], axis=1)


def _attn_fwd(proj, bias, sinks):
    def body(q_ref, kvp_ref, kvc_ref, bias_ref, sinks_ref, o_ref):
        n = pl.program_id(0)
        q, kvp, kvc = q_ref[...], kvp_ref[...], kvc_ref[...]
        outs = []
        for hk in range(N_KV_HEADS):
            _, _, vcat, probs, _ = _attn_pieces(n, q, kvp, kvc, bias_ref, sinks_ref, hk)
            outs.append(_unstack_heads(_dot(probs.astype(BF16), vcat)))
        o_ref[...] = jnp.concatenate(outs, axis=1)

    return pl.pallas_call(
        body, name="attn_fwd", grid=(SEQ // BLOCK,),
        in_specs=_attn_in_specs(),
        out_specs=pl.BlockSpec((BLOCK, D_ATTN), lambda n: (n, 0)),
        out_shape=jax.ShapeDtypeStruct((SEQ, D_ATTN), F32),
        compiler_params=_cp(("parallel",)),
    )(proj, proj, proj, bias, sinks)


def _attn_bwd(proj, bias, sinks, dcat):
    nb = SEQ // BLOCK

    def body(q_ref, kvp_ref, kvc_ref, bias_ref, sinks_ref, do_ref, dq_ref, dkv_ref, dbias_ref, dsink_ref, dsacc):
        n = pl.program_id(0)

        @pl.when(n == 0)
        def _():
            dkv_ref[...] = jnp.zeros_like(dkv_ref)
            dbias_ref[...] = jnp.zeros_like(dbias_ref)
            dsacc[...] = jnp.zeros_like(dsacc)

        q, kvp, kvc = q_ref[...], kvp_ref[...], kvc_ref[...]
        do_all = do_ref[...]
        dqs, dks, dvs = [], [], []
        for hk in range(N_KV_HEADS):
            qs, kcat, vcat, probs, psink = _attn_pieces(n, q, kvp, kvc, bias_ref, sinks_ref, hk)
            q0 = hk * Q_PER_KV * HEAD_DIM
            do = jnp.concatenate([do_all[:, q0 + g * HEAD_DIM:q0 + (g + 1) * HEAD_DIM] for g in range(Q_PER_KV)],
                                 axis=0).astype(BF16)
            dprobs = _dot(do, vcat, NT)
            dvs.append(_dot(probs.astype(BF16), do, TN))
            rowdot = jnp.sum(probs * dprobs, axis=-1, keepdims=True)
            ds = probs * (dprobs - rowdot)
            dsacc[hk] += -psink * rowdot
            dbias_ref[hk] += ds
            dsb = (ds * (HEAD_DIM ** -0.5)).astype(BF16)
            dqs.append(_unstack_heads(_dot(dsb, kcat)))
            dks.append(_dot(dsb, qs, TN))
        dq_ref[...] = jnp.concatenate(dqs, axis=1)
        upd = jnp.concatenate(dks + dvs, axis=1)
        cur = pl.multiple_of(n * BLOCK, BLOCK)
        dkv_ref[pl.ds(cur, BLOCK), :] += upd[BLOCK:]

        @pl.when(n > 0)
        def _():
            prev = pl.multiple_of((n - 1) * BLOCK, BLOCK)
            dkv_ref[pl.ds(prev, BLOCK), :] += upd[:BLOCK]

        @pl.when(n == nb - 1)
        def _():
            for hk in range(N_KV_HEADS):
                for g in range(Q_PER_KV):
                    tot = jnp.sum(dsacc[hk, g * BLOCK:(g + 1) * BLOCK, :], axis=0, keepdims=True)
                    h = hk * Q_PER_KV + g
                    dsink_ref[h:h + 1, :] = jnp.broadcast_to(tot, (1, LANES))

    return pl.pallas_call(
        body, name="attn_bwd", grid=(nb,),
        in_specs=_attn_in_specs() + [pl.BlockSpec((BLOCK, D_ATTN), lambda n: (n, 0))],
        out_specs=[pl.BlockSpec((BLOCK, D_ATTN), lambda n: (n, 0)), _const((SEQ, 2 * D_KV)),
                   _const((N_KV_HEADS, Q_PER_KV * BLOCK, 2 * BLOCK)), _const((N_Q_HEADS, LANES))],
        out_shape=[jax.ShapeDtypeStruct((SEQ, D_ATTN), F32), jax.ShapeDtypeStruct((SEQ, 2 * D_KV), F32),
                   jax.ShapeDtypeStruct((N_KV_HEADS, Q_PER_KV * BLOCK, 2 * BLOCK), F32),
                   jax.ShapeDtypeStruct((N_Q_HEADS, LANES), F32)],
        scratch_shapes=[pltpu.VMEM((N_KV_HEADS, Q_PER_KV * BLOCK, 1), F32)],
        compiler_params=_cp(("arbitrary",)),
    )(proj, proj, proj, bias, sinks, dcat)


@jax.custom_vjp
def _head_sum(x):
    ones = _head_ones(LANES)
    return jnp.concatenate([_dot_ind(x[:, c:c + LANES], ones, 2) for c in range(0, x.shape[-1], LANES)], axis=1)


_head_sum.defvjp(lambda x: (_head_sum(x), None), lambda _, ct: (_head_sum(ct),))


@jax.custom_vjp
def _bdot(a, w):
    return _dot(a.astype(BF16), w.astype(BF16))


def _bdot_bwd(res, ct):
    a, w = res
    ctb = ct.astype(BF16)
    return _dot(ctb, w.astype(BF16), NT), _dot(a.astype(BF16), ctb, TN)


_bdot.defvjp(lambda a, w: (_bdot(a, w), (a, w)), _bdot_bwd)


def _sigmoid(x):
    return 0.5 * (jnp.tanh(0.5 * x) + 1.0)


def _softplus(x):
    return jnp.maximum(x, 0.0) + jnp.log(1.0 + jnp.exp(-jnp.abs(x)))


def _rwkv_core(r, k, v, zwa, zg, w0, wdu, a0, wiu, wgu, k_k, k_a):
    w_log = -_softplus(-(w0 + _bdot(jnp.tanh(zwa), wdu))) - 0.5
    decay = jnp.exp(-jnp.exp(w_log))
    a = _sigmoid(a0 + _bdot(zwa, wiu))
    g = _bdot(_sigmoid(zg), wgu)
    kk = k * k_k
    kk = kk / jnp.maximum(jnp.sqrt(_head_sum(kk * kk)), 1e-12)
    k2 = k * (1.0 + (a - 1.0) * k_a)
    return r, decay, k2, v, -kk, kk * a, g


def _rwkv_out(o, r, k2, v, g, lng, lnb, rk):
    mu = _head_sum(o) * (1.0 / HEAD_DIM)
    d = o - mu
    var = _head_sum(d * d) * (1.0 / HEAD_DIM)
    on = d * lax.rsqrt(var + GN_EPS) * lng + lnb
    bonus = _head_sum(r * k2 * rk) * v
    return (on + bonus) * g


P_SPLITS = (0, 512, 1024, 1536, 1664, 1792)
N_PREP_PARAMS = 7
HALO = 8


def _shifted_pieces(i, p_ref, halo_ref, mix_ref):
    p = p_ref[:, P_OFF:]
    prev_row = halo_ref[HALO - 1:HALO, P_OFF:] * jnp.where(i > 0, 1.0, 0.0)
    row = lax.broadcasted_iota(jnp.int32, p.shape, 0)
    pprev = jnp.where(row == 0, prev_row, pltpu.roll(p, 1, 0))
    delta = pprev - p
    ps = p + delta * mix_ref[...]
    return [ps[:, a:b] for a, b in zip(P_SPLITS[:-1], P_SPLITS[1:])], delta


def _prep_in_specs():
    return [_rows(TR, D_IN),
            pl.BlockSpec((HALO, D_IN), lambda i: (jnp.maximum(i * (TR // HALO) - 1, 0), 0)),
            _const((1, RWKV_COLS)), _const((1, D_RWKV)), _const((LANES, D_RWKV)), _const((1, D_RWKV)),
            _const((LANES, D_RWKV)), _const((LANES, D_RWKV)), _const((1, D_RWKV)), _const((1, D_RWKV))]


def _rwkv_prep(proj, mix, prm):
    def body(p_ref, halo_ref, mix_ref, *refs):
        prm_refs, outs = refs[:N_PREP_PARAMS], refs[N_PREP_PARAMS:]
        pieces, _ = _shifted_pieces(pl.program_id(0), p_ref, halo_ref, mix_ref)
        vals = _rwkv_core(*pieces, *[t[...] for t in prm_refs])
        for ref, val in zip(outs, vals):
            ref[...] = val

    return pl.pallas_call(
        body, name="rwkv_prep", grid=(SEQ // TR,),
        in_specs=_prep_in_specs(),
        out_specs=[_rows(TR, D_RWKV)] * 7,
        out_shape=[jax.ShapeDtypeStruct((SEQ, D_RWKV), F32)] * 7,
        compiler_params=_cp(("parallel",)),
    )(proj, proj, mix, *prm)


def _rwkv_prep_bwd(proj, mix, prm, cts):
    def body(p_ref, halo_ref, mix_ref, *refs):
        i = pl.program_id(0)
        prm_refs = refs[:N_PREP_PARAMS]
        ct_refs = refs[N_PREP_PARAMS:N_PREP_PARAMS + 10]
        dps_ref, dmix_ref = refs[N_PREP_PARAMS + 10:N_PREP_PARAMS + 12]
        dprm_refs = refs[N_PREP_PARAMS + 12:]
        pieces, delta = _shifted_pieces(i, p_ref, halo_ref, mix_ref)
        _, vjp = jax.vjp(_rwkv_core, *pieces, *[t[...] for t in prm_refs])
        dr1, dr2, dw, dk1, dk2, dv1, dv2, dkkn, db, dg = [t[...] for t in ct_refs]
        grads = vjp((dr1 + dr2, dw, dk1 + dk2, dv1 + dv2, dkkn, db, dg))
        dps = jnp.concatenate(grads[:5], axis=1)
        dps_ref[...] = dps

        @pl.when(i == 0)
        def _():
            dmix_ref[...] = jnp.zeros_like(dmix_ref)
            for ref in dprm_refs:
                ref[...] = jnp.zeros_like(ref)

        dmix_ref[...] += jnp.sum(dps * delta, axis=0, keepdims=True)
        for ref, gval in zip(dprm_refs, grads[5:]):
            ref[...] += gval

    prm_shapes = [(1, D_RWKV), (LANES, D_RWKV), (1, D_RWKV), (LANES, D_RWKV), (LANES, D_RWKV), (1, D_RWKV), (1, D_RWKV)]
    return pl.pallas_call(
        body, name="rwkv_prep_bwd", grid=(SEQ // TR,),
        in_specs=_prep_in_specs() + [_rows(TR, D_RWKV)] * 10,
        out_specs=[_rows(TR, RWKV_COLS), _const((1, RWKV_COLS))] + [_const(s) for s in prm_shapes],
        out_shape=[jax.ShapeDtypeStruct((SEQ, RWKV_COLS), F32), jax.ShapeDtypeStruct((1, RWKV_COLS), F32)]
        + [jax.ShapeDtypeStruct(s, F32) for s in prm_shapes],
        compiler_params=_cp(("arbitrary",)),
    )(proj, proj, mix, *prm, *cts)


def _rwkv_post(o, r, k2, v, g, lng, lnb, rk, attn):
    def body(o_ref, r_ref, k_ref, v_ref, g_ref, lng_ref, lnb_ref, rk_ref, attn_ref, cat_ref):
        rw = _rwkv_out(*[t[...] for t in (o_ref, r_ref, k_ref, v_ref, g_ref, lng_ref, lnb_ref, rk_ref)])
        cat_ref[...] = jnp.concatenate([attn_ref[...], rw], axis=1).astype(BF16)

    return pl.pallas_call(
        body, name="rwkv_post", grid=(SEQ // TR,),
        in_specs=[_rows(TR, D_RWKV)] * 5 + [_const((1, D_RWKV))] * 3 + [_rows(TR, D_ATTN)],
        out_specs=_rows(TR, D_MODEL),
        out_shape=jax.ShapeDtypeStruct((SEQ, D_MODEL), BF16),
        compiler_params=_cp(("parallel",)),
    )(o, r, k2, v, g, lng, lnb, rk, attn)


def _rwkv_post_bwd(o, r, k2, v, g, lng, lnb, rk, dcat):
    def body(o_ref, r_ref, k_ref, v_ref, g_ref, lng_ref, lnb_ref, rk_ref, dcat_ref,
             do_ref, dr_ref, dk_ref, dv_ref, dg_ref, dlng_ref, dlnb_ref, drk_ref):
        i = pl.program_id(0)
        args = [t[...] for t in (o_ref, r_ref, k_ref, v_ref, g_ref, lng_ref, lnb_ref, rk_ref)]
        _, vjp = jax.vjp(_rwkv_out, *args)
        grads = vjp(dcat_ref[:, D_ATTN:])
        for ref, gval in zip((do_ref, dr_ref, dk_ref, dv_ref, dg_ref), grads[:5]):
            ref[...] = gval

        @pl.when(i == 0)
        def _():
            for ref in (dlng_ref, dlnb_ref, drk_ref):
                ref[...] = jnp.zeros_like(ref)

        for ref, gval in zip((dlng_ref, dlnb_ref, drk_ref), grads[5:]):
            ref[...] += gval

    return pl.pallas_call(
        body, name="rwkv_post_bwd", grid=(SEQ // TR,),
        in_specs=[_rows(TR, D_RWKV)] * 5 + [_const((1, D_RWKV))] * 3 + [_rows(TR, D_MODEL)],
        out_specs=[_rows(TR, D_RWKV)] * 5 + [_const((1, D_RWKV))] * 3,
        out_shape=[jax.ShapeDtypeStruct((SEQ, D_RWKV), F32)] * 5 + [jax.ShapeDtypeStruct((1, D_RWKV), F32)] * 3,
        compiler_params=_cp(("arbitrary",)),
    )(o, r, k2, v, g, lng, lnb, rk, dcat)


def _assemble_dproj(dq, dkv, dps, mix):
    last = SEQ // HALO - 1

    def body(dq_ref, dkv_ref, dps_ref, nxt_ref, mix_ref, o_ref):
        i = pl.program_id(0)
        dps = dps_ref[...]
        mixv = mix_ref[...]
        nxt_row = nxt_ref[0:1, :] * jnp.where(i < SEQ // TR - 1, 1.0, 0.0)
        row = lax.broadcasted_iota(jnp.int32, dps.shape, 0)
        up = jnp.where(row == TR - 1, nxt_row, pltpu.roll(dps, TR - 1, 0))
        dp = dps * (1.0 - mixv) + up * mixv
        o_ref[...] = jnp.concatenate([dq_ref[...], dkv_ref[...], dp], axis=1).astype(BF16)

    return pl.pallas_call(
        body, name="assemble_dproj", grid=(SEQ // TR,),
        in_specs=[_rows(TR, D_ATTN), _rows(TR, 2 * D_KV), _rows(TR, RWKV_COLS),
                  pl.BlockSpec((HALO, RWKV_COLS), lambda i: (jnp.minimum((i + 1) * (TR // HALO), last), 0)),
                  _const((1, RWKV_COLS))],
        out_specs=_rows(TR, D_IN),
        out_shape=jax.ShapeDtypeStruct((SEQ, D_IN), BF16),
        compiler_params=_cp(("parallel",)),
    )(dq, dkv, dps, dps, mix)


N_PAIR = D_RWKV // LANES
CHUNK = 64
N_CHUNK = SEQ // CHUNK
GROUP = 8
STATE = (N_PAIR, HEAD_DIM, LANES)


def _lane_sums(lhs_tiles, ones2):
    out = _dot(jnp.concatenate(lhs_tiles, axis=0), ones2)
    return [out[i * HEAD_DIM:(i + 1) * HEAD_DIM] for i in range(len(lhs_tiles))]


def _seg_sum(xs, ones2):
    return _lane_sums([jnp.concatenate(_split(x, 2), axis=1) for x in xs], ones2)


def _seg_sum_rows(xs, ones2):
    out = _dot(jnp.concatenate(_split(jnp.concatenate(xs, axis=0), 2), axis=1), ones2)
    return [out[i * GROUP:(i + 1) * GROUP] for i in range(len(xs))]


def _col_form(rows, diag, ones2):
    zero = jnp.zeros((HEAD_DIM, LANES), BF16)
    tiles = []
    for row in rows:
        hi = row.astype(BF16)
        lo = (row - hi.astype(F32)).astype(BF16)
        tiles.append(jnp.concatenate(
            [jnp.where(diag, jnp.broadcast_to(part, (HEAD_DIM, LANES)), zero) for part in (hi, lo)], axis=1))
    return _lane_sums(tiles, ones2)


def _scan_consts():
    ones2 = jnp.concatenate([_head_ones(LANES)] * 2, axis=0)
    sub = lax.broadcasted_iota(jnp.int32, (HEAD_DIM, LANES), 0)
    lane_in_head = lax.broadcasted_iota(jnp.int32, (HEAD_DIM, LANES), 1) & (HEAD_DIM - 1)
    return ones2, lane_in_head == sub, lane_in_head


def _rows_of_columns(tile):
    t = tile.T
    return jnp.concatenate([t[:CHUNK], t[HEAD_DIM:HEAD_DIM + CHUNK]], axis=1)


def _pair(j):
    return slice(j * LANES, (j + 1) * LANES)


def _scan_fwd(r, w, k, v, kkn, b):
    def body(r_ref, w_ref, k_ref, v_ref, kkn_ref, b_ref, o_ref, st_ref, sa_ref, s_scr):
        c = pl.program_id(0)
        ones2, diag, lane_in_head = _scan_consts()

        @pl.when(c == 0)
        def _():
            s_scr[...] = jnp.zeros_like(s_scr)

        def group(gi, carry):
            row0 = pl.multiple_of(gi * GROUP, GROUP)
            states, ocols = list(carry[:N_PAIR]), list(carry[N_PAIR:])
            tiles = [[t[pl.ds(row0, GROUP), _pair(j)] for t in (r_ref, w_ref, k_ref, v_ref, kkn_ref, b_ref)]
                     for j in range(N_PAIR)]
            def row(j, name, u):
                return tiles[j]["rwkvnb".index(name)][u:u + 1]

            def emit_out(u, after):
                outs = _seg_sum([s[j] * row(j, "r", u + d) for d, s in enumerate(after) for j in range(N_PAIR)], ones2)
                for d in range(2):
                    here = lane_in_head == gi * GROUP + u + d
                    for j in range(N_PAIR):
                        ocols[j] = jnp.where(here, outs[d * N_PAIR + j], ocols[j])

            def vcols_of(u):
                cols = _col_form([row(j, "v", u + d) for d in range(2) for j in range(N_PAIR)], diag, ones2)
                return cols[:N_PAIR], cols[N_PAIR:]

            n_next = [pltpu.roll(tiles[j][4], GROUP - 1, 0) for j in range(N_PAIR)]
            dots = _seg_sum_rows([tiles[j][5] * n_next[j] for j in range(N_PAIR)]
                                 + [tiles[j][2] * n_next[j] for j in range(N_PAIR)], ones2)
            b_n, k_n = dots[:N_PAIR], dots[N_PAIR:]
            w_n = [tiles[j][1] * n_next[j] for j in range(N_PAIR)]

            vcols = vcols_of(0)
            after = None
            for u in range(0, GROUP, 2):
                prods = _seg_sum([states[j] * row(j, "n", u) for j in range(N_PAIR)]
                                 + [states[j] * w_n[j][u:u + 1] for j in range(N_PAIR)], ones2)
                if after is not None:
                    emit_out(u - 2, after)
                nxt = vcols_of(u + 2) if u + 2 < GROUP else None
                first, second = [], []
                for j in range(N_PAIR):
                    sa1 = prods[j]
                    sa2 = prods[N_PAIR + j] + sa1 * b_n[j][u:u + 1] + vcols[0][j] * k_n[j][u:u + 1]
                    s1 = states[j] * row(j, "w", u) + sa1 * row(j, "b", u) + vcols[0][j] * row(j, "k", u)
                    s2 = s1 * row(j, "w", u + 1) + sa2 * row(j, "b", u + 1) + vcols[1][j] * row(j, "k", u + 1)
                    st_ref[row0 + u, j] = s1
                    sa_ref[row0 + u, j] = sa1
                    st_ref[row0 + u + 1, j] = s2
                    sa_ref[row0 + u + 1, j] = sa2
                    first.append(s1)
                    second.append(s2)
                    states[j] = s2
                after, vcols = (first, second), nxt
            emit_out(GROUP - 2, after)
            return tuple(states + ocols)

        zero = jnp.zeros((HEAD_DIM, LANES), F32)
        fin = lax.fori_loop(0, CHUNK // GROUP, group, tuple(s_scr[j] for j in range(N_PAIR)) + (zero,) * N_PAIR)
        for j in range(N_PAIR):
            s_scr[j] = fin[j]
            o_ref[:, _pair(j)] = _rows_of_columns(fin[N_PAIR + j])

    blk = pl.BlockSpec((CHUNK, D_RWKV), lambda c: (c, 0))
    per_step = pl.BlockSpec((CHUNK,) + STATE, lambda c: (c, 0, 0, 0))
    return pl.pallas_call(
        body, name="rwkv_scan_fwd", grid=(N_CHUNK,),
        in_specs=[blk] * 6,
        out_specs=[blk, per_step, per_step],
        out_shape=[jax.ShapeDtypeStruct((SEQ, D_RWKV), F32)] + [jax.ShapeDtypeStruct((SEQ,) + STATE, F32)] * 2,
        scratch_shapes=[pltpu.VMEM(STATE, F32)],
        compiler_params=_cp(("arbitrary",)),
    )(r, w, k, v, kkn, b)


def _scan_bwd(r, w, k, v, kkn, b, do, states, sas, ds_in, prev, name, first_chunk, n_chunks):
    top = first_chunk + n_chunks - 1

    def body(r_ref, w_ref, k_ref, v_ref, kkn_ref, b_ref, do_ref, st_ref, before_ref, sa_ref, ds_in_ref, *rest):
        dr_ref, dw_ref, dk_ref, dv_ref, dkkn_ref, db_ref, ds_out_ref, ds_scr = rest[-8:]
        i = pl.program_id(0)
        ones2, diag, lane_in_head = _scan_consts()

        @pl.when(i == 0)
        def _():
            ds_scr[...] = ds_in_ref[...]

        entry = [before_ref[0, j] * jnp.where(i < top, 1.0, 0.0) for j in range(N_PAIR)]

        def reverse(gr, carry):
            gi = CHUNK // GROUP - 1 - gr
            row0 = pl.multiple_of(gi * GROUP, GROUP)
            dstates, dvcols = list(carry[:N_PAIR]), list(carry[N_PAIR:])
            tiles = [[t[pl.ds(row0, GROUP), _pair(j)]
                      for t in (r_ref, w_ref, k_ref, v_ref, kkn_ref, b_ref, do_ref)] for j in range(N_PAIR)]
            rows = [[[None] * GROUP for _ in range(5)] for _ in range(N_PAIR)]

            def row(j, name, u):
                return tiles[j]["rwkvnbd".index(name)][u:u + 1]

            def cols_of(u):
                cols = _col_form([row(j, name, u - d) for d in range(2) for name in "dv" for j in range(N_PAIR)],
                                 diag, ones2)
                return [[(cols[(2 * d) * N_PAIR + j], cols[(2 * d + 1) * N_PAIR + j]) for j in range(N_PAIR)]
                        for d in range(2)]

            def emit_dv(u, dsps):
                outs = _seg_sum([dsp[j] * row(j, "k", u - d) for d, dsp in enumerate(dsps) for j in range(N_PAIR)], ones2)
                for d in range(2):
                    here = lane_in_head == gi * GROUP + u - d
                    for j in range(N_PAIR):
                        dvcols[j] = jnp.where(here, outs[d * N_PAIR + j], dvcols[j])

            b_prev = [pltpu.roll(tiles[j][5], 1, 0) for j in range(N_PAIR)]
            dots = _seg_sum_rows([tiles[j][4] * b_prev[j] for j in range(N_PAIR)]
                                 + [tiles[j][0] * tiles[j][5] for j in range(N_PAIR)], ones2)
            n_b, r_b = dots[:N_PAIR], dots[N_PAIR:]
            w_b = [tiles[j][1] * b_prev[j] for j in range(N_PAIR)]

            def outputs(u, j, dsp, dsa, docol, vcol):
                tl = gi * GROUP + u
                if u > 0:
                    s_prev = st_ref[tl - 1, j]
                else:
                    s_prev = jnp.where(gi == 0, entry[j], st_ref[jnp.maximum(tl - 1, 0), j])
                rows[j][0][u] = jnp.sum(st_ref[tl, j] * docol, axis=0, keepdims=True)
                rows[j][1][u] = jnp.sum(dsp * s_prev, axis=0, keepdims=True)
                rows[j][2][u] = jnp.sum(dsp * vcol, axis=0, keepdims=True)
                rows[j][3][u] = jnp.sum(s_prev * dsa, axis=0, keepdims=True)
                rows[j][4][u] = jnp.sum(dsp * sa_ref[tl, j], axis=0, keepdims=True)

            cols = cols_of(GROUP - 1)
            before = None
            for u in range(GROUP - 1, 0, -2):
                dsp1 = [dstates[j] + cols[0][j][0] * row(j, "r", u) for j in range(N_PAIR)]
                prods = _seg_sum([dsp1[j] * row(j, "b", u) for j in range(N_PAIR)]
                                 + [dsp1[j] * w_b[j][u:u + 1] for j in range(N_PAIR)], ones2)
                if before is not None:
                    emit_dv(u + 2, before)
                nxt = cols_of(u - 2) if u >= 2 else None
                dsp2 = []
                for j in range(N_PAIR):
                    dsa1 = prods[j]
                    dsa2 = prods[N_PAIR + j] + dsa1 * n_b[j][u:u + 1] + cols[1][j][0] * r_b[j][u - 1:u]
                    mid = dsp1[j] * row(j, "w", u) + dsa1 * row(j, "n", u) + cols[1][j][0] * row(j, "r", u - 1)
                    outputs(u, j, dsp1[j], dsa1, *cols[0][j])
                    outputs(u - 1, j, mid, dsa2, *cols[1][j])
                    dstates[j] = mid * row(j, "w", u - 1) + dsa2 * row(j, "n", u - 1)
                    dsp2.append(mid)
                before, cols = (dsp1, dsp2), nxt
            emit_dv(1, before)
            for j in range(N_PAIR):
                for ref, rr in zip((dr_ref, dw_ref, dk_ref, dkkn_ref, db_ref), rows[j]):
                    ref[pl.ds(row0, GROUP), _pair(j)] = jnp.concatenate(rr, axis=0)
            return tuple(dstates + dvcols)

        zero = jnp.zeros((HEAD_DIM, LANES), F32)
        dfin = lax.fori_loop(0, CHUNK // GROUP, reverse, tuple(ds_scr[j] for j in range(N_PAIR)) + (zero,) * N_PAIR)
        for j in range(N_PAIR):
            ds_scr[j] = dfin[j]
            dv_ref[:, _pair(j)] = _rows_of_columns(dfin[N_PAIR + j])

        @pl.when(i == n_chunks - 1)
        def _():
            ds_out_ref[...] = ds_scr[...]

    blk = pl.BlockSpec((CHUNK, D_RWKV), lambda i: (top - i, 0))
    per_step = pl.BlockSpec((CHUNK,) + STATE, lambda i: (top - i, 0, 0, 0))
    step_before = pl.BlockSpec((1,) + STATE, lambda i: (jnp.maximum((top - i) * CHUNK - 1, 0), 0, 0, 0))
    prev = [] if prev is None else list(prev)
    outs = pl.pallas_call(
        body, name=name, grid=(n_chunks,),
        in_specs=[blk] * 7 + [per_step, step_before, per_step, _const(STATE)] + [ANY] * len(prev),
        out_specs=[blk] * 6 + [_const(STATE)],
        out_shape=[jax.ShapeDtypeStruct((SEQ, D_RWKV), F32)] * 6 + [jax.ShapeDtypeStruct(STATE, F32)],
        scratch_shapes=[pltpu.VMEM(STATE, F32)],
        input_output_aliases={11 + t: t for t in range(len(prev))},
        compiler_params=_cp(("arbitrary",)),
    )(r, w, k, v, kkn, b, do, states, states, sas, ds_in, *prev)
    return outs[:6], outs[6]


def _stacked(rows, cols, pick):
    return pl.BlockSpec((None, rows, cols), pick)


def _local_step(x, target, sm, win_st):
    def tied(t, token):
        return t if token is None else t + token[0:1, 0:1].reshape((1,) * t.ndim)

    zpad = jnp.zeros((LORA_DECAY, D_RWKV), F32)
    prm = [sm["w0"], jnp.concatenate([sm["w_decay_up"], zpad], axis=0), sm["a0"],
           jnp.concatenate([zpad, sm["w_iclr_up"]], axis=0), sm["w_gate_up"], sm["k_k"], sm["k_a"]]
    mix = sm["rwkv_shift_mix"]
    onehot = jnp.asarray(_t5_onehot(), BF16)
    sinks = sm["sinks"].reshape(N_Q_HEADS)
    lng, lnb, rk = sm["ln_x_g"], sm["ln_x_b"], sm["r_k"].reshape(1, D_RWKV)

    h1 = _norm_cast(x, sm["norm_mix_pre"], "norm_in")
    proj = _matmul(h1, win_st, "nn", "proj", m=SEQ, n=D_IN, k=D_MODEL, tm=SEQ, tn=640,
                   b_spec=_stacked(D_MODEL, 640, lambda i, j: (j, 0, 0)))
    bias = _bias_table(sm["rel_bias"].T, onehot).reshape(N_KV_HEADS, Q_PER_KV * BLOCK, 2 * BLOCK)
    attn = _attn_fwd(proj, bias, sinks)
    r, w, k2, v, kkn, b, g = _rwkv_prep(proj, mix, prm)
    o, states, sas = _scan_fwd(r, w, k2, v, kkn, b)
    wout, wup_st, wdown = yield ("rest_weights", o)
    cat = _rwkv_post(o, r, k2, v, g, lng, lnb, rk, attn)
    mixo = _matmul(cat, wout, "nn", "out_proj", m=SEQ, n=D_MODEL, k=D_MODEL, tm=SEQ, tn=512)
    x2, h3 = _mix_norm(x, mixo, sm["norm_mix_post"], sm["norm_ffn_pre"])
    u_gate, u_val, act = _ffn_up_act(h3, wup_st, sm["conv_w"], sm["conv_b"])
    f = _matmul(act, wdown, "nn", "ffn_down", m=SEQ, n=D_MODEL, k=D_FF, tm=1024, tn=512)
    loss, dy, df, d_g4 = _loss_head(x2, f, sm["norm_ffn_post"], target)

    d_wdown = _matmul(act, df, "tn", "d_wdown", m=D_FF, n=D_MODEL, k=SEQ, tm=512, tn=D_MODEL)
    du, d_convw, d_convb = _ffn_act_bwd(u_gate, u_val, df, wdown, sm["conv_w"], sm["conv_b"])
    d_convw = d_convw.transpose(1, 0, 2).reshape(3, 2 * D_FF)
    d_convb = d_convb.reshape(1, 2 * D_FF)
    dh3 = _matmul_nt_shards(du, wup_st, "d_h3", m=SEQ, n=D_MODEL, tm=512, tn=512,
                            a_spec=pl.BlockSpec((2, 512, D_FF), lambda i, j: (0, i, 0)),
                            a_piece=lambda ref, s: ref[s // 2, :, (s % 2) * 2048:(s % 2 + 1) * 2048])
    d_wup = _matmul(h3, du, "tn", "d_wup", m=D_MODEL, n=2 * D_FF, k=SEQ, tm=D_MODEL, tn=512,
                    b_spec=pl.BlockSpec((None, SEQ, 512), lambda i, j: (j // 8, 0, j % 8)),
                    out=((N_CHIPS, D_MODEL, 2048), _stacked(D_MODEL, 512, lambda i, j: (j // 4, 0, j % 4))))
    dx2, dmix, d_g2, d_g3 = _mid_bwd(x2, mixo, dy, dh3, sm["norm_mix_post"], sm["norm_ffn_pre"])
    dcat = _matmul(dmix, wout, "nt", "d_cat", m=SEQ, n=D_MODEL, k=D_MODEL, tm=SEQ, tn=512)
    d_wout = _matmul(cat, dmix, "tn", "d_wout", m=D_MODEL, n=D_MODEL, k=SEQ, tm=512, tn=D_MODEL)
    token = yield ("grads_a", (d_wdown, d_wup, d_wout))
    do, dr_p, dk_p, dv_p, dg, d_lng, d_lnb, d_rk = _rwkv_post_bwd(o, r, k2, v, g, lng, tied(lnb, token), rk, dcat)
    half = N_CHUNK // 2
    ds_end = jnp.zeros(STATE, F32)
    late, ds_mid = _scan_bwd(r, w, k2, v, kkn, b, do, states, sas, ds_end, None, "rwkv_scan_bwd_late", half, half)
    token = yield ("seam_1", ds_mid)
    scan_cts, ds_first = _scan_bwd(r, w, k2, v, kkn, b, do, states, sas, tied(ds_mid, token), late,
                                   "rwkv_scan_bwd_early", 0, half)
    dr_s, dw_s, dk_s, dv_s, dkkn_s, db_s = scan_cts
    token = yield ("seam_2", ds_first)
    prep_grads = _rwkv_prep_bwd(proj, tied(mix, token), prm,
                                (dr_s, dr_p, dw_s, dk_s, dk_p, dv_s, dv_p, dkkn_s, db_s, dg))
    dps, d_mix, d_w0, d_wdu, d_a0, d_wiu, d_wgu, d_kk, d_ka = prep_grads
    dq, dkv, dbias, dsink = _attn_bwd(proj, bias, sinks, dcat)
    d_relb = _bias_table_bwd(dbias.reshape(N_Q_HEADS, N_REL), onehot).T
    dproj = _assemble_dproj(dq, dkv, dps, mix)
    d_win = _matmul(h1, dproj, "tn", "d_win", m=D_MODEL, n=D_IN, k=SEQ, tm=D_MODEL, tn=640,
                    out=((N_CHIPS, D_MODEL, 640), _stacked(D_MODEL, 640, lambda i, j: (j, 0, 0))))
    token = yield ("grads_b", d_win)
    dh1 = _matmul_nt_shards(dproj, win_st, "d_h1", m=SEQ, n=D_MODEL, tm=1024, tn=D_MODEL,
                            a_spec=pl.BlockSpec((1024, D_IN), lambda i, j: (i, 0)),
                            a_piece=lambda ref, s: ref[:, s * 640:(s + 1) * 640])
    grad_x, d_g1 = _first_bwd(x, dx2, dh1, tied(sm["norm_mix_pre"], token))

    grads = {
        "norm_mix_pre": d_g1, "norm_mix_post": d_g2, "norm_ffn_pre": d_g3, "norm_ffn_post": d_g4,
        "w_in": d_win, "rel_bias": d_relb, "sinks": dsink[:, 0].reshape(1, N_Q_HEADS),
        "rwkv_shift_mix": d_mix, "w0": d_w0, "w_decay_up": d_wdu[:LORA_DECAY], "a0": d_a0,
        "w_iclr_up": d_wiu[LORA_DECAY:], "w_gate_up": d_wgu, "k_k": d_kk, "k_a": d_ka,
        "r_k": d_rk.reshape(1, N_Q_HEADS, HEAD_DIM), "ln_x_g": d_lng, "ln_x_b": d_lnb,
        "w_out": d_wout, "w_ffn_up": d_wup, "conv_w": d_convw, "conv_b": d_convb, "w_ffn_down": d_wdown,
    }
    return loss, grad_x, grads


def _place():
    x, y, c = lax.axis_index("x"), lax.axis_index("y"), lax.axis_index("c")
    chips = [(1 - x, y), (x, 1 - y), (1 - x, 1 - y)]
    return x, y, c, chips


def _remote(src, dst, sems, idx, to):
    return pltpu.make_async_remote_copy(src_ref=src, dst_ref=dst, send_sem=sems[0].at[idx], recv_sem=sems[1].at[idx],
                                        device_id=to, device_id_type=MESH)


def _half(c, rows):
    return pl.ds(pl.multiple_of(c * (rows // 2), 16), rows // 2)


def _gather_weights(big, small):
    nb, ns = len(big), len(small)

    def body(*refs):
        ins, outs = refs[:nb + ns], refs[nb + ns:2 * (nb + ns)]
        ici, d2d, sml, loc = refs[2 * (nb + ns):2 * (nb + ns) + 2], refs[-5:-3], refs[-3:-1], refs[-1]
        x, y, c, chips = _place()
        me = 2 * x + y
        sib = (x, y, 1 - c)
        local = [pltpu.make_async_copy(ins[a], outs[a].at[me], loc.at[a]) for a in range(nb + ns)]
        for cp in local:
            cp.start()
        sends = []
        for a in range(nb):
            rows = _half(c, big[a].shape[0])
            for kk, chip in enumerate(chips):
                sends.append(_remote(ins[a].at[rows], outs[a].at[me, rows], ici, a * 3 + kk, (*chip, c)))
        for a in range(ns):
            for kk, chip in enumerate(chips):
                sends.append(_remote(ins[nb + a], outs[nb + a].at[me], sml, a * 3 + kk, (*chip, c)))
        for cp in sends:
            cp.start()
        passed = []
        for a in range(nb):
            rows = _half(c, big[a].shape[0])
            for kk, (px, py) in enumerate(chips):
                got = outs[a].at[2 * px + py, rows]
                _remote(got, got, ici, a * 3 + kk, sib).wait_recv()
                fwd = _remote(got, got, d2d, a * 3 + kk, sib)
                fwd.start()
                passed.append(fwd)
        for a in range(nb):
            other = _half(1 - c, big[a].shape[0])
            for kk, (px, py) in enumerate(chips):
                land = outs[a].at[2 * px + py, other]
                _remote(land, land, d2d, a * 3 + kk, sib).wait_recv()
        for a in range(ns):
            for kk, (px, py) in enumerate(chips):
                land = outs[nb + a].at[2 * px + py]
                _remote(land, land, sml, a * 3 + kk, sib).wait_recv()
        for cp in sends + passed:
            cp.wait_send()
        for cp in local:
            cp.wait()

    arrs = list(big) + list(small)
    return pl.pallas_call(
        body, name="gather_weights",
        in_specs=[ANY] * len(arrs), out_specs=[ANY] * len(arrs),
        out_shape=[jax.ShapeDtypeStruct((N_CHIPS,) + t.shape, t.dtype) for t in arrs],
        scratch_shapes=[pltpu.SemaphoreType.DMA((3 * nb,)), pltpu.SemaphoreType.DMA((3 * nb,)),
                        pltpu.SemaphoreType.DMA((3 * nb,)), pltpu.SemaphoreType.DMA((3 * nb,)),
                        pltpu.SemaphoreType.DMA((3 * ns,)), pltpu.SemaphoreType.DMA((3 * ns,)),
                        pltpu.SemaphoreType.DMA((nb + ns,))],
        compiler_params=pltpu.CompilerParams(has_side_effects=True),
    )(*arrs)


HBM = pl.BlockSpec(memory_space=pltpu.HBM)
SEM = pl.BlockSpec(memory_space=pltpu.SEMAPHORE)
EFFECT = pltpu.SideEffectType.DATAFLOW_SIDE_EFFECTING


def _copies_start(name, bufs, plan, n):
    nb = len(bufs)

    def body(*refs):
        ins, sems, token = refs[:nb], refs[nb:nb + 2 * n], refs[-1]
        for kk, (src, dst, dev) in enumerate(plan(ins)):
            pltpu.make_async_remote_copy(src_ref=src, dst_ref=dst, send_sem=sems[2 * kk], recv_sem=sems[2 * kk + 1],
                                         device_id=dev, device_id_type=MESH).start()
        token[...] = jnp.zeros_like(token)

    outs = pl.pallas_call(
        body, name=name,
        out_shape=tuple([pltpu.SemaphoreType.DMA(())] * (2 * n) + [pltpu.HBM(t.shape, t.dtype) for t in bufs]
                        + [jax.ShapeDtypeStruct((8, LANES), F32)]),
        in_specs=[HBM] * nb,
        out_specs=tuple([SEM] * (2 * n) + [HBM] * nb + [pl.BlockSpec(memory_space=pltpu.VMEM)]),
        input_output_aliases={t: 2 * n + t for t in range(nb)},
        compiler_params=pltpu.CompilerParams(has_side_effects=EFFECT),
    )(*[pltpu.with_memory_space_constraint(t, pltpu.HBM) for t in bufs])
    return outs[:2 * n], outs[2 * n:2 * n + nb], outs[-1]


def _copies_wait(name, sems, bufs, plan, n, after):
    nb = len(bufs)
    after = list(after) if isinstance(after, (list, tuple)) else [after]

    def body(*refs):
        ins, sem_refs = refs[:nb], refs[nb:nb + 2 * n]
        for kk, (src, dst, dev) in enumerate(plan(ins)):
            cp = pltpu.make_async_remote_copy(src_ref=src, dst_ref=dst, send_sem=sem_refs[2 * kk],
                                              recv_sem=sem_refs[2 * kk + 1], device_id=dev, device_id_type=MESH)
            cp.wait_send()
            cp.wait_recv()

    return pl.pallas_call(
        body, name=name,
        out_shape=tuple(pltpu.HBM(t.shape, t.dtype) for t in bufs),
        in_specs=[HBM] * nb + [SEM] * (2 * n) + [ANY] * len(after),
        out_specs=tuple([HBM] * nb),
        input_output_aliases={t: t for t in range(nb)},
        compiler_params=pltpu.CompilerParams(has_side_effects=EFFECT),
    )(*bufs, *sems, *after)


def _plan_gather(n_w):
    def plan(refs):
        x, y, c, chips = _place()
        me = 2 * x + y
        return [(refs[a], refs[n_w + a].at[me], (*chip, c)) for a in range(n_w) for chip in chips]
    return plan


def _plan_pair_halves(n_g, rows):
    def plan(refs):
        x, y, c, _ = _place()
        return [(refs[a].at[:, _half(1 - c, rows[a])], refs[n_g + a], (x, y, 1 - c)) for a in range(n_g)]
    return plan


def _plan_chip_parts(n_g):
    def plan(refs):
        x, y, c, chips = _place()
        me = 2 * x + y
        return [(refs[a].at[2 * px + py], refs[n_g + a].at[me], (px, py, c))
                for a in range(n_g) for (px, py) in chips]
    return plan


def _plan_pair_fill(n_g, rows):
    def plan(refs):
        x, y, c, _ = _place()
        return [(refs[a].at[_half(c, rows[a])], refs[a].at[_half(c, rows[a])], (x, y, 1 - c)) for a in range(n_g)]
    return plan


def _pair_add(g, got, name):
    _, rows, cols = g.shape
    hr = rows // 2
    tr = min(hr, 256)
    nb = hr // tr

    def body(g_ref, got_ref, p_ref, own_ref):
        val = (g_ref[...] + got_ref[...]).astype(BF16)
        p_ref[...] = val

        @pl.when(pl.program_id(1) == 2 * lax.axis_index("x") + lax.axis_index("y"))
        def _():
            own_ref[...] = val

    def mine(i, s):
        return (2 * lax.axis_index("x") + lax.axis_index("y"), i, 0)

    return pl.pallas_call(
        body, name=name, grid=(nb, N_CHIPS),
        in_specs=[pl.BlockSpec((None, tr, cols), lambda i, s: (s, lax.axis_index("c") * nb + i, 0)),
                  pl.BlockSpec((None, tr, cols), lambda i, s: (s, i, 0))],
        out_specs=[pl.BlockSpec((None, tr, cols), lambda i, s: (s, i, 0)), pl.BlockSpec((None, tr, cols), mine)],
        out_shape=[jax.ShapeDtypeStruct((N_CHIPS, hr, cols), BF16)] * 2,
        compiler_params=_cp(("parallel", "arbitrary")),
    )(g, got)


def _chip_sum(parts, name):
    _, hr, cols = parts.shape
    tr = min(hr, 128)
    nb = hr // tr

    def body(t_ref, o_ref):
        part = [t_ref[s].astype(F32) for s in range(N_CHIPS)]
        o_ref[...] = ((part[0] + part[1]) + part[2]) + part[3]

    return pl.pallas_call(
        body, name=name, grid=(nb,),
        in_specs=[pl.BlockSpec((N_CHIPS, tr, cols), lambda i: (0, i, 0))],
        out_specs=pl.BlockSpec((tr, cols), lambda i: (lax.axis_index("c") * nb + i, 0)),
        out_shape=jax.ShapeDtypeStruct((2 * hr, cols), F32),
        compiler_params=_cp(("parallel",)),
    )(parts)


class _Reduction:
    def __init__(self, tag, rows):
        self.tag, self.n, self.rows = tag, len(rows), rows
        self.plans = (_plan_pair_halves(self.n, rows), _plan_chip_parts(self.n), _plan_pair_fill(self.n, rows))
        self.flight = None

    def _name(self, what):
        return f"grad_{self.tag}_{what}"

    def start(self, gs):
        gots = [lax.empty((N_CHIPS, t.shape[1] // 2, t.shape[2]), F32) for t in gs]
        self.flight = _copies_start(self._name("pair_start"), list(gs) + gots, self.plans[0], self.n)
        return self.flight[2]

    def after_pair(self, after):
        sems, bufs, _ = self.flight
        out = _copies_wait(self._name("pair_wait"), sems, bufs, self.plans[0], self.n, after)
        sums = [_pair_add(g, got, self._name(f"pair_add_{i}"))
                for i, (g, got) in enumerate(zip(out[:self.n], out[self.n:]))]
        self.flight = _copies_start(self._name("chip_start"), [p for p, _ in sums] + [own for _, own in sums],
                                    self.plans[1], 3 * self.n)
        return self.flight[2]

    def after_chips(self, after):
        sems, bufs, _ = self.flight
        out = _copies_wait(self._name("chip_wait"), sems, bufs, self.plans[1], 3 * self.n, after)
        fulls = [_chip_sum(t, self._name(f"chip_sum_{i}")) for i, t in enumerate(out[self.n:])]
        self.flight = _copies_start(self._name("fill_start"), fulls, self.plans[2], self.n)
        return self.flight[2]

    def finish(self, after):
        sems, bufs, _ = self.flight
        return _copies_wait(self._name("fill_wait"), sems, bufs, self.plans[2], self.n, after)


def _adamw_math(w, g, m, v):
    nm = ADAM_B1 * m + (1.0 - ADAM_B1) * g
    nv = ADAM_B2 * v + (1.0 - ADAM_B2) * (g * g)
    m_hat = nm / (1.0 - ADAM_B1 ** ADAM_STEP)
    v_hat = nv / (1.0 - ADAM_B2 ** ADAM_STEP)
    return -ADAM_LR * (m_hat / (jnp.sqrt(v_hat) + ADAM_EPS) + ADAM_WD * w), nm, nv


def _adamw(w, g, m, v, name, tr):
    r, cdim = w.shape

    def body(w_ref, g_ref, m_ref, v_ref, d_ref, nm_ref, nv_ref):
        d_ref[...], nm_ref[...], nv_ref[...] = _adamw_math(w_ref[...], g_ref[...], m_ref[...], v_ref[...])

    return pl.pallas_call(
        body, name=name, grid=(r // tr,), in_specs=[_rows(tr, cdim)] * 4, out_specs=[_rows(tr, cdim)] * 3,
        out_shape=[jax.ShapeDtypeStruct((r, cdim), F32)] * 3, compiler_params=_cp(("parallel",)),
    )(w, g, m, v)


def _adamw_small(w, parts, m, v):
    def body(w_ref, p_ref, m_ref, v_ref, d_ref, nm_ref, nv_ref, g_ref):
        g = p_ref[0]
        for dev in range(1, N_DEV):
            g = g + p_ref[dev]
        g_ref[...] = g
        d_ref[...], nm_ref[...], nv_ref[...] = _adamw_math(w_ref[...], g, m_ref[...], v_ref[...])

    return pl.pallas_call(
        body, name="adamw_small", grid=(1,),
        in_specs=[_const(w.shape), _const(parts.shape), _const(w.shape), _const(w.shape)],
        out_specs=[_const(w.shape)] * 4, out_shape=[jax.ShapeDtypeStruct(w.shape, F32)] * 4,
        compiler_params=_cp(("arbitrary",)),
    )(w, parts, m, v)


REPLICATED = (("norm_mix_pre", 1024), ("norm_mix_post", 1024), ("norm_ffn_pre", 1024), ("norm_ffn_post", 1024),
              ("rel_bias", 256), ("sinks", 8), ("rwkv_shift_mix", 1792), ("w0", 512), ("a0", 512), ("k_k", 512),
              ("k_a", 512), ("r_k", 512), ("ln_x_g", 512), ("ln_x_b", 512), ("conv_b", 8192))
SMALL_SHARDED = (("w_decay_up", LORA_DECAY, D_RWKV), ("w_iclr_up", LORA_ICLR, D_RWKV),
                 ("w_gate_up", LORA_GATE, D_RWKV), ("conv_w", 3, 2 * D_FF))
BIG = (("w_in", D_MODEL, 640), ("w_out", 256, D_MODEL), ("w_ffn_up", D_MODEL, 2048), ("w_ffn_down", 1024, D_MODEL))
PACK_ALIGN = 8 * LANES


def _pack(pieces):
    flat = []
    for t in pieces:
        t = t.reshape(-1)
        pad = (-t.shape[0]) % LANES
        flat.append(jnp.pad(t, (0, pad)) if pad else t)
    flat = jnp.concatenate(flat)
    pad = (-flat.shape[0]) % PACK_ALIGN
    return jnp.pad(flat, (0, pad)).reshape(-1, LANES)


def _unpack(buf, sizes):
    flat, out, off = buf.reshape(-1), [], 0
    for n in sizes:
        out.append(flat[off:off + n])
        off += n + ((-n) % LANES)
    return out


def kernel(x, norm_mix_pre, norm_mix_post, norm_ffn_pre, norm_ffn_post, w_in, rel_bias, sinks, rwkv_shift_mix, w0, w_decay_up, a0, w_iclr_up, w_gate_up, k_k, k_a, r_k, ln_x_g, ln_x_b, w_out, w_ffn_up, conv_w, conv_b, w_ffn_down, loss_target, m_norm_mix_pre, m_norm_mix_post, m_norm_ffn_pre, m_norm_ffn_post, m_w_in, m_rel_bias, m_sinks, m_rwkv_shift_mix, m_w0, m_w_decay_up, m_a0, m_w_iclr_up, m_w_gate_up, m_k_k, m_k_a, m_r_k, m_ln_x_g, m_ln_x_b, m_w_out, m_w_ffn_up, m_conv_w, m_conv_b, m_w_ffn_down, v_norm_mix_pre, v_norm_mix_post, v_norm_ffn_pre, v_norm_ffn_post, v_w_in, v_rel_bias, v_sinks, v_rwkv_shift_mix, v_w0, v_w_decay_up, v_a0, v_w_iclr_up, v_w_gate_up, v_k_k, v_k_a, v_r_k, v_ln_x_g, v_ln_x_b, v_w_out, v_w_ffn_up, v_conv_w, v_conv_b, v_w_ffn_down):
    given = dict(locals())
    names = [n for n, _ in REPLICATED] + [n for n, _, _ in SMALL_SHARDED] + [n for n, _, _ in BIG]
    order = ["norm_mix_pre", "norm_mix_post", "norm_ffn_pre", "norm_ffn_post", "w_in", "rel_bias", "sinks",
             "rwkv_shift_mix", "w0", "w_decay_up", "a0", "w_iclr_up", "w_gate_up", "k_k", "k_a", "r_k", "ln_x_g",
             "ln_x_b", "w_out", "w_ffn_up", "conv_w", "conv_b", "w_ffn_down"]
    assert sorted(names) == sorted(order)
    shard = 2 * lax.axis_index("x") + lax.axis_index("y")

    big_sh = {n: given[n].reshape(a, b).astype(BF16) for n, a, b in BIG}
    small_sh = [given[n].reshape(r, c // N_CHIPS) for n, r, c in SMALL_SHARDED]
    gathered = _gather_weights([big_sh["w_in"]], small_sh)
    rest = ("w_out", "w_ffn_up", "w_ffn_down")
    win_st, rest_sh = lax.optimization_barrier((gathered[0], [big_sh[n] for n in rest]))
    sm = {n: given[n] for n, _ in REPLICATED}
    sm["r_k"] = r_k.reshape(N_Q_HEADS, HEAD_DIM)
    for (n, r, c), st in zip(SMALL_SHARDED, gathered[1:]):
        sm[n] = st.transpose(1, 0, 2).reshape(r, c)

    lands = [lax.dynamic_update_slice(lax.empty((N_CHIPS,) + t.shape, BF16), t[None], (shard, 0, 0)) for t in rest_sh]
    plan_w = _plan_gather(len(rest))
    w_sems, w_bufs, token = _copies_start("gather_rest_start", rest_sh + lands, plan_w, 9)
    sm["norm_mix_pre"] = norm_mix_pre + token[0:1, 0:1]

    def on_rest_weights(after):
        out = _copies_wait("gather_rest_wait", w_sems, w_bufs, plan_w, 9, after)
        wout_st, wup_st, wdown_st = out[3:]
        return wout_st.reshape(D_MODEL, D_MODEL), wup_st, wdown_st.reshape(D_FF, D_MODEL)

    red_a = _Reduction("a", (1024, D_MODEL, 256))
    red_b = _Reduction("b", (D_MODEL,))

    def on_grads_a(gs):
        d_wdown, d_wup, d_wout = gs
        return red_a.start([d_wdown.reshape(N_CHIPS, 1024, D_MODEL), d_wup, d_wout.reshape(N_CHIPS, 256, D_MODEL)])

    handlers = {"rest_weights": on_rest_weights, "grads_a": on_grads_a, "seam_1": red_a.after_pair,
                "seam_2": red_a.after_chips, "grads_b": lambda g: red_b.start([g])}
    steps = _local_step(x[0], loss_target[0], sm, win_st)
    kind, payload = next(steps)
    while True:
        try:
            kind, payload = steps.send(handlers[kind](payload))
        except StopIteration as done:
            loss, grad_x, grads = done.value
            break

    small_names = [n for n, _ in REPLICATED] + [n for n, _, _ in SMALL_SHARDED]

    def shard_cols(t, s):
        return t[:, s * (t.shape[1] // N_CHIPS):(s + 1) * (t.shape[1] // N_CHIPS)]

    for_chip = jnp.stack([_pack([loss[0]] + [grads[n] for n, _ in REPLICATED]
                                + [shard_cols(grads[n], s) for n, _, _ in SMALL_SHARDED]) for s in range(N_CHIPS)])
    me = 2 * shard + lax.axis_index("c")
    mine = lax.dynamic_index_in_dim(for_chip, shard, 0, keepdims=True)
    land = lax.dynamic_update_slice(lax.empty((N_DEV,) + for_chip.shape[1:], F32), mine, (me, 0, 0))

    def plan_small(refs):
        x, y, c, _ = _place()
        out = []
        for rel in range(1, N_DEV):
            px, py, pc = x ^ (rel >> 2), y ^ ((rel >> 1) & 1), c ^ (rel & 1)
            out.append((refs[0].at[2 * px + py], refs[1].at[4 * x + 2 * y + c], (px, py, pc)))
        return out

    s_sems, s_bufs, _ = _copies_start("grad_small_start", [for_chip, land], plan_small, N_DEV - 1)

    red_b.after_pair(grad_x)
    g_out = {}
    g_out["w_ffn_down"], g_out["w_ffn_up"], g_out["w_out"] = red_a.finish(grad_x)

    delta, new_m, new_v = {}, {}, {}

    def update(n, a, b):
        delta[n], new_m[n], new_v[n] = _adamw(given[n].reshape(a, b), g_out[n], given["m_" + n].reshape(a, b),
                                              given["v_" + n].reshape(a, b), "adamw_" + n, 128)

    for n, a, b in BIG[1:]:
        update(n, a, b)
    done = [delta[n] for n, _, _ in BIG[1:]]
    red_b.after_chips(done)
    parts = _copies_wait("grad_small_wait", s_sems, s_bufs, plan_small, N_DEV - 1, done)[1]
    no_param = jnp.zeros((LANES,), F32)
    packs = [_pack([no_param] + [given[pre + n] for n in small_names]) for pre in ("", "m_", "v_")]
    small_sizes = [LANES] + [int(np.prod(given[n].shape)) for n in small_names]
    upd = [_unpack(t, small_sizes) for t in _adamw_small(packs[0], parts, packs[1], packs[2])]
    loss = upd[3][0][0]
    for n, d, nm, nv, g in zip(small_names, *[u[1:] for u in upd]):
        shape = given[n].shape
        delta[n], new_m[n], new_v[n], g_out[n] = (t.reshape(shape) for t in (d, nm, nv, g))
    g_out["w_in"], = red_b.finish(upd[0][0])
    update(*BIG[0])

    def shaped(d):
        return [d[n].reshape(given[n].shape) for n in order]

    return (loss, grad_x.reshape(x.shape), *shaped(g_out), *shaped(delta), *shaped(new_m), *shaped(new_v))
```

```python
import math

import numpy as np
import jax
import jax.numpy as jnp
from jax import lax
from jax.experimental import pallas as pl
from jax.experimental.pallas import tpu as pltpu

F32 = jnp.float32
BF16 = jnp.bfloat16
MESH = pl.DeviceIdType.MESH

SEQ = 2048
D_MODEL = 1024
HEAD_DIM = 64
D_ATTN = 512
D_RWKV = 512
D_KV = 128
N_Q_HEADS = 8
N_KV_HEADS = 2
Q_PER_KV = 4
BLOCK = 128
N_BUCKETS = 32
MAX_DISTANCE = 128
LORA_DECAY = 64
LORA_ICLR = 64
LORA_GATE = 128
RWKV_COLS = 3 * D_RWKV + LORA_DECAY + LORA_ICLR + LORA_GATE
P_OFF = D_ATTN + 2 * D_KV
D_IN = P_OFF + RWKV_COLS
D_FF = 4096
NORM_EPS = 1e-6
GN_EPS = 64e-5
NEG_INF = -1e30
N_CHIPS = 4
N_DEV = 8

ADAM_LR = 0.001
ADAM_B1 = 0.9
ADAM_B2 = 0.999
ADAM_EPS = 1e-08
ADAM_WD = 0.01
ADAM_STEP = 10

VMEM_LIMIT = 52 * 1024 * 1024
LANES = 128


def _cp(sem=None, vmem=VMEM_LIMIT):
    kw = dict(vmem_limit_bytes=vmem)
    if sem is not None:
        kw["dimension_semantics"] = sem
    return pltpu.CompilerParams(**kw)


def _rows(tr, nc):
    return pl.BlockSpec((tr, nc), lambda i: (i, 0))


def _const(shape):
    return pl.BlockSpec(shape, lambda *_: (0,) * len(shape))


ANY = pl.BlockSpec(memory_space=pl.ANY)


def _split(x, n):
    parts = []
    for _ in range(n - 1):
        h = x.astype(BF16)
        parts.append(h)
        x = x - h.astype(F32)
    parts.append(x.astype(BF16))
    return parts


def _dot(a, b, dn=(((1,), (0,)), ((), ()))):
    return lax.dot_general(a, b, dn, preferred_element_type=F32)


NN = (((1,), (0,)), ((), ()))
NT = (((1,), (1,)), ((), ()))
TN = (((0,), (0,)), ((), ()))


def _dot_ind(x, ind_bf16, n=3):
    acc = None
    for part in _split(x, n):
        t = _dot(part, ind_bf16)
        acc = t if acc is None else acc + t
    return acc


def _head_ones(n):
    r = lax.broadcasted_iota(jnp.int32, (n, n), 0) >> 6
    c = lax.broadcasted_iota(jnp.int32, (n, n), 1) >> 6
    return jnp.where(r == c, 1.0, 0.0).astype(BF16)


def _matmul(a, b, mode, name, *, m, n, k, tm, tn, a_spec=None, b_spec=None, out=None, out_dtype=F32):
    keep_at = mode == "tn" and m == tm and n > tn

    def body(a_ref, b_ref, o_ref, *scratch):
        if keep_at:
            at_ref, = scratch

            @pl.when(pl.program_id(1) == 0)
            def _():
                at_ref[...] = a_ref[...].T

            o_ref[...] = _dot(at_ref[...], b_ref[...], NN).astype(out_dtype)
        else:
            o_ref[...] = _dot(a_ref[...], b_ref[...], {"nn": NN, "nt": NT, "tn": TN}[mode]).astype(out_dtype)

    if a_spec is None:
        a_spec = pl.BlockSpec((k, tm), lambda i, j: (0, i)) if mode == "tn" else pl.BlockSpec((tm, k), lambda i, j: (i, 0))
    if b_spec is None:
        b_spec = pl.BlockSpec((tn, k), lambda i, j: (j, 0)) if mode == "nt" else pl.BlockSpec((k, tn), lambda i, j: (0, j))
    return pl.pallas_call(
        body, name=name, grid=(m // tm, n // tn),
        in_specs=[a_spec, b_spec],
        out_specs=pl.BlockSpec((tm, tn), lambda i, j: (i, j)) if out is None else out[1],
        out_shape=jax.ShapeDtypeStruct((m, n) if out is None else out[0], out_dtype),
        scratch_shapes=[pltpu.VMEM((tm, k), a.dtype)] if keep_at else [],
        compiler_params=_cp(("parallel", "arbitrary" if keep_at else "parallel")),
    )(a, b)


def _matmul_nt_shards(a, b_st, name, *, m, n, tm, tn, a_spec, a_piece):
    ks = b_st.shape[2]

    def body(a_ref, b_ref, o_ref):
        acc = _dot(a_piece(a_ref, 0), b_ref[0], NT)
        for s in range(1, N_CHIPS):
            acc = acc + _dot(a_piece(a_ref, s), b_ref[s], NT)
        o_ref[...] = acc

    return pl.pallas_call(
        body, name=name, grid=(m // tm, n // tn),
        in_specs=[a_spec, pl.BlockSpec((N_CHIPS, tn, ks), lambda i, j: (0, j, 0))],
        out_specs=pl.BlockSpec((tm, tn), lambda i, j: (i, j)),
        out_shape=jax.ShapeDtypeStruct((m, n), F32),
        compiler_params=_cp(("parallel", "parallel")),
    )(a, b_st)


def _rstd(x):
    return lax.rsqrt(jnp.mean(x * x, axis=-1, keepdims=True) + NORM_EPS)


def _rms_bwd(x, r, g, dy):
    gy = dy * g
    return r * gy - x * ((r * r * r) * (jnp.sum(x * gy, axis=-1, keepdims=True) / x.shape[-1]))


TR = 256


def _norm_cast(x, g, name):
    def body(x_ref, g_ref, h_ref):
        x = x_ref[...]
        h_ref[...] = (x * _rstd(x) * g_ref[...]).astype(BF16)

    return pl.pallas_call(
        body, name=name, grid=(SEQ // TR,),
        in_specs=[_rows(TR, D_MODEL), _const((1, D_MODEL))],
        out_specs=_rows(TR, D_MODEL),
        out_shape=jax.ShapeDtypeStruct((SEQ, D_MODEL), BF16),
        compiler_params=_cp(("parallel",)),
    )(x, g)


def _mix_norm(x, mix, g2, g3):
    def body(x_ref, mix_ref, g2_ref, g3_ref, x2_ref, h3_ref):
        mixv = mix_ref[...]
        x2 = x_ref[...] + mixv * _rstd(mixv) * g2_ref[...]
        x2_ref[...] = x2
        h3_ref[...] = (x2 * _rstd(x2) * g3_ref[...]).astype(BF16)

    return pl.pallas_call(
        body, name="mix_norm", grid=(SEQ // TR,),
        in_specs=[_rows(TR, D_MODEL), _rows(TR, D_MODEL), _const((1, D_MODEL)), _const((1, D_MODEL))],
        out_specs=[_rows(TR, D_MODEL), _rows(TR, D_MODEL)],
        out_shape=[jax.ShapeDtypeStruct((SEQ, D_MODEL), F32), jax.ShapeDtypeStruct((SEQ, D_MODEL), BF16)],
        compiler_params=_cp(("parallel",)),
    )(x, mix, g2, g3)


def _loss_head(x2, f, g4, target):
    def body(x2_ref, f_ref, g4_ref, t_ref, loss_ref, dy_ref, df_ref, dg_ref):
        i = pl.program_id(0)
        f = f_ref[...]
        g4 = g4_ref[...]
        r = _rstd(f)
        e = x2_ref[...] + f * r * g4 - t_ref[...]
        dy = e * (1.0 / D_MODEL)
        dy_ref[...] = dy
        df_ref[...] = _rms_bwd(f, r, g4, dy).astype(BF16)
        part = 0.5 * jnp.sum(jnp.sum(e * e, axis=-1, keepdims=True), axis=0, keepdims=True) * (1.0 / D_MODEL)
        dg = jnp.sum(dy * f * r, axis=0, keepdims=True)

        @pl.when(i == 0)
        def _():
            loss_ref[...] = jnp.zeros_like(loss_ref)
            dg_ref[...] = jnp.zeros_like(dg_ref)

        loss_ref[...] += jnp.broadcast_to(part, loss_ref.shape)
        dg_ref[...] += dg

    return pl.pallas_call(
        body, name="loss_head", grid=(SEQ // TR,),
        in_specs=[_rows(TR, D_MODEL), _rows(TR, D_MODEL), _const((1, D_MODEL)), _rows(TR, D_MODEL)],
        out_specs=[_const((8, LANES)), _rows(TR, D_MODEL), _rows(TR, D_MODEL), _const((1, D_MODEL))],
        out_shape=[jax.ShapeDtypeStruct((8, LANES), F32), jax.ShapeDtypeStruct((SEQ, D_MODEL), F32),
                   jax.ShapeDtypeStruct((SEQ, D_MODEL), BF16), jax.ShapeDtypeStruct((1, D_MODEL), F32)],
        compiler_params=_cp(("arbitrary",)),
    )(x2, f, g4, target)


def _mid_bwd(x2, mix, dy, dh3, g2, g3):
    def body(x2_ref, mix_ref, dy_ref, dh3_ref, g2_ref, g3_ref, dx2_ref, dmix_ref, dg2_ref, dg3_ref):
        i = pl.program_id(0)
        x2 = x2_ref[...]
        mixv = mix_ref[...]
        dh3 = dh3_ref[...]
        r3 = _rstd(x2)
        dx2 = dy_ref[...] + _rms_bwd(x2, r3, g3_ref[...], dh3)
        dx2_ref[...] = dx2
        r2 = _rstd(mixv)
        dmix_ref[...] = _rms_bwd(mixv, r2, g2_ref[...], dx2).astype(BF16)

        @pl.when(i == 0)
        def _():
            dg2_ref[...] = jnp.zeros_like(dg2_ref)
            dg3_ref[...] = jnp.zeros_like(dg3_ref)

        dg3_ref[...] += jnp.sum(dh3 * x2 * r3, axis=0, keepdims=True)
        dg2_ref[...] += jnp.sum(dx2 * mixv * r2, axis=0, keepdims=True)

    return pl.pallas_call(
        body, name="mid_bwd", grid=(SEQ // TR,),
        in_specs=[_rows(TR, D_MODEL)] * 4 + [_const((1, D_MODEL))] * 2,
        out_specs=[_rows(TR, D_MODEL), _rows(TR, D_MODEL), _const((1, D_MODEL)), _const((1, D_MODEL))],
        out_shape=[jax.ShapeDtypeStruct((SEQ, D_MODEL), F32), jax.ShapeDtypeStruct((SEQ, D_MODEL), BF16),
                   jax.ShapeDtypeStruct((1, D_MODEL), F32), jax.ShapeDtypeStruct((1, D_MODEL), F32)],
        compiler_params=_cp(("arbitrary",)),
    )(x2, mix, dy, dh3, g2, g3)


def _first_bwd(x, dx2, dh1, g1):
    def body(x_ref, dx2_ref, dh1_ref, g1_ref, dx_ref, dg1_ref):
        i = pl.program_id(0)
        x = x_ref[...]
        dh1 = dh1_ref[...]
        r = _rstd(x)
        dx_ref[...] = dx2_ref[...] + _rms_bwd(x, r, g1_ref[...], dh1)

        @pl.when(i == 0)
        def _():
            dg1_ref[...] = jnp.zeros_like(dg1_ref)

        dg1_ref[...] += jnp.sum(dh1 * x * r, axis=0, keepdims=True)

    return pl.pallas_call(
        body, name="first_bwd", grid=(SEQ // TR,),
        in_specs=[_rows(TR, D_MODEL)] * 3 + [_const((1, D_MODEL))],
        out_specs=[_rows(TR, D_MODEL), _const((1, D_MODEL))],
        out_shape=[jax.ShapeDtypeStruct((SEQ, D_MODEL), F32), jax.ShapeDtypeStruct((1, D_MODEL), F32)],
        compiler_params=_cp(("arbitrary",)),
    )(x, dx2, dh1, g1)


TC = 256
N_CB = D_FF // TC
GELU_C = math.sqrt(2.0 / math.pi)


def _shift_down(u, s):
    rolled = pltpu.roll(u, s, 0)
    row = lax.broadcasted_iota(jnp.int32, u.shape, 0)
    return jnp.where(row >= s, rolled, 0.0)


def _shift_up(u, s):
    n = u.shape[0]
    rolled = pltpu.roll(u, n - s, 0)
    row = lax.broadcasted_iota(jnp.int32, u.shape, 0)
    return jnp.where(row < n - s, rolled, 0.0)


def _conv3(u, w, b):
    return b + w[0:1] * _shift_down(u, 2) + w[1:2] * _shift_down(u, 1) + w[2:3] * u


def _gelu_and_grad(x):
    inner = GELU_C * (x + 0.044715 * (x * x * x))
    t = jnp.tanh(inner)
    gelu = 0.5 * x * (1.0 + t)
    dgelu = 0.5 * (1.0 + t) + 0.5 * x * (1.0 - t * t) * (GELU_C * (1.0 + 3 * 0.044715 * (x * x)))
    return gelu, dgelu


def _ffn_specs():
    col = lambda off: pl.BlockSpec((SEQ, TC), lambda *g: (0, g[-1] + off))
    w = lambda off: pl.BlockSpec((3, TC), lambda *g: (0, g[-1] + off))
    b = lambda off: pl.BlockSpec((1, TC), lambda *g: (0, g[-1] + off))
    return col, w, b


def _ffn_up_act(h3, wup_st, conv_w, conv_b):
    col, w, b = _ffn_specs()
    per_shard = wup_st.shape[2] // TC

    def body(h_ref, g0_ref, v0_ref, gn_ref, vn_ref, wg_ref, wv_ref, bg_ref, bv_ref, ug_ref, uv_ref, act_ref, u_scr):
        j = pl.program_id(0)
        h = h_ref[...]

        @pl.when(j == 0)
        def _():
            u_scr[0, 0] = _dot(h, g0_ref[...])
            u_scr[0, 1] = _dot(h, v0_ref[...])

        slot = j % 2
        ug, uv = u_scr[slot, 0], u_scr[slot, 1]
        u_scr[1 - slot, 0] = _dot(h, gn_ref[...])
        u_scr[1 - slot, 1] = _dot(h, vn_ref[...])
        ug_ref[...] = ug
        uv_ref[...] = uv
        gate = _conv3(ug, wg_ref[...], bg_ref[...])
        val = _conv3(uv, wv_ref[...], bv_ref[...])
        act_ref[...] = (_gelu_and_grad(gate)[0] * val).astype(BF16)

    def up_block(first_shard, ahead):
        def pick(j):
            blk = jnp.minimum(j + ahead, N_CB - 1)
            return (first_shard + blk // per_shard, 0, blk % per_shard)
        return pl.BlockSpec((None, D_MODEL, TC), pick)

    return pl.pallas_call(
        body, name="ffn_up_act", grid=(N_CB,),
        in_specs=[_const((SEQ, D_MODEL)), up_block(0, 0), up_block(2, 0), up_block(0, 1), up_block(2, 1),
                  w(0), w(N_CB), b(0), b(N_CB)],
        out_specs=[col(0)] * 3,
        out_shape=[jax.ShapeDtypeStruct((SEQ, D_FF), F32)] * 2 + [jax.ShapeDtypeStruct((SEQ, D_FF), BF16)],
        scratch_shapes=[pltpu.VMEM((2, 2, SEQ, TC), F32)],
        compiler_params=_cp(("arbitrary",)),
    )(h3, wup_st, wup_st, wup_st, wup_st, conv_w, conv_w, conv_b, conv_b)


def _ffn_act_bwd(u_gate, u_val, df, wdown, conv_w, conv_b):
    col, w, b = _ffn_specs()
    both = lambda rows: pl.BlockSpec((2, rows, TC), lambda j: (0, 0, j))

    def body(ug_ref, uv_ref, df_ref, wd0_ref, wdn_ref, wg_ref, wv_ref, bg_ref, bv_ref, du_ref, dw_ref, db_ref, da_scr):
        j = pl.program_id(0)

        @pl.when(j == 0)
        def _():
            da_scr[0] = _dot(df_ref[...], wd0_ref[...], NT)

        slot = j % 2
        da = da_scr[slot]
        da_scr[1 - slot] = _dot(df_ref[...], wdn_ref[...], NT)
        ug, uv = ug_ref[...], uv_ref[...]
        wg, wv = wg_ref[...], wv_ref[...]
        gate = _conv3(ug, wg, bg_ref[...])
        val = _conv3(uv, wv, bv_ref[...])
        gelu, dgelu = _gelu_and_grad(gate)
        for h, (duc, uh, wh) in enumerate(((da * val * dgelu, ug, wg), (da * gelu, uv, wv))):
            up1, up2 = _shift_up(duc, 1), _shift_up(duc, 2)
            du_ref[h] = (wh[2:3] * duc + wh[1:2] * up1 + wh[0:1] * up2).astype(BF16)
            db_ref[h] = jnp.sum(duc, axis=0, keepdims=True)
            dw_ref[h] = jnp.concatenate(
                [jnp.sum(up2 * uh, axis=0, keepdims=True), jnp.sum(up1 * uh, axis=0, keepdims=True),
                 jnp.sum(duc * uh, axis=0, keepdims=True)], axis=0)

    return pl.pallas_call(
        body, name="ffn_act_bwd", grid=(N_CB,),
        in_specs=[col(0), col(0), _const((SEQ, D_MODEL)), pl.BlockSpec((TC, D_MODEL), lambda j: (j, 0)),
                  pl.BlockSpec((TC, D_MODEL), lambda j: (jnp.minimum(j + 1, N_CB - 1), 0)),
                  w(0), w(N_CB), b(0), b(N_CB)],
        out_specs=[both(SEQ), both(3), both(1)],
        out_shape=[jax.ShapeDtypeStruct((2, SEQ, D_FF), BF16), jax.ShapeDtypeStruct((2, 3, D_FF), F32),
                   jax.ShapeDtypeStruct((2, 1, D_FF), F32)],
        scratch_shapes=[pltpu.VMEM((2, SEQ, TC), F32)],
        compiler_params=_cp(("arbitrary",)),
    )(u_gate, u_val, df, wdown, wdown, conv_w, conv_w, conv_b, conv_b)


def _t5_onehot():
    rel = (np.arange(BLOCK)[:, None] + BLOCK) - np.arange(2 * BLOCK)[None, :]
    n = np.maximum(rel, 0)
    max_exact = N_BUCKETS // 2
    large = max_exact + (np.log(np.maximum(n, 1).astype(np.float32) / np.float32(max_exact))
                         / np.float32(math.log(MAX_DISTANCE / max_exact))
                         * np.float32(N_BUCKETS - max_exact)).astype(np.int32)
    large = np.minimum(large, N_BUCKETS - 1)
    bucket = np.where(n < max_exact, n, large).reshape(-1)
    return (bucket[None, :] == np.arange(N_BUCKETS)[:, None]).astype(np.float32)


N_REL = BLOCK * 2 * BLOCK


def _bias_table(rel_bias_t, onehot):
    def body(rb_ref, oh_ref, o_ref):
        o_ref[...] = _dot_ind(rb_ref[...], oh_ref[...])

    return pl.pallas_call(
        body, name="bias_table", grid=(1,),
        in_specs=[_const((N_Q_HEADS, N_BUCKETS)), _const((N_BUCKETS, N_REL))],
        out_specs=_const((N_Q_HEADS, N_REL)),
        out_shape=jax.ShapeDtypeStruct((N_Q_HEADS, N_REL), F32),
        compiler_params=_cp(("arbitrary",)),
    )(rel_bias_t, onehot)


def _bias_table_bwd(dbias, onehot):
    def body(db_ref, oh_ref, o_ref):
        acc = None
        for part in _split(db_ref[...], 3):
            t = _dot(part, oh_ref[...], NT)
            acc = t if acc is None else acc + t
        o_ref[...] = acc

    return pl.pallas_call(
        body, name="bias_table_bwd", grid=(1,),
        in_specs=[_const((N_Q_HEADS, N_REL)), _const((N_BUCKETS, N_REL))],
        out_specs=_const((N_Q_HEADS, N_BUCKETS)),
        out_shape=jax.ShapeDtypeStruct((N_Q_HEADS, N_BUCKETS), F32),
        compiler_params=_cp(("arbitrary",)),
    )(dbias, onehot)


def _attn_pieces(n, q, kvp, kvc, bias_ref, sinks_ref, hk):
    qi = lax.broadcasted_iota(jnp.int32, (BLOCK, 2 * BLOCK), 0)
    kj = lax.broadcasted_iota(jnp.int32, (BLOCK, 2 * BLOCK), 1)
    rel = qi + BLOCK - kj
    first_key = jnp.where(n > 0, 0, BLOCK)
    ok = jnp.where(rel >= 0, jnp.where(rel < BLOCK, jnp.where(kj >= first_key, 1.0, 0.0), 0.0), 0.0)
    ok4 = jnp.concatenate([ok] * Q_PER_KV, axis=0) > 0.5
    c0 = hk * HEAD_DIM
    kcat = jnp.concatenate([kvp[:, c0:c0 + HEAD_DIM], kvc[:, c0:c0 + HEAD_DIM]], axis=0).astype(BF16)
    vcat = jnp.concatenate([kvp[:, D_KV + c0:D_KV + c0 + HEAD_DIM], kvc[:, D_KV + c0:D_KV + c0 + HEAD_DIM]],
                           axis=0).astype(BF16)
    q0 = hk * Q_PER_KV * HEAD_DIM
    qs = jnp.concatenate([q[:, q0 + g * HEAD_DIM:q0 + (g + 1) * HEAD_DIM] for g in range(Q_PER_KV)],
                         axis=0).astype(BF16)
    s = _dot(qs, kcat, NT) * (HEAD_DIM ** -0.5) + bias_ref[hk]
    s = jnp.where(ok4, s, NEG_INF)
    row = lax.broadcasted_iota(jnp.int32, (Q_PER_KV * BLOCK, 1), 0)
    sink = jnp.zeros((Q_PER_KV * BLOCK, 1), F32)
    for g in range(Q_PER_KV):
        sink = jnp.where((row >> 7) == g, sinks_ref[hk * Q_PER_KV + g], sink)
    m = jnp.maximum(jnp.max(s, axis=-1, keepdims=True), sink)
    p = jnp.exp(s - m)
    es = jnp.exp(sink - m)
    inv = 1.0 / (jnp.sum(p, axis=-1, keepdims=True) + es)
    return qs, kcat, vcat, p * inv, es * inv


def _attn_in_specs():
    return [pl.BlockSpec((BLOCK, D_ATTN), lambda n: (n, 0)),
            pl.BlockSpec((BLOCK, 2 * D_KV), lambda n: (jnp.maximum(n - 1, 0), D_ATTN // (2 * D_KV))),
            pl.BlockSpec((BLOCK, 2 * D_KV), lambda n: (n, D_ATTN // (2 * D_KV))),
            _const((N_KV_HEADS, Q_PER_KV * BLOCK, 2 * BLOCK)),
            pl.BlockSpec(memory_space=pltpu.SMEM)]


def _unstack_heads(t):
    return jnp.concatenate([t[g * BLOCK:(g + 1) * BLOCK] for g in range(Q_PER_KV)], axis=1)


def _attn_fwd(proj, bias, sinks):
    def body(q_ref, kvp_ref, kvc_ref, bias_ref, sinks_ref, o_ref):
        n = pl.program_id(0)
        q, kvp, kvc = q_ref[...], kvp_ref[...], kvc_ref[...]
        outs = []
        for hk in range(N_KV_HEADS):
            _, _, vcat, probs, _ = _attn_pieces(n, q, kvp, kvc, bias_ref, sinks_ref, hk)
            outs.append(_unstack_heads(_dot(probs.astype(BF16), vcat)))
        o_ref[...] = jnp.concatenate(outs, axis=1)

    return pl.pallas_call(
        body, name="attn_fwd", grid=(SEQ // BLOCK,),
        in_specs=_attn_in_specs(),
        out_specs=pl.BlockSpec((BLOCK, D_ATTN), lambda n: (n, 0)),
        out_shape=jax.ShapeDtypeStruct((SEQ, D_ATTN), F32),
        compiler_params=_cp(("parallel",)),
    )(proj, proj, proj, bias, sinks)


def _attn_bwd(proj, bias, sinks, dcat):
    nb = SEQ // BLOCK

    def body(q_ref, kvp_ref, kvc_ref, bias_ref, sinks_ref, do_ref, dq_ref, dkv_ref, dbias_ref, dsink_ref, dsacc):
        n = pl.program_id(0)

        @pl.when(n == 0)
        def _():
            dkv_ref[...] = jnp.zeros_like(dkv_ref)
            dbias_ref[...] = jnp.zeros_like(dbias_ref)
            dsacc[...] = jnp.zeros_like(dsacc)

        q, kvp, kvc = q_ref[...], kvp_ref[...], kvc_ref[...]
        do_all = do_ref[...]
        dqs, dks, dvs = [], [], []
        for hk in range(N_KV_HEADS):
            qs, kcat, vcat, probs, psink = _attn_pieces(n, q, kvp, kvc, bias_ref, sinks_ref, hk)
            q0 = hk * Q_PER_KV * HEAD_DIM
            do = jnp.concatenate([do_all[:, q0 + g * HEAD_DIM:q0 + (g + 1) * HEAD_DIM] for g in range(Q_PER_KV)],
                                 axis=0).astype(BF16)
            dprobs = _dot(do, vcat, NT)
            dvs.append(_dot(probs.astype(BF16), do, TN))
            rowdot = jnp.sum(probs * dprobs, axis=-1, keepdims=True)
            ds = probs * (dprobs - rowdot)
            dsacc[hk] += -psink * rowdot
            dbias_ref[hk] += ds
            dsb = (ds * (HEAD_DIM ** -0.5)).astype(BF16)
            dqs.append(_unstack_heads(_dot(dsb, kcat)))
            dks.append(_dot(dsb, qs, TN))
        dq_ref[...] = jnp.concatenate(dqs, axis=1)
        upd = jnp.concatenate(dks + dvs, axis=1)
        cur = pl.multiple_of(n * BLOCK, BLOCK)
        dkv_ref[pl.ds(cur, BLOCK), :] += upd[BLOCK:]

        @pl.when(n > 0)
        def _():
            prev = pl.multiple_of((n - 1) * BLOCK, BLOCK)
            dkv_ref[pl.ds(prev, BLOCK), :] += upd[:BLOCK]

        @pl.when(n == nb - 1)
        def _():
            for hk in range(N_KV_HEADS):
                for g in range(Q_PER_KV):
                    tot = jnp.sum(dsacc[hk, g * BLOCK:(g + 1) * BLOCK, :], axis=0, keepdims=True)
                    h = hk * Q_PER_KV + g
                    dsink_ref[h:h + 1, :] = jnp.broadcast_to(tot, (1, LANES))

    return pl.pallas_call(
        body, name="attn_bwd", grid=(nb,),
        in_specs=_attn_in_specs() + [pl.BlockSpec((BLOCK, D_ATTN), lambda n: (n, 0))],
        out_specs=[pl.BlockSpec((BLOCK, D_ATTN), lambda n: (n, 0)), _const((SEQ, 2 * D_KV)),
                   _const((N_KV_HEADS, Q_PER_KV * BLOCK, 2 * BLOCK)), _const((N_Q_HEADS, LANES))],
        out_shape=[jax.ShapeDtypeStruct((SEQ, D_ATTN), F32), jax.ShapeDtypeStruct((SEQ, 2 * D_KV), F32),
                   jax.ShapeDtypeStruct((N_KV_HEADS, Q_PER_KV * BLOCK, 2 * BLOCK), F32),
                   jax.ShapeDtypeStruct((N_Q_HEADS, LANES), F32)],
        scratch_shapes=[pltpu.VMEM((N_KV_HEADS, Q_PER_KV * BLOCK, 1), F32)],
        compiler_params=_cp(("arbitrary",)),
    )(proj, proj, proj, bias, sinks, dcat)


@jax.custom_vjp
def _head_sum(x):
    ones = _head_ones(LANES)
    return jnp.concatenate([_dot_ind(x[:, c:c + LANES], ones, 2) for c in range(0, x.shape[-1], LANES)], axis=1)


_head_sum.defvjp(lambda x: (_head_sum(x), None), lambda _, ct: (_head_sum(ct),))


@jax.custom_vjp
def _bdot(a, w):
    return _dot(a.astype(BF16), w.astype(BF16))


def _bdot_bwd(res, ct):
    a, w = res
    ctb = ct.astype(BF16)
    return _dot(ctb, w.astype(BF16), NT), _dot(a.astype(BF16), ctb, TN)


_bdot.defvjp(lambda a, w: (_bdot(a, w), (a, w)), _bdot_bwd)


def _sigmoid(x):
    return 0.5 * (jnp.tanh(0.5 * x) + 1.0)


def _softplus(x):
    return jnp.maximum(x, 0.0) + jnp.log(1.0 + jnp.exp(-jnp.abs(x)))


def _rwkv_core(r, k, v, zwa, zg, w0, wdu, a0, wiu, wgu, k_k, k_a):
    w_log = -_softplus(-(w0 + _bdot(jnp.tanh(zwa), wdu))) - 0.5
    decay = jnp.exp(-jnp.exp(w_log))
    a = _sigmoid(a0 + _bdot(zwa, wiu))
    g = _bdot(_sigmoid(zg), wgu)
    kk = k * k_k
    kk = kk / jnp.maximum(jnp.sqrt(_head_sum(kk * kk)), 1e-12)
    k2 = k * (1.0 + (a - 1.0) * k_a)
    return r, decay, k2, v, -kk, kk * a, g


def _rwkv_out(o, r, k2, v, g, lng, lnb, rk):
    mu = _head_sum(o) * (1.0 / HEAD_DIM)
    d = o - mu
    var = _head_sum(d * d) * (1.0 / HEAD_DIM)
    on = d * lax.rsqrt(var + GN_EPS) * lng + lnb
    bonus = _head_sum(r * k2 * rk) * v
    return (on + bonus) * g


P_SPLITS = (0, 512, 1024, 1536, 1664, 1792)
N_PREP_PARAMS = 7
HALO = 8


def _shifted_pieces(i, p_ref, halo_ref, mix_ref):
    p = p_ref[:, P_OFF:]
    prev_row = halo_ref[HALO - 1:HALO, P_OFF:] * jnp.where(i > 0, 1.0, 0.0)
    row = lax.broadcasted_iota(jnp.int32, p.shape, 0)
    pprev = jnp.where(row == 0, prev_row, pltpu.roll(p, 1, 0))
    delta = pprev - p
    ps = p + delta * mix_ref[...]
    return [ps[:, a:b] for a, b in zip(P_SPLITS[:-1], P_SPLITS[1:])], delta


def _prep_in_specs():
    return [_rows(TR, D_IN),
            pl.BlockSpec((HALO, D_IN), lambda i: (jnp.maximum(i * (TR // HALO) - 1, 0), 0)),
            _const((1, RWKV_COLS)), _const((1, D_RWKV)), _const((LANES, D_RWKV)), _const((1, D_RWKV)),
            _const((LANES, D_RWKV)), _const((LANES, D_RWKV)), _const((1, D_RWKV)), _const((1, D_RWKV))]


def _rwkv_prep(proj, mix, prm):
    def body(p_ref, halo_ref, mix_ref, *refs):
        prm_refs, outs = refs[:N_PREP_PARAMS], refs[N_PREP_PARAMS:]
        pieces, _ = _shifted_pieces(pl.program_id(0), p_ref, halo_ref, mix_ref)
        vals = _rwkv_core(*pieces, *[t[...] for t in prm_refs])
        for ref, val in zip(outs, vals):
            ref[...] = val

    return pl.pallas_call(
        body, name="rwkv_prep", grid=(SEQ // TR,),
        in_specs=_prep_in_specs(),
        out_specs=[_rows(TR, D_RWKV)] * 7,
        out_shape=[jax.ShapeDtypeStruct((SEQ, D_RWKV), F32)] * 7,
        compiler_params=_cp(("parallel",)),
    )(proj, proj, mix, *prm)


def _rwkv_prep_bwd(proj, mix, prm, cts):
    def body(p_ref, halo_ref, mix_ref, *refs):
        i = pl.program_id(0)
        prm_refs = refs[:N_PREP_PARAMS]
        ct_refs = refs[N_PREP_PARAMS:N_PREP_PARAMS + 10]
        dps_ref, dmix_ref = refs[N_PREP_PARAMS + 10:N_PREP_PARAMS + 12]
        dprm_refs = refs[N_PREP_PARAMS + 12:]
        pieces, delta = _shifted_pieces(i, p_ref, halo_ref, mix_ref)
        _, vjp = jax.vjp(_rwkv_core, *pieces, *[t[...] for t in prm_refs])
        dr1, dr2, dw, dk1, dk2, dv1, dv2, dkkn, db, dg = [t[...] for t in ct_refs]
        grads = vjp((dr1 + dr2, dw, dk1 + dk2, dv1 + dv2, dkkn, db, dg))
        dps = jnp.concatenate(grads[:5], axis=1)
        dps_ref[...] = dps

        @pl.when(i == 0)
        def _():
            dmix_ref[...] = jnp.zeros_like(dmix_ref)
            for ref in dprm_refs:
                ref[...] = jnp.zeros_like(ref)

        dmix_ref[...] += jnp.sum(dps * delta, axis=0, keepdims=True)
        for ref, gval in zip(dprm_refs, grads[5:]):
            ref[...] += gval

    prm_shapes = [(1, D_RWKV), (LANES, D_RWKV), (1, D_RWKV), (LANES, D_RWKV), (LANES, D_RWKV), (1, D_RWKV), (1, D_RWKV)]
    return pl.pallas_call(
        body, name="rwkv_prep_bwd", grid=(SEQ // TR,),
        in_specs=_prep_in_specs() + [_rows(TR, D_RWKV)] * 10,
        out_specs=[_rows(TR, RWKV_COLS), _const((1, RWKV_COLS))] + [_const(s) for s in prm_shapes],
        out_shape=[jax.ShapeDtypeStruct((SEQ, RWKV_COLS), F32), jax.ShapeDtypeStruct((1, RWKV_COLS), F32)]
        + [jax.ShapeDtypeStruct(s, F32) for s in prm_shapes],
        compiler_params=_cp(("arbitrary",)),
    )(proj, proj, mix, *prm, *cts)


def _rwkv_post(o, r, k2, v, g, lng, lnb, rk, attn):
    def body(o_ref, r_ref, k_ref, v_ref, g_ref, lng_ref, lnb_ref, rk_ref, attn_ref, cat_ref):
        rw = _rwkv_out(*[t[...] for t in (o_ref, r_ref, k_ref, v_ref, g_ref, lng_ref, lnb_ref, rk_ref)])
        cat_ref[...] = jnp.concatenate([attn_ref[...], rw], axis=1).astype(BF16)

    return pl.pallas_call(
        body, name="rwkv_post", grid=(SEQ // TR,),
        in_specs=[_rows(TR, D_RWKV)] * 5 + [_const((1, D_RWKV))] * 3 + [_rows(TR, D_ATTN)],
        out_specs=_rows(TR, D_MODEL),
        out_shape=jax.ShapeDtypeStruct((SEQ, D_MODEL), BF16),
        compiler_params=_cp(("parallel",)),
    )(o, r, k2, v, g, lng, lnb, rk, attn)


def _rwkv_post_bwd(o, r, k2, v, g, lng, lnb, rk, dcat):
    def body(o_ref, r_ref, k_ref, v_ref, g_ref, lng_ref, lnb_ref, rk_ref, dcat_ref,
             do_ref, dr_ref, dk_ref, dv_ref, dg_ref, dlng_ref, dlnb_ref, drk_ref):
        i = pl.program_id(0)
        args = [t[...] for t in (o_ref, r_ref, k_ref, v_ref, g_ref, lng_ref, lnb_ref, rk_ref)]
        _, vjp = jax.vjp(_rwkv_out, *args)
        grads = vjp(dcat_ref[:, D_ATTN:])
        for ref, gval in zip((do_ref, dr_ref, dk_ref, dv_ref, dg_ref), grads[:5]):
            ref[...] = gval

        @pl.when(i == 0)
        def _():
            for ref in (dlng_ref, dlnb_ref, drk_ref):
                ref[...] = jnp.zeros_like(ref)

        for ref, gval in zip((dlng_ref, dlnb_ref, drk_ref), grads[5:]):
            ref[...] += gval

    return pl.pallas_call(
        body, name="rwkv_post_bwd", grid=(SEQ // TR,),
        in_specs=[_rows(TR, D_RWKV)] * 5 + [_const((1, D_RWKV))] * 3 + [_rows(TR, D_MODEL)],
        out_specs=[_rows(TR, D_RWKV)] * 5 + [_const((1, D_RWKV))] * 3,
        out_shape=[jax.ShapeDtypeStruct((SEQ, D_RWKV), F32)] * 5 + [jax.ShapeDtypeStruct((1, D_RWKV), F32)] * 3,
        compiler_params=_cp(("arbitrary",)),
    )(o, r, k2, v, g, lng, lnb, rk, dcat)


def _assemble_dproj(dq, dkv, dps, mix):
    last = SEQ // HALO - 1

    def body(dq_ref, dkv_ref, dps_ref, nxt_ref, mix_ref, o_ref):
        i = pl.program_id(0)
        dps = dps_ref[...]
        mixv = mix_ref[...]
        nxt_row = nxt_ref[0:1, :] * jnp.where(i < SEQ // TR - 1, 1.0, 0.0)
        row = lax.broadcasted_iota(jnp.int32, dps.shape, 0)
        up = jnp.where(row == TR - 1, nxt_row, pltpu.roll(dps, TR - 1, 0))
        dp = dps * (1.0 - mixv) + up * mixv
        o_ref[...] = jnp.concatenate([dq_ref[...], dkv_ref[...], dp], axis=1).astype(BF16)

    return pl.pallas_call(
        body, name="assemble_dproj", grid=(SEQ // TR,),
        in_specs=[_rows(TR, D_ATTN), _rows(TR, 2 * D_KV), _rows(TR, RWKV_COLS),
                  pl.BlockSpec((HALO, RWKV_COLS), lambda i: (jnp.minimum((i + 1) * (TR // HALO), last), 0)),
                  _const((1, RWKV_COLS))],
        out_specs=_rows(TR, D_IN),
        out_shape=jax.ShapeDtypeStruct((SEQ, D_IN), BF16),
        compiler_params=_cp(("parallel",)),
    )(dq, dkv, dps, dps, mix)


N_PAIR = D_RWKV // LANES
CHUNK = 64
N_CHUNK = SEQ // CHUNK
GROUP = 8
STATE = (N_PAIR, HEAD_DIM, LANES)


def _lane_sums(lhs_tiles, ones2):
    out = _dot(jnp.concatenate(lhs_tiles, axis=0), ones2)
    return [out[i * HEAD_DIM:(i + 1) * HEAD_DIM] for i in range(len(lhs_tiles))]


def _seg_sum(xs, ones2):
    return _lane_sums([jnp.concatenate(_split(x, 2), axis=1) for x in xs], ones2)


def _seg_sum_rows(xs, ones2):
    out = _dot(jnp.concatenate(_split(jnp.concatenate(xs, axis=0), 2), axis=1), ones2)
    return [out[i * GROUP:(i + 1) * GROUP] for i in range(len(xs))]


def _col_form(rows, diag, ones2):
    zero = jnp.zeros((HEAD_DIM, LANES), BF16)
    tiles = []
    for row in rows:
        hi = row.astype(BF16)
        lo = (row - hi.astype(F32)).astype(BF16)
        tiles.append(jnp.concatenate(
            [jnp.where(diag, jnp.broadcast_to(part, (HEAD_DIM, LANES)), zero) for part in (hi, lo)], axis=1))
    return _lane_sums(tiles, ones2)


def _scan_consts():
    ones2 = jnp.concatenate([_head_ones(LANES)] * 2, axis=0)
    sub = lax.broadcasted_iota(jnp.int32, (HEAD_DIM, LANES), 0)
    lane_in_head = lax.broadcasted_iota(jnp.int32, (HEAD_DIM, LANES), 1) & (HEAD_DIM - 1)
    return ones2, lane_in_head == sub, lane_in_head


def _rows_of_columns(tile):
    t = tile.T
    return jnp.concatenate([t[:CHUNK], t[HEAD_DIM:HEAD_DIM + CHUNK]], axis=1)


def _pair(j):
    return slice(j * LANES, (j + 1) * LANES)


def _scan_fwd(r, w, k, v, kkn, b):
    def body(r_ref, w_ref, k_ref, v_ref, kkn_ref, b_ref, o_ref, st_ref, sa_ref, s_scr):
        c = pl.program_id(0)
        ones2, diag, lane_in_head = _scan_consts()

        @pl.when(c == 0)
        def _():
            s_scr[...] = jnp.zeros_like(s_scr)

        def group(gi, carry):
            row0 = pl.multiple_of(gi * GROUP, GROUP)
            states, ocols = list(carry[:N_PAIR]), list(carry[N_PAIR:])
            tiles = [[t[pl.ds(row0, GROUP), _pair(j)] for t in (r_ref, w_ref, k_ref, v_ref, kkn_ref, b_ref)]
                     for j in range(N_PAIR)]
            def row(j, name, u):
                return tiles[j]["rwkvnb".index(name)][u:u + 1]

            def emit_out(u, after):
                outs = _seg_sum([s[j] * row(j, "r", u + d) for d, s in enumerate(after) for j in range(N_PAIR)], ones2)
                for d in range(2):
                    here = lane_in_head == gi * GROUP + u + d
                    for j in range(N_PAIR):
                        ocols[j] = jnp.where(here, outs[d * N_PAIR + j], ocols[j])

            def vcols_of(u):
                cols = _col_form([row(j, "v", u + d) for d in range(2) for j in range(N_PAIR)], diag, ones2)
                return cols[:N_PAIR], cols[N_PAIR:]

            n_next = [pltpu.roll(tiles[j][4], GROUP - 1, 0) for j in range(N_PAIR)]
            dots = _seg_sum_rows([tiles[j][5] * n_next[j] for j in range(N_PAIR)]
                                 + [tiles[j][2] * n_next[j] for j in range(N_PAIR)], ones2)
            b_n, k_n = dots[:N_PAIR], dots[N_PAIR:]
            w_n = [tiles[j][1] * n_next[j] for j in range(N_PAIR)]

            vcols = vcols_of(0)
            after = None
            for u in range(0, GROUP, 2):
                prods = _seg_sum([states[j] * row(j, "n", u) for j in range(N_PAIR)]
                                 + [states[j] * w_n[j][u:u + 1] for j in range(N_PAIR)], ones2)
                if after is not None:
                    emit_out(u - 2, after)
                nxt = vcols_of(u + 2) if u + 2 < GROUP else None
                first, second = [], []
                for j in range(N_PAIR):
                    sa1 = prods[j]
                    sa2 = prods[N_PAIR + j] + sa1 * b_n[j][u:u + 1] + vcols[0][j] * k_n[j][u:u + 1]
                    s1 = states[j] * row(j, "w", u) + sa1 * row(j, "b", u) + vcols[0][j] * row(j, "k", u)
                    s2 = s1 * row(j, "w", u + 1) + sa2 * row(j, "b", u + 1) + vcols[1][j] * row(j, "k", u + 1)
                    st_ref[row0 + u, j] = s1
                    sa_ref[row0 + u, j] = sa1
                    st_ref[row0 + u + 1, j] = s2
                    sa_ref[row0 + u + 1, j] = sa2
                    first.append(s1)
                    second.append(s2)
                    states[j] = s2
                after, vcols = (first, second), nxt
            emit_out(GROUP - 2, after)
            return tuple(states + ocols)

        zero = jnp.zeros((HEAD_DIM, LANES), F32)
        fin = lax.fori_loop(0, CHUNK // GROUP, group, tuple(s_scr[j] for j in range(N_PAIR)) + (zero,) * N_PAIR)
        for j in range(N_PAIR):
            s_scr[j] = fin[j]
            o_ref[:, _pair(j)] = _rows_of_columns(fin[N_PAIR + j])

    blk = pl.BlockSpec((CHUNK, D_RWKV), lambda c: (c, 0))
    per_step = pl.BlockSpec((CHUNK,) + STATE, lambda c: (c, 0, 0, 0))
    return pl.pallas_call(
        body, name="rwkv_scan_fwd", grid=(N_CHUNK,),
        in_specs=[blk] * 6,
        out_specs=[blk, per_step, per_step],
        out_shape=[jax.ShapeDtypeStruct((SEQ, D_RWKV), F32)] + [jax.ShapeDtypeStruct((SEQ,) + STATE, F32)] * 2,
        scratch_shapes=[pltpu.VMEM(STATE, F32)],
        compiler_params=_cp(("arbitrary",)),
    )(r, w, k, v, kkn, b)


def _scan_bwd(r, w, k, v, kkn, b, do, states, sas, ds_in, prev, name, first_chunk, n_chunks):
    top = first_chunk + n_chunks - 1

    def body(r_ref, w_ref, k_ref, v_ref, kkn_ref, b_ref, do_ref, st_ref, before_ref, sa_ref, ds_in_ref, *rest):
        dr_ref, dw_ref, dk_ref, dv_ref, dkkn_ref, db_ref, ds_out_ref, ds_scr = rest[-8:]
        i = pl.program_id(0)
        ones2, diag, lane_in_head = _scan_consts()

        @pl.when(i == 0)
        def _():
            ds_scr[...] = ds_in_ref[...]

        entry = [before_ref[0, j] * jnp.where(i < top, 1.0, 0.0) for j in range(N_PAIR)]

        def reverse(gr, carry):
            gi = CHUNK // GROUP - 1 - gr
            row0 = pl.multiple_of(gi * GROUP, GROUP)
            dstates, dvcols = list(carry[:N_PAIR]), list(carry[N_PAIR:])
            tiles = [[t[pl.ds(row0, GROUP), _pair(j)]
                      for t in (r_ref, w_ref, k_ref, v_ref, kkn_ref, b_ref, do_ref)] for j in range(N_PAIR)]
            rows = [[[None] * GROUP for _ in range(5)] for _ in range(N_PAIR)]

            def row(j, name, u):
                return tiles[j]["rwkvnbd".index(name)][u:u + 1]

            def cols_of(u):
                cols = _col_form([row(j, name, u - d) for d in range(2) for name in "dv" for j in range(N_PAIR)],
                                 diag, ones2)
                return [[(cols[(2 * d) * N_PAIR + j], cols[(2 * d + 1) * N_PAIR + j]) for j in range(N_PAIR)]
                        for d in range(2)]

            def emit_dv(u, dsps):
                outs = _seg_sum([dsp[j] * row(j, "k", u - d) for d, dsp in enumerate(dsps) for j in range(N_PAIR)], ones2)
                for d in range(2):
                    here = lane_in_head == gi * GROUP + u - d
                    for j in range(N_PAIR):
                        dvcols[j] = jnp.where(here, outs[d * N_PAIR + j], dvcols[j])

            b_prev = [pltpu.roll(tiles[j][5], 1, 0) for j in range(N_PAIR)]
            dots = _seg_sum_rows([tiles[j][4] * b_prev[j] for j in range(N_PAIR)]
                                 + [tiles[j][0] * tiles[j][5] for j in range(N_PAIR)], ones2)
            n_b, r_b = dots[:N_PAIR], dots[N_PAIR:]
            w_b = [tiles[j][1] * b_prev[j] for j in range(N_PAIR)]

            def outputs(u, j, dsp, dsa, docol, vcol):
                tl = gi * GROUP + u
                if u > 0:
                    s_prev = st_ref[tl - 1, j]
                else:
                    s_prev = jnp.where(gi == 0, entry[j], st_ref[jnp.maximum(tl - 1, 0), j])
                rows[j][0][u] = jnp.sum(st_ref[tl, j] * docol, axis=0, keepdims=True)
                rows[j][1][u] = jnp.sum(dsp * s_prev, axis=0, keepdims=True)
                rows[j][2][u] = jnp.sum(dsp * vcol, axis=0, keepdims=True)
                rows[j][3][u] = jnp.sum(s_prev * dsa, axis=0, keepdims=True)
                rows[j][4][u] = jnp.sum(dsp * sa_ref[tl, j], axis=0, keepdims=True)

            cols = cols_of(GROUP - 1)
            before = None
            for u in range(GROUP - 1, 0, -2):
                dsp1 = [dstates[j] + cols[0][j][0] * row(j, "r", u) for j in range(N_PAIR)]
                prods = _seg_sum([dsp1[j] * row(j, "b", u) for j in range(N_PAIR)]
                                 + [dsp1[j] * w_b[j][u:u + 1] for j in range(N_PAIR)], ones2)
                if before is not None:
                    emit_dv(u + 2, before)
                nxt = cols_of(u - 2) if u >= 2 else None
                dsp2 = []
                for j in range(N_PAIR):
                    dsa1 = prods[j]
                    dsa2 = prods[N_PAIR + j] + dsa1 * n_b[j][u:u + 1] + cols[1][j][0] * r_b[j][u - 1:u]
                    mid = dsp1[j] * row(j, "w", u) + dsa1 * row(j, "n", u) + cols[1][j][0] * row(j, "r", u - 1)
                    outputs(u, j, dsp1[j], dsa1, *cols[0][j])
                    outputs(u - 1, j, mid, dsa2, *cols[1][j])
                    dstates[j] = mid * row(j, "w", u - 1) + dsa2 * row(j, "n", u - 1)
                    dsp2.append(mid)
                before, cols = (dsp1, dsp2), nxt
            emit_dv(1, before)
            for j in range(N_PAIR):
                for ref, rr in zip((dr_ref, dw_ref, dk_ref, dkkn_ref, db_ref), rows[j]):
                    ref[pl.ds(row0, GROUP), _pair(j)] = jnp.concatenate(rr, axis=0)
            return tuple(dstates + dvcols)

        zero = jnp.zeros((HEAD_DIM, LANES), F32)
        dfin = lax.fori_loop(0, CHUNK // GROUP, reverse, tuple(ds_scr[j] for j in range(N_PAIR)) + (zero,) * N_PAIR)
        for j in range(N_PAIR):
            ds_scr[j] = dfin[j]
            dv_ref[:, _pair(j)] = _rows_of_columns(dfin[N_PAIR + j])

        @pl.when(i == n_chunks - 1)
        def _():
            ds_out_ref[...] = ds_scr[...]

    blk = pl.BlockSpec((CHUNK, D_RWKV), lambda i: (top - i, 0))
    per_step = pl.BlockSpec((CHUNK,) + STATE, lambda i: (top - i, 0, 0, 0))
    step_before = pl.BlockSpec((1,) + STATE, lambda i: (jnp.maximum((top - i) * CHUNK - 1, 0), 0, 0, 0))
    prev = [] if prev is None else list(prev)
    outs = pl.pallas_call(
        body, name=name, grid=(n_chunks,),
        in_specs=[blk] * 7 + [per_step, step_before, per_step, _const(STATE)] + [ANY] * len(prev),
        out_specs=[blk] * 6 + [_const(STATE)],
        out_shape=[jax.ShapeDtypeStruct((SEQ, D_RWKV), F32)] * 6 + [jax.ShapeDtypeStruct(STATE, F32)],
        scratch_shapes=[pltpu.VMEM(STATE, F32)],
        input_output_aliases={11 + t: t for t in range(len(prev))},
        compiler_params=_cp(("arbitrary",)),
    )(r, w, k, v, kkn, b, do, states, states, sas, ds_in, *prev)
    return outs[:6], outs[6]


def _stacked(rows, cols, pick):
    return pl.BlockSpec((None, rows, cols), pick)


def _local_step(x, target, sm, win_st):
    def tied(t, token):
        return t if token is None else t + token[0:1, 0:1].reshape((1,) * t.ndim)

    zpad = jnp.zeros((LORA_DECAY, D_RWKV), F32)
    prm = [sm["w0"], jnp.concatenate([sm["w_decay_up"], zpad], axis=0), sm["a0"],
           jnp.concatenate([zpad, sm["w_iclr_up"]], axis=0), sm["w_gate_up"], sm["k_k"], sm["k_a"]]
    mix = sm["rwkv_shift_mix"]
    onehot = jnp.asarray(_t5_onehot(), BF16)
    sinks = sm["sinks"].reshape(N_Q_HEADS)
    lng, lnb, rk = sm["ln_x_g"], sm["ln_x_b"], sm["r_k"].reshape(1, D_RWKV)

    h1 = _norm_cast(x, sm["norm_mix_pre"], "norm_in")
    proj = _matmul(h1, win_st, "nn", "proj", m=SEQ, n=D_IN, k=D_MODEL, tm=SEQ, tn=640,
                   b_spec=_stacked(D_MODEL, 640, lambda i, j: (j, 0, 0)))
    bias = _bias_table(sm["rel_bias"].T, onehot).reshape(N_KV_HEADS, Q_PER_KV * BLOCK, 2 * BLOCK)
    attn = _attn_fwd(proj, bias, sinks)
    r, w, k2, v, kkn, b, g = _rwkv_prep(proj, mix, prm)
    o, states, sas = _scan_fwd(r, w, k2, v, kkn, b)
    wout, wup_st, wdown = yield ("rest_weights", o)
    cat = _rwkv_post(o, r, k2, v, g, lng, lnb, rk, attn)
    mixo = _matmul(cat, wout, "nn", "out_proj", m=SEQ, n=D_MODEL, k=D_MODEL, tm=SEQ, tn=512)
    x2, h3 = _mix_norm(x, mixo, sm["norm_mix_post"], sm["norm_ffn_pre"])
    u_gate, u_val, act = _ffn_up_act(h3, wup_st, sm["conv_w"], sm["conv_b"])
    f = _matmul(act, wdown, "nn", "ffn_down", m=SEQ, n=D_MODEL, k=D_FF, tm=1024, tn=512)
    loss, dy, df, d_g4 = _loss_head(x2, f, sm["norm_ffn_post"], target)

    d_wdown = _matmul(act, df, "tn", "d_wdown", m=D_FF, n=D_MODEL, k=SEQ, tm=512, tn=D_MODEL)
    du, d_convw, d_convb = _ffn_act_bwd(u_gate, u_val, df, wdown, sm["conv_w"], sm["conv_b"])
    d_convw = d_convw.transpose(1, 0, 2).reshape(3, 2 * D_FF)
    d_convb = d_convb.reshape(1, 2 * D_FF)
    dh3 = _matmul_nt_shards(du, wup_st, "d_h3", m=SEQ, n=D_MODEL, tm=512, tn=512,
                            a_spec=pl.BlockSpec((2, 512, D_FF), lambda i, j: (0, i, 0)),
                            a_piece=lambda ref, s: ref[s // 2, :, (s % 2) * 2048:(s % 2 + 1) * 2048])
    d_wup = _matmul(h3, du, "tn", "d_wup", m=D_MODEL, n=2 * D_FF, k=SEQ, tm=D_MODEL, tn=512,
                    b_spec=pl.BlockSpec((None, SEQ, 512), lambda i, j: (j // 8, 0, j % 8)),
                    out=((N_CHIPS, D_MODEL, 2048), _stacked(D_MODEL, 512, lambda i, j: (j // 4, 0, j % 4))))
    dx2, dmix, d_g2, d_g3 = _mid_bwd(x2, mixo, dy, dh3, sm["norm_mix_post"], sm["norm_ffn_pre"])
    dcat = _matmul(dmix, wout, "nt", "d_cat", m=SEQ, n=D_MODEL, k=D_MODEL, tm=SEQ, tn=512)
    d_wout = _matmul(cat, dmix, "tn", "d_wout", m=D_MODEL, n=D_MODEL, k=SEQ, tm=512, tn=D_MODEL)
    token = yield ("grads_a", (d_wdown, d_wup, d_wout))
    do, dr_p, dk_p, dv_p, dg, d_lng, d_lnb, d_rk = _rwkv_post_bwd(o, r, k2, v, g, lng, tied(lnb, token), rk, dcat)
    half = N_CHUNK // 2
    ds_end = jnp.zeros(STATE, F32)
    late, ds_mid = _scan_bwd(r, w, k2, v, kkn, b, do, states, sas, ds_end, None, "rwkv_scan_bwd_late", half, half)
    token = yield ("seam_1", ds_mid)
    scan_cts, ds_first = _scan_bwd(r, w, k2, v, kkn, b, do, states, sas, tied(ds_mid, token), late,
                                   "rwkv_scan_bwd_early", 0, half)
    dr_s, dw_s, dk_s, dv_s, dkkn_s, db_s = scan_cts
    token = yield ("seam_2", ds_first)
    prep_grads = _rwkv_prep_bwd(proj, tied(mix, token), prm,
                                (dr_s, dr_p, dw_s, dk_s, dk_p, dv_s, dv_p, dkkn_s, db_s, dg))
    dps, d_mix, d_w0, d_wdu, d_a0, d_wiu, d_wgu, d_kk, d_ka = prep_grads
    dq, dkv, dbias, dsink = _attn_bwd(proj, bias, sinks, dcat)
    d_relb = _bias_table_bwd(dbias.reshape(N_Q_HEADS, N_REL), onehot).T
    dproj = _assemble_dproj(dq, dkv, dps, mix)
    d_win = _matmul(h1, dproj, "tn", "d_win", m=D_MODEL, n=D_IN, k=SEQ, tm=D_MODEL, tn=640,
                    out=((N_CHIPS, D_MODEL, 640), _stacked(D_MODEL, 640, lambda i, j: (j, 0, 0))))
    token = yield ("grads_b", d_win)
    dh1 = _matmul_nt_shards(dproj, win_st, "d_h1", m=SEQ, n=D_MODEL, tm=1024, tn=D_MODEL,
                            a_spec=pl.BlockSpec((1024, D_IN), lambda i, j: (i, 0)),
                            a_piece=lambda ref, s: ref[:, s * 640:(s + 1) * 640])
    grad_x, d_g1 = _first_bwd(x, dx2, dh1, tied(sm["norm_mix_pre"], token))

    grads = {
        "norm_mix_pre": d_g1, "norm_mix_post": d_g2, "norm_ffn_pre": d_g3, "norm_ffn_post": d_g4,
        "w_in": d_win, "rel_bias": d_relb, "sinks": dsink[:, 0].reshape(1, N_Q_HEADS),
        "rwkv_shift_mix": d_mix, "w0": d_w0, "w_decay_up": d_wdu[:LORA_DECAY], "a0": d_a0,
        "w_iclr_up": d_wiu[LORA_DECAY:], "w_gate_up": d_wgu, "k_k": d_kk, "k_a": d_ka,
        "r_k": d_rk.reshape(1, N_Q_HEADS, HEAD_DIM), "ln_x_g": d_lng, "ln_x_b": d_lnb,
        "w_out": d_wout, "w_ffn_up": d_wup, "conv_w": d_convw, "conv_b": d_convb, "w_ffn_down": d_wdown,
    }
    return loss, grad_x, grads


def _place():
    x, y, c = lax.axis_index("x"), lax.axis_index("y"), lax.axis_index("c")
    chips = [(1 - x, y), (x, 1 - y), (1 - x, 1 - y)]
    return x, y, c, chips


def _remote(src, dst, sems, idx, to):
    return pltpu.make_async_remote_copy(src_ref=src, dst_ref=dst, send_sem=sems[0].at[idx], recv_sem=sems[1].at[idx],
                                        device_id=to, device_id_type=MESH)


def _half(c, rows):
    return pl.ds(pl.multiple_of(c * (rows // 2), 16), rows // 2)


def _gather_weights(big, small):
    nb, ns = len(big), len(small)

    def body(*refs):
        ins, outs = refs[:nb + ns], refs[nb + ns:2 * (nb + ns)]
        ici, d2d, sml, loc = refs[2 * (nb + ns):2 * (nb + ns) + 2], refs[-5:-3], refs[-3:-1], refs[-1]
        x, y, c, chips = _place()
        me = 2 * x + y
        sib = (x, y, 1 - c)
        local = [pltpu.make_async_copy(ins[a], outs[a].at[me], loc.at[a]) for a in range(nb + ns)]
        for cp in local:
            cp.start()
        sends = []
        for a in range(nb):
            rows = _half(c, big[a].shape[0])
            for kk, chip in enumerate(chips):
                sends.append(_remote(ins[a].at[rows], outs[a].at[me, rows], ici, a * 3 + kk, (*chip, c)))
        for a in range(ns):
            for kk, chip in enumerate(chips):
                sends.append(_remote(ins[nb + a], outs[nb + a].at[me], sml, a * 3 + kk, (*chip, c)))
        for cp in sends:
            cp.start()
        passed = []
        for a in range(nb):
            rows = _half(c, big[a].shape[0])
            for kk, (px, py) in enumerate(chips):
                got = outs[a].at[2 * px + py, rows]
                _remote(got, got, ici, a * 3 + kk, sib).wait_recv()
                fwd = _remote(got, got, d2d, a * 3 + kk, sib)
                fwd.start()
                passed.append(fwd)
        for a in range(nb):
            other = _half(1 - c, big[a].shape[0])
            for kk, (px, py) in enumerate(chips):
                land = outs[a].at[2 * px + py, other]
                _remote(land, land, d2d, a * 3 + kk, sib).wait_recv()
        for a in range(ns):
            for kk, (px, py) in enumerate(chips):
                land = outs[nb + a].at[2 * px + py]
                _remote(land, land, sml, a * 3 + kk, sib).wait_recv()
        for cp in sends + passed:
            cp.wait_send()
        for cp in local:
            cp.wait()

    arrs = list(big) + list(small)
    return pl.pallas_call(
        body, name="gather_weights",
        in_specs=[ANY] * len(arrs), out_specs=[ANY] * len(arrs),
        out_shape=[jax.ShapeDtypeStruct((N_CHIPS,) + t.shape, t.dtype) for t in arrs],
        scratch_shapes=[pltpu.SemaphoreType.DMA((3 * nb,)), pltpu.SemaphoreType.DMA((3 * nb,)),
                        pltpu.SemaphoreType.DMA((3 * nb,)), pltpu.SemaphoreType.DMA((3 * nb,)),
                        pltpu.SemaphoreType.DMA((3 * ns,)), pltpu.SemaphoreType.DMA((3 * ns,)),
                        pltpu.SemaphoreType.DMA((nb + ns,))],
        compiler_params=pltpu.CompilerParams(has_side_effects=True),
    )(*arrs)


HBM = pl.BlockSpec(memory_space=pltpu.HBM)
SEM = pl.BlockSpec(memory_space=pltpu.SEMAPHORE)
EFFECT = pltpu.SideEffectType.DATAFLOW_SIDE_EFFECTING


def _copies_start(name, bufs, plan, n):
    nb = len(bufs)

    def body(*refs):
        ins, sems, token = refs[:nb], refs[nb:nb + 2 * n], refs[-1]
        for kk, (src, dst, dev) in enumerate(plan(ins)):
            pltpu.make_async_remote_copy(src_ref=src, dst_ref=dst, send_sem=sems[2 * kk], recv_sem=sems[2 * kk + 1],
                                         device_id=dev, device_id_type=MESH).start()
        token[...] = jnp.zeros_like(token)

    outs = pl.pallas_call(
        body, name=name,
        out_shape=tuple([pltpu.SemaphoreType.DMA(())] * (2 * n) + [pltpu.HBM(t.shape, t.dtype) for t in bufs]
                        + [jax.ShapeDtypeStruct((8, LANES), F32)]),
        in_specs=[HBM] * nb,
        out_specs=tuple([SEM] * (2 * n) + [HBM] * nb + [pl.BlockSpec(memory_space=pltpu.VMEM)]),
        input_output_aliases={t: 2 * n + t for t in range(nb)},
        compiler_params=pltpu.CompilerParams(has_side_effects=EFFECT),
    )(*[pltpu.with_memory_space_constraint(t, pltpu.HBM) for t in bufs])
    return outs[:2 * n], outs[2 * n:2 * n + nb], outs[-1]


def _copies_wait(name, sems, bufs, plan, n, after):
    nb = len(bufs)
    after = list(after) if isinstance(after, (list, tuple)) else [after]

    def body(*refs):
        ins, sem_refs = refs[:nb], refs[nb:nb + 2 * n]
        for kk, (src, dst, dev) in enumerate(plan(ins)):
            cp = pltpu.make_async_remote_copy(src_ref=src, dst_ref=dst, send_sem=sem_refs[2 * kk],
                                              recv_sem=sem_refs[2 * kk + 1], device_id=dev, device_id_type=MESH)
            cp.wait_send()
            cp.wait_recv()

    return pl.pallas_call(
        body, name=name,
        out_shape=tuple(pltpu.HBM(t.shape, t.dtype) for t in bufs),
        in_specs=[HBM] * nb + [SEM] * (2 * n) + [ANY] * len(after),
        out_specs=tuple([HBM] * nb),
        input_output_aliases={t: t for t in range(nb)},
        compiler_params=pltpu.CompilerParams(has_side_effects=EFFECT),
    )(*bufs, *sems, *after)


def _plan_gather(n_w):
    def plan(refs):
        x, y, c, chips = _place()
        me = 2 * x + y
        return [(refs[a], refs[n_w + a].at[me], (*chip, c)) for a in range(n_w) for chip in chips]
    return plan


def _plan_pair_halves(n_g, rows):
    def plan(refs):
        x, y, c, _ = _place()
        return [(refs[a].at[:, _half(1 - c, rows[a])], refs[n_g + a], (x, y, 1 - c)) for a in range(n_g)]
    return plan


def _plan_chip_parts(n_g):
    def plan(refs):
        x, y, c, chips = _place()
        me = 2 * x + y
        return [(refs[a].at[2 * px + py], refs[n_g + a].at[me], (px, py, c))
                for a in range(n_g) for (px, py) in chips]
    return plan


def _plan_pair_fill(n_g, rows):
    def plan(refs):
        x, y, c, _ = _place()
        return [(refs[a].at[_half(c, rows[a])], refs[a].at[_half(c, rows[a])], (x, y, 1 - c)) for a in range(n_g)]
    return plan


def _pair_add(g, got, name):
    _, rows, cols = g.shape
    hr = rows // 2
    tr = min(hr, 256)
    nb = hr // tr

    def body(g_ref, got_ref, p_ref, own_ref):
        val = (g_ref[...] + got_ref[...]).astype(BF16)
        p_ref[...] = val

        @pl.when(pl.program_id(1) == 2 * lax.axis_index("x") + lax.axis_index("y"))
        def _():
            own_ref[...] = val

    def mine(i, s):
        return (2 * lax.axis_index("x") + lax.axis_index("y"), i, 0)

    return pl.pallas_call(
        body, name=name, grid=(nb, N_CHIPS),
        in_specs=[pl.BlockSpec((None, tr, cols), lambda i, s: (s, lax.axis_index("c") * nb + i, 0)),
                  pl.BlockSpec((None, tr, cols), lambda i, s: (s, i, 0))],
        out_specs=[pl.BlockSpec((None, tr, cols), lambda i, s: (s, i, 0)), pl.BlockSpec((None, tr, cols), mine)],
        out_shape=[jax.ShapeDtypeStruct((N_CHIPS, hr, cols), BF16)] * 2,
        compiler_params=_cp(("parallel", "arbitrary")),
    )(g, got)


def _chip_sum(parts, name):
    _, hr, cols = parts.shape
    tr = min(hr, 128)
    nb = hr // tr

    def body(t_ref, o_ref):
        part = [t_ref[s].astype(F32) for s in range(N_CHIPS)]
        o_ref[...] = ((part[0] + part[1]) + part[2]) + part[3]

    return pl.pallas_call(
        body, name=name, grid=(nb,),
        in_specs=[pl.BlockSpec((N_CHIPS, tr, cols), lambda i: (0, i, 0))],
        out_specs=pl.BlockSpec((tr, cols), lambda i: (lax.axis_index("c") * nb + i, 0)),
        out_shape=jax.ShapeDtypeStruct((2 * hr, cols), F32),
        compiler_params=_cp(("parallel",)),
    )(parts)


class _Reduction:
    def __init__(self, tag, rows):
        self.tag, self.n, self.rows = tag, len(rows), rows
        self.plans = (_plan_pair_halves(self.n, rows), _plan_chip_parts(self.n), _plan_pair_fill(self.n, rows))
        self.flight = None

    def _name(self, what):
        return f"grad_{self.tag}_{what}"

    def start(self, gs):
        gots = [lax.empty((N_CHIPS, t.shape[1] // 2, t.shape[2]), F32) for t in gs]
        self.flight = _copies_start(self._name("pair_start"), list(gs) + gots, self.plans[0], self.n)
        return self.flight[2]

    def after_pair(self, after):
        sems, bufs, _ = self.flight
        out = _copies_wait(self._name("pair_wait"), sems, bufs, self.plans[0], self.n, after)
        sums = [_pair_add(g, got, self._name(f"pair_add_{i}"))
                for i, (g, got) in enumerate(zip(out[:self.n], out[self.n:]))]
        self.flight = _copies_start(self._name("chip_start"), [p for p, _ in sums] + [own for _, own in sums],
                                    self.plans[1], 3 * self.n)
        return self.flight[2]

    def after_chips(self, after):
        sems, bufs, _ = self.flight
        out = _copies_wait(self._name("chip_wait"), sems, bufs, self.plans[1], 3 * self.n, after)
        fulls = [_chip_sum(t, self._name(f"chip_sum_{i}")) for i, t in enumerate(out[self.n:])]
        self.flight = _copies_start(self._name("fill_start"), fulls, self.plans[2], self.n)
        return self.flight[2]

    def finish(self, after):
        sems, bufs, _ = self.flight
        return _copies_wait(self._name("fill_wait"), sems, bufs, self.plans[2], self.n, after)


def _adamw_math(w, g, m, v):
    nm = ADAM_B1 * m + (1.0 - ADAM_B1) * g
    nv = ADAM_B2 * v + (1.0 - ADAM_B2) * (g * g)
    m_hat = nm / (1.0 - ADAM_B1 ** ADAM_STEP)
    v_hat = nv / (1.0 - ADAM_B2 ** ADAM_STEP)
    return -ADAM_LR * (m_hat / (jnp.sqrt(v_hat) + ADAM_EPS) + ADAM_WD * w), nm, nv


def _adamw(w, g, m, v, name, tr):
    r, cdim = w.shape

    def body(w_ref, g_ref, m_ref, v_ref, d_ref, nm_ref, nv_ref):
        d_ref[...], nm_ref[...], nv_ref[...] = _adamw_math(w_ref[...], g_ref[...], m_ref[...], v_ref[...])

    return pl.pallas_call(
        body, name=name, grid=(r // tr,), in_specs=[_rows(tr, cdim)] * 4, out_specs=[_rows(tr, cdim)] * 3,
        out_shape=[jax.ShapeDtypeStruct((r, cdim), F32)] * 3, compiler_params=_cp(("parallel",)),
    )(w, g, m, v)


def _adamw_small(w, parts, m, v):
    def body(w_ref, p_ref, m_ref, v_ref, d_ref, nm_ref, nv_ref, g_ref):
        g = p_ref[0]
        for dev in range(1, N_DEV):
            g = g + p_ref[dev]
        g_ref[...] = g
        d_ref[...], nm_ref[...], nv_ref[...] = _adamw_math(w_ref[...], g, m_ref[...], v_ref[...])

    return pl.pallas_call(
        body, name="adamw_small", grid=(1,),
        in_specs=[_const(w.shape), _const(parts.shape), _const(w.shape), _const(w.shape)],
        out_specs=[_const(w.shape)] * 4, out_shape=[jax.ShapeDtypeStruct(w.shape, F32)] * 4,
        compiler_params=_cp(("arbitrary",)),
    )(w, parts, m, v)


REPLICATED = (("norm_mix_pre", 1024), ("norm_mix_post", 1024), ("norm_ffn_pre", 1024), ("norm_ffn_post", 1024),
              ("rel_bias", 256), ("sinks", 8), ("rwkv_shift_mix", 1792), ("w0", 512), ("a0", 512), ("k_k", 512),
              ("k_a", 512), ("r_k", 512), ("ln_x_g", 512), ("ln_x_b", 512), ("conv_b", 8192))
SMALL_SHARDED = (("w_decay_up", LORA_DECAY, D_RWKV), ("w_iclr_up", LORA_ICLR, D_RWKV),
                 ("w_gate_up", LORA_GATE, D_RWKV), ("conv_w", 3, 2 * D_FF))
BIG = (("w_in", D_MODEL, 640), ("w_out", 256, D_MODEL), ("w_ffn_up", D_MODEL, 2048), ("w_ffn_down", 1024, D_MODEL))
PACK_ALIGN = 8 * LANES


def _pack(pieces):
    flat = []
    for t in pieces:
        t = t.reshape(-1)
        pad = (-t.shape[0]) % LANES
        flat.append(jnp.pad(t, (0, pad)) if pad else t)
    flat = jnp.concatenate(flat)
    pad = (-flat.shape[0]) % PACK_ALIGN
    return jnp.pad(flat, (0, pad)).reshape(-1, LANES)


def _unpack(buf, sizes):
    flat, out, off = buf.reshape(-1), [], 0
    for n in sizes:
        out.append(flat[off:off + n])
        off += n + ((-n) % LANES)
    return out


def kernel(x, norm_mix_pre, norm_mix_post, norm_ffn_pre, norm_ffn_post, w_in, rel_bias, sinks, rwkv_shift_mix, w0, w_decay_up, a0, w_iclr_up, w_gate_up, k_k, k_a, r_k, ln_x_g, ln_x_b, w_out, w_ffn_up, conv_w, conv_b, w_ffn_down, loss_target, m_norm_mix_pre, m_norm_mix_post, m_norm_ffn_pre, m_norm_ffn_post, m_w_in, m_rel_bias, m_sinks, m_rwkv_shift_mix, m_w0, m_w_decay_up, m_a0, m_w_iclr_up, m_w_gate_up, m_k_k, m_k_a, m_r_k, m_ln_x_g, m_ln_x_b, m_w_out, m_w_ffn_up, m_conv_w, m_conv_b, m_w_ffn_down, v_norm_mix_pre, v_norm_mix_post, v_norm_ffn_pre, v_norm_ffn_post, v_w_in, v_rel_bias, v_sinks, v_rwkv_shift_mix, v_w0, v_w_decay_up, v_a0, v_w_iclr_up, v_w_gate_up, v_k_k, v_k_a, v_r_k, v_ln_x_g, v_ln_x_b, v_w_out, v_w_ffn_up, v_conv_w, v_conv_b, v_w_ffn_down):
    given = dict(locals())
    names = [n for n, _ in REPLICATED] + [n for n, _, _ in SMALL_SHARDED] + [n for n, _, _ in BIG]
    order = ["norm_mix_pre", "norm_mix_post", "norm_ffn_pre", "norm_ffn_post", "w_in", "rel_bias", "sinks",
             "rwkv_shift_mix", "w0", "w_decay_up", "a0", "w_iclr_up", "w_gate_up", "k_k", "k_a", "r_k", "ln_x_g",
             "ln_x_b", "w_out", "w_ffn_up", "conv_w", "conv_b", "w_ffn_down"]
    assert sorted(names) == sorted(order)
    shard = 2 * lax.axis_index("x") + lax.axis_index("y")

    big_sh = {n: given[n].reshape(a, b).astype(BF16) for n, a, b in BIG}
    small_sh = [given[n].reshape(r, c // N_CHIPS) for n, r, c in SMALL_SHARDED]
    gathered = _gather_weights([big_sh["w_in"]], small_sh)
    rest = ("w_out", "w_ffn_up", "w_ffn_down")
    win_st, rest_sh = lax.optimization_barrier((gathered[0], [big_sh[n] for n in rest]))
    sm = {n: given[n] for n, _ in REPLICATED}
    sm["r_k"] = r_k.reshape(N_Q_HEADS, HEAD_DIM)
    for (n, r, c), st in zip(SMALL_SHARDED, gathered[1:]):
        sm[n] = st.transpose(1, 0, 2).reshape(r, c)

    lands = [lax.dynamic_update_slice(lax.empty((N_CHIPS,) + t.shape, BF16), t[None], (shard, 0, 0)) for t in rest_sh]
    plan_w = _plan_gather(len(rest))
    w_sems, w_bufs, token = _copies_start("gather_rest_start", rest_sh + lands, plan_w, 9)
    sm["norm_mix_pre"] = norm_mix_pre + token[0:1, 0:1]

    def on_rest_weights(after):
        out = _copies_wait("gather_rest_wait", w_sems, w_bufs, plan_w, 9, after)
        wout_st, wup_st, wdown_st = out[3:]
        return wout_st.reshape(D_MODEL, D_MODEL), wup_st, wdown_st.reshape(D_FF, D_MODEL)

    red_a = _Reduction("a", (1024, D_MODEL, 256))
    red_b = _Reduction("b", (D_MODEL,))

    def on_grads_a(gs):
        d_wdown, d_wup, d_wout = gs
        return red_a.start([d_wdown.reshape(N_CHIPS, 1024, D_MODEL), d_wup, d_wout.reshape(N_CHIPS, 256, D_MODEL)])

    handlers = {"rest_weights": on_rest_weights, "grads_a": on_grads_a, "seam_1": red_a.after_pair,
                "seam_2": red_a.after_chips, "grads_b": lambda g: red_b.start([g])}
    steps = _local_step(x[0], loss_target[0], sm, win_st)
    kind, payload = next(steps)
    while True:
        try:
            kind, payload = steps.send(handlers[kind](payload))
        except StopIteration as done:
            loss, grad_x, grads = done.value
            break

    small_names = [n for n, _ in REPLICATED] + [n for n, _, _ in SMALL_SHARDED]

    def shard_cols(t, s):
        return t[:, s * (t.shape[1] // N_CHIPS):(s + 1) * (t.shape[1] // N_CHIPS)]

    for_chip = jnp.stack([_pack([loss[0]] + [grads[n] for n, _ in REPLICATED]
                                + [shard_cols(grads[n], s) for n, _, _ in SMALL_SHARDED]) for s in range(N_CHIPS)])
    me = 2 * shard + lax.axis_index("c")
    mine = lax.dynamic_index_in_dim(for_chip, shard, 0, keepdims=True)
    land = lax.dynamic_update_slice(lax.empty((N_DEV,) + for_chip.shape[1:], F32), mine, (me, 0, 0))

    def plan_small(refs):
        x, y, c, _ = _place()
        out = []
        for rel in range(1, N_DEV):
            px, py, pc = x ^ (rel >> 2), y ^ ((rel >> 1) & 1), c ^ (rel & 1)
            out.append((refs[0].at[2 * px + py], refs[1].at[4 * x + 2 * y + c], (px, py, pc)))
        return out

    s_sems, s_bufs, _ = _copies_start("grad_small_start", [for_chip, land], plan_small, N_DEV - 1)

    red_b.after_pair(grad_x)
    g_out = {}
    g_out["w_ffn_down"], g_out["w_ffn_up"], g_out["w_out"] = red_a.finish(grad_x)

    delta, new_m, new_v = {}, {}, {}

    def update(n, a, b):
        delta[n], new_m[n], new_v[n] = _adamw(given[n].reshape(a, b), g_out[n], given["m_" + n].reshape(a, b),
                                              given["v_" + n].reshape(a, b), "adamw_" + n, 128)

    for n, a, b in BIG[1:]:
        update(n, a, b)
    done = [delta[n] for n, _, _ in BIG[1:]]
    red_b.after_chips(done)
    parts = _copies_wait("grad_small_wait", s_sems, s_bufs, plan_small, N_DEV - 1, done)[1]
    no_param = jnp.zeros((LANES,), F32)
    packs = [_pack([no_param] + [given[pre + n] for n in small_names]) for pre in ("", "m_", "v_")]
    small_sizes = [LANES] + [int(np.prod(given[n].shape)) for n in small_names]
    upd = [_unpack(t, small_sizes) for t in _adamw_small(packs[0], parts, packs[1], packs[2])]
    loss = upd[3][0][0]
    for n, d, nm, nv, g in zip(small_names, *[u[1:] for u in upd]):
        shape = given[n].shape
        delta[n], new_m[n], new_v[n], g_out[n] = (t.reshape(shape) for t in (d, nm, nv, g))
    g_out["w_in"], = red_b.finish(upd[0][0])
    update(*BIG[0])

    def shaped(d):
        return [d[n].reshape(given[n].shape) for n in order]

    return (loss, grad_x.reshape(x.shape), *shaped(g_out), *shaped(delta), *shaped(new_m), *shaped(new_v))
```

```python
import math

import numpy as np
import jax
import jax.numpy as jnp
from jax import lax
from jax.experimental import pallas as pl
from jax.experimental.pallas import tpu as pltpu

F32 = jnp.float32
BF16 = jnp.bfloat16
MESH = pl.DeviceIdType.MESH

SEQ = 2048
D_MODEL = 1024
HEAD_DIM = 64
D_ATTN = 512
D_RWKV = 512
D_KV = 128
N_Q_HEADS = 8
N_KV_HEADS = 2
Q_PER_KV = 4
BLOCK = 128
N_BUCKETS = 32
MAX_DISTANCE = 128
LORA_DECAY = 64
LORA_ICLR = 64
LORA_GATE = 128
RWKV_COLS = 3 * D_RWKV + LORA_DECAY + LORA_ICLR + LORA_GATE
P_OFF = D_ATTN + 2 * D_KV
D_IN = P_OFF + RWKV_COLS
D_FF = 4096
NORM_EPS = 1e-6
GN_EPS = 64e-5
NEG_INF = -1e30
N_CHIPS = 4
N_DEV = 8

ADAM_LR = 0.001
ADAM_B1 = 0.9
ADAM_B2 = 0.999
ADAM_EPS = 1e-08
ADAM_WD = 0.01
ADAM_STEP = 10

VMEM_LIMIT = 52 * 1024 * 1024
LANES = 128


def _cp(sem=None, vmem=VMEM_LIMIT):
    kw = dict(vmem_limit_bytes=vmem)
    if sem is not None:
        kw["dimension_semantics"] = sem
    return pltpu.CompilerParams(**kw)


def _rows(tr, nc):
    return pl.BlockSpec((tr, nc), lambda i: (i, 0))


def _const(shape):
    return pl.BlockSpec(shape, lambda *_: (0,) * len(shape))


ANY = pl.BlockSpec(memory_space=pl.ANY)


def _split(x, n):
    parts = []
    for _ in range(n - 1):
        h = x.astype(BF16)
        parts.append(h)
        x = x - h.astype(F32)
    parts.append(x.astype(BF16))
    return parts


def _dot(a, b, dn=(((1,), (0,)), ((), ()))):
    return lax.dot_general(a, b, dn, preferred_element_type=F32)


NN = (((1,), (0,)), ((), ()))
NT = (((1,), (1,)), ((), ()))
TN = (((0,), (0,)), ((), ()))


def _dot_ind(x, ind_bf16, n=3):
    acc = None
    for part in _split(x, n):
        t = _dot(part, ind_bf16)
        acc = t if acc is None else acc + t
    return acc


def _head_ones(n):
    r = lax.broadcasted_iota(jnp.int32, (n, n), 0) >> 6
    c = lax.broadcasted_iota(jnp.int32, (n, n), 1) >> 6
    return jnp.where(r == c, 1.0, 0.0).astype(BF16)


def _matmul(a, b, mode, name, *, m, n, k, tm, tn, a_spec=None, b_spec=None, out=None, out_dtype=F32):
    keep_at = mode == "tn" and m == tm and n > tn

    def body(a_ref, b_ref, o_ref, *scratch):
        if keep_at:
            at_ref, = scratch

            @pl.when(pl.program_id(1) == 0)
            def _():
                at_ref[...] = a_ref[...].T

            o_ref[...] = _dot(at_ref[...], b_ref[...], NN).astype(out_dtype)
        else:
            o_ref[...] = _dot(a_ref[...], b_ref[...], {"nn": NN, "nt": NT, "tn": TN}[mode]).astype(out_dtype)

    if a_spec is None:
        a_spec = pl.BlockSpec((k, tm), lambda i, j: (0, i)) if mode == "tn" else pl.BlockSpec((tm, k), lambda i, j: (i, 0))
    if b_spec is None:
        b_spec = pl.BlockSpec((tn, k), lambda i, j: (j, 0)) if mode == "nt" else pl.BlockSpec((k, tn), lambda i, j: (0, j))
    return pl.pallas_call(
        body, name=name, grid=(m // tm, n // tn),
        in_specs=[a_spec, b_spec],
        out_specs=pl.BlockSpec((tm, tn), lambda i, j: (i, j)) if out is None else out[1],
        out_shape=jax.ShapeDtypeStruct((m, n) if out is None else out[0], out_dtype),
        scratch_shapes=[pltpu.VMEM((tm, k), a.dtype)] if keep_at else [],
        compiler_params=_cp(("parallel", "arbitrary" if keep_at else "parallel")),
    )(a, b)


def _matmul_nt_shards(a, b_st, name, *, m, n, tm, tn, a_spec, a_piece):
    ks = b_st.shape[2]

    def body(a_ref, b_ref, o_ref):
        acc = _dot(a_piece(a_ref, 0), b_ref[0], NT)
        for s in range(1, N_CHIPS):
            acc = acc + _dot(a_piece(a_ref, s), b_ref[s], NT)
        o_ref[...] = acc

    return pl.pallas_call(
        body, name=name, grid=(m // tm, n // tn),
        in_specs=[a_spec, pl.BlockSpec((N_CHIPS, tn, ks), lambda i, j: (0, j, 0))],
        out_specs=pl.BlockSpec((tm, tn), lambda i, j: (i, j)),
        out_shape=jax.ShapeDtypeStruct((m, n), F32),
        compiler_params=_cp(("parallel", "parallel")),
    )(a, b_st)


def _rstd(x):
    return lax.rsqrt(jnp.mean(x * x, axis=-1, keepdims=True) + NORM_EPS)


def _rms_bwd(x, r, g, dy):
    gy = dy * g
    return r * gy - x * ((r * r * r) * (jnp.sum(x * gy, axis=-1, keepdims=True) / x.shape[-1]))


TR = 256


def _norm_cast(x, g, name):
    def body(x_ref, g_ref, h_ref):
        x = x_ref[...]
        h_ref[...] = (x * _rstd(x) * g_ref[...]).astype(BF16)

    return pl.pallas_call(
        body, name=name, grid=(SEQ // TR,),
        in_specs=[_rows(TR, D_MODEL), _const((1, D_MODEL))],
        out_specs=_rows(TR, D_MODEL),
        out_shape=jax.ShapeDtypeStruct((SEQ, D_MODEL), BF16),
        compiler_params=_cp(("parallel",)),
    )(x, g)


def _mix_norm(x, mix, g2, g3):
    def body(x_ref, mix_ref, g2_ref, g3_ref, x2_ref, h3_ref):
        mixv = mix_ref[...]
        x2 = x_ref[...] + mixv * _rstd(mixv) * g2_ref[...]
        x2_ref[...] = x2
        h3_ref[...] = (x2 * _rstd(x2) * g3_ref[...]).astype(BF16)

    return pl.pallas_call(
        body, name="mix_norm", grid=(SEQ // TR,),
        in_specs=[_rows(TR, D_MODEL), _rows(TR, D_MODEL), _const((1, D_MODEL)), _const((1, D_MODEL))],
        out_specs=[_rows(TR, D_MODEL), _rows(TR, D_MODEL)],
        out_shape=[jax.ShapeDtypeStruct((SEQ, D_MODEL), F32), jax.ShapeDtypeStruct((SEQ, D_MODEL), BF16)],
        compiler_params=_cp(("parallel",)),
    )(x, mix, g2, g3)


def _loss_head(x2, f, g4, target):
    def body(x2_ref, f_ref, g4_ref, t_ref, loss_ref, dy_ref, df_ref, dg_ref):
        i = pl.program_id(0)
        f = f_ref[...]
        g4 = g4_ref[...]
        r = _rstd(f)
        e = x2_ref[...] + f * r * g4 - t_ref[...]
        dy = e * (1.0 / D_MODEL)
        dy_ref[...] = dy
        df_ref[...] = _rms_bwd(f, r, g4, dy).astype(BF16)
        part = 0.5 * jnp.sum(jnp.sum(e * e, axis=-1, keepdims=True), axis=0, keepdims=True) * (1.0 / D_MODEL)
        dg = jnp.sum(dy * f * r, axis=0, keepdims=True)

        @pl.when(i == 0)
        def _():
            loss_ref[...] = jnp.zeros_like(loss_ref)
            dg_ref[...] = jnp.zeros_like(dg_ref)

        loss_ref[...] += jnp.broadcast_to(part, loss_ref.shape)
        dg_ref[...] += dg

    return pl.pallas_call(
        body, name="loss_head", grid=(SEQ // TR,),
        in_specs=[_rows(TR, D_MODEL), _rows(TR, D_MODEL), _const((1, D_MODEL)), _rows(TR, D_MODEL)],
        out_specs=[_const((8, LANES)), _rows(TR, D_MODEL), _rows(TR, D_MODEL), _const((1, D_MODEL))],
        out_shape=[jax.ShapeDtypeStruct((8, LANES), F32), jax.ShapeDtypeStruct((SEQ, D_MODEL), F32),
                   jax.ShapeDtypeStruct((SEQ, D_MODEL), BF16), jax.ShapeDtypeStruct((1, D_MODEL), F32)],
        compiler_params=_cp(("arbitrary",)),
    )(x2, f, g4, target)


def _mid_bwd(x2, mix, dy, dh3, g2, g3):
    def body(x2_ref, mix_ref, dy_ref, dh3_ref, g2_ref, g3_ref, dx2_ref, dmix_ref, dg2_ref, dg3_ref):
        i = pl.program_id(0)
        x2 = x2_ref[...]
        mixv = mix_ref[...]
        dh3 = dh3_ref[...]
        r3 = _rstd(x2)
        dx2 = dy_ref[...] + _rms_bwd(x2, r3, g3_ref[...], dh3)
        dx2_ref[...] = dx2
        r2 = _rstd(mixv)
        dmix_ref[...] = _rms_bwd(mixv, r2, g2_ref[...], dx2).astype(BF16)

        @pl.when(i == 0)
        def _():
            dg2_ref[...] = jnp.zeros_like(dg2_ref)
            dg3_ref[...] = jnp.zeros_like(dg3_ref)

        dg3_ref[...] += jnp.sum(dh3 * x2 * r3, axis=0, keepdims=True)
        dg2_ref[...] += jnp.sum(dx2 * mixv * r2, axis=0, keepdims=True)

    return pl.pallas_call(
        body, name="mid_bwd", grid=(SEQ // TR,),
        in_specs=[_rows(TR, D_MODEL)] * 4 + [_const((1, D_MODEL))] * 2,
        out_specs=[_rows(TR, D_MODEL), _rows(TR, D_MODEL), _const((1, D_MODEL)), _const((1, D_MODEL))],
        out_shape=[jax.ShapeDtypeStruct((SEQ, D_MODEL), F32), jax.ShapeDtypeStruct((SEQ, D_MODEL), BF16),
                   jax.ShapeDtypeStruct((1, D_MODEL), F32), jax.ShapeDtypeStruct((1, D_MODEL), F32)],
        compiler_params=_cp(("arbitrary",)),
    )(x2, mix, dy, dh3, g2, g3)


def _first_bwd(x, dx2, dh1, g1):
    def body(x_ref, dx2_ref, dh1_ref, g1_ref, dx_ref, dg1_ref):
        i = pl.program_id(0)
        x = x_ref[...]
        dh1 = dh1_ref[...]
        r = _rstd(x)
        dx_ref[...] = dx2_ref[...] + _rms_bwd(x, r, g1_ref[...], dh1)

        @pl.when(i == 0)
        def _():
            dg1_ref[...] = jnp.zeros_like(dg1_ref)

        dg1_ref[...] += jnp.sum(dh1 * x * r, axis=0, keepdims=True)

    return pl.pallas_call(
        body, name="first_bwd", grid=(SEQ // TR,),
        in_specs=[_rows(TR, D_MODEL)] * 3 + [_const((1, D_MODEL))],
        out_specs=[_rows(TR, D_MODEL), _const((1, D_MODEL))],
        out_shape=[jax.ShapeDtypeStruct((SEQ, D_MODEL), F32), jax.ShapeDtypeStruct((1, D_MODEL), F32)],
        compiler_params=_cp(("arbitrary",)),
    )(x, dx2, dh1, g1)


TC = 256
N_CB = D_FF // TC
GELU_C = math.sqrt(2.0 / math.pi)


def _shift_down(u, s):
    rolled = pltpu.roll(u, s, 0)
    row = lax.broadcasted_iota(jnp.int32, u.shape, 0)
    return jnp.where(row >= s, rolled, 0.0)


def _shift_up(u, s):
    n = u.shape[0]
    rolled = pltpu.roll(u, n - s, 0)
    row = lax.broadcasted_iota(jnp.int32, u.shape, 0)
    return jnp.where(row < n - s, rolled, 0.0)


def _conv3(u, w, b):
    return b + w[0:1] * _shift_down(u, 2) + w[1:2] * _shift_down(u, 1) + w[2:3] * u


def _gelu_and_grad(x):
    inner = GELU_C * (x + 0.044715 * (x * x * x))
    t = jnp.tanh(inner)
    gelu = 0.5 * x * (1.0 + t)
    dgelu = 0.5 * (1.0 + t) + 0.5 * x * (1.0 - t * t) * (GELU_C * (1.0 + 3 * 0.044715 * (x * x)))
    return gelu, dgelu


def _ffn_specs():
    col = lambda off: pl.BlockSpec((SEQ, TC), lambda *g: (0, g[-1] + off))
    w = lambda off: pl.BlockSpec((3, TC), lambda *g: (0, g[-1] + off))
    b = lambda off: pl.BlockSpec((1, TC), lambda *g: (0, g[-1] + off))
    return col, w, b


def _ffn_up_act(h3, wup_st, conv_w, conv_b):
    col, w, b = _ffn_specs()
    per_shard = wup_st.shape[2] // TC

    def body(h_ref, upg_ref, upv_ref, wg_ref, wv_ref, bg_ref, bv_ref, ug_ref, uv_ref, act_ref):
        h = h_ref[...]
        ug = _dot(h, upg_ref[...])
        uv = _dot(h, upv_ref[...])
        ug_ref[...] = ug
        uv_ref[...] = uv
        gate = _conv3(ug, wg_ref[...], bg_ref[...])
        val = _conv3(uv, wv_ref[...], bv_ref[...])
        act_ref[...] = (_gelu_and_grad(gate)[0] * val).astype(BF16)

    return pl.pallas_call(
        body, name="ffn_up_act", grid=(N_CB,),
        in_specs=[_const((SEQ, D_MODEL)),
                  pl.BlockSpec((None, D_MODEL, TC), lambda j: (j // per_shard, 0, j % per_shard)),
                  pl.BlockSpec((None, D_MODEL, TC), lambda j: (2 + j // per_shard, 0, j % per_shard)),
                  w(0), w(N_CB), b(0), b(N_CB)],
        out_specs=[col(0)] * 3,
        out_shape=[jax.ShapeDtypeStruct((SEQ, D_FF), F32)] * 2 + [jax.ShapeDtypeStruct((SEQ, D_FF), BF16)],
        compiler_params=_cp(("parallel",)),
    )(h3, wup_st, wup_st, conv_w, conv_w, conv_b, conv_b)


def _ffn_act_bwd(u_gate, u_val, df, wdown, conv_w, conv_b):
    col, w, b = _ffn_specs()
    both = lambda rows: pl.BlockSpec((2, rows, TC), lambda j: (0, 0, j))

    def body(ug_ref, uv_ref, df_ref, wd_ref, wg_ref, wv_ref, bg_ref, bv_ref, du_ref, dw_ref, db_ref):
        da = _dot(df_ref[...], wd_ref[...], NT)
        ug, uv = ug_ref[...], uv_ref[...]
        wg, wv = wg_ref[...], wv_ref[...]
        gate = _conv3(ug, wg, bg_ref[...])
        val = _conv3(uv, wv, bv_ref[...])
        gelu, dgelu = _gelu_and_grad(gate)
        for h, (duc, uh, wh) in enumerate(((da * val * dgelu, ug, wg), (da * gelu, uv, wv))):
            up1, up2 = _shift_up(duc, 1), _shift_up(duc, 2)
            du_ref[h] = (wh[2:3] * duc + wh[1:2] * up1 + wh[0:1] * up2).astype(BF16)
            db_ref[h] = jnp.sum(duc, axis=0, keepdims=True)
            dw_ref[h] = jnp.concatenate(
                [jnp.sum(up2 * uh, axis=0, keepdims=True), jnp.sum(up1 * uh, axis=0, keepdims=True),
                 jnp.sum(duc * uh, axis=0, keepdims=True)], axis=0)

    return pl.pallas_call(
        body, name="ffn_act_bwd", grid=(N_CB,),
        in_specs=[col(0), col(0), _const((SEQ, D_MODEL)), pl.BlockSpec((TC, D_MODEL), lambda j: (j, 0)),
                  w(0), w(N_CB), b(0), b(N_CB)],
        out_specs=[both(SEQ), both(3), both(1)],
        out_shape=[jax.ShapeDtypeStruct((2, SEQ, D_FF), BF16), jax.ShapeDtypeStruct((2, 3, D_FF), F32),
                   jax.ShapeDtypeStruct((2, 1, D_FF), F32)],
        compiler_params=_cp(("parallel",)),
    )(u_gate, u_val, df, wdown, conv_w, conv_w, conv_b, conv_b)


def _t5_onehot():
    rel = (np.arange(BLOCK)[:, None] + BLOCK) - np.arange(2 * BLOCK)[None, :]
    n = np.maximum(rel, 0)
    max_exact = N_BUCKETS // 2
    large = max_exact + (np.log(np.maximum(n, 1).astype(np.float32) / np.float32(max_exact))
                         / np.float32(math.log(MAX_DISTANCE / max_exact))
                         * np.float32(N_BUCKETS - max_exact)).astype(np.int32)
    large = np.minimum(large, N_BUCKETS - 1)
    bucket = np.where(n < max_exact, n, large).reshape(-1)
    return (bucket[None, :] == np.arange(N_BUCKETS)[:, None]).astype(np.float32)


N_REL = BLOCK * 2 * BLOCK


def _bias_table(rel_bias_t, onehot):
    def body(rb_ref, oh_ref, o_ref):
        o_ref[...] = _dot_ind(rb_ref[...], oh_ref[...])

    return pl.pallas_call(
        body, name="bias_table", grid=(1,),
        in_specs=[_const((N_Q_HEADS, N_BUCKETS)), _const((N_BUCKETS, N_REL))],
        out_specs=_const((N_Q_HEADS, N_REL)),
        out_shape=jax.ShapeDtypeStruct((N_Q_HEADS, N_REL), F32),
        compiler_params=_cp(("arbitrary",)),
    )(rel_bias_t, onehot)


def _bias_table_bwd(dbias, onehot):
    def body(db_ref, oh_ref, o_ref):
        acc = None
        for part in _split(db_ref[...], 3):
            t = _dot(part, oh_ref[...], NT)
            acc = t if acc is None else acc + t
        o_ref[...] = acc

    return pl.pallas_call(
        body, name="bias_table_bwd", grid=(1,),
        in_specs=[_const((N_Q_HEADS, N_REL)), _const((N_BUCKETS, N_REL))],
        out_specs=_const((N_Q_HEADS, N_BUCKETS)),
        out_shape=jax.ShapeDtypeStruct((N_Q_HEADS, N_BUCKETS), F32),
        compiler_params=_cp(("arbitrary",)),
    )(dbias, onehot)


def _attn_pieces(n, q, kvp, kvc, bias_ref, sinks_ref, hk):
    qi = lax.broadcasted_iota(jnp.int32, (BLOCK, 2 * BLOCK), 0)
    kj = lax.broadcasted_iota(jnp.int32, (BLOCK, 2 * BLOCK), 1)
    rel = qi + BLOCK - kj
    first_key = jnp.where(n > 0, 0, BLOCK)
    ok = jnp.where(rel >= 0, jnp.where(rel < BLOCK, jnp.where(kj >= first_key, 1.0, 0.0), 0.0), 0.0)
    ok4 = jnp.concatenate([ok] * Q_PER_KV, axis=0) > 0.5
    c0 = hk * HEAD_DIM
    kcat = jnp.concatenate([kvp[:, c0:c0 + HEAD_DIM], kvc[:, c0:c0 + HEAD_DIM]], axis=0).astype(BF16)
    vcat = jnp.concatenate([kvp[:, D_KV + c0:D_KV + c0 + HEAD_DIM], kvc[:, D_KV + c0:D_KV + c0 + HEAD_DIM]],
                           axis=0).astype(BF16)
    q0 = hk * Q_PER_KV * HEAD_DIM
    qs = jnp.concatenate([q[:, q0 + g * HEAD_DIM:q0 + (g + 1) * HEAD_DIM] for g in range(Q_PER_KV)],
                         axis=0).astype(BF16)
    s = _dot(qs, kcat, NT) * (HEAD_DIM ** -0.5) + bias_ref[hk]
    s = jnp.where(ok4, s, NEG_INF)
    row = lax.broadcasted_iota(jnp.int32, (Q_PER_KV * BLOCK, 1), 0)
    sink = jnp.zeros((Q_PER_KV * BLOCK, 1), F32)
    for g in range(Q_PER_KV):
        sink = jnp.where((row >> 7) == g, sinks_ref[hk * Q_PER_KV + g], sink)
    m = jnp.maximum(jnp.max(s, axis=-1, keepdims=True), sink)
    p = jnp.exp(s - m)
    es = jnp.exp(sink - m)
    inv = 1.0 / (jnp.sum(p, axis=-1, keepdims=True) + es)
    return qs, kcat, vcat, p * inv, es * inv


def _attn_in_specs():
    return [pl.BlockSpec((BLOCK, D_ATTN), lambda n: (n, 0)),
            pl.BlockSpec((BLOCK, 2 * D_KV), lambda n: (jnp.maximum(n - 1, 0), D_ATTN // (2 * D_KV))),
            pl.BlockSpec((BLOCK, 2 * D_KV), lambda n: (n, D_ATTN // (2 * D_KV))),
            _const((N_KV_HEADS, Q_PER_KV * BLOCK, 2 * BLOCK)),
            pl.BlockSpec(memory_space=pltpu.SMEM)]


def _unstack_heads(t):
    return jnp.concatenate([t[g * BLOCK:(g + 1) * BLOCK] for g in range(Q_PER_KV)], axis=1)


def _attn_fwd(proj, bias, sinks):
    def body(q_ref, kvp_ref, kvc_ref, bias_ref, sinks_ref, o_ref):
        n = pl.program_id(0)
        q, kvp, kvc = q_ref[...], kvp_ref[...], kvc_ref[...]
        outs = []
        for hk in range(N_KV_HEADS):
            _, _, vcat, probs, _ = _attn_pieces(n, q, kvp, kvc, bias_ref, sinks_ref, hk)
            outs.append(_unstack_heads(_dot(probs.astype(BF16), vcat)))
        o_ref[...] = jnp.concatenate(outs, axis=1)

    return pl.pallas_call(
        body, name="attn_fwd", grid=(SEQ // BLOCK,),
        in_specs=_attn_in_specs(),
        out_specs=pl.BlockSpec((BLOCK, D_ATTN), lambda n: (n, 0)),
        out_shape=jax.ShapeDtypeStruct((SEQ, D_ATTN), F32),
        compiler_params=_cp(("parallel",)),
    )(proj, proj, proj, bias, sinks)


def _attn_bwd(proj, bias, sinks, dcat):
    nb = SEQ // BLOCK

    def body(q_ref, kvp_ref, kvc_ref, bias_ref, sinks_ref, do_ref, dq_ref, dkv_ref, dbias_ref, dsink_ref, dsacc):
        n = pl.program_id(0)

        @pl.when(n == 0)
        def _():
            dkv_ref[...] = jnp.zeros_like(dkv_ref)
            dbias_ref[...] = jnp.zeros_like(dbias_ref)
            dsacc[...] = jnp.zeros_like(dsacc)

        q, kvp, kvc = q_ref[...], kvp_ref[...], kvc_ref[...]
        do_all = do_ref[...]
        dqs, dks, dvs = [], [], []
        for hk in range(N_KV_HEADS):
            qs, kcat, vcat, probs, psink = _attn_pieces(n, q, kvp, kvc, bias_ref, sinks_ref, hk)
            q0 = hk * Q_PER_KV * HEAD_DIM
            do = jnp.concatenate([do_all[:, q0 + g * HEAD_DIM:q0 + (g + 1) * HEAD_DIM] for g in range(Q_PER_KV)],
                                 axis=0).astype(BF16)
            dprobs = _dot(do, vcat, NT)
            dvs.append(_dot(probs.astype(BF16), do, TN))
            rowdot = jnp.sum(probs * dprobs, axis=-1, keepdims=True)
            ds = probs * (dprobs - rowdot)
            dsacc[hk] += -psink * rowdot
            dbias_ref[hk] += ds
            dsb = (ds * (HEAD_DIM ** -0.5)).astype(BF16)
            dqs.append(_unstack_heads(_dot(dsb, kcat)))
            dks.append(_dot(dsb, qs, TN))
        dq_ref[...] = jnp.concatenate(dqs, axis=1)
        upd = jnp.concatenate(dks + dvs, axis=1)
        cur = pl.multiple_of(n * BLOCK, BLOCK)
        dkv_ref[pl.ds(cur, BLOCK), :] += upd[BLOCK:]

        @pl.when(n > 0)
        def _():
            prev = pl.multiple_of((n - 1) * BLOCK, BLOCK)
            dkv_ref[pl.ds(prev, BLOCK), :] += upd[:BLOCK]

        @pl.when(n == nb - 1)
        def _():
            for hk in range(N_KV_HEADS):
                for g in range(Q_PER_KV):
                    tot = jnp.sum(dsacc[hk, g * BLOCK:(g + 1) * BLOCK, :], axis=0, keepdims=True)
                    h = hk * Q_PER_KV + g
                    dsink_ref[h:h + 1, :] = jnp.broadcast_to(tot, (1, LANES))

    return pl.pallas_call(
        body, name="attn_bwd", grid=(nb,),
        in_specs=_attn_in_specs() + [pl.BlockSpec((BLOCK, D_ATTN), lambda n: (n, 0))],
        out_specs=[pl.BlockSpec((BLOCK, D_ATTN), lambda n: (n, 0)), _const((SEQ, 2 * D_KV)),
                   _const((N_KV_HEADS, Q_PER_KV * BLOCK, 2 * BLOCK)), _const((N_Q_HEADS, LANES))],
        out_shape=[jax.ShapeDtypeStruct((SEQ, D_ATTN), F32), jax.ShapeDtypeStruct((SEQ, 2 * D_KV), F32),
                   jax.ShapeDtypeStruct((N_KV_HEADS, Q_PER_KV * BLOCK, 2 * BLOCK), F32),
                   jax.ShapeDtypeStruct((N_Q_HEADS, LANES), F32)],
        scratch_shapes=[pltpu.VMEM((N_KV_HEADS, Q_PER_KV * BLOCK, 1), F32)],
        compiler_params=_cp(("arbitrary",)),
    )(proj, proj, proj, bias, sinks, dcat)


@jax.custom_vjp
def _head_sum(x):
    ones = _head_ones(LANES)
    return jnp.concatenate([_dot_ind(x[:, c:c + LANES], ones, 2) for c in range(0, x.shape[-1], LANES)], axis=1)


_head_sum.defvjp(lambda x: (_head_sum(x), None), lambda _, ct: (_head_sum(ct),))


@jax.custom_vjp
def _bdot(a, w):
    return _dot(a.astype(BF16), w.astype(BF16))


def _bdot_bwd(res, ct):
    a, w = res
    ctb = ct.astype(BF16)
    return _dot(ctb, w.astype(BF16), NT), _dot(a.astype(BF16), ctb, TN)


_bdot.defvjp(lambda a, w: (_bdot(a, w), (a, w)), _bdot_bwd)


def _sigmoid(x):
    return 0.5 * (jnp.tanh(0.5 * x) + 1.0)


def _softplus(x):
    return jnp.maximum(x, 0.0) + jnp.log(1.0 + jnp.exp(-jnp.abs(x)))


def _rwkv_core(r, k, v, zwa, zg, w0, wdu, a0, wiu, wgu, k_k, k_a):
    w_log = -_softplus(-(w0 + _bdot(jnp.tanh(zwa), wdu))) - 0.5
    decay = jnp.exp(-jnp.exp(w_log))
    a = _sigmoid(a0 + _bdot(zwa, wiu))
    g = _bdot(_sigmoid(zg), wgu)
    kk = k * k_k
    kk = kk / jnp.maximum(jnp.sqrt(_head_sum(kk * kk)), 1e-12)
    k2 = k * (1.0 + (a - 1.0) * k_a)
    return r, decay, k2, v, -kk, kk * a, g


def _rwkv_out(o, r, k2, v, g, lng, lnb, rk):
    mu = _head_sum(o) * (1.0 / HEAD_DIM)
    d = o - mu
    var = _head_sum(d * d) * (1.0 / HEAD_DIM)
    on = d * lax.rsqrt(var + GN_EPS) * lng + lnb
    bonus = _head_sum(r * k2 * rk) * v
    return (on + bonus) * g


P_SPLITS = (0, 512, 1024, 1536, 1664, 1792)
N_PREP_PARAMS = 7
HALO = 8


def _shifted_pieces(i, p_ref, halo_ref, mix_ref):
    p = p_ref[:, P_OFF:]
    prev_row = halo_ref[HALO - 1:HALO, P_OFF:] * jnp.where(i > 0, 1.0, 0.0)
    row = lax.broadcasted_iota(jnp.int32, p.shape, 0)
    pprev = jnp.where(row == 0, prev_row, pltpu.roll(p, 1, 0))
    delta = pprev - p
    ps = p + delta * mix_ref[...]
    return [ps[:, a:b] for a, b in zip(P_SPLITS[:-1], P_SPLITS[1:])], delta


def _prep_in_specs():
    return [_rows(TR, D_IN),
            pl.BlockSpec((HALO, D_IN), lambda i: (jnp.maximum(i * (TR // HALO) - 1, 0), 0)),
            _const((1, RWKV_COLS)), _const((1, D_RWKV)), _const((LANES, D_RWKV)), _const((1, D_RWKV)),
            _const((LANES, D_RWKV)), _const((LANES, D_RWKV)), _const((1, D_RWKV)), _const((1, D_RWKV))]


def _rwkv_prep(proj, mix, prm):
    def body(p_ref, halo_ref, mix_ref, *refs):
        prm_refs, outs = refs[:N_PREP_PARAMS], refs[N_PREP_PARAMS:]
        pieces, _ = _shifted_pieces(pl.program_id(0), p_ref, halo_ref, mix_ref)
        vals = _rwkv_core(*pieces, *[t[...] for t in prm_refs])
        for ref, val in zip(outs, vals):
            ref[...] = val

    return pl.pallas_call(
        body, name="rwkv_prep", grid=(SEQ // TR,),
        in_specs=_prep_in_specs(),
        out_specs=[_rows(TR, D_RWKV)] * 7,
        out_shape=[jax.ShapeDtypeStruct((SEQ, D_RWKV), F32)] * 7,
        compiler_params=_cp(("parallel",)),
    )(proj, proj, mix, *prm)


def _rwkv_prep_bwd(proj, mix, prm, cts):
    def body(p_ref, halo_ref, mix_ref, *refs):
        i = pl.program_id(0)
        prm_refs = refs[:N_PREP_PARAMS]
        ct_refs = refs[N_PREP_PARAMS:N_PREP_PARAMS + 10]
        dps_ref, dmix_ref = refs[N_PREP_PARAMS + 10:N_PREP_PARAMS + 12]
        dprm_refs = refs[N_PREP_PARAMS + 12:]
        pieces, delta = _shifted_pieces(i, p_ref, halo_ref, mix_ref)
        _, vjp = jax.vjp(_rwkv_core, *pieces, *[t[...] for t in prm_refs])
        dr1, dr2, dw, dk1, dk2, dv1, dv2, dkkn, db, dg = [t[...] for t in ct_refs]
        grads = vjp((dr1 + dr2, dw, dk1 + dk2, dv1 + dv2, dkkn, db, dg))
        dps = jnp.concatenate(grads[:5], axis=1)
        dps_ref[...] = dps

        @pl.when(i == 0)
        def _():
            dmix_ref[...] = jnp.zeros_like(dmix_ref)
            for ref in dprm_refs:
                ref[...] = jnp.zeros_like(ref)

        dmix_ref[...] += jnp.sum(dps * delta, axis=0, keepdims=True)
        for ref, gval in zip(dprm_refs, grads[5:]):
            ref[...] += gval

    prm_shapes = [(1, D_RWKV), (LANES, D_RWKV), (1, D_RWKV), (LANES, D_RWKV), (LANES, D_RWKV), (1, D_RWKV), (1, D_RWKV)]
    return pl.pallas_call(
        body, name="rwkv_prep_bwd", grid=(SEQ // TR,),
        in_specs=_prep_in_specs() + [_rows(TR, D_RWKV)] * 10,
        out_specs=[_rows(TR, RWKV_COLS), _const((1, RWKV_COLS))] + [_const(s) for s in prm_shapes],
        out_shape=[jax.ShapeDtypeStruct((SEQ, RWKV_COLS), F32), jax.ShapeDtypeStruct((1, RWKV_COLS), F32)]
        + [jax.ShapeDtypeStruct(s, F32) for s in prm_shapes],
        compiler_params=_cp(("arbitrary",)),
    )(proj, proj, mix, *prm, *cts)


def _rwkv_post(o, r, k2, v, g, lng, lnb, rk, attn):
    def body(o_ref, r_ref, k_ref, v_ref, g_ref, lng_ref, lnb_ref, rk_ref, attn_ref, cat_ref):
        rw = _rwkv_out(*[t[...] for t in (o_ref, r_ref, k_ref, v_ref, g_ref, lng_ref, lnb_ref, rk_ref)])
        cat_ref[...] = jnp.concatenate([attn_ref[...], rw], axis=1).astype(BF16)

    return pl.pallas_call(
        body, name="rwkv_post", grid=(SEQ // TR,),
        in_specs=[_rows(TR, D_RWKV)] * 5 + [_const((1, D_RWKV))] * 3 + [_rows(TR, D_ATTN)],
        out_specs=_rows(TR, D_MODEL),
        out_shape=jax.ShapeDtypeStruct((SEQ, D_MODEL), BF16),
        compiler_params=_cp(("parallel",)),
    )(o, r, k2, v, g, lng, lnb, rk, attn)


def _rwkv_post_bwd(o, r, k2, v, g, lng, lnb, rk, dcat):
    def body(o_ref, r_ref, k_ref, v_ref, g_ref, lng_ref, lnb_ref, rk_ref, dcat_ref,
             do_ref, dr_ref, dk_ref, dv_ref, dg_ref, dlng_ref, dlnb_ref, drk_ref):
        i = pl.program_id(0)
        args = [t[...] for t in (o_ref, r_ref, k_ref, v_ref, g_ref, lng_ref, lnb_ref, rk_ref)]
        _, vjp = jax.vjp(_rwkv_out, *args)
        grads = vjp(dcat_ref[:, D_ATTN:])
        for ref, gval in zip((do_ref, dr_ref, dk_ref, dv_ref, dg_ref), grads[:5]):
            ref[...] = gval

        @pl.when(i == 0)
        def _():
            for ref in (dlng_ref, dlnb_ref, drk_ref):
                ref[...] = jnp.zeros_like(ref)

        for ref, gval in zip((dlng_ref, dlnb_ref, drk_ref), grads[5:]):
            ref[...] += gval

    return pl.pallas_call(
        body, name="rwkv_post_bwd", grid=(SEQ // TR,),
        in_specs=[_rows(TR, D_RWKV)] * 5 + [_const((1, D_RWKV))] * 3 + [_rows(TR, D_MODEL)],
        out_specs=[_rows(TR, D_RWKV)] * 5 + [_const((1, D_RWKV))] * 3,
        out_shape=[jax.ShapeDtypeStruct((SEQ, D_RWKV), F32)] * 5 + [jax.ShapeDtypeStruct((1, D_RWKV), F32)] * 3,
        compiler_params=_cp(("arbitrary",)),
    )(o, r, k2, v, g, lng, lnb, rk, dcat)


def _assemble_dproj(dq, dkv, dps, mix):
    last = SEQ // HALO - 1

    def body(dq_ref, dkv_ref, dps_ref, nxt_ref, mix_ref, o_ref):
        i = pl.program_id(0)
        dps = dps_ref[...]
        mixv = mix_ref[...]
        nxt_row = nxt_ref[0:1, :] * jnp.where(i < SEQ // TR - 1, 1.0, 0.0)
        row = lax.broadcasted_iota(jnp.int32, dps.shape, 0)
        up = jnp.where(row == TR - 1, nxt_row, pltpu.roll(dps, TR - 1, 0))
        dp = dps * (1.0 - mixv) + up * mixv
        o_ref[...] = jnp.concatenate([dq_ref[...], dkv_ref[...], dp], axis=1).astype(BF16)

    return pl.pallas_call(
        body, name="assemble_dproj", grid=(SEQ // TR,),
        in_specs=[_rows(TR, D_ATTN), _rows(TR, 2 * D_KV), _rows(TR, RWKV_COLS),
                  pl.BlockSpec((HALO, RWKV_COLS), lambda i: (jnp.minimum((i + 1) * (TR // HALO), last), 0)),
                  _const((1, RWKV_COLS))],
        out_specs=_rows(TR, D_IN),
        out_shape=jax.ShapeDtypeStruct((SEQ, D_IN), BF16),
        compiler_params=_cp(("parallel",)),
    )(dq, dkv, dps, dps, mix)


N_PAIR = D_RWKV // LANES
CHUNK = 64
N_CHUNK = SEQ // CHUNK
GROUP = 8
STATE = (N_PAIR, HEAD_DIM, LANES)


def _lane_sums(lhs_tiles, ones2):
    out = _dot(jnp.concatenate(lhs_tiles, axis=0), ones2)
    return [out[i * HEAD_DIM:(i + 1) * HEAD_DIM] for i in range(len(lhs_tiles))]


def _seg_sum(xs, ones2):
    return _lane_sums([jnp.concatenate(_split(x, 2), axis=1) for x in xs], ones2)


def _seg_sum_rows(xs, ones2):
    out = _dot(jnp.concatenate(_split(jnp.concatenate(xs, axis=0), 2), axis=1), ones2)
    return [out[i * GROUP:(i + 1) * GROUP] for i in range(len(xs))]


def _col_form(rows, diag, ones2):
    zero = jnp.zeros((HEAD_DIM, LANES), BF16)
    tiles = []
    for row in rows:
        hi = row.astype(BF16)
        lo = (row - hi.astype(F32)).astype(BF16)
        tiles.append(jnp.concatenate(
            [jnp.where(diag, jnp.broadcast_to(part, (HEAD_DIM, LANES)), zero) for part in (hi, lo)], axis=1))
    return _lane_sums(tiles, ones2)


def _scan_consts():
    ones2 = jnp.concatenate([_head_ones(LANES)] * 2, axis=0)
    sub = lax.broadcasted_iota(jnp.int32, (HEAD_DIM, LANES), 0)
    lane_in_head = lax.broadcasted_iota(jnp.int32, (HEAD_DIM, LANES), 1) & (HEAD_DIM - 1)
    return ones2, lane_in_head == sub, lane_in_head


def _rows_of_columns(tile):
    t = tile.T
    return jnp.concatenate([t[:CHUNK], t[HEAD_DIM:HEAD_DIM + CHUNK]], axis=1)


def _pair(j):
    return slice(j * LANES, (j + 1) * LANES)


def _scan_fwd(r, w, k, v, kkn, b):
    def body(r_ref, w_ref, k_ref, v_ref, kkn_ref, b_ref, o_ref, st_ref, sa_ref, s_scr):
        c = pl.program_id(0)
        ones2, diag, lane_in_head = _scan_consts()

        @pl.when(c == 0)
        def _():
            s_scr[...] = jnp.zeros_like(s_scr)

        def group(gi, carry):
            row0 = pl.multiple_of(gi * GROUP, GROUP)
            states, ocols = list(carry[:N_PAIR]), list(carry[N_PAIR:])
            tiles = [[t[pl.ds(row0, GROUP), _pair(j)] for t in (r_ref, w_ref, k_ref, v_ref, kkn_ref, b_ref)]
                     for j in range(N_PAIR)]
            def row(j, name, u):
                return tiles[j]["rwkvnb".index(name)][u:u + 1]

            def emit_out(u, after):
                outs = _seg_sum([s[j] * row(j, "r", u + d) for d, s in enumerate(after) for j in range(N_PAIR)], ones2)
                for d in range(2):
                    here = lane_in_head == gi * GROUP + u + d
                    for j in range(N_PAIR):
                        ocols[j] = jnp.where(here, outs[d * N_PAIR + j], ocols[j])

            def vcols_of(u):
                cols = _col_form([row(j, "v", u + d) for d in range(2) for j in range(N_PAIR)], diag, ones2)
                return cols[:N_PAIR], cols[N_PAIR:]

            n_next = [pltpu.roll(tiles[j][4], GROUP - 1, 0) for j in range(N_PAIR)]
            dots = _seg_sum_rows([tiles[j][5] * n_next[j] for j in range(N_PAIR)]
                                 + [tiles[j][2] * n_next[j] for j in range(N_PAIR)], ones2)
            b_n, k_n = dots[:N_PAIR], dots[N_PAIR:]
            w_n = [tiles[j][1] * n_next[j] for j in range(N_PAIR)]

            vcols = vcols_of(0)
            after = None
            for u in range(0, GROUP, 2):
                prods = _seg_sum([states[j] * row(j, "n", u) for j in range(N_PAIR)]
                                 + [states[j] * w_n[j][u:u + 1] for j in range(N_PAIR)], ones2)
                if after is not None:
                    emit_out(u - 2, after)
                nxt = vcols_of(u + 2) if u + 2 < GROUP else None
                first, second = [], []
                for j in range(N_PAIR):
                    sa1 = prods[j]
                    sa2 = prods[N_PAIR + j] + sa1 * b_n[j][u:u + 1] + vcols[0][j] * k_n[j][u:u + 1]
                    s1 = states[j] * row(j, "w", u) + sa1 * row(j, "b", u) + vcols[0][j] * row(j, "k", u)
                    s2 = s1 * row(j, "w", u + 1) + sa2 * row(j, "b", u + 1) + vcols[1][j] * row(j, "k", u + 1)
                    st_ref[row0 + u, j] = s1
                    sa_ref[row0 + u, j] = sa1
                    st_ref[row0 + u + 1, j] = s2
                    sa_ref[row0 + u + 1, j] = sa2
                    first.append(s1)
                    second.append(s2)
                    states[j] = s2
                after, vcols = (first, second), nxt
            emit_out(GROUP - 2, after)
            return tuple(states + ocols)

        zero = jnp.zeros((HEAD_DIM, LANES), F32)
        fin = lax.fori_loop(0, CHUNK // GROUP, group, tuple(s_scr[j] for j in range(N_PAIR)) + (zero,) * N_PAIR)
        for j in range(N_PAIR):
            s_scr[j] = fin[j]
            o_ref[:, _pair(j)] = _rows_of_columns(fin[N_PAIR + j])

    blk = pl.BlockSpec((CHUNK, D_RWKV), lambda c: (c, 0))
    per_step = pl.BlockSpec((CHUNK,) + STATE, lambda c: (c, 0, 0, 0))
    return pl.pallas_call(
        body, name="rwkv_scan_fwd", grid=(N_CHUNK,),
        in_specs=[blk] * 6,
        out_specs=[blk, per_step, per_step],
        out_shape=[jax.ShapeDtypeStruct((SEQ, D_RWKV), F32)] + [jax.ShapeDtypeStruct((SEQ,) + STATE, F32)] * 2,
        scratch_shapes=[pltpu.VMEM(STATE, F32)],
        compiler_params=_cp(("arbitrary",)),
    )(r, w, k, v, kkn, b)


def _scan_bwd(r, w, k, v, kkn, b, do, states, sas, ds_in, prev, name, first_chunk, n_chunks):
    top = first_chunk + n_chunks - 1

    def body(r_ref, w_ref, k_ref, v_ref, kkn_ref, b_ref, do_ref, st_ref, before_ref, sa_ref, ds_in_ref, *rest):
        dr_ref, dw_ref, dk_ref, dv_ref, dkkn_ref, db_ref, ds_out_ref, ds_scr = rest[-8:]
        i = pl.program_id(0)
        ones2, diag, lane_in_head = _scan_consts()

        @pl.when(i == 0)
        def _():
            ds_scr[...] = ds_in_ref[...]

        entry = [before_ref[0, j] * jnp.where(i < top, 1.0, 0.0) for j in range(N_PAIR)]

        def reverse(gr, carry):
            gi = CHUNK // GROUP - 1 - gr
            row0 = pl.multiple_of(gi * GROUP, GROUP)
            dstates, dvcols = list(carry[:N_PAIR]), list(carry[N_PAIR:])
            tiles = [[t[pl.ds(row0, GROUP), _pair(j)]
                      for t in (r_ref, w_ref, k_ref, v_ref, kkn_ref, b_ref, do_ref)] for j in range(N_PAIR)]
            rows = [[[None] * GROUP for _ in range(5)] for _ in range(N_PAIR)]

            def row(j, name, u):
                return tiles[j]["rwkvnbd".index(name)][u:u + 1]

            def cols_of(u):
                cols = _col_form([row(j, name, u - d) for d in range(2) for name in "dv" for j in range(N_PAIR)],
                                 diag, ones2)
                return [[(cols[(2 * d) * N_PAIR + j], cols[(2 * d + 1) * N_PAIR + j]) for j in range(N_PAIR)]
                        for d in range(2)]

            def emit_dv(u, dsps):
                outs = _seg_sum([dsp[j] * row(j, "k", u - d) for d, dsp in enumerate(dsps) for j in range(N_PAIR)], ones2)
                for d in range(2):
                    here = lane_in_head == gi * GROUP + u - d
                    for j in range(N_PAIR):
                        dvcols[j] = jnp.where(here, outs[d * N_PAIR + j], dvcols[j])

            b_prev = [pltpu.roll(tiles[j][5], 1, 0) for j in range(N_PAIR)]
            dots = _seg_sum_rows([tiles[j][4] * b_prev[j] for j in range(N_PAIR)]
                                 + [tiles[j][0] * tiles[j][5] for j in range(N_PAIR)], ones2)
            n_b, r_b = dots[:N_PAIR], dots[N_PAIR:]
            w_b = [tiles[j][1] * b_prev[j] for j in range(N_PAIR)]

            def outputs(u, j, dsp, dsa, docol, vcol):
                tl = gi * GROUP + u
                if u > 0:
                    s_prev = st_ref[tl - 1, j]
                else:
                    s_prev = jnp.where(gi == 0, entry[j], st_ref[jnp.maximum(tl - 1, 0), j])
                rows[j][0][u] = jnp.sum(st_ref[tl, j] * docol, axis=0, keepdims=True)
                rows[j][1][u] = jnp.sum(dsp * s_prev, axis=0, keepdims=True)
                rows[j][2][u] = jnp.sum(dsp * vcol, axis=0, keepdims=True)
                rows[j][3][u] = jnp.sum(s_prev * dsa, axis=0, keepdims=True)
                rows[j][4][u] = jnp.sum(dsp * sa_ref[tl, j], axis=0, keepdims=True)

            cols = cols_of(GROUP - 1)
            before = None
            for u in range(GROUP - 1, 0, -2):
                dsp1 = [dstates[j] + cols[0][j][0] * row(j, "r", u) for j in range(N_PAIR)]
                prods = _seg_sum([dsp1[j] * row(j, "b", u) for j in range(N_PAIR)]
                                 + [dsp1[j] * w_b[j][u:u + 1] for j in range(N_PAIR)], ones2)
                if before is not None:
                    emit_dv(u + 2, before)
                nxt = cols_of(u - 2) if u >= 2 else None
                dsp2 = []
                for j in range(N_PAIR):
                    dsa1 = prods[j]
                    dsa2 = prods[N_PAIR + j] + dsa1 * n_b[j][u:u + 1] + cols[1][j][0] * r_b[j][u - 1:u]
                    mid = dsp1[j] * row(j, "w", u) + dsa1 * row(j, "n", u) + cols[1][j][0] * row(j, "r", u - 1)
                    outputs(u, j, dsp1[j], dsa1, *cols[0][j])
                    outputs(u - 1, j, mid, dsa2, *cols[1][j])
                    dstates[j] = mid * row(j, "w", u - 1) + dsa2 * row(j, "n", u - 1)
                    dsp2.append(mid)
                before, cols = (dsp1, dsp2), nxt
            emit_dv(1, before)
            for j in range(N_PAIR):
                for ref, rr in zip((dr_ref, dw_ref, dk_ref, dkkn_ref, db_ref), rows[j]):
                    ref[pl.ds(row0, GROUP), _pair(j)] = jnp.concatenate(rr, axis=0)
            return tuple(dstates + dvcols)

        zero = jnp.zeros((HEAD_DIM, LANES), F32)
        dfin = lax.fori_loop(0, CHUNK // GROUP, reverse, tuple(ds_scr[j] for j in range(N_PAIR)) + (zero,) * N_PAIR)
        for j in range(N_PAIR):
            ds_scr[j] = dfin[j]
            dv_ref[:, _pair(j)] = _rows_of_columns(dfin[N_PAIR + j])

        @pl.when(i == n_chunks - 1)
        def _():
            ds_out_ref[...] = ds_scr[...]

    blk = pl.BlockSpec((CHUNK, D_RWKV), lambda i: (top - i, 0))
    per_step = pl.BlockSpec((CHUNK,) + STATE, lambda i: (top - i, 0, 0, 0))
    step_before = pl.BlockSpec((1,) + STATE, lambda i: (jnp.maximum((top - i) * CHUNK - 1, 0), 0, 0, 0))
    prev = [] if prev is None else list(prev)
    outs = pl.pallas_call(
        body, name=name, grid=(n_chunks,),
        in_specs=[blk] * 7 + [per_step, step_before, per_step, _const(STATE)] + [ANY] * len(prev),
        out_specs=[blk] * 6 + [_const(STATE)],
        out_shape=[jax.ShapeDtypeStruct((SEQ, D_RWKV), F32)] * 6 + [jax.ShapeDtypeStruct(STATE, F32)],
        scratch_shapes=[pltpu.VMEM(STATE, F32)],
        input_output_aliases={11 + t: t for t in range(len(prev))},
        compiler_params=_cp(("arbitrary",)),
    )(r, w, k, v, kkn, b, do, states, states, sas, ds_in, *prev)
    return outs[:6], outs[6]


def _stacked(rows, cols, pick):
    return pl.BlockSpec((None, rows, cols), pick)


def _local_step(x, target, sm, win_st):
    def tied(t, token):
        return t if token is None else t + token[0:1, 0:1].reshape((1,) * t.ndim)

    zpad = jnp.zeros((LORA_DECAY, D_RWKV), F32)
    prm = [sm["w0"], jnp.concatenate([sm["w_decay_up"], zpad], axis=0), sm["a0"],
           jnp.concatenate([zpad, sm["w_iclr_up"]], axis=0), sm["w_gate_up"], sm["k_k"], sm["k_a"]]
    mix = sm["rwkv_shift_mix"]
    onehot = jnp.asarray(_t5_onehot(), BF16)
    sinks = sm["sinks"].reshape(N_Q_HEADS)
    lng, lnb, rk = sm["ln_x_g"], sm["ln_x_b"], sm["r_k"].reshape(1, D_RWKV)

    h1 = _norm_cast(x, sm["norm_mix_pre"], "norm_in")
    proj = _matmul(h1, win_st, "nn", "proj", m=SEQ, n=D_IN, k=D_MODEL, tm=SEQ, tn=640,
                   b_spec=_stacked(D_MODEL, 640, lambda i, j: (j, 0, 0)))
    bias = _bias_table(sm["rel_bias"].T, onehot).reshape(N_KV_HEADS, Q_PER_KV * BLOCK, 2 * BLOCK)
    attn = _attn_fwd(proj, bias, sinks)
    r, w, k2, v, kkn, b, g = _rwkv_prep(proj, mix, prm)
    o, states, sas = _scan_fwd(r, w, k2, v, kkn, b)
    wout, wup_st, wdown = yield ("rest_weights", o)
    cat = _rwkv_post(o, r, k2, v, g, lng, lnb, rk, attn)
    mixo = _matmul(cat, wout, "nn", "out_proj", m=SEQ, n=D_MODEL, k=D_MODEL, tm=SEQ, tn=512)
    x2, h3 = _mix_norm(x, mixo, sm["norm_mix_post"], sm["norm_ffn_pre"])
    u_gate, u_val, act = _ffn_up_act(h3, wup_st, sm["conv_w"], sm["conv_b"])
    f = _matmul(act, wdown, "nn", "ffn_down", m=SEQ, n=D_MODEL, k=D_FF, tm=1024, tn=512)
    loss, dy, df, d_g4 = _loss_head(x2, f, sm["norm_ffn_post"], target)

    d_wdown = _matmul(act, df, "tn", "d_wdown", m=D_FF, n=D_MODEL, k=SEQ, tm=512, tn=D_MODEL)
    du, d_convw, d_convb = _ffn_act_bwd(u_gate, u_val, df, wdown, sm["conv_w"], sm["conv_b"])
    d_convw = d_convw.transpose(1, 0, 2).reshape(3, 2 * D_FF)
    d_convb = d_convb.reshape(1, 2 * D_FF)
    dh3 = _matmul_nt_shards(du, wup_st, "d_h3", m=SEQ, n=D_MODEL, tm=512, tn=512,
                            a_spec=pl.BlockSpec((2, 512, D_FF), lambda i, j: (0, i, 0)),
                            a_piece=lambda ref, s: ref[s // 2, :, (s % 2) * 2048:(s % 2 + 1) * 2048])
    d_wup = _matmul(h3, du, "tn", "d_wup", m=D_MODEL, n=2 * D_FF, k=SEQ, tm=D_MODEL, tn=512,
                    b_spec=pl.BlockSpec((None, SEQ, 512), lambda i, j: (j // 8, 0, j % 8)),
                    out=((N_CHIPS, D_MODEL, 2048), _stacked(D_MODEL, 512, lambda i, j: (j // 4, 0, j % 4))))
    dx2, dmix, d_g2, d_g3 = _mid_bwd(x2, mixo, dy, dh3, sm["norm_mix_post"], sm["norm_ffn_pre"])
    dcat = _matmul(dmix, wout, "nt", "d_cat", m=SEQ, n=D_MODEL, k=D_MODEL, tm=SEQ, tn=512)
    d_wout = _matmul(cat, dmix, "tn", "d_wout", m=D_MODEL, n=D_MODEL, k=SEQ, tm=512, tn=D_MODEL)
    token = yield ("grads_a", (d_wdown, d_wup, d_wout))
    do, dr_p, dk_p, dv_p, dg, d_lng, d_lnb, d_rk = _rwkv_post_bwd(o, r, k2, v, g, lng, tied(lnb, token), rk, dcat)
    half = N_CHUNK // 2
    ds_end = jnp.zeros(STATE, F32)
    late, ds_mid = _scan_bwd(r, w, k2, v, kkn, b, do, states, sas, ds_end, None, "rwkv_scan_bwd_late", half, half)
    token = yield ("seam_1", ds_mid)
    scan_cts, ds_first = _scan_bwd(r, w, k2, v, kkn, b, do, states, sas, tied(ds_mid, token), late,
                                   "rwkv_scan_bwd_early", 0, half)
    dr_s, dw_s, dk_s, dv_s, dkkn_s, db_s = scan_cts
    token = yield ("seam_2", ds_first)
    prep_grads = _rwkv_prep_bwd(proj, tied(mix, token), prm,
                                (dr_s, dr_p, dw_s, dk_s, dk_p, dv_s, dv_p, dkkn_s, db_s, dg))
    dps, d_mix, d_w0, d_wdu, d_a0, d_wiu, d_wgu, d_kk, d_ka = prep_grads
    dq, dkv, dbias, dsink = _attn_bwd(proj, bias, sinks, dcat)
    d_relb = _bias_table_bwd(dbias.reshape(N_Q_HEADS, N_REL), onehot).T
    dproj = _assemble_dproj(dq, dkv, dps, mix)
    d_win = _matmul(h1, dproj, "tn", "d_win", m=D_MODEL, n=D_IN, k=SEQ, tm=D_MODEL, tn=640,
                    out=((N_CHIPS, D_MODEL, 640), _stacked(D_MODEL, 640, lambda i, j: (j, 0, 0))))
    token = yield ("grads_b", d_win)
    dh1 = _matmul_nt_shards(dproj, win_st, "d_h1", m=SEQ, n=D_MODEL, tm=1024, tn=D_MODEL,
                            a_spec=pl.BlockSpec((1024, D_IN), lambda i, j: (i, 0)),
                            a_piece=lambda ref, s: ref[:, s * 640:(s + 1) * 640])
    grad_x, d_g1 = _first_bwd(x, dx2, dh1, tied(sm["norm_mix_pre"], token))

    grads = {
        "norm_mix_pre": d_g1, "norm_mix_post": d_g2, "norm_ffn_pre": d_g3, "norm_ffn_post": d_g4,
        "w_in": d_win, "rel_bias": d_relb, "sinks": dsink[:, 0].reshape(1, N_Q_HEADS),
        "rwkv_shift_mix": d_mix, "w0": d_w0, "w_decay_up": d_wdu[:LORA_DECAY], "a0": d_a0,
        "w_iclr_up": d_wiu[LORA_DECAY:], "w_gate_up": d_wgu, "k_k": d_kk, "k_a": d_ka,
        "r_k": d_rk.reshape(1, N_Q_HEADS, HEAD_DIM), "ln_x_g": d_lng, "ln_x_b": d_lnb,
        "w_out": d_wout, "w_ffn_up": d_wup, "conv_w": d_convw, "conv_b": d_convb, "w_ffn_down": d_wdown,
    }
    return loss, grad_x, grads


def _place():
    x, y, c = lax.axis_index("x"), lax.axis_index("y"), lax.axis_index("c")
    chips = [(1 - x, y), (x, 1 - y), (1 - x, 1 - y)]
    return x, y, c, chips


def _remote(src, dst, sems, idx, to):
    return pltpu.make_async_remote_copy(src_ref=src, dst_ref=dst, send_sem=sems[0].at[idx], recv_sem=sems[1].at[idx],
                                        device_id=to, device_id_type=MESH)


def _half(c, rows):
    return pl.ds(pl.multiple_of(c * (rows // 2), 16), rows // 2)


def _gather_weights(big, small):
    nb, ns = len(big), len(small)

    def body(*refs):
        ins, outs = refs[:nb + ns], refs[nb + ns:2 * (nb + ns)]
        ici, d2d, sml, loc = refs[2 * (nb + ns):2 * (nb + ns) + 2], refs[-5:-3], refs[-3:-1], refs[-1]
        x, y, c, chips = _place()
        me = 2 * x + y
        sib = (x, y, 1 - c)
        local = [pltpu.make_async_copy(ins[a], outs[a].at[me], loc.at[a]) for a in range(nb + ns)]
        for cp in local:
            cp.start()
        sends = []
        for a in range(nb):
            rows = _half(c, big[a].shape[0])
            for kk, chip in enumerate(chips):
                sends.append(_remote(ins[a].at[rows], outs[a].at[me, rows], ici, a * 3 + kk, (*chip, c)))
        for a in range(ns):
            for kk, chip in enumerate(chips):
                sends.append(_remote(ins[nb + a], outs[nb + a].at[me], sml, a * 3 + kk, (*chip, c)))
        for cp in sends:
            cp.start()
        passed = []
        for a in range(nb):
            rows = _half(c, big[a].shape[0])
            for kk, (px, py) in enumerate(chips):
                got = outs[a].at[2 * px + py, rows]
                _remote(got, got, ici, a * 3 + kk, sib).wait_recv()
                fwd = _remote(got, got, d2d, a * 3 + kk, sib)
                fwd.start()
                passed.append(fwd)
        for a in range(nb):
            other = _half(1 - c, big[a].shape[0])
            for kk, (px, py) in enumerate(chips):
                land = outs[a].at[2 * px + py, other]
                _remote(land, land, d2d, a * 3 + kk, sib).wait_recv()
        for a in range(ns):
            for kk, (px, py) in enumerate(chips):
                land = outs[nb + a].at[2 * px + py]
                _remote(land, land, sml, a * 3 + kk, sib).wait_recv()
        for cp in sends + passed:
            cp.wait_send()
        for cp in local:
            cp.wait()

    arrs = list(big) + list(small)
    return pl.pallas_call(
        body, name="gather_weights",
        in_specs=[ANY] * len(arrs), out_specs=[ANY] * len(arrs),
        out_shape=[jax.ShapeDtypeStruct((N_CHIPS,) + t.shape, t.dtype) for t in arrs],
        scratch_shapes=[pltpu.SemaphoreType.DMA((3 * nb,)), pltpu.SemaphoreType.DMA((3 * nb,)),
                        pltpu.SemaphoreType.DMA((3 * nb,)), pltpu.SemaphoreType.DMA((3 * nb,)),
                        pltpu.SemaphoreType.DMA((3 * ns,)), pltpu.SemaphoreType.DMA((3 * ns,)),
                        pltpu.SemaphoreType.DMA((nb + ns,))],
        compiler_params=pltpu.CompilerParams(has_side_effects=True),
    )(*arrs)


HBM = pl.BlockSpec(memory_space=pltpu.HBM)
SEM = pl.BlockSpec(memory_space=pltpu.SEMAPHORE)
EFFECT = pltpu.SideEffectType.DATAFLOW_SIDE_EFFECTING


def _copies_start(name, bufs, plan, n):
    nb = len(bufs)

    def body(*refs):
        ins, sems, token = refs[:nb], refs[nb:nb + 2 * n], refs[-1]
        for kk, (src, dst, dev) in enumerate(plan(ins)):
            pltpu.make_async_remote_copy(src_ref=src, dst_ref=dst, send_sem=sems[2 * kk], recv_sem=sems[2 * kk + 1],
                                         device_id=dev, device_id_type=MESH).start()
        token[...] = jnp.zeros_like(token)

    outs = pl.pallas_call(
        body, name=name,
        out_shape=tuple([pltpu.SemaphoreType.DMA(())] * (2 * n) + [pltpu.HBM(t.shape, t.dtype) for t in bufs]
                        + [jax.ShapeDtypeStruct((8, LANES), F32)]),
        in_specs=[HBM] * nb,
        out_specs=tuple([SEM] * (2 * n) + [HBM] * nb + [pl.BlockSpec(memory_space=pltpu.VMEM)]),
        input_output_aliases={t: 2 * n + t for t in range(nb)},
        compiler_params=pltpu.CompilerParams(has_side_effects=EFFECT),
    )(*[pltpu.with_memory_space_constraint(t, pltpu.HBM) for t in bufs])
    return outs[:2 * n], outs[2 * n:2 * n + nb], outs[-1]


def _copies_wait(name, sems, bufs, plan, n, after):
    nb = len(bufs)
    after = list(after) if isinstance(after, (list, tuple)) else [after]

    def body(*refs):
        ins, sem_refs = refs[:nb], refs[nb:nb + 2 * n]
        for kk, (src, dst, dev) in enumerate(plan(ins)):
            cp = pltpu.make_async_remote_copy(src_ref=src, dst_ref=dst, send_sem=sem_refs[2 * kk],
                                              recv_sem=sem_refs[2 * kk + 1], device_id=dev, device_id_type=MESH)
            cp.wait_send()
            cp.wait_recv()

    return pl.pallas_call(
        body, name=name,
        out_shape=tuple(pltpu.HBM(t.shape, t.dtype) for t in bufs),
        in_specs=[HBM] * nb + [SEM] * (2 * n) + [ANY] * len(after),
        out_specs=tuple([HBM] * nb),
        input_output_aliases={t: t for t in range(nb)},
        compiler_params=pltpu.CompilerParams(has_side_effects=EFFECT),
    )(*bufs, *sems, *after)


def _plan_gather(n_w):
    def plan(refs):
        x, y, c, chips = _place()
        me = 2 * x + y
        return [(refs[a], refs[n_w + a].at[me], (*chip, c)) for a in range(n_w) for chip in chips]
    return plan


def _plan_pair_halves(n_g, rows):
    def plan(refs):
        x, y, c, _ = _place()
        return [(refs[a].at[:, _half(1 - c, rows[a])], refs[n_g + a], (x, y, 1 - c)) for a in range(n_g)]
    return plan


def _plan_chip_parts(n_g):
    def plan(refs):
        x, y, c, chips = _place()
        me = 2 * x + y
        return [(refs[a].at[2 * px + py], refs[n_g + a].at[me], (px, py, c))
                for a in range(n_g) for (px, py) in chips]
    return plan


def _plan_pair_fill(n_g, rows):
    def plan(refs):
        x, y, c, _ = _place()
        return [(refs[a].at[_half(c, rows[a])], refs[a].at[_half(c, rows[a])], (x, y, 1 - c)) for a in range(n_g)]
    return plan


def _pair_add(g, got, name):
    _, rows, cols = g.shape
    hr = rows // 2
    tr = min(hr, 256)
    nb = hr // tr

    def body(g_ref, got_ref, p_ref, own_ref):
        val = (g_ref[...] + got_ref[...]).astype(BF16)
        p_ref[...] = val

        @pl.when(pl.program_id(1) == 2 * lax.axis_index("x") + lax.axis_index("y"))
        def _():
            own_ref[...] = val

    def mine(i, s):
        return (2 * lax.axis_index("x") + lax.axis_index("y"), i, 0)

    return pl.pallas_call(
        body, name=name, grid=(nb, N_CHIPS),
        in_specs=[pl.BlockSpec((None, tr, cols), lambda i, s: (s, lax.axis_index("c") * nb + i, 0)),
                  pl.BlockSpec((None, tr, cols), lambda i, s: (s, i, 0))],
        out_specs=[pl.BlockSpec((None, tr, cols), lambda i, s: (s, i, 0)), pl.BlockSpec((None, tr, cols), mine)],
        out_shape=[jax.ShapeDtypeStruct((N_CHIPS, hr, cols), BF16)] * 2,
        compiler_params=_cp(("parallel", "arbitrary")),
    )(g, got)


def _chip_sum(parts, name):
    _, hr, cols = parts.shape
    tr = min(hr, 128)
    nb = hr // tr

    def body(t_ref, o_ref):
        part = [t_ref[s].astype(F32) for s in range(N_CHIPS)]
        o_ref[...] = ((part[0] + part[1]) + part[2]) + part[3]

    return pl.pallas_call(
        body, name=name, grid=(nb,),
        in_specs=[pl.BlockSpec((N_CHIPS, tr, cols), lambda i: (0, i, 0))],
        out_specs=pl.BlockSpec((tr, cols), lambda i: (lax.axis_index("c") * nb + i, 0)),
        out_shape=jax.ShapeDtypeStruct((2 * hr, cols), F32),
        compiler_params=_cp(("parallel",)),
    )(parts)


class _Reduction:
    def __init__(self, tag, rows):
        self.tag, self.n, self.rows = tag, len(rows), rows
        self.plans = (_plan_pair_halves(self.n, rows), _plan_chip_parts(self.n), _plan_pair_fill(self.n, rows))
        self.flight = None

    def _name(self, what):
        return f"grad_{self.tag}_{what}"

    def start(self, gs):
        gots = [lax.empty((N_CHIPS, t.shape[1] // 2, t.shape[2]), F32) for t in gs]
        self.flight = _copies_start(self._name("pair_start"), list(gs) + gots, self.plans[0], self.n)
        return self.flight[2]

    def after_pair(self, after):
        sems, bufs, _ = self.flight
        out = _copies_wait(self._name("pair_wait"), sems, bufs, self.plans[0], self.n, after)
        sums = [_pair_add(g, got, self._name(f"pair_add_{i}"))
                for i, (g, got) in enumerate(zip(out[:self.n], out[self.n:]))]
        self.flight = _copies_start(self._name("chip_start"), [p for p, _ in sums] + [own for _, own in sums],
                                    self.plans[1], 3 * self.n)
        return self.flight[2]

    def after_chips(self, after):
        sems, bufs, _ = self.flight
        out = _copies_wait(self._name("chip_wait"), sems, bufs, self.plans[1], 3 * self.n, after)
        fulls = [_chip_sum(t, self._name(f"chip_sum_{i}")) for i, t in enumerate(out[self.n:])]
        self.flight = _copies_start(self._name("fill_start"), fulls, self.plans[2], self.n)
        return self.flight[2]

    def finish(self, after):
        sems, bufs, _ = self.flight
        return _copies_wait(self._name("fill_wait"), sems, bufs, self.plans[2], self.n, after)


def _adamw_math(w, g, m, v):
    nm = ADAM_B1 * m + (1.0 - ADAM_B1) * g
    nv = ADAM_B2 * v + (1.0 - ADAM_B2) * (g * g)
    m_hat = nm / (1.0 - ADAM_B1 ** ADAM_STEP)
    v_hat = nv / (1.0 - ADAM_B2 ** ADAM_STEP)
    return -ADAM_LR * (m_hat / (jnp.sqrt(v_hat) + ADAM_EPS) + ADAM_WD * w), nm, nv


def _adamw(w, g, m, v, name, tr):
    r, cdim = w.shape

    def body(w_ref, g_ref, m_ref, v_ref, d_ref, nm_ref, nv_ref):
        d_ref[...], nm_ref[...], nv_ref[...] = _adamw_math(w_ref[...], g_ref[...], m_ref[...], v_ref[...])

    return pl.pallas_call(
        body, name=name, grid=(r // tr,), in_specs=[_rows(tr, cdim)] * 4, out_specs=[_rows(tr, cdim)] * 3,
        out_shape=[jax.ShapeDtypeStruct((r, cdim), F32)] * 3, compiler_params=_cp(("parallel",)),
    )(w, g, m, v)


def _adamw_small(w, parts, m, v):
    def body(w_ref, p_ref, m_ref, v_ref, d_ref, nm_ref, nv_ref, g_ref):
        g = p_ref[0]
        for dev in range(1, N_DEV):
            g = g + p_ref[dev]
        g_ref[...] = g
        d_ref[...], nm_ref[...], nv_ref[...] = _adamw_math(w_ref[...], g, m_ref[...], v_ref[...])

    return pl.pallas_call(
        body, name="adamw_small", grid=(1,),
        in_specs=[_const(w.shape), _const(parts.shape), _const(w.shape), _const(w.shape)],
        out_specs=[_const(w.shape)] * 4, out_shape=[jax.ShapeDtypeStruct(w.shape, F32)] * 4,
        compiler_params=_cp(("arbitrary",)),
    )(w, parts, m, v)


REPLICATED = (("norm_mix_pre", 1024), ("norm_mix_post", 1024), ("norm_ffn_pre", 1024), ("norm_ffn_post", 1024),
              ("rel_bias", 256), ("sinks", 8), ("rwkv_shift_mix", 1792), ("w0", 512), ("a0", 512), ("k_k", 512),
              ("k_a", 512), ("r_k", 512), ("ln_x_g", 512), ("ln_x_b", 512), ("conv_b", 8192))
SMALL_SHARDED = (("w_decay_up", LORA_DECAY, D_RWKV), ("w_iclr_up", LORA_ICLR, D_RWKV),
                 ("w_gate_up", LORA_GATE, D_RWKV), ("conv_w", 3, 2 * D_FF))
BIG = (("w_in", D_MODEL, 640), ("w_out", 256, D_MODEL), ("w_ffn_up", D_MODEL, 2048), ("w_ffn_down", 1024, D_MODEL))
PACK_ALIGN = 8 * LANES


def _pack(pieces):
    flat = []
    for t in pieces:
        t = t.reshape(-1)
        pad = (-t.shape[0]) % LANES
        flat.append(jnp.pad(t, (0, pad)) if pad else t)
    flat = jnp.concatenate(flat)
    pad = (-flat.shape[0]) % PACK_ALIGN
    return jnp.pad(flat, (0, pad)).reshape(-1, LANES)


def _unpack(buf, sizes):
    flat, out, off = buf.reshape(-1), [], 0
    for n in sizes:
        out.append(flat[off:off + n])
        off += n + ((-n) % LANES)
    return out


def kernel(x, norm_mix_pre, norm_mix_post, norm_ffn_pre, norm_ffn_post, w_in, rel_bias, sinks, rwkv_shift_mix, w0, w_decay_up, a0, w_iclr_up, w_gate_up, k_k, k_a, r_k, ln_x_g, ln_x_b, w_out, w_ffn_up, conv_w, conv_b, w_ffn_down, loss_target, m_norm_mix_pre, m_norm_mix_post, m_norm_ffn_pre, m_norm_ffn_post, m_w_in, m_rel_bias, m_sinks, m_rwkv_shift_mix, m_w0, m_w_decay_up, m_a0, m_w_iclr_up, m_w_gate_up, m_k_k, m_k_a, m_r_k, m_ln_x_g, m_ln_x_b, m_w_out, m_w_ffn_up, m_conv_w, m_conv_b, m_w_ffn_down, v_norm_mix_pre, v_norm_mix_post, v_norm_ffn_pre, v_norm_ffn_post, v_w_in, v_rel_bias, v_sinks, v_rwkv_shift_mix, v_w0, v_w_decay_up, v_a0, v_w_iclr_up, v_w_gate_up, v_k_k, v_k_a, v_r_k, v_ln_x_g, v_ln_x_b, v_w_out, v_w_ffn_up, v_conv_w, v_conv_b, v_w_ffn_down):
    given = dict(locals())
    names = [n for n, _ in REPLICATED] + [n for n, _, _ in SMALL_SHARDED] + [n for n, _, _ in BIG]
    order = ["norm_mix_pre", "norm_mix_post", "norm_ffn_pre", "norm_ffn_post", "w_in", "rel_bias", "sinks",
             "rwkv_shift_mix", "w0", "w_decay_up", "a0", "w_iclr_up", "w_gate_up", "k_k", "k_a", "r_k", "ln_x_g",
             "ln_x_b", "w_out", "w_ffn_up", "conv_w", "conv_b", "w_ffn_down"]
    assert sorted(names) == sorted(order)
    shard = 2 * lax.axis_index("x") + lax.axis_index("y")

    big_sh = {n: given[n].reshape(a, b).astype(BF16) for n, a, b in BIG}
    small_sh = [given[n].reshape(r, c // N_CHIPS) for n, r, c in SMALL_SHARDED]
    gathered = _gather_weights([big_sh["w_in"]], small_sh)
    rest = ("w_out", "w_ffn_up", "w_ffn_down")
    win_st, rest_sh = lax.optimization_barrier((gathered[0], [big_sh[n] for n in rest]))
    sm = {n: given[n] for n, _ in REPLICATED}
    sm["r_k"] = r_k.reshape(N_Q_HEADS, HEAD_DIM)
    for (n, r, c), st in zip(SMALL_SHARDED, gathered[1:]):
        sm[n] = st.transpose(1, 0, 2).reshape(r, c)

    lands = [lax.dynamic_update_slice(lax.empty((N_CHIPS,) + t.shape, BF16), t[None], (shard, 0, 0)) for t in rest_sh]
    plan_w = _plan_gather(len(rest))
    w_sems, w_bufs, token = _copies_start("gather_rest_start", rest_sh + lands, plan_w, 9)
    sm["norm_mix_pre"] = norm_mix_pre + token[0:1, 0:1]

    def on_rest_weights(after):
        out = _copies_wait("gather_rest_wait", w_sems, w_bufs, plan_w, 9, after)
        wout_st, wup_st, wdown_st = out[3:]
        return wout_st.reshape(D_MODEL, D_MODEL), wup_st, wdown_st.reshape(D_FF, D_MODEL)

    red_a = _Reduction("a", (1024, D_MODEL, 256))
    red_b = _Reduction("b", (D_MODEL,))

    def on_grads_a(gs):
        d_wdown, d_wup, d_wout = gs
        return red_a.start([d_wdown.reshape(N_CHIPS, 1024, D_MODEL), d_wup, d_wout.reshape(N_CHIPS, 256, D_MODEL)])

    handlers = {"rest_weights": on_rest_weights, "grads_a": on_grads_a, "seam_1": red_a.after_pair,
                "seam_2": red_a.after_chips, "grads_b": lambda g: red_b.start([g])}
    steps = _local_step(x[0], loss_target[0], sm, win_st)
    kind, payload = next(steps)
    while True:
        try:
            kind, payload = steps.send(handlers[kind](payload))
        except StopIteration as done:
            loss, grad_x, grads = done.value
            break

    small_names = [n for n, _ in REPLICATED] + [n for n, _, _ in SMALL_SHARDED]

    def shard_cols(t, s):
        return t[:, s * (t.shape[1] // N_CHIPS):(s + 1) * (t.shape[1] // N_CHIPS)]

    for_chip = jnp.stack([_pack([loss[0]] + [grads[n] for n, _ in REPLICATED]
                                + [shard_cols(grads[n], s) for n, _, _ in SMALL_SHARDED]) for s in range(N_CHIPS)])
    me = 2 * shard + lax.axis_index("c")
    mine = lax.dynamic_index_in_dim(for_chip, shard, 0, keepdims=True)
    land = lax.dynamic_update_slice(lax.empty((N_DEV,) + for_chip.shape[1:], F32), mine, (me, 0, 0))

    def plan_small(refs):
        x, y, c, _ = _place()
        out = []
        for rel in range(1, N_DEV):
            px, py, pc = x ^ (rel >> 2), y ^ ((rel >> 1) & 1), c ^ (rel & 1)
            out.append((refs[0].at[2 * px + py], refs[1].at[4 * x + 2 * y + c], (px, py, pc)))
        return out

    s_sems, s_bufs, _ = _copies_start("grad_small_start", [for_chip, land], plan_small, N_DEV - 1)

    red_b.after_pair(grad_x)
    g_out = {}
    g_out["w_ffn_down"], g_out["w_ffn_up"], g_out["w_out"] = red_a.finish(grad_x)

    delta, new_m, new_v = {}, {}, {}

    def update(n, a, b):
        delta[n], new_m[n], new_v[n] = _adamw(given[n].reshape(a, b), g_out[n], given["m_" + n].reshape(a, b),
                                              given["v_" + n].reshape(a, b), "adamw_" + n, 128)

    for n, a, b in BIG[1:]:
        update(n, a, b)
    done = [delta[n] for n, _, _ in BIG[1:]]
    red_b.after_chips(done)
    parts = _copies_wait("grad_small_wait", s_sems, s_bufs, plan_small, N_DEV - 1, done)[1]
    no_param = jnp.zeros((LANES,), F32)
    packs = [_pack([no_param] + [given[pre + n] for n in small_names]) for pre in ("", "m_", "v_")]
    small_sizes = [LANES] + [int(np.prod(given[n].shape)) for n in small_names]
    upd = [_unpack(t, small_sizes) for t in _adamw_small(packs[0], parts, packs[1], packs[2])]
    loss = upd[3][0][0]
    for n, d, nm, nv, g in zip(small_names, *[u[1:] for u in upd]):
        shape = given[n].shape
        delta[n], new_m[n], new_v[n], g_out[n] = (t.reshape(shape) for t in (d, nm, nv, g))
    g_out["w_in"], = red_b.finish(upd[0][0])
    update(*BIG[0])

    def shaped(d):
        return [d[n].reshape(given[n].shape) for n in order]

    return (loss, grad_x.reshape(x.shape), *shaped(g_out), *shaped(delta), *shaped(new_m), *shaped(new_v))
```

```python
import math

import numpy as np
import jax
import jax.numpy as jnp
from jax import lax
from jax.experimental import pallas as pl
from jax.experimental.pallas import tpu as pltpu

F32 = jnp.float32
BF16 = jnp.bfloat16
MESH = pl.DeviceIdType.MESH

SEQ = 2048
D_MODEL = 1024
HEAD_DIM = 64
D_ATTN = 512
D_RWKV = 512
D_KV = 128
N_Q_HEADS = 8
N_KV_HEADS = 2
Q_PER_KV = 4
BLOCK = 128
N_BUCKETS = 32
MAX_DISTANCE = 128
LORA_DECAY = 64
LORA_ICLR = 64
LORA_GATE = 128
RWKV_COLS = 3 * D_RWKV + LORA_DECAY + LORA_ICLR + LORA_GATE
P_OFF = D_ATTN + 2 * D_KV
D_IN = P_OFF + RWKV_COLS
D_FF = 4096
NORM_EPS = 1e-6
GN_EPS = 64e-5
NEG_INF = -1e30
N_CHIPS = 4
N_DEV = 8

ADAM_LR = 0.001
ADAM_B1 = 0.9
ADAM_B2 = 0.999
ADAM_EPS = 1e-08
ADAM_WD = 0.01
ADAM_STEP = 10

VMEM_LIMIT = 52 * 1024 * 1024
LANES = 128


def _cp(sem=None, vmem=VMEM_LIMIT):
    kw = dict(vmem_limit_bytes=vmem)
    if sem is not None:
        kw["dimension_semantics"] = sem
    return pltpu.CompilerParams(**kw)


def _rows(tr, nc):
    return pl.BlockSpec((tr, nc), lambda i: (i, 0))


def _const(shape):
    return pl.BlockSpec(shape, lambda *_: (0,) * len(shape))


ANY = pl.BlockSpec(memory_space=pl.ANY)


def _split(x, n):
    parts = []
    for _ in range(n - 1):
        h = x.astype(BF16)
        parts.append(h)
        x = x - h.astype(F32)
    parts.append(x.astype(BF16))
    return parts


def _dot(a, b, dn=(((1,), (0,)), ((), ()))):
    return lax.dot_general(a, b, dn, preferred_element_type=F32)


NN = (((1,), (0,)), ((), ()))
NT = (((1,), (1,)), ((), ()))
TN = (((0,), (0,)), ((), ()))


def _dot_ind(x, ind_bf16, n=3):
    acc = None
    for part in _split(x, n):
        t = _dot(part, ind_bf16)
        acc = t if acc is None else acc + t
    return acc


def _head_ones(n):
    r = lax.broadcasted_iota(jnp.int32, (n, n), 0) >> 6
    c = lax.broadcasted_iota(jnp.int32, (n, n), 1) >> 6
    return jnp.where(r == c, 1.0, 0.0).astype(BF16)


def _matmul(a, b, mode, name, *, m, n, k, tm, tn, a_spec=None, b_spec=None, out=None, out_dtype=F32):
    keep_at = mode == "tn" and m == tm and n > tn

    def body(a_ref, b_ref, o_ref, *scratch):
        if keep_at:
            at_ref, = scratch

            @pl.when(pl.program_id(1) == 0)
            def _():
                at_ref[...] = a_ref[...].T

            o_ref[...] = _dot(at_ref[...], b_ref[...], NN).astype(out_dtype)
        else:
            o_ref[...] = _dot(a_ref[...], b_ref[...], {"nn": NN, "nt": NT, "tn": TN}[mode]).astype(out_dtype)

    if a_spec is None:
        a_spec = pl.BlockSpec((k, tm), lambda i, j: (0, i)) if mode == "tn" else pl.BlockSpec((tm, k), lambda i, j: (i, 0))
    if b_spec is None:
        b_spec = pl.BlockSpec((tn, k), lambda i, j: (j, 0)) if mode == "nt" else pl.BlockSpec((k, tn), lambda i, j: (0, j))
    return pl.pallas_call(
        body, name=name, grid=(m // tm, n // tn),
        in_specs=[a_spec, b_spec],
        out_specs=pl.BlockSpec((tm, tn), lambda i, j: (i, j)) if out is None else out[1],
        out_shape=jax.ShapeDtypeStruct((m, n) if out is None else out[0], out_dtype),
        scratch_shapes=[pltpu.VMEM((tm, k), a.dtype)] if keep_at else [],
        compiler_params=_cp(("parallel", "arbitrary" if keep_at else "parallel")),
    )(a, b)


def _matmul_nt_shards(a, b_st, name, *, m, n, tm, tn, a_spec, a_piece):
    ks = b_st.shape[2]

    def body(a_ref, b_ref, o_ref):
        acc = _dot(a_piece(a_ref, 0), b_ref[0], NT)
        for s in range(1, N_CHIPS):
            acc = acc + _dot(a_piece(a_ref, s), b_ref[s], NT)
        o_ref[...] = acc

    return pl.pallas_call(
        body, name=name, grid=(m // tm, n // tn),
        in_specs=[a_spec, pl.BlockSpec((N_CHIPS, tn, ks), lambda i, j: (0, j, 0))],
        out_specs=pl.BlockSpec((tm, tn), lambda i, j: (i, j)),
        out_shape=jax.ShapeDtypeStruct((m, n), F32),
        compiler_params=_cp(("parallel", "parallel")),
    )(a, b_st)


def _rstd(x):
    return lax.rsqrt(jnp.mean(x * x, axis=-1, keepdims=True) + NORM_EPS)


def _rms_bwd(x, r, g, dy):
    gy = dy * g
    return r * gy - x * ((r * r * r) * (jnp.sum(x * gy, axis=-1, keepdims=True) / x.shape[-1]))


TR = 256


def _norm_cast(x, g, name):
    def body(x_ref, g_ref, h_ref):
        x = x_ref[...]
        h_ref[...] = (x * _rstd(x) * g_ref[...]).astype(BF16)

    return pl.pallas_call(
        body, name=name, grid=(SEQ // TR,),
        in_specs=[_rows(TR, D_MODEL), _const((1, D_MODEL))],
        out_specs=_rows(TR, D_MODEL),
        out_shape=jax.ShapeDtypeStruct((SEQ, D_MODEL), BF16),
        compiler_params=_cp(("parallel",)),
    )(x, g)


def _mix_norm(x, mix, g2, g3):
    def body(x_ref, mix_ref, g2_ref, g3_ref, x2_ref, h3_ref):
        mixv = mix_ref[...]
        x2 = x_ref[...] + mixv * _rstd(mixv) * g2_ref[...]
        x2_ref[...] = x2
        h3_ref[...] = (x2 * _rstd(x2) * g3_ref[...]).astype(BF16)

    return pl.pallas_call(
        body, name="mix_norm", grid=(SEQ // TR,),
        in_specs=[_rows(TR, D_MODEL), _rows(TR, D_MODEL), _const((1, D_MODEL)), _const((1, D_MODEL))],
        out_specs=[_rows(TR, D_MODEL), _rows(TR, D_MODEL)],
        out_shape=[jax.ShapeDtypeStruct((SEQ, D_MODEL), F32), jax.ShapeDtypeStruct((SEQ, D_MODEL), BF16)],
        compiler_params=_cp(("parallel",)),
    )(x, mix, g2, g3)


def _loss_head(x2, f, g4, target):
    def body(x2_ref, f_ref, g4_ref, t_ref, loss_ref, dy_ref, df_ref, dg_ref):
        i = pl.program_id(0)
        f = f_ref[...]
        g4 = g4_ref[...]
        r = _rstd(f)
        e = x2_ref[...] + f * r * g4 - t_ref[...]
        dy = e * (1.0 / D_MODEL)
        dy_ref[...] = dy
        df_ref[...] = _rms_bwd(f, r, g4, dy).astype(BF16)
        part = 0.5 * jnp.sum(jnp.sum(e * e, axis=-1, keepdims=True), axis=0, keepdims=True) * (1.0 / D_MODEL)
        dg = jnp.sum(dy * f * r, axis=0, keepdims=True)

        @pl.when(i == 0)
        def _():
            loss_ref[...] = jnp.zeros_like(loss_ref)
            dg_ref[...] = jnp.zeros_like(dg_ref)

        loss_ref[...] += jnp.broadcast_to(part, loss_ref.shape)
        dg_ref[...] += dg

    return pl.pallas_call(
        body, name="loss_head", grid=(SEQ // TR,),
        in_specs=[_rows(TR, D_MODEL), _rows(TR, D_MODEL), _const((1, D_MODEL)), _rows(TR, D_MODEL)],
        out_specs=[_const((8, LANES)), _rows(TR, D_MODEL), _rows(TR, D_MODEL), _const((1, D_MODEL))],
        out_shape=[jax.ShapeDtypeStruct((8, LANES), F32), jax.ShapeDtypeStruct((SEQ, D_MODEL), F32),
                   jax.ShapeDtypeStruct((SEQ, D_MODEL), BF16), jax.ShapeDtypeStruct((1, D_MODEL), F32)],
        compiler_params=_cp(("arbitrary",)),
    )(x2, f, g4, target)


def _mid_bwd(x2, mix, dy, dh3, g2, g3):
    def body(x2_ref, mix_ref, dy_ref, dh3_ref, g2_ref, g3_ref, dx2_ref, dmix_ref, dg2_ref, dg3_ref):
        i = pl.program_id(0)
        x2 = x2_ref[...]
        mixv = mix_ref[...]
        dh3 = dh3_ref[...]
        r3 = _rstd(x2)
        dx2 = dy_ref[...] + _rms_bwd(x2, r3, g3_ref[...], dh3)
        dx2_ref[...] = dx2
        r2 = _rstd(mixv)
        dmix_ref[...] = _rms_bwd(mixv, r2, g2_ref[...], dx2).astype(BF16)

        @pl.when(i == 0)
        def _():
            dg2_ref[...] = jnp.zeros_like(dg2_ref)
            dg3_ref[...] = jnp.zeros_like(dg3_ref)

        dg3_ref[...] += jnp.sum(dh3 * x2 * r3, axis=0, keepdims=True)
        dg2_ref[...] += jnp.sum(dx2 * mixv * r2, axis=0, keepdims=True)

    return pl.pallas_call(
        body, name="mid_bwd", grid=(SEQ // TR,),
        in_specs=[_rows(TR, D_MODEL)] * 4 + [_const((1, D_MODEL))] * 2,
        out_specs=[_rows(TR, D_MODEL), _rows(TR, D_MODEL), _const((1, D_MODEL)), _const((1, D_MODEL))],
        out_shape=[jax.ShapeDtypeStruct((SEQ, D_MODEL), F32), jax.ShapeDtypeStruct((SEQ, D_MODEL), BF16),
                   jax.ShapeDtypeStruct((1, D_MODEL), F32), jax.ShapeDtypeStruct((1, D_MODEL), F32)],
        compiler_params=_cp(("arbitrary",)),
    )(x2, mix, dy, dh3, g2, g3)


def _first_bwd(x, dx2, dh1, g1):
    def body(x_ref, dx2_ref, dh1_ref, g1_ref, dx_ref, dg1_ref):
        i = pl.program_id(0)
        x = x_ref[...]
        dh1 = dh1_ref[...]
        r = _rstd(x)
        dx_ref[...] = dx2_ref[...] + _rms_bwd(x, r, g1_ref[...], dh1)

        @pl.when(i == 0)
        def _():
            dg1_ref[...] = jnp.zeros_like(dg1_ref)

        dg1_ref[...] += jnp.sum(dh1 * x * r, axis=0, keepdims=True)

    return pl.pallas_call(
        body, name="first_bwd", grid=(SEQ // TR,),
        in_specs=[_rows(TR, D_MODEL)] * 3 + [_const((1, D_MODEL))],
        out_specs=[_rows(TR, D_MODEL), _const((1, D_MODEL))],
        out_shape=[jax.ShapeDtypeStruct((SEQ, D_MODEL), F32), jax.ShapeDtypeStruct((1, D_MODEL), F32)],
        compiler_params=_cp(("arbitrary",)),
    )(x, dx2, dh1, g1)


TC = 256
N_CB = D_FF // TC
GELU_C = math.sqrt(2.0 / math.pi)


def _shift_down(u, s):
    rolled = pltpu.roll(u, s, 0)
    row = lax.broadcasted_iota(jnp.int32, u.shape, 0)
    return jnp.where(row >= s, rolled, 0.0)


def _shift_up(u, s):
    n = u.shape[0]
    rolled = pltpu.roll(u, n - s, 0)
    row = lax.broadcasted_iota(jnp.int32, u.shape, 0)
    return jnp.where(row < n - s, rolled, 0.0)


def _conv3(u, w, b):
    return b + w[0:1] * _shift_down(u, 2) + w[1:2] * _shift_down(u, 1) + w[2:3] * u


def _gelu_and_grad(x):
    inner = GELU_C * (x + 0.044715 * (x * x * x))
    t = jnp.tanh(inner)
    gelu = 0.5 * x * (1.0 + t)
    dgelu = 0.5 * (1.0 + t) + 0.5 * x * (1.0 - t * t) * (GELU_C * (1.0 + 3 * 0.044715 * (x * x)))
    return gelu, dgelu


def _ffn_specs():
    col = lambda off: pl.BlockSpec((SEQ, TC), lambda *g: (0, g[-1] + off))
    w = lambda off: pl.BlockSpec((3, TC), lambda *g: (0, g[-1] + off))
    b = lambda off: pl.BlockSpec((1, TC), lambda *g: (0, g[-1] + off))
    return col, w, b


def _ffn_up_act(h3, wup_st, conv_w, conv_b):
    col, w, b = _ffn_specs()
    per_shard = wup_st.shape[2] // TC

    def body(h_ref, upg_ref, upv_ref, wg_ref, wv_ref, bg_ref, bv_ref, ug_ref, uv_ref, act_ref):
        h = h_ref[...]
        ug = _dot(h, upg_ref[...])
        uv = _dot(h, upv_ref[...])
        ug_ref[...] = ug
        uv_ref[...] = uv
        gate = _conv3(ug, wg_ref[...], bg_ref[...])
        val = _conv3(uv, wv_ref[...], bv_ref[...])
        act_ref[...] = (_gelu_and_grad(gate)[0] * val).astype(BF16)

    return pl.pallas_call(
        body, name="ffn_up_act", grid=(N_CB,),
        in_specs=[_const((SEQ, D_MODEL)),
                  pl.BlockSpec((None, D_MODEL, TC), lambda j: (j // per_shard, 0, j % per_shard)),
                  pl.BlockSpec((None, D_MODEL, TC), lambda j: (2 + j // per_shard, 0, j % per_shard)),
                  w(0), w(N_CB), b(0), b(N_CB)],
        out_specs=[col(0)] * 3,
        out_shape=[jax.ShapeDtypeStruct((SEQ, D_FF), F32)] * 2 + [jax.ShapeDtypeStruct((SEQ, D_FF), BF16)],
        compiler_params=_cp(("parallel",)),
    )(h3, wup_st, wup_st, conv_w, conv_w, conv_b, conv_b)


def _ffn_act_bwd(u_gate, u_val, df, wdown, conv_w, conv_b):
    col, w, b = _ffn_specs()
    both = lambda rows: pl.BlockSpec((2, rows, TC), lambda j: (0, 0, j))

    def body(ug_ref, uv_ref, df_ref, wd_ref, wg_ref, wv_ref, bg_ref, bv_ref, du_ref, dw_ref, db_ref):
        da = _dot(df_ref[...], wd_ref[...], NT)
        ug, uv = ug_ref[...], uv_ref[...]
        wg, wv = wg_ref[...], wv_ref[...]
        gate = _conv3(ug, wg, bg_ref[...])
        val = _conv3(uv, wv, bv_ref[...])
        gelu, dgelu = _gelu_and_grad(gate)
        for h, (duc, uh, wh) in enumerate(((da * val * dgelu, ug, wg), (da * gelu, uv, wv))):
            up1, up2 = _shift_up(duc, 1), _shift_up(duc, 2)
            du_ref[h] = (wh[2:3] * duc + wh[1:2] * up1 + wh[0:1] * up2).astype(BF16)
            db_ref[h] = jnp.sum(duc, axis=0, keepdims=True)
            dw_ref[h] = jnp.concatenate(
                [jnp.sum(up2 * uh, axis=0, keepdims=True), jnp.sum(up1 * uh, axis=0, keepdims=True),
                 jnp.sum(duc * uh, axis=0, keepdims=True)], axis=0)

    return pl.pallas_call(
        body, name="ffn_act_bwd", grid=(N_CB,),
        in_specs=[col(0), col(0), _const((SEQ, D_MODEL)), pl.BlockSpec((TC, D_MODEL), lambda j: (j, 0)),
                  w(0), w(N_CB), b(0), b(N_CB)],
        out_specs=[both(SEQ), both(3), both(1)],
        out_shape=[jax.ShapeDtypeStruct((2, SEQ, D_FF), BF16), jax.ShapeDtypeStruct((2, 3, D_FF), F32),
                   jax.ShapeDtypeStruct((2, 1, D_FF), F32)],
        compiler_params=_cp(("parallel",)),
    )(u_gate, u_val, df, wdown, conv_w, conv_w, conv_b, conv_b)


def _t5_onehot():
    rel = (np.arange(BLOCK)[:, None] + BLOCK) - np.arange(2 * BLOCK)[None, :]
    n = np.maximum(rel, 0)
    max_exact = N_BUCKETS // 2
    large = max_exact + (np.log(np.maximum(n, 1).astype(np.float32) / np.float32(max_exact))
                         / np.float32(math.log(MAX_DISTANCE / max_exact))
                         * np.float32(N_BUCKETS - max_exact)).astype(np.int32)
    large = np.minimum(large, N_BUCKETS - 1)
    bucket = np.where(n < max_exact, n, large).reshape(-1)
    return (bucket[None, :] == np.arange(N_BUCKETS)[:, None]).astype(np.float32)


N_REL = BLOCK * 2 * BLOCK


def _bias_table(rel_bias_t, onehot):
    def body(rb_ref, oh_ref, o_ref):
        o_ref[...] = _dot_ind(rb_ref[...], oh_ref[...])

    return pl.pallas_call(
        body, name="bias_table", grid=(1,),
        in_specs=[_const((N_Q_HEADS, N_BUCKETS)), _const((N_BUCKETS, N_REL))],
        out_specs=_const((N_Q_HEADS, N_REL)),
        out_shape=jax.ShapeDtypeStruct((N_Q_HEADS, N_REL), F32),
        compiler_params=_cp(("arbitrary",)),
    )(rel_bias_t, onehot)


def _bias_table_bwd(dbias, onehot):
    def body(db_ref, oh_ref, o_ref):
        acc = None
        for part in _split(db_ref[...], 3):
            t = _dot(part, oh_ref[...], NT)
            acc = t if acc is None else acc + t
        o_ref[...] = acc

    return pl.pallas_call(
        body, name="bias_table_bwd", grid=(1,),
        in_specs=[_const((N_Q_HEADS, N_REL)), _const((N_BUCKETS, N_REL))],
        out_specs=_const((N_Q_HEADS, N_BUCKETS)),
        out_shape=jax.ShapeDtypeStruct((N_Q_HEADS, N_BUCKETS), F32),
        compiler_params=_cp(("arbitrary",)),
    )(dbias, onehot)


def _attn_pieces(n, q, kvp, kvc, bias_ref, sinks_ref, hk):
    qi = lax.broadcasted_iota(jnp.int32, (BLOCK, 2 * BLOCK), 0)
    kj = lax.broadcasted_iota(jnp.int32, (BLOCK, 2 * BLOCK), 1)
    rel = qi + BLOCK - kj
    first_key = jnp.where(n > 0, 0, BLOCK)
    ok = jnp.where(rel >= 0, jnp.where(rel < BLOCK, jnp.where(kj >= first_key, 1.0, 0.0), 0.0), 0.0)
    ok4 = jnp.concatenate([ok] * Q_PER_KV, axis=0) > 0.5
    c0 = hk * HEAD_DIM
    kcat = jnp.concatenate([kvp[:, c0:c0 + HEAD_DIM], kvc[:, c0:c0 + HEAD_DIM]], axis=0).astype(BF16)
    vcat = jnp.concatenate([kvp[:, D_KV + c0:D_KV + c0 + HEAD_DIM], kvc[:, D_KV + c0:D_KV + c0 + HEAD_DIM]],
                           axis=0).astype(BF16)
    q0 = hk * Q_PER_KV * HEAD_DIM
    qs = jnp.concatenate([q[:, q0 + g * HEAD_DIM:q0 + (g + 1) * HEAD_DIM] for g in range(Q_PER_KV)],
                         axis=0).astype(BF16)
    s = _dot(qs, kcat, NT) * (HEAD_DIM ** -0.5) + bias_ref[hk]
    s = jnp.where(ok4, s, NEG_INF)
    row = lax.broadcasted_iota(jnp.int32, (Q_PER_KV * BLOCK, 1), 0)
    sink = jnp.zeros((Q_PER_KV * BLOCK, 1), F32)
    for g in range(Q_PER_KV):
        sink = jnp.where((row >> 7) == g, sinks_ref[hk * Q_PER_KV + g], sink)
    m = jnp.maximum(jnp.max(s, axis=-1, keepdims=True), sink)
    p = jnp.exp(s - m)
    es = jnp.exp(sink - m)
    inv = 1.0 / (jnp.sum(p, axis=-1, keepdims=True) + es)
    return qs, kcat, vcat, p * inv, es * inv


def _attn_in_specs():
    return [pl.BlockSpec((BLOCK, D_ATTN), lambda n: (n, 0)),
            pl.BlockSpec((BLOCK, 2 * D_KV), lambda n: (jnp.maximum(n - 1, 0), D_ATTN // (2 * D_KV))),
            pl.BlockSpec((BLOCK, 2 * D_KV), lambda n: (n, D_ATTN // (2 * D_KV))),
            _const((N_KV_HEADS, Q_PER_KV * BLOCK, 2 * BLOCK)),
            pl.BlockSpec(memory_space=pltpu.SMEM)]


def _unstack_heads(t):
    return jnp.concatenate([t[g * BLOCK:(g + 1) * BLOCK] for g in range(Q_PER_KV)], axis=1)


def _attn_fwd(proj, bias, sinks):
    def body(q_ref, kvp_ref, kvc_ref, bias_ref, sinks_ref, o_ref):
        n = pl.program_id(0)
        q, kvp, kvc = q_ref[...], kvp_ref[...], kvc_ref[...]
        outs = []
        for hk in range(N_KV_HEADS):
            _, _, vcat, probs, _ = _attn_pieces(n, q, kvp, kvc, bias_ref, sinks_ref, hk)
            outs.append(_unstack_heads(_dot(probs.astype(BF16), vcat)))
        o_ref[...] = jnp.concatenate(outs, axis=1)

    return pl.pallas_call(
        body, name="attn_fwd", grid=(SEQ // BLOCK,),
        in_specs=_attn_in_specs(),
        out_specs=pl.BlockSpec((BLOCK, D_ATTN), lambda n: (n, 0)),
        out_shape=jax.ShapeDtypeStruct((SEQ, D_ATTN), F32),
        compiler_params=_cp(("parallel",)),
    )(proj, proj, proj, bias, sinks)


def _attn_bwd(proj, bias, sinks, dcat):
    nb = SEQ // BLOCK

    def body(q_ref, kvp_ref, kvc_ref, bias_ref, sinks_ref, do_ref, dq_ref, dkv_ref, dbias_ref, dsink_ref, dsacc):
        n = pl.program_id(0)

        @pl.when(n == 0)
        def _():
            dkv_ref[...] = jnp.zeros_like(dkv_ref)
            dbias_ref[...] = jnp.zeros_like(dbias_ref)
            dsacc[...] = jnp.zeros_like(dsacc)

        q, kvp, kvc = q_ref[...], kvp_ref[...], kvc_ref[...]
        do_all = do_ref[...]
        dqs, dks, dvs = [], [], []
        for hk in range(N_KV_HEADS):
            qs, kcat, vcat, probs, psink = _attn_pieces(n, q, kvp, kvc, bias_ref, sinks_ref, hk)
            q0 = hk * Q_PER_KV * HEAD_DIM
            do = jnp.concatenate([do_all[:, q0 + g * HEAD_DIM:q0 + (g + 1) * HEAD_DIM] for g in range(Q_PER_KV)],
                                 axis=0).astype(BF16)
            dprobs = _dot(do, vcat, NT)
            dvs.append(_dot(probs.astype(BF16), do, TN))
            rowdot = jnp.sum(probs * dprobs, axis=-1, keepdims=True)
            ds = probs * (dprobs - rowdot)
            dsacc[hk] += -psink * rowdot
            dbias_ref[hk] += ds
            dsb = (ds * (HEAD_DIM ** -0.5)).astype(BF16)
            dqs.append(_unstack_heads(_dot(dsb, kcat)))
            dks.append(_dot(dsb, qs, TN))
        dq_ref[...] = jnp.concatenate(dqs, axis=1)
        upd = jnp.concatenate(dks + dvs, axis=1)
        cur = pl.multiple_of(n * BLOCK, BLOCK)
        dkv_ref[pl.ds(cur, BLOCK), :] += upd[BLOCK:]

        @pl.when(n > 0)
        def _():
            prev = pl.multiple_of((n - 1) * BLOCK, BLOCK)
            dkv_ref[pl.ds(prev, BLOCK), :] += upd[:BLOCK]

        @pl.when(n == nb - 1)
        def _():
            for hk in range(N_KV_HEADS):
                for g in range(Q_PER_KV):
                    tot = jnp.sum(dsacc[hk, g * BLOCK:(g + 1) * BLOCK, :], axis=0, keepdims=True)
                    h = hk * Q_PER_KV + g
                    dsink_ref[h:h + 1, :] = jnp.broadcast_to(tot, (1, LANES))

    return pl.pallas_call(
        body, name="attn_bwd", grid=(nb,),
        in_specs=_attn_in_specs() + [pl.BlockSpec((BLOCK, D_ATTN), lambda n: (n, 0))],
        out_specs=[pl.BlockSpec((BLOCK, D_ATTN), lambda n: (n, 0)), _const((SEQ, 2 * D_KV)),
                   _const((N_KV_HEADS, Q_PER_KV * BLOCK, 2 * BLOCK)), _const((N_Q_HEADS, LANES))],
        out_shape=[jax.ShapeDtypeStruct((SEQ, D_ATTN), F32), jax.ShapeDtypeStruct((SEQ, 2 * D_KV), F32),
                   jax.ShapeDtypeStruct((N_KV_HEADS, Q_PER_KV * BLOCK, 2 * BLOCK), F32),
                   jax.ShapeDtypeStruct((N_Q_HEADS, LANES), F32)],
        scratch_shapes=[pltpu.VMEM((N_KV_HEADS, Q_PER_KV * BLOCK, 1), F32)],
        compiler_params=_cp(("arbitrary",)),
    )(proj, proj, proj, bias, sinks, dcat)


@jax.custom_vjp
def _head_sum(x):
    ones = _head_ones(LANES)
    return jnp.concatenate([_dot_ind(x[:, c:c + LANES], ones, 2) for c in range(0, x.shape[-1], LANES)], axis=1)


_head_sum.defvjp(lambda x: (_head_sum(x), None), lambda _, ct: (_head_sum(ct),))


@jax.custom_vjp
def _bdot(a, w):
    return _dot(a.astype(BF16), w.astype(BF16))


def _bdot_bwd(res, ct):
    a, w = res
    ctb = ct.astype(BF16)
    return _dot(ctb, w.astype(BF16), NT), _dot(a.astype(BF16), ctb, TN)


_bdot.defvjp(lambda a, w: (_bdot(a, w), (a, w)), _bdot_bwd)


def _sigmoid(x):
    return 0.5 * (jnp.tanh(0.5 * x) + 1.0)


def _softplus(x):
    return jnp.maximum(x, 0.0) + jnp.log(1.0 + jnp.exp(-jnp.abs(x)))


def _rwkv_core(r, k, v, zwa, zg, w0, wdu, a0, wiu, wgu, k_k, k_a):
    w_log = -_softplus(-(w0 + _bdot(jnp.tanh(zwa), wdu))) - 0.5
    decay = jnp.exp(-jnp.exp(w_log))
    a = _sigmoid(a0 + _bdot(zwa, wiu))
    g = _bdot(_sigmoid(zg), wgu)
    kk = k * k_k
    kk = kk / jnp.maximum(jnp.sqrt(_head_sum(kk * kk)), 1e-12)
    k2 = k * (1.0 + (a - 1.0) * k_a)
    return r, decay, k2, v, -kk, kk * a, g


def _rwkv_out(o, r, k2, v, g, lng, lnb, rk):
    mu = _head_sum(o) * (1.0 / HEAD_DIM)
    d = o - mu
    var = _head_sum(d * d) * (1.0 / HEAD_DIM)
    on = d * lax.rsqrt(var + GN_EPS) * lng + lnb
    bonus = _head_sum(r * k2 * rk) * v
    return (on + bonus) * g


P_SPLITS = (0, 512, 1024, 1536, 1664, 1792)
N_PREP_PARAMS = 7
HALO = 8


def _shifted_pieces(i, p_ref, halo_ref, mix_ref):
    p = p_ref[:, P_OFF:]
    prev_row = halo_ref[HALO - 1:HALO, P_OFF:] * jnp.where(i > 0, 1.0, 0.0)
    row = lax.broadcasted_iota(jnp.int32, p.shape, 0)
    pprev = jnp.where(row == 0, prev_row, pltpu.roll(p, 1, 0))
    delta = pprev - p
    ps = p + delta * mix_ref[...]
    return [ps[:, a:b] for a, b in zip(P_SPLITS[:-1], P_SPLITS[1:])], delta


def _prep_in_specs():
    return [_rows(TR, D_IN),
            pl.BlockSpec((HALO, D_IN), lambda i: (jnp.maximum(i * (TR // HALO) - 1, 0), 0)),
            _const((1, RWKV_COLS)), _const((1, D_RWKV)), _const((LANES, D_RWKV)), _const((1, D_RWKV)),
            _const((LANES, D_RWKV)), _const((LANES, D_RWKV)), _const((1, D_RWKV)), _const((1, D_RWKV))]


def _rwkv_prep(proj, mix, prm):
    def body(p_ref, halo_ref, mix_ref, *refs):
        prm_refs, outs = refs[:N_PREP_PARAMS], refs[N_PREP_PARAMS:]
        pieces, _ = _shifted_pieces(pl.program_id(0), p_ref, halo_ref, mix_ref)
        vals = _rwkv_core(*pieces, *[t[...] for t in prm_refs])
        for ref, val in zip(outs, vals):
            ref[...] = val

    return pl.pallas_call(
        body, name="rwkv_prep", grid=(SEQ // TR,),
        in_specs=_prep_in_specs(),
        out_specs=[_rows(TR, D_RWKV)] * 7,
        out_shape=[jax.ShapeDtypeStruct((SEQ, D_RWKV), F32)] * 7,
        compiler_params=_cp(("parallel",)),
    )(proj, proj, mix, *prm)


def _rwkv_prep_bwd(proj, mix, prm, cts):
    def body(p_ref, halo_ref, mix_ref, *refs):
        i = pl.program_id(0)
        prm_refs = refs[:N_PREP_PARAMS]
        ct_refs = refs[N_PREP_PARAMS:N_PREP_PARAMS + 10]
        dps_ref, dmix_ref = refs[N_PREP_PARAMS + 10:N_PREP_PARAMS + 12]
        dprm_refs = refs[N_PREP_PARAMS + 12:]
        pieces, delta = _shifted_pieces(i, p_ref, halo_ref, mix_ref)
        _, vjp = jax.vjp(_rwkv_core, *pieces, *[t[...] for t in prm_refs])
        dr1, dr2, dw, dk1, dk2, dv1, dv2, dkkn, db, dg = [t[...] for t in ct_refs]
        grads = vjp((dr1 + dr2, dw, dk1 + dk2, dv1 + dv2, dkkn, db, dg))
        dps = jnp.concatenate(grads[:5], axis=1)
        dps_ref[...] = dps

        @pl.when(i == 0)
        def _():
            dmix_ref[...] = jnp.zeros_like(dmix_ref)
            for ref in dprm_refs:
                ref[...] = jnp.zeros_like(ref)

        dmix_ref[...] += jnp.sum(dps * delta, axis=0, keepdims=True)
        for ref, gval in zip(dprm_refs, grads[5:]):
            ref[...] += gval

    prm_shapes = [(1, D_RWKV), (LANES, D_RWKV), (1, D_RWKV), (LANES, D_RWKV), (LANES, D_RWKV), (1, D_RWKV), (1, D_RWKV)]
    return pl.pallas_call(
        body, name="rwkv_prep_bwd", grid=(SEQ // TR,),
        in_specs=_prep_in_specs() + [_rows(TR, D_RWKV)] * 10,
        out_specs=[_rows(TR, RWKV_COLS), _const((1, RWKV_COLS))] + [_const(s) for s in prm_shapes],
        out_shape=[jax.ShapeDtypeStruct((SEQ, RWKV_COLS), F32), jax.ShapeDtypeStruct((1, RWKV_COLS), F32)]
        + [jax.ShapeDtypeStruct(s, F32) for s in prm_shapes],
        compiler_params=_cp(("arbitrary",)),
    )(proj, proj, mix, *prm, *cts)


def _rwkv_post(o, r, k2, v, g, lng, lnb, rk, attn):
    def body(o_ref, r_ref, k_ref, v_ref, g_ref, lng_ref, lnb_ref, rk_ref, attn_ref, cat_ref):
        rw = _rwkv_out(*[t[...] for t in (o_ref, r_ref, k_ref, v_ref, g_ref, lng_ref, lnb_ref, rk_ref)])
        cat_ref[...] = jnp.concatenate([attn_ref[...], rw], axis=1).astype(BF16)

    return pl.pallas_call(
        body, name="rwkv_post", grid=(SEQ // TR,),
        in_specs=[_rows(TR, D_RWKV)] * 5 + [_const((1, D_RWKV))] * 3 + [_rows(TR, D_ATTN)],
        out_specs=_rows(TR, D_MODEL),
        out_shape=jax.ShapeDtypeStruct((SEQ, D_MODEL), BF16),
        compiler_params=_cp(("parallel",)),
    )(o, r, k2, v, g, lng, lnb, rk, attn)


def _rwkv_post_bwd(o, r, k2, v, g, lng, lnb, rk, dcat):
    def body(o_ref, r_ref, k_ref, v_ref, g_ref, lng_ref, lnb_ref, rk_ref, dcat_ref,
             do_ref, dr_ref, dk_ref, dv_ref, dg_ref, dlng_ref, dlnb_ref, drk_ref):
        i = pl.program_id(0)
        args = [t[...] for t in (o_ref, r_ref, k_ref, v_ref, g_ref, lng_ref, lnb_ref, rk_ref)]
        _, vjp = jax.vjp(_rwkv_out, *args)
        grads = vjp(dcat_ref[:, D_ATTN:])
        for ref, gval in zip((do_ref, dr_ref, dk_ref, dv_ref, dg_ref), grads[:5]):
            ref[...] = gval

        @pl.when(i == 0)
        def _():
            for ref in (dlng_ref, dlnb_ref, drk_ref):
                ref[...] = jnp.zeros_like(ref)

        for ref, gval in zip((dlng_ref, dlnb_ref, drk_ref), grads[5:]):
            ref[...] += gval

    return pl.pallas_call(
        body, name="rwkv_post_bwd", grid=(SEQ // TR,),
        in_specs=[_rows(TR, D_RWKV)] * 5 + [_const((1, D_RWKV))] * 3 + [_rows(TR, D_MODEL)],
        out_specs=[_rows(TR, D_RWKV)] * 5 + [_const((1, D_RWKV))] * 3,
        out_shape=[jax.ShapeDtypeStruct((SEQ, D_RWKV), F32)] * 5 + [jax.ShapeDtypeStruct((1, D_RWKV), F32)] * 3,
        compiler_params=_cp(("arbitrary",)),
    )(o, r, k2, v, g, lng, lnb, rk, dcat)


def _assemble_dproj(dq, dkv, dps, mix):
    last = SEQ // HALO - 1

    def body(dq_ref, dkv_ref, dps_ref, nxt_ref, mix_ref, o_ref):
        i = pl.program_id(0)
        dps = dps_ref[...]
        mixv = mix_ref[...]
        nxt_row = nxt_ref[0:1, :] * jnp.where(i < SEQ // TR - 1, 1.0, 0.0)
        row = lax.broadcasted_iota(jnp.int32, dps.shape, 0)
        up = jnp.where(row == TR - 1, nxt_row, pltpu.roll(dps, TR - 1, 0))
        dp = dps * (1.0 - mixv) + up * mixv
        o_ref[...] = jnp.concatenate([dq_ref[...], dkv_ref[...], dp], axis=1).astype(BF16)

    return pl.pallas_call(
        body, name="assemble_dproj", grid=(SEQ // TR,),
        in_specs=[_rows(TR, D_ATTN), _rows(TR, 2 * D_KV), _rows(TR, RWKV_COLS),
                  pl.BlockSpec((HALO, RWKV_COLS), lambda i: (jnp.minimum((i + 1) * (TR // HALO), last), 0)),
                  _const((1, RWKV_COLS))],
        out_specs=_rows(TR, D_IN),
        out_shape=jax.ShapeDtypeStruct((SEQ, D_IN), BF16),
        compiler_params=_cp(("parallel",)),
    )(dq, dkv, dps, dps, mix)


N_PAIR = D_RWKV // LANES
CHUNK = 64
N_CHUNK = SEQ // CHUNK
GROUP = 8
STATE = (N_PAIR, HEAD_DIM, LANES)


def _lane_sums(lhs_tiles, ones2):
    out = _dot(jnp.concatenate(lhs_tiles, axis=0), ones2)
    return [out[i * HEAD_DIM:(i + 1) * HEAD_DIM] for i in range(len(lhs_tiles))]


def _seg_sum(xs, ones2):
    return _lane_sums([jnp.concatenate(_split(x, 2), axis=1) for x in xs], ones2)


def _seg_sum_rows(xs, ones2):
    out = _dot(jnp.concatenate(_split(jnp.concatenate(xs, axis=0), 2), axis=1), ones2)
    return [out[i * GROUP:(i + 1) * GROUP] for i in range(len(xs))]


def _col_form(rows, diag, ones2):
    zero = jnp.zeros((HEAD_DIM, LANES), BF16)
    tiles = []
    for row in rows:
        hi = row.astype(BF16)
        lo = (row - hi.astype(F32)).astype(BF16)
        tiles.append(jnp.concatenate(
            [jnp.where(diag, jnp.broadcast_to(part, (HEAD_DIM, LANES)), zero) for part in (hi, lo)], axis=1))
    return _lane_sums(tiles, ones2)


def _scan_consts():
    ones2 = jnp.concatenate([_head_ones(LANES)] * 2, axis=0)
    sub = lax.broadcasted_iota(jnp.int32, (HEAD_DIM, LANES), 0)
    lane_in_head = lax.broadcasted_iota(jnp.int32, (HEAD_DIM, LANES), 1) & (HEAD_DIM - 1)
    return ones2, lane_in_head == sub, lane_in_head


def _rows_of_columns(tile):
    t = tile.T
    return jnp.concatenate([t[:CHUNK], t[HEAD_DIM:HEAD_DIM + CHUNK]], axis=1)


def _pair(j):
    return slice(j * LANES, (j + 1) * LANES)


def _scan_fwd(r, w, k, v, kkn, b):
    def body(r_ref, w_ref, k_ref, v_ref, kkn_ref, b_ref, o_ref, st_ref, sa_ref, s_scr):
        c = pl.program_id(0)
        ones2, diag, lane_in_head = _scan_consts()

        @pl.when(c == 0)
        def _():
            s_scr[...] = jnp.zeros_like(s_scr)

        def group(gi, carry):
            row0 = pl.multiple_of(gi * GROUP, GROUP)
            states, ocols = list(carry[:N_PAIR]), list(carry[N_PAIR:])
            tiles = [[t[pl.ds(row0, GROUP), _pair(j)] for t in (r_ref, w_ref, k_ref, v_ref, kkn_ref, b_ref)]
                     for j in range(N_PAIR)]
            def row(j, name, u):
                return tiles[j]["rwkvnb".index(name)][u:u + 1]

            def emit_out(u, after):
                outs = _seg_sum([s[j] * row(j, "r", u + d) for d, s in enumerate(after) for j in range(N_PAIR)], ones2)
                for d in range(2):
                    here = lane_in_head == gi * GROUP + u + d
                    for j in range(N_PAIR):
                        ocols[j] = jnp.where(here, outs[d * N_PAIR + j], ocols[j])

            def vcols_of(u):
                cols = _col_form([row(j, "v", u + d) for d in range(2) for j in range(N_PAIR)], diag, ones2)
                return cols[:N_PAIR], cols[N_PAIR:]

            n_next = [pltpu.roll(tiles[j][4], GROUP - 1, 0) for j in range(N_PAIR)]
            dots = _seg_sum_rows([tiles[j][5] * n_next[j] for j in range(N_PAIR)]
                                 + [tiles[j][2] * n_next[j] for j in range(N_PAIR)], ones2)
            b_n, k_n = dots[:N_PAIR], dots[N_PAIR:]
            w_n = [tiles[j][1] * n_next[j] for j in range(N_PAIR)]

            vcols = vcols_of(0)
            after = None
            for u in range(0, GROUP, 2):
                prods = _seg_sum([states[j] * row(j, "n", u) for j in range(N_PAIR)]
                                 + [states[j] * w_n[j][u:u + 1] for j in range(N_PAIR)], ones2)
                if after is not None:
                    emit_out(u - 2, after)
                nxt = vcols_of(u + 2) if u + 2 < GROUP else None
                first, second = [], []
                for j in range(N_PAIR):
                    sa1 = prods[j]
                    sa2 = prods[N_PAIR + j] + sa1 * b_n[j][u:u + 1] + vcols[0][j] * k_n[j][u:u + 1]
                    s1 = states[j] * row(j, "w", u) + sa1 * row(j, "b", u) + vcols[0][j] * row(j, "k", u)
                    s2 = s1 * row(j, "w", u + 1) + sa2 * row(j, "b", u + 1) + vcols[1][j] * row(j, "k", u + 1)
                    st_ref[row0 + u, j] = s1
                    sa_ref[row0 + u, j] = sa1
                    st_ref[row0 + u + 1, j] = s2
                    sa_ref[row0 + u + 1, j] = sa2
                    first.append(s1)
                    second.append(s2)
                    states[j] = s2
                after, vcols = (first, second), nxt
            emit_out(GROUP - 2, after)
            return tuple(states + ocols)

        zero = jnp.zeros((HEAD_DIM, LANES), F32)
        fin = lax.fori_loop(0, CHUNK // GROUP, group, tuple(s_scr[j] for j in range(N_PAIR)) + (zero,) * N_PAIR)
        for j in range(N_PAIR):
            s_scr[j] = fin[j]
            o_ref[:, _pair(j)] = _rows_of_columns(fin[N_PAIR + j])

    blk = pl.BlockSpec((CHUNK, D_RWKV), lambda c: (c, 0))
    per_step = pl.BlockSpec((CHUNK,) + STATE, lambda c: (c, 0, 0, 0))
    return pl.pallas_call(
        body, name="rwkv_scan_fwd", grid=(N_CHUNK,),
        in_specs=[blk] * 6,
        out_specs=[blk, per_step, per_step],
        out_shape=[jax.ShapeDtypeStruct((SEQ, D_RWKV), F32)] + [jax.ShapeDtypeStruct((SEQ,) + STATE, F32)] * 2,
        scratch_shapes=[pltpu.VMEM(STATE, F32)],
        compiler_params=_cp(("arbitrary",)),
    )(r, w, k, v, kkn, b)


def _scan_bwd(r, w, k, v, kkn, b, do, states, sas, ds_in, prev, name, first_chunk, n_chunks):
    top = first_chunk + n_chunks - 1

    def body(r_ref, w_ref, k_ref, v_ref, kkn_ref, b_ref, do_ref, st_ref, before_ref, sa_ref, ds_in_ref, *rest):
        dr_ref, dw_ref, dk_ref, dv_ref, dkkn_ref, db_ref, ds_out_ref, ds_scr = rest[-8:]
        i = pl.program_id(0)
        ones2, diag, lane_in_head = _scan_consts()

        @pl.when(i == 0)
        def _():
            ds_scr[...] = ds_in_ref[...]

        entry = [before_ref[0, j] * jnp.where(i < top, 1.0, 0.0) for j in range(N_PAIR)]

        def reverse(gr, carry):
            gi = CHUNK // GROUP - 1 - gr
            row0 = pl.multiple_of(gi * GROUP, GROUP)
            dstates, dvcols = list(carry[:N_PAIR]), list(carry[N_PAIR:])
            tiles = [[t[pl.ds(row0, GROUP), _pair(j)]
                      for t in (r_ref, w_ref, k_ref, v_ref, kkn_ref, b_ref, do_ref)] for j in range(N_PAIR)]
            rows = [[[None] * GROUP for _ in range(5)] for _ in range(N_PAIR)]

            def row(j, name, u):
                return tiles[j]["rwkvnbd".index(name)][u:u + 1]

            def cols_of(u):
                cols = _col_form([row(j, name, u - d) for d in range(2) for name in "dv" for j in range(N_PAIR)],
                                 diag, ones2)
                return [[(cols[(2 * d) * N_PAIR + j], cols[(2 * d + 1) * N_PAIR + j]) for j in range(N_PAIR)]
                        for d in range(2)]

            def emit_dv(u, dsps):
                outs = _seg_sum([dsp[j] * row(j, "k", u - d) for d, dsp in enumerate(dsps) for j in range(N_PAIR)], ones2)
                for d in range(2):
                    here = lane_in_head == gi * GROUP + u - d
                    for j in range(N_PAIR):
                        dvcols[j] = jnp.where(here, outs[d * N_PAIR + j], dvcols[j])

            b_prev = [pltpu.roll(tiles[j][5], 1, 0) for j in range(N_PAIR)]
            dots = _seg_sum_rows([tiles[j][4] * b_prev[j] for j in range(N_PAIR)]
                                 + [tiles[j][0] * tiles[j][5] for j in range(N_PAIR)], ones2)
            n_b, r_b = dots[:N_PAIR], dots[N_PAIR:]
            w_b = [tiles[j][1] * b_prev[j] for j in range(N_PAIR)]

            def outputs(u, j, dsp, dsa, docol, vcol):
                tl = gi * GROUP + u
                if u > 0:
                    s_prev = st_ref[tl - 1, j]
                else:
                    s_prev = jnp.where(gi == 0, entry[j], st_ref[jnp.maximum(tl - 1, 0), j])
                rows[j][0][u] = jnp.sum(st_ref[tl, j] * docol, axis=0, keepdims=True)
                rows[j][1][u] = jnp.sum(dsp * s_prev, axis=0, keepdims=True)
                rows[j][2][u] = jnp.sum(dsp * vcol, axis=0, keepdims=True)
                rows[j][3][u] = jnp.sum(s_prev * dsa, axis=0, keepdims=True)
                rows[j][4][u] = jnp.sum(dsp * sa_ref[tl, j], axis=0, keepdims=True)

            cols = cols_of(GROUP - 1)
            before = None
            for u in range(GROUP - 1, 0, -2):
                dsp1 = [dstates[j] + cols[0][j][0] * row(j, "r", u) for j in range(N_PAIR)]
                prods = _seg_sum([dsp1[j] * row(j, "b", u) for j in range(N_PAIR)]
                                 + [dsp1[j] * w_b[j][u:u + 1] for j in range(N_PAIR)], ones2)
                if before is not None:
                    emit_dv(u + 2, before)
                nxt = cols_of(u - 2) if u >= 2 else None
                dsp2 = []
                for j in range(N_PAIR):
                    dsa1 = prods[j]
                    dsa2 = prods[N_PAIR + j] + dsa1 * n_b[j][u:u + 1] + cols[1][j][0] * r_b[j][u - 1:u]
                    mid = dsp1[j] * row(j, "w", u) + dsa1 * row(j, "n", u) + cols[1][j][0] * row(j, "r", u - 1)
                    outputs(u, j, dsp1[j], dsa1, *cols[0][j])
                    outputs(u - 1, j, mid, dsa2, *cols[1][j])
                    dstates[j] = mid * row(j, "w", u - 1) + dsa2 * row(j, "n", u - 1)
                    dsp2.append(mid)
                before, cols = (dsp1, dsp2), nxt
            emit_dv(1, before)
            for j in range(N_PAIR):
                for ref, rr in zip((dr_ref, dw_ref, dk_ref, dkkn_ref, db_ref), rows[j]):
                    ref[pl.ds(row0, GROUP), _pair(j)] = jnp.concatenate(rr, axis=0)
            return tuple(dstates + dvcols)

        zero = jnp.zeros((HEAD_DIM, LANES), F32)
        dfin = lax.fori_loop(0, CHUNK // GROUP, reverse, tuple(ds_scr[j] for j in range(N_PAIR)) + (zero,) * N_PAIR)
        for j in range(N_PAIR):
            ds_scr[j] = dfin[j]
            dv_ref[:, _pair(j)] = _rows_of_columns(dfin[N_PAIR + j])

        @pl.when(i == n_chunks - 1)
        def _():
            ds_out_ref[...] = ds_scr[...]

    blk = pl.BlockSpec((CHUNK, D_RWKV), lambda i: (top - i, 0))
    per_step = pl.BlockSpec((CHUNK,) + STATE, lambda i: (top - i, 0, 0, 0))
    step_before = pl.BlockSpec((1,) + STATE, lambda i: (jnp.maximum((top - i) * CHUNK - 1, 0), 0, 0, 0))
    prev = [] if prev is None else list(prev)
    outs = pl.pallas_call(
        body, name=name, grid=(n_chunks,),
        in_specs=[blk] * 7 + [per_step, step_before, per_step, _const(STATE)] + [ANY] * len(prev),
        out_specs=[blk] * 6 + [_const(STATE)],
        out_shape=[jax.ShapeDtypeStruct((SEQ, D_RWKV), F32)] * 6 + [jax.ShapeDtypeStruct(STATE, F32)],
        scratch_shapes=[pltpu.VMEM(STATE, F32)],
        input_output_aliases={11 + t: t for t in range(len(prev))},
        compiler_params=_cp(("arbitrary",)),
    )(r, w, k, v, kkn, b, do, states, states, sas, ds_in, *prev)
    return outs[:6], outs[6]


def _stacked(rows, cols, pick):
    return pl.BlockSpec((None, rows, cols), pick)


def _local_step(x, target, sm, win_st):
    def tied(t, token):
        return t if token is None else t + token[0:1, 0:1].reshape((1,) * t.ndim)

    zpad = jnp.zeros((LORA_DECAY, D_RWKV), F32)
    prm = [sm["w0"], jnp.concatenate([sm["w_decay_up"], zpad], axis=0), sm["a0"],
           jnp.concatenate([zpad, sm["w_iclr_up"]], axis=0), sm["w_gate_up"], sm["k_k"], sm["k_a"]]
    mix = sm["rwkv_shift_mix"]
    onehot = jnp.asarray(_t5_onehot(), BF16)
    sinks = sm["sinks"].reshape(N_Q_HEADS)
    lng, lnb, rk = sm["ln_x_g"], sm["ln_x_b"], sm["r_k"].reshape(1, D_RWKV)

    h1 = _norm_cast(x, sm["norm_mix_pre"], "norm_in")
    proj = _matmul(h1, win_st, "nn", "proj", m=SEQ, n=D_IN, k=D_MODEL, tm=SEQ, tn=640,
                   b_spec=_stacked(D_MODEL, 640, lambda i, j: (j, 0, 0)))
    bias = _bias_table(sm["rel_bias"].T, onehot).reshape(N_KV_HEADS, Q_PER_KV * BLOCK, 2 * BLOCK)
    attn = _attn_fwd(proj, bias, sinks)
    r, w, k2, v, kkn, b, g = _rwkv_prep(proj, mix, prm)
    o, states, sas = _scan_fwd(r, w, k2, v, kkn, b)
    wout, wup_st, wdown = yield ("rest_weights", o)
    cat = _rwkv_post(o, r, k2, v, g, lng, lnb, rk, attn)
    mixo = _matmul(cat, wout, "nn", "out_proj", m=SEQ, n=D_MODEL, k=D_MODEL, tm=SEQ, tn=512)
    x2, h3 = _mix_norm(x, mixo, sm["norm_mix_post"], sm["norm_ffn_pre"])
    u_gate, u_val, act = _ffn_up_act(h3, wup_st, sm["conv_w"], sm["conv_b"])
    f = _matmul(act, wdown, "nn", "ffn_down", m=SEQ, n=D_MODEL, k=D_FF, tm=1024, tn=512)
    loss, dy, df, d_g4 = _loss_head(x2, f, sm["norm_ffn_post"], target)

    d_wdown = _matmul(act, df, "tn", "d_wdown", m=D_FF, n=D_MODEL, k=SEQ, tm=512, tn=D_MODEL)
    du, d_convw, d_convb = _ffn_act_bwd(u_gate, u_val, df, wdown, sm["conv_w"], sm["conv_b"])
    d_convw = d_convw.transpose(1, 0, 2).reshape(3, 2 * D_FF)
    d_convb = d_convb.reshape(1, 2 * D_FF)
    dh3 = _matmul_nt_shards(du, wup_st, "d_h3", m=SEQ, n=D_MODEL, tm=512, tn=512,
                            a_spec=pl.BlockSpec((2, 512, D_FF), lambda i, j: (0, i, 0)),
                            a_piece=lambda ref, s: ref[s // 2, :, (s % 2) * 2048:(s % 2 + 1) * 2048])
    d_wup = _matmul(h3, du, "tn", "d_wup", m=D_MODEL, n=2 * D_FF, k=SEQ, tm=D_MODEL, tn=512,
                    b_spec=pl.BlockSpec((None, SEQ, 512), lambda i, j: (j // 8, 0, j % 8)),
                    out=((N_CHIPS, D_MODEL, 2048), _stacked(D_MODEL, 512, lambda i, j: (j // 4, 0, j % 4))))
    dx2, dmix, d_g2, d_g3 = _mid_bwd(x2, mixo, dy, dh3, sm["norm_mix_post"], sm["norm_ffn_pre"])
    dcat = _matmul(dmix, wout, "nt", "d_cat", m=SEQ, n=D_MODEL, k=D_MODEL, tm=SEQ, tn=512)
    d_wout = _matmul(cat, dmix, "tn", "d_wout", m=D_MODEL, n=D_MODEL, k=SEQ, tm=512, tn=D_MODEL)
    token = yield ("grads_a", (d_wdown, d_wup, d_wout))
    do, dr_p, dk_p, dv_p, dg, d_lng, d_lnb, d_rk = _rwkv_post_bwd(o, r, k2, v, g, lng, tied(lnb, token), rk, dcat)
    half = N_CHUNK // 2
    ds_end = jnp.zeros(STATE, F32)
    late, ds_mid = _scan_bwd(r, w, k2, v, kkn, b, do, states, sas, ds_end, None, "rwkv_scan_bwd_late", half, half)
    token = yield ("seam_1", ds_mid)
    scan_cts, ds_first = _scan_bwd(r, w, k2, v, kkn, b, do, states, sas, tied(ds_mid, token), late,
                                   "rwkv_scan_bwd_early", 0, half)
    dr_s, dw_s, dk_s, dv_s, dkkn_s, db_s = scan_cts
    token = yield ("seam_2", ds_first)
    prep_grads = _rwkv_prep_bwd(proj, tied(mix, token), prm,
                                (dr_s, dr_p, dw_s, dk_s, dk_p, dv_s, dv_p, dkkn_s, db_s, dg))
    dps, d_mix, d_w0, d_wdu, d_a0, d_wiu, d_wgu, d_kk, d_ka = prep_grads
    dq, dkv, dbias, dsink = _attn_bwd(proj, bias, sinks, dcat)
    d_relb = _bias_table_bwd(dbias.reshape(N_Q_HEADS, N_REL), onehot).T
    dproj = _assemble_dproj(dq, dkv, dps, mix)
    d_win = _matmul(h1, dproj, "tn", "d_win", m=D_MODEL, n=D_IN, k=SEQ, tm=D_MODEL, tn=640,
                    out=((N_CHIPS, D_MODEL, 640), _stacked(D_MODEL, 640, lambda i, j: (j, 0, 0))))
    token = yield ("grads_b", d_win)
    dh1 = _matmul_nt_shards(dproj, win_st, "d_h1", m=SEQ, n=D_MODEL, tm=1024, tn=D_MODEL,
                            a_spec=pl.BlockSpec((1024, D_IN), lambda i, j: (i, 0)),
                            a_piece=lambda ref, s: ref[:, s * 640:(s + 1) * 640])
    grad_x, d_g1 = _first_bwd(x, dx2, dh1, tied(sm["norm_mix_pre"], token))

    grads = {
        "norm_mix_pre": d_g1, "norm_mix_post": d_g2, "norm_ffn_pre": d_g3, "norm_ffn_post": d_g4,
        "w_in": d_win, "rel_bias": d_relb, "sinks": dsink[:, 0].reshape(1, N_Q_HEADS),
        "rwkv_shift_mix": d_mix, "w0": d_w0, "w_decay_up": d_wdu[:LORA_DECAY], "a0": d_a0,
        "w_iclr_up": d_wiu[LORA_DECAY:], "w_gate_up": d_wgu, "k_k": d_kk, "k_a": d_ka,
        "r_k": d_rk.reshape(1, N_Q_HEADS, HEAD_DIM), "ln_x_g": d_lng, "ln_x_b": d_lnb,
        "w_out": d_wout, "w_ffn_up": d_wup, "conv_w": d_convw, "conv_b": d_convb, "w_ffn_down": d_wdown,
    }
    return loss, grad_x, grads


def _place():
    x, y, c = lax.axis_index("x"), lax.axis_index("y"), lax.axis_index("c")
    chips = [(1 - x, y), (x, 1 - y), (1 - x, 1 - y)]
    return x, y, c, chips


def _remote(src, dst, sems, idx, to):
    return pltpu.make_async_remote_copy(src_ref=src, dst_ref=dst, send_sem=sems[0].at[idx], recv_sem=sems[1].at[idx],
                                        device_id=to, device_id_type=MESH)


def _half(c, rows):
    return pl.ds(pl.multiple_of(c * (rows // 2), 16), rows // 2)


def _gather_weights(big, small):
    nb, ns = len(big), len(small)

    def body(*refs):
        ins, outs = refs[:nb + ns], refs[nb + ns:2 * (nb + ns)]
        ici, d2d, sml, loc = refs[2 * (nb + ns):2 * (nb + ns) + 2], refs[-5:-3], refs[-3:-1], refs[-1]
        x, y, c, chips = _place()
        me = 2 * x + y
        sib = (x, y, 1 - c)
        local = [pltpu.make_async_copy(ins[a], outs[a].at[me], loc.at[a]) for a in range(nb + ns)]
        for cp in local:
            cp.start()
        sends = []
        for a in range(nb):
            rows = _half(c, big[a].shape[0])
            for kk, chip in enumerate(chips):
                sends.append(_remote(ins[a].at[rows], outs[a].at[me, rows], ici, a * 3 + kk, (*chip, c)))
        for a in range(ns):
            for kk, chip in enumerate(chips):
                sends.append(_remote(ins[nb + a], outs[nb + a].at[me], sml, a * 3 + kk, (*chip, c)))
        for cp in sends:
            cp.start()
        passed = []
        for a in range(nb):
            rows = _half(c, big[a].shape[0])
            for kk, (px, py) in enumerate(chips):
                got = outs[a].at[2 * px + py, rows]
                _remote(got, got, ici, a * 3 + kk, sib).wait_recv()
                fwd = _remote(got, got, d2d, a * 3 + kk, sib)
                fwd.start()
                passed.append(fwd)
        for a in range(nb):
            other = _half(1 - c, big[a].shape[0])
            for kk, (px, py) in enumerate(chips):
                land = outs[a].at[2 * px + py, other]
                _remote(land, land, d2d, a * 3 + kk, sib).wait_recv()
        for a in range(ns):
            for kk, (px, py) in enumerate(chips):
                land = outs[nb + a].at[2 * px + py]
                _remote(land, land, sml, a * 3 + kk, sib).wait_recv()
        for cp in sends + passed:
            cp.wait_send()
        for cp in local:
            cp.wait()

    arrs = list(big) + list(small)
    in_vmem = pl.BlockSpec(memory_space=pltpu.VMEM)
    return pl.pallas_call(
        body, name="gather_weights",
        in_specs=[in_vmem] * len(arrs), out_specs=[in_vmem] * len(arrs),
        out_shape=[jax.ShapeDtypeStruct((N_CHIPS,) + t.shape, t.dtype) for t in arrs],
        scratch_shapes=[pltpu.SemaphoreType.DMA((3 * nb,)), pltpu.SemaphoreType.DMA((3 * nb,)),
                        pltpu.SemaphoreType.DMA((3 * nb,)), pltpu.SemaphoreType.DMA((3 * nb,)),
                        pltpu.SemaphoreType.DMA((3 * ns,)), pltpu.SemaphoreType.DMA((3 * ns,)),
                        pltpu.SemaphoreType.DMA((nb + ns,))],
        compiler_params=pltpu.CompilerParams(has_side_effects=True, vmem_limit_bytes=VMEM_LIMIT),
    )(*arrs)


HBM = pl.BlockSpec(memory_space=pltpu.HBM)
SEM = pl.BlockSpec(memory_space=pltpu.SEMAPHORE)
EFFECT = pltpu.SideEffectType.DATAFLOW_SIDE_EFFECTING


def _copies_start(name, bufs, plan, n):
    nb = len(bufs)

    def body(*refs):
        ins, sems, token = refs[:nb], refs[nb:nb + 2 * n], refs[-1]
        for kk, (src, dst, dev) in enumerate(plan(ins)):
            pltpu.make_async_remote_copy(src_ref=src, dst_ref=dst, send_sem=sems[2 * kk], recv_sem=sems[2 * kk + 1],
                                         device_id=dev, device_id_type=MESH).start()
        token[...] = jnp.zeros_like(token)

    outs = pl.pallas_call(
        body, name=name,
        out_shape=tuple([pltpu.SemaphoreType.DMA(())] * (2 * n) + [pltpu.HBM(t.shape, t.dtype) for t in bufs]
                        + [jax.ShapeDtypeStruct((8, LANES), F32)]),
        in_specs=[HBM] * nb,
        out_specs=tuple([SEM] * (2 * n) + [HBM] * nb + [pl.BlockSpec(memory_space=pltpu.VMEM)]),
        input_output_aliases={t: 2 * n + t for t in range(nb)},
        compiler_params=pltpu.CompilerParams(has_side_effects=EFFECT),
    )(*[pltpu.with_memory_space_constraint(t, pltpu.HBM) for t in bufs])
    return outs[:2 * n], outs[2 * n:2 * n + nb], outs[-1]


def _copies_wait(name, sems, bufs, plan, n, after):
    nb = len(bufs)
    after = list(after) if isinstance(after, (list, tuple)) else [after]

    def body(*refs):
        ins, sem_refs = refs[:nb], refs[nb:nb + 2 * n]
        for kk, (src, dst, dev) in enumerate(plan(ins)):
            cp = pltpu.make_async_remote_copy(src_ref=src, dst_ref=dst, send_sem=sem_refs[2 * kk],
                                              recv_sem=sem_refs[2 * kk + 1], device_id=dev, device_id_type=MESH)
            cp.wait_send()
            cp.wait_recv()

    return pl.pallas_call(
        body, name=name,
        out_shape=tuple(pltpu.HBM(t.shape, t.dtype) for t in bufs),
        in_specs=[HBM] * nb + [SEM] * (2 * n) + [ANY] * len(after),
        out_specs=tuple([HBM] * nb),
        input_output_aliases={t: t for t in range(nb)},
        compiler_params=pltpu.CompilerParams(has_side_effects=EFFECT),
    )(*bufs, *sems, *after)


def _plan_gather(n_w):
    def plan(refs):
        x, y, c, chips = _place()
        me = 2 * x + y
        return [(refs[a], refs[n_w + a].at[me], (*chip, c)) for a in range(n_w) for chip in chips]
    return plan


def _plan_pair_halves(n_g, rows):
    def plan(refs):
        x, y, c, _ = _place()
        return [(refs[a].at[:, _half(1 - c, rows[a])], refs[n_g + a], (x, y, 1 - c)) for a in range(n_g)]
    return plan


def _plan_chip_parts(n_g):
    def plan(refs):
        x, y, c, chips = _place()
        me = 2 * x + y
        return [(refs[a].at[2 * px + py], refs[n_g + a].at[me], (px, py, c))
                for a in range(n_g) for (px, py) in chips]
    return plan


def _plan_pair_fill(n_g, rows):
    def plan(refs):
        x, y, c, _ = _place()
        return [(refs[a].at[_half(c, rows[a])], refs[a].at[_half(c, rows[a])], (x, y, 1 - c)) for a in range(n_g)]
    return plan


def _pair_add(g, got, name):
    _, rows, cols = g.shape
    hr = rows // 2
    tr = min(hr, 256)
    nb = hr // tr

    def body(g_ref, got_ref, p_ref, own_ref):
        val = (g_ref[...] + got_ref[...]).astype(BF16)
        p_ref[...] = val

        @pl.when(pl.program_id(1) == 2 * lax.axis_index("x") + lax.axis_index("y"))
        def _():
            own_ref[...] = val

    def mine(i, s):
        return (2 * lax.axis_index("x") + lax.axis_index("y"), i, 0)

    return pl.pallas_call(
        body, name=name, grid=(nb, N_CHIPS),
        in_specs=[pl.BlockSpec((None, tr, cols), lambda i, s: (s, lax.axis_index("c") * nb + i, 0)),
                  pl.BlockSpec((None, tr, cols), lambda i, s: (s, i, 0))],
        out_specs=[pl.BlockSpec((None, tr, cols), lambda i, s: (s, i, 0)), pl.BlockSpec((None, tr, cols), mine)],
        out_shape=[jax.ShapeDtypeStruct((N_CHIPS, hr, cols), BF16)] * 2,
        compiler_params=_cp(("parallel", "arbitrary")),
    )(g, got)


def _chip_sum(parts, name):
    _, hr, cols = parts.shape
    tr = min(hr, 128)
    nb = hr // tr

    def body(t_ref, o_ref):
        part = [t_ref[s].astype(F32) for s in range(N_CHIPS)]
        o_ref[...] = ((part[0] + part[1]) + part[2]) + part[3]

    return pl.pallas_call(
        body, name=name, grid=(nb,),
        in_specs=[pl.BlockSpec((N_CHIPS, tr, cols), lambda i: (0, i, 0))],
        out_specs=pl.BlockSpec((tr, cols), lambda i: (lax.axis_index("c") * nb + i, 0)),
        out_shape=jax.ShapeDtypeStruct((2 * hr, cols), F32),
        compiler_params=_cp(("parallel",)),
    )(parts)


class _Reduction:
    def __init__(self, tag, rows):
        self.tag, self.n, self.rows = tag, len(rows), rows
        self.plans = (_plan_pair_halves(self.n, rows), _plan_chip_parts(self.n), _plan_pair_fill(self.n, rows))
        self.flight = None

    def _name(self, what):
        return f"grad_{self.tag}_{what}"

    def start(self, gs):
        gots = [lax.empty((N_CHIPS, t.shape[1] // 2, t.shape[2]), F32) for t in gs]
        self.flight = _copies_start(self._name("pair_start"), list(gs) + gots, self.plans[0], self.n)
        return self.flight[2]

    def after_pair(self, after):
        sems, bufs, _ = self.flight
        out = _copies_wait(self._name("pair_wait"), sems, bufs, self.plans[0], self.n, after)
        sums = [_pair_add(g, got, self._name(f"pair_add_{i}"))
                for i, (g, got) in enumerate(zip(out[:self.n], out[self.n:]))]
        self.flight = _copies_start(self._name("chip_start"), [p for p, _ in sums] + [own for _, own in sums],
                                    self.plans[1], 3 * self.n)
        return self.flight[2]

    def after_chips(self, after):
        sems, bufs, _ = self.flight
        out = _copies_wait(self._name("chip_wait"), sems, bufs, self.plans[1], 3 * self.n, after)
        fulls = [_chip_sum(t, self._name(f"chip_sum_{i}")) for i, t in enumerate(out[self.n:])]
        self.flight = _copies_start(self._name("fill_start"), fulls, self.plans[2], self.n)
        return self.flight[2]

    def finish(self, after):
        sems, bufs, _ = self.flight
        return _copies_wait(self._name("fill_wait"), sems, bufs, self.plans[2], self.n, after)


def _adamw_math(w, g, m, v):
    nm = ADAM_B1 * m + (1.0 - ADAM_B1) * g
    nv = ADAM_B2 * v + (1.0 - ADAM_B2) * (g * g)
    m_hat = nm / (1.0 - ADAM_B1 ** ADAM_STEP)
    v_hat = nv / (1.0 - ADAM_B2 ** ADAM_STEP)
    return -ADAM_LR * (m_hat / (jnp.sqrt(v_hat) + ADAM_EPS) + ADAM_WD * w), nm, nv


def _adamw(w, g, m, v, name, tr):
    r, cdim = w.shape

    def body(w_ref, g_ref, m_ref, v_ref, d_ref, nm_ref, nv_ref):
        d_ref[...], nm_ref[...], nv_ref[...] = _adamw_math(w_ref[...], g_ref[...], m_ref[...], v_ref[...])

    return pl.pallas_call(
        body, name=name, grid=(r // tr,), in_specs=[_rows(tr, cdim)] * 4, out_specs=[_rows(tr, cdim)] * 3,
        out_shape=[jax.ShapeDtypeStruct((r, cdim), F32)] * 3, compiler_params=_cp(("parallel",)),
    )(w, g, m, v)


def _adamw_small(w, parts, m, v):
    def body(w_ref, p_ref, m_ref, v_ref, d_ref, nm_ref, nv_ref, g_ref):
        g = p_ref[0]
        for dev in range(1, N_DEV):
            g = g + p_ref[dev]
        g_ref[...] = g
        d_ref[...], nm_ref[...], nv_ref[...] = _adamw_math(w_ref[...], g, m_ref[...], v_ref[...])

    return pl.pallas_call(
        body, name="adamw_small", grid=(1,),
        in_specs=[_const(w.shape), _const(parts.shape), _const(w.shape), _const(w.shape)],
        out_specs=[_const(w.shape)] * 4, out_shape=[jax.ShapeDtypeStruct(w.shape, F32)] * 4,
        compiler_params=_cp(("arbitrary",)),
    )(w, parts, m, v)


REPLICATED = (("norm_mix_pre", 1024), ("norm_mix_post", 1024), ("norm_ffn_pre", 1024), ("norm_ffn_post", 1024),
              ("rel_bias", 256), ("sinks", 8), ("rwkv_shift_mix", 1792), ("w0", 512), ("a0", 512), ("k_k", 512),
              ("k_a", 512), ("r_k", 512), ("ln_x_g", 512), ("ln_x_b", 512), ("conv_b", 8192))
SMALL_SHARDED = (("w_decay_up", LORA_DECAY, D_RWKV), ("w_iclr_up", LORA_ICLR, D_RWKV),
                 ("w_gate_up", LORA_GATE, D_RWKV), ("conv_w", 3, 2 * D_FF))
BIG = (("w_in", D_MODEL, 640), ("w_out", 256, D_MODEL), ("w_ffn_up", D_MODEL, 2048), ("w_ffn_down", 1024, D_MODEL))
PACK_ALIGN = 8 * LANES


def _pack(pieces):
    flat = []
    for t in pieces:
        t = t.reshape(-1)
        pad = (-t.shape[0]) % LANES
        flat.append(jnp.pad(t, (0, pad)) if pad else t)
    flat = jnp.concatenate(flat)
    pad = (-flat.shape[0]) % PACK_ALIGN
    return jnp.pad(flat, (0, pad)).reshape(-1, LANES)


def _unpack(buf, sizes):
    flat, out, off = buf.reshape(-1), [], 0
    for n in sizes:
        out.append(flat[off:off + n])
        off += n + ((-n) % LANES)
    return out


def kernel(x, norm_mix_pre, norm_mix_post, norm_ffn_pre, norm_ffn_post, w_in, rel_bias, sinks, rwkv_shift_mix, w0, w_decay_up, a0, w_iclr_up, w_gate_up, k_k, k_a, r_k, ln_x_g, ln_x_b, w_out, w_ffn_up, conv_w, conv_b, w_ffn_down, loss_target, m_norm_mix_pre, m_norm_mix_post, m_norm_ffn_pre, m_norm_ffn_post, m_w_in, m_rel_bias, m_sinks, m_rwkv_shift_mix, m_w0, m_w_decay_up, m_a0, m_w_iclr_up, m_w_gate_up, m_k_k, m_k_a, m_r_k, m_ln_x_g, m_ln_x_b, m_w_out, m_w_ffn_up, m_conv_w, m_conv_b, m_w_ffn_down, v_norm_mix_pre, v_norm_mix_post, v_norm_ffn_pre, v_norm_ffn_post, v_w_in, v_rel_bias, v_sinks, v_rwkv_shift_mix, v_w0, v_w_decay_up, v_a0, v_w_iclr_up, v_w_gate_up, v_k_k, v_k_a, v_r_k, v_ln_x_g, v_ln_x_b, v_w_out, v_w_ffn_up, v_conv_w, v_conv_b, v_w_ffn_down):
    given = dict(locals())
    names = [n for n, _ in REPLICATED] + [n for n, _, _ in SMALL_SHARDED] + [n for n, _, _ in BIG]
    order = ["norm_mix_pre", "norm_mix_post", "norm_ffn_pre", "norm_ffn_post", "w_in", "rel_bias", "sinks",
             "rwkv_shift_mix", "w0", "w_decay_up", "a0", "w_iclr_up", "w_gate_up", "k_k", "k_a", "r_k", "ln_x_g",
             "ln_x_b", "w_out", "w_ffn_up", "conv_w", "conv_b", "w_ffn_down"]
    assert sorted(names) == sorted(order)
    shard = 2 * lax.axis_index("x") + lax.axis_index("y")

    big_sh = {n: given[n].reshape(a, b).astype(BF16) for n, a, b in BIG}
    small_sh = [given[n].reshape(r, c // N_CHIPS) for n, r, c in SMALL_SHARDED]
    gathered = _gather_weights([big_sh["w_in"]], small_sh)
    rest = ("w_out", "w_ffn_up", "w_ffn_down")
    win_st, rest_sh = lax.optimization_barrier((gathered[0], [big_sh[n] for n in rest]))
    sm = {n: given[n] for n, _ in REPLICATED}
    sm["r_k"] = r_k.reshape(N_Q_HEADS, HEAD_DIM)
    for (n, r, c), st in zip(SMALL_SHARDED, gathered[1:]):
        sm[n] = st.transpose(1, 0, 2).reshape(r, c)

    lands = [lax.dynamic_update_slice(lax.empty((N_CHIPS,) + t.shape, BF16), t[None], (shard, 0, 0)) for t in rest_sh]
    plan_w = _plan_gather(len(rest))
    w_sems, w_bufs, token = _copies_start("gather_rest_start", rest_sh + lands, plan_w, 9)
    sm["norm_mix_pre"] = norm_mix_pre + token[0:1, 0:1]

    def on_rest_weights(after):
        out = _copies_wait("gather_rest_wait", w_sems, w_bufs, plan_w, 9, after)
        wout_st, wup_st, wdown_st = out[3:]
        return wout_st.reshape(D_MODEL, D_MODEL), wup_st, wdown_st.reshape(D_FF, D_MODEL)

    red_a = _Reduction("a", (1024, D_MODEL, 256))
    red_b = _Reduction("b", (D_MODEL,))

    def on_grads_a(gs):
        d_wdown, d_wup, d_wout = gs
        return red_a.start([d_wdown.reshape(N_CHIPS, 1024, D_MODEL), d_wup, d_wout.reshape(N_CHIPS, 256, D_MODEL)])

    handlers = {"rest_weights": on_rest_weights, "grads_a": on_grads_a, "seam_1": red_a.after_pair,
                "seam_2": red_a.after_chips, "grads_b": lambda g: red_b.start([g])}
    steps = _local_step(x[0], loss_target[0], sm, win_st)
    kind, payload = next(steps)
    while True:
        try:
            kind, payload = steps.send(handlers[kind](payload))
        except StopIteration as done:
            loss, grad_x, grads = done.value
            break

    small_names = [n for n, _ in REPLICATED] + [n for n, _, _ in SMALL_SHARDED]

    def shard_cols(t, s):
        return t[:, s * (t.shape[1] // N_CHIPS):(s + 1) * (t.shape[1] // N_CHIPS)]

    for_chip = jnp.stack([_pack([loss[0]] + [grads[n] for n, _ in REPLICATED]
                                + [shard_cols(grads[n], s) for n, _, _ in SMALL_SHARDED]) for s in range(N_CHIPS)])
    me = 2 * shard + lax.axis_index("c")
    mine = lax.dynamic_index_in_dim(for_chip, shard, 0, keepdims=True)
    land = lax.dynamic_update_slice(lax.empty((N_DEV,) + for_chip.shape[1:], F32), mine, (me, 0, 0))

    def plan_small(refs):
        x, y, c, _ = _place()
        out = []
        for rel in range(1, N_DEV):
            px, py, pc = x ^ (rel >> 2), y ^ ((rel >> 1) & 1), c ^ (rel & 1)
            out.append((refs[0].at[2 * px + py], refs[1].at[4 * x + 2 * y + c], (px, py, pc)))
        return out

    s_sems, s_bufs, _ = _copies_start("grad_small_start", [for_chip, land], plan_small, N_DEV - 1)

    red_b.after_pair(grad_x)
    g_out = {}
    g_out["w_ffn_down"], g_out["w_ffn_up"], g_out["w_out"] = red_a.finish(grad_x)

    delta, new_m, new_v = {}, {}, {}

    def update(n, a, b):
        delta[n], new_m[n], new_v[n] = _adamw(given[n].reshape(a, b), g_out[n], given["m_" + n].reshape(a, b),
                                              given["v_" + n].reshape(a, b), "adamw_" + n, 128)

    for n, a, b in BIG[1:]:
        update(n, a, b)
    done = [delta[n] for n, _, _ in BIG[1:]]
    red_b.after_chips(done)
    parts = _copies_wait("grad_small_wait", s_sems, s_bufs, plan_small, N_DEV - 1, done)[1]
    no_param = jnp.zeros((LANES,), F32)
    packs = [_pack([no_param] + [given[pre + n] for n in small_names]) for pre in ("", "m_", "v_")]
    small_sizes = [LANES] + [int(np.prod(given[n].shape)) for n in small_names]
    upd = [_unpack(t, small_sizes) for t in _adamw_small(packs[0], parts, packs[1], packs[2])]
    loss = upd[3][0][0]
    for n, d, nm, nv, g in zip(small_names, *[u[1:] for u in upd]):
        shape = given[n].shape
        delta[n], new_m[n], new_v[n], g_out[n] = (t.reshape(shape) for t in (d, nm, nv, g))
    g_out["w_in"], = red_b.finish(upd[0][0])
    update(*BIG[0])

    def shaped(d):
        return [d[n].reshape(given[n].shape) for n in order]

    return (loss, grad_x.reshape(x.shape), *shaped(g_out), *shaped(delta), *shaped(new_m), *shaped(new_v))
```

```python
import math

import numpy as np
import jax
import jax.numpy as jnp
from jax import lax
from jax.experimental import pallas as pl
from jax.experimental.pallas import tpu as pltpu

F32 = jnp.float32
BF16 = jnp.bfloat16
MESH = pl.DeviceIdType.MESH

SEQ = 2048
D_MODEL = 1024
HEAD_DIM = 64
D_ATTN = 512
D_RWKV = 512
D_KV = 128
N_Q_HEADS = 8
N_KV_HEADS = 2
Q_PER_KV = 4
BLOCK = 128
N_BUCKETS = 32
MAX_DISTANCE = 128
LORA_DECAY = 64
LORA_ICLR = 64
LORA_GATE = 128
RWKV_COLS = 3 * D_RWKV + LORA_DECAY + LORA_ICLR + LORA_GATE
P_OFF = D_ATTN + 2 * D_KV
D_IN = P_OFF + RWKV_COLS
D_FF = 4096
NORM_EPS = 1e-6
GN_EPS = 64e-5
NEG_INF = -1e30
N_CHIPS = 4
N_DEV = 8
HEAD_SHIFT = HEAD_DIM.bit_length() - 1
BLOCK_SHIFT = BLOCK.bit_length() - 1

ADAM_LR = 0.001
ADAM_B1 = 0.9
ADAM_B2 = 0.999
ADAM_EPS = 1e-08
ADAM_WD = 0.01
ADAM_STEP = 10

VMEM_LIMIT = 52 * 1024 * 1024
LANES = 128


def _cp(sem=None, vmem=VMEM_LIMIT):
    kw = dict(vmem_limit_bytes=vmem)
    if sem is not None:
        kw["dimension_semantics"] = sem
    return pltpu.CompilerParams(**kw)


def _rows(tr, nc):
    return pl.BlockSpec((tr, nc), lambda i: (i, 0))


def _const(shape):
    return pl.BlockSpec(shape, lambda *_: (0,) * len(shape))


ANY = pl.BlockSpec(memory_space=pl.ANY)


def _split(x, n):
    parts = []
    for _ in range(n - 1):
        h = x.astype(BF16)
        parts.append(h)
        x = x - h.astype(F32)
    parts.append(x.astype(BF16))
    return parts


NN = (((1,), (0,)), ((), ()))
NT = (((1,), (1,)), ((), ()))
TN = (((0,), (0,)), ((), ()))


def _dot(a, b, dn=NN):
    return lax.dot_general(a, b, dn, preferred_element_type=F32)


def _dot_ind(x, ind_bf16, n=3):
    acc = None
    for part in _split(x, n):
        t = _dot(part, ind_bf16)
        acc = t if acc is None else acc + t
    return acc


def _head_ones(n):
    r = lax.broadcasted_iota(jnp.int32, (n, n), 0) >> HEAD_SHIFT
    c = lax.broadcasted_iota(jnp.int32, (n, n), 1) >> HEAD_SHIFT
    return jnp.where(r == c, 1.0, 0.0).astype(BF16)


def _matmul(a, b, mode, name, *, m, n, k, tm, tn, a_spec=None, b_spec=None, out=None):
    dn = {"nn": NN, "nt": NT, "tn": TN}[mode]

    def body(a_ref, b_ref, o_ref):
        o_ref[...] = _dot(a_ref[...], b_ref[...], dn)

    if a_spec is None:
        a_spec = pl.BlockSpec((k, tm), lambda i, j: (0, i)) if mode == "tn" else pl.BlockSpec((tm, k), lambda i, j: (i, 0))
    if b_spec is None:
        b_spec = pl.BlockSpec((tn, k), lambda i, j: (j, 0)) if mode == "nt" else pl.BlockSpec((k, tn), lambda i, j: (0, j))
    return pl.pallas_call(
        body, name=name, grid=(m // tm, n // tn),
        in_specs=[a_spec, b_spec],
        out_specs=pl.BlockSpec((tm, tn), lambda i, j: (i, j)) if out is None else out[1],
        out_shape=jax.ShapeDtypeStruct((m, n) if out is None else out[0], F32),
        compiler_params=_cp(("parallel", "parallel")),
    )(a, b)


def _matmul_nt_shards(a, b_st, name, *, m, n, tm, tn, a_spec, a_piece):
    ks = b_st.shape[2]

    def body(a_ref, b_ref, o_ref):
        acc = _dot(a_piece(a_ref, 0), b_ref[0], NT)
        for s in range(1, N_CHIPS):
            acc = acc + _dot(a_piece(a_ref, s), b_ref[s], NT)
        o_ref[...] = acc

    return pl.pallas_call(
        body, name=name, grid=(m // tm, n // tn),
        in_specs=[a_spec, pl.BlockSpec((N_CHIPS, tn, ks), lambda i, j: (0, j, 0))],
        out_specs=pl.BlockSpec((tm, tn), lambda i, j: (i, j)),
        out_shape=jax.ShapeDtypeStruct((m, n), F32),
        compiler_params=_cp(("parallel", "parallel")),
    )(a, b_st)


def _rstd(x):
    return lax.rsqrt(jnp.mean(x * x, axis=-1, keepdims=True) + NORM_EPS)


def _rms_bwd(x, r, g, dy):
    gy = dy * g
    return r * gy - x * ((r * r * r) * (jnp.sum(x * gy, axis=-1, keepdims=True) / x.shape[-1]))


TR = 256


def _norm_cast(x, g, name):
    def body(x_ref, g_ref, h_ref):
        x = x_ref[...]
        h_ref[...] = (x * _rstd(x) * g_ref[...]).astype(BF16)

    return pl.pallas_call(
        body, name=name, grid=(SEQ // TR,),
        in_specs=[_rows(TR, D_MODEL), _const((1, D_MODEL))],
        out_specs=_rows(TR, D_MODEL),
        out_shape=jax.ShapeDtypeStruct((SEQ, D_MODEL), BF16),
        compiler_params=_cp(("parallel",)),
    )(x, g)


def _mix_norm(x, mix, g2, g3):
    def body(x_ref, mix_ref, g2_ref, g3_ref, x2_ref, h3_ref):
        mixv = mix_ref[...]
        x2 = x_ref[...] + mixv * _rstd(mixv) * g2_ref[...]
        x2_ref[...] = x2
        h3_ref[...] = (x2 * _rstd(x2) * g3_ref[...]).astype(BF16)

    return pl.pallas_call(
        body, name="mix_norm", grid=(SEQ // TR,),
        in_specs=[_rows(TR, D_MODEL), _rows(TR, D_MODEL), _const((1, D_MODEL)), _const((1, D_MODEL))],
        out_specs=[_rows(TR, D_MODEL), _rows(TR, D_MODEL)],
        out_shape=[jax.ShapeDtypeStruct((SEQ, D_MODEL), F32), jax.ShapeDtypeStruct((SEQ, D_MODEL), BF16)],
        compiler_params=_cp(("parallel",)),
    )(x, mix, g2, g3)


def _loss_head(x2, f, g4, target):
    def body(x2_ref, f_ref, g4_ref, t_ref, loss_ref, dy_ref, df_ref, dg_ref):
        i = pl.program_id(0)
        f = f_ref[...]
        g4 = g4_ref[...]
        r = _rstd(f)
        e = x2_ref[...] + f * r * g4 - t_ref[...]
        dy = e * (1.0 / D_MODEL)
        dy_ref[...] = dy
        df_ref[...] = _rms_bwd(f, r, g4, dy).astype(BF16)
        part = 0.5 * jnp.sum(jnp.sum(e * e, axis=-1, keepdims=True), axis=0, keepdims=True) * (1.0 / D_MODEL)
        dg = jnp.sum(dy * f * r, axis=0, keepdims=True)

        @pl.when(i == 0)
        def _():
            loss_ref[...] = jnp.zeros_like(loss_ref)
            dg_ref[...] = jnp.zeros_like(dg_ref)

        loss_ref[...] += jnp.broadcast_to(part, loss_ref.shape)
        dg_ref[...] += dg

    return pl.pallas_call(
        body, name="loss_head", grid=(SEQ // TR,),
        in_specs=[_rows(TR, D_MODEL), _rows(TR, D_MODEL), _const((1, D_MODEL)), _rows(TR, D_MODEL)],
        out_specs=[_const((8, LANES)), _rows(TR, D_MODEL), _rows(TR, D_MODEL), _const((1, D_MODEL))],
        out_shape=[jax.ShapeDtypeStruct((8, LANES), F32), jax.ShapeDtypeStruct((SEQ, D_MODEL), F32),
                   jax.ShapeDtypeStruct((SEQ, D_MODEL), BF16), jax.ShapeDtypeStruct((1, D_MODEL), F32)],
        compiler_params=_cp(("arbitrary",)),
    )(x2, f, g4, target)


def _mid_bwd(x2, mix, dy, dh3, g2, g3):
    def body(x2_ref, mix_ref, dy_ref, dh3_ref, g2_ref, g3_ref, dx2_ref, dmix_ref, dg2_ref, dg3_ref):
        i = pl.program_id(0)
        x2 = x2_ref[...]
        mixv = mix_ref[...]
        dh3 = dh3_ref[...]
        r3 = _rstd(x2)
        dx2 = dy_ref[...] + _rms_bwd(x2, r3, g3_ref[...], dh3)
        dx2_ref[...] = dx2
        r2 = _rstd(mixv)
        dmix_ref[...] = _rms_bwd(mixv, r2, g2_ref[...], dx2).astype(BF16)

        @pl.when(i == 0)
        def _():
            dg2_ref[...] = jnp.zeros_like(dg2_ref)
            dg3_ref[...] = jnp.zeros_like(dg3_ref)

        dg3_ref[...] += jnp.sum(dh3 * x2 * r3, axis=0, keepdims=True)
        dg2_ref[...] += jnp.sum(dx2 * mixv * r2, axis=0, keepdims=True)

    return pl.pallas_call(
        body, name="mid_bwd", grid=(SEQ // TR,),
        in_specs=[_rows(TR, D_MODEL)] * 4 + [_const((1, D_MODEL))] * 2,
        out_specs=[_rows(TR, D_MODEL), _rows(TR, D_MODEL), _const((1, D_MODEL)), _const((1, D_MODEL))],
        out_shape=[jax.ShapeDtypeStruct((SEQ, D_MODEL), F32), jax.ShapeDtypeStruct((SEQ, D_MODEL), BF16),
                   jax.ShapeDtypeStruct((1, D_MODEL), F32), jax.ShapeDtypeStruct((1, D_MODEL), F32)],
        compiler_params=_cp(("arbitrary",)),
    )(x2, mix, dy, dh3, g2, g3)


def _first_bwd(x, dx2, dh1, g1):
    def body(x_ref, dx2_ref, dh1_ref, g1_ref, dx_ref, dg1_ref):
        i = pl.program_id(0)
        x = x_ref[...]
        dh1 = dh1_ref[...]
        r = _rstd(x)
        dx_ref[...] = dx2_ref[...] + _rms_bwd(x, r, g1_ref[...], dh1)

        @pl.when(i == 0)
        def _():
            dg1_ref[...] = jnp.zeros_like(dg1_ref)

        dg1_ref[...] += jnp.sum(dh1 * x * r, axis=0, keepdims=True)

    return pl.pallas_call(
        body, name="first_bwd", grid=(SEQ // TR,),
        in_specs=[_rows(TR, D_MODEL)] * 3 + [_const((1, D_MODEL))],
        out_specs=[_rows(TR, D_MODEL), _const((1, D_MODEL))],
        out_shape=[jax.ShapeDtypeStruct((SEQ, D_MODEL), F32), jax.ShapeDtypeStruct((1, D_MODEL), F32)],
        compiler_params=_cp(("arbitrary",)),
    )(x, dx2, dh1, g1)


TC = 256
N_CB = D_FF // TC
GELU_C = math.sqrt(2.0 / math.pi)


def _shift_down(u, s):
    rolled = pltpu.roll(u, s, 0)
    row = lax.broadcasted_iota(jnp.int32, u.shape, 0)
    return jnp.where(row >= s, rolled, 0.0)


def _shift_up(u, s):
    n = u.shape[0]
    rolled = pltpu.roll(u, n - s, 0)
    row = lax.broadcasted_iota(jnp.int32, u.shape, 0)
    return jnp.where(row < n - s, rolled, 0.0)


def _conv3(u, w, b):
    return b + w[0:1] * _shift_down(u, 2) + w[1:2] * _shift_down(u, 1) + w[2:3] * u


def _gelu_and_grad(x):
    inner = GELU_C * (x + 0.044715 * (x * x * x))
    t = jnp.tanh(inner)
    gelu = 0.5 * x * (1.0 + t)
    dgelu = 0.5 * (1.0 + t) + 0.5 * x * (1.0 - t * t) * (GELU_C * (1.0 + 3 * 0.044715 * (x * x)))
    return gelu, dgelu


def _ffn_specs():
    col = lambda off: pl.BlockSpec((SEQ, TC), lambda *g: (0, g[-1] + off))
    w = lambda off: pl.BlockSpec((3, TC), lambda *g: (0, g[-1] + off))
    b = lambda off: pl.BlockSpec((1, TC), lambda *g: (0, g[-1] + off))
    return col, w, b


def _ffn_up_act(h3, wup_st, conv_w, conv_b):
    col, w, b = _ffn_specs()
    per_shard = wup_st.shape[2] // TC

    def body(h_ref, upg_ref, upv_ref, wg_ref, wv_ref, bg_ref, bv_ref, ug_ref, uv_ref, act_ref):
        h = h_ref[...]
        ug = _dot(h, upg_ref[...])
        uv = _dot(h, upv_ref[...])
        ug_ref[...] = ug
        uv_ref[...] = uv
        gate = _conv3(ug, wg_ref[...], bg_ref[...])
        val = _conv3(uv, wv_ref[...], bv_ref[...])
        act_ref[...] = (_gelu_and_grad(gate)[0] * val).astype(BF16)

    return pl.pallas_call(
        body, name="ffn_up_act", grid=(N_CB,),
        in_specs=[_const((SEQ, D_MODEL)),
                  pl.BlockSpec((None, D_MODEL, TC), lambda j: (j // per_shard, 0, j % per_shard)),
                  pl.BlockSpec((None, D_MODEL, TC), lambda j: (2 + j // per_shard, 0, j % per_shard)),
                  w(0), w(N_CB), b(0), b(N_CB)],
        out_specs=[col(0)] * 3,
        out_shape=[jax.ShapeDtypeStruct((SEQ, D_FF), F32)] * 2 + [jax.ShapeDtypeStruct((SEQ, D_FF), BF16)],
        compiler_params=_cp(("parallel",)),
    )(h3, wup_st, wup_st, conv_w, conv_w, conv_b, conv_b)


def _ffn_act_bwd(u_gate, u_val, df, wdown, conv_w, conv_b):
    col, w, b = _ffn_specs()
    both = lambda rows: pl.BlockSpec((2, rows, TC), lambda j: (0, 0, j))

    def body(ug_ref, uv_ref, df_ref, wd_ref, wg_ref, wv_ref, bg_ref, bv_ref, du_ref, dw_ref, db_ref):
        da = _dot(df_ref[...], wd_ref[...], NT)
        ug, uv = ug_ref[...], uv_ref[...]
        wg, wv = wg_ref[...], wv_ref[...]
        gate = _conv3(ug, wg, bg_ref[...])
        val = _conv3(uv, wv, bv_ref[...])
        gelu, dgelu = _gelu_and_grad(gate)
        for h, (duc, uh, wh) in enumerate(((da * val * dgelu, ug, wg), (da * gelu, uv, wv))):
            up1, up2 = _shift_up(duc, 1), _shift_up(duc, 2)
            du_ref[h] = (wh[2:3] * duc + wh[1:2] * up1 + wh[0:1] * up2).astype(BF16)
            db_ref[h] = jnp.sum(duc, axis=0, keepdims=True)
            dw_ref[h] = jnp.concatenate(
                [jnp.sum(up2 * uh, axis=0, keepdims=True), jnp.sum(up1 * uh, axis=0, keepdims=True),
                 jnp.sum(duc * uh, axis=0, keepdims=True)], axis=0)

    return pl.pallas_call(
        body, name="ffn_act_bwd", grid=(N_CB,),
        in_specs=[col(0), col(0), _const((SEQ, D_MODEL)), pl.BlockSpec((TC, D_MODEL), lambda j: (j, 0)),
                  w(0), w(N_CB), b(0), b(N_CB)],
        out_specs=[both(SEQ), both(3), both(1)],
        out_shape=[jax.ShapeDtypeStruct((2, SEQ, D_FF), BF16), jax.ShapeDtypeStruct((2, 3, D_FF), F32),
                   jax.ShapeDtypeStruct((2, 1, D_FF), F32)],
        compiler_params=_cp(("parallel",)),
    )(u_gate, u_val, df, wdown, conv_w, conv_w, conv_b, conv_b)


def _t5_onehot():
    rel = (np.arange(BLOCK)[:, None] + BLOCK) - np.arange(2 * BLOCK)[None, :]
    n = np.maximum(rel, 0)
    max_exact = N_BUCKETS // 2
    large = max_exact + (np.log(np.maximum(n, 1).astype(np.float32) / np.float32(max_exact))
                         / np.float32(math.log(MAX_DISTANCE / max_exact))
                         * np.float32(N_BUCKETS - max_exact)).astype(np.int32)
    large = np.minimum(large, N_BUCKETS - 1)
    bucket = np.where(n < max_exact, n, large).reshape(-1)
    return (bucket[None, :] == np.arange(N_BUCKETS)[:, None]).astype(np.float32)


N_REL = BLOCK * 2 * BLOCK


def _bias_table(rel_bias_t, onehot):
    def body(rb_ref, oh_ref, o_ref):
        o_ref[...] = _dot_ind(rb_ref[...], oh_ref[...])

    return pl.pallas_call(
        body, name="bias_table", grid=(1,),
        in_specs=[_const((N_Q_HEADS, N_BUCKETS)), _const((N_BUCKETS, N_REL))],
        out_specs=_const((N_Q_HEADS, N_REL)),
        out_shape=jax.ShapeDtypeStruct((N_Q_HEADS, N_REL), F32),
        compiler_params=_cp(("arbitrary",)),
    )(rel_bias_t, onehot)


def _bias_table_bwd(dbias, onehot):
    def body(db_ref, oh_ref, o_ref):
        acc = None
        for part in _split(db_ref[...], 3):
            t = _dot(part, oh_ref[...], NT)
            acc = t if acc is None else acc + t
        o_ref[...] = acc

    return pl.pallas_call(
        body, name="bias_table_bwd", grid=(1,),
        in_specs=[_const((N_Q_HEADS, N_REL)), _const((N_BUCKETS, N_REL))],
        out_specs=_const((N_Q_HEADS, N_BUCKETS)),
        out_shape=jax.ShapeDtypeStruct((N_Q_HEADS, N_BUCKETS), F32),
        compiler_params=_cp(("arbitrary",)),
    )(dbias, onehot)


def _attn_pieces(n, q, kvp, kvc, bias_ref, sinks_ref, hk):
    qi = lax.broadcasted_iota(jnp.int32, (BLOCK, 2 * BLOCK), 0)
    kj = lax.broadcasted_iota(jnp.int32, (BLOCK, 2 * BLOCK), 1)
    rel = qi + BLOCK - kj
    first_key = jnp.where(n > 0, 0, BLOCK)
    ok = jnp.where(rel >= 0, jnp.where(rel < BLOCK, jnp.where(kj >= first_key, 1.0, 0.0), 0.0), 0.0)
    ok4 = jnp.concatenate([ok] * Q_PER_KV, axis=0) > 0.5
    c0 = hk * HEAD_DIM
    kcat = jnp.concatenate([kvp[:, c0:c0 + HEAD_DIM], kvc[:, c0:c0 + HEAD_DIM]], axis=0).astype(BF16)
    vcat = jnp.concatenate([kvp[:, D_KV + c0:D_KV + c0 + HEAD_DIM], kvc[:, D_KV + c0:D_KV + c0 + HEAD_DIM]],
                           axis=0).astype(BF16)
    q0 = hk * Q_PER_KV * HEAD_DIM
    qs = jnp.concatenate([q[:, q0 + g * HEAD_DIM:q0 + (g + 1) * HEAD_DIM] for g in range(Q_PER_KV)],
                         axis=0).astype(BF16)
    s = _dot(qs, kcat, NT) * (HEAD_DIM ** -0.5) + bias_ref[hk]
    s = jnp.where(ok4, s, NEG_INF)
    row = lax.broadcasted_iota(jnp.int32, (Q_PER_KV * BLOCK, 1), 0)
    sink = jnp.zeros((Q_PER_KV * BLOCK, 1), F32)
    for g in range(Q_PER_KV):
        sink = jnp.where((row >> BLOCK_SHIFT) == g, sinks_ref[hk * Q_PER_KV + g], sink)
    m = jnp.maximum(jnp.max(s, axis=-1, keepdims=True), sink)
    p = jnp.exp(s - m)
    es = jnp.exp(sink - m)
    inv = 1.0 / (jnp.sum(p, axis=-1, keepdims=True) + es)
    return qs, kcat, vcat, p * inv, es * inv


def _attn_in_specs():
    return [pl.BlockSpec((BLOCK, D_ATTN), lambda n: (n, 0)),
            pl.BlockSpec((BLOCK, 2 * D_KV), lambda n: (jnp.maximum(n - 1, 0), D_ATTN // (2 * D_KV))),
            pl.BlockSpec((BLOCK, 2 * D_KV), lambda n: (n, D_ATTN // (2 * D_KV))),
            _const((N_KV_HEADS, Q_PER_KV * BLOCK, 2 * BLOCK)),
            pl.BlockSpec(memory_space=pltpu.SMEM)]


def _unstack_heads(t):
    return jnp.concatenate([t[g * BLOCK:(g + 1) * BLOCK] for g in range(Q_PER_KV)], axis=1)


def _attn_fwd(proj, bias, sinks):
    def body(q_ref, kvp_ref, kvc_ref, bias_ref, sinks_ref, o_ref):
        n = pl.program_id(0)
        q, kvp, kvc = q_ref[...], kvp_ref[...], kvc_ref[...]
        outs = []
        for hk in range(N_KV_HEADS):
            _, _, vcat, probs, _ = _attn_pieces(n, q, kvp, kvc, bias_ref, sinks_ref, hk)
            outs.append(_unstack_heads(_dot(probs.astype(BF16), vcat)))
        o_ref[...] = jnp.concatenate(outs, axis=1)

    return pl.pallas_call(
        body, name="attn_fwd", grid=(SEQ // BLOCK,),
        in_specs=_attn_in_specs(),
        out_specs=pl.BlockSpec((BLOCK, D_ATTN), lambda n: (n, 0)),
        out_shape=jax.ShapeDtypeStruct((SEQ, D_ATTN), F32),
        compiler_params=_cp(("parallel",)),
    )(proj, proj, proj, bias, sinks)


def _attn_bwd(proj, bias, sinks, dcat):
    nb = SEQ // BLOCK

    def body(q_ref, kvp_ref, kvc_ref, bias_ref, sinks_ref, do_ref, dq_ref, dkv_ref, dbias_ref, dsink_ref, dsacc):
        n = pl.program_id(0)

        @pl.when(n == 0)
        def _():
            dkv_ref[...] = jnp.zeros_like(dkv_ref)
            dbias_ref[...] = jnp.zeros_like(dbias_ref)
            dsacc[...] = jnp.zeros_like(dsacc)

        q, kvp, kvc = q_ref[...], kvp_ref[...], kvc_ref[...]
        do_all = do_ref[...]
        dqs, dks, dvs = [], [], []
        for hk in range(N_KV_HEADS):
            qs, kcat, vcat, probs, psink = _attn_pieces(n, q, kvp, kvc, bias_ref, sinks_ref, hk)
            q0 = hk * Q_PER_KV * HEAD_DIM
            do = jnp.concatenate([do_all[:, q0 + g * HEAD_DIM:q0 + (g + 1) * HEAD_DIM] for g in range(Q_PER_KV)],
                                 axis=0).astype(BF16)
            dprobs = _dot(do, vcat, NT)
            dvs.append(_dot(probs.astype(BF16), do, TN))
            rowdot = jnp.sum(probs * dprobs, axis=-1, keepdims=True)
            ds = probs * (dprobs - rowdot)
            dsacc[hk] += -psink * rowdot
            dbias_ref[hk] += ds
            dsb = (ds * (HEAD_DIM ** -0.5)).astype(BF16)
            dqs.append(_unstack_heads(_dot(dsb, kcat)))
            dks.append(_dot(dsb, qs, TN))
        dq_ref[...] = jnp.concatenate(dqs, axis=1)
        upd = jnp.concatenate(dks + dvs, axis=1)
        cur = pl.multiple_of(n * BLOCK, BLOCK)
        dkv_ref[pl.ds(cur, BLOCK), :] += upd[BLOCK:]

        @pl.when(n > 0)
        def _():
            prev = pl.multiple_of((n - 1) * BLOCK, BLOCK)
            dkv_ref[pl.ds(prev, BLOCK), :] += upd[:BLOCK]

        @pl.when(n == nb - 1)
        def _():
            for hk in range(N_KV_HEADS):
                for g in range(Q_PER_KV):
                    tot = jnp.sum(dsacc[hk, g * BLOCK:(g + 1) * BLOCK, :], axis=0, keepdims=True)
                    h = hk * Q_PER_KV + g
                    dsink_ref[h:h + 1, :] = jnp.broadcast_to(tot, (1, LANES))

    return pl.pallas_call(
        body, name="attn_bwd", grid=(nb,),
        in_specs=_attn_in_specs() + [pl.BlockSpec((BLOCK, D_ATTN), lambda n: (n, 0))],
        out_specs=[pl.BlockSpec((BLOCK, D_ATTN), lambda n: (n, 0)), _const((SEQ, 2 * D_KV)),
                   _const((N_KV_HEADS, Q_PER_KV * BLOCK, 2 * BLOCK)), _const((N_Q_HEADS, LANES))],
        out_shape=[jax.ShapeDtypeStruct((SEQ, D_ATTN), F32), jax.ShapeDtypeStruct((SEQ, 2 * D_KV), F32),
                   jax.ShapeDtypeStruct((N_KV_HEADS, Q_PER_KV * BLOCK, 2 * BLOCK), F32),
                   jax.ShapeDtypeStruct((N_Q_HEADS, LANES), F32)],
        scratch_shapes=[pltpu.VMEM((N_KV_HEADS, Q_PER_KV * BLOCK, 1), F32)],
        compiler_params=_cp(("arbitrary",)),
    )(proj, proj, proj, bias, sinks, dcat)


@jax.custom_vjp
def _head_sum(x):
    ones = _head_ones(LANES)
    return jnp.concatenate([_dot_ind(x[:, c:c + LANES], ones, 2) for c in range(0, x.shape[-1], LANES)], axis=1)


_head_sum.defvjp(lambda x: (_head_sum(x), None), lambda _, ct: (_head_sum(ct),))


@jax.custom_vjp
def _bdot(a, w):
    return _dot(a.astype(BF16), w.astype(BF16))


def _bdot_bwd(res, ct):
    a, w = res
    ctb = ct.astype(BF16)
    return _dot(ctb, w.astype(BF16), NT), _dot(a.astype(BF16), ctb, TN)


_bdot.defvjp(lambda a, w: (_bdot(a, w), (a, w)), _bdot_bwd)


def _sigmoid(x):
    return 0.5 * (jnp.tanh(0.5 * x) + 1.0)


def _softplus(x):
    return jnp.maximum(x, 0.0) + jnp.log(1.0 + jnp.exp(-jnp.abs(x)))


def _rwkv_core(r, k, v, zwa, zg, w0, wdu, a0, wiu, wgu, k_k, k_a):
    w_log = -_softplus(-(w0 + _bdot(jnp.tanh(zwa), wdu))) - 0.5
    decay = jnp.exp(-jnp.exp(w_log))
    a = _sigmoid(a0 + _bdot(zwa, wiu))
    g = _bdot(_sigmoid(zg), wgu)
    kk = k * k_k
    kk = kk / jnp.maximum(jnp.sqrt(_head_sum(kk * kk)), 1e-12)
    k2 = k * (1.0 + (a - 1.0) * k_a)
    return r, decay, k2, v, -kk, kk * a, g


def _rwkv_out(o, r, k2, v, g, lng, lnb, rk):
    mu = _head_sum(o) * (1.0 / HEAD_DIM)
    d = o - mu
    var = _head_sum(d * d) * (1.0 / HEAD_DIM)
    on = d * lax.rsqrt(var + GN_EPS) * lng + lnb
    bonus = _head_sum(r * k2 * rk) * v
    return (on + bonus) * g


P_SPLITS = (0, 512, 1024, 1536, 1664, 1792)
N_PREP_PARAMS = 7
HALO = 8


def _shifted_pieces(i, p_ref, halo_ref, mix_ref):
    p = p_ref[:, P_OFF:]
    prev_row = halo_ref[HALO - 1:HALO, P_OFF:] * jnp.where(i > 0, 1.0, 0.0)
    row = lax.broadcasted_iota(jnp.int32, p.shape, 0)
    pprev = jnp.where(row == 0, prev_row, pltpu.roll(p, 1, 0))
    delta = pprev - p
    ps = p + delta * mix_ref[...]
    return [ps[:, a:b] for a, b in zip(P_SPLITS[:-1], P_SPLITS[1:])], delta


def _prep_in_specs():
    return [_rows(TR, D_IN),
            pl.BlockSpec((HALO, D_IN), lambda i: (jnp.maximum(i * (TR // HALO) - 1, 0), 0)),
            _const((1, RWKV_COLS)), _const((1, D_RWKV)), _const((LANES, D_RWKV)), _const((1, D_RWKV)),
            _const((LANES, D_RWKV)), _const((LANES, D_RWKV)), _const((1, D_RWKV)), _const((1, D_RWKV))]


def _rwkv_prep(proj, mix, prm):
    def body(p_ref, halo_ref, mix_ref, *refs):
        prm_refs, outs = refs[:N_PREP_PARAMS], refs[N_PREP_PARAMS:]
        pieces, _ = _shifted_pieces(pl.program_id(0), p_ref, halo_ref, mix_ref)
        vals = _rwkv_core(*pieces, *[t[...] for t in prm_refs])
        for ref, val in zip(outs, vals):
            ref[...] = val

    return pl.pallas_call(
        body, name="rwkv_prep", grid=(SEQ // TR,),
        in_specs=_prep_in_specs(),
        out_specs=[_rows(TR, D_RWKV)] * 7,
        out_shape=[jax.ShapeDtypeStruct((SEQ, D_RWKV), F32)] * 7,
        compiler_params=_cp(("parallel",)),
    )(proj, proj, mix, *prm)


def _rwkv_prep_bwd(proj, mix, prm, cts):
    def body(p_ref, halo_ref, mix_ref, *refs):
        i = pl.program_id(0)
        prm_refs = refs[:N_PREP_PARAMS]
        ct_refs = refs[N_PREP_PARAMS:N_PREP_PARAMS + 10]
        dps_ref, dmix_ref = refs[N_PREP_PARAMS + 10:N_PREP_PARAMS + 12]
        dprm_refs = refs[N_PREP_PARAMS + 12:]
        pieces, delta = _shifted_pieces(i, p_ref, halo_ref, mix_ref)
        _, vjp = jax.vjp(_rwkv_core, *pieces, *[t[...] for t in prm_refs])
        dr1, dr2, dw, dk1, dk2, dv1, dv2, dkkn, db, dg = [t[...] for t in ct_refs]
        grads = vjp((dr1 + dr2, dw, dk1 + dk2, dv1 + dv2, dkkn, db, dg))
        dps = jnp.concatenate(grads[:5], axis=1)
        dps_ref[...] = dps

        @pl.when(i == 0)
        def _():
            dmix_ref[...] = jnp.zeros_like(dmix_ref)
            for ref in dprm_refs:
                ref[...] = jnp.zeros_like(ref)

        dmix_ref[...] += jnp.sum(dps * delta, axis=0, keepdims=True)
        for ref, gval in zip(dprm_refs, grads[5:]):
            ref[...] += gval

    prm_shapes = [(1, D_RWKV), (LANES, D_RWKV), (1, D_RWKV), (LANES, D_RWKV), (LANES, D_RWKV), (1, D_RWKV), (1, D_RWKV)]
    return pl.pallas_call(
        body, name="rwkv_prep_bwd", grid=(SEQ // TR,),
        in_specs=_prep_in_specs() + [_rows(TR, D_RWKV)] * 10,
        out_specs=[_rows(TR, RWKV_COLS), _const((1, RWKV_COLS))] + [_const(s) for s in prm_shapes],
        out_shape=[jax.ShapeDtypeStruct((SEQ, RWKV_COLS), F32), jax.ShapeDtypeStruct((1, RWKV_COLS), F32)]
        + [jax.ShapeDtypeStruct(s, F32) for s in prm_shapes],
        compiler_params=_cp(("arbitrary",)),
    )(proj, proj, mix, *prm, *cts)


def _rwkv_post(o, r, k2, v, g, lng, lnb, rk, attn):
    def body(o_ref, r_ref, k_ref, v_ref, g_ref, lng_ref, lnb_ref, rk_ref, attn_ref, cat_ref):
        rw = _rwkv_out(*[t[...] for t in (o_ref, r_ref, k_ref, v_ref, g_ref, lng_ref, lnb_ref, rk_ref)])
        cat_ref[...] = jnp.concatenate([attn_ref[...], rw], axis=1).astype(BF16)

    return pl.pallas_call(
        body, name="rwkv_post", grid=(SEQ // TR,),
        in_specs=[_rows(TR, D_RWKV)] * 5 + [_const((1, D_RWKV))] * 3 + [_rows(TR, D_ATTN)],
        out_specs=_rows(TR, D_MODEL),
        out_shape=jax.ShapeDtypeStruct((SEQ, D_MODEL), BF16),
        compiler_params=_cp(("parallel",)),
    )(o, r, k2, v, g, lng, lnb, rk, attn)


def _rwkv_post_bwd(o, r, k2, v, g, lng, lnb, rk, dcat):
    def body(o_ref, r_ref, k_ref, v_ref, g_ref, lng_ref, lnb_ref, rk_ref, dcat_ref,
             do_ref, dr_ref, dk_ref, dv_ref, dg_ref, dlng_ref, dlnb_ref, drk_ref):
        i = pl.program_id(0)
        args = [t[...] for t in (o_ref, r_ref, k_ref, v_ref, g_ref, lng_ref, lnb_ref, rk_ref)]
        _, vjp = jax.vjp(_rwkv_out, *args)
        grads = vjp(dcat_ref[:, D_ATTN:])
        for ref, gval in zip((do_ref, dr_ref, dk_ref, dv_ref, dg_ref), grads[:5]):
            ref[...] = gval

        @pl.when(i == 0)
        def _():
            for ref in (dlng_ref, dlnb_ref, drk_ref):
                ref[...] = jnp.zeros_like(ref)

        for ref, gval in zip((dlng_ref, dlnb_ref, drk_ref), grads[5:]):
            ref[...] += gval

    return pl.pallas_call(
        body, name="rwkv_post_bwd", grid=(SEQ // TR,),
        in_specs=[_rows(TR, D_RWKV)] * 5 + [_const((1, D_RWKV))] * 3 + [_rows(TR, D_MODEL)],
        out_specs=[_rows(TR, D_RWKV)] * 5 + [_const((1, D_RWKV))] * 3,
        out_shape=[jax.ShapeDtypeStruct((SEQ, D_RWKV), F32)] * 5 + [jax.ShapeDtypeStruct((1, D_RWKV), F32)] * 3,
        compiler_params=_cp(("arbitrary",)),
    )(o, r, k2, v, g, lng, lnb, rk, dcat)


def _assemble_dproj(dq, dkv, dps, mix):
    last = SEQ // HALO - 1

    def body(dq_ref, dkv_ref, dps_ref, nxt_ref, mix_ref, o_ref):
        i = pl.program_id(0)
        dps = dps_ref[...]
        mixv = mix_ref[...]
        nxt_row = nxt_ref[0:1, :] * jnp.where(i < SEQ // TR - 1, 1.0, 0.0)
        row = lax.broadcasted_iota(jnp.int32, dps.shape, 0)
        up = jnp.where(row == TR - 1, nxt_row, pltpu.roll(dps, TR - 1, 0))
        dp = dps * (1.0 - mixv) + up * mixv
        o_ref[...] = jnp.concatenate([dq_ref[...], dkv_ref[...], dp], axis=1).astype(BF16)

    return pl.pallas_call(
        body, name="assemble_dproj", grid=(SEQ // TR,),
        in_specs=[_rows(TR, D_ATTN), _rows(TR, 2 * D_KV), _rows(TR, RWKV_COLS),
                  pl.BlockSpec((HALO, RWKV_COLS), lambda i: (jnp.minimum((i + 1) * (TR // HALO), last), 0)),
                  _const((1, RWKV_COLS))],
        out_specs=_rows(TR, D_IN),
        out_shape=jax.ShapeDtypeStruct((SEQ, D_IN), BF16),
        compiler_params=_cp(("parallel",)),
    )(dq, dkv, dps, dps, mix)


N_PAIR = D_RWKV // LANES
CHUNK = 64
N_CHUNK = SEQ // CHUNK
GROUP = 8
STATE = (N_PAIR, HEAD_DIM, LANES)


def _lane_sums(lhs_tiles, ones2):
    out = _dot(jnp.concatenate(lhs_tiles, axis=0), ones2)
    return [out[i * HEAD_DIM:(i + 1) * HEAD_DIM] for i in range(len(lhs_tiles))]


def _seg_sum(xs, ones2):
    return _lane_sums([jnp.concatenate(_split(x, 2), axis=1) for x in xs], ones2)


def _seg_sum_rows(xs, ones2):
    out = _dot(jnp.concatenate(_split(jnp.concatenate(xs, axis=0), 2), axis=1), ones2)
    return [out[i * GROUP:(i + 1) * GROUP] for i in range(len(xs))]


def _col_form(rows, diag, ones2):
    zero = jnp.zeros((HEAD_DIM, LANES), BF16)
    tiles = []
    for row in rows:
        hi = row.astype(BF16)
        lo = (row - hi.astype(F32)).astype(BF16)
        tiles.append(jnp.concatenate(
            [jnp.where(diag, jnp.broadcast_to(part, (HEAD_DIM, LANES)), zero) for part in (hi, lo)], axis=1))
    return _lane_sums(tiles, ones2)


def _scan_consts():
    ones2 = jnp.concatenate([_head_ones(LANES)] * 2, axis=0)
    sub = lax.broadcasted_iota(jnp.int32, (HEAD_DIM, LANES), 0)
    lane_in_head = lax.broadcasted_iota(jnp.int32, (HEAD_DIM, LANES), 1) & (HEAD_DIM - 1)
    return ones2, lane_in_head == sub, lane_in_head


def _rows_of_columns(tile):
    t = tile.T
    return jnp.concatenate([t[:CHUNK], t[HEAD_DIM:HEAD_DIM + CHUNK]], axis=1)


def _pair(j):
    return slice(j * LANES, (j + 1) * LANES)


def _scan_fwd(r, w, k, v, kkn, b):
    def body(r_ref, w_ref, k_ref, v_ref, kkn_ref, b_ref, o_ref, st_ref, sa_ref, s_scr):
        c = pl.program_id(0)
        ones2, diag, lane_in_head = _scan_consts()

        @pl.when(c == 0)
        def _():
            s_scr[...] = jnp.zeros_like(s_scr)

        def group(gi, carry):
            row0 = pl.multiple_of(gi * GROUP, GROUP)
            states, ocols = list(carry[:N_PAIR]), list(carry[N_PAIR:])
            tiles = [[t[pl.ds(row0, GROUP), _pair(j)] for t in (r_ref, w_ref, k_ref, v_ref, kkn_ref, b_ref)]
                     for j in range(N_PAIR)]
            def row(j, name, u):
                return tiles[j]["rwkvnb".index(name)][u:u + 1]

            def emit_out(u, after):
                outs = _seg_sum([s[j] * row(j, "r", u + d) for d, s in enumerate(after) for j in range(N_PAIR)], ones2)
                for d in range(2):
                    here = lane_in_head == gi * GROUP + u + d
                    for j in range(N_PAIR):
                        ocols[j] = jnp.where(here, outs[d * N_PAIR + j], ocols[j])

            def vcols_of(u):
                cols = _col_form([row(j, "v", u + d) for d in range(2) for j in range(N_PAIR)], diag, ones2)
                return cols[:N_PAIR], cols[N_PAIR:]

            n_next = [pltpu.roll(tiles[j][4], GROUP - 1, 0) for j in range(N_PAIR)]
            dots = _seg_sum_rows([tiles[j][5] * n_next[j] for j in range(N_PAIR)]
                                 + [tiles[j][2] * n_next[j] for j in range(N_PAIR)], ones2)
            b_n, k_n = dots[:N_PAIR], dots[N_PAIR:]
            w_n = [tiles[j][1] * n_next[j] for j in range(N_PAIR)]

            vcols = vcols_of(0)
            after = None
            for u in range(0, GROUP, 2):
                prods = _seg_sum([states[j] * row(j, "n", u) for j in range(N_PAIR)]
                                 + [states[j] * w_n[j][u:u + 1] for j in range(N_PAIR)], ones2)
                if after is not None:
                    emit_out(u - 2, after)
                nxt = vcols_of(u + 2) if u + 2 < GROUP else None
                first, second = [], []
                for j in range(N_PAIR):
                    sa1 = prods[j]
                    sa2 = prods[N_PAIR + j] + sa1 * b_n[j][u:u + 1] + vcols[0][j] * k_n[j][u:u + 1]
                    s1 = states[j] * row(j, "w", u) + sa1 * row(j, "b", u) + vcols[0][j] * row(j, "k", u)
                    s2 = s1 * row(j, "w", u + 1) + sa2 * row(j, "b", u + 1) + vcols[1][j] * row(j, "k", u + 1)
                    st_ref[row0 + u, j] = s1
                    sa_ref[row0 + u, j] = sa1
                    st_ref[row0 + u + 1, j] = s2
                    sa_ref[row0 + u + 1, j] = sa2
                    first.append(s1)
                    second.append(s2)
                    states[j] = s2
                after, vcols = (first, second), nxt
            emit_out(GROUP - 2, after)
            return tuple(states + ocols)

        zero = jnp.zeros((HEAD_DIM, LANES), F32)
        fin = lax.fori_loop(0, CHUNK // GROUP, group, tuple(s_scr[j] for j in range(N_PAIR)) + (zero,) * N_PAIR)
        for j in range(N_PAIR):
            s_scr[j] = fin[j]
            o_ref[:, _pair(j)] = _rows_of_columns(fin[N_PAIR + j])

    blk = pl.BlockSpec((CHUNK, D_RWKV), lambda c: (c, 0))
    per_step = pl.BlockSpec((CHUNK,) + STATE, lambda c: (c, 0, 0, 0))
    return pl.pallas_call(
        body, name="rwkv_scan_fwd", grid=(N_CHUNK,),
        in_specs=[blk] * 6,
        out_specs=[blk, per_step, per_step],
        out_shape=[jax.ShapeDtypeStruct((SEQ, D_RWKV), F32)] + [jax.ShapeDtypeStruct((SEQ,) + STATE, F32)] * 2,
        scratch_shapes=[pltpu.VMEM(STATE, F32)],
        compiler_params=_cp(("arbitrary",)),
    )(r, w, k, v, kkn, b)


def _scan_bwd(r, w, k, v, kkn, b, do, states, sas, ds_in, prev, name, first_chunk, n_chunks):
    top = first_chunk + n_chunks - 1

    def body(r_ref, w_ref, k_ref, v_ref, kkn_ref, b_ref, do_ref, st_ref, before_ref, sa_ref, ds_in_ref, *rest):
        dr_ref, dw_ref, dk_ref, dv_ref, dkkn_ref, db_ref, ds_out_ref, ds_scr = rest[-8:]
        i = pl.program_id(0)
        ones2, diag, lane_in_head = _scan_consts()

        @pl.when(i == 0)
        def _():
            ds_scr[...] = ds_in_ref[...]

        entry = [before_ref[0, j] * jnp.where(i < top, 1.0, 0.0) for j in range(N_PAIR)]

        def reverse(gr, carry):
            gi = CHUNK // GROUP - 1 - gr
            row0 = pl.multiple_of(gi * GROUP, GROUP)
            dstates, dvcols = list(carry[:N_PAIR]), list(carry[N_PAIR:])
            tiles = [[t[pl.ds(row0, GROUP), _pair(j)]
                      for t in (r_ref, w_ref, k_ref, v_ref, kkn_ref, b_ref, do_ref)] for j in range(N_PAIR)]
            rows = [[[None] * GROUP for _ in range(5)] for _ in range(N_PAIR)]

            def row(j, name, u):
                return tiles[j]["rwkvnbd".index(name)][u:u + 1]

            def cols_of(u):
                cols = _col_form([row(j, name, u - d) for d in range(2) for name in "dv" for j in range(N_PAIR)],
                                 diag, ones2)
                return [[(cols[(2 * d) * N_PAIR + j], cols[(2 * d + 1) * N_PAIR + j]) for j in range(N_PAIR)]
                        for d in range(2)]

            def emit_dv(u, dsps):
                outs = _seg_sum([dsp[j] * row(j, "k", u - d) for d, dsp in enumerate(dsps) for j in range(N_PAIR)], ones2)
                for d in range(2):
                    here = lane_in_head == gi * GROUP + u - d
                    for j in range(N_PAIR):
                        dvcols[j] = jnp.where(here, outs[d * N_PAIR + j], dvcols[j])

            b_prev = [pltpu.roll(tiles[j][5], 1, 0) for j in range(N_PAIR)]
            dots = _seg_sum_rows([tiles[j][4] * b_prev[j] for j in range(N_PAIR)]
                                 + [tiles[j][0] * tiles[j][5] for j in range(N_PAIR)], ones2)
            n_b, r_b = dots[:N_PAIR], dots[N_PAIR:]
            w_b = [tiles[j][1] * b_prev[j] for j in range(N_PAIR)]

            def outputs(u, j, dsp, dsa, docol, vcol):
                tl = gi * GROUP + u
                if u > 0:
                    s_prev = st_ref[tl - 1, j]
                else:
                    s_prev = jnp.where(gi == 0, entry[j], st_ref[jnp.maximum(tl - 1, 0), j])
                rows[j][0][u] = jnp.sum(st_ref[tl, j] * docol, axis=0, keepdims=True)
                rows[j][1][u] = jnp.sum(dsp * s_prev, axis=0, keepdims=True)
                rows[j][2][u] = jnp.sum(dsp * vcol, axis=0, keepdims=True)
                rows[j][3][u] = jnp.sum(s_prev * dsa, axis=0, keepdims=True)
                rows[j][4][u] = jnp.sum(dsp * sa_ref[tl, j], axis=0, keepdims=True)

            cols = cols_of(GROUP - 1)
            before = None
            for u in range(GROUP - 1, 0, -2):
                dsp1 = [dstates[j] + cols[0][j][0] * row(j, "r", u) for j in range(N_PAIR)]
                prods = _seg_sum([dsp1[j] * row(j, "b", u) for j in range(N_PAIR)]
                                 + [dsp1[j] * w_b[j][u:u + 1] for j in range(N_PAIR)], ones2)
                if before is not None:
                    emit_dv(u + 2, before)
                nxt = cols_of(u - 2) if u >= 2 else None
                dsp2 = []
                for j in range(N_PAIR):
                    dsa1 = prods[j]
                    dsa2 = prods[N_PAIR + j] + dsa1 * n_b[j][u:u + 1] + cols[1][j][0] * r_b[j][u - 1:u]
                    mid = dsp1[j] * row(j, "w", u) + dsa1 * row(j, "n", u) + cols[1][j][0] * row(j, "r", u - 1)
                    outputs(u, j, dsp1[j], dsa1, *cols[0][j])
                    outputs(u - 1, j, mid, dsa2, *cols[1][j])
                    dstates[j] = mid * row(j, "w", u - 1) + dsa2 * row(j, "n", u - 1)
                    dsp2.append(mid)
                before, cols = (dsp1, dsp2), nxt
            emit_dv(1, before)
            for j in range(N_PAIR):
                for ref, rr in zip((dr_ref, dw_ref, dk_ref, dkkn_ref, db_ref), rows[j]):
                    ref[pl.ds(row0, GROUP), _pair(j)] = jnp.concatenate(rr, axis=0)
            return tuple(dstates + dvcols)

        zero = jnp.zeros((HEAD_DIM, LANES), F32)
        dfin = lax.fori_loop(0, CHUNK // GROUP, reverse, tuple(ds_scr[j] for j in range(N_PAIR)) + (zero,) * N_PAIR)
        for j in range(N_PAIR):
            ds_scr[j] = dfin[j]
            dv_ref[:, _pair(j)] = _rows_of_columns(dfin[N_PAIR + j])

        @pl.when(i == n_chunks - 1)
        def _():
            ds_out_ref[...] = ds_scr[...]

    blk = pl.BlockSpec((CHUNK, D_RWKV), lambda i: (top - i, 0))
    per_step = pl.BlockSpec((CHUNK,) + STATE, lambda i: (top - i, 0, 0, 0))
    step_before = pl.BlockSpec((1,) + STATE, lambda i: (jnp.maximum((top - i) * CHUNK - 1, 0), 0, 0, 0))
    prev = [] if prev is None else list(prev)
    outs = pl.pallas_call(
        body, name=name, grid=(n_chunks,),
        in_specs=[blk] * 7 + [per_step, step_before, per_step, _const(STATE)] + [ANY] * len(prev),
        out_specs=[blk] * 6 + [_const(STATE)],
        out_shape=[jax.ShapeDtypeStruct((SEQ, D_RWKV), F32)] * 6 + [jax.ShapeDtypeStruct(STATE, F32)],
        scratch_shapes=[pltpu.VMEM(STATE, F32)],
        input_output_aliases={11 + t: t for t in range(len(prev))},
        compiler_params=_cp(("arbitrary",)),
    )(r, w, k, v, kkn, b, do, states, states, sas, ds_in, *prev)
    return outs[:6], outs[6]


def _stacked(rows, cols, pick):
    return pl.BlockSpec((None, rows, cols), pick)


def _local_step(x, target, sm, win_st):
    def tied(t, token):
        return t if token is None else t + token[0:1, 0:1].reshape((1,) * t.ndim)

    zpad = jnp.zeros((LORA_DECAY, D_RWKV), F32)
    prm = [sm["w0"], jnp.concatenate([sm["w_decay_up"], zpad], axis=0), sm["a0"],
           jnp.concatenate([zpad, sm["w_iclr_up"]], axis=0), sm["w_gate_up"], sm["k_k"], sm["k_a"]]
    mix = sm["rwkv_shift_mix"]
    onehot = jnp.asarray(_t5_onehot(), BF16)
    sinks = sm["sinks"].reshape(N_Q_HEADS)
    lng, lnb, rk = sm["ln_x_g"], sm["ln_x_b"], sm["r_k"].reshape(1, D_RWKV)

    h1 = _norm_cast(x, sm["norm_mix_pre"], "norm_in")
    proj = _matmul(h1, win_st, "nn", "proj", m=SEQ, n=D_IN, k=D_MODEL, tm=SEQ, tn=640,
                   b_spec=_stacked(D_MODEL, 640, lambda i, j: (j, 0, 0)))
    bias = _bias_table(sm["rel_bias"].T, onehot).reshape(N_KV_HEADS, Q_PER_KV * BLOCK, 2 * BLOCK)
    attn = _attn_fwd(proj, bias, sinks)
    r, w, k2, v, kkn, b, g = _rwkv_prep(proj, mix, prm)
    o, states, sas = _scan_fwd(r, w, k2, v, kkn, b)
    wout, wup_st, wdown = yield ("rest_weights", o)
    cat = _rwkv_post(o, r, k2, v, g, lng, lnb, rk, attn)
    mixo = _matmul(cat, wout, "nn", "out_proj", m=SEQ, n=D_MODEL, k=D_MODEL, tm=SEQ, tn=512)
    x2, h3 = _mix_norm(x, mixo, sm["norm_mix_post"], sm["norm_ffn_pre"])
    u_gate, u_val, act = _ffn_up_act(h3, wup_st, sm["conv_w"], sm["conv_b"])
    f = _matmul(act, wdown, "nn", "ffn_down", m=SEQ, n=D_MODEL, k=D_FF, tm=1024, tn=512)
    loss, dy, df, d_g4 = _loss_head(x2, f, sm["norm_ffn_post"], target)

    d_wdown = _matmul(act, df, "tn", "d_wdown", m=D_FF, n=D_MODEL, k=SEQ, tm=512, tn=D_MODEL)
    du, d_convw, d_convb = _ffn_act_bwd(u_gate, u_val, df, wdown, sm["conv_w"], sm["conv_b"])
    d_convw = d_convw.transpose(1, 0, 2).reshape(3, 2 * D_FF)
    d_convb = d_convb.reshape(1, 2 * D_FF)
    dh3 = _matmul_nt_shards(du, wup_st, "d_h3", m=SEQ, n=D_MODEL, tm=512, tn=512,
                            a_spec=pl.BlockSpec((2, 512, D_FF), lambda i, j: (0, i, 0)),
                            a_piece=lambda ref, s: ref[s // 2, :, (s % 2) * 2048:(s % 2 + 1) * 2048])
    d_wup = _matmul(h3, du, "tn", "d_wup", m=D_MODEL, n=2 * D_FF, k=SEQ, tm=D_MODEL, tn=512,
                    b_spec=pl.BlockSpec((None, SEQ, 512), lambda i, j: (j // 8, 0, j % 8)),
                    out=((N_CHIPS, D_MODEL, 2048), _stacked(D_MODEL, 512, lambda i, j: (j // 4, 0, j % 4))))
    dx2, dmix, d_g2, d_g3 = _mid_bwd(x2, mixo, dy, dh3, sm["norm_mix_post"], sm["norm_ffn_pre"])
    dcat = _matmul(dmix, wout, "nt", "d_cat", m=SEQ, n=D_MODEL, k=D_MODEL, tm=SEQ, tn=512)
    d_wout = _matmul(cat, dmix, "tn", "d_wout", m=D_MODEL, n=D_MODEL, k=SEQ, tm=512, tn=D_MODEL)
    token = yield ("grads_a", (d_wdown, d_wup, d_wout))
    do, dr_p, dk_p, dv_p, dg, d_lng, d_lnb, d_rk = _rwkv_post_bwd(o, r, k2, v, g, lng, tied(lnb, token), rk, dcat)
    half = N_CHUNK // 2
    ds_end = jnp.zeros(STATE, F32)
    late, ds_mid = _scan_bwd(r, w, k2, v, kkn, b, do, states, sas, ds_end, None, "rwkv_scan_bwd_late", half, half)
    token = yield ("seam_1", ds_mid)
    scan_cts, ds_first = _scan_bwd(r, w, k2, v, kkn, b, do, states, sas, tied(ds_mid, token), late,
                                   "rwkv_scan_bwd_early", 0, half)
    dr_s, dw_s, dk_s, dv_s, dkkn_s, db_s = scan_cts
    token = yield ("seam_2", ds_first)
    prep_grads = _rwkv_prep_bwd(proj, tied(mix, token), prm,
                                (dr_s, dr_p, dw_s, dk_s, dk_p, dv_s, dv_p, dkkn_s, db_s, dg))
    dps, d_mix, d_w0, d_wdu, d_a0, d_wiu, d_wgu, d_kk, d_ka = prep_grads
    dq, dkv, dbias, dsink = _attn_bwd(proj, bias, sinks, dcat)
    d_relb = _bias_table_bwd(dbias.reshape(N_Q_HEADS, N_REL), onehot).T
    dproj = _assemble_dproj(dq, dkv, dps, mix)
    d_win = _matmul(h1, dproj, "tn", "d_win", m=D_MODEL, n=D_IN, k=SEQ, tm=D_MODEL, tn=640,
                    out=((N_CHIPS, D_MODEL, 640), _stacked(D_MODEL, 640, lambda i, j: (j, 0, 0))))
    token = yield ("grads_b", d_win)
    dh1 = _matmul_nt_shards(dproj, win_st, "d_h1", m=SEQ, n=D_MODEL, tm=1024, tn=D_MODEL,
                            a_spec=pl.BlockSpec((1024, D_IN), lambda i, j: (i, 0)),
                            a_piece=lambda ref, s: ref[:, s * 640:(s + 1) * 640])
    grad_x, d_g1 = _first_bwd(x, dx2, dh1, tied(sm["norm_mix_pre"], token))

    grads = {
        "norm_mix_pre": d_g1, "norm_mix_post": d_g2, "norm_ffn_pre": d_g3, "norm_ffn_post": d_g4,
        "w_in": d_win, "rel_bias": d_relb, "sinks": dsink[:, 0].reshape(1, N_Q_HEADS),
        "rwkv_shift_mix": d_mix, "w0": d_w0, "w_decay_up": d_wdu[:LORA_DECAY], "a0": d_a0,
        "w_iclr_up": d_wiu[LORA_DECAY:], "w_gate_up": d_wgu, "k_k": d_kk, "k_a": d_ka,
        "r_k": d_rk.reshape(1, N_Q_HEADS, HEAD_DIM), "ln_x_g": d_lng, "ln_x_b": d_lnb,
        "w_out": d_wout, "w_ffn_up": d_wup, "conv_w": d_convw, "conv_b": d_convb, "w_ffn_down": d_wdown,
    }
    return loss, grad_x, grads


def _place():
    x, y, c = lax.axis_index("x"), lax.axis_index("y"), lax.axis_index("c")
    chips = [(1 - x, y), (x, 1 - y), (1 - x, 1 - y)]
    return x, y, c, chips


def _remote(src, dst, sems, idx, to):
    return pltpu.make_async_remote_copy(src_ref=src, dst_ref=dst, send_sem=sems[0].at[idx], recv_sem=sems[1].at[idx],
                                        device_id=to, device_id_type=MESH)


ROW_ALIGN = 16


def _half(c, rows):
    return pl.ds(pl.multiple_of(c * (rows // 2), ROW_ALIGN), rows // 2)


def _gather_weights(big, small):
    nb, ns = len(big), len(small)

    def body(*refs):
        ins, outs = refs[:nb + ns], refs[nb + ns:2 * (nb + ns)]
        ici, d2d, sml, loc = refs[2 * (nb + ns):2 * (nb + ns) + 2], refs[-5:-3], refs[-3:-1], refs[-1]
        x, y, c, chips = _place()
        me = 2 * x + y
        sib = (x, y, 1 - c)
        local = [pltpu.make_async_copy(ins[a], outs[a].at[me], loc.at[a]) for a in range(nb + ns)]
        for cp in local:
            cp.start()
        sends = []
        for a in range(nb):
            rows = _half(c, big[a].shape[0])
            for kk, chip in enumerate(chips):
                sends.append(_remote(ins[a].at[rows], outs[a].at[me, rows], ici, a * 3 + kk, (*chip, c)))
        for a in range(ns):
            for kk, chip in enumerate(chips):
                sends.append(_remote(ins[nb + a], outs[nb + a].at[me], sml, a * 3 + kk, (*chip, c)))
        for cp in sends:
            cp.start()
        passed = []
        for a in range(nb):
            rows = _half(c, big[a].shape[0])
            for kk, (px, py) in enumerate(chips):
                got = outs[a].at[2 * px + py, rows]
                _remote(got, got, ici, a * 3 + kk, sib).wait_recv()
                fwd = _remote(got, got, d2d, a * 3 + kk, sib)
                fwd.start()
                passed.append(fwd)
        for a in range(nb):
            other = _half(1 - c, big[a].shape[0])
            for kk, (px, py) in enumerate(chips):
                land = outs[a].at[2 * px + py, other]
                _remote(land, land, d2d, a * 3 + kk, sib).wait_recv()
        for a in range(ns):
            for kk, (px, py) in enumerate(chips):
                land = outs[nb + a].at[2 * px + py]
                _remote(land, land, sml, a * 3 + kk, sib).wait_recv()
        for cp in sends + passed:
            cp.wait_send()
        for cp in local:
            cp.wait()

    arrs = list(big) + list(small)
    in_vmem = pl.BlockSpec(memory_space=pltpu.VMEM)
    return pl.pallas_call(
        body, name="gather_weights",
        in_specs=[in_vmem] * len(arrs), out_specs=[in_vmem] * len(arrs),
        out_shape=[jax.ShapeDtypeStruct((N_CHIPS,) + t.shape, t.dtype) for t in arrs],
        scratch_shapes=[pltpu.SemaphoreType.DMA((3 * nb,)), pltpu.SemaphoreType.DMA((3 * nb,)),
                        pltpu.SemaphoreType.DMA((3 * nb,)), pltpu.SemaphoreType.DMA((3 * nb,)),
                        pltpu.SemaphoreType.DMA((3 * ns,)), pltpu.SemaphoreType.DMA((3 * ns,)),
                        pltpu.SemaphoreType.DMA((nb + ns,))],
        compiler_params=pltpu.CompilerParams(has_side_effects=True, vmem_limit_bytes=VMEM_LIMIT),
    )(*arrs)


HBM = pl.BlockSpec(memory_space=pltpu.HBM)
SEM = pl.BlockSpec(memory_space=pltpu.SEMAPHORE)
EFFECT = pltpu.SideEffectType.DATAFLOW_SIDE_EFFECTING


def _copies_start(name, bufs, plan, n):
    nb = len(bufs)

    def body(*refs):
        ins, sems, token = refs[:nb], refs[nb:nb + 2 * n], refs[-1]
        for kk, (src, dst, dev) in enumerate(plan(ins)):
            pltpu.make_async_remote_copy(src_ref=src, dst_ref=dst, send_sem=sems[2 * kk], recv_sem=sems[2 * kk + 1],
                                         device_id=dev, device_id_type=MESH).start()
        token[...] = jnp.zeros_like(token)

    outs = pl.pallas_call(
        body, name=name,
        out_shape=tuple([pltpu.SemaphoreType.DMA(())] * (2 * n) + [pltpu.HBM(t.shape, t.dtype) for t in bufs]
                        + [jax.ShapeDtypeStruct((8, LANES), F32)]),
        in_specs=[HBM] * nb,
        out_specs=tuple([SEM] * (2 * n) + [HBM] * nb + [pl.BlockSpec(memory_space=pltpu.VMEM)]),
        input_output_aliases={t: 2 * n + t for t in range(nb)},
        compiler_params=pltpu.CompilerParams(has_side_effects=EFFECT),
    )(*[pltpu.with_memory_space_constraint(t, pltpu.HBM) for t in bufs])
    return outs[:2 * n], outs[2 * n:2 * n + nb], outs[-1]


def _copies_wait(name, sems, bufs, plan, n, after):
    nb = len(bufs)
    after = list(after) if isinstance(after, (list, tuple)) else [after]

    def body(*refs):
        ins, sem_refs = refs[:nb], refs[nb:nb + 2 * n]
        for kk, (src, dst, dev) in enumerate(plan(ins)):
            cp = pltpu.make_async_remote_copy(src_ref=src, dst_ref=dst, send_sem=sem_refs[2 * kk],
                                              recv_sem=sem_refs[2 * kk + 1], device_id=dev, device_id_type=MESH)
            cp.wait_send()
            cp.wait_recv()

    return pl.pallas_call(
        body, name=name,
        out_shape=tuple(pltpu.HBM(t.shape, t.dtype) for t in bufs),
        in_specs=[HBM] * nb + [SEM] * (2 * n) + [ANY] * len(after),
        out_specs=tuple([HBM] * nb),
        input_output_aliases={t: t for t in range(nb)},
        compiler_params=pltpu.CompilerParams(has_side_effects=EFFECT),
    )(*bufs, *sems, *after)


def _plan_gather(n_w):
    def plan(refs):
        x, y, c, chips = _place()
        me = 2 * x + y
        return [(refs[a], refs[n_w + a].at[me], (*chip, c)) for a in range(n_w) for chip in chips]
    return plan


def _plan_pair_halves(n_g, rows):
    def plan(refs):
        x, y, c, _ = _place()
        return [(refs[a].at[:, _half(1 - c, rows[a])], refs[n_g + a], (x, y, 1 - c)) for a in range(n_g)]
    return plan


def _plan_chip_parts(n_g):
    def plan(refs):
        x, y, c, chips = _place()
        me = 2 * x + y
        return [(refs[a].at[2 * px + py], refs[n_g + a].at[me], (px, py, c))
                for a in range(n_g) for (px, py) in chips]
    return plan


def _plan_pair_fill(n_g, rows):
    def plan(refs):
        x, y, c, _ = _place()
        return [(refs[a].at[_half(c, rows[a])], refs[a].at[_half(c, rows[a])], (x, y, 1 - c)) for a in range(n_g)]
    return plan


def _pair_add(g, got, name):
    _, rows, cols = g.shape
    hr = rows // 2
    tr = min(hr, 256)
    nb = hr // tr

    def body(g_ref, got_ref, p_ref, own_ref):
        val = (g_ref[...] + got_ref[...]).astype(BF16)
        p_ref[...] = val

        @pl.when(pl.program_id(1) == 2 * lax.axis_index("x") + lax.axis_index("y"))
        def _():
            own_ref[...] = val

    def mine(i, s):
        return (2 * lax.axis_index("x") + lax.axis_index("y"), i, 0)

    return pl.pallas_call(
        body, name=name, grid=(nb, N_CHIPS),
        in_specs=[pl.BlockSpec((None, tr, cols), lambda i, s: (s, lax.axis_index("c") * nb + i, 0)),
                  pl.BlockSpec((None, tr, cols), lambda i, s: (s, i, 0))],
        out_specs=[pl.BlockSpec((None, tr, cols), lambda i, s: (s, i, 0)), pl.BlockSpec((None, tr, cols), mine)],
        out_shape=[jax.ShapeDtypeStruct((N_CHIPS, hr, cols), BF16)] * 2,
        compiler_params=_cp(("parallel", "arbitrary")),
    )(g, got)


def _chip_sum(parts, name):
    _, hr, cols = parts.shape
    tr = min(hr, 128)
    nb = hr // tr

    def body(t_ref, o_ref):
        part = [t_ref[s].astype(F32) for s in range(N_CHIPS)]
        o_ref[...] = ((part[0] + part[1]) + part[2]) + part[3]

    return pl.pallas_call(
        body, name=name, grid=(nb,),
        in_specs=[pl.BlockSpec((N_CHIPS, tr, cols), lambda i: (0, i, 0))],
        out_specs=pl.BlockSpec((tr, cols), lambda i: (lax.axis_index("c") * nb + i, 0)),
        out_shape=jax.ShapeDtypeStruct((2 * hr, cols), F32),
        compiler_params=_cp(("parallel",)),
    )(parts)


class _Reduction:
    def __init__(self, tag, rows):
        self.tag, self.n, self.rows = tag, len(rows), rows
        self.plans = (_plan_pair_halves(self.n, rows), _plan_chip_parts(self.n), _plan_pair_fill(self.n, rows))
        self.flight = None

    def _name(self, what):
        return f"grad_{self.tag}_{what}"

    def start(self, gs):
        gots = [lax.empty((N_CHIPS, t.shape[1] // 2, t.shape[2]), F32) for t in gs]
        self.flight = _copies_start(self._name("pair_start"), list(gs) + gots, self.plans[0], self.n)
        return self.flight[2]

    def after_pair(self, after):
        sems, bufs, _ = self.flight
        out = _copies_wait(self._name("pair_wait"), sems, bufs, self.plans[0], self.n, after)
        sums = [_pair_add(g, got, self._name(f"pair_add_{i}"))
                for i, (g, got) in enumerate(zip(out[:self.n], out[self.n:]))]
        self.flight = _copies_start(self._name("chip_start"), [p for p, _ in sums] + [own for _, own in sums],
                                    self.plans[1], 3 * self.n)
        return self.flight[2]

    def after_chips(self, after):
        sems, bufs, _ = self.flight
        out = _copies_wait(self._name("chip_wait"), sems, bufs, self.plans[1], 3 * self.n, after)
        fulls = [_chip_sum(t, self._name(f"chip_sum_{i}")) for i, t in enumerate(out[self.n:])]
        self.flight = _copies_start(self._name("fill_start"), fulls, self.plans[2], self.n)
        return self.flight[2]

    def finish(self, after):
        sems, bufs, _ = self.flight
        return _copies_wait(self._name("fill_wait"), sems, bufs, self.plans[2], self.n, after)


def _adamw_math(w, g, m, v):
    nm = ADAM_B1 * m + (1.0 - ADAM_B1) * g
    nv = ADAM_B2 * v + (1.0 - ADAM_B2) * (g * g)
    m_hat = nm / (1.0 - ADAM_B1 ** ADAM_STEP)
    v_hat = nv / (1.0 - ADAM_B2 ** ADAM_STEP)
    return -ADAM_LR * (m_hat / (jnp.sqrt(v_hat) + ADAM_EPS) + ADAM_WD * w), nm, nv


def _adamw(w, g, m, v, name, tr):
    r, cdim = w.shape

    def body(w_ref, g_ref, m_ref, v_ref, d_ref, nm_ref, nv_ref):
        d_ref[...], nm_ref[...], nv_ref[...] = _adamw_math(w_ref[...], g_ref[...], m_ref[...], v_ref[...])

    return pl.pallas_call(
        body, name=name, grid=(r // tr,), in_specs=[_rows(tr, cdim)] * 4, out_specs=[_rows(tr, cdim)] * 3,
        out_shape=[jax.ShapeDtypeStruct((r, cdim), F32)] * 3, compiler_params=_cp(("parallel",)),
    )(w, g, m, v)


def _adamw_small(w, parts, m, v):
    def body(w_ref, p_ref, m_ref, v_ref, d_ref, nm_ref, nv_ref, g_ref):
        g = p_ref[0]
        for dev in range(1, N_DEV):
            g = g + p_ref[dev]
        g_ref[...] = g
        d_ref[...], nm_ref[...], nv_ref[...] = _adamw_math(w_ref[...], g, m_ref[...], v_ref[...])

    return pl.pallas_call(
        body, name="adamw_small", grid=(1,),
        in_specs=[_const(w.shape), _const(parts.shape), _const(w.shape), _const(w.shape)],
        out_specs=[_const(w.shape)] * 4, out_shape=[jax.ShapeDtypeStruct(w.shape, F32)] * 4,
        compiler_params=_cp(("arbitrary",)),
    )(w, parts, m, v)


REPLICATED = (("norm_mix_pre", 1024), ("norm_mix_post", 1024), ("norm_ffn_pre", 1024), ("norm_ffn_post", 1024),
              ("rel_bias", 256), ("sinks", 8), ("rwkv_shift_mix", 1792), ("w0", 512), ("a0", 512), ("k_k", 512),
              ("k_a", 512), ("r_k", 512), ("ln_x_g", 512), ("ln_x_b", 512), ("conv_b", 8192))
SMALL_SHARDED = (("w_decay_up", LORA_DECAY, D_RWKV), ("w_iclr_up", LORA_ICLR, D_RWKV),
                 ("w_gate_up", LORA_GATE, D_RWKV), ("conv_w", 3, 2 * D_FF))
BIG = (("w_in", D_MODEL, 640), ("w_out", 256, D_MODEL), ("w_ffn_up", D_MODEL, 2048), ("w_ffn_down", 1024, D_MODEL))
PACK_ALIGN = 8 * LANES


def _pack(pieces):
    flat = []
    for t in pieces:
        t = t.reshape(-1)
        pad = (-t.shape[0]) % LANES
        flat.append(jnp.pad(t, (0, pad)) if pad else t)
    flat = jnp.concatenate(flat)
    pad = (-flat.shape[0]) % PACK_ALIGN
    return jnp.pad(flat, (0, pad)).reshape(-1, LANES)


def _unpack(buf, sizes):
    flat, out, off = buf.reshape(-1), [], 0
    for n in sizes:
        out.append(flat[off:off + n])
        off += n + ((-n) % LANES)
    return out


def kernel(x, norm_mix_pre, norm_mix_post, norm_ffn_pre, norm_ffn_post, w_in, rel_bias, sinks, rwkv_shift_mix, w0, w_decay_up, a0, w_iclr_up, w_gate_up, k_k, k_a, r_k, ln_x_g, ln_x_b, w_out, w_ffn_up, conv_w, conv_b, w_ffn_down, loss_target, m_norm_mix_pre, m_norm_mix_post, m_norm_ffn_pre, m_norm_ffn_post, m_w_in, m_rel_bias, m_sinks, m_rwkv_shift_mix, m_w0, m_w_decay_up, m_a0, m_w_iclr_up, m_w_gate_up, m_k_k, m_k_a, m_r_k, m_ln_x_g, m_ln_x_b, m_w_out, m_w_ffn_up, m_conv_w, m_conv_b, m_w_ffn_down, v_norm_mix_pre, v_norm_mix_post, v_norm_ffn_pre, v_norm_ffn_post, v_w_in, v_rel_bias, v_sinks, v_rwkv_shift_mix, v_w0, v_w_decay_up, v_a0, v_w_iclr_up, v_w_gate_up, v_k_k, v_k_a, v_r_k, v_ln_x_g, v_ln_x_b, v_w_out, v_w_ffn_up, v_conv_w, v_conv_b, v_w_ffn_down):
    given = dict(locals())
    names = [n for n, _ in REPLICATED] + [n for n, _, _ in SMALL_SHARDED] + [n for n, _, _ in BIG]
    order = ["norm_mix_pre", "norm_mix_post", "norm_ffn_pre", "norm_ffn_post", "w_in", "rel_bias", "sinks",
             "rwkv_shift_mix", "w0", "w_decay_up", "a0", "w_iclr_up", "w_gate_up", "k_k", "k_a", "r_k", "ln_x_g",
             "ln_x_b", "w_out", "w_ffn_up", "conv_w", "conv_b", "w_ffn_down"]
    assert sorted(names) == sorted(order)
    shard = 2 * lax.axis_index("x") + lax.axis_index("y")

    big_sh = {n: given[n].reshape(a, b).astype(BF16) for n, a, b in BIG}
    small_sh = [given[n].reshape(r, c // N_CHIPS) for n, r, c in SMALL_SHARDED]
    gathered = _gather_weights([big_sh["w_in"]], small_sh)
    rest = ("w_out", "w_ffn_up", "w_ffn_down")
    win_st, rest_sh = lax.optimization_barrier((gathered[0], [big_sh[n] for n in rest]))
    sm = {n: given[n] for n, _ in REPLICATED}
    sm["r_k"] = r_k.reshape(N_Q_HEADS, HEAD_DIM)
    for (n, r, c), st in zip(SMALL_SHARDED, gathered[1:]):
        sm[n] = st.transpose(1, 0, 2).reshape(r, c)

    lands = [lax.dynamic_update_slice(lax.empty((N_CHIPS,) + t.shape, BF16), t[None], (shard, 0, 0)) for t in rest_sh]
    plan_w = _plan_gather(len(rest))
    w_sems, w_bufs, token = _copies_start("gather_rest_start", rest_sh + lands, plan_w, 9)
    sm["norm_mix_pre"] = norm_mix_pre + token[0:1, 0:1]

    def on_rest_weights(after):
        out = _copies_wait("gather_rest_wait", w_sems, w_bufs, plan_w, 9, after)
        wout_st, wup_st, wdown_st = out[3:]
        return wout_st.reshape(D_MODEL, D_MODEL), wup_st, wdown_st.reshape(D_FF, D_MODEL)

    red_a = _Reduction("a", (1024, D_MODEL, 256))
    red_b = _Reduction("b", (D_MODEL,))

    def on_grads_a(gs):
        d_wdown, d_wup, d_wout = gs
        return red_a.start([d_wdown.reshape(N_CHIPS, 1024, D_MODEL), d_wup, d_wout.reshape(N_CHIPS, 256, D_MODEL)])

    handlers = {"rest_weights": on_rest_weights, "grads_a": on_grads_a, "seam_1": red_a.after_pair,
                "seam_2": red_a.after_chips, "grads_b": lambda g: red_b.start([g])}
    steps = _local_step(x[0], loss_target[0], sm, win_st)
    kind, payload = next(steps)
    while True:
        try:
            kind, payload = steps.send(handlers[kind](payload))
        except StopIteration as done:
            loss, grad_x, grads = done.value
            break

    small_names = [n for n, _ in REPLICATED] + [n for n, _, _ in SMALL_SHARDED]

    def shard_cols(t, s):
        return t[:, s * (t.shape[1] // N_CHIPS):(s + 1) * (t.shape[1] // N_CHIPS)]

    for_chip = jnp.stack([_pack([loss[0]] + [grads[n] for n, _ in REPLICATED]
                                + [shard_cols(grads[n], s) for n, _, _ in SMALL_SHARDED]) for s in range(N_CHIPS)])
    me = 2 * shard + lax.axis_index("c")
    mine = lax.dynamic_index_in_dim(for_chip, shard, 0, keepdims=True)
    land = lax.dynamic_update_slice(lax.empty((N_DEV,) + for_chip.shape[1:], F32), mine, (me, 0, 0))

    def plan_small(refs):
        x, y, c, _ = _place()
        out = []
        for rel in range(1, N_DEV):
            px, py, pc = x ^ (rel >> 2), y ^ ((rel >> 1) & 1), c ^ (rel & 1)
            out.append((refs[0].at[2 * px + py], refs[1].at[4 * x + 2 * y + c], (px, py, pc)))
        return out

    s_sems, s_bufs, _ = _copies_start("grad_small_start", [for_chip, land], plan_small, N_DEV - 1)

    red_b.after_pair(grad_x)
    g_out = {}
    g_out["w_ffn_down"], g_out["w_ffn_up"], g_out["w_out"] = red_a.finish(grad_x)

    delta, new_m, new_v = {}, {}, {}

    def update(n, a, b):
        delta[n], new_m[n], new_v[n] = _adamw(given[n].reshape(a, b), g_out[n], given["m_" + n].reshape(a, b),
                                              given["v_" + n].reshape(a, b), "adamw_" + n, 128)

    for n, a, b in BIG[1:]:
        update(n, a, b)
    done = [delta[n] for n, _, _ in BIG[1:]]
    red_b.after_chips(done)
    parts = _copies_wait("grad_small_wait", s_sems, s_bufs, plan_small, N_DEV - 1, done)[1]
    no_param = jnp.zeros((LANES,), F32)
    packs = [_pack([no_param] + [given[pre + n] for n in small_names]) for pre in ("", "m_", "v_")]
    small_sizes = [LANES] + [int(np.prod(given[n].shape)) for n in small_names]
    upd = [_unpack(t, small_sizes) for t in _adamw_small(packs[0], parts, packs[1], packs[2])]
    loss = upd[3][0][0]
    for n, d, nm, nv, g in zip(small_names, *[u[1:] for u in upd]):
        shape = given[n].shape
        delta[n], new_m[n], new_v[n], g_out[n] = (t.reshape(shape) for t in (d, nm, nv, g))
    g_out["w_in"], = red_b.finish(upd[0][0])
    update(*BIG[0])

    def shaped(d):
        return [d[n].reshape(given[n].shape) for n in order]

    return (loss, grad_x.reshape(x.shape), *shaped(g_out), *shaped(delta), *shaped(new_m), *shaped(new_v))
```

```python
import math

import numpy as np
import jax
import jax.numpy as jnp
from jax import lax
from jax.experimental import pallas as pl
from jax.experimental.pallas import tpu as pltpu

F32 = jnp.float32
BF16 = jnp.bfloat16
MESH = pl.DeviceIdType.MESH

SEQ = 2048
D_MODEL = 1024
HEAD_DIM = 64
D_ATTN = 512
D_RWKV = 512
D_KV = 128
N_Q_HEADS = 8
N_KV_HEADS = 2
Q_PER_KV = 4
BLOCK = 128
N_BUCKETS = 32
MAX_DISTANCE = 128
LORA_DECAY = 64
LORA_ICLR = 64
LORA_GATE = 128
RWKV_COLS = 3 * D_RWKV + LORA_DECAY + LORA_ICLR + LORA_GATE
P_OFF = D_ATTN + 2 * D_KV
D_IN = P_OFF + RWKV_COLS
D_FF = 4096
NORM_EPS = 1e-6
GN_EPS = 64e-5
NEG_INF = -1e30
N_CHIPS = 4
N_DEV = 8
HEAD_SHIFT = HEAD_DIM.bit_length() - 1
BLOCK_SHIFT = BLOCK.bit_length() - 1

ADAM_LR = 0.001
ADAM_B1 = 0.9
ADAM_B2 = 0.999
ADAM_EPS = 1e-08
ADAM_WD = 0.01
ADAM_STEP = 10

VMEM_LIMIT = 52 * 1024 * 1024
LANES = 128


def _cp(sem=None, vmem=VMEM_LIMIT):
    kw = dict(vmem_limit_bytes=vmem)
    if sem is not None:
        kw["dimension_semantics"] = sem
    return pltpu.CompilerParams(**kw)


def _rows(tr, nc):
    return pl.BlockSpec((tr, nc), lambda i: (i, 0))


def _const(shape):
    return pl.BlockSpec(shape, lambda *_: (0,) * len(shape))


ANY = pl.BlockSpec(memory_space=pl.ANY)


def _split(x, n):
    parts = []
    for _ in range(n - 1):
        h = x.astype(BF16)
        parts.append(h)
        x = x - h.astype(F32)
    parts.append(x.astype(BF16))
    return parts


NN = (((1,), (0,)), ((), ()))
NT = (((1,), (1,)), ((), ()))
TN = (((0,), (0,)), ((), ()))


def _dot(a, b, dn=NN):
    return lax.dot_general(a, b, dn, preferred_element_type=F32)


def _dot_ind(x, ind_bf16, n=3):
    acc = None
    for part in _split(x, n):
        t = _dot(part, ind_bf16)
        acc = t if acc is None else acc + t
    return acc


def _head_ones(n):
    r = lax.broadcasted_iota(jnp.int32, (n, n), 0) >> HEAD_SHIFT
    c = lax.broadcasted_iota(jnp.int32, (n, n), 1) >> HEAD_SHIFT
    return jnp.where(r == c, 1.0, 0.0).astype(BF16)


def _matmul(a, b, mode, name, *, m, n, k, tm, tn, a_spec=None, b_spec=None, out=None):
    dn = {"nn": NN, "nt": NT, "tn": TN}[mode]

    def body(a_ref, b_ref, o_ref):
        o_ref[...] = _dot(a_ref[...], b_ref[...], dn)

    if a_spec is None:
        a_spec = pl.BlockSpec((k, tm), lambda i, j: (0, i)) if mode == "tn" else pl.BlockSpec((tm, k), lambda i, j: (i, 0))
    if b_spec is None:
        b_spec = pl.BlockSpec((tn, k), lambda i, j: (j, 0)) if mode == "nt" else pl.BlockSpec((k, tn), lambda i, j: (0, j))
    return pl.pallas_call(
        body, name=name, grid=(m // tm, n // tn),
        in_specs=[a_spec, b_spec],
        out_specs=pl.BlockSpec((tm, tn), lambda i, j: (i, j)) if out is None else out[1],
        out_shape=jax.ShapeDtypeStruct((m, n) if out is None else out[0], F32),
        compiler_params=_cp(("parallel", "parallel")),
    )(a, b)


def _matmul_nt_shards(a, b_st, name, *, m, n, tm, tn, a_spec, a_piece):
    ks = b_st.shape[2]

    def body(a_ref, b_ref, o_ref):
        acc = _dot(a_piece(a_ref, 0), b_ref[0], NT)
        for s in range(1, N_CHIPS):
            acc = acc + _dot(a_piece(a_ref, s), b_ref[s], NT)
        o_ref[...] = acc

    return pl.pallas_call(
        body, name=name, grid=(m // tm, n // tn),
        in_specs=[a_spec, pl.BlockSpec((N_CHIPS, tn, ks), lambda i, j: (0, j, 0))],
        out_specs=pl.BlockSpec((tm, tn), lambda i, j: (i, j)),
        out_shape=jax.ShapeDtypeStruct((m, n), F32),
        compiler_params=_cp(("parallel", "parallel")),
    )(a, b_st)


def _rstd(x):
    return lax.rsqrt(jnp.mean(x * x, axis=-1, keepdims=True) + NORM_EPS)


def _rms_bwd(x, r, g, dy):
    gy = dy * g
    return r * gy - x * ((r * r * r) * (jnp.sum(x * gy, axis=-1, keepdims=True) / x.shape[-1]))


TR = 256


def _norm_cast(x, g, name):
    def body(x_ref, g_ref, h_ref):
        x = x_ref[...]
        h_ref[...] = (x * _rstd(x) * g_ref[...]).astype(BF16)

    return pl.pallas_call(
        body, name=name, grid=(SEQ // TR,),
        in_specs=[_rows(TR, D_MODEL), _const((1, D_MODEL))],
        out_specs=_rows(TR, D_MODEL),
        out_shape=jax.ShapeDtypeStruct((SEQ, D_MODEL), BF16),
        compiler_params=_cp(("parallel",)),
    )(x, g)


def _mix_norm(x, mix, g2, g3):
    def body(x_ref, mix_ref, g2_ref, g3_ref, x2_ref, h3_ref):
        mixv = mix_ref[...]
        x2 = x_ref[...] + mixv * _rstd(mixv) * g2_ref[...]
        x2_ref[...] = x2
        h3_ref[...] = (x2 * _rstd(x2) * g3_ref[...]).astype(BF16)

    return pl.pallas_call(
        body, name="mix_norm", grid=(SEQ // TR,),
        in_specs=[_rows(TR, D_MODEL), _rows(TR, D_MODEL), _const((1, D_MODEL)), _const((1, D_MODEL))],
        out_specs=[_rows(TR, D_MODEL), _rows(TR, D_MODEL)],
        out_shape=[jax.ShapeDtypeStruct((SEQ, D_MODEL), F32), jax.ShapeDtypeStruct((SEQ, D_MODEL), BF16)],
        compiler_params=_cp(("parallel",)),
    )(x, mix, g2, g3)


def _loss_head(x2, f, g4, target):
    def body(x2_ref, f_ref, g4_ref, t_ref, loss_ref, dy_ref, df_ref, dg_ref):
        i = pl.program_id(0)
        f = f_ref[...]
        g4 = g4_ref[...]
        r = _rstd(f)
        e = x2_ref[...] + f * r * g4 - t_ref[...]
        dy = e * (1.0 / D_MODEL)
        dy_ref[...] = dy
        df_ref[...] = _rms_bwd(f, r, g4, dy).astype(BF16)
        part = 0.5 * jnp.sum(jnp.sum(e * e, axis=-1, keepdims=True), axis=0, keepdims=True) * (1.0 / D_MODEL)
        dg = jnp.sum(dy * f * r, axis=0, keepdims=True)

        @pl.when(i == 0)
        def _():
            loss_ref[...] = jnp.zeros_like(loss_ref)
            dg_ref[...] = jnp.zeros_like(dg_ref)

        loss_ref[...] += jnp.broadcast_to(part, loss_ref.shape)
        dg_ref[...] += dg

    return pl.pallas_call(
        body, name="loss_head", grid=(SEQ // TR,),
        in_specs=[_rows(TR, D_MODEL), _rows(TR, D_MODEL), _const((1, D_MODEL)), _rows(TR, D_MODEL)],
        out_specs=[_const((8, LANES)), _rows(TR, D_MODEL), _rows(TR, D_MODEL), _const((1, D_MODEL))],
        out_shape=[jax.ShapeDtypeStruct((8, LANES), F32), jax.ShapeDtypeStruct((SEQ, D_MODEL), F32),
                   jax.ShapeDtypeStruct((SEQ, D_MODEL), BF16), jax.ShapeDtypeStruct((1, D_MODEL), F32)],
        compiler_params=_cp(("arbitrary",)),
    )(x2, f, g4, target)


def _mid_bwd(x2, mix, dy, dh3, g2, g3):
    def body(x2_ref, mix_ref, dy_ref, dh3_ref, g2_ref, g3_ref, dx2_ref, dmix_ref, dg2_ref, dg3_ref):
        i = pl.program_id(0)
        x2 = x2_ref[...]
        mixv = mix_ref[...]
        dh3 = dh3_ref[...]
        r3 = _rstd(x2)
        dx2 = dy_ref[...] + _rms_bwd(x2, r3, g3_ref[...], dh3)
        dx2_ref[...] = dx2
        r2 = _rstd(mixv)
        dmix_ref[...] = _rms_bwd(mixv, r2, g2_ref[...], dx2).astype(BF16)

        @pl.when(i == 0)
        def _():
            dg2_ref[...] = jnp.zeros_like(dg2_ref)
            dg3_ref[...] = jnp.zeros_like(dg3_ref)

        dg3_ref[...] += jnp.sum(dh3 * x2 * r3, axis=0, keepdims=True)
        dg2_ref[...] += jnp.sum(dx2 * mixv * r2, axis=0, keepdims=True)

    return pl.pallas_call(
        body, name="mid_bwd", grid=(SEQ // TR,),
        in_specs=[_rows(TR, D_MODEL)] * 4 + [_const((1, D_MODEL))] * 2,
        out_specs=[_rows(TR, D_MODEL), _rows(TR, D_MODEL), _const((1, D_MODEL)), _const((1, D_MODEL))],
        out_shape=[jax.ShapeDtypeStruct((SEQ, D_MODEL), F32), jax.ShapeDtypeStruct((SEQ, D_MODEL), BF16),
                   jax.ShapeDtypeStruct((1, D_MODEL), F32), jax.ShapeDtypeStruct((1, D_MODEL), F32)],
        compiler_params=_cp(("arbitrary",)),
    )(x2, mix, dy, dh3, g2, g3)


def _first_bwd(x, dx2, dh1, g1):
    def body(x_ref, dx2_ref, dh1_ref, g1_ref, dx_ref, dg1_ref):
        i = pl.program_id(0)
        x = x_ref[...]
        dh1 = dh1_ref[...]
        r = _rstd(x)
        dx_ref[...] = dx2_ref[...] + _rms_bwd(x, r, g1_ref[...], dh1)

        @pl.when(i == 0)
        def _():
            dg1_ref[...] = jnp.zeros_like(dg1_ref)

        dg1_ref[...] += jnp.sum(dh1 * x * r, axis=0, keepdims=True)

    return pl.pallas_call(
        body, name="first_bwd", grid=(SEQ // TR,),
        in_specs=[_rows(TR, D_MODEL)] * 3 + [_const((1, D_MODEL))],
        out_specs=[_rows(TR, D_MODEL), _const((1, D_MODEL))],
        out_shape=[jax.ShapeDtypeStruct((SEQ, D_MODEL), F32), jax.ShapeDtypeStruct((1, D_MODEL), F32)],
        compiler_params=_cp(("arbitrary",)),
    )(x, dx2, dh1, g1)


TC = 256
N_CB = D_FF // TC
GELU_C = math.sqrt(2.0 / math.pi)


def _shift_down(u, s):
    rolled = pltpu.roll(u, s, 0)
    row = lax.broadcasted_iota(jnp.int32, u.shape, 0)
    return jnp.where(row >= s, rolled, 0.0)


def _shift_up(u, s):
    n = u.shape[0]
    rolled = pltpu.roll(u, n - s, 0)
    row = lax.broadcasted_iota(jnp.int32, u.shape, 0)
    return jnp.where(row < n - s, rolled, 0.0)


def _conv3(u, w, b):
    return b + w[0:1] * _shift_down(u, 2) + w[1:2] * _shift_down(u, 1) + w[2:3] * u


def _gelu_and_grad(x):
    inner = GELU_C * (x + 0.044715 * (x * x * x))
    t = jnp.tanh(inner)
    gelu = 0.5 * x * (1.0 + t)
    dgelu = 0.5 * (1.0 + t) + 0.5 * x * (1.0 - t * t) * (GELU_C * (1.0 + 3 * 0.044715 * (x * x)))
    return gelu, dgelu


def _ffn_specs():
    col = lambda off: pl.BlockSpec((SEQ, TC), lambda *g: (0, g[-1] + off))
    w = lambda off: pl.BlockSpec((3, TC), lambda *g: (0, g[-1] + off))
    b = lambda off: pl.BlockSpec((1, TC), lambda *g: (0, g[-1] + off))
    return col, w, b


def _ffn_up_act(h3, wup_st, conv_w, conv_b):
    col, w, b = _ffn_specs()
    per_shard = wup_st.shape[2] // TC

    def body(h_ref, upg_ref, upv_ref, wg_ref, wv_ref, bg_ref, bv_ref, ug_ref, uv_ref, act_ref):
        h = h_ref[...]
        ug = _dot(h, upg_ref[...])
        uv = _dot(h, upv_ref[...])
        ug_ref[...] = ug
        uv_ref[...] = uv
        gate = _conv3(ug, wg_ref[...], bg_ref[...])
        val = _conv3(uv, wv_ref[...], bv_ref[...])
        act_ref[...] = (_gelu_and_grad(gate)[0] * val).astype(BF16)

    return pl.pallas_call(
        body, name="ffn_up_act", grid=(N_CB,),
        in_specs=[_const((SEQ, D_MODEL)),
                  pl.BlockSpec((None, D_MODEL, TC), lambda j: (j // per_shard, 0, j % per_shard)),
                  pl.BlockSpec((None, D_MODEL, TC), lambda j: (2 + j // per_shard, 0, j % per_shard)),
                  w(0), w(N_CB), b(0), b(N_CB)],
        out_specs=[col(0)] * 3,
        out_shape=[jax.ShapeDtypeStruct((SEQ, D_FF), F32)] * 2 + [jax.ShapeDtypeStruct((SEQ, D_FF), BF16)],
        compiler_params=_cp(("parallel",)),
    )(h3, wup_st, wup_st, conv_w, conv_w, conv_b, conv_b)


def _ffn_act_bwd(u_gate, u_val, df, wdown, conv_w, conv_b):
    col, w, b = _ffn_specs()
    both = lambda rows: pl.BlockSpec((2, rows, TC), lambda j: (0, 0, j))

    def body(ug_ref, uv_ref, df_ref, wd_ref, wg_ref, wv_ref, bg_ref, bv_ref, du_ref, dw_ref, db_ref):
        da = _dot(df_ref[...], wd_ref[...], NT)
        ug, uv = ug_ref[...], uv_ref[...]
        wg, wv = wg_ref[...], wv_ref[...]
        gate = _conv3(ug, wg, bg_ref[...])
        val = _conv3(uv, wv, bv_ref[...])
        gelu, dgelu = _gelu_and_grad(gate)
        for h, (duc, uh, wh) in enumerate(((da * val * dgelu, ug, wg), (da * gelu, uv, wv))):
            up1, up2 = _shift_up(duc, 1), _shift_up(duc, 2)
            du_ref[h] = (wh[2:3] * duc + wh[1:2] * up1 + wh[0:1] * up2).astype(BF16)
            db_ref[h] = jnp.sum(duc, axis=0, keepdims=True)
            dw_ref[h] = jnp.concatenate(
                [jnp.sum(up2 * uh, axis=0, keepdims=True), jnp.sum(up1 * uh, axis=0, keepdims=True),
                 jnp.sum(duc * uh, axis=0, keepdims=True)], axis=0)

    return pl.pallas_call(
        body, name="ffn_act_bwd", grid=(N_CB,),
        in_specs=[col(0), col(0), _const((SEQ, D_MODEL)), pl.BlockSpec((TC, D_MODEL), lambda j: (j, 0)),
                  w(0), w(N_CB), b(0), b(N_CB)],
        out_specs=[both(SEQ), both(3), both(1)],
        out_shape=[jax.ShapeDtypeStruct((2, SEQ, D_FF), BF16), jax.ShapeDtypeStruct((2, 3, D_FF), F32),
                   jax.ShapeDtypeStruct((2, 1, D_FF), F32)],
        compiler_params=_cp(("parallel",)),
    )(u_gate, u_val, df, wdown, conv_w, conv_w, conv_b, conv_b)


def _t5_onehot():
    rel = (np.arange(BLOCK)[:, None] + BLOCK) - np.arange(2 * BLOCK)[None, :]
    n = np.maximum(rel, 0)
    max_exact = N_BUCKETS // 2
    large = max_exact + (np.log(np.maximum(n, 1).astype(np.float32) / np.float32(max_exact))
                         / np.float32(math.log(MAX_DISTANCE / max_exact))
                         * np.float32(N_BUCKETS - max_exact)).astype(np.int32)
    large = np.minimum(large, N_BUCKETS - 1)
    bucket = np.where(n < max_exact, n, large).reshape(-1)
    return (bucket[None, :] == np.arange(N_BUCKETS)[:, None]).astype(np.float32)


N_REL = BLOCK * 2 * BLOCK


def _bias_table(rel_bias_t, onehot):
    def body(rb_ref, oh_ref, o_ref):
        o_ref[...] = _dot_ind(rb_ref[...], oh_ref[...])

    return pl.pallas_call(
        body, name="bias_table", grid=(1,),
        in_specs=[_const((N_Q_HEADS, N_BUCKETS)), _const((N_BUCKETS, N_REL))],
        out_specs=_const((N_Q_HEADS, N_REL)),
        out_shape=jax.ShapeDtypeStruct((N_Q_HEADS, N_REL), F32),
        compiler_params=_cp(("arbitrary",)),
    )(rel_bias_t, onehot)


def _bias_table_bwd(dbias, onehot):
    def body(db_ref, oh_ref, o_ref):
        acc = None
        for part in _split(db_ref[...], 3):
            t = _dot(part, oh_ref[...], NT)
            acc = t if acc is None else acc + t
        o_ref[...] = acc

    return pl.pallas_call(
        body, name="bias_table_bwd", grid=(1,),
        in_specs=[_const((N_Q_HEADS, N_REL)), _const((N_BUCKETS, N_REL))],
        out_specs=_const((N_Q_HEADS, N_BUCKETS)),
        out_shape=jax.ShapeDtypeStruct((N_Q_HEADS, N_BUCKETS), F32),
        compiler_params=_cp(("arbitrary",)),
    )(dbias, onehot)


def _attn_pieces(n, q, kvp, kvc, bias_ref, sinks_ref, hk):
    qi = lax.broadcasted_iota(jnp.int32, (BLOCK, 2 * BLOCK), 0)
    kj = lax.broadcasted_iota(jnp.int32, (BLOCK, 2 * BLOCK), 1)
    rel = qi + BLOCK - kj
    first_key = jnp.where(n > 0, 0, BLOCK)
    ok = jnp.where(rel >= 0, jnp.where(rel < BLOCK, jnp.where(kj >= first_key, 1.0, 0.0), 0.0), 0.0)
    ok4 = jnp.concatenate([ok] * Q_PER_KV, axis=0) > 0.5
    c0 = hk * HEAD_DIM
    kcat = jnp.concatenate([kvp[:, c0:c0 + HEAD_DIM], kvc[:, c0:c0 + HEAD_DIM]], axis=0).astype(BF16)
    vcat = jnp.concatenate([kvp[:, D_KV + c0:D_KV + c0 + HEAD_DIM], kvc[:, D_KV + c0:D_KV + c0 + HEAD_DIM]],
                           axis=0).astype(BF16)
    q0 = hk * Q_PER_KV * HEAD_DIM
    qs = jnp.concatenate([q[:, q0 + g * HEAD_DIM:q0 + (g + 1) * HEAD_DIM] for g in range(Q_PER_KV)],
                         axis=0).astype(BF16)
    s = _dot(qs, kcat, NT) * (HEAD_DIM ** -0.5) + bias_ref[hk]
    s = jnp.where(ok4, s, NEG_INF)
    row = lax.broadcasted_iota(jnp.int32, (Q_PER_KV * BLOCK, 1), 0)
    sink = jnp.zeros((Q_PER_KV * BLOCK, 1), F32)
    for g in range(Q_PER_KV):
        sink = jnp.where((row >> BLOCK_SHIFT) == g, sinks_ref[hk * Q_PER_KV + g], sink)
    m = jnp.maximum(jnp.max(s, axis=-1, keepdims=True), sink)
    p = jnp.exp(s - m)
    es = jnp.exp(sink - m)
    inv = 1.0 / (jnp.sum(p, axis=-1, keepdims=True) + es)
    return qs, kcat, vcat, p * inv, es * inv


def _attn_in_specs():
    return [pl.BlockSpec((BLOCK, D_ATTN), lambda n: (n, 0)),
            pl.BlockSpec((BLOCK, 2 * D_KV), lambda n: (jnp.maximum(n - 1, 0), D_ATTN // (2 * D_KV))),
            pl.BlockSpec((BLOCK, 2 * D_KV), lambda n: (n, D_ATTN // (2 * D_KV))),
            _const((N_KV_HEADS, Q_PER_KV * BLOCK, 2 * BLOCK)),
            pl.BlockSpec(memory_space=pltpu.SMEM)]


def _unstack_heads(t):
    return jnp.concatenate([t[g * BLOCK:(g + 1) * BLOCK] for g in range(Q_PER_KV)], axis=1)


def _attn_fwd(proj, bias, sinks):
    def body(q_ref, kvp_ref, kvc_ref, bias_ref, sinks_ref, o_ref):
        n = pl.program_id(0)
        q, kvp, kvc = q_ref[...], kvp_ref[...], kvc_ref[...]
        outs = []
        for hk in range(N_KV_HEADS):
            _, _, vcat, probs, _ = _attn_pieces(n, q, kvp, kvc, bias_ref, sinks_ref, hk)
            outs.append(_unstack_heads(_dot(probs.astype(BF16), vcat)))
        o_ref[...] = jnp.concatenate(outs, axis=1)

    return pl.pallas_call(
        body, name="attn_fwd", grid=(SEQ // BLOCK,),
        in_specs=_attn_in_specs(),
        out_specs=pl.BlockSpec((BLOCK, D_ATTN), lambda n: (n, 0)),
        out_shape=jax.ShapeDtypeStruct((SEQ, D_ATTN), F32),
        compiler_params=_cp(("parallel",)),
    )(proj, proj, proj, bias, sinks)


def _attn_bwd(proj, bias, sinks, dcat):
    nb = SEQ // BLOCK

    def body(q_ref, kvp_ref, kvc_ref, bias_ref, sinks_ref, do_ref, dq_ref, dkv_ref, dbias_ref, dsink_ref, dsacc):
        n = pl.program_id(0)

        @pl.when(n == 0)
        def _():
            dkv_ref[...] = jnp.zeros_like(dkv_ref)
            dbias_ref[...] = jnp.zeros_like(dbias_ref)
            dsacc[...] = jnp.zeros_like(dsacc)

        q, kvp, kvc = q_ref[...], kvp_ref[...], kvc_ref[...]
        do_all = do_ref[...]
        dqs, dks, dvs = [], [], []
        for hk in range(N_KV_HEADS):
            qs, kcat, vcat, probs, psink = _attn_pieces(n, q, kvp, kvc, bias_ref, sinks_ref, hk)
            q0 = hk * Q_PER_KV * HEAD_DIM
            do = jnp.concatenate([do_all[:, q0 + g * HEAD_DIM:q0 + (g + 1) * HEAD_DIM] for g in range(Q_PER_KV)],
                                 axis=0).astype(BF16)
            dprobs = _dot(do, vcat, NT)
            dvs.append(_dot(probs.astype(BF16), do, TN))
            rowdot = jnp.sum(probs * dprobs, axis=-1, keepdims=True)
            ds = probs * (dprobs - rowdot)
            dsacc[hk] += -psink * rowdot
            dbias_ref[hk] += ds
            dsb = (ds * (HEAD_DIM ** -0.5)).astype(BF16)
            dqs.append(_unstack_heads(_dot(dsb, kcat)))
            dks.append(_dot(dsb, qs, TN))
        dq_ref[...] = jnp.concatenate(dqs, axis=1)
        upd = jnp.concatenate(dks + dvs, axis=1)
        cur = pl.multiple_of(n * BLOCK, BLOCK)
        dkv_ref[pl.ds(cur, BLOCK), :] += upd[BLOCK:]

        @pl.when(n > 0)
        def _():
            prev = pl.multiple_of((n - 1) * BLOCK, BLOCK)
            dkv_ref[pl.ds(prev, BLOCK), :] += upd[:BLOCK]

        @pl.when(n == nb - 1)
        def _():
            for hk in range(N_KV_HEADS):
                for g in range(Q_PER_KV):
                    tot = jnp.sum(dsacc[hk, g * BLOCK:(g + 1) * BLOCK, :], axis=0, keepdims=True)
                    h = hk * Q_PER_KV + g
                    dsink_ref[h:h + 1, :] = jnp.broadcast_to(tot, (1, LANES))

    return pl.pallas_call(
        body, name="attn_bwd", grid=(nb,),
        in_specs=_attn_in_specs() + [pl.BlockSpec((BLOCK, D_ATTN), lambda n: (n, 0))],
        out_specs=[pl.BlockSpec((BLOCK, D_ATTN), lambda n: (n, 0)), _const((SEQ, 2 * D_KV)),
                   _const((N_KV_HEADS, Q_PER_KV * BLOCK, 2 * BLOCK)), _const((N_Q_HEADS, LANES))],
        out_shape=[jax.ShapeDtypeStruct((SEQ, D_ATTN), F32), jax.ShapeDtypeStruct((SEQ, 2 * D_KV), F32),
                   jax.ShapeDtypeStruct((N_KV_HEADS, Q_PER_KV * BLOCK, 2 * BLOCK), F32),
                   jax.ShapeDtypeStruct((N_Q_HEADS, LANES), F32)],
        scratch_shapes=[pltpu.VMEM((N_KV_HEADS, Q_PER_KV * BLOCK, 1), F32)],
        compiler_params=_cp(("arbitrary",)),
    )(proj, proj, proj, bias, sinks, dcat)


@jax.custom_vjp
def _head_sum(x):
    ones = _head_ones(LANES)
    return jnp.concatenate([_dot_ind(x[:, c:c + LANES], ones, 2) for c in range(0, x.shape[-1], LANES)], axis=1)


_head_sum.defvjp(lambda x: (_head_sum(x), None), lambda _, ct: (_head_sum(ct),))


@jax.custom_vjp
def _bdot(a, w):
    return _dot(a.astype(BF16), w.astype(BF16))


def _bdot_bwd(res, ct):
    a, w = res
    ctb = ct.astype(BF16)
    return _dot(ctb, w.astype(BF16), NT), _dot(a.astype(BF16), ctb, TN)


_bdot.defvjp(lambda a, w: (_bdot(a, w), (a, w)), _bdot_bwd)


def _sigmoid(x):
    return 0.5 * (jnp.tanh(0.5 * x) + 1.0)


def _softplus(x):
    return jnp.maximum(x, 0.0) + jnp.log(1.0 + jnp.exp(-jnp.abs(x)))


def _rwkv_core(r, k, v, zwa, zg, w0, wdu, a0, wiu, wgu, k_k, k_a):
    w_log = -_softplus(-(w0 + _bdot(jnp.tanh(zwa), wdu))) - 0.5
    decay = jnp.exp(-jnp.exp(w_log))
    a = _sigmoid(a0 + _bdot(zwa, wiu))
    g = _bdot(_sigmoid(zg), wgu)
    kk = k * k_k
    kk = kk / jnp.maximum(jnp.sqrt(_head_sum(kk * kk)), 1e-12)
    k2 = k * (1.0 + (a - 1.0) * k_a)
    return r, decay, k2, v, -kk, kk * a, g


def _rwkv_out(o, r, k2, v, g, lng, lnb, rk):
    mu = _head_sum(o) * (1.0 / HEAD_DIM)
    d = o - mu
    var = _head_sum(d * d) * (1.0 / HEAD_DIM)
    on = d * lax.rsqrt(var + GN_EPS) * lng + lnb
    bonus = _head_sum(r * k2 * rk) * v
    return (on + bonus) * g


P_SPLITS = (0, 512, 1024, 1536, 1664, 1792)
N_PREP_PARAMS = 7
HALO = 8


def _shifted_pieces(i, p_ref, halo_ref, mix_ref):
    p = p_ref[:, P_OFF:]
    prev_row = halo_ref[HALO - 1:HALO, P_OFF:] * jnp.where(i > 0, 1.0, 0.0)
    row = lax.broadcasted_iota(jnp.int32, p.shape, 0)
    pprev = jnp.where(row == 0, prev_row, pltpu.roll(p, 1, 0))
    delta = pprev - p
    ps = p + delta * mix_ref[...]
    return [ps[:, a:b] for a, b in zip(P_SPLITS[:-1], P_SPLITS[1:])], delta


def _prep_in_specs():
    return [_rows(TR, D_IN),
            pl.BlockSpec((HALO, D_IN), lambda i: (jnp.maximum(i * (TR // HALO) - 1, 0), 0)),
            _const((1, RWKV_COLS)), _const((1, D_RWKV)), _const((LANES, D_RWKV)), _const((1, D_RWKV)),
            _const((LANES, D_RWKV)), _const((LANES, D_RWKV)), _const((1, D_RWKV)), _const((1, D_RWKV))]


def _rwkv_prep(proj, mix, prm):
    def body(p_ref, halo_ref, mix_ref, *refs):
        prm_refs, outs = refs[:N_PREP_PARAMS], refs[N_PREP_PARAMS:]
        pieces, _ = _shifted_pieces(pl.program_id(0), p_ref, halo_ref, mix_ref)
        vals = _rwkv_core(*pieces, *[t[...] for t in prm_refs])
        for ref, val in zip(outs, vals):
            ref[...] = val

    return pl.pallas_call(
        body, name="rwkv_prep", grid=(SEQ // TR,),
        in_specs=_prep_in_specs(),
        out_specs=[_rows(TR, D_RWKV)] * 7,
        out_shape=[jax.ShapeDtypeStruct((SEQ, D_RWKV), F32)] * 7,
        compiler_params=_cp(("parallel",)),
    )(proj, proj, mix, *prm)


def _rwkv_prep_bwd(proj, mix, prm, cts):
    def body(p_ref, halo_ref, mix_ref, *refs):
        i = pl.program_id(0)
        prm_refs = refs[:N_PREP_PARAMS]
        ct_refs = refs[N_PREP_PARAMS:N_PREP_PARAMS + 10]
        dps_ref, dmix_ref = refs[N_PREP_PARAMS + 10:N_PREP_PARAMS + 12]
        dprm_refs = refs[N_PREP_PARAMS + 12:]
        pieces, delta = _shifted_pieces(i, p_ref, halo_ref, mix_ref)
        _, vjp = jax.vjp(_rwkv_core, *pieces, *[t[...] for t in prm_refs])
        dr1, dr2, dw, dk1, dk2, dv1, dv2, dkkn, db, dg = [t[...] for t in ct_refs]
        grads = vjp((dr1 + dr2, dw, dk1 + dk2, dv1 + dv2, dkkn, db, dg))
        dps = jnp.concatenate(grads[:5], axis=1)
        dps_ref[...] = dps

        @pl.when(i == 0)
        def _():
            dmix_ref[...] = jnp.zeros_like(dmix_ref)
            for ref in dprm_refs:
                ref[...] = jnp.zeros_like(ref)

        dmix_ref[...] += jnp.sum(dps * delta, axis=0, keepdims=True)
        for ref, gval in zip(dprm_refs, grads[5:]):
            ref[...] += gval

    prm_shapes = [(1, D_RWKV), (LANES, D_RWKV), (1, D_RWKV), (LANES, D_RWKV), (LANES, D_RWKV), (1, D_RWKV), (1, D_RWKV)]
    return pl.pallas_call(
        body, name="rwkv_prep_bwd", grid=(SEQ // TR,),
        in_specs=_prep_in_specs() + [_rows(TR, D_RWKV)] * 10,
        out_specs=[_rows(TR, RWKV_COLS), _const((1, RWKV_COLS))] + [_const(s) for s in prm_shapes],
        out_shape=[jax.ShapeDtypeStruct((SEQ, RWKV_COLS), F32), jax.ShapeDtypeStruct((1, RWKV_COLS), F32)]
        + [jax.ShapeDtypeStruct(s, F32) for s in prm_shapes],
        compiler_params=_cp(("arbitrary",)),
    )(proj, proj, mix, *prm, *cts)


def _rwkv_post(o, r, k2, v, g, lng, lnb, rk, attn):
    def body(o_ref, r_ref, k_ref, v_ref, g_ref, lng_ref, lnb_ref, rk_ref, attn_ref, cat_ref):
        rw = _rwkv_out(*[t[...] for t in (o_ref, r_ref, k_ref, v_ref, g_ref, lng_ref, lnb_ref, rk_ref)])
        cat_ref[...] = jnp.concatenate([attn_ref[...], rw], axis=1).astype(BF16)

    return pl.pallas_call(
        body, name="rwkv_post", grid=(SEQ // TR,),
        in_specs=[_rows(TR, D_RWKV)] * 5 + [_const((1, D_RWKV))] * 3 + [_rows(TR, D_ATTN)],
        out_specs=_rows(TR, D_MODEL),
        out_shape=jax.ShapeDtypeStruct((SEQ, D_MODEL), BF16),
        compiler_params=_cp(("parallel",)),
    )(o, r, k2, v, g, lng, lnb, rk, attn)


def _rwkv_post_bwd(o, r, k2, v, g, lng, lnb, rk, dcat):
    def body(o_ref, r_ref, k_ref, v_ref, g_ref, lng_ref, lnb_ref, rk_ref, dcat_ref,
             do_ref, dr_ref, dk_ref, dv_ref, dg_ref, dlng_ref, dlnb_ref, drk_ref):
        i = pl.program_id(0)
        args = [t[...] for t in (o_ref, r_ref, k_ref, v_ref, g_ref, lng_ref, lnb_ref, rk_ref)]
        _, vjp = jax.vjp(_rwkv_out, *args)
        grads = vjp(dcat_ref[:, D_ATTN:])
        for ref, gval in zip((do_ref, dr_ref, dk_ref, dv_ref, dg_ref), grads[:5]):
            ref[...] = gval

        @pl.when(i == 0)
        def _():
            for ref in (dlng_ref, dlnb_ref, drk_ref):
                ref[...] = jnp.zeros_like(ref)

        for ref, gval in zip((dlng_ref, dlnb_ref, drk_ref), grads[5:]):
            ref[...] += gval

    return pl.pallas_call(
        body, name="rwkv_post_bwd", grid=(SEQ // TR,),
        in_specs=[_rows(TR, D_RWKV)] * 5 + [_const((1, D_RWKV))] * 3 + [_rows(TR, D_MODEL)],
        out_specs=[_rows(TR, D_RWKV)] * 5 + [_const((1, D_RWKV))] * 3,
        out_shape=[jax.ShapeDtypeStruct((SEQ, D_RWKV), F32)] * 5 + [jax.ShapeDtypeStruct((1, D_RWKV), F32)] * 3,
        compiler_params=_cp(("arbitrary",)),
    )(o, r, k2, v, g, lng, lnb, rk, dcat)


def _assemble_dproj(dq, dkv, dps, mix):
    last = SEQ // HALO - 1

    def body(dq_ref, dkv_ref, dps_ref, nxt_ref, mix_ref, o_ref):
        i = pl.program_id(0)
        dps = dps_ref[...]
        mixv = mix_ref[...]
        nxt_row = nxt_ref[0:1, :] * jnp.where(i < SEQ // TR - 1, 1.0, 0.0)
        row = lax.broadcasted_iota(jnp.int32, dps.shape, 0)
        up = jnp.where(row == TR - 1, nxt_row, pltpu.roll(dps, TR - 1, 0))
        dp = dps * (1.0 - mixv) + up * mixv
        o_ref[...] = jnp.concatenate([dq_ref[...], dkv_ref[...], dp], axis=1).astype(BF16)

    return pl.pallas_call(
        body, name="assemble_dproj", grid=(SEQ // TR,),
        in_specs=[_rows(TR, D_ATTN), _rows(TR, 2 * D_KV), _rows(TR, RWKV_COLS),
                  pl.BlockSpec((HALO, RWKV_COLS), lambda i: (jnp.minimum((i + 1) * (TR // HALO), last), 0)),
                  _const((1, RWKV_COLS))],
        out_specs=_rows(TR, D_IN),
        out_shape=jax.ShapeDtypeStruct((SEQ, D_IN), BF16),
        compiler_params=_cp(("parallel",)),
    )(dq, dkv, dps, dps, mix)


N_PAIR = D_RWKV // LANES
CHUNK = 64
N_CHUNK = SEQ // CHUNK
GROUP = 8
STATE = (N_PAIR, HEAD_DIM, LANES)


def _lane_sums(lhs_tiles, ones2):
    out = _dot(jnp.concatenate(lhs_tiles, axis=0), ones2)
    return [out[i * HEAD_DIM:(i + 1) * HEAD_DIM] for i in range(len(lhs_tiles))]


def _seg_sum(xs, ones2):
    return _lane_sums([jnp.concatenate(_split(x, 2), axis=1) for x in xs], ones2)


def _seg_sum_rows(xs, ones2):
    out = _dot(jnp.concatenate(_split(jnp.concatenate(xs, axis=0), 2), axis=1), ones2)
    return [out[i * GROUP:(i + 1) * GROUP] for i in range(len(xs))]


def _col_form(rows, diag, ones2):
    zero = jnp.zeros((HEAD_DIM, LANES), BF16)
    tiles = []
    for row in rows:
        hi = row.astype(BF16)
        lo = (row - hi.astype(F32)).astype(BF16)
        tiles.append(jnp.concatenate(
            [jnp.where(diag, jnp.broadcast_to(part, (HEAD_DIM, LANES)), zero) for part in (hi, lo)], axis=1))
    return _lane_sums(tiles, ones2)


def _scan_consts():
    ones2 = jnp.concatenate([_head_ones(LANES)] * 2, axis=0)
    sub = lax.broadcasted_iota(jnp.int32, (HEAD_DIM, LANES), 0)
    lane_in_head = lax.broadcasted_iota(jnp.int32, (HEAD_DIM, LANES), 1) & (HEAD_DIM - 1)
    return ones2, lane_in_head == sub, lane_in_head


def _rows_of_columns(tile):
    t = tile.T
    return jnp.concatenate([t[:CHUNK], t[HEAD_DIM:HEAD_DIM + CHUNK]], axis=1)


def _pair(j):
    return slice(j * LANES, (j + 1) * LANES)


def _scan_fwd(r, w, k, v, kkn, b):
    def body(r_ref, w_ref, k_ref, v_ref, kkn_ref, b_ref, o_ref, st_ref, sa_ref, s_scr):
        c = pl.program_id(0)
        ones2, diag, lane_in_head = _scan_consts()

        @pl.when(c == 0)
        def _():
            s_scr[...] = jnp.zeros_like(s_scr)

        def group(gi, carry):
            row0 = pl.multiple_of(gi * GROUP, GROUP)
            states, ocols = list(carry[:N_PAIR]), list(carry[N_PAIR:])
            tiles = [[t[pl.ds(row0, GROUP), _pair(j)] for t in (r_ref, w_ref, k_ref, v_ref, kkn_ref, b_ref)]
                     for j in range(N_PAIR)]
            def row(j, name, u):
                return tiles[j]["rwkvnb".index(name)][u:u + 1]

            def emit_out(u, after):
                outs = _seg_sum([s[j] * row(j, "r", u + d) for d, s in enumerate(after) for j in range(N_PAIR)], ones2)
                for d in range(2):
                    here = lane_in_head == gi * GROUP + u + d
                    for j in range(N_PAIR):
                        ocols[j] = jnp.where(here, outs[d * N_PAIR + j], ocols[j])

            def vcols_of(u):
                cols = _col_form([row(j, "v", u + d) for d in range(2) for j in range(N_PAIR)], diag, ones2)
                return cols[:N_PAIR], cols[N_PAIR:]

            n_next = [pltpu.roll(tiles[j][4], GROUP - 1, 0) for j in range(N_PAIR)]
            dots = _seg_sum_rows([tiles[j][5] * n_next[j] for j in range(N_PAIR)]
                                 + [tiles[j][2] * n_next[j] for j in range(N_PAIR)], ones2)
            b_n, k_n = dots[:N_PAIR], dots[N_PAIR:]
            w_n = [tiles[j][1] * n_next[j] for j in range(N_PAIR)]

            vcols = vcols_of(0)
            after = None
            for u in range(0, GROUP, 2):
                prods = _seg_sum([states[j] * row(j, "n", u) for j in range(N_PAIR)]
                                 + [states[j] * w_n[j][u:u + 1] for j in range(N_PAIR)], ones2)
                if after is not None:
                    emit_out(u - 2, after)
                nxt = vcols_of(u + 2) if u + 2 < GROUP else None
                first, second = [], []
                for j in range(N_PAIR):
                    sa1 = prods[j]
                    sa2 = prods[N_PAIR + j] + sa1 * b_n[j][u:u + 1] + vcols[0][j] * k_n[j][u:u + 1]
                    s1 = states[j] * row(j, "w", u) + sa1 * row(j, "b", u) + vcols[0][j] * row(j, "k", u)
                    s2 = s1 * row(j, "w", u + 1) + sa2 * row(j, "b", u + 1) + vcols[1][j] * row(j, "k", u + 1)
                    st_ref[row0 + u, j] = s1
                    sa_ref[row0 + u, j] = sa1
                    st_ref[row0 + u + 1, j] = s2
                    sa_ref[row0 + u + 1, j] = sa2
                    first.append(s1)
                    second.append(s2)
                    states[j] = s2
                after, vcols = (first, second), nxt
            emit_out(GROUP - 2, after)
            return tuple(states + ocols)

        zero = jnp.zeros((HEAD_DIM, LANES), F32)
        fin = lax.fori_loop(0, CHUNK // GROUP, group, tuple(s_scr[j] for j in range(N_PAIR)) + (zero,) * N_PAIR)
        for j in range(N_PAIR):
            s_scr[j] = fin[j]
            o_ref[:, _pair(j)] = _rows_of_columns(fin[N_PAIR + j])

    blk = pl.BlockSpec((CHUNK, D_RWKV), lambda c: (c, 0))
    per_step = pl.BlockSpec((CHUNK,) + STATE, lambda c: (c, 0, 0, 0))
    return pl.pallas_call(
        body, name="rwkv_scan_fwd", grid=(N_CHUNK,),
        in_specs=[blk] * 6,
        out_specs=[blk, per_step, per_step],
        out_shape=[jax.ShapeDtypeStruct((SEQ, D_RWKV), F32)] + [jax.ShapeDtypeStruct((SEQ,) + STATE, F32)] * 2,
        scratch_shapes=[pltpu.VMEM(STATE, F32)],
        compiler_params=_cp(("arbitrary",)),
    )(r, w, k, v, kkn, b)


def _scan_bwd(r, w, k, v, kkn, b, do, states, sas, ds_in, prev, name, first_chunk, n_chunks):
    top = first_chunk + n_chunks - 1

    def body(r_ref, w_ref, k_ref, v_ref, kkn_ref, b_ref, do_ref, st_ref, before_ref, sa_ref, ds_in_ref, *rest):
        dr_ref, dw_ref, dk_ref, dv_ref, dkkn_ref, db_ref, ds_out_ref, ds_scr = rest[-8:]
        i = pl.program_id(0)
        ones2, diag, lane_in_head = _scan_consts()

        @pl.when(i == 0)
        def _():
            ds_scr[...] = ds_in_ref[...]

        entry = [before_ref[0, j] * jnp.where(i < top, 1.0, 0.0) for j in range(N_PAIR)]

        def reverse(gr, carry):
            gi = CHUNK // GROUP - 1 - gr
            row0 = pl.multiple_of(gi * GROUP, GROUP)
            dstates, dvcols = list(carry[:N_PAIR]), list(carry[N_PAIR:])
            tiles = [[t[pl.ds(row0, GROUP), _pair(j)]
                      for t in (r_ref, w_ref, k_ref, v_ref, kkn_ref, b_ref, do_ref)] for j in range(N_PAIR)]
            rows = [[[None] * GROUP for _ in range(5)] for _ in range(N_PAIR)]

            def row(j, name, u):
                return tiles[j]["rwkvnbd".index(name)][u:u + 1]

            def cols_of(u):
                cols = _col_form([row(j, name, u - d) for d in range(2) for name in "dv" for j in range(N_PAIR)],
                                 diag, ones2)
                return [[(cols[(2 * d) * N_PAIR + j], cols[(2 * d + 1) * N_PAIR + j]) for j in range(N_PAIR)]
                        for d in range(2)]

            def emit_dv(u, dsps):
                outs = _seg_sum([dsp[j] * row(j, "k", u - d) for d, dsp in enumerate(dsps) for j in range(N_PAIR)], ones2)
                for d in range(2):
                    here = lane_in_head == gi * GROUP + u - d
                    for j in range(N_PAIR):
                        dvcols[j] = jnp.where(here, outs[d * N_PAIR + j], dvcols[j])

            b_prev = [pltpu.roll(tiles[j][5], 1, 0) for j in range(N_PAIR)]
            dots = _seg_sum_rows([tiles[j][4] * b_prev[j] for j in range(N_PAIR)]
                                 + [tiles[j][0] * tiles[j][5] for j in range(N_PAIR)], ones2)
            n_b, r_b = dots[:N_PAIR], dots[N_PAIR:]
            w_b = [tiles[j][1] * b_prev[j] for j in range(N_PAIR)]

            def outputs(u, j, dsp, dsa, docol, vcol):
                tl = gi * GROUP + u
                if u > 0:
                    s_prev = st_ref[tl - 1, j]
                else:
                    s_prev = jnp.where(gi == 0, entry[j], st_ref[jnp.maximum(tl - 1, 0), j])
                rows[j][0][u] = jnp.sum(st_ref[tl, j] * docol, axis=0, keepdims=True)
                rows[j][1][u] = jnp.sum(dsp * s_prev, axis=0, keepdims=True)
                rows[j][2][u] = jnp.sum(dsp * vcol, axis=0, keepdims=True)
                rows[j][3][u] = jnp.sum(s_prev * dsa, axis=0, keepdims=True)
                rows[j][4][u] = jnp.sum(dsp * sa_ref[tl, j], axis=0, keepdims=True)

            cols = cols_of(GROUP - 1)
            before = None
            for u in range(GROUP - 1, 0, -2):
                dsp1 = [dstates[j] + cols[0][j][0] * row(j, "r", u) for j in range(N_PAIR)]
                prods = _seg_sum([dsp1[j] * row(j, "b", u) for j in range(N_PAIR)]
                                 + [dsp1[j] * w_b[j][u:u + 1] for j in range(N_PAIR)], ones2)
                if before is not None:
                    emit_dv(u + 2, before)
                nxt = cols_of(u - 2) if u >= 2 else None
                dsp2 = []
                for j in range(N_PAIR):
                    dsa1 = prods[j]
                    dsa2 = prods[N_PAIR + j] + dsa1 * n_b[j][u:u + 1] + cols[1][j][0] * r_b[j][u - 1:u]
                    mid = dsp1[j] * row(j, "w", u) + dsa1 * row(j, "n", u) + cols[1][j][0] * row(j, "r", u - 1)
                    outputs(u, j, dsp1[j], dsa1, *cols[0][j])
                    outputs(u - 1, j, mid, dsa2, *cols[1][j])
                    dstates[j] = mid * row(j, "w", u - 1) + dsa2 * row(j, "n", u - 1)
                    dsp2.append(mid)
                before, cols = (dsp1, dsp2), nxt
            emit_dv(1, before)
            for j in range(N_PAIR):
                for ref, rr in zip((dr_ref, dw_ref, dk_ref, dkkn_ref, db_ref), rows[j]):
                    ref[pl.ds(row0, GROUP), _pair(j)] = jnp.concatenate(rr, axis=0)
            return tuple(dstates + dvcols)

        zero = jnp.zeros((HEAD_DIM, LANES), F32)
        dfin = lax.fori_loop(0, CHUNK // GROUP, reverse, tuple(ds_scr[j] for j in range(N_PAIR)) + (zero,) * N_PAIR)
        for j in range(N_PAIR):
            ds_scr[j] = dfin[j]
            dv_ref[:, _pair(j)] = _rows_of_columns(dfin[N_PAIR + j])

        @pl.when(i == n_chunks - 1)
        def _():
            ds_out_ref[...] = ds_scr[...]

    blk = pl.BlockSpec((CHUNK, D_RWKV), lambda i: (top - i, 0))
    per_step = pl.BlockSpec((CHUNK,) + STATE, lambda i: (top - i, 0, 0, 0))
    step_before = pl.BlockSpec((1,) + STATE, lambda i: (jnp.maximum((top - i) * CHUNK - 1, 0), 0, 0, 0))
    prev = [] if prev is None else list(prev)
    outs = pl.pallas_call(
        body, name=name, grid=(n_chunks,),
        in_specs=[blk] * 7 + [per_step, step_before, per_step, _const(STATE)] + [ANY] * len(prev),
        out_specs=[blk] * 6 + [_const(STATE)],
        out_shape=[jax.ShapeDtypeStruct((SEQ, D_RWKV), F32)] * 6 + [jax.ShapeDtypeStruct(STATE, F32)],
        scratch_shapes=[pltpu.VMEM(STATE, F32)],
        input_output_aliases={11 + t: t for t in range(len(prev))},
        compiler_params=_cp(("arbitrary",)),
    )(r, w, k, v, kkn, b, do, states, states, sas, ds_in, *prev)
    return outs[:6], outs[6]


def _stacked(rows, cols, pick):
    return pl.BlockSpec((None, rows, cols), pick)


def _local_step(x, target, sm, win_st):
    def tied(t, token):
        return t if token is None else t + token[0:1, 0:1].reshape((1,) * t.ndim)

    zpad = jnp.zeros((LORA_DECAY, D_RWKV), F32)
    prm = [sm["w0"], jnp.concatenate([sm["w_decay_up"], zpad], axis=0), sm["a0"],
           jnp.concatenate([zpad, sm["w_iclr_up"]], axis=0), sm["w_gate_up"], sm["k_k"], sm["k_a"]]
    mix = sm["rwkv_shift_mix"]
    onehot = jnp.asarray(_t5_onehot(), BF16)
    sinks = sm["sinks"].reshape(N_Q_HEADS)
    lng, lnb, rk = sm["ln_x_g"], sm["ln_x_b"], sm["r_k"].reshape(1, D_RWKV)

    h1 = _norm_cast(x, sm["norm_mix_pre"], "norm_in")
    proj = _matmul(h1, win_st, "nn", "proj", m=SEQ, n=D_IN, k=D_MODEL, tm=SEQ, tn=640,
                   b_spec=_stacked(D_MODEL, 640, lambda i, j: (j, 0, 0)))
    bias = _bias_table(sm["rel_bias"].T, onehot).reshape(N_KV_HEADS, Q_PER_KV * BLOCK, 2 * BLOCK)
    attn = _attn_fwd(proj, bias, sinks)
    r, w, k2, v, kkn, b, g = _rwkv_prep(proj, mix, prm)
    o, states, sas = _scan_fwd(r, w, k2, v, kkn, b)
    wout, wup_st, wdown = yield ("rest_weights", o)
    cat = _rwkv_post(o, r, k2, v, g, lng, lnb, rk, attn)
    mixo = _matmul(cat, wout, "nn", "out_proj", m=SEQ, n=D_MODEL, k=D_MODEL, tm=SEQ, tn=512)
    x2, h3 = _mix_norm(x, mixo, sm["norm_mix_post"], sm["norm_ffn_pre"])
    u_gate, u_val, act = _ffn_up_act(h3, wup_st, sm["conv_w"], sm["conv_b"])
    f = _matmul(act, wdown, "nn", "ffn_down", m=SEQ, n=D_MODEL, k=D_FF, tm=1024, tn=512)
    loss, dy, df, d_g4 = _loss_head(x2, f, sm["norm_ffn_post"], target)

    d_wdown = _matmul(act, df, "tn", "d_wdown", m=D_FF, n=D_MODEL, k=SEQ, tm=512, tn=D_MODEL)
    du, d_convw, d_convb = _ffn_act_bwd(u_gate, u_val, df, wdown, sm["conv_w"], sm["conv_b"])
    d_convw = d_convw.transpose(1, 0, 2).reshape(3, 2 * D_FF)
    d_convb = d_convb.reshape(1, 2 * D_FF)
    dh3 = _matmul_nt_shards(du, wup_st, "d_h3", m=SEQ, n=D_MODEL, tm=512, tn=512,
                            a_spec=pl.BlockSpec((2, 512, D_FF), lambda i, j: (0, i, 0)),
                            a_piece=lambda ref, s: ref[s // 2, :, (s % 2) * 2048:(s % 2 + 1) * 2048])
    d_wup = _matmul(h3, du, "tn", "d_wup", m=D_MODEL, n=2 * D_FF, k=SEQ, tm=D_MODEL, tn=512,
                    b_spec=pl.BlockSpec((None, SEQ, 512), lambda i, j: (j // 8, 0, j % 8)),
                    out=((N_CHIPS, D_MODEL, 2048), _stacked(D_MODEL, 512, lambda i, j: (j // 4, 0, j % 4))))
    dx2, dmix, d_g2, d_g3 = _mid_bwd(x2, mixo, dy, dh3, sm["norm_mix_post"], sm["norm_ffn_pre"])
    dcat = _matmul(dmix, wout, "nt", "d_cat", m=SEQ, n=D_MODEL, k=D_MODEL, tm=SEQ, tn=512)
    d_wout = _matmul(cat, dmix, "tn", "d_wout", m=D_MODEL, n=D_MODEL, k=SEQ, tm=512, tn=D_MODEL)
    token = yield ("grads_a", (d_wdown, d_wup, d_wout))
    do, dr_p, dk_p, dv_p, dg, d_lng, d_lnb, d_rk = _rwkv_post_bwd(o, r, k2, v, g, lng, tied(lnb, token), rk, dcat)
    half = N_CHUNK // 2
    ds_end = jnp.zeros(STATE, F32)
    late, ds_mid = _scan_bwd(r, w, k2, v, kkn, b, do, states, sas, ds_end, None, "rwkv_scan_bwd_late", half, half)
    token = yield ("seam_1", ds_mid)
    scan_cts, ds_first = _scan_bwd(r, w, k2, v, kkn, b, do, states, sas, tied(ds_mid, token), late,
                                   "rwkv_scan_bwd_early", 0, half)
    dr_s, dw_s, dk_s, dv_s, dkkn_s, db_s = scan_cts
    token = yield ("seam_2", ds_first)
    prep_grads = _rwkv_prep_bwd(proj, tied(mix, token), prm,
                                (dr_s, dr_p, dw_s, dk_s, dk_p, dv_s, dv_p, dkkn_s, db_s, dg))
    dps, d_mix, d_w0, d_wdu, d_a0, d_wiu, d_wgu, d_kk, d_ka = prep_grads
    dq, dkv, dbias, dsink = _attn_bwd(proj, bias, sinks, dcat)
    d_relb = _bias_table_bwd(dbias.reshape(N_Q_HEADS, N_REL), onehot).T
    dproj = _assemble_dproj(dq, dkv, dps, mix)
    d_win = _matmul(h1, dproj, "tn", "d_win", m=D_MODEL, n=D_IN, k=SEQ, tm=D_MODEL, tn=640,
                    out=((N_CHIPS, D_MODEL, 640), _stacked(D_MODEL, 640, lambda i, j: (j, 0, 0))))
    token = yield ("grads_b", d_win)
    dh1 = _matmul_nt_shards(dproj, win_st, "d_h1", m=SEQ, n=D_MODEL, tm=1024, tn=D_MODEL,
                            a_spec=pl.BlockSpec((1024, D_IN), lambda i, j: (i, 0)),
                            a_piece=lambda ref, s: ref[:, s * 640:(s + 1) * 640])
    grad_x, d_g1 = _first_bwd(x, dx2, dh1, tied(sm["norm_mix_pre"], token))

    grads = {
        "norm_mix_pre": d_g1, "norm_mix_post": d_g2, "norm_ffn_pre": d_g3, "norm_ffn_post": d_g4,
        "w_in": d_win, "rel_bias": d_relb, "sinks": dsink[:, 0].reshape(1, N_Q_HEADS),
        "rwkv_shift_mix": d_mix, "w0": d_w0, "w_decay_up": d_wdu[:LORA_DECAY], "a0": d_a0,
        "w_iclr_up": d_wiu[LORA_DECAY:], "w_gate_up": d_wgu, "k_k": d_kk, "k_a": d_ka,
        "r_k": d_rk.reshape(1, N_Q_HEADS, HEAD_DIM), "ln_x_g": d_lng, "ln_x_b": d_lnb,
        "w_out": d_wout, "w_ffn_up": d_wup, "conv_w": d_convw, "conv_b": d_convb, "w_ffn_down": d_wdown,
    }
    return loss, grad_x, grads


def _place():
    x, y, c = lax.axis_index("x"), lax.axis_index("y"), lax.axis_index("c")
    chips = [(1 - x, y), (x, 1 - y), (1 - x, 1 - y)]
    return x, y, c, chips


def _remote(src, dst, sems, idx, to):
    return pltpu.make_async_remote_copy(src_ref=src, dst_ref=dst, send_sem=sems[0].at[idx], recv_sem=sems[1].at[idx],
                                        device_id=to, device_id_type=MESH)


ROW_ALIGN = 16


def _half(c, rows):
    return pl.ds(pl.multiple_of(c * (rows // 2), ROW_ALIGN), rows // 2)


def _gather_weights(big, small):
    nb, ns = len(big), len(small)

    def body(*refs):
        ins, outs = refs[:nb + ns], refs[nb + ns:2 * (nb + ns)]
        ici, d2d, sml, loc = refs[2 * (nb + ns):2 * (nb + ns) + 2], refs[-5:-3], refs[-3:-1], refs[-1]
        x, y, c, chips = _place()
        me = 2 * x + y
        sib = (x, y, 1 - c)
        local = [pltpu.make_async_copy(ins[a], outs[a].at[me], loc.at[a]) for a in range(nb + ns)]
        for cp in local:
            cp.start()
        sends = []
        for a in range(nb):
            rows = _half(c, big[a].shape[0])
            for kk, chip in enumerate(chips):
                sends.append(_remote(ins[a].at[rows], outs[a].at[me, rows], ici, a * 3 + kk, (*chip, c)))
        for a in range(ns):
            for kk, chip in enumerate(chips):
                sends.append(_remote(ins[nb + a], outs[nb + a].at[me], sml, a * 3 + kk, (*chip, c)))
        for cp in sends:
            cp.start()
        passed = []
        for a in range(nb):
            rows = _half(c, big[a].shape[0])
            for kk, (px, py) in enumerate(chips):
                got = outs[a].at[2 * px + py, rows]
                _remote(got, got, ici, a * 3 + kk, sib).wait_recv()
                fwd = _remote(got, got, d2d, a * 3 + kk, sib)
                fwd.start()
                passed.append(fwd)
        for a in range(nb):
            other = _half(1 - c, big[a].shape[0])
            for kk, (px, py) in enumerate(chips):
                land = outs[a].at[2 * px + py, other]
                _remote(land, land, d2d, a * 3 + kk, sib).wait_recv()
        for a in range(ns):
            for kk, (px, py) in enumerate(chips):
                land = outs[nb + a].at[2 * px + py]
                _remote(land, land, sml, a * 3 + kk, sib).wait_recv()
        for cp in sends + passed:
            cp.wait_send()
        for cp in local:
            cp.wait()

    arrs = list(big) + list(small)
    in_vmem = pl.BlockSpec(memory_space=pltpu.VMEM)
    return pl.pallas_call(
        body, name="gather_weights",
        in_specs=[in_vmem] * len(arrs), out_specs=[in_vmem] * len(arrs),
        out_shape=[jax.ShapeDtypeStruct((N_CHIPS,) + t.shape, t.dtype) for t in arrs],
        scratch_shapes=[pltpu.SemaphoreType.DMA((3 * nb,)), pltpu.SemaphoreType.DMA((3 * nb,)),
                        pltpu.SemaphoreType.DMA((3 * nb,)), pltpu.SemaphoreType.DMA((3 * nb,)),
                        pltpu.SemaphoreType.DMA((3 * ns,)), pltpu.SemaphoreType.DMA((3 * ns,)),
                        pltpu.SemaphoreType.DMA((nb + ns,))],
        compiler_params=pltpu.CompilerParams(has_side_effects=True, vmem_limit_bytes=VMEM_LIMIT),
    )(*arrs)


HBM = pl.BlockSpec(memory_space=pltpu.HBM)
SEM = pl.BlockSpec(memory_space=pltpu.SEMAPHORE)
EFFECT = pltpu.SideEffectType.DATAFLOW_SIDE_EFFECTING


def _copies_start(name, bufs, plan, n):
    nb = len(bufs)

    def body(*refs):
        ins, sems, token = refs[:nb], refs[nb:nb + 2 * n], refs[-1]
        for kk, (src, dst, dev) in enumerate(plan(ins)):
            pltpu.make_async_remote_copy(src_ref=src, dst_ref=dst, send_sem=sems[2 * kk], recv_sem=sems[2 * kk + 1],
                                         device_id=dev, device_id_type=MESH).start()
        token[...] = jnp.zeros_like(token)

    outs = pl.pallas_call(
        body, name=name,
        out_shape=tuple([pltpu.SemaphoreType.DMA(())] * (2 * n) + [pltpu.HBM(t.shape, t.dtype) for t in bufs]
                        + [jax.ShapeDtypeStruct((8, LANES), F32)]),
        in_specs=[HBM] * nb,
        out_specs=tuple([SEM] * (2 * n) + [HBM] * nb + [pl.BlockSpec(memory_space=pltpu.VMEM)]),
        input_output_aliases={t: 2 * n + t for t in range(nb)},
        compiler_params=pltpu.CompilerParams(has_side_effects=EFFECT),
    )(*[pltpu.with_memory_space_constraint(t, pltpu.HBM) for t in bufs])
    return outs[:2 * n], outs[2 * n:2 * n + nb], outs[-1]


def _copies_wait(name, sems, bufs, plan, n, after):
    nb = len(bufs)
    after = list(after) if isinstance(after, (list, tuple)) else [after]

    def body(*refs):
        ins, sem_refs = refs[:nb], refs[nb:nb + 2 * n]
        for kk, (src, dst, dev) in enumerate(plan(ins)):
            cp = pltpu.make_async_remote_copy(src_ref=src, dst_ref=dst, send_sem=sem_refs[2 * kk],
                                              recv_sem=sem_refs[2 * kk + 1], device_id=dev, device_id_type=MESH)
            cp.wait_send()
            cp.wait_recv()

    return pl.pallas_call(
        body, name=name,
        out_shape=tuple(pltpu.HBM(t.shape, t.dtype) for t in bufs),
        in_specs=[HBM] * nb + [SEM] * (2 * n) + [ANY] * len(after),
        out_specs=tuple([HBM] * nb),
        input_output_aliases={t: t for t in range(nb)},
        compiler_params=pltpu.CompilerParams(has_side_effects=EFFECT),
    )(*bufs, *sems, *after)


def _plan_gather(n_w):
    def plan(refs):
        x, y, c, chips = _place()
        me = 2 * x + y
        return [(refs[a], refs[n_w + a].at[me], (*chip, c)) for a in range(n_w) for chip in chips]
    return plan


def _plan_pair_halves(n_g, rows):
    def plan(refs):
        x, y, c, _ = _place()
        return [(refs[a].at[:, _half(1 - c, rows[a])], refs[n_g + a], (x, y, 1 - c)) for a in range(n_g)]
    return plan


def _plan_chip_parts(n_g):
    def plan(refs):
        x, y, c, chips = _place()
        me = 2 * x + y
        return [(refs[a].at[2 * px + py], refs[n_g + a].at[me], (px, py, c))
                for a in range(n_g) for (px, py) in chips]
    return plan


def _plan_pair_fill(n_g, rows):
    def plan(refs):
        x, y, c, _ = _place()
        return [(refs[a].at[_half(c, rows[a])], refs[a].at[_half(c, rows[a])], (x, y, 1 - c)) for a in range(n_g)]
    return plan


def _pair_add(g, got, name):
    _, rows, cols = g.shape
    hr = rows // 2
    tr = min(hr, 256)
    nb = hr // tr

    def body(g_ref, got_ref, p_ref, own_ref):
        val = (g_ref[...] + got_ref[...]).astype(BF16)
        p_ref[...] = val

        @pl.when(pl.program_id(1) == 2 * lax.axis_index("x") + lax.axis_index("y"))
        def _():
            own_ref[...] = val

    def mine(i, s):
        return (2 * lax.axis_index("x") + lax.axis_index("y"), i, 0)

    return pl.pallas_call(
        body, name=name, grid=(nb, N_CHIPS),
        in_specs=[pl.BlockSpec((None, tr, cols), lambda i, s: (s, lax.axis_index("c") * nb + i, 0)),
                  pl.BlockSpec((None, tr, cols), lambda i, s: (s, i, 0))],
        out_specs=[pl.BlockSpec((None, tr, cols), lambda i, s: (s, i, 0)), pl.BlockSpec((None, tr, cols), mine)],
        out_shape=[jax.ShapeDtypeStruct((N_CHIPS, hr, cols), BF16)] * 2,
        compiler_params=_cp(("parallel", "arbitrary")),
    )(g, got)


def _chip_sum(parts, name):
    _, hr, cols = parts.shape
    tr = min(hr, 128)
    nb = hr // tr

    def body(t_ref, o_ref):
        part = [t_ref[s].astype(F32) for s in range(N_CHIPS)]
        o_ref[...] = ((part[0] + part[1]) + part[2]) + part[3]

    return pl.pallas_call(
        body, name=name, grid=(nb,),
        in_specs=[pl.BlockSpec((N_CHIPS, tr, cols), lambda i: (0, i, 0))],
        out_specs=pl.BlockSpec((tr, cols), lambda i: (lax.axis_index("c") * nb + i, 0)),
        out_shape=jax.ShapeDtypeStruct((2 * hr, cols), F32),
        compiler_params=_cp(("parallel",)),
    )(parts)


class _Reduction:
    def __init__(self, tag, rows):
        self.tag, self.n, self.rows = tag, len(rows), rows
        self.plans = (_plan_pair_halves(self.n, rows), _plan_chip_parts(self.n), _plan_pair_fill(self.n, rows))
        self.flight = None

    def _name(self, what):
        return f"grad_{self.tag}_{what}"

    def start(self, gs):
        gots = [lax.empty((N_CHIPS, t.shape[1] // 2, t.shape[2]), F32) for t in gs]
        self.flight = _copies_start(self._name("pair_start"), list(gs) + gots, self.plans[0], self.n)
        return self.flight[2]

    def after_pair(self, after):
        sems, bufs, _ = self.flight
        out = _copies_wait(self._name("pair_wait"), sems, bufs, self.plans[0], self.n, after)
        sums = [_pair_add(g, got, self._name(f"pair_add_{i}"))
                for i, (g, got) in enumerate(zip(out[:self.n], out[self.n:]))]
        self.flight = _copies_start(self._name("chip_start"), [p for p, _ in sums] + [own for _, own in sums],
                                    self.plans[1], 3 * self.n)
        return self.flight[2]

    def after_chips(self, after):
        sems, bufs, _ = self.flight
        out = _copies_wait(self._name("chip_wait"), sems, bufs, self.plans[1], 3 * self.n, after)
        fulls = [_chip_sum(t, self._name(f"chip_sum_{i}")) for i, t in enumerate(out[self.n:])]
        self.flight = _copies_start(self._name("fill_start"), fulls, self.plans[2], self.n)
        return self.flight[2]

    def finish(self, after):
        sems, bufs, _ = self.flight
        return _copies_wait(self._name("fill_wait"), sems, bufs, self.plans[2], self.n, after)


def _adamw_math(w, g, m, v):
    nm = ADAM_B1 * m + (1.0 - ADAM_B1) * g
    nv = ADAM_B2 * v + (1.0 - ADAM_B2) * (g * g)
    m_hat = nm / (1.0 - ADAM_B1 ** ADAM_STEP)
    v_hat = nv / (1.0 - ADAM_B2 ** ADAM_STEP)
    return -ADAM_LR * (m_hat / (jnp.sqrt(v_hat) + ADAM_EPS) + ADAM_WD * w), nm, nv


def _adamw(w, g, m, v, name, tr):
    r, cdim = w.shape

    def body(w_ref, g_ref, m_ref, v_ref, d_ref, nm_ref, nv_ref):
        d_ref[...], nm_ref[...], nv_ref[...] = _adamw_math(w_ref[...], g_ref[...], m_ref[...], v_ref[...])

    return pl.pallas_call(
        body, name=name, grid=(r // tr,), in_specs=[_rows(tr, cdim)] * 4, out_specs=[_rows(tr, cdim)] * 3,
        out_shape=[jax.ShapeDtypeStruct((r, cdim), F32)] * 3, compiler_params=_cp(("parallel",)),
    )(w, g, m, v)


def _adamw_small(w, parts, m, v, shapes):
    n_rows = w.shape[0]

    def scatter(src, outs):
        row = 0
        for (rows, cols), out in zip(shapes, outs):
            if cols == LANES:
                out[...] = src[row:row + rows, :]
            elif cols > LANES:
                per = cols // LANES
                for r in range(rows):
                    for cb in range(per):
                        out[r:r + 1, cb * LANES:(cb + 1) * LANES] = src[row + r * per + cb:row + r * per + cb + 1, :]
            else:
                per = LANES // cols
                for r in range(rows):
                    out[r:r + 1, :] = src[row + r // per:row + r // per + 1, (r % per) * cols:(r % per + 1) * cols]
            row += -(-rows * cols // LANES)

    def body(w_ref, p_ref, m_ref, v_ref, *rest):
        outs, scr = rest[:-4], rest[-4:]
        g = p_ref[0]
        for dev in range(1, N_DEV):
            g = g + p_ref[dev]
        scr[3][...] = g
        scr[0][...], scr[1][...], scr[2][...] = _adamw_math(w_ref[...], g, m_ref[...], v_ref[...])
        n = len(shapes)
        for kind in range(4):
            scatter(scr[kind], outs[kind * n:(kind + 1) * n])

    outs = pl.pallas_call(
        body, name="adamw_small", grid=(1,),
        in_specs=[_const(w.shape), _const(parts.shape), _const(w.shape), _const(w.shape)],
        out_specs=[_const(s) for s in shapes] * 4, out_shape=[jax.ShapeDtypeStruct(s, F32) for s in shapes] * 4,
        scratch_shapes=[pltpu.VMEM((n_rows, LANES), F32)] * 4,
        compiler_params=_cp(("arbitrary",)),
    )(w, parts, m, v)
    n = len(shapes)
    return [outs[kind * n:(kind + 1) * n] for kind in range(4)]


REPLICATED = (("norm_mix_pre", 1024), ("norm_mix_post", 1024), ("norm_ffn_pre", 1024), ("norm_ffn_post", 1024),
              ("rel_bias", 256), ("sinks", 8), ("rwkv_shift_mix", 1792), ("w0", 512), ("a0", 512), ("k_k", 512),
              ("k_a", 512), ("r_k", 512), ("ln_x_g", 512), ("ln_x_b", 512), ("conv_b", 8192))
SMALL_SHARDED = (("w_decay_up", LORA_DECAY, D_RWKV), ("w_iclr_up", LORA_ICLR, D_RWKV),
                 ("w_gate_up", LORA_GATE, D_RWKV), ("conv_w", 3, 2 * D_FF))
BIG = (("w_in", D_MODEL, 640), ("w_out", 256, D_MODEL), ("w_ffn_up", D_MODEL, 2048), ("w_ffn_down", 1024, D_MODEL))
PACK_ALIGN = 8 * LANES


def _pack(pieces):
    flat = []
    for t in pieces:
        t = t.reshape(-1)
        pad = (-t.shape[0]) % LANES
        flat.append(jnp.pad(t, (0, pad)) if pad else t)
    flat = jnp.concatenate(flat)
    pad = (-flat.shape[0]) % PACK_ALIGN
    return jnp.pad(flat, (0, pad)).reshape(-1, LANES)


def kernel(x, norm_mix_pre, norm_mix_post, norm_ffn_pre, norm_ffn_post, w_in, rel_bias, sinks, rwkv_shift_mix, w0, w_decay_up, a0, w_iclr_up, w_gate_up, k_k, k_a, r_k, ln_x_g, ln_x_b, w_out, w_ffn_up, conv_w, conv_b, w_ffn_down, loss_target, m_norm_mix_pre, m_norm_mix_post, m_norm_ffn_pre, m_norm_ffn_post, m_w_in, m_rel_bias, m_sinks, m_rwkv_shift_mix, m_w0, m_w_decay_up, m_a0, m_w_iclr_up, m_w_gate_up, m_k_k, m_k_a, m_r_k, m_ln_x_g, m_ln_x_b, m_w_out, m_w_ffn_up, m_conv_w, m_conv_b, m_w_ffn_down, v_norm_mix_pre, v_norm_mix_post, v_norm_ffn_pre, v_norm_ffn_post, v_w_in, v_rel_bias, v_sinks, v_rwkv_shift_mix, v_w0, v_w_decay_up, v_a0, v_w_iclr_up, v_w_gate_up, v_k_k, v_k_a, v_r_k, v_ln_x_g, v_ln_x_b, v_w_out, v_w_ffn_up, v_conv_w, v_conv_b, v_w_ffn_down):
    given = dict(locals())
    names = [n for n, _ in REPLICATED] + [n for n, _, _ in SMALL_SHARDED] + [n for n, _, _ in BIG]
    order = ["norm_mix_pre", "norm_mix_post", "norm_ffn_pre", "norm_ffn_post", "w_in", "rel_bias", "sinks",
             "rwkv_shift_mix", "w0", "w_decay_up", "a0", "w_iclr_up", "w_gate_up", "k_k", "k_a", "r_k", "ln_x_g",
             "ln_x_b", "w_out", "w_ffn_up", "conv_w", "conv_b", "w_ffn_down"]
    assert sorted(names) == sorted(order)
    shard = 2 * lax.axis_index("x") + lax.axis_index("y")

    big_sh = {n: given[n].reshape(a, b).astype(BF16) for n, a, b in BIG}
    small_sh = [given[n].reshape(r, c // N_CHIPS) for n, r, c in SMALL_SHARDED]
    gathered = _gather_weights([big_sh["w_in"]], small_sh)
    rest = ("w_out", "w_ffn_up", "w_ffn_down")
    win_st, rest_sh = lax.optimization_barrier((gathered[0], [big_sh[n] for n in rest]))
    sm = {n: given[n] for n, _ in REPLICATED}
    sm["r_k"] = r_k.reshape(N_Q_HEADS, HEAD_DIM)
    for (n, r, c), st in zip(SMALL_SHARDED, gathered[1:]):
        sm[n] = st.transpose(1, 0, 2).reshape(r, c)

    lands = [lax.dynamic_update_slice(lax.empty((N_CHIPS,) + t.shape, BF16), t[None], (shard, 0, 0)) for t in rest_sh]
    plan_w = _plan_gather(len(rest))
    w_sems, w_bufs, token = _copies_start("gather_rest_start", rest_sh + lands, plan_w, 9)
    sm["norm_mix_pre"] = norm_mix_pre + token[0:1, 0:1]

    def on_rest_weights(after):
        out = _copies_wait("gather_rest_wait", w_sems, w_bufs, plan_w, 9, after)
        wout_st, wup_st, wdown_st = out[3:]
        return wout_st.reshape(D_MODEL, D_MODEL), wup_st, wdown_st.reshape(D_FF, D_MODEL)

    red_a = _Reduction("a", (1024, D_MODEL, 256))
    red_b = _Reduction("b", (D_MODEL,))

    def on_grads_a(gs):
        d_wdown, d_wup, d_wout = gs
        return red_a.start([d_wdown.reshape(N_CHIPS, 1024, D_MODEL), d_wup, d_wout.reshape(N_CHIPS, 256, D_MODEL)])

    handlers = {"rest_weights": on_rest_weights, "grads_a": on_grads_a, "seam_1": red_a.after_pair,
                "seam_2": red_a.after_chips, "grads_b": lambda g: red_b.start([g])}
    steps = _local_step(x[0], loss_target[0], sm, win_st)
    kind, payload = next(steps)
    while True:
        try:
            kind, payload = steps.send(handlers[kind](payload))
        except StopIteration as done:
            loss, grad_x, grads = done.value
            break

    small_names = [n for n, _ in REPLICATED] + [n for n, _, _ in SMALL_SHARDED]

    def shard_cols(t, s):
        return t[:, s * (t.shape[1] // N_CHIPS):(s + 1) * (t.shape[1] // N_CHIPS)]

    for_chip = jnp.stack([_pack([loss[0]] + [grads[n] for n, _ in REPLICATED]
                                + [shard_cols(grads[n], s) for n, _, _ in SMALL_SHARDED]) for s in range(N_CHIPS)])
    me = 2 * shard + lax.axis_index("c")
    mine = lax.dynamic_index_in_dim(for_chip, shard, 0, keepdims=True)
    land = lax.dynamic_update_slice(lax.empty((N_DEV,) + for_chip.shape[1:], F32), mine, (me, 0, 0))

    def plan_small(refs):
        x, y, c, _ = _place()
        out = []
        for rel in range(1, N_DEV):
            px, py, pc = x ^ (rel >> 2), y ^ ((rel >> 1) & 1), c ^ (rel & 1)
            out.append((refs[0].at[2 * px + py], refs[1].at[4 * x + 2 * y + c], (px, py, pc)))
        return out

    s_sems, s_bufs, s_token = _copies_start("grad_small_start", [for_chip, land], plan_small, N_DEV - 1)

    red_b.after_pair([grad_x, s_token])
    g_out = {}
    g_out["w_ffn_down"], g_out["w_ffn_up"], g_out["w_out"] = red_a.finish(grad_x)

    delta, new_m, new_v = {}, {}, {}

    def update(n, a, b):
        delta[n], new_m[n], new_v[n] = _adamw(given[n].reshape(a, b), g_out[n], given["m_" + n].reshape(a, b),
                                              given["v_" + n].reshape(a, b), "adamw_" + n, 128)

    for n, a, b in BIG[1:]:
        update(n, a, b)
    done = [delta[n] for n, _, _ in BIG[1:]]
    red_b.after_chips(done)
    parts = _copies_wait("grad_small_wait", s_sems, s_bufs, plan_small, N_DEV - 1, done)[1]
    no_param = jnp.zeros((LANES,), F32)
    packs = [_pack([no_param] + [given[pre + n] for n in small_names]) for pre in ("", "m_", "v_")]

    def piece_shape(n):
        shape = given[n].shape
        rows, cols = int(np.prod(shape[:-1])), shape[-1]
        whole = cols % LANES == 0 or (LANES % cols == 0 and (rows * cols) % LANES == 0 and cols >= HEAD_DIM)
        return (rows, cols) if whole else (-(-rows * cols // LANES), LANES)

    shapes = [(1, LANES)] + [piece_shape(n) for n in small_names]
    upd = _adamw_small(packs[0], parts, packs[1], packs[2], shapes)
    loss = upd[3][0][0, 0]
    for i, n in enumerate(small_names):
        shape = given[n].shape
        size = int(np.prod(shape))
        delta[n], new_m[n], new_v[n], g_out[n] = (u[1 + i].reshape(-1)[:size].reshape(shape) for u in upd)
    g_out["w_in"], = red_b.finish(upd[0][0])
    update(*BIG[0])

    def shaped(d):
        return [d[n].reshape(given[n].shape) for n in order]

    return (loss, grad_x.reshape(x.shape), *shaped(g_out), *shaped(delta), *shaped(new_m), *shaped(new_v))
```

```python
import math

import numpy as np
import jax
import jax.numpy as jnp
from jax import lax
from jax.experimental import pallas as pl
from jax.experimental.pallas import tpu as pltpu

F32 = jnp.float32
BF16 = jnp.bfloat16
MESH = pl.DeviceIdType.MESH

SEQ = 2048
D_MODEL = 1024
HEAD_DIM = 64
D_ATTN = 512
D_RWKV = 512
D_KV = 128
N_Q_HEADS = 8
N_KV_HEADS = 2
Q_PER_KV = 4
BLOCK = 128
N_BUCKETS = 32
MAX_DISTANCE = 128
LORA_DECAY = 64
LORA_ICLR = 64
LORA_GATE = 128
RWKV_COLS = 3 * D_RWKV + LORA_DECAY + LORA_ICLR + LORA_GATE
P_OFF = D_ATTN + 2 * D_KV
D_IN = P_OFF + RWKV_COLS
D_FF = 4096
NORM_EPS = 1e-6
GN_EPS = 64e-5
NEG_INF = -1e30
N_CHIPS = 4
N_DEV = 8
HEAD_SHIFT = HEAD_DIM.bit_length() - 1
BLOCK_SHIFT = BLOCK.bit_length() - 1

ADAM_LR = 0.001
ADAM_B1 = 0.9
ADAM_B2 = 0.999
ADAM_EPS = 1e-08
ADAM_WD = 0.01
ADAM_STEP = 10

VMEM_LIMIT = 52 * 1024 * 1024
LANES = 128


def _cp(sem=None, vmem=VMEM_LIMIT):
    kw = dict(vmem_limit_bytes=vmem)
    if sem is not None:
        kw["dimension_semantics"] = sem
    return pltpu.CompilerParams(**kw)


def _rows(tr, nc):
    return pl.BlockSpec((tr, nc), lambda i: (i, 0))


def _const(shape):
    return pl.BlockSpec(shape, lambda *_: (0,) * len(shape))


ANY = pl.BlockSpec(memory_space=pl.ANY)


def _split(x, n):
    parts = []
    for _ in range(n - 1):
        h = x.astype(BF16)
        parts.append(h)
        x = x - h.astype(F32)
    parts.append(x.astype(BF16))
    return parts


NN = (((1,), (0,)), ((), ()))
NT = (((1,), (1,)), ((), ()))
TN = (((0,), (0,)), ((), ()))


def _dot(a, b, dn=NN):
    return lax.dot_general(a, b, dn, preferred_element_type=F32)


def _dot_ind(x, ind_bf16, n=3):
    acc = None
    for part in _split(x, n):
        t = _dot(part, ind_bf16)
        acc = t if acc is None else acc + t
    return acc


def _head_ones(n):
    r = lax.broadcasted_iota(jnp.int32, (n, n), 0) >> HEAD_SHIFT
    c = lax.broadcasted_iota(jnp.int32, (n, n), 1) >> HEAD_SHIFT
    return jnp.where(r == c, 1.0, 0.0).astype(BF16)


def _matmul(a, b, mode, name, *, m, n, k, tm, tn, a_spec=None, b_spec=None, out=None):
    dn = {"nn": NN, "nt": NT, "tn": TN}[mode]

    def body(a_ref, b_ref, o_ref):
        o_ref[...] = _dot(a_ref[...], b_ref[...], dn)

    if a_spec is None:
        a_spec = pl.BlockSpec((k, tm), lambda i, j: (0, i)) if mode == "tn" else pl.BlockSpec((tm, k), lambda i, j: (i, 0))
    if b_spec is None:
        b_spec = pl.BlockSpec((tn, k), lambda i, j: (j, 0)) if mode == "nt" else pl.BlockSpec((k, tn), lambda i, j: (0, j))
    return pl.pallas_call(
        body, name=name, grid=(m // tm, n // tn),
        in_specs=[a_spec, b_spec],
        out_specs=pl.BlockSpec((tm, tn), lambda i, j: (i, j)) if out is None else out[1],
        out_shape=jax.ShapeDtypeStruct((m, n) if out is None else out[0], F32),
        compiler_params=_cp(("parallel", "parallel")),
    )(a, b)


def _matmul_nt_shards(a, b_st, name, *, m, n, tm, tn, a_spec, a_piece):
    ks = b_st.shape[2]

    def body(a_ref, b_ref, o_ref):
        acc = _dot(a_piece(a_ref, 0), b_ref[0], NT)
        for s in range(1, N_CHIPS):
            acc = acc + _dot(a_piece(a_ref, s), b_ref[s], NT)
        o_ref[...] = acc

    return pl.pallas_call(
        body, name=name, grid=(m // tm, n // tn),
        in_specs=[a_spec, pl.BlockSpec((N_CHIPS, tn, ks), lambda i, j: (0, j, 0))],
        out_specs=pl.BlockSpec((tm, tn), lambda i, j: (i, j)),
        out_shape=jax.ShapeDtypeStruct((m, n), F32),
        compiler_params=_cp(("parallel", "parallel")),
    )(a, b_st)


def _rstd(x):
    return lax.rsqrt(jnp.mean(x * x, axis=-1, keepdims=True) + NORM_EPS)


def _rms_bwd(x, r, g, dy):
    gy = dy * g
    return r * gy - x * ((r * r * r) * (jnp.sum(x * gy, axis=-1, keepdims=True) / x.shape[-1]))


TR = 256


def _norm_cast(x, g, name):
    def body(x_ref, g_ref, h_ref):
        x = x_ref[...]
        h_ref[...] = (x * _rstd(x) * g_ref[...]).astype(BF16)

    return pl.pallas_call(
        body, name=name, grid=(SEQ // TR,),
        in_specs=[_rows(TR, D_MODEL), _const((1, D_MODEL))],
        out_specs=_rows(TR, D_MODEL),
        out_shape=jax.ShapeDtypeStruct((SEQ, D_MODEL), BF16),
        compiler_params=_cp(("parallel",)),
    )(x, g)


def _mix_norm(x, mix, g2, g3):
    def body(x_ref, mix_ref, g2_ref, g3_ref, x2_ref, h3_ref):
        mixv = mix_ref[...]
        x2 = x_ref[...] + mixv * _rstd(mixv) * g2_ref[...]
        x2_ref[...] = x2
        h3_ref[...] = (x2 * _rstd(x2) * g3_ref[...]).astype(BF16)

    return pl.pallas_call(
        body, name="mix_norm", grid=(SEQ // TR,),
        in_specs=[_rows(TR, D_MODEL), _rows(TR, D_MODEL), _const((1, D_MODEL)), _const((1, D_MODEL))],
        out_specs=[_rows(TR, D_MODEL), _rows(TR, D_MODEL)],
        out_shape=[jax.ShapeDtypeStruct((SEQ, D_MODEL), F32), jax.ShapeDtypeStruct((SEQ, D_MODEL), BF16)],
        compiler_params=_cp(("parallel",)),
    )(x, mix, g2, g3)


def _loss_head(x2, f, g4, target):
    def body(x2_ref, f_ref, g4_ref, t_ref, loss_ref, dy_ref, df_ref, dg_ref):
        i = pl.program_id(0)
        f = f_ref[...]
        g4 = g4_ref[...]
        r = _rstd(f)
        e = x2_ref[...] + f * r * g4 - t_ref[...]
        dy = e * (1.0 / D_MODEL)
        dy_ref[...] = dy
        df_ref[...] = _rms_bwd(f, r, g4, dy).astype(BF16)
        part = 0.5 * jnp.sum(jnp.sum(e * e, axis=-1, keepdims=True), axis=0, keepdims=True) * (1.0 / D_MODEL)
        dg = jnp.sum(dy * f * r, axis=0, keepdims=True)

        @pl.when(i == 0)
        def _():
            loss_ref[...] = jnp.zeros_like(loss_ref)
            dg_ref[...] = jnp.zeros_like(dg_ref)

        loss_ref[...] += jnp.broadcast_to(part, loss_ref.shape)
        dg_ref[...] += dg

    return pl.pallas_call(
        body, name="loss_head", grid=(SEQ // TR,),
        in_specs=[_rows(TR, D_MODEL), _rows(TR, D_MODEL), _const((1, D_MODEL)), _rows(TR, D_MODEL)],
        out_specs=[_const((8, LANES)), _rows(TR, D_MODEL), _rows(TR, D_MODEL), _const((1, D_MODEL))],
        out_shape=[jax.ShapeDtypeStruct((8, LANES), F32), jax.ShapeDtypeStruct((SEQ, D_MODEL), F32),
                   jax.ShapeDtypeStruct((SEQ, D_MODEL), BF16), jax.ShapeDtypeStruct((1, D_MODEL), F32)],
        compiler_params=_cp(("arbitrary",)),
    )(x2, f, g4, target)


def _mid_bwd(x2, mix, dy, dh3, g2, g3):
    def body(x2_ref, mix_ref, dy_ref, dh3_ref, g2_ref, g3_ref, dx2_ref, dmix_ref, dg2_ref, dg3_ref):
        i = pl.program_id(0)
        x2 = x2_ref[...]
        mixv = mix_ref[...]
        dh3 = dh3_ref[...]
        r3 = _rstd(x2)
        dx2 = dy_ref[...] + _rms_bwd(x2, r3, g3_ref[...], dh3)
        dx2_ref[...] = dx2
        r2 = _rstd(mixv)
        dmix_ref[...] = _rms_bwd(mixv, r2, g2_ref[...], dx2).astype(BF16)

        @pl.when(i == 0)
        def _():
            dg2_ref[...] = jnp.zeros_like(dg2_ref)
            dg3_ref[...] = jnp.zeros_like(dg3_ref)

        dg3_ref[...] += jnp.sum(dh3 * x2 * r3, axis=0, keepdims=True)
        dg2_ref[...] += jnp.sum(dx2 * mixv * r2, axis=0, keepdims=True)

    return pl.pallas_call(
        body, name="mid_bwd", grid=(SEQ // TR,),
        in_specs=[_rows(TR, D_MODEL)] * 4 + [_const((1, D_MODEL))] * 2,
        out_specs=[_rows(TR, D_MODEL), _rows(TR, D_MODEL), _const((1, D_MODEL)), _const((1, D_MODEL))],
        out_shape=[jax.ShapeDtypeStruct((SEQ, D_MODEL), F32), jax.ShapeDtypeStruct((SEQ, D_MODEL), BF16),
                   jax.ShapeDtypeStruct((1, D_MODEL), F32), jax.ShapeDtypeStruct((1, D_MODEL), F32)],
        compiler_params=_cp(("arbitrary",)),
    )(x2, mix, dy, dh3, g2, g3)


def _first_bwd(x, dx2, dh1, g1):
    def body(x_ref, dx2_ref, dh1_ref, g1_ref, dx_ref, dg1_ref):
        i = pl.program_id(0)
        x = x_ref[...]
        dh1 = dh1_ref[...]
        r = _rstd(x)
        dx_ref[...] = dx2_ref[...] + _rms_bwd(x, r, g1_ref[...], dh1)

        @pl.when(i == 0)
        def _():
            dg1_ref[...] = jnp.zeros_like(dg1_ref)

        dg1_ref[...] += jnp.sum(dh1 * x * r, axis=0, keepdims=True)

    return pl.pallas_call(
        body, name="first_bwd", grid=(SEQ // TR,),
        in_specs=[_rows(TR, D_MODEL)] * 3 + [_const((1, D_MODEL))],
        out_specs=[_rows(TR, D_MODEL), _const((1, D_MODEL))],
        out_shape=[jax.ShapeDtypeStruct((SEQ, D_MODEL), F32), jax.ShapeDtypeStruct((1, D_MODEL), F32)],
        compiler_params=_cp(("arbitrary",)),
    )(x, dx2, dh1, g1)


TC = 256
N_CB = D_FF // TC
GELU_C = math.sqrt(2.0 / math.pi)


def _shift_down(u, s):
    rolled = pltpu.roll(u, s, 0)
    row = lax.broadcasted_iota(jnp.int32, u.shape, 0)
    return jnp.where(row >= s, rolled, 0.0)


def _shift_up(u, s):
    n = u.shape[0]
    rolled = pltpu.roll(u, n - s, 0)
    row = lax.broadcasted_iota(jnp.int32, u.shape, 0)
    return jnp.where(row < n - s, rolled, 0.0)


def _conv3(u, w, b):
    return b + w[0:1] * _shift_down(u, 2) + w[1:2] * _shift_down(u, 1) + w[2:3] * u


def _gelu_and_grad(x):
    inner = GELU_C * (x + 0.044715 * (x * x * x))
    t = jnp.tanh(inner)
    gelu = 0.5 * x * (1.0 + t)
    dgelu = 0.5 * (1.0 + t) + 0.5 * x * (1.0 - t * t) * (GELU_C * (1.0 + 3 * 0.044715 * (x * x)))
    return gelu, dgelu


def _ffn_specs():
    col = lambda off: pl.BlockSpec((SEQ, TC), lambda *g: (0, g[-1] + off))
    w = lambda off: pl.BlockSpec((3, TC), lambda *g: (0, g[-1] + off))
    b = lambda off: pl.BlockSpec((1, TC), lambda *g: (0, g[-1] + off))
    return col, w, b


def _ffn_up_act(h3, wup_st, conv_w, conv_b):
    col, w, b = _ffn_specs()
    per_shard = wup_st.shape[2] // TC

    def body(h_ref, upg_ref, upv_ref, wg_ref, wv_ref, bg_ref, bv_ref, ug_ref, uv_ref, act_ref):
        h = h_ref[...]
        ug = _dot(h, upg_ref[...])
        uv = _dot(h, upv_ref[...])
        ug_ref[...] = ug
        uv_ref[...] = uv
        gate = _conv3(ug, wg_ref[...], bg_ref[...])
        val = _conv3(uv, wv_ref[...], bv_ref[...])
        act_ref[...] = (_gelu_and_grad(gate)[0] * val).astype(BF16)

    return pl.pallas_call(
        body, name="ffn_up_act", grid=(N_CB,),
        in_specs=[_const((SEQ, D_MODEL)),
                  pl.BlockSpec((None, D_MODEL, TC), lambda j: (j // per_shard, 0, j % per_shard)),
                  pl.BlockSpec((None, D_MODEL, TC), lambda j: (2 + j // per_shard, 0, j % per_shard)),
                  w(0), w(N_CB), b(0), b(N_CB)],
        out_specs=[col(0)] * 3,
        out_shape=[jax.ShapeDtypeStruct((SEQ, D_FF), F32)] * 2 + [jax.ShapeDtypeStruct((SEQ, D_FF), BF16)],
        compiler_params=_cp(("parallel",)),
    )(h3, wup_st, wup_st, conv_w, conv_w, conv_b, conv_b)


def _ffn_act_bwd(u_gate, u_val, df, wdown, conv_w, conv_b):
    col, w, b = _ffn_specs()
    both = lambda rows: pl.BlockSpec((2, rows, TC), lambda j: (0, 0, j))

    def body(ug_ref, uv_ref, df_ref, wd_ref, wg_ref, wv_ref, bg_ref, bv_ref, du_ref, dw_ref, db_ref):
        da = _dot(df_ref[...], wd_ref[...], NT)
        ug, uv = ug_ref[...], uv_ref[...]
        wg, wv = wg_ref[...], wv_ref[...]
        gate = _conv3(ug, wg, bg_ref[...])
        val = _conv3(uv, wv, bv_ref[...])
        gelu, dgelu = _gelu_and_grad(gate)
        for h, (duc, uh, wh) in enumerate(((da * val * dgelu, ug, wg), (da * gelu, uv, wv))):
            up1, up2 = _shift_up(duc, 1), _shift_up(duc, 2)
            du_ref[h] = (wh[2:3] * duc + wh[1:2] * up1 + wh[0:1] * up2).astype(BF16)
            db_ref[h] = jnp.sum(duc, axis=0, keepdims=True)
            dw_ref[h] = jnp.concatenate(
                [jnp.sum(up2 * uh, axis=0, keepdims=True), jnp.sum(up1 * uh, axis=0, keepdims=True),
                 jnp.sum(duc * uh, axis=0, keepdims=True)], axis=0)

    return pl.pallas_call(
        body, name="ffn_act_bwd", grid=(N_CB,),
        in_specs=[col(0), col(0), _const((SEQ, D_MODEL)), pl.BlockSpec((TC, D_MODEL), lambda j: (j, 0)),
                  w(0), w(N_CB), b(0), b(N_CB)],
        out_specs=[both(SEQ), both(3), both(1)],
        out_shape=[jax.ShapeDtypeStruct((2, SEQ, D_FF), BF16), jax.ShapeDtypeStruct((2, 3, D_FF), F32),
                   jax.ShapeDtypeStruct((2, 1, D_FF), F32)],
        compiler_params=_cp(("parallel",)),
    )(u_gate, u_val, df, wdown, conv_w, conv_w, conv_b, conv_b)


def _t5_onehot():
    rel = (np.arange(BLOCK)[:, None] + BLOCK) - np.arange(2 * BLOCK)[None, :]
    n = np.maximum(rel, 0)
    max_exact = N_BUCKETS // 2
    large = max_exact + (np.log(np.maximum(n, 1).astype(np.float32) / np.float32(max_exact))
                         / np.float32(math.log(MAX_DISTANCE / max_exact))
                         * np.float32(N_BUCKETS - max_exact)).astype(np.int32)
    large = np.minimum(large, N_BUCKETS - 1)
    bucket = np.where(n < max_exact, n, large).reshape(-1)
    return (bucket[None, :] == np.arange(N_BUCKETS)[:, None]).astype(np.float32)


N_REL = BLOCK * 2 * BLOCK


def _bias_table(rel_bias_t, onehot):
    def body(rb_ref, oh_ref, o_ref):
        o_ref[...] = _dot_ind(rb_ref[...], oh_ref[...])

    return pl.pallas_call(
        body, name="bias_table", grid=(1,),
        in_specs=[_const((N_Q_HEADS, N_BUCKETS)), _const((N_BUCKETS, N_REL))],
        out_specs=_const((N_Q_HEADS, N_REL)),
        out_shape=jax.ShapeDtypeStruct((N_Q_HEADS, N_REL), F32),
        compiler_params=_cp(("arbitrary",)),
    )(rel_bias_t, onehot)


def _bias_table_bwd(dbias, onehot):
    def body(db_ref, oh_ref, o_ref):
        acc = None
        for part in _split(db_ref[...], 3):
            t = _dot(part, oh_ref[...], NT)
            acc = t if acc is None else acc + t
        o_ref[...] = acc

    return pl.pallas_call(
        body, name="bias_table_bwd", grid=(1,),
        in_specs=[_const((N_Q_HEADS, N_REL)), _const((N_BUCKETS, N_REL))],
        out_specs=_const((N_Q_HEADS, N_BUCKETS)),
        out_shape=jax.ShapeDtypeStruct((N_Q_HEADS, N_BUCKETS), F32),
        compiler_params=_cp(("arbitrary",)),
    )(dbias, onehot)


def _attn_pieces(n, q, kvp, kvc, bias_ref, sinks_ref, hk):
    qi = lax.broadcasted_iota(jnp.int32, (BLOCK, 2 * BLOCK), 0)
    kj = lax.broadcasted_iota(jnp.int32, (BLOCK, 2 * BLOCK), 1)
    rel = qi + BLOCK - kj
    first_key = jnp.where(n > 0, 0, BLOCK)
    ok = jnp.where(rel >= 0, jnp.where(rel < BLOCK, jnp.where(kj >= first_key, 1.0, 0.0), 0.0), 0.0)
    ok4 = jnp.concatenate([ok] * Q_PER_KV, axis=0) > 0.5
    c0 = hk * HEAD_DIM
    kcat = jnp.concatenate([kvp[:, c0:c0 + HEAD_DIM], kvc[:, c0:c0 + HEAD_DIM]], axis=0).astype(BF16)
    vcat = jnp.concatenate([kvp[:, D_KV + c0:D_KV + c0 + HEAD_DIM], kvc[:, D_KV + c0:D_KV + c0 + HEAD_DIM]],
                           axis=0).astype(BF16)
    q0 = hk * Q_PER_KV * HEAD_DIM
    qs = jnp.concatenate([q[:, q0 + g * HEAD_DIM:q0 + (g + 1) * HEAD_DIM] for g in range(Q_PER_KV)],
                         axis=0).astype(BF16)
    s = _dot(qs, kcat, NT) * (HEAD_DIM ** -0.5) + bias_ref[hk]
    s = jnp.where(ok4, s, NEG_INF)
    row = lax.broadcasted_iota(jnp.int32, (Q_PER_KV * BLOCK, 1), 0)
    sink = jnp.zeros((Q_PER_KV * BLOCK, 1), F32)
    for g in range(Q_PER_KV):
        sink = jnp.where((row >> BLOCK_SHIFT) == g, sinks_ref[hk * Q_PER_KV + g], sink)
    m = jnp.maximum(jnp.max(s, axis=-1, keepdims=True), sink)
    p = jnp.exp(s - m)
    es = jnp.exp(sink - m)
    inv = 1.0 / (jnp.sum(p, axis=-1, keepdims=True) + es)
    return qs, kcat, vcat, p * inv, es * inv


def _attn_in_specs():
    return [pl.BlockSpec((BLOCK, D_ATTN), lambda n: (n, 0)),
            pl.BlockSpec((BLOCK, 2 * D_KV), lambda n: (jnp.maximum(n - 1, 0), D_ATTN // (2 * D_KV))),
            pl.BlockSpec((BLOCK, 2 * D_KV), lambda n: (n, D_ATTN // (2 * D_KV))),
            _const((N_KV_HEADS, Q_PER_KV * BLOCK, 2 * BLOCK)),
            pl.BlockSpec(memory_space=pltpu.SMEM)]


def _unstack_heads(t):
    return jnp.concatenate([t[g * BLOCK:(g + 1) * BLOCK] for g in range(Q_PER_KV)], axis=1)


def _attn_fwd(proj, bias, sinks):
    def body(q_ref, kvp_ref, kvc_ref, bias_ref, sinks_ref, o_ref):
        n = pl.program_id(0)
        q, kvp, kvc = q_ref[...], kvp_ref[...], kvc_ref[...]
        outs = []
        for hk in range(N_KV_HEADS):
            _, _, vcat, probs, _ = _attn_pieces(n, q, kvp, kvc, bias_ref, sinks_ref, hk)
            outs.append(_unstack_heads(_dot(probs.astype(BF16), vcat)))
        o_ref[...] = jnp.concatenate(outs, axis=1)

    return pl.pallas_call(
        body, name="attn_fwd", grid=(SEQ // BLOCK,),
        in_specs=_attn_in_specs(),
        out_specs=pl.BlockSpec((BLOCK, D_ATTN), lambda n: (n, 0)),
        out_shape=jax.ShapeDtypeStruct((SEQ, D_ATTN), F32),
        compiler_params=_cp(("parallel",)),
    )(proj, proj, proj, bias, sinks)


def _attn_bwd(proj, bias, sinks, dcat):
    nb = SEQ // BLOCK

    def body(q_ref, kvp_ref, kvc_ref, bias_ref, sinks_ref, do_ref, dq_ref, dkv_ref, dbias_ref, dsink_ref, dsacc):
        n = pl.program_id(0)

        @pl.when(n == 0)
        def _():
            dkv_ref[...] = jnp.zeros_like(dkv_ref)
            dbias_ref[...] = jnp.zeros_like(dbias_ref)
            dsacc[...] = jnp.zeros_like(dsacc)

        q, kvp, kvc = q_ref[...], kvp_ref[...], kvc_ref[...]
        do_all = do_ref[...]
        dqs, dks, dvs = [], [], []
        for hk in range(N_KV_HEADS):
            qs, kcat, vcat, probs, psink = _attn_pieces(n, q, kvp, kvc, bias_ref, sinks_ref, hk)
            q0 = hk * Q_PER_KV * HEAD_DIM
            do = jnp.concatenate([do_all[:, q0 + g * HEAD_DIM:q0 + (g + 1) * HEAD_DIM] for g in range(Q_PER_KV)],
                                 axis=0).astype(BF16)
            dprobs = _dot(do, vcat, NT)
            dvs.append(_dot(probs.astype(BF16), do, TN))
            rowdot = jnp.sum(probs * dprobs, axis=-1, keepdims=True)
            ds = probs * (dprobs - rowdot)
            dsacc[hk] += -psink * rowdot
            dbias_ref[hk] += ds
            dsb = (ds * (HEAD_DIM ** -0.5)).astype(BF16)
            dqs.append(_unstack_heads(_dot(dsb, kcat)))
            dks.append(_dot(dsb, qs, TN))
        dq_ref[...] = jnp.concatenate(dqs, axis=1)
        upd = jnp.concatenate(dks + dvs, axis=1)
        cur = pl.multiple_of(n * BLOCK, BLOCK)
        dkv_ref[pl.ds(cur, BLOCK), :] += upd[BLOCK:]

        @pl.when(n > 0)
        def _():
            prev = pl.multiple_of((n - 1) * BLOCK, BLOCK)
            dkv_ref[pl.ds(prev, BLOCK), :] += upd[:BLOCK]

        @pl.when(n == nb - 1)
        def _():
            for hk in range(N_KV_HEADS):
                for g in range(Q_PER_KV):
                    tot = jnp.sum(dsacc[hk, g * BLOCK:(g + 1) * BLOCK, :], axis=0, keepdims=True)
                    h = hk * Q_PER_KV + g
                    dsink_ref[h:h + 1, :] = jnp.broadcast_to(tot, (1, LANES))

    return pl.pallas_call(
        body, name="attn_bwd", grid=(nb,),
        in_specs=_attn_in_specs() + [pl.BlockSpec((BLOCK, D_ATTN), lambda n: (n, 0))],
        out_specs=[pl.BlockSpec((BLOCK, D_ATTN), lambda n: (n, 0)), _const((SEQ, 2 * D_KV)),
                   _const((N_KV_HEADS, Q_PER_KV * BLOCK, 2 * BLOCK)), _const((N_Q_HEADS, LANES))],
        out_shape=[jax.ShapeDtypeStruct((SEQ, D_ATTN), F32), jax.ShapeDtypeStruct((SEQ, 2 * D_KV), F32),
                   jax.ShapeDtypeStruct((N_KV_HEADS, Q_PER_KV * BLOCK, 2 * BLOCK), F32),
                   jax.ShapeDtypeStruct((N_Q_HEADS, LANES), F32)],
        scratch_shapes=[pltpu.VMEM((N_KV_HEADS, Q_PER_KV * BLOCK, 1), F32)],
        compiler_params=_cp(("arbitrary",)),
    )(proj, proj, proj, bias, sinks, dcat)


@jax.custom_vjp
def _head_sum(x):
    ones = _head_ones(LANES)
    return jnp.concatenate([_dot_ind(x[:, c:c + LANES], ones, 2) for c in range(0, x.shape[-1], LANES)], axis=1)


_head_sum.defvjp(lambda x: (_head_sum(x), None), lambda _, ct: (_head_sum(ct),))


@jax.custom_vjp
def _bdot(a, w):
    return _dot(a.astype(BF16), w.astype(BF16))


def _bdot_bwd(res, ct):
    a, w = res
    ctb = ct.astype(BF16)
    return _dot(ctb, w.astype(BF16), NT), _dot(a.astype(BF16), ctb, TN)


_bdot.defvjp(lambda a, w: (_bdot(a, w), (a, w)), _bdot_bwd)


def _sigmoid(x):
    return 0.5 * (jnp.tanh(0.5 * x) + 1.0)


def _softplus(x):
    return jnp.maximum(x, 0.0) + jnp.log(1.0 + jnp.exp(-jnp.abs(x)))


def _rwkv_core(r, k, v, zwa, zg, w0, wdu, a0, wiu, wgu, k_k, k_a):
    w_log = -_softplus(-(w0 + _bdot(jnp.tanh(zwa), wdu))) - 0.5
    decay = jnp.exp(-jnp.exp(w_log))
    a = _sigmoid(a0 + _bdot(zwa, wiu))
    g = _bdot(_sigmoid(zg), wgu)
    kk = k * k_k
    kk = kk / jnp.maximum(jnp.sqrt(_head_sum(kk * kk)), 1e-12)
    k2 = k * (1.0 + (a - 1.0) * k_a)
    return r, decay, k2, v, -kk, kk * a, g


def _rwkv_out(o, r, k2, v, g, lng, lnb, rk):
    mu = _head_sum(o) * (1.0 / HEAD_DIM)
    d = o - mu
    var = _head_sum(d * d) * (1.0 / HEAD_DIM)
    on = d * lax.rsqrt(var + GN_EPS) * lng + lnb
    bonus = _head_sum(r * k2 * rk) * v
    return (on + bonus) * g


P_SPLITS = (0, 512, 1024, 1536, 1664, 1792)
N_PREP_PARAMS = 7
HALO = 8


def _shifted_pieces(i, p_ref, halo_ref, mix_ref):
    p = p_ref[:, P_OFF:]
    prev_row = halo_ref[HALO - 1:HALO, P_OFF:] * jnp.where(i > 0, 1.0, 0.0)
    row = lax.broadcasted_iota(jnp.int32, p.shape, 0)
    pprev = jnp.where(row == 0, prev_row, pltpu.roll(p, 1, 0))
    delta = pprev - p
    ps = p + delta * mix_ref[...]
    return [ps[:, a:b] for a, b in zip(P_SPLITS[:-1], P_SPLITS[1:])], delta


def _prep_in_specs():
    return [_rows(TR, D_IN),
            pl.BlockSpec((HALO, D_IN), lambda i: (jnp.maximum(i * (TR // HALO) - 1, 0), 0)),
            _const((1, RWKV_COLS)), _const((1, D_RWKV)), _const((LANES, D_RWKV)), _const((1, D_RWKV)),
            _const((LANES, D_RWKV)), _const((LANES, D_RWKV)), _const((1, D_RWKV)), _const((1, D_RWKV))]


def _rwkv_prep(proj, mix, prm):
    def body(p_ref, halo_ref, mix_ref, *refs):
        prm_refs, outs = refs[:N_PREP_PARAMS], refs[N_PREP_PARAMS:]
        pieces, _ = _shifted_pieces(pl.program_id(0), p_ref, halo_ref, mix_ref)
        vals = _rwkv_core(*pieces, *[t[...] for t in prm_refs])
        for ref, val in zip(outs, vals):
            ref[...] = val

    return pl.pallas_call(
        body, name="rwkv_prep", grid=(SEQ // TR,),
        in_specs=_prep_in_specs(),
        out_specs=[_rows(TR, D_RWKV)] * 7,
        out_shape=[jax.ShapeDtypeStruct((SEQ, D_RWKV), F32)] * 7,
        compiler_params=_cp(("parallel",)),
    )(proj, proj, mix, *prm)


def _rwkv_prep_bwd(proj, mix, prm, cts):
    def body(p_ref, halo_ref, mix_ref, *refs):
        i = pl.program_id(0)
        prm_refs = refs[:N_PREP_PARAMS]
        ct_refs = refs[N_PREP_PARAMS:N_PREP_PARAMS + 10]
        dps_ref, dmix_ref = refs[N_PREP_PARAMS + 10:N_PREP_PARAMS + 12]
        dprm_refs = refs[N_PREP_PARAMS + 12:]
        pieces, delta = _shifted_pieces(i, p_ref, halo_ref, mix_ref)
        _, vjp = jax.vjp(_rwkv_core, *pieces, *[t[...] for t in prm_refs])
        dr1, dr2, dw, dk1, dk2, dv1, dv2, dkkn, db, dg = [t[...] for t in ct_refs]
        grads = vjp((dr1 + dr2, dw, dk1 + dk2, dv1 + dv2, dkkn, db, dg))
        dps = jnp.concatenate(grads[:5], axis=1)
        dps_ref[...] = dps

        @pl.when(i == 0)
        def _():
            dmix_ref[...] = jnp.zeros_like(dmix_ref)
            for ref in dprm_refs:
                ref[...] = jnp.zeros_like(ref)

        dmix_ref[...] += jnp.sum(dps * delta, axis=0, keepdims=True)
        for ref, gval in zip(dprm_refs, grads[5:]):
            ref[...] += gval

    prm_shapes = [(1, D_RWKV), (LANES, D_RWKV), (1, D_RWKV), (LANES, D_RWKV), (LANES, D_RWKV), (1, D_RWKV), (1, D_RWKV)]
    return pl.pallas_call(
        body, name="rwkv_prep_bwd", grid=(SEQ // TR,),
        in_specs=_prep_in_specs() + [_rows(TR, D_RWKV)] * 10,
        out_specs=[_rows(TR, RWKV_COLS), _const((1, RWKV_COLS))] + [_const(s) for s in prm_shapes],
        out_shape=[jax.ShapeDtypeStruct((SEQ, RWKV_COLS), F32), jax.ShapeDtypeStruct((1, RWKV_COLS), F32)]
        + [jax.ShapeDtypeStruct(s, F32) for s in prm_shapes],
        compiler_params=_cp(("arbitrary",)),
    )(proj, proj, mix, *prm, *cts)


def _rwkv_post(o, r, k2, v, g, lng, lnb, rk, attn):
    def body(o_ref, r_ref, k_ref, v_ref, g_ref, lng_ref, lnb_ref, rk_ref, attn_ref, cat_ref):
        rw = _rwkv_out(*[t[...] for t in (o_ref, r_ref, k_ref, v_ref, g_ref, lng_ref, lnb_ref, rk_ref)])
        cat_ref[...] = jnp.concatenate([attn_ref[...], rw], axis=1).astype(BF16)

    return pl.pallas_call(
        body, name="rwkv_post", grid=(SEQ // TR,),
        in_specs=[_rows(TR, D_RWKV)] * 5 + [_const((1, D_RWKV))] * 3 + [_rows(TR, D_ATTN)],
        out_specs=_rows(TR, D_MODEL),
        out_shape=jax.ShapeDtypeStruct((SEQ, D_MODEL), BF16),
        compiler_params=_cp(("parallel",)),
    )(o, r, k2, v, g, lng, lnb, rk, attn)


def _rwkv_post_bwd(o, r, k2, v, g, lng, lnb, rk, dcat):
    def body(o_ref, r_ref, k_ref, v_ref, g_ref, lng_ref, lnb_ref, rk_ref, dcat_ref,
             do_ref, dr_ref, dk_ref, dv_ref, dg_ref, dlng_ref, dlnb_ref, drk_ref):
        i = pl.program_id(0)
        args = [t[...] for t in (o_ref, r_ref, k_ref, v_ref, g_ref, lng_ref, lnb_ref, rk_ref)]
        _, vjp = jax.vjp(_rwkv_out, *args)
        grads = vjp(dcat_ref[:, D_ATTN:])
        for ref, gval in zip((do_ref, dr_ref, dk_ref, dv_ref, dg_ref), grads[:5]):
            ref[...] = gval

        @pl.when(i == 0)
        def _():
            for ref in (dlng_ref, dlnb_ref, drk_ref):
                ref[...] = jnp.zeros_like(ref)

        for ref, gval in zip((dlng_ref, dlnb_ref, drk_ref), grads[5:]):
            ref[...] += gval

    return pl.pallas_call(
        body, name="rwkv_post_bwd", grid=(SEQ // TR,),
        in_specs=[_rows(TR, D_RWKV)] * 5 + [_const((1, D_RWKV))] * 3 + [_rows(TR, D_MODEL)],
        out_specs=[_rows(TR, D_RWKV)] * 5 + [_const((1, D_RWKV))] * 3,
        out_shape=[jax.ShapeDtypeStruct((SEQ, D_RWKV), F32)] * 5 + [jax.ShapeDtypeStruct((1, D_RWKV), F32)] * 3,
        compiler_params=_cp(("arbitrary",)),
    )(o, r, k2, v, g, lng, lnb, rk, dcat)


def _assemble_dproj(dq, dkv, dps, mix):
    last = SEQ // HALO - 1

    def body(dq_ref, dkv_ref, dps_ref, nxt_ref, mix_ref, o_ref):
        i = pl.program_id(0)
        dps = dps_ref[...]
        mixv = mix_ref[...]
        nxt_row = nxt_ref[0:1, :] * jnp.where(i < SEQ // TR - 1, 1.0, 0.0)
        row = lax.broadcasted_iota(jnp.int32, dps.shape, 0)
        up = jnp.where(row == TR - 1, nxt_row, pltpu.roll(dps, TR - 1, 0))
        dp = dps * (1.0 - mixv) + up * mixv
        o_ref[...] = jnp.concatenate([dq_ref[...], dkv_ref[...], dp], axis=1).astype(BF16)

    return pl.pallas_call(
        body, name="assemble_dproj", grid=(SEQ // TR,),
        in_specs=[_rows(TR, D_ATTN), _rows(TR, 2 * D_KV), _rows(TR, RWKV_COLS),
                  pl.BlockSpec((HALO, RWKV_COLS), lambda i: (jnp.minimum((i + 1) * (TR // HALO), last), 0)),
                  _const((1, RWKV_COLS))],
        out_specs=_rows(TR, D_IN),
        out_shape=jax.ShapeDtypeStruct((SEQ, D_IN), BF16),
        compiler_params=_cp(("parallel",)),
    )(dq, dkv, dps, dps, mix)


N_PAIR = D_RWKV // LANES
CHUNK = 64
N_CHUNK = SEQ // CHUNK
GROUP = 8
STATE = (N_PAIR, HEAD_DIM, LANES)


def _lane_sums(lhs_tiles, ones2):
    out = _dot(jnp.concatenate(lhs_tiles, axis=0), ones2)
    return [out[i * HEAD_DIM:(i + 1) * HEAD_DIM] for i in range(len(lhs_tiles))]


def _seg_sum(xs, ones2):
    return _lane_sums([jnp.concatenate(_split(x, 2), axis=1) for x in xs], ones2)


def _seg_sum_rows(xs, ones2):
    out = _dot(jnp.concatenate(_split(jnp.concatenate(xs, axis=0), 2), axis=1), ones2)
    return [out[i * GROUP:(i + 1) * GROUP] for i in range(len(xs))]


def _col_form(rows, diag, ones2):
    zero = jnp.zeros((HEAD_DIM, LANES), BF16)
    tiles = []
    for row in rows:
        hi = row.astype(BF16)
        lo = (row - hi.astype(F32)).astype(BF16)
        tiles.append(jnp.concatenate(
            [jnp.where(diag, jnp.broadcast_to(part, (HEAD_DIM, LANES)), zero) for part in (hi, lo)], axis=1))
    return _lane_sums(tiles, ones2)


def _scan_consts():
    ones2 = jnp.concatenate([_head_ones(LANES)] * 2, axis=0)
    sub = lax.broadcasted_iota(jnp.int32, (HEAD_DIM, LANES), 0)
    lane_in_head = lax.broadcasted_iota(jnp.int32, (HEAD_DIM, LANES), 1) & (HEAD_DIM - 1)
    return ones2, lane_in_head == sub, lane_in_head


def _rows_of_columns(tile):
    t = tile.T
    return jnp.concatenate([t[:CHUNK], t[HEAD_DIM:HEAD_DIM + CHUNK]], axis=1)


def _pair(j):
    return slice(j * LANES, (j + 1) * LANES)


def _scan_fwd(r, w, k, v, kkn, b):
    def body(r_ref, w_ref, k_ref, v_ref, kkn_ref, b_ref, o_ref, st_ref, sa_ref, s_scr):
        c = pl.program_id(0)
        ones2, diag, lane_in_head = _scan_consts()

        @pl.when(c == 0)
        def _():
            s_scr[...] = jnp.zeros_like(s_scr)

        def group(gi, carry):
            row0 = pl.multiple_of(gi * GROUP, GROUP)
            states, ocols = list(carry[:N_PAIR]), list(carry[N_PAIR:])
            tiles = [[t[pl.ds(row0, GROUP), _pair(j)] for t in (r_ref, w_ref, k_ref, v_ref, kkn_ref, b_ref)]
                     for j in range(N_PAIR)]
            def row(j, name, u):
                return tiles[j]["rwkvnb".index(name)][u:u + 1]

            def emit_out(u, after):
                outs = _seg_sum([s[j] * row(j, "r", u + d) for d, s in enumerate(after) for j in range(N_PAIR)], ones2)
                for d in range(2):
                    here = lane_in_head == gi * GROUP + u + d
                    for j in range(N_PAIR):
                        ocols[j] = jnp.where(here, outs[d * N_PAIR + j], ocols[j])

            def vcols_of(u):
                cols = _col_form([row(j, "v", u + d) for d in range(2) for j in range(N_PAIR)], diag, ones2)
                return cols[:N_PAIR], cols[N_PAIR:]

            n_next = [pltpu.roll(tiles[j][4], GROUP - 1, 0) for j in range(N_PAIR)]
            dots = _seg_sum_rows([tiles[j][5] * n_next[j] for j in range(N_PAIR)]
                                 + [tiles[j][2] * n_next[j] for j in range(N_PAIR)], ones2)
            b_n, k_n = dots[:N_PAIR], dots[N_PAIR:]
            w_n = [tiles[j][1] * n_next[j] for j in range(N_PAIR)]

            vcols = vcols_of(0)
            after = None
            for u in range(0, GROUP, 2):
                prods = _seg_sum([states[j] * row(j, "n", u) for j in range(N_PAIR)]
                                 + [states[j] * w_n[j][u:u + 1] for j in range(N_PAIR)], ones2)
                if after is not None:
                    emit_out(u - 2, after)
                nxt = vcols_of(u + 2) if u + 2 < GROUP else None
                first, second = [], []
                for j in range(N_PAIR):
                    sa1 = prods[j]
                    sa2 = prods[N_PAIR + j] + sa1 * b_n[j][u:u + 1] + vcols[0][j] * k_n[j][u:u + 1]
                    s1 = states[j] * row(j, "w", u) + sa1 * row(j, "b", u) + vcols[0][j] * row(j, "k", u)
                    s2 = s1 * row(j, "w", u + 1) + sa2 * row(j, "b", u + 1) + vcols[1][j] * row(j, "k", u + 1)
                    st_ref[row0 + u, j] = s1
                    sa_ref[row0 + u, j] = sa1
                    st_ref[row0 + u + 1, j] = s2
                    sa_ref[row0 + u + 1, j] = sa2
                    first.append(s1)
                    second.append(s2)
                    states[j] = s2
                after, vcols = (first, second), nxt
            emit_out(GROUP - 2, after)
            return tuple(states + ocols)

        zero = jnp.zeros((HEAD_DIM, LANES), F32)
        fin = lax.fori_loop(0, CHUNK // GROUP, group, tuple(s_scr[j] for j in range(N_PAIR)) + (zero,) * N_PAIR)
        for j in range(N_PAIR):
            s_scr[j] = fin[j]
            o_ref[:, _pair(j)] = _rows_of_columns(fin[N_PAIR + j])

    blk = pl.BlockSpec((CHUNK, D_RWKV), lambda c: (c, 0))
    per_step = pl.BlockSpec((CHUNK,) + STATE, lambda c: (c, 0, 0, 0))
    return pl.pallas_call(
        body, name="rwkv_scan_fwd", grid=(N_CHUNK,),
        in_specs=[blk] * 6,
        out_specs=[blk, per_step, per_step],
        out_shape=[jax.ShapeDtypeStruct((SEQ, D_RWKV), F32)] + [jax.ShapeDtypeStruct((SEQ,) + STATE, F32)] * 2,
        scratch_shapes=[pltpu.VMEM(STATE, F32)],
        compiler_params=_cp(("arbitrary",)),
    )(r, w, k, v, kkn, b)


def _scan_bwd(r, w, k, v, kkn, b, do, states, sas, ds_in, prev, name, first_chunk, n_chunks):
    top = first_chunk + n_chunks - 1

    def body(r_ref, w_ref, k_ref, v_ref, kkn_ref, b_ref, do_ref, st_ref, before_ref, sa_ref, ds_in_ref, *rest):
        dr_ref, dw_ref, dk_ref, dv_ref, dkkn_ref, db_ref, ds_out_ref, ds_scr = rest[-8:]
        i = pl.program_id(0)
        ones2, diag, lane_in_head = _scan_consts()

        @pl.when(i == 0)
        def _():
            ds_scr[...] = ds_in_ref[...]

        entry = [before_ref[0, j] * jnp.where(i < top, 1.0, 0.0) for j in range(N_PAIR)]

        def reverse(gr, carry):
            gi = CHUNK // GROUP - 1 - gr
            row0 = pl.multiple_of(gi * GROUP, GROUP)
            dstates, dvcols = list(carry[:N_PAIR]), list(carry[N_PAIR:])
            tiles = [[t[pl.ds(row0, GROUP), _pair(j)]
                      for t in (r_ref, w_ref, k_ref, v_ref, kkn_ref, b_ref, do_ref)] for j in range(N_PAIR)]
            rows = [[[None] * GROUP for _ in range(5)] for _ in range(N_PAIR)]

            def row(j, name, u):
                return tiles[j]["rwkvnbd".index(name)][u:u + 1]

            def cols_of(u):
                cols = _col_form([row(j, name, u - d) for d in range(2) for name in "dv" for j in range(N_PAIR)],
                                 diag, ones2)
                return [[(cols[(2 * d) * N_PAIR + j], cols[(2 * d + 1) * N_PAIR + j]) for j in range(N_PAIR)]
                        for d in range(2)]

            def emit_dv(u, dsps):
                outs = _seg_sum([dsp[j] * row(j, "k", u - d) for d, dsp in enumerate(dsps) for j in range(N_PAIR)], ones2)
                for d in range(2):
                    here = lane_in_head == gi * GROUP + u - d
                    for j in range(N_PAIR):
                        dvcols[j] = jnp.where(here, outs[d * N_PAIR + j], dvcols[j])

            b_prev = [pltpu.roll(tiles[j][5], 1, 0) for j in range(N_PAIR)]
            dots = _seg_sum_rows([tiles[j][4] * b_prev[j] for j in range(N_PAIR)]
                                 + [tiles[j][0] * tiles[j][5] for j in range(N_PAIR)], ones2)
            n_b, r_b = dots[:N_PAIR], dots[N_PAIR:]
            w_b = [tiles[j][1] * b_prev[j] for j in range(N_PAIR)]

            def outputs(u, j, dsp, dsa, docol, vcol):
                tl = gi * GROUP + u
                if u > 0:
                    s_prev = st_ref[tl - 1, j]
                else:
                    s_prev = jnp.where(gi == 0, entry[j], st_ref[jnp.maximum(tl - 1, 0), j])
                rows[j][0][u] = jnp.sum(st_ref[tl, j] * docol, axis=0, keepdims=True)
                rows[j][1][u] = jnp.sum(dsp * s_prev, axis=0, keepdims=True)
                rows[j][2][u] = jnp.sum(dsp * vcol, axis=0, keepdims=True)
                rows[j][3][u] = jnp.sum(s_prev * dsa, axis=0, keepdims=True)
                rows[j][4][u] = jnp.sum(dsp * sa_ref[tl, j], axis=0, keepdims=True)

            cols = cols_of(GROUP - 1)
            before = None
            for u in range(GROUP - 1, 0, -2):
                dsp1 = [dstates[j] + cols[0][j][0] * row(j, "r", u) for j in range(N_PAIR)]
                prods = _seg_sum([dsp1[j] * row(j, "b", u) for j in range(N_PAIR)]
                                 + [dsp1[j] * w_b[j][u:u + 1] for j in range(N_PAIR)], ones2)
                if before is not None:
                    emit_dv(u + 2, before)
                nxt = cols_of(u - 2) if u >= 2 else None
                dsp2 = []
                for j in range(N_PAIR):
                    dsa1 = prods[j]
                    dsa2 = prods[N_PAIR + j] + dsa1 * n_b[j][u:u + 1] + cols[1][j][0] * r_b[j][u - 1:u]
                    mid = dsp1[j] * row(j, "w", u) + dsa1 * row(j, "n", u) + cols[1][j][0] * row(j, "r", u - 1)
                    outputs(u, j, dsp1[j], dsa1, *cols[0][j])
                    outputs(u - 1, j, mid, dsa2, *cols[1][j])
                    dstates[j] = mid * row(j, "w", u - 1) + dsa2 * row(j, "n", u - 1)
                    dsp2.append(mid)
                before, cols = (dsp1, dsp2), nxt
            emit_dv(1, before)
            for j in range(N_PAIR):
                for ref, rr in zip((dr_ref, dw_ref, dk_ref, dkkn_ref, db_ref), rows[j]):
                    ref[pl.ds(row0, GROUP), _pair(j)] = jnp.concatenate(rr, axis=0)
            return tuple(dstates + dvcols)

        zero = jnp.zeros((HEAD_DIM, LANES), F32)
        dfin = lax.fori_loop(0, CHUNK // GROUP, reverse, tuple(ds_scr[j] for j in range(N_PAIR)) + (zero,) * N_PAIR)
        for j in range(N_PAIR):
            ds_scr[j] = dfin[j]
            dv_ref[:, _pair(j)] = _rows_of_columns(dfin[N_PAIR + j])

        @pl.when(i == n_chunks - 1)
        def _():
            ds_out_ref[...] = ds_scr[...]

    blk = pl.BlockSpec((CHUNK, D_RWKV), lambda i: (top - i, 0))
    per_step = pl.BlockSpec((CHUNK,) + STATE, lambda i: (top - i, 0, 0, 0))
    step_before = pl.BlockSpec((1,) + STATE, lambda i: (jnp.maximum((top - i) * CHUNK - 1, 0), 0, 0, 0))
    prev = [] if prev is None else list(prev)
    outs = pl.pallas_call(
        body, name=name, grid=(n_chunks,),
        in_specs=[blk] * 7 + [per_step, step_before, per_step, _const(STATE)] + [ANY] * len(prev),
        out_specs=[blk] * 6 + [_const(STATE)],
        out_shape=[jax.ShapeDtypeStruct((SEQ, D_RWKV), F32)] * 6 + [jax.ShapeDtypeStruct(STATE, F32)],
        scratch_shapes=[pltpu.VMEM(STATE, F32)],
        input_output_aliases={11 + t: t for t in range(len(prev))},
        compiler_params=_cp(("arbitrary",)),
    )(r, w, k, v, kkn, b, do, states, states, sas, ds_in, *prev)
    return outs[:6], outs[6]


def _stacked(rows, cols, pick):
    return pl.BlockSpec((None, rows, cols), pick)


def _local_step(x, target, sm, win_st):
    def tied(t, token):
        return t if token is None else t + token[0:1, 0:1].reshape((1,) * t.ndim)

    zpad = jnp.zeros((LORA_DECAY, D_RWKV), F32)
    prm = [sm["w0"], jnp.concatenate([sm["w_decay_up"], zpad], axis=0), sm["a0"],
           jnp.concatenate([zpad, sm["w_iclr_up"]], axis=0), sm["w_gate_up"], sm["k_k"], sm["k_a"]]
    mix = sm["rwkv_shift_mix"]
    onehot = jnp.asarray(_t5_onehot(), BF16)
    sinks = sm["sinks"].reshape(N_Q_HEADS)
    lng, lnb, rk = sm["ln_x_g"], sm["ln_x_b"], sm["r_k"].reshape(1, D_RWKV)

    h1 = _norm_cast(x, sm["norm_mix_pre"], "norm_in")
    proj = _matmul(h1, win_st, "nn", "proj", m=SEQ, n=D_IN, k=D_MODEL, tm=SEQ, tn=640,
                   b_spec=_stacked(D_MODEL, 640, lambda i, j: (j, 0, 0)))
    bias = _bias_table(sm["rel_bias"].T, onehot).reshape(N_KV_HEADS, Q_PER_KV * BLOCK, 2 * BLOCK)
    attn = _attn_fwd(proj, bias, sinks)
    r, w, k2, v, kkn, b, g = _rwkv_prep(proj, mix, prm)
    o, states, sas = _scan_fwd(r, w, k2, v, kkn, b)
    wout, wup_st, wdown = yield ("rest_weights", o)
    cat = _rwkv_post(o, r, k2, v, g, lng, lnb, rk, attn)
    mixo = _matmul(cat, wout, "nn", "out_proj", m=SEQ, n=D_MODEL, k=D_MODEL, tm=SEQ, tn=512)
    x2, h3 = _mix_norm(x, mixo, sm["norm_mix_post"], sm["norm_ffn_pre"])
    u_gate, u_val, act = _ffn_up_act(h3, wup_st, sm["conv_w"], sm["conv_b"])
    f = _matmul(act, wdown, "nn", "ffn_down", m=SEQ, n=D_MODEL, k=D_FF, tm=1024, tn=512)
    loss, dy, df, d_g4 = _loss_head(x2, f, sm["norm_ffn_post"], target)

    d_wdown = _matmul(act, df, "tn", "d_wdown", m=D_FF, n=D_MODEL, k=SEQ, tm=512, tn=D_MODEL)
    du, d_convw, d_convb = _ffn_act_bwd(u_gate, u_val, df, wdown, sm["conv_w"], sm["conv_b"])
    d_convw = d_convw.transpose(1, 0, 2).reshape(3, 2 * D_FF)
    d_convb = d_convb.reshape(1, 2 * D_FF)
    dh3 = _matmul_nt_shards(du, wup_st, "d_h3", m=SEQ, n=D_MODEL, tm=512, tn=512,
                            a_spec=pl.BlockSpec((2, 512, D_FF), lambda i, j: (0, i, 0)),
                            a_piece=lambda ref, s: ref[s // 2, :, (s % 2) * 2048:(s % 2 + 1) * 2048])
    d_wup = _matmul(h3, du, "tn", "d_wup", m=D_MODEL, n=2 * D_FF, k=SEQ, tm=D_MODEL, tn=512,
                    b_spec=pl.BlockSpec((None, SEQ, 512), lambda i, j: (j // 8, 0, j % 8)),
                    out=((N_CHIPS, D_MODEL, 2048), _stacked(D_MODEL, 512, lambda i, j: (j // 4, 0, j % 4))))
    dx2, dmix, d_g2, d_g3 = _mid_bwd(x2, mixo, dy, dh3, sm["norm_mix_post"], sm["norm_ffn_pre"])
    dcat = _matmul(dmix, wout, "nt", "d_cat", m=SEQ, n=D_MODEL, k=D_MODEL, tm=SEQ, tn=512)
    d_wout = _matmul(cat, dmix, "tn", "d_wout", m=D_MODEL, n=D_MODEL, k=SEQ, tm=512, tn=D_MODEL)
    token = yield ("grads_a", (d_wdown, d_wup, d_wout))
    do, dr_p, dk_p, dv_p, dg, d_lng, d_lnb, d_rk = _rwkv_post_bwd(o, r, k2, v, g, lng, tied(lnb, token), rk, dcat)
    half = N_CHUNK // 2
    ds_end = jnp.zeros(STATE, F32)
    late, ds_mid = _scan_bwd(r, w, k2, v, kkn, b, do, states, sas, ds_end, None, "rwkv_scan_bwd_late", half, half)
    token = yield ("seam_1", ds_mid)
    scan_cts, ds_first = _scan_bwd(r, w, k2, v, kkn, b, do, states, sas, tied(ds_mid, token), late,
                                   "rwkv_scan_bwd_early", 0, half)
    dr_s, dw_s, dk_s, dv_s, dkkn_s, db_s = scan_cts
    token = yield ("seam_2", ds_first)
    prep_grads = _rwkv_prep_bwd(proj, tied(mix, token), prm,
                                (dr_s, dr_p, dw_s, dk_s, dk_p, dv_s, dv_p, dkkn_s, db_s, dg))
    dps, d_mix, d_w0, d_wdu, d_a0, d_wiu, d_wgu, d_kk, d_ka = prep_grads
    dq, dkv, dbias, dsink = _attn_bwd(proj, bias, sinks, dcat)
    d_relb = _bias_table_bwd(dbias.reshape(N_Q_HEADS, N_REL), onehot).T
    dproj = _assemble_dproj(dq, dkv, dps, mix)
    d_win = _matmul(h1, dproj, "tn", "d_win", m=D_MODEL, n=D_IN, k=SEQ, tm=D_MODEL, tn=640,
                    out=((N_CHIPS, D_MODEL, 640), _stacked(D_MODEL, 640, lambda i, j: (j, 0, 0))))
    token = yield ("grads_b", d_win)
    dh1 = _matmul_nt_shards(dproj, win_st, "d_h1", m=SEQ, n=D_MODEL, tm=1024, tn=D_MODEL,
                            a_spec=pl.BlockSpec((1024, D_IN), lambda i, j: (i, 0)),
                            a_piece=lambda ref, s: ref[:, s * 640:(s + 1) * 640])
    grad_x, d_g1 = _first_bwd(x, dx2, dh1, tied(sm["norm_mix_pre"], token))

    grads = {
        "norm_mix_pre": d_g1, "norm_mix_post": d_g2, "norm_ffn_pre": d_g3, "norm_ffn_post": d_g4,
        "w_in": d_win, "rel_bias": d_relb, "sinks": dsink[:, 0].reshape(1, N_Q_HEADS),
        "rwkv_shift_mix": d_mix, "w0": d_w0, "w_decay_up": d_wdu[:LORA_DECAY], "a0": d_a0,
        "w_iclr_up": d_wiu[LORA_DECAY:], "w_gate_up": d_wgu, "k_k": d_kk, "k_a": d_ka,
        "r_k": d_rk.reshape(1, N_Q_HEADS, HEAD_DIM), "ln_x_g": d_lng, "ln_x_b": d_lnb,
        "w_out": d_wout, "w_ffn_up": d_wup, "conv_w": d_convw, "conv_b": d_convb, "w_ffn_down": d_wdown,
    }
    return loss, grad_x, grads


def _place():
    x, y, c = lax.axis_index("x"), lax.axis_index("y"), lax.axis_index("c")
    chips = [(1 - x, y), (x, 1 - y), (1 - x, 1 - y)]
    return x, y, c, chips


def _remote(src, dst, sems, idx, to):
    return pltpu.make_async_remote_copy(src_ref=src, dst_ref=dst, send_sem=sems[0].at[idx], recv_sem=sems[1].at[idx],
                                        device_id=to, device_id_type=MESH)


ROW_ALIGN = 16


def _half(c, rows):
    return pl.ds(pl.multiple_of(c * (rows // 2), ROW_ALIGN), rows // 2)


def _gather_weights(big, small):
    nb, ns = len(big), len(small)

    def body(*refs):
        ins, outs = refs[:nb + ns], refs[nb + ns:2 * (nb + ns)]
        ici, d2d, sml, loc = refs[2 * (nb + ns):2 * (nb + ns) + 2], refs[-5:-3], refs[-3:-1], refs[-1]
        x, y, c, chips = _place()
        me = 2 * x + y
        sib = (x, y, 1 - c)
        local = [pltpu.make_async_copy(ins[a], outs[a].at[me], loc.at[a]) for a in range(nb + ns)]
        for cp in local:
            cp.start()
        sends = []
        for a in range(nb):
            rows = _half(c, big[a].shape[0])
            for kk, chip in enumerate(chips):
                sends.append(_remote(ins[a].at[rows], outs[a].at[me, rows], ici, a * 3 + kk, (*chip, c)))
        for a in range(ns):
            for kk, chip in enumerate(chips):
                sends.append(_remote(ins[nb + a], outs[nb + a].at[me], sml, a * 3 + kk, (*chip, c)))
        for cp in sends:
            cp.start()
        passed = []
        for a in range(nb):
            rows = _half(c, big[a].shape[0])
            for kk, (px, py) in enumerate(chips):
                got = outs[a].at[2 * px + py, rows]
                _remote(got, got, ici, a * 3 + kk, sib).wait_recv()
                fwd = _remote(got, got, d2d, a * 3 + kk, sib)
                fwd.start()
                passed.append(fwd)
        for a in range(nb):
            other = _half(1 - c, big[a].shape[0])
            for kk, (px, py) in enumerate(chips):
                land = outs[a].at[2 * px + py, other]
                _remote(land, land, d2d, a * 3 + kk, sib).wait_recv()
        for a in range(ns):
            for kk, (px, py) in enumerate(chips):
                land = outs[nb + a].at[2 * px + py]
                _remote(land, land, sml, a * 3 + kk, sib).wait_recv()
        for cp in sends + passed:
            cp.wait_send()
        for cp in local:
            cp.wait()

    arrs = list(big) + list(small)
    in_vmem = pl.BlockSpec(memory_space=pltpu.VMEM)
    return pl.pallas_call(
        body, name="gather_weights",
        in_specs=[in_vmem] * len(arrs), out_specs=[in_vmem] * len(arrs),
        out_shape=[jax.ShapeDtypeStruct((N_CHIPS,) + t.shape, t.dtype) for t in arrs],
        scratch_shapes=[pltpu.SemaphoreType.DMA((3 * nb,)), pltpu.SemaphoreType.DMA((3 * nb,)),
                        pltpu.SemaphoreType.DMA((3 * nb,)), pltpu.SemaphoreType.DMA((3 * nb,)),
                        pltpu.SemaphoreType.DMA((3 * ns,)), pltpu.SemaphoreType.DMA((3 * ns,)),
                        pltpu.SemaphoreType.DMA((nb + ns,))],
        compiler_params=pltpu.CompilerParams(has_side_effects=True, vmem_limit_bytes=VMEM_LIMIT),
    )(*arrs)


HBM = pl.BlockSpec(memory_space=pltpu.HBM)
SEM = pl.BlockSpec(memory_space=pltpu.SEMAPHORE)
EFFECT = pltpu.SideEffectType.DATAFLOW_SIDE_EFFECTING


def _copies_start(name, bufs, plan, n):
    nb = len(bufs)

    def body(*refs):
        ins, sems, token = refs[:nb], refs[nb:nb + 2 * n], refs[-1]
        for kk, (src, dst, dev) in enumerate(plan(ins)):
            pltpu.make_async_remote_copy(src_ref=src, dst_ref=dst, send_sem=sems[2 * kk], recv_sem=sems[2 * kk + 1],
                                         device_id=dev, device_id_type=MESH).start()
        token[...] = jnp.zeros_like(token)

    outs = pl.pallas_call(
        body, name=name,
        out_shape=tuple([pltpu.SemaphoreType.DMA(())] * (2 * n) + [pltpu.HBM(t.shape, t.dtype) for t in bufs]
                        + [jax.ShapeDtypeStruct((8, LANES), F32)]),
        in_specs=[HBM] * nb,
        out_specs=tuple([SEM] * (2 * n) + [HBM] * nb + [pl.BlockSpec(memory_space=pltpu.VMEM)]),
        input_output_aliases={t: 2 * n + t for t in range(nb)},
        compiler_params=pltpu.CompilerParams(has_side_effects=EFFECT),
    )(*[pltpu.with_memory_space_constraint(t, pltpu.HBM) for t in bufs])
    return outs[:2 * n], outs[2 * n:2 * n + nb], outs[-1]


def _copies_wait(name, sems, bufs, plan, n, after):
    nb = len(bufs)
    after = list(after) if isinstance(after, (list, tuple)) else [after]

    def body(*refs):
        ins, sem_refs = refs[:nb], refs[nb:nb + 2 * n]
        for kk, (src, dst, dev) in enumerate(plan(ins)):
            cp = pltpu.make_async_remote_copy(src_ref=src, dst_ref=dst, send_sem=sem_refs[2 * kk],
                                              recv_sem=sem_refs[2 * kk + 1], device_id=dev, device_id_type=MESH)
            cp.wait_send()
            cp.wait_recv()

    return pl.pallas_call(
        body, name=name,
        out_shape=tuple(pltpu.HBM(t.shape, t.dtype) for t in bufs),
        in_specs=[HBM] * nb + [SEM] * (2 * n) + [ANY] * len(after),
        out_specs=tuple([HBM] * nb),
        input_output_aliases={t: t for t in range(nb)},
        compiler_params=pltpu.CompilerParams(has_side_effects=EFFECT),
    )(*bufs, *sems, *after)


def _plan_gather(n_w):
    def plan(refs):
        x, y, c, chips = _place()
        me = 2 * x + y
        return [(refs[a], refs[n_w + a].at[me], (*chip, c)) for a in range(n_w) for chip in chips + [(x, y)]]
    return plan


def _plan_pair_halves(n_g, rows):
    def plan(refs):
        x, y, c, _ = _place()
        return [(refs[a].at[:, _half(1 - c, rows[a])], refs[n_g + a], (x, y, 1 - c)) for a in range(n_g)]
    return plan


def _plan_chip_parts(n_g):
    def plan(refs):
        x, y, c, chips = _place()
        me = 2 * x + y
        return [(refs[a].at[2 * px + py], refs[n_g + a].at[me], (px, py, c))
                for a in range(n_g) for (px, py) in chips]
    return plan


def _plan_pair_fill(n_g, rows):
    def plan(refs):
        x, y, c, _ = _place()
        return [(refs[a].at[_half(c, rows[a])], refs[a].at[_half(c, rows[a])], (x, y, 1 - c)) for a in range(n_g)]
    return plan


def _pair_add(g, got, name):
    _, rows, cols = g.shape
    hr = rows // 2
    tr = min(hr, 256)
    nb = hr // tr

    def body(g_ref, got_ref, p_ref, own_ref):
        val = (g_ref[...] + got_ref[...]).astype(BF16)
        p_ref[...] = val

        @pl.when(pl.program_id(1) == 2 * lax.axis_index("x") + lax.axis_index("y"))
        def _():
            own_ref[...] = val

    def mine(i, s):
        return (2 * lax.axis_index("x") + lax.axis_index("y"), i, 0)

    return pl.pallas_call(
        body, name=name, grid=(nb, N_CHIPS),
        in_specs=[pl.BlockSpec((None, tr, cols), lambda i, s: (s, lax.axis_index("c") * nb + i, 0)),
                  pl.BlockSpec((None, tr, cols), lambda i, s: (s, i, 0))],
        out_specs=[pl.BlockSpec((None, tr, cols), lambda i, s: (s, i, 0)), pl.BlockSpec((None, tr, cols), mine)],
        out_shape=[jax.ShapeDtypeStruct((N_CHIPS, hr, cols), BF16)] * 2,
        compiler_params=_cp(("parallel", "arbitrary")),
    )(g, got)


def _chip_sum(parts, name):
    _, hr, cols = parts.shape
    tr = min(hr, 128)
    nb = hr // tr

    def body(t_ref, o_ref):
        part = [t_ref[s].astype(F32) for s in range(N_CHIPS)]
        o_ref[...] = ((part[0] + part[1]) + part[2]) + part[3]

    return pl.pallas_call(
        body, name=name, grid=(nb,),
        in_specs=[pl.BlockSpec((N_CHIPS, tr, cols), lambda i: (0, i, 0))],
        out_specs=pl.BlockSpec((tr, cols), lambda i: (lax.axis_index("c") * nb + i, 0)),
        out_shape=jax.ShapeDtypeStruct((2 * hr, cols), F32),
        compiler_params=_cp(("parallel",)),
    )(parts)


class _Reduction:
    def __init__(self, tag, rows):
        self.tag, self.n, self.rows = tag, len(rows), rows
        self.plans = (_plan_pair_halves(self.n, rows), _plan_chip_parts(self.n), _plan_pair_fill(self.n, rows))
        self.flight = None

    def _name(self, what):
        return f"grad_{self.tag}_{what}"

    def start(self, gs):
        gots = [lax.empty((N_CHIPS, t.shape[1] // 2, t.shape[2]), F32) for t in gs]
        self.flight = _copies_start(self._name("pair_start"), list(gs) + gots, self.plans[0], self.n)
        return self.flight[2]

    def after_pair(self, after):
        sems, bufs, _ = self.flight
        out = _copies_wait(self._name("pair_wait"), sems, bufs, self.plans[0], self.n, after)
        sums = [_pair_add(g, got, self._name(f"pair_add_{i}"))
                for i, (g, got) in enumerate(zip(out[:self.n], out[self.n:]))]
        self.flight = _copies_start(self._name("chip_start"), [p for p, _ in sums] + [own for _, own in sums],
                                    self.plans[1], 3 * self.n)
        return self.flight[2]

    def after_chips(self, after):
        sems, bufs, _ = self.flight
        out = _copies_wait(self._name("chip_wait"), sems, bufs, self.plans[1], 3 * self.n, after)
        fulls = [_chip_sum(t, self._name(f"chip_sum_{i}")) for i, t in enumerate(out[self.n:])]
        self.flight = _copies_start(self._name("fill_start"), fulls, self.plans[2], self.n)
        return self.flight[2]

    def finish(self, after):
        sems, bufs, _ = self.flight
        return _copies_wait(self._name("fill_wait"), sems, bufs, self.plans[2], self.n, after)


def _adamw_math(w, g, m, v):
    nm = ADAM_B1 * m + (1.0 - ADAM_B1) * g
    nv = ADAM_B2 * v + (1.0 - ADAM_B2) * (g * g)
    m_hat = nm / (1.0 - ADAM_B1 ** ADAM_STEP)
    v_hat = nv / (1.0 - ADAM_B2 ** ADAM_STEP)
    return -ADAM_LR * (m_hat / (jnp.sqrt(v_hat) + ADAM_EPS) + ADAM_WD * w), nm, nv


def _adamw(w, g, m, v, name, tr):
    r, cdim = w.shape

    def body(w_ref, g_ref, m_ref, v_ref, d_ref, nm_ref, nv_ref):
        d_ref[...], nm_ref[...], nv_ref[...] = _adamw_math(w_ref[...], g_ref[...], m_ref[...], v_ref[...])

    return pl.pallas_call(
        body, name=name, grid=(r // tr,), in_specs=[_rows(tr, cdim)] * 4, out_specs=[_rows(tr, cdim)] * 3,
        out_shape=[jax.ShapeDtypeStruct((r, cdim), F32)] * 3, compiler_params=_cp(("parallel",)),
    )(w, g, m, v)


def _adamw_small(w, parts, m, v, shapes):
    n_rows = w.shape[0]

    def scatter(src, outs):
        row = 0
        for (rows, cols), out in zip(shapes, outs):
            if cols == LANES:
                out[...] = src[row:row + rows, :]
            elif cols > LANES:
                per = cols // LANES
                for r in range(rows):
                    for cb in range(per):
                        out[r:r + 1, cb * LANES:(cb + 1) * LANES] = src[row + r * per + cb:row + r * per + cb + 1, :]
            else:
                per = LANES // cols
                for r in range(rows):
                    out[r:r + 1, :] = src[row + r // per:row + r // per + 1, (r % per) * cols:(r % per + 1) * cols]
            row += -(-rows * cols // LANES)

    def body(w_ref, p_ref, m_ref, v_ref, *rest):
        outs, scr = rest[:-4], rest[-4:]
        g = p_ref[0]
        for dev in range(1, N_DEV):
            g = g + p_ref[dev]
        scr[3][...] = g
        scr[0][...], scr[1][...], scr[2][...] = _adamw_math(w_ref[...], g, m_ref[...], v_ref[...])
        n = len(shapes)
        for kind in range(4):
            scatter(scr[kind], outs[kind * n:(kind + 1) * n])

    outs = pl.pallas_call(
        body, name="adamw_small", grid=(1,),
        in_specs=[_const(w.shape), _const(parts.shape), _const(w.shape), _const(w.shape)],
        out_specs=[_const(s) for s in shapes] * 4, out_shape=[jax.ShapeDtypeStruct(s, F32) for s in shapes] * 4,
        scratch_shapes=[pltpu.VMEM((n_rows, LANES), F32)] * 4,
        compiler_params=_cp(("arbitrary",)),
    )(w, parts, m, v)
    n = len(shapes)
    return [outs[kind * n:(kind + 1) * n] for kind in range(4)]


REPLICATED = (("norm_mix_pre", 1024), ("norm_mix_post", 1024), ("norm_ffn_pre", 1024), ("norm_ffn_post", 1024),
              ("rel_bias", 256), ("sinks", 8), ("rwkv_shift_mix", 1792), ("w0", 512), ("a0", 512), ("k_k", 512),
              ("k_a", 512), ("r_k", 512), ("ln_x_g", 512), ("ln_x_b", 512), ("conv_b", 8192))
SMALL_SHARDED = (("w_decay_up", LORA_DECAY, D_RWKV), ("w_iclr_up", LORA_ICLR, D_RWKV),
                 ("w_gate_up", LORA_GATE, D_RWKV), ("conv_w", 3, 2 * D_FF))
BIG = (("w_in", D_MODEL, 640), ("w_out", 256, D_MODEL), ("w_ffn_up", D_MODEL, 2048), ("w_ffn_down", 1024, D_MODEL))
PACK_ALIGN = 8 * LANES


def _pack(pieces):
    flat = []
    for t in pieces:
        t = t.reshape(-1)
        pad = (-t.shape[0]) % LANES
        flat.append(jnp.pad(t, (0, pad)) if pad else t)
    flat = jnp.concatenate(flat)
    pad = (-flat.shape[0]) % PACK_ALIGN
    return jnp.pad(flat, (0, pad)).reshape(-1, LANES)


def kernel(x, norm_mix_pre, norm_mix_post, norm_ffn_pre, norm_ffn_post, w_in, rel_bias, sinks, rwkv_shift_mix, w0, w_decay_up, a0, w_iclr_up, w_gate_up, k_k, k_a, r_k, ln_x_g, ln_x_b, w_out, w_ffn_up, conv_w, conv_b, w_ffn_down, loss_target, m_norm_mix_pre, m_norm_mix_post, m_norm_ffn_pre, m_norm_ffn_post, m_w_in, m_rel_bias, m_sinks, m_rwkv_shift_mix, m_w0, m_w_decay_up, m_a0, m_w_iclr_up, m_w_gate_up, m_k_k, m_k_a, m_r_k, m_ln_x_g, m_ln_x_b, m_w_out, m_w_ffn_up, m_conv_w, m_conv_b, m_w_ffn_down, v_norm_mix_pre, v_norm_mix_post, v_norm_ffn_pre, v_norm_ffn_post, v_w_in, v_rel_bias, v_sinks, v_rwkv_shift_mix, v_w0, v_w_decay_up, v_a0, v_w_iclr_up, v_w_gate_up, v_k_k, v_k_a, v_r_k, v_ln_x_g, v_ln_x_b, v_w_out, v_w_ffn_up, v_conv_w, v_conv_b, v_w_ffn_down):
    given = dict(locals())
    names = [n for n, _ in REPLICATED] + [n for n, _, _ in SMALL_SHARDED] + [n for n, _, _ in BIG]
    order = ["norm_mix_pre", "norm_mix_post", "norm_ffn_pre", "norm_ffn_post", "w_in", "rel_bias", "sinks",
             "rwkv_shift_mix", "w0", "w_decay_up", "a0", "w_iclr_up", "w_gate_up", "k_k", "k_a", "r_k", "ln_x_g",
             "ln_x_b", "w_out", "w_ffn_up", "conv_w", "conv_b", "w_ffn_down"]
    assert sorted(names) == sorted(order)

    big_sh = {n: given[n].reshape(a, b).astype(BF16) for n, a, b in BIG}
    small_sh = [given[n].reshape(r, c // N_CHIPS) for n, r, c in SMALL_SHARDED]
    gathered = _gather_weights([big_sh["w_in"]], small_sh)
    rest = ("w_out", "w_ffn_up", "w_ffn_down")
    win_st, rest_sh = lax.optimization_barrier((gathered[0], [big_sh[n] for n in rest]))
    sm = {n: given[n] for n, _ in REPLICATED}
    sm["r_k"] = r_k.reshape(N_Q_HEADS, HEAD_DIM)
    for (n, r, c), st in zip(SMALL_SHARDED, gathered[1:]):
        sm[n] = st.transpose(1, 0, 2).reshape(r, c)

    lands = [lax.empty((N_CHIPS,) + t.shape, BF16) for t in rest_sh]
    plan_w = _plan_gather(len(rest))
    n_w = N_CHIPS * len(rest)
    w_sems, w_bufs, token = _copies_start("gather_rest_start", rest_sh + lands, plan_w, n_w)
    sm["norm_mix_pre"] = norm_mix_pre + token[0:1, 0:1]

    def on_rest_weights(after):
        out = _copies_wait("gather_rest_wait", w_sems, w_bufs, plan_w, n_w, after)
        wout_st, wup_st, wdown_st = out[3:]
        return wout_st.reshape(D_MODEL, D_MODEL), wup_st, wdown_st.reshape(D_FF, D_MODEL)

    red_a = _Reduction("a", (1024, D_MODEL, 256))
    red_b = _Reduction("b", (D_MODEL,))

    def on_grads_a(gs):
        d_wdown, d_wup, d_wout = gs
        return red_a.start([d_wdown.reshape(N_CHIPS, 1024, D_MODEL), d_wup, d_wout.reshape(N_CHIPS, 256, D_MODEL)])

    handlers = {"rest_weights": on_rest_weights, "grads_a": on_grads_a, "seam_1": red_a.after_pair,
                "seam_2": red_a.after_chips, "grads_b": lambda g: red_b.start([g])}
    steps = _local_step(x[0], loss_target[0], sm, win_st)
    kind, payload = next(steps)
    while True:
        try:
            kind, payload = steps.send(handlers[kind](payload))
        except StopIteration as done:
            loss, grad_x, grads = done.value
            break

    small_names = [n for n, _ in REPLICATED] + [n for n, _, _ in SMALL_SHARDED]

    def shard_cols(t, s):
        return t[:, s * (t.shape[1] // N_CHIPS):(s + 1) * (t.shape[1] // N_CHIPS)]

    for_chip = jnp.stack([_pack([loss[0]] + [grads[n] for n, _ in REPLICATED]
                                + [shard_cols(grads[n], s) for n, _, _ in SMALL_SHARDED]) for s in range(N_CHIPS)])
    land = lax.empty((N_DEV,) + for_chip.shape[1:], F32)

    def plan_small(refs):
        x, y, c, _ = _place()
        out = []
        for rel in range(N_DEV):
            px, py, pc = x ^ (rel >> 2), y ^ ((rel >> 1) & 1), c ^ (rel & 1)
            out.append((refs[0].at[2 * px + py], refs[1].at[4 * x + 2 * y + c], (px, py, pc)))
        return out

    s_sems, s_bufs, s_token = _copies_start("grad_small_start", [for_chip, land], plan_small, N_DEV)

    red_b.after_pair([grad_x, s_token])
    g_out = {}
    g_out["w_ffn_down"], g_out["w_ffn_up"], g_out["w_out"] = red_a.finish(grad_x)

    delta, new_m, new_v = {}, {}, {}

    def update(n, a, b):
        delta[n], new_m[n], new_v[n] = _adamw(given[n].reshape(a, b), g_out[n], given["m_" + n].reshape(a, b),
                                              given["v_" + n].reshape(a, b), "adamw_" + n, 128)

    for n, a, b in BIG[1:]:
        update(n, a, b)
    done = [delta[n] for n, _, _ in BIG[1:]]
    red_b.after_chips(done)
    parts = _copies_wait("grad_small_wait", s_sems, s_bufs, plan_small, N_DEV, done)[1]
    no_param = jnp.zeros((LANES,), F32)
    packs = [_pack([no_param] + [given[pre + n] for n in small_names]) for pre in ("", "m_", "v_")]

    def piece_shape(n):
        shape = given[n].shape
        rows, cols = int(np.prod(shape[:-1])), shape[-1]
        whole = cols % LANES == 0 or (LANES % cols == 0 and (rows * cols) % LANES == 0 and cols >= HEAD_DIM)
        return (rows, cols) if whole else (-(-rows * cols // LANES), LANES)

    shapes = [(1, LANES)] + [piece_shape(n) for n in small_names]
    upd = _adamw_small(packs[0], parts, packs[1], packs[2], shapes)
    loss = upd[3][0][0, 0]
    for i, n in enumerate(small_names):
        shape = given[n].shape
        size = int(np.prod(shape))
        delta[n], new_m[n], new_v[n], g_out[n] = (u[1 + i].reshape(-1)[:size].reshape(shape) for u in upd)
    g_out["w_in"], = red_b.finish(upd[0][0])
    update(*BIG[0])

    def shaped(d):
        return [d[n].reshape(given[n].shape) for n in order]

    return (loss, grad_x.reshape(x.shape), *shaped(g_out), *shaped(delta), *shaped(new_m), *shaped(new_v))
```

```python
import math

import numpy as np
import jax
import jax.numpy as jnp
from jax import lax
from jax.experimental import pallas as pl
from jax.experimental.pallas import tpu as pltpu

F32 = jnp.float32
BF16 = jnp.bfloat16
MESH = pl.DeviceIdType.MESH

SEQ = 2048
D_MODEL = 1024
HEAD_DIM = 64
D_ATTN = 512
D_RWKV = 512
D_KV = 128
N_Q_HEADS = 8
N_KV_HEADS = 2
Q_PER_KV = 4
BLOCK = 128
N_BUCKETS = 32
MAX_DISTANCE = 128
LORA_DECAY = 64
LORA_ICLR = 64
LORA_GATE = 128
RWKV_COLS = 3 * D_RWKV + LORA_DECAY + LORA_ICLR + LORA_GATE
P_OFF = D_ATTN + 2 * D_KV
D_IN = P_OFF + RWKV_COLS
D_FF = 4096
NORM_EPS = 1e-6
GN_EPS = 64e-5
NEG_INF = -1e30
N_CHIPS = 4
N_DEV = 8
HEAD_SHIFT = HEAD_DIM.bit_length() - 1
BLOCK_SHIFT = BLOCK.bit_length() - 1

ADAM_LR = 0.001
ADAM_B1 = 0.9
ADAM_B2 = 0.999
ADAM_EPS = 1e-08
ADAM_WD = 0.01
ADAM_STEP = 10

VMEM_LIMIT = 52 * 1024 * 1024
LANES = 128


def _cp(sem=None, vmem=VMEM_LIMIT):
    kw = dict(vmem_limit_bytes=vmem)
    if sem is not None:
        kw["dimension_semantics"] = sem
    return pltpu.CompilerParams(**kw)


def _rows(tr, nc):
    return pl.BlockSpec((tr, nc), lambda i: (i, 0))


def _const(shape):
    return pl.BlockSpec(shape, lambda *_: (0,) * len(shape))


ANY = pl.BlockSpec(memory_space=pl.ANY)


def _split(x, n):
    parts = []
    for _ in range(n - 1):
        h = x.astype(BF16)
        parts.append(h)
        x = x - h.astype(F32)
    parts.append(x.astype(BF16))
    return parts


NN = (((1,), (0,)), ((), ()))
NT = (((1,), (1,)), ((), ()))
TN = (((0,), (0,)), ((), ()))


def _dot(a, b, dn=NN):
    return lax.dot_general(a, b, dn, preferred_element_type=F32)


def _dot_ind(x, ind_bf16, n=3):
    acc = None
    for part in _split(x, n):
        t = _dot(part, ind_bf16)
        acc = t if acc is None else acc + t
    return acc


def _head_ones(n):
    r = lax.broadcasted_iota(jnp.int32, (n, n), 0) >> HEAD_SHIFT
    c = lax.broadcasted_iota(jnp.int32, (n, n), 1) >> HEAD_SHIFT
    return jnp.where(r == c, 1.0, 0.0).astype(BF16)


def _matmul(a, b, mode, name, *, m, n, k, tm, tn, a_spec=None, b_spec=None, out=None):
    dn = {"nn": NN, "nt": NT, "tn": TN}[mode]

    def body(a_ref, b_ref, o_ref):
        o_ref[...] = _dot(a_ref[...], b_ref[...], dn)

    if a_spec is None:
        a_spec = pl.BlockSpec((k, tm), lambda i, j: (0, i)) if mode == "tn" else pl.BlockSpec((tm, k), lambda i, j: (i, 0))
    if b_spec is None:
        b_spec = pl.BlockSpec((tn, k), lambda i, j: (j, 0)) if mode == "nt" else pl.BlockSpec((k, tn), lambda i, j: (0, j))
    return pl.pallas_call(
        body, name=name, grid=(m // tm, n // tn),
        in_specs=[a_spec, b_spec],
        out_specs=pl.BlockSpec((tm, tn), lambda i, j: (i, j)) if out is None else out[1],
        out_shape=jax.ShapeDtypeStruct((m, n) if out is None else out[0], F32),
        compiler_params=_cp(("parallel", "parallel")),
    )(a, b)


def _matmul_nt_shards(a, b_st, name, *, m, n, tm, tn, a_spec, a_piece):
    ks = b_st.shape[2]

    def body(a_ref, b_ref, o_ref):
        acc = _dot(a_piece(a_ref, 0), b_ref[0], NT)
        for s in range(1, N_CHIPS):
            acc = acc + _dot(a_piece(a_ref, s), b_ref[s], NT)
        o_ref[...] = acc

    return pl.pallas_call(
        body, name=name, grid=(m // tm, n // tn),
        in_specs=[a_spec, pl.BlockSpec((N_CHIPS, tn, ks), lambda i, j: (0, j, 0))],
        out_specs=pl.BlockSpec((tm, tn), lambda i, j: (i, j)),
        out_shape=jax.ShapeDtypeStruct((m, n), F32),
        compiler_params=_cp(("parallel", "parallel")),
    )(a, b_st)


def _rstd(x):
    return lax.rsqrt(jnp.mean(x * x, axis=-1, keepdims=True) + NORM_EPS)


def _rms_bwd(x, r, g, dy):
    gy = dy * g
    return r * gy - x * ((r * r * r) * (jnp.sum(x * gy, axis=-1, keepdims=True) / x.shape[-1]))


TR = 256


def _norm_cast(x, g, name):
    def body(x_ref, g_ref, h_ref):
        x = x_ref[...]
        h_ref[...] = (x * _rstd(x) * g_ref[...]).astype(BF16)

    return pl.pallas_call(
        body, name=name, grid=(SEQ // TR,),
        in_specs=[_rows(TR, D_MODEL), _const((1, D_MODEL))],
        out_specs=_rows(TR, D_MODEL),
        out_shape=jax.ShapeDtypeStruct((SEQ, D_MODEL), BF16),
        compiler_params=_cp(("parallel",)),
    )(x, g)


def _mix_norm(x, mix, g2, g3):
    def body(x_ref, mix_ref, g2_ref, g3_ref, x2_ref, h3_ref):
        mixv = mix_ref[...]
        x2 = x_ref[...] + mixv * _rstd(mixv) * g2_ref[...]
        x2_ref[...] = x2
        h3_ref[...] = (x2 * _rstd(x2) * g3_ref[...]).astype(BF16)

    return pl.pallas_call(
        body, name="mix_norm", grid=(SEQ // TR,),
        in_specs=[_rows(TR, D_MODEL), _rows(TR, D_MODEL), _const((1, D_MODEL)), _const((1, D_MODEL))],
        out_specs=[_rows(TR, D_MODEL), _rows(TR, D_MODEL)],
        out_shape=[jax.ShapeDtypeStruct((SEQ, D_MODEL), F32), jax.ShapeDtypeStruct((SEQ, D_MODEL), BF16)],
        compiler_params=_cp(("parallel",)),
    )(x, mix, g2, g3)


def _loss_head(x2, f, g4, target):
    def body(x2_ref, f_ref, g4_ref, t_ref, loss_ref, dy_ref, df_ref, dg_ref):
        i = pl.program_id(0)
        f = f_ref[...]
        g4 = g4_ref[...]
        r = _rstd(f)
        e = x2_ref[...] + f * r * g4 - t_ref[...]
        dy = e * (1.0 / D_MODEL)
        dy_ref[...] = dy
        df_ref[...] = _rms_bwd(f, r, g4, dy).astype(BF16)
        part = 0.5 * jnp.sum(jnp.sum(e * e, axis=-1, keepdims=True), axis=0, keepdims=True) * (1.0 / D_MODEL)
        dg = jnp.sum(dy * f * r, axis=0, keepdims=True)

        @pl.when(i == 0)
        def _():
            loss_ref[...] = jnp.zeros_like(loss_ref)
            dg_ref[...] = jnp.zeros_like(dg_ref)

        loss_ref[...] += jnp.broadcast_to(part, loss_ref.shape)
        dg_ref[...] += dg

    return pl.pallas_call(
        body, name="loss_head", grid=(SEQ // TR,),
        in_specs=[_rows(TR, D_MODEL), _rows(TR, D_MODEL), _const((1, D_MODEL)), _rows(TR, D_MODEL)],
        out_specs=[_const((8, LANES)), _rows(TR, D_MODEL), _rows(TR, D_MODEL), _const((1, D_MODEL))],
        out_shape=[jax.ShapeDtypeStruct((8, LANES), F32), jax.ShapeDtypeStruct((SEQ, D_MODEL), F32),
                   jax.ShapeDtypeStruct((SEQ, D_MODEL), BF16), jax.ShapeDtypeStruct((1, D_MODEL), F32)],
        compiler_params=_cp(("arbitrary",)),
    )(x2, f, g4, target)


def _mid_bwd(x2, mix, dy, dh3, g2, g3):
    def body(x2_ref, mix_ref, dy_ref, dh3_ref, g2_ref, g3_ref, dx2_ref, dmix_ref, dg2_ref, dg3_ref):
        i = pl.program_id(0)
        x2 = x2_ref[...]
        mixv = mix_ref[...]
        dh3 = dh3_ref[...]
        r3 = _rstd(x2)
        dx2 = dy_ref[...] + _rms_bwd(x2, r3, g3_ref[...], dh3)
        dx2_ref[...] = dx2
        r2 = _rstd(mixv)
        dmix_ref[...] = _rms_bwd(mixv, r2, g2_ref[...], dx2).astype(BF16)

        @pl.when(i == 0)
        def _():
            dg2_ref[...] = jnp.zeros_like(dg2_ref)
            dg3_ref[...] = jnp.zeros_like(dg3_ref)

        dg3_ref[...] += jnp.sum(dh3 * x2 * r3, axis=0, keepdims=True)
        dg2_ref[...] += jnp.sum(dx2 * mixv * r2, axis=0, keepdims=True)

    return pl.pallas_call(
        body, name="mid_bwd", grid=(SEQ // TR,),
        in_specs=[_rows(TR, D_MODEL)] * 4 + [_const((1, D_MODEL))] * 2,
        out_specs=[_rows(TR, D_MODEL), _rows(TR, D_MODEL), _const((1, D_MODEL)), _const((1, D_MODEL))],
        out_shape=[jax.ShapeDtypeStruct((SEQ, D_MODEL), F32), jax.ShapeDtypeStruct((SEQ, D_MODEL), BF16),
                   jax.ShapeDtypeStruct((1, D_MODEL), F32), jax.ShapeDtypeStruct((1, D_MODEL), F32)],
        compiler_params=_cp(("arbitrary",)),
    )(x2, mix, dy, dh3, g2, g3)


def _first_bwd(x, dx2, dh1, g1):
    def body(x_ref, dx2_ref, dh1_ref, g1_ref, dx_ref, dg1_ref):
        i = pl.program_id(0)
        x = x_ref[...]
        dh1 = dh1_ref[...]
        r = _rstd(x)
        dx_ref[...] = dx2_ref[...] + _rms_bwd(x, r, g1_ref[...], dh1)

        @pl.when(i == 0)
        def _():
            dg1_ref[...] = jnp.zeros_like(dg1_ref)

        dg1_ref[...] += jnp.sum(dh1 * x * r, axis=0, keepdims=True)

    return pl.pallas_call(
        body, name="first_bwd", grid=(SEQ // TR,),
        in_specs=[_rows(TR, D_MODEL)] * 3 + [_const((1, D_MODEL))],
        out_specs=[_rows(TR, D_MODEL), _const((1, D_MODEL))],
        out_shape=[jax.ShapeDtypeStruct((SEQ, D_MODEL), F32), jax.ShapeDtypeStruct((1, D_MODEL), F32)],
        compiler_params=_cp(("arbitrary",)),
    )(x, dx2, dh1, g1)


TC = 256
N_CB = D_FF // TC
GELU_C = math.sqrt(2.0 / math.pi)


def _shift_down(u, s):
    rolled = pltpu.roll(u, s, 0)
    row = lax.broadcasted_iota(jnp.int32, u.shape, 0)
    return jnp.where(row >= s, rolled, 0.0)


def _shift_up(u, s):
    n = u.shape[0]
    rolled = pltpu.roll(u, n - s, 0)
    row = lax.broadcasted_iota(jnp.int32, u.shape, 0)
    return jnp.where(row < n - s, rolled, 0.0)


def _conv3(u, w, b):
    return b + w[0:1] * _shift_down(u, 2) + w[1:2] * _shift_down(u, 1) + w[2:3] * u


def _gelu_and_grad(x):
    inner = GELU_C * (x + 0.044715 * (x * x * x))
    t = jnp.tanh(inner)
    gelu = 0.5 * x * (1.0 + t)
    dgelu = 0.5 * (1.0 + t) + 0.5 * x * (1.0 - t * t) * (GELU_C * (1.0 + 3 * 0.044715 * (x * x)))
    return gelu, dgelu


def _ffn_specs():
    col = lambda off: pl.BlockSpec((SEQ, TC), lambda *g: (0, g[-1] + off))
    w = lambda off: pl.BlockSpec((3, TC), lambda *g: (0, g[-1] + off))
    b = lambda off: pl.BlockSpec((1, TC), lambda *g: (0, g[-1] + off))
    return col, w, b


def _ffn_up_act(h3, wup_st, conv_w, conv_b):
    col, w, b = _ffn_specs()
    per_shard = wup_st.shape[2] // TC

    def body(h_ref, upg_ref, upv_ref, wg_ref, wv_ref, bg_ref, bv_ref, ug_ref, uv_ref, act_ref):
        h = h_ref[...]
        ug = _dot(h, upg_ref[...])
        uv = _dot(h, upv_ref[...])
        ug_ref[...] = ug
        uv_ref[...] = uv
        gate = _conv3(ug, wg_ref[...], bg_ref[...])
        val = _conv3(uv, wv_ref[...], bv_ref[...])
        act_ref[...] = (_gelu_and_grad(gate)[0] * val).astype(BF16)

    return pl.pallas_call(
        body, name="ffn_up_act", grid=(N_CB,),
        in_specs=[_const((SEQ, D_MODEL)),
                  pl.BlockSpec((None, D_MODEL, TC), lambda j: (j // per_shard, 0, j % per_shard)),
                  pl.BlockSpec((None, D_MODEL, TC), lambda j: (2 + j // per_shard, 0, j % per_shard)),
                  w(0), w(N_CB), b(0), b(N_CB)],
        out_specs=[col(0)] * 3,
        out_shape=[jax.ShapeDtypeStruct((SEQ, D_FF), F32)] * 2 + [jax.ShapeDtypeStruct((SEQ, D_FF), BF16)],
        compiler_params=_cp(("parallel",)),
    )(h3, wup_st, wup_st, conv_w, conv_w, conv_b, conv_b)


def _ffn_act_bwd(u_gate, u_val, df, wdown, conv_w, conv_b):
    col, w, b = _ffn_specs()
    both = lambda rows: pl.BlockSpec((2, rows, TC), lambda j: (0, 0, j))

    def body(ug_ref, uv_ref, df_ref, wd_ref, wg_ref, wv_ref, bg_ref, bv_ref, du_ref, dw_ref, db_ref):
        da = _dot(df_ref[...], wd_ref[...], NT)
        ug, uv = ug_ref[...], uv_ref[...]
        wg, wv = wg_ref[...], wv_ref[...]
        gate = _conv3(ug, wg, bg_ref[...])
        val = _conv3(uv, wv, bv_ref[...])
        gelu, dgelu = _gelu_and_grad(gate)
        for h, (duc, uh, wh) in enumerate(((da * val * dgelu, ug, wg), (da * gelu, uv, wv))):
            up1, up2 = _shift_up(duc, 1), _shift_up(duc, 2)
            du_ref[h] = (wh[2:3] * duc + wh[1:2] * up1 + wh[0:1] * up2).astype(BF16)
            db_ref[h] = jnp.sum(duc, axis=0, keepdims=True)
            dw_ref[h] = jnp.concatenate(
                [jnp.sum(up2 * uh, axis=0, keepdims=True), jnp.sum(up1 * uh, axis=0, keepdims=True),
                 jnp.sum(duc * uh, axis=0, keepdims=True)], axis=0)

    return pl.pallas_call(
        body, name="ffn_act_bwd", grid=(N_CB,),
        in_specs=[col(0), col(0), _const((SEQ, D_MODEL)), pl.BlockSpec((TC, D_MODEL), lambda j: (j, 0)),
                  w(0), w(N_CB), b(0), b(N_CB)],
        out_specs=[both(SEQ), both(3), both(1)],
        out_shape=[jax.ShapeDtypeStruct((2, SEQ, D_FF), BF16), jax.ShapeDtypeStruct((2, 3, D_FF), F32),
                   jax.ShapeDtypeStruct((2, 1, D_FF), F32)],
        compiler_params=_cp(("parallel",)),
    )(u_gate, u_val, df, wdown, conv_w, conv_w, conv_b, conv_b)


def _t5_onehot():
    rel = (np.arange(BLOCK)[:, None] + BLOCK) - np.arange(2 * BLOCK)[None, :]
    n = np.maximum(rel, 0)
    max_exact = N_BUCKETS // 2
    large = max_exact + (np.log(np.maximum(n, 1).astype(np.float32) / np.float32(max_exact))
                         / np.float32(math.log(MAX_DISTANCE / max_exact))
                         * np.float32(N_BUCKETS - max_exact)).astype(np.int32)
    large = np.minimum(large, N_BUCKETS - 1)
    bucket = np.where(n < max_exact, n, large).reshape(-1)
    return (bucket[None, :] == np.arange(N_BUCKETS)[:, None]).astype(np.float32)


N_REL = BLOCK * 2 * BLOCK


def _bias_table(rel_bias_t, onehot):
    def body(rb_ref, oh_ref, o_ref):
        o_ref[...] = _dot_ind(rb_ref[...], oh_ref[...])

    return pl.pallas_call(
        body, name="bias_table", grid=(1,),
        in_specs=[_const((N_Q_HEADS, N_BUCKETS)), _const((N_BUCKETS, N_REL))],
        out_specs=_const((N_Q_HEADS, N_REL)),
        out_shape=jax.ShapeDtypeStruct((N_Q_HEADS, N_REL), F32),
        compiler_params=_cp(("arbitrary",)),
    )(rel_bias_t, onehot)


def _bias_table_bwd(dbias, onehot):
    def body(db_ref, oh_ref, o_ref):
        acc = None
        for part in _split(db_ref[...], 3):
            t = _dot(part, oh_ref[...], NT)
            acc = t if acc is None else acc + t
        o_ref[...] = acc

    return pl.pallas_call(
        body, name="bias_table_bwd", grid=(1,),
        in_specs=[_const((N_Q_HEADS, N_REL)), _const((N_BUCKETS, N_REL))],
        out_specs=_const((N_Q_HEADS, N_BUCKETS)),
        out_shape=jax.ShapeDtypeStruct((N_Q_HEADS, N_BUCKETS), F32),
        compiler_params=_cp(("arbitrary",)),
    )(dbias, onehot)


def _attn_pieces(n, q, kvp, kvc, bias_ref, sinks_ref, hk):
    qi = lax.broadcasted_iota(jnp.int32, (BLOCK, 2 * BLOCK), 0)
    kj = lax.broadcasted_iota(jnp.int32, (BLOCK, 2 * BLOCK), 1)
    rel = qi + BLOCK - kj
    first_key = jnp.where(n > 0, 0, BLOCK)
    ok = jnp.where(rel >= 0, jnp.where(rel < BLOCK, jnp.where(kj >= first_key, 1.0, 0.0), 0.0), 0.0)
    ok4 = jnp.concatenate([ok] * Q_PER_KV, axis=0) > 0.5
    c0 = hk * HEAD_DIM
    kcat = jnp.concatenate([kvp[:, c0:c0 + HEAD_DIM], kvc[:, c0:c0 + HEAD_DIM]], axis=0).astype(BF16)
    vcat = jnp.concatenate([kvp[:, D_KV + c0:D_KV + c0 + HEAD_DIM], kvc[:, D_KV + c0:D_KV + c0 + HEAD_DIM]],
                           axis=0).astype(BF16)
    q0 = hk * Q_PER_KV * HEAD_DIM
    qs = jnp.concatenate([q[:, q0 + g * HEAD_DIM:q0 + (g + 1) * HEAD_DIM] for g in range(Q_PER_KV)],
                         axis=0).astype(BF16)
    s = _dot(qs, kcat, NT) * (HEAD_DIM ** -0.5) + bias_ref[hk]
    s = jnp.where(ok4, s, NEG_INF)
    row = lax.broadcasted_iota(jnp.int32, (Q_PER_KV * BLOCK, 1), 0)
    sink = jnp.zeros((Q_PER_KV * BLOCK, 1), F32)
    for g in range(Q_PER_KV):
        sink = jnp.where((row >> BLOCK_SHIFT) == g, sinks_ref[hk * Q_PER_KV + g], sink)
    m = jnp.maximum(jnp.max(s, axis=-1, keepdims=True), sink)
    p = jnp.exp(s - m)
    es = jnp.exp(sink - m)
    inv = 1.0 / (jnp.sum(p, axis=-1, keepdims=True) + es)
    return qs, kcat, vcat, p * inv, es * inv


def _attn_in_specs():
    return [pl.BlockSpec((BLOCK, D_ATTN), lambda n: (n, 0)),
            pl.BlockSpec((BLOCK, 2 * D_KV), lambda n: (jnp.maximum(n - 1, 0), D_ATTN // (2 * D_KV))),
            pl.BlockSpec((BLOCK, 2 * D_KV), lambda n: (n, D_ATTN // (2 * D_KV))),
            _const((N_KV_HEADS, Q_PER_KV * BLOCK, 2 * BLOCK)),
            pl.BlockSpec(memory_space=pltpu.SMEM)]


def _unstack_heads(t):
    return jnp.concatenate([t[g * BLOCK:(g + 1) * BLOCK] for g in range(Q_PER_KV)], axis=1)


def _attn_fwd(proj, bias, sinks):
    def body(q_ref, kvp_ref, kvc_ref, bias_ref, sinks_ref, o_ref):
        n = pl.program_id(0)
        q, kvp, kvc = q_ref[...], kvp_ref[...], kvc_ref[...]
        outs = []
        for hk in range(N_KV_HEADS):
            _, _, vcat, probs, _ = _attn_pieces(n, q, kvp, kvc, bias_ref, sinks_ref, hk)
            outs.append(_unstack_heads(_dot(probs.astype(BF16), vcat)))
        o_ref[...] = jnp.concatenate(outs, axis=1)

    return pl.pallas_call(
        body, name="attn_fwd", grid=(SEQ // BLOCK,),
        in_specs=_attn_in_specs(),
        out_specs=pl.BlockSpec((BLOCK, D_ATTN), lambda n: (n, 0)),
        out_shape=jax.ShapeDtypeStruct((SEQ, D_ATTN), F32),
        compiler_params=_cp(("parallel",)),
    )(proj, proj, proj, bias, sinks)


def _attn_bwd(proj, bias, sinks, dcat):
    nb = SEQ // BLOCK

    def body(q_ref, kvp_ref, kvc_ref, bias_ref, sinks_ref, do_ref, dq_ref, dkv_ref, dbias_ref, dsink_ref, dsacc):
        n = pl.program_id(0)

        @pl.when(n == 0)
        def _():
            dkv_ref[...] = jnp.zeros_like(dkv_ref)
            dbias_ref[...] = jnp.zeros_like(dbias_ref)
            dsacc[...] = jnp.zeros_like(dsacc)

        q, kvp, kvc = q_ref[...], kvp_ref[...], kvc_ref[...]
        do_all = do_ref[...]
        dqs, dks, dvs = [], [], []
        for hk in range(N_KV_HEADS):
            qs, kcat, vcat, probs, psink = _attn_pieces(n, q, kvp, kvc, bias_ref, sinks_ref, hk)
            q0 = hk * Q_PER_KV * HEAD_DIM
            do = jnp.concatenate([do_all[:, q0 + g * HEAD_DIM:q0 + (g + 1) * HEAD_DIM] for g in range(Q_PER_KV)],
                                 axis=0).astype(BF16)
            dprobs = _dot(do, vcat, NT)
            dvs.append(_dot(probs.astype(BF16), do, TN))
            rowdot = jnp.sum(probs * dprobs, axis=-1, keepdims=True)
            ds = probs * (dprobs - rowdot)
            dsacc[hk] += -psink * rowdot
            dbias_ref[hk] += ds
            dsb = (ds * (HEAD_DIM ** -0.5)).astype(BF16)
            dqs.append(_unstack_heads(_dot(dsb, kcat)))
            dks.append(_dot(dsb, qs, TN))
        dq_ref[...] = jnp.concatenate(dqs, axis=1)
        upd = jnp.concatenate(dks + dvs, axis=1)
        cur = pl.multiple_of(n * BLOCK, BLOCK)
        dkv_ref[pl.ds(cur, BLOCK), :] += upd[BLOCK:]

        @pl.when(n > 0)
        def _():
            prev = pl.multiple_of((n - 1) * BLOCK, BLOCK)
            dkv_ref[pl.ds(prev, BLOCK), :] += upd[:BLOCK]

        @pl.when(n == nb - 1)
        def _():
            for hk in range(N_KV_HEADS):
                for g in range(Q_PER_KV):
                    tot = jnp.sum(dsacc[hk, g * BLOCK:(g + 1) * BLOCK, :], axis=0, keepdims=True)
                    h = hk * Q_PER_KV + g
                    dsink_ref[h:h + 1, :] = jnp.broadcast_to(tot, (1, LANES))

    return pl.pallas_call(
        body, name="attn_bwd", grid=(nb,),
        in_specs=_attn_in_specs() + [pl.BlockSpec((BLOCK, D_ATTN), lambda n: (n, 0))],
        out_specs=[pl.BlockSpec((BLOCK, D_ATTN), lambda n: (n, 0)), _const((SEQ, 2 * D_KV)),
                   _const((N_KV_HEADS, Q_PER_KV * BLOCK, 2 * BLOCK)), _const((N_Q_HEADS, LANES))],
        out_shape=[jax.ShapeDtypeStruct((SEQ, D_ATTN), F32), jax.ShapeDtypeStruct((SEQ, 2 * D_KV), F32),
                   jax.ShapeDtypeStruct((N_KV_HEADS, Q_PER_KV * BLOCK, 2 * BLOCK), F32),
                   jax.ShapeDtypeStruct((N_Q_HEADS, LANES), F32)],
        scratch_shapes=[pltpu.VMEM((N_KV_HEADS, Q_PER_KV * BLOCK, 1), F32)],
        compiler_params=_cp(("arbitrary",)),
    )(proj, proj, proj, bias, sinks, dcat)


@jax.custom_vjp
def _head_sum(x):
    ones = _head_ones(LANES)
    return jnp.concatenate([_dot_ind(x[:, c:c + LANES], ones, 2) for c in range(0, x.shape[-1], LANES)], axis=1)


_head_sum.defvjp(lambda x: (_head_sum(x), None), lambda _, ct: (_head_sum(ct),))


@jax.custom_vjp
def _bdot(a, w):
    return _dot(a.astype(BF16), w.astype(BF16))


def _bdot_bwd(res, ct):
    a, w = res
    ctb = ct.astype(BF16)
    return _dot(ctb, w.astype(BF16), NT), _dot(a.astype(BF16), ctb, TN)


_bdot.defvjp(lambda a, w: (_bdot(a, w), (a, w)), _bdot_bwd)


def _sigmoid(x):
    return 0.5 * (jnp.tanh(0.5 * x) + 1.0)


def _softplus(x):
    return jnp.maximum(x, 0.0) + jnp.log(1.0 + jnp.exp(-jnp.abs(x)))


def _rwkv_core(r, k, v, zwa, zg, w0, wdu, a0, wiu, wgu, k_k, k_a):
    w_log = -_softplus(-(w0 + _bdot(jnp.tanh(zwa), wdu))) - 0.5
    decay = jnp.exp(-jnp.exp(w_log))
    a = _sigmoid(a0 + _bdot(zwa, wiu))
    g = _bdot(_sigmoid(zg), wgu)
    kk = k * k_k
    kk = kk / jnp.maximum(jnp.sqrt(_head_sum(kk * kk)), 1e-12)
    k2 = k * (1.0 + (a - 1.0) * k_a)
    return r, decay, k2, v, -kk, kk * a, g


def _rwkv_out(o, r, k2, v, g, lng, lnb, rk):
    mu = _head_sum(o) * (1.0 / HEAD_DIM)
    d = o - mu
    var = _head_sum(d * d) * (1.0 / HEAD_DIM)
    on = d * lax.rsqrt(var + GN_EPS) * lng + lnb
    bonus = _head_sum(r * k2 * rk) * v
    return (on + bonus) * g


P_SPLITS = (0, 512, 1024, 1536, 1664, 1792)
N_PREP_PARAMS = 7
HALO = 8


def _shifted_pieces(i, p_ref, halo_ref, mix_ref):
    p = p_ref[:, P_OFF:]
    prev_row = halo_ref[HALO - 1:HALO, P_OFF:] * jnp.where(i > 0, 1.0, 0.0)
    row = lax.broadcasted_iota(jnp.int32, p.shape, 0)
    pprev = jnp.where(row == 0, prev_row, pltpu.roll(p, 1, 0))
    delta = pprev - p
    ps = p + delta * mix_ref[...]
    return [ps[:, a:b] for a, b in zip(P_SPLITS[:-1], P_SPLITS[1:])], delta


def _prep_in_specs():
    return [_rows(TR, D_IN),
            pl.BlockSpec((HALO, D_IN), lambda i: (jnp.maximum(i * (TR // HALO) - 1, 0), 0)),
            _const((1, RWKV_COLS)), _const((1, D_RWKV)), _const((LANES, D_RWKV)), _const((1, D_RWKV)),
            _const((LANES, D_RWKV)), _const((LANES, D_RWKV)), _const((1, D_RWKV)), _const((1, D_RWKV))]


def _rwkv_prep(proj, mix, prm):
    def body(p_ref, halo_ref, mix_ref, *refs):
        prm_refs, outs = refs[:N_PREP_PARAMS], refs[N_PREP_PARAMS:]
        pieces, _ = _shifted_pieces(pl.program_id(0), p_ref, halo_ref, mix_ref)
        vals = _rwkv_core(*pieces, *[t[...] for t in prm_refs])
        for ref, val in zip(outs, vals):
            ref[...] = val

    return pl.pallas_call(
        body, name="rwkv_prep", grid=(SEQ // TR,),
        in_specs=_prep_in_specs(),
        out_specs=[_rows(TR, D_RWKV)] * 7,
        out_shape=[jax.ShapeDtypeStruct((SEQ, D_RWKV), F32)] * 7,
        compiler_params=_cp(("parallel",)),
    )(proj, proj, mix, *prm)


def _rwkv_prep_bwd(proj, mix, prm, cts):
    def body(p_ref, halo_ref, mix_ref, *refs):
        i = pl.program_id(0)
        prm_refs = refs[:N_PREP_PARAMS]
        ct_refs = refs[N_PREP_PARAMS:N_PREP_PARAMS + 10]
        dps_ref, dmix_ref = refs[N_PREP_PARAMS + 10:N_PREP_PARAMS + 12]
        dprm_refs = refs[N_PREP_PARAMS + 12:]
        pieces, delta = _shifted_pieces(i, p_ref, halo_ref, mix_ref)
        _, vjp = jax.vjp(_rwkv_core, *pieces, *[t[...] for t in prm_refs])
        dr1, dr2, dw, dk1, dk2, dv1, dv2, dkkn, db, dg = [t[...] for t in ct_refs]
        grads = vjp((dr1 + dr2, dw, dk1 + dk2, dv1 + dv2, dkkn, db, dg))
        dps = jnp.concatenate(grads[:5], axis=1)
        dps_ref[...] = dps

        @pl.when(i == 0)
        def _():
            dmix_ref[...] = jnp.zeros_like(dmix_ref)
            for ref in dprm_refs:
                ref[...] = jnp.zeros_like(ref)

        dmix_ref[...] += jnp.sum(dps * delta, axis=0, keepdims=True)
        for ref, gval in zip(dprm_refs, grads[5:]):
            ref[...] += gval

    prm_shapes = [(1, D_RWKV), (LANES, D_RWKV), (1, D_RWKV), (LANES, D_RWKV), (LANES, D_RWKV), (1, D_RWKV), (1, D_RWKV)]
    return pl.pallas_call(
        body, name="rwkv_prep_bwd", grid=(SEQ // TR,),
        in_specs=_prep_in_specs() + [_rows(TR, D_RWKV)] * 10,
        out_specs=[_rows(TR, RWKV_COLS), _const((1, RWKV_COLS))] + [_const(s) for s in prm_shapes],
        out_shape=[jax.ShapeDtypeStruct((SEQ, RWKV_COLS), F32), jax.ShapeDtypeStruct((1, RWKV_COLS), F32)]
        + [jax.ShapeDtypeStruct(s, F32) for s in prm_shapes],
        compiler_params=_cp(("arbitrary",)),
    )(proj, proj, mix, *prm, *cts)


def _rwkv_post(o, r, k2, v, g, lng, lnb, rk, attn):
    def body(o_ref, r_ref, k_ref, v_ref, g_ref, lng_ref, lnb_ref, rk_ref, attn_ref, cat_ref):
        rw = _rwkv_out(*[t[...] for t in (o_ref, r_ref, k_ref, v_ref, g_ref, lng_ref, lnb_ref, rk_ref)])
        cat_ref[...] = jnp.concatenate([attn_ref[...], rw], axis=1).astype(BF16)

    return pl.pallas_call(
        body, name="rwkv_post", grid=(SEQ // TR,),
        in_specs=[_rows(TR, D_RWKV)] * 5 + [_const((1, D_RWKV))] * 3 + [_rows(TR, D_ATTN)],
        out_specs=_rows(TR, D_MODEL),
        out_shape=jax.ShapeDtypeStruct((SEQ, D_MODEL), BF16),
        compiler_params=_cp(("parallel",)),
    )(o, r, k2, v, g, lng, lnb, rk, attn)


def _rwkv_post_bwd(o, r, k2, v, g, lng, lnb, rk, dcat):
    def body(o_ref, r_ref, k_ref, v_ref, g_ref, lng_ref, lnb_ref, rk_ref, dcat_ref,
             do_ref, dr_ref, dk_ref, dv_ref, dg_ref, dlng_ref, dlnb_ref, drk_ref):
        i = pl.program_id(0)
        args = [t[...] for t in (o_ref, r_ref, k_ref, v_ref, g_ref, lng_ref, lnb_ref, rk_ref)]
        _, vjp = jax.vjp(_rwkv_out, *args)
        grads = vjp(dcat_ref[:, D_ATTN:])
        for ref, gval in zip((do_ref, dr_ref, dk_ref, dv_ref, dg_ref), grads[:5]):
            ref[...] = gval

        @pl.when(i == 0)
        def _():
            for ref in (dlng_ref, dlnb_ref, drk_ref):
                ref[...] = jnp.zeros_like(ref)

        for ref, gval in zip((dlng_ref, dlnb_ref, drk_ref), grads[5:]):
            ref[...] += gval

    return pl.pallas_call(
        body, name="rwkv_post_bwd", grid=(SEQ // TR,),
        in_specs=[_rows(TR, D_RWKV)] * 5 + [_const((1, D_RWKV))] * 3 + [_rows(TR, D_MODEL)],
        out_specs=[_rows(TR, D_RWKV)] * 5 + [_const((1, D_RWKV))] * 3,
        out_shape=[jax.ShapeDtypeStruct((SEQ, D_RWKV), F32)] * 5 + [jax.ShapeDtypeStruct((1, D_RWKV), F32)] * 3,
        compiler_params=_cp(("arbitrary",)),
    )(o, r, k2, v, g, lng, lnb, rk, dcat)


def _assemble_dproj(dq, dkv, dps, mix):
    last = SEQ // HALO - 1

    def body(dq_ref, dkv_ref, dps_ref, nxt_ref, mix_ref, o_ref):
        i = pl.program_id(0)
        dps = dps_ref[...]
        mixv = mix_ref[...]
        nxt_row = nxt_ref[0:1, :] * jnp.where(i < SEQ // TR - 1, 1.0, 0.0)
        row = lax.broadcasted_iota(jnp.int32, dps.shape, 0)
        up = jnp.where(row == TR - 1, nxt_row, pltpu.roll(dps, TR - 1, 0))
        dp = dps * (1.0 - mixv) + up * mixv
        o_ref[...] = jnp.concatenate([dq_ref[...], dkv_ref[...], dp], axis=1).astype(BF16)

    return pl.pallas_call(
        body, name="assemble_dproj", grid=(SEQ // TR,),
        in_specs=[_rows(TR, D_ATTN), _rows(TR, 2 * D_KV), _rows(TR, RWKV_COLS),
                  pl.BlockSpec((HALO, RWKV_COLS), lambda i: (jnp.minimum((i + 1) * (TR // HALO), last), 0)),
                  _const((1, RWKV_COLS))],
        out_specs=_rows(TR, D_IN),
        out_shape=jax.ShapeDtypeStruct((SEQ, D_IN), BF16),
        compiler_params=_cp(("parallel",)),
    )(dq, dkv, dps, dps, mix)


N_PAIR = D_RWKV // LANES
CHUNK = 64
N_CHUNK = SEQ // CHUNK
GROUP = 8
STATE = (N_PAIR, HEAD_DIM, LANES)


def _lane_sums(lhs_tiles, ones2):
    out = _dot(jnp.concatenate(lhs_tiles, axis=0), ones2)
    return [out[i * HEAD_DIM:(i + 1) * HEAD_DIM] for i in range(len(lhs_tiles))]


def _seg_sum(xs, ones2):
    return _lane_sums([jnp.concatenate(_split(x, 2), axis=1) for x in xs], ones2)


def _seg_sum_rows(xs, ones2):
    out = _dot(jnp.concatenate(_split(jnp.concatenate(xs, axis=0), 2), axis=1), ones2)
    return [out[i * GROUP:(i + 1) * GROUP] for i in range(len(xs))]


def _col_form(rows, diag, ones2):
    zero = jnp.zeros((HEAD_DIM, LANES), BF16)
    tiles = []
    for row in rows:
        hi = row.astype(BF16)
        lo = (row - hi.astype(F32)).astype(BF16)
        tiles.append(jnp.concatenate(
            [jnp.where(diag, jnp.broadcast_to(part, (HEAD_DIM, LANES)), zero) for part in (hi, lo)], axis=1))
    return _lane_sums(tiles, ones2)


def _scan_consts():
    ones2 = jnp.concatenate([_head_ones(LANES)] * 2, axis=0)
    sub = lax.broadcasted_iota(jnp.int32, (HEAD_DIM, LANES), 0)
    lane_in_head = lax.broadcasted_iota(jnp.int32, (HEAD_DIM, LANES), 1) & (HEAD_DIM - 1)
    return ones2, lane_in_head == sub, lane_in_head


def _rows_of_columns(tile):
    t = tile.T
    return jnp.concatenate([t[:CHUNK], t[HEAD_DIM:HEAD_DIM + CHUNK]], axis=1)


def _pair(j):
    return slice(j * LANES, (j + 1) * LANES)


def _scan_fwd(r, w, k, v, kkn, b):
    def body(r_ref, w_ref, k_ref, v_ref, kkn_ref, b_ref, o_ref, st_ref, sa_ref, s_scr):
        c = pl.program_id(0)
        ones2, diag, lane_in_head = _scan_consts()

        @pl.when(c == 0)
        def _():
            s_scr[...] = jnp.zeros_like(s_scr)

        def group(gi, carry):
            row0 = pl.multiple_of(gi * GROUP, GROUP)
            states, ocols = list(carry[:N_PAIR]), list(carry[N_PAIR:])
            tiles = [[t[pl.ds(row0, GROUP), _pair(j)] for t in (r_ref, w_ref, k_ref, v_ref, kkn_ref, b_ref)]
                     for j in range(N_PAIR)]
            def row(j, name, u):
                return tiles[j]["rwkvnb".index(name)][u:u + 1]

            def emit_out(u, after):
                outs = _seg_sum([s[j] * row(j, "r", u + d) for d, s in enumerate(after) for j in range(N_PAIR)], ones2)
                for d in range(2):
                    here = lane_in_head == gi * GROUP + u + d
                    for j in range(N_PAIR):
                        ocols[j] = jnp.where(here, outs[d * N_PAIR + j], ocols[j])

            def vcols_of(u):
                cols = _col_form([row(j, "v", u + d) for d in range(2) for j in range(N_PAIR)], diag, ones2)
                return cols[:N_PAIR], cols[N_PAIR:]

            n_next = [pltpu.roll(tiles[j][4], GROUP - 1, 0) for j in range(N_PAIR)]
            dots = _seg_sum_rows([tiles[j][5] * n_next[j] for j in range(N_PAIR)]
                                 + [tiles[j][2] * n_next[j] for j in range(N_PAIR)], ones2)
            b_n, k_n = dots[:N_PAIR], dots[N_PAIR:]
            w_n = [tiles[j][1] * n_next[j] for j in range(N_PAIR)]

            vcols = vcols_of(0)
            after = None
            for u in range(0, GROUP, 2):
                prods = _seg_sum([states[j] * row(j, "n", u) for j in range(N_PAIR)]
                                 + [states[j] * w_n[j][u:u + 1] for j in range(N_PAIR)], ones2)
                if after is not None:
                    emit_out(u - 2, after)
                nxt = vcols_of(u + 2) if u + 2 < GROUP else None
                first, second = [], []
                for j in range(N_PAIR):
                    sa1 = prods[j]
                    sa2 = prods[N_PAIR + j] + sa1 * b_n[j][u:u + 1] + vcols[0][j] * k_n[j][u:u + 1]
                    s1 = states[j] * row(j, "w", u) + sa1 * row(j, "b", u) + vcols[0][j] * row(j, "k", u)
                    s2 = s1 * row(j, "w", u + 1) + sa2 * row(j, "b", u + 1) + vcols[1][j] * row(j, "k", u + 1)
                    st_ref[row0 + u, j] = s1
                    sa_ref[row0 + u, j] = sa1
                    st_ref[row0 + u + 1, j] = s2
                    sa_ref[row0 + u + 1, j] = sa2
                    first.append(s1)
                    second.append(s2)
                    states[j] = s2
                after, vcols = (first, second), nxt
            emit_out(GROUP - 2, after)
            return tuple(states + ocols)

        zero = jnp.zeros((HEAD_DIM, LANES), F32)
        fin = lax.fori_loop(0, CHUNK // GROUP, group, tuple(s_scr[j] for j in range(N_PAIR)) + (zero,) * N_PAIR)
        for j in range(N_PAIR):
            s_scr[j] = fin[j]
            o_ref[:, _pair(j)] = _rows_of_columns(fin[N_PAIR + j])

    blk = pl.BlockSpec((CHUNK, D_RWKV), lambda c: (c, 0))
    per_step = pl.BlockSpec((CHUNK,) + STATE, lambda c: (c, 0, 0, 0))
    return pl.pallas_call(
        body, name="rwkv_scan_fwd", grid=(N_CHUNK,),
        in_specs=[blk] * 6,
        out_specs=[blk, per_step, per_step],
        out_shape=[jax.ShapeDtypeStruct((SEQ, D_RWKV), F32)] + [jax.ShapeDtypeStruct((SEQ,) + STATE, F32)] * 2,
        scratch_shapes=[pltpu.VMEM(STATE, F32)],
        compiler_params=_cp(("arbitrary",)),
    )(r, w, k, v, kkn, b)


def _scan_bwd(r, w, k, v, kkn, b, do, states, sas, ds_in, prev, name, first_chunk, n_chunks):
    top = first_chunk + n_chunks - 1

    def body(r_ref, w_ref, k_ref, v_ref, kkn_ref, b_ref, do_ref, st_ref, before_ref, sa_ref, ds_in_ref, *rest):
        dr_ref, dw_ref, dk_ref, dv_ref, dkkn_ref, db_ref, ds_out_ref, ds_scr = rest[-8:]
        i = pl.program_id(0)
        ones2, diag, lane_in_head = _scan_consts()

        @pl.when(i == 0)
        def _():
            ds_scr[...] = ds_in_ref[...]

        entry = [before_ref[0, j] * jnp.where(i < top, 1.0, 0.0) for j in range(N_PAIR)]

        def reverse(gr, carry):
            gi = CHUNK // GROUP - 1 - gr
            row0 = pl.multiple_of(gi * GROUP, GROUP)
            dstates, dvcols = list(carry[:N_PAIR]), list(carry[N_PAIR:])
            tiles = [[t[pl.ds(row0, GROUP), _pair(j)]
                      for t in (r_ref, w_ref, k_ref, v_ref, kkn_ref, b_ref, do_ref)] for j in range(N_PAIR)]
            rows = [[[None] * GROUP for _ in range(5)] for _ in range(N_PAIR)]

            def row(j, name, u):
                return tiles[j]["rwkvnbd".index(name)][u:u + 1]

            def cols_of(u):
                cols = _col_form([row(j, name, u - d) for d in range(2) for name in "dv" for j in range(N_PAIR)],
                                 diag, ones2)
                return [[(cols[(2 * d) * N_PAIR + j], cols[(2 * d + 1) * N_PAIR + j]) for j in range(N_PAIR)]
                        for d in range(2)]

            def emit_dv(u, dsps):
                outs = _seg_sum([dsp[j] * row(j, "k", u - d) for d, dsp in enumerate(dsps) for j in range(N_PAIR)], ones2)
                for d in range(2):
                    here = lane_in_head == gi * GROUP + u - d
                    for j in range(N_PAIR):
                        dvcols[j] = jnp.where(here, outs[d * N_PAIR + j], dvcols[j])

            b_prev = [pltpu.roll(tiles[j][5], 1, 0) for j in range(N_PAIR)]
            dots = _seg_sum_rows([tiles[j][4] * b_prev[j] for j in range(N_PAIR)]
                                 + [tiles[j][0] * tiles[j][5] for j in range(N_PAIR)], ones2)
            n_b, r_b = dots[:N_PAIR], dots[N_PAIR:]
            w_b = [tiles[j][1] * b_prev[j] for j in range(N_PAIR)]

            def outputs(u, j, dsp, dsa, docol, vcol):
                tl = gi * GROUP + u
                if u > 0:
                    s_prev = st_ref[tl - 1, j]
                else:
                    s_prev = jnp.where(gi == 0, entry[j], st_ref[jnp.maximum(tl - 1, 0), j])
                rows[j][0][u] = jnp.sum(st_ref[tl, j] * docol, axis=0, keepdims=True)
                rows[j][1][u] = jnp.sum(dsp * s_prev, axis=0, keepdims=True)
                rows[j][2][u] = jnp.sum(dsp * vcol, axis=0, keepdims=True)
                rows[j][3][u] = jnp.sum(s_prev * dsa, axis=0, keepdims=True)
                rows[j][4][u] = jnp.sum(dsp * sa_ref[tl, j], axis=0, keepdims=True)

            cols = cols_of(GROUP - 1)
            before = None
            for u in range(GROUP - 1, 0, -2):
                dsp1 = [dstates[j] + cols[0][j][0] * row(j, "r", u) for j in range(N_PAIR)]
                prods = _seg_sum([dsp1[j] * row(j, "b", u) for j in range(N_PAIR)]
                                 + [dsp1[j] * w_b[j][u:u + 1] for j in range(N_PAIR)], ones2)
                if before is not None:
                    emit_dv(u + 2, before)
                nxt = cols_of(u - 2) if u >= 2 else None
                dsp2 = []
                for j in range(N_PAIR):
                    dsa1 = prods[j]
                    dsa2 = prods[N_PAIR + j] + dsa1 * n_b[j][u:u + 1] + cols[1][j][0] * r_b[j][u - 1:u]
                    mid = dsp1[j] * row(j, "w", u) + dsa1 * row(j, "n", u) + cols[1][j][0] * row(j, "r", u - 1)
                    outputs(u, j, dsp1[j], dsa1, *cols[0][j])
                    outputs(u - 1, j, mid, dsa2, *cols[1][j])
                    dstates[j] = mid * row(j, "w", u - 1) + dsa2 * row(j, "n", u - 1)
                    dsp2.append(mid)
                before, cols = (dsp1, dsp2), nxt
            emit_dv(1, before)
            for j in range(N_PAIR):
                for ref, rr in zip((dr_ref, dw_ref, dk_ref, dkkn_ref, db_ref), rows[j]):
                    ref[pl.ds(row0, GROUP), _pair(j)] = jnp.concatenate(rr, axis=0)
            return tuple(dstates + dvcols)

        zero = jnp.zeros((HEAD_DIM, LANES), F32)
        dfin = lax.fori_loop(0, CHUNK // GROUP, reverse, tuple(ds_scr[j] for j in range(N_PAIR)) + (zero,) * N_PAIR)
        for j in range(N_PAIR):
            ds_scr[j] = dfin[j]
            dv_ref[:, _pair(j)] = _rows_of_columns(dfin[N_PAIR + j])

        @pl.when(i == n_chunks - 1)
        def _():
            ds_out_ref[...] = ds_scr[...]

    blk = pl.BlockSpec((CHUNK, D_RWKV), lambda i: (top - i, 0))
    per_step = pl.BlockSpec((CHUNK,) + STATE, lambda i: (top - i, 0, 0, 0))
    step_before = pl.BlockSpec((1,) + STATE, lambda i: (jnp.maximum((top - i) * CHUNK - 1, 0), 0, 0, 0))
    prev = [] if prev is None else list(prev)
    outs = pl.pallas_call(
        body, name=name, grid=(n_chunks,),
        in_specs=[blk] * 7 + [per_step, step_before, per_step, _const(STATE)] + [ANY] * len(prev),
        out_specs=[blk] * 6 + [_const(STATE)],
        out_shape=[jax.ShapeDtypeStruct((SEQ, D_RWKV), F32)] * 6 + [jax.ShapeDtypeStruct(STATE, F32)],
        scratch_shapes=[pltpu.VMEM(STATE, F32)],
        input_output_aliases={11 + t: t for t in range(len(prev))},
        compiler_params=_cp(("arbitrary",)),
    )(r, w, k, v, kkn, b, do, states, states, sas, ds_in, *prev)
    return outs[:6], outs[6]


def _stacked(rows, cols, pick):
    return pl.BlockSpec((None, rows, cols), pick)


def _local_step(x, target, sm, win_st):
    def tied(t, token):
        return t if token is None else t + token[0:1, 0:1].reshape((1,) * t.ndim)

    zpad = jnp.zeros((LORA_DECAY, D_RWKV), F32)
    prm = [sm["w0"], jnp.concatenate([sm["w_decay_up"], zpad], axis=0), sm["a0"],
           jnp.concatenate([zpad, sm["w_iclr_up"]], axis=0), sm["w_gate_up"], sm["k_k"], sm["k_a"]]
    mix = sm["rwkv_shift_mix"]
    onehot = jnp.asarray(_t5_onehot(), BF16)
    sinks = sm["sinks"].reshape(N_Q_HEADS)
    lng, lnb, rk = sm["ln_x_g"], sm["ln_x_b"], sm["r_k"].reshape(1, D_RWKV)

    h1 = _norm_cast(x, sm["norm_mix_pre"], "norm_in")
    proj = _matmul(h1, win_st, "nn", "proj", m=SEQ, n=D_IN, k=D_MODEL, tm=SEQ, tn=640,
                   b_spec=_stacked(D_MODEL, 640, lambda i, j: (j, 0, 0)))
    bias = _bias_table(sm["rel_bias"].T, onehot).reshape(N_KV_HEADS, Q_PER_KV * BLOCK, 2 * BLOCK)
    attn = _attn_fwd(proj, bias, sinks)
    r, w, k2, v, kkn, b, g = _rwkv_prep(proj, mix, prm)
    o, states, sas = _scan_fwd(r, w, k2, v, kkn, b)
    wout, wup_st, wdown = yield ("rest_weights", o)
    cat = _rwkv_post(o, r, k2, v, g, lng, lnb, rk, attn)
    mixo = _matmul(cat, wout, "nn", "out_proj", m=SEQ, n=D_MODEL, k=D_MODEL, tm=SEQ, tn=512)
    x2, h3 = _mix_norm(x, mixo, sm["norm_mix_post"], sm["norm_ffn_pre"])
    u_gate, u_val, act = _ffn_up_act(h3, wup_st, sm["conv_w"], sm["conv_b"])
    f = _matmul(act, wdown, "nn", "ffn_down", m=SEQ, n=D_MODEL, k=D_FF, tm=1024, tn=512)
    loss, dy, df, d_g4 = _loss_head(x2, f, sm["norm_ffn_post"], target)

    d_wdown = _matmul(act, df, "tn", "d_wdown", m=D_FF, n=D_MODEL, k=SEQ, tm=512, tn=D_MODEL)
    du, d_convw, d_convb = _ffn_act_bwd(u_gate, u_val, df, wdown, sm["conv_w"], sm["conv_b"])
    d_convw = d_convw.transpose(1, 0, 2).reshape(3, 2 * D_FF)
    d_convb = d_convb.reshape(1, 2 * D_FF)
    dh3 = _matmul_nt_shards(du, wup_st, "d_h3", m=SEQ, n=D_MODEL, tm=512, tn=512,
                            a_spec=pl.BlockSpec((2, 512, D_FF), lambda i, j: (0, i, 0)),
                            a_piece=lambda ref, s: ref[s // 2, :, (s % 2) * 2048:(s % 2 + 1) * 2048])
    d_wup = _matmul(h3, du, "tn", "d_wup", m=D_MODEL, n=2 * D_FF, k=SEQ, tm=D_MODEL, tn=512,
                    b_spec=pl.BlockSpec((None, SEQ, 512), lambda i, j: (j // 8, 0, j % 8)),
                    out=((N_CHIPS, D_MODEL, 2048), _stacked(D_MODEL, 512, lambda i, j: (j // 4, 0, j % 4))))
    dx2, dmix, d_g2, d_g3 = _mid_bwd(x2, mixo, dy, dh3, sm["norm_mix_post"], sm["norm_ffn_pre"])
    dcat = _matmul(dmix, wout, "nt", "d_cat", m=SEQ, n=D_MODEL, k=D_MODEL, tm=SEQ, tn=512)
    d_wout = _matmul(cat, dmix, "tn", "d_wout", m=D_MODEL, n=D_MODEL, k=SEQ, tm=512, tn=D_MODEL)
    token = yield ("grads_a", (d_wdown, d_wup, d_wout))
    do, dr_p, dk_p, dv_p, dg, d_lng, d_lnb, d_rk = _rwkv_post_bwd(o, r, k2, v, g, lng, tied(lnb, token), rk, dcat)
    half = N_CHUNK // 2
    ds_end = jnp.zeros(STATE, F32)
    late, ds_mid = _scan_bwd(r, w, k2, v, kkn, b, do, states, sas, ds_end, None, "rwkv_scan_bwd_late", half, half)
    token = yield ("seam_1", ds_mid)
    scan_cts, ds_first = _scan_bwd(r, w, k2, v, kkn, b, do, states, sas, tied(ds_mid, token), late,
                                   "rwkv_scan_bwd_early", 0, half)
    dr_s, dw_s, dk_s, dv_s, dkkn_s, db_s = scan_cts
    token = yield ("seam_2", ds_first)
    prep_grads = _rwkv_prep_bwd(proj, tied(mix, token), prm,
                                (dr_s, dr_p, dw_s, dk_s, dk_p, dv_s, dv_p, dkkn_s, db_s, dg))
    dps, d_mix, d_w0, d_wdu, d_a0, d_wiu, d_wgu, d_kk, d_ka = prep_grads
    dq, dkv, dbias, dsink = _attn_bwd(proj, bias, sinks, dcat)
    d_relb = _bias_table_bwd(dbias.reshape(N_Q_HEADS, N_REL), onehot).T
    dproj = _assemble_dproj(dq, dkv, dps, mix)
    d_win = _matmul(h1, dproj, "tn", "d_win", m=D_MODEL, n=D_IN, k=SEQ, tm=D_MODEL, tn=640,
                    out=((N_CHIPS, D_MODEL, 640), _stacked(D_MODEL, 640, lambda i, j: (j, 0, 0))))
    token = yield ("grads_b", d_win)
    dh1 = _matmul_nt_shards(dproj, win_st, "d_h1", m=SEQ, n=D_MODEL, tm=1024, tn=D_MODEL,
                            a_spec=pl.BlockSpec((1024, D_IN), lambda i, j: (i, 0)),
                            a_piece=lambda ref, s: ref[:, s * 640:(s + 1) * 640])
    grad_x, d_g1 = _first_bwd(x, dx2, dh1, tied(sm["norm_mix_pre"], token))

    grads = {
        "norm_mix_pre": d_g1, "norm_mix_post": d_g2, "norm_ffn_pre": d_g3, "norm_ffn_post": d_g4,
        "w_in": d_win, "rel_bias": d_relb, "sinks": dsink[:, 0].reshape(1, N_Q_HEADS),
        "rwkv_shift_mix": d_mix, "w0": d_w0, "w_decay_up": d_wdu[:LORA_DECAY], "a0": d_a0,
        "w_iclr_up": d_wiu[LORA_DECAY:], "w_gate_up": d_wgu, "k_k": d_kk, "k_a": d_ka,
        "r_k": d_rk.reshape(1, N_Q_HEADS, HEAD_DIM), "ln_x_g": d_lng, "ln_x_b": d_lnb,
        "w_out": d_wout, "w_ffn_up": d_wup, "conv_w": d_convw, "conv_b": d_convb, "w_ffn_down": d_wdown,
    }
    return loss, grad_x, grads


def _place():
    x, y, c = lax.axis_index("x"), lax.axis_index("y"), lax.axis_index("c")
    chips = [(1 - x, y), (x, 1 - y), (1 - x, 1 - y)]
    return x, y, c, chips


def _remote(src, dst, sems, idx, to):
    return pltpu.make_async_remote_copy(src_ref=src, dst_ref=dst, send_sem=sems[0].at[idx], recv_sem=sems[1].at[idx],
                                        device_id=to, device_id_type=MESH)


ROW_ALIGN = 16


def _half(c, rows):
    return pl.ds(pl.multiple_of(c * (rows // 2), ROW_ALIGN), rows // 2)


def _gather_weights(big, small):
    nb, ns = len(big), len(small)

    def body(*refs):
        ins, outs = refs[:nb + ns], refs[nb + ns:2 * (nb + ns)]
        ici, d2d, sml, loc = refs[2 * (nb + ns):2 * (nb + ns) + 2], refs[-5:-3], refs[-3:-1], refs[-1]
        x, y, c, chips = _place()
        me = 2 * x + y
        sib = (x, y, 1 - c)
        local = [pltpu.make_async_copy(ins[a], outs[a].at[me], loc.at[a]) for a in range(nb + ns)]
        for cp in local:
            cp.start()
        sends = []
        for a in range(nb):
            rows = _half(c, big[a].shape[0])
            for kk, chip in enumerate(chips):
                sends.append(_remote(ins[a].at[rows], outs[a].at[me, rows], ici, a * 3 + kk, (*chip, c)))
        for a in range(ns):
            for kk, chip in enumerate(chips):
                sends.append(_remote(ins[nb + a], outs[nb + a].at[me], sml, a * 3 + kk, (*chip, c)))
        for cp in sends:
            cp.start()
        passed = []
        for a in range(nb):
            rows = _half(c, big[a].shape[0])
            for kk, (px, py) in enumerate(chips):
                got = outs[a].at[2 * px + py, rows]
                _remote(got, got, ici, a * 3 + kk, sib).wait_recv()
                fwd = _remote(got, got, d2d, a * 3 + kk, sib)
                fwd.start()
                passed.append(fwd)
        for a in range(nb):
            other = _half(1 - c, big[a].shape[0])
            for kk, (px, py) in enumerate(chips):
                land = outs[a].at[2 * px + py, other]
                _remote(land, land, d2d, a * 3 + kk, sib).wait_recv()
        for a in range(ns):
            for kk, (px, py) in enumerate(chips):
                land = outs[nb + a].at[2 * px + py]
                _remote(land, land, sml, a * 3 + kk, sib).wait_recv()
        for cp in sends + passed:
            cp.wait_send()
        for cp in local:
            cp.wait()

    arrs = list(big) + list(small)
    in_vmem = pl.BlockSpec(memory_space=pltpu.VMEM)
    return pl.pallas_call(
        body, name="gather_weights",
        in_specs=[in_vmem] * len(arrs), out_specs=[in_vmem] * len(arrs),
        out_shape=[jax.ShapeDtypeStruct((N_CHIPS,) + t.shape, t.dtype) for t in arrs],
        scratch_shapes=[pltpu.SemaphoreType.DMA((3 * nb,)), pltpu.SemaphoreType.DMA((3 * nb,)),
                        pltpu.SemaphoreType.DMA((3 * nb,)), pltpu.SemaphoreType.DMA((3 * nb,)),
                        pltpu.SemaphoreType.DMA((3 * ns,)), pltpu.SemaphoreType.DMA((3 * ns,)),
                        pltpu.SemaphoreType.DMA((nb + ns,))],
        compiler_params=pltpu.CompilerParams(has_side_effects=True, vmem_limit_bytes=VMEM_LIMIT),
    )(*arrs)


HBM = pl.BlockSpec(memory_space=pltpu.HBM)
SEM = pl.BlockSpec(memory_space=pltpu.SEMAPHORE)
EFFECT = pltpu.SideEffectType.DATAFLOW_SIDE_EFFECTING


def _copies_start(name, bufs, plan, n, partners=None):
    nb = len(bufs)

    def body(*refs):
        ins, sems, token = refs[:nb], refs[nb:nb + 2 * n], refs[-1]
        if partners is not None:
            barrier = pltpu.get_barrier_semaphore()
            peers = partners[1]()
            for peer in peers:
                pl.semaphore_signal(barrier, inc=1, device_id=peer, device_id_type=MESH)
            pl.semaphore_wait(barrier, len(peers))
        for kk, (src, dst, dev) in enumerate(plan(ins)):
            pltpu.make_async_remote_copy(src_ref=src, dst_ref=dst, send_sem=sems[2 * kk], recv_sem=sems[2 * kk + 1],
                                         device_id=dev, device_id_type=MESH).start()
        token[...] = jnp.zeros_like(token)

    outs = pl.pallas_call(
        body, name=name,
        out_shape=tuple([pltpu.SemaphoreType.DMA(())] * (2 * n) + [pltpu.HBM(t.shape, t.dtype) for t in bufs]
                        + [jax.ShapeDtypeStruct((8, LANES), F32)]),
        in_specs=[HBM] * nb,
        out_specs=tuple([SEM] * (2 * n) + [HBM] * nb + [pl.BlockSpec(memory_space=pltpu.VMEM)]),
        input_output_aliases={t: 2 * n + t for t in range(nb)},
        compiler_params=pltpu.CompilerParams(has_side_effects=EFFECT,
                                             collective_id=None if partners is None else partners[0]),
    )(*[pltpu.with_memory_space_constraint(t, pltpu.HBM) for t in bufs])
    return outs[:2 * n], outs[2 * n:2 * n + nb], outs[-1]


def _copies_wait(name, sems, bufs, plan, n, after):
    nb = len(bufs)
    after = list(after) if isinstance(after, (list, tuple)) else [after]

    def body(*refs):
        ins, sem_refs = refs[:nb], refs[nb:nb + 2 * n]
        for kk, (src, dst, dev) in enumerate(plan(ins)):
            cp = pltpu.make_async_remote_copy(src_ref=src, dst_ref=dst, send_sem=sem_refs[2 * kk],
                                              recv_sem=sem_refs[2 * kk + 1], device_id=dev, device_id_type=MESH)
            cp.wait_send()
            cp.wait_recv()

    return pl.pallas_call(
        body, name=name,
        out_shape=tuple(pltpu.HBM(t.shape, t.dtype) for t in bufs),
        in_specs=[HBM] * nb + [SEM] * (2 * n) + [ANY] * len(after),
        out_specs=tuple([HBM] * nb),
        input_output_aliases={t: t for t in range(nb)},
        compiler_params=pltpu.CompilerParams(has_side_effects=EFFECT),
    )(*bufs, *sems, *after)


def _plan_gather(n_w):
    def plan(refs):
        x, y, c, chips = _place()
        me = 2 * x + y
        return [(refs[a], refs[n_w + a].at[me], (*chip, c)) for a in range(n_w) for chip in chips + [(x, y)]]
    return plan


def _plan_pair_halves(n_g, rows):
    def plan(refs):
        x, y, c, _ = _place()
        return [(refs[a].at[:, _half(1 - c, rows[a])], refs[n_g + a], (x, y, 1 - c)) for a in range(n_g)]
    return plan


def _plan_chip_parts(n_g):
    def plan(refs):
        x, y, c, chips = _place()
        me = 2 * x + y
        return [(refs[a].at[2 * px + py], refs[n_g + a].at[me], (px, py, c))
                for a in range(n_g) for (px, py) in chips]
    return plan


def _plan_pair_fill(n_g, rows):
    def plan(refs):
        x, y, c, _ = _place()
        return [(refs[a].at[_half(c, rows[a])], refs[a].at[_half(c, rows[a])], (x, y, 1 - c)) for a in range(n_g)]
    return plan


def _pair_add(g, got, name):
    _, rows, cols = g.shape
    hr = rows // 2
    tr = min(hr, 256)
    nb = hr // tr

    def body(g_ref, got_ref, p_ref, own_ref):
        val = (g_ref[...] + got_ref[...]).astype(BF16)
        p_ref[...] = val

        @pl.when(pl.program_id(1) == 2 * lax.axis_index("x") + lax.axis_index("y"))
        def _():
            own_ref[...] = val

    def mine(i, s):
        return (2 * lax.axis_index("x") + lax.axis_index("y"), i, 0)

    return pl.pallas_call(
        body, name=name, grid=(nb, N_CHIPS),
        in_specs=[pl.BlockSpec((None, tr, cols), lambda i, s: (s, lax.axis_index("c") * nb + i, 0)),
                  pl.BlockSpec((None, tr, cols), lambda i, s: (s, i, 0))],
        out_specs=[pl.BlockSpec((None, tr, cols), lambda i, s: (s, i, 0)), pl.BlockSpec((None, tr, cols), mine)],
        out_shape=[jax.ShapeDtypeStruct((N_CHIPS, hr, cols), BF16)] * 2,
        compiler_params=_cp(("parallel", "arbitrary")),
    )(g, got)


def _chip_sum(parts, name):
    _, hr, cols = parts.shape
    tr = min(hr, 128)
    nb = hr // tr

    def body(t_ref, o_ref):
        part = [t_ref[s].astype(F32) for s in range(N_CHIPS)]
        o_ref[...] = ((part[0] + part[1]) + part[2]) + part[3]

    return pl.pallas_call(
        body, name=name, grid=(nb,),
        in_specs=[pl.BlockSpec((N_CHIPS, tr, cols), lambda i: (0, i, 0))],
        out_specs=pl.BlockSpec((tr, cols), lambda i: (lax.axis_index("c") * nb + i, 0)),
        out_shape=jax.ShapeDtypeStruct((2 * hr, cols), F32),
        compiler_params=_cp(("parallel",)),
    )(parts)


class _Reduction:
    def __init__(self, tag, rows, first_id):
        self.tag, self.n, self.rows, self.first_id = tag, len(rows), rows, first_id
        self.plans = (_plan_pair_halves(self.n, rows), _plan_chip_parts(self.n), _plan_pair_fill(self.n, rows))
        self.flight = None

    def _name(self, what):
        return f"grad_{self.tag}_{what}"

    @staticmethod
    def _sibling():
        x, y, c, _ = _place()
        return [(x, y, 1 - c)]

    @staticmethod
    def _same_core_elsewhere():
        x, y, c, chips = _place()
        return [(*chip, c) for chip in chips]

    def start(self, gs):
        gots = [lax.empty((N_CHIPS, t.shape[1] // 2, t.shape[2]), F32) for t in gs]
        self.flight = _copies_start(self._name("pair_start"), list(gs) + gots, self.plans[0], self.n,
                                    (self.first_id, self._sibling))
        return self.flight[2]

    def after_pair(self, after):
        sems, bufs, _ = self.flight
        out = _copies_wait(self._name("pair_wait"), sems, bufs, self.plans[0], self.n, after)
        sums = [_pair_add(g, got, self._name(f"pair_add_{i}"))
                for i, (g, got) in enumerate(zip(out[:self.n], out[self.n:]))]
        self.flight = _copies_start(self._name("chip_start"), [p for p, _ in sums] + [own for _, own in sums],
                                    self.plans[1], 3 * self.n, (self.first_id + 1, self._same_core_elsewhere))
        return self.flight[2]

    def after_chips(self, after):
        sems, bufs, _ = self.flight
        out = _copies_wait(self._name("chip_wait"), sems, bufs, self.plans[1], 3 * self.n, after)
        fulls = [_chip_sum(t, self._name(f"chip_sum_{i}")) for i, t in enumerate(out[self.n:])]
        self.flight = _copies_start(self._name("fill_start"), fulls, self.plans[2], self.n,
                                    (self.first_id + 2, self._sibling))
        return self.flight[2]

    def finish(self, after):
        sems, bufs, _ = self.flight
        return _copies_wait(self._name("fill_wait"), sems, bufs, self.plans[2], self.n, after)


def _adamw_math(w, g, m, v):
    nm = ADAM_B1 * m + (1.0 - ADAM_B1) * g
    nv = ADAM_B2 * v + (1.0 - ADAM_B2) * (g * g)
    m_hat = nm / (1.0 - ADAM_B1 ** ADAM_STEP)
    v_hat = nv / (1.0 - ADAM_B2 ** ADAM_STEP)
    return -ADAM_LR * (m_hat / (jnp.sqrt(v_hat) + ADAM_EPS) + ADAM_WD * w), nm, nv


def _adamw(w, g, m, v, name, tr):
    r, cdim = w.shape

    def body(w_ref, g_ref, m_ref, v_ref, d_ref, nm_ref, nv_ref):
        d_ref[...], nm_ref[...], nv_ref[...] = _adamw_math(w_ref[...], g_ref[...], m_ref[...], v_ref[...])

    return pl.pallas_call(
        body, name=name, grid=(r // tr,), in_specs=[_rows(tr, cdim)] * 4, out_specs=[_rows(tr, cdim)] * 3,
        out_shape=[jax.ShapeDtypeStruct((r, cdim), F32)] * 3, compiler_params=_cp(("parallel",)),
    )(w, g, m, v)


def _adamw_small(w, parts, m, v, shapes):
    n_rows = w.shape[0]

    def scatter(src, outs):
        row = 0
        for (rows, cols), out in zip(shapes, outs):
            if cols == LANES:
                out[...] = src[row:row + rows, :]
            elif cols > LANES:
                per = cols // LANES
                for r in range(rows):
                    for cb in range(per):
                        out[r:r + 1, cb * LANES:(cb + 1) * LANES] = src[row + r * per + cb:row + r * per + cb + 1, :]
            else:
                per = LANES // cols
                for r in range(rows):
                    out[r:r + 1, :] = src[row + r // per:row + r // per + 1, (r % per) * cols:(r % per + 1) * cols]
            row += -(-rows * cols // LANES)

    def body(w_ref, p_ref, m_ref, v_ref, *rest):
        outs, scr = rest[:-4], rest[-4:]
        g = p_ref[0]
        for dev in range(1, N_DEV):
            g = g + p_ref[dev]
        scr[3][...] = g
        scr[0][...], scr[1][...], scr[2][...] = _adamw_math(w_ref[...], g, m_ref[...], v_ref[...])
        n = len(shapes)
        for kind in range(4):
            scatter(scr[kind], outs[kind * n:(kind + 1) * n])

    outs = pl.pallas_call(
        body, name="adamw_small", grid=(1,),
        in_specs=[_const(w.shape), _const(parts.shape), _const(w.shape), _const(w.shape)],
        out_specs=[_const(s) for s in shapes] * 4, out_shape=[jax.ShapeDtypeStruct(s, F32) for s in shapes] * 4,
        scratch_shapes=[pltpu.VMEM((n_rows, LANES), F32)] * 4,
        compiler_params=_cp(("arbitrary",)),
    )(w, parts, m, v)
    n = len(shapes)
    return [outs[kind * n:(kind + 1) * n] for kind in range(4)]


REPLICATED = (("norm_mix_pre", 1024), ("norm_mix_post", 1024), ("norm_ffn_pre", 1024), ("norm_ffn_post", 1024),
              ("rel_bias", 256), ("sinks", 8), ("rwkv_shift_mix", 1792), ("w0", 512), ("a0", 512), ("k_k", 512),
              ("k_a", 512), ("r_k", 512), ("ln_x_g", 512), ("ln_x_b", 512), ("conv_b", 8192))
SMALL_SHARDED = (("w_decay_up", LORA_DECAY, D_RWKV), ("w_iclr_up", LORA_ICLR, D_RWKV),
                 ("w_gate_up", LORA_GATE, D_RWKV), ("conv_w", 3, 2 * D_FF))
BIG = (("w_in", D_MODEL, 640), ("w_out", 256, D_MODEL), ("w_ffn_up", D_MODEL, 2048), ("w_ffn_down", 1024, D_MODEL))
PACK_ALIGN = 8 * LANES


def _pack(pieces):
    flat = []
    for t in pieces:
        t = t.reshape(-1)
        pad = (-t.shape[0]) % LANES
        flat.append(jnp.pad(t, (0, pad)) if pad else t)
    flat = jnp.concatenate(flat)
    pad = (-flat.shape[0]) % PACK_ALIGN
    return jnp.pad(flat, (0, pad)).reshape(-1, LANES)


def kernel(x, norm_mix_pre, norm_mix_post, norm_ffn_pre, norm_ffn_post, w_in, rel_bias, sinks, rwkv_shift_mix, w0, w_decay_up, a0, w_iclr_up, w_gate_up, k_k, k_a, r_k, ln_x_g, ln_x_b, w_out, w_ffn_up, conv_w, conv_b, w_ffn_down, loss_target, m_norm_mix_pre, m_norm_mix_post, m_norm_ffn_pre, m_norm_ffn_post, m_w_in, m_rel_bias, m_sinks, m_rwkv_shift_mix, m_w0, m_w_decay_up, m_a0, m_w_iclr_up, m_w_gate_up, m_k_k, m_k_a, m_r_k, m_ln_x_g, m_ln_x_b, m_w_out, m_w_ffn_up, m_conv_w, m_conv_b, m_w_ffn_down, v_norm_mix_pre, v_norm_mix_post, v_norm_ffn_pre, v_norm_ffn_post, v_w_in, v_rel_bias, v_sinks, v_rwkv_shift_mix, v_w0, v_w_decay_up, v_a0, v_w_iclr_up, v_w_gate_up, v_k_k, v_k_a, v_r_k, v_ln_x_g, v_ln_x_b, v_w_out, v_w_ffn_up, v_conv_w, v_conv_b, v_w_ffn_down):
    given = dict(locals())
    names = [n for n, _ in REPLICATED] + [n for n, _, _ in SMALL_SHARDED] + [n for n, _, _ in BIG]
    order = ["norm_mix_pre", "norm_mix_post", "norm_ffn_pre", "norm_ffn_post", "w_in", "rel_bias", "sinks",
             "rwkv_shift_mix", "w0", "w_decay_up", "a0", "w_iclr_up", "w_gate_up", "k_k", "k_a", "r_k", "ln_x_g",
             "ln_x_b", "w_out", "w_ffn_up", "conv_w", "conv_b", "w_ffn_down"]
    assert sorted(names) == sorted(order)

    big_sh = {n: given[n].reshape(a, b).astype(BF16) for n, a, b in BIG}
    small_sh = [given[n].reshape(r, c // N_CHIPS) for n, r, c in SMALL_SHARDED]
    gathered = _gather_weights([big_sh["w_in"]], small_sh)
    rest = ("w_out", "w_ffn_up", "w_ffn_down")
    win_st, rest_sh = lax.optimization_barrier((gathered[0], [big_sh[n] for n in rest]))
    sm = {n: given[n] for n, _ in REPLICATED}
    sm["r_k"] = r_k.reshape(N_Q_HEADS, HEAD_DIM)
    for (n, r, c), st in zip(SMALL_SHARDED, gathered[1:]):
        sm[n] = st.transpose(1, 0, 2).reshape(r, c)

    lands = [lax.empty((N_CHIPS,) + t.shape, BF16) for t in rest_sh]
    plan_w = _plan_gather(len(rest))
    n_w = N_CHIPS * len(rest)
    w_sems, w_bufs, token = _copies_start("gather_rest_start", rest_sh + lands, plan_w, n_w)
    sm["norm_mix_pre"] = norm_mix_pre + token[0:1, 0:1]

    def on_rest_weights(after):
        out = _copies_wait("gather_rest_wait", w_sems, w_bufs, plan_w, n_w, after)
        wout_st, wup_st, wdown_st = out[3:]
        return wout_st.reshape(D_MODEL, D_MODEL), wup_st, wdown_st.reshape(D_FF, D_MODEL)

    red_a = _Reduction("a", (1024, D_MODEL, 256), first_id=0)
    red_b = _Reduction("b", (D_MODEL,), first_id=3)

    def on_grads_a(gs):
        d_wdown, d_wup, d_wout = gs
        return red_a.start([d_wdown.reshape(N_CHIPS, 1024, D_MODEL), d_wup, d_wout.reshape(N_CHIPS, 256, D_MODEL)])

    handlers = {"rest_weights": on_rest_weights, "grads_a": on_grads_a, "seam_1": red_a.after_pair,
                "seam_2": red_a.after_chips, "grads_b": lambda g: red_b.start([g])}
    steps = _local_step(x[0], loss_target[0], sm, win_st)
    kind, payload = next(steps)
    while True:
        try:
            kind, payload = steps.send(handlers[kind](payload))
        except StopIteration as done:
            loss, grad_x, grads = done.value
            break

    small_names = [n for n, _ in REPLICATED] + [n for n, _, _ in SMALL_SHARDED]

    def shard_cols(t, s):
        return t[:, s * (t.shape[1] // N_CHIPS):(s + 1) * (t.shape[1] // N_CHIPS)]

    for_chip = jnp.stack([_pack([loss[0]] + [grads[n] for n, _ in REPLICATED]
                                + [shard_cols(grads[n], s) for n, _, _ in SMALL_SHARDED]) for s in range(N_CHIPS)])
    land = lax.empty((N_DEV,) + for_chip.shape[1:], F32)

    def plan_small(refs):
        x, y, c, _ = _place()
        out = []
        for rel in range(N_DEV):
            px, py, pc = x ^ (rel >> 2), y ^ ((rel >> 1) & 1), c ^ (rel & 1)
            out.append((refs[0].at[2 * px + py], refs[1].at[4 * x + 2 * y + c], (px, py, pc)))
        return out

    s_sems, s_bufs, s_token = _copies_start("grad_small_start", [for_chip, land], plan_small, N_DEV)

    red_b.after_pair([grad_x, s_token])
    g_out = {}
    g_out["w_ffn_down"], g_out["w_ffn_up"], g_out["w_out"] = red_a.finish(grad_x)

    delta, new_m, new_v = {}, {}, {}

    def update(n, a, b):
        delta[n], new_m[n], new_v[n] = _adamw(given[n].reshape(a, b), g_out[n], given["m_" + n].reshape(a, b),
                                              given["v_" + n].reshape(a, b), "adamw_" + n, 128)

    for n, a, b in BIG[1:]:
        update(n, a, b)
    done = [delta[n] for n, _, _ in BIG[1:]]
    red_b.after_chips(done)
    parts = _copies_wait("grad_small_wait", s_sems, s_bufs, plan_small, N_DEV, done)[1]
    no_param = jnp.zeros((LANES,), F32)
    packs = [_pack([no_param] + [given[pre + n] for n in small_names]) for pre in ("", "m_", "v_")]

    def piece_shape(n):
        shape = given[n].shape
        rows, cols = int(np.prod(shape[:-1])), shape[-1]
        whole = cols % LANES == 0 or (LANES % cols == 0 and (rows * cols) % LANES == 0 and cols >= HEAD_DIM)
        return (rows, cols) if whole else (-(-rows * cols // LANES), LANES)

    shapes = [(1, LANES)] + [piece_shape(n) for n in small_names]
    upd = _adamw_small(packs[0], parts, packs[1], packs[2], shapes)
    loss = upd[3][0][0, 0]
    for i, n in enumerate(small_names):
        shape = given[n].shape
        size = int(np.prod(shape))
        delta[n], new_m[n], new_v[n], g_out[n] = (u[1 + i].reshape(-1)[:size].reshape(shape) for u in upd)
    g_out["w_in"], = red_b.finish(upd[0][0])
    update(*BIG[0])

    def shaped(d):
        return [d[n].reshape(given[n].shape) for n in order]

    return (loss, grad_x.reshape(x.shape), *shaped(g_out), *shaped(delta), *shaped(new_m), *shaped(new_v))
```

```python
import math

import numpy as np
import jax
import jax.numpy as jnp
from jax import lax
from jax.experimental import pallas as pl
from jax.experimental.pallas import tpu as pltpu

F32 = jnp.float32
BF16 = jnp.bfloat16
MESH = pl.DeviceIdType.MESH

SEQ = 2048
D_MODEL = 1024
HEAD_DIM = 64
D_ATTN = 512
D_RWKV = 512
D_KV = 128
N_Q_HEADS = 8
N_KV_HEADS = 2
Q_PER_KV = 4
BLOCK = 128
N_BUCKETS = 32
MAX_DISTANCE = 128
LORA_DECAY = 64
LORA_ICLR = 64
LORA_GATE = 128
RWKV_COLS = 3 * D_RWKV + LORA_DECAY + LORA_ICLR + LORA_GATE
P_OFF = D_ATTN + 2 * D_KV
D_IN = P_OFF + RWKV_COLS
D_FF = 4096
NORM_EPS = 1e-6
GN_EPS = 64e-5
NEG_INF = -1e30
N_CHIPS = 4
N_DEV = 8
HEAD_SHIFT = HEAD_DIM.bit_length() - 1
BLOCK_SHIFT = BLOCK.bit_length() - 1

ADAM_LR = 0.001
ADAM_B1 = 0.9
ADAM_B2 = 0.999
ADAM_EPS = 1e-08
ADAM_WD = 0.01
ADAM_STEP = 10

VMEM_LIMIT = 52 * 1024 * 1024
LANES = 128


def _cp(sem=None, vmem=VMEM_LIMIT):
    kw = dict(vmem_limit_bytes=vmem)
    if sem is not None:
        kw["dimension_semantics"] = sem
    return pltpu.CompilerParams(**kw)


def _rows(tr, nc):
    return pl.BlockSpec((tr, nc), lambda i: (i, 0))


def _const(shape):
    return pl.BlockSpec(shape, lambda *_: (0,) * len(shape))


ANY = pl.BlockSpec(memory_space=pl.ANY)


def _split(x, n):
    parts = []
    for _ in range(n - 1):
        h = x.astype(BF16)
        parts.append(h)
        x = x - h.astype(F32)
    parts.append(x.astype(BF16))
    return parts


NN = (((1,), (0,)), ((), ()))
NT = (((1,), (1,)), ((), ()))
TN = (((0,), (0,)), ((), ()))


def _dot(a, b, dn=NN):
    return lax.dot_general(a, b, dn, preferred_element_type=F32)


def _dot_ind(x, ind_bf16, n=3):
    acc = None
    for part in _split(x, n):
        t = _dot(part, ind_bf16)
        acc = t if acc is None else acc + t
    return acc


def _head_ones(n):
    r = lax.broadcasted_iota(jnp.int32, (n, n), 0) >> HEAD_SHIFT
    c = lax.broadcasted_iota(jnp.int32, (n, n), 1) >> HEAD_SHIFT
    return jnp.where(r == c, 1.0, 0.0).astype(BF16)


def _matmul(a, b, mode, name, *, m, n, k, tm, tn, a_spec=None, b_spec=None, out=None):
    dn = {"nn": NN, "nt": NT, "tn": TN}[mode]

    def body(a_ref, b_ref, o_ref):
        o_ref[...] = _dot(a_ref[...], b_ref[...], dn)

    if a_spec is None:
        a_spec = pl.BlockSpec((k, tm), lambda i, j: (0, i)) if mode == "tn" else pl.BlockSpec((tm, k), lambda i, j: (i, 0))
    if b_spec is None:
        b_spec = pl.BlockSpec((tn, k), lambda i, j: (j, 0)) if mode == "nt" else pl.BlockSpec((k, tn), lambda i, j: (0, j))
    return pl.pallas_call(
        body, name=name, grid=(m // tm, n // tn),
        in_specs=[a_spec, b_spec],
        out_specs=pl.BlockSpec((tm, tn), lambda i, j: (i, j)) if out is None else out[1],
        out_shape=jax.ShapeDtypeStruct((m, n) if out is None else out[0], F32),
        compiler_params=_cp(("parallel", "parallel")),
    )(a, b)


def _matmul_nt_shards(a, b_st, name, *, m, n, tm, tn, a_spec, a_piece):
    ks = b_st.shape[2]

    def body(a_ref, b_ref, o_ref):
        acc = _dot(a_piece(a_ref, 0), b_ref[0], NT)
        for s in range(1, N_CHIPS):
            acc = acc + _dot(a_piece(a_ref, s), b_ref[s], NT)
        o_ref[...] = acc

    return pl.pallas_call(
        body, name=name, grid=(m // tm, n // tn),
        in_specs=[a_spec, pl.BlockSpec((N_CHIPS, tn, ks), lambda i, j: (0, j, 0))],
        out_specs=pl.BlockSpec((tm, tn), lambda i, j: (i, j)),
        out_shape=jax.ShapeDtypeStruct((m, n), F32),
        compiler_params=_cp(("parallel", "parallel")),
    )(a, b_st)


def _rstd(x):
    return lax.rsqrt(jnp.mean(x * x, axis=-1, keepdims=True) + NORM_EPS)


def _rms_bwd(x, r, g, dy):
    gy = dy * g
    return r * gy - x * ((r * r * r) * (jnp.sum(x * gy, axis=-1, keepdims=True) / x.shape[-1]))


TR = 256


def _norm_cast(x, g, name):
    def body(x_ref, g_ref, h_ref):
        x = x_ref[...]
        h_ref[...] = (x * _rstd(x) * g_ref[...]).astype(BF16)

    return pl.pallas_call(
        body, name=name, grid=(SEQ // TR,),
        in_specs=[_rows(TR, D_MODEL), _const((1, D_MODEL))],
        out_specs=_rows(TR, D_MODEL),
        out_shape=jax.ShapeDtypeStruct((SEQ, D_MODEL), BF16),
        compiler_params=_cp(("parallel",)),
    )(x, g)


def _mix_norm(x, mix, g2, g3):
    def body(x_ref, mix_ref, g2_ref, g3_ref, x2_ref, h3_ref):
        mixv = mix_ref[...]
        x2 = x_ref[...] + mixv * _rstd(mixv) * g2_ref[...]
        x2_ref[...] = x2
        h3_ref[...] = (x2 * _rstd(x2) * g3_ref[...]).astype(BF16)

    return pl.pallas_call(
        body, name="mix_norm", grid=(SEQ // TR,),
        in_specs=[_rows(TR, D_MODEL), _rows(TR, D_MODEL), _const((1, D_MODEL)), _const((1, D_MODEL))],
        out_specs=[_rows(TR, D_MODEL), _rows(TR, D_MODEL)],
        out_shape=[jax.ShapeDtypeStruct((SEQ, D_MODEL), F32), jax.ShapeDtypeStruct((SEQ, D_MODEL), BF16)],
        compiler_params=_cp(("parallel",)),
    )(x, mix, g2, g3)


def _loss_head(x2, f, g4, target):
    def body(x2_ref, f_ref, g4_ref, t_ref, loss_ref, dy_ref, df_ref, dg_ref):
        i = pl.program_id(0)
        f = f_ref[...]
        g4 = g4_ref[...]
        r = _rstd(f)
        e = x2_ref[...] + f * r * g4 - t_ref[...]
        dy = e * (1.0 / D_MODEL)
        dy_ref[...] = dy
        df_ref[...] = _rms_bwd(f, r, g4, dy).astype(BF16)
        part = 0.5 * jnp.sum(jnp.sum(e * e, axis=-1, keepdims=True), axis=0, keepdims=True) * (1.0 / D_MODEL)
        dg = jnp.sum(dy * f * r, axis=0, keepdims=True)

        @pl.when(i == 0)
        def _():
            loss_ref[...] = jnp.zeros_like(loss_ref)
            dg_ref[...] = jnp.zeros_like(dg_ref)

        loss_ref[...] += jnp.broadcast_to(part, loss_ref.shape)
        dg_ref[...] += dg

    return pl.pallas_call(
        body, name="loss_head", grid=(SEQ // TR,),
        in_specs=[_rows(TR, D_MODEL), _rows(TR, D_MODEL), _const((1, D_MODEL)), _rows(TR, D_MODEL)],
        out_specs=[_const((8, LANES)), _rows(TR, D_MODEL), _rows(TR, D_MODEL), _const((1, D_MODEL))],
        out_shape=[jax.ShapeDtypeStruct((8, LANES), F32), jax.ShapeDtypeStruct((SEQ, D_MODEL), F32),
                   jax.ShapeDtypeStruct((SEQ, D_MODEL), BF16), jax.ShapeDtypeStruct((1, D_MODEL), F32)],
        compiler_params=_cp(("arbitrary",)),
    )(x2, f, g4, target)


def _mid_bwd(x2, mix, dy, dh3, g2, g3):
    def body(x2_ref, mix_ref, dy_ref, dh3_ref, g2_ref, g3_ref, dx2_ref, dmix_ref, dg2_ref, dg3_ref):
        i = pl.program_id(0)
        x2 = x2_ref[...]
        mixv = mix_ref[...]
        dh3 = dh3_ref[...]
        r3 = _rstd(x2)
        dx2 = dy_ref[...] + _rms_bwd(x2, r3, g3_ref[...], dh3)
        dx2_ref[...] = dx2
        r2 = _rstd(mixv)
        dmix_ref[...] = _rms_bwd(mixv, r2, g2_ref[...], dx2).astype(BF16)

        @pl.when(i == 0)
        def _():
            dg2_ref[...] = jnp.zeros_like(dg2_ref)
            dg3_ref[...] = jnp.zeros_like(dg3_ref)

        dg3_ref[...] += jnp.sum(dh3 * x2 * r3, axis=0, keepdims=True)
        dg2_ref[...] += jnp.sum(dx2 * mixv * r2, axis=0, keepdims=True)

    return pl.pallas_call(
        body, name="mid_bwd", grid=(SEQ // TR,),
        in_specs=[_rows(TR, D_MODEL)] * 4 + [_const((1, D_MODEL))] * 2,
        out_specs=[_rows(TR, D_MODEL), _rows(TR, D_MODEL), _const((1, D_MODEL)), _const((1, D_MODEL))],
        out_shape=[jax.ShapeDtypeStruct((SEQ, D_MODEL), F32), jax.ShapeDtypeStruct((SEQ, D_MODEL), BF16),
                   jax.ShapeDtypeStruct((1, D_MODEL), F32), jax.ShapeDtypeStruct((1, D_MODEL), F32)],
        compiler_params=_cp(("arbitrary",)),
    )(x2, mix, dy, dh3, g2, g3)


def _first_bwd(x, dx2, dh1, g1):
    def body(x_ref, dx2_ref, dh1_ref, g1_ref, dx_ref, dg1_ref):
        i = pl.program_id(0)
        x = x_ref[...]
        dh1 = dh1_ref[...]
        r = _rstd(x)
        dx_ref[...] = dx2_ref[...] + _rms_bwd(x, r, g1_ref[...], dh1)

        @pl.when(i == 0)
        def _():
            dg1_ref[...] = jnp.zeros_like(dg1_ref)

        dg1_ref[...] += jnp.sum(dh1 * x * r, axis=0, keepdims=True)

    return pl.pallas_call(
        body, name="first_bwd", grid=(SEQ // TR,),
        in_specs=[_rows(TR, D_MODEL)] * 3 + [_const((1, D_MODEL))],
        out_specs=[_rows(TR, D_MODEL), _const((1, D_MODEL))],
        out_shape=[jax.ShapeDtypeStruct((SEQ, D_MODEL), F32), jax.ShapeDtypeStruct((1, D_MODEL), F32)],
        compiler_params=_cp(("arbitrary",)),
    )(x, dx2, dh1, g1)


TC = 256
N_CB = D_FF // TC
GELU_C = math.sqrt(2.0 / math.pi)


def _shift_down(u, s):
    rolled = pltpu.roll(u, s, 0)
    row = lax.broadcasted_iota(jnp.int32, u.shape, 0)
    return jnp.where(row >= s, rolled, 0.0)


def _shift_up(u, s):
    n = u.shape[0]
    rolled = pltpu.roll(u, n - s, 0)
    row = lax.broadcasted_iota(jnp.int32, u.shape, 0)
    return jnp.where(row < n - s, rolled, 0.0)


def _conv3(u, w, b):
    return b + w[0:1] * _shift_down(u, 2) + w[1:2] * _shift_down(u, 1) + w[2:3] * u


def _gelu_and_grad(x):
    inner = GELU_C * (x + 0.044715 * (x * x * x))
    t = jnp.tanh(inner)
    gelu = 0.5 * x * (1.0 + t)
    dgelu = 0.5 * (1.0 + t) + 0.5 * x * (1.0 - t * t) * (GELU_C * (1.0 + 3 * 0.044715 * (x * x)))
    return gelu, dgelu


def _ffn_specs():
    col = lambda off: pl.BlockSpec((SEQ, TC), lambda *g: (0, g[-1] + off))
    w = lambda off: pl.BlockSpec((3, TC), lambda *g: (0, g[-1] + off))
    b = lambda off: pl.BlockSpec((1, TC), lambda *g: (0, g[-1] + off))
    return col, w, b


def _ffn_up_act(h3, wup_st, conv_w, conv_b):
    col, w, b = _ffn_specs()
    per_shard = wup_st.shape[2] // TC

    def body(h_ref, upg_ref, upv_ref, wg_ref, wv_ref, bg_ref, bv_ref, ug_ref, uv_ref, gate_ref, val_ref, act_ref):
        h = h_ref[...]
        ug = _dot(h, upg_ref[...])
        uv = _dot(h, upv_ref[...])
        ug_ref[...] = ug
        uv_ref[...] = uv
        gate = _conv3(ug, wg_ref[...], bg_ref[...])
        val = _conv3(uv, wv_ref[...], bv_ref[...])
        gate_ref[...] = gate
        val_ref[...] = val
        act_ref[...] = (_gelu_and_grad(gate)[0] * val).astype(BF16)

    return pl.pallas_call(
        body, name="ffn_up_act", grid=(N_CB,),
        in_specs=[_const((SEQ, D_MODEL)),
                  pl.BlockSpec((None, D_MODEL, TC), lambda j: (j // per_shard, 0, j % per_shard)),
                  pl.BlockSpec((None, D_MODEL, TC), lambda j: (2 + j // per_shard, 0, j % per_shard)),
                  w(0), w(N_CB), b(0), b(N_CB)],
        out_specs=[col(0)] * 5,
        out_shape=[jax.ShapeDtypeStruct((SEQ, D_FF), F32)] * 4 + [jax.ShapeDtypeStruct((SEQ, D_FF), BF16)],
        compiler_params=_cp(("parallel",)),
    )(h3, wup_st, wup_st, conv_w, conv_w, conv_b, conv_b)


def _ffn_act_bwd(u_gate, u_val, gate, val, df, wdown, conv_w):
    col, w, _ = _ffn_specs()
    both = lambda rows: pl.BlockSpec((2, rows, TC), lambda j: (0, 0, j))

    def body(ug_ref, uv_ref, gate_ref, val_ref, df_ref, wd_ref, wg_ref, wv_ref, du_ref, dw_ref, db_ref):
        da = _dot(df_ref[...], wd_ref[...], NT)
        gelu, dgelu = _gelu_and_grad(gate_ref[...])
        halves = ((da * val_ref[...] * dgelu, ug_ref, wg_ref[...]), (da * gelu, uv_ref, wv_ref[...]))
        for h, (duc, u_ref, wh) in enumerate(halves):
            uh = u_ref[...]
            up1, up2 = _shift_up(duc, 1), _shift_up(duc, 2)
            du_ref[h] = (wh[2:3] * duc + wh[1:2] * up1 + wh[0:1] * up2).astype(BF16)
            db_ref[h] = jnp.sum(duc, axis=0, keepdims=True)
            dw_ref[h] = jnp.concatenate(
                [jnp.sum(up2 * uh, axis=0, keepdims=True), jnp.sum(up1 * uh, axis=0, keepdims=True),
                 jnp.sum(duc * uh, axis=0, keepdims=True)], axis=0)

    return pl.pallas_call(
        body, name="ffn_act_bwd", grid=(N_CB,),
        in_specs=[col(0)] * 4 + [_const((SEQ, D_MODEL)), pl.BlockSpec((TC, D_MODEL), lambda j: (j, 0)), w(0), w(N_CB)],
        out_specs=[both(SEQ), both(3), both(1)],
        out_shape=[jax.ShapeDtypeStruct((2, SEQ, D_FF), BF16), jax.ShapeDtypeStruct((2, 3, D_FF), F32),
                   jax.ShapeDtypeStruct((2, 1, D_FF), F32)],
        compiler_params=_cp(("parallel",)),
    )(u_gate, u_val, gate, val, df, wdown, conv_w, conv_w)


def _t5_onehot():
    rel = (np.arange(BLOCK)[:, None] + BLOCK) - np.arange(2 * BLOCK)[None, :]
    n = np.maximum(rel, 0)
    max_exact = N_BUCKETS // 2
    large = max_exact + (np.log(np.maximum(n, 1).astype(np.float32) / np.float32(max_exact))
                         / np.float32(math.log(MAX_DISTANCE / max_exact))
                         * np.float32(N_BUCKETS - max_exact)).astype(np.int32)
    large = np.minimum(large, N_BUCKETS - 1)
    bucket = np.where(n < max_exact, n, large).reshape(-1)
    return (bucket[None, :] == np.arange(N_BUCKETS)[:, None]).astype(np.float32)


N_REL = BLOCK * 2 * BLOCK


def _bias_table(rel_bias_t, onehot):
    def body(rb_ref, oh_ref, o_ref):
        o_ref[...] = _dot_ind(rb_ref[...], oh_ref[...])

    return pl.pallas_call(
        body, name="bias_table", grid=(1,),
        in_specs=[_const((N_Q_HEADS, N_BUCKETS)), _const((N_BUCKETS, N_REL))],
        out_specs=_const((N_Q_HEADS, N_REL)),
        out_shape=jax.ShapeDtypeStruct((N_Q_HEADS, N_REL), F32),
        compiler_params=_cp(("arbitrary",)),
    )(rel_bias_t, onehot)


def _bias_table_bwd(dbias, onehot):
    def body(db_ref, oh_ref, o_ref):
        acc = None
        for part in _split(db_ref[...], 3):
            t = _dot(part, oh_ref[...], NT)
            acc = t if acc is None else acc + t
        o_ref[...] = acc

    return pl.pallas_call(
        body, name="bias_table_bwd", grid=(1,),
        in_specs=[_const((N_Q_HEADS, N_REL)), _const((N_BUCKETS, N_REL))],
        out_specs=_const((N_Q_HEADS, N_BUCKETS)),
        out_shape=jax.ShapeDtypeStruct((N_Q_HEADS, N_BUCKETS), F32),
        compiler_params=_cp(("arbitrary",)),
    )(dbias, onehot)


def _attn_pieces(n, q, kvp, kvc, bias_ref, sinks_ref, hk):
    qi = lax.broadcasted_iota(jnp.int32, (BLOCK, 2 * BLOCK), 0)
    kj = lax.broadcasted_iota(jnp.int32, (BLOCK, 2 * BLOCK), 1)
    rel = qi + BLOCK - kj
    first_key = jnp.where(n > 0, 0, BLOCK)
    ok = jnp.where(rel >= 0, jnp.where(rel < BLOCK, jnp.where(kj >= first_key, 1.0, 0.0), 0.0), 0.0)
    ok4 = jnp.concatenate([ok] * Q_PER_KV, axis=0) > 0.5
    c0 = hk * HEAD_DIM
    kcat = jnp.concatenate([kvp[:, c0:c0 + HEAD_DIM], kvc[:, c0:c0 + HEAD_DIM]], axis=0).astype(BF16)
    vcat = jnp.concatenate([kvp[:, D_KV + c0:D_KV + c0 + HEAD_DIM], kvc[:, D_KV + c0:D_KV + c0 + HEAD_DIM]],
                           axis=0).astype(BF16)
    q0 = hk * Q_PER_KV * HEAD_DIM
    qs = jnp.concatenate([q[:, q0 + g * HEAD_DIM:q0 + (g + 1) * HEAD_DIM] for g in range(Q_PER_KV)],
                         axis=0).astype(BF16)
    s = _dot(qs, kcat, NT) * (HEAD_DIM ** -0.5) + bias_ref[hk]
    s = jnp.where(ok4, s, NEG_INF)
    row = lax.broadcasted_iota(jnp.int32, (Q_PER_KV * BLOCK, 1), 0)
    sink = jnp.zeros((Q_PER_KV * BLOCK, 1), F32)
    for g in range(Q_PER_KV):
        sink = jnp.where((row >> BLOCK_SHIFT) == g, sinks_ref[hk * Q_PER_KV + g], sink)
    m = jnp.maximum(jnp.max(s, axis=-1, keepdims=True), sink)
    p = jnp.exp(s - m)
    es = jnp.exp(sink - m)
    inv = 1.0 / (jnp.sum(p, axis=-1, keepdims=True) + es)
    return qs, kcat, vcat, p * inv, es * inv


def _attn_in_specs():
    return [pl.BlockSpec((BLOCK, D_ATTN), lambda n: (n, 0)),
            pl.BlockSpec((BLOCK, 2 * D_KV), lambda n: (jnp.maximum(n - 1, 0), D_ATTN // (2 * D_KV))),
            pl.BlockSpec((BLOCK, 2 * D_KV), lambda n: (n, D_ATTN // (2 * D_KV))),
            _const((N_KV_HEADS, Q_PER_KV * BLOCK, 2 * BLOCK)),
            pl.BlockSpec(memory_space=pltpu.SMEM)]


def _unstack_heads(t):
    return jnp.concatenate([t[g * BLOCK:(g + 1) * BLOCK] for g in range(Q_PER_KV)], axis=1)


def _attn_fwd(proj, bias, sinks):
    def body(q_ref, kvp_ref, kvc_ref, bias_ref, sinks_ref, o_ref):
        n = pl.program_id(0)
        q, kvp, kvc = q_ref[...], kvp_ref[...], kvc_ref[...]
        outs = []
        for hk in range(N_KV_HEADS):
            _, _, vcat, probs, _ = _attn_pieces(n, q, kvp, kvc, bias_ref, sinks_ref, hk)
            outs.append(_unstack_heads(_dot(probs.astype(BF16), vcat)))
        o_ref[...] = jnp.concatenate(outs, axis=1)

    return pl.pallas_call(
        body, name="attn_fwd", grid=(SEQ // BLOCK,),
        in_specs=_attn_in_specs(),
        out_specs=pl.BlockSpec((BLOCK, D_ATTN), lambda n: (n, 0)),
        out_shape=jax.ShapeDtypeStruct((SEQ, D_ATTN), F32),
        compiler_params=_cp(("parallel",)),
    )(proj, proj, proj, bias, sinks)


def _attn_bwd(proj, bias, sinks, dcat):
    nb = SEQ // BLOCK

    def body(q_ref, kvp_ref, kvc_ref, bias_ref, sinks_ref, do_ref, dq_ref, dkv_ref, dbias_ref, dsink_ref, dsacc):
        n = pl.program_id(0)

        @pl.when(n == 0)
        def _():
            dkv_ref[...] = jnp.zeros_like(dkv_ref)
            dbias_ref[...] = jnp.zeros_like(dbias_ref)
            dsacc[...] = jnp.zeros_like(dsacc)

        q, kvp, kvc = q_ref[...], kvp_ref[...], kvc_ref[...]
        do_all = do_ref[...]
        dqs, dks, dvs = [], [], []
        for hk in range(N_KV_HEADS):
            qs, kcat, vcat, probs, psink = _attn_pieces(n, q, kvp, kvc, bias_ref, sinks_ref, hk)
            q0 = hk * Q_PER_KV * HEAD_DIM
            do = jnp.concatenate([do_all[:, q0 + g * HEAD_DIM:q0 + (g + 1) * HEAD_DIM] for g in range(Q_PER_KV)],
                                 axis=0).astype(BF16)
            dprobs = _dot(do, vcat, NT)
            dvs.append(_dot(probs.astype(BF16), do, TN))
            rowdot = jnp.sum(probs * dprobs, axis=-1, keepdims=True)
            ds = probs * (dprobs - rowdot)
            dsacc[hk] += -psink * rowdot
            dbias_ref[hk] += ds
            dsb = (ds * (HEAD_DIM ** -0.5)).astype(BF16)
            dqs.append(_unstack_heads(_dot(dsb, kcat)))
            dks.append(_dot(dsb, qs, TN))
        dq_ref[...] = jnp.concatenate(dqs, axis=1)
        upd = jnp.concatenate(dks + dvs, axis=1)
        cur = pl.multiple_of(n * BLOCK, BLOCK)
        dkv_ref[pl.ds(cur, BLOCK), :] += upd[BLOCK:]

        @pl.when(n > 0)
        def _():
            prev = pl.multiple_of((n - 1) * BLOCK, BLOCK)
            dkv_ref[pl.ds(prev, BLOCK), :] += upd[:BLOCK]

        @pl.when(n == nb - 1)
        def _():
            for hk in range(N_KV_HEADS):
                for g in range(Q_PER_KV):
                    tot = jnp.sum(dsacc[hk, g * BLOCK:(g + 1) * BLOCK, :], axis=0, keepdims=True)
                    h = hk * Q_PER_KV + g
                    dsink_ref[h:h + 1, :] = jnp.broadcast_to(tot, (1, LANES))

    return pl.pallas_call(
        body, name="attn_bwd", grid=(nb,),
        in_specs=_attn_in_specs() + [pl.BlockSpec((BLOCK, D_ATTN), lambda n: (n, 0))],
        out_specs=[pl.BlockSpec((BLOCK, D_ATTN), lambda n: (n, 0)), _const((SEQ, 2 * D_KV)),
                   _const((N_KV_HEADS, Q_PER_KV * BLOCK, 2 * BLOCK)), _const((N_Q_HEADS, LANES))],
        out_shape=[jax.ShapeDtypeStruct((SEQ, D_ATTN), F32), jax.ShapeDtypeStruct((SEQ, 2 * D_KV), F32),
                   jax.ShapeDtypeStruct((N_KV_HEADS, Q_PER_KV * BLOCK, 2 * BLOCK), F32),
                   jax.ShapeDtypeStruct((N_Q_HEADS, LANES), F32)],
        scratch_shapes=[pltpu.VMEM((N_KV_HEADS, Q_PER_KV * BLOCK, 1), F32)],
        compiler_params=_cp(("arbitrary",)),
    )(proj, proj, proj, bias, sinks, dcat)


@jax.custom_vjp
def _head_sum(x):
    ones = _head_ones(LANES)
    return jnp.concatenate([_dot_ind(x[:, c:c + LANES], ones, 2) for c in range(0, x.shape[-1], LANES)], axis=1)


_head_sum.defvjp(lambda x: (_head_sum(x), None), lambda _, ct: (_head_sum(ct),))


@jax.custom_vjp
def _bdot(a, w):
    return _dot(a.astype(BF16), w.astype(BF16))


def _bdot_bwd(res, ct):
    a, w = res
    ctb = ct.astype(BF16)
    return _dot(ctb, w.astype(BF16), NT), _dot(a.astype(BF16), ctb, TN)


_bdot.defvjp(lambda a, w: (_bdot(a, w), (a, w)), _bdot_bwd)


def _sigmoid(x):
    return 0.5 * (jnp.tanh(0.5 * x) + 1.0)


def _softplus(x):
    return jnp.maximum(x, 0.0) + jnp.log(1.0 + jnp.exp(-jnp.abs(x)))


def _rwkv_core(r, k, v, zwa, zg, w0, wdu, a0, wiu, wgu, k_k, k_a):
    w_log = -_softplus(-(w0 + _bdot(jnp.tanh(zwa), wdu))) - 0.5
    decay = jnp.exp(-jnp.exp(w_log))
    a = _sigmoid(a0 + _bdot(zwa, wiu))
    g = _bdot(_sigmoid(zg), wgu)
    kk = k * k_k
    kk = kk / jnp.maximum(jnp.sqrt(_head_sum(kk * kk)), 1e-12)
    k2 = k * (1.0 + (a - 1.0) * k_a)
    return r, decay, k2, v, -kk, kk * a, g


def _rwkv_out(o, r, k2, v, g, lng, lnb, rk):
    mu = _head_sum(o) * (1.0 / HEAD_DIM)
    d = o - mu
    var = _head_sum(d * d) * (1.0 / HEAD_DIM)
    on = d * lax.rsqrt(var + GN_EPS) * lng + lnb
    bonus = _head_sum(r * k2 * rk) * v
    return (on + bonus) * g


P_SPLITS = (0, 512, 1024, 1536, 1664, 1792)
N_PREP_PARAMS = 7
HALO = 8


def _shifted_pieces(i, p_ref, halo_ref, mix_ref):
    p = p_ref[:, P_OFF:]
    prev_row = halo_ref[HALO - 1:HALO, P_OFF:] * jnp.where(i > 0, 1.0, 0.0)
    row = lax.broadcasted_iota(jnp.int32, p.shape, 0)
    pprev = jnp.where(row == 0, prev_row, pltpu.roll(p, 1, 0))
    delta = pprev - p
    ps = p + delta * mix_ref[...]
    return [ps[:, a:b] for a, b in zip(P_SPLITS[:-1], P_SPLITS[1:])], delta


def _prep_in_specs():
    return [_rows(TR, D_IN),
            pl.BlockSpec((HALO, D_IN), lambda i: (jnp.maximum(i * (TR // HALO) - 1, 0), 0)),
            _const((1, RWKV_COLS)), _const((1, D_RWKV)), _const((LANES, D_RWKV)), _const((1, D_RWKV)),
            _const((LANES, D_RWKV)), _const((LANES, D_RWKV)), _const((1, D_RWKV)), _const((1, D_RWKV))]


def _rwkv_prep(proj, mix, prm):
    def body(p_ref, halo_ref, mix_ref, *refs):
        prm_refs, outs = refs[:N_PREP_PARAMS], refs[N_PREP_PARAMS:]
        pieces, _ = _shifted_pieces(pl.program_id(0), p_ref, halo_ref, mix_ref)
        vals = _rwkv_core(*pieces, *[t[...] for t in prm_refs])
        for ref, val in zip(outs, vals):
            ref[...] = val

    return pl.pallas_call(
        body, name="rwkv_prep", grid=(SEQ // TR,),
        in_specs=_prep_in_specs(),
        out_specs=[_rows(TR, D_RWKV)] * 7,
        out_shape=[jax.ShapeDtypeStruct((SEQ, D_RWKV), F32)] * 7,
        compiler_params=_cp(("parallel",)),
    )(proj, proj, mix, *prm)


def _rwkv_prep_bwd(proj, mix, prm, cts):
    def body(p_ref, halo_ref, mix_ref, *refs):
        i = pl.program_id(0)
        prm_refs = refs[:N_PREP_PARAMS]
        ct_refs = refs[N_PREP_PARAMS:N_PREP_PARAMS + 10]
        dps_ref, dmix_ref = refs[N_PREP_PARAMS + 10:N_PREP_PARAMS + 12]
        dprm_refs = refs[N_PREP_PARAMS + 12:]
        pieces, delta = _shifted_pieces(i, p_ref, halo_ref, mix_ref)
        _, vjp = jax.vjp(_rwkv_core, *pieces, *[t[...] for t in prm_refs])
        dr1, dr2, dw, dk1, dk2, dv1, dv2, dkkn, db, dg = [t[...] for t in ct_refs]
        grads = vjp((dr1 + dr2, dw, dk1 + dk2, dv1 + dv2, dkkn, db, dg))
        dps = jnp.concatenate(grads[:5], axis=1)
        dps_ref[...] = dps

        @pl.when(i == 0)
        def _():
            dmix_ref[...] = jnp.zeros_like(dmix_ref)
            for ref in dprm_refs:
                ref[...] = jnp.zeros_like(ref)

        dmix_ref[...] += jnp.sum(dps * delta, axis=0, keepdims=True)
        for ref, gval in zip(dprm_refs, grads[5:]):
            ref[...] += gval

    prm_shapes = [(1, D_RWKV), (LANES, D_RWKV), (1, D_RWKV), (LANES, D_RWKV), (LANES, D_RWKV), (1, D_RWKV), (1, D_RWKV)]
    return pl.pallas_call(
        body, name="rwkv_prep_bwd", grid=(SEQ // TR,),
        in_specs=_prep_in_specs() + [_rows(TR, D_RWKV)] * 10,
        out_specs=[_rows(TR, RWKV_COLS), _const((1, RWKV_COLS))] + [_const(s) for s in prm_shapes],
        out_shape=[jax.ShapeDtypeStruct((SEQ, RWKV_COLS), F32), jax.ShapeDtypeStruct((1, RWKV_COLS), F32)]
        + [jax.ShapeDtypeStruct(s, F32) for s in prm_shapes],
        compiler_params=_cp(("arbitrary",)),
    )(proj, proj, mix, *prm, *cts)


def _rwkv_post(o, r, k2, v, g, lng, lnb, rk, attn):
    def body(o_ref, r_ref, k_ref, v_ref, g_ref, lng_ref, lnb_ref, rk_ref, attn_ref, cat_ref):
        rw = _rwkv_out(*[t[...] for t in (o_ref, r_ref, k_ref, v_ref, g_ref, lng_ref, lnb_ref, rk_ref)])
        cat_ref[...] = jnp.concatenate([attn_ref[...], rw], axis=1).astype(BF16)

    return pl.pallas_call(
        body, name="rwkv_post", grid=(SEQ // TR,),
        in_specs=[_rows(TR, D_RWKV)] * 5 + [_const((1, D_RWKV))] * 3 + [_rows(TR, D_ATTN)],
        out_specs=_rows(TR, D_MODEL),
        out_shape=jax.ShapeDtypeStruct((SEQ, D_MODEL), BF16),
        compiler_params=_cp(("parallel",)),
    )(o, r, k2, v, g, lng, lnb, rk, attn)


def _rwkv_post_bwd(o, r, k2, v, g, lng, lnb, rk, dcat):
    def body(o_ref, r_ref, k_ref, v_ref, g_ref, lng_ref, lnb_ref, rk_ref, dcat_ref,
             do_ref, dr_ref, dk_ref, dv_ref, dg_ref, dlng_ref, dlnb_ref, drk_ref):
        i = pl.program_id(0)
        args = [t[...] for t in (o_ref, r_ref, k_ref, v_ref, g_ref, lng_ref, lnb_ref, rk_ref)]
        _, vjp = jax.vjp(_rwkv_out, *args)
        grads = vjp(dcat_ref[:, D_ATTN:])
        for ref, gval in zip((do_ref, dr_ref, dk_ref, dv_ref, dg_ref), grads[:5]):
            ref[...] = gval

        @pl.when(i == 0)
        def _():
            for ref in (dlng_ref, dlnb_ref, drk_ref):
                ref[...] = jnp.zeros_like(ref)

        for ref, gval in zip((dlng_ref, dlnb_ref, drk_ref), grads[5:]):
            ref[...] += gval

    return pl.pallas_call(
        body, name="rwkv_post_bwd", grid=(SEQ // TR,),
        in_specs=[_rows(TR, D_RWKV)] * 5 + [_const((1, D_RWKV))] * 3 + [_rows(TR, D_MODEL)],
        out_specs=[_rows(TR, D_RWKV)] * 5 + [_const((1, D_RWKV))] * 3,
        out_shape=[jax.ShapeDtypeStruct((SEQ, D_RWKV), F32)] * 5 + [jax.ShapeDtypeStruct((1, D_RWKV), F32)] * 3,
        compiler_params=_cp(("arbitrary",)),
    )(o, r, k2, v, g, lng, lnb, rk, dcat)


def _assemble_dproj(dq, dkv, dps, mix):
    last = SEQ // HALO - 1

    def body(dq_ref, dkv_ref, dps_ref, nxt_ref, mix_ref, o_ref):
        i = pl.program_id(0)
        dps = dps_ref[...]
        mixv = mix_ref[...]
        nxt_row = nxt_ref[0:1, :] * jnp.where(i < SEQ // TR - 1, 1.0, 0.0)
        row = lax.broadcasted_iota(jnp.int32, dps.shape, 0)
        up = jnp.where(row == TR - 1, nxt_row, pltpu.roll(dps, TR - 1, 0))
        dp = dps * (1.0 - mixv) + up * mixv
        o_ref[...] = jnp.concatenate([dq_ref[...], dkv_ref[...], dp], axis=1).astype(BF16)

    return pl.pallas_call(
        body, name="assemble_dproj", grid=(SEQ // TR,),
        in_specs=[_rows(TR, D_ATTN), _rows(TR, 2 * D_KV), _rows(TR, RWKV_COLS),
                  pl.BlockSpec((HALO, RWKV_COLS), lambda i: (jnp.minimum((i + 1) * (TR // HALO), last), 0)),
                  _const((1, RWKV_COLS))],
        out_specs=_rows(TR, D_IN),
        out_shape=jax.ShapeDtypeStruct((SEQ, D_IN), BF16),
        compiler_params=_cp(("parallel",)),
    )(dq, dkv, dps, dps, mix)


N_PAIR = D_RWKV // LANES
CHUNK = 64
N_CHUNK = SEQ // CHUNK
GROUP = 8
STATE = (N_PAIR, HEAD_DIM, LANES)


def _lane_sums(lhs_tiles, ones2):
    out = _dot(jnp.concatenate(lhs_tiles, axis=0), ones2)
    return [out[i * HEAD_DIM:(i + 1) * HEAD_DIM] for i in range(len(lhs_tiles))]


def _seg_sum(xs, ones2):
    return _lane_sums([jnp.concatenate(_split(x, 2), axis=1) for x in xs], ones2)


def _seg_sum_rows(xs, ones2):
    out = _dot(jnp.concatenate(_split(jnp.concatenate(xs, axis=0), 2), axis=1), ones2)
    return [out[i * GROUP:(i + 1) * GROUP] for i in range(len(xs))]


def _col_form(rows, diag, ones2):
    zero = jnp.zeros((HEAD_DIM, LANES), BF16)
    tiles = []
    for row in rows:
        hi = row.astype(BF16)
        lo = (row - hi.astype(F32)).astype(BF16)
        tiles.append(jnp.concatenate(
            [jnp.where(diag, jnp.broadcast_to(part, (HEAD_DIM, LANES)), zero) for part in (hi, lo)], axis=1))
    return _lane_sums(tiles, ones2)


def _scan_consts():
    ones2 = jnp.concatenate([_head_ones(LANES)] * 2, axis=0)
    sub = lax.broadcasted_iota(jnp.int32, (HEAD_DIM, LANES), 0)
    lane_in_head = lax.broadcasted_iota(jnp.int32, (HEAD_DIM, LANES), 1) & (HEAD_DIM - 1)
    return ones2, lane_in_head == sub, lane_in_head


def _rows_of_columns(tile):
    t = tile.T
    return jnp.concatenate([t[:CHUNK], t[HEAD_DIM:HEAD_DIM + CHUNK]], axis=1)


def _pair(j):
    return slice(j * LANES, (j + 1) * LANES)


def _scan_fwd(r, w, k, v, kkn, b):
    def body(r_ref, w_ref, k_ref, v_ref, kkn_ref, b_ref, o_ref, st_ref, sa_ref, s_scr):
        c = pl.program_id(0)
        ones2, diag, lane_in_head = _scan_consts()

        @pl.when(c == 0)
        def _():
            s_scr[...] = jnp.zeros_like(s_scr)

        def group(gi, carry):
            row0 = pl.multiple_of(gi * GROUP, GROUP)
            states, ocols = list(carry[:N_PAIR]), list(carry[N_PAIR:])
            tiles = [[t[pl.ds(row0, GROUP), _pair(j)] for t in (r_ref, w_ref, k_ref, v_ref, kkn_ref, b_ref)]
                     for j in range(N_PAIR)]
            def row(j, name, u):
                return tiles[j]["rwkvnb".index(name)][u:u + 1]

            def emit_out(u, after):
                outs = _seg_sum([s[j] * row(j, "r", u + d) for d, s in enumerate(after) for j in range(N_PAIR)], ones2)
                for d in range(2):
                    here = lane_in_head == gi * GROUP + u + d
                    for j in range(N_PAIR):
                        ocols[j] = jnp.where(here, outs[d * N_PAIR + j], ocols[j])

            def vcols_of(u):
                cols = _col_form([row(j, "v", u + d) for d in range(2) for j in range(N_PAIR)], diag, ones2)
                return cols[:N_PAIR], cols[N_PAIR:]

            n_next = [pltpu.roll(tiles[j][4], GROUP - 1, 0) for j in range(N_PAIR)]
            dots = _seg_sum_rows([tiles[j][5] * n_next[j] for j in range(N_PAIR)]
                                 + [tiles[j][2] * n_next[j] for j in range(N_PAIR)], ones2)
            b_n, k_n = dots[:N_PAIR], dots[N_PAIR:]
            w_n = [tiles[j][1] * n_next[j] for j in range(N_PAIR)]

            vcols = vcols_of(0)
            after = None
            for u in range(0, GROUP, 2):
                prods = _seg_sum([states[j] * row(j, "n", u) for j in range(N_PAIR)]
                                 + [states[j] * w_n[j][u:u + 1] for j in range(N_PAIR)], ones2)
                if after is not None:
                    emit_out(u - 2, after)
                nxt = vcols_of(u + 2) if u + 2 < GROUP else None
                first, second = [], []
                for j in range(N_PAIR):
                    sa1 = prods[j]
                    sa2 = prods[N_PAIR + j] + sa1 * b_n[j][u:u + 1] + vcols[0][j] * k_n[j][u:u + 1]
                    s1 = states[j] * row(j, "w", u) + sa1 * row(j, "b", u) + vcols[0][j] * row(j, "k", u)
                    s2 = s1 * row(j, "w", u + 1) + sa2 * row(j, "b", u + 1) + vcols[1][j] * row(j, "k", u + 1)
                    st_ref[row0 + u, j] = s1
                    sa_ref[row0 + u, j] = sa1
                    st_ref[row0 + u + 1, j] = s2
                    sa_ref[row0 + u + 1, j] = sa2
                    first.append(s1)
                    second.append(s2)
                    states[j] = s2
                after, vcols = (first, second), nxt
            emit_out(GROUP - 2, after)
            return tuple(states + ocols)

        zero = jnp.zeros((HEAD_DIM, LANES), F32)
        fin = lax.fori_loop(0, CHUNK // GROUP, group, tuple(s_scr[j] for j in range(N_PAIR)) + (zero,) * N_PAIR)
        for j in range(N_PAIR):
            s_scr[j] = fin[j]
            o_ref[:, _pair(j)] = _rows_of_columns(fin[N_PAIR + j])

    blk = pl.BlockSpec((CHUNK, D_RWKV), lambda c: (c, 0))
    per_step = pl.BlockSpec((CHUNK,) + STATE, lambda c: (c, 0, 0, 0))
    return pl.pallas_call(
        body, name="rwkv_scan_fwd", grid=(N_CHUNK,),
        in_specs=[blk] * 6,
        out_specs=[blk, per_step, per_step],
        out_shape=[jax.ShapeDtypeStruct((SEQ, D_RWKV), F32)] + [jax.ShapeDtypeStruct((SEQ,) + STATE, F32)] * 2,
        scratch_shapes=[pltpu.VMEM(STATE, F32)],
        compiler_params=_cp(("arbitrary",)),
    )(r, w, k, v, kkn, b)


def _scan_bwd(r, w, k, v, kkn, b, do, states, sas, ds_in, prev, name, first_chunk, n_chunks):
    top = first_chunk + n_chunks - 1

    def body(r_ref, w_ref, k_ref, v_ref, kkn_ref, b_ref, do_ref, st_ref, before_ref, sa_ref, ds_in_ref, *rest):
        dr_ref, dw_ref, dk_ref, dv_ref, dkkn_ref, db_ref, ds_out_ref, ds_scr = rest[-8:]
        i = pl.program_id(0)
        ones2, diag, lane_in_head = _scan_consts()

        @pl.when(i == 0)
        def _():
            ds_scr[...] = ds_in_ref[...]

        entry = [before_ref[0, j] * jnp.where(i < top, 1.0, 0.0) for j in range(N_PAIR)]

        def reverse(gr, carry):
            gi = CHUNK // GROUP - 1 - gr
            row0 = pl.multiple_of(gi * GROUP, GROUP)
            dstates, dvcols = list(carry[:N_PAIR]), list(carry[N_PAIR:])
            tiles = [[t[pl.ds(row0, GROUP), _pair(j)]
                      for t in (r_ref, w_ref, k_ref, v_ref, kkn_ref, b_ref, do_ref)] for j in range(N_PAIR)]
            rows = [[[None] * GROUP for _ in range(5)] for _ in range(N_PAIR)]

            def row(j, name, u):
                return tiles[j]["rwkvnbd".index(name)][u:u + 1]

            def cols_of(u):
                cols = _col_form([row(j, name, u - d) for d in range(2) for name in "dv" for j in range(N_PAIR)],
                                 diag, ones2)
                return [[(cols[(2 * d) * N_PAIR + j], cols[(2 * d + 1) * N_PAIR + j]) for j in range(N_PAIR)]
                        for d in range(2)]

            def emit_dv(u, dsps):
                outs = _seg_sum([dsp[j] * row(j, "k", u - d) for d, dsp in enumerate(dsps) for j in range(N_PAIR)], ones2)
                for d in range(2):
                    here = lane_in_head == gi * GROUP + u - d
                    for j in range(N_PAIR):
                        dvcols[j] = jnp.where(here, outs[d * N_PAIR + j], dvcols[j])

            b_prev = [pltpu.roll(tiles[j][5], 1, 0) for j in range(N_PAIR)]
            dots = _seg_sum_rows([tiles[j][4] * b_prev[j] for j in range(N_PAIR)]
                                 + [tiles[j][0] * tiles[j][5] for j in range(N_PAIR)], ones2)
            n_b, r_b = dots[:N_PAIR], dots[N_PAIR:]
            w_b = [tiles[j][1] * b_prev[j] for j in range(N_PAIR)]

            def outputs(u, j, dsp, dsa, docol, vcol):
                tl = gi * GROUP + u
                if u > 0:
                    s_prev = st_ref[tl - 1, j]
                else:
                    s_prev = jnp.where(gi == 0, entry[j], st_ref[jnp.maximum(tl - 1, 0), j])
                rows[j][0][u] = jnp.sum(st_ref[tl, j] * docol, axis=0, keepdims=True)
                rows[j][1][u] = jnp.sum(dsp * s_prev, axis=0, keepdims=True)
                rows[j][2][u] = jnp.sum(dsp * vcol, axis=0, keepdims=True)
                rows[j][3][u] = jnp.sum(s_prev * dsa, axis=0, keepdims=True)
                rows[j][4][u] = jnp.sum(dsp * sa_ref[tl, j], axis=0, keepdims=True)

            cols = cols_of(GROUP - 1)
            before = None
            for u in range(GROUP - 1, 0, -2):
                dsp1 = [dstates[j] + cols[0][j][0] * row(j, "r", u) for j in range(N_PAIR)]
                prods = _seg_sum([dsp1[j] * row(j, "b", u) for j in range(N_PAIR)]
                                 + [dsp1[j] * w_b[j][u:u + 1] for j in range(N_PAIR)], ones2)
                if before is not None:
                    emit_dv(u + 2, before)
                nxt = cols_of(u - 2) if u >= 2 else None
                dsp2 = []
                for j in range(N_PAIR):
                    dsa1 = prods[j]
                    dsa2 = prods[N_PAIR + j] + dsa1 * n_b[j][u:u + 1] + cols[1][j][0] * r_b[j][u - 1:u]
                    mid = dsp1[j] * row(j, "w", u) + dsa1 * row(j, "n", u) + cols[1][j][0] * row(j, "r", u - 1)
                    outputs(u, j, dsp1[j], dsa1, *cols[0][j])
                    outputs(u - 1, j, mid, dsa2, *cols[1][j])
                    dstates[j] = mid * row(j, "w", u - 1) + dsa2 * row(j, "n", u - 1)
                    dsp2.append(mid)
                before, cols = (dsp1, dsp2), nxt
            emit_dv(1, before)
            for j in range(N_PAIR):
                for ref, rr in zip((dr_ref, dw_ref, dk_ref, dkkn_ref, db_ref), rows[j]):
                    ref[pl.ds(row0, GROUP), _pair(j)] = jnp.concatenate(rr, axis=0)
            return tuple(dstates + dvcols)

        zero = jnp.zeros((HEAD_DIM, LANES), F32)
        dfin = lax.fori_loop(0, CHUNK // GROUP, reverse, tuple(ds_scr[j] for j in range(N_PAIR)) + (zero,) * N_PAIR)
        for j in range(N_PAIR):
            ds_scr[j] = dfin[j]
            dv_ref[:, _pair(j)] = _rows_of_columns(dfin[N_PAIR + j])

        @pl.when(i == n_chunks - 1)
        def _():
            ds_out_ref[...] = ds_scr[...]

    blk = pl.BlockSpec((CHUNK, D_RWKV), lambda i: (top - i, 0))
    per_step = pl.BlockSpec((CHUNK,) + STATE, lambda i: (top - i, 0, 0, 0))
    step_before = pl.BlockSpec((1,) + STATE, lambda i: (jnp.maximum((top - i) * CHUNK - 1, 0), 0, 0, 0))
    prev = [] if prev is None else list(prev)
    outs = pl.pallas_call(
        body, name=name, grid=(n_chunks,),
        in_specs=[blk] * 7 + [per_step, step_before, per_step, _const(STATE)] + [ANY] * len(prev),
        out_specs=[blk] * 6 + [_const(STATE)],
        out_shape=[jax.ShapeDtypeStruct((SEQ, D_RWKV), F32)] * 6 + [jax.ShapeDtypeStruct(STATE, F32)],
        scratch_shapes=[pltpu.VMEM(STATE, F32)],
        input_output_aliases={11 + t: t for t in range(len(prev))},
        compiler_params=_cp(("arbitrary",)),
    )(r, w, k, v, kkn, b, do, states, states, sas, ds_in, *prev)
    return outs[:6], outs[6]


def _stacked(rows, cols, pick):
    return pl.BlockSpec((None, rows, cols), pick)


def _local_step(x, target, sm, win_st):
    def tied(t, token):
        return t if token is None else t + token[0:1, 0:1].reshape((1,) * t.ndim)

    zpad = jnp.zeros((LORA_DECAY, D_RWKV), F32)
    prm = [sm["w0"], jnp.concatenate([sm["w_decay_up"], zpad], axis=0), sm["a0"],
           jnp.concatenate([zpad, sm["w_iclr_up"]], axis=0), sm["w_gate_up"], sm["k_k"], sm["k_a"]]
    mix = sm["rwkv_shift_mix"]
    onehot = jnp.asarray(_t5_onehot(), BF16)
    sinks = sm["sinks"].reshape(N_Q_HEADS)
    lng, lnb, rk = sm["ln_x_g"], sm["ln_x_b"], sm["r_k"].reshape(1, D_RWKV)

    h1 = _norm_cast(x, sm["norm_mix_pre"], "norm_in")
    proj = _matmul(h1, win_st, "nn", "proj", m=SEQ, n=D_IN, k=D_MODEL, tm=SEQ, tn=640,
                   b_spec=_stacked(D_MODEL, 640, lambda i, j: (j, 0, 0)))
    bias = _bias_table(sm["rel_bias"].T, onehot).reshape(N_KV_HEADS, Q_PER_KV * BLOCK, 2 * BLOCK)
    attn = _attn_fwd(proj, bias, sinks)
    r, w, k2, v, kkn, b, g = _rwkv_prep(proj, mix, prm)
    o, states, sas = _scan_fwd(r, w, k2, v, kkn, b)
    wout, wup_st, wdown = yield ("rest_weights", o)
    cat = _rwkv_post(o, r, k2, v, g, lng, lnb, rk, attn)
    mixo = _matmul(cat, wout, "nn", "out_proj", m=SEQ, n=D_MODEL, k=D_MODEL, tm=SEQ, tn=512)
    x2, h3 = _mix_norm(x, mixo, sm["norm_mix_post"], sm["norm_ffn_pre"])
    u_gate, u_val, gate, val, act = _ffn_up_act(h3, wup_st, sm["conv_w"], sm["conv_b"])
    f = _matmul(act, wdown, "nn", "ffn_down", m=SEQ, n=D_MODEL, k=D_FF, tm=1024, tn=512)
    loss, dy, df, d_g4 = _loss_head(x2, f, sm["norm_ffn_post"], target)

    d_wdown = _matmul(act, df, "tn", "d_wdown", m=D_FF, n=D_MODEL, k=SEQ, tm=512, tn=D_MODEL)
    du, d_convw, d_convb = _ffn_act_bwd(u_gate, u_val, gate, val, df, wdown, sm["conv_w"])
    d_convw = d_convw.transpose(1, 0, 2).reshape(3, 2 * D_FF)
    d_convb = d_convb.reshape(1, 2 * D_FF)
    dh3 = _matmul_nt_shards(du, wup_st, "d_h3", m=SEQ, n=D_MODEL, tm=512, tn=512,
                            a_spec=pl.BlockSpec((2, 512, D_FF), lambda i, j: (0, i, 0)),
                            a_piece=lambda ref, s: ref[s // 2, :, (s % 2) * 2048:(s % 2 + 1) * 2048])
    d_wup = _matmul(h3, du, "tn", "d_wup", m=D_MODEL, n=2 * D_FF, k=SEQ, tm=D_MODEL, tn=512,
                    b_spec=pl.BlockSpec((None, SEQ, 512), lambda i, j: (j // 8, 0, j % 8)),
                    out=((N_CHIPS, D_MODEL, 2048), _stacked(D_MODEL, 512, lambda i, j: (j // 4, 0, j % 4))))
    dx2, dmix, d_g2, d_g3 = _mid_bwd(x2, mixo, dy, dh3, sm["norm_mix_post"], sm["norm_ffn_pre"])
    dcat = _matmul(dmix, wout, "nt", "d_cat", m=SEQ, n=D_MODEL, k=D_MODEL, tm=SEQ, tn=512)
    d_wout = _matmul(cat, dmix, "tn", "d_wout", m=D_MODEL, n=D_MODEL, k=SEQ, tm=512, tn=D_MODEL)
    token = yield ("grads_a", (d_wdown, d_wup, d_wout))
    do, dr_p, dk_p, dv_p, dg, d_lng, d_lnb, d_rk = _rwkv_post_bwd(o, r, k2, v, g, lng, tied(lnb, token), rk, dcat)
    half = N_CHUNK // 2
    ds_end = jnp.zeros(STATE, F32)
    late, ds_mid = _scan_bwd(r, w, k2, v, kkn, b, do, states, sas, ds_end, None, "rwkv_scan_bwd_late", half, half)
    token = yield ("seam_1", ds_mid)
    scan_cts, ds_first = _scan_bwd(r, w, k2, v, kkn, b, do, states, sas, tied(ds_mid, token), late,
                                   "rwkv_scan_bwd_early", 0, half)
    dr_s, dw_s, dk_s, dv_s, dkkn_s, db_s = scan_cts
    token = yield ("seam_2", ds_first)
    prep_grads = _rwkv_prep_bwd(proj, tied(mix, token), prm,
                                (dr_s, dr_p, dw_s, dk_s, dk_p, dv_s, dv_p, dkkn_s, db_s, dg))
    dps, d_mix, d_w0, d_wdu, d_a0, d_wiu, d_wgu, d_kk, d_ka = prep_grads
    dq, dkv, dbias, dsink = _attn_bwd(proj, bias, sinks, dcat)
    d_relb = _bias_table_bwd(dbias.reshape(N_Q_HEADS, N_REL), onehot).T
    dproj = _assemble_dproj(dq, dkv, dps, mix)
    d_win = _matmul(h1, dproj, "tn", "d_win", m=D_MODEL, n=D_IN, k=SEQ, tm=D_MODEL, tn=640,
                    out=((N_CHIPS, D_MODEL, 640), _stacked(D_MODEL, 640, lambda i, j: (j, 0, 0))))
    token = yield ("grads_b", d_win)
    dh1 = _matmul_nt_shards(dproj, win_st, "d_h1", m=SEQ, n=D_MODEL, tm=1024, tn=D_MODEL,
                            a_spec=pl.BlockSpec((1024, D_IN), lambda i, j: (i, 0)),
                            a_piece=lambda ref, s: ref[:, s * 640:(s + 1) * 640])
    grad_x, d_g1 = _first_bwd(x, dx2, dh1, tied(sm["norm_mix_pre"], token))

    grads = {
        "norm_mix_pre": d_g1, "norm_mix_post": d_g2, "norm_ffn_pre": d_g3, "norm_ffn_post": d_g4,
        "w_in": d_win, "rel_bias": d_relb, "sinks": dsink[:, 0].reshape(1, N_Q_HEADS),
        "rwkv_shift_mix": d_mix, "w0": d_w0, "w_decay_up": d_wdu[:LORA_DECAY], "a0": d_a0,
        "w_iclr_up": d_wiu[LORA_DECAY:], "w_gate_up": d_wgu, "k_k": d_kk, "k_a": d_ka,
        "r_k": d_rk.reshape(1, N_Q_HEADS, HEAD_DIM), "ln_x_g": d_lng, "ln_x_b": d_lnb,
        "w_out": d_wout, "w_ffn_up": d_wup, "conv_w": d_convw, "conv_b": d_convb, "w_ffn_down": d_wdown,
    }
    return loss, grad_x, grads


def _place():
    x, y, c = lax.axis_index("x"), lax.axis_index("y"), lax.axis_index("c")
    chips = [(1 - x, y), (x, 1 - y), (1 - x, 1 - y)]
    return x, y, c, chips


def _remote(src, dst, sems, idx, to):
    return pltpu.make_async_remote_copy(src_ref=src, dst_ref=dst, send_sem=sems[0].at[idx], recv_sem=sems[1].at[idx],
                                        device_id=to, device_id_type=MESH)


ROW_ALIGN = 16


def _half(c, rows):
    return pl.ds(pl.multiple_of(c * (rows // 2), ROW_ALIGN), rows // 2)


def _gather_weights(big, small):
    nb, ns = len(big), len(small)

    def body(*refs):
        ins, outs = refs[:nb + ns], refs[nb + ns:2 * (nb + ns)]
        ici, d2d, sml, loc = refs[2 * (nb + ns):2 * (nb + ns) + 2], refs[-5:-3], refs[-3:-1], refs[-1]
        x, y, c, chips = _place()
        me = 2 * x + y
        sib = (x, y, 1 - c)
        local = [pltpu.make_async_copy(ins[a], outs[a].at[me], loc.at[a]) for a in range(nb + ns)]
        for cp in local:
            cp.start()
        sends = []
        for a in range(nb):
            rows = _half(c, big[a].shape[0])
            for kk, chip in enumerate(chips):
                sends.append(_remote(ins[a].at[rows], outs[a].at[me, rows], ici, a * 3 + kk, (*chip, c)))
        for a in range(ns):
            for kk, chip in enumerate(chips):
                sends.append(_remote(ins[nb + a], outs[nb + a].at[me], sml, a * 3 + kk, (*chip, c)))
        for cp in sends:
            cp.start()
        passed = []
        for a in range(nb):
            rows = _half(c, big[a].shape[0])
            for kk, (px, py) in enumerate(chips):
                got = outs[a].at[2 * px + py, rows]
                _remote(got, got, ici, a * 3 + kk, sib).wait_recv()
                fwd = _remote(got, got, d2d, a * 3 + kk, sib)
                fwd.start()
                passed.append(fwd)
        for a in range(nb):
            other = _half(1 - c, big[a].shape[0])
            for kk, (px, py) in enumerate(chips):
                land = outs[a].at[2 * px + py, other]
                _remote(land, land, d2d, a * 3 + kk, sib).wait_recv()
        for a in range(ns):
            for kk, (px, py) in enumerate(chips):
                land = outs[nb + a].at[2 * px + py]
                _remote(land, land, sml, a * 3 + kk, sib).wait_recv()
        for cp in sends + passed:
            cp.wait_send()
        for cp in local:
            cp.wait()

    arrs = list(big) + list(small)
    in_vmem = pl.BlockSpec(memory_space=pltpu.VMEM)
    return pl.pallas_call(
        body, name="gather_weights",
        in_specs=[in_vmem] * len(arrs), out_specs=[in_vmem] * len(arrs),
        out_shape=[jax.ShapeDtypeStruct((N_CHIPS,) + t.shape, t.dtype) for t in arrs],
        scratch_shapes=[pltpu.SemaphoreType.DMA((3 * nb,)), pltpu.SemaphoreType.DMA((3 * nb,)),
                        pltpu.SemaphoreType.DMA((3 * nb,)), pltpu.SemaphoreType.DMA((3 * nb,)),
                        pltpu.SemaphoreType.DMA((3 * ns,)), pltpu.SemaphoreType.DMA((3 * ns,)),
                        pltpu.SemaphoreType.DMA((nb + ns,))],
        compiler_params=pltpu.CompilerParams(has_side_effects=True, vmem_limit_bytes=VMEM_LIMIT),
    )(*arrs)


HBM = pl.BlockSpec(memory_space=pltpu.HBM)
SEM = pl.BlockSpec(memory_space=pltpu.SEMAPHORE)
EFFECT = pltpu.SideEffectType.DATAFLOW_SIDE_EFFECTING


def _copies_start(name, bufs, plan, n, partners=None):
    nb = len(bufs)

    def body(*refs):
        ins, sems, token = refs[:nb], refs[nb:nb + 2 * n], refs[-1]
        if partners is not None:
            barrier = pltpu.get_barrier_semaphore()
            peers = partners[1]()
            for peer in peers:
                pl.semaphore_signal(barrier, inc=1, device_id=peer, device_id_type=MESH)
            pl.semaphore_wait(barrier, len(peers))
        for kk, (src, dst, dev) in enumerate(plan(ins)):
            pltpu.make_async_remote_copy(src_ref=src, dst_ref=dst, send_sem=sems[2 * kk], recv_sem=sems[2 * kk + 1],
                                         device_id=dev, device_id_type=MESH).start()
        token[...] = jnp.zeros_like(token)

    outs = pl.pallas_call(
        body, name=name,
        out_shape=tuple([pltpu.SemaphoreType.DMA(())] * (2 * n) + [pltpu.HBM(t.shape, t.dtype) for t in bufs]
                        + [jax.ShapeDtypeStruct((8, LANES), F32)]),
        in_specs=[HBM] * nb,
        out_specs=tuple([SEM] * (2 * n) + [HBM] * nb + [pl.BlockSpec(memory_space=pltpu.VMEM)]),
        input_output_aliases={t: 2 * n + t for t in range(nb)},
        compiler_params=pltpu.CompilerParams(has_side_effects=EFFECT,
                                             collective_id=None if partners is None else partners[0]),
    )(*[pltpu.with_memory_space_constraint(t, pltpu.HBM) for t in bufs])
    return outs[:2 * n], outs[2 * n:2 * n + nb], outs[-1]


def _copies_wait(name, sems, bufs, plan, n, after):
    nb = len(bufs)
    after = list(after) if isinstance(after, (list, tuple)) else [after]

    def body(*refs):
        ins, sem_refs = refs[:nb], refs[nb:nb + 2 * n]
        for kk, (src, dst, dev) in enumerate(plan(ins)):
            cp = pltpu.make_async_remote_copy(src_ref=src, dst_ref=dst, send_sem=sem_refs[2 * kk],
                                              recv_sem=sem_refs[2 * kk + 1], device_id=dev, device_id_type=MESH)
            cp.wait_send()
            cp.wait_recv()

    return pl.pallas_call(
        body, name=name,
        out_shape=tuple(pltpu.HBM(t.shape, t.dtype) for t in bufs),
        in_specs=[HBM] * nb + [SEM] * (2 * n) + [ANY] * len(after),
        out_specs=tuple([HBM] * nb),
        input_output_aliases={t: t for t in range(nb)},
        compiler_params=pltpu.CompilerParams(has_side_effects=EFFECT),
    )(*bufs, *sems, *after)


def _plan_gather(n_w):
    def plan(refs):
        x, y, c, chips = _place()
        me = 2 * x + y
        return [(refs[a], refs[n_w + a].at[me], (*chip, c)) for a in range(n_w) for chip in chips + [(x, y)]]
    return plan


def _plan_pair_halves(n_g, rows):
    def plan(refs):
        x, y, c, _ = _place()
        return [(refs[a].at[:, _half(1 - c, rows[a])], refs[n_g + a], (x, y, 1 - c)) for a in range(n_g)]
    return plan


def _plan_chip_parts(n_g):
    def plan(refs):
        x, y, c, chips = _place()
        me = 2 * x + y
        return [(refs[a].at[2 * px + py], refs[n_g + a].at[me], (px, py, c))
                for a in range(n_g) for (px, py) in chips]
    return plan


def _plan_pair_fill(n_g, rows):
    def plan(refs):
        x, y, c, _ = _place()
        return [(refs[a].at[_half(c, rows[a])], refs[a].at[_half(c, rows[a])], (x, y, 1 - c)) for a in range(n_g)]
    return plan


def _pair_add(g, got, name):
    _, rows, cols = g.shape
    hr = rows // 2
    tr = min(hr, 256)
    nb = hr // tr

    def body(g_ref, got_ref, p_ref, own_ref):
        val = (g_ref[...] + got_ref[...]).astype(BF16)
        p_ref[...] = val

        @pl.when(pl.program_id(1) == 2 * lax.axis_index("x") + lax.axis_index("y"))
        def _():
            own_ref[...] = val

    def mine(i, s):
        return (2 * lax.axis_index("x") + lax.axis_index("y"), i, 0)

    return pl.pallas_call(
        body, name=name, grid=(nb, N_CHIPS),
        in_specs=[pl.BlockSpec((None, tr, cols), lambda i, s: (s, lax.axis_index("c") * nb + i, 0)),
                  pl.BlockSpec((None, tr, cols), lambda i, s: (s, i, 0))],
        out_specs=[pl.BlockSpec((None, tr, cols), lambda i, s: (s, i, 0)), pl.BlockSpec((None, tr, cols), mine)],
        out_shape=[jax.ShapeDtypeStruct((N_CHIPS, hr, cols), BF16)] * 2,
        compiler_params=_cp(("parallel", "arbitrary")),
    )(g, got)


def _chip_sum(parts, name):
    _, hr, cols = parts.shape
    tr = min(hr, 128)
    nb = hr // tr

    def body(t_ref, o_ref):
        part = [t_ref[s].astype(F32) for s in range(N_CHIPS)]
        o_ref[...] = ((part[0] + part[1]) + part[2]) + part[3]

    return pl.pallas_call(
        body, name=name, grid=(nb,),
        in_specs=[pl.BlockSpec((N_CHIPS, tr, cols), lambda i: (0, i, 0))],
        out_specs=pl.BlockSpec((tr, cols), lambda i: (lax.axis_index("c") * nb + i, 0)),
        out_shape=jax.ShapeDtypeStruct((2 * hr, cols), F32),
        compiler_params=_cp(("parallel",)),
    )(parts)


class _Reduction:
    def __init__(self, tag, rows, first_id):
        self.tag, self.n, self.rows, self.first_id = tag, len(rows), rows, first_id
        self.plans = (_plan_pair_halves(self.n, rows), _plan_chip_parts(self.n), _plan_pair_fill(self.n, rows))
        self.flight = None

    def _name(self, what):
        return f"grad_{self.tag}_{what}"

    @staticmethod
    def _sibling():
        x, y, c, _ = _place()
        return [(x, y, 1 - c)]

    @staticmethod
    def _same_core_elsewhere():
        x, y, c, chips = _place()
        return [(*chip, c) for chip in chips]

    def start(self, gs):
        gots = [lax.empty((N_CHIPS, t.shape[1] // 2, t.shape[2]), F32) for t in gs]
        self.flight = _copies_start(self._name("pair_start"), list(gs) + gots, self.plans[0], self.n,
                                    (self.first_id, self._sibling))
        return self.flight[2]

    def after_pair(self, after):
        sems, bufs, _ = self.flight
        out = _copies_wait(self._name("pair_wait"), sems, bufs, self.plans[0], self.n, after)
        sums = [_pair_add(g, got, self._name(f"pair_add_{i}"))
                for i, (g, got) in enumerate(zip(out[:self.n], out[self.n:]))]
        self.flight = _copies_start(self._name("chip_start"), [p for p, _ in sums] + [own for _, own in sums],
                                    self.plans[1], 3 * self.n, (self.first_id + 1, self._same_core_elsewhere))
        return self.flight[2]

    def after_chips(self, after):
        sems, bufs, _ = self.flight
        out = _copies_wait(self._name("chip_wait"), sems, bufs, self.plans[1], 3 * self.n, after)
        fulls = [_chip_sum(t, self._name(f"chip_sum_{i}")) for i, t in enumerate(out[self.n:])]
        self.flight = _copies_start(self._name("fill_start"), fulls, self.plans[2], self.n,
                                    (self.first_id + 2, self._sibling))
        return self.flight[2]

    def finish(self, after):
        sems, bufs, _ = self.flight
        return _copies_wait(self._name("fill_wait"), sems, bufs, self.plans[2], self.n, after)


def _adamw_math(w, g, m, v):
    nm = ADAM_B1 * m + (1.0 - ADAM_B1) * g
    nv = ADAM_B2 * v + (1.0 - ADAM_B2) * (g * g)
    m_hat = nm / (1.0 - ADAM_B1 ** ADAM_STEP)
    v_hat = nv / (1.0 - ADAM_B2 ** ADAM_STEP)
    return -ADAM_LR * (m_hat / (jnp.sqrt(v_hat) + ADAM_EPS) + ADAM_WD * w), nm, nv


def _adamw(w, g, m, v, name, tr):
    r, cdim = w.shape

    def body(w_ref, g_ref, m_ref, v_ref, d_ref, nm_ref, nv_ref):
        d_ref[...], nm_ref[...], nv_ref[...] = _adamw_math(w_ref[...], g_ref[...], m_ref[...], v_ref[...])

    return pl.pallas_call(
        body, name=name, grid=(r // tr,), in_specs=[_rows(tr, cdim)] * 4, out_specs=[_rows(tr, cdim)] * 3,
        out_shape=[jax.ShapeDtypeStruct((r, cdim), F32)] * 3, compiler_params=_cp(("parallel",)),
    )(w, g, m, v)


def _adamw_small(w, parts, m, v, shapes):
    n_rows = w.shape[0]

    def scatter(src, outs):
        row = 0
        for (rows, cols), out in zip(shapes, outs):
            if cols == LANES:
                out[...] = src[row:row + rows, :]
            elif cols > LANES:
                per = cols // LANES
                for r in range(rows):
                    for cb in range(per):
                        out[r:r + 1, cb * LANES:(cb + 1) * LANES] = src[row + r * per + cb:row + r * per + cb + 1, :]
            else:
                per = LANES // cols
                for r in range(rows):
                    out[r:r + 1, :] = src[row + r // per:row + r // per + 1, (r % per) * cols:(r % per + 1) * cols]
            row += -(-rows * cols // LANES)

    def body(w_ref, p_ref, m_ref, v_ref, *rest):
        outs, scr = rest[:-4], rest[-4:]
        g = p_ref[0]
        for dev in range(1, N_DEV):
            g = g + p_ref[dev]
        scr[3][...] = g
        scr[0][...], scr[1][...], scr[2][...] = _adamw_math(w_ref[...], g, m_ref[...], v_ref[...])
        n = len(shapes)
        for kind in range(4):
            scatter(scr[kind], outs[kind * n:(kind + 1) * n])

    outs = pl.pallas_call(
        body, name="adamw_small", grid=(1,),
        in_specs=[_const(w.shape), _const(parts.shape), _const(w.shape), _const(w.shape)],
        out_specs=[_const(s) for s in shapes] * 4, out_shape=[jax.ShapeDtypeStruct(s, F32) for s in shapes] * 4,
        scratch_shapes=[pltpu.VMEM((n_rows, LANES), F32)] * 4,
        compiler_params=_cp(("arbitrary",)),
    )(w, parts, m, v)
    n = len(shapes)
    return [outs[kind * n:(kind + 1) * n] for kind in range(4)]


REPLICATED = (("norm_mix_pre", 1024), ("norm_mix_post", 1024), ("norm_ffn_pre", 1024), ("norm_ffn_post", 1024),
              ("rel_bias", 256), ("sinks", 8), ("rwkv_shift_mix", 1792), ("w0", 512), ("a0", 512), ("k_k", 512),
              ("k_a", 512), ("r_k", 512), ("ln_x_g", 512), ("ln_x_b", 512), ("conv_b", 8192))
SMALL_SHARDED = (("w_decay_up", LORA_DECAY, D_RWKV), ("w_iclr_up", LORA_ICLR, D_RWKV),
                 ("w_gate_up", LORA_GATE, D_RWKV), ("conv_w", 3, 2 * D_FF))
BIG = (("w_in", D_MODEL, 640), ("w_out", 256, D_MODEL), ("w_ffn_up", D_MODEL, 2048), ("w_ffn_down", 1024, D_MODEL))
PACK_ALIGN = 8 * LANES


def _pack(pieces):
    flat = []
    for t in pieces:
        t = t.reshape(-1)
        pad = (-t.shape[0]) % LANES
        flat.append(jnp.pad(t, (0, pad)) if pad else t)
    flat = jnp.concatenate(flat)
    pad = (-flat.shape[0]) % PACK_ALIGN
    return jnp.pad(flat, (0, pad)).reshape(-1, LANES)


def kernel(x, norm_mix_pre, norm_mix_post, norm_ffn_pre, norm_ffn_post, w_in, rel_bias, sinks, rwkv_shift_mix, w0, w_decay_up, a0, w_iclr_up, w_gate_up, k_k, k_a, r_k, ln_x_g, ln_x_b, w_out, w_ffn_up, conv_w, conv_b, w_ffn_down, loss_target, m_norm_mix_pre, m_norm_mix_post, m_norm_ffn_pre, m_norm_ffn_post, m_w_in, m_rel_bias, m_sinks, m_rwkv_shift_mix, m_w0, m_w_decay_up, m_a0, m_w_iclr_up, m_w_gate_up, m_k_k, m_k_a, m_r_k, m_ln_x_g, m_ln_x_b, m_w_out, m_w_ffn_up, m_conv_w, m_conv_b, m_w_ffn_down, v_norm_mix_pre, v_norm_mix_post, v_norm_ffn_pre, v_norm_ffn_post, v_w_in, v_rel_bias, v_sinks, v_rwkv_shift_mix, v_w0, v_w_decay_up, v_a0, v_w_iclr_up, v_w_gate_up, v_k_k, v_k_a, v_r_k, v_ln_x_g, v_ln_x_b, v_w_out, v_w_ffn_up, v_conv_w, v_conv_b, v_w_ffn_down):
    given = dict(locals())
    names = [n for n, _ in REPLICATED] + [n for n, _, _ in SMALL_SHARDED] + [n for n, _, _ in BIG]
    order = ["norm_mix_pre", "norm_mix_post", "norm_ffn_pre", "norm_ffn_post", "w_in", "rel_bias", "sinks",
             "rwkv_shift_mix", "w0", "w_decay_up", "a0", "w_iclr_up", "w_gate_up", "k_k", "k_a", "r_k", "ln_x_g",
             "ln_x_b", "w_out", "w_ffn_up", "conv_w", "conv_b", "w_ffn_down"]
    assert sorted(names) == sorted(order)

    big_sh = {n: given[n].reshape(a, b).astype(BF16) for n, a, b in BIG}
    small_sh = [given[n].reshape(r, c // N_CHIPS) for n, r, c in SMALL_SHARDED]
    gathered = _gather_weights([big_sh["w_in"]], small_sh)
    rest = ("w_out", "w_ffn_up", "w_ffn_down")
    win_st, rest_sh = lax.optimization_barrier((gathered[0], [big_sh[n] for n in rest]))
    sm = {n: given[n] for n, _ in REPLICATED}
    sm["r_k"] = r_k.reshape(N_Q_HEADS, HEAD_DIM)
    for (n, r, c), st in zip(SMALL_SHARDED, gathered[1:]):
        sm[n] = st.transpose(1, 0, 2).reshape(r, c)

    lands = [lax.empty((N_CHIPS,) + t.shape, BF16) for t in rest_sh]
    plan_w = _plan_gather(len(rest))
    n_w = N_CHIPS * len(rest)
    w_sems, w_bufs, token = _copies_start("gather_rest_start", rest_sh + lands, plan_w, n_w)
    sm["norm_mix_pre"] = norm_mix_pre + token[0:1, 0:1]

    def on_rest_weights(after):
        out = _copies_wait("gather_rest_wait", w_sems, w_bufs, plan_w, n_w, after)
        wout_st, wup_st, wdown_st = out[3:]
        return wout_st.reshape(D_MODEL, D_MODEL), wup_st, wdown_st.reshape(D_FF, D_MODEL)

    red_a = _Reduction("a", (1024, D_MODEL, 256), first_id=0)
    red_b = _Reduction("b", (D_MODEL,), first_id=3)

    def on_grads_a(gs):
        d_wdown, d_wup, d_wout = gs
        return red_a.start([d_wdown.reshape(N_CHIPS, 1024, D_MODEL), d_wup, d_wout.reshape(N_CHIPS, 256, D_MODEL)])

    handlers = {"rest_weights": on_rest_weights, "grads_a": on_grads_a, "seam_1": red_a.after_pair,
                "seam_2": red_a.after_chips, "grads_b": lambda g: red_b.start([g])}
    steps = _local_step(x[0], loss_target[0], sm, win_st)
    kind, payload = next(steps)
    while True:
        try:
            kind, payload = steps.send(handlers[kind](payload))
        except StopIteration as done:
            loss, grad_x, grads = done.value
            break

    small_names = [n for n, _ in REPLICATED] + [n for n, _, _ in SMALL_SHARDED]

    def shard_cols(t, s):
        return t[:, s * (t.shape[1] // N_CHIPS):(s + 1) * (t.shape[1] // N_CHIPS)]

    for_chip = jnp.stack([_pack([loss[0]] + [grads[n] for n, _ in REPLICATED]
                                + [shard_cols(grads[n], s) for n, _, _ in SMALL_SHARDED]) for s in range(N_CHIPS)])
    land = lax.empty((N_DEV,) + for_chip.shape[1:], F32)

    def plan_small(refs):
        x, y, c, _ = _place()
        out = []
        for rel in range(N_DEV):
            px, py, pc = x ^ (rel >> 2), y ^ ((rel >> 1) & 1), c ^ (rel & 1)
            out.append((refs[0].at[2 * px + py], refs[1].at[4 * x + 2 * y + c], (px, py, pc)))
        return out

    s_sems, s_bufs, s_token = _copies_start("grad_small_start", [for_chip, land], plan_small, N_DEV)

    red_b.after_pair([grad_x, s_token])
    g_out = {}
    g_out["w_ffn_down"], g_out["w_ffn_up"], g_out["w_out"] = red_a.finish(grad_x)

    delta, new_m, new_v = {}, {}, {}

    def update(n, a, b):
        delta[n], new_m[n], new_v[n] = _adamw(given[n].reshape(a, b), g_out[n], given["m_" + n].reshape(a, b),
                                              given["v_" + n].reshape(a, b), "adamw_" + n, 128)

    for n, a, b in BIG[1:]:
        update(n, a, b)
    done = [delta[n] for n, _, _ in BIG[1:]]
    red_b.after_chips(done)
    parts = _copies_wait("grad_small_wait", s_sems, s_bufs, plan_small, N_DEV, done)[1]
    no_param = jnp.zeros((LANES,), F32)
    packs = [_pack([no_param] + [given[pre + n] for n in small_names]) for pre in ("", "m_", "v_")]

    def piece_shape(n):
        shape = given[n].shape
        rows, cols = int(np.prod(shape[:-1])), shape[-1]
        whole = cols % LANES == 0 or (LANES % cols == 0 and (rows * cols) % LANES == 0 and cols >= HEAD_DIM)
        return (rows, cols) if whole else (-(-rows * cols // LANES), LANES)

    shapes = [(1, LANES)] + [piece_shape(n) for n in small_names]
    upd = _adamw_small(packs[0], parts, packs[1], packs[2], shapes)
    loss = upd[3][0][0, 0]
    for i, n in enumerate(small_names):
        shape = given[n].shape
        size = int(np.prod(shape))
        delta[n], new_m[n], new_v[n], g_out[n] = (u[1 + i].reshape(-1)[:size].reshape(shape) for u in upd)
    g_out["w_in"], = red_b.finish(upd[0][0])
    update(*BIG[0])

    def shaped(d):
        return [d[n].reshape(given[n].shape) for n in order]

    return (loss, grad_x.reshape(x.shape), *shaped(g_out), *shaped(delta), *shaped(new_m), *shaped(new_v))
```

```python
import math

import numpy as np
import jax
import jax.numpy as jnp
from jax import lax
from jax.experimental import pallas as pl
from jax.experimental.pallas import tpu as pltpu

F32 = jnp.float32
BF16 = jnp.bfloat16
MESH = pl.DeviceIdType.MESH

SEQ = 2048
D_MODEL = 1024
HEAD_DIM = 64
D_ATTN = 512
D_RWKV = 512
D_KV = 128
N_Q_HEADS = 8
N_KV_HEADS = 2
Q_PER_KV = 4
BLOCK = 128
N_BUCKETS = 32
MAX_DISTANCE = 128
LORA_DECAY = 64
LORA_ICLR = 64
LORA_GATE = 128
RWKV_COLS = 3 * D_RWKV + LORA_DECAY + LORA_ICLR + LORA_GATE
P_OFF = D_ATTN + 2 * D_KV
D_IN = P_OFF + RWKV_COLS
D_FF = 4096
NORM_EPS = 1e-6
GN_EPS = 64e-5
NEG_INF = -1e30
N_CHIPS = 4
N_DEV = 8
HEAD_SHIFT = HEAD_DIM.bit_length() - 1
BLOCK_SHIFT = BLOCK.bit_length() - 1

ADAM_LR = 0.001
ADAM_B1 = 0.9
ADAM_B2 = 0.999
ADAM_EPS = 1e-08
ADAM_WD = 0.01
ADAM_STEP = 10

VMEM_LIMIT = 52 * 1024 * 1024
LANES = 128


def _cp(sem=None, vmem=VMEM_LIMIT):
    kw = dict(vmem_limit_bytes=vmem)
    if sem is not None:
        kw["dimension_semantics"] = sem
    return pltpu.CompilerParams(**kw)


def _rows(tr, nc):
    return pl.BlockSpec((tr, nc), lambda i: (i, 0))


def _const(shape):
    return pl.BlockSpec(shape, lambda *_: (0,) * len(shape))


ANY = pl.BlockSpec(memory_space=pl.ANY)


def _split(x, n):
    parts = []
    for _ in range(n - 1):
        h = x.astype(BF16)
        parts.append(h)
        x = x - h.astype(F32)
    parts.append(x.astype(BF16))
    return parts


NN = (((1,), (0,)), ((), ()))
NT = (((1,), (1,)), ((), ()))
TN = (((0,), (0,)), ((), ()))


def _dot(a, b, dn=NN):
    return lax.dot_general(a, b, dn, preferred_element_type=F32)


def _dot_ind(x, ind_bf16, n=3):
    acc = None
    for part in _split(x, n):
        t = _dot(part, ind_bf16)
        acc = t if acc is None else acc + t
    return acc


def _head_ones(n):
    r = lax.broadcasted_iota(jnp.int32, (n, n), 0) >> HEAD_SHIFT
    c = lax.broadcasted_iota(jnp.int32, (n, n), 1) >> HEAD_SHIFT
    return jnp.where(r == c, 1.0, 0.0).astype(BF16)


def _matmul(a, b, mode, name, *, m, n, k, tm, tn, a_spec=None, b_spec=None, out=None):
    dn = {"nn": NN, "nt": NT, "tn": TN}[mode]

    def body(a_ref, b_ref, o_ref):
        o_ref[...] = _dot(a_ref[...], b_ref[...], dn)

    if a_spec is None:
        a_spec = pl.BlockSpec((k, tm), lambda i, j: (0, i)) if mode == "tn" else pl.BlockSpec((tm, k), lambda i, j: (i, 0))
    if b_spec is None:
        b_spec = pl.BlockSpec((tn, k), lambda i, j: (j, 0)) if mode == "nt" else pl.BlockSpec((k, tn), lambda i, j: (0, j))
    return pl.pallas_call(
        body, name=name, grid=(m // tm, n // tn),
        in_specs=[a_spec, b_spec],
        out_specs=pl.BlockSpec((tm, tn), lambda i, j: (i, j)) if out is None else out[1],
        out_shape=jax.ShapeDtypeStruct((m, n) if out is None else out[0], F32),
        compiler_params=_cp(("parallel", "parallel")),
    )(a, b)


def _matmul_nt_shards(a, b_st, name, *, m, n, tm, tn, a_spec, a_piece):
    ks = b_st.shape[2]

    def body(a_ref, b_ref, o_ref):
        acc = _dot(a_piece(a_ref, 0), b_ref[0], NT)
        for s in range(1, N_CHIPS):
            acc = acc + _dot(a_piece(a_ref, s), b_ref[s], NT)
        o_ref[...] = acc

    return pl.pallas_call(
        body, name=name, grid=(m // tm, n // tn),
        in_specs=[a_spec, pl.BlockSpec((N_CHIPS, tn, ks), lambda i, j: (0, j, 0))],
        out_specs=pl.BlockSpec((tm, tn), lambda i, j: (i, j)),
        out_shape=jax.ShapeDtypeStruct((m, n), F32),
        compiler_params=_cp(("parallel", "parallel")),
    )(a, b_st)


def _rstd(x):
    return lax.rsqrt(jnp.mean(x * x, axis=-1, keepdims=True) + NORM_EPS)


def _rms_bwd(x, r, g, dy):
    gy = dy * g
    return r * gy - x * ((r * r * r) * (jnp.sum(x * gy, axis=-1, keepdims=True) / x.shape[-1]))


TR = 256
TRN = 512


def _norm_cast(x, g, name):
    def body(x_ref, g_ref, h_ref):
        x = x_ref[...]
        h_ref[...] = (x * _rstd(x) * g_ref[...]).astype(BF16)

    return pl.pallas_call(
        body, name=name, grid=(SEQ // TRN,),
        in_specs=[_rows(TRN, D_MODEL), _const((1, D_MODEL))],
        out_specs=_rows(TRN, D_MODEL),
        out_shape=jax.ShapeDtypeStruct((SEQ, D_MODEL), BF16),
        compiler_params=_cp(("parallel",)),
    )(x, g)


def _mix_norm(x, mix, g2, g3):
    def body(x_ref, mix_ref, g2_ref, g3_ref, x2_ref, h3_ref):
        mixv = mix_ref[...]
        x2 = x_ref[...] + mixv * _rstd(mixv) * g2_ref[...]
        x2_ref[...] = x2
        h3_ref[...] = (x2 * _rstd(x2) * g3_ref[...]).astype(BF16)

    return pl.pallas_call(
        body, name="mix_norm", grid=(SEQ // TRN,),
        in_specs=[_rows(TRN, D_MODEL), _rows(TRN, D_MODEL), _const((1, D_MODEL)), _const((1, D_MODEL))],
        out_specs=[_rows(TRN, D_MODEL), _rows(TRN, D_MODEL)],
        out_shape=[jax.ShapeDtypeStruct((SEQ, D_MODEL), F32), jax.ShapeDtypeStruct((SEQ, D_MODEL), BF16)],
        compiler_params=_cp(("parallel",)),
    )(x, mix, g2, g3)


def _loss_head(x2, f, g4, target):
    def body(x2_ref, f_ref, g4_ref, t_ref, loss_ref, dy_ref, df_ref, dg_ref):
        i = pl.program_id(0)
        f = f_ref[...]
        g4 = g4_ref[...]
        r = _rstd(f)
        e = x2_ref[...] + f * r * g4 - t_ref[...]
        dy = e * (1.0 / D_MODEL)
        dy_ref[...] = dy
        df_ref[...] = _rms_bwd(f, r, g4, dy).astype(BF16)
        part = 0.5 * jnp.sum(jnp.sum(e * e, axis=-1, keepdims=True), axis=0, keepdims=True) * (1.0 / D_MODEL)
        dg = jnp.sum(dy * f * r, axis=0, keepdims=True)

        @pl.when(i == 0)
        def _():
            loss_ref[...] = jnp.zeros_like(loss_ref)
            dg_ref[...] = jnp.zeros_like(dg_ref)

        loss_ref[...] += jnp.broadcast_to(part, loss_ref.shape)
        dg_ref[...] += dg

    return pl.pallas_call(
        body, name="loss_head", grid=(SEQ // TRN,),
        in_specs=[_rows(TRN, D_MODEL), _rows(TRN, D_MODEL), _const((1, D_MODEL)), _rows(TRN, D_MODEL)],
        out_specs=[_const((8, LANES)), _rows(TRN, D_MODEL), _rows(TRN, D_MODEL), _const((1, D_MODEL))],
        out_shape=[jax.ShapeDtypeStruct((8, LANES), F32), jax.ShapeDtypeStruct((SEQ, D_MODEL), F32),
                   jax.ShapeDtypeStruct((SEQ, D_MODEL), BF16), jax.ShapeDtypeStruct((1, D_MODEL), F32)],
        compiler_params=_cp(("arbitrary",)),
    )(x2, f, g4, target)


def _mid_bwd(x2, mix, dy, dh3, g2, g3):
    def body(x2_ref, mix_ref, dy_ref, dh3_ref, g2_ref, g3_ref, dx2_ref, dmix_ref, dg2_ref, dg3_ref):
        i = pl.program_id(0)
        x2 = x2_ref[...]
        mixv = mix_ref[...]
        dh3 = dh3_ref[...]
        r3 = _rstd(x2)
        dx2 = dy_ref[...] + _rms_bwd(x2, r3, g3_ref[...], dh3)
        dx2_ref[...] = dx2
        r2 = _rstd(mixv)
        dmix_ref[...] = _rms_bwd(mixv, r2, g2_ref[...], dx2).astype(BF16)

        @pl.when(i == 0)
        def _():
            dg2_ref[...] = jnp.zeros_like(dg2_ref)
            dg3_ref[...] = jnp.zeros_like(dg3_ref)

        dg3_ref[...] += jnp.sum(dh3 * x2 * r3, axis=0, keepdims=True)
        dg2_ref[...] += jnp.sum(dx2 * mixv * r2, axis=0, keepdims=True)

    return pl.pallas_call(
        body, name="mid_bwd", grid=(SEQ // TRN,),
        in_specs=[_rows(TRN, D_MODEL)] * 4 + [_const((1, D_MODEL))] * 2,
        out_specs=[_rows(TRN, D_MODEL), _rows(TRN, D_MODEL), _const((1, D_MODEL)), _const((1, D_MODEL))],
        out_shape=[jax.ShapeDtypeStruct((SEQ, D_MODEL), F32), jax.ShapeDtypeStruct((SEQ, D_MODEL), BF16),
                   jax.ShapeDtypeStruct((1, D_MODEL), F32), jax.ShapeDtypeStruct((1, D_MODEL), F32)],
        compiler_params=_cp(("arbitrary",)),
    )(x2, mix, dy, dh3, g2, g3)


def _first_bwd(x, dx2, dh1, g1):
    def body(x_ref, dx2_ref, dh1_ref, g1_ref, dx_ref, dg1_ref):
        i = pl.program_id(0)
        x = x_ref[...]
        dh1 = dh1_ref[...]
        r = _rstd(x)
        dx_ref[...] = dx2_ref[...] + _rms_bwd(x, r, g1_ref[...], dh1)

        @pl.when(i == 0)
        def _():
            dg1_ref[...] = jnp.zeros_like(dg1_ref)

        dg1_ref[...] += jnp.sum(dh1 * x * r, axis=0, keepdims=True)

    return pl.pallas_call(
        body, name="first_bwd", grid=(SEQ // TRN,),
        in_specs=[_rows(TRN, D_MODEL)] * 3 + [_const((1, D_MODEL))],
        out_specs=[_rows(TRN, D_MODEL), _const((1, D_MODEL))],
        out_shape=[jax.ShapeDtypeStruct((SEQ, D_MODEL), F32), jax.ShapeDtypeStruct((1, D_MODEL), F32)],
        compiler_params=_cp(("arbitrary",)),
    )(x, dx2, dh1, g1)


TC = 256
N_CB = D_FF // TC
GELU_C = math.sqrt(2.0 / math.pi)


def _shift_down(u, s):
    rolled = pltpu.roll(u, s, 0)
    row = lax.broadcasted_iota(jnp.int32, u.shape, 0)
    return jnp.where(row >= s, rolled, 0.0)


def _shift_up(u, s):
    n = u.shape[0]
    rolled = pltpu.roll(u, n - s, 0)
    row = lax.broadcasted_iota(jnp.int32, u.shape, 0)
    return jnp.where(row < n - s, rolled, 0.0)


def _conv3(u, w, b):
    return b + w[0:1] * _shift_down(u, 2) + w[1:2] * _shift_down(u, 1) + w[2:3] * u


def _gelu_and_grad(x):
    inner = GELU_C * (x + 0.044715 * (x * x * x))
    t = jnp.tanh(inner)
    gelu = 0.5 * x * (1.0 + t)
    dgelu = 0.5 * (1.0 + t) + 0.5 * x * (1.0 - t * t) * (GELU_C * (1.0 + 3 * 0.044715 * (x * x)))
    return gelu, dgelu


def _ffn_specs():
    col = lambda off: pl.BlockSpec((SEQ, TC), lambda *g: (0, g[-1] + off))
    w = lambda off: pl.BlockSpec((3, TC), lambda *g: (0, g[-1] + off))
    b = lambda off: pl.BlockSpec((1, TC), lambda *g: (0, g[-1] + off))
    return col, w, b


def _ffn_up_act(h3, wup_st, conv_w, conv_b):
    col, w, b = _ffn_specs()
    per_shard = wup_st.shape[2] // TC

    def body(h_ref, upg_ref, upv_ref, wg_ref, wv_ref, bg_ref, bv_ref, ug_ref, uv_ref, gate_ref, val_ref, act_ref):
        h = h_ref[...]
        ug = _dot(h, upg_ref[...])
        uv = _dot(h, upv_ref[...])
        ug_ref[...] = ug
        uv_ref[...] = uv
        gate = _conv3(ug, wg_ref[...], bg_ref[...])
        val = _conv3(uv, wv_ref[...], bv_ref[...])
        gate_ref[...] = gate
        val_ref[...] = val
        act_ref[...] = (_gelu_and_grad(gate)[0] * val).astype(BF16)

    return pl.pallas_call(
        body, name="ffn_up_act", grid=(N_CB,),
        in_specs=[_const((SEQ, D_MODEL)),
                  pl.BlockSpec((None, D_MODEL, TC), lambda j: (j // per_shard, 0, j % per_shard)),
                  pl.BlockSpec((None, D_MODEL, TC), lambda j: (2 + j // per_shard, 0, j % per_shard)),
                  w(0), w(N_CB), b(0), b(N_CB)],
        out_specs=[col(0)] * 5,
        out_shape=[jax.ShapeDtypeStruct((SEQ, D_FF), F32)] * 4 + [jax.ShapeDtypeStruct((SEQ, D_FF), BF16)],
        compiler_params=_cp(("parallel",)),
    )(h3, wup_st, wup_st, conv_w, conv_w, conv_b, conv_b)


def _ffn_act_bwd(u_gate, u_val, gate, val, df, wdown, conv_w):
    col, w, _ = _ffn_specs()
    both = lambda rows: pl.BlockSpec((2, rows, TC), lambda j: (0, 0, j))

    def body(ug_ref, uv_ref, gate_ref, val_ref, df_ref, wd_ref, wg_ref, wv_ref, du_ref, dw_ref, db_ref):
        da = _dot(df_ref[...], wd_ref[...], NT)
        gelu, dgelu = _gelu_and_grad(gate_ref[...])
        halves = ((da * val_ref[...] * dgelu, ug_ref, wg_ref[...]), (da * gelu, uv_ref, wv_ref[...]))
        for h, (duc, u_ref, wh) in enumerate(halves):
            uh = u_ref[...]
            up1, up2 = _shift_up(duc, 1), _shift_up(duc, 2)
            du_ref[h] = (wh[2:3] * duc + wh[1:2] * up1 + wh[0:1] * up2).astype(BF16)
            db_ref[h] = jnp.sum(duc, axis=0, keepdims=True)
            dw_ref[h] = jnp.concatenate(
                [jnp.sum(up2 * uh, axis=0, keepdims=True), jnp.sum(up1 * uh, axis=0, keepdims=True),
                 jnp.sum(duc * uh, axis=0, keepdims=True)], axis=0)

    return pl.pallas_call(
        body, name="ffn_act_bwd", grid=(N_CB,),
        in_specs=[col(0)] * 4 + [_const((SEQ, D_MODEL)), pl.BlockSpec((TC, D_MODEL), lambda j: (j, 0)), w(0), w(N_CB)],
        out_specs=[both(SEQ), both(3), both(1)],
        out_shape=[jax.ShapeDtypeStruct((2, SEQ, D_FF), BF16), jax.ShapeDtypeStruct((2, 3, D_FF), F32),
                   jax.ShapeDtypeStruct((2, 1, D_FF), F32)],
        compiler_params=_cp(("parallel",)),
    )(u_gate, u_val, gate, val, df, wdown, conv_w, conv_w)


def _t5_onehot():
    rel = (np.arange(BLOCK)[:, None] + BLOCK) - np.arange(2 * BLOCK)[None, :]
    n = np.maximum(rel, 0)
    max_exact = N_BUCKETS // 2
    large = max_exact + (np.log(np.maximum(n, 1).astype(np.float32) / np.float32(max_exact))
                         / np.float32(math.log(MAX_DISTANCE / max_exact))
                         * np.float32(N_BUCKETS - max_exact)).astype(np.int32)
    large = np.minimum(large, N_BUCKETS - 1)
    bucket = np.where(n < max_exact, n, large).reshape(-1)
    return (bucket[None, :] == np.arange(N_BUCKETS)[:, None]).astype(np.float32)


N_REL = BLOCK * 2 * BLOCK


def _bias_table(rel_bias_t, onehot):
    def body(rb_ref, oh_ref, o_ref):
        o_ref[...] = _dot_ind(rb_ref[...], oh_ref[...])

    return pl.pallas_call(
        body, name="bias_table", grid=(1,),
        in_specs=[_const((N_Q_HEADS, N_BUCKETS)), _const((N_BUCKETS, N_REL))],
        out_specs=_const((N_Q_HEADS, N_REL)),
        out_shape=jax.ShapeDtypeStruct((N_Q_HEADS, N_REL), F32),
        compiler_params=_cp(("arbitrary",)),
    )(rel_bias_t, onehot)


def _bias_table_bwd(dbias, onehot):
    def body(db_ref, oh_ref, o_ref):
        acc = None
        for part in _split(db_ref[...], 3):
            t = _dot(part, oh_ref[...], NT)
            acc = t if acc is None else acc + t
        o_ref[...] = acc

    return pl.pallas_call(
        body, name="bias_table_bwd", grid=(1,),
        in_specs=[_const((N_Q_HEADS, N_REL)), _const((N_BUCKETS, N_REL))],
        out_specs=_const((N_Q_HEADS, N_BUCKETS)),
        out_shape=jax.ShapeDtypeStruct((N_Q_HEADS, N_BUCKETS), F32),
        compiler_params=_cp(("arbitrary",)),
    )(dbias, onehot)


def _attn_pieces(n, q, kvp, kvc, bias_ref, sinks_ref, hk):
    qi = lax.broadcasted_iota(jnp.int32, (BLOCK, 2 * BLOCK), 0)
    kj = lax.broadcasted_iota(jnp.int32, (BLOCK, 2 * BLOCK), 1)
    rel = qi + BLOCK - kj
    first_key = jnp.where(n > 0, 0, BLOCK)
    ok = jnp.where(rel >= 0, jnp.where(rel < BLOCK, jnp.where(kj >= first_key, 1.0, 0.0), 0.0), 0.0)
    ok4 = jnp.concatenate([ok] * Q_PER_KV, axis=0) > 0.5
    c0 = hk * HEAD_DIM
    kcat = jnp.concatenate([kvp[:, c0:c0 + HEAD_DIM], kvc[:, c0:c0 + HEAD_DIM]], axis=0).astype(BF16)
    vcat = jnp.concatenate([kvp[:, D_KV + c0:D_KV + c0 + HEAD_DIM], kvc[:, D_KV + c0:D_KV + c0 + HEAD_DIM]],
                           axis=0).astype(BF16)
    q0 = hk * Q_PER_KV * HEAD_DIM
    qs = jnp.concatenate([q[:, q0 + g * HEAD_DIM:q0 + (g + 1) * HEAD_DIM] for g in range(Q_PER_KV)],
                         axis=0).astype(BF16)
    s = _dot(qs, kcat, NT) * (HEAD_DIM ** -0.5) + bias_ref[hk]
    s = jnp.where(ok4, s, NEG_INF)
    row = lax.broadcasted_iota(jnp.int32, (Q_PER_KV * BLOCK, 1), 0)
    sink = jnp.zeros((Q_PER_KV * BLOCK, 1), F32)
    for g in range(Q_PER_KV):
        sink = jnp.where((row >> BLOCK_SHIFT) == g, sinks_ref[hk * Q_PER_KV + g], sink)
    m = jnp.maximum(jnp.max(s, axis=-1, keepdims=True), sink)
    p = jnp.exp(s - m)
    es = jnp.exp(sink - m)
    inv = 1.0 / (jnp.sum(p, axis=-1, keepdims=True) + es)
    return qs, kcat, vcat, p * inv, es * inv


def _attn_in_specs():
    return [pl.BlockSpec((BLOCK, D_ATTN), lambda n: (n, 0)),
            pl.BlockSpec((BLOCK, 2 * D_KV), lambda n: (jnp.maximum(n - 1, 0), D_ATTN // (2 * D_KV))),
            pl.BlockSpec((BLOCK, 2 * D_KV), lambda n: (n, D_ATTN // (2 * D_KV))),
            _const((N_KV_HEADS, Q_PER_KV * BLOCK, 2 * BLOCK)),
            pl.BlockSpec(memory_space=pltpu.SMEM)]


def _unstack_heads(t):
    return jnp.concatenate([t[g * BLOCK:(g + 1) * BLOCK] for g in range(Q_PER_KV)], axis=1)


def _attn_fwd(proj, bias, sinks):
    def body(q_ref, kvp_ref, kvc_ref, bias_ref, sinks_ref, o_ref):
        n = pl.program_id(0)
        q, kvp, kvc = q_ref[...], kvp_ref[...], kvc_ref[...]
        outs = []
        for hk in range(N_KV_HEADS):
            _, _, vcat, probs, _ = _attn_pieces(n, q, kvp, kvc, bias_ref, sinks_ref, hk)
            outs.append(_unstack_heads(_dot(probs.astype(BF16), vcat)))
        o_ref[...] = jnp.concatenate(outs, axis=1)

    return pl.pallas_call(
        body, name="attn_fwd", grid=(SEQ // BLOCK,),
        in_specs=_attn_in_specs(),
        out_specs=pl.BlockSpec((BLOCK, D_ATTN), lambda n: (n, 0)),
        out_shape=jax.ShapeDtypeStruct((SEQ, D_ATTN), F32),
        compiler_params=_cp(("parallel",)),
    )(proj, proj, proj, bias, sinks)


def _attn_bwd(proj, bias, sinks, dcat):
    nb = SEQ // BLOCK

    def body(q_ref, kvp_ref, kvc_ref, bias_ref, sinks_ref, do_ref, dq_ref, dkv_ref, dbias_ref, dsink_ref, dsacc):
        n = pl.program_id(0)

        @pl.when(n == 0)
        def _():
            dkv_ref[...] = jnp.zeros_like(dkv_ref)
            dbias_ref[...] = jnp.zeros_like(dbias_ref)
            dsacc[...] = jnp.zeros_like(dsacc)

        q, kvp, kvc = q_ref[...], kvp_ref[...], kvc_ref[...]
        do_all = do_ref[...]
        dqs, dks, dvs = [], [], []
        for hk in range(N_KV_HEADS):
            qs, kcat, vcat, probs, psink = _attn_pieces(n, q, kvp, kvc, bias_ref, sinks_ref, hk)
            q0 = hk * Q_PER_KV * HEAD_DIM
            do = jnp.concatenate([do_all[:, q0 + g * HEAD_DIM:q0 + (g + 1) * HEAD_DIM] for g in range(Q_PER_KV)],
                                 axis=0).astype(BF16)
            dprobs = _dot(do, vcat, NT)
            dvs.append(_dot(probs.astype(BF16), do, TN))
            rowdot = jnp.sum(probs * dprobs, axis=-1, keepdims=True)
            ds = probs * (dprobs - rowdot)
            dsacc[hk] += -psink * rowdot
            dbias_ref[hk] += ds
            dsb = (ds * (HEAD_DIM ** -0.5)).astype(BF16)
            dqs.append(_unstack_heads(_dot(dsb, kcat)))
            dks.append(_dot(dsb, qs, TN))
        dq_ref[...] = jnp.concatenate(dqs, axis=1)
        upd = jnp.concatenate(dks + dvs, axis=1)
        cur = pl.multiple_of(n * BLOCK, BLOCK)
        dkv_ref[pl.ds(cur, BLOCK), :] += upd[BLOCK:]

        @pl.when(n > 0)
        def _():
            prev = pl.multiple_of((n - 1) * BLOCK, BLOCK)
            dkv_ref[pl.ds(prev, BLOCK), :] += upd[:BLOCK]

        @pl.when(n == nb - 1)
        def _():
            for hk in range(N_KV_HEADS):
                for g in range(Q_PER_KV):
                    tot = jnp.sum(dsacc[hk, g * BLOCK:(g + 1) * BLOCK, :], axis=0, keepdims=True)
                    h = hk * Q_PER_KV + g
                    dsink_ref[h:h + 1, :] = jnp.broadcast_to(tot, (1, LANES))

    return pl.pallas_call(
        body, name="attn_bwd", grid=(nb,),
        in_specs=_attn_in_specs() + [pl.BlockSpec((BLOCK, D_ATTN), lambda n: (n, 0))],
        out_specs=[pl.BlockSpec((BLOCK, D_ATTN), lambda n: (n, 0)), _const((SEQ, 2 * D_KV)),
                   _const((N_KV_HEADS, Q_PER_KV * BLOCK, 2 * BLOCK)), _const((N_Q_HEADS, LANES))],
        out_shape=[jax.ShapeDtypeStruct((SEQ, D_ATTN), F32), jax.ShapeDtypeStruct((SEQ, 2 * D_KV), F32),
                   jax.ShapeDtypeStruct((N_KV_HEADS, Q_PER_KV * BLOCK, 2 * BLOCK), F32),
                   jax.ShapeDtypeStruct((N_Q_HEADS, LANES), F32)],
        scratch_shapes=[pltpu.VMEM((N_KV_HEADS, Q_PER_KV * BLOCK, 1), F32)],
        compiler_params=_cp(("arbitrary",)),
    )(proj, proj, proj, bias, sinks, dcat)


@jax.custom_vjp
def _head_sum(x):
    ones = _head_ones(LANES)
    return jnp.concatenate([_dot_ind(x[:, c:c + LANES], ones, 2) for c in range(0, x.shape[-1], LANES)], axis=1)


_head_sum.defvjp(lambda x: (_head_sum(x), None), lambda _, ct: (_head_sum(ct),))


@jax.custom_vjp
def _bdot(a, w):
    return _dot(a.astype(BF16), w.astype(BF16))


def _bdot_bwd(res, ct):
    a, w = res
    ctb = ct.astype(BF16)
    return _dot(ctb, w.astype(BF16), NT), _dot(a.astype(BF16), ctb, TN)


_bdot.defvjp(lambda a, w: (_bdot(a, w), (a, w)), _bdot_bwd)


def _sigmoid(x):
    return 0.5 * (jnp.tanh(0.5 * x) + 1.0)


def _softplus(x):
    return jnp.maximum(x, 0.0) + jnp.log(1.0 + jnp.exp(-jnp.abs(x)))


def _rwkv_core(r, k, v, zwa, zg, w0, wdu, a0, wiu, wgu, k_k, k_a):
    w_log = -_softplus(-(w0 + _bdot(jnp.tanh(zwa), wdu))) - 0.5
    decay = jnp.exp(-jnp.exp(w_log))
    a = _sigmoid(a0 + _bdot(zwa, wiu))
    g = _bdot(_sigmoid(zg), wgu)
    kk = k * k_k
    kk = kk / jnp.maximum(jnp.sqrt(_head_sum(kk * kk)), 1e-12)
    k2 = k * (1.0 + (a - 1.0) * k_a)
    return r, decay, k2, v, -kk, kk * a, g


def _rwkv_out(o, r, k2, v, g, lng, lnb, rk):
    mu = _head_sum(o) * (1.0 / HEAD_DIM)
    d = o - mu
    var = _head_sum(d * d) * (1.0 / HEAD_DIM)
    on = d * lax.rsqrt(var + GN_EPS) * lng + lnb
    bonus = _head_sum(r * k2 * rk) * v
    return (on + bonus) * g


P_SPLITS = (0, 512, 1024, 1536, 1664, 1792)
N_PREP_PARAMS = 7
HALO = 8


def _shifted_pieces(i, p_ref, halo_ref, mix_ref):
    p = p_ref[:, P_OFF:]
    prev_row = halo_ref[HALO - 1:HALO, P_OFF:] * jnp.where(i > 0, 1.0, 0.0)
    row = lax.broadcasted_iota(jnp.int32, p.shape, 0)
    pprev = jnp.where(row == 0, prev_row, pltpu.roll(p, 1, 0))
    delta = pprev - p
    ps = p + delta * mix_ref[...]
    return [ps[:, a:b] for a, b in zip(P_SPLITS[:-1], P_SPLITS[1:])], delta


def _prep_in_specs():
    return [_rows(TR, D_IN),
            pl.BlockSpec((HALO, D_IN), lambda i: (jnp.maximum(i * (TR // HALO) - 1, 0), 0)),
            _const((1, RWKV_COLS)), _const((1, D_RWKV)), _const((LANES, D_RWKV)), _const((1, D_RWKV)),
            _const((LANES, D_RWKV)), _const((LANES, D_RWKV)), _const((1, D_RWKV)), _const((1, D_RWKV))]


def _rwkv_prep(proj, mix, prm):
    def body(p_ref, halo_ref, mix_ref, *refs):
        prm_refs, outs = refs[:N_PREP_PARAMS], refs[N_PREP_PARAMS:]
        pieces, _ = _shifted_pieces(pl.program_id(0), p_ref, halo_ref, mix_ref)
        vals = _rwkv_core(*pieces, *[t[...] for t in prm_refs])
        for ref, val in zip(outs, vals):
            ref[...] = val

    return pl.pallas_call(
        body, name="rwkv_prep", grid=(SEQ // TR,),
        in_specs=_prep_in_specs(),
        out_specs=[_rows(TR, D_RWKV)] * 7,
        out_shape=[jax.ShapeDtypeStruct((SEQ, D_RWKV), F32)] * 7,
        compiler_params=_cp(("parallel",)),
    )(proj, proj, mix, *prm)


def _rwkv_prep_bwd(proj, mix, prm, cts):
    def body(p_ref, halo_ref, mix_ref, *refs):
        i = pl.program_id(0)
        prm_refs = refs[:N_PREP_PARAMS]
        ct_refs = refs[N_PREP_PARAMS:N_PREP_PARAMS + 10]
        dps_ref, dmix_ref = refs[N_PREP_PARAMS + 10:N_PREP_PARAMS + 12]
        dprm_refs = refs[N_PREP_PARAMS + 12:]
        pieces, delta = _shifted_pieces(i, p_ref, halo_ref, mix_ref)
        _, vjp = jax.vjp(_rwkv_core, *pieces, *[t[...] for t in prm_refs])
        dr1, dr2, dw, dk1, dk2, dv1, dv2, dkkn, db, dg = [t[...] for t in ct_refs]
        grads = vjp((dr1 + dr2, dw, dk1 + dk2, dv1 + dv2, dkkn, db, dg))
        dps = jnp.concatenate(grads[:5], axis=1)
        dps_ref[...] = dps

        @pl.when(i == 0)
        def _():
            dmix_ref[...] = jnp.zeros_like(dmix_ref)
            for ref in dprm_refs:
                ref[...] = jnp.zeros_like(ref)

        dmix_ref[...] += jnp.sum(dps * delta, axis=0, keepdims=True)
        for ref, gval in zip(dprm_refs, grads[5:]):
            ref[...] += gval

    prm_shapes = [(1, D_RWKV), (LANES, D_RWKV), (1, D_RWKV), (LANES, D_RWKV), (LANES, D_RWKV), (1, D_RWKV), (1, D_RWKV)]
    return pl.pallas_call(
        body, name="rwkv_prep_bwd", grid=(SEQ // TR,),
        in_specs=_prep_in_specs() + [_rows(TR, D_RWKV)] * 10,
        out_specs=[_rows(TR, RWKV_COLS), _const((1, RWKV_COLS))] + [_const(s) for s in prm_shapes],
        out_shape=[jax.ShapeDtypeStruct((SEQ, RWKV_COLS), F32), jax.ShapeDtypeStruct((1, RWKV_COLS), F32)]
        + [jax.ShapeDtypeStruct(s, F32) for s in prm_shapes],
        compiler_params=_cp(("arbitrary",)),
    )(proj, proj, mix, *prm, *cts)


def _rwkv_post(o, r, k2, v, g, lng, lnb, rk, attn):
    def body(o_ref, r_ref, k_ref, v_ref, g_ref, lng_ref, lnb_ref, rk_ref, attn_ref, cat_ref):
        rw = _rwkv_out(*[t[...] for t in (o_ref, r_ref, k_ref, v_ref, g_ref, lng_ref, lnb_ref, rk_ref)])
        cat_ref[...] = jnp.concatenate([attn_ref[...], rw], axis=1).astype(BF16)

    return pl.pallas_call(
        body, name="rwkv_post", grid=(SEQ // TR,),
        in_specs=[_rows(TR, D_RWKV)] * 5 + [_const((1, D_RWKV))] * 3 + [_rows(TR, D_ATTN)],
        out_specs=_rows(TR, D_MODEL),
        out_shape=jax.ShapeDtypeStruct((SEQ, D_MODEL), BF16),
        compiler_params=_cp(("parallel",)),
    )(o, r, k2, v, g, lng, lnb, rk, attn)


def _rwkv_post_bwd(o, r, k2, v, g, lng, lnb, rk, dcat):
    def body(o_ref, r_ref, k_ref, v_ref, g_ref, lng_ref, lnb_ref, rk_ref, dcat_ref,
             do_ref, dr_ref, dk_ref, dv_ref, dg_ref, dlng_ref, dlnb_ref, drk_ref):
        i = pl.program_id(0)
        args = [t[...] for t in (o_ref, r_ref, k_ref, v_ref, g_ref, lng_ref, lnb_ref, rk_ref)]
        _, vjp = jax.vjp(_rwkv_out, *args)
        grads = vjp(dcat_ref[:, D_ATTN:])
        for ref, gval in zip((do_ref, dr_ref, dk_ref, dv_ref, dg_ref), grads[:5]):
            ref[...] = gval

        @pl.when(i == 0)
        def _():
            for ref in (dlng_ref, dlnb_ref, drk_ref):
                ref[...] = jnp.zeros_like(ref)

        for ref, gval in zip((dlng_ref, dlnb_ref, drk_ref), grads[5:]):
            ref[...] += gval

    return pl.pallas_call(
        body, name="rwkv_post_bwd", grid=(SEQ // TR,),
        in_specs=[_rows(TR, D_RWKV)] * 5 + [_const((1, D_RWKV))] * 3 + [_rows(TR, D_MODEL)],
        out_specs=[_rows(TR, D_RWKV)] * 5 + [_const((1, D_RWKV))] * 3,
        out_shape=[jax.ShapeDtypeStruct((SEQ, D_RWKV), F32)] * 5 + [jax.ShapeDtypeStruct((1, D_RWKV), F32)] * 3,
        compiler_params=_cp(("arbitrary",)),
    )(o, r, k2, v, g, lng, lnb, rk, dcat)


def _assemble_dproj(dq, dkv, dps, mix):
    last = SEQ // HALO - 1

    def body(dq_ref, dkv_ref, dps_ref, nxt_ref, mix_ref, o_ref):
        i = pl.program_id(0)
        dps = dps_ref[...]
        mixv = mix_ref[...]
        nxt_row = nxt_ref[0:1, :] * jnp.where(i < SEQ // TR - 1, 1.0, 0.0)
        row = lax.broadcasted_iota(jnp.int32, dps.shape, 0)
        up = jnp.where(row == TR - 1, nxt_row, pltpu.roll(dps, TR - 1, 0))
        dp = dps * (1.0 - mixv) + up * mixv
        o_ref[...] = jnp.concatenate([dq_ref[...], dkv_ref[...], dp], axis=1).astype(BF16)

    return pl.pallas_call(
        body, name="assemble_dproj", grid=(SEQ // TR,),
        in_specs=[_rows(TR, D_ATTN), _rows(TR, 2 * D_KV), _rows(TR, RWKV_COLS),
                  pl.BlockSpec((HALO, RWKV_COLS), lambda i: (jnp.minimum((i + 1) * (TR // HALO), last), 0)),
                  _const((1, RWKV_COLS))],
        out_specs=_rows(TR, D_IN),
        out_shape=jax.ShapeDtypeStruct((SEQ, D_IN), BF16),
        compiler_params=_cp(("parallel",)),
    )(dq, dkv, dps, dps, mix)


N_PAIR = D_RWKV // LANES
CHUNK = 64
N_CHUNK = SEQ // CHUNK
GROUP = 8
STATE = (N_PAIR, HEAD_DIM, LANES)


def _lane_sums(lhs_tiles, ones2):
    out = _dot(jnp.concatenate(lhs_tiles, axis=0), ones2)
    return [out[i * HEAD_DIM:(i + 1) * HEAD_DIM] for i in range(len(lhs_tiles))]


def _seg_sum(xs, ones2):
    return _lane_sums([jnp.concatenate(_split(x, 2), axis=1) for x in xs], ones2)


def _seg_sum_rows(xs, ones2):
    out = _dot(jnp.concatenate(_split(jnp.concatenate(xs, axis=0), 2), axis=1), ones2)
    return [out[i * GROUP:(i + 1) * GROUP] for i in range(len(xs))]


def _col_form(rows, diag, ones2):
    zero = jnp.zeros((HEAD_DIM, LANES), BF16)
    tiles = []
    for row in rows:
        hi = row.astype(BF16)
        lo = (row - hi.astype(F32)).astype(BF16)
        tiles.append(jnp.concatenate(
            [jnp.where(diag, jnp.broadcast_to(part, (HEAD_DIM, LANES)), zero) for part in (hi, lo)], axis=1))
    return _lane_sums(tiles, ones2)


def _scan_consts():
    ones2 = jnp.concatenate([_head_ones(LANES)] * 2, axis=0)
    sub = lax.broadcasted_iota(jnp.int32, (HEAD_DIM, LANES), 0)
    lane_in_head = lax.broadcasted_iota(jnp.int32, (HEAD_DIM, LANES), 1) & (HEAD_DIM - 1)
    return ones2, lane_in_head == sub, lane_in_head


def _rows_of_columns(tile):
    t = tile.T
    return jnp.concatenate([t[:CHUNK], t[HEAD_DIM:HEAD_DIM + CHUNK]], axis=1)


def _pair(j):
    return slice(j * LANES, (j + 1) * LANES)


def _scan_fwd(r, w, k, v, kkn, b):
    def body(r_ref, w_ref, k_ref, v_ref, kkn_ref, b_ref, o_ref, st_ref, sa_ref, s_scr):
        c = pl.program_id(0)
        ones2, diag, lane_in_head = _scan_consts()

        @pl.when(c == 0)
        def _():
            s_scr[...] = jnp.zeros_like(s_scr)

        def group(gi, carry):
            row0 = pl.multiple_of(gi * GROUP, GROUP)
            states, ocols = list(carry[:N_PAIR]), list(carry[N_PAIR:])
            tiles = [[t[pl.ds(row0, GROUP), _pair(j)] for t in (r_ref, w_ref, k_ref, v_ref, kkn_ref, b_ref)]
                     for j in range(N_PAIR)]
            def row(j, name, u):
                return tiles[j]["rwkvnb".index(name)][u:u + 1]

            def emit_out(u, after):
                outs = _seg_sum([s[j] * row(j, "r", u + d) for d, s in enumerate(after) for j in range(N_PAIR)], ones2)
                for d in range(2):
                    here = lane_in_head == gi * GROUP + u + d
                    for j in range(N_PAIR):
                        ocols[j] = jnp.where(here, outs[d * N_PAIR + j], ocols[j])

            def vcols_of(u):
                cols = _col_form([row(j, "v", u + d) for d in range(2) for j in range(N_PAIR)], diag, ones2)
                return cols[:N_PAIR], cols[N_PAIR:]

            n_next = [pltpu.roll(tiles[j][4], GROUP - 1, 0) for j in range(N_PAIR)]
            dots = _seg_sum_rows([tiles[j][5] * n_next[j] for j in range(N_PAIR)]
                                 + [tiles[j][2] * n_next[j] for j in range(N_PAIR)], ones2)
            b_n, k_n = dots[:N_PAIR], dots[N_PAIR:]
            w_n = [tiles[j][1] * n_next[j] for j in range(N_PAIR)]

            vcols = vcols_of(0)
            after = None
            for u in range(0, GROUP, 2):
                prods = _seg_sum([states[j] * row(j, "n", u) for j in range(N_PAIR)]
                                 + [states[j] * w_n[j][u:u + 1] for j in range(N_PAIR)], ones2)
                if after is not None:
                    emit_out(u - 2, after)
                nxt = vcols_of(u + 2) if u + 2 < GROUP else None
                first, second = [], []
                for j in range(N_PAIR):
                    sa1 = prods[j]
                    sa2 = prods[N_PAIR + j] + sa1 * b_n[j][u:u + 1] + vcols[0][j] * k_n[j][u:u + 1]
                    s1 = states[j] * row(j, "w", u) + sa1 * row(j, "b", u) + vcols[0][j] * row(j, "k", u)
                    s2 = s1 * row(j, "w", u + 1) + sa2 * row(j, "b", u + 1) + vcols[1][j] * row(j, "k", u + 1)
                    st_ref[row0 + u, j] = s1
                    sa_ref[row0 + u, j] = sa1
                    st_ref[row0 + u + 1, j] = s2
                    sa_ref[row0 + u + 1, j] = sa2
                    first.append(s1)
                    second.append(s2)
                    states[j] = s2
                after, vcols = (first, second), nxt
            emit_out(GROUP - 2, after)
            return tuple(states + ocols)

        zero = jnp.zeros((HEAD_DIM, LANES), F32)
        fin = lax.fori_loop(0, CHUNK // GROUP, group, tuple(s_scr[j] for j in range(N_PAIR)) + (zero,) * N_PAIR)
        for j in range(N_PAIR):
            s_scr[j] = fin[j]
            o_ref[:, _pair(j)] = _rows_of_columns(fin[N_PAIR + j])

    blk = pl.BlockSpec((CHUNK, D_RWKV), lambda c: (c, 0))
    per_step = pl.BlockSpec((CHUNK,) + STATE, lambda c: (c, 0, 0, 0))
    return pl.pallas_call(
        body, name="rwkv_scan_fwd", grid=(N_CHUNK,),
        in_specs=[blk] * 6,
        out_specs=[blk, per_step, per_step],
        out_shape=[jax.ShapeDtypeStruct((SEQ, D_RWKV), F32)] + [jax.ShapeDtypeStruct((SEQ,) + STATE, F32)] * 2,
        scratch_shapes=[pltpu.VMEM(STATE, F32)],
        compiler_params=_cp(("arbitrary",)),
    )(r, w, k, v, kkn, b)


def _scan_bwd(r, w, k, v, kkn, b, do, states, sas, ds_in, prev, name, first_chunk, n_chunks):
    top = first_chunk + n_chunks - 1

    def body(r_ref, w_ref, k_ref, v_ref, kkn_ref, b_ref, do_ref, st_ref, before_ref, sa_ref, ds_in_ref, *rest):
        dr_ref, dw_ref, dk_ref, dv_ref, dkkn_ref, db_ref, ds_out_ref, ds_scr = rest[-8:]
        i = pl.program_id(0)
        ones2, diag, lane_in_head = _scan_consts()

        @pl.when(i == 0)
        def _():
            ds_scr[...] = ds_in_ref[...]

        entry = [before_ref[0, j] * jnp.where(i < top, 1.0, 0.0) for j in range(N_PAIR)]

        def reverse(gr, carry):
            gi = CHUNK // GROUP - 1 - gr
            row0 = pl.multiple_of(gi * GROUP, GROUP)
            dstates, dvcols = list(carry[:N_PAIR]), list(carry[N_PAIR:])
            tiles = [[t[pl.ds(row0, GROUP), _pair(j)]
                      for t in (r_ref, w_ref, k_ref, v_ref, kkn_ref, b_ref, do_ref)] for j in range(N_PAIR)]
            rows = [[[None] * GROUP for _ in range(5)] for _ in range(N_PAIR)]

            def row(j, name, u):
                return tiles[j]["rwkvnbd".index(name)][u:u + 1]

            def cols_of(u):
                cols = _col_form([row(j, name, u - d) for d in range(2) for name in "dv" for j in range(N_PAIR)],
                                 diag, ones2)
                return [[(cols[(2 * d) * N_PAIR + j], cols[(2 * d + 1) * N_PAIR + j]) for j in range(N_PAIR)]
                        for d in range(2)]

            def emit_dv(u, dsps):
                outs = _seg_sum([dsp[j] * row(j, "k", u - d) for d, dsp in enumerate(dsps) for j in range(N_PAIR)], ones2)
                for d in range(2):
                    here = lane_in_head == gi * GROUP + u - d
                    for j in range(N_PAIR):
                        dvcols[j] = jnp.where(here, outs[d * N_PAIR + j], dvcols[j])

            b_prev = [pltpu.roll(tiles[j][5], 1, 0) for j in range(N_PAIR)]
            dots = _seg_sum_rows([tiles[j][4] * b_prev[j] for j in range(N_PAIR)]
                                 + [tiles[j][0] * tiles[j][5] for j in range(N_PAIR)], ones2)
            n_b, r_b = dots[:N_PAIR], dots[N_PAIR:]
            w_b = [tiles[j][1] * b_prev[j] for j in range(N_PAIR)]

            def outputs(u, j, dsp, dsa, docol, vcol):
                tl = gi * GROUP + u
                if u > 0:
                    s_prev = st_ref[tl - 1, j]
                else:
                    s_prev = jnp.where(gi == 0, entry[j], st_ref[jnp.maximum(tl - 1, 0), j])
                rows[j][0][u] = jnp.sum(st_ref[tl, j] * docol, axis=0, keepdims=True)
                rows[j][1][u] = jnp.sum(dsp * s_prev, axis=0, keepdims=True)
                rows[j][2][u] = jnp.sum(dsp * vcol, axis=0, keepdims=True)
                rows[j][3][u] = jnp.sum(s_prev * dsa, axis=0, keepdims=True)
                rows[j][4][u] = jnp.sum(dsp * sa_ref[tl, j], axis=0, keepdims=True)

            cols = cols_of(GROUP - 1)
            before = None
            for u in range(GROUP - 1, 0, -2):
                dsp1 = [dstates[j] + cols[0][j][0] * row(j, "r", u) for j in range(N_PAIR)]
                prods = _seg_sum([dsp1[j] * row(j, "b", u) for j in range(N_PAIR)]
                                 + [dsp1[j] * w_b[j][u:u + 1] for j in range(N_PAIR)], ones2)
                if before is not None:
                    emit_dv(u + 2, before)
                nxt = cols_of(u - 2) if u >= 2 else None
                dsp2 = []
                for j in range(N_PAIR):
                    dsa1 = prods[j]
                    dsa2 = prods[N_PAIR + j] + dsa1 * n_b[j][u:u + 1] + cols[1][j][0] * r_b[j][u - 1:u]
                    mid = dsp1[j] * row(j, "w", u) + dsa1 * row(j, "n", u) + cols[1][j][0] * row(j, "r", u - 1)
                    outputs(u, j, dsp1[j], dsa1, *cols[0][j])
                    outputs(u - 1, j, mid, dsa2, *cols[1][j])
                    dstates[j] = mid * row(j, "w", u - 1) + dsa2 * row(j, "n", u - 1)
                    dsp2.append(mid)
                before, cols = (dsp1, dsp2), nxt
            emit_dv(1, before)
            for j in range(N_PAIR):
                for ref, rr in zip((dr_ref, dw_ref, dk_ref, dkkn_ref, db_ref), rows[j]):
                    ref[pl.ds(row0, GROUP), _pair(j)] = jnp.concatenate(rr, axis=0)
            return tuple(dstates + dvcols)

        zero = jnp.zeros((HEAD_DIM, LANES), F32)
        dfin = lax.fori_loop(0, CHUNK // GROUP, reverse, tuple(ds_scr[j] for j in range(N_PAIR)) + (zero,) * N_PAIR)
        for j in range(N_PAIR):
            ds_scr[j] = dfin[j]
            dv_ref[:, _pair(j)] = _rows_of_columns(dfin[N_PAIR + j])

        @pl.when(i == n_chunks - 1)
        def _():
            ds_out_ref[...] = ds_scr[...]

    blk = pl.BlockSpec((CHUNK, D_RWKV), lambda i: (top - i, 0))
    per_step = pl.BlockSpec((CHUNK,) + STATE, lambda i: (top - i, 0, 0, 0))
    step_before = pl.BlockSpec((1,) + STATE, lambda i: (jnp.maximum((top - i) * CHUNK - 1, 0), 0, 0, 0))
    prev = [] if prev is None else list(prev)
    outs = pl.pallas_call(
        body, name=name, grid=(n_chunks,),
        in_specs=[blk] * 7 + [per_step, step_before, per_step, _const(STATE)] + [ANY] * len(prev),
        out_specs=[blk] * 6 + [_const(STATE)],
        out_shape=[jax.ShapeDtypeStruct((SEQ, D_RWKV), F32)] * 6 + [jax.ShapeDtypeStruct(STATE, F32)],
        scratch_shapes=[pltpu.VMEM(STATE, F32)],
        input_output_aliases={11 + t: t for t in range(len(prev))},
        compiler_params=_cp(("arbitrary",)),
    )(r, w, k, v, kkn, b, do, states, states, sas, ds_in, *prev)
    return outs[:6], outs[6]


def _stacked(rows, cols, pick):
    return pl.BlockSpec((None, rows, cols), pick)


def _local_step(x, target, sm, win_st):
    def tied(t, token):
        return t if token is None else t + token[0:1, 0:1].reshape((1,) * t.ndim)

    zpad = jnp.zeros((LORA_DECAY, D_RWKV), F32)
    prm = [sm["w0"], jnp.concatenate([sm["w_decay_up"], zpad], axis=0), sm["a0"],
           jnp.concatenate([zpad, sm["w_iclr_up"]], axis=0), sm["w_gate_up"], sm["k_k"], sm["k_a"]]
    mix = sm["rwkv_shift_mix"]
    onehot = jnp.asarray(_t5_onehot(), BF16)
    sinks = sm["sinks"].reshape(N_Q_HEADS)
    lng, lnb, rk = sm["ln_x_g"], sm["ln_x_b"], sm["r_k"].reshape(1, D_RWKV)

    h1 = _norm_cast(x, sm["norm_mix_pre"], "norm_in")
    proj = _matmul(h1, win_st, "nn", "proj", m=SEQ, n=D_IN, k=D_MODEL, tm=SEQ, tn=640,
                   b_spec=_stacked(D_MODEL, 640, lambda i, j: (j, 0, 0)))
    bias = _bias_table(sm["rel_bias"].T, onehot).reshape(N_KV_HEADS, Q_PER_KV * BLOCK, 2 * BLOCK)
    attn = _attn_fwd(proj, bias, sinks)
    r, w, k2, v, kkn, b, g = _rwkv_prep(proj, mix, prm)
    o, states, sas = _scan_fwd(r, w, k2, v, kkn, b)
    wout, wup_st, wdown = yield ("rest_weights", o)
    cat = _rwkv_post(o, r, k2, v, g, lng, lnb, rk, attn)
    mixo = _matmul(cat, wout, "nn", "out_proj", m=SEQ, n=D_MODEL, k=D_MODEL, tm=SEQ, tn=512)
    x2, h3 = _mix_norm(x, mixo, sm["norm_mix_post"], sm["norm_ffn_pre"])
    u_gate, u_val, gate, val, act = _ffn_up_act(h3, wup_st, sm["conv_w"], sm["conv_b"])
    f = _matmul(act, wdown, "nn", "ffn_down", m=SEQ, n=D_MODEL, k=D_FF, tm=1024, tn=512)
    loss, dy, df, d_g4 = _loss_head(x2, f, sm["norm_ffn_post"], target)

    d_wdown = _matmul(act, df, "tn", "d_wdown", m=D_FF, n=D_MODEL, k=SEQ, tm=512, tn=D_MODEL)
    du, d_convw, d_convb = _ffn_act_bwd(u_gate, u_val, gate, val, df, wdown, sm["conv_w"])
    d_convw = d_convw.transpose(1, 0, 2).reshape(3, 2 * D_FF)
    d_convb = d_convb.reshape(1, 2 * D_FF)
    dh3 = _matmul_nt_shards(du, wup_st, "d_h3", m=SEQ, n=D_MODEL, tm=512, tn=512,
                            a_spec=pl.BlockSpec((2, 512, D_FF), lambda i, j: (0, i, 0)),
                            a_piece=lambda ref, s: ref[s // 2, :, (s % 2) * 2048:(s % 2 + 1) * 2048])
    d_wup = _matmul(h3, du, "tn", "d_wup", m=D_MODEL, n=2 * D_FF, k=SEQ, tm=D_MODEL, tn=512,
                    b_spec=pl.BlockSpec((None, SEQ, 512), lambda i, j: (j // 8, 0, j % 8)),
                    out=((N_CHIPS, D_MODEL, 2048), _stacked(D_MODEL, 512, lambda i, j: (j // 4, 0, j % 4))))
    dx2, dmix, d_g2, d_g3 = _mid_bwd(x2, mixo, dy, dh3, sm["norm_mix_post"], sm["norm_ffn_pre"])
    dcat = _matmul(dmix, wout, "nt", "d_cat", m=SEQ, n=D_MODEL, k=D_MODEL, tm=SEQ, tn=512)
    d_wout = _matmul(cat, dmix, "tn", "d_wout", m=D_MODEL, n=D_MODEL, k=SEQ, tm=512, tn=D_MODEL)
    token = yield ("grads_a", (d_wdown, d_wup, d_wout))
    do, dr_p, dk_p, dv_p, dg, d_lng, d_lnb, d_rk = _rwkv_post_bwd(o, r, k2, v, g, lng, tied(lnb, token), rk, dcat)
    half = N_CHUNK // 2
    ds_end = jnp.zeros(STATE, F32)
    late, ds_mid = _scan_bwd(r, w, k2, v, kkn, b, do, states, sas, ds_end, None, "rwkv_scan_bwd_late", half, half)
    token = yield ("seam_1", ds_mid)
    scan_cts, ds_first = _scan_bwd(r, w, k2, v, kkn, b, do, states, sas, tied(ds_mid, token), late,
                                   "rwkv_scan_bwd_early", 0, half)
    dr_s, dw_s, dk_s, dv_s, dkkn_s, db_s = scan_cts
    token = yield ("seam_2", ds_first)
    prep_grads = _rwkv_prep_bwd(proj, tied(mix, token), prm,
                                (dr_s, dr_p, dw_s, dk_s, dk_p, dv_s, dv_p, dkkn_s, db_s, dg))
    dps, d_mix, d_w0, d_wdu, d_a0, d_wiu, d_wgu, d_kk, d_ka = prep_grads
    dq, dkv, dbias, dsink = _attn_bwd(proj, bias, sinks, dcat)
    d_relb = _bias_table_bwd(dbias.reshape(N_Q_HEADS, N_REL), onehot).T
    dproj = _assemble_dproj(dq, dkv, dps, mix)
    d_win = _matmul(h1, dproj, "tn", "d_win", m=D_MODEL, n=D_IN, k=SEQ, tm=D_MODEL, tn=640,
                    out=((N_CHIPS, D_MODEL, 640), _stacked(D_MODEL, 640, lambda i, j: (j, 0, 0))))
    token = yield ("grads_b", d_win)
    dh1 = _matmul_nt_shards(dproj, win_st, "d_h1", m=SEQ, n=D_MODEL, tm=1024, tn=D_MODEL,
                            a_spec=pl.BlockSpec((1024, D_IN), lambda i, j: (i, 0)),
                            a_piece=lambda ref, s: ref[:, s * 640:(s + 1) * 640])
    grad_x, d_g1 = _first_bwd(x, dx2, dh1, tied(sm["norm_mix_pre"], token))

    grads = {
        "norm_mix_pre": d_g1, "norm_mix_post": d_g2, "norm_ffn_pre": d_g3, "norm_ffn_post": d_g4,
        "w_in": d_win, "rel_bias": d_relb, "sinks": dsink[:, 0].reshape(1, N_Q_HEADS),
        "rwkv_shift_mix": d_mix, "w0": d_w0, "w_decay_up": d_wdu[:LORA_DECAY], "a0": d_a0,
        "w_iclr_up": d_wiu[LORA_DECAY:], "w_gate_up": d_wgu, "k_k": d_kk, "k_a": d_ka,
        "r_k": d_rk.reshape(1, N_Q_HEADS, HEAD_DIM), "ln_x_g": d_lng, "ln_x_b": d_lnb,
        "w_out": d_wout, "w_ffn_up": d_wup, "conv_w": d_convw, "conv_b": d_convb, "w_ffn_down": d_wdown,
    }
    return loss, grad_x, grads


def _place():
    x, y, c = lax.axis_index("x"), lax.axis_index("y"), lax.axis_index("c")
    chips = [(1 - x, y), (x, 1 - y), (1 - x, 1 - y)]
    return x, y, c, chips


def _remote(src, dst, sems, idx, to):
    return pltpu.make_async_remote_copy(src_ref=src, dst_ref=dst, send_sem=sems[0].at[idx], recv_sem=sems[1].at[idx],
                                        device_id=to, device_id_type=MESH)


ROW_ALIGN = 16


def _half(c, rows):
    return pl.ds(pl.multiple_of(c * (rows // 2), ROW_ALIGN), rows // 2)


def _gather_weights(big, small):
    nb, ns = len(big), len(small)

    def body(*refs):
        ins, outs = refs[:nb + ns], refs[nb + ns:2 * (nb + ns)]
        ici, d2d, sml, loc = refs[2 * (nb + ns):2 * (nb + ns) + 2], refs[-5:-3], refs[-3:-1], refs[-1]
        x, y, c, chips = _place()
        me = 2 * x + y
        sib = (x, y, 1 - c)
        local = [pltpu.make_async_copy(ins[a], outs[a].at[me], loc.at[a]) for a in range(nb + ns)]
        for cp in local:
            cp.start()
        sends = []
        for a in range(nb):
            rows = _half(c, big[a].shape[0])
            for kk, chip in enumerate(chips):
                sends.append(_remote(ins[a].at[rows], outs[a].at[me, rows], ici, a * 3 + kk, (*chip, c)))
        for a in range(ns):
            for kk, chip in enumerate(chips):
                sends.append(_remote(ins[nb + a], outs[nb + a].at[me], sml, a * 3 + kk, (*chip, c)))
        for cp in sends:
            cp.start()
        passed = []
        for a in range(nb):
            rows = _half(c, big[a].shape[0])
            for kk, (px, py) in enumerate(chips):
                got = outs[a].at[2 * px + py, rows]
                _remote(got, got, ici, a * 3 + kk, sib).wait_recv()
                fwd = _remote(got, got, d2d, a * 3 + kk, sib)
                fwd.start()
                passed.append(fwd)
        for a in range(nb):
            other = _half(1 - c, big[a].shape[0])
            for kk, (px, py) in enumerate(chips):
                land = outs[a].at[2 * px + py, other]
                _remote(land, land, d2d, a * 3 + kk, sib).wait_recv()
        for a in range(ns):
            for kk, (px, py) in enumerate(chips):
                land = outs[nb + a].at[2 * px + py]
                _remote(land, land, sml, a * 3 + kk, sib).wait_recv()
        for cp in sends + passed:
            cp.wait_send()
        for cp in local:
            cp.wait()

    arrs = list(big) + list(small)
    in_vmem = pl.BlockSpec(memory_space=pltpu.VMEM)
    return pl.pallas_call(
        body, name="gather_weights",
        in_specs=[in_vmem] * len(arrs), out_specs=[in_vmem] * len(arrs),
        out_shape=[jax.ShapeDtypeStruct((N_CHIPS,) + t.shape, t.dtype) for t in arrs],
        scratch_shapes=[pltpu.SemaphoreType.DMA((3 * nb,)), pltpu.SemaphoreType.DMA((3 * nb,)),
                        pltpu.SemaphoreType.DMA((3 * nb,)), pltpu.SemaphoreType.DMA((3 * nb,)),
                        pltpu.SemaphoreType.DMA((3 * ns,)), pltpu.SemaphoreType.DMA((3 * ns,)),
                        pltpu.SemaphoreType.DMA((nb + ns,))],
        compiler_params=pltpu.CompilerParams(has_side_effects=True, vmem_limit_bytes=VMEM_LIMIT),
    )(*arrs)


HBM = pl.BlockSpec(memory_space=pltpu.HBM)
SEM = pl.BlockSpec(memory_space=pltpu.SEMAPHORE)
EFFECT = pltpu.SideEffectType.DATAFLOW_SIDE_EFFECTING


def _copies_start(name, bufs, plan, n, partners=None):
    nb = len(bufs)

    def body(*refs):
        ins, sems, token = refs[:nb], refs[nb:nb + 2 * n], refs[-1]
        if partners is not None:
            barrier = pltpu.get_barrier_semaphore()
            peers = partners[1]()
            for peer in peers:
                pl.semaphore_signal(barrier, inc=1, device_id=peer, device_id_type=MESH)
            pl.semaphore_wait(barrier, len(peers))
        for kk, (src, dst, dev) in enumerate(plan(ins)):
            pltpu.make_async_remote_copy(src_ref=src, dst_ref=dst, send_sem=sems[2 * kk], recv_sem=sems[2 * kk + 1],
                                         device_id=dev, device_id_type=MESH).start()
        token[...] = jnp.zeros_like(token)

    outs = pl.pallas_call(
        body, name=name,
        out_shape=tuple([pltpu.SemaphoreType.DMA(())] * (2 * n) + [pltpu.HBM(t.shape, t.dtype) for t in bufs]
                        + [jax.ShapeDtypeStruct((8, LANES), F32)]),
        in_specs=[HBM] * nb,
        out_specs=tuple([SEM] * (2 * n) + [HBM] * nb + [pl.BlockSpec(memory_space=pltpu.VMEM)]),
        input_output_aliases={t: 2 * n + t for t in range(nb)},
        compiler_params=pltpu.CompilerParams(has_side_effects=EFFECT,
                                             collective_id=None if partners is None else partners[0]),
    )(*[pltpu.with_memory_space_constraint(t, pltpu.HBM) for t in bufs])
    return outs[:2 * n], outs[2 * n:2 * n + nb], outs[-1]


def _copies_wait(name, sems, bufs, plan, n, after):
    nb = len(bufs)
    after = list(after) if isinstance(after, (list, tuple)) else [after]

    def body(*refs):
        ins, sem_refs = refs[:nb], refs[nb:nb + 2 * n]
        for kk, (src, dst, dev) in enumerate(plan(ins)):
            cp = pltpu.make_async_remote_copy(src_ref=src, dst_ref=dst, send_sem=sem_refs[2 * kk],
                                              recv_sem=sem_refs[2 * kk + 1], device_id=dev, device_id_type=MESH)
            cp.wait_send()
            cp.wait_recv()

    return pl.pallas_call(
        body, name=name,
        out_shape=tuple(pltpu.HBM(t.shape, t.dtype) for t in bufs),
        in_specs=[HBM] * nb + [SEM] * (2 * n) + [ANY] * len(after),
        out_specs=tuple([HBM] * nb),
        input_output_aliases={t: t for t in range(nb)},
        compiler_params=pltpu.CompilerParams(has_side_effects=EFFECT),
    )(*bufs, *sems, *after)


def _plan_gather(n_w):
    def plan(refs):
        x, y, c, chips = _place()
        me = 2 * x + y
        return [(refs[a], refs[n_w + a].at[me], (*chip, c)) for a in range(n_w) for chip in chips + [(x, y)]]
    return plan


def _plan_pair_halves(n_g, rows):
    def plan(refs):
        x, y, c, _ = _place()
        return [(refs[a].at[:, _half(1 - c, rows[a])], refs[n_g + a], (x, y, 1 - c)) for a in range(n_g)]
    return plan


def _plan_chip_parts(n_g):
    def plan(refs):
        x, y, c, chips = _place()
        me = 2 * x + y
        return [(refs[a].at[2 * px + py], refs[n_g + a].at[me], (px, py, c))
                for a in range(n_g) for (px, py) in chips]
    return plan


def _plan_pair_fill(n_g, rows):
    def plan(refs):
        x, y, c, _ = _place()
        return [(refs[a].at[_half(c, rows[a])], refs[a].at[_half(c, rows[a])], (x, y, 1 - c)) for a in range(n_g)]
    return plan


def _pair_add(g, got, name):
    _, rows, cols = g.shape
    hr = rows // 2
    tr = min(hr, 256)
    nb = hr // tr

    def body(g_ref, got_ref, p_ref, own_ref):
        val = (g_ref[...] + got_ref[...]).astype(BF16)
        p_ref[...] = val

        @pl.when(pl.program_id(1) == 2 * lax.axis_index("x") + lax.axis_index("y"))
        def _():
            own_ref[...] = val

    def mine(i, s):
        return (2 * lax.axis_index("x") + lax.axis_index("y"), i, 0)

    return pl.pallas_call(
        body, name=name, grid=(nb, N_CHIPS),
        in_specs=[pl.BlockSpec((None, tr, cols), lambda i, s: (s, lax.axis_index("c") * nb + i, 0)),
                  pl.BlockSpec((None, tr, cols), lambda i, s: (s, i, 0))],
        out_specs=[pl.BlockSpec((None, tr, cols), lambda i, s: (s, i, 0)), pl.BlockSpec((None, tr, cols), mine)],
        out_shape=[jax.ShapeDtypeStruct((N_CHIPS, hr, cols), BF16)] * 2,
        compiler_params=_cp(("parallel", "arbitrary")),
    )(g, got)


def _chip_sum(parts, name):
    _, hr, cols = parts.shape
    tr = min(hr, 128)
    nb = hr // tr

    def body(t_ref, o_ref):
        part = [t_ref[s].astype(F32) for s in range(N_CHIPS)]
        o_ref[...] = ((part[0] + part[1]) + part[2]) + part[3]

    return pl.pallas_call(
        body, name=name, grid=(nb,),
        in_specs=[pl.BlockSpec((N_CHIPS, tr, cols), lambda i: (0, i, 0))],
        out_specs=pl.BlockSpec((tr, cols), lambda i: (lax.axis_index("c") * nb + i, 0)),
        out_shape=jax.ShapeDtypeStruct((2 * hr, cols), F32),
        compiler_params=_cp(("parallel",)),
    )(parts)


class _Reduction:
    def __init__(self, tag, rows, first_id):
        self.tag, self.n, self.rows, self.first_id = tag, len(rows), rows, first_id
        self.plans = (_plan_pair_halves(self.n, rows), _plan_chip_parts(self.n), _plan_pair_fill(self.n, rows))
        self.flight = None

    def _name(self, what):
        return f"grad_{self.tag}_{what}"

    @staticmethod
    def _sibling():
        x, y, c, _ = _place()
        return [(x, y, 1 - c)]

    @staticmethod
    def _same_core_elsewhere():
        x, y, c, chips = _place()
        return [(*chip, c) for chip in chips]

    def start(self, gs):
        gots = [lax.empty((N_CHIPS, t.shape[1] // 2, t.shape[2]), F32) for t in gs]
        self.flight = _copies_start(self._name("pair_start"), list(gs) + gots, self.plans[0], self.n,
                                    (self.first_id, self._sibling))
        return self.flight[2]

    def after_pair(self, after):
        sems, bufs, _ = self.flight
        out = _copies_wait(self._name("pair_wait"), sems, bufs, self.plans[0], self.n, after)
        sums = [_pair_add(g, got, self._name(f"pair_add_{i}"))
                for i, (g, got) in enumerate(zip(out[:self.n], out[self.n:]))]
        self.flight = _copies_start(self._name("chip_start"), [p for p, _ in sums] + [own for _, own in sums],
                                    self.plans[1], 3 * self.n, (self.first_id + 1, self._same_core_elsewhere))
        return self.flight[2]

    def after_chips(self, after):
        sems, bufs, _ = self.flight
        out = _copies_wait(self._name("chip_wait"), sems, bufs, self.plans[1], 3 * self.n, after)
        fulls = [_chip_sum(t, self._name(f"chip_sum_{i}")) for i, t in enumerate(out[self.n:])]
        self.flight = _copies_start(self._name("fill_start"), fulls, self.plans[2], self.n,
                                    (self.first_id + 2, self._sibling))
        return self.flight[2]

    def finish(self, after):
        sems, bufs, _ = self.flight
        return _copies_wait(self._name("fill_wait"), sems, bufs, self.plans[2], self.n, after)


def _adamw_math(w, g, m, v):
    nm = ADAM_B1 * m + (1.0 - ADAM_B1) * g
    nv = ADAM_B2 * v + (1.0 - ADAM_B2) * (g * g)
    m_hat = nm / (1.0 - ADAM_B1 ** ADAM_STEP)
    v_hat = nv / (1.0 - ADAM_B2 ** ADAM_STEP)
    return -ADAM_LR * (m_hat / (jnp.sqrt(v_hat) + ADAM_EPS) + ADAM_WD * w), nm, nv


def _adamw(w, g, m, v, name, tr):
    r, cdim = w.shape

    def body(w_ref, g_ref, m_ref, v_ref, d_ref, nm_ref, nv_ref):
        d_ref[...], nm_ref[...], nv_ref[...] = _adamw_math(w_ref[...], g_ref[...], m_ref[...], v_ref[...])

    return pl.pallas_call(
        body, name=name, grid=(r // tr,), in_specs=[_rows(tr, cdim)] * 4, out_specs=[_rows(tr, cdim)] * 3,
        out_shape=[jax.ShapeDtypeStruct((r, cdim), F32)] * 3, compiler_params=_cp(("parallel",)),
    )(w, g, m, v)


def _adamw_small(w, parts, m, v, shapes):
    n_rows = w.shape[0]

    def scatter(src, outs):
        row = 0
        for (rows, cols), out in zip(shapes, outs):
            if cols == LANES:
                out[...] = src[row:row + rows, :]
            elif cols > LANES:
                per = cols // LANES
                for r in range(rows):
                    for cb in range(per):
                        out[r:r + 1, cb * LANES:(cb + 1) * LANES] = src[row + r * per + cb:row + r * per + cb + 1, :]
            else:
                per = LANES // cols
                for r in range(rows):
                    out[r:r + 1, :] = src[row + r // per:row + r // per + 1, (r % per) * cols:(r % per + 1) * cols]
            row += -(-rows * cols // LANES)

    def body(w_ref, p_ref, m_ref, v_ref, *rest):
        outs, scr = rest[:-4], rest[-4:]
        g = p_ref[0]
        for dev in range(1, N_DEV):
            g = g + p_ref[dev]
        scr[3][...] = g
        scr[0][...], scr[1][...], scr[2][...] = _adamw_math(w_ref[...], g, m_ref[...], v_ref[...])
        n = len(shapes)
        for kind in range(4):
            scatter(scr[kind], outs[kind * n:(kind + 1) * n])

    outs = pl.pallas_call(
        body, name="adamw_small", grid=(1,),
        in_specs=[_const(w.shape), _const(parts.shape), _const(w.shape), _const(w.shape)],
        out_specs=[_const(s) for s in shapes] * 4, out_shape=[jax.ShapeDtypeStruct(s, F32) for s in shapes] * 4,
        scratch_shapes=[pltpu.VMEM((n_rows, LANES), F32)] * 4,
        compiler_params=_cp(("arbitrary",)),
    )(w, parts, m, v)
    n = len(shapes)
    return [outs[kind * n:(kind + 1) * n] for kind in range(4)]


REPLICATED = (("norm_mix_pre", 1024), ("norm_mix_post", 1024), ("norm_ffn_pre", 1024), ("norm_ffn_post", 1024),
              ("rel_bias", 256), ("sinks", 8), ("rwkv_shift_mix", 1792), ("w0", 512), ("a0", 512), ("k_k", 512),
              ("k_a", 512), ("r_k", 512), ("ln_x_g", 512), ("ln_x_b", 512), ("conv_b", 8192))
SMALL_SHARDED = (("w_decay_up", LORA_DECAY, D_RWKV), ("w_iclr_up", LORA_ICLR, D_RWKV),
                 ("w_gate_up", LORA_GATE, D_RWKV), ("conv_w", 3, 2 * D_FF))
BIG = (("w_in", D_MODEL, 640), ("w_out", 256, D_MODEL), ("w_ffn_up", D_MODEL, 2048), ("w_ffn_down", 1024, D_MODEL))
PACK_ALIGN = 8 * LANES


def _pack(pieces):
    flat = []
    for t in pieces:
        t = t.reshape(-1)
        pad = (-t.shape[0]) % LANES
        flat.append(jnp.pad(t, (0, pad)) if pad else t)
    flat = jnp.concatenate(flat)
    pad = (-flat.shape[0]) % PACK_ALIGN
    return jnp.pad(flat, (0, pad)).reshape(-1, LANES)


def kernel(x, norm_mix_pre, norm_mix_post, norm_ffn_pre, norm_ffn_post, w_in, rel_bias, sinks, rwkv_shift_mix, w0, w_decay_up, a0, w_iclr_up, w_gate_up, k_k, k_a, r_k, ln_x_g, ln_x_b, w_out, w_ffn_up, conv_w, conv_b, w_ffn_down, loss_target, m_norm_mix_pre, m_norm_mix_post, m_norm_ffn_pre, m_norm_ffn_post, m_w_in, m_rel_bias, m_sinks, m_rwkv_shift_mix, m_w0, m_w_decay_up, m_a0, m_w_iclr_up, m_w_gate_up, m_k_k, m_k_a, m_r_k, m_ln_x_g, m_ln_x_b, m_w_out, m_w_ffn_up, m_conv_w, m_conv_b, m_w_ffn_down, v_norm_mix_pre, v_norm_mix_post, v_norm_ffn_pre, v_norm_ffn_post, v_w_in, v_rel_bias, v_sinks, v_rwkv_shift_mix, v_w0, v_w_decay_up, v_a0, v_w_iclr_up, v_w_gate_up, v_k_k, v_k_a, v_r_k, v_ln_x_g, v_ln_x_b, v_w_out, v_w_ffn_up, v_conv_w, v_conv_b, v_w_ffn_down):
    given = dict(locals())
    names = [n for n, _ in REPLICATED] + [n for n, _, _ in SMALL_SHARDED] + [n for n, _, _ in BIG]
    order = ["norm_mix_pre", "norm_mix_post", "norm_ffn_pre", "norm_ffn_post", "w_in", "rel_bias", "sinks",
             "rwkv_shift_mix", "w0", "w_decay_up", "a0", "w_iclr_up", "w_gate_up", "k_k", "k_a", "r_k", "ln_x_g",
             "ln_x_b", "w_out", "w_ffn_up", "conv_w", "conv_b", "w_ffn_down"]
    assert sorted(names) == sorted(order)

    big_sh = {n: given[n].reshape(a, b).astype(BF16) for n, a, b in BIG}
    small_sh = [given[n].reshape(r, c // N_CHIPS) for n, r, c in SMALL_SHARDED]
    gathered = _gather_weights([big_sh["w_in"]], small_sh)
    rest = ("w_out", "w_ffn_up", "w_ffn_down")
    win_st, rest_sh = lax.optimization_barrier((gathered[0], [big_sh[n] for n in rest]))
    sm = {n: given[n] for n, _ in REPLICATED}
    sm["r_k"] = r_k.reshape(N_Q_HEADS, HEAD_DIM)
    for (n, r, c), st in zip(SMALL_SHARDED, gathered[1:]):
        sm[n] = st.transpose(1, 0, 2).reshape(r, c)

    lands = [lax.empty((N_CHIPS,) + t.shape, BF16) for t in rest_sh]
    plan_w = _plan_gather(len(rest))
    n_w = N_CHIPS * len(rest)
    w_sems, w_bufs, token = _copies_start("gather_rest_start", rest_sh + lands, plan_w, n_w)
    sm["norm_mix_pre"] = norm_mix_pre + token[0:1, 0:1]

    def on_rest_weights(after):
        out = _copies_wait("gather_rest_wait", w_sems, w_bufs, plan_w, n_w, after)
        wout_st, wup_st, wdown_st = out[3:]
        return wout_st.reshape(D_MODEL, D_MODEL), wup_st, wdown_st.reshape(D_FF, D_MODEL)

    red_a = _Reduction("a", (1024, D_MODEL, 256), first_id=0)
    red_b = _Reduction("b", (D_MODEL,), first_id=3)

    def on_grads_a(gs):
        d_wdown, d_wup, d_wout = gs
        return red_a.start([d_wdown.reshape(N_CHIPS, 1024, D_MODEL), d_wup, d_wout.reshape(N_CHIPS, 256, D_MODEL)])

    handlers = {"rest_weights": on_rest_weights, "grads_a": on_grads_a, "seam_1": red_a.after_pair,
                "seam_2": red_a.after_chips, "grads_b": lambda g: red_b.start([g])}
    steps = _local_step(x[0], loss_target[0], sm, win_st)
    kind, payload = next(steps)
    while True:
        try:
            kind, payload = steps.send(handlers[kind](payload))
        except StopIteration as done:
            loss, grad_x, grads = done.value
            break

    small_names = [n for n, _ in REPLICATED] + [n for n, _, _ in SMALL_SHARDED]

    def shard_cols(t, s):
        return t[:, s * (t.shape[1] // N_CHIPS):(s + 1) * (t.shape[1] // N_CHIPS)]

    for_chip = jnp.stack([_pack([loss[0]] + [grads[n] for n, _ in REPLICATED]
                                + [shard_cols(grads[n], s) for n, _, _ in SMALL_SHARDED]) for s in range(N_CHIPS)])
    land = lax.empty((N_DEV,) + for_chip.shape[1:], F32)

    def plan_small(refs):
        x, y, c, _ = _place()
        out = []
        for rel in range(N_DEV):
            px, py, pc = x ^ (rel >> 2), y ^ ((rel >> 1) & 1), c ^ (rel & 1)
            out.append((refs[0].at[2 * px + py], refs[1].at[4 * x + 2 * y + c], (px, py, pc)))
        return out

    s_sems, s_bufs, s_token = _copies_start("grad_small_start", [for_chip, land], plan_small, N_DEV)

    red_b.after_pair([grad_x, s_token])
    g_out = {}
    g_out["w_ffn_down"], g_out["w_ffn_up"], g_out["w_out"] = red_a.finish(grad_x)

    delta, new_m, new_v = {}, {}, {}

    def update(n, a, b):
        delta[n], new_m[n], new_v[n] = _adamw(given[n].reshape(a, b), g_out[n], given["m_" + n].reshape(a, b),
                                              given["v_" + n].reshape(a, b), "adamw_" + n, 256)

    for n, a, b in BIG[1:]:
        update(n, a, b)
    done = [delta[n] for n, _, _ in BIG[1:]]
    red_b.after_chips(done)
    parts = _copies_wait("grad_small_wait", s_sems, s_bufs, plan_small, N_DEV, done)[1]
    no_param = jnp.zeros((LANES,), F32)
    packs = [_pack([no_param] + [given[pre + n] for n in small_names]) for pre in ("", "m_", "v_")]

    def piece_shape(n):
        shape = given[n].shape
        rows, cols = int(np.prod(shape[:-1])), shape[-1]
        whole = cols % LANES == 0 or (LANES % cols == 0 and (rows * cols) % LANES == 0 and cols >= HEAD_DIM)
        return (rows, cols) if whole else (-(-rows * cols // LANES), LANES)

    shapes = [(1, LANES)] + [piece_shape(n) for n in small_names]
    upd = _adamw_small(packs[0], parts, packs[1], packs[2], shapes)
    loss = upd[3][0][0, 0]
    for i, n in enumerate(small_names):
        shape = given[n].shape
        size = int(np.prod(shape))
        delta[n], new_m[n], new_v[n], g_out[n] = (u[1 + i].reshape(-1)[:size].reshape(shape) for u in upd)
    g_out["w_in"], = red_b.finish(upd[0][0])
    update(*BIG[0])

    def shaped(d):
        return [d[n].reshape(given[n].shape) for n in order]

    return (loss, grad_x.reshape(x.shape), *shaped(g_out), *shaped(delta), *shaped(new_m), *shaped(new_v))
```

```python
import math

import numpy as np
import jax
import jax.numpy as jnp
from jax import lax
from jax.experimental import pallas as pl
from jax.experimental.pallas import tpu as pltpu

F32 = jnp.float32
BF16 = jnp.bfloat16
MESH = pl.DeviceIdType.MESH

SEQ = 2048
D_MODEL = 1024
HEAD_DIM = 64
D_ATTN = 512
D_RWKV = 512
D_KV = 128
N_Q_HEADS = 8
N_KV_HEADS = 2
Q_PER_KV = 4
BLOCK = 128
N_BUCKETS = 32
MAX_DISTANCE = 128
LORA_DECAY = 64
LORA_ICLR = 64
LORA_GATE = 128
RWKV_COLS = 3 * D_RWKV + LORA_DECAY + LORA_ICLR + LORA_GATE
P_OFF = D_ATTN + 2 * D_KV
D_IN = P_OFF + RWKV_COLS
D_FF = 4096
NORM_EPS = 1e-6
GN_EPS = 64e-5
NEG_INF = -1e30
N_CHIPS = 4
N_DEV = 8
HEAD_SHIFT = HEAD_DIM.bit_length() - 1
BLOCK_SHIFT = BLOCK.bit_length() - 1

ADAM_LR = 0.001
ADAM_B1 = 0.9
ADAM_B2 = 0.999
ADAM_EPS = 1e-08
ADAM_WD = 0.01
ADAM_STEP = 10

VMEM_LIMIT = 52 * 1024 * 1024
LANES = 128


def _cp(sem=None, vmem=VMEM_LIMIT):
    kw = dict(vmem_limit_bytes=vmem)
    if sem is not None:
        kw["dimension_semantics"] = sem
    return pltpu.CompilerParams(**kw)


def _rows(tr, nc):
    return pl.BlockSpec((tr, nc), lambda i: (i, 0))


def _const(shape):
    return pl.BlockSpec(shape, lambda *_: (0,) * len(shape))


ANY = pl.BlockSpec(memory_space=pl.ANY)


def _split(x, n):
    parts = []
    for _ in range(n - 1):
        h = x.astype(BF16)
        parts.append(h)
        x = x - h.astype(F32)
    parts.append(x.astype(BF16))
    return parts


NN = (((1,), (0,)), ((), ()))
NT = (((1,), (1,)), ((), ()))
TN = (((0,), (0,)), ((), ()))


def _dot(a, b, dn=NN):
    return lax.dot_general(a, b, dn, preferred_element_type=F32)


def _dot_ind(x, ind_bf16, n=3):
    acc = None
    for part in _split(x, n):
        t = _dot(part, ind_bf16)
        acc = t if acc is None else acc + t
    return acc


def _head_ones(n):
    r = lax.broadcasted_iota(jnp.int32, (n, n), 0) >> HEAD_SHIFT
    c = lax.broadcasted_iota(jnp.int32, (n, n), 1) >> HEAD_SHIFT
    return jnp.where(r == c, 1.0, 0.0).astype(BF16)


def _matmul(a, b, mode, name, *, m, n, k, tm, tn, a_spec=None, b_spec=None, out=None):
    dn = {"nn": NN, "nt": NT, "tn": TN}[mode]

    def body(a_ref, b_ref, o_ref):
        o_ref[...] = _dot(a_ref[...], b_ref[...], dn)

    if a_spec is None:
        a_spec = pl.BlockSpec((k, tm), lambda i, j: (0, i)) if mode == "tn" else pl.BlockSpec((tm, k), lambda i, j: (i, 0))
    if b_spec is None:
        b_spec = pl.BlockSpec((tn, k), lambda i, j: (j, 0)) if mode == "nt" else pl.BlockSpec((k, tn), lambda i, j: (0, j))
    return pl.pallas_call(
        body, name=name, grid=(m // tm, n // tn),
        in_specs=[a_spec, b_spec],
        out_specs=pl.BlockSpec((tm, tn), lambda i, j: (i, j)) if out is None else out[1],
        out_shape=jax.ShapeDtypeStruct((m, n) if out is None else out[0], F32),
        compiler_params=_cp(("parallel", "parallel")),
    )(a, b)


def _matmul_nt_shards(a, b_st, name, *, m, n, tm, tn, a_spec, a_piece):
    ks = b_st.shape[2]

    def body(a_ref, b_ref, o_ref):
        acc = _dot(a_piece(a_ref, 0), b_ref[0], NT)
        for s in range(1, N_CHIPS):
            acc = acc + _dot(a_piece(a_ref, s), b_ref[s], NT)
        o_ref[...] = acc

    return pl.pallas_call(
        body, name=name, grid=(m // tm, n // tn),
        in_specs=[a_spec, pl.BlockSpec((N_CHIPS, tn, ks), lambda i, j: (0, j, 0))],
        out_specs=pl.BlockSpec((tm, tn), lambda i, j: (i, j)),
        out_shape=jax.ShapeDtypeStruct((m, n), F32),
        compiler_params=_cp(("parallel", "parallel")),
    )(a, b_st)


def _rstd(x):
    return lax.rsqrt(jnp.mean(x * x, axis=-1, keepdims=True) + NORM_EPS)


def _rms_bwd(x, r, g, dy):
    gy = dy * g
    return r * gy - x * ((r * r * r) * (jnp.sum(x * gy, axis=-1, keepdims=True) / x.shape[-1]))


TR = 256
TRN = 512


def _norm_cast(x, g, name):
    def body(x_ref, g_ref, h_ref):
        x = x_ref[...]
        h_ref[...] = (x * _rstd(x) * g_ref[...]).astype(BF16)

    return pl.pallas_call(
        body, name=name, grid=(SEQ // TRN,),
        in_specs=[_rows(TRN, D_MODEL), _const((1, D_MODEL))],
        out_specs=_rows(TRN, D_MODEL),
        out_shape=jax.ShapeDtypeStruct((SEQ, D_MODEL), BF16),
        compiler_params=_cp(("parallel",)),
    )(x, g)


def _mix_norm(x, mix, g2, g3):
    def body(x_ref, mix_ref, g2_ref, g3_ref, x2_ref, h3_ref):
        mixv = mix_ref[...]
        x2 = x_ref[...] + mixv * _rstd(mixv) * g2_ref[...]
        x2_ref[...] = x2
        h3_ref[...] = (x2 * _rstd(x2) * g3_ref[...]).astype(BF16)

    return pl.pallas_call(
        body, name="mix_norm", grid=(SEQ // TRN,),
        in_specs=[_rows(TRN, D_MODEL), _rows(TRN, D_MODEL), _const((1, D_MODEL)), _const((1, D_MODEL))],
        out_specs=[_rows(TRN, D_MODEL), _rows(TRN, D_MODEL)],
        out_shape=[jax.ShapeDtypeStruct((SEQ, D_MODEL), F32), jax.ShapeDtypeStruct((SEQ, D_MODEL), BF16)],
        compiler_params=_cp(("parallel",)),
    )(x, mix, g2, g3)


def _loss_head(x2, f, g4, target):
    def body(x2_ref, f_ref, g4_ref, t_ref, loss_ref, dy_ref, df_ref, dg_ref):
        i = pl.program_id(0)
        f = f_ref[...]
        g4 = g4_ref[...]
        r = _rstd(f)
        e = x2_ref[...] + f * r * g4 - t_ref[...]
        dy = e * (1.0 / D_MODEL)
        dy_ref[...] = dy
        df_ref[...] = _rms_bwd(f, r, g4, dy).astype(BF16)
        part = 0.5 * jnp.sum(jnp.sum(e * e, axis=-1, keepdims=True), axis=0, keepdims=True) * (1.0 / D_MODEL)
        dg = jnp.sum(dy * f * r, axis=0, keepdims=True)

        @pl.when(i == 0)
        def _():
            loss_ref[...] = jnp.zeros_like(loss_ref)
            dg_ref[...] = jnp.zeros_like(dg_ref)

        loss_ref[...] += jnp.broadcast_to(part, loss_ref.shape)
        dg_ref[...] += dg

    return pl.pallas_call(
        body, name="loss_head", grid=(SEQ // TRN,),
        in_specs=[_rows(TRN, D_MODEL), _rows(TRN, D_MODEL), _const((1, D_MODEL)), _rows(TRN, D_MODEL)],
        out_specs=[_const((8, LANES)), _rows(TRN, D_MODEL), _rows(TRN, D_MODEL), _const((1, D_MODEL))],
        out_shape=[jax.ShapeDtypeStruct((8, LANES), F32), jax.ShapeDtypeStruct((SEQ, D_MODEL), F32),
                   jax.ShapeDtypeStruct((SEQ, D_MODEL), BF16), jax.ShapeDtypeStruct((1, D_MODEL), F32)],
        compiler_params=_cp(("arbitrary",)),
    )(x2, f, g4, target)


def _mid_bwd(x2, mix, dy, dh3, g2, g3):
    def body(x2_ref, mix_ref, dy_ref, dh3_ref, g2_ref, g3_ref, dx2_ref, dmix_ref, dg2_ref, dg3_ref):
        i = pl.program_id(0)
        x2 = x2_ref[...]
        mixv = mix_ref[...]
        dh3 = dh3_ref[...]
        r3 = _rstd(x2)
        dx2 = dy_ref[...] + _rms_bwd(x2, r3, g3_ref[...], dh3)
        dx2_ref[...] = dx2
        r2 = _rstd(mixv)
        dmix_ref[...] = _rms_bwd(mixv, r2, g2_ref[...], dx2).astype(BF16)

        @pl.when(i == 0)
        def _():
            dg2_ref[...] = jnp.zeros_like(dg2_ref)
            dg3_ref[...] = jnp.zeros_like(dg3_ref)

        dg3_ref[...] += jnp.sum(dh3 * x2 * r3, axis=0, keepdims=True)
        dg2_ref[...] += jnp.sum(dx2 * mixv * r2, axis=0, keepdims=True)

    return pl.pallas_call(
        body, name="mid_bwd", grid=(SEQ // TRN,),
        in_specs=[_rows(TRN, D_MODEL)] * 4 + [_const((1, D_MODEL))] * 2,
        out_specs=[_rows(TRN, D_MODEL), _rows(TRN, D_MODEL), _const((1, D_MODEL)), _const((1, D_MODEL))],
        out_shape=[jax.ShapeDtypeStruct((SEQ, D_MODEL), F32), jax.ShapeDtypeStruct((SEQ, D_MODEL), BF16),
                   jax.ShapeDtypeStruct((1, D_MODEL), F32), jax.ShapeDtypeStruct((1, D_MODEL), F32)],
        compiler_params=_cp(("arbitrary",)),
    )(x2, mix, dy, dh3, g2, g3)


def _first_bwd(x, dx2, dh1, g1):
    def body(x_ref, dx2_ref, dh1_ref, g1_ref, dx_ref, dg1_ref):
        i = pl.program_id(0)
        x = x_ref[...]
        dh1 = dh1_ref[...]
        r = _rstd(x)
        dx_ref[...] = dx2_ref[...] + _rms_bwd(x, r, g1_ref[...], dh1)

        @pl.when(i == 0)
        def _():
            dg1_ref[...] = jnp.zeros_like(dg1_ref)

        dg1_ref[...] += jnp.sum(dh1 * x * r, axis=0, keepdims=True)

    return pl.pallas_call(
        body, name="first_bwd", grid=(SEQ // TRN,),
        in_specs=[_rows(TRN, D_MODEL)] * 3 + [_const((1, D_MODEL))],
        out_specs=[_rows(TRN, D_MODEL), _const((1, D_MODEL))],
        out_shape=[jax.ShapeDtypeStruct((SEQ, D_MODEL), F32), jax.ShapeDtypeStruct((1, D_MODEL), F32)],
        compiler_params=_cp(("arbitrary",)),
    )(x, dx2, dh1, g1)


TC = 256
N_CB = D_FF // TC
GELU_C = math.sqrt(2.0 / math.pi)


def _shift_down(u, s):
    rolled = pltpu.roll(u, s, 0)
    row = lax.broadcasted_iota(jnp.int32, u.shape, 0)
    return jnp.where(row >= s, rolled, 0.0)


def _shift_up(u, s):
    n = u.shape[0]
    rolled = pltpu.roll(u, n - s, 0)
    row = lax.broadcasted_iota(jnp.int32, u.shape, 0)
    return jnp.where(row < n - s, rolled, 0.0)


def _conv3(u, w, b):
    return b + w[0:1] * _shift_down(u, 2) + w[1:2] * _shift_down(u, 1) + w[2:3] * u


def _gelu_and_grad(x):
    inner = GELU_C * (x + 0.044715 * (x * x * x))
    t = jnp.tanh(inner)
    gelu = 0.5 * x * (1.0 + t)
    dgelu = 0.5 * (1.0 + t) + 0.5 * x * (1.0 - t * t) * (GELU_C * (1.0 + 3 * 0.044715 * (x * x)))
    return gelu, dgelu


def _ffn_specs():
    col = lambda off: pl.BlockSpec((SEQ, TC), lambda *g: (0, g[-1] + off))
    w = lambda off: pl.BlockSpec((3, TC), lambda *g: (0, g[-1] + off))
    b = lambda off: pl.BlockSpec((1, TC), lambda *g: (0, g[-1] + off))
    return col, w, b


def _ffn_up_act(h3, wup_st, conv_w, conv_b):
    col, w, b = _ffn_specs()
    per_shard = wup_st.shape[2] // TC

    def body(h_ref, upg_ref, upv_ref, wg_ref, wv_ref, bg_ref, bv_ref, ug_ref, uv_ref, gate_ref, val_ref, act_ref):
        h = h_ref[...]
        ug = _dot(h, upg_ref[...])
        uv = _dot(h, upv_ref[...])
        ug_ref[...] = ug
        uv_ref[...] = uv
        gate = _conv3(ug, wg_ref[...], bg_ref[...])
        val = _conv3(uv, wv_ref[...], bv_ref[...])
        gate_ref[...] = gate
        val_ref[...] = val
        act_ref[...] = (_gelu_and_grad(gate)[0] * val).astype(BF16)

    return pl.pallas_call(
        body, name="ffn_up_act", grid=(N_CB,),
        in_specs=[_const((SEQ, D_MODEL)),
                  pl.BlockSpec((None, D_MODEL, TC), lambda j: (j // per_shard, 0, j % per_shard)),
                  pl.BlockSpec((None, D_MODEL, TC), lambda j: (2 + j // per_shard, 0, j % per_shard)),
                  w(0), w(N_CB), b(0), b(N_CB)],
        out_specs=[col(0)] * 5,
        out_shape=[jax.ShapeDtypeStruct((SEQ, D_FF), F32)] * 4 + [jax.ShapeDtypeStruct((SEQ, D_FF), BF16)],
        compiler_params=_cp(("parallel",)),
    )(h3, wup_st, wup_st, conv_w, conv_w, conv_b, conv_b)


def _ffn_act_bwd(u_gate, u_val, gate, val, df, wdown, conv_w):
    col, w, _ = _ffn_specs()
    both = lambda rows: pl.BlockSpec((2, rows, TC), lambda j: (0, 0, j))

    def body(ug_ref, uv_ref, gate_ref, val_ref, df_ref, wd_ref, wg_ref, wv_ref, du_ref, dw_ref, db_ref):
        da = _dot(df_ref[...], wd_ref[...], NT)
        gelu, dgelu = _gelu_and_grad(gate_ref[...])
        halves = ((da * val_ref[...] * dgelu, ug_ref, wg_ref[...]), (da * gelu, uv_ref, wv_ref[...]))
        for h, (duc, u_ref, wh) in enumerate(halves):
            uh = u_ref[...]
            up1, up2 = _shift_up(duc, 1), _shift_up(duc, 2)
            du_ref[h] = (wh[2:3] * duc + wh[1:2] * up1 + wh[0:1] * up2).astype(BF16)
            db_ref[h] = jnp.sum(duc, axis=0, keepdims=True)
            dw_ref[h] = jnp.concatenate(
                [jnp.sum(up2 * uh, axis=0, keepdims=True), jnp.sum(up1 * uh, axis=0, keepdims=True),
                 jnp.sum(duc * uh, axis=0, keepdims=True)], axis=0)

    return pl.pallas_call(
        body, name="ffn_act_bwd", grid=(N_CB,),
        in_specs=[col(0)] * 4 + [_const((SEQ, D_MODEL)), pl.BlockSpec((TC, D_MODEL), lambda j: (j, 0)), w(0), w(N_CB)],
        out_specs=[both(SEQ), both(3), both(1)],
        out_shape=[jax.ShapeDtypeStruct((2, SEQ, D_FF), BF16), jax.ShapeDtypeStruct((2, 3, D_FF), F32),
                   jax.ShapeDtypeStruct((2, 1, D_FF), F32)],
        compiler_params=_cp(("parallel",)),
    )(u_gate, u_val, gate, val, df, wdown, conv_w, conv_w)


def _t5_onehot():
    rel = (np.arange(BLOCK)[:, None] + BLOCK) - np.arange(2 * BLOCK)[None, :]
    n = np.maximum(rel, 0)
    max_exact = N_BUCKETS // 2
    large = max_exact + (np.log(np.maximum(n, 1).astype(np.float32) / np.float32(max_exact))
                         / np.float32(math.log(MAX_DISTANCE / max_exact))
                         * np.float32(N_BUCKETS - max_exact)).astype(np.int32)
    large = np.minimum(large, N_BUCKETS - 1)
    bucket = np.where(n < max_exact, n, large).reshape(-1)
    return (bucket[None, :] == np.arange(N_BUCKETS)[:, None]).astype(np.float32)


N_REL = BLOCK * 2 * BLOCK


def _bias_table(rel_bias_t, onehot):
    def body(rb_ref, oh_ref, o_ref):
        o_ref[...] = _dot_ind(rb_ref[...], oh_ref[...])

    return pl.pallas_call(
        body, name="bias_table", grid=(1,),
        in_specs=[_const((N_Q_HEADS, N_BUCKETS)), _const((N_BUCKETS, N_REL))],
        out_specs=_const((N_Q_HEADS, N_REL)),
        out_shape=jax.ShapeDtypeStruct((N_Q_HEADS, N_REL), F32),
        compiler_params=_cp(("arbitrary",)),
    )(rel_bias_t, onehot)


def _bias_table_bwd(dbias, onehot):
    def body(db_ref, oh_ref, o_ref):
        acc = None
        for part in _split(db_ref[...], 3):
            t = _dot(part, oh_ref[...], NT)
            acc = t if acc is None else acc + t
        o_ref[...] = acc

    return pl.pallas_call(
        body, name="bias_table_bwd", grid=(1,),
        in_specs=[_const((N_Q_HEADS, N_REL)), _const((N_BUCKETS, N_REL))],
        out_specs=_const((N_Q_HEADS, N_BUCKETS)),
        out_shape=jax.ShapeDtypeStruct((N_Q_HEADS, N_BUCKETS), F32),
        compiler_params=_cp(("arbitrary",)),
    )(dbias, onehot)


def _attn_pieces(n, q, kvp, kvc, bias_ref, sinks_ref, hk):
    qi = lax.broadcasted_iota(jnp.int32, (BLOCK, 2 * BLOCK), 0)
    kj = lax.broadcasted_iota(jnp.int32, (BLOCK, 2 * BLOCK), 1)
    rel = qi + BLOCK - kj
    first_key = jnp.where(n > 0, 0, BLOCK)
    ok = jnp.where(rel >= 0, jnp.where(rel < BLOCK, jnp.where(kj >= first_key, 1.0, 0.0), 0.0), 0.0)
    ok4 = jnp.concatenate([ok] * Q_PER_KV, axis=0) > 0.5
    c0 = hk * HEAD_DIM
    kcat = jnp.concatenate([kvp[:, c0:c0 + HEAD_DIM], kvc[:, c0:c0 + HEAD_DIM]], axis=0).astype(BF16)
    vcat = jnp.concatenate([kvp[:, D_KV + c0:D_KV + c0 + HEAD_DIM], kvc[:, D_KV + c0:D_KV + c0 + HEAD_DIM]],
                           axis=0).astype(BF16)
    q0 = hk * Q_PER_KV * HEAD_DIM
    qs = jnp.concatenate([q[:, q0 + g * HEAD_DIM:q0 + (g + 1) * HEAD_DIM] for g in range(Q_PER_KV)],
                         axis=0).astype(BF16)
    s = _dot(qs, kcat, NT) * (HEAD_DIM ** -0.5) + bias_ref[hk]
    s = jnp.where(ok4, s, NEG_INF)
    row = lax.broadcasted_iota(jnp.int32, (Q_PER_KV * BLOCK, 1), 0)
    sink = jnp.zeros((Q_PER_KV * BLOCK, 1), F32)
    for g in range(Q_PER_KV):
        sink = jnp.where((row >> BLOCK_SHIFT) == g, sinks_ref[hk * Q_PER_KV + g], sink)
    m = jnp.maximum(jnp.max(s, axis=-1, keepdims=True), sink)
    p = jnp.exp(s - m)
    es = jnp.exp(sink - m)
    inv = 1.0 / (jnp.sum(p, axis=-1, keepdims=True) + es)
    return qs, kcat, vcat, p * inv, es * inv


def _attn_in_specs():
    return [pl.BlockSpec((BLOCK, D_ATTN), lambda n: (n, 0)),
            pl.BlockSpec((BLOCK, 2 * D_KV), lambda n: (jnp.maximum(n - 1, 0), D_ATTN // (2 * D_KV))),
            pl.BlockSpec((BLOCK, 2 * D_KV), lambda n: (n, D_ATTN // (2 * D_KV))),
            _const((N_KV_HEADS, Q_PER_KV * BLOCK, 2 * BLOCK)),
            pl.BlockSpec(memory_space=pltpu.SMEM)]


def _unstack_heads(t):
    return jnp.concatenate([t[g * BLOCK:(g + 1) * BLOCK] for g in range(Q_PER_KV)], axis=1)


def _attn_fwd(proj, bias, sinks):
    def body(q_ref, kvp_ref, kvc_ref, bias_ref, sinks_ref, o_ref):
        n = pl.program_id(0)
        q, kvp, kvc = q_ref[...], kvp_ref[...], kvc_ref[...]
        outs = []
        for hk in range(N_KV_HEADS):
            _, _, vcat, probs, _ = _attn_pieces(n, q, kvp, kvc, bias_ref, sinks_ref, hk)
            outs.append(_unstack_heads(_dot(probs.astype(BF16), vcat)))
        o_ref[...] = jnp.concatenate(outs, axis=1)

    return pl.pallas_call(
        body, name="attn_fwd", grid=(SEQ // BLOCK,),
        in_specs=_attn_in_specs(),
        out_specs=pl.BlockSpec((BLOCK, D_ATTN), lambda n: (n, 0)),
        out_shape=jax.ShapeDtypeStruct((SEQ, D_ATTN), F32),
        compiler_params=_cp(("parallel",)),
    )(proj, proj, proj, bias, sinks)


def _attn_bwd(proj, bias, sinks, dcat):
    nb = SEQ // BLOCK

    def body(q_ref, kvp_ref, kvc_ref, bias_ref, sinks_ref, do_ref, dq_ref, dkv_ref, dbias_ref, dsink_ref, dsacc):
        n = pl.program_id(0)

        @pl.when(n == 0)
        def _():
            dkv_ref[...] = jnp.zeros_like(dkv_ref)
            dbias_ref[...] = jnp.zeros_like(dbias_ref)
            dsacc[...] = jnp.zeros_like(dsacc)

        q, kvp, kvc = q_ref[...], kvp_ref[...], kvc_ref[...]
        do_all = do_ref[...]
        dqs, dks, dvs = [], [], []
        for hk in range(N_KV_HEADS):
            qs, kcat, vcat, probs, psink = _attn_pieces(n, q, kvp, kvc, bias_ref, sinks_ref, hk)
            q0 = hk * Q_PER_KV * HEAD_DIM
            do = jnp.concatenate([do_all[:, q0 + g * HEAD_DIM:q0 + (g + 1) * HEAD_DIM] for g in range(Q_PER_KV)],
                                 axis=0).astype(BF16)
            dprobs = _dot(do, vcat, NT)
            dvs.append(_dot(probs.astype(BF16), do, TN))
            rowdot = jnp.sum(probs * dprobs, axis=-1, keepdims=True)
            ds = probs * (dprobs - rowdot)
            dsacc[hk] += -psink * rowdot
            dbias_ref[hk] += ds
            dsb = (ds * (HEAD_DIM ** -0.5)).astype(BF16)
            dqs.append(_unstack_heads(_dot(dsb, kcat)))
            dks.append(_dot(dsb, qs, TN))
        dq_ref[...] = jnp.concatenate(dqs, axis=1)
        upd = jnp.concatenate(dks + dvs, axis=1)
        cur = pl.multiple_of(n * BLOCK, BLOCK)
        dkv_ref[pl.ds(cur, BLOCK), :] += upd[BLOCK:]

        @pl.when(n > 0)
        def _():
            prev = pl.multiple_of((n - 1) * BLOCK, BLOCK)
            dkv_ref[pl.ds(prev, BLOCK), :] += upd[:BLOCK]

        @pl.when(n == nb - 1)
        def _():
            for hk in range(N_KV_HEADS):
                for g in range(Q_PER_KV):
                    tot = jnp.sum(dsacc[hk, g * BLOCK:(g + 1) * BLOCK, :], axis=0, keepdims=True)
                    h = hk * Q_PER_KV + g
                    dsink_ref[h:h + 1, :] = jnp.broadcast_to(tot, (1, LANES))

    return pl.pallas_call(
        body, name="attn_bwd", grid=(nb,),
        in_specs=_attn_in_specs() + [pl.BlockSpec((BLOCK, D_ATTN), lambda n: (n, 0))],
        out_specs=[pl.BlockSpec((BLOCK, D_ATTN), lambda n: (n, 0)), _const((SEQ, 2 * D_KV)),
                   _const((N_KV_HEADS, Q_PER_KV * BLOCK, 2 * BLOCK)), _const((N_Q_HEADS, LANES))],
        out_shape=[jax.ShapeDtypeStruct((SEQ, D_ATTN), F32), jax.ShapeDtypeStruct((SEQ, 2 * D_KV), F32),
                   jax.ShapeDtypeStruct((N_KV_HEADS, Q_PER_KV * BLOCK, 2 * BLOCK), F32),
                   jax.ShapeDtypeStruct((N_Q_HEADS, LANES), F32)],
        scratch_shapes=[pltpu.VMEM((N_KV_HEADS, Q_PER_KV * BLOCK, 1), F32)],
        compiler_params=_cp(("arbitrary",)),
    )(proj, proj, proj, bias, sinks, dcat)


@jax.custom_vjp
def _head_sum(x):
    ones = _head_ones(LANES)
    return jnp.concatenate([_dot_ind(x[:, c:c + LANES], ones, 2) for c in range(0, x.shape[-1], LANES)], axis=1)


_head_sum.defvjp(lambda x: (_head_sum(x), None), lambda _, ct: (_head_sum(ct),))


@jax.custom_vjp
def _bdot(a, w):
    return _dot(a.astype(BF16), w.astype(BF16))


def _bdot_bwd(res, ct):
    a, w = res
    ctb = ct.astype(BF16)
    return _dot(ctb, w.astype(BF16), NT), _dot(a.astype(BF16), ctb, TN)


_bdot.defvjp(lambda a, w: (_bdot(a, w), (a, w)), _bdot_bwd)


def _sigmoid(x):
    return 0.5 * (jnp.tanh(0.5 * x) + 1.0)


def _softplus(x):
    return jnp.maximum(x, 0.0) + jnp.log(1.0 + jnp.exp(-jnp.abs(x)))


def _rwkv_core(r, k, v, zwa, zg, w0, wdu, a0, wiu, wgu, k_k, k_a):
    w_log = -_softplus(-(w0 + _bdot(jnp.tanh(zwa), wdu))) - 0.5
    decay = jnp.exp(-jnp.exp(w_log))
    a = _sigmoid(a0 + _bdot(zwa, wiu))
    g = _bdot(_sigmoid(zg), wgu)
    kk = k * k_k
    kk = kk / jnp.maximum(jnp.sqrt(_head_sum(kk * kk)), 1e-12)
    k2 = k * (1.0 + (a - 1.0) * k_a)
    return r, decay, k2, v, -kk, kk * a, g


def _rwkv_out(o, r, k2, v, g, lng, lnb, rk):
    mu = _head_sum(o) * (1.0 / HEAD_DIM)
    d = o - mu
    var = _head_sum(d * d) * (1.0 / HEAD_DIM)
    on = d * lax.rsqrt(var + GN_EPS) * lng + lnb
    bonus = _head_sum(r * k2 * rk) * v
    return (on + bonus) * g


P_SPLITS = (0, 512, 1024, 1536, 1664, 1792)
N_PREP_PARAMS = 7
HALO = 8


def _shifted_pieces(i, p_ref, halo_ref, mix_ref):
    p = p_ref[:, P_OFF:]
    prev_row = halo_ref[HALO - 1:HALO, P_OFF:] * jnp.where(i > 0, 1.0, 0.0)
    row = lax.broadcasted_iota(jnp.int32, p.shape, 0)
    pprev = jnp.where(row == 0, prev_row, pltpu.roll(p, 1, 0))
    delta = pprev - p
    ps = p + delta * mix_ref[...]
    return [ps[:, a:b] for a, b in zip(P_SPLITS[:-1], P_SPLITS[1:])], delta


def _prep_in_specs():
    return [_rows(TR, D_IN),
            pl.BlockSpec((HALO, D_IN), lambda i: (jnp.maximum(i * (TR // HALO) - 1, 0), 0)),
            _const((1, RWKV_COLS)), _const((1, D_RWKV)), _const((LANES, D_RWKV)), _const((1, D_RWKV)),
            _const((LANES, D_RWKV)), _const((LANES, D_RWKV)), _const((1, D_RWKV)), _const((1, D_RWKV))]


def _rwkv_prep(proj, mix, prm):
    def body(p_ref, halo_ref, mix_ref, *refs):
        prm_refs, outs = refs[:N_PREP_PARAMS], refs[N_PREP_PARAMS:]
        pieces, _ = _shifted_pieces(pl.program_id(0), p_ref, halo_ref, mix_ref)
        vals = _rwkv_core(*pieces, *[t[...] for t in prm_refs])
        for ref, val in zip(outs, vals):
            ref[...] = val

    return pl.pallas_call(
        body, name="rwkv_prep", grid=(SEQ // TR,),
        in_specs=_prep_in_specs(),
        out_specs=[_rows(TR, D_RWKV)] * 7,
        out_shape=[jax.ShapeDtypeStruct((SEQ, D_RWKV), F32)] * 7,
        compiler_params=_cp(("parallel",)),
    )(proj, proj, mix, *prm)


def _rwkv_prep_bwd(proj, mix, prm, cts):
    def body(p_ref, halo_ref, mix_ref, *refs):
        i = pl.program_id(0)
        prm_refs = refs[:N_PREP_PARAMS]
        ct_refs = refs[N_PREP_PARAMS:N_PREP_PARAMS + 10]
        dps_ref, dmix_ref = refs[N_PREP_PARAMS + 10:N_PREP_PARAMS + 12]
        dprm_refs = refs[N_PREP_PARAMS + 12:]
        pieces, delta = _shifted_pieces(i, p_ref, halo_ref, mix_ref)
        _, vjp = jax.vjp(_rwkv_core, *pieces, *[t[...] for t in prm_refs])
        dr1, dr2, dw, dk1, dk2, dv1, dv2, dkkn, db, dg = [t[...] for t in ct_refs]
        grads = vjp((dr1 + dr2, dw, dk1 + dk2, dv1 + dv2, dkkn, db, dg))
        dps = jnp.concatenate(grads[:5], axis=1)
        dps_ref[...] = dps

        @pl.when(i == 0)
        def _():
            dmix_ref[...] = jnp.zeros_like(dmix_ref)
            for ref in dprm_refs:
                ref[...] = jnp.zeros_like(ref)

        dmix_ref[...] += jnp.sum(dps * delta, axis=0, keepdims=True)
        for ref, gval in zip(dprm_refs, grads[5:]):
            ref[...] += gval

    prm_shapes = [(1, D_RWKV), (LANES, D_RWKV), (1, D_RWKV), (LANES, D_RWKV), (LANES, D_RWKV), (1, D_RWKV), (1, D_RWKV)]
    return pl.pallas_call(
        body, name="rwkv_prep_bwd", grid=(SEQ // TR,),
        in_specs=_prep_in_specs() + [_rows(TR, D_RWKV)] * 10,
        out_specs=[_rows(TR, RWKV_COLS), _const((1, RWKV_COLS))] + [_const(s) for s in prm_shapes],
        out_shape=[jax.ShapeDtypeStruct((SEQ, RWKV_COLS), F32), jax.ShapeDtypeStruct((1, RWKV_COLS), F32)]
        + [jax.ShapeDtypeStruct(s, F32) for s in prm_shapes],
        compiler_params=_cp(("arbitrary",)),
    )(proj, proj, mix, *prm, *cts)


def _rwkv_post(o, r, k2, v, g, lng, lnb, rk, attn):
    def body(o_ref, r_ref, k_ref, v_ref, g_ref, lng_ref, lnb_ref, rk_ref, attn_ref, cat_ref):
        rw = _rwkv_out(*[t[...] for t in (o_ref, r_ref, k_ref, v_ref, g_ref, lng_ref, lnb_ref, rk_ref)])
        cat_ref[...] = jnp.concatenate([attn_ref[...], rw], axis=1).astype(BF16)

    return pl.pallas_call(
        body, name="rwkv_post", grid=(SEQ // TR,),
        in_specs=[_rows(TR, D_RWKV)] * 5 + [_const((1, D_RWKV))] * 3 + [_rows(TR, D_ATTN)],
        out_specs=_rows(TR, D_MODEL),
        out_shape=jax.ShapeDtypeStruct((SEQ, D_MODEL), BF16),
        compiler_params=_cp(("parallel",)),
    )(o, r, k2, v, g, lng, lnb, rk, attn)


def _rwkv_post_bwd(o, r, k2, v, g, lng, lnb, rk, dcat):
    def body(o_ref, r_ref, k_ref, v_ref, g_ref, lng_ref, lnb_ref, rk_ref, dcat_ref,
             do_ref, dr_ref, dk_ref, dv_ref, dg_ref, dlng_ref, dlnb_ref, drk_ref):
        i = pl.program_id(0)
        args = [t[...] for t in (o_ref, r_ref, k_ref, v_ref, g_ref, lng_ref, lnb_ref, rk_ref)]
        _, vjp = jax.vjp(_rwkv_out, *args)
        grads = vjp(dcat_ref[:, D_ATTN:])
        for ref, gval in zip((do_ref, dr_ref, dk_ref, dv_ref, dg_ref), grads[:5]):
            ref[...] = gval

        @pl.when(i == 0)
        def _():
            for ref in (dlng_ref, dlnb_ref, drk_ref):
                ref[...] = jnp.zeros_like(ref)

        for ref, gval in zip((dlng_ref, dlnb_ref, drk_ref), grads[5:]):
            ref[...] += gval

    return pl.pallas_call(
        body, name="rwkv_post_bwd", grid=(SEQ // TR,),
        in_specs=[_rows(TR, D_RWKV)] * 5 + [_const((1, D_RWKV))] * 3 + [_rows(TR, D_MODEL)],
        out_specs=[_rows(TR, D_RWKV)] * 5 + [_const((1, D_RWKV))] * 3,
        out_shape=[jax.ShapeDtypeStruct((SEQ, D_RWKV), F32)] * 5 + [jax.ShapeDtypeStruct((1, D_RWKV), F32)] * 3,
        compiler_params=_cp(("arbitrary",)),
    )(o, r, k2, v, g, lng, lnb, rk, dcat)


def _assemble_dproj(dq, dkv, dps, mix):
    last = SEQ // HALO - 1

    def body(dq_ref, dkv_ref, dps_ref, nxt_ref, mix_ref, o_ref):
        i = pl.program_id(0)
        dps = dps_ref[...]
        mixv = mix_ref[...]
        nxt_row = nxt_ref[0:1, :] * jnp.where(i < SEQ // TR - 1, 1.0, 0.0)
        row = lax.broadcasted_iota(jnp.int32, dps.shape, 0)
        up = jnp.where(row == TR - 1, nxt_row, pltpu.roll(dps, TR - 1, 0))
        dp = dps * (1.0 - mixv) + up * mixv
        o_ref[...] = jnp.concatenate([dq_ref[...], dkv_ref[...], dp], axis=1).astype(BF16)

    return pl.pallas_call(
        body, name="assemble_dproj", grid=(SEQ // TR,),
        in_specs=[_rows(TR, D_ATTN), _rows(TR, 2 * D_KV), _rows(TR, RWKV_COLS),
                  pl.BlockSpec((HALO, RWKV_COLS), lambda i: (jnp.minimum((i + 1) * (TR // HALO), last), 0)),
                  _const((1, RWKV_COLS))],
        out_specs=_rows(TR, D_IN),
        out_shape=jax.ShapeDtypeStruct((SEQ, D_IN), BF16),
        compiler_params=_cp(("parallel",)),
    )(dq, dkv, dps, dps, mix)


N_PAIR = D_RWKV // LANES
CHUNK = 64
N_CHUNK = SEQ // CHUNK
GROUP = 8
STATE = (N_PAIR, HEAD_DIM, LANES)


def _lane_sums(lhs_tiles, ones2):
    out = _dot(jnp.concatenate(lhs_tiles, axis=0), ones2)
    return [out[i * HEAD_DIM:(i + 1) * HEAD_DIM] for i in range(len(lhs_tiles))]


def _seg_sum(xs, ones2):
    return _lane_sums([jnp.concatenate(_split(x, 2), axis=1) for x in xs], ones2)


def _seg_sum_rows(xs, ones2):
    out = _dot(jnp.concatenate(_split(jnp.concatenate(xs, axis=0), 2), axis=1), ones2)
    return [out[i * GROUP:(i + 1) * GROUP] for i in range(len(xs))]


def _col_form(rows, diag, ones2):
    zero = jnp.zeros((HEAD_DIM, LANES), BF16)
    tiles = []
    for row in rows:
        hi = row.astype(BF16)
        lo = (row - hi.astype(F32)).astype(BF16)
        tiles.append(jnp.concatenate(
            [jnp.where(diag, jnp.broadcast_to(part, (HEAD_DIM, LANES)), zero) for part in (hi, lo)], axis=1))
    return _lane_sums(tiles, ones2)


def _scan_consts():
    ones2 = jnp.concatenate([_head_ones(LANES)] * 2, axis=0)
    sub = lax.broadcasted_iota(jnp.int32, (HEAD_DIM, LANES), 0)
    lane_in_head = lax.broadcasted_iota(jnp.int32, (HEAD_DIM, LANES), 1) & (HEAD_DIM - 1)
    return ones2, lane_in_head == sub, lane_in_head


def _rows_of_columns(tile):
    t = tile.T
    return jnp.concatenate([t[:CHUNK], t[HEAD_DIM:HEAD_DIM + CHUNK]], axis=1)


def _pair(j):
    return slice(j * LANES, (j + 1) * LANES)


def _scan_fwd(r, w, k, v, kkn, b):
    def body(r_ref, w_ref, k_ref, v_ref, kkn_ref, b_ref, o_ref, st_ref, sa_ref, s_scr):
        c = pl.program_id(0)
        ones2, diag, lane_in_head = _scan_consts()

        @pl.when(c == 0)
        def _():
            s_scr[...] = jnp.zeros_like(s_scr)

        def group(gi, carry):
            row0 = pl.multiple_of(gi * GROUP, GROUP)
            states, ocols = list(carry[:N_PAIR]), list(carry[N_PAIR:])
            tiles = [[t[pl.ds(row0, GROUP), _pair(j)] for t in (r_ref, w_ref, k_ref, v_ref, kkn_ref, b_ref)]
                     for j in range(N_PAIR)]
            def row(j, name, u):
                return tiles[j]["rwkvnb".index(name)][u:u + 1]

            def emit_out(u, after):
                outs = _seg_sum([s[j] * row(j, "r", u + d) for d, s in enumerate(after) for j in range(N_PAIR)], ones2)
                for d in range(2):
                    here = lane_in_head == gi * GROUP + u + d
                    for j in range(N_PAIR):
                        ocols[j] = jnp.where(here, outs[d * N_PAIR + j], ocols[j])

            def vcols_of(u):
                cols = _col_form([row(j, "v", u + d) for d in range(2) for j in range(N_PAIR)], diag, ones2)
                return cols[:N_PAIR], cols[N_PAIR:]

            n_next = [pltpu.roll(tiles[j][4], GROUP - 1, 0) for j in range(N_PAIR)]
            dots = _seg_sum_rows([tiles[j][5] * n_next[j] for j in range(N_PAIR)]
                                 + [tiles[j][2] * n_next[j] for j in range(N_PAIR)], ones2)
            b_n, k_n = dots[:N_PAIR], dots[N_PAIR:]
            w_n = [tiles[j][1] * n_next[j] for j in range(N_PAIR)]

            vcols = vcols_of(0)
            after = None
            for u in range(0, GROUP, 2):
                prods = _seg_sum([states[j] * row(j, "n", u) for j in range(N_PAIR)]
                                 + [states[j] * w_n[j][u:u + 1] for j in range(N_PAIR)], ones2)
                if after is not None:
                    emit_out(u - 2, after)
                nxt = vcols_of(u + 2) if u + 2 < GROUP else None
                first, second = [], []
                for j in range(N_PAIR):
                    sa1 = prods[j]
                    sa2 = prods[N_PAIR + j] + sa1 * b_n[j][u:u + 1] + vcols[0][j] * k_n[j][u:u + 1]
                    s1 = states[j] * row(j, "w", u) + sa1 * row(j, "b", u) + vcols[0][j] * row(j, "k", u)
                    s2 = s1 * row(j, "w", u + 1) + sa2 * row(j, "b", u + 1) + vcols[1][j] * row(j, "k", u + 1)
                    st_ref[row0 + u, j] = s1
                    sa_ref[row0 + u, j] = sa1
                    st_ref[row0 + u + 1, j] = s2
                    sa_ref[row0 + u + 1, j] = sa2
                    first.append(s1)
                    second.append(s2)
                    states[j] = s2
                after, vcols = (first, second), nxt
            emit_out(GROUP - 2, after)
            return tuple(states + ocols)

        zero = jnp.zeros((HEAD_DIM, LANES), F32)
        fin = lax.fori_loop(0, CHUNK // GROUP, group, tuple(s_scr[j] for j in range(N_PAIR)) + (zero,) * N_PAIR)
        for j in range(N_PAIR):
            s_scr[j] = fin[j]
            o_ref[:, _pair(j)] = _rows_of_columns(fin[N_PAIR + j])

    blk = pl.BlockSpec((CHUNK, D_RWKV), lambda c: (c, 0))
    per_step = pl.BlockSpec((CHUNK,) + STATE, lambda c: (c, 0, 0, 0))
    return pl.pallas_call(
        body, name="rwkv_scan_fwd", grid=(N_CHUNK,),
        in_specs=[blk] * 6,
        out_specs=[blk, per_step, per_step],
        out_shape=[jax.ShapeDtypeStruct((SEQ, D_RWKV), F32)] + [jax.ShapeDtypeStruct((SEQ,) + STATE, F32)] * 2,
        scratch_shapes=[pltpu.VMEM(STATE, F32)],
        compiler_params=_cp(("arbitrary",)),
    )(r, w, k, v, kkn, b)


def _scan_bwd(r, w, k, v, kkn, b, do, states, sas, ds_in, prev, name, first_chunk, n_chunks):
    top = first_chunk + n_chunks - 1

    def body(r_ref, w_ref, k_ref, v_ref, kkn_ref, b_ref, do_ref, st_ref, before_ref, sa_ref, ds_in_ref, *rest):
        dr_ref, dw_ref, dk_ref, dv_ref, dkkn_ref, db_ref, ds_out_ref, ds_scr = rest[-8:]
        i = pl.program_id(0)
        ones2, diag, lane_in_head = _scan_consts()

        @pl.when(i == 0)
        def _():
            ds_scr[...] = ds_in_ref[...]

        entry = [before_ref[0, j] * jnp.where(i < top, 1.0, 0.0) for j in range(N_PAIR)]

        def reverse(gr, carry):
            gi = CHUNK // GROUP - 1 - gr
            row0 = pl.multiple_of(gi * GROUP, GROUP)
            dstates, dvcols = list(carry[:N_PAIR]), list(carry[N_PAIR:])
            tiles = [[t[pl.ds(row0, GROUP), _pair(j)]
                      for t in (r_ref, w_ref, k_ref, v_ref, kkn_ref, b_ref, do_ref)] for j in range(N_PAIR)]
            rows = [[[None] * GROUP for _ in range(5)] for _ in range(N_PAIR)]

            def row(j, name, u):
                return tiles[j]["rwkvnbd".index(name)][u:u + 1]

            def cols_of(u):
                cols = _col_form([row(j, name, u - d) for d in range(2) for name in "dv" for j in range(N_PAIR)],
                                 diag, ones2)
                return [[(cols[(2 * d) * N_PAIR + j], cols[(2 * d + 1) * N_PAIR + j]) for j in range(N_PAIR)]
                        for d in range(2)]

            def emit_dv(u, dsps):
                outs = _seg_sum([dsp[j] * row(j, "k", u - d) for d, dsp in enumerate(dsps) for j in range(N_PAIR)], ones2)
                for d in range(2):
                    here = lane_in_head == gi * GROUP + u - d
                    for j in range(N_PAIR):
                        dvcols[j] = jnp.where(here, outs[d * N_PAIR + j], dvcols[j])

            b_prev = [pltpu.roll(tiles[j][5], 1, 0) for j in range(N_PAIR)]
            dots = _seg_sum_rows([tiles[j][4] * b_prev[j] for j in range(N_PAIR)]
                                 + [tiles[j][0] * tiles[j][5] for j in range(N_PAIR)], ones2)
            n_b, r_b = dots[:N_PAIR], dots[N_PAIR:]
            w_b = [tiles[j][1] * b_prev[j] for j in range(N_PAIR)]

            def outputs(u, j, dsp, dsa, docol, vcol):
                tl = gi * GROUP + u
                if u > 0:
                    s_prev = st_ref[tl - 1, j]
                else:
                    s_prev = jnp.where(gi == 0, entry[j], st_ref[jnp.maximum(tl - 1, 0), j])
                rows[j][0][u] = jnp.sum(st_ref[tl, j] * docol, axis=0, keepdims=True)
                rows[j][1][u] = jnp.sum(dsp * s_prev, axis=0, keepdims=True)
                rows[j][2][u] = jnp.sum(dsp * vcol, axis=0, keepdims=True)
                rows[j][3][u] = jnp.sum(s_prev * dsa, axis=0, keepdims=True)
                rows[j][4][u] = jnp.sum(dsp * sa_ref[tl, j], axis=0, keepdims=True)

            cols = cols_of(GROUP - 1)
            before = None
            for u in range(GROUP - 1, 0, -2):
                dsp1 = [dstates[j] + cols[0][j][0] * row(j, "r", u) for j in range(N_PAIR)]
                prods = _seg_sum([dsp1[j] * row(j, "b", u) for j in range(N_PAIR)]
                                 + [dsp1[j] * w_b[j][u:u + 1] for j in range(N_PAIR)], ones2)
                if before is not None:
                    emit_dv(u + 2, before)
                nxt = cols_of(u - 2) if u >= 2 else None
                dsp2 = []
                for j in range(N_PAIR):
                    dsa1 = prods[j]
                    dsa2 = prods[N_PAIR + j] + dsa1 * n_b[j][u:u + 1] + cols[1][j][0] * r_b[j][u - 1:u]
                    mid = dsp1[j] * row(j, "w", u) + dsa1 * row(j, "n", u) + cols[1][j][0] * row(j, "r", u - 1)
                    outputs(u, j, dsp1[j], dsa1, *cols[0][j])
                    outputs(u - 1, j, mid, dsa2, *cols[1][j])
                    dstates[j] = mid * row(j, "w", u - 1) + dsa2 * row(j, "n", u - 1)
                    dsp2.append(mid)
                before, cols = (dsp1, dsp2), nxt
            emit_dv(1, before)
            for j in range(N_PAIR):
                for ref, rr in zip((dr_ref, dw_ref, dk_ref, dkkn_ref, db_ref), rows[j]):
                    ref[pl.ds(row0, GROUP), _pair(j)] = jnp.concatenate(rr, axis=0)
            return tuple(dstates + dvcols)

        zero = jnp.zeros((HEAD_DIM, LANES), F32)
        dfin = lax.fori_loop(0, CHUNK // GROUP, reverse, tuple(ds_scr[j] for j in range(N_PAIR)) + (zero,) * N_PAIR)
        for j in range(N_PAIR):
            ds_scr[j] = dfin[j]
            dv_ref[:, _pair(j)] = _rows_of_columns(dfin[N_PAIR + j])

        @pl.when(i == n_chunks - 1)
        def _():
            ds_out_ref[...] = ds_scr[...]

    blk = pl.BlockSpec((CHUNK, D_RWKV), lambda i: (top - i, 0))
    per_step = pl.BlockSpec((CHUNK,) + STATE, lambda i: (top - i, 0, 0, 0))
    step_before = pl.BlockSpec((1,) + STATE, lambda i: (jnp.maximum((top - i) * CHUNK - 1, 0), 0, 0, 0))
    prev = [] if prev is None else list(prev)
    outs = pl.pallas_call(
        body, name=name, grid=(n_chunks,),
        in_specs=[blk] * 7 + [per_step, step_before, per_step, _const(STATE)] + [ANY] * len(prev),
        out_specs=[blk] * 6 + [_const(STATE)],
        out_shape=[jax.ShapeDtypeStruct((SEQ, D_RWKV), F32)] * 6 + [jax.ShapeDtypeStruct(STATE, F32)],
        scratch_shapes=[pltpu.VMEM(STATE, F32)],
        input_output_aliases={11 + t: t for t in range(len(prev))},
        compiler_params=_cp(("arbitrary",)),
    )(r, w, k, v, kkn, b, do, states, states, sas, ds_in, *prev)
    return outs[:6], outs[6]


def _stacked(rows, cols, pick):
    return pl.BlockSpec((None, rows, cols), pick)


def _local_step(x, target, sm, win_st):
    def tied(t, token):
        return t if token is None else t + token[0:1, 0:1].reshape((1,) * t.ndim)

    zpad = jnp.zeros((LORA_DECAY, D_RWKV), F32)
    prm = [sm["w0"], jnp.concatenate([sm["w_decay_up"], zpad], axis=0), sm["a0"],
           jnp.concatenate([zpad, sm["w_iclr_up"]], axis=0), sm["w_gate_up"], sm["k_k"], sm["k_a"]]
    mix = sm["rwkv_shift_mix"]
    onehot = jnp.asarray(_t5_onehot(), BF16)
    sinks = sm["sinks"].reshape(N_Q_HEADS)
    lng, lnb, rk = sm["ln_x_g"], sm["ln_x_b"], sm["r_k"].reshape(1, D_RWKV)

    h1 = _norm_cast(x, sm["norm_mix_pre"], "norm_in")
    proj = _matmul(h1, win_st, "nn", "proj", m=SEQ, n=D_IN, k=D_MODEL, tm=SEQ, tn=640,
                   b_spec=_stacked(D_MODEL, 640, lambda i, j: (j, 0, 0)))
    bias = _bias_table(sm["rel_bias"].T, onehot).reshape(N_KV_HEADS, Q_PER_KV * BLOCK, 2 * BLOCK)
    attn = _attn_fwd(proj, bias, sinks)
    r, w, k2, v, kkn, b, g = _rwkv_prep(proj, mix, prm)
    o, states, sas = _scan_fwd(r, w, k2, v, kkn, b)
    wout, wup_st, wdown = yield ("rest_weights", o)
    cat = _rwkv_post(o, r, k2, v, g, lng, lnb, rk, attn)
    mixo = _matmul(cat, wout, "nn", "out_proj", m=SEQ, n=D_MODEL, k=D_MODEL, tm=SEQ, tn=512)
    x2, h3 = _mix_norm(x, mixo, sm["norm_mix_post"], sm["norm_ffn_pre"])
    u_gate, u_val, gate, val, act = _ffn_up_act(h3, wup_st, sm["conv_w"], sm["conv_b"])
    f = _matmul(act, wdown, "nn", "ffn_down", m=SEQ, n=D_MODEL, k=D_FF, tm=1024, tn=512)
    loss, dy, df, d_g4 = _loss_head(x2, f, sm["norm_ffn_post"], target)

    d_wdown = _matmul(act, df, "tn", "d_wdown", m=D_FF, n=D_MODEL, k=SEQ, tm=512, tn=D_MODEL)
    du, d_convw, d_convb = _ffn_act_bwd(u_gate, u_val, gate, val, df, wdown, sm["conv_w"])
    d_convw = d_convw.transpose(1, 0, 2).reshape(3, 2 * D_FF)
    d_convb = d_convb.reshape(1, 2 * D_FF)
    dh3 = _matmul_nt_shards(du, wup_st, "d_h3", m=SEQ, n=D_MODEL, tm=512, tn=512,
                            a_spec=pl.BlockSpec((2, 512, D_FF), lambda i, j: (0, i, 0)),
                            a_piece=lambda ref, s: ref[s // 2, :, (s % 2) * 2048:(s % 2 + 1) * 2048])
    d_wup = _matmul(h3, du, "tn", "d_wup", m=D_MODEL, n=2 * D_FF, k=SEQ, tm=D_MODEL, tn=512,
                    b_spec=pl.BlockSpec((None, SEQ, 512), lambda i, j: (j // 8, 0, j % 8)),
                    out=((N_CHIPS, D_MODEL, 2048), _stacked(D_MODEL, 512, lambda i, j: (j // 4, 0, j % 4))))
    dx2, dmix, d_g2, d_g3 = _mid_bwd(x2, mixo, dy, dh3, sm["norm_mix_post"], sm["norm_ffn_pre"])
    dcat = _matmul(dmix, wout, "nt", "d_cat", m=SEQ, n=D_MODEL, k=D_MODEL, tm=SEQ, tn=512)
    d_wout = _matmul(cat, dmix, "tn", "d_wout", m=D_MODEL, n=D_MODEL, k=SEQ, tm=512, tn=D_MODEL)
    token = yield ("grads_a", (d_wdown, d_wup, d_wout))
    do, dr_p, dk_p, dv_p, dg, d_lng, d_lnb, d_rk = _rwkv_post_bwd(o, r, k2, v, g, lng, tied(lnb, token), rk, dcat)
    half = N_CHUNK // 2
    ds_end = jnp.zeros(STATE, F32)
    late, ds_mid = _scan_bwd(r, w, k2, v, kkn, b, do, states, sas, ds_end, None, "rwkv_scan_bwd_late", half, half)
    token = yield ("seam_1", ds_mid)
    scan_cts, ds_first = _scan_bwd(r, w, k2, v, kkn, b, do, states, sas, tied(ds_mid, token), late,
                                   "rwkv_scan_bwd_early", 0, half)
    dr_s, dw_s, dk_s, dv_s, dkkn_s, db_s = scan_cts
    token = yield ("seam_2", ds_first)
    prep_grads = _rwkv_prep_bwd(proj, tied(mix, token), prm,
                                (dr_s, dr_p, dw_s, dk_s, dk_p, dv_s, dv_p, dkkn_s, db_s, dg))
    dps, d_mix, d_w0, d_wdu, d_a0, d_wiu, d_wgu, d_kk, d_ka = prep_grads
    dq, dkv, dbias, dsink = _attn_bwd(proj, bias, sinks, dcat)
    d_relb = _bias_table_bwd(dbias.reshape(N_Q_HEADS, N_REL), onehot).T
    dproj = _assemble_dproj(dq, dkv, dps, mix)
    d_win = _matmul(h1, dproj, "tn", "d_win", m=D_MODEL, n=D_IN, k=SEQ, tm=D_MODEL, tn=640,
                    out=((N_CHIPS, D_MODEL, 640), _stacked(D_MODEL, 640, lambda i, j: (j, 0, 0))))
    token = yield ("grads_b", d_win)
    dh1 = _matmul_nt_shards(dproj, win_st, "d_h1", m=SEQ, n=D_MODEL, tm=1024, tn=D_MODEL,
                            a_spec=pl.BlockSpec((1024, D_IN), lambda i, j: (i, 0)),
                            a_piece=lambda ref, s: ref[:, s * 640:(s + 1) * 640])
    grad_x, d_g1 = _first_bwd(x, dx2, dh1, tied(sm["norm_mix_pre"], token))

    grads = {
        "norm_mix_pre": d_g1, "norm_mix_post": d_g2, "norm_ffn_pre": d_g3, "norm_ffn_post": d_g4,
        "w_in": d_win, "rel_bias": d_relb, "sinks": dsink[:, 0].reshape(1, N_Q_HEADS),
        "rwkv_shift_mix": d_mix, "w0": d_w0, "w_decay_up": d_wdu[:LORA_DECAY], "a0": d_a0,
        "w_iclr_up": d_wiu[LORA_DECAY:], "w_gate_up": d_wgu, "k_k": d_kk, "k_a": d_ka,
        "r_k": d_rk.reshape(1, N_Q_HEADS, HEAD_DIM), "ln_x_g": d_lng, "ln_x_b": d_lnb,
        "w_out": d_wout, "w_ffn_up": d_wup, "conv_w": d_convw, "conv_b": d_convb, "w_ffn_down": d_wdown,
    }
    return loss, grad_x, grads


def _place():
    x, y, c = lax.axis_index("x"), lax.axis_index("y"), lax.axis_index("c")
    chips = [(1 - x, y), (x, 1 - y), (1 - x, 1 - y)]
    return x, y, c, chips


def _remote(src, dst, sems, idx, to):
    return pltpu.make_async_remote_copy(src_ref=src, dst_ref=dst, send_sem=sems[0].at[idx], recv_sem=sems[1].at[idx],
                                        device_id=to, device_id_type=MESH)


ROW_ALIGN = 16


def _half(c, rows):
    return pl.ds(pl.multiple_of(c * (rows // 2), ROW_ALIGN), rows // 2)


def _gather_weights(big, small):
    nb, ns = len(big), len(small)

    def body(*refs):
        ins, outs = refs[:nb + ns], refs[nb + ns:2 * (nb + ns)]
        ici, d2d, sml, loc = refs[2 * (nb + ns):2 * (nb + ns) + 2], refs[-5:-3], refs[-3:-1], refs[-1]
        x, y, c, chips = _place()
        me = 2 * x + y
        sib = (x, y, 1 - c)
        local = [pltpu.make_async_copy(ins[a], outs[a].at[me], loc.at[a]) for a in range(nb + ns)]
        for cp in local:
            cp.start()
        sends = []
        for a in range(nb):
            rows = _half(c, big[a].shape[0])
            for kk, chip in enumerate(chips):
                sends.append(_remote(ins[a].at[rows], outs[a].at[me, rows], ici, a * 3 + kk, (*chip, c)))
        for a in range(ns):
            for kk, chip in enumerate(chips):
                sends.append(_remote(ins[nb + a], outs[nb + a].at[me], sml, a * 3 + kk, (*chip, c)))
        for cp in sends:
            cp.start()
        passed = []
        for a in range(nb):
            rows = _half(c, big[a].shape[0])
            for kk, (px, py) in enumerate(chips):
                got = outs[a].at[2 * px + py, rows]
                _remote(got, got, ici, a * 3 + kk, sib).wait_recv()
                fwd = _remote(got, got, d2d, a * 3 + kk, sib)
                fwd.start()
                passed.append(fwd)
        for a in range(nb):
            other = _half(1 - c, big[a].shape[0])
            for kk, (px, py) in enumerate(chips):
                land = outs[a].at[2 * px + py, other]
                _remote(land, land, d2d, a * 3 + kk, sib).wait_recv()
        for a in range(ns):
            for kk, (px, py) in enumerate(chips):
                land = outs[nb + a].at[2 * px + py]
                _remote(land, land, sml, a * 3 + kk, sib).wait_recv()
        for cp in sends + passed:
            cp.wait_send()
        for cp in local:
            cp.wait()

    arrs = list(big) + list(small)
    in_vmem = pl.BlockSpec(memory_space=pltpu.VMEM)
    return pl.pallas_call(
        body, name="gather_weights",
        in_specs=[in_vmem] * len(arrs), out_specs=[in_vmem] * len(arrs),
        out_shape=[jax.ShapeDtypeStruct((N_CHIPS,) + t.shape, t.dtype) for t in arrs],
        scratch_shapes=[pltpu.SemaphoreType.DMA((3 * nb,)), pltpu.SemaphoreType.DMA((3 * nb,)),
                        pltpu.SemaphoreType.DMA((3 * nb,)), pltpu.SemaphoreType.DMA((3 * nb,)),
                        pltpu.SemaphoreType.DMA((3 * ns,)), pltpu.SemaphoreType.DMA((3 * ns,)),
                        pltpu.SemaphoreType.DMA((nb + ns,))],
        compiler_params=pltpu.CompilerParams(has_side_effects=True, vmem_limit_bytes=VMEM_LIMIT),
    )(*arrs)


HBM = pl.BlockSpec(memory_space=pltpu.HBM)
SEM = pl.BlockSpec(memory_space=pltpu.SEMAPHORE)
EFFECT = pltpu.SideEffectType.DATAFLOW_SIDE_EFFECTING


def _copies_start(name, bufs, plan, n, partners=None):
    nb = len(bufs)

    def body(*refs):
        ins, sems, token = refs[:nb], refs[nb:nb + 2 * n], refs[-1]
        if partners is not None:
            barrier = pltpu.get_barrier_semaphore()
            peers = partners[1]()
            for peer in peers:
                pl.semaphore_signal(barrier, inc=1, device_id=peer, device_id_type=MESH)
            pl.semaphore_wait(barrier, len(peers))
        for kk, (src, dst, dev) in enumerate(plan(ins)):
            pltpu.make_async_remote_copy(src_ref=src, dst_ref=dst, send_sem=sems[2 * kk], recv_sem=sems[2 * kk + 1],
                                         device_id=dev, device_id_type=MESH).start()
        token[...] = jnp.zeros_like(token)

    outs = pl.pallas_call(
        body, name=name,
        out_shape=tuple([pltpu.SemaphoreType.DMA(())] * (2 * n) + [pltpu.HBM(t.shape, t.dtype) for t in bufs]
                        + [jax.ShapeDtypeStruct((8, LANES), F32)]),
        in_specs=[HBM] * nb,
        out_specs=tuple([SEM] * (2 * n) + [HBM] * nb + [pl.BlockSpec(memory_space=pltpu.VMEM)]),
        input_output_aliases={t: 2 * n + t for t in range(nb)},
        compiler_params=pltpu.CompilerParams(has_side_effects=EFFECT,
                                             collective_id=None if partners is None else partners[0]),
    )(*[pltpu.with_memory_space_constraint(t, pltpu.HBM) for t in bufs])
    return outs[:2 * n], outs[2 * n:2 * n + nb], outs[-1]


def _copies_wait(name, sems, bufs, plan, n, after):
    nb = len(bufs)
    after = list(after) if isinstance(after, (list, tuple)) else [after]

    def body(*refs):
        ins, sem_refs = refs[:nb], refs[nb:nb + 2 * n]
        for kk, (src, dst, dev) in enumerate(plan(ins)):
            cp = pltpu.make_async_remote_copy(src_ref=src, dst_ref=dst, send_sem=sem_refs[2 * kk],
                                              recv_sem=sem_refs[2 * kk + 1], device_id=dev, device_id_type=MESH)
            cp.wait_send()
            cp.wait_recv()

    return pl.pallas_call(
        body, name=name,
        out_shape=tuple(pltpu.HBM(t.shape, t.dtype) for t in bufs),
        in_specs=[HBM] * nb + [SEM] * (2 * n) + [ANY] * len(after),
        out_specs=tuple([HBM] * nb),
        input_output_aliases={t: t for t in range(nb)},
        compiler_params=pltpu.CompilerParams(has_side_effects=EFFECT),
    )(*bufs, *sems, *after)


def _plan_gather(n_w):
    def plan(refs):
        x, y, c, chips = _place()
        me = 2 * x + y
        return [(refs[a], refs[n_w + a].at[me], (*chip, c)) for a in range(n_w) for chip in chips + [(x, y)]]
    return plan


def _plan_pair_halves(n_g, rows):
    def plan(refs):
        x, y, c, _ = _place()
        return [(refs[a].at[:, _half(1 - c, rows[a])], refs[n_g + a], (x, y, 1 - c)) for a in range(n_g)]
    return plan


def _plan_chip_parts(n_g):
    def plan(refs):
        x, y, c, chips = _place()
        me = 2 * x + y
        return [(refs[a].at[2 * px + py], refs[n_g + a].at[me], (px, py, c))
                for a in range(n_g) for (px, py) in chips]
    return plan


def _plan_pair_fill(n_g, rows):
    def plan(refs):
        x, y, c, _ = _place()
        return [(refs[a].at[_half(c, rows[a])], refs[a].at[_half(c, rows[a])], (x, y, 1 - c)) for a in range(n_g)]
    return plan


def _pair_add(g, got, name):
    _, rows, cols = g.shape
    hr = rows // 2
    tr = min(hr, 512)
    nb = hr // tr

    def body(g_ref, got_ref, p_ref, own_ref):
        val = (g_ref[...] + got_ref[...]).astype(BF16)
        p_ref[...] = val

        @pl.when(pl.program_id(1) == 2 * lax.axis_index("x") + lax.axis_index("y"))
        def _():
            own_ref[...] = val

    def mine(i, s):
        return (2 * lax.axis_index("x") + lax.axis_index("y"), i, 0)

    return pl.pallas_call(
        body, name=name, grid=(nb, N_CHIPS),
        in_specs=[pl.BlockSpec((None, tr, cols), lambda i, s: (s, lax.axis_index("c") * nb + i, 0)),
                  pl.BlockSpec((None, tr, cols), lambda i, s: (s, i, 0))],
        out_specs=[pl.BlockSpec((None, tr, cols), lambda i, s: (s, i, 0)), pl.BlockSpec((None, tr, cols), mine)],
        out_shape=[jax.ShapeDtypeStruct((N_CHIPS, hr, cols), BF16)] * 2,
        compiler_params=_cp(("parallel", "arbitrary")),
    )(g, got)


def _chip_sum(parts, name):
    _, hr, cols = parts.shape
    tr = min(hr, 256)
    nb = hr // tr

    def body(t_ref, o_ref):
        part = [t_ref[s].astype(F32) for s in range(N_CHIPS)]
        o_ref[...] = ((part[0] + part[1]) + part[2]) + part[3]

    return pl.pallas_call(
        body, name=name, grid=(nb,),
        in_specs=[pl.BlockSpec((N_CHIPS, tr, cols), lambda i: (0, i, 0))],
        out_specs=pl.BlockSpec((tr, cols), lambda i: (lax.axis_index("c") * nb + i, 0)),
        out_shape=jax.ShapeDtypeStruct((2 * hr, cols), F32),
        compiler_params=_cp(("parallel",)),
    )(parts)


class _Reduction:
    def __init__(self, tag, rows, first_id):
        self.tag, self.n, self.rows, self.first_id = tag, len(rows), rows, first_id
        self.plans = (_plan_pair_halves(self.n, rows), _plan_chip_parts(self.n), _plan_pair_fill(self.n, rows))
        self.flight = None

    def _name(self, what):
        return f"grad_{self.tag}_{what}"

    @staticmethod
    def _sibling():
        x, y, c, _ = _place()
        return [(x, y, 1 - c)]

    @staticmethod
    def _same_core_elsewhere():
        x, y, c, chips = _place()
        return [(*chip, c) for chip in chips]

    def start(self, gs):
        gots = [lax.empty((N_CHIPS, t.shape[1] // 2, t.shape[2]), F32) for t in gs]
        self.flight = _copies_start(self._name("pair_start"), list(gs) + gots, self.plans[0], self.n,
                                    (self.first_id, self._sibling))
        return self.flight[2]

    def after_pair(self, after):
        sems, bufs, _ = self.flight
        out = _copies_wait(self._name("pair_wait"), sems, bufs, self.plans[0], self.n, after)
        sums = [_pair_add(g, got, self._name(f"pair_add_{i}"))
                for i, (g, got) in enumerate(zip(out[:self.n], out[self.n:]))]
        self.flight = _copies_start(self._name("chip_start"), [p for p, _ in sums] + [own for _, own in sums],
                                    self.plans[1], 3 * self.n, (self.first_id + 1, self._same_core_elsewhere))
        return self.flight[2]

    def after_chips(self, after):
        sems, bufs, _ = self.flight
        out = _copies_wait(self._name("chip_wait"), sems, bufs, self.plans[1], 3 * self.n, after)
        fulls = [_chip_sum(t, self._name(f"chip_sum_{i}")) for i, t in enumerate(out[self.n:])]
        self.flight = _copies_start(self._name("fill_start"), fulls, self.plans[2], self.n,
                                    (self.first_id + 2, self._sibling))
        return self.flight[2]

    def finish(self, after):
        sems, bufs, _ = self.flight
        return _copies_wait(self._name("fill_wait"), sems, bufs, self.plans[2], self.n, after)


def _adamw_math(w, g, m, v):
    nm = ADAM_B1 * m + (1.0 - ADAM_B1) * g
    nv = ADAM_B2 * v + (1.0 - ADAM_B2) * (g * g)
    m_hat = nm / (1.0 - ADAM_B1 ** ADAM_STEP)
    v_hat = nv / (1.0 - ADAM_B2 ** ADAM_STEP)
    return -ADAM_LR * (m_hat / (jnp.sqrt(v_hat) + ADAM_EPS) + ADAM_WD * w), nm, nv


def _adamw(w, g, m, v, name, tr):
    r, cdim = w.shape

    def body(w_ref, g_ref, m_ref, v_ref, d_ref, nm_ref, nv_ref):
        d_ref[...], nm_ref[...], nv_ref[...] = _adamw_math(w_ref[...], g_ref[...], m_ref[...], v_ref[...])

    return pl.pallas_call(
        body, name=name, grid=(r // tr,), in_specs=[_rows(tr, cdim)] * 4, out_specs=[_rows(tr, cdim)] * 3,
        out_shape=[jax.ShapeDtypeStruct((r, cdim), F32)] * 3, compiler_params=_cp(("parallel",)),
    )(w, g, m, v)


def _adamw_small(w, parts, m, v, shapes):
    n_rows = w.shape[0]

    def scatter(src, outs):
        row = 0
        for (rows, cols), out in zip(shapes, outs):
            if cols == LANES:
                out[...] = src[row:row + rows, :]
            elif cols > LANES:
                per = cols // LANES
                for r in range(rows):
                    for cb in range(per):
                        out[r:r + 1, cb * LANES:(cb + 1) * LANES] = src[row + r * per + cb:row + r * per + cb + 1, :]
            else:
                per = LANES // cols
                for r in range(rows):
                    out[r:r + 1, :] = src[row + r // per:row + r // per + 1, (r % per) * cols:(r % per + 1) * cols]
            row += -(-rows * cols // LANES)

    def body(w_ref, p_ref, m_ref, v_ref, *rest):
        outs, scr = rest[:-4], rest[-4:]
        g = p_ref[0]
        for dev in range(1, N_DEV):
            g = g + p_ref[dev]
        scr[3][...] = g
        scr[0][...], scr[1][...], scr[2][...] = _adamw_math(w_ref[...], g, m_ref[...], v_ref[...])
        n = len(shapes)
        for kind in range(4):
            scatter(scr[kind], outs[kind * n:(kind + 1) * n])

    outs = pl.pallas_call(
        body, name="adamw_small", grid=(1,),
        in_specs=[_const(w.shape), _const(parts.shape), _const(w.shape), _const(w.shape)],
        out_specs=[_const(s) for s in shapes] * 4, out_shape=[jax.ShapeDtypeStruct(s, F32) for s in shapes] * 4,
        scratch_shapes=[pltpu.VMEM((n_rows, LANES), F32)] * 4,
        compiler_params=_cp(("arbitrary",)),
    )(w, parts, m, v)
    n = len(shapes)
    return [outs[kind * n:(kind + 1) * n] for kind in range(4)]


REPLICATED = (("norm_mix_pre", 1024), ("norm_mix_post", 1024), ("norm_ffn_pre", 1024), ("norm_ffn_post", 1024),
              ("rel_bias", 256), ("sinks", 8), ("rwkv_shift_mix", 1792), ("w0", 512), ("a0", 512), ("k_k", 512),
              ("k_a", 512), ("r_k", 512), ("ln_x_g", 512), ("ln_x_b", 512), ("conv_b", 8192))
SMALL_SHARDED = (("w_decay_up", LORA_DECAY, D_RWKV), ("w_iclr_up", LORA_ICLR, D_RWKV),
                 ("w_gate_up", LORA_GATE, D_RWKV), ("conv_w", 3, 2 * D_FF))
BIG = (("w_in", D_MODEL, 640), ("w_out", 256, D_MODEL), ("w_ffn_up", D_MODEL, 2048), ("w_ffn_down", 1024, D_MODEL))
PACK_ALIGN = 8 * LANES


def _pack(pieces):
    flat = []
    for t in pieces:
        t = t.reshape(-1)
        pad = (-t.shape[0]) % LANES
        flat.append(jnp.pad(t, (0, pad)) if pad else t)
    flat = jnp.concatenate(flat)
    pad = (-flat.shape[0]) % PACK_ALIGN
    return jnp.pad(flat, (0, pad)).reshape(-1, LANES)


def kernel(x, norm_mix_pre, norm_mix_post, norm_ffn_pre, norm_ffn_post, w_in, rel_bias, sinks, rwkv_shift_mix, w0, w_decay_up, a0, w_iclr_up, w_gate_up, k_k, k_a, r_k, ln_x_g, ln_x_b, w_out, w_ffn_up, conv_w, conv_b, w_ffn_down, loss_target, m_norm_mix_pre, m_norm_mix_post, m_norm_ffn_pre, m_norm_ffn_post, m_w_in, m_rel_bias, m_sinks, m_rwkv_shift_mix, m_w0, m_w_decay_up, m_a0, m_w_iclr_up, m_w_gate_up, m_k_k, m_k_a, m_r_k, m_ln_x_g, m_ln_x_b, m_w_out, m_w_ffn_up, m_conv_w, m_conv_b, m_w_ffn_down, v_norm_mix_pre, v_norm_mix_post, v_norm_ffn_pre, v_norm_ffn_post, v_w_in, v_rel_bias, v_sinks, v_rwkv_shift_mix, v_w0, v_w_decay_up, v_a0, v_w_iclr_up, v_w_gate_up, v_k_k, v_k_a, v_r_k, v_ln_x_g, v_ln_x_b, v_w_out, v_w_ffn_up, v_conv_w, v_conv_b, v_w_ffn_down):
    given = dict(locals())
    names = [n for n, _ in REPLICATED] + [n for n, _, _ in SMALL_SHARDED] + [n for n, _, _ in BIG]
    order = ["norm_mix_pre", "norm_mix_post", "norm_ffn_pre", "norm_ffn_post", "w_in", "rel_bias", "sinks",
             "rwkv_shift_mix", "w0", "w_decay_up", "a0", "w_iclr_up", "w_gate_up", "k_k", "k_a", "r_k", "ln_x_g",
             "ln_x_b", "w_out", "w_ffn_up", "conv_w", "conv_b", "w_ffn_down"]
    assert sorted(names) == sorted(order)

    big_sh = {n: given[n].reshape(a, b).astype(BF16) for n, a, b in BIG}
    small_sh = [given[n].reshape(r, c // N_CHIPS) for n, r, c in SMALL_SHARDED]
    gathered = _gather_weights([big_sh["w_in"]], small_sh)
    rest = ("w_out", "w_ffn_up", "w_ffn_down")
    win_st, rest_sh = lax.optimization_barrier((gathered[0], [big_sh[n] for n in rest]))
    sm = {n: given[n] for n, _ in REPLICATED}
    sm["r_k"] = r_k.reshape(N_Q_HEADS, HEAD_DIM)
    for (n, r, c), st in zip(SMALL_SHARDED, gathered[1:]):
        sm[n] = st.transpose(1, 0, 2).reshape(r, c)

    lands = [lax.empty((N_CHIPS,) + t.shape, BF16) for t in rest_sh]
    plan_w = _plan_gather(len(rest))
    n_w = N_CHIPS * len(rest)
    w_sems, w_bufs, token = _copies_start("gather_rest_start", rest_sh + lands, plan_w, n_w)
    sm["norm_mix_pre"] = norm_mix_pre + token[0:1, 0:1]

    def on_rest_weights(after):
        out = _copies_wait("gather_rest_wait", w_sems, w_bufs, plan_w, n_w, after)
        wout_st, wup_st, wdown_st = out[3:]
        return wout_st.reshape(D_MODEL, D_MODEL), wup_st, wdown_st.reshape(D_FF, D_MODEL)

    red_a = _Reduction("a", (1024, D_MODEL, 256), first_id=0)
    red_b = _Reduction("b", (D_MODEL,), first_id=3)

    def on_grads_a(gs):
        d_wdown, d_wup, d_wout = gs
        return red_a.start([d_wdown.reshape(N_CHIPS, 1024, D_MODEL), d_wup, d_wout.reshape(N_CHIPS, 256, D_MODEL)])

    handlers = {"rest_weights": on_rest_weights, "grads_a": on_grads_a, "seam_1": red_a.after_pair,
                "seam_2": red_a.after_chips, "grads_b": lambda g: red_b.start([g])}
    steps = _local_step(x[0], loss_target[0], sm, win_st)
    kind, payload = next(steps)
    while True:
        try:
            kind, payload = steps.send(handlers[kind](payload))
        except StopIteration as done:
            loss, grad_x, grads = done.value
            break

    small_names = [n for n, _ in REPLICATED] + [n for n, _, _ in SMALL_SHARDED]

    def shard_cols(t, s):
        return t[:, s * (t.shape[1] // N_CHIPS):(s + 1) * (t.shape[1] // N_CHIPS)]

    for_chip = jnp.stack([_pack([loss[0]] + [grads[n] for n, _ in REPLICATED]
                                + [shard_cols(grads[n], s) for n, _, _ in SMALL_SHARDED]) for s in range(N_CHIPS)])
    land = lax.empty((N_DEV,) + for_chip.shape[1:], F32)

    def plan_small(refs):
        x, y, c, _ = _place()
        out = []
        for rel in range(N_DEV):
            px, py, pc = x ^ (rel >> 2), y ^ ((rel >> 1) & 1), c ^ (rel & 1)
            out.append((refs[0].at[2 * px + py], refs[1].at[4 * x + 2 * y + c], (px, py, pc)))
        return out

    s_sems, s_bufs, s_token = _copies_start("grad_small_start", [for_chip, land], plan_small, N_DEV)

    red_b.after_pair([grad_x, s_token])
    g_out = {}
    g_out["w_ffn_down"], g_out["w_ffn_up"], g_out["w_out"] = red_a.finish(grad_x)

    delta, new_m, new_v = {}, {}, {}

    def update(n, a, b):
        delta[n], new_m[n], new_v[n] = _adamw(given[n].reshape(a, b), g_out[n], given["m_" + n].reshape(a, b),
                                              given["v_" + n].reshape(a, b), "adamw_" + n, 256)

    for n, a, b in BIG[1:]:
        update(n, a, b)
    done = [delta[n] for n, _, _ in BIG[1:]]
    red_b.after_chips(done)
    parts = _copies_wait("grad_small_wait", s_sems, s_bufs, plan_small, N_DEV, done)[1]
    no_param = jnp.zeros((LANES,), F32)
    packs = [_pack([no_param] + [given[pre + n] for n in small_names]) for pre in ("", "m_", "v_")]

    def piece_shape(n):
        shape = given[n].shape
        rows, cols = int(np.prod(shape[:-1])), shape[-1]
        whole = cols % LANES == 0 or (LANES % cols == 0 and (rows * cols) % LANES == 0 and cols >= HEAD_DIM)
        return (rows, cols) if whole else (-(-rows * cols // LANES), LANES)

    shapes = [(1, LANES)] + [piece_shape(n) for n in small_names]
    upd = _adamw_small(packs[0], parts, packs[1], packs[2], shapes)
    loss = upd[3][0][0, 0]
    for i, n in enumerate(small_names):
        shape = given[n].shape
        size = int(np.prod(shape))
        delta[n], new_m[n], new_v[n], g_out[n] = (u[1 + i].reshape(-1)[:size].reshape(shape) for u in upd)
    g_out["w_in"], = red_b.finish(upd[0][0])
    update(*BIG[0])

    def shaped(d):
        return [d[n].reshape(given[n].shape) for n in order]

    return (loss, grad_x.reshape(x.shape), *shaped(g_out), *shaped(delta), *shaped(new_m), *shaped(new_v))
```

```python
import math

import numpy as np
import jax
import jax.numpy as jnp
from jax import lax
from jax.experimental import pallas as pl
from jax.experimental.pallas import tpu as pltpu

F32 = jnp.float32
BF16 = jnp.bfloat16
MESH = pl.DeviceIdType.MESH

SEQ = 2048
D_MODEL = 1024
HEAD_DIM = 64
D_ATTN = 512
D_RWKV = 512
D_KV = 128
N_Q_HEADS = 8
N_KV_HEADS = 2
Q_PER_KV = 4
BLOCK = 128
N_BUCKETS = 32
MAX_DISTANCE = 128
LORA_DECAY = 64
LORA_ICLR = 64
LORA_GATE = 128
RWKV_COLS = 3 * D_RWKV + LORA_DECAY + LORA_ICLR + LORA_GATE
P_OFF = D_ATTN + 2 * D_KV
D_IN = P_OFF + RWKV_COLS
D_FF = 4096
NORM_EPS = 1e-6
GN_EPS = 64e-5
NEG_INF = -1e30
N_CHIPS = 4
N_DEV = 8
HEAD_SHIFT = HEAD_DIM.bit_length() - 1
BLOCK_SHIFT = BLOCK.bit_length() - 1

ADAM_LR = 0.001
ADAM_B1 = 0.9
ADAM_B2 = 0.999
ADAM_EPS = 1e-08
ADAM_WD = 0.01
ADAM_STEP = 10

VMEM_LIMIT = 52 * 1024 * 1024
LANES = 128


def _cp(sem=None, vmem=VMEM_LIMIT):
    kw = dict(vmem_limit_bytes=vmem)
    if sem is not None:
        kw["dimension_semantics"] = sem
    return pltpu.CompilerParams(**kw)


def _rows(tr, nc):
    return pl.BlockSpec((tr, nc), lambda i: (i, 0))


def _const(shape):
    return pl.BlockSpec(shape, lambda *_: (0,) * len(shape))


ANY = pl.BlockSpec(memory_space=pl.ANY)


def _split(x, n):
    parts = []
    for _ in range(n - 1):
        h = x.astype(BF16)
        parts.append(h)
        x = x - h.astype(F32)
    parts.append(x.astype(BF16))
    return parts


NN = (((1,), (0,)), ((), ()))
NT = (((1,), (1,)), ((), ()))
TN = (((0,), (0,)), ((), ()))


def _dot(a, b, dn=NN):
    return lax.dot_general(a, b, dn, preferred_element_type=F32)


def _dot_ind(x, ind_bf16, n=3):
    acc = None
    for part in _split(x, n):
        t = _dot(part, ind_bf16)
        acc = t if acc is None else acc + t
    return acc


def _head_ones(n):
    r = lax.broadcasted_iota(jnp.int32, (n, n), 0) >> HEAD_SHIFT
    c = lax.broadcasted_iota(jnp.int32, (n, n), 1) >> HEAD_SHIFT
    return jnp.where(r == c, 1.0, 0.0).astype(BF16)


def _matmul(a, b, mode, name, *, m, n, k, tm, tn, a_spec=None, b_spec=None, out=None):
    dn = {"nn": NN, "nt": NT, "tn": TN}[mode]

    def body(a_ref, b_ref, o_ref):
        o_ref[...] = _dot(a_ref[...], b_ref[...], dn)

    if a_spec is None:
        a_spec = pl.BlockSpec((k, tm), lambda i, j: (0, i)) if mode == "tn" else pl.BlockSpec((tm, k), lambda i, j: (i, 0))
    if b_spec is None:
        b_spec = pl.BlockSpec((tn, k), lambda i, j: (j, 0)) if mode == "nt" else pl.BlockSpec((k, tn), lambda i, j: (0, j))
    return pl.pallas_call(
        body, name=name, grid=(m // tm, n // tn),
        in_specs=[a_spec, b_spec],
        out_specs=pl.BlockSpec((tm, tn), lambda i, j: (i, j)) if out is None else out[1],
        out_shape=jax.ShapeDtypeStruct((m, n) if out is None else out[0], F32),
        compiler_params=_cp(("parallel", "parallel")),
    )(a, b)


def _matmul_nt_shards(a, b_st, name, *, m, n, tm, tn, a_spec, a_piece):
    ks = b_st.shape[2]

    def body(a_ref, b_ref, o_ref):
        acc = _dot(a_piece(a_ref, 0), b_ref[0], NT)
        for s in range(1, N_CHIPS):
            acc = acc + _dot(a_piece(a_ref, s), b_ref[s], NT)
        o_ref[...] = acc

    return pl.pallas_call(
        body, name=name, grid=(m // tm, n // tn),
        in_specs=[a_spec, pl.BlockSpec((N_CHIPS, tn, ks), lambda i, j: (0, j, 0))],
        out_specs=pl.BlockSpec((tm, tn), lambda i, j: (i, j)),
        out_shape=jax.ShapeDtypeStruct((m, n), F32),
        compiler_params=_cp(("parallel", "parallel")),
    )(a, b_st)


def _rstd(x):
    return lax.rsqrt(jnp.mean(x * x, axis=-1, keepdims=True) + NORM_EPS)


def _rms_bwd(x, r, g, dy):
    gy = dy * g
    return r * gy - x * ((r * r * r) * (jnp.sum(x * gy, axis=-1, keepdims=True) / x.shape[-1]))


TR = 256
TRN = 512


def _norm_cast(x, g, name):
    def body(x_ref, g_ref, h_ref):
        x = x_ref[...]
        h_ref[...] = (x * _rstd(x) * g_ref[...]).astype(BF16)

    return pl.pallas_call(
        body, name=name, grid=(SEQ // TRN,),
        in_specs=[_rows(TRN, D_MODEL), _const((1, D_MODEL))],
        out_specs=_rows(TRN, D_MODEL),
        out_shape=jax.ShapeDtypeStruct((SEQ, D_MODEL), BF16),
        compiler_params=_cp(("parallel",)),
    )(x, g)


def _mix_norm(x, mix, g2, g3):
    def body(x_ref, mix_ref, g2_ref, g3_ref, x2_ref, h3_ref):
        mixv = mix_ref[...]
        x2 = x_ref[...] + mixv * _rstd(mixv) * g2_ref[...]
        x2_ref[...] = x2
        h3_ref[...] = (x2 * _rstd(x2) * g3_ref[...]).astype(BF16)

    return pl.pallas_call(
        body, name="mix_norm", grid=(SEQ // TRN,),
        in_specs=[_rows(TRN, D_MODEL), _rows(TRN, D_MODEL), _const((1, D_MODEL)), _const((1, D_MODEL))],
        out_specs=[_rows(TRN, D_MODEL), _rows(TRN, D_MODEL)],
        out_shape=[jax.ShapeDtypeStruct((SEQ, D_MODEL), F32), jax.ShapeDtypeStruct((SEQ, D_MODEL), BF16)],
        compiler_params=_cp(("parallel",)),
    )(x, mix, g2, g3)


def _loss_head(x2, f, g4, target):
    def body(x2_ref, f_ref, g4_ref, t_ref, loss_ref, dy_ref, df_ref, dg_ref):
        i = pl.program_id(0)
        f = f_ref[...]
        g4 = g4_ref[...]
        r = _rstd(f)
        e = x2_ref[...] + f * r * g4 - t_ref[...]
        dy = e * (1.0 / D_MODEL)
        dy_ref[...] = dy
        df_ref[...] = _rms_bwd(f, r, g4, dy).astype(BF16)
        part = 0.5 * jnp.sum(jnp.sum(e * e, axis=-1, keepdims=True), axis=0, keepdims=True) * (1.0 / D_MODEL)
        dg = jnp.sum(dy * f * r, axis=0, keepdims=True)

        @pl.when(i == 0)
        def _():
            loss_ref[...] = jnp.zeros_like(loss_ref)
            dg_ref[...] = jnp.zeros_like(dg_ref)

        loss_ref[...] += jnp.broadcast_to(part, loss_ref.shape)
        dg_ref[...] += dg

    return pl.pallas_call(
        body, name="loss_head", grid=(SEQ // TRN,),
        in_specs=[_rows(TRN, D_MODEL), _rows(TRN, D_MODEL), _const((1, D_MODEL)), _rows(TRN, D_MODEL)],
        out_specs=[_const((8, LANES)), _rows(TRN, D_MODEL), _rows(TRN, D_MODEL), _const((1, D_MODEL))],
        out_shape=[jax.ShapeDtypeStruct((8, LANES), F32), jax.ShapeDtypeStruct((SEQ, D_MODEL), F32),
                   jax.ShapeDtypeStruct((SEQ, D_MODEL), BF16), jax.ShapeDtypeStruct((1, D_MODEL), F32)],
        compiler_params=_cp(("arbitrary",)),
    )(x2, f, g4, target)


def _mid_bwd(x2, mix, dy, dh3, g2, g3):
    def body(x2_ref, mix_ref, dy_ref, dh3_ref, g2_ref, g3_ref, dx2_ref, dmix_ref, dg2_ref, dg3_ref):
        i = pl.program_id(0)
        x2 = x2_ref[...]
        mixv = mix_ref[...]
        dh3 = dh3_ref[...]
        r3 = _rstd(x2)
        dx2 = dy_ref[...] + _rms_bwd(x2, r3, g3_ref[...], dh3)
        dx2_ref[...] = dx2
        r2 = _rstd(mixv)
        dmix_ref[...] = _rms_bwd(mixv, r2, g2_ref[...], dx2).astype(BF16)

        @pl.when(i == 0)
        def _():
            dg2_ref[...] = jnp.zeros_like(dg2_ref)
            dg3_ref[...] = jnp.zeros_like(dg3_ref)

        dg3_ref[...] += jnp.sum(dh3 * x2 * r3, axis=0, keepdims=True)
        dg2_ref[...] += jnp.sum(dx2 * mixv * r2, axis=0, keepdims=True)

    return pl.pallas_call(
        body, name="mid_bwd", grid=(SEQ // TRN,),
        in_specs=[_rows(TRN, D_MODEL)] * 4 + [_const((1, D_MODEL))] * 2,
        out_specs=[_rows(TRN, D_MODEL), _rows(TRN, D_MODEL), _const((1, D_MODEL)), _const((1, D_MODEL))],
        out_shape=[jax.ShapeDtypeStruct((SEQ, D_MODEL), F32), jax.ShapeDtypeStruct((SEQ, D_MODEL), BF16),
                   jax.ShapeDtypeStruct((1, D_MODEL), F32), jax.ShapeDtypeStruct((1, D_MODEL), F32)],
        compiler_params=_cp(("arbitrary",)),
    )(x2, mix, dy, dh3, g2, g3)


def _first_bwd(x, dx2, dh1, g1):
    def body(x_ref, dx2_ref, dh1_ref, g1_ref, dx_ref, dg1_ref):
        i = pl.program_id(0)
        x = x_ref[...]
        dh1 = dh1_ref[...]
        r = _rstd(x)
        dx_ref[...] = dx2_ref[...] + _rms_bwd(x, r, g1_ref[...], dh1)

        @pl.when(i == 0)
        def _():
            dg1_ref[...] = jnp.zeros_like(dg1_ref)

        dg1_ref[...] += jnp.sum(dh1 * x * r, axis=0, keepdims=True)

    return pl.pallas_call(
        body, name="first_bwd", grid=(SEQ // TRN,),
        in_specs=[_rows(TRN, D_MODEL)] * 3 + [_const((1, D_MODEL))],
        out_specs=[_rows(TRN, D_MODEL), _const((1, D_MODEL))],
        out_shape=[jax.ShapeDtypeStruct((SEQ, D_MODEL), F32), jax.ShapeDtypeStruct((1, D_MODEL), F32)],
        compiler_params=_cp(("arbitrary",)),
    )(x, dx2, dh1, g1)


TC = 256
N_CB = D_FF // TC
GELU_C = math.sqrt(2.0 / math.pi)


def _shift_down(u, s):
    rolled = pltpu.roll(u, s, 0)
    row = lax.broadcasted_iota(jnp.int32, u.shape, 0)
    return jnp.where(row >= s, rolled, 0.0)


def _shift_up(u, s):
    n = u.shape[0]
    rolled = pltpu.roll(u, n - s, 0)
    row = lax.broadcasted_iota(jnp.int32, u.shape, 0)
    return jnp.where(row < n - s, rolled, 0.0)


def _conv3(u, w, b):
    return b + w[0:1] * _shift_down(u, 2) + w[1:2] * _shift_down(u, 1) + w[2:3] * u


def _gelu_and_grad(x):
    inner = GELU_C * (x + 0.044715 * (x * x * x))
    t = jnp.tanh(inner)
    gelu = 0.5 * x * (1.0 + t)
    dgelu = 0.5 * (1.0 + t) + 0.5 * x * (1.0 - t * t) * (GELU_C * (1.0 + 3 * 0.044715 * (x * x)))
    return gelu, dgelu


def _ffn_specs():
    col = lambda off: pl.BlockSpec((SEQ, TC), lambda *g: (0, g[-1] + off))
    w = lambda off: pl.BlockSpec((3, TC), lambda *g: (0, g[-1] + off))
    b = lambda off: pl.BlockSpec((1, TC), lambda *g: (0, g[-1] + off))
    return col, w, b


def _ffn_up_act(h3, wup_st, conv_w, conv_b):
    col, w, b = _ffn_specs()
    per_shard = wup_st.shape[2] // TC

    def body(h_ref, upg_ref, upv_ref, wg_ref, wv_ref, bg_ref, bv_ref, ug_ref, uv_ref, gate_ref, val_ref, act_ref):
        h = h_ref[...]
        ug = _dot(h, upg_ref[...])
        uv = _dot(h, upv_ref[...])
        ug_ref[...] = ug
        uv_ref[...] = uv
        gate = _conv3(ug, wg_ref[...], bg_ref[...])
        val = _conv3(uv, wv_ref[...], bv_ref[...])
        gate_ref[...] = gate
        val_ref[...] = val
        act_ref[...] = (_gelu_and_grad(gate)[0] * val).astype(BF16)

    return pl.pallas_call(
        body, name="ffn_up_act", grid=(N_CB,),
        in_specs=[_const((SEQ, D_MODEL)),
                  pl.BlockSpec((None, D_MODEL, TC), lambda j: (j // per_shard, 0, j % per_shard)),
                  pl.BlockSpec((None, D_MODEL, TC), lambda j: (2 + j // per_shard, 0, j % per_shard)),
                  w(0), w(N_CB), b(0), b(N_CB)],
        out_specs=[col(0)] * 5,
        out_shape=[jax.ShapeDtypeStruct((SEQ, D_FF), F32)] * 4 + [jax.ShapeDtypeStruct((SEQ, D_FF), BF16)],
        compiler_params=_cp(("parallel",)),
    )(h3, wup_st, wup_st, conv_w, conv_w, conv_b, conv_b)


def _ffn_act_bwd(u_gate, u_val, gate, val, df, wdown, conv_w):
    col, w, _ = _ffn_specs()
    both = lambda rows: pl.BlockSpec((2, rows, TC), lambda j: (0, 0, j))

    def body(ug_ref, uv_ref, gate_ref, val_ref, df_ref, wd_ref, wg_ref, wv_ref, du_ref, dw_ref, db_ref):
        da = _dot(df_ref[...], wd_ref[...], NT)
        gelu, dgelu = _gelu_and_grad(gate_ref[...])
        halves = ((da * val_ref[...] * dgelu, ug_ref, wg_ref[...]), (da * gelu, uv_ref, wv_ref[...]))
        for h, (duc, u_ref, wh) in enumerate(halves):
            uh = u_ref[...]
            up1, up2 = _shift_up(duc, 1), _shift_up(duc, 2)
            du_ref[h] = (wh[2:3] * duc + wh[1:2] * up1 + wh[0:1] * up2).astype(BF16)
            db_ref[h] = jnp.sum(duc, axis=0, keepdims=True)
            dw_ref[h] = jnp.concatenate(
                [jnp.sum(up2 * uh, axis=0, keepdims=True), jnp.sum(up1 * uh, axis=0, keepdims=True),
                 jnp.sum(duc * uh, axis=0, keepdims=True)], axis=0)

    return pl.pallas_call(
        body, name="ffn_act_bwd", grid=(N_CB,),
        in_specs=[col(0)] * 4 + [_const((SEQ, D_MODEL)), pl.BlockSpec((TC, D_MODEL), lambda j: (j, 0)), w(0), w(N_CB)],
        out_specs=[both(SEQ), both(3), both(1)],
        out_shape=[jax.ShapeDtypeStruct((2, SEQ, D_FF), BF16), jax.ShapeDtypeStruct((2, 3, D_FF), F32),
                   jax.ShapeDtypeStruct((2, 1, D_FF), F32)],
        compiler_params=_cp(("parallel",)),
    )(u_gate, u_val, gate, val, df, wdown, conv_w, conv_w)


def _t5_onehot():
    rel = (np.arange(BLOCK)[:, None] + BLOCK) - np.arange(2 * BLOCK)[None, :]
    n = np.maximum(rel, 0)
    max_exact = N_BUCKETS // 2
    large = max_exact + (np.log(np.maximum(n, 1).astype(np.float32) / np.float32(max_exact))
                         / np.float32(math.log(MAX_DISTANCE / max_exact))
                         * np.float32(N_BUCKETS - max_exact)).astype(np.int32)
    large = np.minimum(large, N_BUCKETS - 1)
    bucket = np.where(n < max_exact, n, large).reshape(-1)
    return (bucket[None, :] == np.arange(N_BUCKETS)[:, None]).astype(np.float32)


N_REL = BLOCK * 2 * BLOCK


def _bias_table(rel_bias_t, onehot):
    def body(rb_ref, oh_ref, o_ref):
        o_ref[...] = _dot_ind(rb_ref[...], oh_ref[...])

    return pl.pallas_call(
        body, name="bias_table", grid=(1,),
        in_specs=[_const((N_Q_HEADS, N_BUCKETS)), _const((N_BUCKETS, N_REL))],
        out_specs=_const((N_Q_HEADS, N_REL)),
        out_shape=jax.ShapeDtypeStruct((N_Q_HEADS, N_REL), F32),
        compiler_params=_cp(("arbitrary",)),
    )(rel_bias_t, onehot)


def _bias_table_bwd(dbias, onehot):
    def body(db_ref, oh_ref, o_ref):
        acc = None
        for part in _split(db_ref[...], 3):
            t = _dot(part, oh_ref[...], NT)
            acc = t if acc is None else acc + t
        o_ref[...] = acc

    return pl.pallas_call(
        body, name="bias_table_bwd", grid=(1,),
        in_specs=[_const((N_Q_HEADS, N_REL)), _const((N_BUCKETS, N_REL))],
        out_specs=_const((N_Q_HEADS, N_BUCKETS)),
        out_shape=jax.ShapeDtypeStruct((N_Q_HEADS, N_BUCKETS), F32),
        compiler_params=_cp(("arbitrary",)),
    )(dbias, onehot)


def _attn_pieces(n, q, kvp, kvc, bias_ref, sinks_ref, hk):
    qi = lax.broadcasted_iota(jnp.int32, (BLOCK, 2 * BLOCK), 0)
    kj = lax.broadcasted_iota(jnp.int32, (BLOCK, 2 * BLOCK), 1)
    rel = qi + BLOCK - kj
    first_key = jnp.where(n > 0, 0, BLOCK)
    ok = jnp.where(rel >= 0, jnp.where(rel < BLOCK, jnp.where(kj >= first_key, 1.0, 0.0), 0.0), 0.0)
    ok4 = jnp.concatenate([ok] * Q_PER_KV, axis=0) > 0.5
    c0 = hk * HEAD_DIM
    kcat = jnp.concatenate([kvp[:, c0:c0 + HEAD_DIM], kvc[:, c0:c0 + HEAD_DIM]], axis=0).astype(BF16)
    vcat = jnp.concatenate([kvp[:, D_KV + c0:D_KV + c0 + HEAD_DIM], kvc[:, D_KV + c0:D_KV + c0 + HEAD_DIM]],
                           axis=0).astype(BF16)
    q0 = hk * Q_PER_KV * HEAD_DIM
    qs = jnp.concatenate([q[:, q0 + g * HEAD_DIM:q0 + (g + 1) * HEAD_DIM] for g in range(Q_PER_KV)],
                         axis=0).astype(BF16)
    s = _dot(qs, kcat, NT) * (HEAD_DIM ** -0.5) + bias_ref[hk]
    s = jnp.where(ok4, s, NEG_INF)
    row = lax.broadcasted_iota(jnp.int32, (Q_PER_KV * BLOCK, 1), 0)
    sink = jnp.zeros((Q_PER_KV * BLOCK, 1), F32)
    for g in range(Q_PER_KV):
        sink = jnp.where((row >> BLOCK_SHIFT) == g, sinks_ref[hk * Q_PER_KV + g], sink)
    m = jnp.maximum(jnp.max(s, axis=-1, keepdims=True), sink)
    p = jnp.exp(s - m)
    es = jnp.exp(sink - m)
    inv = 1.0 / (jnp.sum(p, axis=-1, keepdims=True) + es)
    return qs, kcat, vcat, p * inv, es * inv


def _attn_in_specs():
    return [pl.BlockSpec((BLOCK, D_ATTN), lambda n: (n, 0)),
            pl.BlockSpec((BLOCK, 2 * D_KV), lambda n: (jnp.maximum(n - 1, 0), D_ATTN // (2 * D_KV))),
            pl.BlockSpec((BLOCK, 2 * D_KV), lambda n: (n, D_ATTN // (2 * D_KV))),
            _const((N_KV_HEADS, Q_PER_KV * BLOCK, 2 * BLOCK)),
            pl.BlockSpec(memory_space=pltpu.SMEM)]


def _unstack_heads(t):
    return jnp.concatenate([t[g * BLOCK:(g + 1) * BLOCK] for g in range(Q_PER_KV)], axis=1)


def _attn_fwd(proj, bias, sinks):
    def body(q_ref, kvp_ref, kvc_ref, bias_ref, sinks_ref, o_ref):
        n = pl.program_id(0)
        q, kvp, kvc = q_ref[...], kvp_ref[...], kvc_ref[...]
        outs = []
        for hk in range(N_KV_HEADS):
            _, _, vcat, probs, _ = _attn_pieces(n, q, kvp, kvc, bias_ref, sinks_ref, hk)
            outs.append(_unstack_heads(_dot(probs.astype(BF16), vcat)))
        o_ref[...] = jnp.concatenate(outs, axis=1)

    return pl.pallas_call(
        body, name="attn_fwd", grid=(SEQ // BLOCK,),
        in_specs=_attn_in_specs(),
        out_specs=pl.BlockSpec((BLOCK, D_ATTN), lambda n: (n, 0)),
        out_shape=jax.ShapeDtypeStruct((SEQ, D_ATTN), F32),
        compiler_params=_cp(("parallel",)),
    )(proj, proj, proj, bias, sinks)


def _attn_bwd(proj, bias, sinks, dcat):
    nb = SEQ // BLOCK

    def body(q_ref, kvp_ref, kvc_ref, bias_ref, sinks_ref, do_ref, dq_ref, dkv_ref, dbias_ref, dsink_ref, dsacc):
        n = pl.program_id(0)

        @pl.when(n == 0)
        def _():
            dkv_ref[...] = jnp.zeros_like(dkv_ref)
            dbias_ref[...] = jnp.zeros_like(dbias_ref)
            dsacc[...] = jnp.zeros_like(dsacc)

        q, kvp, kvc = q_ref[...], kvp_ref[...], kvc_ref[...]
        do_all = do_ref[...]
        dqs, dks, dvs = [], [], []
        for hk in range(N_KV_HEADS):
            qs, kcat, vcat, probs, psink = _attn_pieces(n, q, kvp, kvc, bias_ref, sinks_ref, hk)
            q0 = hk * Q_PER_KV * HEAD_DIM
            do = jnp.concatenate([do_all[:, q0 + g * HEAD_DIM:q0 + (g + 1) * HEAD_DIM] for g in range(Q_PER_KV)],
                                 axis=0).astype(BF16)
            dprobs = _dot(do, vcat, NT)
            dvs.append(_dot(probs.astype(BF16), do, TN))
            rowdot = jnp.sum(probs * dprobs, axis=-1, keepdims=True)
            ds = probs * (dprobs - rowdot)
            dsacc[hk] += -psink * rowdot
            dbias_ref[hk] += ds
            dsb = (ds * (HEAD_DIM ** -0.5)).astype(BF16)
            dqs.append(_unstack_heads(_dot(dsb, kcat)))
            dks.append(_dot(dsb, qs, TN))
        dq_ref[...] = jnp.concatenate(dqs, axis=1)
        upd = jnp.concatenate(dks + dvs, axis=1)
        cur = pl.multiple_of(n * BLOCK, BLOCK)
        dkv_ref[pl.ds(cur, BLOCK), :] += upd[BLOCK:]

        @pl.when(n > 0)
        def _():
            prev = pl.multiple_of((n - 1) * BLOCK, BLOCK)
            dkv_ref[pl.ds(prev, BLOCK), :] += upd[:BLOCK]

        @pl.when(n == nb - 1)
        def _():
            for hk in range(N_KV_HEADS):
                for g in range(Q_PER_KV):
                    tot = jnp.sum(dsacc[hk, g * BLOCK:(g + 1) * BLOCK, :], axis=0, keepdims=True)
                    h = hk * Q_PER_KV + g
                    dsink_ref[h:h + 1, :] = jnp.broadcast_to(tot, (1, LANES))

    return pl.pallas_call(
        body, name="attn_bwd", grid=(nb,),
        in_specs=_attn_in_specs() + [pl.BlockSpec((BLOCK, D_ATTN), lambda n: (n, 0))],
        out_specs=[pl.BlockSpec((BLOCK, D_ATTN), lambda n: (n, 0)), _const((SEQ, 2 * D_KV)),
                   _const((N_KV_HEADS, Q_PER_KV * BLOCK, 2 * BLOCK)), _const((N_Q_HEADS, LANES))],
        out_shape=[jax.ShapeDtypeStruct((SEQ, D_ATTN), F32), jax.ShapeDtypeStruct((SEQ, 2 * D_KV), F32),
                   jax.ShapeDtypeStruct((N_KV_HEADS, Q_PER_KV * BLOCK, 2 * BLOCK), F32),
                   jax.ShapeDtypeStruct((N_Q_HEADS, LANES), F32)],
        scratch_shapes=[pltpu.VMEM((N_KV_HEADS, Q_PER_KV * BLOCK, 1), F32)],
        compiler_params=_cp(("arbitrary",)),
    )(proj, proj, proj, bias, sinks, dcat)


@jax.custom_vjp
def _head_sum(x):
    ones = _head_ones(LANES)
    return jnp.concatenate([_dot_ind(x[:, c:c + LANES], ones, 2) for c in range(0, x.shape[-1], LANES)], axis=1)


_head_sum.defvjp(lambda x: (_head_sum(x), None), lambda _, ct: (_head_sum(ct),))


@jax.custom_vjp
def _bdot(a, w):
    return _dot(a.astype(BF16), w.astype(BF16))


def _bdot_bwd(res, ct):
    a, w = res
    ctb = ct.astype(BF16)
    return _dot(ctb, w.astype(BF16), NT), _dot(a.astype(BF16), ctb, TN)


_bdot.defvjp(lambda a, w: (_bdot(a, w), (a, w)), _bdot_bwd)


def _sigmoid(x):
    return 0.5 * (jnp.tanh(0.5 * x) + 1.0)


def _softplus(x):
    return jnp.maximum(x, 0.0) + jnp.log(1.0 + jnp.exp(-jnp.abs(x)))


def _rwkv_core(r, k, v, zwa, zg, w0, wdu, a0, wiu, wgu, k_k, k_a):
    w_log = -_softplus(-(w0 + _bdot(jnp.tanh(zwa), wdu))) - 0.5
    decay = jnp.exp(-jnp.exp(w_log))
    a = _sigmoid(a0 + _bdot(zwa, wiu))
    g = _bdot(_sigmoid(zg), wgu)
    kk = k * k_k
    kk = kk / jnp.maximum(jnp.sqrt(_head_sum(kk * kk)), 1e-12)
    k2 = k * (1.0 + (a - 1.0) * k_a)
    return r, decay, k2, v, -kk, kk * a, g


def _rwkv_out(o, r, k2, v, g, lng, lnb, rk):
    mu = _head_sum(o) * (1.0 / HEAD_DIM)
    d = o - mu
    var = _head_sum(d * d) * (1.0 / HEAD_DIM)
    on = d * lax.rsqrt(var + GN_EPS) * lng + lnb
    bonus = _head_sum(r * k2 * rk) * v
    return (on + bonus) * g


P_SPLITS = (0, 512, 1024, 1536, 1664, 1792)
N_PREP_PARAMS = 7
HALO = 8


def _shifted_pieces(i, p_ref, halo_ref, mix_ref):
    p = p_ref[:, P_OFF:]
    prev_row = halo_ref[HALO - 1:HALO, P_OFF:] * jnp.where(i > 0, 1.0, 0.0)
    row = lax.broadcasted_iota(jnp.int32, p.shape, 0)
    pprev = jnp.where(row == 0, prev_row, pltpu.roll(p, 1, 0))
    delta = pprev - p
    ps = p + delta * mix_ref[...]
    return [ps[:, a:b] for a, b in zip(P_SPLITS[:-1], P_SPLITS[1:])], delta


def _prep_in_specs():
    return [_rows(TR, D_IN),
            pl.BlockSpec((HALO, D_IN), lambda i: (jnp.maximum(i * (TR // HALO) - 1, 0), 0)),
            _const((1, RWKV_COLS)), _const((1, D_RWKV)), _const((LANES, D_RWKV)), _const((1, D_RWKV)),
            _const((LANES, D_RWKV)), _const((LANES, D_RWKV)), _const((1, D_RWKV)), _const((1, D_RWKV))]


def _rwkv_prep(proj, mix, prm):
    def body(p_ref, halo_ref, mix_ref, *refs):
        prm_refs, outs = refs[:N_PREP_PARAMS], refs[N_PREP_PARAMS:]
        pieces, _ = _shifted_pieces(pl.program_id(0), p_ref, halo_ref, mix_ref)
        vals = _rwkv_core(*pieces, *[t[...] for t in prm_refs])
        for ref, val in zip(outs, vals):
            ref[...] = val

    return pl.pallas_call(
        body, name="rwkv_prep", grid=(SEQ // TR,),
        in_specs=_prep_in_specs(),
        out_specs=[_rows(TR, D_RWKV)] * 7,
        out_shape=[jax.ShapeDtypeStruct((SEQ, D_RWKV), F32)] * 7,
        compiler_params=_cp(("parallel",)),
    )(proj, proj, mix, *prm)


def _rwkv_prep_bwd(proj, mix, prm, cts):
    def body(p_ref, halo_ref, mix_ref, *refs):
        i = pl.program_id(0)
        prm_refs = refs[:N_PREP_PARAMS]
        ct_refs = refs[N_PREP_PARAMS:N_PREP_PARAMS + 10]
        dps_ref, dmix_ref = refs[N_PREP_PARAMS + 10:N_PREP_PARAMS + 12]
        dprm_refs = refs[N_PREP_PARAMS + 12:]
        pieces, delta = _shifted_pieces(i, p_ref, halo_ref, mix_ref)
        _, vjp = jax.vjp(_rwkv_core, *pieces, *[t[...] for t in prm_refs])
        dr1, dr2, dw, dk1, dk2, dv1, dv2, dkkn, db, dg = [t[...] for t in ct_refs]
        grads = vjp((dr1 + dr2, dw, dk1 + dk2, dv1 + dv2, dkkn, db, dg))
        dps = jnp.concatenate(grads[:5], axis=1)
        dps_ref[...] = dps

        @pl.when(i == 0)
        def _():
            dmix_ref[...] = jnp.zeros_like(dmix_ref)
            for ref in dprm_refs:
                ref[...] = jnp.zeros_like(ref)

        dmix_ref[...] += jnp.sum(dps * delta, axis=0, keepdims=True)
        for ref, gval in zip(dprm_refs, grads[5:]):
            ref[...] += gval

    prm_shapes = [(1, D_RWKV), (LANES, D_RWKV), (1, D_RWKV), (LANES, D_RWKV), (LANES, D_RWKV), (1, D_RWKV), (1, D_RWKV)]
    return pl.pallas_call(
        body, name="rwkv_prep_bwd", grid=(SEQ // TR,),
        in_specs=_prep_in_specs() + [_rows(TR, D_RWKV)] * 10,
        out_specs=[_rows(TR, RWKV_COLS), _const((1, RWKV_COLS))] + [_const(s) for s in prm_shapes],
        out_shape=[jax.ShapeDtypeStruct((SEQ, RWKV_COLS), F32), jax.ShapeDtypeStruct((1, RWKV_COLS), F32)]
        + [jax.ShapeDtypeStruct(s, F32) for s in prm_shapes],
        compiler_params=_cp(("arbitrary",)),
    )(proj, proj, mix, *prm, *cts)


def _rwkv_post(o, r, k2, v, g, lng, lnb, rk, attn):
    def body(o_ref, r_ref, k_ref, v_ref, g_ref, lng_ref, lnb_ref, rk_ref, attn_ref, cat_ref):
        rw = _rwkv_out(*[t[...] for t in (o_ref, r_ref, k_ref, v_ref, g_ref, lng_ref, lnb_ref, rk_ref)])
        cat_ref[...] = jnp.concatenate([attn_ref[...], rw], axis=1).astype(BF16)

    return pl.pallas_call(
        body, name="rwkv_post", grid=(SEQ // TR,),
        in_specs=[_rows(TR, D_RWKV)] * 5 + [_const((1, D_RWKV))] * 3 + [_rows(TR, D_ATTN)],
        out_specs=_rows(TR, D_MODEL),
        out_shape=jax.ShapeDtypeStruct((SEQ, D_MODEL), BF16),
        compiler_params=_cp(("parallel",)),
    )(o, r, k2, v, g, lng, lnb, rk, attn)


def _rwkv_post_bwd(o, r, k2, v, g, lng, lnb, rk, dcat):
    def body(o_ref, r_ref, k_ref, v_ref, g_ref, lng_ref, lnb_ref, rk_ref, dcat_ref,
             do_ref, dr_ref, dk_ref, dv_ref, dg_ref, dlng_ref, dlnb_ref, drk_ref):
        i = pl.program_id(0)
        args = [t[...] for t in (o_ref, r_ref, k_ref, v_ref, g_ref, lng_ref, lnb_ref, rk_ref)]
        _, vjp = jax.vjp(_rwkv_out, *args)
        grads = vjp(dcat_ref[:, D_ATTN:])
        for ref, gval in zip((do_ref, dr_ref, dk_ref, dv_ref, dg_ref), grads[:5]):
            ref[...] = gval

        @pl.when(i == 0)
        def _():
            for ref in (dlng_ref, dlnb_ref, drk_ref):
                ref[...] = jnp.zeros_like(ref)

        for ref, gval in zip((dlng_ref, dlnb_ref, drk_ref), grads[5:]):
            ref[...] += gval

    return pl.pallas_call(
        body, name="rwkv_post_bwd", grid=(SEQ // TR,),
        in_specs=[_rows(TR, D_RWKV)] * 5 + [_const((1, D_RWKV))] * 3 + [_rows(TR, D_MODEL)],
        out_specs=[_rows(TR, D_RWKV)] * 5 + [_const((1, D_RWKV))] * 3,
        out_shape=[jax.ShapeDtypeStruct((SEQ, D_RWKV), F32)] * 5 + [jax.ShapeDtypeStruct((1, D_RWKV), F32)] * 3,
        compiler_params=_cp(("arbitrary",)),
    )(o, r, k2, v, g, lng, lnb, rk, dcat)


def _assemble_dproj(dq, dkv, dps, mix):
    last = SEQ // HALO - 1

    def body(dq_ref, dkv_ref, dps_ref, nxt_ref, mix_ref, o_ref):
        i = pl.program_id(0)
        dps = dps_ref[...]
        mixv = mix_ref[...]
        nxt_row = nxt_ref[0:1, :] * jnp.where(i < SEQ // TR - 1, 1.0, 0.0)
        row = lax.broadcasted_iota(jnp.int32, dps.shape, 0)
        up = jnp.where(row == TR - 1, nxt_row, pltpu.roll(dps, TR - 1, 0))
        dp = dps * (1.0 - mixv) + up * mixv
        o_ref[...] = jnp.concatenate([dq_ref[...], dkv_ref[...], dp], axis=1).astype(BF16)

    return pl.pallas_call(
        body, name="assemble_dproj", grid=(SEQ // TR,),
        in_specs=[_rows(TR, D_ATTN), _rows(TR, 2 * D_KV), _rows(TR, RWKV_COLS),
                  pl.BlockSpec((HALO, RWKV_COLS), lambda i: (jnp.minimum((i + 1) * (TR // HALO), last), 0)),
                  _const((1, RWKV_COLS))],
        out_specs=_rows(TR, D_IN),
        out_shape=jax.ShapeDtypeStruct((SEQ, D_IN), BF16),
        compiler_params=_cp(("parallel",)),
    )(dq, dkv, dps, dps, mix)


N_PAIR = D_RWKV // LANES
CHUNK = 64
N_CHUNK = SEQ // CHUNK
GROUP = 8
STATE = (N_PAIR, HEAD_DIM, LANES)


def _lane_sums(lhs_tiles, ones2):
    out = _dot(jnp.concatenate(lhs_tiles, axis=0), ones2)
    return [out[i * HEAD_DIM:(i + 1) * HEAD_DIM] for i in range(len(lhs_tiles))]


def _seg_sum(xs, ones2):
    return _lane_sums([jnp.concatenate(_split(x, 2), axis=1) for x in xs], ones2)


def _seg_sum_rows(xs, ones2):
    out = _dot(jnp.concatenate(_split(jnp.concatenate(xs, axis=0), 2), axis=1), ones2)
    return [out[i * GROUP:(i + 1) * GROUP] for i in range(len(xs))]


def _col_form(rows, diag, ones2):
    zero = jnp.zeros((HEAD_DIM, LANES), BF16)
    tiles = []
    for row in rows:
        hi = row.astype(BF16)
        lo = (row - hi.astype(F32)).astype(BF16)
        tiles.append(jnp.concatenate(
            [jnp.where(diag, jnp.broadcast_to(part, (HEAD_DIM, LANES)), zero) for part in (hi, lo)], axis=1))
    return _lane_sums(tiles, ones2)


def _scan_consts():
    ones2 = jnp.concatenate([_head_ones(LANES)] * 2, axis=0)
    sub = lax.broadcasted_iota(jnp.int32, (HEAD_DIM, LANES), 0)
    lane_in_head = lax.broadcasted_iota(jnp.int32, (HEAD_DIM, LANES), 1) & (HEAD_DIM - 1)
    return ones2, lane_in_head == sub, lane_in_head


def _rows_of_columns(tile):
    t = tile.T
    return jnp.concatenate([t[:CHUNK], t[HEAD_DIM:HEAD_DIM + CHUNK]], axis=1)


def _pair(j):
    return slice(j * LANES, (j + 1) * LANES)


def _scan_fwd(r, w, k, v, kkn, b):
    def body(r_ref, w_ref, k_ref, v_ref, kkn_ref, b_ref, o_ref, st_ref, sa_ref, s_scr):
        c = pl.program_id(0)
        ones2, diag, lane_in_head = _scan_consts()

        @pl.when(c == 0)
        def _():
            s_scr[...] = jnp.zeros_like(s_scr)

        def group(gi, carry):
            row0 = pl.multiple_of(gi * GROUP, GROUP)
            states, ocols = list(carry[:N_PAIR]), list(carry[N_PAIR:])
            tiles = [[t[pl.ds(row0, GROUP), _pair(j)] for t in (r_ref, w_ref, k_ref, v_ref, kkn_ref, b_ref)]
                     for j in range(N_PAIR)]
            def row(j, name, u):
                return tiles[j]["rwkvnb".index(name)][u:u + 1]

            def emit_out(u, after):
                outs = _seg_sum([s[j] * row(j, "r", u + d) for d, s in enumerate(after) for j in range(N_PAIR)], ones2)
                for d in range(2):
                    here = lane_in_head == gi * GROUP + u + d
                    for j in range(N_PAIR):
                        ocols[j] = jnp.where(here, outs[d * N_PAIR + j], ocols[j])

            def vcols_of(u):
                cols = _col_form([row(j, "v", u + d) for d in range(2) for j in range(N_PAIR)], diag, ones2)
                return cols[:N_PAIR], cols[N_PAIR:]

            n_next = [pltpu.roll(tiles[j][4], GROUP - 1, 0) for j in range(N_PAIR)]
            dots = _seg_sum_rows([tiles[j][5] * n_next[j] for j in range(N_PAIR)]
                                 + [tiles[j][2] * n_next[j] for j in range(N_PAIR)], ones2)
            b_n, k_n = dots[:N_PAIR], dots[N_PAIR:]
            w_n = [tiles[j][1] * n_next[j] for j in range(N_PAIR)]

            vcols = vcols_of(0)
            after = None
            for u in range(0, GROUP, 2):
                prods = _seg_sum([states[j] * row(j, "n", u) for j in range(N_PAIR)]
                                 + [states[j] * w_n[j][u:u + 1] for j in range(N_PAIR)], ones2)
                if after is not None:
                    emit_out(u - 2, after)
                nxt = vcols_of(u + 2) if u + 2 < GROUP else None
                first, second = [], []
                for j in range(N_PAIR):
                    sa1 = prods[j]
                    sa2 = prods[N_PAIR + j] + sa1 * b_n[j][u:u + 1] + vcols[0][j] * k_n[j][u:u + 1]
                    s1 = states[j] * row(j, "w", u) + sa1 * row(j, "b", u) + vcols[0][j] * row(j, "k", u)
                    s2 = s1 * row(j, "w", u + 1) + sa2 * row(j, "b", u + 1) + vcols[1][j] * row(j, "k", u + 1)
                    st_ref[row0 + u, j] = s1
                    sa_ref[row0 + u, j] = sa1
                    st_ref[row0 + u + 1, j] = s2
                    sa_ref[row0 + u + 1, j] = sa2
                    first.append(s1)
                    second.append(s2)
                    states[j] = s2
                after, vcols = (first, second), nxt
            emit_out(GROUP - 2, after)
            return tuple(states + ocols)

        zero = jnp.zeros((HEAD_DIM, LANES), F32)
        fin = lax.fori_loop(0, CHUNK // GROUP, group, tuple(s_scr[j] for j in range(N_PAIR)) + (zero,) * N_PAIR)
        for j in range(N_PAIR):
            s_scr[j] = fin[j]
            o_ref[:, _pair(j)] = _rows_of_columns(fin[N_PAIR + j])

    blk = pl.BlockSpec((CHUNK, D_RWKV), lambda c: (c, 0))
    per_step = pl.BlockSpec((CHUNK,) + STATE, lambda c: (c, 0, 0, 0))
    return pl.pallas_call(
        body, name="rwkv_scan_fwd", grid=(N_CHUNK,),
        in_specs=[blk] * 6,
        out_specs=[blk, per_step, per_step],
        out_shape=[jax.ShapeDtypeStruct((SEQ, D_RWKV), F32)] + [jax.ShapeDtypeStruct((SEQ,) + STATE, F32)] * 2,
        scratch_shapes=[pltpu.VMEM(STATE, F32)],
        compiler_params=_cp(("arbitrary",)),
    )(r, w, k, v, kkn, b)


def _scan_bwd(r, w, k, v, kkn, b, do, states, sas, ds_in, prev, name, first_chunk, n_chunks):
    top = first_chunk + n_chunks - 1

    def body(r_ref, w_ref, k_ref, v_ref, kkn_ref, b_ref, do_ref, st_ref, before_ref, sa_ref, ds_in_ref, *rest):
        dr_ref, dw_ref, dk_ref, dv_ref, dkkn_ref, db_ref, ds_out_ref, ds_scr = rest[-8:]
        i = pl.program_id(0)
        ones2, diag, lane_in_head = _scan_consts()

        @pl.when(i == 0)
        def _():
            ds_scr[...] = ds_in_ref[...]

        entry = [before_ref[0, j] * jnp.where(i < top, 1.0, 0.0) for j in range(N_PAIR)]

        def reverse(gr, carry):
            gi = CHUNK // GROUP - 1 - gr
            row0 = pl.multiple_of(gi * GROUP, GROUP)
            dstates, dvcols = list(carry[:N_PAIR]), list(carry[N_PAIR:])
            tiles = [[t[pl.ds(row0, GROUP), _pair(j)]
                      for t in (r_ref, w_ref, k_ref, v_ref, kkn_ref, b_ref, do_ref)] for j in range(N_PAIR)]
            rows = [[[None] * GROUP for _ in range(5)] for _ in range(N_PAIR)]

            def row(j, name, u):
                return tiles[j]["rwkvnbd".index(name)][u:u + 1]

            def cols_of(u):
                cols = _col_form([row(j, name, u - d) for d in range(2) for name in "dv" for j in range(N_PAIR)],
                                 diag, ones2)
                return [[(cols[(2 * d) * N_PAIR + j], cols[(2 * d + 1) * N_PAIR + j]) for j in range(N_PAIR)]
                        for d in range(2)]

            def emit_dv(u, dsps):
                outs = _seg_sum([dsp[j] * row(j, "k", u - d) for d, dsp in enumerate(dsps) for j in range(N_PAIR)], ones2)
                for d in range(2):
                    here = lane_in_head == gi * GROUP + u - d
                    for j in range(N_PAIR):
                        dvcols[j] = jnp.where(here, outs[d * N_PAIR + j], dvcols[j])

            b_prev = [pltpu.roll(tiles[j][5], 1, 0) for j in range(N_PAIR)]
            dots = _seg_sum_rows([tiles[j][4] * b_prev[j] for j in range(N_PAIR)]
                                 + [tiles[j][0] * tiles[j][5] for j in range(N_PAIR)], ones2)
            n_b, r_b = dots[:N_PAIR], dots[N_PAIR:]
            w_b = [tiles[j][1] * b_prev[j] for j in range(N_PAIR)]

            def outputs(u, j, dsp, dsa, docol, vcol):
                tl = gi * GROUP + u
                if u > 0:
                    s_prev = st_ref[tl - 1, j]
                else:
                    s_prev = jnp.where(gi == 0, entry[j], st_ref[jnp.maximum(tl - 1, 0), j])
                rows[j][0][u] = jnp.sum(st_ref[tl, j] * docol, axis=0, keepdims=True)
                rows[j][1][u] = jnp.sum(dsp * s_prev, axis=0, keepdims=True)
                rows[j][2][u] = jnp.sum(dsp * vcol, axis=0, keepdims=True)
                rows[j][3][u] = jnp.sum(s_prev * dsa, axis=0, keepdims=True)
                rows[j][4][u] = jnp.sum(dsp * sa_ref[tl, j], axis=0, keepdims=True)

            cols = cols_of(GROUP - 1)
            before = None
            for u in range(GROUP - 1, 0, -2):
                dsp1 = [dstates[j] + cols[0][j][0] * row(j, "r", u) for j in range(N_PAIR)]
                prods = _seg_sum([dsp1[j] * row(j, "b", u) for j in range(N_PAIR)]
                                 + [dsp1[j] * w_b[j][u:u + 1] for j in range(N_PAIR)], ones2)
                if before is not None:
                    emit_dv(u + 2, before)
                nxt = cols_of(u - 2) if u >= 2 else None
                dsp2 = []
                for j in range(N_PAIR):
                    dsa1 = prods[j]
                    dsa2 = prods[N_PAIR + j] + dsa1 * n_b[j][u:u + 1] + cols[1][j][0] * r_b[j][u - 1:u]
                    mid = dsp1[j] * row(j, "w", u) + dsa1 * row(j, "n", u) + cols[1][j][0] * row(j, "r", u - 1)
                    outputs(u, j, dsp1[j], dsa1, *cols[0][j])
                    outputs(u - 1, j, mid, dsa2, *cols[1][j])
                    dstates[j] = mid * row(j, "w", u - 1) + dsa2 * row(j, "n", u - 1)
                    dsp2.append(mid)
                before, cols = (dsp1, dsp2), nxt
            emit_dv(1, before)
            for j in range(N_PAIR):
                for ref, rr in zip((dr_ref, dw_ref, dk_ref, dkkn_ref, db_ref), rows[j]):
                    ref[pl.ds(row0, GROUP), _pair(j)] = jnp.concatenate(rr, axis=0)
            return tuple(dstates + dvcols)

        zero = jnp.zeros((HEAD_DIM, LANES), F32)
        dfin = lax.fori_loop(0, CHUNK // GROUP, reverse, tuple(ds_scr[j] for j in range(N_PAIR)) + (zero,) * N_PAIR)
        for j in range(N_PAIR):
            ds_scr[j] = dfin[j]
            dv_ref[:, _pair(j)] = _rows_of_columns(dfin[N_PAIR + j])

        @pl.when(i == n_chunks - 1)
        def _():
            ds_out_ref[...] = ds_scr[...]

    blk = pl.BlockSpec((CHUNK, D_RWKV), lambda i: (top - i, 0))
    per_step = pl.BlockSpec((CHUNK,) + STATE, lambda i: (top - i, 0, 0, 0))
    step_before = pl.BlockSpec((1,) + STATE, lambda i: (jnp.maximum((top - i) * CHUNK - 1, 0), 0, 0, 0))
    prev = [] if prev is None else list(prev)
    outs = pl.pallas_call(
        body, name=name, grid=(n_chunks,),
        in_specs=[blk] * 7 + [per_step, step_before, per_step, _const(STATE)] + [ANY] * len(prev),
        out_specs=[blk] * 6 + [_const(STATE)],
        out_shape=[jax.ShapeDtypeStruct((SEQ, D_RWKV), F32)] * 6 + [jax.ShapeDtypeStruct(STATE, F32)],
        scratch_shapes=[pltpu.VMEM(STATE, F32)],
        input_output_aliases={11 + t: t for t in range(len(prev))},
        compiler_params=_cp(("arbitrary",)),
    )(r, w, k, v, kkn, b, do, states, states, sas, ds_in, *prev)
    return outs[:6], outs[6]


def _stacked(rows, cols, pick):
    return pl.BlockSpec((None, rows, cols), pick)


def _local_step(x, target, sm, win_st):
    def tied(t, token):
        return t if token is None else t + token[0:1, 0:1].reshape((1,) * t.ndim)

    zpad = jnp.zeros((LORA_DECAY, D_RWKV), F32)
    prm = [sm["w0"], jnp.concatenate([sm["w_decay_up"], zpad], axis=0), sm["a0"],
           jnp.concatenate([zpad, sm["w_iclr_up"]], axis=0), sm["w_gate_up"], sm["k_k"], sm["k_a"]]
    mix = sm["rwkv_shift_mix"]
    onehot = jnp.asarray(_t5_onehot(), BF16)
    sinks = sm["sinks"].reshape(N_Q_HEADS)
    lng, lnb, rk = sm["ln_x_g"], sm["ln_x_b"], sm["r_k"].reshape(1, D_RWKV)

    h1 = _norm_cast(x, sm["norm_mix_pre"], "norm_in")
    proj = _matmul(h1, win_st, "nn", "proj", m=SEQ, n=D_IN, k=D_MODEL, tm=SEQ, tn=640,
                   b_spec=_stacked(D_MODEL, 640, lambda i, j: (j, 0, 0)))
    bias = _bias_table(sm["rel_bias"].T, onehot).reshape(N_KV_HEADS, Q_PER_KV * BLOCK, 2 * BLOCK)
    attn = _attn_fwd(proj, bias, sinks)
    r, w, k2, v, kkn, b, g = _rwkv_prep(proj, mix, prm)
    o, states, sas = _scan_fwd(r, w, k2, v, kkn, b)
    wout, wup_st, wdown = yield ("rest_weights", o)
    cat = _rwkv_post(o, r, k2, v, g, lng, lnb, rk, attn)
    mixo = _matmul(cat, wout, "nn", "out_proj", m=SEQ, n=D_MODEL, k=D_MODEL, tm=SEQ, tn=512)
    x2, h3 = _mix_norm(x, mixo, sm["norm_mix_post"], sm["norm_ffn_pre"])
    u_gate, u_val, gate, val, act = _ffn_up_act(h3, wup_st, sm["conv_w"], sm["conv_b"])
    f = _matmul(act, wdown, "nn", "ffn_down", m=SEQ, n=D_MODEL, k=D_FF, tm=1024, tn=512)
    loss, dy, df, d_g4 = _loss_head(x2, f, sm["norm_ffn_post"], target)

    d_wdown = _matmul(act, df, "tn", "d_wdown", m=D_FF, n=D_MODEL, k=SEQ, tm=1024, tn=D_MODEL)
    du, d_convw, d_convb = _ffn_act_bwd(u_gate, u_val, gate, val, df, wdown, sm["conv_w"])
    d_convw = d_convw.transpose(1, 0, 2).reshape(3, 2 * D_FF)
    d_convb = d_convb.reshape(1, 2 * D_FF)
    dh3 = _matmul_nt_shards(du, wup_st, "d_h3", m=SEQ, n=D_MODEL, tm=512, tn=512,
                            a_spec=pl.BlockSpec((2, 512, D_FF), lambda i, j: (0, i, 0)),
                            a_piece=lambda ref, s: ref[s // 2, :, (s % 2) * 2048:(s % 2 + 1) * 2048])
    d_wup = _matmul(h3, du, "tn", "d_wup", m=D_MODEL, n=2 * D_FF, k=SEQ, tm=D_MODEL, tn=1024,
                    b_spec=pl.BlockSpec((None, SEQ, 1024), lambda i, j: (j // 4, 0, j % 4)),
                    out=((N_CHIPS, D_MODEL, 2048), _stacked(D_MODEL, 1024, lambda i, j: (j // 2, 0, j % 2))))
    dx2, dmix, d_g2, d_g3 = _mid_bwd(x2, mixo, dy, dh3, sm["norm_mix_post"], sm["norm_ffn_pre"])
    dcat = _matmul(dmix, wout, "nt", "d_cat", m=SEQ, n=D_MODEL, k=D_MODEL, tm=SEQ, tn=512)
    d_wout = _matmul(cat, dmix, "tn", "d_wout", m=D_MODEL, n=D_MODEL, k=SEQ, tm=512, tn=D_MODEL)
    token = yield ("grads_a", (d_wdown, d_wup, d_wout))
    do, dr_p, dk_p, dv_p, dg, d_lng, d_lnb, d_rk = _rwkv_post_bwd(o, r, k2, v, g, lng, tied(lnb, token), rk, dcat)
    half = N_CHUNK // 2
    ds_end = jnp.zeros(STATE, F32)
    late, ds_mid = _scan_bwd(r, w, k2, v, kkn, b, do, states, sas, ds_end, None, "rwkv_scan_bwd_late", half, half)
    token = yield ("seam_1", ds_mid)
    scan_cts, ds_first = _scan_bwd(r, w, k2, v, kkn, b, do, states, sas, tied(ds_mid, token), late,
                                   "rwkv_scan_bwd_early", 0, half)
    dr_s, dw_s, dk_s, dv_s, dkkn_s, db_s = scan_cts
    token = yield ("seam_2", ds_first)
    prep_grads = _rwkv_prep_bwd(proj, tied(mix, token), prm,
                                (dr_s, dr_p, dw_s, dk_s, dk_p, dv_s, dv_p, dkkn_s, db_s, dg))
    dps, d_mix, d_w0, d_wdu, d_a0, d_wiu, d_wgu, d_kk, d_ka = prep_grads
    dq, dkv, dbias, dsink = _attn_bwd(proj, bias, sinks, dcat)
    d_relb = _bias_table_bwd(dbias.reshape(N_Q_HEADS, N_REL), onehot).T
    dproj = _assemble_dproj(dq, dkv, dps, mix)
    d_win = _matmul(h1, dproj, "tn", "d_win", m=D_MODEL, n=D_IN, k=SEQ, tm=D_MODEL, tn=640,
                    out=((N_CHIPS, D_MODEL, 640), _stacked(D_MODEL, 640, lambda i, j: (j, 0, 0))))
    token = yield ("grads_b", d_win)
    dh1 = _matmul_nt_shards(dproj, win_st, "d_h1", m=SEQ, n=D_MODEL, tm=1024, tn=D_MODEL,
                            a_spec=pl.BlockSpec((1024, D_IN), lambda i, j: (i, 0)),
                            a_piece=lambda ref, s: ref[:, s * 640:(s + 1) * 640])
    grad_x, d_g1 = _first_bwd(x, dx2, dh1, tied(sm["norm_mix_pre"], token))

    grads = {
        "norm_mix_pre": d_g1, "norm_mix_post": d_g2, "norm_ffn_pre": d_g3, "norm_ffn_post": d_g4,
        "w_in": d_win, "rel_bias": d_relb, "sinks": dsink[:, 0].reshape(1, N_Q_HEADS),
        "rwkv_shift_mix": d_mix, "w0": d_w0, "w_decay_up": d_wdu[:LORA_DECAY], "a0": d_a0,
        "w_iclr_up": d_wiu[LORA_DECAY:], "w_gate_up": d_wgu, "k_k": d_kk, "k_a": d_ka,
        "r_k": d_rk.reshape(1, N_Q_HEADS, HEAD_DIM), "ln_x_g": d_lng, "ln_x_b": d_lnb,
        "w_out": d_wout, "w_ffn_up": d_wup, "conv_w": d_convw, "conv_b": d_convb, "w_ffn_down": d_wdown,
    }
    return loss, grad_x, grads


def _place():
    x, y, c = lax.axis_index("x"), lax.axis_index("y"), lax.axis_index("c")
    chips = [(1 - x, y), (x, 1 - y), (1 - x, 1 - y)]
    return x, y, c, chips


def _remote(src, dst, sems, idx, to):
    return pltpu.make_async_remote_copy(src_ref=src, dst_ref=dst, send_sem=sems[0].at[idx], recv_sem=sems[1].at[idx],
                                        device_id=to, device_id_type=MESH)


ROW_ALIGN = 16


def _half(c, rows):
    return pl.ds(pl.multiple_of(c * (rows // 2), ROW_ALIGN), rows // 2)


def _gather_weights(big, small):
    nb, ns = len(big), len(small)

    def body(*refs):
        ins, outs = refs[:nb + ns], refs[nb + ns:2 * (nb + ns)]
        ici, d2d, sml, loc = refs[2 * (nb + ns):2 * (nb + ns) + 2], refs[-5:-3], refs[-3:-1], refs[-1]
        x, y, c, chips = _place()
        me = 2 * x + y
        sib = (x, y, 1 - c)
        local = [pltpu.make_async_copy(ins[a], outs[a].at[me], loc.at[a]) for a in range(nb + ns)]
        for cp in local:
            cp.start()
        sends = []
        for a in range(nb):
            rows = _half(c, big[a].shape[0])
            for kk, chip in enumerate(chips):
                sends.append(_remote(ins[a].at[rows], outs[a].at[me, rows], ici, a * 3 + kk, (*chip, c)))
        for a in range(ns):
            for kk, chip in enumerate(chips):
                sends.append(_remote(ins[nb + a], outs[nb + a].at[me], sml, a * 3 + kk, (*chip, c)))
        for cp in sends:
            cp.start()
        passed = []
        for a in range(nb):
            rows = _half(c, big[a].shape[0])
            for kk, (px, py) in enumerate(chips):
                got = outs[a].at[2 * px + py, rows]
                _remote(got, got, ici, a * 3 + kk, sib).wait_recv()
                fwd = _remote(got, got, d2d, a * 3 + kk, sib)
                fwd.start()
                passed.append(fwd)
        for a in range(nb):
            other = _half(1 - c, big[a].shape[0])
            for kk, (px, py) in enumerate(chips):
                land = outs[a].at[2 * px + py, other]
                _remote(land, land, d2d, a * 3 + kk, sib).wait_recv()
        for a in range(ns):
            for kk, (px, py) in enumerate(chips):
                land = outs[nb + a].at[2 * px + py]
                _remote(land, land, sml, a * 3 + kk, sib).wait_recv()
        for cp in sends + passed:
            cp.wait_send()
        for cp in local:
            cp.wait()

    arrs = list(big) + list(small)
    in_vmem = pl.BlockSpec(memory_space=pltpu.VMEM)
    return pl.pallas_call(
        body, name="gather_weights",
        in_specs=[in_vmem] * len(arrs), out_specs=[in_vmem] * len(arrs),
        out_shape=[jax.ShapeDtypeStruct((N_CHIPS,) + t.shape, t.dtype) for t in arrs],
        scratch_shapes=[pltpu.SemaphoreType.DMA((3 * nb,)), pltpu.SemaphoreType.DMA((3 * nb,)),
                        pltpu.SemaphoreType.DMA((3 * nb,)), pltpu.SemaphoreType.DMA((3 * nb,)),
                        pltpu.SemaphoreType.DMA((3 * ns,)), pltpu.SemaphoreType.DMA((3 * ns,)),
                        pltpu.SemaphoreType.DMA((nb + ns,))],
        compiler_params=pltpu.CompilerParams(has_side_effects=True, vmem_limit_bytes=VMEM_LIMIT),
    )(*arrs)


HBM = pl.BlockSpec(memory_space=pltpu.HBM)
SEM = pl.BlockSpec(memory_space=pltpu.SEMAPHORE)
EFFECT = pltpu.SideEffectType.DATAFLOW_SIDE_EFFECTING


def _copies_start(name, bufs, plan, n, partners=None):
    nb = len(bufs)

    def body(*refs):
        ins, sems, token = refs[:nb], refs[nb:nb + 2 * n], refs[-1]
        if partners is not None:
            barrier = pltpu.get_barrier_semaphore()
            peers = partners[1]()
            for peer in peers:
                pl.semaphore_signal(barrier, inc=1, device_id=peer, device_id_type=MESH)
            pl.semaphore_wait(barrier, len(peers))
        for kk, (src, dst, dev) in enumerate(plan(ins)):
            pltpu.make_async_remote_copy(src_ref=src, dst_ref=dst, send_sem=sems[2 * kk], recv_sem=sems[2 * kk + 1],
                                         device_id=dev, device_id_type=MESH).start()
        token[...] = jnp.zeros_like(token)

    outs = pl.pallas_call(
        body, name=name,
        out_shape=tuple([pltpu.SemaphoreType.DMA(())] * (2 * n) + [pltpu.HBM(t.shape, t.dtype) for t in bufs]
                        + [jax.ShapeDtypeStruct((8, LANES), F32)]),
        in_specs=[HBM] * nb,
        out_specs=tuple([SEM] * (2 * n) + [HBM] * nb + [pl.BlockSpec(memory_space=pltpu.VMEM)]),
        input_output_aliases={t: 2 * n + t for t in range(nb)},
        compiler_params=pltpu.CompilerParams(has_side_effects=EFFECT,
                                             collective_id=None if partners is None else partners[0]),
    )(*[pltpu.with_memory_space_constraint(t, pltpu.HBM) for t in bufs])
    return outs[:2 * n], outs[2 * n:2 * n + nb], outs[-1]


def _copies_wait(name, sems, bufs, plan, n, after):
    nb = len(bufs)
    after = list(after) if isinstance(after, (list, tuple)) else [after]

    def body(*refs):
        ins, sem_refs = refs[:nb], refs[nb:nb + 2 * n]
        for kk, (src, dst, dev) in enumerate(plan(ins)):
            cp = pltpu.make_async_remote_copy(src_ref=src, dst_ref=dst, send_sem=sem_refs[2 * kk],
                                              recv_sem=sem_refs[2 * kk + 1], device_id=dev, device_id_type=MESH)
            cp.wait_send()
            cp.wait_recv()

    return pl.pallas_call(
        body, name=name,
        out_shape=tuple(pltpu.HBM(t.shape, t.dtype) for t in bufs),
        in_specs=[HBM] * nb + [SEM] * (2 * n) + [ANY] * len(after),
        out_specs=tuple([HBM] * nb),
        input_output_aliases={t: t for t in range(nb)},
        compiler_params=pltpu.CompilerParams(has_side_effects=EFFECT),
    )(*bufs, *sems, *after)


def _plan_gather(n_w):
    def plan(refs):
        x, y, c, chips = _place()
        me = 2 * x + y
        return [(refs[a], refs[n_w + a].at[me], (*chip, c)) for a in range(n_w) for chip in chips + [(x, y)]]
    return plan


def _plan_pair_halves(n_g, rows):
    def plan(refs):
        x, y, c, _ = _place()
        return [(refs[a].at[:, _half(1 - c, rows[a])], refs[n_g + a], (x, y, 1 - c)) for a in range(n_g)]
    return plan


def _plan_chip_parts(n_g):
    def plan(refs):
        x, y, c, chips = _place()
        me = 2 * x + y
        return [(refs[a].at[2 * px + py], refs[n_g + a].at[me], (px, py, c))
                for a in range(n_g) for (px, py) in chips]
    return plan


def _plan_pair_fill(n_g, rows):
    def plan(refs):
        x, y, c, _ = _place()
        return [(refs[a].at[_half(c, rows[a])], refs[a].at[_half(c, rows[a])], (x, y, 1 - c)) for a in range(n_g)]
    return plan


def _pair_add(g, got, name):
    _, rows, cols = g.shape
    hr = rows // 2
    tr = min(hr, 512)
    nb = hr // tr

    def body(g_ref, got_ref, p_ref, own_ref):
        val = (g_ref[...] + got_ref[...]).astype(BF16)
        p_ref[...] = val

        @pl.when(pl.program_id(1) == 2 * lax.axis_index("x") + lax.axis_index("y"))
        def _():
            own_ref[...] = val

    def mine(i, s):
        return (2 * lax.axis_index("x") + lax.axis_index("y"), i, 0)

    return pl.pallas_call(
        body, name=name, grid=(nb, N_CHIPS),
        in_specs=[pl.BlockSpec((None, tr, cols), lambda i, s: (s, lax.axis_index("c") * nb + i, 0)),
                  pl.BlockSpec((None, tr, cols), lambda i, s: (s, i, 0))],
        out_specs=[pl.BlockSpec((None, tr, cols), lambda i, s: (s, i, 0)), pl.BlockSpec((None, tr, cols), mine)],
        out_shape=[jax.ShapeDtypeStruct((N_CHIPS, hr, cols), BF16)] * 2,
        compiler_params=_cp(("parallel", "arbitrary")),
    )(g, got)


def _chip_sum(parts, name):
    _, hr, cols = parts.shape
    tr = min(hr, 256)
    nb = hr // tr

    def body(t_ref, o_ref):
        part = [t_ref[s].astype(F32) for s in range(N_CHIPS)]
        o_ref[...] = ((part[0] + part[1]) + part[2]) + part[3]

    return pl.pallas_call(
        body, name=name, grid=(nb,),
        in_specs=[pl.BlockSpec((N_CHIPS, tr, cols), lambda i: (0, i, 0))],
        out_specs=pl.BlockSpec((tr, cols), lambda i: (lax.axis_index("c") * nb + i, 0)),
        out_shape=jax.ShapeDtypeStruct((2 * hr, cols), F32),
        compiler_params=_cp(("parallel",)),
    )(parts)


class _Reduction:
    def __init__(self, tag, rows, first_id):
        self.tag, self.n, self.rows, self.first_id = tag, len(rows), rows, first_id
        self.plans = (_plan_pair_halves(self.n, rows), _plan_chip_parts(self.n), _plan_pair_fill(self.n, rows))
        self.flight = None

    def _name(self, what):
        return f"grad_{self.tag}_{what}"

    @staticmethod
    def _sibling():
        x, y, c, _ = _place()
        return [(x, y, 1 - c)]

    @staticmethod
    def _same_core_elsewhere():
        x, y, c, chips = _place()
        return [(*chip, c) for chip in chips]

    def start(self, gs):
        gots = [lax.empty((N_CHIPS, t.shape[1] // 2, t.shape[2]), F32) for t in gs]
        self.flight = _copies_start(self._name("pair_start"), list(gs) + gots, self.plans[0], self.n,
                                    (self.first_id, self._sibling))
        return self.flight[2]

    def after_pair(self, after):
        sems, bufs, _ = self.flight
        out = _copies_wait(self._name("pair_wait"), sems, bufs, self.plans[0], self.n, after)
        sums = [_pair_add(g, got, self._name(f"pair_add_{i}"))
                for i, (g, got) in enumerate(zip(out[:self.n], out[self.n:]))]
        self.flight = _copies_start(self._name("chip_start"), [p for p, _ in sums] + [own for _, own in sums],
                                    self.plans[1], 3 * self.n, (self.first_id + 1, self._same_core_elsewhere))
        return self.flight[2]

    def after_chips(self, after):
        sems, bufs, _ = self.flight
        out = _copies_wait(self._name("chip_wait"), sems, bufs, self.plans[1], 3 * self.n, after)
        fulls = [_chip_sum(t, self._name(f"chip_sum_{i}")) for i, t in enumerate(out[self.n:])]
        self.flight = _copies_start(self._name("fill_start"), fulls, self.plans[2], self.n,
                                    (self.first_id + 2, self._sibling))
        return self.flight[2]

    def finish(self, after):
        sems, bufs, _ = self.flight
        return _copies_wait(self._name("fill_wait"), sems, bufs, self.plans[2], self.n, after)


def _adamw_math(w, g, m, v):
    nm = ADAM_B1 * m + (1.0 - ADAM_B1) * g
    nv = ADAM_B2 * v + (1.0 - ADAM_B2) * (g * g)
    m_hat = nm / (1.0 - ADAM_B1 ** ADAM_STEP)
    v_hat = nv / (1.0 - ADAM_B2 ** ADAM_STEP)
    return -ADAM_LR * (m_hat / (jnp.sqrt(v_hat) + ADAM_EPS) + ADAM_WD * w), nm, nv


def _adamw(w, g, m, v, name, tr):
    r, cdim = w.shape

    def body(w_ref, g_ref, m_ref, v_ref, d_ref, nm_ref, nv_ref):
        d_ref[...], nm_ref[...], nv_ref[...] = _adamw_math(w_ref[...], g_ref[...], m_ref[...], v_ref[...])

    return pl.pallas_call(
        body, name=name, grid=(r // tr,), in_specs=[_rows(tr, cdim)] * 4, out_specs=[_rows(tr, cdim)] * 3,
        out_shape=[jax.ShapeDtypeStruct((r, cdim), F32)] * 3, compiler_params=_cp(("parallel",)),
    )(w, g, m, v)


def _adamw_small(w, parts, m, v, shapes):
    n_rows = w.shape[0]

    def scatter(src, outs):
        row = 0
        for (rows, cols), out in zip(shapes, outs):
            if cols == LANES:
                out[...] = src[row:row + rows, :]
            elif cols > LANES:
                per = cols // LANES
                for r in range(rows):
                    for cb in range(per):
                        out[r:r + 1, cb * LANES:(cb + 1) * LANES] = src[row + r * per + cb:row + r * per + cb + 1, :]
            else:
                per = LANES // cols
                for r in range(rows):
                    out[r:r + 1, :] = src[row + r // per:row + r // per + 1, (r % per) * cols:(r % per + 1) * cols]
            row += -(-rows * cols // LANES)

    def body(w_ref, p_ref, m_ref, v_ref, *rest):
        outs, scr = rest[:-4], rest[-4:]
        g = p_ref[0]
        for dev in range(1, N_DEV):
            g = g + p_ref[dev]
        scr[3][...] = g
        scr[0][...], scr[1][...], scr[2][...] = _adamw_math(w_ref[...], g, m_ref[...], v_ref[...])
        n = len(shapes)
        for kind in range(4):
            scatter(scr[kind], outs[kind * n:(kind + 1) * n])

    outs = pl.pallas_call(
        body, name="adamw_small", grid=(1,),
        in_specs=[_const(w.shape), _const(parts.shape), _const(w.shape), _const(w.shape)],
        out_specs=[_const(s) for s in shapes] * 4, out_shape=[jax.ShapeDtypeStruct(s, F32) for s in shapes] * 4,
        scratch_shapes=[pltpu.VMEM((n_rows, LANES), F32)] * 4,
        compiler_params=_cp(("arbitrary",)),
    )(w, parts, m, v)
    n = len(shapes)
    return [outs[kind * n:(kind + 1) * n] for kind in range(4)]


REPLICATED = (("norm_mix_pre", 1024), ("norm_mix_post", 1024), ("norm_ffn_pre", 1024), ("norm_ffn_post", 1024),
              ("rel_bias", 256), ("sinks", 8), ("rwkv_shift_mix", 1792), ("w0", 512), ("a0", 512), ("k_k", 512),
              ("k_a", 512), ("r_k", 512), ("ln_x_g", 512), ("ln_x_b", 512), ("conv_b", 8192))
SMALL_SHARDED = (("w_decay_up", LORA_DECAY, D_RWKV), ("w_iclr_up", LORA_ICLR, D_RWKV),
                 ("w_gate_up", LORA_GATE, D_RWKV), ("conv_w", 3, 2 * D_FF))
BIG = (("w_in", D_MODEL, 640), ("w_out", 256, D_MODEL), ("w_ffn_up", D_MODEL, 2048), ("w_ffn_down", 1024, D_MODEL))
PACK_ALIGN = 8 * LANES


def _pack(pieces):
    flat = []
    for t in pieces:
        t = t.reshape(-1)
        pad = (-t.shape[0]) % LANES
        flat.append(jnp.pad(t, (0, pad)) if pad else t)
    flat = jnp.concatenate(flat)
    pad = (-flat.shape[0]) % PACK_ALIGN
    return jnp.pad(flat, (0, pad)).reshape(-1, LANES)


def kernel(x, norm_mix_pre, norm_mix_post, norm_ffn_pre, norm_ffn_post, w_in, rel_bias, sinks, rwkv_shift_mix, w0, w_decay_up, a0, w_iclr_up, w_gate_up, k_k, k_a, r_k, ln_x_g, ln_x_b, w_out, w_ffn_up, conv_w, conv_b, w_ffn_down, loss_target, m_norm_mix_pre, m_norm_mix_post, m_norm_ffn_pre, m_norm_ffn_post, m_w_in, m_rel_bias, m_sinks, m_rwkv_shift_mix, m_w0, m_w_decay_up, m_a0, m_w_iclr_up, m_w_gate_up, m_k_k, m_k_a, m_r_k, m_ln_x_g, m_ln_x_b, m_w_out, m_w_ffn_up, m_conv_w, m_conv_b, m_w_ffn_down, v_norm_mix_pre, v_norm_mix_post, v_norm_ffn_pre, v_norm_ffn_post, v_w_in, v_rel_bias, v_sinks, v_rwkv_shift_mix, v_w0, v_w_decay_up, v_a0, v_w_iclr_up, v_w_gate_up, v_k_k, v_k_a, v_r_k, v_ln_x_g, v_ln_x_b, v_w_out, v_w_ffn_up, v_conv_w, v_conv_b, v_w_ffn_down):
    given = dict(locals())
    names = [n for n, _ in REPLICATED] + [n for n, _, _ in SMALL_SHARDED] + [n for n, _, _ in BIG]
    order = ["norm_mix_pre", "norm_mix_post", "norm_ffn_pre", "norm_ffn_post", "w_in", "rel_bias", "sinks",
             "rwkv_shift_mix", "w0", "w_decay_up", "a0", "w_iclr_up", "w_gate_up", "k_k", "k_a", "r_k", "ln_x_g",
             "ln_x_b", "w_out", "w_ffn_up", "conv_w", "conv_b", "w_ffn_down"]
    assert sorted(names) == sorted(order)

    big_sh = {n: given[n].reshape(a, b).astype(BF16) for n, a, b in BIG}
    small_sh = [given[n].reshape(r, c // N_CHIPS) for n, r, c in SMALL_SHARDED]
    gathered = _gather_weights([big_sh["w_in"]], small_sh)
    rest = ("w_out", "w_ffn_up", "w_ffn_down")
    win_st, rest_sh = lax.optimization_barrier((gathered[0], [big_sh[n] for n in rest]))
    sm = {n: given[n] for n, _ in REPLICATED}
    sm["r_k"] = r_k.reshape(N_Q_HEADS, HEAD_DIM)
    for (n, r, c), st in zip(SMALL_SHARDED, gathered[1:]):
        sm[n] = st.transpose(1, 0, 2).reshape(r, c)

    lands = [lax.empty((N_CHIPS,) + t.shape, BF16) for t in rest_sh]
    plan_w = _plan_gather(len(rest))
    n_w = N_CHIPS * len(rest)
    w_sems, w_bufs, token = _copies_start("gather_rest_start", rest_sh + lands, plan_w, n_w)
    sm["norm_mix_pre"] = norm_mix_pre + token[0:1, 0:1]

    def on_rest_weights(after):
        out = _copies_wait("gather_rest_wait", w_sems, w_bufs, plan_w, n_w, after)
        wout_st, wup_st, wdown_st = out[3:]
        return wout_st.reshape(D_MODEL, D_MODEL), wup_st, wdown_st.reshape(D_FF, D_MODEL)

    red_a = _Reduction("a", (1024, D_MODEL, 256), first_id=0)
    red_b = _Reduction("b", (D_MODEL,), first_id=3)

    def on_grads_a(gs):
        d_wdown, d_wup, d_wout = gs
        return red_a.start([d_wdown.reshape(N_CHIPS, 1024, D_MODEL), d_wup, d_wout.reshape(N_CHIPS, 256, D_MODEL)])

    handlers = {"rest_weights": on_rest_weights, "grads_a": on_grads_a, "seam_1": red_a.after_pair,
                "seam_2": red_a.after_chips, "grads_b": lambda g: red_b.start([g])}
    steps = _local_step(x[0], loss_target[0], sm, win_st)
    kind, payload = next(steps)
    while True:
        try:
            kind, payload = steps.send(handlers[kind](payload))
        except StopIteration as done:
            loss, grad_x, grads = done.value
            break

    small_names = [n for n, _ in REPLICATED] + [n for n, _, _ in SMALL_SHARDED]

    def shard_cols(t, s):
        return t[:, s * (t.shape[1] // N_CHIPS):(s + 1) * (t.shape[1] // N_CHIPS)]

    for_chip = jnp.stack([_pack([loss[0]] + [grads[n] for n, _ in REPLICATED]
                                + [shard_cols(grads[n], s) for n, _, _ in SMALL_SHARDED]) for s in range(N_CHIPS)])
    land = lax.empty((N_DEV,) + for_chip.shape[1:], F32)

    def plan_small(refs):
        x, y, c, _ = _place()
        out = []
        for rel in range(N_DEV):
            px, py, pc = x ^ (rel >> 2), y ^ ((rel >> 1) & 1), c ^ (rel & 1)
            out.append((refs[0].at[2 * px + py], refs[1].at[4 * x + 2 * y + c], (px, py, pc)))
        return out

    s_sems, s_bufs, s_token = _copies_start("grad_small_start", [for_chip, land], plan_small, N_DEV)

    red_b.after_pair([grad_x, s_token])
    g_out = {}
    g_out["w_ffn_down"], g_out["w_ffn_up"], g_out["w_out"] = red_a.finish(grad_x)

    delta, new_m, new_v = {}, {}, {}

    def update(n, a, b):
        delta[n], new_m[n], new_v[n] = _adamw(given[n].reshape(a, b), g_out[n], given["m_" + n].reshape(a, b),
                                              given["v_" + n].reshape(a, b), "adamw_" + n, 256)

    for n, a, b in BIG[1:]:
        update(n, a, b)
    done = [delta[n] for n, _, _ in BIG[1:]]
    red_b.after_chips(done)
    parts = _copies_wait("grad_small_wait", s_sems, s_bufs, plan_small, N_DEV, done)[1]
    no_param = jnp.zeros((LANES,), F32)
    packs = [_pack([no_param] + [given[pre + n] for n in small_names]) for pre in ("", "m_", "v_")]

    def piece_shape(n):
        shape = given[n].shape
        rows, cols = int(np.prod(shape[:-1])), shape[-1]
        whole = cols % LANES == 0 or (LANES % cols == 0 and (rows * cols) % LANES == 0 and cols >= HEAD_DIM)
        return (rows, cols) if whole else (-(-rows * cols // LANES), LANES)

    shapes = [(1, LANES)] + [piece_shape(n) for n in small_names]
    upd = _adamw_small(packs[0], parts, packs[1], packs[2], shapes)
    loss = upd[3][0][0, 0]
    for i, n in enumerate(small_names):
        shape = given[n].shape
        size = int(np.prod(shape))
        delta[n], new_m[n], new_v[n], g_out[n] = (u[1 + i].reshape(-1)[:size].reshape(shape) for u in upd)
    g_out["w_in"], = red_b.finish(upd[0][0])
    update(*BIG[0])

    def shaped(d):
        return [d[n].reshape(given[n].shape) for n in order]

    return (loss, grad_x.reshape(x.shape), *shaped(g_out), *shaped(delta), *shaped(new_m), *shaped(new_v))
```

```python
import math

import numpy as np
import jax
import jax.numpy as jnp
from jax import lax
from jax.experimental import pallas as pl
from jax.experimental.pallas import tpu as pltpu

F32 = jnp.float32
BF16 = jnp.bfloat16
MESH = pl.DeviceIdType.MESH

SEQ = 2048
D_MODEL = 1024
HEAD_DIM = 64
D_ATTN = 512
D_RWKV = 512
D_KV = 128
N_Q_HEADS = 8
N_KV_HEADS = 2
Q_PER_KV = 4
BLOCK = 128
N_BUCKETS = 32
MAX_DISTANCE = 128
LORA_DECAY = 64
LORA_ICLR = 64
LORA_GATE = 128
RWKV_COLS = 3 * D_RWKV + LORA_DECAY + LORA_ICLR + LORA_GATE
P_OFF = D_ATTN + 2 * D_KV
D_IN = P_OFF + RWKV_COLS
D_FF = 4096
NORM_EPS = 1e-6
GN_EPS = 64e-5
NEG_INF = -1e30
N_CHIPS = 4
N_DEV = 8
HEAD_SHIFT = HEAD_DIM.bit_length() - 1
BLOCK_SHIFT = BLOCK.bit_length() - 1

ADAM_LR = 0.001
ADAM_B1 = 0.9
ADAM_B2 = 0.999
ADAM_EPS = 1e-08
ADAM_WD = 0.01
ADAM_STEP = 10

VMEM_LIMIT = 52 * 1024 * 1024
LANES = 128


def _cp(sem=None, vmem=VMEM_LIMIT):
    kw = dict(vmem_limit_bytes=vmem)
    if sem is not None:
        kw["dimension_semantics"] = sem
    return pltpu.CompilerParams(**kw)


def _rows(tr, nc):
    return pl.BlockSpec((tr, nc), lambda i: (i, 0))


def _const(shape):
    return pl.BlockSpec(shape, lambda *_: (0,) * len(shape))


ANY = pl.BlockSpec(memory_space=pl.ANY)


def _split(x, n):
    parts = []
    for _ in range(n - 1):
        h = x.astype(BF16)
        parts.append(h)
        x = x - h.astype(F32)
    parts.append(x.astype(BF16))
    return parts


NN = (((1,), (0,)), ((), ()))
NT = (((1,), (1,)), ((), ()))
TN = (((0,), (0,)), ((), ()))


def _dot(a, b, dn=NN):
    return lax.dot_general(a, b, dn, preferred_element_type=F32)


def _dot_ind(x, ind_bf16, n=3):
    acc = None
    for part in _split(x, n):
        t = _dot(part, ind_bf16)
        acc = t if acc is None else acc + t
    return acc


def _head_ones(n):
    r = lax.broadcasted_iota(jnp.int32, (n, n), 0) >> HEAD_SHIFT
    c = lax.broadcasted_iota(jnp.int32, (n, n), 1) >> HEAD_SHIFT
    return jnp.where(r == c, 1.0, 0.0).astype(BF16)


def _matmul(a, b, mode, name, *, m, n, k, tm, tn, a_spec=None, b_spec=None, out=None):
    dn = {"nn": NN, "nt": NT, "tn": TN}[mode]

    def body(a_ref, b_ref, o_ref):
        o_ref[...] = _dot(a_ref[...], b_ref[...], dn)

    if a_spec is None:
        a_spec = pl.BlockSpec((k, tm), lambda i, j: (0, i)) if mode == "tn" else pl.BlockSpec((tm, k), lambda i, j: (i, 0))
    if b_spec is None:
        b_spec = pl.BlockSpec((tn, k), lambda i, j: (j, 0)) if mode == "nt" else pl.BlockSpec((k, tn), lambda i, j: (0, j))
    return pl.pallas_call(
        body, name=name, grid=(m // tm, n // tn),
        in_specs=[a_spec, b_spec],
        out_specs=pl.BlockSpec((tm, tn), lambda i, j: (i, j)) if out is None else out[1],
        out_shape=jax.ShapeDtypeStruct((m, n) if out is None else out[0], F32),
        compiler_params=_cp(("parallel", "parallel")),
    )(a, b)


def _matmul_nt_shards(a, b_st, name, *, m, n, tm, tn, a_spec, a_piece):
    ks = b_st.shape[2]

    def body(a_ref, b_ref, o_ref):
        acc = _dot(a_piece(a_ref, 0), b_ref[0], NT)
        for s in range(1, N_CHIPS):
            acc = acc + _dot(a_piece(a_ref, s), b_ref[s], NT)
        o_ref[...] = acc

    return pl.pallas_call(
        body, name=name, grid=(m // tm, n // tn),
        in_specs=[a_spec, pl.BlockSpec((N_CHIPS, tn, ks), lambda i, j: (0, j, 0))],
        out_specs=pl.BlockSpec((tm, tn), lambda i, j: (i, j)),
        out_shape=jax.ShapeDtypeStruct((m, n), F32),
        compiler_params=_cp(("parallel", "parallel")),
    )(a, b_st)


def _rstd(x):
    return lax.rsqrt(jnp.mean(x * x, axis=-1, keepdims=True) + NORM_EPS)


def _rms_bwd(x, r, g, dy):
    gy = dy * g
    return r * gy - x * ((r * r * r) * (jnp.sum(x * gy, axis=-1, keepdims=True) / x.shape[-1]))


TR = 256
TRN = 512


def _norm_cast(x, g, name):
    def body(x_ref, g_ref, h_ref):
        x = x_ref[...]
        h_ref[...] = (x * _rstd(x) * g_ref[...]).astype(BF16)

    return pl.pallas_call(
        body, name=name, grid=(SEQ // TRN,),
        in_specs=[_rows(TRN, D_MODEL), _const((1, D_MODEL))],
        out_specs=_rows(TRN, D_MODEL),
        out_shape=jax.ShapeDtypeStruct((SEQ, D_MODEL), BF16),
        compiler_params=_cp(("parallel",)),
    )(x, g)


def _mix_norm(x, mix, g2, g3):
    def body(x_ref, mix_ref, g2_ref, g3_ref, x2_ref, h3_ref):
        mixv = mix_ref[...]
        x2 = x_ref[...] + mixv * _rstd(mixv) * g2_ref[...]
        x2_ref[...] = x2
        h3_ref[...] = (x2 * _rstd(x2) * g3_ref[...]).astype(BF16)

    return pl.pallas_call(
        body, name="mix_norm", grid=(SEQ // TRN,),
        in_specs=[_rows(TRN, D_MODEL), _rows(TRN, D_MODEL), _const((1, D_MODEL)), _const((1, D_MODEL))],
        out_specs=[_rows(TRN, D_MODEL), _rows(TRN, D_MODEL)],
        out_shape=[jax.ShapeDtypeStruct((SEQ, D_MODEL), F32), jax.ShapeDtypeStruct((SEQ, D_MODEL), BF16)],
        compiler_params=_cp(("parallel",)),
    )(x, mix, g2, g3)


def _loss_head(x2, f, g4, target):
    def body(x2_ref, f_ref, g4_ref, t_ref, loss_ref, dy_ref, df_ref, dg_ref):
        i = pl.program_id(0)
        f = f_ref[...]
        g4 = g4_ref[...]
        r = _rstd(f)
        e = x2_ref[...] + f * r * g4 - t_ref[...]
        dy = e * (1.0 / D_MODEL)
        dy_ref[...] = dy
        df_ref[...] = _rms_bwd(f, r, g4, dy).astype(BF16)
        part = 0.5 * jnp.sum(jnp.sum(e * e, axis=-1, keepdims=True), axis=0, keepdims=True) * (1.0 / D_MODEL)
        dg = jnp.sum(dy * f * r, axis=0, keepdims=True)

        @pl.when(i == 0)
        def _():
            loss_ref[...] = jnp.zeros_like(loss_ref)
            dg_ref[...] = jnp.zeros_like(dg_ref)

        loss_ref[...] += jnp.broadcast_to(part, loss_ref.shape)
        dg_ref[...] += dg

    return pl.pallas_call(
        body, name="loss_head", grid=(SEQ // TRN,),
        in_specs=[_rows(TRN, D_MODEL), _rows(TRN, D_MODEL), _const((1, D_MODEL)), _rows(TRN, D_MODEL)],
        out_specs=[_const((8, LANES)), _rows(TRN, D_MODEL), _rows(TRN, D_MODEL), _const((1, D_MODEL))],
        out_shape=[jax.ShapeDtypeStruct((8, LANES), F32), jax.ShapeDtypeStruct((SEQ, D_MODEL), F32),
                   jax.ShapeDtypeStruct((SEQ, D_MODEL), BF16), jax.ShapeDtypeStruct((1, D_MODEL), F32)],
        compiler_params=_cp(("arbitrary",)),
    )(x2, f, g4, target)


def _mid_bwd(x2, mix, dy, dh3, g2, g3):
    def body(x2_ref, mix_ref, dy_ref, dh3_ref, g2_ref, g3_ref, dx2_ref, dmix_ref, dg2_ref, dg3_ref):
        i = pl.program_id(0)
        x2 = x2_ref[...]
        mixv = mix_ref[...]
        dh3 = dh3_ref[...]
        r3 = _rstd(x2)
        dx2 = dy_ref[...] + _rms_bwd(x2, r3, g3_ref[...], dh3)
        dx2_ref[...] = dx2
        r2 = _rstd(mixv)
        dmix_ref[...] = _rms_bwd(mixv, r2, g2_ref[...], dx2).astype(BF16)

        @pl.when(i == 0)
        def _():
            dg2_ref[...] = jnp.zeros_like(dg2_ref)
            dg3_ref[...] = jnp.zeros_like(dg3_ref)

        dg3_ref[...] += jnp.sum(dh3 * x2 * r3, axis=0, keepdims=True)
        dg2_ref[...] += jnp.sum(dx2 * mixv * r2, axis=0, keepdims=True)

    return pl.pallas_call(
        body, name="mid_bwd", grid=(SEQ // TRN,),
        in_specs=[_rows(TRN, D_MODEL)] * 4 + [_const((1, D_MODEL))] * 2,
        out_specs=[_rows(TRN, D_MODEL), _rows(TRN, D_MODEL), _const((1, D_MODEL)), _const((1, D_MODEL))],
        out_shape=[jax.ShapeDtypeStruct((SEQ, D_MODEL), F32), jax.ShapeDtypeStruct((SEQ, D_MODEL), BF16),
                   jax.ShapeDtypeStruct((1, D_MODEL), F32), jax.ShapeDtypeStruct((1, D_MODEL), F32)],
        compiler_params=_cp(("arbitrary",)),
    )(x2, mix, dy, dh3, g2, g3)


def _first_bwd(x, dx2, dh1, g1):
    def body(x_ref, dx2_ref, dh1_ref, g1_ref, dx_ref, dg1_ref):
        i = pl.program_id(0)
        x = x_ref[...]
        dh1 = dh1_ref[...]
        r = _rstd(x)
        dx_ref[...] = dx2_ref[...] + _rms_bwd(x, r, g1_ref[...], dh1)

        @pl.when(i == 0)
        def _():
            dg1_ref[...] = jnp.zeros_like(dg1_ref)

        dg1_ref[...] += jnp.sum(dh1 * x * r, axis=0, keepdims=True)

    return pl.pallas_call(
        body, name="first_bwd", grid=(SEQ // TRN,),
        in_specs=[_rows(TRN, D_MODEL)] * 3 + [_const((1, D_MODEL))],
        out_specs=[_rows(TRN, D_MODEL), _const((1, D_MODEL))],
        out_shape=[jax.ShapeDtypeStruct((SEQ, D_MODEL), F32), jax.ShapeDtypeStruct((1, D_MODEL), F32)],
        compiler_params=_cp(("arbitrary",)),
    )(x, dx2, dh1, g1)


TC = 256
N_CB = D_FF // TC
GELU_C = math.sqrt(2.0 / math.pi)


def _shift_down(u, s):
    rolled = pltpu.roll(u, s, 0)
    row = lax.broadcasted_iota(jnp.int32, u.shape, 0)
    return jnp.where(row >= s, rolled, 0.0)


def _shift_up(u, s):
    n = u.shape[0]
    rolled = pltpu.roll(u, n - s, 0)
    row = lax.broadcasted_iota(jnp.int32, u.shape, 0)
    return jnp.where(row < n - s, rolled, 0.0)


def _conv3(u, w, b):
    return b + w[0:1] * _shift_down(u, 2) + w[1:2] * _shift_down(u, 1) + w[2:3] * u


def _gelu_and_grad(x):
    inner = GELU_C * (x + 0.044715 * (x * x * x))
    t = jnp.tanh(inner)
    gelu = 0.5 * x * (1.0 + t)
    dgelu = 0.5 * (1.0 + t) + 0.5 * x * (1.0 - t * t) * (GELU_C * (1.0 + 3 * 0.044715 * (x * x)))
    return gelu, dgelu


def _ffn_specs():
    col = lambda off: pl.BlockSpec((SEQ, TC), lambda *g: (0, g[-1] + off))
    w = lambda off: pl.BlockSpec((3, TC), lambda *g: (0, g[-1] + off))
    b = lambda off: pl.BlockSpec((1, TC), lambda *g: (0, g[-1] + off))
    return col, w, b


def _ffn_up_act(h3, wup_st, conv_w, conv_b):
    col, w, b = _ffn_specs()
    per_shard = wup_st.shape[2] // TC

    def body(h_ref, upg_ref, upv_ref, wg_ref, wv_ref, bg_ref, bv_ref, ug_ref, uv_ref, gate_ref, val_ref, act_ref):
        h = h_ref[...]
        ug = _dot(h, upg_ref[...])
        uv = _dot(h, upv_ref[...])
        ug_ref[...] = ug
        uv_ref[...] = uv
        gate = _conv3(ug, wg_ref[...], bg_ref[...])
        val = _conv3(uv, wv_ref[...], bv_ref[...])
        gate_ref[...] = gate
        val_ref[...] = val
        act_ref[...] = (_gelu_and_grad(gate)[0] * val).astype(BF16)

    return pl.pallas_call(
        body, name="ffn_up_act", grid=(N_CB,),
        in_specs=[_const((SEQ, D_MODEL)),
                  pl.BlockSpec((None, D_MODEL, TC), lambda j: (j // per_shard, 0, j % per_shard)),
                  pl.BlockSpec((None, D_MODEL, TC), lambda j: (2 + j // per_shard, 0, j % per_shard)),
                  w(0), w(N_CB), b(0), b(N_CB)],
        out_specs=[col(0)] * 5,
        out_shape=[jax.ShapeDtypeStruct((SEQ, D_FF), F32)] * 4 + [jax.ShapeDtypeStruct((SEQ, D_FF), BF16)],
        compiler_params=_cp(("parallel",)),
    )(h3, wup_st, wup_st, conv_w, conv_w, conv_b, conv_b)


def _ffn_act_bwd(u_gate, u_val, gate, val, df, wdown, conv_w):
    col, w, _ = _ffn_specs()
    both = lambda rows: pl.BlockSpec((2, rows, TC), lambda j: (0, 0, j))

    def body(ug_ref, uv_ref, gate_ref, val_ref, df_ref, wd_ref, wg_ref, wv_ref, du_ref, dw_ref, db_ref):
        da = _dot(df_ref[...], wd_ref[...], NT)
        gelu, dgelu = _gelu_and_grad(gate_ref[...])
        halves = ((da * val_ref[...] * dgelu, ug_ref, wg_ref[...]), (da * gelu, uv_ref, wv_ref[...]))
        for h, (duc, u_ref, wh) in enumerate(halves):
            uh = u_ref[...]
            up1, up2 = _shift_up(duc, 1), _shift_up(duc, 2)
            du_ref[h] = (wh[2:3] * duc + wh[1:2] * up1 + wh[0:1] * up2).astype(BF16)
            db_ref[h] = jnp.sum(duc, axis=0, keepdims=True)
            dw_ref[h] = jnp.concatenate(
                [jnp.sum(up2 * uh, axis=0, keepdims=True), jnp.sum(up1 * uh, axis=0, keepdims=True),
                 jnp.sum(duc * uh, axis=0, keepdims=True)], axis=0)

    return pl.pallas_call(
        body, name="ffn_act_bwd", grid=(N_CB,),
        in_specs=[col(0)] * 4 + [_const((SEQ, D_MODEL)), pl.BlockSpec((TC, D_MODEL), lambda j: (j, 0)), w(0), w(N_CB)],
        out_specs=[both(SEQ), both(3), both(1)],
        out_shape=[jax.ShapeDtypeStruct((2, SEQ, D_FF), BF16), jax.ShapeDtypeStruct((2, 3, D_FF), F32),
                   jax.ShapeDtypeStruct((2, 1, D_FF), F32)],
        compiler_params=_cp(("parallel",)),
    )(u_gate, u_val, gate, val, df, wdown, conv_w, conv_w)


def _t5_onehot():
    rel = (np.arange(BLOCK)[:, None] + BLOCK) - np.arange(2 * BLOCK)[None, :]
    n = np.maximum(rel, 0)
    max_exact = N_BUCKETS // 2
    large = max_exact + (np.log(np.maximum(n, 1).astype(np.float32) / np.float32(max_exact))
                         / np.float32(math.log(MAX_DISTANCE / max_exact))
                         * np.float32(N_BUCKETS - max_exact)).astype(np.int32)
    large = np.minimum(large, N_BUCKETS - 1)
    bucket = np.where(n < max_exact, n, large).reshape(-1)
    return (bucket[None, :] == np.arange(N_BUCKETS)[:, None]).astype(np.float32)


N_REL = BLOCK * 2 * BLOCK


def _bias_table(rel_bias_t, onehot):
    def body(rb_ref, oh_ref, o_ref):
        o_ref[...] = _dot_ind(rb_ref[...], oh_ref[...])

    return pl.pallas_call(
        body, name="bias_table", grid=(1,),
        in_specs=[_const((N_Q_HEADS, N_BUCKETS)), _const((N_BUCKETS, N_REL))],
        out_specs=_const((N_Q_HEADS, N_REL)),
        out_shape=jax.ShapeDtypeStruct((N_Q_HEADS, N_REL), F32),
        compiler_params=_cp(("arbitrary",)),
    )(rel_bias_t, onehot)


def _bias_table_bwd(dbias, onehot):
    def body(db_ref, oh_ref, o_ref):
        acc = None
        for part in _split(db_ref[...], 3):
            t = _dot(part, oh_ref[...], NT)
            acc = t if acc is None else acc + t
        o_ref[...] = acc

    return pl.pallas_call(
        body, name="bias_table_bwd", grid=(1,),
        in_specs=[_const((N_Q_HEADS, N_REL)), _const((N_BUCKETS, N_REL))],
        out_specs=_const((N_Q_HEADS, N_BUCKETS)),
        out_shape=jax.ShapeDtypeStruct((N_Q_HEADS, N_BUCKETS), F32),
        compiler_params=_cp(("arbitrary",)),
    )(dbias, onehot)


def _attn_pieces(n, q, kvp, kvc, bias_ref, sinks_ref, hk):
    qi = lax.broadcasted_iota(jnp.int32, (BLOCK, 2 * BLOCK), 0)
    kj = lax.broadcasted_iota(jnp.int32, (BLOCK, 2 * BLOCK), 1)
    rel = qi + BLOCK - kj
    first_key = jnp.where(n > 0, 0, BLOCK)
    ok = jnp.where(rel >= 0, jnp.where(rel < BLOCK, jnp.where(kj >= first_key, 1.0, 0.0), 0.0), 0.0)
    ok4 = jnp.concatenate([ok] * Q_PER_KV, axis=0) > 0.5
    c0 = hk * HEAD_DIM
    kcat = jnp.concatenate([kvp[:, c0:c0 + HEAD_DIM], kvc[:, c0:c0 + HEAD_DIM]], axis=0).astype(BF16)
    vcat = jnp.concatenate([kvp[:, D_KV + c0:D_KV + c0 + HEAD_DIM], kvc[:, D_KV + c0:D_KV + c0 + HEAD_DIM]],
                           axis=0).astype(BF16)
    q0 = hk * Q_PER_KV * HEAD_DIM
    qs = jnp.concatenate([q[:, q0 + g * HEAD_DIM:q0 + (g + 1) * HEAD_DIM] for g in range(Q_PER_KV)],
                         axis=0).astype(BF16)
    s = _dot(qs, kcat, NT) * (HEAD_DIM ** -0.5) + bias_ref[hk]
    s = jnp.where(ok4, s, NEG_INF)
    row = lax.broadcasted_iota(jnp.int32, (Q_PER_KV * BLOCK, 1), 0)
    sink = jnp.zeros((Q_PER_KV * BLOCK, 1), F32)
    for g in range(Q_PER_KV):
        sink = jnp.where((row >> BLOCK_SHIFT) == g, sinks_ref[hk * Q_PER_KV + g], sink)
    m = jnp.maximum(jnp.max(s, axis=-1, keepdims=True), sink)
    p = jnp.exp(s - m)
    es = jnp.exp(sink - m)
    inv = 1.0 / (jnp.sum(p, axis=-1, keepdims=True) + es)
    return qs, kcat, vcat, p * inv, es * inv


def _attn_in_specs():
    return [pl.BlockSpec((BLOCK, D_ATTN), lambda n: (n, 0)),
            pl.BlockSpec((BLOCK, 2 * D_KV), lambda n: (jnp.maximum(n - 1, 0), D_ATTN // (2 * D_KV))),
            pl.BlockSpec((BLOCK, 2 * D_KV), lambda n: (n, D_ATTN // (2 * D_KV))),
            _const((N_KV_HEADS, Q_PER_KV * BLOCK, 2 * BLOCK)),
            pl.BlockSpec(memory_space=pltpu.SMEM)]


def _unstack_heads(t):
    return jnp.concatenate([t[g * BLOCK:(g + 1) * BLOCK] for g in range(Q_PER_KV)], axis=1)


def _attn_fwd(proj, bias, sinks):
    def body(q_ref, kvp_ref, kvc_ref, bias_ref, sinks_ref, o_ref):
        n = pl.program_id(0)
        q, kvp, kvc = q_ref[...], kvp_ref[...], kvc_ref[...]
        outs = []
        for hk in range(N_KV_HEADS):
            _, _, vcat, probs, _ = _attn_pieces(n, q, kvp, kvc, bias_ref, sinks_ref, hk)
            outs.append(_unstack_heads(_dot(probs.astype(BF16), vcat)))
        o_ref[...] = jnp.concatenate(outs, axis=1)

    return pl.pallas_call(
        body, name="attn_fwd", grid=(SEQ // BLOCK,),
        in_specs=_attn_in_specs(),
        out_specs=pl.BlockSpec((BLOCK, D_ATTN), lambda n: (n, 0)),
        out_shape=jax.ShapeDtypeStruct((SEQ, D_ATTN), F32),
        compiler_params=_cp(("parallel",)),
    )(proj, proj, proj, bias, sinks)


def _attn_bwd(proj, bias, sinks, dcat):
    nb = SEQ // BLOCK

    def body(q_ref, kvp_ref, kvc_ref, bias_ref, sinks_ref, do_ref, dq_ref, dkv_ref, dbias_ref, dsink_ref, dsacc):
        n = pl.program_id(0)

        @pl.when(n == 0)
        def _():
            dkv_ref[...] = jnp.zeros_like(dkv_ref)
            dbias_ref[...] = jnp.zeros_like(dbias_ref)
            dsacc[...] = jnp.zeros_like(dsacc)

        q, kvp, kvc = q_ref[...], kvp_ref[...], kvc_ref[...]
        do_all = do_ref[...]
        dqs, dks, dvs = [], [], []
        for hk in range(N_KV_HEADS):
            qs, kcat, vcat, probs, psink = _attn_pieces(n, q, kvp, kvc, bias_ref, sinks_ref, hk)
            q0 = hk * Q_PER_KV * HEAD_DIM
            do = jnp.concatenate([do_all[:, q0 + g * HEAD_DIM:q0 + (g + 1) * HEAD_DIM] for g in range(Q_PER_KV)],
                                 axis=0).astype(BF16)
            dprobs = _dot(do, vcat, NT)
            dvs.append(_dot(probs.astype(BF16), do, TN))
            rowdot = jnp.sum(probs * dprobs, axis=-1, keepdims=True)
            ds = probs * (dprobs - rowdot)
            dsacc[hk] += -psink * rowdot
            dbias_ref[hk] += ds
            dsb = (ds * (HEAD_DIM ** -0.5)).astype(BF16)
            dqs.append(_unstack_heads(_dot(dsb, kcat)))
            dks.append(_dot(dsb, qs, TN))
        dq_ref[...] = jnp.concatenate(dqs, axis=1)
        upd = jnp.concatenate(dks + dvs, axis=1)
        cur = pl.multiple_of(n * BLOCK, BLOCK)
        dkv_ref[pl.ds(cur, BLOCK), :] += upd[BLOCK:]

        @pl.when(n > 0)
        def _():
            prev = pl.multiple_of((n - 1) * BLOCK, BLOCK)
            dkv_ref[pl.ds(prev, BLOCK), :] += upd[:BLOCK]

        @pl.when(n == nb - 1)
        def _():
            for hk in range(N_KV_HEADS):
                for g in range(Q_PER_KV):
                    tot = jnp.sum(dsacc[hk, g * BLOCK:(g + 1) * BLOCK, :], axis=0, keepdims=True)
                    h = hk * Q_PER_KV + g
                    dsink_ref[h:h + 1, :] = jnp.broadcast_to(tot, (1, LANES))

    return pl.pallas_call(
        body, name="attn_bwd", grid=(nb,),
        in_specs=_attn_in_specs() + [pl.BlockSpec((BLOCK, D_ATTN), lambda n: (n, 0))],
        out_specs=[pl.BlockSpec((BLOCK, D_ATTN), lambda n: (n, 0)), _const((SEQ, 2 * D_KV)),
                   _const((N_KV_HEADS, Q_PER_KV * BLOCK, 2 * BLOCK)), _const((N_Q_HEADS, LANES))],
        out_shape=[jax.ShapeDtypeStruct((SEQ, D_ATTN), F32), jax.ShapeDtypeStruct((SEQ, 2 * D_KV), F32),
                   jax.ShapeDtypeStruct((N_KV_HEADS, Q_PER_KV * BLOCK, 2 * BLOCK), F32),
                   jax.ShapeDtypeStruct((N_Q_HEADS, LANES), F32)],
        scratch_shapes=[pltpu.VMEM((N_KV_HEADS, Q_PER_KV * BLOCK, 1), F32)],
        compiler_params=_cp(("arbitrary",)),
    )(proj, proj, proj, bias, sinks, dcat)


@jax.custom_vjp
def _head_sum(x):
    ones = _head_ones(LANES)
    return jnp.concatenate([_dot_ind(x[:, c:c + LANES], ones, 2) for c in range(0, x.shape[-1], LANES)], axis=1)


_head_sum.defvjp(lambda x: (_head_sum(x), None), lambda _, ct: (_head_sum(ct),))


@jax.custom_vjp
def _bdot(a, w):
    return _dot(a.astype(BF16), w.astype(BF16))


def _bdot_bwd(res, ct):
    a, w = res
    ctb = ct.astype(BF16)
    return _dot(ctb, w.astype(BF16), NT), _dot(a.astype(BF16), ctb, TN)


_bdot.defvjp(lambda a, w: (_bdot(a, w), (a, w)), _bdot_bwd)


def _sigmoid(x):
    return 0.5 * (jnp.tanh(0.5 * x) + 1.0)


def _softplus(x):
    return jnp.maximum(x, 0.0) + jnp.log(1.0 + jnp.exp(-jnp.abs(x)))


def _rwkv_core(r, k, v, zwa, zg, w0, wdu, a0, wiu, wgu, k_k, k_a):
    w_log = -_softplus(-(w0 + _bdot(jnp.tanh(zwa), wdu))) - 0.5
    decay = jnp.exp(-jnp.exp(w_log))
    a = _sigmoid(a0 + _bdot(zwa, wiu))
    g = _bdot(_sigmoid(zg), wgu)
    kk = k * k_k
    kk = kk / jnp.maximum(jnp.sqrt(_head_sum(kk * kk)), 1e-12)
    k2 = k * (1.0 + (a - 1.0) * k_a)
    return r, decay, k2, v, -kk, kk * a, g


def _rwkv_out(o, r, k2, v, g, lng, lnb, rk):
    mu = _head_sum(o) * (1.0 / HEAD_DIM)
    d = o - mu
    var = _head_sum(d * d) * (1.0 / HEAD_DIM)
    on = d * lax.rsqrt(var + GN_EPS) * lng + lnb
    bonus = _head_sum(r * k2 * rk) * v
    return (on + bonus) * g


P_SPLITS = (0, 512, 1024, 1536, 1664, 1792)
N_PREP_PARAMS = 7
HALO = 8


def _shifted_pieces(i, p_ref, halo_ref, mix_ref):
    p = p_ref[:, P_OFF:]
    prev_row = halo_ref[HALO - 1:HALO, P_OFF:] * jnp.where(i > 0, 1.0, 0.0)
    row = lax.broadcasted_iota(jnp.int32, p.shape, 0)
    pprev = jnp.where(row == 0, prev_row, pltpu.roll(p, 1, 0))
    delta = pprev - p
    ps = p + delta * mix_ref[...]
    return [ps[:, a:b] for a, b in zip(P_SPLITS[:-1], P_SPLITS[1:])], delta


def _prep_in_specs():
    return [_rows(TR, D_IN),
            pl.BlockSpec((HALO, D_IN), lambda i: (jnp.maximum(i * (TR // HALO) - 1, 0), 0)),
            _const((1, RWKV_COLS)), _const((1, D_RWKV)), _const((LANES, D_RWKV)), _const((1, D_RWKV)),
            _const((LANES, D_RWKV)), _const((LANES, D_RWKV)), _const((1, D_RWKV)), _const((1, D_RWKV))]


def _rwkv_prep(proj, mix, prm):
    def body(p_ref, halo_ref, mix_ref, *refs):
        prm_refs, outs = refs[:N_PREP_PARAMS], refs[N_PREP_PARAMS:]
        pieces, _ = _shifted_pieces(pl.program_id(0), p_ref, halo_ref, mix_ref)
        vals = _rwkv_core(*pieces, *[t[...] for t in prm_refs])
        for ref, val in zip(outs, vals):
            ref[...] = val

    return pl.pallas_call(
        body, name="rwkv_prep", grid=(SEQ // TR,),
        in_specs=_prep_in_specs(),
        out_specs=[_rows(TR, D_RWKV)] * 7,
        out_shape=[jax.ShapeDtypeStruct((SEQ, D_RWKV), F32)] * 7,
        compiler_params=_cp(("parallel",)),
    )(proj, proj, mix, *prm)


def _rwkv_prep_bwd(proj, mix, prm, cts):
    def body(p_ref, halo_ref, mix_ref, *refs):
        i = pl.program_id(0)
        prm_refs = refs[:N_PREP_PARAMS]
        ct_refs = refs[N_PREP_PARAMS:N_PREP_PARAMS + 10]
        dps_ref, dmix_ref = refs[N_PREP_PARAMS + 10:N_PREP_PARAMS + 12]
        dprm_refs = refs[N_PREP_PARAMS + 12:]
        pieces, delta = _shifted_pieces(i, p_ref, halo_ref, mix_ref)
        _, vjp = jax.vjp(_rwkv_core, *pieces, *[t[...] for t in prm_refs])
        dr1, dr2, dw, dk1, dk2, dv1, dv2, dkkn, db, dg = [t[...] for t in ct_refs]
        grads = vjp((dr1 + dr2, dw, dk1 + dk2, dv1 + dv2, dkkn, db, dg))
        dps = jnp.concatenate(grads[:5], axis=1)
        dps_ref[...] = dps

        @pl.when(i == 0)
        def _():
            dmix_ref[...] = jnp.zeros_like(dmix_ref)
            for ref in dprm_refs:
                ref[...] = jnp.zeros_like(ref)

        dmix_ref[...] += jnp.sum(dps * delta, axis=0, keepdims=True)
        for ref, gval in zip(dprm_refs, grads[5:]):
            ref[...] += gval

    prm_shapes = [(1, D_RWKV), (LANES, D_RWKV), (1, D_RWKV), (LANES, D_RWKV), (LANES, D_RWKV), (1, D_RWKV), (1, D_RWKV)]
    return pl.pallas_call(
        body, name="rwkv_prep_bwd", grid=(SEQ // TR,),
        in_specs=_prep_in_specs() + [_rows(TR, D_RWKV)] * 10,
        out_specs=[_rows(TR, RWKV_COLS), _const((1, RWKV_COLS))] + [_const(s) for s in prm_shapes],
        out_shape=[jax.ShapeDtypeStruct((SEQ, RWKV_COLS), F32), jax.ShapeDtypeStruct((1, RWKV_COLS), F32)]
        + [jax.ShapeDtypeStruct(s, F32) for s in prm_shapes],
        compiler_params=_cp(("arbitrary",)),
    )(proj, proj, mix, *prm, *cts)


def _rwkv_post(o, r, k2, v, g, lng, lnb, rk, attn):
    def body(o_ref, r_ref, k_ref, v_ref, g_ref, lng_ref, lnb_ref, rk_ref, attn_ref, cat_ref):
        rw = _rwkv_out(*[t[...] for t in (o_ref, r_ref, k_ref, v_ref, g_ref, lng_ref, lnb_ref, rk_ref)])
        cat_ref[...] = jnp.concatenate([attn_ref[...], rw], axis=1).astype(BF16)

    return pl.pallas_call(
        body, name="rwkv_post", grid=(SEQ // TR,),
        in_specs=[_rows(TR, D_RWKV)] * 5 + [_const((1, D_RWKV))] * 3 + [_rows(TR, D_ATTN)],
        out_specs=_rows(TR, D_MODEL),
        out_shape=jax.ShapeDtypeStruct((SEQ, D_MODEL), BF16),
        compiler_params=_cp(("parallel",)),
    )(o, r, k2, v, g, lng, lnb, rk, attn)


def _rwkv_post_bwd(o, r, k2, v, g, lng, lnb, rk, dcat):
    def body(o_ref, r_ref, k_ref, v_ref, g_ref, lng_ref, lnb_ref, rk_ref, dcat_ref,
             do_ref, dr_ref, dk_ref, dv_ref, dg_ref, dlng_ref, dlnb_ref, drk_ref):
        i = pl.program_id(0)
        args = [t[...] for t in (o_ref, r_ref, k_ref, v_ref, g_ref, lng_ref, lnb_ref, rk_ref)]
        _, vjp = jax.vjp(_rwkv_out, *args)
        grads = vjp(dcat_ref[:, D_ATTN:])
        for ref, gval in zip((do_ref, dr_ref, dk_ref, dv_ref, dg_ref), grads[:5]):
            ref[...] = gval

        @pl.when(i == 0)
        def _():
            for ref in (dlng_ref, dlnb_ref, drk_ref):
                ref[...] = jnp.zeros_like(ref)

        for ref, gval in zip((dlng_ref, dlnb_ref, drk_ref), grads[5:]):
            ref[...] += gval

    return pl.pallas_call(
        body, name="rwkv_post_bwd", grid=(SEQ // TR,),
        in_specs=[_rows(TR, D_RWKV)] * 5 + [_const((1, D_RWKV))] * 3 + [_rows(TR, D_MODEL)],
        out_specs=[_rows(TR, D_RWKV)] * 5 + [_const((1, D_RWKV))] * 3,
        out_shape=[jax.ShapeDtypeStruct((SEQ, D_RWKV), F32)] * 5 + [jax.ShapeDtypeStruct((1, D_RWKV), F32)] * 3,
        compiler_params=_cp(("arbitrary",)),
    )(o, r, k2, v, g, lng, lnb, rk, dcat)


def _assemble_dproj(dq, dkv, dps, mix):
    last = SEQ // HALO - 1

    def body(dq_ref, dkv_ref, dps_ref, nxt_ref, mix_ref, o_ref):
        i = pl.program_id(0)
        dps = dps_ref[...]
        mixv = mix_ref[...]
        nxt_row = nxt_ref[0:1, :] * jnp.where(i < SEQ // TR - 1, 1.0, 0.0)
        row = lax.broadcasted_iota(jnp.int32, dps.shape, 0)
        up = jnp.where(row == TR - 1, nxt_row, pltpu.roll(dps, TR - 1, 0))
        dp = dps * (1.0 - mixv) + up * mixv
        o_ref[...] = jnp.concatenate([dq_ref[...], dkv_ref[...], dp], axis=1).astype(BF16)

    return pl.pallas_call(
        body, name="assemble_dproj", grid=(SEQ // TR,),
        in_specs=[_rows(TR, D_ATTN), _rows(TR, 2 * D_KV), _rows(TR, RWKV_COLS),
                  pl.BlockSpec((HALO, RWKV_COLS), lambda i: (jnp.minimum((i + 1) * (TR // HALO), last), 0)),
                  _const((1, RWKV_COLS))],
        out_specs=_rows(TR, D_IN),
        out_shape=jax.ShapeDtypeStruct((SEQ, D_IN), BF16),
        compiler_params=_cp(("parallel",)),
    )(dq, dkv, dps, dps, mix)


N_PAIR = D_RWKV // LANES
CHUNK = 64
N_CHUNK = SEQ // CHUNK
GROUP = 16
STATE = (N_PAIR, HEAD_DIM, LANES)


def _lane_sums(lhs_tiles, ones2):
    out = _dot(jnp.concatenate(lhs_tiles, axis=0), ones2)
    return [out[i * HEAD_DIM:(i + 1) * HEAD_DIM] for i in range(len(lhs_tiles))]


def _seg_sum(xs, ones2):
    return _lane_sums([jnp.concatenate(_split(x, 2), axis=1) for x in xs], ones2)


def _seg_sum_rows(xs, ones2):
    out = _dot(jnp.concatenate(_split(jnp.concatenate(xs, axis=0), 2), axis=1), ones2)
    return [out[i * GROUP:(i + 1) * GROUP] for i in range(len(xs))]


def _col_form(rows, diag, ones2):
    zero = jnp.zeros((HEAD_DIM, LANES), BF16)
    tiles = []
    for row in rows:
        hi = row.astype(BF16)
        lo = (row - hi.astype(F32)).astype(BF16)
        tiles.append(jnp.concatenate(
            [jnp.where(diag, jnp.broadcast_to(part, (HEAD_DIM, LANES)), zero) for part in (hi, lo)], axis=1))
    return _lane_sums(tiles, ones2)


def _scan_consts():
    ones2 = jnp.concatenate([_head_ones(LANES)] * 2, axis=0)
    sub = lax.broadcasted_iota(jnp.int32, (HEAD_DIM, LANES), 0)
    lane_in_head = lax.broadcasted_iota(jnp.int32, (HEAD_DIM, LANES), 1) & (HEAD_DIM - 1)
    return ones2, lane_in_head == sub, lane_in_head


def _rows_of_columns(tile):
    t = tile.T
    return jnp.concatenate([t[:CHUNK], t[HEAD_DIM:HEAD_DIM + CHUNK]], axis=1)


def _pair(j):
    return slice(j * LANES, (j + 1) * LANES)


def _scan_fwd(r, w, k, v, kkn, b):
    def body(r_ref, w_ref, k_ref, v_ref, kkn_ref, b_ref, o_ref, st_ref, sa_ref, s_scr):
        c = pl.program_id(0)
        ones2, diag, lane_in_head = _scan_consts()

        @pl.when(c == 0)
        def _():
            s_scr[...] = jnp.zeros_like(s_scr)

        def group(gi, carry):
            row0 = pl.multiple_of(gi * GROUP, GROUP)
            states, ocols = list(carry[:N_PAIR]), list(carry[N_PAIR:])
            tiles = [[t[pl.ds(row0, GROUP), _pair(j)] for t in (r_ref, w_ref, k_ref, v_ref, kkn_ref, b_ref)]
                     for j in range(N_PAIR)]
            def row(j, name, u):
                return tiles[j]["rwkvnb".index(name)][u:u + 1]

            def emit_out(u, after):
                outs = _seg_sum([s[j] * row(j, "r", u + d) for d, s in enumerate(after) for j in range(N_PAIR)], ones2)
                for d in range(2):
                    here = lane_in_head == gi * GROUP + u + d
                    for j in range(N_PAIR):
                        ocols[j] = jnp.where(here, outs[d * N_PAIR + j], ocols[j])

            def vcols_of(u):
                cols = _col_form([row(j, "v", u + d) for d in range(2) for j in range(N_PAIR)], diag, ones2)
                return cols[:N_PAIR], cols[N_PAIR:]

            n_next = [pltpu.roll(tiles[j][4], GROUP - 1, 0) for j in range(N_PAIR)]
            dots = _seg_sum_rows([tiles[j][5] * n_next[j] for j in range(N_PAIR)]
                                 + [tiles[j][2] * n_next[j] for j in range(N_PAIR)], ones2)
            b_n, k_n = dots[:N_PAIR], dots[N_PAIR:]
            w_n = [tiles[j][1] * n_next[j] for j in range(N_PAIR)]

            vcols = vcols_of(0)
            after = None
            for u in range(0, GROUP, 2):
                prods = _seg_sum([states[j] * row(j, "n", u) for j in range(N_PAIR)]
                                 + [states[j] * w_n[j][u:u + 1] for j in range(N_PAIR)], ones2)
                if after is not None:
                    emit_out(u - 2, after)
                nxt = vcols_of(u + 2) if u + 2 < GROUP else None
                first, second = [], []
                for j in range(N_PAIR):
                    sa1 = prods[j]
                    sa2 = prods[N_PAIR + j] + sa1 * b_n[j][u:u + 1] + vcols[0][j] * k_n[j][u:u + 1]
                    s1 = states[j] * row(j, "w", u) + sa1 * row(j, "b", u) + vcols[0][j] * row(j, "k", u)
                    s2 = s1 * row(j, "w", u + 1) + sa2 * row(j, "b", u + 1) + vcols[1][j] * row(j, "k", u + 1)
                    st_ref[row0 + u, j] = s1
                    sa_ref[row0 + u, j] = sa1
                    st_ref[row0 + u + 1, j] = s2
                    sa_ref[row0 + u + 1, j] = sa2
                    first.append(s1)
                    second.append(s2)
                    states[j] = s2
                after, vcols = (first, second), nxt
            emit_out(GROUP - 2, after)
            return tuple(states + ocols)

        zero = jnp.zeros((HEAD_DIM, LANES), F32)
        fin = lax.fori_loop(0, CHUNK // GROUP, group, tuple(s_scr[j] for j in range(N_PAIR)) + (zero,) * N_PAIR)
        for j in range(N_PAIR):
            s_scr[j] = fin[j]
            o_ref[:, _pair(j)] = _rows_of_columns(fin[N_PAIR + j])

    blk = pl.BlockSpec((CHUNK, D_RWKV), lambda c: (c, 0))
    per_step = pl.BlockSpec((CHUNK,) + STATE, lambda c: (c, 0, 0, 0))
    return pl.pallas_call(
        body, name="rwkv_scan_fwd", grid=(N_CHUNK,),
        in_specs=[blk] * 6,
        out_specs=[blk, per_step, per_step],
        out_shape=[jax.ShapeDtypeStruct((SEQ, D_RWKV), F32)] + [jax.ShapeDtypeStruct((SEQ,) + STATE, F32)] * 2,
        scratch_shapes=[pltpu.VMEM(STATE, F32)],
        compiler_params=_cp(("arbitrary",)),
    )(r, w, k, v, kkn, b)


def _scan_bwd(r, w, k, v, kkn, b, do, states, sas, ds_in, prev, name, first_chunk, n_chunks):
    top = first_chunk + n_chunks - 1

    def body(r_ref, w_ref, k_ref, v_ref, kkn_ref, b_ref, do_ref, st_ref, before_ref, sa_ref, ds_in_ref, *rest):
        dr_ref, dw_ref, dk_ref, dv_ref, dkkn_ref, db_ref, ds_out_ref, ds_scr = rest[-8:]
        i = pl.program_id(0)
        ones2, diag, lane_in_head = _scan_consts()

        @pl.when(i == 0)
        def _():
            ds_scr[...] = ds_in_ref[...]

        entry = [before_ref[0, j] * jnp.where(i < top, 1.0, 0.0) for j in range(N_PAIR)]

        def reverse(gr, carry):
            gi = CHUNK // GROUP - 1 - gr
            row0 = pl.multiple_of(gi * GROUP, GROUP)
            dstates, dvcols = list(carry[:N_PAIR]), list(carry[N_PAIR:])
            tiles = [[t[pl.ds(row0, GROUP), _pair(j)]
                      for t in (r_ref, w_ref, k_ref, v_ref, kkn_ref, b_ref, do_ref)] for j in range(N_PAIR)]
            rows = [[[None] * GROUP for _ in range(5)] for _ in range(N_PAIR)]

            def row(j, name, u):
                return tiles[j]["rwkvnbd".index(name)][u:u + 1]

            def cols_of(u):
                cols = _col_form([row(j, name, u - d) for d in range(2) for name in "dv" for j in range(N_PAIR)],
                                 diag, ones2)
                return [[(cols[(2 * d) * N_PAIR + j], cols[(2 * d + 1) * N_PAIR + j]) for j in range(N_PAIR)]
                        for d in range(2)]

            def emit_dv(u, dsps):
                outs = _seg_sum([dsp[j] * row(j, "k", u - d) for d, dsp in enumerate(dsps) for j in range(N_PAIR)], ones2)
                for d in range(2):
                    here = lane_in_head == gi * GROUP + u - d
                    for j in range(N_PAIR):
                        dvcols[j] = jnp.where(here, outs[d * N_PAIR + j], dvcols[j])

            b_prev = [pltpu.roll(tiles[j][5], 1, 0) for j in range(N_PAIR)]
            dots = _seg_sum_rows([tiles[j][4] * b_prev[j] for j in range(N_PAIR)]
                                 + [tiles[j][0] * tiles[j][5] for j in range(N_PAIR)], ones2)
            n_b, r_b = dots[:N_PAIR], dots[N_PAIR:]
            w_b = [tiles[j][1] * b_prev[j] for j in range(N_PAIR)]

            def outputs(u, j, dsp, dsa, docol, vcol):
                tl = gi * GROUP + u
                if u > 0:
                    s_prev = st_ref[tl - 1, j]
                else:
                    s_prev = jnp.where(gi == 0, entry[j], st_ref[jnp.maximum(tl - 1, 0), j])
                rows[j][0][u] = jnp.sum(st_ref[tl, j] * docol, axis=0, keepdims=True)
                rows[j][1][u] = jnp.sum(dsp * s_prev, axis=0, keepdims=True)
                rows[j][2][u] = jnp.sum(dsp * vcol, axis=0, keepdims=True)
                rows[j][3][u] = jnp.sum(s_prev * dsa, axis=0, keepdims=True)
                rows[j][4][u] = jnp.sum(dsp * sa_ref[tl, j], axis=0, keepdims=True)

            cols = cols_of(GROUP - 1)
            before = None
            for u in range(GROUP - 1, 0, -2):
                dsp1 = [dstates[j] + cols[0][j][0] * row(j, "r", u) for j in range(N_PAIR)]
                prods = _seg_sum([dsp1[j] * row(j, "b", u) for j in range(N_PAIR)]
                                 + [dsp1[j] * w_b[j][u:u + 1] for j in range(N_PAIR)], ones2)
                if before is not None:
                    emit_dv(u + 2, before)
                nxt = cols_of(u - 2) if u >= 2 else None
                dsp2 = []
                for j in range(N_PAIR):
                    dsa1 = prods[j]
                    dsa2 = prods[N_PAIR + j] + dsa1 * n_b[j][u:u + 1] + cols[1][j][0] * r_b[j][u - 1:u]
                    mid = dsp1[j] * row(j, "w", u) + dsa1 * row(j, "n", u) + cols[1][j][0] * row(j, "r", u - 1)
                    outputs(u, j, dsp1[j], dsa1, *cols[0][j])
                    outputs(u - 1, j, mid, dsa2, *cols[1][j])
                    dstates[j] = mid * row(j, "w", u - 1) + dsa2 * row(j, "n", u - 1)
                    dsp2.append(mid)
                before, cols = (dsp1, dsp2), nxt
            emit_dv(1, before)
            for j in range(N_PAIR):
                for ref, rr in zip((dr_ref, dw_ref, dk_ref, dkkn_ref, db_ref), rows[j]):
                    ref[pl.ds(row0, GROUP), _pair(j)] = jnp.concatenate(rr, axis=0)
            return tuple(dstates + dvcols)

        zero = jnp.zeros((HEAD_DIM, LANES), F32)
        dfin = lax.fori_loop(0, CHUNK // GROUP, reverse, tuple(ds_scr[j] for j in range(N_PAIR)) + (zero,) * N_PAIR)
        for j in range(N_PAIR):
            ds_scr[j] = dfin[j]
            dv_ref[:, _pair(j)] = _rows_of_columns(dfin[N_PAIR + j])

        @pl.when(i == n_chunks - 1)
        def _():
            ds_out_ref[...] = ds_scr[...]

    blk = pl.BlockSpec((CHUNK, D_RWKV), lambda i: (top - i, 0))
    per_step = pl.BlockSpec((CHUNK,) + STATE, lambda i: (top - i, 0, 0, 0))
    step_before = pl.BlockSpec((1,) + STATE, lambda i: (jnp.maximum((top - i) * CHUNK - 1, 0), 0, 0, 0))
    prev = [] if prev is None else list(prev)
    outs = pl.pallas_call(
        body, name=name, grid=(n_chunks,),
        in_specs=[blk] * 7 + [per_step, step_before, per_step, _const(STATE)] + [ANY] * len(prev),
        out_specs=[blk] * 6 + [_const(STATE)],
        out_shape=[jax.ShapeDtypeStruct((SEQ, D_RWKV), F32)] * 6 + [jax.ShapeDtypeStruct(STATE, F32)],
        scratch_shapes=[pltpu.VMEM(STATE, F32)],
        input_output_aliases={11 + t: t for t in range(len(prev))},
        compiler_params=_cp(("arbitrary",)),
    )(r, w, k, v, kkn, b, do, states, states, sas, ds_in, *prev)
    return outs[:6], outs[6]


def _stacked(rows, cols, pick):
    return pl.BlockSpec((None, rows, cols), pick)


def _local_step(x, target, sm, win_st):
    def tied(t, token):
        return t if token is None else t + token[0:1, 0:1].reshape((1,) * t.ndim)

    zpad = jnp.zeros((LORA_DECAY, D_RWKV), F32)
    prm = [sm["w0"], jnp.concatenate([sm["w_decay_up"], zpad], axis=0), sm["a0"],
           jnp.concatenate([zpad, sm["w_iclr_up"]], axis=0), sm["w_gate_up"], sm["k_k"], sm["k_a"]]
    mix = sm["rwkv_shift_mix"]
    onehot = jnp.asarray(_t5_onehot(), BF16)
    sinks = sm["sinks"].reshape(N_Q_HEADS)
    lng, lnb, rk = sm["ln_x_g"], sm["ln_x_b"], sm["r_k"].reshape(1, D_RWKV)

    h1 = _norm_cast(x, sm["norm_mix_pre"], "norm_in")
    proj = _matmul(h1, win_st, "nn", "proj", m=SEQ, n=D_IN, k=D_MODEL, tm=SEQ, tn=640,
                   b_spec=_stacked(D_MODEL, 640, lambda i, j: (j, 0, 0)))
    bias = _bias_table(sm["rel_bias"].T, onehot).reshape(N_KV_HEADS, Q_PER_KV * BLOCK, 2 * BLOCK)
    attn = _attn_fwd(proj, bias, sinks)
    r, w, k2, v, kkn, b, g = _rwkv_prep(proj, mix, prm)
    o, states, sas = _scan_fwd(r, w, k2, v, kkn, b)
    wout, wup_st, wdown = yield ("rest_weights", o)
    cat = _rwkv_post(o, r, k2, v, g, lng, lnb, rk, attn)
    mixo = _matmul(cat, wout, "nn", "out_proj", m=SEQ, n=D_MODEL, k=D_MODEL, tm=SEQ, tn=512)
    x2, h3 = _mix_norm(x, mixo, sm["norm_mix_post"], sm["norm_ffn_pre"])
    u_gate, u_val, gate, val, act = _ffn_up_act(h3, wup_st, sm["conv_w"], sm["conv_b"])
    f = _matmul(act, wdown, "nn", "ffn_down", m=SEQ, n=D_MODEL, k=D_FF, tm=1024, tn=512)
    loss, dy, df, d_g4 = _loss_head(x2, f, sm["norm_ffn_post"], target)

    d_wdown = _matmul(act, df, "tn", "d_wdown", m=D_FF, n=D_MODEL, k=SEQ, tm=1024, tn=D_MODEL)
    du, d_convw, d_convb = _ffn_act_bwd(u_gate, u_val, gate, val, df, wdown, sm["conv_w"])
    d_convw = d_convw.transpose(1, 0, 2).reshape(3, 2 * D_FF)
    d_convb = d_convb.reshape(1, 2 * D_FF)
    dh3 = _matmul_nt_shards(du, wup_st, "d_h3", m=SEQ, n=D_MODEL, tm=512, tn=512,
                            a_spec=pl.BlockSpec((2, 512, D_FF), lambda i, j: (0, i, 0)),
                            a_piece=lambda ref, s: ref[s // 2, :, (s % 2) * 2048:(s % 2 + 1) * 2048])
    d_wup = _matmul(h3, du, "tn", "d_wup", m=D_MODEL, n=2 * D_FF, k=SEQ, tm=D_MODEL, tn=1024,
                    b_spec=pl.BlockSpec((None, SEQ, 1024), lambda i, j: (j // 4, 0, j % 4)),
                    out=((N_CHIPS, D_MODEL, 2048), _stacked(D_MODEL, 1024, lambda i, j: (j // 2, 0, j % 2))))
    dx2, dmix, d_g2, d_g3 = _mid_bwd(x2, mixo, dy, dh3, sm["norm_mix_post"], sm["norm_ffn_pre"])
    dcat = _matmul(dmix, wout, "nt", "d_cat", m=SEQ, n=D_MODEL, k=D_MODEL, tm=SEQ, tn=512)
    d_wout = _matmul(cat, dmix, "tn", "d_wout", m=D_MODEL, n=D_MODEL, k=SEQ, tm=512, tn=D_MODEL)
    token = yield ("grads_a", (d_wdown, d_wup, d_wout))
    do, dr_p, dk_p, dv_p, dg, d_lng, d_lnb, d_rk = _rwkv_post_bwd(o, r, k2, v, g, lng, tied(lnb, token), rk, dcat)
    half = N_CHUNK // 2
    ds_end = jnp.zeros(STATE, F32)
    late, ds_mid = _scan_bwd(r, w, k2, v, kkn, b, do, states, sas, ds_end, None, "rwkv_scan_bwd_late", half, half)
    token = yield ("seam_1", ds_mid)
    scan_cts, ds_first = _scan_bwd(r, w, k2, v, kkn, b, do, states, sas, tied(ds_mid, token), late,
                                   "rwkv_scan_bwd_early", 0, half)
    dr_s, dw_s, dk_s, dv_s, dkkn_s, db_s = scan_cts
    token = yield ("seam_2", ds_first)
    prep_grads = _rwkv_prep_bwd(proj, tied(mix, token), prm,
                                (dr_s, dr_p, dw_s, dk_s, dk_p, dv_s, dv_p, dkkn_s, db_s, dg))
    dps, d_mix, d_w0, d_wdu, d_a0, d_wiu, d_wgu, d_kk, d_ka = prep_grads
    dq, dkv, dbias, dsink = _attn_bwd(proj, bias, sinks, dcat)
    d_relb = _bias_table_bwd(dbias.reshape(N_Q_HEADS, N_REL), onehot).T
    dproj = _assemble_dproj(dq, dkv, dps, mix)
    d_win = _matmul(h1, dproj, "tn", "d_win", m=D_MODEL, n=D_IN, k=SEQ, tm=D_MODEL, tn=640,
                    out=((N_CHIPS, D_MODEL, 640), _stacked(D_MODEL, 640, lambda i, j: (j, 0, 0))))
    token = yield ("grads_b", d_win)
    dh1 = _matmul_nt_shards(dproj, win_st, "d_h1", m=SEQ, n=D_MODEL, tm=1024, tn=D_MODEL,
                            a_spec=pl.BlockSpec((1024, D_IN), lambda i, j: (i, 0)),
                            a_piece=lambda ref, s: ref[:, s * 640:(s + 1) * 640])
    grad_x, d_g1 = _first_bwd(x, dx2, dh1, tied(sm["norm_mix_pre"], token))

    grads = {
        "norm_mix_pre": d_g1, "norm_mix_post": d_g2, "norm_ffn_pre": d_g3, "norm_ffn_post": d_g4,
        "w_in": d_win, "rel_bias": d_relb, "sinks": dsink[:, 0].reshape(1, N_Q_HEADS),
        "rwkv_shift_mix": d_mix, "w0": d_w0, "w_decay_up": d_wdu[:LORA_DECAY], "a0": d_a0,
        "w_iclr_up": d_wiu[LORA_DECAY:], "w_gate_up": d_wgu, "k_k": d_kk, "k_a": d_ka,
        "r_k": d_rk.reshape(1, N_Q_HEADS, HEAD_DIM), "ln_x_g": d_lng, "ln_x_b": d_lnb,
        "w_out": d_wout, "w_ffn_up": d_wup, "conv_w": d_convw, "conv_b": d_convb, "w_ffn_down": d_wdown,
    }
    return loss, grad_x, grads


def _place():
    x, y, c = lax.axis_index("x"), lax.axis_index("y"), lax.axis_index("c")
    chips = [(1 - x, y), (x, 1 - y), (1 - x, 1 - y)]
    return x, y, c, chips


def _remote(src, dst, sems, idx, to):
    return pltpu.make_async_remote_copy(src_ref=src, dst_ref=dst, send_sem=sems[0].at[idx], recv_sem=sems[1].at[idx],
                                        device_id=to, device_id_type=MESH)


ROW_ALIGN = 16


def _half(c, rows):
    return pl.ds(pl.multiple_of(c * (rows // 2), ROW_ALIGN), rows // 2)


def _gather_weights(big, small):
    nb, ns = len(big), len(small)

    def body(*refs):
        ins, outs = refs[:nb + ns], refs[nb + ns:2 * (nb + ns)]
        ici, d2d, sml, loc = refs[2 * (nb + ns):2 * (nb + ns) + 2], refs[-5:-3], refs[-3:-1], refs[-1]
        x, y, c, chips = _place()
        me = 2 * x + y
        sib = (x, y, 1 - c)
        local = [pltpu.make_async_copy(ins[a], outs[a].at[me], loc.at[a]) for a in range(nb + ns)]
        for cp in local:
            cp.start()
        sends = []
        for a in range(nb):
            rows = _half(c, big[a].shape[0])
            for kk, chip in enumerate(chips):
                sends.append(_remote(ins[a].at[rows], outs[a].at[me, rows], ici, a * 3 + kk, (*chip, c)))
        for a in range(ns):
            for kk, chip in enumerate(chips):
                sends.append(_remote(ins[nb + a], outs[nb + a].at[me], sml, a * 3 + kk, (*chip, c)))
        for cp in sends:
            cp.start()
        passed = []
        for a in range(nb):
            rows = _half(c, big[a].shape[0])
            for kk, (px, py) in enumerate(chips):
                got = outs[a].at[2 * px + py, rows]
                _remote(got, got, ici, a * 3 + kk, sib).wait_recv()
                fwd = _remote(got, got, d2d, a * 3 + kk, sib)
                fwd.start()
                passed.append(fwd)
        for a in range(nb):
            other = _half(1 - c, big[a].shape[0])
            for kk, (px, py) in enumerate(chips):
                land = outs[a].at[2 * px + py, other]
                _remote(land, land, d2d, a * 3 + kk, sib).wait_recv()
        for a in range(ns):
            for kk, (px, py) in enumerate(chips):
                land = outs[nb + a].at[2 * px + py]
                _remote(land, land, sml, a * 3 + kk, sib).wait_recv()
        for cp in sends + passed:
            cp.wait_send()
        for cp in local:
            cp.wait()

    arrs = list(big) + list(small)
    in_vmem = pl.BlockSpec(memory_space=pltpu.VMEM)
    return pl.pallas_call(
        body, name="gather_weights",
        in_specs=[in_vmem] * len(arrs), out_specs=[in_vmem] * len(arrs),
        out_shape=[jax.ShapeDtypeStruct((N_CHIPS,) + t.shape, t.dtype) for t in arrs],
        scratch_shapes=[pltpu.SemaphoreType.DMA((3 * nb,)), pltpu.SemaphoreType.DMA((3 * nb,)),
                        pltpu.SemaphoreType.DMA((3 * nb,)), pltpu.SemaphoreType.DMA((3 * nb,)),
                        pltpu.SemaphoreType.DMA((3 * ns,)), pltpu.SemaphoreType.DMA((3 * ns,)),
                        pltpu.SemaphoreType.DMA((nb + ns,))],
        compiler_params=pltpu.CompilerParams(has_side_effects=True, vmem_limit_bytes=VMEM_LIMIT),
    )(*arrs)


HBM = pl.BlockSpec(memory_space=pltpu.HBM)
SEM = pl.BlockSpec(memory_space=pltpu.SEMAPHORE)
EFFECT = pltpu.SideEffectType.DATAFLOW_SIDE_EFFECTING


def _copies_start(name, bufs, plan, n, partners=None):
    nb = len(bufs)

    def body(*refs):
        ins, sems, token = refs[:nb], refs[nb:nb + 2 * n], refs[-1]
        if partners is not None:
            barrier = pltpu.get_barrier_semaphore()
            peers = partners[1]()
            for peer in peers:
                pl.semaphore_signal(barrier, inc=1, device_id=peer, device_id_type=MESH)
            pl.semaphore_wait(barrier, len(peers))
        for kk, (src, dst, dev) in enumerate(plan(ins)):
            pltpu.make_async_remote_copy(src_ref=src, dst_ref=dst, send_sem=sems[2 * kk], recv_sem=sems[2 * kk + 1],
                                         device_id=dev, device_id_type=MESH).start()
        token[...] = jnp.zeros_like(token)

    outs = pl.pallas_call(
        body, name=name,
        out_shape=tuple([pltpu.SemaphoreType.DMA(())] * (2 * n) + [pltpu.HBM(t.shape, t.dtype) for t in bufs]
                        + [jax.ShapeDtypeStruct((8, LANES), F32)]),
        in_specs=[HBM] * nb,
        out_specs=tuple([SEM] * (2 * n) + [HBM] * nb + [pl.BlockSpec(memory_space=pltpu.VMEM)]),
        input_output_aliases={t: 2 * n + t for t in range(nb)},
        compiler_params=pltpu.CompilerParams(has_side_effects=EFFECT,
                                             collective_id=None if partners is None else partners[0]),
    )(*[pltpu.with_memory_space_constraint(t, pltpu.HBM) for t in bufs])
    return outs[:2 * n], outs[2 * n:2 * n + nb], outs[-1]


def _copies_wait(name, sems, bufs, plan, n, after):
    nb = len(bufs)
    after = list(after) if isinstance(after, (list, tuple)) else [after]

    def body(*refs):
        ins, sem_refs = refs[:nb], refs[nb:nb + 2 * n]
        for kk, (src, dst, dev) in enumerate(plan(ins)):
            cp = pltpu.make_async_remote_copy(src_ref=src, dst_ref=dst, send_sem=sem_refs[2 * kk],
                                              recv_sem=sem_refs[2 * kk + 1], device_id=dev, device_id_type=MESH)
            cp.wait_send()
            cp.wait_recv()

    return pl.pallas_call(
        body, name=name,
        out_shape=tuple(pltpu.HBM(t.shape, t.dtype) for t in bufs),
        in_specs=[HBM] * nb + [SEM] * (2 * n) + [ANY] * len(after),
        out_specs=tuple([HBM] * nb),
        input_output_aliases={t: t for t in range(nb)},
        compiler_params=pltpu.CompilerParams(has_side_effects=EFFECT),
    )(*bufs, *sems, *after)


def _plan_gather(n_w):
    def plan(refs):
        x, y, c, chips = _place()
        me = 2 * x + y
        return [(refs[a], refs[n_w + a].at[me], (*chip, c)) for a in range(n_w) for chip in chips + [(x, y)]]
    return plan


def _plan_pair_halves(n_g, rows):
    def plan(refs):
        x, y, c, _ = _place()
        return [(refs[a].at[:, _half(1 - c, rows[a])], refs[n_g + a], (x, y, 1 - c)) for a in range(n_g)]
    return plan


def _plan_chip_parts(n_g):
    def plan(refs):
        x, y, c, chips = _place()
        me = 2 * x + y
        return [(refs[a].at[2 * px + py], refs[n_g + a].at[me], (px, py, c))
                for a in range(n_g) for (px, py) in chips]
    return plan


def _plan_pair_fill(n_g, rows):
    def plan(refs):
        x, y, c, _ = _place()
        return [(refs[a].at[_half(c, rows[a])], refs[a].at[_half(c, rows[a])], (x, y, 1 - c)) for a in range(n_g)]
    return plan


def _pair_add(g, got, name):
    _, rows, cols = g.shape
    hr = rows // 2
    tr = min(hr, 512)
    nb = hr // tr

    def body(g_ref, got_ref, p_ref, own_ref):
        val = (g_ref[...] + got_ref[...]).astype(BF16)
        p_ref[...] = val

        @pl.when(pl.program_id(1) == 2 * lax.axis_index("x") + lax.axis_index("y"))
        def _():
            own_ref[...] = val

    def mine(i, s):
        return (2 * lax.axis_index("x") + lax.axis_index("y"), i, 0)

    return pl.pallas_call(
        body, name=name, grid=(nb, N_CHIPS),
        in_specs=[pl.BlockSpec((None, tr, cols), lambda i, s: (s, lax.axis_index("c") * nb + i, 0)),
                  pl.BlockSpec((None, tr, cols), lambda i, s: (s, i, 0))],
        out_specs=[pl.BlockSpec((None, tr, cols), lambda i, s: (s, i, 0)), pl.BlockSpec((None, tr, cols), mine)],
        out_shape=[jax.ShapeDtypeStruct((N_CHIPS, hr, cols), BF16)] * 2,
        compiler_params=_cp(("parallel", "arbitrary")),
    )(g, got)


def _chip_sum(parts, name):
    _, hr, cols = parts.shape
    tr = min(hr, 256)
    nb = hr // tr

    def body(t_ref, o_ref):
        part = [t_ref[s].astype(F32) for s in range(N_CHIPS)]
        o_ref[...] = ((part[0] + part[1]) + part[2]) + part[3]

    return pl.pallas_call(
        body, name=name, grid=(nb,),
        in_specs=[pl.BlockSpec((N_CHIPS, tr, cols), lambda i: (0, i, 0))],
        out_specs=pl.BlockSpec((tr, cols), lambda i: (lax.axis_index("c") * nb + i, 0)),
        out_shape=jax.ShapeDtypeStruct((2 * hr, cols), F32),
        compiler_params=_cp(("parallel",)),
    )(parts)


class _Reduction:
    def __init__(self, tag, rows, first_id):
        self.tag, self.n, self.rows, self.first_id = tag, len(rows), rows, first_id
        self.plans = (_plan_pair_halves(self.n, rows), _plan_chip_parts(self.n), _plan_pair_fill(self.n, rows))
        self.flight = None

    def _name(self, what):
        return f"grad_{self.tag}_{what}"

    @staticmethod
    def _sibling():
        x, y, c, _ = _place()
        return [(x, y, 1 - c)]

    @staticmethod
    def _same_core_elsewhere():
        x, y, c, chips = _place()
        return [(*chip, c) for chip in chips]

    def start(self, gs):
        gots = [lax.empty((N_CHIPS, t.shape[1] // 2, t.shape[2]), F32) for t in gs]
        self.flight = _copies_start(self._name("pair_start"), list(gs) + gots, self.plans[0], self.n,
                                    (self.first_id, self._sibling))
        return self.flight[2]

    def after_pair(self, after):
        sems, bufs, _ = self.flight
        out = _copies_wait(self._name("pair_wait"), sems, bufs, self.plans[0], self.n, after)
        sums = [_pair_add(g, got, self._name(f"pair_add_{i}"))
                for i, (g, got) in enumerate(zip(out[:self.n], out[self.n:]))]
        self.flight = _copies_start(self._name("chip_start"), [p for p, _ in sums] + [own for _, own in sums],
                                    self.plans[1], 3 * self.n, (self.first_id + 1, self._same_core_elsewhere))
        return self.flight[2]

    def after_chips(self, after):
        sems, bufs, _ = self.flight
        out = _copies_wait(self._name("chip_wait"), sems, bufs, self.plans[1], 3 * self.n, after)
        fulls = [_chip_sum(t, self._name(f"chip_sum_{i}")) for i, t in enumerate(out[self.n:])]
        self.flight = _copies_start(self._name("fill_start"), fulls, self.plans[2], self.n,
                                    (self.first_id + 2, self._sibling))
        return self.flight[2]

    def finish(self, after):
        sems, bufs, _ = self.flight
        return _copies_wait(self._name("fill_wait"), sems, bufs, self.plans[2], self.n, after)


def _adamw_math(w, g, m, v):
    nm = ADAM_B1 * m + (1.0 - ADAM_B1) * g
    nv = ADAM_B2 * v + (1.0 - ADAM_B2) * (g * g)
    m_hat = nm / (1.0 - ADAM_B1 ** ADAM_STEP)
    v_hat = nv / (1.0 - ADAM_B2 ** ADAM_STEP)
    return -ADAM_LR * (m_hat / (jnp.sqrt(v_hat) + ADAM_EPS) + ADAM_WD * w), nm, nv


def _adamw(w, g, m, v, name, tr):
    r, cdim = w.shape

    def body(w_ref, g_ref, m_ref, v_ref, d_ref, nm_ref, nv_ref):
        d_ref[...], nm_ref[...], nv_ref[...] = _adamw_math(w_ref[...], g_ref[...], m_ref[...], v_ref[...])

    return pl.pallas_call(
        body, name=name, grid=(r // tr,), in_specs=[_rows(tr, cdim)] * 4, out_specs=[_rows(tr, cdim)] * 3,
        out_shape=[jax.ShapeDtypeStruct((r, cdim), F32)] * 3, compiler_params=_cp(("parallel",)),
    )(w, g, m, v)


def _adamw_small(w, parts, m, v, shapes):
    n_rows = w.shape[0]

    def scatter(src, outs):
        row = 0
        for (rows, cols), out in zip(shapes, outs):
            if cols == LANES:
                out[...] = src[row:row + rows, :]
            elif cols > LANES:
                per = cols // LANES
                for r in range(rows):
                    for cb in range(per):
                        out[r:r + 1, cb * LANES:(cb + 1) * LANES] = src[row + r * per + cb:row + r * per + cb + 1, :]
            else:
                per = LANES // cols
                for r in range(rows):
                    out[r:r + 1, :] = src[row + r // per:row + r // per + 1, (r % per) * cols:(r % per + 1) * cols]
            row += -(-rows * cols // LANES)

    def body(w_ref, p_ref, m_ref, v_ref, *rest):
        outs, scr = rest[:-4], rest[-4:]
        g = p_ref[0]
        for dev in range(1, N_DEV):
            g = g + p_ref[dev]
        scr[3][...] = g
        scr[0][...], scr[1][...], scr[2][...] = _adamw_math(w_ref[...], g, m_ref[...], v_ref[...])
        n = len(shapes)
        for kind in range(4):
            scatter(scr[kind], outs[kind * n:(kind + 1) * n])

    outs = pl.pallas_call(
        body, name="adamw_small", grid=(1,),
        in_specs=[_const(w.shape), _const(parts.shape), _const(w.shape), _const(w.shape)],
        out_specs=[_const(s) for s in shapes] * 4, out_shape=[jax.ShapeDtypeStruct(s, F32) for s in shapes] * 4,
        scratch_shapes=[pltpu.VMEM((n_rows, LANES), F32)] * 4,
        compiler_params=_cp(("arbitrary",)),
    )(w, parts, m, v)
    n = len(shapes)
    return [outs[kind * n:(kind + 1) * n] for kind in range(4)]


REPLICATED = (("norm_mix_pre", 1024), ("norm_mix_post", 1024), ("norm_ffn_pre", 1024), ("norm_ffn_post", 1024),
              ("rel_bias", 256), ("sinks", 8), ("rwkv_shift_mix", 1792), ("w0", 512), ("a0", 512), ("k_k", 512),
              ("k_a", 512), ("r_k", 512), ("ln_x_g", 512), ("ln_x_b", 512), ("conv_b", 8192))
SMALL_SHARDED = (("w_decay_up", LORA_DECAY, D_RWKV), ("w_iclr_up", LORA_ICLR, D_RWKV),
                 ("w_gate_up", LORA_GATE, D_RWKV), ("conv_w", 3, 2 * D_FF))
BIG = (("w_in", D_MODEL, 640), ("w_out", 256, D_MODEL), ("w_ffn_up", D_MODEL, 2048), ("w_ffn_down", 1024, D_MODEL))
PACK_ALIGN = 8 * LANES


def _pack(pieces):
    flat = []
    for t in pieces:
        t = t.reshape(-1)
        pad = (-t.shape[0]) % LANES
        flat.append(jnp.pad(t, (0, pad)) if pad else t)
    flat = jnp.concatenate(flat)
    pad = (-flat.shape[0]) % PACK_ALIGN
    return jnp.pad(flat, (0, pad)).reshape(-1, LANES)


def kernel(x, norm_mix_pre, norm_mix_post, norm_ffn_pre, norm_ffn_post, w_in, rel_bias, sinks, rwkv_shift_mix, w0, w_decay_up, a0, w_iclr_up, w_gate_up, k_k, k_a, r_k, ln_x_g, ln_x_b, w_out, w_ffn_up, conv_w, conv_b, w_ffn_down, loss_target, m_norm_mix_pre, m_norm_mix_post, m_norm_ffn_pre, m_norm_ffn_post, m_w_in, m_rel_bias, m_sinks, m_rwkv_shift_mix, m_w0, m_w_decay_up, m_a0, m_w_iclr_up, m_w_gate_up, m_k_k, m_k_a, m_r_k, m_ln_x_g, m_ln_x_b, m_w_out, m_w_ffn_up, m_conv_w, m_conv_b, m_w_ffn_down, v_norm_mix_pre, v_norm_mix_post, v_norm_ffn_pre, v_norm_ffn_post, v_w_in, v_rel_bias, v_sinks, v_rwkv_shift_mix, v_w0, v_w_decay_up, v_a0, v_w_iclr_up, v_w_gate_up, v_k_k, v_k_a, v_r_k, v_ln_x_g, v_ln_x_b, v_w_out, v_w_ffn_up, v_conv_w, v_conv_b, v_w_ffn_down):
    given = dict(locals())
    names = [n for n, _ in REPLICATED] + [n for n, _, _ in SMALL_SHARDED] + [n for n, _, _ in BIG]
    order = ["norm_mix_pre", "norm_mix_post", "norm_ffn_pre", "norm_ffn_post", "w_in", "rel_bias", "sinks",
             "rwkv_shift_mix", "w0", "w_decay_up", "a0", "w_iclr_up", "w_gate_up", "k_k", "k_a", "r_k", "ln_x_g",
             "ln_x_b", "w_out", "w_ffn_up", "conv_w", "conv_b", "w_ffn_down"]
    assert sorted(names) == sorted(order)

    big_sh = {n: given[n].reshape(a, b).astype(BF16) for n, a, b in BIG}
    small_sh = [given[n].reshape(r, c // N_CHIPS) for n, r, c in SMALL_SHARDED]
    gathered = _gather_weights([big_sh["w_in"]], small_sh)
    rest = ("w_out", "w_ffn_up", "w_ffn_down")
    win_st, rest_sh = lax.optimization_barrier((gathered[0], [big_sh[n] for n in rest]))
    sm = {n: given[n] for n, _ in REPLICATED}
    sm["r_k"] = r_k.reshape(N_Q_HEADS, HEAD_DIM)
    for (n, r, c), st in zip(SMALL_SHARDED, gathered[1:]):
        sm[n] = st.transpose(1, 0, 2).reshape(r, c)

    lands = [lax.empty((N_CHIPS,) + t.shape, BF16) for t in rest_sh]
    plan_w = _plan_gather(len(rest))
    n_w = N_CHIPS * len(rest)
    w_sems, w_bufs, token = _copies_start("gather_rest_start", rest_sh + lands, plan_w, n_w)
    sm["norm_mix_pre"] = norm_mix_pre + token[0:1, 0:1]

    def on_rest_weights(after):
        out = _copies_wait("gather_rest_wait", w_sems, w_bufs, plan_w, n_w, after)
        wout_st, wup_st, wdown_st = out[3:]
        return wout_st.reshape(D_MODEL, D_MODEL), wup_st, wdown_st.reshape(D_FF, D_MODEL)

    red_a = _Reduction("a", (1024, D_MODEL, 256), first_id=0)
    red_b = _Reduction("b", (D_MODEL,), first_id=3)

    def on_grads_a(gs):
        d_wdown, d_wup, d_wout = gs
        return red_a.start([d_wdown.reshape(N_CHIPS, 1024, D_MODEL), d_wup, d_wout.reshape(N_CHIPS, 256, D_MODEL)])

    handlers = {"rest_weights": on_rest_weights, "grads_a": on_grads_a, "seam_1": red_a.after_pair,
                "seam_2": red_a.after_chips, "grads_b": lambda g: red_b.start([g])}
    steps = _local_step(x[0], loss_target[0], sm, win_st)
    kind, payload = next(steps)
    while True:
        try:
            kind, payload = steps.send(handlers[kind](payload))
        except StopIteration as done:
            loss, grad_x, grads = done.value
            break

    small_names = [n for n, _ in REPLICATED] + [n for n, _, _ in SMALL_SHARDED]

    def shard_cols(t, s):
        return t[:, s * (t.shape[1] // N_CHIPS):(s + 1) * (t.shape[1] // N_CHIPS)]

    for_chip = jnp.stack([_pack([loss[0]] + [grads[n] for n, _ in REPLICATED]
                                + [shard_cols(grads[n], s) for n, _, _ in SMALL_SHARDED]) for s in range(N_CHIPS)])
    land = lax.empty((N_DEV,) + for_chip.shape[1:], F32)

    def plan_small(refs):
        x, y, c, _ = _place()
        out = []
        for rel in range(N_DEV):
            px, py, pc = x ^ (rel >> 2), y ^ ((rel >> 1) & 1), c ^ (rel & 1)
            out.append((refs[0].at[2 * px + py], refs[1].at[4 * x + 2 * y + c], (px, py, pc)))
        return out

    s_sems, s_bufs, s_token = _copies_start("grad_small_start", [for_chip, land], plan_small, N_DEV)

    red_b.after_pair([grad_x, s_token])
    g_out = {}
    g_out["w_ffn_down"], g_out["w_ffn_up"], g_out["w_out"] = red_a.finish(grad_x)

    delta, new_m, new_v = {}, {}, {}

    def update(n, a, b):
        delta[n], new_m[n], new_v[n] = _adamw(given[n].reshape(a, b), g_out[n], given["m_" + n].reshape(a, b),
                                              given["v_" + n].reshape(a, b), "adamw_" + n, 256)

    for n, a, b in BIG[1:]:
        update(n, a, b)
    done = [delta[n] for n, _, _ in BIG[1:]]
    red_b.after_chips(done)
    parts = _copies_wait("grad_small_wait", s_sems, s_bufs, plan_small, N_DEV, done)[1]
    no_param = jnp.zeros((LANES,), F32)
    packs = [_pack([no_param] + [given[pre + n] for n in small_names]) for pre in ("", "m_", "v_")]

    def piece_shape(n):
        shape = given[n].shape
        rows, cols = int(np.prod(shape[:-1])), shape[-1]
        whole = cols % LANES == 0 or (LANES % cols == 0 and (rows * cols) % LANES == 0 and cols >= HEAD_DIM)
        return (rows, cols) if whole else (-(-rows * cols // LANES), LANES)

    shapes = [(1, LANES)] + [piece_shape(n) for n in small_names]
    upd = _adamw_small(packs[0], parts, packs[1], packs[2], shapes)
    loss = upd[3][0][0, 0]
    for i, n in enumerate(small_names):
        shape = given[n].shape
        size = int(np.prod(shape))
        delta[n], new_m[n], new_v[n], g_out[n] = (u[1 + i].reshape(-1)[:size].reshape(shape) for u in upd)
    g_out["w_in"], = red_b.finish(upd[0][0])
    update(*BIG[0])

    def shaped(d):
        return [d[n].reshape(given[n].shape) for n in order]

    return (loss, grad_x.reshape(x.shape), *shaped(g_out), *shaped(delta), *shaped(new_m), *shaped(new_v))
```

```python
import math

import numpy as np
import jax
import jax.numpy as jnp
from jax import lax
from jax.experimental import pallas as pl
from jax.experimental.pallas import tpu as pltpu

F32 = jnp.float32
BF16 = jnp.bfloat16
MESH = pl.DeviceIdType.MESH

SEQ = 2048
D_MODEL = 1024
HEAD_DIM = 64
D_ATTN = 512
D_RWKV = 512
D_KV = 128
N_Q_HEADS = 8
N_KV_HEADS = 2
Q_PER_KV = 4
BLOCK = 128
N_BUCKETS = 32
MAX_DISTANCE = 128
LORA_DECAY = 64
LORA_ICLR = 64
LORA_GATE = 128
RWKV_COLS = 3 * D_RWKV + LORA_DECAY + LORA_ICLR + LORA_GATE
P_OFF = D_ATTN + 2 * D_KV
D_IN = P_OFF + RWKV_COLS
D_FF = 4096
NORM_EPS = 1e-6
GN_EPS = 64e-5
NEG_INF = -1e30
N_CHIPS = 4
N_DEV = 8
HEAD_SHIFT = HEAD_DIM.bit_length() - 1
BLOCK_SHIFT = BLOCK.bit_length() - 1

ADAM_LR = 0.001
ADAM_B1 = 0.9
ADAM_B2 = 0.999
ADAM_EPS = 1e-08
ADAM_WD = 0.01
ADAM_STEP = 10

VMEM_LIMIT = 52 * 1024 * 1024
LANES = 128


def _cp(sem=None, vmem=VMEM_LIMIT):
    kw = dict(vmem_limit_bytes=vmem)
    if sem is not None:
        kw["dimension_semantics"] = sem
    return pltpu.CompilerParams(**kw)


def _rows(tr, nc):
    return pl.BlockSpec((tr, nc), lambda i: (i, 0))


def _const(shape):
    return pl.BlockSpec(shape, lambda *_: (0,) * len(shape))


ANY = pl.BlockSpec(memory_space=pl.ANY)


def _split(x, n):
    parts = []
    for _ in range(n - 1):
        h = x.astype(BF16)
        parts.append(h)
        x = x - h.astype(F32)
    parts.append(x.astype(BF16))
    return parts


NN = (((1,), (0,)), ((), ()))
NT = (((1,), (1,)), ((), ()))
TN = (((0,), (0,)), ((), ()))


def _dot(a, b, dn=NN):
    return lax.dot_general(a, b, dn, preferred_element_type=F32)


def _dot_ind(x, ind_bf16, n=3):
    acc = None
    for part in _split(x, n):
        t = _dot(part, ind_bf16)
        acc = t if acc is None else acc + t
    return acc


def _head_ones(n):
    r = lax.broadcasted_iota(jnp.int32, (n, n), 0) >> HEAD_SHIFT
    c = lax.broadcasted_iota(jnp.int32, (n, n), 1) >> HEAD_SHIFT
    return jnp.where(r == c, 1.0, 0.0).astype(BF16)


def _matmul(a, b, mode, name, *, m, n, k, tm, tn, a_spec=None, b_spec=None, out=None):
    dn = {"nn": NN, "nt": NT, "tn": TN}[mode]

    def body(a_ref, b_ref, o_ref):
        o_ref[...] = _dot(a_ref[...], b_ref[...], dn)

    if a_spec is None:
        a_spec = pl.BlockSpec((k, tm), lambda i, j: (0, i)) if mode == "tn" else pl.BlockSpec((tm, k), lambda i, j: (i, 0))
    if b_spec is None:
        b_spec = pl.BlockSpec((tn, k), lambda i, j: (j, 0)) if mode == "nt" else pl.BlockSpec((k, tn), lambda i, j: (0, j))
    return pl.pallas_call(
        body, name=name, grid=(m // tm, n // tn),
        in_specs=[a_spec, b_spec],
        out_specs=pl.BlockSpec((tm, tn), lambda i, j: (i, j)) if out is None else out[1],
        out_shape=jax.ShapeDtypeStruct((m, n) if out is None else out[0], F32),
        compiler_params=_cp(("parallel", "parallel")),
    )(a, b)


def _matmul_nt_shards(a, b_st, name, *, m, n, tm, tn, a_spec, a_piece):
    ks = b_st.shape[2]

    def body(a_ref, b_ref, o_ref):
        acc = _dot(a_piece(a_ref, 0), b_ref[0], NT)
        for s in range(1, N_CHIPS):
            acc = acc + _dot(a_piece(a_ref, s), b_ref[s], NT)
        o_ref[...] = acc

    return pl.pallas_call(
        body, name=name, grid=(m // tm, n // tn),
        in_specs=[a_spec, pl.BlockSpec((N_CHIPS, tn, ks), lambda i, j: (0, j, 0))],
        out_specs=pl.BlockSpec((tm, tn), lambda i, j: (i, j)),
        out_shape=jax.ShapeDtypeStruct((m, n), F32),
        compiler_params=_cp(("parallel", "parallel")),
    )(a, b_st)


def _rstd(x):
    return lax.rsqrt(jnp.mean(x * x, axis=-1, keepdims=True) + NORM_EPS)


def _rms_bwd(x, r, g, dy):
    gy = dy * g
    return r * gy - x * ((r * r * r) * (jnp.sum(x * gy, axis=-1, keepdims=True) / x.shape[-1]))


TR = 256
TRN = 512


def _norm_cast(x, g, name):
    def body(x_ref, g_ref, h_ref):
        x = x_ref[...]
        h_ref[...] = (x * _rstd(x) * g_ref[...]).astype(BF16)

    return pl.pallas_call(
        body, name=name, grid=(SEQ // TRN,),
        in_specs=[_rows(TRN, D_MODEL), _const((1, D_MODEL))],
        out_specs=_rows(TRN, D_MODEL),
        out_shape=jax.ShapeDtypeStruct((SEQ, D_MODEL), BF16),
        compiler_params=_cp(("parallel",)),
    )(x, g)


def _mix_norm(x, mix, g2, g3):
    def body(x_ref, mix_ref, g2_ref, g3_ref, x2_ref, h3_ref):
        mixv = mix_ref[...]
        x2 = x_ref[...] + mixv * _rstd(mixv) * g2_ref[...]
        x2_ref[...] = x2
        h3_ref[...] = (x2 * _rstd(x2) * g3_ref[...]).astype(BF16)

    return pl.pallas_call(
        body, name="mix_norm", grid=(SEQ // TRN,),
        in_specs=[_rows(TRN, D_MODEL), _rows(TRN, D_MODEL), _const((1, D_MODEL)), _const((1, D_MODEL))],
        out_specs=[_rows(TRN, D_MODEL), _rows(TRN, D_MODEL)],
        out_shape=[jax.ShapeDtypeStruct((SEQ, D_MODEL), F32), jax.ShapeDtypeStruct((SEQ, D_MODEL), BF16)],
        compiler_params=_cp(("parallel",)),
    )(x, mix, g2, g3)


def _loss_head(x2, f, g4, target):
    def body(x2_ref, f_ref, g4_ref, t_ref, loss_ref, dy_ref, df_ref, dg_ref):
        i = pl.program_id(0)
        f = f_ref[...]
        g4 = g4_ref[...]
        r = _rstd(f)
        e = x2_ref[...] + f * r * g4 - t_ref[...]
        dy = e * (1.0 / D_MODEL)
        dy_ref[...] = dy
        df_ref[...] = _rms_bwd(f, r, g4, dy).astype(BF16)
        part = 0.5 * jnp.sum(jnp.sum(e * e, axis=-1, keepdims=True), axis=0, keepdims=True) * (1.0 / D_MODEL)
        dg = jnp.sum(dy * f * r, axis=0, keepdims=True)

        @pl.when(i == 0)
        def _():
            loss_ref[...] = jnp.zeros_like(loss_ref)
            dg_ref[...] = jnp.zeros_like(dg_ref)

        loss_ref[...] += jnp.broadcast_to(part, loss_ref.shape)
        dg_ref[...] += dg

    return pl.pallas_call(
        body, name="loss_head", grid=(SEQ // TRN,),
        in_specs=[_rows(TRN, D_MODEL), _rows(TRN, D_MODEL), _const((1, D_MODEL)), _rows(TRN, D_MODEL)],
        out_specs=[_const((8, LANES)), _rows(TRN, D_MODEL), _rows(TRN, D_MODEL), _const((1, D_MODEL))],
        out_shape=[jax.ShapeDtypeStruct((8, LANES), F32), jax.ShapeDtypeStruct((SEQ, D_MODEL), F32),
                   jax.ShapeDtypeStruct((SEQ, D_MODEL), BF16), jax.ShapeDtypeStruct((1, D_MODEL), F32)],
        compiler_params=_cp(("arbitrary",)),
    )(x2, f, g4, target)


def _mid_bwd(x2, mix, dy, dh3, g2, g3):
    def body(x2_ref, mix_ref, dy_ref, dh3_ref, g2_ref, g3_ref, dx2_ref, dmix_ref, dg2_ref, dg3_ref):
        i = pl.program_id(0)
        x2 = x2_ref[...]
        mixv = mix_ref[...]
        dh3 = dh3_ref[...]
        r3 = _rstd(x2)
        dx2 = dy_ref[...] + _rms_bwd(x2, r3, g3_ref[...], dh3)
        dx2_ref[...] = dx2
        r2 = _rstd(mixv)
        dmix_ref[...] = _rms_bwd(mixv, r2, g2_ref[...], dx2).astype(BF16)

        @pl.when(i == 0)
        def _():
            dg2_ref[...] = jnp.zeros_like(dg2_ref)
            dg3_ref[...] = jnp.zeros_like(dg3_ref)

        dg3_ref[...] += jnp.sum(dh3 * x2 * r3, axis=0, keepdims=True)
        dg2_ref[...] += jnp.sum(dx2 * mixv * r2, axis=0, keepdims=True)

    return pl.pallas_call(
        body, name="mid_bwd", grid=(SEQ // TRN,),
        in_specs=[_rows(TRN, D_MODEL)] * 4 + [_const((1, D_MODEL))] * 2,
        out_specs=[_rows(TRN, D_MODEL), _rows(TRN, D_MODEL), _const((1, D_MODEL)), _const((1, D_MODEL))],
        out_shape=[jax.ShapeDtypeStruct((SEQ, D_MODEL), F32), jax.ShapeDtypeStruct((SEQ, D_MODEL), BF16),
                   jax.ShapeDtypeStruct((1, D_MODEL), F32), jax.ShapeDtypeStruct((1, D_MODEL), F32)],
        compiler_params=_cp(("arbitrary",)),
    )(x2, mix, dy, dh3, g2, g3)


def _first_bwd(x, dx2, dh1, g1):
    def body(x_ref, dx2_ref, dh1_ref, g1_ref, dx_ref, dg1_ref):
        i = pl.program_id(0)
        x = x_ref[...]
        dh1 = dh1_ref[...]
        r = _rstd(x)
        dx_ref[...] = dx2_ref[...] + _rms_bwd(x, r, g1_ref[...], dh1)

        @pl.when(i == 0)
        def _():
            dg1_ref[...] = jnp.zeros_like(dg1_ref)

        dg1_ref[...] += jnp.sum(dh1 * x * r, axis=0, keepdims=True)

    return pl.pallas_call(
        body, name="first_bwd", grid=(SEQ // TRN,),
        in_specs=[_rows(TRN, D_MODEL)] * 3 + [_const((1, D_MODEL))],
        out_specs=[_rows(TRN, D_MODEL), _const((1, D_MODEL))],
        out_shape=[jax.ShapeDtypeStruct((SEQ, D_MODEL), F32), jax.ShapeDtypeStruct((1, D_MODEL), F32)],
        compiler_params=_cp(("arbitrary",)),
    )(x, dx2, dh1, g1)


TC = 256
N_CB = D_FF // TC
GELU_C = math.sqrt(2.0 / math.pi)


def _shift_down(u, s):
    rolled = pltpu.roll(u, s, 0)
    row = lax.broadcasted_iota(jnp.int32, u.shape, 0)
    return jnp.where(row >= s, rolled, 0.0)


def _shift_up(u, s):
    n = u.shape[0]
    rolled = pltpu.roll(u, n - s, 0)
    row = lax.broadcasted_iota(jnp.int32, u.shape, 0)
    return jnp.where(row < n - s, rolled, 0.0)


def _conv3(u, w, b):
    return b + w[0:1] * _shift_down(u, 2) + w[1:2] * _shift_down(u, 1) + w[2:3] * u


def _gelu_and_grad(x):
    inner = GELU_C * (x + 0.044715 * (x * x * x))
    t = jnp.tanh(inner)
    gelu = 0.5 * x * (1.0 + t)
    dgelu = 0.5 * (1.0 + t) + 0.5 * x * (1.0 - t * t) * (GELU_C * (1.0 + 3 * 0.044715 * (x * x)))
    return gelu, dgelu


def _ffn_specs():
    col = lambda off: pl.BlockSpec((SEQ, TC), lambda *g: (0, g[-1] + off))
    w = lambda off: pl.BlockSpec((3, TC), lambda *g: (0, g[-1] + off))
    b = lambda off: pl.BlockSpec((1, TC), lambda *g: (0, g[-1] + off))
    return col, w, b


def _ffn_up_act(h3, wup_st, conv_w, conv_b):
    col, w, b = _ffn_specs()
    per_shard = wup_st.shape[2] // TC

    def body(h_ref, upg_ref, upv_ref, wg_ref, wv_ref, bg_ref, bv_ref, ug_ref, uv_ref, gate_ref, val_ref, act_ref):
        h = h_ref[...]
        ug = _dot(h, upg_ref[...])
        uv = _dot(h, upv_ref[...])
        ug_ref[...] = ug
        uv_ref[...] = uv
        gate = _conv3(ug, wg_ref[...], bg_ref[...])
        val = _conv3(uv, wv_ref[...], bv_ref[...])
        gate_ref[...] = gate
        val_ref[...] = val
        act_ref[...] = (_gelu_and_grad(gate)[0] * val).astype(BF16)

    return pl.pallas_call(
        body, name="ffn_up_act", grid=(N_CB,),
        in_specs=[_const((SEQ, D_MODEL)),
                  pl.BlockSpec((None, D_MODEL, TC), lambda j: (j // per_shard, 0, j % per_shard)),
                  pl.BlockSpec((None, D_MODEL, TC), lambda j: (2 + j // per_shard, 0, j % per_shard)),
                  w(0), w(N_CB), b(0), b(N_CB)],
        out_specs=[col(0)] * 5,
        out_shape=[jax.ShapeDtypeStruct((SEQ, D_FF), F32)] * 4 + [jax.ShapeDtypeStruct((SEQ, D_FF), BF16)],
        compiler_params=_cp(("parallel",)),
    )(h3, wup_st, wup_st, conv_w, conv_w, conv_b, conv_b)


def _ffn_act_bwd(u_gate, u_val, gate, val, df, wdown, conv_w):
    col, w, _ = _ffn_specs()
    both = lambda rows: pl.BlockSpec((2, rows, TC), lambda j: (0, 0, j))

    def body(ug_ref, uv_ref, gate_ref, val_ref, df_ref, wd_ref, wg_ref, wv_ref, du_ref, dw_ref, db_ref):
        da = _dot(df_ref[...], wd_ref[...], NT)
        gelu, dgelu = _gelu_and_grad(gate_ref[...])
        halves = ((da * val_ref[...] * dgelu, ug_ref, wg_ref[...]), (da * gelu, uv_ref, wv_ref[...]))
        for h, (duc, u_ref, wh) in enumerate(halves):
            uh = u_ref[...]
            up1, up2 = _shift_up(duc, 1), _shift_up(duc, 2)
            du_ref[h] = (wh[2:3] * duc + wh[1:2] * up1 + wh[0:1] * up2).astype(BF16)
            db_ref[h] = jnp.sum(duc, axis=0, keepdims=True)
            dw_ref[h] = jnp.concatenate(
                [jnp.sum(up2 * uh, axis=0, keepdims=True), jnp.sum(up1 * uh, axis=0, keepdims=True),
                 jnp.sum(duc * uh, axis=0, keepdims=True)], axis=0)

    return pl.pallas_call(
        body, name="ffn_act_bwd", grid=(N_CB,),
        in_specs=[col(0)] * 4 + [_const((SEQ, D_MODEL)), pl.BlockSpec((TC, D_MODEL), lambda j: (j, 0)), w(0), w(N_CB)],
        out_specs=[both(SEQ), both(3), both(1)],
        out_shape=[jax.ShapeDtypeStruct((2, SEQ, D_FF), BF16), jax.ShapeDtypeStruct((2, 3, D_FF), F32),
                   jax.ShapeDtypeStruct((2, 1, D_FF), F32)],
        compiler_params=_cp(("parallel",)),
    )(u_gate, u_val, gate, val, df, wdown, conv_w, conv_w)


def _t5_onehot():
    rel = (np.arange(BLOCK)[:, None] + BLOCK) - np.arange(2 * BLOCK)[None, :]
    n = np.maximum(rel, 0)
    max_exact = N_BUCKETS // 2
    large = max_exact + (np.log(np.maximum(n, 1).astype(np.float32) / np.float32(max_exact))
                         / np.float32(math.log(MAX_DISTANCE / max_exact))
                         * np.float32(N_BUCKETS - max_exact)).astype(np.int32)
    large = np.minimum(large, N_BUCKETS - 1)
    bucket = np.where(n < max_exact, n, large).reshape(-1)
    return (bucket[None, :] == np.arange(N_BUCKETS)[:, None]).astype(np.float32)


N_REL = BLOCK * 2 * BLOCK


def _bias_table(rel_bias_t, onehot):
    def body(rb_ref, oh_ref, o_ref):
        o_ref[...] = _dot_ind(rb_ref[...], oh_ref[...])

    return pl.pallas_call(
        body, name="bias_table", grid=(1,),
        in_specs=[_const((N_Q_HEADS, N_BUCKETS)), _const((N_BUCKETS, N_REL))],
        out_specs=_const((N_Q_HEADS, N_REL)),
        out_shape=jax.ShapeDtypeStruct((N_Q_HEADS, N_REL), F32),
        compiler_params=_cp(("arbitrary",)),
    )(rel_bias_t, onehot)


def _bias_table_bwd(dbias, onehot):
    def body(db_ref, oh_ref, o_ref):
        acc = None
        for part in _split(db_ref[...], 3):
            t = _dot(part, oh_ref[...], NT)
            acc = t if acc is None else acc + t
        o_ref[...] = acc

    return pl.pallas_call(
        body, name="bias_table_bwd", grid=(1,),
        in_specs=[_const((N_Q_HEADS, N_REL)), _const((N_BUCKETS, N_REL))],
        out_specs=_const((N_Q_HEADS, N_BUCKETS)),
        out_shape=jax.ShapeDtypeStruct((N_Q_HEADS, N_BUCKETS), F32),
        compiler_params=_cp(("arbitrary",)),
    )(dbias, onehot)


def _attn_pieces(n, q, kvp, kvc, bias_ref, sinks_ref, hk):
    qi = lax.broadcasted_iota(jnp.int32, (BLOCK, 2 * BLOCK), 0)
    kj = lax.broadcasted_iota(jnp.int32, (BLOCK, 2 * BLOCK), 1)
    rel = qi + BLOCK - kj
    first_key = jnp.where(n > 0, 0, BLOCK)
    ok = jnp.where(rel >= 0, jnp.where(rel < BLOCK, jnp.where(kj >= first_key, 1.0, 0.0), 0.0), 0.0)
    ok4 = jnp.concatenate([ok] * Q_PER_KV, axis=0) > 0.5
    c0 = hk * HEAD_DIM
    kcat = jnp.concatenate([kvp[:, c0:c0 + HEAD_DIM], kvc[:, c0:c0 + HEAD_DIM]], axis=0).astype(BF16)
    vcat = jnp.concatenate([kvp[:, D_KV + c0:D_KV + c0 + HEAD_DIM], kvc[:, D_KV + c0:D_KV + c0 + HEAD_DIM]],
                           axis=0).astype(BF16)
    q0 = hk * Q_PER_KV * HEAD_DIM
    qs = jnp.concatenate([q[:, q0 + g * HEAD_DIM:q0 + (g + 1) * HEAD_DIM] for g in range(Q_PER_KV)],
                         axis=0).astype(BF16)
    s = _dot(qs, kcat, NT) * (HEAD_DIM ** -0.5) + bias_ref[hk]
    s = jnp.where(ok4, s, NEG_INF)
    row = lax.broadcasted_iota(jnp.int32, (Q_PER_KV * BLOCK, 1), 0)
    sink = jnp.zeros((Q_PER_KV * BLOCK, 1), F32)
    for g in range(Q_PER_KV):
        sink = jnp.where((row >> BLOCK_SHIFT) == g, sinks_ref[hk * Q_PER_KV + g], sink)
    m = jnp.maximum(jnp.max(s, axis=-1, keepdims=True), sink)
    p = jnp.exp(s - m)
    es = jnp.exp(sink - m)
    inv = 1.0 / (jnp.sum(p, axis=-1, keepdims=True) + es)
    return qs, kcat, vcat, p * inv, es * inv


def _attn_in_specs():
    return [pl.BlockSpec((BLOCK, D_ATTN), lambda n: (n, 0)),
            pl.BlockSpec((BLOCK, 2 * D_KV), lambda n: (jnp.maximum(n - 1, 0), D_ATTN // (2 * D_KV))),
            pl.BlockSpec((BLOCK, 2 * D_KV), lambda n: (n, D_ATTN // (2 * D_KV))),
            _const((N_KV_HEADS, Q_PER_KV * BLOCK, 2 * BLOCK)),
            pl.BlockSpec(memory_space=pltpu.SMEM)]


def _unstack_heads(t):
    return jnp.concatenate([t[g * BLOCK:(g + 1) * BLOCK] for g in range(Q_PER_KV)], axis=1)


def _attn_fwd(proj, bias, sinks):
    def body(q_ref, kvp_ref, kvc_ref, bias_ref, sinks_ref, o_ref):
        n = pl.program_id(0)
        q, kvp, kvc = q_ref[...], kvp_ref[...], kvc_ref[...]
        outs = []
        for hk in range(N_KV_HEADS):
            _, _, vcat, probs, _ = _attn_pieces(n, q, kvp, kvc, bias_ref, sinks_ref, hk)
            outs.append(_unstack_heads(_dot(probs.astype(BF16), vcat)))
        o_ref[...] = jnp.concatenate(outs, axis=1)

    return pl.pallas_call(
        body, name="attn_fwd", grid=(SEQ // BLOCK,),
        in_specs=_attn_in_specs(),
        out_specs=pl.BlockSpec((BLOCK, D_ATTN), lambda n: (n, 0)),
        out_shape=jax.ShapeDtypeStruct((SEQ, D_ATTN), F32),
        compiler_params=_cp(("parallel",)),
    )(proj, proj, proj, bias, sinks)


def _attn_bwd(proj, bias, sinks, dcat):
    nb = SEQ // BLOCK

    def body(q_ref, kvp_ref, kvc_ref, bias_ref, sinks_ref, do_ref, dq_ref, dkv_ref, dbias_ref, dsink_ref, dsacc):
        n = pl.program_id(0)

        @pl.when(n == 0)
        def _():
            dkv_ref[...] = jnp.zeros_like(dkv_ref)
            dbias_ref[...] = jnp.zeros_like(dbias_ref)
            dsacc[...] = jnp.zeros_like(dsacc)

        q, kvp, kvc = q_ref[...], kvp_ref[...], kvc_ref[...]
        do_all = do_ref[...]
        dqs, dks, dvs = [], [], []
        for hk in range(N_KV_HEADS):
            qs, kcat, vcat, probs, psink = _attn_pieces(n, q, kvp, kvc, bias_ref, sinks_ref, hk)
            q0 = hk * Q_PER_KV * HEAD_DIM
            do = jnp.concatenate([do_all[:, q0 + g * HEAD_DIM:q0 + (g + 1) * HEAD_DIM] for g in range(Q_PER_KV)],
                                 axis=0).astype(BF16)
            dprobs = _dot(do, vcat, NT)
            dvs.append(_dot(probs.astype(BF16), do, TN))
            rowdot = jnp.sum(probs * dprobs, axis=-1, keepdims=True)
            ds = probs * (dprobs - rowdot)
            dsacc[hk] += -psink * rowdot
            dbias_ref[hk] += ds
            dsb = (ds * (HEAD_DIM ** -0.5)).astype(BF16)
            dqs.append(_unstack_heads(_dot(dsb, kcat)))
            dks.append(_dot(dsb, qs, TN))
        dq_ref[...] = jnp.concatenate(dqs, axis=1)
        upd = jnp.concatenate(dks + dvs, axis=1)
        cur = pl.multiple_of(n * BLOCK, BLOCK)
        dkv_ref[pl.ds(cur, BLOCK), :] += upd[BLOCK:]

        @pl.when(n > 0)
        def _():
            prev = pl.multiple_of((n - 1) * BLOCK, BLOCK)
            dkv_ref[pl.ds(prev, BLOCK), :] += upd[:BLOCK]

        @pl.when(n == nb - 1)
        def _():
            for hk in range(N_KV_HEADS):
                for g in range(Q_PER_KV):
                    tot = jnp.sum(dsacc[hk, g * BLOCK:(g + 1) * BLOCK, :], axis=0, keepdims=True)
                    h = hk * Q_PER_KV + g
                    dsink_ref[h:h + 1, :] = jnp.broadcast_to(tot, (1, LANES))

    return pl.pallas_call(
        body, name="attn_bwd", grid=(nb,),
        in_specs=_attn_in_specs() + [pl.BlockSpec((BLOCK, D_ATTN), lambda n: (n, 0))],
        out_specs=[pl.BlockSpec((BLOCK, D_ATTN), lambda n: (n, 0)), _const((SEQ, 2 * D_KV)),
                   _const((N_KV_HEADS, Q_PER_KV * BLOCK, 2 * BLOCK)), _const((N_Q_HEADS, LANES))],
        out_shape=[jax.ShapeDtypeStruct((SEQ, D_ATTN), F32), jax.ShapeDtypeStruct((SEQ, 2 * D_KV), F32),
                   jax.ShapeDtypeStruct((N_KV_HEADS, Q_PER_KV * BLOCK, 2 * BLOCK), F32),
                   jax.ShapeDtypeStruct((N_Q_HEADS, LANES), F32)],
        scratch_shapes=[pltpu.VMEM((N_KV_HEADS, Q_PER_KV * BLOCK, 1), F32)],
        compiler_params=_cp(("arbitrary",)),
    )(proj, proj, proj, bias, sinks, dcat)


@jax.custom_vjp
def _head_sum(x):
    ones = _head_ones(LANES)
    return jnp.concatenate([_dot_ind(x[:, c:c + LANES], ones, 2) for c in range(0, x.shape[-1], LANES)], axis=1)


_head_sum.defvjp(lambda x: (_head_sum(x), None), lambda _, ct: (_head_sum(ct),))


@jax.custom_vjp
def _bdot(a, w):
    return _dot(a.astype(BF16), w.astype(BF16))


def _bdot_bwd(res, ct):
    a, w = res
    ctb = ct.astype(BF16)
    return _dot(ctb, w.astype(BF16), NT), _dot(a.astype(BF16), ctb, TN)


_bdot.defvjp(lambda a, w: (_bdot(a, w), (a, w)), _bdot_bwd)


def _sigmoid(x):
    return 0.5 * (jnp.tanh(0.5 * x) + 1.0)


def _softplus(x):
    return jnp.maximum(x, 0.0) + jnp.log(1.0 + jnp.exp(-jnp.abs(x)))


def _rwkv_core(r, k, v, zwa, zg, w0, wdu, a0, wiu, wgu, k_k, k_a):
    w_log = -_softplus(-(w0 + _bdot(jnp.tanh(zwa), wdu))) - 0.5
    decay = jnp.exp(-jnp.exp(w_log))
    a = _sigmoid(a0 + _bdot(zwa, wiu))
    g = _bdot(_sigmoid(zg), wgu)
    kk = k * k_k
    kk = kk / jnp.maximum(jnp.sqrt(_head_sum(kk * kk)), 1e-12)
    k2 = k * (1.0 + (a - 1.0) * k_a)
    return r, decay, k2, v, -kk, kk * a, g


def _rwkv_out(o, r, k2, v, g, lng, lnb, rk):
    mu = _head_sum(o) * (1.0 / HEAD_DIM)
    d = o - mu
    var = _head_sum(d * d) * (1.0 / HEAD_DIM)
    on = d * lax.rsqrt(var + GN_EPS) * lng + lnb
    bonus = _head_sum(r * k2 * rk) * v
    return (on + bonus) * g


P_SPLITS = (0, 512, 1024, 1536, 1664, 1792)
N_PREP_PARAMS = 7
HALO = 8


def _shifted_pieces(i, p_ref, halo_ref, mix_ref):
    p = p_ref[:, P_OFF:]
    prev_row = halo_ref[HALO - 1:HALO, P_OFF:] * jnp.where(i > 0, 1.0, 0.0)
    row = lax.broadcasted_iota(jnp.int32, p.shape, 0)
    pprev = jnp.where(row == 0, prev_row, pltpu.roll(p, 1, 0))
    delta = pprev - p
    ps = p + delta * mix_ref[...]
    return [ps[:, a:b] for a, b in zip(P_SPLITS[:-1], P_SPLITS[1:])], delta


def _prep_in_specs():
    return [_rows(TR, D_IN),
            pl.BlockSpec((HALO, D_IN), lambda i: (jnp.maximum(i * (TR // HALO) - 1, 0), 0)),
            _const((1, RWKV_COLS)), _const((1, D_RWKV)), _const((LANES, D_RWKV)), _const((1, D_RWKV)),
            _const((LANES, D_RWKV)), _const((LANES, D_RWKV)), _const((1, D_RWKV)), _const((1, D_RWKV))]


def _rwkv_prep(proj, mix, prm):
    def body(p_ref, halo_ref, mix_ref, *refs):
        prm_refs, outs = refs[:N_PREP_PARAMS], refs[N_PREP_PARAMS:]
        pieces, _ = _shifted_pieces(pl.program_id(0), p_ref, halo_ref, mix_ref)
        vals = _rwkv_core(*pieces, *[t[...] for t in prm_refs])
        for ref, val in zip(outs, vals):
            ref[...] = val

    return pl.pallas_call(
        body, name="rwkv_prep", grid=(SEQ // TR,),
        in_specs=_prep_in_specs(),
        out_specs=[_rows(TR, D_RWKV)] * 7,
        out_shape=[jax.ShapeDtypeStruct((SEQ, D_RWKV), F32)] * 7,
        compiler_params=_cp(("parallel",)),
    )(proj, proj, mix, *prm)


def _rwkv_prep_bwd(proj, mix, prm, cts):
    def body(p_ref, halo_ref, mix_ref, *refs):
        i = pl.program_id(0)
        prm_refs = refs[:N_PREP_PARAMS]
        ct_refs = refs[N_PREP_PARAMS:N_PREP_PARAMS + 10]
        dps_ref, dmix_ref = refs[N_PREP_PARAMS + 10:N_PREP_PARAMS + 12]
        dprm_refs = refs[N_PREP_PARAMS + 12:]
        pieces, delta = _shifted_pieces(i, p_ref, halo_ref, mix_ref)
        _, vjp = jax.vjp(_rwkv_core, *pieces, *[t[...] for t in prm_refs])
        dr1, dr2, dw, dk1, dk2, dv1, dv2, dkkn, db, dg = [t[...] for t in ct_refs]
        grads = vjp((dr1 + dr2, dw, dk1 + dk2, dv1 + dv2, dkkn, db, dg))
        dps = jnp.concatenate(grads[:5], axis=1)
        dps_ref[...] = dps

        @pl.when(i == 0)
        def _():
            dmix_ref[...] = jnp.zeros_like(dmix_ref)
            for ref in dprm_refs:
                ref[...] = jnp.zeros_like(ref)

        dmix_ref[...] += jnp.sum(dps * delta, axis=0, keepdims=True)
        for ref, gval in zip(dprm_refs, grads[5:]):
            ref[...] += gval

    prm_shapes = [(1, D_RWKV), (LANES, D_RWKV), (1, D_RWKV), (LANES, D_RWKV), (LANES, D_RWKV), (1, D_RWKV), (1, D_RWKV)]
    return pl.pallas_call(
        body, name="rwkv_prep_bwd", grid=(SEQ // TR,),
        in_specs=_prep_in_specs() + [_rows(TR, D_RWKV)] * 10,
        out_specs=[_rows(TR, RWKV_COLS), _const((1, RWKV_COLS))] + [_const(s) for s in prm_shapes],
        out_shape=[jax.ShapeDtypeStruct((SEQ, RWKV_COLS), F32), jax.ShapeDtypeStruct((1, RWKV_COLS), F32)]
        + [jax.ShapeDtypeStruct(s, F32) for s in prm_shapes],
        compiler_params=_cp(("arbitrary",)),
    )(proj, proj, mix, *prm, *cts)


def _rwkv_post(o, r, k2, v, g, lng, lnb, rk, attn):
    def body(o_ref, r_ref, k_ref, v_ref, g_ref, lng_ref, lnb_ref, rk_ref, attn_ref, cat_ref):
        rw = _rwkv_out(*[t[...] for t in (o_ref, r_ref, k_ref, v_ref, g_ref, lng_ref, lnb_ref, rk_ref)])
        cat_ref[...] = jnp.concatenate([attn_ref[...], rw], axis=1).astype(BF16)

    return pl.pallas_call(
        body, name="rwkv_post", grid=(SEQ // TR,),
        in_specs=[_rows(TR, D_RWKV)] * 5 + [_const((1, D_RWKV))] * 3 + [_rows(TR, D_ATTN)],
        out_specs=_rows(TR, D_MODEL),
        out_shape=jax.ShapeDtypeStruct((SEQ, D_MODEL), BF16),
        compiler_params=_cp(("parallel",)),
    )(o, r, k2, v, g, lng, lnb, rk, attn)


def _rwkv_post_bwd(o, r, k2, v, g, lng, lnb, rk, dcat):
    def body(o_ref, r_ref, k_ref, v_ref, g_ref, lng_ref, lnb_ref, rk_ref, dcat_ref,
             do_ref, dr_ref, dk_ref, dv_ref, dg_ref, dlng_ref, dlnb_ref, drk_ref):
        i = pl.program_id(0)
        args = [t[...] for t in (o_ref, r_ref, k_ref, v_ref, g_ref, lng_ref, lnb_ref, rk_ref)]
        _, vjp = jax.vjp(_rwkv_out, *args)
        grads = vjp(dcat_ref[:, D_ATTN:])
        for ref, gval in zip((do_ref, dr_ref, dk_ref, dv_ref, dg_ref), grads[:5]):
            ref[...] = gval

        @pl.when(i == 0)
        def _():
            for ref in (dlng_ref, dlnb_ref, drk_ref):
                ref[...] = jnp.zeros_like(ref)

        for ref, gval in zip((dlng_ref, dlnb_ref, drk_ref), grads[5:]):
            ref[...] += gval

    return pl.pallas_call(
        body, name="rwkv_post_bwd", grid=(SEQ // TR,),
        in_specs=[_rows(TR, D_RWKV)] * 5 + [_const((1, D_RWKV))] * 3 + [_rows(TR, D_MODEL)],
        out_specs=[_rows(TR, D_RWKV)] * 5 + [_const((1, D_RWKV))] * 3,
        out_shape=[jax.ShapeDtypeStruct((SEQ, D_RWKV), F32)] * 5 + [jax.ShapeDtypeStruct((1, D_RWKV), F32)] * 3,
        compiler_params=_cp(("arbitrary",)),
    )(o, r, k2, v, g, lng, lnb, rk, dcat)


def _assemble_dproj(dq, dkv, dps, mix):
    last = SEQ // HALO - 1

    def body(dq_ref, dkv_ref, dps_ref, nxt_ref, mix_ref, o_ref):
        i = pl.program_id(0)
        dps = dps_ref[...]
        mixv = mix_ref[...]
        nxt_row = nxt_ref[0:1, :] * jnp.where(i < SEQ // TR - 1, 1.0, 0.0)
        row = lax.broadcasted_iota(jnp.int32, dps.shape, 0)
        up = jnp.where(row == TR - 1, nxt_row, pltpu.roll(dps, TR - 1, 0))
        dp = dps * (1.0 - mixv) + up * mixv
        o_ref[...] = jnp.concatenate([dq_ref[...], dkv_ref[...], dp], axis=1).astype(BF16)

    return pl.pallas_call(
        body, name="assemble_dproj", grid=(SEQ // TR,),
        in_specs=[_rows(TR, D_ATTN), _rows(TR, 2 * D_KV), _rows(TR, RWKV_COLS),
                  pl.BlockSpec((HALO, RWKV_COLS), lambda i: (jnp.minimum((i + 1) * (TR // HALO), last), 0)),
                  _const((1, RWKV_COLS))],
        out_specs=_rows(TR, D_IN),
        out_shape=jax.ShapeDtypeStruct((SEQ, D_IN), BF16),
        compiler_params=_cp(("parallel",)),
    )(dq, dkv, dps, dps, mix)


N_PAIR = D_RWKV // LANES
CHUNK = 64
N_CHUNK = SEQ // CHUNK
GROUP = 32
STATE = (N_PAIR, HEAD_DIM, LANES)


def _lane_sums(lhs_tiles, ones2):
    out = _dot(jnp.concatenate(lhs_tiles, axis=0), ones2)
    return [out[i * HEAD_DIM:(i + 1) * HEAD_DIM] for i in range(len(lhs_tiles))]


def _seg_sum(xs, ones2):
    return _lane_sums([jnp.concatenate(_split(x, 2), axis=1) for x in xs], ones2)


def _seg_sum_rows(xs, ones2):
    out = _dot(jnp.concatenate(_split(jnp.concatenate(xs, axis=0), 2), axis=1), ones2)
    return [out[i * GROUP:(i + 1) * GROUP] for i in range(len(xs))]


def _col_form(rows, diag, ones2):
    zero = jnp.zeros((HEAD_DIM, LANES), BF16)
    tiles = []
    for row in rows:
        hi = row.astype(BF16)
        lo = (row - hi.astype(F32)).astype(BF16)
        tiles.append(jnp.concatenate(
            [jnp.where(diag, jnp.broadcast_to(part, (HEAD_DIM, LANES)), zero) for part in (hi, lo)], axis=1))
    return _lane_sums(tiles, ones2)


def _scan_consts():
    ones2 = jnp.concatenate([_head_ones(LANES)] * 2, axis=0)
    sub = lax.broadcasted_iota(jnp.int32, (HEAD_DIM, LANES), 0)
    lane_in_head = lax.broadcasted_iota(jnp.int32, (HEAD_DIM, LANES), 1) & (HEAD_DIM - 1)
    return ones2, lane_in_head == sub, lane_in_head


def _rows_of_columns(tile):
    t = tile.T
    return jnp.concatenate([t[:CHUNK], t[HEAD_DIM:HEAD_DIM + CHUNK]], axis=1)


def _pair(j):
    return slice(j * LANES, (j + 1) * LANES)


def _scan_fwd(r, w, k, v, kkn, b):
    def body(r_ref, w_ref, k_ref, v_ref, kkn_ref, b_ref, o_ref, st_ref, sa_ref, s_scr):
        c = pl.program_id(0)
        ones2, diag, lane_in_head = _scan_consts()

        @pl.when(c == 0)
        def _():
            s_scr[...] = jnp.zeros_like(s_scr)

        def group(gi, carry):
            row0 = pl.multiple_of(gi * GROUP, GROUP)
            states, ocols = list(carry[:N_PAIR]), list(carry[N_PAIR:])
            tiles = [[t[pl.ds(row0, GROUP), _pair(j)] for t in (r_ref, w_ref, k_ref, v_ref, kkn_ref, b_ref)]
                     for j in range(N_PAIR)]
            def row(j, name, u):
                return tiles[j]["rwkvnb".index(name)][u:u + 1]

            def emit_out(u, after):
                outs = _seg_sum([s[j] * row(j, "r", u + d) for d, s in enumerate(after) for j in range(N_PAIR)], ones2)
                for d in range(2):
                    here = lane_in_head == gi * GROUP + u + d
                    for j in range(N_PAIR):
                        ocols[j] = jnp.where(here, outs[d * N_PAIR + j], ocols[j])

            def vcols_of(u):
                cols = _col_form([row(j, "v", u + d) for d in range(2) for j in range(N_PAIR)], diag, ones2)
                return cols[:N_PAIR], cols[N_PAIR:]

            n_next = [pltpu.roll(tiles[j][4], GROUP - 1, 0) for j in range(N_PAIR)]
            dots = _seg_sum_rows([tiles[j][5] * n_next[j] for j in range(N_PAIR)]
                                 + [tiles[j][2] * n_next[j] for j in range(N_PAIR)], ones2)
            b_n, k_n = dots[:N_PAIR], dots[N_PAIR:]
            w_n = [tiles[j][1] * n_next[j] for j in range(N_PAIR)]

            vcols = vcols_of(0)
            after = None
            for u in range(0, GROUP, 2):
                prods = _seg_sum([states[j] * row(j, "n", u) for j in range(N_PAIR)]
                                 + [states[j] * w_n[j][u:u + 1] for j in range(N_PAIR)], ones2)
                if after is not None:
                    emit_out(u - 2, after)
                nxt = vcols_of(u + 2) if u + 2 < GROUP else None
                first, second = [], []
                for j in range(N_PAIR):
                    sa1 = prods[j]
                    sa2 = prods[N_PAIR + j] + sa1 * b_n[j][u:u + 1] + vcols[0][j] * k_n[j][u:u + 1]
                    s1 = states[j] * row(j, "w", u) + sa1 * row(j, "b", u) + vcols[0][j] * row(j, "k", u)
                    s2 = s1 * row(j, "w", u + 1) + sa2 * row(j, "b", u + 1) + vcols[1][j] * row(j, "k", u + 1)
                    st_ref[row0 + u, j] = s1
                    sa_ref[row0 + u, j] = sa1
                    st_ref[row0 + u + 1, j] = s2
                    sa_ref[row0 + u + 1, j] = sa2
                    first.append(s1)
                    second.append(s2)
                    states[j] = s2
                after, vcols = (first, second), nxt
            emit_out(GROUP - 2, after)
            return tuple(states + ocols)

        zero = jnp.zeros((HEAD_DIM, LANES), F32)
        fin = lax.fori_loop(0, CHUNK // GROUP, group, tuple(s_scr[j] for j in range(N_PAIR)) + (zero,) * N_PAIR)
        for j in range(N_PAIR):
            s_scr[j] = fin[j]
            o_ref[:, _pair(j)] = _rows_of_columns(fin[N_PAIR + j])

    blk = pl.BlockSpec((CHUNK, D_RWKV), lambda c: (c, 0))
    per_step = pl.BlockSpec((CHUNK,) + STATE, lambda c: (c, 0, 0, 0))
    return pl.pallas_call(
        body, name="rwkv_scan_fwd", grid=(N_CHUNK,),
        in_specs=[blk] * 6,
        out_specs=[blk, per_step, per_step],
        out_shape=[jax.ShapeDtypeStruct((SEQ, D_RWKV), F32)] + [jax.ShapeDtypeStruct((SEQ,) + STATE, F32)] * 2,
        scratch_shapes=[pltpu.VMEM(STATE, F32)],
        compiler_params=_cp(("arbitrary",)),
    )(r, w, k, v, kkn, b)


def _scan_bwd(r, w, k, v, kkn, b, do, states, sas, ds_in, prev, name, first_chunk, n_chunks):
    top = first_chunk + n_chunks - 1

    def body(r_ref, w_ref, k_ref, v_ref, kkn_ref, b_ref, do_ref, st_ref, before_ref, sa_ref, ds_in_ref, *rest):
        dr_ref, dw_ref, dk_ref, dv_ref, dkkn_ref, db_ref, ds_out_ref, ds_scr = rest[-8:]
        i = pl.program_id(0)
        ones2, diag, lane_in_head = _scan_consts()

        @pl.when(i == 0)
        def _():
            ds_scr[...] = ds_in_ref[...]

        entry = [before_ref[0, j] * jnp.where(i < top, 1.0, 0.0) for j in range(N_PAIR)]

        def reverse(gr, carry):
            gi = CHUNK // GROUP - 1 - gr
            row0 = pl.multiple_of(gi * GROUP, GROUP)
            dstates, dvcols = list(carry[:N_PAIR]), list(carry[N_PAIR:])
            tiles = [[t[pl.ds(row0, GROUP), _pair(j)]
                      for t in (r_ref, w_ref, k_ref, v_ref, kkn_ref, b_ref, do_ref)] for j in range(N_PAIR)]
            rows = [[[None] * GROUP for _ in range(5)] for _ in range(N_PAIR)]

            def row(j, name, u):
                return tiles[j]["rwkvnbd".index(name)][u:u + 1]

            def cols_of(u):
                cols = _col_form([row(j, name, u - d) for d in range(2) for name in "dv" for j in range(N_PAIR)],
                                 diag, ones2)
                return [[(cols[(2 * d) * N_PAIR + j], cols[(2 * d + 1) * N_PAIR + j]) for j in range(N_PAIR)]
                        for d in range(2)]

            def emit_dv(u, dsps):
                outs = _seg_sum([dsp[j] * row(j, "k", u - d) for d, dsp in enumerate(dsps) for j in range(N_PAIR)], ones2)
                for d in range(2):
                    here = lane_in_head == gi * GROUP + u - d
                    for j in range(N_PAIR):
                        dvcols[j] = jnp.where(here, outs[d * N_PAIR + j], dvcols[j])

            b_prev = [pltpu.roll(tiles[j][5], 1, 0) for j in range(N_PAIR)]
            dots = _seg_sum_rows([tiles[j][4] * b_prev[j] for j in range(N_PAIR)]
                                 + [tiles[j][0] * tiles[j][5] for j in range(N_PAIR)], ones2)
            n_b, r_b = dots[:N_PAIR], dots[N_PAIR:]
            w_b = [tiles[j][1] * b_prev[j] for j in range(N_PAIR)]

            def outputs(u, j, dsp, dsa, docol, vcol):
                tl = gi * GROUP + u
                if u > 0:
                    s_prev = st_ref[tl - 1, j]
                else:
                    s_prev = jnp.where(gi == 0, entry[j], st_ref[jnp.maximum(tl - 1, 0), j])
                rows[j][0][u] = jnp.sum(st_ref[tl, j] * docol, axis=0, keepdims=True)
                rows[j][1][u] = jnp.sum(dsp * s_prev, axis=0, keepdims=True)
                rows[j][2][u] = jnp.sum(dsp * vcol, axis=0, keepdims=True)
                rows[j][3][u] = jnp.sum(s_prev * dsa, axis=0, keepdims=True)
                rows[j][4][u] = jnp.sum(dsp * sa_ref[tl, j], axis=0, keepdims=True)

            cols = cols_of(GROUP - 1)
            before = None
            for u in range(GROUP - 1, 0, -2):
                dsp1 = [dstates[j] + cols[0][j][0] * row(j, "r", u) for j in range(N_PAIR)]
                prods = _seg_sum([dsp1[j] * row(j, "b", u) for j in range(N_PAIR)]
                                 + [dsp1[j] * w_b[j][u:u + 1] for j in range(N_PAIR)], ones2)
                if before is not None:
                    emit_dv(u + 2, before)
                nxt = cols_of(u - 2) if u >= 2 else None
                dsp2 = []
                for j in range(N_PAIR):
                    dsa1 = prods[j]
                    dsa2 = prods[N_PAIR + j] + dsa1 * n_b[j][u:u + 1] + cols[1][j][0] * r_b[j][u - 1:u]
                    mid = dsp1[j] * row(j, "w", u) + dsa1 * row(j, "n", u) + cols[1][j][0] * row(j, "r", u - 1)
                    outputs(u, j, dsp1[j], dsa1, *cols[0][j])
                    outputs(u - 1, j, mid, dsa2, *cols[1][j])
                    dstates[j] = mid * row(j, "w", u - 1) + dsa2 * row(j, "n", u - 1)
                    dsp2.append(mid)
                before, cols = (dsp1, dsp2), nxt
            emit_dv(1, before)
            for j in range(N_PAIR):
                for ref, rr in zip((dr_ref, dw_ref, dk_ref, dkkn_ref, db_ref), rows[j]):
                    ref[pl.ds(row0, GROUP), _pair(j)] = jnp.concatenate(rr, axis=0)
            return tuple(dstates + dvcols)

        zero = jnp.zeros((HEAD_DIM, LANES), F32)
        dfin = lax.fori_loop(0, CHUNK // GROUP, reverse, tuple(ds_scr[j] for j in range(N_PAIR)) + (zero,) * N_PAIR)
        for j in range(N_PAIR):
            ds_scr[j] = dfin[j]
            dv_ref[:, _pair(j)] = _rows_of_columns(dfin[N_PAIR + j])

        @pl.when(i == n_chunks - 1)
        def _():
            ds_out_ref[...] = ds_scr[...]

    blk = pl.BlockSpec((CHUNK, D_RWKV), lambda i: (top - i, 0))
    per_step = pl.BlockSpec((CHUNK,) + STATE, lambda i: (top - i, 0, 0, 0))
    step_before = pl.BlockSpec((1,) + STATE, lambda i: (jnp.maximum((top - i) * CHUNK - 1, 0), 0, 0, 0))
    prev = [] if prev is None else list(prev)
    outs = pl.pallas_call(
        body, name=name, grid=(n_chunks,),
        in_specs=[blk] * 7 + [per_step, step_before, per_step, _const(STATE)] + [ANY] * len(prev),
        out_specs=[blk] * 6 + [_const(STATE)],
        out_shape=[jax.ShapeDtypeStruct((SEQ, D_RWKV), F32)] * 6 + [jax.ShapeDtypeStruct(STATE, F32)],
        scratch_shapes=[pltpu.VMEM(STATE, F32)],
        input_output_aliases={11 + t: t for t in range(len(prev))},
        compiler_params=_cp(("arbitrary",)),
    )(r, w, k, v, kkn, b, do, states, states, sas, ds_in, *prev)
    return outs[:6], outs[6]


def _stacked(rows, cols, pick):
    return pl.BlockSpec((None, rows, cols), pick)


def _local_step(x, target, sm, win_st):
    def tied(t, token):
        return t if token is None else t + token[0:1, 0:1].reshape((1,) * t.ndim)

    zpad = jnp.zeros((LORA_DECAY, D_RWKV), F32)
    prm = [sm["w0"], jnp.concatenate([sm["w_decay_up"], zpad], axis=0), sm["a0"],
           jnp.concatenate([zpad, sm["w_iclr_up"]], axis=0), sm["w_gate_up"], sm["k_k"], sm["k_a"]]
    mix = sm["rwkv_shift_mix"]
    onehot = jnp.asarray(_t5_onehot(), BF16)
    sinks = sm["sinks"].reshape(N_Q_HEADS)
    lng, lnb, rk = sm["ln_x_g"], sm["ln_x_b"], sm["r_k"].reshape(1, D_RWKV)

    h1 = _norm_cast(x, sm["norm_mix_pre"], "norm_in")
    proj = _matmul(h1, win_st, "nn", "proj", m=SEQ, n=D_IN, k=D_MODEL, tm=SEQ, tn=640,
                   b_spec=_stacked(D_MODEL, 640, lambda i, j: (j, 0, 0)))
    bias = _bias_table(sm["rel_bias"].T, onehot).reshape(N_KV_HEADS, Q_PER_KV * BLOCK, 2 * BLOCK)
    attn = _attn_fwd(proj, bias, sinks)
    r, w, k2, v, kkn, b, g = _rwkv_prep(proj, mix, prm)
    o, states, sas = _scan_fwd(r, w, k2, v, kkn, b)
    wout, wup_st, wdown = yield ("rest_weights", o)
    cat = _rwkv_post(o, r, k2, v, g, lng, lnb, rk, attn)
    mixo = _matmul(cat, wout, "nn", "out_proj", m=SEQ, n=D_MODEL, k=D_MODEL, tm=SEQ, tn=512)
    x2, h3 = _mix_norm(x, mixo, sm["norm_mix_post"], sm["norm_ffn_pre"])
    u_gate, u_val, gate, val, act = _ffn_up_act(h3, wup_st, sm["conv_w"], sm["conv_b"])
    f = _matmul(act, wdown, "nn", "ffn_down", m=SEQ, n=D_MODEL, k=D_FF, tm=1024, tn=512)
    loss, dy, df, d_g4 = _loss_head(x2, f, sm["norm_ffn_post"], target)

    d_wdown = _matmul(act, df, "tn", "d_wdown", m=D_FF, n=D_MODEL, k=SEQ, tm=1024, tn=D_MODEL)
    du, d_convw, d_convb = _ffn_act_bwd(u_gate, u_val, gate, val, df, wdown, sm["conv_w"])
    d_convw = d_convw.transpose(1, 0, 2).reshape(3, 2 * D_FF)
    d_convb = d_convb.reshape(1, 2 * D_FF)
    dh3 = _matmul_nt_shards(du, wup_st, "d_h3", m=SEQ, n=D_MODEL, tm=512, tn=512,
                            a_spec=pl.BlockSpec((2, 512, D_FF), lambda i, j: (0, i, 0)),
                            a_piece=lambda ref, s: ref[s // 2, :, (s % 2) * 2048:(s % 2 + 1) * 2048])
    d_wup = _matmul(h3, du, "tn", "d_wup", m=D_MODEL, n=2 * D_FF, k=SEQ, tm=D_MODEL, tn=1024,
                    b_spec=pl.BlockSpec((None, SEQ, 1024), lambda i, j: (j // 4, 0, j % 4)),
                    out=((N_CHIPS, D_MODEL, 2048), _stacked(D_MODEL, 1024, lambda i, j: (j // 2, 0, j % 2))))
    dx2, dmix, d_g2, d_g3 = _mid_bwd(x2, mixo, dy, dh3, sm["norm_mix_post"], sm["norm_ffn_pre"])
    dcat = _matmul(dmix, wout, "nt", "d_cat", m=SEQ, n=D_MODEL, k=D_MODEL, tm=SEQ, tn=512)
    d_wout = _matmul(cat, dmix, "tn", "d_wout", m=D_MODEL, n=D_MODEL, k=SEQ, tm=512, tn=D_MODEL)
    token = yield ("grads_a", (d_wdown, d_wup, d_wout))
    do, dr_p, dk_p, dv_p, dg, d_lng, d_lnb, d_rk = _rwkv_post_bwd(o, r, k2, v, g, lng, tied(lnb, token), rk, dcat)
    half = N_CHUNK // 2
    ds_end = jnp.zeros(STATE, F32)
    late, ds_mid = _scan_bwd(r, w, k2, v, kkn, b, do, states, sas, ds_end, None, "rwkv_scan_bwd_late", half, half)
    token = yield ("seam_1", ds_mid)
    scan_cts, ds_first = _scan_bwd(r, w, k2, v, kkn, b, do, states, sas, tied(ds_mid, token), late,
                                   "rwkv_scan_bwd_early", 0, half)
    dr_s, dw_s, dk_s, dv_s, dkkn_s, db_s = scan_cts
    token = yield ("seam_2", ds_first)
    prep_grads = _rwkv_prep_bwd(proj, tied(mix, token), prm,
                                (dr_s, dr_p, dw_s, dk_s, dk_p, dv_s, dv_p, dkkn_s, db_s, dg))
    dps, d_mix, d_w0, d_wdu, d_a0, d_wiu, d_wgu, d_kk, d_ka = prep_grads
    dq, dkv, dbias, dsink = _attn_bwd(proj, bias, sinks, dcat)
    d_relb = _bias_table_bwd(dbias.reshape(N_Q_HEADS, N_REL), onehot).T
    dproj = _assemble_dproj(dq, dkv, dps, mix)
    d_win = _matmul(h1, dproj, "tn", "d_win", m=D_MODEL, n=D_IN, k=SEQ, tm=D_MODEL, tn=640,
                    out=((N_CHIPS, D_MODEL, 640), _stacked(D_MODEL, 640, lambda i, j: (j, 0, 0))))
    token = yield ("grads_b", d_win)
    dh1 = _matmul_nt_shards(dproj, win_st, "d_h1", m=SEQ, n=D_MODEL, tm=1024, tn=D_MODEL,
                            a_spec=pl.BlockSpec((1024, D_IN), lambda i, j: (i, 0)),
                            a_piece=lambda ref, s: ref[:, s * 640:(s + 1) * 640])
    grad_x, d_g1 = _first_bwd(x, dx2, dh1, tied(sm["norm_mix_pre"], token))

    grads = {
        "norm_mix_pre": d_g1, "norm_mix_post": d_g2, "norm_ffn_pre": d_g3, "norm_ffn_post": d_g4,
        "w_in": d_win, "rel_bias": d_relb, "sinks": dsink[:, 0].reshape(1, N_Q_HEADS),
        "rwkv_shift_mix": d_mix, "w0": d_w0, "w_decay_up": d_wdu[:LORA_DECAY], "a0": d_a0,
        "w_iclr_up": d_wiu[LORA_DECAY:], "w_gate_up": d_wgu, "k_k": d_kk, "k_a": d_ka,
        "r_k": d_rk.reshape(1, N_Q_HEADS, HEAD_DIM), "ln_x_g": d_lng, "ln_x_b": d_lnb,
        "w_out": d_wout, "w_ffn_up": d_wup, "conv_w": d_convw, "conv_b": d_convb, "w_ffn_down": d_wdown,
    }
    return loss, grad_x, grads


def _place():
    x, y, c = lax.axis_index("x"), lax.axis_index("y"), lax.axis_index("c")
    chips = [(1 - x, y), (x, 1 - y), (1 - x, 1 - y)]
    return x, y, c, chips


def _remote(src, dst, sems, idx, to):
    return pltpu.make_async_remote_copy(src_ref=src, dst_ref=dst, send_sem=sems[0].at[idx], recv_sem=sems[1].at[idx],
                                        device_id=to, device_id_type=MESH)


ROW_ALIGN = 16


def _half(c, rows):
    return pl.ds(pl.multiple_of(c * (rows // 2), ROW_ALIGN), rows // 2)


def _gather_weights(big, small):
    nb, ns = len(big), len(small)

    def body(*refs):
        ins, outs = refs[:nb + ns], refs[nb + ns:2 * (nb + ns)]
        ici, d2d, sml, loc = refs[2 * (nb + ns):2 * (nb + ns) + 2], refs[-5:-3], refs[-3:-1], refs[-1]
        x, y, c, chips = _place()
        me = 2 * x + y
        sib = (x, y, 1 - c)
        local = [pltpu.make_async_copy(ins[a], outs[a].at[me], loc.at[a]) for a in range(nb + ns)]
        for cp in local:
            cp.start()
        sends = []
        for a in range(nb):
            rows = _half(c, big[a].shape[0])
            for kk, chip in enumerate(chips):
                sends.append(_remote(ins[a].at[rows], outs[a].at[me, rows], ici, a * 3 + kk, (*chip, c)))
        for a in range(ns):
            for kk, chip in enumerate(chips):
                sends.append(_remote(ins[nb + a], outs[nb + a].at[me], sml, a * 3 + kk, (*chip, c)))
        for cp in sends:
            cp.start()
        passed = []
        for a in range(nb):
            rows = _half(c, big[a].shape[0])
            for kk, (px, py) in enumerate(chips):
                got = outs[a].at[2 * px + py, rows]
                _remote(got, got, ici, a * 3 + kk, sib).wait_recv()
                fwd = _remote(got, got, d2d, a * 3 + kk, sib)
                fwd.start()
                passed.append(fwd)
        for a in range(nb):
            other = _half(1 - c, big[a].shape[0])
            for kk, (px, py) in enumerate(chips):
                land = outs[a].at[2 * px + py, other]
                _remote(land, land, d2d, a * 3 + kk, sib).wait_recv()
        for a in range(ns):
            for kk, (px, py) in enumerate(chips):
                land = outs[nb + a].at[2 * px + py]
                _remote(land, land, sml, a * 3 + kk, sib).wait_recv()
        for cp in sends + passed:
            cp.wait_send()
        for cp in local:
            cp.wait()

    arrs = list(big) + list(small)
    in_vmem = pl.BlockSpec(memory_space=pltpu.VMEM)
    return pl.pallas_call(
        body, name="gather_weights",
        in_specs=[in_vmem] * len(arrs), out_specs=[in_vmem] * len(arrs),
        out_shape=[jax.ShapeDtypeStruct((N_CHIPS,) + t.shape, t.dtype) for t in arrs],
        scratch_shapes=[pltpu.SemaphoreType.DMA((3 * nb,)), pltpu.SemaphoreType.DMA((3 * nb,)),
                        pltpu.SemaphoreType.DMA((3 * nb,)), pltpu.SemaphoreType.DMA((3 * nb,)),
                        pltpu.SemaphoreType.DMA((3 * ns,)), pltpu.SemaphoreType.DMA((3 * ns,)),
                        pltpu.SemaphoreType.DMA((nb + ns,))],
        compiler_params=pltpu.CompilerParams(has_side_effects=True, vmem_limit_bytes=VMEM_LIMIT),
    )(*arrs)


HBM = pl.BlockSpec(memory_space=pltpu.HBM)
SEM = pl.BlockSpec(memory_space=pltpu.SEMAPHORE)
EFFECT = pltpu.SideEffectType.DATAFLOW_SIDE_EFFECTING


def _copies_start(name, bufs, plan, n, partners=None):
    nb = len(bufs)

    def body(*refs):
        ins, sems, token = refs[:nb], refs[nb:nb + 2 * n], refs[-1]
        if partners is not None:
            barrier = pltpu.get_barrier_semaphore()
            peers = partners[1]()
            for peer in peers:
                pl.semaphore_signal(barrier, inc=1, device_id=peer, device_id_type=MESH)
            pl.semaphore_wait(barrier, len(peers))
        for kk, (src, dst, dev) in enumerate(plan(ins)):
            pltpu.make_async_remote_copy(src_ref=src, dst_ref=dst, send_sem=sems[2 * kk], recv_sem=sems[2 * kk + 1],
                                         device_id=dev, device_id_type=MESH).start()
        token[...] = jnp.zeros_like(token)

    outs = pl.pallas_call(
        body, name=name,
        out_shape=tuple([pltpu.SemaphoreType.DMA(())] * (2 * n) + [pltpu.HBM(t.shape, t.dtype) for t in bufs]
                        + [jax.ShapeDtypeStruct((8, LANES), F32)]),
        in_specs=[HBM] * nb,
        out_specs=tuple([SEM] * (2 * n) + [HBM] * nb + [pl.BlockSpec(memory_space=pltpu.VMEM)]),
        input_output_aliases={t: 2 * n + t for t in range(nb)},
        compiler_params=pltpu.CompilerParams(has_side_effects=EFFECT,
                                             collective_id=None if partners is None else partners[0]),
    )(*[pltpu.with_memory_space_constraint(t, pltpu.HBM) for t in bufs])
    return outs[:2 * n], outs[2 * n:2 * n + nb], outs[-1]


def _copies_wait(name, sems, bufs, plan, n, after):
    nb = len(bufs)
    after = list(after) if isinstance(after, (list, tuple)) else [after]

    def body(*refs):
        ins, sem_refs = refs[:nb], refs[nb:nb + 2 * n]
        for kk, (src, dst, dev) in enumerate(plan(ins)):
            cp = pltpu.make_async_remote_copy(src_ref=src, dst_ref=dst, send_sem=sem_refs[2 * kk],
                                              recv_sem=sem_refs[2 * kk + 1], device_id=dev, device_id_type=MESH)
            cp.wait_send()
            cp.wait_recv()

    return pl.pallas_call(
        body, name=name,
        out_shape=tuple(pltpu.HBM(t.shape, t.dtype) for t in bufs),
        in_specs=[HBM] * nb + [SEM] * (2 * n) + [ANY] * len(after),
        out_specs=tuple([HBM] * nb),
        input_output_aliases={t: t for t in range(nb)},
        compiler_params=pltpu.CompilerParams(has_side_effects=EFFECT),
    )(*bufs, *sems, *after)


def _plan_gather(n_w):
    def plan(refs):
        x, y, c, chips = _place()
        me = 2 * x + y
        return [(refs[a], refs[n_w + a].at[me], (*chip, c)) for a in range(n_w) for chip in chips + [(x, y)]]
    return plan


def _plan_pair_halves(n_g, rows):
    def plan(refs):
        x, y, c, _ = _place()
        return [(refs[a].at[:, _half(1 - c, rows[a])], refs[n_g + a], (x, y, 1 - c)) for a in range(n_g)]
    return plan


def _plan_chip_parts(n_g):
    def plan(refs):
        x, y, c, chips = _place()
        me = 2 * x + y
        return [(refs[a].at[2 * px + py], refs[n_g + a].at[me], (px, py, c))
                for a in range(n_g) for (px, py) in chips]
    return plan


def _plan_pair_fill(n_g, rows):
    def plan(refs):
        x, y, c, _ = _place()
        return [(refs[a].at[_half(c, rows[a])], refs[a].at[_half(c, rows[a])], (x, y, 1 - c)) for a in range(n_g)]
    return plan


def _pair_add(g, got, name):
    _, rows, cols = g.shape
    hr = rows // 2
    tr = min(hr, 512)
    nb = hr // tr

    def body(g_ref, got_ref, p_ref, own_ref):
        val = (g_ref[...] + got_ref[...]).astype(BF16)
        p_ref[...] = val

        @pl.when(pl.program_id(1) == 2 * lax.axis_index("x") + lax.axis_index("y"))
        def _():
            own_ref[...] = val

    def mine(i, s):
        return (2 * lax.axis_index("x") + lax.axis_index("y"), i, 0)

    return pl.pallas_call(
        body, name=name, grid=(nb, N_CHIPS),
        in_specs=[pl.BlockSpec((None, tr, cols), lambda i, s: (s, lax.axis_index("c") * nb + i, 0)),
                  pl.BlockSpec((None, tr, cols), lambda i, s: (s, i, 0))],
        out_specs=[pl.BlockSpec((None, tr, cols), lambda i, s: (s, i, 0)), pl.BlockSpec((None, tr, cols), mine)],
        out_shape=[jax.ShapeDtypeStruct((N_CHIPS, hr, cols), BF16)] * 2,
        compiler_params=_cp(("parallel", "arbitrary")),
    )(g, got)


def _chip_sum(parts, name):
    _, hr, cols = parts.shape
    tr = min(hr, 256)
    nb = hr // tr

    def body(t_ref, o_ref):
        part = [t_ref[s].astype(F32) for s in range(N_CHIPS)]
        o_ref[...] = ((part[0] + part[1]) + part[2]) + part[3]

    return pl.pallas_call(
        body, name=name, grid=(nb,),
        in_specs=[pl.BlockSpec((N_CHIPS, tr, cols), lambda i: (0, i, 0))],
        out_specs=pl.BlockSpec((tr, cols), lambda i: (lax.axis_index("c") * nb + i, 0)),
        out_shape=jax.ShapeDtypeStruct((2 * hr, cols), F32),
        compiler_params=_cp(("parallel",)),
    )(parts)


class _Reduction:
    def __init__(self, tag, rows, first_id):
        self.tag, self.n, self.rows, self.first_id = tag, len(rows), rows, first_id
        self.plans = (_plan_pair_halves(self.n, rows), _plan_chip_parts(self.n), _plan_pair_fill(self.n, rows))
        self.flight = None

    def _name(self, what):
        return f"grad_{self.tag}_{what}"

    @staticmethod
    def _sibling():
        x, y, c, _ = _place()
        return [(x, y, 1 - c)]

    @staticmethod
    def _same_core_elsewhere():
        x, y, c, chips = _place()
        return [(*chip, c) for chip in chips]

    def start(self, gs):
        gots = [lax.empty((N_CHIPS, t.shape[1] // 2, t.shape[2]), F32) for t in gs]
        self.flight = _copies_start(self._name("pair_start"), list(gs) + gots, self.plans[0], self.n,
                                    (self.first_id, self._sibling))
        return self.flight[2]

    def after_pair(self, after):
        sems, bufs, _ = self.flight
        out = _copies_wait(self._name("pair_wait"), sems, bufs, self.plans[0], self.n, after)
        sums = [_pair_add(g, got, self._name(f"pair_add_{i}"))
                for i, (g, got) in enumerate(zip(out[:self.n], out[self.n:]))]
        self.flight = _copies_start(self._name("chip_start"), [p for p, _ in sums] + [own for _, own in sums],
                                    self.plans[1], 3 * self.n, (self.first_id + 1, self._same_core_elsewhere))
        return self.flight[2]

    def after_chips(self, after):
        sems, bufs, _ = self.flight
        out = _copies_wait(self._name("chip_wait"), sems, bufs, self.plans[1], 3 * self.n, after)
        fulls = [_chip_sum(t, self._name(f"chip_sum_{i}")) for i, t in enumerate(out[self.n:])]
        self.flight = _copies_start(self._name("fill_start"), fulls, self.plans[2], self.n,
                                    (self.first_id + 2, self._sibling))
        return self.flight[2]

    def finish(self, after):
        sems, bufs, _ = self.flight
        return _copies_wait(self._name("fill_wait"), sems, bufs, self.plans[2], self.n, after)


def _adamw_math(w, g, m, v):
    nm = ADAM_B1 * m + (1.0 - ADAM_B1) * g
    nv = ADAM_B2 * v + (1.0 - ADAM_B2) * (g * g)
    m_hat = nm / (1.0 - ADAM_B1 ** ADAM_STEP)
    v_hat = nv / (1.0 - ADAM_B2 ** ADAM_STEP)
    return -ADAM_LR * (m_hat / (jnp.sqrt(v_hat) + ADAM_EPS) + ADAM_WD * w), nm, nv


def _adamw(w, g, m, v, name, tr):
    r, cdim = w.shape

    def body(w_ref, g_ref, m_ref, v_ref, d_ref, nm_ref, nv_ref):
        d_ref[...], nm_ref[...], nv_ref[...] = _adamw_math(w_ref[...], g_ref[...], m_ref[...], v_ref[...])

    return pl.pallas_call(
        body, name=name, grid=(r // tr,), in_specs=[_rows(tr, cdim)] * 4, out_specs=[_rows(tr, cdim)] * 3,
        out_shape=[jax.ShapeDtypeStruct((r, cdim), F32)] * 3, compiler_params=_cp(("parallel",)),
    )(w, g, m, v)


def _adamw_small(w, parts, m, v, shapes):
    n_rows = w.shape[0]

    def scatter(src, outs):
        row = 0
        for (rows, cols), out in zip(shapes, outs):
            if cols == LANES:
                out[...] = src[row:row + rows, :]
            elif cols > LANES:
                per = cols // LANES
                for r in range(rows):
                    for cb in range(per):
                        out[r:r + 1, cb * LANES:(cb + 1) * LANES] = src[row + r * per + cb:row + r * per + cb + 1, :]
            else:
                per = LANES // cols
                for r in range(rows):
                    out[r:r + 1, :] = src[row + r // per:row + r // per + 1, (r % per) * cols:(r % per + 1) * cols]
            row += -(-rows * cols // LANES)

    def body(w_ref, p_ref, m_ref, v_ref, *rest):
        outs, scr = rest[:-4], rest[-4:]
        g = p_ref[0]
        for dev in range(1, N_DEV):
            g = g + p_ref[dev]
        scr[3][...] = g
        scr[0][...], scr[1][...], scr[2][...] = _adamw_math(w_ref[...], g, m_ref[...], v_ref[...])
        n = len(shapes)
        for kind in range(4):
            scatter(scr[kind], outs[kind * n:(kind + 1) * n])

    outs = pl.pallas_call(
        body, name="adamw_small", grid=(1,),
        in_specs=[_const(w.shape), _const(parts.shape), _const(w.shape), _const(w.shape)],
        out_specs=[_const(s) for s in shapes] * 4, out_shape=[jax.ShapeDtypeStruct(s, F32) for s in shapes] * 4,
        scratch_shapes=[pltpu.VMEM((n_rows, LANES), F32)] * 4,
        compiler_params=_cp(("arbitrary",)),
    )(w, parts, m, v)
    n = len(shapes)
    return [outs[kind * n:(kind + 1) * n] for kind in range(4)]


REPLICATED = (("norm_mix_pre", 1024), ("norm_mix_post", 1024), ("norm_ffn_pre", 1024), ("norm_ffn_post", 1024),
              ("rel_bias", 256), ("sinks", 8), ("rwkv_shift_mix", 1792), ("w0", 512), ("a0", 512), ("k_k", 512),
              ("k_a", 512), ("r_k", 512), ("ln_x_g", 512), ("ln_x_b", 512), ("conv_b", 8192))
SMALL_SHARDED = (("w_decay_up", LORA_DECAY, D_RWKV), ("w_iclr_up", LORA_ICLR, D_RWKV),
                 ("w_gate_up", LORA_GATE, D_RWKV), ("conv_w", 3, 2 * D_FF))
BIG = (("w_in", D_MODEL, 640), ("w_out", 256, D_MODEL), ("w_ffn_up", D_MODEL, 2048), ("w_ffn_down", 1024, D_MODEL))
PACK_ALIGN = 8 * LANES


def _pack(pieces):
    flat = []
    for t in pieces:
        t = t.reshape(-1)
        pad = (-t.shape[0]) % LANES
        flat.append(jnp.pad(t, (0, pad)) if pad else t)
    flat = jnp.concatenate(flat)
    pad = (-flat.shape[0]) % PACK_ALIGN
    return jnp.pad(flat, (0, pad)).reshape(-1, LANES)


def kernel(x, norm_mix_pre, norm_mix_post, norm_ffn_pre, norm_ffn_post, w_in, rel_bias, sinks, rwkv_shift_mix, w0, w_decay_up, a0, w_iclr_up, w_gate_up, k_k, k_a, r_k, ln_x_g, ln_x_b, w_out, w_ffn_up, conv_w, conv_b, w_ffn_down, loss_target, m_norm_mix_pre, m_norm_mix_post, m_norm_ffn_pre, m_norm_ffn_post, m_w_in, m_rel_bias, m_sinks, m_rwkv_shift_mix, m_w0, m_w_decay_up, m_a0, m_w_iclr_up, m_w_gate_up, m_k_k, m_k_a, m_r_k, m_ln_x_g, m_ln_x_b, m_w_out, m_w_ffn_up, m_conv_w, m_conv_b, m_w_ffn_down, v_norm_mix_pre, v_norm_mix_post, v_norm_ffn_pre, v_norm_ffn_post, v_w_in, v_rel_bias, v_sinks, v_rwkv_shift_mix, v_w0, v_w_decay_up, v_a0, v_w_iclr_up, v_w_gate_up, v_k_k, v_k_a, v_r_k, v_ln_x_g, v_ln_x_b, v_w_out, v_w_ffn_up, v_conv_w, v_conv_b, v_w_ffn_down):
    given = dict(locals())
    names = [n for n, _ in REPLICATED] + [n for n, _, _ in SMALL_SHARDED] + [n for n, _, _ in BIG]
    order = ["norm_mix_pre", "norm_mix_post", "norm_ffn_pre", "norm_ffn_post", "w_in", "rel_bias", "sinks",
             "rwkv_shift_mix", "w0", "w_decay_up", "a0", "w_iclr_up", "w_gate_up", "k_k", "k_a", "r_k", "ln_x_g",
             "ln_x_b", "w_out", "w_ffn_up", "conv_w", "conv_b", "w_ffn_down"]
    assert sorted(names) == sorted(order)

    big_sh = {n: given[n].reshape(a, b).astype(BF16) for n, a, b in BIG}
    small_sh = [given[n].reshape(r, c // N_CHIPS) for n, r, c in SMALL_SHARDED]
    gathered = _gather_weights([big_sh["w_in"]], small_sh)
    rest = ("w_out", "w_ffn_up", "w_ffn_down")
    win_st, rest_sh = lax.optimization_barrier((gathered[0], [big_sh[n] for n in rest]))
    sm = {n: given[n] for n, _ in REPLICATED}
    sm["r_k"] = r_k.reshape(N_Q_HEADS, HEAD_DIM)
    for (n, r, c), st in zip(SMALL_SHARDED, gathered[1:]):
        sm[n] = st.transpose(1, 0, 2).reshape(r, c)

    lands = [lax.empty((N_CHIPS,) + t.shape, BF16) for t in rest_sh]
    plan_w = _plan_gather(len(rest))
    n_w = N_CHIPS * len(rest)
    w_sems, w_bufs, token = _copies_start("gather_rest_start", rest_sh + lands, plan_w, n_w)
    sm["norm_mix_pre"] = norm_mix_pre + token[0:1, 0:1]

    def on_rest_weights(after):
        out = _copies_wait("gather_rest_wait", w_sems, w_bufs, plan_w, n_w, after)
        wout_st, wup_st, wdown_st = out[3:]
        return wout_st.reshape(D_MODEL, D_MODEL), wup_st, wdown_st.reshape(D_FF, D_MODEL)

    red_a = _Reduction("a", (1024, D_MODEL, 256), first_id=0)
    red_b = _Reduction("b", (D_MODEL,), first_id=3)

    def on_grads_a(gs):
        d_wdown, d_wup, d_wout = gs
        return red_a.start([d_wdown.reshape(N_CHIPS, 1024, D_MODEL), d_wup, d_wout.reshape(N_CHIPS, 256, D_MODEL)])

    handlers = {"rest_weights": on_rest_weights, "grads_a": on_grads_a, "seam_1": red_a.after_pair,
                "seam_2": red_a.after_chips, "grads_b": lambda g: red_b.start([g])}
    steps = _local_step(x[0], loss_target[0], sm, win_st)
    kind, payload = next(steps)
    while True:
        try:
            kind, payload = steps.send(handlers[kind](payload))
        except StopIteration as done:
            loss, grad_x, grads = done.value
            break

    small_names = [n for n, _ in REPLICATED] + [n for n, _, _ in SMALL_SHARDED]

    def shard_cols(t, s):
        return t[:, s * (t.shape[1] // N_CHIPS):(s + 1) * (t.shape[1] // N_CHIPS)]

    for_chip = jnp.stack([_pack([loss[0]] + [grads[n] for n, _ in REPLICATED]
                                + [shard_cols(grads[n], s) for n, _, _ in SMALL_SHARDED]) for s in range(N_CHIPS)])
    land = lax.empty((N_DEV,) + for_chip.shape[1:], F32)

    def plan_small(refs):
        x, y, c, _ = _place()
        out = []
        for rel in range(N_DEV):
            px, py, pc = x ^ (rel >> 2), y ^ ((rel >> 1) & 1), c ^ (rel & 1)
            out.append((refs[0].at[2 * px + py], refs[1].at[4 * x + 2 * y + c], (px, py, pc)))
        return out

    s_sems, s_bufs, s_token = _copies_start("grad_small_start", [for_chip, land], plan_small, N_DEV)

    red_b.after_pair([grad_x, s_token])
    g_out = {}
    g_out["w_ffn_down"], g_out["w_ffn_up"], g_out["w_out"] = red_a.finish(grad_x)

    delta, new_m, new_v = {}, {}, {}

    def update(n, a, b):
        delta[n], new_m[n], new_v[n] = _adamw(given[n].reshape(a, b), g_out[n], given["m_" + n].reshape(a, b),
                                              given["v_" + n].reshape(a, b), "adamw_" + n, 256)

    for n, a, b in BIG[1:]:
        update(n, a, b)
    done = [delta[n] for n, _, _ in BIG[1:]]
    red_b.after_chips(done)
    parts = _copies_wait("grad_small_wait", s_sems, s_bufs, plan_small, N_DEV, done)[1]
    no_param = jnp.zeros((LANES,), F32)
    packs = [_pack([no_param] + [given[pre + n] for n in small_names]) for pre in ("", "m_", "v_")]

    def piece_shape(n):
        shape = given[n].shape
        rows, cols = int(np.prod(shape[:-1])), shape[-1]
        whole = cols % LANES == 0 or (LANES % cols == 0 and (rows * cols) % LANES == 0 and cols >= HEAD_DIM)
        return (rows, cols) if whole else (-(-rows * cols // LANES), LANES)

    shapes = [(1, LANES)] + [piece_shape(n) for n in small_names]
    upd = _adamw_small(packs[0], parts, packs[1], packs[2], shapes)
    loss = upd[3][0][0, 0]
    for i, n in enumerate(small_names):
        shape = given[n].shape
        size = int(np.prod(shape))
        delta[n], new_m[n], new_v[n], g_out[n] = (u[1 + i].reshape(-1)[:size].reshape(shape) for u in upd)
    g_out["w_in"], = red_b.finish(upd[0][0])
    update(*BIG[0])

    def shaped(d):
        return [d[n].reshape(given[n].shape) for n in order]

    return (loss, grad_x.reshape(x.shape), *shaped(g_out), *shaped(delta), *shaped(new_m), *shaped(new_v))
```

```python
import math

import numpy as np
import jax
import jax.numpy as jnp
from jax import lax
from jax.experimental import pallas as pl
from jax.experimental.pallas import tpu as pltpu

F32 = jnp.float32
BF16 = jnp.bfloat16
MESH = pl.DeviceIdType.MESH

SEQ = 2048
D_MODEL = 1024
HEAD_DIM = 64
D_ATTN = 512
D_RWKV = 512
D_KV = 128
N_Q_HEADS = 8
N_KV_HEADS = 2
Q_PER_KV = 4
BLOCK = 128
N_BUCKETS = 32
MAX_DISTANCE = 128
LORA_DECAY = 64
LORA_ICLR = 64
LORA_GATE = 128
RWKV_COLS = 3 * D_RWKV + LORA_DECAY + LORA_ICLR + LORA_GATE
P_OFF = D_ATTN + 2 * D_KV
D_IN = P_OFF + RWKV_COLS
D_FF = 4096
NORM_EPS = 1e-6
GN_EPS = 64e-5
NEG_INF = -1e30
N_CHIPS = 4
N_DEV = 8
HEAD_SHIFT = HEAD_DIM.bit_length() - 1
BLOCK_SHIFT = BLOCK.bit_length() - 1

ADAM_LR = 0.001
ADAM_B1 = 0.9
ADAM_B2 = 0.999
ADAM_EPS = 1e-08
ADAM_WD = 0.01
ADAM_STEP = 10

VMEM_LIMIT = 52 * 1024 * 1024
LANES = 128


def _cp(sem=None, vmem=VMEM_LIMIT):
    kw = dict(vmem_limit_bytes=vmem)
    if sem is not None:
        kw["dimension_semantics"] = sem
    return pltpu.CompilerParams(**kw)


def _rows(tr, nc):
    return pl.BlockSpec((tr, nc), lambda i: (i, 0))


def _const(shape):
    return pl.BlockSpec(shape, lambda *_: (0,) * len(shape))


ANY = pl.BlockSpec(memory_space=pl.ANY)


def _split(x, n):
    parts = []
    for _ in range(n - 1):
        h = x.astype(BF16)
        parts.append(h)
        x = x - h.astype(F32)
    parts.append(x.astype(BF16))
    return parts


NN = (((1,), (0,)), ((), ()))
NT = (((1,), (1,)), ((), ()))
TN = (((0,), (0,)), ((), ()))


def _dot(a, b, dn=NN):
    return lax.dot_general(a, b, dn, preferred_element_type=F32)


def _dot_ind(x, ind_bf16, n=3):
    acc = None
    for part in _split(x, n):
        t = _dot(part, ind_bf16)
        acc = t if acc is None else acc + t
    return acc


def _head_ones(n):
    r = lax.broadcasted_iota(jnp.int32, (n, n), 0) >> HEAD_SHIFT
    c = lax.broadcasted_iota(jnp.int32, (n, n), 1) >> HEAD_SHIFT
    return jnp.where(r == c, 1.0, 0.0).astype(BF16)


def _matmul(a, b, mode, name, *, m, n, k, tm, tn, a_spec=None, b_spec=None, out=None):
    dn = {"nn": NN, "nt": NT, "tn": TN}[mode]

    def body(a_ref, b_ref, o_ref):
        o_ref[...] = _dot(a_ref[...], b_ref[...], dn)

    if a_spec is None:
        a_spec = pl.BlockSpec((k, tm), lambda i, j: (0, i)) if mode == "tn" else pl.BlockSpec((tm, k), lambda i, j: (i, 0))
    if b_spec is None:
        b_spec = pl.BlockSpec((tn, k), lambda i, j: (j, 0)) if mode == "nt" else pl.BlockSpec((k, tn), lambda i, j: (0, j))
    return pl.pallas_call(
        body, name=name, grid=(m // tm, n // tn),
        in_specs=[a_spec, b_spec],
        out_specs=pl.BlockSpec((tm, tn), lambda i, j: (i, j)) if out is None else out[1],
        out_shape=jax.ShapeDtypeStruct((m, n) if out is None else out[0], F32),
        compiler_params=_cp(("parallel", "parallel")),
    )(a, b)


def _matmul_nt_shards(a, b_st, name, *, m, n, tm, tn, a_spec, a_piece):
    ks = b_st.shape[2]

    def body(a_ref, b_ref, o_ref):
        acc = _dot(a_piece(a_ref, 0), b_ref[0], NT)
        for s in range(1, N_CHIPS):
            acc = acc + _dot(a_piece(a_ref, s), b_ref[s], NT)
        o_ref[...] = acc

    return pl.pallas_call(
        body, name=name, grid=(m // tm, n // tn),
        in_specs=[a_spec, pl.BlockSpec((N_CHIPS, tn, ks), lambda i, j: (0, j, 0))],
        out_specs=pl.BlockSpec((tm, tn), lambda i, j: (i, j)),
        out_shape=jax.ShapeDtypeStruct((m, n), F32),
        compiler_params=_cp(("parallel", "parallel")),
    )(a, b_st)


def _rstd(x):
    return lax.rsqrt(jnp.mean(x * x, axis=-1, keepdims=True) + NORM_EPS)


def _rms_bwd(x, r, g, dy):
    gy = dy * g
    return r * gy - x * ((r * r * r) * (jnp.sum(x * gy, axis=-1, keepdims=True) / x.shape[-1]))


TR = 256
TRN = 512


def _norm_cast(x, g, name):
    def body(x_ref, g_ref, h_ref):
        x = x_ref[...]
        h_ref[...] = (x * _rstd(x) * g_ref[...]).astype(BF16)

    return pl.pallas_call(
        body, name=name, grid=(SEQ // TRN,),
        in_specs=[_rows(TRN, D_MODEL), _const((1, D_MODEL))],
        out_specs=_rows(TRN, D_MODEL),
        out_shape=jax.ShapeDtypeStruct((SEQ, D_MODEL), BF16),
        compiler_params=_cp(("parallel",)),
    )(x, g)


def _mix_norm(x, mix, g2, g3):
    def body(x_ref, mix_ref, g2_ref, g3_ref, x2_ref, h3_ref):
        mixv = mix_ref[...]
        x2 = x_ref[...] + mixv * _rstd(mixv) * g2_ref[...]
        x2_ref[...] = x2
        h3_ref[...] = (x2 * _rstd(x2) * g3_ref[...]).astype(BF16)

    return pl.pallas_call(
        body, name="mix_norm", grid=(SEQ // TRN,),
        in_specs=[_rows(TRN, D_MODEL), _rows(TRN, D_MODEL), _const((1, D_MODEL)), _const((1, D_MODEL))],
        out_specs=[_rows(TRN, D_MODEL), _rows(TRN, D_MODEL)],
        out_shape=[jax.ShapeDtypeStruct((SEQ, D_MODEL), F32), jax.ShapeDtypeStruct((SEQ, D_MODEL), BF16)],
        compiler_params=_cp(("parallel",)),
    )(x, mix, g2, g3)


def _loss_head(x2, f, g4, target):
    def body(x2_ref, f_ref, g4_ref, t_ref, loss_ref, dy_ref, df_ref, dg_ref):
        i = pl.program_id(0)
        f = f_ref[...]
        g4 = g4_ref[...]
        r = _rstd(f)
        e = x2_ref[...] + f * r * g4 - t_ref[...]
        dy = e * (1.0 / D_MODEL)
        dy_ref[...] = dy
        df_ref[...] = _rms_bwd(f, r, g4, dy).astype(BF16)
        part = 0.5 * jnp.sum(jnp.sum(e * e, axis=-1, keepdims=True), axis=0, keepdims=True) * (1.0 / D_MODEL)
        dg = jnp.sum(dy * f * r, axis=0, keepdims=True)

        @pl.when(i == 0)
        def _():
            loss_ref[...] = jnp.zeros_like(loss_ref)
            dg_ref[...] = jnp.zeros_like(dg_ref)

        loss_ref[...] += jnp.broadcast_to(part, loss_ref.shape)
        dg_ref[...] += dg

    return pl.pallas_call(
        body, name="loss_head", grid=(SEQ // TRN,),
        in_specs=[_rows(TRN, D_MODEL), _rows(TRN, D_MODEL), _const((1, D_MODEL)), _rows(TRN, D_MODEL)],
        out_specs=[_const((8, LANES)), _rows(TRN, D_MODEL), _rows(TRN, D_MODEL), _const((1, D_MODEL))],
        out_shape=[jax.ShapeDtypeStruct((8, LANES), F32), jax.ShapeDtypeStruct((SEQ, D_MODEL), F32),
                   jax.ShapeDtypeStruct((SEQ, D_MODEL), BF16), jax.ShapeDtypeStruct((1, D_MODEL), F32)],
        compiler_params=_cp(("arbitrary",)),
    )(x2, f, g4, target)


def _mid_bwd(x2, mix, dy, dh3, g2, g3):
    def body(x2_ref, mix_ref, dy_ref, dh3_ref, g2_ref, g3_ref, dx2_ref, dmix_ref, dg2_ref, dg3_ref):
        i = pl.program_id(0)
        x2 = x2_ref[...]
        mixv = mix_ref[...]
        dh3 = dh3_ref[...]
        r3 = _rstd(x2)
        dx2 = dy_ref[...] + _rms_bwd(x2, r3, g3_ref[...], dh3)
        dx2_ref[...] = dx2
        r2 = _rstd(mixv)
        dmix_ref[...] = _rms_bwd(mixv, r2, g2_ref[...], dx2).astype(BF16)

        @pl.when(i == 0)
        def _():
            dg2_ref[...] = jnp.zeros_like(dg2_ref)
            dg3_ref[...] = jnp.zeros_like(dg3_ref)

        dg3_ref[...] += jnp.sum(dh3 * x2 * r3, axis=0, keepdims=True)
        dg2_ref[...] += jnp.sum(dx2 * mixv * r2, axis=0, keepdims=True)

    return pl.pallas_call(
        body, name="mid_bwd", grid=(SEQ // TRN,),
        in_specs=[_rows(TRN, D_MODEL)] * 4 + [_const((1, D_MODEL))] * 2,
        out_specs=[_rows(TRN, D_MODEL), _rows(TRN, D_MODEL), _const((1, D_MODEL)), _const((1, D_MODEL))],
        out_shape=[jax.ShapeDtypeStruct((SEQ, D_MODEL), F32), jax.ShapeDtypeStruct((SEQ, D_MODEL), BF16),
                   jax.ShapeDtypeStruct((1, D_MODEL), F32), jax.ShapeDtypeStruct((1, D_MODEL), F32)],
        compiler_params=_cp(("arbitrary",)),
    )(x2, mix, dy, dh3, g2, g3)


def _first_bwd(x, dx2, dh1, g1):
    def body(x_ref, dx2_ref, dh1_ref, g1_ref, dx_ref, dg1_ref):
        i = pl.program_id(0)
        x = x_ref[...]
        dh1 = dh1_ref[...]
        r = _rstd(x)
        dx_ref[...] = dx2_ref[...] + _rms_bwd(x, r, g1_ref[...], dh1)

        @pl.when(i == 0)
        def _():
            dg1_ref[...] = jnp.zeros_like(dg1_ref)

        dg1_ref[...] += jnp.sum(dh1 * x * r, axis=0, keepdims=True)

    return pl.pallas_call(
        body, name="first_bwd", grid=(SEQ // TRN,),
        in_specs=[_rows(TRN, D_MODEL)] * 3 + [_const((1, D_MODEL))],
        out_specs=[_rows(TRN, D_MODEL), _const((1, D_MODEL))],
        out_shape=[jax.ShapeDtypeStruct((SEQ, D_MODEL), F32), jax.ShapeDtypeStruct((1, D_MODEL), F32)],
        compiler_params=_cp(("arbitrary",)),
    )(x, dx2, dh1, g1)


TC = 256
N_CB = D_FF // TC
GELU_C = math.sqrt(2.0 / math.pi)


def _shift_down(u, s):
    rolled = pltpu.roll(u, s, 0)
    row = lax.broadcasted_iota(jnp.int32, u.shape, 0)
    return jnp.where(row >= s, rolled, 0.0)


def _shift_up(u, s):
    n = u.shape[0]
    rolled = pltpu.roll(u, n - s, 0)
    row = lax.broadcasted_iota(jnp.int32, u.shape, 0)
    return jnp.where(row < n - s, rolled, 0.0)


def _conv3(u, w, b):
    return b + w[0:1] * _shift_down(u, 2) + w[1:2] * _shift_down(u, 1) + w[2:3] * u


def _gelu_and_grad(x):
    inner = GELU_C * (x + 0.044715 * (x * x * x))
    t = jnp.tanh(inner)
    gelu = 0.5 * x * (1.0 + t)
    dgelu = 0.5 * (1.0 + t) + 0.5 * x * (1.0 - t * t) * (GELU_C * (1.0 + 3 * 0.044715 * (x * x)))
    return gelu, dgelu


def _ffn_specs():
    col = lambda off: pl.BlockSpec((SEQ, TC), lambda *g: (0, g[-1] + off))
    w = lambda off: pl.BlockSpec((3, TC), lambda *g: (0, g[-1] + off))
    b = lambda off: pl.BlockSpec((1, TC), lambda *g: (0, g[-1] + off))
    return col, w, b


def _ffn_up_act(h3, wup_st, conv_w, conv_b):
    col, w, b = _ffn_specs()
    per_shard = wup_st.shape[2] // TC

    def body(h_ref, upg_ref, upv_ref, wg_ref, wv_ref, bg_ref, bv_ref, ug_ref, uv_ref, gate_ref, val_ref, act_ref):
        h = h_ref[...]
        ug = _dot(h, upg_ref[...])
        uv = _dot(h, upv_ref[...])
        ug_ref[...] = ug
        uv_ref[...] = uv
        gate = _conv3(ug, wg_ref[...], bg_ref[...])
        val = _conv3(uv, wv_ref[...], bv_ref[...])
        gate_ref[...] = gate
        val_ref[...] = val
        act_ref[...] = (_gelu_and_grad(gate)[0] * val).astype(BF16)

    return pl.pallas_call(
        body, name="ffn_up_act", grid=(N_CB,),
        in_specs=[_const((SEQ, D_MODEL)),
                  pl.BlockSpec((None, D_MODEL, TC), lambda j: (j // per_shard, 0, j % per_shard)),
                  pl.BlockSpec((None, D_MODEL, TC), lambda j: (2 + j // per_shard, 0, j % per_shard)),
                  w(0), w(N_CB), b(0), b(N_CB)],
        out_specs=[col(0)] * 5,
        out_shape=[jax.ShapeDtypeStruct((SEQ, D_FF), F32)] * 4 + [jax.ShapeDtypeStruct((SEQ, D_FF), BF16)],
        compiler_params=_cp(("parallel",)),
    )(h3, wup_st, wup_st, conv_w, conv_w, conv_b, conv_b)


def _ffn_act_bwd(u_gate, u_val, gate, val, df, wdown, conv_w):
    col, w, _ = _ffn_specs()
    both = lambda rows: pl.BlockSpec((2, rows, TC), lambda j: (0, 0, j))

    def body(ug_ref, uv_ref, gate_ref, val_ref, df_ref, wd_ref, wg_ref, wv_ref, du_ref, dw_ref, db_ref):
        da = _dot(df_ref[...], wd_ref[...], NT)
        gelu, dgelu = _gelu_and_grad(gate_ref[...])
        halves = ((da * val_ref[...] * dgelu, ug_ref, wg_ref[...]), (da * gelu, uv_ref, wv_ref[...]))
        for h, (duc, u_ref, wh) in enumerate(halves):
            uh = u_ref[...]
            up1, up2 = _shift_up(duc, 1), _shift_up(duc, 2)
            du_ref[h] = (wh[2:3] * duc + wh[1:2] * up1 + wh[0:1] * up2).astype(BF16)
            db_ref[h] = jnp.sum(duc, axis=0, keepdims=True)
            dw_ref[h] = jnp.concatenate(
                [jnp.sum(up2 * uh, axis=0, keepdims=True), jnp.sum(up1 * uh, axis=0, keepdims=True),
                 jnp.sum(duc * uh, axis=0, keepdims=True)], axis=0)

    return pl.pallas_call(
        body, name="ffn_act_bwd", grid=(N_CB,),
        in_specs=[col(0)] * 4 + [_const((SEQ, D_MODEL)), pl.BlockSpec((TC, D_MODEL), lambda j: (j, 0)), w(0), w(N_CB)],
        out_specs=[both(SEQ), both(3), both(1)],
        out_shape=[jax.ShapeDtypeStruct((2, SEQ, D_FF), BF16), jax.ShapeDtypeStruct((2, 3, D_FF), F32),
                   jax.ShapeDtypeStruct((2, 1, D_FF), F32)],
        compiler_params=_cp(("parallel",)),
    )(u_gate, u_val, gate, val, df, wdown, conv_w, conv_w)


def _t5_onehot():
    rel = (np.arange(BLOCK)[:, None] + BLOCK) - np.arange(2 * BLOCK)[None, :]
    n = np.maximum(rel, 0)
    max_exact = N_BUCKETS // 2
    large = max_exact + (np.log(np.maximum(n, 1).astype(np.float32) / np.float32(max_exact))
                         / np.float32(math.log(MAX_DISTANCE / max_exact))
                         * np.float32(N_BUCKETS - max_exact)).astype(np.int32)
    large = np.minimum(large, N_BUCKETS - 1)
    bucket = np.where(n < max_exact, n, large).reshape(-1)
    return (bucket[None, :] == np.arange(N_BUCKETS)[:, None]).astype(np.float32)


N_REL = BLOCK * 2 * BLOCK


def _bias_table(rel_bias_t, onehot):
    def body(rb_ref, oh_ref, o_ref):
        o_ref[...] = _dot_ind(rb_ref[...], oh_ref[...])

    return pl.pallas_call(
        body, name="bias_table", grid=(1,),
        in_specs=[_const((N_Q_HEADS, N_BUCKETS)), _const((N_BUCKETS, N_REL))],
        out_specs=_const((N_Q_HEADS, N_REL)),
        out_shape=jax.ShapeDtypeStruct((N_Q_HEADS, N_REL), F32),
        compiler_params=_cp(("arbitrary",)),
    )(rel_bias_t, onehot)


def _bias_table_bwd(dbias, onehot):
    def body(db_ref, oh_ref, o_ref):
        acc = None
        for part in _split(db_ref[...], 3):
            t = _dot(part, oh_ref[...], NT)
            acc = t if acc is None else acc + t
        o_ref[...] = acc

    return pl.pallas_call(
        body, name="bias_table_bwd", grid=(1,),
        in_specs=[_const((N_Q_HEADS, N_REL)), _const((N_BUCKETS, N_REL))],
        out_specs=_const((N_Q_HEADS, N_BUCKETS)),
        out_shape=jax.ShapeDtypeStruct((N_Q_HEADS, N_BUCKETS), F32),
        compiler_params=_cp(("arbitrary",)),
    )(dbias, onehot)


def _attn_pieces(n, q, kvp, kvc, bias_ref, sinks_ref, hk):
    qi = lax.broadcasted_iota(jnp.int32, (BLOCK, 2 * BLOCK), 0)
    kj = lax.broadcasted_iota(jnp.int32, (BLOCK, 2 * BLOCK), 1)
    rel = qi + BLOCK - kj
    first_key = jnp.where(n > 0, 0, BLOCK)
    ok = jnp.where(rel >= 0, jnp.where(rel < BLOCK, jnp.where(kj >= first_key, 1.0, 0.0), 0.0), 0.0)
    ok4 = jnp.concatenate([ok] * Q_PER_KV, axis=0) > 0.5
    c0 = hk * HEAD_DIM
    kcat = jnp.concatenate([kvp[:, c0:c0 + HEAD_DIM], kvc[:, c0:c0 + HEAD_DIM]], axis=0).astype(BF16)
    vcat = jnp.concatenate([kvp[:, D_KV + c0:D_KV + c0 + HEAD_DIM], kvc[:, D_KV + c0:D_KV + c0 + HEAD_DIM]],
                           axis=0).astype(BF16)
    q0 = hk * Q_PER_KV * HEAD_DIM
    qs = jnp.concatenate([q[:, q0 + g * HEAD_DIM:q0 + (g + 1) * HEAD_DIM] for g in range(Q_PER_KV)],
                         axis=0).astype(BF16)
    s = _dot(qs, kcat, NT) * (HEAD_DIM ** -0.5) + bias_ref[hk]
    s = jnp.where(ok4, s, NEG_INF)
    row = lax.broadcasted_iota(jnp.int32, (Q_PER_KV * BLOCK, 1), 0)
    sink = jnp.zeros((Q_PER_KV * BLOCK, 1), F32)
    for g in range(Q_PER_KV):
        sink = jnp.where((row >> BLOCK_SHIFT) == g, sinks_ref[hk * Q_PER_KV + g], sink)
    m = jnp.maximum(jnp.max(s, axis=-1, keepdims=True), sink)
    p = jnp.exp(s - m)
    es = jnp.exp(sink - m)
    inv = 1.0 / (jnp.sum(p, axis=-1, keepdims=True) + es)
    return qs, kcat, vcat, p * inv, es * inv


def _attn_in_specs():
    return [pl.BlockSpec((BLOCK, D_ATTN), lambda n: (n, 0)),
            pl.BlockSpec((BLOCK, 2 * D_KV), lambda n: (jnp.maximum(n - 1, 0), D_ATTN // (2 * D_KV))),
            pl.BlockSpec((BLOCK, 2 * D_KV), lambda n: (n, D_ATTN // (2 * D_KV))),
            _const((N_KV_HEADS, Q_PER_KV * BLOCK, 2 * BLOCK)),
            pl.BlockSpec(memory_space=pltpu.SMEM)]


def _unstack_heads(t):
    return jnp.concatenate([t[g * BLOCK:(g + 1) * BLOCK] for g in range(Q_PER_KV)], axis=1)


def _attn_fwd(proj, bias, sinks):
    def body(q_ref, kvp_ref, kvc_ref, bias_ref, sinks_ref, o_ref):
        n = pl.program_id(0)
        q, kvp, kvc = q_ref[...], kvp_ref[...], kvc_ref[...]
        outs = []
        for hk in range(N_KV_HEADS):
            _, _, vcat, probs, _ = _attn_pieces(n, q, kvp, kvc, bias_ref, sinks_ref, hk)
            outs.append(_unstack_heads(_dot(probs.astype(BF16), vcat)))
        o_ref[...] = jnp.concatenate(outs, axis=1)

    return pl.pallas_call(
        body, name="attn_fwd", grid=(SEQ // BLOCK,),
        in_specs=_attn_in_specs(),
        out_specs=pl.BlockSpec((BLOCK, D_ATTN), lambda n: (n, 0)),
        out_shape=jax.ShapeDtypeStruct((SEQ, D_ATTN), F32),
        compiler_params=_cp(("parallel",)),
    )(proj, proj, proj, bias, sinks)


def _attn_bwd(proj, bias, sinks, dcat):
    nb = SEQ // BLOCK

    def body(q_ref, kvp_ref, kvc_ref, bias_ref, sinks_ref, do_ref, dq_ref, dkv_ref, dbias_ref, dsink_ref, dsacc):
        n = pl.program_id(0)

        @pl.when(n == 0)
        def _():
            dkv_ref[...] = jnp.zeros_like(dkv_ref)
            dbias_ref[...] = jnp.zeros_like(dbias_ref)
            dsacc[...] = jnp.zeros_like(dsacc)

        q, kvp, kvc = q_ref[...], kvp_ref[...], kvc_ref[...]
        do_all = do_ref[...]
        dqs, dks, dvs = [], [], []
        for hk in range(N_KV_HEADS):
            qs, kcat, vcat, probs, psink = _attn_pieces(n, q, kvp, kvc, bias_ref, sinks_ref, hk)
            q0 = hk * Q_PER_KV * HEAD_DIM
            do = jnp.concatenate([do_all[:, q0 + g * HEAD_DIM:q0 + (g + 1) * HEAD_DIM] for g in range(Q_PER_KV)],
                                 axis=0).astype(BF16)
            dprobs = _dot(do, vcat, NT)
            dvs.append(_dot(probs.astype(BF16), do, TN))
            rowdot = jnp.sum(probs * dprobs, axis=-1, keepdims=True)
            ds = probs * (dprobs - rowdot)
            dsacc[hk] += -psink * rowdot
            dbias_ref[hk] += ds
            dsb = (ds * (HEAD_DIM ** -0.5)).astype(BF16)
            dqs.append(_unstack_heads(_dot(dsb, kcat)))
            dks.append(_dot(dsb, qs, TN))
        dq_ref[...] = jnp.concatenate(dqs, axis=1)
        upd = jnp.concatenate(dks + dvs, axis=1)
        cur = pl.multiple_of(n * BLOCK, BLOCK)
        dkv_ref[pl.ds(cur, BLOCK), :] += upd[BLOCK:]

        @pl.when(n > 0)
        def _():
            prev = pl.multiple_of((n - 1) * BLOCK, BLOCK)
            dkv_ref[pl.ds(prev, BLOCK), :] += upd[:BLOCK]

        @pl.when(n == nb - 1)
        def _():
            for hk in range(N_KV_HEADS):
                for g in range(Q_PER_KV):
                    tot = jnp.sum(dsacc[hk, g * BLOCK:(g + 1) * BLOCK, :], axis=0, keepdims=True)
                    h = hk * Q_PER_KV + g
                    dsink_ref[h:h + 1, :] = jnp.broadcast_to(tot, (1, LANES))

    return pl.pallas_call(
        body, name="attn_bwd", grid=(nb,),
        in_specs=_attn_in_specs() + [pl.BlockSpec((BLOCK, D_ATTN), lambda n: (n, 0))],
        out_specs=[pl.BlockSpec((BLOCK, D_ATTN), lambda n: (n, 0)), _const((SEQ, 2 * D_KV)),
                   _const((N_KV_HEADS, Q_PER_KV * BLOCK, 2 * BLOCK)), _const((N_Q_HEADS, LANES))],
        out_shape=[jax.ShapeDtypeStruct((SEQ, D_ATTN), F32), jax.ShapeDtypeStruct((SEQ, 2 * D_KV), F32),
                   jax.ShapeDtypeStruct((N_KV_HEADS, Q_PER_KV * BLOCK, 2 * BLOCK), F32),
                   jax.ShapeDtypeStruct((N_Q_HEADS, LANES), F32)],
        scratch_shapes=[pltpu.VMEM((N_KV_HEADS, Q_PER_KV * BLOCK, 1), F32)],
        compiler_params=_cp(("arbitrary",)),
    )(proj, proj, proj, bias, sinks, dcat)


@jax.custom_vjp
def _head_sum(x):
    ones = _head_ones(LANES)
    return jnp.concatenate([_dot_ind(x[:, c:c + LANES], ones, 2) for c in range(0, x.shape[-1], LANES)], axis=1)


_head_sum.defvjp(lambda x: (_head_sum(x), None), lambda _, ct: (_head_sum(ct),))


@jax.custom_vjp
def _bdot(a, w):
    return _dot(a.astype(BF16), w.astype(BF16))


def _bdot_bwd(res, ct):
    a, w = res
    ctb = ct.astype(BF16)
    return _dot(ctb, w.astype(BF16), NT), _dot(a.astype(BF16), ctb, TN)


_bdot.defvjp(lambda a, w: (_bdot(a, w), (a, w)), _bdot_bwd)


def _sigmoid(x):
    return 0.5 * (jnp.tanh(0.5 * x) + 1.0)


def _softplus(x):
    return jnp.maximum(x, 0.0) + jnp.log(1.0 + jnp.exp(-jnp.abs(x)))


def _rwkv_core(r, k, v, zwa, zg, w0, wdu, a0, wiu, wgu, k_k, k_a):
    w_log = -_softplus(-(w0 + _bdot(jnp.tanh(zwa), wdu))) - 0.5
    decay = jnp.exp(-jnp.exp(w_log))
    a = _sigmoid(a0 + _bdot(zwa, wiu))
    g = _bdot(_sigmoid(zg), wgu)
    kk = k * k_k
    kk = kk / jnp.maximum(jnp.sqrt(_head_sum(kk * kk)), 1e-12)
    k2 = k * (1.0 + (a - 1.0) * k_a)
    return r, decay, k2, v, -kk, kk * a, g


def _rwkv_out(o, r, k2, v, g, lng, lnb, rk):
    mu = _head_sum(o) * (1.0 / HEAD_DIM)
    d = o - mu
    var = _head_sum(d * d) * (1.0 / HEAD_DIM)
    on = d * lax.rsqrt(var + GN_EPS) * lng + lnb
    bonus = _head_sum(r * k2 * rk) * v
    return (on + bonus) * g


P_SPLITS = (0, 512, 1024, 1536, 1664, 1792)
N_PREP_PARAMS = 7
HALO = 8


def _shifted_pieces(i, p_ref, halo_ref, mix_ref):
    p = p_ref[:, P_OFF:]
    prev_row = halo_ref[HALO - 1:HALO, P_OFF:] * jnp.where(i > 0, 1.0, 0.0)
    row = lax.broadcasted_iota(jnp.int32, p.shape, 0)
    pprev = jnp.where(row == 0, prev_row, pltpu.roll(p, 1, 0))
    delta = pprev - p
    ps = p + delta * mix_ref[...]
    return [ps[:, a:b] for a, b in zip(P_SPLITS[:-1], P_SPLITS[1:])], delta


def _prep_in_specs():
    return [_rows(TR, D_IN),
            pl.BlockSpec((HALO, D_IN), lambda i: (jnp.maximum(i * (TR // HALO) - 1, 0), 0)),
            _const((1, RWKV_COLS)), _const((1, D_RWKV)), _const((LANES, D_RWKV)), _const((1, D_RWKV)),
            _const((LANES, D_RWKV)), _const((LANES, D_RWKV)), _const((1, D_RWKV)), _const((1, D_RWKV))]


def _rwkv_prep(proj, mix, prm):
    def body(p_ref, halo_ref, mix_ref, *refs):
        prm_refs, outs = refs[:N_PREP_PARAMS], refs[N_PREP_PARAMS:]
        pieces, _ = _shifted_pieces(pl.program_id(0), p_ref, halo_ref, mix_ref)
        vals = _rwkv_core(*pieces, *[t[...] for t in prm_refs])
        for ref, val in zip(outs, vals):
            ref[...] = val

    return pl.pallas_call(
        body, name="rwkv_prep", grid=(SEQ // TR,),
        in_specs=_prep_in_specs(),
        out_specs=[_rows(TR, D_RWKV)] * 7,
        out_shape=[jax.ShapeDtypeStruct((SEQ, D_RWKV), F32)] * 7,
        compiler_params=_cp(("parallel",)),
    )(proj, proj, mix, *prm)


def _rwkv_prep_bwd(proj, mix, prm, cts):
    def body(p_ref, halo_ref, mix_ref, *refs):
        i = pl.program_id(0)
        prm_refs = refs[:N_PREP_PARAMS]
        ct_refs = refs[N_PREP_PARAMS:N_PREP_PARAMS + 10]
        dps_ref, dmix_ref = refs[N_PREP_PARAMS + 10:N_PREP_PARAMS + 12]
        dprm_refs = refs[N_PREP_PARAMS + 12:]
        pieces, delta = _shifted_pieces(i, p_ref, halo_ref, mix_ref)
        _, vjp = jax.vjp(_rwkv_core, *pieces, *[t[...] for t in prm_refs])
        dr1, dr2, dw, dk1, dk2, dv1, dv2, dkkn, db, dg = [t[...] for t in ct_refs]
        grads = vjp((dr1 + dr2, dw, dk1 + dk2, dv1 + dv2, dkkn, db, dg))
        dps = jnp.concatenate(grads[:5], axis=1)
        dps_ref[...] = dps

        @pl.when(i == 0)
        def _():
            dmix_ref[...] = jnp.zeros_like(dmix_ref)
            for ref in dprm_refs:
                ref[...] = jnp.zeros_like(ref)

        dmix_ref[...] += jnp.sum(dps * delta, axis=0, keepdims=True)
        for ref, gval in zip(dprm_refs, grads[5:]):
            ref[...] += gval

    prm_shapes = [(1, D_RWKV), (LANES, D_RWKV), (1, D_RWKV), (LANES, D_RWKV), (LANES, D_RWKV), (1, D_RWKV), (1, D_RWKV)]
    return pl.pallas_call(
        body, name="rwkv_prep_bwd", grid=(SEQ // TR,),
        in_specs=_prep_in_specs() + [_rows(TR, D_RWKV)] * 10,
        out_specs=[_rows(TR, RWKV_COLS), _const((1, RWKV_COLS))] + [_const(s) for s in prm_shapes],
        out_shape=[jax.ShapeDtypeStruct((SEQ, RWKV_COLS), F32), jax.ShapeDtypeStruct((1, RWKV_COLS), F32)]
        + [jax.ShapeDtypeStruct(s, F32) for s in prm_shapes],
        compiler_params=_cp(("arbitrary",)),
    )(proj, proj, mix, *prm, *cts)


def _rwkv_post(o, r, k2, v, g, lng, lnb, rk, attn):
    def body(o_ref, r_ref, k_ref, v_ref, g_ref, lng_ref, lnb_ref, rk_ref, attn_ref, cat_ref):
        rw = _rwkv_out(*[t[...] for t in (o_ref, r_ref, k_ref, v_ref, g_ref, lng_ref, lnb_ref, rk_ref)])
        cat_ref[...] = jnp.concatenate([attn_ref[...], rw], axis=1).astype(BF16)

    return pl.pallas_call(
        body, name="rwkv_post", grid=(SEQ // TR,),
        in_specs=[_rows(TR, D_RWKV)] * 5 + [_const((1, D_RWKV))] * 3 + [_rows(TR, D_ATTN)],
        out_specs=_rows(TR, D_MODEL),
        out_shape=jax.ShapeDtypeStruct((SEQ, D_MODEL), BF16),
        compiler_params=_cp(("parallel",)),
    )(o, r, k2, v, g, lng, lnb, rk, attn)


def _rwkv_post_bwd(o, r, k2, v, g, lng, lnb, rk, dcat):
    def body(o_ref, r_ref, k_ref, v_ref, g_ref, lng_ref, lnb_ref, rk_ref, dcat_ref,
             do_ref, dr_ref, dk_ref, dv_ref, dg_ref, dlng_ref, dlnb_ref, drk_ref):
        i = pl.program_id(0)
        args = [t[...] for t in (o_ref, r_ref, k_ref, v_ref, g_ref, lng_ref, lnb_ref, rk_ref)]
        _, vjp = jax.vjp(_rwkv_out, *args)
        grads = vjp(dcat_ref[:, D_ATTN:])
        for ref, gval in zip((do_ref, dr_ref, dk_ref, dv_ref, dg_ref), grads[:5]):
            ref[...] = gval

        @pl.when(i == 0)
        def _():
            for ref in (dlng_ref, dlnb_ref, drk_ref):
                ref[...] = jnp.zeros_like(ref)

        for ref, gval in zip((dlng_ref, dlnb_ref, drk_ref), grads[5:]):
            ref[...] += gval

    return pl.pallas_call(
        body, name="rwkv_post_bwd", grid=(SEQ // TR,),
        in_specs=[_rows(TR, D_RWKV)] * 5 + [_const((1, D_RWKV))] * 3 + [_rows(TR, D_MODEL)],
        out_specs=[_rows(TR, D_RWKV)] * 5 + [_const((1, D_RWKV))] * 3,
        out_shape=[jax.ShapeDtypeStruct((SEQ, D_RWKV), F32)] * 5 + [jax.ShapeDtypeStruct((1, D_RWKV), F32)] * 3,
        compiler_params=_cp(("arbitrary",)),
    )(o, r, k2, v, g, lng, lnb, rk, dcat)


def _assemble_dproj(dq, dkv, dps, mix):
    last = SEQ // HALO - 1

    def body(dq_ref, dkv_ref, dps_ref, nxt_ref, mix_ref, o_ref):
        i = pl.program_id(0)
        dps = dps_ref[...]
        mixv = mix_ref[...]
        nxt_row = nxt_ref[0:1, :] * jnp.where(i < SEQ // TR - 1, 1.0, 0.0)
        row = lax.broadcasted_iota(jnp.int32, dps.shape, 0)
        up = jnp.where(row == TR - 1, nxt_row, pltpu.roll(dps, TR - 1, 0))
        dp = dps * (1.0 - mixv) + up * mixv
        o_ref[...] = jnp.concatenate([dq_ref[...], dkv_ref[...], dp], axis=1).astype(BF16)

    return pl.pallas_call(
        body, name="assemble_dproj", grid=(SEQ // TR,),
        in_specs=[_rows(TR, D_ATTN), _rows(TR, 2 * D_KV), _rows(TR, RWKV_COLS),
                  pl.BlockSpec((HALO, RWKV_COLS), lambda i: (jnp.minimum((i + 1) * (TR // HALO), last), 0)),
                  _const((1, RWKV_COLS))],
        out_specs=_rows(TR, D_IN),
        out_shape=jax.ShapeDtypeStruct((SEQ, D_IN), BF16),
        compiler_params=_cp(("parallel",)),
    )(dq, dkv, dps, dps, mix)


N_PAIR = D_RWKV // LANES
CHUNK = 64
N_CHUNK = SEQ // CHUNK
GROUP = 64
STATE = (N_PAIR, HEAD_DIM, LANES)


def _lane_sums(lhs_tiles, ones2):
    out = _dot(jnp.concatenate(lhs_tiles, axis=0), ones2)
    return [out[i * HEAD_DIM:(i + 1) * HEAD_DIM] for i in range(len(lhs_tiles))]


def _seg_sum(xs, ones2):
    return _lane_sums([jnp.concatenate(_split(x, 2), axis=1) for x in xs], ones2)


def _seg_sum_rows(xs, ones2):
    out = _dot(jnp.concatenate(_split(jnp.concatenate(xs, axis=0), 2), axis=1), ones2)
    return [out[i * GROUP:(i + 1) * GROUP] for i in range(len(xs))]


def _col_form(rows, diag, ones2):
    zero = jnp.zeros((HEAD_DIM, LANES), BF16)
    tiles = []
    for row in rows:
        hi = row.astype(BF16)
        lo = (row - hi.astype(F32)).astype(BF16)
        tiles.append(jnp.concatenate(
            [jnp.where(diag, jnp.broadcast_to(part, (HEAD_DIM, LANES)), zero) for part in (hi, lo)], axis=1))
    return _lane_sums(tiles, ones2)


def _scan_consts():
    ones2 = jnp.concatenate([_head_ones(LANES)] * 2, axis=0)
    sub = lax.broadcasted_iota(jnp.int32, (HEAD_DIM, LANES), 0)
    lane_in_head = lax.broadcasted_iota(jnp.int32, (HEAD_DIM, LANES), 1) & (HEAD_DIM - 1)
    return ones2, lane_in_head == sub, lane_in_head


def _rows_of_columns(tile):
    t = tile.T
    return jnp.concatenate([t[:CHUNK], t[HEAD_DIM:HEAD_DIM + CHUNK]], axis=1)


def _pair(j):
    return slice(j * LANES, (j + 1) * LANES)


def _scan_fwd(r, w, k, v, kkn, b):
    def body(r_ref, w_ref, k_ref, v_ref, kkn_ref, b_ref, o_ref, st_ref, sa_ref, s_scr):
        c = pl.program_id(0)
        ones2, diag, lane_in_head = _scan_consts()

        @pl.when(c == 0)
        def _():
            s_scr[...] = jnp.zeros_like(s_scr)

        def group(gi, carry):
            row0 = pl.multiple_of(gi * GROUP, GROUP)
            states, ocols = list(carry[:N_PAIR]), list(carry[N_PAIR:])
            tiles = [[t[pl.ds(row0, GROUP), _pair(j)] for t in (r_ref, w_ref, k_ref, v_ref, kkn_ref, b_ref)]
                     for j in range(N_PAIR)]
            def row(j, name, u):
                return tiles[j]["rwkvnb".index(name)][u:u + 1]

            def emit_out(u, after):
                outs = _seg_sum([s[j] * row(j, "r", u + d) for d, s in enumerate(after) for j in range(N_PAIR)], ones2)
                for d in range(2):
                    here = lane_in_head == gi * GROUP + u + d
                    for j in range(N_PAIR):
                        ocols[j] = jnp.where(here, outs[d * N_PAIR + j], ocols[j])

            def vcols_of(u):
                cols = _col_form([row(j, "v", u + d) for d in range(2) for j in range(N_PAIR)], diag, ones2)
                return cols[:N_PAIR], cols[N_PAIR:]

            n_next = [pltpu.roll(tiles[j][4], GROUP - 1, 0) for j in range(N_PAIR)]
            dots = _seg_sum_rows([tiles[j][5] * n_next[j] for j in range(N_PAIR)]
                                 + [tiles[j][2] * n_next[j] for j in range(N_PAIR)], ones2)
            b_n, k_n = dots[:N_PAIR], dots[N_PAIR:]
            w_n = [tiles[j][1] * n_next[j] for j in range(N_PAIR)]

            vcols = vcols_of(0)
            after = None
            for u in range(0, GROUP, 2):
                prods = _seg_sum([states[j] * row(j, "n", u) for j in range(N_PAIR)]
                                 + [states[j] * w_n[j][u:u + 1] for j in range(N_PAIR)], ones2)
                if after is not None:
                    emit_out(u - 2, after)
                nxt = vcols_of(u + 2) if u + 2 < GROUP else None
                first, second = [], []
                for j in range(N_PAIR):
                    sa1 = prods[j]
                    sa2 = prods[N_PAIR + j] + sa1 * b_n[j][u:u + 1] + vcols[0][j] * k_n[j][u:u + 1]
                    s1 = states[j] * row(j, "w", u) + sa1 * row(j, "b", u) + vcols[0][j] * row(j, "k", u)
                    s2 = s1 * row(j, "w", u + 1) + sa2 * row(j, "b", u + 1) + vcols[1][j] * row(j, "k", u + 1)
                    st_ref[row0 + u, j] = s1
                    sa_ref[row0 + u, j] = sa1
                    st_ref[row0 + u + 1, j] = s2
                    sa_ref[row0 + u + 1, j] = sa2
                    first.append(s1)
                    second.append(s2)
                    states[j] = s2
                after, vcols = (first, second), nxt
            emit_out(GROUP - 2, after)
            return tuple(states + ocols)

        zero = jnp.zeros((HEAD_DIM, LANES), F32)
        fin = lax.fori_loop(0, CHUNK // GROUP, group, tuple(s_scr[j] for j in range(N_PAIR)) + (zero,) * N_PAIR)
        for j in range(N_PAIR):
            s_scr[j] = fin[j]
            o_ref[:, _pair(j)] = _rows_of_columns(fin[N_PAIR + j])

    blk = pl.BlockSpec((CHUNK, D_RWKV), lambda c: (c, 0))
    per_step = pl.BlockSpec((CHUNK,) + STATE, lambda c: (c, 0, 0, 0))
    return pl.pallas_call(
        body, name="rwkv_scan_fwd", grid=(N_CHUNK,),
        in_specs=[blk] * 6,
        out_specs=[blk, per_step, per_step],
        out_shape=[jax.ShapeDtypeStruct((SEQ, D_RWKV), F32)] + [jax.ShapeDtypeStruct((SEQ,) + STATE, F32)] * 2,
        scratch_shapes=[pltpu.VMEM(STATE, F32)],
        compiler_params=_cp(("arbitrary",)),
    )(r, w, k, v, kkn, b)


def _scan_bwd(r, w, k, v, kkn, b, do, states, sas, ds_in, prev, name, first_chunk, n_chunks):
    top = first_chunk + n_chunks - 1

    def body(r_ref, w_ref, k_ref, v_ref, kkn_ref, b_ref, do_ref, st_ref, before_ref, sa_ref, ds_in_ref, *rest):
        dr_ref, dw_ref, dk_ref, dv_ref, dkkn_ref, db_ref, ds_out_ref, ds_scr = rest[-8:]
        i = pl.program_id(0)
        ones2, diag, lane_in_head = _scan_consts()

        @pl.when(i == 0)
        def _():
            ds_scr[...] = ds_in_ref[...]

        entry = [before_ref[0, j] * jnp.where(i < top, 1.0, 0.0) for j in range(N_PAIR)]

        def reverse(gr, carry):
            gi = CHUNK // GROUP - 1 - gr
            row0 = pl.multiple_of(gi * GROUP, GROUP)
            dstates, dvcols = list(carry[:N_PAIR]), list(carry[N_PAIR:])
            tiles = [[t[pl.ds(row0, GROUP), _pair(j)]
                      for t in (r_ref, w_ref, k_ref, v_ref, kkn_ref, b_ref, do_ref)] for j in range(N_PAIR)]
            rows = [[[None] * GROUP for _ in range(5)] for _ in range(N_PAIR)]

            def row(j, name, u):
                return tiles[j]["rwkvnbd".index(name)][u:u + 1]

            def cols_of(u):
                cols = _col_form([row(j, name, u - d) for d in range(2) for name in "dv" for j in range(N_PAIR)],
                                 diag, ones2)
                return [[(cols[(2 * d) * N_PAIR + j], cols[(2 * d + 1) * N_PAIR + j]) for j in range(N_PAIR)]
                        for d in range(2)]

            def emit_dv(u, dsps):
                outs = _seg_sum([dsp[j] * row(j, "k", u - d) for d, dsp in enumerate(dsps) for j in range(N_PAIR)], ones2)
                for d in range(2):
                    here = lane_in_head == gi * GROUP + u - d
                    for j in range(N_PAIR):
                        dvcols[j] = jnp.where(here, outs[d * N_PAIR + j], dvcols[j])

            b_prev = [pltpu.roll(tiles[j][5], 1, 0) for j in range(N_PAIR)]
            dots = _seg_sum_rows([tiles[j][4] * b_prev[j] for j in range(N_PAIR)]
                                 + [tiles[j][0] * tiles[j][5] for j in range(N_PAIR)], ones2)
            n_b, r_b = dots[:N_PAIR], dots[N_PAIR:]
            w_b = [tiles[j][1] * b_prev[j] for j in range(N_PAIR)]

            def outputs(u, j, dsp, dsa, docol, vcol):
                tl = gi * GROUP + u
                if u > 0:
                    s_prev = st_ref[tl - 1, j]
                else:
                    s_prev = jnp.where(gi == 0, entry[j], st_ref[jnp.maximum(tl - 1, 0), j])
                rows[j][0][u] = jnp.sum(st_ref[tl, j] * docol, axis=0, keepdims=True)
                rows[j][1][u] = jnp.sum(dsp * s_prev, axis=0, keepdims=True)
                rows[j][2][u] = jnp.sum(dsp * vcol, axis=0, keepdims=True)
                rows[j][3][u] = jnp.sum(s_prev * dsa, axis=0, keepdims=True)
                rows[j][4][u] = jnp.sum(dsp * sa_ref[tl, j], axis=0, keepdims=True)

            cols = cols_of(GROUP - 1)
            before = None
            for u in range(GROUP - 1, 0, -2):
                dsp1 = [dstates[j] + cols[0][j][0] * row(j, "r", u) for j in range(N_PAIR)]
                prods = _seg_sum([dsp1[j] * row(j, "b", u) for j in range(N_PAIR)]
                                 + [dsp1[j] * w_b[j][u:u + 1] for j in range(N_PAIR)], ones2)
                if before is not None:
                    emit_dv(u + 2, before)
                nxt = cols_of(u - 2) if u >= 2 else None
                dsp2 = []
                for j in range(N_PAIR):
                    dsa1 = prods[j]
                    dsa2 = prods[N_PAIR + j] + dsa1 * n_b[j][u:u + 1] + cols[1][j][0] * r_b[j][u - 1:u]
                    mid = dsp1[j] * row(j, "w", u) + dsa1 * row(j, "n", u) + cols[1][j][0] * row(j, "r", u - 1)
                    outputs(u, j, dsp1[j], dsa1, *cols[0][j])
                    outputs(u - 1, j, mid, dsa2, *cols[1][j])
                    dstates[j] = mid * row(j, "w", u - 1) + dsa2 * row(j, "n", u - 1)
                    dsp2.append(mid)
                before, cols = (dsp1, dsp2), nxt
            emit_dv(1, before)
            for j in range(N_PAIR):
                for ref, rr in zip((dr_ref, dw_ref, dk_ref, dkkn_ref, db_ref), rows[j]):
                    ref[pl.ds(row0, GROUP), _pair(j)] = jnp.concatenate(rr, axis=0)
            return tuple(dstates + dvcols)

        zero = jnp.zeros((HEAD_DIM, LANES), F32)
        dfin = lax.fori_loop(0, CHUNK // GROUP, reverse, tuple(ds_scr[j] for j in range(N_PAIR)) + (zero,) * N_PAIR)
        for j in range(N_PAIR):
            ds_scr[j] = dfin[j]
            dv_ref[:, _pair(j)] = _rows_of_columns(dfin[N_PAIR + j])

        @pl.when(i == n_chunks - 1)
        def _():
            ds_out_ref[...] = ds_scr[...]

    blk = pl.BlockSpec((CHUNK, D_RWKV), lambda i: (top - i, 0))
    per_step = pl.BlockSpec((CHUNK,) + STATE, lambda i: (top - i, 0, 0, 0))
    step_before = pl.BlockSpec((1,) + STATE, lambda i: (jnp.maximum((top - i) * CHUNK - 1, 0), 0, 0, 0))
    prev = [] if prev is None else list(prev)
    outs = pl.pallas_call(
        body, name=name, grid=(n_chunks,),
        in_specs=[blk] * 7 + [per_step, step_before, per_step, _const(STATE)] + [ANY] * len(prev),
        out_specs=[blk] * 6 + [_const(STATE)],
        out_shape=[jax.ShapeDtypeStruct((SEQ, D_RWKV), F32)] * 6 + [jax.ShapeDtypeStruct(STATE, F32)],
        scratch_shapes=[pltpu.VMEM(STATE, F32)],
        input_output_aliases={11 + t: t for t in range(len(prev))},
        compiler_params=_cp(("arbitrary",)),
    )(r, w, k, v, kkn, b, do, states, states, sas, ds_in, *prev)
    return outs[:6], outs[6]


def _stacked(rows, cols, pick):
    return pl.BlockSpec((None, rows, cols), pick)


def _local_step(x, target, sm, win_st):
    def tied(t, token):
        return t if token is None else t + token[0:1, 0:1].reshape((1,) * t.ndim)

    zpad = jnp.zeros((LORA_DECAY, D_RWKV), F32)
    prm = [sm["w0"], jnp.concatenate([sm["w_decay_up"], zpad], axis=0), sm["a0"],
           jnp.concatenate([zpad, sm["w_iclr_up"]], axis=0), sm["w_gate_up"], sm["k_k"], sm["k_a"]]
    mix = sm["rwkv_shift_mix"]
    onehot = jnp.asarray(_t5_onehot(), BF16)
    sinks = sm["sinks"].reshape(N_Q_HEADS)
    lng, lnb, rk = sm["ln_x_g"], sm["ln_x_b"], sm["r_k"].reshape(1, D_RWKV)

    h1 = _norm_cast(x, sm["norm_mix_pre"], "norm_in")
    proj = _matmul(h1, win_st, "nn", "proj", m=SEQ, n=D_IN, k=D_MODEL, tm=SEQ, tn=640,
                   b_spec=_stacked(D_MODEL, 640, lambda i, j: (j, 0, 0)))
    bias = _bias_table(sm["rel_bias"].T, onehot).reshape(N_KV_HEADS, Q_PER_KV * BLOCK, 2 * BLOCK)
    attn = _attn_fwd(proj, bias, sinks)
    r, w, k2, v, kkn, b, g = _rwkv_prep(proj, mix, prm)
    o, states, sas = _scan_fwd(r, w, k2, v, kkn, b)
    wout, wup_st, wdown = yield ("rest_weights", o)
    cat = _rwkv_post(o, r, k2, v, g, lng, lnb, rk, attn)
    mixo = _matmul(cat, wout, "nn", "out_proj", m=SEQ, n=D_MODEL, k=D_MODEL, tm=SEQ, tn=512)
    x2, h3 = _mix_norm(x, mixo, sm["norm_mix_post"], sm["norm_ffn_pre"])
    u_gate, u_val, gate, val, act = _ffn_up_act(h3, wup_st, sm["conv_w"], sm["conv_b"])
    f = _matmul(act, wdown, "nn", "ffn_down", m=SEQ, n=D_MODEL, k=D_FF, tm=1024, tn=512)
    loss, dy, df, d_g4 = _loss_head(x2, f, sm["norm_ffn_post"], target)

    d_wdown = _matmul(act, df, "tn", "d_wdown", m=D_FF, n=D_MODEL, k=SEQ, tm=1024, tn=D_MODEL)
    du, d_convw, d_convb = _ffn_act_bwd(u_gate, u_val, gate, val, df, wdown, sm["conv_w"])
    d_convw = d_convw.transpose(1, 0, 2).reshape(3, 2 * D_FF)
    d_convb = d_convb.reshape(1, 2 * D_FF)
    dh3 = _matmul_nt_shards(du, wup_st, "d_h3", m=SEQ, n=D_MODEL, tm=512, tn=512,
                            a_spec=pl.BlockSpec((2, 512, D_FF), lambda i, j: (0, i, 0)),
                            a_piece=lambda ref, s: ref[s // 2, :, (s % 2) * 2048:(s % 2 + 1) * 2048])
    d_wup = _matmul(h3, du, "tn", "d_wup", m=D_MODEL, n=2 * D_FF, k=SEQ, tm=D_MODEL, tn=1024,
                    b_spec=pl.BlockSpec((None, SEQ, 1024), lambda i, j: (j // 4, 0, j % 4)),
                    out=((N_CHIPS, D_MODEL, 2048), _stacked(D_MODEL, 1024, lambda i, j: (j // 2, 0, j % 2))))
    dx2, dmix, d_g2, d_g3 = _mid_bwd(x2, mixo, dy, dh3, sm["norm_mix_post"], sm["norm_ffn_pre"])
    dcat = _matmul(dmix, wout, "nt", "d_cat", m=SEQ, n=D_MODEL, k=D_MODEL, tm=SEQ, tn=512)
    d_wout = _matmul(cat, dmix, "tn", "d_wout", m=D_MODEL, n=D_MODEL, k=SEQ, tm=512, tn=D_MODEL)
    token = yield ("grads_a", (d_wdown, d_wup, d_wout))
    do, dr_p, dk_p, dv_p, dg, d_lng, d_lnb, d_rk = _rwkv_post_bwd(o, r, k2, v, g, lng, tied(lnb, token), rk, dcat)
    half = N_CHUNK // 2
    ds_end = jnp.zeros(STATE, F32)
    late, ds_mid = _scan_bwd(r, w, k2, v, kkn, b, do, states, sas, ds_end, None, "rwkv_scan_bwd_late", half, half)
    token = yield ("seam_1", ds_mid)
    scan_cts, ds_first = _scan_bwd(r, w, k2, v, kkn, b, do, states, sas, tied(ds_mid, token), late,
                                   "rwkv_scan_bwd_early", 0, half)
    dr_s, dw_s, dk_s, dv_s, dkkn_s, db_s = scan_cts
    token = yield ("seam_2", ds_first)
    prep_grads = _rwkv_prep_bwd(proj, tied(mix, token), prm,
                                (dr_s, dr_p, dw_s, dk_s, dk_p, dv_s, dv_p, dkkn_s, db_s, dg))
    dps, d_mix, d_w0, d_wdu, d_a0, d_wiu, d_wgu, d_kk, d_ka = prep_grads
    dq, dkv, dbias, dsink = _attn_bwd(proj, bias, sinks, dcat)
    d_relb = _bias_table_bwd(dbias.reshape(N_Q_HEADS, N_REL), onehot).T
    dproj = _assemble_dproj(dq, dkv, dps, mix)
    d_win = _matmul(h1, dproj, "tn", "d_win", m=D_MODEL, n=D_IN, k=SEQ, tm=D_MODEL, tn=640,
                    out=((N_CHIPS, D_MODEL, 640), _stacked(D_MODEL, 640, lambda i, j: (j, 0, 0))))
    token = yield ("grads_b", d_win)
    dh1 = _matmul_nt_shards(dproj, win_st, "d_h1", m=SEQ, n=D_MODEL, tm=1024, tn=D_MODEL,
                            a_spec=pl.BlockSpec((1024, D_IN), lambda i, j: (i, 0)),
                            a_piece=lambda ref, s: ref[:, s * 640:(s + 1) * 640])
    grad_x, d_g1 = _first_bwd(x, dx2, dh1, tied(sm["norm_mix_pre"], token))

    grads = {
        "norm_mix_pre": d_g1, "norm_mix_post": d_g2, "norm_ffn_pre": d_g3, "norm_ffn_post": d_g4,
        "w_in": d_win, "rel_bias": d_relb, "sinks": dsink[:, 0].reshape(1, N_Q_HEADS),
        "rwkv_shift_mix": d_mix, "w0": d_w0, "w_decay_up": d_wdu[:LORA_DECAY], "a0": d_a0,
        "w_iclr_up": d_wiu[LORA_DECAY:], "w_gate_up": d_wgu, "k_k": d_kk, "k_a": d_ka,
        "r_k": d_rk.reshape(1, N_Q_HEADS, HEAD_DIM), "ln_x_g": d_lng, "ln_x_b": d_lnb,
        "w_out": d_wout, "w_ffn_up": d_wup, "conv_w": d_convw, "conv_b": d_convb, "w_ffn_down": d_wdown,
    }
    return loss, grad_x, grads


def _place():
    x, y, c = lax.axis_index("x"), lax.axis_index("y"), lax.axis_index("c")
    chips = [(1 - x, y), (x, 1 - y), (1 - x, 1 - y)]
    return x, y, c, chips


def _remote(src, dst, sems, idx, to):
    return pltpu.make_async_remote_copy(src_ref=src, dst_ref=dst, send_sem=sems[0].at[idx], recv_sem=sems[1].at[idx],
                                        device_id=to, device_id_type=MESH)


ROW_ALIGN = 16


def _half(c, rows):
    return pl.ds(pl.multiple_of(c * (rows // 2), ROW_ALIGN), rows // 2)


def _gather_weights(big, small):
    nb, ns = len(big), len(small)

    def body(*refs):
        ins, outs = refs[:nb + ns], refs[nb + ns:2 * (nb + ns)]
        ici, d2d, sml, loc = refs[2 * (nb + ns):2 * (nb + ns) + 2], refs[-5:-3], refs[-3:-1], refs[-1]
        x, y, c, chips = _place()
        me = 2 * x + y
        sib = (x, y, 1 - c)
        local = [pltpu.make_async_copy(ins[a], outs[a].at[me], loc.at[a]) for a in range(nb + ns)]
        for cp in local:
            cp.start()
        sends = []
        for a in range(nb):
            rows = _half(c, big[a].shape[0])
            for kk, chip in enumerate(chips):
                sends.append(_remote(ins[a].at[rows], outs[a].at[me, rows], ici, a * 3 + kk, (*chip, c)))
        for a in range(ns):
            for kk, chip in enumerate(chips):
                sends.append(_remote(ins[nb + a], outs[nb + a].at[me], sml, a * 3 + kk, (*chip, c)))
        for cp in sends:
            cp.start()
        passed = []
        for a in range(nb):
            rows = _half(c, big[a].shape[0])
            for kk, (px, py) in enumerate(chips):
                got = outs[a].at[2 * px + py, rows]
                _remote(got, got, ici, a * 3 + kk, sib).wait_recv()
                fwd = _remote(got, got, d2d, a * 3 + kk, sib)
                fwd.start()
                passed.append(fwd)
        for a in range(nb):
            other = _half(1 - c, big[a].shape[0])
            for kk, (px, py) in enumerate(chips):
                land = outs[a].at[2 * px + py, other]
                _remote(land, land, d2d, a * 3 + kk, sib).wait_recv()
        for a in range(ns):
            for kk, (px, py) in enumerate(chips):
                land = outs[nb + a].at[2 * px + py]
                _remote(land, land, sml, a * 3 + kk, sib).wait_recv()
        for cp in sends + passed:
            cp.wait_send()
        for cp in local:
            cp.wait()

    arrs = list(big) + list(small)
    in_vmem = pl.BlockSpec(memory_space=pltpu.VMEM)
    return pl.pallas_call(
        body, name="gather_weights",
        in_specs=[in_vmem] * len(arrs), out_specs=[in_vmem] * len(arrs),
        out_shape=[jax.ShapeDtypeStruct((N_CHIPS,) + t.shape, t.dtype) for t in arrs],
        scratch_shapes=[pltpu.SemaphoreType.DMA((3 * nb,)), pltpu.SemaphoreType.DMA((3 * nb,)),
                        pltpu.SemaphoreType.DMA((3 * nb,)), pltpu.SemaphoreType.DMA((3 * nb,)),
                        pltpu.SemaphoreType.DMA((3 * ns,)), pltpu.SemaphoreType.DMA((3 * ns,)),
                        pltpu.SemaphoreType.DMA((nb + ns,))],
        compiler_params=pltpu.CompilerParams(has_side_effects=True, vmem_limit_bytes=VMEM_LIMIT),
    )(*arrs)


HBM = pl.BlockSpec(memory_space=pltpu.HBM)
SEM = pl.BlockSpec(memory_space=pltpu.SEMAPHORE)
EFFECT = pltpu.SideEffectType.DATAFLOW_SIDE_EFFECTING


def _copies_start(name, bufs, plan, n, partners=None):
    nb = len(bufs)

    def body(*refs):
        ins, sems, token = refs[:nb], refs[nb:nb + 2 * n], refs[-1]
        if partners is not None:
            barrier = pltpu.get_barrier_semaphore()
            peers = partners[1]()
            for peer in peers:
                pl.semaphore_signal(barrier, inc=1, device_id=peer, device_id_type=MESH)
            pl.semaphore_wait(barrier, len(peers))
        for kk, (src, dst, dev) in enumerate(plan(ins)):
            pltpu.make_async_remote_copy(src_ref=src, dst_ref=dst, send_sem=sems[2 * kk], recv_sem=sems[2 * kk + 1],
                                         device_id=dev, device_id_type=MESH).start()
        token[...] = jnp.zeros_like(token)

    outs = pl.pallas_call(
        body, name=name,
        out_shape=tuple([pltpu.SemaphoreType.DMA(())] * (2 * n) + [pltpu.HBM(t.shape, t.dtype) for t in bufs]
                        + [jax.ShapeDtypeStruct((8, LANES), F32)]),
        in_specs=[HBM] * nb,
        out_specs=tuple([SEM] * (2 * n) + [HBM] * nb + [pl.BlockSpec(memory_space=pltpu.VMEM)]),
        input_output_aliases={t: 2 * n + t for t in range(nb)},
        compiler_params=pltpu.CompilerParams(has_side_effects=EFFECT,
                                             collective_id=None if partners is None else partners[0]),
    )(*[pltpu.with_memory_space_constraint(t, pltpu.HBM) for t in bufs])
    return outs[:2 * n], outs[2 * n:2 * n + nb], outs[-1]


def _copies_wait(name, sems, bufs, plan, n, after):
    nb = len(bufs)
    after = list(after) if isinstance(after, (list, tuple)) else [after]

    def body(*refs):
        ins, sem_refs = refs[:nb], refs[nb:nb + 2 * n]
        for kk, (src, dst, dev) in enumerate(plan(ins)):
            cp = pltpu.make_async_remote_copy(src_ref=src, dst_ref=dst, send_sem=sem_refs[2 * kk],
                                              recv_sem=sem_refs[2 * kk + 1], device_id=dev, device_id_type=MESH)
            cp.wait_send()
            cp.wait_recv()

    return pl.pallas_call(
        body, name=name,
        out_shape=tuple(pltpu.HBM(t.shape, t.dtype) for t in bufs),
        in_specs=[HBM] * nb + [SEM] * (2 * n) + [ANY] * len(after),
        out_specs=tuple([HBM] * nb),
        input_output_aliases={t: t for t in range(nb)},
        compiler_params=pltpu.CompilerParams(has_side_effects=EFFECT),
    )(*bufs, *sems, *after)


def _plan_gather(n_w):
    def plan(refs):
        x, y, c, chips = _place()
        me = 2 * x + y
        return [(refs[a], refs[n_w + a].at[me], (*chip, c)) for a in range(n_w) for chip in chips + [(x, y)]]
    return plan


def _plan_pair_halves(n_g, rows):
    def plan(refs):
        x, y, c, _ = _place()
        return [(refs[a].at[:, _half(1 - c, rows[a])], refs[n_g + a], (x, y, 1 - c)) for a in range(n_g)]
    return plan


def _plan_chip_parts(n_g):
    def plan(refs):
        x, y, c, chips = _place()
        me = 2 * x + y
        return [(refs[a].at[2 * px + py], refs[n_g + a].at[me], (px, py, c))
                for a in range(n_g) for (px, py) in chips]
    return plan


def _plan_pair_fill(n_g, rows):
    def plan(refs):
        x, y, c, _ = _place()
        return [(refs[a].at[_half(c, rows[a])], refs[a].at[_half(c, rows[a])], (x, y, 1 - c)) for a in range(n_g)]
    return plan


def _pair_add(g, got, name):
    _, rows, cols = g.shape
    hr = rows // 2
    tr = min(hr, 512)
    nb = hr // tr

    def body(g_ref, got_ref, p_ref, own_ref):
        val = (g_ref[...] + got_ref[...]).astype(BF16)
        p_ref[...] = val

        @pl.when(pl.program_id(1) == 2 * lax.axis_index("x") + lax.axis_index("y"))
        def _():
            own_ref[...] = val

    def mine(i, s):
        return (2 * lax.axis_index("x") + lax.axis_index("y"), i, 0)

    return pl.pallas_call(
        body, name=name, grid=(nb, N_CHIPS),
        in_specs=[pl.BlockSpec((None, tr, cols), lambda i, s: (s, lax.axis_index("c") * nb + i, 0)),
                  pl.BlockSpec((None, tr, cols), lambda i, s: (s, i, 0))],
        out_specs=[pl.BlockSpec((None, tr, cols), lambda i, s: (s, i, 0)), pl.BlockSpec((None, tr, cols), mine)],
        out_shape=[jax.ShapeDtypeStruct((N_CHIPS, hr, cols), BF16)] * 2,
        compiler_params=_cp(("parallel", "arbitrary")),
    )(g, got)


def _chip_sum(parts, name):
    _, hr, cols = parts.shape
    tr = min(hr, 256)
    nb = hr // tr

    def body(t_ref, o_ref):
        part = [t_ref[s].astype(F32) for s in range(N_CHIPS)]
        o_ref[...] = ((part[0] + part[1]) + part[2]) + part[3]

    return pl.pallas_call(
        body, name=name, grid=(nb,),
        in_specs=[pl.BlockSpec((N_CHIPS, tr, cols), lambda i: (0, i, 0))],
        out_specs=pl.BlockSpec((tr, cols), lambda i: (lax.axis_index("c") * nb + i, 0)),
        out_shape=jax.ShapeDtypeStruct((2 * hr, cols), F32),
        compiler_params=_cp(("parallel",)),
    )(parts)


class _Reduction:
    def __init__(self, tag, rows, first_id):
        self.tag, self.n, self.rows, self.first_id = tag, len(rows), rows, first_id
        self.plans = (_plan_pair_halves(self.n, rows), _plan_chip_parts(self.n), _plan_pair_fill(self.n, rows))
        self.flight = None

    def _name(self, what):
        return f"grad_{self.tag}_{what}"

    @staticmethod
    def _sibling():
        x, y, c, _ = _place()
        return [(x, y, 1 - c)]

    @staticmethod
    def _same_core_elsewhere():
        x, y, c, chips = _place()
        return [(*chip, c) for chip in chips]

    def start(self, gs):
        gots = [lax.empty((N_CHIPS, t.shape[1] // 2, t.shape[2]), F32) for t in gs]
        self.flight = _copies_start(self._name("pair_start"), list(gs) + gots, self.plans[0], self.n,
                                    (self.first_id, self._sibling))
        return self.flight[2]

    def after_pair(self, after):
        sems, bufs, _ = self.flight
        out = _copies_wait(self._name("pair_wait"), sems, bufs, self.plans[0], self.n, after)
        sums = [_pair_add(g, got, self._name(f"pair_add_{i}"))
                for i, (g, got) in enumerate(zip(out[:self.n], out[self.n:]))]
        self.flight = _copies_start(self._name("chip_start"), [p for p, _ in sums] + [own for _, own in sums],
                                    self.plans[1], 3 * self.n, (self.first_id + 1, self._same_core_elsewhere))
        return self.flight[2]

    def after_chips(self, after):
        sems, bufs, _ = self.flight
        out = _copies_wait(self._name("chip_wait"), sems, bufs, self.plans[1], 3 * self.n, after)
        fulls = [_chip_sum(t, self._name(f"chip_sum_{i}")) for i, t in enumerate(out[self.n:])]
        self.flight = _copies_start(self._name("fill_start"), fulls, self.plans[2], self.n,
                                    (self.first_id + 2, self._sibling))
        return self.flight[2]

    def finish(self, after):
        sems, bufs, _ = self.flight
        return _copies_wait(self._name("fill_wait"), sems, bufs, self.plans[2], self.n, after)


def _adamw_math(w, g, m, v):
    nm = ADAM_B1 * m + (1.0 - ADAM_B1) * g
    nv = ADAM_B2 * v + (1.0 - ADAM_B2) * (g * g)
    m_hat = nm / (1.0 - ADAM_B1 ** ADAM_STEP)
    v_hat = nv / (1.0 - ADAM_B2 ** ADAM_STEP)
    return -ADAM_LR * (m_hat / (jnp.sqrt(v_hat) + ADAM_EPS) + ADAM_WD * w), nm, nv


def _adamw(w, g, m, v, name, tr):
    r, cdim = w.shape

    def body(w_ref, g_ref, m_ref, v_ref, d_ref, nm_ref, nv_ref):
        d_ref[...], nm_ref[...], nv_ref[...] = _adamw_math(w_ref[...], g_ref[...], m_ref[...], v_ref[...])

    return pl.pallas_call(
        body, name=name, grid=(r // tr,), in_specs=[_rows(tr, cdim)] * 4, out_specs=[_rows(tr, cdim)] * 3,
        out_shape=[jax.ShapeDtypeStruct((r, cdim), F32)] * 3, compiler_params=_cp(("parallel",)),
    )(w, g, m, v)


def _adamw_small(w, parts, m, v, shapes):
    n_rows = w.shape[0]

    def scatter(src, outs):
        row = 0
        for (rows, cols), out in zip(shapes, outs):
            if cols == LANES:
                out[...] = src[row:row + rows, :]
            elif cols > LANES:
                per = cols // LANES
                for r in range(rows):
                    for cb in range(per):
                        out[r:r + 1, cb * LANES:(cb + 1) * LANES] = src[row + r * per + cb:row + r * per + cb + 1, :]
            else:
                per = LANES // cols
                for r in range(rows):
                    out[r:r + 1, :] = src[row + r // per:row + r // per + 1, (r % per) * cols:(r % per + 1) * cols]
            row += -(-rows * cols // LANES)

    def body(w_ref, p_ref, m_ref, v_ref, *rest):
        outs, scr = rest[:-4], rest[-4:]
        g = p_ref[0]
        for dev in range(1, N_DEV):
            g = g + p_ref[dev]
        scr[3][...] = g
        scr[0][...], scr[1][...], scr[2][...] = _adamw_math(w_ref[...], g, m_ref[...], v_ref[...])
        n = len(shapes)
        for kind in range(4):
            scatter(scr[kind], outs[kind * n:(kind + 1) * n])

    outs = pl.pallas_call(
        body, name="adamw_small", grid=(1,),
        in_specs=[_const(w.shape), _const(parts.shape), _const(w.shape), _const(w.shape)],
        out_specs=[_const(s) for s in shapes] * 4, out_shape=[jax.ShapeDtypeStruct(s, F32) for s in shapes] * 4,
        scratch_shapes=[pltpu.VMEM((n_rows, LANES), F32)] * 4,
        compiler_params=_cp(("arbitrary",)),
    )(w, parts, m, v)
    n = len(shapes)
    return [outs[kind * n:(kind + 1) * n] for kind in range(4)]


REPLICATED = (("norm_mix_pre", 1024), ("norm_mix_post", 1024), ("norm_ffn_pre", 1024), ("norm_ffn_post", 1024),
              ("rel_bias", 256), ("sinks", 8), ("rwkv_shift_mix", 1792), ("w0", 512), ("a0", 512), ("k_k", 512),
              ("k_a", 512), ("r_k", 512), ("ln_x_g", 512), ("ln_x_b", 512), ("conv_b", 8192))
SMALL_SHARDED = (("w_decay_up", LORA_DECAY, D_RWKV), ("w_iclr_up", LORA_ICLR, D_RWKV),
                 ("w_gate_up", LORA_GATE, D_RWKV), ("conv_w", 3, 2 * D_FF))
BIG = (("w_in", D_MODEL, 640), ("w_out", 256, D_MODEL), ("w_ffn_up", D_MODEL, 2048), ("w_ffn_down", 1024, D_MODEL))
PACK_ALIGN = 8 * LANES


def _pack(pieces):
    flat = []
    for t in pieces:
        t = t.reshape(-1)
        pad = (-t.shape[0]) % LANES
        flat.append(jnp.pad(t, (0, pad)) if pad else t)
    flat = jnp.concatenate(flat)
    pad = (-flat.shape[0]) % PACK_ALIGN
    return jnp.pad(flat, (0, pad)).reshape(-1, LANES)


def kernel(x, norm_mix_pre, norm_mix_post, norm_ffn_pre, norm_ffn_post, w_in, rel_bias, sinks, rwkv_shift_mix, w0, w_decay_up, a0, w_iclr_up, w_gate_up, k_k, k_a, r_k, ln_x_g, ln_x_b, w_out, w_ffn_up, conv_w, conv_b, w_ffn_down, loss_target, m_norm_mix_pre, m_norm_mix_post, m_norm_ffn_pre, m_norm_ffn_post, m_w_in, m_rel_bias, m_sinks, m_rwkv_shift_mix, m_w0, m_w_decay_up, m_a0, m_w_iclr_up, m_w_gate_up, m_k_k, m_k_a, m_r_k, m_ln_x_g, m_ln_x_b, m_w_out, m_w_ffn_up, m_conv_w, m_conv_b, m_w_ffn_down, v_norm_mix_pre, v_norm_mix_post, v_norm_ffn_pre, v_norm_ffn_post, v_w_in, v_rel_bias, v_sinks, v_rwkv_shift_mix, v_w0, v_w_decay_up, v_a0, v_w_iclr_up, v_w_gate_up, v_k_k, v_k_a, v_r_k, v_ln_x_g, v_ln_x_b, v_w_out, v_w_ffn_up, v_conv_w, v_conv_b, v_w_ffn_down):
    given = dict(locals())
    names = [n for n, _ in REPLICATED] + [n for n, _, _ in SMALL_SHARDED] + [n for n, _, _ in BIG]
    order = ["norm_mix_pre", "norm_mix_post", "norm_ffn_pre", "norm_ffn_post", "w_in", "rel_bias", "sinks",
             "rwkv_shift_mix", "w0", "w_decay_up", "a0", "w_iclr_up", "w_gate_up", "k_k", "k_a", "r_k", "ln_x_g",
             "ln_x_b", "w_out", "w_ffn_up", "conv_w", "conv_b", "w_ffn_down"]
    assert sorted(names) == sorted(order)

    big_sh = {n: given[n].reshape(a, b).astype(BF16) for n, a, b in BIG}
    small_sh = [given[n].reshape(r, c // N_CHIPS) for n, r, c in SMALL_SHARDED]
    gathered = _gather_weights([big_sh["w_in"]], small_sh)
    rest = ("w_out", "w_ffn_up", "w_ffn_down")
    win_st, rest_sh = lax.optimization_barrier((gathered[0], [big_sh[n] for n in rest]))
    sm = {n: given[n] for n, _ in REPLICATED}
    sm["r_k"] = r_k.reshape(N_Q_HEADS, HEAD_DIM)
    for (n, r, c), st in zip(SMALL_SHARDED, gathered[1:]):
        sm[n] = st.transpose(1, 0, 2).reshape(r, c)

    lands = [lax.empty((N_CHIPS,) + t.shape, BF16) for t in rest_sh]
    plan_w = _plan_gather(len(rest))
    n_w = N_CHIPS * len(rest)
    w_sems, w_bufs, token = _copies_start("gather_rest_start", rest_sh + lands, plan_w, n_w)
    sm["norm_mix_pre"] = norm_mix_pre + token[0:1, 0:1]

    def on_rest_weights(after):
        out = _copies_wait("gather_rest_wait", w_sems, w_bufs, plan_w, n_w, after)
        wout_st, wup_st, wdown_st = out[3:]
        return wout_st.reshape(D_MODEL, D_MODEL), wup_st, wdown_st.reshape(D_FF, D_MODEL)

    red_a = _Reduction("a", (1024, D_MODEL, 256), first_id=0)
    red_b = _Reduction("b", (D_MODEL,), first_id=3)

    def on_grads_a(gs):
        d_wdown, d_wup, d_wout = gs
        return red_a.start([d_wdown.reshape(N_CHIPS, 1024, D_MODEL), d_wup, d_wout.reshape(N_CHIPS, 256, D_MODEL)])

    handlers = {"rest_weights": on_rest_weights, "grads_a": on_grads_a, "seam_1": red_a.after_pair,
                "seam_2": red_a.after_chips, "grads_b": lambda g: red_b.start([g])}
    steps = _local_step(x[0], loss_target[0], sm, win_st)
    kind, payload = next(steps)
    while True:
        try:
            kind, payload = steps.send(handlers[kind](payload))
        except StopIteration as done:
            loss, grad_x, grads = done.value
            break

    small_names = [n for n, _ in REPLICATED] + [n for n, _, _ in SMALL_SHARDED]

    def shard_cols(t, s):
        return t[:, s * (t.shape[1] // N_CHIPS):(s + 1) * (t.shape[1] // N_CHIPS)]

    for_chip = jnp.stack([_pack([loss[0]] + [grads[n] for n, _ in REPLICATED]
                                + [shard_cols(grads[n], s) for n, _, _ in SMALL_SHARDED]) for s in range(N_CHIPS)])
    land = lax.empty((N_DEV,) + for_chip.shape[1:], F32)

    def plan_small(refs):
        x, y, c, _ = _place()
        out = []
        for rel in range(N_DEV):
            px, py, pc = x ^ (rel >> 2), y ^ ((rel >> 1) & 1), c ^ (rel & 1)
            out.append((refs[0].at[2 * px + py], refs[1].at[4 * x + 2 * y + c], (px, py, pc)))
        return out

    s_sems, s_bufs, s_token = _copies_start("grad_small_start", [for_chip, land], plan_small, N_DEV)

    red_b.after_pair([grad_x, s_token])
    g_out = {}
    g_out["w_ffn_down"], g_out["w_ffn_up"], g_out["w_out"] = red_a.finish(grad_x)

    delta, new_m, new_v = {}, {}, {}

    def update(n, a, b):
        delta[n], new_m[n], new_v[n] = _adamw(given[n].reshape(a, b), g_out[n], given["m_" + n].reshape(a, b),
                                              given["v_" + n].reshape(a, b), "adamw_" + n, 256)

    for n, a, b in BIG[1:]:
        update(n, a, b)
    done = [delta[n] for n, _, _ in BIG[1:]]
    red_b.after_chips(done)
    parts = _copies_wait("grad_small_wait", s_sems, s_bufs, plan_small, N_DEV, done)[1]
    no_param = jnp.zeros((LANES,), F32)
    packs = [_pack([no_param] + [given[pre + n] for n in small_names]) for pre in ("", "m_", "v_")]

    def piece_shape(n):
        shape = given[n].shape
        rows, cols = int(np.prod(shape[:-1])), shape[-1]
        whole = cols % LANES == 0 or (LANES % cols == 0 and (rows * cols) % LANES == 0 and cols >= HEAD_DIM)
        return (rows, cols) if whole else (-(-rows * cols // LANES), LANES)

    shapes = [(1, LANES)] + [piece_shape(n) for n in small_names]
    upd = _adamw_small(packs[0], parts, packs[1], packs[2], shapes)
    loss = upd[3][0][0, 0]
    for i, n in enumerate(small_names):
        shape = given[n].shape
        size = int(np.prod(shape))
        delta[n], new_m[n], new_v[n], g_out[n] = (u[1 + i].reshape(-1)[:size].reshape(shape) for u in upd)
    g_out["w_in"], = red_b.finish(upd[0][0])
    update(*BIG[0])

    def shaped(d):
        return [d[n].reshape(given[n].shape) for n in order]

    return (loss, grad_x.reshape(x.shape), *shaped(g_out), *shaped(delta), *shaped(new_m), *shaped(new_v))
```

```python
import math

import numpy as np
import jax
import jax.numpy as jnp
from jax import lax
from jax.experimental import pallas as pl
from jax.experimental.pallas import tpu as pltpu

F32 = jnp.float32
BF16 = jnp.bfloat16
MESH = pl.DeviceIdType.MESH

SEQ = 2048
D_MODEL = 1024
HEAD_DIM = 64
D_ATTN = 512
D_RWKV = 512
D_KV = 128
N_Q_HEADS = 8
N_KV_HEADS = 2
Q_PER_KV = 4
BLOCK = 128
N_BUCKETS = 32
MAX_DISTANCE = 128
LORA_DECAY = 64
LORA_ICLR = 64
LORA_GATE = 128
RWKV_COLS = 3 * D_RWKV + LORA_DECAY + LORA_ICLR + LORA_GATE
P_OFF = D_ATTN + 2 * D_KV
D_IN = P_OFF + RWKV_COLS
D_FF = 4096
NORM_EPS = 1e-6
GN_EPS = 64e-5
NEG_INF = -1e30
N_CHIPS = 4
N_DEV = 8
HEAD_SHIFT = HEAD_DIM.bit_length() - 1
BLOCK_SHIFT = BLOCK.bit_length() - 1

ADAM_LR = 0.001
ADAM_B1 = 0.9
ADAM_B2 = 0.999
ADAM_EPS = 1e-08
ADAM_WD = 0.01
ADAM_STEP = 10

VMEM_LIMIT = 52 * 1024 * 1024
LANES = 128


def _cp(sem=None, vmem=VMEM_LIMIT):
    kw = dict(vmem_limit_bytes=vmem)
    if sem is not None:
        kw["dimension_semantics"] = sem
    return pltpu.CompilerParams(**kw)


def _rows(tr, nc):
    return pl.BlockSpec((tr, nc), lambda i: (i, 0))


def _const(shape):
    return pl.BlockSpec(shape, lambda *_: (0,) * len(shape))


ANY = pl.BlockSpec(memory_space=pl.ANY)


def _split(x, n):
    parts = []
    for _ in range(n - 1):
        h = x.astype(BF16)
        parts.append(h)
        x = x - h.astype(F32)
    parts.append(x.astype(BF16))
    return parts


NN = (((1,), (0,)), ((), ()))
NT = (((1,), (1,)), ((), ()))
TN = (((0,), (0,)), ((), ()))


def _dot(a, b, dn=NN):
    return lax.dot_general(a, b, dn, preferred_element_type=F32)


def _dot_ind(x, ind_bf16, n=3):
    acc = None
    for part in _split(x, n):
        t = _dot(part, ind_bf16)
        acc = t if acc is None else acc + t
    return acc


def _head_ones(n):
    r = lax.broadcasted_iota(jnp.int32, (n, n), 0) >> HEAD_SHIFT
    c = lax.broadcasted_iota(jnp.int32, (n, n), 1) >> HEAD_SHIFT
    return jnp.where(r == c, 1.0, 0.0).astype(BF16)


def _matmul(a, b, mode, name, *, m, n, k, tm, tn, a_spec=None, b_spec=None, out=None):
    dn = {"nn": NN, "nt": NT, "tn": TN}[mode]

    def body(a_ref, b_ref, o_ref):
        o_ref[...] = _dot(a_ref[...], b_ref[...], dn)

    if a_spec is None:
        a_spec = pl.BlockSpec((k, tm), lambda i, j: (0, i)) if mode == "tn" else pl.BlockSpec((tm, k), lambda i, j: (i, 0))
    if b_spec is None:
        b_spec = pl.BlockSpec((tn, k), lambda i, j: (j, 0)) if mode == "nt" else pl.BlockSpec((k, tn), lambda i, j: (0, j))
    return pl.pallas_call(
        body, name=name, grid=(m // tm, n // tn),
        in_specs=[a_spec, b_spec],
        out_specs=pl.BlockSpec((tm, tn), lambda i, j: (i, j)) if out is None else out[1],
        out_shape=jax.ShapeDtypeStruct((m, n) if out is None else out[0], F32),
        compiler_params=_cp(("parallel", "parallel")),
    )(a, b)


def _matmul_nt_shards(a, b_st, name, *, m, n, tm, tn, a_spec, a_piece):
    ks = b_st.shape[2]

    def body(a_ref, b_ref, o_ref):
        acc = _dot(a_piece(a_ref, 0), b_ref[0], NT)
        for s in range(1, N_CHIPS):
            acc = acc + _dot(a_piece(a_ref, s), b_ref[s], NT)
        o_ref[...] = acc

    return pl.pallas_call(
        body, name=name, grid=(m // tm, n // tn),
        in_specs=[a_spec, pl.BlockSpec((N_CHIPS, tn, ks), lambda i, j: (0, j, 0))],
        out_specs=pl.BlockSpec((tm, tn), lambda i, j: (i, j)),
        out_shape=jax.ShapeDtypeStruct((m, n), F32),
        compiler_params=_cp(("parallel", "parallel")),
    )(a, b_st)


def _rstd(x):
    return lax.rsqrt(jnp.mean(x * x, axis=-1, keepdims=True) + NORM_EPS)


def _rms_bwd(x, r, g, dy):
    gy = dy * g
    return r * gy - x * ((r * r * r) * (jnp.sum(x * gy, axis=-1, keepdims=True) / x.shape[-1]))


TR = 256
TRN = 512


def _norm_cast(x, g, name):
    def body(x_ref, g_ref, h_ref):
        x = x_ref[...]
        h_ref[...] = (x * _rstd(x) * g_ref[...]).astype(BF16)

    return pl.pallas_call(
        body, name=name, grid=(SEQ // TRN,),
        in_specs=[_rows(TRN, D_MODEL), _const((1, D_MODEL))],
        out_specs=_rows(TRN, D_MODEL),
        out_shape=jax.ShapeDtypeStruct((SEQ, D_MODEL), BF16),
        compiler_params=_cp(("parallel",)),
    )(x, g)


def _mix_norm(x, mix, g2, g3):
    def body(x_ref, mix_ref, g2_ref, g3_ref, x2_ref, h3_ref):
        mixv = mix_ref[...]
        x2 = x_ref[...] + mixv * _rstd(mixv) * g2_ref[...]
        x2_ref[...] = x2
        h3_ref[...] = (x2 * _rstd(x2) * g3_ref[...]).astype(BF16)

    return pl.pallas_call(
        body, name="mix_norm", grid=(SEQ // TRN,),
        in_specs=[_rows(TRN, D_MODEL), _rows(TRN, D_MODEL), _const((1, D_MODEL)), _const((1, D_MODEL))],
        out_specs=[_rows(TRN, D_MODEL), _rows(TRN, D_MODEL)],
        out_shape=[jax.ShapeDtypeStruct((SEQ, D_MODEL), F32), jax.ShapeDtypeStruct((SEQ, D_MODEL), BF16)],
        compiler_params=_cp(("parallel",)),
    )(x, mix, g2, g3)


def _loss_head(x2, f, g4, target):
    def body(x2_ref, f_ref, g4_ref, t_ref, loss_ref, dy_ref, df_ref, dg_ref):
        i = pl.program_id(0)
        f = f_ref[...]
        g4 = g4_ref[...]
        r = _rstd(f)
        e = x2_ref[...] + f * r * g4 - t_ref[...]
        dy = e * (1.0 / D_MODEL)
        dy_ref[...] = dy
        df_ref[...] = _rms_bwd(f, r, g4, dy).astype(BF16)
        part = 0.5 * jnp.sum(jnp.sum(e * e, axis=-1, keepdims=True), axis=0, keepdims=True) * (1.0 / D_MODEL)
        dg = jnp.sum(dy * f * r, axis=0, keepdims=True)

        @pl.when(i == 0)
        def _():
            loss_ref[...] = jnp.zeros_like(loss_ref)
            dg_ref[...] = jnp.zeros_like(dg_ref)

        loss_ref[...] += jnp.broadcast_to(part, loss_ref.shape)
        dg_ref[...] += dg

    return pl.pallas_call(
        body, name="loss_head", grid=(SEQ // TRN,),
        in_specs=[_rows(TRN, D_MODEL), _rows(TRN, D_MODEL), _const((1, D_MODEL)), _rows(TRN, D_MODEL)],
        out_specs=[_const((8, LANES)), _rows(TRN, D_MODEL), _rows(TRN, D_MODEL), _const((1, D_MODEL))],
        out_shape=[jax.ShapeDtypeStruct((8, LANES), F32), jax.ShapeDtypeStruct((SEQ, D_MODEL), F32),
                   jax.ShapeDtypeStruct((SEQ, D_MODEL), BF16), jax.ShapeDtypeStruct((1, D_MODEL), F32)],
        compiler_params=_cp(("arbitrary",)),
    )(x2, f, g4, target)


def _mid_bwd(x2, mix, dy, dh3, g2, g3):
    def body(x2_ref, mix_ref, dy_ref, dh3_ref, g2_ref, g3_ref, dx2_ref, dmix_ref, dg2_ref, dg3_ref):
        i = pl.program_id(0)
        x2 = x2_ref[...]
        mixv = mix_ref[...]
        dh3 = dh3_ref[...]
        r3 = _rstd(x2)
        dx2 = dy_ref[...] + _rms_bwd(x2, r3, g3_ref[...], dh3)
        dx2_ref[...] = dx2
        r2 = _rstd(mixv)
        dmix_ref[...] = _rms_bwd(mixv, r2, g2_ref[...], dx2).astype(BF16)

        @pl.when(i == 0)
        def _():
            dg2_ref[...] = jnp.zeros_like(dg2_ref)
            dg3_ref[...] = jnp.zeros_like(dg3_ref)

        dg3_ref[...] += jnp.sum(dh3 * x2 * r3, axis=0, keepdims=True)
        dg2_ref[...] += jnp.sum(dx2 * mixv * r2, axis=0, keepdims=True)

    return pl.pallas_call(
        body, name="mid_bwd", grid=(SEQ // TRN,),
        in_specs=[_rows(TRN, D_MODEL)] * 4 + [_const((1, D_MODEL))] * 2,
        out_specs=[_rows(TRN, D_MODEL), _rows(TRN, D_MODEL), _const((1, D_MODEL)), _const((1, D_MODEL))],
        out_shape=[jax.ShapeDtypeStruct((SEQ, D_MODEL), F32), jax.ShapeDtypeStruct((SEQ, D_MODEL), BF16),
                   jax.ShapeDtypeStruct((1, D_MODEL), F32), jax.ShapeDtypeStruct((1, D_MODEL), F32)],
        compiler_params=_cp(("arbitrary",)),
    )(x2, mix, dy, dh3, g2, g3)


def _first_bwd(x, dx2, dh1, g1):
    def body(x_ref, dx2_ref, dh1_ref, g1_ref, dx_ref, dg1_ref):
        i = pl.program_id(0)
        x = x_ref[...]
        dh1 = dh1_ref[...]
        r = _rstd(x)
        dx_ref[...] = dx2_ref[...] + _rms_bwd(x, r, g1_ref[...], dh1)

        @pl.when(i == 0)
        def _():
            dg1_ref[...] = jnp.zeros_like(dg1_ref)

        dg1_ref[...] += jnp.sum(dh1 * x * r, axis=0, keepdims=True)

    return pl.pallas_call(
        body, name="first_bwd", grid=(SEQ // TRN,),
        in_specs=[_rows(TRN, D_MODEL)] * 3 + [_const((1, D_MODEL))],
        out_specs=[_rows(TRN, D_MODEL), _const((1, D_MODEL))],
        out_shape=[jax.ShapeDtypeStruct((SEQ, D_MODEL), F32), jax.ShapeDtypeStruct((1, D_MODEL), F32)],
        compiler_params=_cp(("arbitrary",)),
    )(x, dx2, dh1, g1)


TC = 256
N_CB = D_FF // TC
GELU_C = math.sqrt(2.0 / math.pi)


def _shift_down(u, s):
    rolled = pltpu.roll(u, s, 0)
    row = lax.broadcasted_iota(jnp.int32, u.shape, 0)
    return jnp.where(row >= s, rolled, 0.0)


def _shift_up(u, s):
    n = u.shape[0]
    rolled = pltpu.roll(u, n - s, 0)
    row = lax.broadcasted_iota(jnp.int32, u.shape, 0)
    return jnp.where(row < n - s, rolled, 0.0)


def _conv3(u, w, b):
    return b + w[0:1] * _shift_down(u, 2) + w[1:2] * _shift_down(u, 1) + w[2:3] * u


def _gelu_and_grad(x):
    inner = GELU_C * (x + 0.044715 * (x * x * x))
    t = jnp.tanh(inner)
    gelu = 0.5 * x * (1.0 + t)
    dgelu = 0.5 * (1.0 + t) + 0.5 * x * (1.0 - t * t) * (GELU_C * (1.0 + 3 * 0.044715 * (x * x)))
    return gelu, dgelu


def _ffn_specs():
    col = lambda off: pl.BlockSpec((SEQ, TC), lambda *g: (0, g[-1] + off))
    w = lambda off: pl.BlockSpec((3, TC), lambda *g: (0, g[-1] + off))
    b = lambda off: pl.BlockSpec((1, TC), lambda *g: (0, g[-1] + off))
    return col, w, b


def _ffn_up_act(h3, wup_st, conv_w, conv_b):
    col, w, b = _ffn_specs()
    per_shard = wup_st.shape[2] // TC

    def body(h_ref, upg_ref, upv_ref, wg_ref, wv_ref, bg_ref, bv_ref, ug_ref, uv_ref, gate_ref, val_ref, act_ref):
        h = h_ref[...]
        ug = _dot(h, upg_ref[...])
        uv = _dot(h, upv_ref[...])
        ug_ref[...] = ug
        uv_ref[...] = uv
        gate = _conv3(ug, wg_ref[...], bg_ref[...])
        val = _conv3(uv, wv_ref[...], bv_ref[...])
        gate_ref[...] = gate
        val_ref[...] = val
        act_ref[...] = (_gelu_and_grad(gate)[0] * val).astype(BF16)

    return pl.pallas_call(
        body, name="ffn_up_act", grid=(N_CB,),
        in_specs=[_const((SEQ, D_MODEL)),
                  pl.BlockSpec((None, D_MODEL, TC), lambda j: (j // per_shard, 0, j % per_shard)),
                  pl.BlockSpec((None, D_MODEL, TC), lambda j: (2 + j // per_shard, 0, j % per_shard)),
                  w(0), w(N_CB), b(0), b(N_CB)],
        out_specs=[col(0)] * 5,
        out_shape=[jax.ShapeDtypeStruct((SEQ, D_FF), F32)] * 4 + [jax.ShapeDtypeStruct((SEQ, D_FF), BF16)],
        compiler_params=_cp(("parallel",)),
    )(h3, wup_st, wup_st, conv_w, conv_w, conv_b, conv_b)


def _ffn_act_bwd(u_gate, u_val, gate, val, df, wdown, conv_w):
    col, w, _ = _ffn_specs()
    both = lambda rows: pl.BlockSpec((2, rows, TC), lambda j: (0, 0, j))

    def body(ug_ref, uv_ref, gate_ref, val_ref, df_ref, wd_ref, wg_ref, wv_ref, du_ref, dw_ref, db_ref):
        da = _dot(df_ref[...], wd_ref[...], NT)
        gelu, dgelu = _gelu_and_grad(gate_ref[...])
        halves = ((da * val_ref[...] * dgelu, ug_ref, wg_ref[...]), (da * gelu, uv_ref, wv_ref[...]))
        for h, (duc, u_ref, wh) in enumerate(halves):
            uh = u_ref[...]
            up1, up2 = _shift_up(duc, 1), _shift_up(duc, 2)
            du_ref[h] = (wh[2:3] * duc + wh[1:2] * up1 + wh[0:1] * up2).astype(BF16)
            db_ref[h] = jnp.sum(duc, axis=0, keepdims=True)
            dw_ref[h] = jnp.concatenate(
                [jnp.sum(up2 * uh, axis=0, keepdims=True), jnp.sum(up1 * uh, axis=0, keepdims=True),
                 jnp.sum(duc * uh, axis=0, keepdims=True)], axis=0)

    return pl.pallas_call(
        body, name="ffn_act_bwd", grid=(N_CB,),
        in_specs=[col(0)] * 4 + [_const((SEQ, D_MODEL)), pl.BlockSpec((TC, D_MODEL), lambda j: (j, 0)), w(0), w(N_CB)],
        out_specs=[both(SEQ), both(3), both(1)],
        out_shape=[jax.ShapeDtypeStruct((2, SEQ, D_FF), BF16), jax.ShapeDtypeStruct((2, 3, D_FF), F32),
                   jax.ShapeDtypeStruct((2, 1, D_FF), F32)],
        compiler_params=_cp(("parallel",)),
    )(u_gate, u_val, gate, val, df, wdown, conv_w, conv_w)


def _t5_onehot():
    rel = (np.arange(BLOCK)[:, None] + BLOCK) - np.arange(2 * BLOCK)[None, :]
    n = np.maximum(rel, 0)
    max_exact = N_BUCKETS // 2
    large = max_exact + (np.log(np.maximum(n, 1).astype(np.float32) / np.float32(max_exact))
                         / np.float32(math.log(MAX_DISTANCE / max_exact))
                         * np.float32(N_BUCKETS - max_exact)).astype(np.int32)
    large = np.minimum(large, N_BUCKETS - 1)
    bucket = np.where(n < max_exact, n, large).reshape(-1)
    return (bucket[None, :] == np.arange(N_BUCKETS)[:, None]).astype(np.float32)


N_REL = BLOCK * 2 * BLOCK


def _bias_table(rel_bias_t, onehot):
    def body(rb_ref, oh_ref, o_ref):
        o_ref[...] = _dot_ind(rb_ref[...], oh_ref[...])

    return pl.pallas_call(
        body, name="bias_table", grid=(1,),
        in_specs=[_const((N_Q_HEADS, N_BUCKETS)), _const((N_BUCKETS, N_REL))],
        out_specs=_const((N_Q_HEADS, N_REL)),
        out_shape=jax.ShapeDtypeStruct((N_Q_HEADS, N_REL), F32),
        compiler_params=_cp(("arbitrary",)),
    )(rel_bias_t, onehot)


def _bias_table_bwd(dbias, onehot):
    def body(db_ref, oh_ref, o_ref):
        acc = None
        for part in _split(db_ref[...], 3):
            t = _dot(part, oh_ref[...], NT)
            acc = t if acc is None else acc + t
        o_ref[...] = acc

    return pl.pallas_call(
        body, name="bias_table_bwd", grid=(1,),
        in_specs=[_const((N_Q_HEADS, N_REL)), _const((N_BUCKETS, N_REL))],
        out_specs=_const((N_Q_HEADS, N_BUCKETS)),
        out_shape=jax.ShapeDtypeStruct((N_Q_HEADS, N_BUCKETS), F32),
        compiler_params=_cp(("arbitrary",)),
    )(dbias, onehot)


def _attn_pieces(n, q, kvp, kvc, bias_ref, sinks_ref, hk):
    qi = lax.broadcasted_iota(jnp.int32, (BLOCK, 2 * BLOCK), 0)
    kj = lax.broadcasted_iota(jnp.int32, (BLOCK, 2 * BLOCK), 1)
    rel = qi + BLOCK - kj
    first_key = jnp.where(n > 0, 0, BLOCK)
    ok = jnp.where(rel >= 0, jnp.where(rel < BLOCK, jnp.where(kj >= first_key, 1.0, 0.0), 0.0), 0.0)
    ok4 = jnp.concatenate([ok] * Q_PER_KV, axis=0) > 0.5
    c0 = hk * HEAD_DIM
    kcat = jnp.concatenate([kvp[:, c0:c0 + HEAD_DIM], kvc[:, c0:c0 + HEAD_DIM]], axis=0).astype(BF16)
    vcat = jnp.concatenate([kvp[:, D_KV + c0:D_KV + c0 + HEAD_DIM], kvc[:, D_KV + c0:D_KV + c0 + HEAD_DIM]],
                           axis=0).astype(BF16)
    q0 = hk * Q_PER_KV * HEAD_DIM
    qs = jnp.concatenate([q[:, q0 + g * HEAD_DIM:q0 + (g + 1) * HEAD_DIM] for g in range(Q_PER_KV)],
                         axis=0).astype(BF16)
    s = _dot(qs, kcat, NT) * (HEAD_DIM ** -0.5) + bias_ref[hk]
    s = jnp.where(ok4, s, NEG_INF)
    row = lax.broadcasted_iota(jnp.int32, (Q_PER_KV * BLOCK, 1), 0)
    sink = jnp.zeros((Q_PER_KV * BLOCK, 1), F32)
    for g in range(Q_PER_KV):
        sink = jnp.where((row >> BLOCK_SHIFT) == g, sinks_ref[hk * Q_PER_KV + g], sink)
    m = jnp.maximum(jnp.max(s, axis=-1, keepdims=True), sink)
    p = jnp.exp(s - m)
    es = jnp.exp(sink - m)
    inv = 1.0 / (jnp.sum(p, axis=-1, keepdims=True) + es)
    return qs, kcat, vcat, p * inv, es * inv


def _attn_in_specs():
    return [pl.BlockSpec((BLOCK, D_ATTN), lambda n: (n, 0)),
            pl.BlockSpec((BLOCK, 2 * D_KV), lambda n: (jnp.maximum(n - 1, 0), D_ATTN // (2 * D_KV))),
            pl.BlockSpec((BLOCK, 2 * D_KV), lambda n: (n, D_ATTN // (2 * D_KV))),
            _const((N_KV_HEADS, Q_PER_KV * BLOCK, 2 * BLOCK)),
            pl.BlockSpec(memory_space=pltpu.SMEM)]


def _unstack_heads(t):
    return jnp.concatenate([t[g * BLOCK:(g + 1) * BLOCK] for g in range(Q_PER_KV)], axis=1)


def _attn_fwd(proj, bias, sinks):
    def body(q_ref, kvp_ref, kvc_ref, bias_ref, sinks_ref, o_ref):
        n = pl.program_id(0)
        q, kvp, kvc = q_ref[...], kvp_ref[...], kvc_ref[...]
        pieces = [_attn_pieces(n, q, kvp, kvc, bias_ref, sinks_ref, hk) for hk in range(N_KV_HEADS)]
        outs = [_unstack_heads(_dot(probs.astype(BF16), vcat)) for _, _, vcat, probs, _ in pieces]
        o_ref[...] = jnp.concatenate(outs, axis=1)

    return pl.pallas_call(
        body, name="attn_fwd", grid=(SEQ // BLOCK,),
        in_specs=_attn_in_specs(),
        out_specs=pl.BlockSpec((BLOCK, D_ATTN), lambda n: (n, 0)),
        out_shape=jax.ShapeDtypeStruct((SEQ, D_ATTN), F32),
        compiler_params=_cp(("parallel",)),
    )(proj, proj, proj, bias, sinks)


def _attn_bwd(proj, bias, sinks, dcat):
    nb = SEQ // BLOCK

    def body(q_ref, kvp_ref, kvc_ref, bias_ref, sinks_ref, do_ref, dq_ref, dkv_ref, dbias_ref, dsink_ref, dsacc):
        n = pl.program_id(0)

        @pl.when(n == 0)
        def _():
            dkv_ref[...] = jnp.zeros_like(dkv_ref)
            dbias_ref[...] = jnp.zeros_like(dbias_ref)
            dsacc[...] = jnp.zeros_like(dsacc)

        q, kvp, kvc = q_ref[...], kvp_ref[...], kvc_ref[...]
        do_all = do_ref[...]
        heads = range(N_KV_HEADS)
        pieces = [_attn_pieces(n, q, kvp, kvc, bias_ref, sinks_ref, hk) for hk in heads]
        dos, dprobs, dvs = [], [], []
        for hk in heads:
            _, _, vcat, probs, _ = pieces[hk]
            q0 = hk * Q_PER_KV * HEAD_DIM
            do = jnp.concatenate([do_all[:, q0 + g * HEAD_DIM:q0 + (g + 1) * HEAD_DIM] for g in range(Q_PER_KV)],
                                 axis=0).astype(BF16)
            dprobs.append(_dot(do, vcat, NT))
            dvs.append(_dot(probs.astype(BF16), do, TN))
        dsbs = []
        for hk in heads:
            _, _, _, probs, psink = pieces[hk]
            rowdot = jnp.sum(probs * dprobs[hk], axis=-1, keepdims=True)
            ds = probs * (dprobs[hk] - rowdot)
            dsacc[hk] += -psink * rowdot
            dbias_ref[hk] += ds
            dsbs.append((ds * (HEAD_DIM ** -0.5)).astype(BF16))
        dqs = [_unstack_heads(_dot(dsbs[hk], pieces[hk][1])) for hk in heads]
        dks = [_dot(dsbs[hk], pieces[hk][0], TN) for hk in heads]
        dq_ref[...] = jnp.concatenate(dqs, axis=1)
        upd = jnp.concatenate(dks + dvs, axis=1)
        cur = pl.multiple_of(n * BLOCK, BLOCK)
        dkv_ref[pl.ds(cur, BLOCK), :] += upd[BLOCK:]

        @pl.when(n > 0)
        def _():
            prev = pl.multiple_of((n - 1) * BLOCK, BLOCK)
            dkv_ref[pl.ds(prev, BLOCK), :] += upd[:BLOCK]

        @pl.when(n == nb - 1)
        def _():
            for hk in range(N_KV_HEADS):
                for g in range(Q_PER_KV):
                    tot = jnp.sum(dsacc[hk, g * BLOCK:(g + 1) * BLOCK, :], axis=0, keepdims=True)
                    h = hk * Q_PER_KV + g
                    dsink_ref[h:h + 1, :] = jnp.broadcast_to(tot, (1, LANES))

    return pl.pallas_call(
        body, name="attn_bwd", grid=(nb,),
        in_specs=_attn_in_specs() + [pl.BlockSpec((BLOCK, D_ATTN), lambda n: (n, 0))],
        out_specs=[pl.BlockSpec((BLOCK, D_ATTN), lambda n: (n, 0)), _const((SEQ, 2 * D_KV)),
                   _const((N_KV_HEADS, Q_PER_KV * BLOCK, 2 * BLOCK)), _const((N_Q_HEADS, LANES))],
        out_shape=[jax.ShapeDtypeStruct((SEQ, D_ATTN), F32), jax.ShapeDtypeStruct((SEQ, 2 * D_KV), F32),
                   jax.ShapeDtypeStruct((N_KV_HEADS, Q_PER_KV * BLOCK, 2 * BLOCK), F32),
                   jax.ShapeDtypeStruct((N_Q_HEADS, LANES), F32)],
        scratch_shapes=[pltpu.VMEM((N_KV_HEADS, Q_PER_KV * BLOCK, 1), F32)],
        compiler_params=_cp(("arbitrary",)),
    )(proj, proj, proj, bias, sinks, dcat)


@jax.custom_vjp
def _head_sum(x):
    ones = _head_ones(LANES)
    return jnp.concatenate([_dot_ind(x[:, c:c + LANES], ones, 2) for c in range(0, x.shape[-1], LANES)], axis=1)


_head_sum.defvjp(lambda x: (_head_sum(x), None), lambda _, ct: (_head_sum(ct),))


@jax.custom_vjp
def _bdot(a, w):
    return _dot(a.astype(BF16), w.astype(BF16))


def _bdot_bwd(res, ct):
    a, w = res
    ctb = ct.astype(BF16)
    return _dot(ctb, w.astype(BF16), NT), _dot(a.astype(BF16), ctb, TN)


_bdot.defvjp(lambda a, w: (_bdot(a, w), (a, w)), _bdot_bwd)


def _sigmoid(x):
    return 0.5 * (jnp.tanh(0.5 * x) + 1.0)


def _softplus(x):
    return jnp.maximum(x, 0.0) + jnp.log(1.0 + jnp.exp(-jnp.abs(x)))


def _rwkv_core(r, k, v, zwa, zg, w0, wdu, a0, wiu, wgu, k_k, k_a):
    w_log = -_softplus(-(w0 + _bdot(jnp.tanh(zwa), wdu))) - 0.5
    decay = jnp.exp(-jnp.exp(w_log))
    a = _sigmoid(a0 + _bdot(zwa, wiu))
    g = _bdot(_sigmoid(zg), wgu)
    kk = k * k_k
    kk = kk / jnp.maximum(jnp.sqrt(_head_sum(kk * kk)), 1e-12)
    k2 = k * (1.0 + (a - 1.0) * k_a)
    return r, decay, k2, v, -kk, kk * a, g


def _rwkv_out(o, r, k2, v, g, lng, lnb, rk):
    mu = _head_sum(o) * (1.0 / HEAD_DIM)
    d = o - mu
    var = _head_sum(d * d) * (1.0 / HEAD_DIM)
    on = d * lax.rsqrt(var + GN_EPS) * lng + lnb
    bonus = _head_sum(r * k2 * rk) * v
    return (on + bonus) * g


P_SPLITS = (0, 512, 1024, 1536, 1664, 1792)
N_PREP_PARAMS = 7
HALO = 8


def _shifted_pieces(i, p_ref, halo_ref, mix_ref):
    p = p_ref[:, P_OFF:]
    prev_row = halo_ref[HALO - 1:HALO, P_OFF:] * jnp.where(i > 0, 1.0, 0.0)
    row = lax.broadcasted_iota(jnp.int32, p.shape, 0)
    pprev = jnp.where(row == 0, prev_row, pltpu.roll(p, 1, 0))
    delta = pprev - p
    ps = p + delta * mix_ref[...]
    return [ps[:, a:b] for a, b in zip(P_SPLITS[:-1], P_SPLITS[1:])], delta


def _prep_in_specs():
    return [_rows(TR, D_IN),
            pl.BlockSpec((HALO, D_IN), lambda i: (jnp.maximum(i * (TR // HALO) - 1, 0), 0)),
            _const((1, RWKV_COLS)), _const((1, D_RWKV)), _const((LANES, D_RWKV)), _const((1, D_RWKV)),
            _const((LANES, D_RWKV)), _const((LANES, D_RWKV)), _const((1, D_RWKV)), _const((1, D_RWKV))]


def _rwkv_prep(proj, mix, prm):
    def body(p_ref, halo_ref, mix_ref, *refs):
        prm_refs, outs = refs[:N_PREP_PARAMS], refs[N_PREP_PARAMS:]
        pieces, _ = _shifted_pieces(pl.program_id(0), p_ref, halo_ref, mix_ref)
        vals = _rwkv_core(*pieces, *[t[...] for t in prm_refs])
        for ref, val in zip(outs, vals):
            ref[...] = val

    return pl.pallas_call(
        body, name="rwkv_prep", grid=(SEQ // TR,),
        in_specs=_prep_in_specs(),
        out_specs=[_rows(TR, D_RWKV)] * 7,
        out_shape=[jax.ShapeDtypeStruct((SEQ, D_RWKV), F32)] * 7,
        compiler_params=_cp(("parallel",)),
    )(proj, proj, mix, *prm)


def _rwkv_prep_bwd(proj, mix, prm, cts):
    def body(p_ref, halo_ref, mix_ref, *refs):
        i = pl.program_id(0)
        prm_refs = refs[:N_PREP_PARAMS]
        ct_refs = refs[N_PREP_PARAMS:N_PREP_PARAMS + 10]
        dps_ref, dmix_ref = refs[N_PREP_PARAMS + 10:N_PREP_PARAMS + 12]
        dprm_refs = refs[N_PREP_PARAMS + 12:]
        pieces, delta = _shifted_pieces(i, p_ref, halo_ref, mix_ref)
        _, vjp = jax.vjp(_rwkv_core, *pieces, *[t[...] for t in prm_refs])
        dr1, dr2, dw, dk1, dk2, dv1, dv2, dkkn, db, dg = [t[...] for t in ct_refs]
        grads = vjp((dr1 + dr2, dw, dk1 + dk2, dv1 + dv2, dkkn, db, dg))
        dps = jnp.concatenate(grads[:5], axis=1)
        dps_ref[...] = dps

        @pl.when(i == 0)
        def _():
            dmix_ref[...] = jnp.zeros_like(dmix_ref)
            for ref in dprm_refs:
                ref[...] = jnp.zeros_like(ref)

        dmix_ref[...] += jnp.sum(dps * delta, axis=0, keepdims=True)
        for ref, gval in zip(dprm_refs, grads[5:]):
            ref[...] += gval

    prm_shapes = [(1, D_RWKV), (LANES, D_RWKV), (1, D_RWKV), (LANES, D_RWKV), (LANES, D_RWKV), (1, D_RWKV), (1, D_RWKV)]
    return pl.pallas_call(
        body, name="rwkv_prep_bwd", grid=(SEQ // TR,),
        in_specs=_prep_in_specs() + [_rows(TR, D_RWKV)] * 10,
        out_specs=[_rows(TR, RWKV_COLS), _const((1, RWKV_COLS))] + [_const(s) for s in prm_shapes],
        out_shape=[jax.ShapeDtypeStruct((SEQ, RWKV_COLS), F32), jax.ShapeDtypeStruct((1, RWKV_COLS), F32)]
        + [jax.ShapeDtypeStruct(s, F32) for s in prm_shapes],
        compiler_params=_cp(("arbitrary",)),
    )(proj, proj, mix, *prm, *cts)


def _rwkv_post(o, r, k2, v, g, lng, lnb, rk, attn):
    def body(o_ref, r_ref, k_ref, v_ref, g_ref, lng_ref, lnb_ref, rk_ref, attn_ref, cat_ref):
        rw = _rwkv_out(*[t[...] for t in (o_ref, r_ref, k_ref, v_ref, g_ref, lng_ref, lnb_ref, rk_ref)])
        cat_ref[...] = jnp.concatenate([attn_ref[...], rw], axis=1).astype(BF16)

    return pl.pallas_call(
        body, name="rwkv_post", grid=(SEQ // TR,),
        in_specs=[_rows(TR, D_RWKV)] * 5 + [_const((1, D_RWKV))] * 3 + [_rows(TR, D_ATTN)],
        out_specs=_rows(TR, D_MODEL),
        out_shape=jax.ShapeDtypeStruct((SEQ, D_MODEL), BF16),
        compiler_params=_cp(("parallel",)),
    )(o, r, k2, v, g, lng, lnb, rk, attn)


def _rwkv_post_bwd(o, r, k2, v, g, lng, lnb, rk, dcat):
    def body(o_ref, r_ref, k_ref, v_ref, g_ref, lng_ref, lnb_ref, rk_ref, dcat_ref,
             do_ref, dr_ref, dk_ref, dv_ref, dg_ref, dlng_ref, dlnb_ref, drk_ref):
        i = pl.program_id(0)
        args = [t[...] for t in (o_ref, r_ref, k_ref, v_ref, g_ref, lng_ref, lnb_ref, rk_ref)]
        _, vjp = jax.vjp(_rwkv_out, *args)
        grads = vjp(dcat_ref[:, D_ATTN:])
        for ref, gval in zip((do_ref, dr_ref, dk_ref, dv_ref, dg_ref), grads[:5]):
            ref[...] = gval

        @pl.when(i == 0)
        def _():
            for ref in (dlng_ref, dlnb_ref, drk_ref):
                ref[...] = jnp.zeros_like(ref)

        for ref, gval in zip((dlng_ref, dlnb_ref, drk_ref), grads[5:]):
            ref[...] += gval

    return pl.pallas_call(
        body, name="rwkv_post_bwd", grid=(SEQ // TR,),
        in_specs=[_rows(TR, D_RWKV)] * 5 + [_const((1, D_RWKV))] * 3 + [_rows(TR, D_MODEL)],
        out_specs=[_rows(TR, D_RWKV)] * 5 + [_const((1, D_RWKV))] * 3,
        out_shape=[jax.ShapeDtypeStruct((SEQ, D_RWKV), F32)] * 5 + [jax.ShapeDtypeStruct((1, D_RWKV), F32)] * 3,
        compiler_params=_cp(("arbitrary",)),
    )(o, r, k2, v, g, lng, lnb, rk, dcat)


def _assemble_dproj(dq, dkv, dps, mix):
    last = SEQ // HALO - 1

    def body(dq_ref, dkv_ref, dps_ref, nxt_ref, mix_ref, o_ref):
        i = pl.program_id(0)
        dps = dps_ref[...]
        mixv = mix_ref[...]
        nxt_row = nxt_ref[0:1, :] * jnp.where(i < SEQ // TR - 1, 1.0, 0.0)
        row = lax.broadcasted_iota(jnp.int32, dps.shape, 0)
        up = jnp.where(row == TR - 1, nxt_row, pltpu.roll(dps, TR - 1, 0))
        dp = dps * (1.0 - mixv) + up * mixv
        o_ref[...] = jnp.concatenate([dq_ref[...], dkv_ref[...], dp], axis=1).astype(BF16)

    return pl.pallas_call(
        body, name="assemble_dproj", grid=(SEQ // TR,),
        in_specs=[_rows(TR, D_ATTN), _rows(TR, 2 * D_KV), _rows(TR, RWKV_COLS),
                  pl.BlockSpec((HALO, RWKV_COLS), lambda i: (jnp.minimum((i + 1) * (TR // HALO), last), 0)),
                  _const((1, RWKV_COLS))],
        out_specs=_rows(TR, D_IN),
        out_shape=jax.ShapeDtypeStruct((SEQ, D_IN), BF16),
        compiler_params=_cp(("parallel",)),
    )(dq, dkv, dps, dps, mix)


N_PAIR = D_RWKV // LANES
CHUNK = 64
N_CHUNK = SEQ // CHUNK
GROUP = 64
STATE = (N_PAIR, HEAD_DIM, LANES)


def _lane_sums(lhs_tiles, ones2):
    out = _dot(jnp.concatenate(lhs_tiles, axis=0), ones2)
    return [out[i * HEAD_DIM:(i + 1) * HEAD_DIM] for i in range(len(lhs_tiles))]


def _seg_sum(xs, ones2):
    return _lane_sums([jnp.concatenate(_split(x, 2), axis=1) for x in xs], ones2)


def _seg_sum_rows(xs, ones2):
    out = _dot(jnp.concatenate(_split(jnp.concatenate(xs, axis=0), 2), axis=1), ones2)
    return [out[i * GROUP:(i + 1) * GROUP] for i in range(len(xs))]


def _col_form(rows, diag, ones2):
    zero = jnp.zeros((HEAD_DIM, LANES), BF16)
    tiles = []
    for row in rows:
        hi = row.astype(BF16)
        lo = (row - hi.astype(F32)).astype(BF16)
        tiles.append(jnp.concatenate(
            [jnp.where(diag, jnp.broadcast_to(part, (HEAD_DIM, LANES)), zero) for part in (hi, lo)], axis=1))
    return _lane_sums(tiles, ones2)


def _scan_consts():
    ones2 = jnp.concatenate([_head_ones(LANES)] * 2, axis=0)
    sub = lax.broadcasted_iota(jnp.int32, (HEAD_DIM, LANES), 0)
    lane_in_head = lax.broadcasted_iota(jnp.int32, (HEAD_DIM, LANES), 1) & (HEAD_DIM - 1)
    return ones2, lane_in_head == sub, lane_in_head


def _rows_of_columns(tile):
    t = tile.T
    return jnp.concatenate([t[:CHUNK], t[HEAD_DIM:HEAD_DIM + CHUNK]], axis=1)


def _pair(j):
    return slice(j * LANES, (j + 1) * LANES)


def _scan_fwd(r, w, k, v, kkn, b):
    def body(r_ref, w_ref, k_ref, v_ref, kkn_ref, b_ref, o_ref, st_ref, sa_ref, s_scr):
        c = pl.program_id(0)
        ones2, diag, lane_in_head = _scan_consts()

        @pl.when(c == 0)
        def _():
            s_scr[...] = jnp.zeros_like(s_scr)

        def group(gi, carry):
            row0 = pl.multiple_of(gi * GROUP, GROUP)
            states, ocols = list(carry[:N_PAIR]), list(carry[N_PAIR:])
            tiles = [[t[pl.ds(row0, GROUP), _pair(j)] for t in (r_ref, w_ref, k_ref, v_ref, kkn_ref, b_ref)]
                     for j in range(N_PAIR)]
            def row(j, name, u):
                return tiles[j]["rwkvnb".index(name)][u:u + 1]

            def emit_out(u, after):
                outs = _seg_sum([s[j] * row(j, "r", u + d) for d, s in enumerate(after) for j in range(N_PAIR)], ones2)
                for d in range(2):
                    here = lane_in_head == gi * GROUP + u + d
                    for j in range(N_PAIR):
                        ocols[j] = jnp.where(here, outs[d * N_PAIR + j], ocols[j])

            def vcols_of(u):
                cols = _col_form([row(j, "v", u + d) for d in range(2) for j in range(N_PAIR)], diag, ones2)
                return cols[:N_PAIR], cols[N_PAIR:]

            n_next = [pltpu.roll(tiles[j][4], GROUP - 1, 0) for j in range(N_PAIR)]
            dots = _seg_sum_rows([tiles[j][5] * n_next[j] for j in range(N_PAIR)]
                                 + [tiles[j][2] * n_next[j] for j in range(N_PAIR)], ones2)
            b_n, k_n = dots[:N_PAIR], dots[N_PAIR:]
            w_n = [tiles[j][1] * n_next[j] for j in range(N_PAIR)]

            vcols = vcols_of(0)
            after = None
            for u in range(0, GROUP, 2):
                prods = _seg_sum([states[j] * row(j, "n", u) for j in range(N_PAIR)]
                                 + [states[j] * w_n[j][u:u + 1] for j in range(N_PAIR)], ones2)
                if after is not None:
                    emit_out(u - 2, after)
                nxt = vcols_of(u + 2) if u + 2 < GROUP else None
                first, second = [], []
                for j in range(N_PAIR):
                    sa1 = prods[j]
                    sa2 = prods[N_PAIR + j] + sa1 * b_n[j][u:u + 1] + vcols[0][j] * k_n[j][u:u + 1]
                    s1 = states[j] * row(j, "w", u) + sa1 * row(j, "b", u) + vcols[0][j] * row(j, "k", u)
                    s2 = s1 * row(j, "w", u + 1) + sa2 * row(j, "b", u + 1) + vcols[1][j] * row(j, "k", u + 1)
                    st_ref[row0 + u, j] = s1
                    sa_ref[row0 + u, j] = sa1
                    st_ref[row0 + u + 1, j] = s2
                    sa_ref[row0 + u + 1, j] = sa2
                    first.append(s1)
                    second.append(s2)
                    states[j] = s2
                after, vcols = (first, second), nxt
            emit_out(GROUP - 2, after)
            return tuple(states + ocols)

        zero = jnp.zeros((HEAD_DIM, LANES), F32)
        fin = lax.fori_loop(0, CHUNK // GROUP, group, tuple(s_scr[j] for j in range(N_PAIR)) + (zero,) * N_PAIR)
        for j in range(N_PAIR):
            s_scr[j] = fin[j]
            o_ref[:, _pair(j)] = _rows_of_columns(fin[N_PAIR + j])

    blk = pl.BlockSpec((CHUNK, D_RWKV), lambda c: (c, 0))
    per_step = pl.BlockSpec((CHUNK,) + STATE, lambda c: (c, 0, 0, 0))
    return pl.pallas_call(
        body, name="rwkv_scan_fwd", grid=(N_CHUNK,),
        in_specs=[blk] * 6,
        out_specs=[blk, per_step, per_step],
        out_shape=[jax.ShapeDtypeStruct((SEQ, D_RWKV), F32)] + [jax.ShapeDtypeStruct((SEQ,) + STATE, F32)] * 2,
        scratch_shapes=[pltpu.VMEM(STATE, F32)],
        compiler_params=_cp(("arbitrary",)),
    )(r, w, k, v, kkn, b)


def _scan_bwd(r, w, k, v, kkn, b, do, states, sas, ds_in, prev, name, first_chunk, n_chunks):
    top = first_chunk + n_chunks - 1

    def body(r_ref, w_ref, k_ref, v_ref, kkn_ref, b_ref, do_ref, st_ref, before_ref, sa_ref, ds_in_ref, *rest):
        dr_ref, dw_ref, dk_ref, dv_ref, dkkn_ref, db_ref, ds_out_ref, ds_scr = rest[-8:]
        i = pl.program_id(0)
        ones2, diag, lane_in_head = _scan_consts()

        @pl.when(i == 0)
        def _():
            ds_scr[...] = ds_in_ref[...]

        entry = [before_ref[0, j] * jnp.where(i < top, 1.0, 0.0) for j in range(N_PAIR)]

        def reverse(gr, carry):
            gi = CHUNK // GROUP - 1 - gr
            row0 = pl.multiple_of(gi * GROUP, GROUP)
            dstates, dvcols = list(carry[:N_PAIR]), list(carry[N_PAIR:])
            tiles = [[t[pl.ds(row0, GROUP), _pair(j)]
                      for t in (r_ref, w_ref, k_ref, v_ref, kkn_ref, b_ref, do_ref)] for j in range(N_PAIR)]
            rows = [[[None] * GROUP for _ in range(5)] for _ in range(N_PAIR)]

            def row(j, name, u):
                return tiles[j]["rwkvnbd".index(name)][u:u + 1]

            def cols_of(u):
                cols = _col_form([row(j, name, u - d) for d in range(2) for name in "dv" for j in range(N_PAIR)],
                                 diag, ones2)
                return [[(cols[(2 * d) * N_PAIR + j], cols[(2 * d + 1) * N_PAIR + j]) for j in range(N_PAIR)]
                        for d in range(2)]

            def emit_dv(u, dsps):
                outs = _seg_sum([dsp[j] * row(j, "k", u - d) for d, dsp in enumerate(dsps) for j in range(N_PAIR)], ones2)
                for d in range(2):
                    here = lane_in_head == gi * GROUP + u - d
                    for j in range(N_PAIR):
                        dvcols[j] = jnp.where(here, outs[d * N_PAIR + j], dvcols[j])

            b_prev = [pltpu.roll(tiles[j][5], 1, 0) for j in range(N_PAIR)]
            dots = _seg_sum_rows([tiles[j][4] * b_prev[j] for j in range(N_PAIR)]
                                 + [tiles[j][0] * tiles[j][5] for j in range(N_PAIR)], ones2)
            n_b, r_b = dots[:N_PAIR], dots[N_PAIR:]
            w_b = [tiles[j][1] * b_prev[j] for j in range(N_PAIR)]

            def outputs(u, j, dsp, dsa, docol, vcol):
                tl = gi * GROUP + u
                if u > 0:
                    s_prev = st_ref[tl - 1, j]
                else:
                    s_prev = jnp.where(gi == 0, entry[j], st_ref[jnp.maximum(tl - 1, 0), j])
                rows[j][0][u] = jnp.sum(st_ref[tl, j] * docol, axis=0, keepdims=True)
                rows[j][1][u] = jnp.sum(dsp * s_prev, axis=0, keepdims=True)
                rows[j][2][u] = jnp.sum(dsp * vcol, axis=0, keepdims=True)
                rows[j][3][u] = jnp.sum(s_prev * dsa, axis=0, keepdims=True)
                rows[j][4][u] = jnp.sum(dsp * sa_ref[tl, j], axis=0, keepdims=True)

            cols = cols_of(GROUP - 1)
            before = None
            for u in range(GROUP - 1, 0, -2):
                dsp1 = [dstates[j] + cols[0][j][0] * row(j, "r", u) for j in range(N_PAIR)]
                prods = _seg_sum([dsp1[j] * row(j, "b", u) for j in range(N_PAIR)]
                                 + [dsp1[j] * w_b[j][u:u + 1] for j in range(N_PAIR)], ones2)
                if before is not None:
                    emit_dv(u + 2, before)
                nxt = cols_of(u - 2) if u >= 2 else None
                dsp2 = []
                for j in range(N_PAIR):
                    dsa1 = prods[j]
                    dsa2 = prods[N_PAIR + j] + dsa1 * n_b[j][u:u + 1] + cols[1][j][0] * r_b[j][u - 1:u]
                    mid = dsp1[j] * row(j, "w", u) + dsa1 * row(j, "n", u) + cols[1][j][0] * row(j, "r", u - 1)
                    outputs(u, j, dsp1[j], dsa1, *cols[0][j])
                    outputs(u - 1, j, mid, dsa2, *cols[1][j])
                    dstates[j] = mid * row(j, "w", u - 1) + dsa2 * row(j, "n", u - 1)
                    dsp2.append(mid)
                before, cols = (dsp1, dsp2), nxt
            emit_dv(1, before)
            for j in range(N_PAIR):
                for ref, rr in zip((dr_ref, dw_ref, dk_ref, dkkn_ref, db_ref), rows[j]):
                    ref[pl.ds(row0, GROUP), _pair(j)] = jnp.concatenate(rr, axis=0)
            return tuple(dstates + dvcols)

        zero = jnp.zeros((HEAD_DIM, LANES), F32)
        dfin = lax.fori_loop(0, CHUNK // GROUP, reverse, tuple(ds_scr[j] for j in range(N_PAIR)) + (zero,) * N_PAIR)
        for j in range(N_PAIR):
            ds_scr[j] = dfin[j]
            dv_ref[:, _pair(j)] = _rows_of_columns(dfin[N_PAIR + j])

        @pl.when(i == n_chunks - 1)
        def _():
            ds_out_ref[...] = ds_scr[...]

    blk = pl.BlockSpec((CHUNK, D_RWKV), lambda i: (top - i, 0))
    per_step = pl.BlockSpec((CHUNK,) + STATE, lambda i: (top - i, 0, 0, 0))
    step_before = pl.BlockSpec((1,) + STATE, lambda i: (jnp.maximum((top - i) * CHUNK - 1, 0), 0, 0, 0))
    prev = [] if prev is None else list(prev)
    outs = pl.pallas_call(
        body, name=name, grid=(n_chunks,),
        in_specs=[blk] * 7 + [per_step, step_before, per_step, _const(STATE)] + [ANY] * len(prev),
        out_specs=[blk] * 6 + [_const(STATE)],
        out_shape=[jax.ShapeDtypeStruct((SEQ, D_RWKV), F32)] * 6 + [jax.ShapeDtypeStruct(STATE, F32)],
        scratch_shapes=[pltpu.VMEM(STATE, F32)],
        input_output_aliases={11 + t: t for t in range(len(prev))},
        compiler_params=_cp(("arbitrary",)),
    )(r, w, k, v, kkn, b, do, states, states, sas, ds_in, *prev)
    return outs[:6], outs[6]


def _stacked(rows, cols, pick):
    return pl.BlockSpec((None, rows, cols), pick)


def _local_step(x, target, sm, win_st):
    def tied(t, token):
        return t if token is None else t + token[0:1, 0:1].reshape((1,) * t.ndim)

    zpad = jnp.zeros((LORA_DECAY, D_RWKV), F32)
    prm = [sm["w0"], jnp.concatenate([sm["w_decay_up"], zpad], axis=0), sm["a0"],
           jnp.concatenate([zpad, sm["w_iclr_up"]], axis=0), sm["w_gate_up"], sm["k_k"], sm["k_a"]]
    mix = sm["rwkv_shift_mix"]
    onehot = jnp.asarray(_t5_onehot(), BF16)
    sinks = sm["sinks"].reshape(N_Q_HEADS)
    lng, lnb, rk = sm["ln_x_g"], sm["ln_x_b"], sm["r_k"].reshape(1, D_RWKV)

    h1 = _norm_cast(x, sm["norm_mix_pre"], "norm_in")
    proj = _matmul(h1, win_st, "nn", "proj", m=SEQ, n=D_IN, k=D_MODEL, tm=SEQ, tn=640,
                   b_spec=_stacked(D_MODEL, 640, lambda i, j: (j, 0, 0)))
    bias = _bias_table(sm["rel_bias"].T, onehot).reshape(N_KV_HEADS, Q_PER_KV * BLOCK, 2 * BLOCK)
    attn = _attn_fwd(proj, bias, sinks)
    r, w, k2, v, kkn, b, g = _rwkv_prep(proj, mix, prm)
    o, states, sas = _scan_fwd(r, w, k2, v, kkn, b)
    wout, wup_st, wdown = yield ("rest_weights", o)
    cat = _rwkv_post(o, r, k2, v, g, lng, lnb, rk, attn)
    mixo = _matmul(cat, wout, "nn", "out_proj", m=SEQ, n=D_MODEL, k=D_MODEL, tm=SEQ, tn=512)
    x2, h3 = _mix_norm(x, mixo, sm["norm_mix_post"], sm["norm_ffn_pre"])
    u_gate, u_val, gate, val, act = _ffn_up_act(h3, wup_st, sm["conv_w"], sm["conv_b"])
    f = _matmul(act, wdown, "nn", "ffn_down", m=SEQ, n=D_MODEL, k=D_FF, tm=1024, tn=512)
    loss, dy, df, d_g4 = _loss_head(x2, f, sm["norm_ffn_post"], target)

    d_wdown = _matmul(act, df, "tn", "d_wdown", m=D_FF, n=D_MODEL, k=SEQ, tm=1024, tn=D_MODEL)
    du, d_convw, d_convb = _ffn_act_bwd(u_gate, u_val, gate, val, df, wdown, sm["conv_w"])
    d_convw = d_convw.transpose(1, 0, 2).reshape(3, 2 * D_FF)
    d_convb = d_convb.reshape(1, 2 * D_FF)
    dh3 = _matmul_nt_shards(du, wup_st, "d_h3", m=SEQ, n=D_MODEL, tm=512, tn=512,
                            a_spec=pl.BlockSpec((2, 512, D_FF), lambda i, j: (0, i, 0)),
                            a_piece=lambda ref, s: ref[s // 2, :, (s % 2) * 2048:(s % 2 + 1) * 2048])
    d_wup = _matmul(h3, du, "tn", "d_wup", m=D_MODEL, n=2 * D_FF, k=SEQ, tm=D_MODEL, tn=1024,
                    b_spec=pl.BlockSpec((None, SEQ, 1024), lambda i, j: (j // 4, 0, j % 4)),
                    out=((N_CHIPS, D_MODEL, 2048), _stacked(D_MODEL, 1024, lambda i, j: (j // 2, 0, j % 2))))
    dx2, dmix, d_g2, d_g3 = _mid_bwd(x2, mixo, dy, dh3, sm["norm_mix_post"], sm["norm_ffn_pre"])
    dcat = _matmul(dmix, wout, "nt", "d_cat", m=SEQ, n=D_MODEL, k=D_MODEL, tm=SEQ, tn=512)
    d_wout = _matmul(cat, dmix, "tn", "d_wout", m=D_MODEL, n=D_MODEL, k=SEQ, tm=512, tn=D_MODEL)
    token = yield ("grads_a", (d_wdown, d_wup, d_wout))
    do, dr_p, dk_p, dv_p, dg, d_lng, d_lnb, d_rk = _rwkv_post_bwd(o, r, k2, v, g, lng, tied(lnb, token), rk, dcat)
    half = N_CHUNK // 2
    ds_end = jnp.zeros(STATE, F32)
    late, ds_mid = _scan_bwd(r, w, k2, v, kkn, b, do, states, sas, ds_end, None, "rwkv_scan_bwd_late", half, half)
    token = yield ("seam_1", ds_mid)
    scan_cts, ds_first = _scan_bwd(r, w, k2, v, kkn, b, do, states, sas, tied(ds_mid, token), late,
                                   "rwkv_scan_bwd_early", 0, half)
    dr_s, dw_s, dk_s, dv_s, dkkn_s, db_s = scan_cts
    token = yield ("seam_2", ds_first)
    prep_grads = _rwkv_prep_bwd(proj, tied(mix, token), prm,
                                (dr_s, dr_p, dw_s, dk_s, dk_p, dv_s, dv_p, dkkn_s, db_s, dg))
    dps, d_mix, d_w0, d_wdu, d_a0, d_wiu, d_wgu, d_kk, d_ka = prep_grads
    dq, dkv, dbias, dsink = _attn_bwd(proj, bias, sinks, dcat)
    d_relb = _bias_table_bwd(dbias.reshape(N_Q_HEADS, N_REL), onehot).T
    dproj = _assemble_dproj(dq, dkv, dps, mix)
    d_win = _matmul(h1, dproj, "tn", "d_win", m=D_MODEL, n=D_IN, k=SEQ, tm=D_MODEL, tn=640,
                    out=((N_CHIPS, D_MODEL, 640), _stacked(D_MODEL, 640, lambda i, j: (j, 0, 0))))
    token = yield ("grads_b", d_win)
    dh1 = _matmul_nt_shards(dproj, win_st, "d_h1", m=SEQ, n=D_MODEL, tm=1024, tn=D_MODEL,
                            a_spec=pl.BlockSpec((1024, D_IN), lambda i, j: (i, 0)),
                            a_piece=lambda ref, s: ref[:, s * 640:(s + 1) * 640])
    grad_x, d_g1 = _first_bwd(x, dx2, dh1, tied(sm["norm_mix_pre"], token))

    grads = {
        "norm_mix_pre": d_g1, "norm_mix_post": d_g2, "norm_ffn_pre": d_g3, "norm_ffn_post": d_g4,
        "w_in": d_win, "rel_bias": d_relb, "sinks": dsink[:, 0].reshape(1, N_Q_HEADS),
        "rwkv_shift_mix": d_mix, "w0": d_w0, "w_decay_up": d_wdu[:LORA_DECAY], "a0": d_a0,
        "w_iclr_up": d_wiu[LORA_DECAY:], "w_gate_up": d_wgu, "k_k": d_kk, "k_a": d_ka,
        "r_k": d_rk.reshape(1, N_Q_HEADS, HEAD_DIM), "ln_x_g": d_lng, "ln_x_b": d_lnb,
        "w_out": d_wout, "w_ffn_up": d_wup, "conv_w": d_convw, "conv_b": d_convb, "w_ffn_down": d_wdown,
    }
    return loss, grad_x, grads


def _place():
    x, y, c = lax.axis_index("x"), lax.axis_index("y"), lax.axis_index("c")
    chips = [(1 - x, y), (x, 1 - y), (1 - x, 1 - y)]
    return x, y, c, chips


def _remote(src, dst, sems, idx, to):
    return pltpu.make_async_remote_copy(src_ref=src, dst_ref=dst, send_sem=sems[0].at[idx], recv_sem=sems[1].at[idx],
                                        device_id=to, device_id_type=MESH)


ROW_ALIGN = 16


def _half(c, rows):
    return pl.ds(pl.multiple_of(c * (rows // 2), ROW_ALIGN), rows // 2)


def _gather_weights(big, small):
    nb, ns = len(big), len(small)

    def body(*refs):
        ins, outs = refs[:nb + ns], refs[nb + ns:2 * (nb + ns)]
        ici, d2d, sml, loc = refs[2 * (nb + ns):2 * (nb + ns) + 2], refs[-5:-3], refs[-3:-1], refs[-1]
        x, y, c, chips = _place()
        me = 2 * x + y
        sib = (x, y, 1 - c)
        local = [pltpu.make_async_copy(ins[a], outs[a].at[me], loc.at[a]) for a in range(nb + ns)]
        for cp in local:
            cp.start()
        sends = []
        for a in range(nb):
            rows = _half(c, big[a].shape[0])
            for kk, chip in enumerate(chips):
                sends.append(_remote(ins[a].at[rows], outs[a].at[me, rows], ici, a * 3 + kk, (*chip, c)))
        for a in range(ns):
            for kk, chip in enumerate(chips):
                sends.append(_remote(ins[nb + a], outs[nb + a].at[me], sml, a * 3 + kk, (*chip, c)))
        for cp in sends:
            cp.start()
        passed = []
        for a in range(nb):
            rows = _half(c, big[a].shape[0])
            for kk, (px, py) in enumerate(chips):
                got = outs[a].at[2 * px + py, rows]
                _remote(got, got, ici, a * 3 + kk, sib).wait_recv()
                fwd = _remote(got, got, d2d, a * 3 + kk, sib)
                fwd.start()
                passed.append(fwd)
        for a in range(nb):
            other = _half(1 - c, big[a].shape[0])
            for kk, (px, py) in enumerate(chips):
                land = outs[a].at[2 * px + py, other]
                _remote(land, land, d2d, a * 3 + kk, sib).wait_recv()
        for a in range(ns):
            for kk, (px, py) in enumerate(chips):
                land = outs[nb + a].at[2 * px + py]
                _remote(land, land, sml, a * 3 + kk, sib).wait_recv()
        for cp in sends + passed:
            cp.wait_send()
        for cp in local:
            cp.wait()

    arrs = list(big) + list(small)
    in_vmem = pl.BlockSpec(memory_space=pltpu.VMEM)
    return pl.pallas_call(
        body, name="gather_weights",
        in_specs=[in_vmem] * len(arrs), out_specs=[in_vmem] * len(arrs),
        out_shape=[jax.ShapeDtypeStruct((N_CHIPS,) + t.shape, t.dtype) for t in arrs],
        scratch_shapes=[pltpu.SemaphoreType.DMA((3 * nb,)), pltpu.SemaphoreType.DMA((3 * nb,)),
                        pltpu.SemaphoreType.DMA((3 * nb,)), pltpu.SemaphoreType.DMA((3 * nb,)),
                        pltpu.SemaphoreType.DMA((3 * ns,)), pltpu.SemaphoreType.DMA((3 * ns,)),
                        pltpu.SemaphoreType.DMA((nb + ns,))],
        compiler_params=pltpu.CompilerParams(has_side_effects=True, vmem_limit_bytes=VMEM_LIMIT),
    )(*arrs)


HBM = pl.BlockSpec(memory_space=pltpu.HBM)
SEM = pl.BlockSpec(memory_space=pltpu.SEMAPHORE)
EFFECT = pltpu.SideEffectType.DATAFLOW_SIDE_EFFECTING


def _copies_start(name, bufs, plan, n, partners=None):
    nb = len(bufs)

    def body(*refs):
        ins, sems, token = refs[:nb], refs[nb:nb + 2 * n], refs[-1]
        if partners is not None:
            barrier = pltpu.get_barrier_semaphore()
            peers = partners[1]()
            for peer in peers:
                pl.semaphore_signal(barrier, inc=1, device_id=peer, device_id_type=MESH)
            pl.semaphore_wait(barrier, len(peers))
        for kk, (src, dst, dev) in enumerate(plan(ins)):
            pltpu.make_async_remote_copy(src_ref=src, dst_ref=dst, send_sem=sems[2 * kk], recv_sem=sems[2 * kk + 1],
                                         device_id=dev, device_id_type=MESH).start()
        token[...] = jnp.zeros_like(token)

    outs = pl.pallas_call(
        body, name=name,
        out_shape=tuple([pltpu.SemaphoreType.DMA(())] * (2 * n) + [pltpu.HBM(t.shape, t.dtype) for t in bufs]
                        + [jax.ShapeDtypeStruct((8, LANES), F32)]),
        in_specs=[HBM] * nb,
        out_specs=tuple([SEM] * (2 * n) + [HBM] * nb + [pl.BlockSpec(memory_space=pltpu.VMEM)]),
        input_output_aliases={t: 2 * n + t for t in range(nb)},
        compiler_params=pltpu.CompilerParams(has_side_effects=EFFECT,
                                             collective_id=None if partners is None else partners[0]),
    )(*[pltpu.with_memory_space_constraint(t, pltpu.HBM) for t in bufs])
    return outs[:2 * n], outs[2 * n:2 * n + nb], outs[-1]


def _copies_wait(name, sems, bufs, plan, n, after):
    nb = len(bufs)
    after = list(after) if isinstance(after, (list, tuple)) else [after]

    def body(*refs):
        ins, sem_refs = refs[:nb], refs[nb:nb + 2 * n]
        for kk, (src, dst, dev) in enumerate(plan(ins)):
            cp = pltpu.make_async_remote_copy(src_ref=src, dst_ref=dst, send_sem=sem_refs[2 * kk],
                                              recv_sem=sem_refs[2 * kk + 1], device_id=dev, device_id_type=MESH)
            cp.wait_send()
            cp.wait_recv()

    return pl.pallas_call(
        body, name=name,
        out_shape=tuple(pltpu.HBM(t.shape, t.dtype) for t in bufs),
        in_specs=[HBM] * nb + [SEM] * (2 * n) + [ANY] * len(after),
        out_specs=tuple([HBM] * nb),
        input_output_aliases={t: t for t in range(nb)},
        compiler_params=pltpu.CompilerParams(has_side_effects=EFFECT),
    )(*bufs, *sems, *after)


def _plan_gather(n_w):
    def plan(refs):
        x, y, c, chips = _place()
        me = 2 * x + y
        return [(refs[a], refs[n_w + a].at[me], (*chip, c)) for a in range(n_w) for chip in chips + [(x, y)]]
    return plan


def _plan_pair_halves(n_g, rows):
    def plan(refs):
        x, y, c, _ = _place()
        return [(refs[a].at[:, _half(1 - c, rows[a])], refs[n_g + a], (x, y, 1 - c)) for a in range(n_g)]
    return plan


def _plan_chip_parts(n_g):
    def plan(refs):
        x, y, c, chips = _place()
        me = 2 * x + y
        return [(refs[a].at[2 * px + py], refs[n_g + a].at[me], (px, py, c))
                for a in range(n_g) for (px, py) in chips]
    return plan


def _plan_pair_fill(n_g, rows):
    def plan(refs):
        x, y, c, _ = _place()
        return [(refs[a].at[_half(c, rows[a])], refs[a].at[_half(c, rows[a])], (x, y, 1 - c)) for a in range(n_g)]
    return plan


def _pair_add(g, got, name):
    _, rows, cols = g.shape
    hr = rows // 2
    tr = min(hr, 512)
    nb = hr // tr

    def body(g_ref, got_ref, p_ref, own_ref):
        val = (g_ref[...] + got_ref[...]).astype(BF16)
        p_ref[...] = val

        @pl.when(pl.program_id(1) == 2 * lax.axis_index("x") + lax.axis_index("y"))
        def _():
            own_ref[...] = val

    def mine(i, s):
        return (2 * lax.axis_index("x") + lax.axis_index("y"), i, 0)

    return pl.pallas_call(
        body, name=name, grid=(nb, N_CHIPS),
        in_specs=[pl.BlockSpec((None, tr, cols), lambda i, s: (s, lax.axis_index("c") * nb + i, 0)),
                  pl.BlockSpec((None, tr, cols), lambda i, s: (s, i, 0))],
        out_specs=[pl.BlockSpec((None, tr, cols), lambda i, s: (s, i, 0)), pl.BlockSpec((None, tr, cols), mine)],
        out_shape=[jax.ShapeDtypeStruct((N_CHIPS, hr, cols), BF16)] * 2,
        compiler_params=_cp(("parallel", "arbitrary")),
    )(g, got)


def _chip_sum(parts, name):
    _, hr, cols = parts.shape
    tr = min(hr, 256)
    nb = hr // tr

    def body(t_ref, o_ref):
        part = [t_ref[s].astype(F32) for s in range(N_CHIPS)]
        o_ref[...] = ((part[0] + part[1]) + part[2]) + part[3]

    return pl.pallas_call(
        body, name=name, grid=(nb,),
        in_specs=[pl.BlockSpec((N_CHIPS, tr, cols), lambda i: (0, i, 0))],
        out_specs=pl.BlockSpec((tr, cols), lambda i: (lax.axis_index("c") * nb + i, 0)),
        out_shape=jax.ShapeDtypeStruct((2 * hr, cols), F32),
        compiler_params=_cp(("parallel",)),
    )(parts)


class _Reduction:
    def __init__(self, tag, rows, first_id):
        self.tag, self.n, self.rows, self.first_id = tag, len(rows), rows, first_id
        self.plans = (_plan_pair_halves(self.n, rows), _plan_chip_parts(self.n), _plan_pair_fill(self.n, rows))
        self.flight = None

    def _name(self, what):
        return f"grad_{self.tag}_{what}"

    @staticmethod
    def _sibling():
        x, y, c, _ = _place()
        return [(x, y, 1 - c)]

    @staticmethod
    def _same_core_elsewhere():
        x, y, c, chips = _place()
        return [(*chip, c) for chip in chips]

    def start(self, gs):
        gots = [lax.empty((N_CHIPS, t.shape[1] // 2, t.shape[2]), F32) for t in gs]
        self.flight = _copies_start(self._name("pair_start"), list(gs) + gots, self.plans[0], self.n,
                                    (self.first_id, self._sibling))
        return self.flight[2]

    def after_pair(self, after):
        sems, bufs, _ = self.flight
        out = _copies_wait(self._name("pair_wait"), sems, bufs, self.plans[0], self.n, after)
        sums = [_pair_add(g, got, self._name(f"pair_add_{i}"))
                for i, (g, got) in enumerate(zip(out[:self.n], out[self.n:]))]
        self.flight = _copies_start(self._name("chip_start"), [p for p, _ in sums] + [own for _, own in sums],
                                    self.plans[1], 3 * self.n, (self.first_id + 1, self._same_core_elsewhere))
        return self.flight[2]

    def after_chips(self, after):
        sems, bufs, _ = self.flight
        out = _copies_wait(self._name("chip_wait"), sems, bufs, self.plans[1], 3 * self.n, after)
        fulls = [_chip_sum(t, self._name(f"chip_sum_{i}")) for i, t in enumerate(out[self.n:])]
        self.flight = _copies_start(self._name("fill_start"), fulls, self.plans[2], self.n,
                                    (self.first_id + 2, self._sibling))
        return self.flight[2]

    def finish(self, after):
        sems, bufs, _ = self.flight
        return _copies_wait(self._name("fill_wait"), sems, bufs, self.plans[2], self.n, after)


def _adamw_math(w, g, m, v):
    nm = ADAM_B1 * m + (1.0 - ADAM_B1) * g
    nv = ADAM_B2 * v + (1.0 - ADAM_B2) * (g * g)
    m_hat = nm / (1.0 - ADAM_B1 ** ADAM_STEP)
    v_hat = nv / (1.0 - ADAM_B2 ** ADAM_STEP)
    return -ADAM_LR * (m_hat / (jnp.sqrt(v_hat) + ADAM_EPS) + ADAM_WD * w), nm, nv


def _adamw(w, g, m, v, name, tr):
    r, cdim = w.shape

    def body(w_ref, g_ref, m_ref, v_ref, d_ref, nm_ref, nv_ref):
        d_ref[...], nm_ref[...], nv_ref[...] = _adamw_math(w_ref[...], g_ref[...], m_ref[...], v_ref[...])

    return pl.pallas_call(
        body, name=name, grid=(r // tr,), in_specs=[_rows(tr, cdim)] * 4, out_specs=[_rows(tr, cdim)] * 3,
        out_shape=[jax.ShapeDtypeStruct((r, cdim), F32)] * 3, compiler_params=_cp(("parallel",)),
    )(w, g, m, v)


def _adamw_small(w, parts, m, v, shapes):
    n_rows = w.shape[0]

    def scatter(src, outs):
        row = 0
        for (rows, cols), out in zip(shapes, outs):
            if cols == LANES:
                out[...] = src[row:row + rows, :]
            elif cols > LANES:
                per = cols // LANES
                for r in range(rows):
                    for cb in range(per):
                        out[r:r + 1, cb * LANES:(cb + 1) * LANES] = src[row + r * per + cb:row + r * per + cb + 1, :]
            else:
                per = LANES // cols
                for r in range(rows):
                    out[r:r + 1, :] = src[row + r // per:row + r // per + 1, (r % per) * cols:(r % per + 1) * cols]
            row += -(-rows * cols // LANES)

    def body(w_ref, p_ref, m_ref, v_ref, *rest):
        outs, scr = rest[:-4], rest[-4:]
        g = p_ref[0]
        for dev in range(1, N_DEV):
            g = g + p_ref[dev]
        scr[3][...] = g
        scr[0][...], scr[1][...], scr[2][...] = _adamw_math(w_ref[...], g, m_ref[...], v_ref[...])
        n = len(shapes)
        for kind in range(4):
            scatter(scr[kind], outs[kind * n:(kind + 1) * n])

    outs = pl.pallas_call(
        body, name="adamw_small", grid=(1,),
        in_specs=[_const(w.shape), _const(parts.shape), _const(w.shape), _const(w.shape)],
        out_specs=[_const(s) for s in shapes] * 4, out_shape=[jax.ShapeDtypeStruct(s, F32) for s in shapes] * 4,
        scratch_shapes=[pltpu.VMEM((n_rows, LANES), F32)] * 4,
        compiler_params=_cp(("arbitrary",)),
    )(w, parts, m, v)
    n = len(shapes)
    return [outs[kind * n:(kind + 1) * n] for kind in range(4)]


REPLICATED = (("norm_mix_pre", 1024), ("norm_mix_post", 1024), ("norm_ffn_pre", 1024), ("norm_ffn_post", 1024),
              ("rel_bias", 256), ("sinks", 8), ("rwkv_shift_mix", 1792), ("w0", 512), ("a0", 512), ("k_k", 512),
              ("k_a", 512), ("r_k", 512), ("ln_x_g", 512), ("ln_x_b", 512), ("conv_b", 8192))
SMALL_SHARDED = (("w_decay_up", LORA_DECAY, D_RWKV), ("w_iclr_up", LORA_ICLR, D_RWKV),
                 ("w_gate_up", LORA_GATE, D_RWKV), ("conv_w", 3, 2 * D_FF))
BIG = (("w_in", D_MODEL, 640), ("w_out", 256, D_MODEL), ("w_ffn_up", D_MODEL, 2048), ("w_ffn_down", 1024, D_MODEL))
PACK_ALIGN = 8 * LANES


def _pack(pieces):
    flat = []
    for t in pieces:
        t = t.reshape(-1)
        pad = (-t.shape[0]) % LANES
        flat.append(jnp.pad(t, (0, pad)) if pad else t)
    flat = jnp.concatenate(flat)
    pad = (-flat.shape[0]) % PACK_ALIGN
    return jnp.pad(flat, (0, pad)).reshape(-1, LANES)


def kernel(x, norm_mix_pre, norm_mix_post, norm_ffn_pre, norm_ffn_post, w_in, rel_bias, sinks, rwkv_shift_mix, w0, w_decay_up, a0, w_iclr_up, w_gate_up, k_k, k_a, r_k, ln_x_g, ln_x_b, w_out, w_ffn_up, conv_w, conv_b, w_ffn_down, loss_target, m_norm_mix_pre, m_norm_mix_post, m_norm_ffn_pre, m_norm_ffn_post, m_w_in, m_rel_bias, m_sinks, m_rwkv_shift_mix, m_w0, m_w_decay_up, m_a0, m_w_iclr_up, m_w_gate_up, m_k_k, m_k_a, m_r_k, m_ln_x_g, m_ln_x_b, m_w_out, m_w_ffn_up, m_conv_w, m_conv_b, m_w_ffn_down, v_norm_mix_pre, v_norm_mix_post, v_norm_ffn_pre, v_norm_ffn_post, v_w_in, v_rel_bias, v_sinks, v_rwkv_shift_mix, v_w0, v_w_decay_up, v_a0, v_w_iclr_up, v_w_gate_up, v_k_k, v_k_a, v_r_k, v_ln_x_g, v_ln_x_b, v_w_out, v_w_ffn_up, v_conv_w, v_conv_b, v_w_ffn_down):
    given = dict(locals())
    names = [n for n, _ in REPLICATED] + [n for n, _, _ in SMALL_SHARDED] + [n for n, _, _ in BIG]
    order = ["norm_mix_pre", "norm_mix_post", "norm_ffn_pre", "norm_ffn_post", "w_in", "rel_bias", "sinks",
             "rwkv_shift_mix", "w0", "w_decay_up", "a0", "w_iclr_up", "w_gate_up", "k_k", "k_a", "r_k", "ln_x_g",
             "ln_x_b", "w_out", "w_ffn_up", "conv_w", "conv_b", "w_ffn_down"]
    assert sorted(names) == sorted(order)

    big_sh = {n: given[n].reshape(a, b).astype(BF16) for n, a, b in BIG}
    small_sh = [given[n].reshape(r, c // N_CHIPS) for n, r, c in SMALL_SHARDED]
    gathered = _gather_weights([big_sh["w_in"]], small_sh)
    rest = ("w_out", "w_ffn_up", "w_ffn_down")
    win_st, rest_sh = lax.optimization_barrier((gathered[0], [big_sh[n] for n in rest]))
    sm = {n: given[n] for n, _ in REPLICATED}
    sm["r_k"] = r_k.reshape(N_Q_HEADS, HEAD_DIM)
    for (n, r, c), st in zip(SMALL_SHARDED, gathered[1:]):
        sm[n] = st.transpose(1, 0, 2).reshape(r, c)

    lands = [lax.empty((N_CHIPS,) + t.shape, BF16) for t in rest_sh]
    plan_w = _plan_gather(len(rest))
    n_w = N_CHIPS * len(rest)
    w_sems, w_bufs, token = _copies_start("gather_rest_start", rest_sh + lands, plan_w, n_w)
    sm["norm_mix_pre"] = norm_mix_pre + token[0:1, 0:1]

    def on_rest_weights(after):
        out = _copies_wait("gather_rest_wait", w_sems, w_bufs, plan_w, n_w, after)
        wout_st, wup_st, wdown_st = out[3:]
        return wout_st.reshape(D_MODEL, D_MODEL), wup_st, wdown_st.reshape(D_FF, D_MODEL)

    red_a = _Reduction("a", (1024, D_MODEL, 256), first_id=0)
    red_b = _Reduction("b", (D_MODEL,), first_id=3)

    def on_grads_a(gs):
        d_wdown, d_wup, d_wout = gs
        return red_a.start([d_wdown.reshape(N_CHIPS, 1024, D_MODEL), d_wup, d_wout.reshape(N_CHIPS, 256, D_MODEL)])

    handlers = {"rest_weights": on_rest_weights, "grads_a": on_grads_a, "seam_1": red_a.after_pair,
                "seam_2": red_a.after_chips, "grads_b": lambda g: red_b.start([g])}
    steps = _local_step(x[0], loss_target[0], sm, win_st)
    kind, payload = next(steps)
    while True:
        try:
            kind, payload = steps.send(handlers[kind](payload))
        except StopIteration as done:
            loss, grad_x, grads = done.value
            break

    small_names = [n for n, _ in REPLICATED] + [n for n, _, _ in SMALL_SHARDED]

    def shard_cols(t, s):
        return t[:, s * (t.shape[1] // N_CHIPS):(s + 1) * (t.shape[1] // N_CHIPS)]

    for_chip = jnp.stack([_pack([loss[0]] + [grads[n] for n, _ in REPLICATED]
                                + [shard_cols(grads[n], s) for n, _, _ in SMALL_SHARDED]) for s in range(N_CHIPS)])
    land = lax.empty((N_DEV,) + for_chip.shape[1:], F32)

    def plan_small(refs):
        x, y, c, _ = _place()
        out = []
        for rel in range(N_DEV):
            px, py, pc = x ^ (rel >> 2), y ^ ((rel >> 1) & 1), c ^ (rel & 1)
            out.append((refs[0].at[2 * px + py], refs[1].at[4 * x + 2 * y + c], (px, py, pc)))
        return out

    s_sems, s_bufs, s_token = _copies_start("grad_small_start", [for_chip, land], plan_small, N_DEV)

    red_b.after_pair([grad_x, s_token])
    g_out = {}
    g_out["w_ffn_down"], g_out["w_ffn_up"], g_out["w_out"] = red_a.finish(grad_x)

    delta, new_m, new_v = {}, {}, {}

    def update(n, a, b):
        delta[n], new_m[n], new_v[n] = _adamw(given[n].reshape(a, b), g_out[n], given["m_" + n].reshape(a, b),
                                              given["v_" + n].reshape(a, b), "adamw_" + n, 256)

    for n, a, b in BIG[1:]:
        update(n, a, b)
    done = [delta[n] for n, _, _ in BIG[1:]]
    red_b.after_chips(done)
    parts = _copies_wait("grad_small_wait", s_sems, s_bufs, plan_small, N_DEV, done)[1]
    no_param = jnp.zeros((LANES,), F32)
    packs = [_pack([no_param] + [given[pre + n] for n in small_names]) for pre in ("", "m_", "v_")]

    def piece_shape(n):
        shape = given[n].shape
        rows, cols = int(np.prod(shape[:-1])), shape[-1]
        whole = cols % LANES == 0 or (LANES % cols == 0 and (rows * cols) % LANES == 0 and cols >= HEAD_DIM)
        return (rows, cols) if whole else (-(-rows * cols // LANES), LANES)

    shapes = [(1, LANES)] + [piece_shape(n) for n in small_names]
    upd = _adamw_small(packs[0], parts, packs[1], packs[2], shapes)
    loss = upd[3][0][0, 0]
    for i, n in enumerate(small_names):
        shape = given[n].shape
        size = int(np.prod(shape))
        delta[n], new_m[n], new_v[n], g_out[n] = (u[1 + i].reshape(-1)[:size].reshape(shape) for u in upd)
    g_out["w_in"], = red_b.finish(upd[0][0])
    update(*BIG[0])

    def shaped(d):
        return [d[n].reshape(given[n].shape) for n in order]

    return (loss, grad_x.reshape(x.shape), *shaped(g_out), *shaped(delta), *shaped(new_m), *shaped(new_v))
```

```python
import math

import numpy as np
import jax
import jax.numpy as jnp
from jax import lax
from jax.experimental import pallas as pl
from jax.experimental.pallas import tpu as pltpu

F32 = jnp.float32
BF16 = jnp.bfloat16
MESH = pl.DeviceIdType.MESH

SEQ = 2048
D_MODEL = 1024
HEAD_DIM = 64
D_ATTN = 512
D_RWKV = 512
D_KV = 128
N_Q_HEADS = 8
N_KV_HEADS = 2
Q_PER_KV = 4
BLOCK = 128
N_BUCKETS = 32
MAX_DISTANCE = 128
LORA_DECAY = 64
LORA_ICLR = 64
LORA_GATE = 128
RWKV_COLS = 3 * D_RWKV + LORA_DECAY + LORA_ICLR + LORA_GATE
P_OFF = D_ATTN + 2 * D_KV
D_IN = P_OFF + RWKV_COLS
D_FF = 4096
NORM_EPS = 1e-6
GN_EPS = 64e-5
NEG_INF = -1e30
N_CHIPS = 4
N_DEV = 8
HEAD_SHIFT = HEAD_DIM.bit_length() - 1
BLOCK_SHIFT = BLOCK.bit_length() - 1

ADAM_LR = 0.001
ADAM_B1 = 0.9
ADAM_B2 = 0.999
ADAM_EPS = 1e-08
ADAM_WD = 0.01
ADAM_STEP = 10

VMEM_LIMIT = 52 * 1024 * 1024
LANES = 128


def _cp(sem=None, vmem=VMEM_LIMIT):
    kw = dict(vmem_limit_bytes=vmem)
    if sem is not None:
        kw["dimension_semantics"] = sem
    return pltpu.CompilerParams(**kw)


def _rows(tr, nc):
    return pl.BlockSpec((tr, nc), lambda i: (i, 0))


def _const(shape):
    return pl.BlockSpec(shape, lambda *_: (0,) * len(shape))


ANY = pl.BlockSpec(memory_space=pl.ANY)


def _split(x, n):
    parts = []
    for _ in range(n - 1):
        h = x.astype(BF16)
        parts.append(h)
        x = x - h.astype(F32)
    parts.append(x.astype(BF16))
    return parts


NN = (((1,), (0,)), ((), ()))
NT = (((1,), (1,)), ((), ()))
TN = (((0,), (0,)), ((), ()))


def _dot(a, b, dn=NN):
    return lax.dot_general(a, b, dn, preferred_element_type=F32)


def _dot_ind(x, ind_bf16, n=3):
    acc = None
    for part in _split(x, n):
        t = _dot(part, ind_bf16)
        acc = t if acc is None else acc + t
    return acc


def _head_ones(n):
    r = lax.broadcasted_iota(jnp.int32, (n, n), 0) >> HEAD_SHIFT
    c = lax.broadcasted_iota(jnp.int32, (n, n), 1) >> HEAD_SHIFT
    return jnp.where(r == c, 1.0, 0.0).astype(BF16)


def _matmul(a, b, mode, name, *, m, n, k, tm, tn, a_spec=None, b_spec=None, out=None):
    dn = {"nn": NN, "nt": NT, "tn": TN}[mode]

    def body(a_ref, b_ref, o_ref):
        o_ref[...] = _dot(a_ref[...], b_ref[...], dn)

    if a_spec is None:
        a_spec = pl.BlockSpec((k, tm), lambda i, j: (0, i)) if mode == "tn" else pl.BlockSpec((tm, k), lambda i, j: (i, 0))
    if b_spec is None:
        b_spec = pl.BlockSpec((tn, k), lambda i, j: (j, 0)) if mode == "nt" else pl.BlockSpec((k, tn), lambda i, j: (0, j))
    return pl.pallas_call(
        body, name=name, grid=(m // tm, n // tn),
        in_specs=[a_spec, b_spec],
        out_specs=pl.BlockSpec((tm, tn), lambda i, j: (i, j)) if out is None else out[1],
        out_shape=jax.ShapeDtypeStruct((m, n) if out is None else out[0], F32),
        compiler_params=_cp(("parallel", "parallel")),
    )(a, b)


def _matmul_nt_shards(a, b_st, name, *, m, n, tm, tn, a_spec, a_piece):
    ks = b_st.shape[2]

    def body(a_ref, b_ref, o_ref):
        acc = _dot(a_piece(a_ref, 0), b_ref[0], NT)
        for s in range(1, N_CHIPS):
            acc = acc + _dot(a_piece(a_ref, s), b_ref[s], NT)
        o_ref[...] = acc

    return pl.pallas_call(
        body, name=name, grid=(m // tm, n // tn),
        in_specs=[a_spec, pl.BlockSpec((N_CHIPS, tn, ks), lambda i, j: (0, j, 0))],
        out_specs=pl.BlockSpec((tm, tn), lambda i, j: (i, j)),
        out_shape=jax.ShapeDtypeStruct((m, n), F32),
        compiler_params=_cp(("parallel", "parallel")),
    )(a, b_st)


def _rstd(x):
    return lax.rsqrt(jnp.mean(x * x, axis=-1, keepdims=True) + NORM_EPS)


def _rms_bwd(x, r, g, dy):
    gy = dy * g
    return r * gy - x * ((r * r * r) * (jnp.sum(x * gy, axis=-1, keepdims=True) / x.shape[-1]))


TR = 256
TRN = 512


def _norm_cast(x, g, name):
    def body(x_ref, g_ref, h_ref):
        x = x_ref[...]
        h_ref[...] = (x * _rstd(x) * g_ref[...]).astype(BF16)

    return pl.pallas_call(
        body, name=name, grid=(SEQ // TRN,),
        in_specs=[_rows(TRN, D_MODEL), _const((1, D_MODEL))],
        out_specs=_rows(TRN, D_MODEL),
        out_shape=jax.ShapeDtypeStruct((SEQ, D_MODEL), BF16),
        compiler_params=_cp(("parallel",)),
    )(x, g)


def _mix_norm(x, mix, g2, g3):
    def body(x_ref, mix_ref, g2_ref, g3_ref, x2_ref, h3_ref):
        mixv = mix_ref[...]
        x2 = x_ref[...] + mixv * _rstd(mixv) * g2_ref[...]
        x2_ref[...] = x2
        h3_ref[...] = (x2 * _rstd(x2) * g3_ref[...]).astype(BF16)

    return pl.pallas_call(
        body, name="mix_norm", grid=(SEQ // TRN,),
        in_specs=[_rows(TRN, D_MODEL), _rows(TRN, D_MODEL), _const((1, D_MODEL)), _const((1, D_MODEL))],
        out_specs=[_rows(TRN, D_MODEL), _rows(TRN, D_MODEL)],
        out_shape=[jax.ShapeDtypeStruct((SEQ, D_MODEL), F32), jax.ShapeDtypeStruct((SEQ, D_MODEL), BF16)],
        compiler_params=_cp(("parallel",)),
    )(x, mix, g2, g3)


def _loss_head(x2, f, g4, target):
    def body(x2_ref, f_ref, g4_ref, t_ref, loss_ref, dy_ref, df_ref, dg_ref):
        i = pl.program_id(0)
        f = f_ref[...]
        g4 = g4_ref[...]
        r = _rstd(f)
        e = x2_ref[...] + f * r * g4 - t_ref[...]
        dy = e * (1.0 / D_MODEL)
        dy_ref[...] = dy
        df_ref[...] = _rms_bwd(f, r, g4, dy).astype(BF16)
        part = 0.5 * jnp.sum(jnp.sum(e * e, axis=-1, keepdims=True), axis=0, keepdims=True) * (1.0 / D_MODEL)
        dg = jnp.sum(dy * f * r, axis=0, keepdims=True)

        @pl.when(i == 0)
        def _():
            loss_ref[...] = jnp.zeros_like(loss_ref)
            dg_ref[...] = jnp.zeros_like(dg_ref)

        loss_ref[...] += jnp.broadcast_to(part, loss_ref.shape)
        dg_ref[...] += dg

    return pl.pallas_call(
        body, name="loss_head", grid=(SEQ // TRN,),
        in_specs=[_rows(TRN, D_MODEL), _rows(TRN, D_MODEL), _const((1, D_MODEL)), _rows(TRN, D_MODEL)],
        out_specs=[_const((8, LANES)), _rows(TRN, D_MODEL), _rows(TRN, D_MODEL), _const((1, D_MODEL))],
        out_shape=[jax.ShapeDtypeStruct((8, LANES), F32), jax.ShapeDtypeStruct((SEQ, D_MODEL), F32),
                   jax.ShapeDtypeStruct((SEQ, D_MODEL), BF16), jax.ShapeDtypeStruct((1, D_MODEL), F32)],
        compiler_params=_cp(("arbitrary",)),
    )(x2, f, g4, target)


def _mid_bwd(x2, mix, dy, dh3, g2, g3):
    def body(x2_ref, mix_ref, dy_ref, dh3_ref, g2_ref, g3_ref, dx2_ref, dmix_ref, dg2_ref, dg3_ref):
        i = pl.program_id(0)
        x2 = x2_ref[...]
        mixv = mix_ref[...]
        dh3 = dh3_ref[...]
        r3 = _rstd(x2)
        dx2 = dy_ref[...] + _rms_bwd(x2, r3, g3_ref[...], dh3)
        dx2_ref[...] = dx2
        r2 = _rstd(mixv)
        dmix_ref[...] = _rms_bwd(mixv, r2, g2_ref[...], dx2).astype(BF16)

        @pl.when(i == 0)
        def _():
            dg2_ref[...] = jnp.zeros_like(dg2_ref)
            dg3_ref[...] = jnp.zeros_like(dg3_ref)

        dg3_ref[...] += jnp.sum(dh3 * x2 * r3, axis=0, keepdims=True)
        dg2_ref[...] += jnp.sum(dx2 * mixv * r2, axis=0, keepdims=True)

    return pl.pallas_call(
        body, name="mid_bwd", grid=(SEQ // TRN,),
        in_specs=[_rows(TRN, D_MODEL)] * 4 + [_const((1, D_MODEL))] * 2,
        out_specs=[_rows(TRN, D_MODEL), _rows(TRN, D_MODEL), _const((1, D_MODEL)), _const((1, D_MODEL))],
        out_shape=[jax.ShapeDtypeStruct((SEQ, D_MODEL), F32), jax.ShapeDtypeStruct((SEQ, D_MODEL), BF16),
                   jax.ShapeDtypeStruct((1, D_MODEL), F32), jax.ShapeDtypeStruct((1, D_MODEL), F32)],
        compiler_params=_cp(("arbitrary",)),
    )(x2, mix, dy, dh3, g2, g3)


def _first_bwd(x, dx2, dh1, g1):
    def body(x_ref, dx2_ref, dh1_ref, g1_ref, dx_ref, dg1_ref):
        i = pl.program_id(0)
        x = x_ref[...]
        dh1 = dh1_ref[...]
        r = _rstd(x)
        dx_ref[...] = dx2_ref[...] + _rms_bwd(x, r, g1_ref[...], dh1)

        @pl.when(i == 0)
        def _():
            dg1_ref[...] = jnp.zeros_like(dg1_ref)

        dg1_ref[...] += jnp.sum(dh1 * x * r, axis=0, keepdims=True)

    return pl.pallas_call(
        body, name="first_bwd", grid=(SEQ // TRN,),
        in_specs=[_rows(TRN, D_MODEL)] * 3 + [_const((1, D_MODEL))],
        out_specs=[_rows(TRN, D_MODEL), _const((1, D_MODEL))],
        out_shape=[jax.ShapeDtypeStruct((SEQ, D_MODEL), F32), jax.ShapeDtypeStruct((1, D_MODEL), F32)],
        compiler_params=_cp(("arbitrary",)),
    )(x, dx2, dh1, g1)


TC = 256
N_CB = D_FF // TC
GELU_C = math.sqrt(2.0 / math.pi)


def _shift_down(u, s):
    rolled = pltpu.roll(u, s, 0)
    row = lax.broadcasted_iota(jnp.int32, u.shape, 0)
    return jnp.where(row >= s, rolled, 0.0)


def _shift_up(u, s):
    n = u.shape[0]
    rolled = pltpu.roll(u, n - s, 0)
    row = lax.broadcasted_iota(jnp.int32, u.shape, 0)
    return jnp.where(row < n - s, rolled, 0.0)


def _conv3(u, w, b):
    return b + w[0:1] * _shift_down(u, 2) + w[1:2] * _shift_down(u, 1) + w[2:3] * u


def _gelu_and_grad(x):
    inner = GELU_C * (x + 0.044715 * (x * x * x))
    t = jnp.tanh(inner)
    gelu = 0.5 * x * (1.0 + t)
    dgelu = 0.5 * (1.0 + t) + 0.5 * x * (1.0 - t * t) * (GELU_C * (1.0 + 3 * 0.044715 * (x * x)))
    return gelu, dgelu


def _ffn_specs():
    col = lambda off: pl.BlockSpec((SEQ, TC), lambda *g: (0, g[-1] + off))
    w = lambda off: pl.BlockSpec((3, TC), lambda *g: (0, g[-1] + off))
    b = lambda off: pl.BlockSpec((1, TC), lambda *g: (0, g[-1] + off))
    return col, w, b


def _ffn_up_act(h3, wup_st, conv_w, conv_b):
    col, w, b = _ffn_specs()
    per_shard = wup_st.shape[2] // TC

    def body(h_ref, upg_ref, upv_ref, wg_ref, wv_ref, bg_ref, bv_ref, ug_ref, uv_ref, dgate_ref, dval_ref, act_ref):
        h = h_ref[...]
        ug = _dot(h, upg_ref[...])
        uv = _dot(h, upv_ref[...])
        ug_ref[...] = ug
        uv_ref[...] = uv
        gate = _conv3(ug, wg_ref[...], bg_ref[...])
        val = _conv3(uv, wv_ref[...], bv_ref[...])
        gelu, dgelu = _gelu_and_grad(gate)
        dgate_ref[...] = val * dgelu
        dval_ref[...] = gelu
        act_ref[...] = (gelu * val).astype(BF16)

    return pl.pallas_call(
        body, name="ffn_up_act", grid=(N_CB,),
        in_specs=[_const((SEQ, D_MODEL)),
                  pl.BlockSpec((None, D_MODEL, TC), lambda j: (j // per_shard, 0, j % per_shard)),
                  pl.BlockSpec((None, D_MODEL, TC), lambda j: (2 + j // per_shard, 0, j % per_shard)),
                  w(0), w(N_CB), b(0), b(N_CB)],
        out_specs=[col(0)] * 5,
        out_shape=[jax.ShapeDtypeStruct((SEQ, D_FF), F32)] * 4 + [jax.ShapeDtypeStruct((SEQ, D_FF), BF16)],
        compiler_params=_cp(("parallel",)),
    )(h3, wup_st, wup_st, conv_w, conv_w, conv_b, conv_b)


def _ffn_act_bwd(u_gate, u_val, dact_dgate, dact_dval, df, wdown, conv_w):
    col, w, _ = _ffn_specs()
    both = lambda rows: pl.BlockSpec((2, rows, TC), lambda j: (0, 0, j))

    def body(ug_ref, uv_ref, dgate_ref, dval_ref, df_ref, wd_ref, wg_ref, wv_ref, du_ref, dw_ref, db_ref):
        da = _dot(df_ref[...], wd_ref[...], NT)
        halves = ((da * dgate_ref[...], ug_ref, wg_ref[...]), (da * dval_ref[...], uv_ref, wv_ref[...]))
        for h, (duc, u_ref, wh) in enumerate(halves):
            uh = u_ref[...]
            up1, up2 = _shift_up(duc, 1), _shift_up(duc, 2)
            du_ref[h] = (wh[2:3] * duc + wh[1:2] * up1 + wh[0:1] * up2).astype(BF16)
            db_ref[h] = jnp.sum(duc, axis=0, keepdims=True)
            dw_ref[h] = jnp.concatenate(
                [jnp.sum(up2 * uh, axis=0, keepdims=True), jnp.sum(up1 * uh, axis=0, keepdims=True),
                 jnp.sum(duc * uh, axis=0, keepdims=True)], axis=0)

    return pl.pallas_call(
        body, name="ffn_act_bwd", grid=(N_CB,),
        in_specs=[col(0)] * 4 + [_const((SEQ, D_MODEL)), pl.BlockSpec((TC, D_MODEL), lambda j: (j, 0)), w(0), w(N_CB)],
        out_specs=[both(SEQ), both(3), both(1)],
        out_shape=[jax.ShapeDtypeStruct((2, SEQ, D_FF), BF16), jax.ShapeDtypeStruct((2, 3, D_FF), F32),
                   jax.ShapeDtypeStruct((2, 1, D_FF), F32)],
        compiler_params=_cp(("parallel",)),
    )(u_gate, u_val, dact_dgate, dact_dval, df, wdown, conv_w, conv_w)


def _t5_onehot():
    rel = (np.arange(BLOCK)[:, None] + BLOCK) - np.arange(2 * BLOCK)[None, :]
    n = np.maximum(rel, 0)
    max_exact = N_BUCKETS // 2
    large = max_exact + (np.log(np.maximum(n, 1).astype(np.float32) / np.float32(max_exact))
                         / np.float32(math.log(MAX_DISTANCE / max_exact))
                         * np.float32(N_BUCKETS - max_exact)).astype(np.int32)
    large = np.minimum(large, N_BUCKETS - 1)
    bucket = np.where(n < max_exact, n, large).reshape(-1)
    return (bucket[None, :] == np.arange(N_BUCKETS)[:, None]).astype(np.float32)


N_REL = BLOCK * 2 * BLOCK


def _bias_table(rel_bias_t, onehot):
    def body(rb_ref, oh_ref, o_ref):
        o_ref[...] = _dot_ind(rb_ref[...], oh_ref[...])

    return pl.pallas_call(
        body, name="bias_table", grid=(1,),
        in_specs=[_const((N_Q_HEADS, N_BUCKETS)), _const((N_BUCKETS, N_REL))],
        out_specs=_const((N_Q_HEADS, N_REL)),
        out_shape=jax.ShapeDtypeStruct((N_Q_HEADS, N_REL), F32),
        compiler_params=_cp(("arbitrary",)),
    )(rel_bias_t, onehot)


def _bias_table_bwd(dbias, onehot):
    def body(db_ref, oh_ref, o_ref):
        acc = None
        for part in _split(db_ref[...], 3):
            t = _dot(part, oh_ref[...], NT)
            acc = t if acc is None else acc + t
        o_ref[...] = acc

    return pl.pallas_call(
        body, name="bias_table_bwd", grid=(1,),
        in_specs=[_const((N_Q_HEADS, N_REL)), _const((N_BUCKETS, N_REL))],
        out_specs=_const((N_Q_HEADS, N_BUCKETS)),
        out_shape=jax.ShapeDtypeStruct((N_Q_HEADS, N_BUCKETS), F32),
        compiler_params=_cp(("arbitrary",)),
    )(dbias, onehot)


def _attn_pieces(n, q, kvp, kvc, bias_ref, sinks_ref, hk):
    qi = lax.broadcasted_iota(jnp.int32, (BLOCK, 2 * BLOCK), 0)
    kj = lax.broadcasted_iota(jnp.int32, (BLOCK, 2 * BLOCK), 1)
    rel = qi + BLOCK - kj
    first_key = jnp.where(n > 0, 0, BLOCK)
    ok = jnp.where(rel >= 0, jnp.where(rel < BLOCK, jnp.where(kj >= first_key, 1.0, 0.0), 0.0), 0.0)
    ok4 = jnp.concatenate([ok] * Q_PER_KV, axis=0) > 0.5
    c0 = hk * HEAD_DIM
    kcat = jnp.concatenate([kvp[:, c0:c0 + HEAD_DIM], kvc[:, c0:c0 + HEAD_DIM]], axis=0).astype(BF16)
    vcat = jnp.concatenate([kvp[:, D_KV + c0:D_KV + c0 + HEAD_DIM], kvc[:, D_KV + c0:D_KV + c0 + HEAD_DIM]],
                           axis=0).astype(BF16)
    q0 = hk * Q_PER_KV * HEAD_DIM
    qs = jnp.concatenate([q[:, q0 + g * HEAD_DIM:q0 + (g + 1) * HEAD_DIM] for g in range(Q_PER_KV)],
                         axis=0).astype(BF16)
    s = _dot(qs, kcat, NT) * (HEAD_DIM ** -0.5) + bias_ref[hk]
    s = jnp.where(ok4, s, NEG_INF)
    row = lax.broadcasted_iota(jnp.int32, (Q_PER_KV * BLOCK, 1), 0)
    sink = jnp.zeros((Q_PER_KV * BLOCK, 1), F32)
    for g in range(Q_PER_KV):
        sink = jnp.where((row >> BLOCK_SHIFT) == g, sinks_ref[hk * Q_PER_KV + g], sink)
    m = jnp.maximum(jnp.max(s, axis=-1, keepdims=True), sink)
    p = jnp.exp(s - m)
    es = jnp.exp(sink - m)
    inv = 1.0 / (jnp.sum(p, axis=-1, keepdims=True) + es)
    return qs, kcat, vcat, p * inv, es * inv


def _attn_in_specs():
    return [pl.BlockSpec((BLOCK, D_ATTN), lambda n: (n, 0)),
            pl.BlockSpec((BLOCK, 2 * D_KV), lambda n: (jnp.maximum(n - 1, 0), D_ATTN // (2 * D_KV))),
            pl.BlockSpec((BLOCK, 2 * D_KV), lambda n: (n, D_ATTN // (2 * D_KV))),
            _const((N_KV_HEADS, Q_PER_KV * BLOCK, 2 * BLOCK)),
            pl.BlockSpec(memory_space=pltpu.SMEM)]


def _unstack_heads(t):
    return jnp.concatenate([t[g * BLOCK:(g + 1) * BLOCK] for g in range(Q_PER_KV)], axis=1)


def _attn_fwd(proj, bias, sinks):
    def body(q_ref, kvp_ref, kvc_ref, bias_ref, sinks_ref, o_ref):
        n = pl.program_id(0)
        q, kvp, kvc = q_ref[...], kvp_ref[...], kvc_ref[...]
        outs = []
        for hk in range(N_KV_HEADS):
            _, _, vcat, probs, _ = _attn_pieces(n, q, kvp, kvc, bias_ref, sinks_ref, hk)
            outs.append(_unstack_heads(_dot(probs.astype(BF16), vcat)))
        o_ref[...] = jnp.concatenate(outs, axis=1)

    return pl.pallas_call(
        body, name="attn_fwd", grid=(SEQ // BLOCK,),
        in_specs=_attn_in_specs(),
        out_specs=pl.BlockSpec((BLOCK, D_ATTN), lambda n: (n, 0)),
        out_shape=jax.ShapeDtypeStruct((SEQ, D_ATTN), F32),
        compiler_params=_cp(("parallel",)),
    )(proj, proj, proj, bias, sinks)


def _attn_bwd(proj, bias, sinks, dcat):
    nb = SEQ // BLOCK

    def body(q_ref, kvp_ref, kvc_ref, bias_ref, sinks_ref, do_ref, dq_ref, dkv_ref, dbias_ref, dsink_ref, dsacc):
        n = pl.program_id(0)

        @pl.when(n == 0)
        def _():
            dkv_ref[...] = jnp.zeros_like(dkv_ref)
            dbias_ref[...] = jnp.zeros_like(dbias_ref)
            dsacc[...] = jnp.zeros_like(dsacc)

        q, kvp, kvc = q_ref[...], kvp_ref[...], kvc_ref[...]
        do_all = do_ref[...]
        dqs, dks, dvs = [], [], []
        for hk in range(N_KV_HEADS):
            qs, kcat, vcat, probs, psink = _attn_pieces(n, q, kvp, kvc, bias_ref, sinks_ref, hk)
            q0 = hk * Q_PER_KV * HEAD_DIM
            do = jnp.concatenate([do_all[:, q0 + g * HEAD_DIM:q0 + (g + 1) * HEAD_DIM] for g in range(Q_PER_KV)],
                                 axis=0).astype(BF16)
            dprobs = _dot(do, vcat, NT)
            dvs.append(_dot(probs.astype(BF16), do, TN))
            rowdot = jnp.sum(probs * dprobs, axis=-1, keepdims=True)
            ds = probs * (dprobs - rowdot)
            dsacc[hk] += -psink * rowdot
            dbias_ref[hk] += ds
            dsb = (ds * (HEAD_DIM ** -0.5)).astype(BF16)
            dqs.append(_unstack_heads(_dot(dsb, kcat)))
            dks.append(_dot(dsb, qs, TN))
        dq_ref[...] = jnp.concatenate(dqs, axis=1)
        upd = jnp.concatenate(dks + dvs, axis=1)
        cur = pl.multiple_of(n * BLOCK, BLOCK)
        dkv_ref[pl.ds(cur, BLOCK), :] += upd[BLOCK:]

        @pl.when(n > 0)
        def _():
            prev = pl.multiple_of((n - 1) * BLOCK, BLOCK)
            dkv_ref[pl.ds(prev, BLOCK), :] += upd[:BLOCK]

        @pl.when(n == nb - 1)
        def _():
            for hk in range(N_KV_HEADS):
                for g in range(Q_PER_KV):
                    tot = jnp.sum(dsacc[hk, g * BLOCK:(g + 1) * BLOCK, :], axis=0, keepdims=True)
                    h = hk * Q_PER_KV + g
                    dsink_ref[h:h + 1, :] = jnp.broadcast_to(tot, (1, LANES))

    return pl.pallas_call(
        body, name="attn_bwd", grid=(nb,),
        in_specs=_attn_in_specs() + [pl.BlockSpec((BLOCK, D_ATTN), lambda n: (n, 0))],
        out_specs=[pl.BlockSpec((BLOCK, D_ATTN), lambda n: (n, 0)), _const((SEQ, 2 * D_KV)),
                   _const((N_KV_HEADS, Q_PER_KV * BLOCK, 2 * BLOCK)), _const((N_Q_HEADS, LANES))],
        out_shape=[jax.ShapeDtypeStruct((SEQ, D_ATTN), F32), jax.ShapeDtypeStruct((SEQ, 2 * D_KV), F32),
                   jax.ShapeDtypeStruct((N_KV_HEADS, Q_PER_KV * BLOCK, 2 * BLOCK), F32),
                   jax.ShapeDtypeStruct((N_Q_HEADS, LANES), F32)],
        scratch_shapes=[pltpu.VMEM((N_KV_HEADS, Q_PER_KV * BLOCK, 1), F32)],
        compiler_params=_cp(("arbitrary",)),
    )(proj, proj, proj, bias, sinks, dcat)


@jax.custom_vjp
def _head_sum(x):
    ones = _head_ones(LANES)
    return jnp.concatenate([_dot_ind(x[:, c:c + LANES], ones, 2) for c in range(0, x.shape[-1], LANES)], axis=1)


_head_sum.defvjp(lambda x: (_head_sum(x), None), lambda _, ct: (_head_sum(ct),))


@jax.custom_vjp
def _bdot(a, w):
    return _dot(a.astype(BF16), w.astype(BF16))


def _bdot_bwd(res, ct):
    a, w = res
    ctb = ct.astype(BF16)
    return _dot(ctb, w.astype(BF16), NT), _dot(a.astype(BF16), ctb, TN)


_bdot.defvjp(lambda a, w: (_bdot(a, w), (a, w)), _bdot_bwd)


def _sigmoid(x):
    return 0.5 * (jnp.tanh(0.5 * x) + 1.0)


def _softplus(x):
    return jnp.maximum(x, 0.0) + jnp.log(1.0 + jnp.exp(-jnp.abs(x)))


def _rwkv_core(r, k, v, zwa, zg, w0, wdu, a0, wiu, wgu, k_k, k_a):
    w_log = -_softplus(-(w0 + _bdot(jnp.tanh(zwa), wdu))) - 0.5
    decay = jnp.exp(-jnp.exp(w_log))
    a = _sigmoid(a0 + _bdot(zwa, wiu))
    g = _bdot(_sigmoid(zg), wgu)
    kk = k * k_k
    kk = kk / jnp.maximum(jnp.sqrt(_head_sum(kk * kk)), 1e-12)
    k2 = k * (1.0 + (a - 1.0) * k_a)
    return r, decay, k2, v, -kk, kk * a, g


def _rwkv_out(o, r, k2, v, g, lng, lnb, rk):
    mu = _head_sum(o) * (1.0 / HEAD_DIM)
    d = o - mu
    var = _head_sum(d * d) * (1.0 / HEAD_DIM)
    on = d * lax.rsqrt(var + GN_EPS) * lng + lnb
    bonus = _head_sum(r * k2 * rk) * v
    return (on + bonus) * g


P_SPLITS = (0, 512, 1024, 1536, 1664, 1792)
N_PREP_PARAMS = 7
HALO = 8


def _shifted_pieces(i, p_ref, halo_ref, mix_ref):
    p = p_ref[:, P_OFF:]
    prev_row = halo_ref[HALO - 1:HALO, P_OFF:] * jnp.where(i > 0, 1.0, 0.0)
    row = lax.broadcasted_iota(jnp.int32, p.shape, 0)
    pprev = jnp.where(row == 0, prev_row, pltpu.roll(p, 1, 0))
    delta = pprev - p
    ps = p + delta * mix_ref[...]
    return [ps[:, a:b] for a, b in zip(P_SPLITS[:-1], P_SPLITS[1:])], delta


def _prep_in_specs():
    return [_rows(TR, D_IN),
            pl.BlockSpec((HALO, D_IN), lambda i: (jnp.maximum(i * (TR // HALO) - 1, 0), 0)),
            _const((1, RWKV_COLS)), _const((1, D_RWKV)), _const((LANES, D_RWKV)), _const((1, D_RWKV)),
            _const((LANES, D_RWKV)), _const((LANES, D_RWKV)), _const((1, D_RWKV)), _const((1, D_RWKV))]


def _rwkv_prep(proj, mix, prm):
    def body(p_ref, halo_ref, mix_ref, *refs):
        prm_refs, outs = refs[:N_PREP_PARAMS], refs[N_PREP_PARAMS:]
        pieces, _ = _shifted_pieces(pl.program_id(0), p_ref, halo_ref, mix_ref)
        vals = _rwkv_core(*pieces, *[t[...] for t in prm_refs])
        for ref, val in zip(outs, vals):
            ref[...] = val

    return pl.pallas_call(
        body, name="rwkv_prep", grid=(SEQ // TR,),
        in_specs=_prep_in_specs(),
        out_specs=[_rows(TR, D_RWKV)] * 7,
        out_shape=[jax.ShapeDtypeStruct((SEQ, D_RWKV), F32)] * 7,
        compiler_params=_cp(("parallel",)),
    )(proj, proj, mix, *prm)


def _rwkv_prep_bwd(proj, mix, prm, cts):
    def body(p_ref, halo_ref, mix_ref, *refs):
        i = pl.program_id(0)
        prm_refs = refs[:N_PREP_PARAMS]
        ct_refs = refs[N_PREP_PARAMS:N_PREP_PARAMS + 10]
        dps_ref, dmix_ref = refs[N_PREP_PARAMS + 10:N_PREP_PARAMS + 12]
        dprm_refs = refs[N_PREP_PARAMS + 12:]
        pieces, delta = _shifted_pieces(i, p_ref, halo_ref, mix_ref)
        _, vjp = jax.vjp(_rwkv_core, *pieces, *[t[...] for t in prm_refs])
        dr1, dr2, dw, dk1, dk2, dv1, dv2, dkkn, db, dg = [t[...] for t in ct_refs]
        grads = vjp((dr1 + dr2, dw, dk1 + dk2, dv1 + dv2, dkkn, db, dg))
        dps = jnp.concatenate(grads[:5], axis=1)
        dps_ref[...] = dps

        @pl.when(i == 0)
        def _():
            dmix_ref[...] = jnp.zeros_like(dmix_ref)
            for ref in dprm_refs:
                ref[...] = jnp.zeros_like(ref)

        dmix_ref[...] += jnp.sum(dps * delta, axis=0, keepdims=True)
        for ref, gval in zip(dprm_refs, grads[5:]):
            ref[...] += gval

    prm_shapes = [(1, D_RWKV), (LANES, D_RWKV), (1, D_RWKV), (LANES, D_RWKV), (LANES, D_RWKV), (1, D_RWKV), (1, D_RWKV)]
    return pl.pallas_call(
        body, name="rwkv_prep_bwd", grid=(SEQ // TR,),
        in_specs=_prep_in_specs() + [_rows(TR, D_RWKV)] * 10,
        out_specs=[_rows(TR, RWKV_COLS), _const((1, RWKV_COLS))] + [_const(s) for s in prm_shapes],
        out_shape=[jax.ShapeDtypeStruct((SEQ, RWKV_COLS), F32), jax.ShapeDtypeStruct((1, RWKV_COLS), F32)]
        + [jax.ShapeDtypeStruct(s, F32) for s in prm_shapes],
        compiler_params=_cp(("arbitrary",)),
    )(proj, proj, mix, *prm, *cts)


def _rwkv_post(o, r, k2, v, g, lng, lnb, rk, attn):
    def body(o_ref, r_ref, k_ref, v_ref, g_ref, lng_ref, lnb_ref, rk_ref, attn_ref, cat_ref):
        rw = _rwkv_out(*[t[...] for t in (o_ref, r_ref, k_ref, v_ref, g_ref, lng_ref, lnb_ref, rk_ref)])
        cat_ref[...] = jnp.concatenate([attn_ref[...], rw], axis=1).astype(BF16)

    return pl.pallas_call(
        body, name="rwkv_post", grid=(SEQ // TR,),
        in_specs=[_rows(TR, D_RWKV)] * 5 + [_const((1, D_RWKV))] * 3 + [_rows(TR, D_ATTN)],
        out_specs=_rows(TR, D_MODEL),
        out_shape=jax.ShapeDtypeStruct((SEQ, D_MODEL), BF16),
        compiler_params=_cp(("parallel",)),
    )(o, r, k2, v, g, lng, lnb, rk, attn)


def _rwkv_post_bwd(o, r, k2, v, g, lng, lnb, rk, dcat):
    def body(o_ref, r_ref, k_ref, v_ref, g_ref, lng_ref, lnb_ref, rk_ref, dcat_ref,
             do_ref, dr_ref, dk_ref, dv_ref, dg_ref, dlng_ref, dlnb_ref, drk_ref):
        i = pl.program_id(0)
        args = [t[...] for t in (o_ref, r_ref, k_ref, v_ref, g_ref, lng_ref, lnb_ref, rk_ref)]
        _, vjp = jax.vjp(_rwkv_out, *args)
        grads = vjp(dcat_ref[:, D_ATTN:])
        for ref, gval in zip((do_ref, dr_ref, dk_ref, dv_ref, dg_ref), grads[:5]):
            ref[...] = gval

        @pl.when(i == 0)
        def _():
            for ref in (dlng_ref, dlnb_ref, drk_ref):
                ref[...] = jnp.zeros_like(ref)

        for ref, gval in zip((dlng_ref, dlnb_ref, drk_ref), grads[5:]):
            ref[...] += gval

    return pl.pallas_call(
        body, name="rwkv_post_bwd", grid=(SEQ // TR,),
        in_specs=[_rows(TR, D_RWKV)] * 5 + [_const((1, D_RWKV))] * 3 + [_rows(TR, D_MODEL)],
        out_specs=[_rows(TR, D_RWKV)] * 5 + [_const((1, D_RWKV))] * 3,
        out_shape=[jax.ShapeDtypeStruct((SEQ, D_RWKV), F32)] * 5 + [jax.ShapeDtypeStruct((1, D_RWKV), F32)] * 3,
        compiler_params=_cp(("arbitrary",)),
    )(o, r, k2, v, g, lng, lnb, rk, dcat)


def _assemble_dproj(dq, dkv, dps, mix):
    last = SEQ // HALO - 1

    def body(dq_ref, dkv_ref, dps_ref, nxt_ref, mix_ref, o_ref):
        i = pl.program_id(0)
        dps = dps_ref[...]
        mixv = mix_ref[...]
        nxt_row = nxt_ref[0:1, :] * jnp.where(i < SEQ // TR - 1, 1.0, 0.0)
        row = lax.broadcasted_iota(jnp.int32, dps.shape, 0)
        up = jnp.where(row == TR - 1, nxt_row, pltpu.roll(dps, TR - 1, 0))
        dp = dps * (1.0 - mixv) + up * mixv
        o_ref[...] = jnp.concatenate([dq_ref[...], dkv_ref[...], dp], axis=1).astype(BF16)

    return pl.pallas_call(
        body, name="assemble_dproj", grid=(SEQ // TR,),
        in_specs=[_rows(TR, D_ATTN), _rows(TR, 2 * D_KV), _rows(TR, RWKV_COLS),
                  pl.BlockSpec((HALO, RWKV_COLS), lambda i: (jnp.minimum((i + 1) * (TR // HALO), last), 0)),
                  _const((1, RWKV_COLS))],
        out_specs=_rows(TR, D_IN),
        out_shape=jax.ShapeDtypeStruct((SEQ, D_IN), BF16),
        compiler_params=_cp(("parallel",)),
    )(dq, dkv, dps, dps, mix)


N_PAIR = D_RWKV // LANES
CHUNK = 64
N_CHUNK = SEQ // CHUNK
GROUP = 64
STATE = (N_PAIR, HEAD_DIM, LANES)


def _lane_sums(lhs_tiles, ones2):
    out = _dot(jnp.concatenate(lhs_tiles, axis=0), ones2)
    return [out[i * HEAD_DIM:(i + 1) * HEAD_DIM] for i in range(len(lhs_tiles))]


def _seg_sum(xs, ones2):
    return _lane_sums([jnp.concatenate(_split(x, 2), axis=1) for x in xs], ones2)


def _seg_sum_rows(xs, ones2):
    out = _dot(jnp.concatenate(_split(jnp.concatenate(xs, axis=0), 2), axis=1), ones2)
    return [out[i * GROUP:(i + 1) * GROUP] for i in range(len(xs))]


def _col_form(rows, diag, ones2):
    zero = jnp.zeros((HEAD_DIM, LANES), BF16)
    tiles = []
    for row in rows:
        hi = row.astype(BF16)
        lo = (row - hi.astype(F32)).astype(BF16)
        tiles.append(jnp.concatenate(
            [jnp.where(diag, jnp.broadcast_to(part, (HEAD_DIM, LANES)), zero) for part in (hi, lo)], axis=1))
    return _lane_sums(tiles, ones2)


def _scan_consts():
    ones2 = jnp.concatenate([_head_ones(LANES)] * 2, axis=0)
    sub = lax.broadcasted_iota(jnp.int32, (HEAD_DIM, LANES), 0)
    lane_in_head = lax.broadcasted_iota(jnp.int32, (HEAD_DIM, LANES), 1) & (HEAD_DIM - 1)
    return ones2, lane_in_head == sub, lane_in_head


def _rows_of_columns(tile):
    t = tile.T
    return jnp.concatenate([t[:CHUNK], t[HEAD_DIM:HEAD_DIM + CHUNK]], axis=1)


def _pair(j):
    return slice(j * LANES, (j + 1) * LANES)


def _scan_fwd(r, w, k, v, kkn, b):
    def body(r_ref, w_ref, k_ref, v_ref, kkn_ref, b_ref, o_ref, st_ref, sa_ref, s_scr):
        c = pl.program_id(0)
        ones2, diag, lane_in_head = _scan_consts()

        @pl.when(c == 0)
        def _():
            s_scr[...] = jnp.zeros_like(s_scr)

        def group(gi, carry):
            row0 = pl.multiple_of(gi * GROUP, GROUP)
            states, ocols = list(carry[:N_PAIR]), list(carry[N_PAIR:])
            tiles = [[t[pl.ds(row0, GROUP), _pair(j)] for t in (r_ref, w_ref, k_ref, v_ref, kkn_ref, b_ref)]
                     for j in range(N_PAIR)]
            def row(j, name, u):
                return tiles[j]["rwkvnb".index(name)][u:u + 1]

            def emit_out(u, after):
                outs = _seg_sum([s[j] * row(j, "r", u + d) for d, s in enumerate(after) for j in range(N_PAIR)], ones2)
                for d in range(2):
                    here = lane_in_head == gi * GROUP + u + d
                    for j in range(N_PAIR):
                        ocols[j] = jnp.where(here, outs[d * N_PAIR + j], ocols[j])

            def vcols_of(u):
                cols = _col_form([row(j, "v", u + d) for d in range(2) for j in range(N_PAIR)], diag, ones2)
                return cols[:N_PAIR], cols[N_PAIR:]

            n_next = [pltpu.roll(tiles[j][4], GROUP - 1, 0) for j in range(N_PAIR)]
            dots = _seg_sum_rows([tiles[j][5] * n_next[j] for j in range(N_PAIR)]
                                 + [tiles[j][2] * n_next[j] for j in range(N_PAIR)], ones2)
            b_n, k_n = dots[:N_PAIR], dots[N_PAIR:]
            w_n = [tiles[j][1] * n_next[j] for j in range(N_PAIR)]

            vcols = vcols_of(0)
            after = None
            for u in range(0, GROUP, 2):
                prods = _seg_sum([states[j] * row(j, "n", u) for j in range(N_PAIR)]
                                 + [states[j] * w_n[j][u:u + 1] for j in range(N_PAIR)], ones2)
                if after is not None:
                    emit_out(u - 2, after)
                nxt = vcols_of(u + 2) if u + 2 < GROUP else None
                first, second = [], []
                for j in range(N_PAIR):
                    sa1 = prods[j]
                    sa2 = prods[N_PAIR + j] + sa1 * b_n[j][u:u + 1] + vcols[0][j] * k_n[j][u:u + 1]
                    s1 = states[j] * row(j, "w", u) + sa1 * row(j, "b", u) + vcols[0][j] * row(j, "k", u)
                    s2 = s1 * row(j, "w", u + 1) + sa2 * row(j, "b", u + 1) + vcols[1][j] * row(j, "k", u + 1)
                    st_ref[row0 + u, j] = s1
                    sa_ref[row0 + u, j] = sa1
                    st_ref[row0 + u + 1, j] = s2
                    sa_ref[row0 + u + 1, j] = sa2
                    first.append(s1)
                    second.append(s2)
                    states[j] = s2
                after, vcols = (first, second), nxt
            emit_out(GROUP - 2, after)
            return tuple(states + ocols)

        zero = jnp.zeros((HEAD_DIM, LANES), F32)
        fin = lax.fori_loop(0, CHUNK // GROUP, group, tuple(s_scr[j] for j in range(N_PAIR)) + (zero,) * N_PAIR)
        for j in range(N_PAIR):
            s_scr[j] = fin[j]
            o_ref[:, _pair(j)] = _rows_of_columns(fin[N_PAIR + j])

    blk = pl.BlockSpec((CHUNK, D_RWKV), lambda c: (c, 0))
    per_step = pl.BlockSpec((CHUNK,) + STATE, lambda c: (c, 0, 0, 0))
    return pl.pallas_call(
        body, name="rwkv_scan_fwd", grid=(N_CHUNK,),
        in_specs=[blk] * 6,
        out_specs=[blk, per_step, per_step],
        out_shape=[jax.ShapeDtypeStruct((SEQ, D_RWKV), F32)] + [jax.ShapeDtypeStruct((SEQ,) + STATE, F32)] * 2,
        scratch_shapes=[pltpu.VMEM(STATE, F32)],
        compiler_params=_cp(("arbitrary",)),
    )(r, w, k, v, kkn, b)


def _scan_bwd(r, w, k, v, kkn, b, do, states, sas, ds_in, prev, name, first_chunk, n_chunks):
    top = first_chunk + n_chunks - 1

    def body(r_ref, w_ref, k_ref, v_ref, kkn_ref, b_ref, do_ref, st_ref, before_ref, sa_ref, ds_in_ref, *rest):
        dr_ref, dw_ref, dk_ref, dv_ref, dkkn_ref, db_ref, ds_out_ref, ds_scr = rest[-8:]
        i = pl.program_id(0)
        ones2, diag, lane_in_head = _scan_consts()

        @pl.when(i == 0)
        def _():
            ds_scr[...] = ds_in_ref[...]

        entry = [before_ref[0, j] * jnp.where(i < top, 1.0, 0.0) for j in range(N_PAIR)]

        def reverse(gr, carry):
            gi = CHUNK // GROUP - 1 - gr
            row0 = pl.multiple_of(gi * GROUP, GROUP)
            dstates, dvcols = list(carry[:N_PAIR]), list(carry[N_PAIR:])
            tiles = [[t[pl.ds(row0, GROUP), _pair(j)]
                      for t in (r_ref, w_ref, k_ref, v_ref, kkn_ref, b_ref, do_ref)] for j in range(N_PAIR)]
            rows = [[[None] * GROUP for _ in range(5)] for _ in range(N_PAIR)]

            def row(j, name, u):
                return tiles[j]["rwkvnbd".index(name)][u:u + 1]

            def cols_of(u):
                cols = _col_form([row(j, name, u - d) for d in range(2) for name in "dv" for j in range(N_PAIR)],
                                 diag, ones2)
                return [[(cols[(2 * d) * N_PAIR + j], cols[(2 * d + 1) * N_PAIR + j]) for j in range(N_PAIR)]
                        for d in range(2)]

            def emit_dv(u, dsps):
                outs = _seg_sum([dsp[j] * row(j, "k", u - d) for d, dsp in enumerate(dsps) for j in range(N_PAIR)], ones2)
                for d in range(2):
                    here = lane_in_head == gi * GROUP + u - d
                    for j in range(N_PAIR):
                        dvcols[j] = jnp.where(here, outs[d * N_PAIR + j], dvcols[j])

            b_prev = [pltpu.roll(tiles[j][5], 1, 0) for j in range(N_PAIR)]
            dots = _seg_sum_rows([tiles[j][4] * b_prev[j] for j in range(N_PAIR)]
                                 + [tiles[j][0] * tiles[j][5] for j in range(N_PAIR)], ones2)
            n_b, r_b = dots[:N_PAIR], dots[N_PAIR:]
            w_b = [tiles[j][1] * b_prev[j] for j in range(N_PAIR)]

            def outputs(u, j, dsp, dsa, docol, vcol):
                tl = gi * GROUP + u
                if u > 0:
                    s_prev = st_ref[tl - 1, j]
                else:
                    s_prev = jnp.where(gi == 0, entry[j], st_ref[jnp.maximum(tl - 1, 0), j])
                rows[j][0][u] = jnp.sum(st_ref[tl, j] * docol, axis=0, keepdims=True)
                rows[j][1][u] = jnp.sum(dsp * s_prev, axis=0, keepdims=True)
                rows[j][2][u] = jnp.sum(dsp * vcol, axis=0, keepdims=True)
                rows[j][3][u] = jnp.sum(s_prev * dsa, axis=0, keepdims=True)
                rows[j][4][u] = jnp.sum(dsp * sa_ref[tl, j], axis=0, keepdims=True)

            cols = cols_of(GROUP - 1)
            before = None
            for u in range(GROUP - 1, 0, -2):
                dsp1 = [dstates[j] + cols[0][j][0] * row(j, "r", u) for j in range(N_PAIR)]
                prods = _seg_sum([dsp1[j] * row(j, "b", u) for j in range(N_PAIR)]
                                 + [dsp1[j] * w_b[j][u:u + 1] for j in range(N_PAIR)], ones2)
                if before is not None:
                    emit_dv(u + 2, before)
                nxt = cols_of(u - 2) if u >= 2 else None
                dsp2 = []
                for j in range(N_PAIR):
                    dsa1 = prods[j]
                    dsa2 = prods[N_PAIR + j] + dsa1 * n_b[j][u:u + 1] + cols[1][j][0] * r_b[j][u - 1:u]
                    mid = dsp1[j] * row(j, "w", u) + dsa1 * row(j, "n", u) + cols[1][j][0] * row(j, "r", u - 1)
                    outputs(u, j, dsp1[j], dsa1, *cols[0][j])
                    outputs(u - 1, j, mid, dsa2, *cols[1][j])
                    dstates[j] = mid * row(j, "w", u - 1) + dsa2 * row(j, "n", u - 1)
                    dsp2.append(mid)
                before, cols = (dsp1, dsp2), nxt
            emit_dv(1, before)
            for j in range(N_PAIR):
                for ref, rr in zip((dr_ref, dw_ref, dk_ref, dkkn_ref, db_ref), rows[j]):
                    ref[pl.ds(row0, GROUP), _pair(j)] = jnp.concatenate(rr, axis=0)
            return tuple(dstates + dvcols)

        zero = jnp.zeros((HEAD_DIM, LANES), F32)
        dfin = lax.fori_loop(0, CHUNK // GROUP, reverse, tuple(ds_scr[j] for j in range(N_PAIR)) + (zero,) * N_PAIR)
        for j in range(N_PAIR):
            ds_scr[j] = dfin[j]
            dv_ref[:, _pair(j)] = _rows_of_columns(dfin[N_PAIR + j])

        @pl.when(i == n_chunks - 1)
        def _():
            ds_out_ref[...] = ds_scr[...]

    blk = pl.BlockSpec((CHUNK, D_RWKV), lambda i: (top - i, 0))
    per_step = pl.BlockSpec((CHUNK,) + STATE, lambda i: (top - i, 0, 0, 0))
    step_before = pl.BlockSpec((1,) + STATE, lambda i: (jnp.maximum((top - i) * CHUNK - 1, 0), 0, 0, 0))
    prev = [] if prev is None else list(prev)
    outs = pl.pallas_call(
        body, name=name, grid=(n_chunks,),
        in_specs=[blk] * 7 + [per_step, step_before, per_step, _const(STATE)] + [ANY] * len(prev),
        out_specs=[blk] * 6 + [_const(STATE)],
        out_shape=[jax.ShapeDtypeStruct((SEQ, D_RWKV), F32)] * 6 + [jax.ShapeDtypeStruct(STATE, F32)],
        scratch_shapes=[pltpu.VMEM(STATE, F32)],
        input_output_aliases={11 + t: t for t in range(len(prev))},
        compiler_params=_cp(("arbitrary",)),
    )(r, w, k, v, kkn, b, do, states, states, sas, ds_in, *prev)
    return outs[:6], outs[6]


def _stacked(rows, cols, pick):
    return pl.BlockSpec((None, rows, cols), pick)


def _local_step(x, target, sm, win_st):
    def tied(t, token):
        return t if token is None else t + token[0:1, 0:1].reshape((1,) * t.ndim)

    zpad = jnp.zeros((LORA_DECAY, D_RWKV), F32)
    prm = [sm["w0"], jnp.concatenate([sm["w_decay_up"], zpad], axis=0), sm["a0"],
           jnp.concatenate([zpad, sm["w_iclr_up"]], axis=0), sm["w_gate_up"], sm["k_k"], sm["k_a"]]
    mix = sm["rwkv_shift_mix"]
    onehot = jnp.asarray(_t5_onehot(), BF16)
    sinks = sm["sinks"].reshape(N_Q_HEADS)
    lng, lnb, rk = sm["ln_x_g"], sm["ln_x_b"], sm["r_k"].reshape(1, D_RWKV)

    h1 = _norm_cast(x, sm["norm_mix_pre"], "norm_in")
    proj = _matmul(h1, win_st, "nn", "proj", m=SEQ, n=D_IN, k=D_MODEL, tm=SEQ, tn=640,
                   b_spec=_stacked(D_MODEL, 640, lambda i, j: (j, 0, 0)))
    bias = _bias_table(sm["rel_bias"].T, onehot).reshape(N_KV_HEADS, Q_PER_KV * BLOCK, 2 * BLOCK)
    attn = _attn_fwd(proj, bias, sinks)
    r, w, k2, v, kkn, b, g = _rwkv_prep(proj, mix, prm)
    o, states, sas = _scan_fwd(r, w, k2, v, kkn, b)
    wout, wup_st, wdown = yield ("rest_weights", o)
    cat = _rwkv_post(o, r, k2, v, g, lng, lnb, rk, attn)
    mixo = _matmul(cat, wout, "nn", "out_proj", m=SEQ, n=D_MODEL, k=D_MODEL, tm=SEQ, tn=512)
    x2, h3 = _mix_norm(x, mixo, sm["norm_mix_post"], sm["norm_ffn_pre"])
    u_gate, u_val, dact_dgate, dact_dval, act = _ffn_up_act(h3, wup_st, sm["conv_w"], sm["conv_b"])
    f = _matmul(act, wdown, "nn", "ffn_down", m=SEQ, n=D_MODEL, k=D_FF, tm=1024, tn=512)
    loss, dy, df, d_g4 = _loss_head(x2, f, sm["norm_ffn_post"], target)

    d_wdown = _matmul(act, df, "tn", "d_wdown", m=D_FF, n=D_MODEL, k=SEQ, tm=1024, tn=D_MODEL)
    du, d_convw, d_convb = _ffn_act_bwd(u_gate, u_val, dact_dgate, dact_dval, df, wdown, sm["conv_w"])
    d_convw = d_convw.transpose(1, 0, 2).reshape(3, 2 * D_FF)
    d_convb = d_convb.reshape(1, 2 * D_FF)
    dh3 = _matmul_nt_shards(du, wup_st, "d_h3", m=SEQ, n=D_MODEL, tm=512, tn=512,
                            a_spec=pl.BlockSpec((2, 512, D_FF), lambda i, j: (0, i, 0)),
                            a_piece=lambda ref, s: ref[s // 2, :, (s % 2) * 2048:(s % 2 + 1) * 2048])
    d_wup = _matmul(h3, du, "tn", "d_wup", m=D_MODEL, n=2 * D_FF, k=SEQ, tm=D_MODEL, tn=1024,
                    b_spec=pl.BlockSpec((None, SEQ, 1024), lambda i, j: (j // 4, 0, j % 4)),
                    out=((N_CHIPS, D_MODEL, 2048), _stacked(D_MODEL, 1024, lambda i, j: (j // 2, 0, j % 2))))
    dx2, dmix, d_g2, d_g3 = _mid_bwd(x2, mixo, dy, dh3, sm["norm_mix_post"], sm["norm_ffn_pre"])
    dcat = _matmul(dmix, wout, "nt", "d_cat", m=SEQ, n=D_MODEL, k=D_MODEL, tm=SEQ, tn=512)
    d_wout = _matmul(cat, dmix, "tn", "d_wout", m=D_MODEL, n=D_MODEL, k=SEQ, tm=512, tn=D_MODEL)
    token = yield ("grads_a", (d_wdown, d_wup, d_wout))
    do, dr_p, dk_p, dv_p, dg, d_lng, d_lnb, d_rk = _rwkv_post_bwd(o, r, k2, v, g, lng, tied(lnb, token), rk, dcat)
    half = N_CHUNK // 2
    ds_end = jnp.zeros(STATE, F32)
    late, ds_mid = _scan_bwd(r, w, k2, v, kkn, b, do, states, sas, ds_end, None, "rwkv_scan_bwd_late", half, half)
    token = yield ("seam_1", ds_mid)
    scan_cts, ds_first = _scan_bwd(r, w, k2, v, kkn, b, do, states, sas, tied(ds_mid, token), late,
                                   "rwkv_scan_bwd_early", 0, half)
    dr_s, dw_s, dk_s, dv_s, dkkn_s, db_s = scan_cts
    token = yield ("seam_2", ds_first)
    prep_grads = _rwkv_prep_bwd(proj, tied(mix, token), prm,
                                (dr_s, dr_p, dw_s, dk_s, dk_p, dv_s, dv_p, dkkn_s, db_s, dg))
    dps, d_mix, d_w0, d_wdu, d_a0, d_wiu, d_wgu, d_kk, d_ka = prep_grads
    dq, dkv, dbias, dsink = _attn_bwd(proj, bias, sinks, dcat)
    d_relb = _bias_table_bwd(dbias.reshape(N_Q_HEADS, N_REL), onehot).T
    dproj = _assemble_dproj(dq, dkv, dps, mix)
    d_win = _matmul(h1, dproj, "tn", "d_win", m=D_MODEL, n=D_IN, k=SEQ, tm=D_MODEL, tn=640,
                    out=((N_CHIPS, D_MODEL, 640), _stacked(D_MODEL, 640, lambda i, j: (j, 0, 0))))
    token = yield ("grads_b", d_win)
    dh1 = _matmul_nt_shards(dproj, win_st, "d_h1", m=SEQ, n=D_MODEL, tm=1024, tn=D_MODEL,
                            a_spec=pl.BlockSpec((1024, D_IN), lambda i, j: (i, 0)),
                            a_piece=lambda ref, s: ref[:, s * 640:(s + 1) * 640])
    grad_x, d_g1 = _first_bwd(x, dx2, dh1, tied(sm["norm_mix_pre"], token))

    grads = {
        "norm_mix_pre": d_g1, "norm_mix_post": d_g2, "norm_ffn_pre": d_g3, "norm_ffn_post": d_g4,
        "w_in": d_win, "rel_bias": d_relb, "sinks": dsink[:, 0].reshape(1, N_Q_HEADS),
        "rwkv_shift_mix": d_mix, "w0": d_w0, "w_decay_up": d_wdu[:LORA_DECAY], "a0": d_a0,
        "w_iclr_up": d_wiu[LORA_DECAY:], "w_gate_up": d_wgu, "k_k": d_kk, "k_a": d_ka,
        "r_k": d_rk.reshape(1, N_Q_HEADS, HEAD_DIM), "ln_x_g": d_lng, "ln_x_b": d_lnb,
        "w_out": d_wout, "w_ffn_up": d_wup, "conv_w": d_convw, "conv_b": d_convb, "w_ffn_down": d_wdown,
    }
    return loss, grad_x, grads


def _place():
    x, y, c = lax.axis_index("x"), lax.axis_index("y"), lax.axis_index("c")
    chips = [(1 - x, y), (x, 1 - y), (1 - x, 1 - y)]
    return x, y, c, chips


def _remote(src, dst, sems, idx, to):
    return pltpu.make_async_remote_copy(src_ref=src, dst_ref=dst, send_sem=sems[0].at[idx], recv_sem=sems[1].at[idx],
                                        device_id=to, device_id_type=MESH)


ROW_ALIGN = 16


def _half(c, rows):
    return pl.ds(pl.multiple_of(c * (rows // 2), ROW_ALIGN), rows // 2)


def _gather_weights(big, small):
    nb, ns = len(big), len(small)

    def body(*refs):
        ins, outs = refs[:nb + ns], refs[nb + ns:2 * (nb + ns)]
        ici, d2d, sml, loc = refs[2 * (nb + ns):2 * (nb + ns) + 2], refs[-5:-3], refs[-3:-1], refs[-1]
        x, y, c, chips = _place()
        me = 2 * x + y
        sib = (x, y, 1 - c)
        local = [pltpu.make_async_copy(ins[a], outs[a].at[me], loc.at[a]) for a in range(nb + ns)]
        for cp in local:
            cp.start()
        sends = []
        for a in range(nb):
            rows = _half(c, big[a].shape[0])
            for kk, chip in enumerate(chips):
                sends.append(_remote(ins[a].at[rows], outs[a].at[me, rows], ici, a * 3 + kk, (*chip, c)))
        for a in range(ns):
            for kk, chip in enumerate(chips):
                sends.append(_remote(ins[nb + a], outs[nb + a].at[me], sml, a * 3 + kk, (*chip, c)))
        for cp in sends:
            cp.start()
        passed = []
        for a in range(nb):
            rows = _half(c, big[a].shape[0])
            for kk, (px, py) in enumerate(chips):
                got = outs[a].at[2 * px + py, rows]
                _remote(got, got, ici, a * 3 + kk, sib).wait_recv()
                fwd = _remote(got, got, d2d, a * 3 + kk, sib)
                fwd.start()
                passed.append(fwd)
        for a in range(nb):
            other = _half(1 - c, big[a].shape[0])
            for kk, (px, py) in enumerate(chips):
                land = outs[a].at[2 * px + py, other]
                _remote(land, land, d2d, a * 3 + kk, sib).wait_recv()
        for a in range(ns):
            for kk, (px, py) in enumerate(chips):
                land = outs[nb + a].at[2 * px + py]
                _remote(land, land, sml, a * 3 + kk, sib).wait_recv()
        for cp in sends + passed:
            cp.wait_send()
        for cp in local:
            cp.wait()

    arrs = list(big) + list(small)
    in_vmem = pl.BlockSpec(memory_space=pltpu.VMEM)
    return pl.pallas_call(
        body, name="gather_weights",
        in_specs=[in_vmem] * len(arrs), out_specs=[in_vmem] * len(arrs),
        out_shape=[jax.ShapeDtypeStruct((N_CHIPS,) + t.shape, t.dtype) for t in arrs],
        scratch_shapes=[pltpu.SemaphoreType.DMA((3 * nb,)), pltpu.SemaphoreType.DMA((3 * nb,)),
                        pltpu.SemaphoreType.DMA((3 * nb,)), pltpu.SemaphoreType.DMA((3 * nb,)),
                        pltpu.SemaphoreType.DMA((3 * ns,)), pltpu.SemaphoreType.DMA((3 * ns,)),
                        pltpu.SemaphoreType.DMA((nb + ns,))],
        compiler_params=pltpu.CompilerParams(has_side_effects=True, vmem_limit_bytes=VMEM_LIMIT),
    )(*arrs)


HBM = pl.BlockSpec(memory_space=pltpu.HBM)
SEM = pl.BlockSpec(memory_space=pltpu.SEMAPHORE)
EFFECT = pltpu.SideEffectType.DATAFLOW_SIDE_EFFECTING


def _copies_start(name, bufs, plan, n, partners=None):
    nb = len(bufs)

    def body(*refs):
        ins, sems, token = refs[:nb], refs[nb:nb + 2 * n], refs[-1]
        if partners is not None:
            barrier = pltpu.get_barrier_semaphore()
            peers = partners[1]()
            for peer in peers:
                pl.semaphore_signal(barrier, inc=1, device_id=peer, device_id_type=MESH)
            pl.semaphore_wait(barrier, len(peers))
        for kk, (src, dst, dev) in enumerate(plan(ins)):
            pltpu.make_async_remote_copy(src_ref=src, dst_ref=dst, send_sem=sems[2 * kk], recv_sem=sems[2 * kk + 1],
                                         device_id=dev, device_id_type=MESH).start()
        token[...] = jnp.zeros_like(token)

    outs = pl.pallas_call(
        body, name=name,
        out_shape=tuple([pltpu.SemaphoreType.DMA(())] * (2 * n) + [pltpu.HBM(t.shape, t.dtype) for t in bufs]
                        + [jax.ShapeDtypeStruct((8, LANES), F32)]),
        in_specs=[HBM] * nb,
        out_specs=tuple([SEM] * (2 * n) + [HBM] * nb + [pl.BlockSpec(memory_space=pltpu.VMEM)]),
        input_output_aliases={t: 2 * n + t for t in range(nb)},
        compiler_params=pltpu.CompilerParams(has_side_effects=EFFECT,
                                             collective_id=None if partners is None else partners[0]),
    )(*[pltpu.with_memory_space_constraint(t, pltpu.HBM) for t in bufs])
    return outs[:2 * n], outs[2 * n:2 * n + nb], outs[-1]


def _copies_wait(name, sems, bufs, plan, n, after):
    nb = len(bufs)
    after = list(after) if isinstance(after, (list, tuple)) else [after]

    def body(*refs):
        ins, sem_refs = refs[:nb], refs[nb:nb + 2 * n]
        for kk, (src, dst, dev) in enumerate(plan(ins)):
            cp = pltpu.make_async_remote_copy(src_ref=src, dst_ref=dst, send_sem=sem_refs[2 * kk],
                                              recv_sem=sem_refs[2 * kk + 1], device_id=dev, device_id_type=MESH)
            cp.wait_send()
            cp.wait_recv()

    return pl.pallas_call(
        body, name=name,
        out_shape=tuple(pltpu.HBM(t.shape, t.dtype) for t in bufs),
        in_specs=[HBM] * nb + [SEM] * (2 * n) + [ANY] * len(after),
        out_specs=tuple([HBM] * nb),
        input_output_aliases={t: t for t in range(nb)},
        compiler_params=pltpu.CompilerParams(has_side_effects=EFFECT),
    )(*bufs, *sems, *after)


def _plan_gather(n_w):
    def plan(refs):
        x, y, c, chips = _place()
        me = 2 * x + y
        return [(refs[a], refs[n_w + a].at[me], (*chip, c)) for a in range(n_w) for chip in chips + [(x, y)]]
    return plan


def _plan_pair_halves(n_g, rows):
    def plan(refs):
        x, y, c, _ = _place()
        return [(refs[a].at[:, _half(1 - c, rows[a])], refs[n_g + a], (x, y, 1 - c)) for a in range(n_g)]
    return plan


def _plan_chip_parts(n_g):
    def plan(refs):
        x, y, c, chips = _place()
        me = 2 * x + y
        return [(refs[a].at[2 * px + py], refs[n_g + a].at[me], (px, py, c))
                for a in range(n_g) for (px, py) in chips]
    return plan


def _plan_pair_fill(n_g, rows):
    def plan(refs):
        x, y, c, _ = _place()
        return [(refs[a].at[_half(c, rows[a])], refs[a].at[_half(c, rows[a])], (x, y, 1 - c)) for a in range(n_g)]
    return plan


def _pair_add(g, got, name):
    _, rows, cols = g.shape
    hr = rows // 2
    tr = min(hr, 512)
    nb = hr // tr

    def body(g_ref, got_ref, p_ref, own_ref):
        val = (g_ref[...] + got_ref[...]).astype(BF16)
        p_ref[...] = val

        @pl.when(pl.program_id(1) == 2 * lax.axis_index("x") + lax.axis_index("y"))
        def _():
            own_ref[...] = val

    def mine(i, s):
        return (2 * lax.axis_index("x") + lax.axis_index("y"), i, 0)

    return pl.pallas_call(
        body, name=name, grid=(nb, N_CHIPS),
        in_specs=[pl.BlockSpec((None, tr, cols), lambda i, s: (s, lax.axis_index("c") * nb + i, 0)),
                  pl.BlockSpec((None, tr, cols), lambda i, s: (s, i, 0))],
        out_specs=[pl.BlockSpec((None, tr, cols), lambda i, s: (s, i, 0)), pl.BlockSpec((None, tr, cols), mine)],
        out_shape=[jax.ShapeDtypeStruct((N_CHIPS, hr, cols), BF16)] * 2,
        compiler_params=_cp(("parallel", "arbitrary")),
    )(g, got)


def _chip_sum(parts, name):
    _, hr, cols = parts.shape
    tr = min(hr, 256)
    nb = hr // tr

    def body(t_ref, o_ref):
        part = [t_ref[s].astype(F32) for s in range(N_CHIPS)]
        o_ref[...] = ((part[0] + part[1]) + part[2]) + part[3]

    return pl.pallas_call(
        body, name=name, grid=(nb,),
        in_specs=[pl.BlockSpec((N_CHIPS, tr, cols), lambda i: (0, i, 0))],
        out_specs=pl.BlockSpec((tr, cols), lambda i: (lax.axis_index("c") * nb + i, 0)),
        out_shape=jax.ShapeDtypeStruct((2 * hr, cols), F32),
        compiler_params=_cp(("parallel",)),
    )(parts)


class _Reduction:
    def __init__(self, tag, rows, first_id):
        self.tag, self.n, self.rows, self.first_id = tag, len(rows), rows, first_id
        self.plans = (_plan_pair_halves(self.n, rows), _plan_chip_parts(self.n), _plan_pair_fill(self.n, rows))
        self.flight = None

    def _name(self, what):
        return f"grad_{self.tag}_{what}"

    @staticmethod
    def _sibling():
        x, y, c, _ = _place()
        return [(x, y, 1 - c)]

    @staticmethod
    def _same_core_elsewhere():
        x, y, c, chips = _place()
        return [(*chip, c) for chip in chips]

    def start(self, gs):
        gots = [lax.empty((N_CHIPS, t.shape[1] // 2, t.shape[2]), F32) for t in gs]
        self.flight = _copies_start(self._name("pair_start"), list(gs) + gots, self.plans[0], self.n,
                                    (self.first_id, self._sibling))
        return self.flight[2]

    def after_pair(self, after):
        sems, bufs, _ = self.flight
        out = _copies_wait(self._name("pair_wait"), sems, bufs, self.plans[0], self.n, after)
        sums = [_pair_add(g, got, self._name(f"pair_add_{i}"))
                for i, (g, got) in enumerate(zip(out[:self.n], out[self.n:]))]
        self.flight = _copies_start(self._name("chip_start"), [p for p, _ in sums] + [own for _, own in sums],
                                    self.plans[1], 3 * self.n, (self.first_id + 1, self._same_core_elsewhere))
        return self.flight[2]

    def after_chips(self, after):
        sems, bufs, _ = self.flight
        out = _copies_wait(self._name("chip_wait"), sems, bufs, self.plans[1], 3 * self.n, after)
        fulls = [_chip_sum(t, self._name(f"chip_sum_{i}")) for i, t in enumerate(out[self.n:])]
        self.flight = _copies_start(self._name("fill_start"), fulls, self.plans[2], self.n,
                                    (self.first_id + 2, self._sibling))
        return self.flight[2]

    def finish(self, after):
        sems, bufs, _ = self.flight
        return _copies_wait(self._name("fill_wait"), sems, bufs, self.plans[2], self.n, after)


def _adamw_math(w, g, m, v):
    nm = ADAM_B1 * m + (1.0 - ADAM_B1) * g
    nv = ADAM_B2 * v + (1.0 - ADAM_B2) * (g * g)
    m_hat = nm / (1.0 - ADAM_B1 ** ADAM_STEP)
    v_hat = nv / (1.0 - ADAM_B2 ** ADAM_STEP)
    return -ADAM_LR * (m_hat / (jnp.sqrt(v_hat) + ADAM_EPS) + ADAM_WD * w), nm, nv


def _adamw(w, g, m, v, name, tr):
    r, cdim = w.shape

    def body(w_ref, g_ref, m_ref, v_ref, d_ref, nm_ref, nv_ref):
        d_ref[...], nm_ref[...], nv_ref[...] = _adamw_math(w_ref[...], g_ref[...], m_ref[...], v_ref[...])

    return pl.pallas_call(
        body, name=name, grid=(r // tr,), in_specs=[_rows(tr, cdim)] * 4, out_specs=[_rows(tr, cdim)] * 3,
        out_shape=[jax.ShapeDtypeStruct((r, cdim), F32)] * 3, compiler_params=_cp(("parallel",)),
    )(w, g, m, v)


def _adamw_small(w, parts, m, v, shapes):
    n_rows = w.shape[0]

    def scatter(src, outs):
        row = 0
        for (rows, cols), out in zip(shapes, outs):
            if cols == LANES:
                out[...] = src[row:row + rows, :]
            elif cols > LANES:
                per = cols // LANES
                for r in range(rows):
                    for cb in range(per):
                        out[r:r + 1, cb * LANES:(cb + 1) * LANES] = src[row + r * per + cb:row + r * per + cb + 1, :]
            else:
                per = LANES // cols
                for r in range(rows):
                    out[r:r + 1, :] = src[row + r // per:row + r // per + 1, (r % per) * cols:(r % per + 1) * cols]
            row += -(-rows * cols // LANES)

    def body(w_ref, p_ref, m_ref, v_ref, *rest):
        outs, scr = rest[:-4], rest[-4:]
        g = p_ref[0]
        for dev in range(1, N_DEV):
            g = g + p_ref[dev]
        scr[3][...] = g
        scr[0][...], scr[1][...], scr[2][...] = _adamw_math(w_ref[...], g, m_ref[...], v_ref[...])
        n = len(shapes)
        for kind in range(4):
            scatter(scr[kind], outs[kind * n:(kind + 1) * n])

    outs = pl.pallas_call(
        body, name="adamw_small", grid=(1,),
        in_specs=[_const(w.shape), _const(parts.shape), _const(w.shape), _const(w.shape)],
        out_specs=[_const(s) for s in shapes] * 4, out_shape=[jax.ShapeDtypeStruct(s, F32) for s in shapes] * 4,
        scratch_shapes=[pltpu.VMEM((n_rows, LANES), F32)] * 4,
        compiler_params=_cp(("arbitrary",)),
    )(w, parts, m, v)
    n = len(shapes)
    return [outs[kind * n:(kind + 1) * n] for kind in range(4)]


REPLICATED = (("norm_mix_pre", 1024), ("norm_mix_post", 1024), ("norm_ffn_pre", 1024), ("norm_ffn_post", 1024),
              ("rel_bias", 256), ("sinks", 8), ("rwkv_shift_mix", 1792), ("w0", 512), ("a0", 512), ("k_k", 512),
              ("k_a", 512), ("r_k", 512), ("ln_x_g", 512), ("ln_x_b", 512), ("conv_b", 8192))
SMALL_SHARDED = (("w_decay_up", LORA_DECAY, D_RWKV), ("w_iclr_up", LORA_ICLR, D_RWKV),
                 ("w_gate_up", LORA_GATE, D_RWKV), ("conv_w", 3, 2 * D_FF))
BIG = (("w_in", D_MODEL, 640), ("w_out", 256, D_MODEL), ("w_ffn_up", D_MODEL, 2048), ("w_ffn_down", 1024, D_MODEL))
PACK_ALIGN = 8 * LANES


def _pack(pieces):
    flat = []
    for t in pieces:
        t = t.reshape(-1)
        pad = (-t.shape[0]) % LANES
        flat.append(jnp.pad(t, (0, pad)) if pad else t)
    flat = jnp.concatenate(flat)
    pad = (-flat.shape[0]) % PACK_ALIGN
    return jnp.pad(flat, (0, pad)).reshape(-1, LANES)


def kernel(x, norm_mix_pre, norm_mix_post, norm_ffn_pre, norm_ffn_post, w_in, rel_bias, sinks, rwkv_shift_mix, w0, w_decay_up, a0, w_iclr_up, w_gate_up, k_k, k_a, r_k, ln_x_g, ln_x_b, w_out, w_ffn_up, conv_w, conv_b, w_ffn_down, loss_target, m_norm_mix_pre, m_norm_mix_post, m_norm_ffn_pre, m_norm_ffn_post, m_w_in, m_rel_bias, m_sinks, m_rwkv_shift_mix, m_w0, m_w_decay_up, m_a0, m_w_iclr_up, m_w_gate_up, m_k_k, m_k_a, m_r_k, m_ln_x_g, m_ln_x_b, m_w_out, m_w_ffn_up, m_conv_w, m_conv_b, m_w_ffn_down, v_norm_mix_pre, v_norm_mix_post, v_norm_ffn_pre, v_norm_ffn_post, v_w_in, v_rel_bias, v_sinks, v_rwkv_shift_mix, v_w0, v_w_decay_up, v_a0, v_w_iclr_up, v_w_gate_up, v_k_k, v_k_a, v_r_k, v_ln_x_g, v_ln_x_b, v_w_out, v_w_ffn_up, v_conv_w, v_conv_b, v_w_ffn_down):
    given = dict(locals())
    names = [n for n, _ in REPLICATED] + [n for n, _, _ in SMALL_SHARDED] + [n for n, _, _ in BIG]
    order = ["norm_mix_pre", "norm_mix_post", "norm_ffn_pre", "norm_ffn_post", "w_in", "rel_bias", "sinks",
             "rwkv_shift_mix", "w0", "w_decay_up", "a0", "w_iclr_up", "w_gate_up", "k_k", "k_a", "r_k", "ln_x_g",
             "ln_x_b", "w_out", "w_ffn_up", "conv_w", "conv_b", "w_ffn_down"]
    assert sorted(names) == sorted(order)

    big_sh = {n: given[n].reshape(a, b).astype(BF16) for n, a, b in BIG}
    small_sh = [given[n].reshape(r, c // N_CHIPS) for n, r, c in SMALL_SHARDED]
    gathered = _gather_weights([big_sh["w_in"]], small_sh)
    rest = ("w_out", "w_ffn_up", "w_ffn_down")
    win_st, rest_sh = lax.optimization_barrier((gathered[0], [big_sh[n] for n in rest]))
    sm = {n: given[n] for n, _ in REPLICATED}
    sm["r_k"] = r_k.reshape(N_Q_HEADS, HEAD_DIM)
    for (n, r, c), st in zip(SMALL_SHARDED, gathered[1:]):
        sm[n] = st.transpose(1, 0, 2).reshape(r, c)

    lands = [lax.empty((N_CHIPS,) + t.shape, BF16) for t in rest_sh]
    plan_w = _plan_gather(len(rest))
    n_w = N_CHIPS * len(rest)
    w_sems, w_bufs, token = _copies_start("gather_rest_start", rest_sh + lands, plan_w, n_w)
    sm["norm_mix_pre"] = norm_mix_pre + token[0:1, 0:1]

    def on_rest_weights(after):
        out = _copies_wait("gather_rest_wait", w_sems, w_bufs, plan_w, n_w, after)
        wout_st, wup_st, wdown_st = out[3:]
        return wout_st.reshape(D_MODEL, D_MODEL), wup_st, wdown_st.reshape(D_FF, D_MODEL)

    red_a = _Reduction("a", (1024, D_MODEL, 256), first_id=0)
    red_b = _Reduction("b", (D_MODEL,), first_id=3)

    def on_grads_a(gs):
        d_wdown, d_wup, d_wout = gs
        return red_a.start([d_wdown.reshape(N_CHIPS, 1024, D_MODEL), d_wup, d_wout.reshape(N_CHIPS, 256, D_MODEL)])

    handlers = {"rest_weights": on_rest_weights, "grads_a": on_grads_a, "seam_1": red_a.after_pair,
                "seam_2": red_a.after_chips, "grads_b": lambda g: red_b.start([g])}
    steps = _local_step(x[0], loss_target[0], sm, win_st)
    kind, payload = next(steps)
    while True:
        try:
            kind, payload = steps.send(handlers[kind](payload))
        except StopIteration as done:
            loss, grad_x, grads = done.value
            break

    small_names = [n for n, _ in REPLICATED] + [n for n, _, _ in SMALL_SHARDED]

    def shard_cols(t, s):
        return t[:, s * (t.shape[1] // N_CHIPS):(s + 1) * (t.shape[1] // N_CHIPS)]

    for_chip = jnp.stack([_pack([loss[0]] + [grads[n] for n, _ in REPLICATED]
                                + [shard_cols(grads[n], s) for n, _, _ in SMALL_SHARDED]) for s in range(N_CHIPS)])
    land = lax.empty((N_DEV,) + for_chip.shape[1:], F32)

    def plan_small(refs):
        x, y, c, _ = _place()
        out = []
        for rel in range(N_DEV):
            px, py, pc = x ^ (rel >> 2), y ^ ((rel >> 1) & 1), c ^ (rel & 1)
            out.append((refs[0].at[2 * px + py], refs[1].at[4 * x + 2 * y + c], (px, py, pc)))
        return out

    s_sems, s_bufs, s_token = _copies_start("grad_small_start", [for_chip, land], plan_small, N_DEV)

    red_b.after_pair([grad_x, s_token])
    g_out = {}
    g_out["w_ffn_down"], g_out["w_ffn_up"], g_out["w_out"] = red_a.finish(grad_x)

    delta, new_m, new_v = {}, {}, {}

    def update(n, a, b):
        delta[n], new_m[n], new_v[n] = _adamw(given[n].reshape(a, b), g_out[n], given["m_" + n].reshape(a, b),
                                              given["v_" + n].reshape(a, b), "adamw_" + n, 256)

    for n, a, b in BIG[1:]:
        update(n, a, b)
    done = [delta[n] for n, _, _ in BIG[1:]]
    red_b.after_chips(done)
    parts = _copies_wait("grad_small_wait", s_sems, s_bufs, plan_small, N_DEV, done)[1]
    no_param = jnp.zeros((LANES,), F32)
    packs = [_pack([no_param] + [given[pre + n] for n in small_names]) for pre in ("", "m_", "v_")]

    def piece_shape(n):
        shape = given[n].shape
        rows, cols = int(np.prod(shape[:-1])), shape[-1]
        whole = cols % LANES == 0 or (LANES % cols == 0 and (rows * cols) % LANES == 0 and cols >= HEAD_DIM)
        return (rows, cols) if whole else (-(-rows * cols // LANES), LANES)

    shapes = [(1, LANES)] + [piece_shape(n) for n in small_names]
    upd = _adamw_small(packs[0], parts, packs[1], packs[2], shapes)
    loss = upd[3][0][0, 0]
    for i, n in enumerate(small_names):
        shape = given[n].shape
        size = int(np.prod(shape))
        delta[n], new_m[n], new_v[n], g_out[n] = (u[1 + i].reshape(-1)[:size].reshape(shape) for u in upd)
    g_out["w_in"], = red_b.finish(upd[0][0])
    update(*BIG[0])

    def shaped(d):
        return [d[n].reshape(given[n].shape) for n in order]

    return (loss, grad_x.reshape(x.shape), *shaped(g_out), *shaped(delta), *shaped(new_m), *shaped(new_v))
```

```python
import math

import numpy as np
import jax
import jax.numpy as jnp
from jax import lax
from jax.experimental import pallas as pl
from jax.experimental.pallas import tpu as pltpu

F32 = jnp.float32
BF16 = jnp.bfloat16
MESH = pl.DeviceIdType.MESH

SEQ = 2048
D_MODEL = 1024
HEAD_DIM = 64
D_ATTN = 512
D_RWKV = 512
D_KV = 128
N_Q_HEADS = 8
N_KV_HEADS = 2
Q_PER_KV = 4
BLOCK = 128
N_BUCKETS = 32
MAX_DISTANCE = 128
LORA_DECAY = 64
LORA_ICLR = 64
LORA_GATE = 128
RWKV_COLS = 3 * D_RWKV + LORA_DECAY + LORA_ICLR + LORA_GATE
P_OFF = D_ATTN + 2 * D_KV
D_IN = P_OFF + RWKV_COLS
D_FF = 4096
NORM_EPS = 1e-6
GN_EPS = 64e-5
NEG_INF = -1e30
N_CHIPS = 4
N_DEV = 8
HEAD_SHIFT = HEAD_DIM.bit_length() - 1
BLOCK_SHIFT = BLOCK.bit_length() - 1

ADAM_LR = 0.001
ADAM_B1 = 0.9
ADAM_B2 = 0.999
ADAM_EPS = 1e-08
ADAM_WD = 0.01
ADAM_STEP = 10

VMEM_LIMIT = 52 * 1024 * 1024
LANES = 128


def _cp(sem=None, vmem=VMEM_LIMIT):
    kw = dict(vmem_limit_bytes=vmem)
    if sem is not None:
        kw["dimension_semantics"] = sem
    return pltpu.CompilerParams(**kw)


def _rows(tr, nc):
    return pl.BlockSpec((tr, nc), lambda i: (i, 0))


def _const(shape):
    return pl.BlockSpec(shape, lambda *_: (0,) * len(shape))


ANY = pl.BlockSpec(memory_space=pl.ANY)


def _split(x, n):
    parts = []
    for _ in range(n - 1):
        h = x.astype(BF16)
        parts.append(h)
        x = x - h.astype(F32)
    parts.append(x.astype(BF16))
    return parts


NN = (((1,), (0,)), ((), ()))
NT = (((1,), (1,)), ((), ()))
TN = (((0,), (0,)), ((), ()))


def _dot(a, b, dn=NN):
    return lax.dot_general(a, b, dn, preferred_element_type=F32)


def _dot_ind(x, ind_bf16, n=3):
    acc = None
    for part in _split(x, n):
        t = _dot(part, ind_bf16)
        acc = t if acc is None else acc + t
    return acc


def _head_ones(n):
    r = lax.broadcasted_iota(jnp.int32, (n, n), 0) >> HEAD_SHIFT
    c = lax.broadcasted_iota(jnp.int32, (n, n), 1) >> HEAD_SHIFT
    return jnp.where(r == c, 1.0, 0.0).astype(BF16)


def _matmul(a, b, mode, name, *, m, n, k, tm, tn, a_spec=None, b_spec=None, out=None):
    dn = {"nn": NN, "nt": NT, "tn": TN}[mode]

    def body(a_ref, b_ref, o_ref):
        o_ref[...] = _dot(a_ref[...], b_ref[...], dn)

    if a_spec is None:
        a_spec = pl.BlockSpec((k, tm), lambda i, j: (0, i)) if mode == "tn" else pl.BlockSpec((tm, k), lambda i, j: (i, 0))
    if b_spec is None:
        b_spec = pl.BlockSpec((tn, k), lambda i, j: (j, 0)) if mode == "nt" else pl.BlockSpec((k, tn), lambda i, j: (0, j))
    return pl.pallas_call(
        body, name=name, grid=(m // tm, n // tn),
        in_specs=[a_spec, b_spec],
        out_specs=pl.BlockSpec((tm, tn), lambda i, j: (i, j)) if out is None else out[1],
        out_shape=jax.ShapeDtypeStruct((m, n) if out is None else out[0], F32),
        compiler_params=_cp(("parallel", "parallel")),
    )(a, b)


def _matmul_nt_shards(a, b_st, name, *, m, n, tm, tn, a_spec, a_piece):
    ks = b_st.shape[2]

    def body(a_ref, b_ref, o_ref):
        acc = _dot(a_piece(a_ref, 0), b_ref[0], NT)
        for s in range(1, N_CHIPS):
            acc = acc + _dot(a_piece(a_ref, s), b_ref[s], NT)
        o_ref[...] = acc

    return pl.pallas_call(
        body, name=name, grid=(m // tm, n // tn),
        in_specs=[a_spec, pl.BlockSpec((N_CHIPS, tn, ks), lambda i, j: (0, j, 0))],
        out_specs=pl.BlockSpec((tm, tn), lambda i, j: (i, j)),
        out_shape=jax.ShapeDtypeStruct((m, n), F32),
        compiler_params=_cp(("parallel", "parallel")),
    )(a, b_st)


def _rstd(x):
    return lax.rsqrt(jnp.mean(x * x, axis=-1, keepdims=True) + NORM_EPS)


def _rms_bwd(x, r, g, dy):
    gy = dy * g
    return r * gy - x * ((r * r * r) * (jnp.sum(x * gy, axis=-1, keepdims=True) / x.shape[-1]))


TR = 256
TRN = 512


def _norm_cast(x, g, name):
    def body(x_ref, g_ref, h_ref):
        x = x_ref[...]
        h_ref[...] = (x * _rstd(x) * g_ref[...]).astype(BF16)

    return pl.pallas_call(
        body, name=name, grid=(SEQ // TRN,),
        in_specs=[_rows(TRN, D_MODEL), _const((1, D_MODEL))],
        out_specs=_rows(TRN, D_MODEL),
        out_shape=jax.ShapeDtypeStruct((SEQ, D_MODEL), BF16),
        compiler_params=_cp(("parallel",)),
    )(x, g)


def _mix_norm(x, mix, g2, g3):
    def body(x_ref, mix_ref, g2_ref, g3_ref, x2_ref, h3_ref):
        mixv = mix_ref[...]
        x2 = x_ref[...] + mixv * _rstd(mixv) * g2_ref[...]
        x2_ref[...] = x2
        h3_ref[...] = (x2 * _rstd(x2) * g3_ref[...]).astype(BF16)

    return pl.pallas_call(
        body, name="mix_norm", grid=(SEQ // TRN,),
        in_specs=[_rows(TRN, D_MODEL), _rows(TRN, D_MODEL), _const((1, D_MODEL)), _const((1, D_MODEL))],
        out_specs=[_rows(TRN, D_MODEL), _rows(TRN, D_MODEL)],
        out_shape=[jax.ShapeDtypeStruct((SEQ, D_MODEL), F32), jax.ShapeDtypeStruct((SEQ, D_MODEL), BF16)],
        compiler_params=_cp(("parallel",)),
    )(x, mix, g2, g3)


def _loss_head(x2, f, g4, target):
    def body(x2_ref, f_ref, g4_ref, t_ref, loss_ref, dy_ref, df_ref, dg_ref):
        i = pl.program_id(0)
        f = f_ref[...]
        g4 = g4_ref[...]
        r = _rstd(f)
        e = x2_ref[...] + f * r * g4 - t_ref[...]
        dy = e * (1.0 / D_MODEL)
        dy_ref[...] = dy
        df_ref[...] = _rms_bwd(f, r, g4, dy).astype(BF16)
        part = 0.5 * jnp.sum(jnp.sum(e * e, axis=-1, keepdims=True), axis=0, keepdims=True) * (1.0 / D_MODEL)
        dg = jnp.sum(dy * f * r, axis=0, keepdims=True)

        @pl.when(i == 0)
        def _():
            loss_ref[...] = jnp.zeros_like(loss_ref)
            dg_ref[...] = jnp.zeros_like(dg_ref)

        loss_ref[...] += jnp.broadcast_to(part, loss_ref.shape)
        dg_ref[...] += dg

    return pl.pallas_call(
        body, name="loss_head", grid=(SEQ // TRN,),
        in_specs=[_rows(TRN, D_MODEL), _rows(TRN, D_MODEL), _const((1, D_MODEL)), _rows(TRN, D_MODEL)],
        out_specs=[_const((8, LANES)), _rows(TRN, D_MODEL), _rows(TRN, D_MODEL), _const((1, D_MODEL))],
        out_shape=[jax.ShapeDtypeStruct((8, LANES), F32), jax.ShapeDtypeStruct((SEQ, D_MODEL), F32),
                   jax.ShapeDtypeStruct((SEQ, D_MODEL), BF16), jax.ShapeDtypeStruct((1, D_MODEL), F32)],
        compiler_params=_cp(("arbitrary",)),
    )(x2, f, g4, target)


def _mid_bwd(x2, mix, dy, dh3, g2, g3):
    def body(x2_ref, mix_ref, dy_ref, dh3_ref, g2_ref, g3_ref, dx2_ref, dmix_ref, dg2_ref, dg3_ref):
        i = pl.program_id(0)
        x2 = x2_ref[...]
        mixv = mix_ref[...]
        dh3 = dh3_ref[...]
        r3 = _rstd(x2)
        dx2 = dy_ref[...] + _rms_bwd(x2, r3, g3_ref[...], dh3)
        dx2_ref[...] = dx2
        r2 = _rstd(mixv)
        dmix_ref[...] = _rms_bwd(mixv, r2, g2_ref[...], dx2).astype(BF16)

        @pl.when(i == 0)
        def _():
            dg2_ref[...] = jnp.zeros_like(dg2_ref)
            dg3_ref[...] = jnp.zeros_like(dg3_ref)

        dg3_ref[...] += jnp.sum(dh3 * x2 * r3, axis=0, keepdims=True)
        dg2_ref[...] += jnp.sum(dx2 * mixv * r2, axis=0, keepdims=True)

    return pl.pallas_call(
        body, name="mid_bwd", grid=(SEQ // TRN,),
        in_specs=[_rows(TRN, D_MODEL)] * 4 + [_const((1, D_MODEL))] * 2,
        out_specs=[_rows(TRN, D_MODEL), _rows(TRN, D_MODEL), _const((1, D_MODEL)), _const((1, D_MODEL))],
        out_shape=[jax.ShapeDtypeStruct((SEQ, D_MODEL), F32), jax.ShapeDtypeStruct((SEQ, D_MODEL), BF16),
                   jax.ShapeDtypeStruct((1, D_MODEL), F32), jax.ShapeDtypeStruct((1, D_MODEL), F32)],
        compiler_params=_cp(("arbitrary",)),
    )(x2, mix, dy, dh3, g2, g3)


def _first_bwd(x, dx2, dh1, g1):
    def body(x_ref, dx2_ref, dh1_ref, g1_ref, dx_ref, dg1_ref):
        i = pl.program_id(0)
        x = x_ref[...]
        dh1 = dh1_ref[...]
        r = _rstd(x)
        dx_ref[...] = dx2_ref[...] + _rms_bwd(x, r, g1_ref[...], dh1)

        @pl.when(i == 0)
        def _():
            dg1_ref[...] = jnp.zeros_like(dg1_ref)

        dg1_ref[...] += jnp.sum(dh1 * x * r, axis=0, keepdims=True)

    return pl.pallas_call(
        body, name="first_bwd", grid=(SEQ // TRN,),
        in_specs=[_rows(TRN, D_MODEL)] * 3 + [_const((1, D_MODEL))],
        out_specs=[_rows(TRN, D_MODEL), _const((1, D_MODEL))],
        out_shape=[jax.ShapeDtypeStruct((SEQ, D_MODEL), F32), jax.ShapeDtypeStruct((1, D_MODEL), F32)],
        compiler_params=_cp(("arbitrary",)),
    )(x, dx2, dh1, g1)


TC = 256
N_CB = D_FF // TC
GELU_C = math.sqrt(2.0 / math.pi)


def _shift_down(u, s):
    rolled = pltpu.roll(u, s, 0)
    row = lax.broadcasted_iota(jnp.int32, u.shape, 0)
    return jnp.where(row >= s, rolled, 0.0)


def _shift_up(u, s):
    n = u.shape[0]
    rolled = pltpu.roll(u, n - s, 0)
    row = lax.broadcasted_iota(jnp.int32, u.shape, 0)
    return jnp.where(row < n - s, rolled, 0.0)


def _conv3(u, w, b):
    return b + w[0:1] * _shift_down(u, 2) + w[1:2] * _shift_down(u, 1) + w[2:3] * u


def _gelu_and_grad(x):
    inner = GELU_C * (x + 0.044715 * (x * x * x))
    t = jnp.tanh(inner)
    gelu = 0.5 * x * (1.0 + t)
    dgelu = 0.5 * (1.0 + t) + 0.5 * x * (1.0 - t * t) * (GELU_C * (1.0 + 3 * 0.044715 * (x * x)))
    return gelu, dgelu


def _ffn_specs():
    col = lambda off: pl.BlockSpec((SEQ, TC), lambda *g: (0, g[-1] + off))
    w = lambda off: pl.BlockSpec((3, TC), lambda *g: (0, g[-1] + off))
    b = lambda off: pl.BlockSpec((1, TC), lambda *g: (0, g[-1] + off))
    return col, w, b


def _ffn_up_act(h3, wup_st, conv_w, conv_b):
    col, w, b = _ffn_specs()
    per_shard = wup_st.shape[2] // TC

    def body(h_ref, upg_ref, upv_ref, wg_ref, wv_ref, bg_ref, bv_ref, ug_ref, uv_ref, dgate_ref, dval_ref, act_ref):
        h = h_ref[...]
        ug = _dot(h, upg_ref[...])
        uv = _dot(h, upv_ref[...])
        ug_ref[...] = ug
        uv_ref[...] = uv
        gate = _conv3(ug, wg_ref[...], bg_ref[...])
        val = _conv3(uv, wv_ref[...], bv_ref[...])
        gelu, dgelu = _gelu_and_grad(gate)
        dgate_ref[...] = val * dgelu
        dval_ref[...] = gelu
        act_ref[...] = (gelu * val).astype(BF16)

    return pl.pallas_call(
        body, name="ffn_up_act", grid=(N_CB,),
        in_specs=[_const((SEQ, D_MODEL)),
                  pl.BlockSpec((None, D_MODEL, TC), lambda j: (j // per_shard, 0, j % per_shard)),
                  pl.BlockSpec((None, D_MODEL, TC), lambda j: (2 + j // per_shard, 0, j % per_shard)),
                  w(0), w(N_CB), b(0), b(N_CB)],
        out_specs=[col(0)] * 5,
        out_shape=[jax.ShapeDtypeStruct((SEQ, D_FF), F32)] * 4 + [jax.ShapeDtypeStruct((SEQ, D_FF), BF16)],
        compiler_params=_cp(("parallel",)),
    )(h3, wup_st, wup_st, conv_w, conv_w, conv_b, conv_b)


def _ffn_act_bwd(u_gate, u_val, dact_dgate, dact_dval, df, wdown, conv_w, h3):
    col, w, _ = _ffn_specs()
    both = lambda rows: pl.BlockSpec((2, rows, TC), lambda j: (0, 0, j))
    shard_cols = 2 * D_FF // N_CHIPS
    per_shard = shard_cols // TC

    def body(ug_ref, uv_ref, dgate_ref, dval_ref, df_ref, wd_ref, wg_ref, wv_ref, h_ref,
             du_ref, dw_ref, db_ref, dwup_ref):
        da = _dot(df_ref[...], wd_ref[...], NT)
        halves = ((da * dgate_ref[...], ug_ref, wg_ref[...]), (da * dval_ref[...], uv_ref, wv_ref[...]))
        for h, (duc, u_ref, wh) in enumerate(halves):
            uh = u_ref[...]
            up1, up2 = _shift_up(duc, 1), _shift_up(duc, 2)
            du = (wh[2:3] * duc + wh[1:2] * up1 + wh[0:1] * up2).astype(BF16)
            du_ref[h] = du
            dwup_ref[h] = _dot(h_ref[...], du, TN)
            db_ref[h] = jnp.sum(duc, axis=0, keepdims=True)
            dw_ref[h] = jnp.concatenate(
                [jnp.sum(up2 * uh, axis=0, keepdims=True), jnp.sum(up1 * uh, axis=0, keepdims=True),
                 jnp.sum(duc * uh, axis=0, keepdims=True)], axis=0)

    du, d_convw, d_convb, d_wup = pl.pallas_call(
        body, name="ffn_act_bwd", grid=(N_CB,),
        in_specs=[col(0)] * 4 + [_const((SEQ, D_MODEL)), pl.BlockSpec((TC, D_MODEL), lambda j: (j, 0)), w(0), w(N_CB),
                                 _const((SEQ, D_MODEL))],
        out_specs=[both(SEQ), both(3), both(1),
                   pl.BlockSpec((2, None, D_MODEL, TC), lambda j: (0, j // per_shard, 0, j % per_shard))],
        out_shape=[jax.ShapeDtypeStruct((2, SEQ, D_FF), BF16), jax.ShapeDtypeStruct((2, 3, D_FF), F32),
                   jax.ShapeDtypeStruct((2, 1, D_FF), F32),
                   jax.ShapeDtypeStruct((2, N_CHIPS // 2, D_MODEL, shard_cols), F32)],
        compiler_params=_cp(("parallel",)),
    )(u_gate, u_val, dact_dgate, dact_dval, df, wdown, conv_w, conv_w, h3)
    return du, d_convw, d_convb, d_wup.reshape(N_CHIPS, D_MODEL, shard_cols)


def _t5_onehot():
    rel = (np.arange(BLOCK)[:, None] + BLOCK) - np.arange(2 * BLOCK)[None, :]
    n = np.maximum(rel, 0)
    max_exact = N_BUCKETS // 2
    large = max_exact + (np.log(np.maximum(n, 1).astype(np.float32) / np.float32(max_exact))
                         / np.float32(math.log(MAX_DISTANCE / max_exact))
                         * np.float32(N_BUCKETS - max_exact)).astype(np.int32)
    large = np.minimum(large, N_BUCKETS - 1)
    bucket = np.where(n < max_exact, n, large).reshape(-1)
    return (bucket[None, :] == np.arange(N_BUCKETS)[:, None]).astype(np.float32)


N_REL = BLOCK * 2 * BLOCK


def _bias_table(rel_bias_t, onehot):
    def body(rb_ref, oh_ref, o_ref):
        o_ref[...] = _dot_ind(rb_ref[...], oh_ref[...])

    return pl.pallas_call(
        body, name="bias_table", grid=(1,),
        in_specs=[_const((N_Q_HEADS, N_BUCKETS)), _const((N_BUCKETS, N_REL))],
        out_specs=_const((N_Q_HEADS, N_REL)),
        out_shape=jax.ShapeDtypeStruct((N_Q_HEADS, N_REL), F32),
        compiler_params=_cp(("arbitrary",)),
    )(rel_bias_t, onehot)


def _bias_table_bwd(dbias, onehot):
    def body(db_ref, oh_ref, o_ref):
        acc = None
        for part in _split(db_ref[...], 3):
            t = _dot(part, oh_ref[...], NT)
            acc = t if acc is None else acc + t
        o_ref[...] = acc

    return pl.pallas_call(
        body, name="bias_table_bwd", grid=(1,),
        in_specs=[_const((N_Q_HEADS, N_REL)), _const((N_BUCKETS, N_REL))],
        out_specs=_const((N_Q_HEADS, N_BUCKETS)),
        out_shape=jax.ShapeDtypeStruct((N_Q_HEADS, N_BUCKETS), F32),
        compiler_params=_cp(("arbitrary",)),
    )(dbias, onehot)


def _attn_pieces(n, q, kvp, kvc, bias_ref, sinks_ref, hk):
    qi = lax.broadcasted_iota(jnp.int32, (BLOCK, 2 * BLOCK), 0)
    kj = lax.broadcasted_iota(jnp.int32, (BLOCK, 2 * BLOCK), 1)
    rel = qi + BLOCK - kj
    first_key = jnp.where(n > 0, 0, BLOCK)
    ok = jnp.where(rel >= 0, jnp.where(rel < BLOCK, jnp.where(kj >= first_key, 1.0, 0.0), 0.0), 0.0)
    ok4 = jnp.concatenate([ok] * Q_PER_KV, axis=0) > 0.5
    c0 = hk * HEAD_DIM
    kcat = jnp.concatenate([kvp[:, c0:c0 + HEAD_DIM], kvc[:, c0:c0 + HEAD_DIM]], axis=0).astype(BF16)
    vcat = jnp.concatenate([kvp[:, D_KV + c0:D_KV + c0 + HEAD_DIM], kvc[:, D_KV + c0:D_KV + c0 + HEAD_DIM]],
                           axis=0).astype(BF16)
    q0 = hk * Q_PER_KV * HEAD_DIM
    qs = jnp.concatenate([q[:, q0 + g * HEAD_DIM:q0 + (g + 1) * HEAD_DIM] for g in range(Q_PER_KV)],
                         axis=0).astype(BF16)
    s = _dot(qs, kcat, NT) * (HEAD_DIM ** -0.5) + bias_ref[hk]
    s = jnp.where(ok4, s, NEG_INF)
    row = lax.broadcasted_iota(jnp.int32, (Q_PER_KV * BLOCK, 1), 0)
    sink = jnp.zeros((Q_PER_KV * BLOCK, 1), F32)
    for g in range(Q_PER_KV):
        sink = jnp.where((row >> BLOCK_SHIFT) == g, sinks_ref[hk * Q_PER_KV + g], sink)
    m = jnp.maximum(jnp.max(s, axis=-1, keepdims=True), sink)
    p = jnp.exp(s - m)
    es = jnp.exp(sink - m)
    inv = 1.0 / (jnp.sum(p, axis=-1, keepdims=True) + es)
    return qs, kcat, vcat, p * inv, es * inv


def _attn_in_specs():
    return [pl.BlockSpec((BLOCK, D_ATTN), lambda n: (n, 0)),
            pl.BlockSpec((BLOCK, 2 * D_KV), lambda n: (jnp.maximum(n - 1, 0), D_ATTN // (2 * D_KV))),
            pl.BlockSpec((BLOCK, 2 * D_KV), lambda n: (n, D_ATTN // (2 * D_KV))),
            _const((N_KV_HEADS, Q_PER_KV * BLOCK, 2 * BLOCK)),
            pl.BlockSpec(memory_space=pltpu.SMEM)]


def _unstack_heads(t):
    return jnp.concatenate([t[g * BLOCK:(g + 1) * BLOCK] for g in range(Q_PER_KV)], axis=1)


def _attn_fwd(proj, bias, sinks):
    def body(q_ref, kvp_ref, kvc_ref, bias_ref, sinks_ref, o_ref):
        n = pl.program_id(0)
        q, kvp, kvc = q_ref[...], kvp_ref[...], kvc_ref[...]
        outs = []
        for hk in range(N_KV_HEADS):
            _, _, vcat, probs, _ = _attn_pieces(n, q, kvp, kvc, bias_ref, sinks_ref, hk)
            outs.append(_unstack_heads(_dot(probs.astype(BF16), vcat)))
        o_ref[...] = jnp.concatenate(outs, axis=1)

    return pl.pallas_call(
        body, name="attn_fwd", grid=(SEQ // BLOCK,),
        in_specs=_attn_in_specs(),
        out_specs=pl.BlockSpec((BLOCK, D_ATTN), lambda n: (n, 0)),
        out_shape=jax.ShapeDtypeStruct((SEQ, D_ATTN), F32),
        compiler_params=_cp(("parallel",)),
    )(proj, proj, proj, bias, sinks)


def _attn_bwd(proj, bias, sinks, dcat):
    nb = SEQ // BLOCK

    def body(q_ref, kvp_ref, kvc_ref, bias_ref, sinks_ref, do_ref, dq_ref, dkv_ref, dbias_ref, dsink_ref, dsacc):
        n = pl.program_id(0)

        @pl.when(n == 0)
        def _():
            dkv_ref[...] = jnp.zeros_like(dkv_ref)
            dbias_ref[...] = jnp.zeros_like(dbias_ref)
            dsacc[...] = jnp.zeros_like(dsacc)

        q, kvp, kvc = q_ref[...], kvp_ref[...], kvc_ref[...]
        do_all = do_ref[...]
        dqs, dks, dvs = [], [], []
        for hk in range(N_KV_HEADS):
            qs, kcat, vcat, probs, psink = _attn_pieces(n, q, kvp, kvc, bias_ref, sinks_ref, hk)
            q0 = hk * Q_PER_KV * HEAD_DIM
            do = jnp.concatenate([do_all[:, q0 + g * HEAD_DIM:q0 + (g + 1) * HEAD_DIM] for g in range(Q_PER_KV)],
                                 axis=0).astype(BF16)
            dprobs = _dot(do, vcat, NT)
            dvs.append(_dot(probs.astype(BF16), do, TN))
            rowdot = jnp.sum(probs * dprobs, axis=-1, keepdims=True)
            ds = probs * (dprobs - rowdot)
            dsacc[hk] += -psink * rowdot
            dbias_ref[hk] += ds
            dsb = (ds * (HEAD_DIM ** -0.5)).astype(BF16)
            dqs.append(_unstack_heads(_dot(dsb, kcat)))
            dks.append(_dot(dsb, qs, TN))
        dq_ref[...] = jnp.concatenate(dqs, axis=1)
        upd = jnp.concatenate(dks + dvs, axis=1)
        cur = pl.multiple_of(n * BLOCK, BLOCK)
        dkv_ref[pl.ds(cur, BLOCK), :] += upd[BLOCK:]

        @pl.when(n > 0)
        def _():
            prev = pl.multiple_of((n - 1) * BLOCK, BLOCK)
            dkv_ref[pl.ds(prev, BLOCK), :] += upd[:BLOCK]

        @pl.when(n == nb - 1)
        def _():
            for hk in range(N_KV_HEADS):
                for g in range(Q_PER_KV):
                    tot = jnp.sum(dsacc[hk, g * BLOCK:(g + 1) * BLOCK, :], axis=0, keepdims=True)
                    h = hk * Q_PER_KV + g
                    dsink_ref[h:h + 1, :] = jnp.broadcast_to(tot, (1, LANES))

    return pl.pallas_call(
        body, name="attn_bwd", grid=(nb,),
        in_specs=_attn_in_specs() + [pl.BlockSpec((BLOCK, D_ATTN), lambda n: (n, 0))],
        out_specs=[pl.BlockSpec((BLOCK, D_ATTN), lambda n: (n, 0)), _const((SEQ, 2 * D_KV)),
                   _const((N_KV_HEADS, Q_PER_KV * BLOCK, 2 * BLOCK)), _const((N_Q_HEADS, LANES))],
        out_shape=[jax.ShapeDtypeStruct((SEQ, D_ATTN), F32), jax.ShapeDtypeStruct((SEQ, 2 * D_KV), F32),
                   jax.ShapeDtypeStruct((N_KV_HEADS, Q_PER_KV * BLOCK, 2 * BLOCK), F32),
                   jax.ShapeDtypeStruct((N_Q_HEADS, LANES), F32)],
        scratch_shapes=[pltpu.VMEM((N_KV_HEADS, Q_PER_KV * BLOCK, 1), F32)],
        compiler_params=_cp(("arbitrary",)),
    )(proj, proj, proj, bias, sinks, dcat)


@jax.custom_vjp
def _head_sum(x):
    ones = _head_ones(LANES)
    return jnp.concatenate([_dot_ind(x[:, c:c + LANES], ones, 2) for c in range(0, x.shape[-1], LANES)], axis=1)


_head_sum.defvjp(lambda x: (_head_sum(x), None), lambda _, ct: (_head_sum(ct),))


@jax.custom_vjp
def _bdot(a, w):
    return _dot(a.astype(BF16), w.astype(BF16))


def _bdot_bwd(res, ct):
    a, w = res
    ctb = ct.astype(BF16)
    return _dot(ctb, w.astype(BF16), NT), _dot(a.astype(BF16), ctb, TN)


_bdot.defvjp(lambda a, w: (_bdot(a, w), (a, w)), _bdot_bwd)


def _sigmoid(x):
    return 0.5 * (jnp.tanh(0.5 * x) + 1.0)


def _softplus(x):
    return jnp.maximum(x, 0.0) + jnp.log(1.0 + jnp.exp(-jnp.abs(x)))


def _rwkv_core(r, k, v, zwa, zg, w0, wdu, a0, wiu, wgu, k_k, k_a):
    w_log = -_softplus(-(w0 + _bdot(jnp.tanh(zwa), wdu))) - 0.5
    decay = jnp.exp(-jnp.exp(w_log))
    a = _sigmoid(a0 + _bdot(zwa, wiu))
    g = _bdot(_sigmoid(zg), wgu)
    kk = k * k_k
    kk = kk / jnp.maximum(jnp.sqrt(_head_sum(kk * kk)), 1e-12)
    k2 = k * (1.0 + (a - 1.0) * k_a)
    return r, decay, k2, v, -kk, kk * a, g


def _rwkv_out(o, r, k2, v, g, lng, lnb, rk):
    mu = _head_sum(o) * (1.0 / HEAD_DIM)
    d = o - mu
    var = _head_sum(d * d) * (1.0 / HEAD_DIM)
    on = d * lax.rsqrt(var + GN_EPS) * lng + lnb
    bonus = _head_sum(r * k2 * rk) * v
    return (on + bonus) * g


P_SPLITS = (0, 512, 1024, 1536, 1664, 1792)
N_PREP_PARAMS = 7
HALO = 8


def _shifted_pieces(i, p_ref, halo_ref, mix_ref):
    p = p_ref[:, P_OFF:]
    prev_row = halo_ref[HALO - 1:HALO, P_OFF:] * jnp.where(i > 0, 1.0, 0.0)
    row = lax.broadcasted_iota(jnp.int32, p.shape, 0)
    pprev = jnp.where(row == 0, prev_row, pltpu.roll(p, 1, 0))
    delta = pprev - p
    ps = p + delta * mix_ref[...]
    return [ps[:, a:b] for a, b in zip(P_SPLITS[:-1], P_SPLITS[1:])], delta


def _prep_in_specs():
    return [_rows(TR, D_IN),
            pl.BlockSpec((HALO, D_IN), lambda i: (jnp.maximum(i * (TR // HALO) - 1, 0), 0)),
            _const((1, RWKV_COLS)), _const((1, D_RWKV)), _const((LANES, D_RWKV)), _const((1, D_RWKV)),
            _const((LANES, D_RWKV)), _const((LANES, D_RWKV)), _const((1, D_RWKV)), _const((1, D_RWKV))]


def _rwkv_prep(proj, mix, prm):
    def body(p_ref, halo_ref, mix_ref, *refs):
        prm_refs, outs = refs[:N_PREP_PARAMS], refs[N_PREP_PARAMS:]
        pieces, _ = _shifted_pieces(pl.program_id(0), p_ref, halo_ref, mix_ref)
        vals = _rwkv_core(*pieces, *[t[...] for t in prm_refs])
        for ref, val in zip(outs, vals):
            ref[...] = val

    return pl.pallas_call(
        body, name="rwkv_prep", grid=(SEQ // TR,),
        in_specs=_prep_in_specs(),
        out_specs=[_rows(TR, D_RWKV)] * 7,
        out_shape=[jax.ShapeDtypeStruct((SEQ, D_RWKV), F32)] * 7,
        compiler_params=_cp(("parallel",)),
    )(proj, proj, mix, *prm)


def _rwkv_prep_bwd(proj, mix, prm, cts):
    def body(p_ref, halo_ref, mix_ref, *refs):
        i = pl.program_id(0)
        prm_refs = refs[:N_PREP_PARAMS]
        ct_refs = refs[N_PREP_PARAMS:N_PREP_PARAMS + 10]
        dps_ref, dmix_ref = refs[N_PREP_PARAMS + 10:N_PREP_PARAMS + 12]
        dprm_refs = refs[N_PREP_PARAMS + 12:]
        pieces, delta = _shifted_pieces(i, p_ref, halo_ref, mix_ref)
        _, vjp = jax.vjp(_rwkv_core, *pieces, *[t[...] for t in prm_refs])
        dr1, dr2, dw, dk1, dk2, dv1, dv2, dkkn, db, dg = [t[...] for t in ct_refs]
        grads = vjp((dr1 + dr2, dw, dk1 + dk2, dv1 + dv2, dkkn, db, dg))
        dps = jnp.concatenate(grads[:5], axis=1)
        dps_ref[...] = dps

        @pl.when(i == 0)
        def _():
            dmix_ref[...] = jnp.zeros_like(dmix_ref)
            for ref in dprm_refs:
                ref[...] = jnp.zeros_like(ref)

        dmix_ref[...] += jnp.sum(dps * delta, axis=0, keepdims=True)
        for ref, gval in zip(dprm_refs, grads[5:]):
            ref[...] += gval

    prm_shapes = [(1, D_RWKV), (LANES, D_RWKV), (1, D_RWKV), (LANES, D_RWKV), (LANES, D_RWKV), (1, D_RWKV), (1, D_RWKV)]
    return pl.pallas_call(
        body, name="rwkv_prep_bwd", grid=(SEQ // TR,),
        in_specs=_prep_in_specs() + [_rows(TR, D_RWKV)] * 10,
        out_specs=[_rows(TR, RWKV_COLS), _const((1, RWKV_COLS))] + [_const(s) for s in prm_shapes],
        out_shape=[jax.ShapeDtypeStruct((SEQ, RWKV_COLS), F32), jax.ShapeDtypeStruct((1, RWKV_COLS), F32)]
        + [jax.ShapeDtypeStruct(s, F32) for s in prm_shapes],
        compiler_params=_cp(("arbitrary",)),
    )(proj, proj, mix, *prm, *cts)


def _rwkv_post(o, r, k2, v, g, lng, lnb, rk, attn):
    def body(o_ref, r_ref, k_ref, v_ref, g_ref, lng_ref, lnb_ref, rk_ref, attn_ref, cat_ref):
        rw = _rwkv_out(*[t[...] for t in (o_ref, r_ref, k_ref, v_ref, g_ref, lng_ref, lnb_ref, rk_ref)])
        cat_ref[...] = jnp.concatenate([attn_ref[...], rw], axis=1).astype(BF16)

    return pl.pallas_call(
        body, name="rwkv_post", grid=(SEQ // TR,),
        in_specs=[_rows(TR, D_RWKV)] * 5 + [_const((1, D_RWKV))] * 3 + [_rows(TR, D_ATTN)],
        out_specs=_rows(TR, D_MODEL),
        out_shape=jax.ShapeDtypeStruct((SEQ, D_MODEL), BF16),
        compiler_params=_cp(("parallel",)),
    )(o, r, k2, v, g, lng, lnb, rk, attn)


def _rwkv_post_bwd(o, r, k2, v, g, lng, lnb, rk, dcat):
    def body(o_ref, r_ref, k_ref, v_ref, g_ref, lng_ref, lnb_ref, rk_ref, dcat_ref,
             do_ref, dr_ref, dk_ref, dv_ref, dg_ref, dlng_ref, dlnb_ref, drk_ref):
        i = pl.program_id(0)
        args = [t[...] for t in (o_ref, r_ref, k_ref, v_ref, g_ref, lng_ref, lnb_ref, rk_ref)]
        _, vjp = jax.vjp(_rwkv_out, *args)
        grads = vjp(dcat_ref[:, D_ATTN:])
        for ref, gval in zip((do_ref, dr_ref, dk_ref, dv_ref, dg_ref), grads[:5]):
            ref[...] = gval

        @pl.when(i == 0)
        def _():
            for ref in (dlng_ref, dlnb_ref, drk_ref):
                ref[...] = jnp.zeros_like(ref)

        for ref, gval in zip((dlng_ref, dlnb_ref, drk_ref), grads[5:]):
            ref[...] += gval

    return pl.pallas_call(
        body, name="rwkv_post_bwd", grid=(SEQ // TR,),
        in_specs=[_rows(TR, D_RWKV)] * 5 + [_const((1, D_RWKV))] * 3 + [_rows(TR, D_MODEL)],
        out_specs=[_rows(TR, D_RWKV)] * 5 + [_const((1, D_RWKV))] * 3,
        out_shape=[jax.ShapeDtypeStruct((SEQ, D_RWKV), F32)] * 5 + [jax.ShapeDtypeStruct((1, D_RWKV), F32)] * 3,
        compiler_params=_cp(("arbitrary",)),
    )(o, r, k2, v, g, lng, lnb, rk, dcat)


def _assemble_dproj(dq, dkv, dps, mix):
    last = SEQ // HALO - 1

    def body(dq_ref, dkv_ref, dps_ref, nxt_ref, mix_ref, o_ref):
        i = pl.program_id(0)
        dps = dps_ref[...]
        mixv = mix_ref[...]
        nxt_row = nxt_ref[0:1, :] * jnp.where(i < SEQ // TR - 1, 1.0, 0.0)
        row = lax.broadcasted_iota(jnp.int32, dps.shape, 0)
        up = jnp.where(row == TR - 1, nxt_row, pltpu.roll(dps, TR - 1, 0))
        dp = dps * (1.0 - mixv) + up * mixv
        o_ref[...] = jnp.concatenate([dq_ref[...], dkv_ref[...], dp], axis=1).astype(BF16)

    return pl.pallas_call(
        body, name="assemble_dproj", grid=(SEQ // TR,),
        in_specs=[_rows(TR, D_ATTN), _rows(TR, 2 * D_KV), _rows(TR, RWKV_COLS),
                  pl.BlockSpec((HALO, RWKV_COLS), lambda i: (jnp.minimum((i + 1) * (TR // HALO), last), 0)),
                  _const((1, RWKV_COLS))],
        out_specs=_rows(TR, D_IN),
        out_shape=jax.ShapeDtypeStruct((SEQ, D_IN), BF16),
        compiler_params=_cp(("parallel",)),
    )(dq, dkv, dps, dps, mix)


N_PAIR = D_RWKV // LANES
CHUNK = 64
N_CHUNK = SEQ // CHUNK
GROUP = 64
STATE = (N_PAIR, HEAD_DIM, LANES)


def _lane_sums(lhs_tiles, ones2):
    out = _dot(jnp.concatenate(lhs_tiles, axis=0), ones2)
    return [out[i * HEAD_DIM:(i + 1) * HEAD_DIM] for i in range(len(lhs_tiles))]


def _seg_sum(xs, ones2):
    return _lane_sums([jnp.concatenate(_split(x, 2), axis=1) for x in xs], ones2)


def _seg_sum_rows(xs, ones2):
    out = _dot(jnp.concatenate(_split(jnp.concatenate(xs, axis=0), 2), axis=1), ones2)
    return [out[i * GROUP:(i + 1) * GROUP] for i in range(len(xs))]


def _col_form(rows, diag, ones2):
    zero = jnp.zeros((HEAD_DIM, LANES), BF16)
    tiles = []
    for row in rows:
        hi = row.astype(BF16)
        lo = (row - hi.astype(F32)).astype(BF16)
        tiles.append(jnp.concatenate(
            [jnp.where(diag, jnp.broadcast_to(part, (HEAD_DIM, LANES)), zero) for part in (hi, lo)], axis=1))
    return _lane_sums(tiles, ones2)


def _scan_consts():
    ones2 = jnp.concatenate([_head_ones(LANES)] * 2, axis=0)
    sub = lax.broadcasted_iota(jnp.int32, (HEAD_DIM, LANES), 0)
    lane_in_head = lax.broadcasted_iota(jnp.int32, (HEAD_DIM, LANES), 1) & (HEAD_DIM - 1)
    return ones2, lane_in_head == sub, lane_in_head


def _rows_of_columns(tile):
    t = tile.T
    return jnp.concatenate([t[:CHUNK], t[HEAD_DIM:HEAD_DIM + CHUNK]], axis=1)


def _pair(j):
    return slice(j * LANES, (j + 1) * LANES)


def _scan_fwd(r, w, k, v, kkn, b):
    def body(r_ref, w_ref, k_ref, v_ref, kkn_ref, b_ref, o_ref, st_ref, sa_ref, s_scr):
        c = pl.program_id(0)
        ones2, diag, lane_in_head = _scan_consts()

        @pl.when(c == 0)
        def _():
            s_scr[...] = jnp.zeros_like(s_scr)

        def group(gi, carry):
            row0 = pl.multiple_of(gi * GROUP, GROUP)
            states, ocols = list(carry[:N_PAIR]), list(carry[N_PAIR:])
            tiles = [[t[pl.ds(row0, GROUP), _pair(j)] for t in (r_ref, w_ref, k_ref, v_ref, kkn_ref, b_ref)]
                     for j in range(N_PAIR)]
            def row(j, name, u):
                return tiles[j]["rwkvnb".index(name)][u:u + 1]

            def emit_out(u, after):
                outs = _seg_sum([s[j] * row(j, "r", u + d) for d, s in enumerate(after) for j in range(N_PAIR)], ones2)
                for d in range(2):
                    here = lane_in_head == gi * GROUP + u + d
                    for j in range(N_PAIR):
                        ocols[j] = jnp.where(here, outs[d * N_PAIR + j], ocols[j])

            def vcols_of(u):
                cols = _col_form([row(j, "v", u + d) for d in range(2) for j in range(N_PAIR)], diag, ones2)
                return cols[:N_PAIR], cols[N_PAIR:]

            n_next = [pltpu.roll(tiles[j][4], GROUP - 1, 0) for j in range(N_PAIR)]
            dots = _seg_sum_rows([tiles[j][5] * n_next[j] for j in range(N_PAIR)]
                                 + [tiles[j][2] * n_next[j] for j in range(N_PAIR)], ones2)
            b_n, k_n = dots[:N_PAIR], dots[N_PAIR:]
            w_n = [tiles[j][1] * n_next[j] for j in range(N_PAIR)]

            vcols = vcols_of(0)
            after = None
            for u in range(0, GROUP, 2):
                prods = _seg_sum([states[j] * row(j, "n", u) for j in range(N_PAIR)]
                                 + [states[j] * w_n[j][u:u + 1] for j in range(N_PAIR)], ones2)
                if after is not None:
                    emit_out(u - 2, after)
                nxt = vcols_of(u + 2) if u + 2 < GROUP else None
                first, second = [], []
                for j in range(N_PAIR):
                    sa1 = prods[j]
                    sa2 = prods[N_PAIR + j] + sa1 * b_n[j][u:u + 1] + vcols[0][j] * k_n[j][u:u + 1]
                    s1 = states[j] * row(j, "w", u) + sa1 * row(j, "b", u) + vcols[0][j] * row(j, "k", u)
                    s2 = s1 * row(j, "w", u + 1) + sa2 * row(j, "b", u + 1) + vcols[1][j] * row(j, "k", u + 1)
                    st_ref[row0 + u, j] = s1
                    sa_ref[row0 + u, j] = sa1
                    st_ref[row0 + u + 1, j] = s2
                    sa_ref[row0 + u + 1, j] = sa2
                    first.append(s1)
                    second.append(s2)
                    states[j] = s2
                after, vcols = (first, second), nxt
            emit_out(GROUP - 2, after)
            return tuple(states + ocols)

        zero = jnp.zeros((HEAD_DIM, LANES), F32)
        fin = lax.fori_loop(0, CHUNK // GROUP, group, tuple(s_scr[j] for j in range(N_PAIR)) + (zero,) * N_PAIR)
        for j in range(N_PAIR):
            s_scr[j] = fin[j]
            o_ref[:, _pair(j)] = _rows_of_columns(fin[N_PAIR + j])

    blk = pl.BlockSpec((CHUNK, D_RWKV), lambda c: (c, 0))
    per_step = pl.BlockSpec((CHUNK,) + STATE, lambda c: (c, 0, 0, 0))
    return pl.pallas_call(
        body, name="rwkv_scan_fwd", grid=(N_CHUNK,),
        in_specs=[blk] * 6,
        out_specs=[blk, per_step, per_step],
        out_shape=[jax.ShapeDtypeStruct((SEQ, D_RWKV), F32)] + [jax.ShapeDtypeStruct((SEQ,) + STATE, F32)] * 2,
        scratch_shapes=[pltpu.VMEM(STATE, F32)],
        compiler_params=_cp(("arbitrary",)),
    )(r, w, k, v, kkn, b)


def _scan_bwd(r, w, k, v, kkn, b, do, states, sas, ds_in, prev, name, first_chunk, n_chunks):
    top = first_chunk + n_chunks - 1

    def body(r_ref, w_ref, k_ref, v_ref, kkn_ref, b_ref, do_ref, st_ref, before_ref, sa_ref, ds_in_ref, *rest):
        dr_ref, dw_ref, dk_ref, dv_ref, dkkn_ref, db_ref, ds_out_ref, ds_scr = rest[-8:]
        i = pl.program_id(0)
        ones2, diag, lane_in_head = _scan_consts()

        @pl.when(i == 0)
        def _():
            ds_scr[...] = ds_in_ref[...]

        entry = [before_ref[0, j] * jnp.where(i < top, 1.0, 0.0) for j in range(N_PAIR)]

        def reverse(gr, carry):
            gi = CHUNK // GROUP - 1 - gr
            row0 = pl.multiple_of(gi * GROUP, GROUP)
            dstates, dvcols = list(carry[:N_PAIR]), list(carry[N_PAIR:])
            tiles = [[t[pl.ds(row0, GROUP), _pair(j)]
                      for t in (r_ref, w_ref, k_ref, v_ref, kkn_ref, b_ref, do_ref)] for j in range(N_PAIR)]
            rows = [[[None] * GROUP for _ in range(5)] for _ in range(N_PAIR)]

            def row(j, name, u):
                return tiles[j]["rwkvnbd".index(name)][u:u + 1]

            def cols_of(u):
                cols = _col_form([row(j, name, u - d) for d in range(2) for name in "dv" for j in range(N_PAIR)],
                                 diag, ones2)
                return [[(cols[(2 * d) * N_PAIR + j], cols[(2 * d + 1) * N_PAIR + j]) for j in range(N_PAIR)]
                        for d in range(2)]

            def emit_dv(u, dsps):
                outs = _seg_sum([dsp[j] * row(j, "k", u - d) for d, dsp in enumerate(dsps) for j in range(N_PAIR)], ones2)
                for d in range(2):
                    here = lane_in_head == gi * GROUP + u - d
                    for j in range(N_PAIR):
                        dvcols[j] = jnp.where(here, outs[d * N_PAIR + j], dvcols[j])

            b_prev = [pltpu.roll(tiles[j][5], 1, 0) for j in range(N_PAIR)]
            dots = _seg_sum_rows([tiles[j][4] * b_prev[j] for j in range(N_PAIR)]
                                 + [tiles[j][0] * tiles[j][5] for j in range(N_PAIR)], ones2)
            n_b, r_b = dots[:N_PAIR], dots[N_PAIR:]
            w_b = [tiles[j][1] * b_prev[j] for j in range(N_PAIR)]

            def outputs(u, j, dsp, dsa, docol, vcol):
                tl = gi * GROUP + u
                if u > 0:
                    s_prev = st_ref[tl - 1, j]
                else:
                    s_prev = jnp.where(gi == 0, entry[j], st_ref[jnp.maximum(tl - 1, 0), j])
                rows[j][0][u] = jnp.sum(st_ref[tl, j] * docol, axis=0, keepdims=True)
                rows[j][1][u] = jnp.sum(dsp * s_prev, axis=0, keepdims=True)
                rows[j][2][u] = jnp.sum(dsp * vcol, axis=0, keepdims=True)
                rows[j][3][u] = jnp.sum(s_prev * dsa, axis=0, keepdims=True)
                rows[j][4][u] = jnp.sum(dsp * sa_ref[tl, j], axis=0, keepdims=True)

            cols = cols_of(GROUP - 1)
            before = None
            for u in range(GROUP - 1, 0, -2):
                dsp1 = [dstates[j] + cols[0][j][0] * row(j, "r", u) for j in range(N_PAIR)]
                prods = _seg_sum([dsp1[j] * row(j, "b", u) for j in range(N_PAIR)]
                                 + [dsp1[j] * w_b[j][u:u + 1] for j in range(N_PAIR)], ones2)
                if before is not None:
                    emit_dv(u + 2, before)
                nxt = cols_of(u - 2) if u >= 2 else None
                dsp2 = []
                for j in range(N_PAIR):
                    dsa1 = prods[j]
                    dsa2 = prods[N_PAIR + j] + dsa1 * n_b[j][u:u + 1] + cols[1][j][0] * r_b[j][u - 1:u]
                    mid = dsp1[j] * row(j, "w", u) + dsa1 * row(j, "n", u) + cols[1][j][0] * row(j, "r", u - 1)
                    outputs(u, j, dsp1[j], dsa1, *cols[0][j])
                    outputs(u - 1, j, mid, dsa2, *cols[1][j])
                    dstates[j] = mid * row(j, "w", u - 1) + dsa2 * row(j, "n", u - 1)
                    dsp2.append(mid)
                before, cols = (dsp1, dsp2), nxt
            emit_dv(1, before)
            for j in range(N_PAIR):
                for ref, rr in zip((dr_ref, dw_ref, dk_ref, dkkn_ref, db_ref), rows[j]):
                    ref[pl.ds(row0, GROUP), _pair(j)] = jnp.concatenate(rr, axis=0)
            return tuple(dstates + dvcols)

        zero = jnp.zeros((HEAD_DIM, LANES), F32)
        dfin = lax.fori_loop(0, CHUNK // GROUP, reverse, tuple(ds_scr[j] for j in range(N_PAIR)) + (zero,) * N_PAIR)
        for j in range(N_PAIR):
            ds_scr[j] = dfin[j]
            dv_ref[:, _pair(j)] = _rows_of_columns(dfin[N_PAIR + j])

        @pl.when(i == n_chunks - 1)
        def _():
            ds_out_ref[...] = ds_scr[...]

    blk = pl.BlockSpec((CHUNK, D_RWKV), lambda i: (top - i, 0))
    per_step = pl.BlockSpec((CHUNK,) + STATE, lambda i: (top - i, 0, 0, 0))
    step_before = pl.BlockSpec((1,) + STATE, lambda i: (jnp.maximum((top - i) * CHUNK - 1, 0), 0, 0, 0))
    prev = [] if prev is None else list(prev)
    outs = pl.pallas_call(
        body, name=name, grid=(n_chunks,),
        in_specs=[blk] * 7 + [per_step, step_before, per_step, _const(STATE)] + [ANY] * len(prev),
        out_specs=[blk] * 6 + [_const(STATE)],
        out_shape=[jax.ShapeDtypeStruct((SEQ, D_RWKV), F32)] * 6 + [jax.ShapeDtypeStruct(STATE, F32)],
        scratch_shapes=[pltpu.VMEM(STATE, F32)],
        input_output_aliases={11 + t: t for t in range(len(prev))},
        compiler_params=_cp(("arbitrary",)),
    )(r, w, k, v, kkn, b, do, states, states, sas, ds_in, *prev)
    return outs[:6], outs[6]


def _stacked(rows, cols, pick):
    return pl.BlockSpec((None, rows, cols), pick)


def _local_step(x, target, sm, win_st):
    def tied(t, token):
        return t if token is None else t + token[0:1, 0:1].reshape((1,) * t.ndim)

    zpad = jnp.zeros((LORA_DECAY, D_RWKV), F32)
    prm = [sm["w0"], jnp.concatenate([sm["w_decay_up"], zpad], axis=0), sm["a0"],
           jnp.concatenate([zpad, sm["w_iclr_up"]], axis=0), sm["w_gate_up"], sm["k_k"], sm["k_a"]]
    mix = sm["rwkv_shift_mix"]
    onehot = jnp.asarray(_t5_onehot(), BF16)
    sinks = sm["sinks"].reshape(N_Q_HEADS)
    lng, lnb, rk = sm["ln_x_g"], sm["ln_x_b"], sm["r_k"].reshape(1, D_RWKV)

    h1 = _norm_cast(x, sm["norm_mix_pre"], "norm_in")
    proj = _matmul(h1, win_st, "nn", "proj", m=SEQ, n=D_IN, k=D_MODEL, tm=SEQ, tn=640,
                   b_spec=_stacked(D_MODEL, 640, lambda i, j: (j, 0, 0)))
    bias = _bias_table(sm["rel_bias"].T, onehot).reshape(N_KV_HEADS, Q_PER_KV * BLOCK, 2 * BLOCK)
    attn = _attn_fwd(proj, bias, sinks)
    r, w, k2, v, kkn, b, g = _rwkv_prep(proj, mix, prm)
    o, states, sas = _scan_fwd(r, w, k2, v, kkn, b)
    wout, wup_st, wdown = yield ("rest_weights", o)
    cat = _rwkv_post(o, r, k2, v, g, lng, lnb, rk, attn)
    mixo = _matmul(cat, wout, "nn", "out_proj", m=SEQ, n=D_MODEL, k=D_MODEL, tm=SEQ, tn=512)
    x2, h3 = _mix_norm(x, mixo, sm["norm_mix_post"], sm["norm_ffn_pre"])
    u_gate, u_val, dact_dgate, dact_dval, act = _ffn_up_act(h3, wup_st, sm["conv_w"], sm["conv_b"])
    f = _matmul(act, wdown, "nn", "ffn_down", m=SEQ, n=D_MODEL, k=D_FF, tm=1024, tn=512)
    loss, dy, df, d_g4 = _loss_head(x2, f, sm["norm_ffn_post"], target)

    d_wdown = _matmul(act, df, "tn", "d_wdown", m=D_FF, n=D_MODEL, k=SEQ, tm=1024, tn=D_MODEL)
    du, d_convw, d_convb, d_wup = _ffn_act_bwd(u_gate, u_val, dact_dgate, dact_dval, df, wdown, sm["conv_w"], h3)
    d_convw = d_convw.transpose(1, 0, 2).reshape(3, 2 * D_FF)
    d_convb = d_convb.reshape(1, 2 * D_FF)
    dh3 = _matmul_nt_shards(du, wup_st, "d_h3", m=SEQ, n=D_MODEL, tm=512, tn=512,
                            a_spec=pl.BlockSpec((2, 512, D_FF), lambda i, j: (0, i, 0)),
                            a_piece=lambda ref, s: ref[s // 2, :, (s % 2) * 2048:(s % 2 + 1) * 2048])
    dx2, dmix, d_g2, d_g3 = _mid_bwd(x2, mixo, dy, dh3, sm["norm_mix_post"], sm["norm_ffn_pre"])
    dcat = _matmul(dmix, wout, "nt", "d_cat", m=SEQ, n=D_MODEL, k=D_MODEL, tm=SEQ, tn=512)
    d_wout = _matmul(cat, dmix, "tn", "d_wout", m=D_MODEL, n=D_MODEL, k=SEQ, tm=512, tn=D_MODEL)
    token = yield ("grads_a", (d_wdown, d_wup, d_wout))
    do, dr_p, dk_p, dv_p, dg, d_lng, d_lnb, d_rk = _rwkv_post_bwd(o, r, k2, v, g, lng, tied(lnb, token), rk, dcat)
    half = N_CHUNK // 2
    ds_end = jnp.zeros(STATE, F32)
    late, ds_mid = _scan_bwd(r, w, k2, v, kkn, b, do, states, sas, ds_end, None, "rwkv_scan_bwd_late", half, half)
    token = yield ("seam_1", ds_mid)
    scan_cts, ds_first = _scan_bwd(r, w, k2, v, kkn, b, do, states, sas, tied(ds_mid, token), late,
                                   "rwkv_scan_bwd_early", 0, half)
    dr_s, dw_s, dk_s, dv_s, dkkn_s, db_s = scan_cts
    token = yield ("seam_2", ds_first)
    prep_grads = _rwkv_prep_bwd(proj, tied(mix, token), prm,
                                (dr_s, dr_p, dw_s, dk_s, dk_p, dv_s, dv_p, dkkn_s, db_s, dg))
    dps, d_mix, d_w0, d_wdu, d_a0, d_wiu, d_wgu, d_kk, d_ka = prep_grads
    dq, dkv, dbias, dsink = _attn_bwd(proj, bias, sinks, dcat)
    d_relb = _bias_table_bwd(dbias.reshape(N_Q_HEADS, N_REL), onehot).T
    dproj = _assemble_dproj(dq, dkv, dps, mix)
    d_win = _matmul(h1, dproj, "tn", "d_win", m=D_MODEL, n=D_IN, k=SEQ, tm=D_MODEL, tn=640,
                    out=((N_CHIPS, D_MODEL, 640), _stacked(D_MODEL, 640, lambda i, j: (j, 0, 0))))
    token = yield ("grads_b", d_win)
    dh1 = _matmul_nt_shards(dproj, win_st, "d_h1", m=SEQ, n=D_MODEL, tm=1024, tn=D_MODEL,
                            a_spec=pl.BlockSpec((1024, D_IN), lambda i, j: (i, 0)),
                            a_piece=lambda ref, s: ref[:, s * 640:(s + 1) * 640])
    grad_x, d_g1 = _first_bwd(x, dx2, dh1, tied(sm["norm_mix_pre"], token))

    grads = {
        "norm_mix_pre": d_g1, "norm_mix_post": d_g2, "norm_ffn_pre": d_g3, "norm_ffn_post": d_g4,
        "w_in": d_win, "rel_bias": d_relb, "sinks": dsink[:, 0].reshape(1, N_Q_HEADS),
        "rwkv_shift_mix": d_mix, "w0": d_w0, "w_decay_up": d_wdu[:LORA_DECAY], "a0": d_a0,
        "w_iclr_up": d_wiu[LORA_DECAY:], "w_gate_up": d_wgu, "k_k": d_kk, "k_a": d_ka,
        "r_k": d_rk.reshape(1, N_Q_HEADS, HEAD_DIM), "ln_x_g": d_lng, "ln_x_b": d_lnb,
        "w_out": d_wout, "w_ffn_up": d_wup, "conv_w": d_convw, "conv_b": d_convb, "w_ffn_down": d_wdown,
    }
    return loss, grad_x, grads


def _place():
    x, y, c = lax.axis_index("x"), lax.axis_index("y"), lax.axis_index("c")
    chips = [(1 - x, y), (x, 1 - y), (1 - x, 1 - y)]
    return x, y, c, chips


def _remote(src, dst, sems, idx, to):
    return pltpu.make_async_remote_copy(src_ref=src, dst_ref=dst, send_sem=sems[0].at[idx], recv_sem=sems[1].at[idx],
                                        device_id=to, device_id_type=MESH)


ROW_ALIGN = 16


def _half(c, rows):
    return pl.ds(pl.multiple_of(c * (rows // 2), ROW_ALIGN), rows // 2)


def _gather_weights(big, small):
    nb, ns = len(big), len(small)

    def body(*refs):
        ins, outs = refs[:nb + ns], refs[nb + ns:2 * (nb + ns)]
        ici, d2d, sml, loc = refs[2 * (nb + ns):2 * (nb + ns) + 2], refs[-5:-3], refs[-3:-1], refs[-1]
        x, y, c, chips = _place()
        me = 2 * x + y
        sib = (x, y, 1 - c)
        local = [pltpu.make_async_copy(ins[a], outs[a].at[me], loc.at[a]) for a in range(nb + ns)]
        for cp in local:
            cp.start()
        sends = []
        for a in range(nb):
            rows = _half(c, big[a].shape[0])
            for kk, chip in enumerate(chips):
                sends.append(_remote(ins[a].at[rows], outs[a].at[me, rows], ici, a * 3 + kk, (*chip, c)))
        for a in range(ns):
            for kk, chip in enumerate(chips):
                sends.append(_remote(ins[nb + a], outs[nb + a].at[me], sml, a * 3 + kk, (*chip, c)))
        for cp in sends:
            cp.start()
        passed = []
        for a in range(nb):
            rows = _half(c, big[a].shape[0])
            for kk, (px, py) in enumerate(chips):
                got = outs[a].at[2 * px + py, rows]
                _remote(got, got, ici, a * 3 + kk, sib).wait_recv()
                fwd = _remote(got, got, d2d, a * 3 + kk, sib)
                fwd.start()
                passed.append(fwd)
        for a in range(nb):
            other = _half(1 - c, big[a].shape[0])
            for kk, (px, py) in enumerate(chips):
                land = outs[a].at[2 * px + py, other]
                _remote(land, land, d2d, a * 3 + kk, sib).wait_recv()
        for a in range(ns):
            for kk, (px, py) in enumerate(chips):
                land = outs[nb + a].at[2 * px + py]
                _remote(land, land, sml, a * 3 + kk, sib).wait_recv()
        for cp in sends + passed:
            cp.wait_send()
        for cp in local:
            cp.wait()

    arrs = list(big) + list(small)
    in_vmem = pl.BlockSpec(memory_space=pltpu.VMEM)
    return pl.pallas_call(
        body, name="gather_weights",
        in_specs=[in_vmem] * len(arrs), out_specs=[in_vmem] * len(arrs),
        out_shape=[jax.ShapeDtypeStruct((N_CHIPS,) + t.shape, t.dtype) for t in arrs],
        scratch_shapes=[pltpu.SemaphoreType.DMA((3 * nb,)), pltpu.SemaphoreType.DMA((3 * nb,)),
                        pltpu.SemaphoreType.DMA((3 * nb,)), pltpu.SemaphoreType.DMA((3 * nb,)),
                        pltpu.SemaphoreType.DMA((3 * ns,)), pltpu.SemaphoreType.DMA((3 * ns,)),
                        pltpu.SemaphoreType.DMA((nb + ns,))],
        compiler_params=pltpu.CompilerParams(has_side_effects=True, vmem_limit_bytes=VMEM_LIMIT),
    )(*arrs)


HBM = pl.BlockSpec(memory_space=pltpu.HBM)
SEM = pl.BlockSpec(memory_space=pltpu.SEMAPHORE)
EFFECT = pltpu.SideEffectType.DATAFLOW_SIDE_EFFECTING


def _copies_start(name, bufs, plan, n, partners=None):
    nb = len(bufs)

    def body(*refs):
        ins, sems, token = refs[:nb], refs[nb:nb + 2 * n], refs[-1]
        if partners is not None:
            barrier = pltpu.get_barrier_semaphore()
            peers = partners[1]()
            for peer in peers:
                pl.semaphore_signal(barrier, inc=1, device_id=peer, device_id_type=MESH)
            pl.semaphore_wait(barrier, len(peers))
        for kk, (src, dst, dev) in enumerate(plan(ins)):
            pltpu.make_async_remote_copy(src_ref=src, dst_ref=dst, send_sem=sems[2 * kk], recv_sem=sems[2 * kk + 1],
                                         device_id=dev, device_id_type=MESH).start()
        token[...] = jnp.zeros_like(token)

    outs = pl.pallas_call(
        body, name=name,
        out_shape=tuple([pltpu.SemaphoreType.DMA(())] * (2 * n) + [pltpu.HBM(t.shape, t.dtype) for t in bufs]
                        + [jax.ShapeDtypeStruct((8, LANES), F32)]),
        in_specs=[HBM] * nb,
        out_specs=tuple([SEM] * (2 * n) + [HBM] * nb + [pl.BlockSpec(memory_space=pltpu.VMEM)]),
        input_output_aliases={t: 2 * n + t for t in range(nb)},
        compiler_params=pltpu.CompilerParams(has_side_effects=EFFECT,
                                             collective_id=None if partners is None else partners[0]),
    )(*[pltpu.with_memory_space_constraint(t, pltpu.HBM) for t in bufs])
    return outs[:2 * n], outs[2 * n:2 * n + nb], outs[-1]


def _copies_wait(name, sems, bufs, plan, n, after):
    nb = len(bufs)
    after = list(after) if isinstance(after, (list, tuple)) else [after]

    def body(*refs):
        ins, sem_refs = refs[:nb], refs[nb:nb + 2 * n]
        for kk, (src, dst, dev) in enumerate(plan(ins)):
            cp = pltpu.make_async_remote_copy(src_ref=src, dst_ref=dst, send_sem=sem_refs[2 * kk],
                                              recv_sem=sem_refs[2 * kk + 1], device_id=dev, device_id_type=MESH)
            cp.wait_send()
            cp.wait_recv()

    return pl.pallas_call(
        body, name=name,
        out_shape=tuple(pltpu.HBM(t.shape, t.dtype) for t in bufs),
        in_specs=[HBM] * nb + [SEM] * (2 * n) + [ANY] * len(after),
        out_specs=tuple([HBM] * nb),
        input_output_aliases={t: t for t in range(nb)},
        compiler_params=pltpu.CompilerParams(has_side_effects=EFFECT),
    )(*bufs, *sems, *after)


def _plan_gather(n_w):
    def plan(refs):
        x, y, c, chips = _place()
        me = 2 * x + y
        return [(refs[a], refs[n_w + a].at[me], (*chip, c)) for a in range(n_w) for chip in chips + [(x, y)]]
    return plan


def _plan_pair_halves(n_g, rows):
    def plan(refs):
        x, y, c, _ = _place()
        return [(refs[a].at[:, _half(1 - c, rows[a])], refs[n_g + a], (x, y, 1 - c)) for a in range(n_g)]
    return plan


def _plan_chip_parts(n_g):
    def plan(refs):
        x, y, c, chips = _place()
        me = 2 * x + y
        return [(refs[a].at[2 * px + py], refs[n_g + a].at[me], (px, py, c))
                for a in range(n_g) for (px, py) in chips]
    return plan


def _plan_pair_fill(n_g, rows):
    def plan(refs):
        x, y, c, _ = _place()
        return [(refs[a].at[_half(c, rows[a])], refs[a].at[_half(c, rows[a])], (x, y, 1 - c)) for a in range(n_g)]
    return plan


def _pair_add(g, got, name):
    _, rows, cols = g.shape
    hr = rows // 2
    tr = min(hr, 512)
    nb = hr // tr

    def body(g_ref, got_ref, p_ref, own_ref):
        val = (g_ref[...] + got_ref[...]).astype(BF16)
        p_ref[...] = val

        @pl.when(pl.program_id(1) == 2 * lax.axis_index("x") + lax.axis_index("y"))
        def _():
            own_ref[...] = val

    def mine(i, s):
        return (2 * lax.axis_index("x") + lax.axis_index("y"), i, 0)

    return pl.pallas_call(
        body, name=name, grid=(nb, N_CHIPS),
        in_specs=[pl.BlockSpec((None, tr, cols), lambda i, s: (s, lax.axis_index("c") * nb + i, 0)),
                  pl.BlockSpec((None, tr, cols), lambda i, s: (s, i, 0))],
        out_specs=[pl.BlockSpec((None, tr, cols), lambda i, s: (s, i, 0)), pl.BlockSpec((None, tr, cols), mine)],
        out_shape=[jax.ShapeDtypeStruct((N_CHIPS, hr, cols), BF16)] * 2,
        compiler_params=_cp(("parallel", "arbitrary")),
    )(g, got)


def _chip_sum(parts, name):
    _, hr, cols = parts.shape
    tr = min(hr, 256)
    nb = hr // tr

    def body(t_ref, o_ref):
        part = [t_ref[s].astype(F32) for s in range(N_CHIPS)]
        o_ref[...] = ((part[0] + part[1]) + part[2]) + part[3]

    return pl.pallas_call(
        body, name=name, grid=(nb,),
        in_specs=[pl.BlockSpec((N_CHIPS, tr, cols), lambda i: (0, i, 0))],
        out_specs=pl.BlockSpec((tr, cols), lambda i: (lax.axis_index("c") * nb + i, 0)),
        out_shape=jax.ShapeDtypeStruct((2 * hr, cols), F32),
        compiler_params=_cp(("parallel",)),
    )(parts)


class _Reduction:
    def __init__(self, tag, rows, first_id):
        self.tag, self.n, self.rows, self.first_id = tag, len(rows), rows, first_id
        self.plans = (_plan_pair_halves(self.n, rows), _plan_chip_parts(self.n), _plan_pair_fill(self.n, rows))
        self.flight = None

    def _name(self, what):
        return f"grad_{self.tag}_{what}"

    @staticmethod
    def _sibling():
        x, y, c, _ = _place()
        return [(x, y, 1 - c)]

    @staticmethod
    def _same_core_elsewhere():
        x, y, c, chips = _place()
        return [(*chip, c) for chip in chips]

    def start(self, gs):
        gots = [lax.empty((N_CHIPS, t.shape[1] // 2, t.shape[2]), F32) for t in gs]
        self.flight = _copies_start(self._name("pair_start"), list(gs) + gots, self.plans[0], self.n,
                                    (self.first_id, self._sibling))
        return self.flight[2]

    def after_pair(self, after):
        sems, bufs, _ = self.flight
        out = _copies_wait(self._name("pair_wait"), sems, bufs, self.plans[0], self.n, after)
        sums = [_pair_add(g, got, self._name(f"pair_add_{i}"))
                for i, (g, got) in enumerate(zip(out[:self.n], out[self.n:]))]
        self.flight = _copies_start(self._name("chip_start"), [p for p, _ in sums] + [own for _, own in sums],
                                    self.plans[1], 3 * self.n, (self.first_id + 1, self._same_core_elsewhere))
        return self.flight[2]

    def after_chips(self, after):
        sems, bufs, _ = self.flight
        out = _copies_wait(self._name("chip_wait"), sems, bufs, self.plans[1], 3 * self.n, after)
        fulls = [_chip_sum(t, self._name(f"chip_sum_{i}")) for i, t in enumerate(out[self.n:])]
        self.flight = _copies_start(self._name("fill_start"), fulls, self.plans[2], self.n,
                                    (self.first_id + 2, self._sibling))
        return self.flight[2]

    def finish(self, after):
        sems, bufs, _ = self.flight
        return _copies_wait(self._name("fill_wait"), sems, bufs, self.plans[2], self.n, after)


def _adamw_math(w, g, m, v):
    nm = ADAM_B1 * m + (1.0 - ADAM_B1) * g
    nv = ADAM_B2 * v + (1.0 - ADAM_B2) * (g * g)
    m_hat = nm / (1.0 - ADAM_B1 ** ADAM_STEP)
    v_hat = nv / (1.0 - ADAM_B2 ** ADAM_STEP)
    return -ADAM_LR * (m_hat / (jnp.sqrt(v_hat) + ADAM_EPS) + ADAM_WD * w), nm, nv


def _adamw(w, g, m, v, name, tr):
    r, cdim = w.shape

    def body(w_ref, g_ref, m_ref, v_ref, d_ref, nm_ref, nv_ref):
        d_ref[...], nm_ref[...], nv_ref[...] = _adamw_math(w_ref[...], g_ref[...], m_ref[...], v_ref[...])

    return pl.pallas_call(
        body, name=name, grid=(r // tr,), in_specs=[_rows(tr, cdim)] * 4, out_specs=[_rows(tr, cdim)] * 3,
        out_shape=[jax.ShapeDtypeStruct((r, cdim), F32)] * 3, compiler_params=_cp(("parallel",)),
    )(w, g, m, v)


def _adamw_small(w, parts, m, v, shapes):
    n_rows = w.shape[0]

    def scatter(src, outs):
        row = 0
        for (rows, cols), out in zip(shapes, outs):
            if cols == LANES:
                out[...] = src[row:row + rows, :]
            elif cols > LANES:
                per = cols // LANES
                for r in range(rows):
                    for cb in range(per):
                        out[r:r + 1, cb * LANES:(cb + 1) * LANES] = src[row + r * per + cb:row + r * per + cb + 1, :]
            else:
                per = LANES // cols
                for r in range(rows):
                    out[r:r + 1, :] = src[row + r // per:row + r // per + 1, (r % per) * cols:(r % per + 1) * cols]
            row += -(-rows * cols // LANES)

    def body(w_ref, p_ref, m_ref, v_ref, *rest):
        outs, scr = rest[:-4], rest[-4:]
        g = p_ref[0]
        for dev in range(1, N_DEV):
            g = g + p_ref[dev]
        scr[3][...] = g
        scr[0][...], scr[1][...], scr[2][...] = _adamw_math(w_ref[...], g, m_ref[...], v_ref[...])
        n = len(shapes)
        for kind in range(4):
            scatter(scr[kind], outs[kind * n:(kind + 1) * n])

    outs = pl.pallas_call(
        body, name="adamw_small", grid=(1,),
        in_specs=[_const(w.shape), _const(parts.shape), _const(w.shape), _const(w.shape)],
        out_specs=[_const(s) for s in shapes] * 4, out_shape=[jax.ShapeDtypeStruct(s, F32) for s in shapes] * 4,
        scratch_shapes=[pltpu.VMEM((n_rows, LANES), F32)] * 4,
        compiler_params=_cp(("arbitrary",)),
    )(w, parts, m, v)
    n = len(shapes)
    return [outs[kind * n:(kind + 1) * n] for kind in range(4)]


REPLICATED = (("norm_mix_pre", 1024), ("norm_mix_post", 1024), ("norm_ffn_pre", 1024), ("norm_ffn_post", 1024),
              ("rel_bias", 256), ("sinks", 8), ("rwkv_shift_mix", 1792), ("w0", 512), ("a0", 512), ("k_k", 512),
              ("k_a", 512), ("r_k", 512), ("ln_x_g", 512), ("ln_x_b", 512), ("conv_b", 8192))
SMALL_SHARDED = (("w_decay_up", LORA_DECAY, D_RWKV), ("w_iclr_up", LORA_ICLR, D_RWKV),
                 ("w_gate_up", LORA_GATE, D_RWKV), ("conv_w", 3, 2 * D_FF))
BIG = (("w_in", D_MODEL, 640), ("w_out", 256, D_MODEL), ("w_ffn_up", D_MODEL, 2048), ("w_ffn_down", 1024, D_MODEL))
PACK_ALIGN = 8 * LANES


def _pack(pieces):
    flat = []
    for t in pieces:
        t = t.reshape(-1)
        pad = (-t.shape[0]) % LANES
        flat.append(jnp.pad(t, (0, pad)) if pad else t)
    flat = jnp.concatenate(flat)
    pad = (-flat.shape[0]) % PACK_ALIGN
    return jnp.pad(flat, (0, pad)).reshape(-1, LANES)


def kernel(x, norm_mix_pre, norm_mix_post, norm_ffn_pre, norm_ffn_post, w_in, rel_bias, sinks, rwkv_shift_mix, w0, w_decay_up, a0, w_iclr_up, w_gate_up, k_k, k_a, r_k, ln_x_g, ln_x_b, w_out, w_ffn_up, conv_w, conv_b, w_ffn_down, loss_target, m_norm_mix_pre, m_norm_mix_post, m_norm_ffn_pre, m_norm_ffn_post, m_w_in, m_rel_bias, m_sinks, m_rwkv_shift_mix, m_w0, m_w_decay_up, m_a0, m_w_iclr_up, m_w_gate_up, m_k_k, m_k_a, m_r_k, m_ln_x_g, m_ln_x_b, m_w_out, m_w_ffn_up, m_conv_w, m_conv_b, m_w_ffn_down, v_norm_mix_pre, v_norm_mix_post, v_norm_ffn_pre, v_norm_ffn_post, v_w_in, v_rel_bias, v_sinks, v_rwkv_shift_mix, v_w0, v_w_decay_up, v_a0, v_w_iclr_up, v_w_gate_up, v_k_k, v_k_a, v_r_k, v_ln_x_g, v_ln_x_b, v_w_out, v_w_ffn_up, v_conv_w, v_conv_b, v_w_ffn_down):
    given = dict(locals())
    names = [n for n, _ in REPLICATED] + [n for n, _, _ in SMALL_SHARDED] + [n for n, _, _ in BIG]
    order = ["norm_mix_pre", "norm_mix_post", "norm_ffn_pre", "norm_ffn_post", "w_in", "rel_bias", "sinks",
             "rwkv_shift_mix", "w0", "w_decay_up", "a0", "w_iclr_up", "w_gate_up", "k_k", "k_a", "r_k", "ln_x_g",
             "ln_x_b", "w_out", "w_ffn_up", "conv_w", "conv_b", "w_ffn_down"]
    assert sorted(names) == sorted(order)

    big_sh = {n: given[n].reshape(a, b).astype(BF16) for n, a, b in BIG}
    small_sh = [given[n].reshape(r, c // N_CHIPS) for n, r, c in SMALL_SHARDED]
    gathered = _gather_weights([big_sh["w_in"]], small_sh)
    rest = ("w_out", "w_ffn_up", "w_ffn_down")
    win_st, rest_sh = lax.optimization_barrier((gathered[0], [big_sh[n] for n in rest]))
    sm = {n: given[n] for n, _ in REPLICATED}
    sm["r_k"] = r_k.reshape(N_Q_HEADS, HEAD_DIM)
    for (n, r, c), st in zip(SMALL_SHARDED, gathered[1:]):
        sm[n] = st.transpose(1, 0, 2).reshape(r, c)

    lands = [lax.empty((N_CHIPS,) + t.shape, BF16) for t in rest_sh]
    plan_w = _plan_gather(len(rest))
    n_w = N_CHIPS * len(rest)
    w_sems, w_bufs, token = _copies_start("gather_rest_start", rest_sh + lands, plan_w, n_w)
    sm["norm_mix_pre"] = norm_mix_pre + token[0:1, 0:1]

    def on_rest_weights(after):
        out = _copies_wait("gather_rest_wait", w_sems, w_bufs, plan_w, n_w, after)
        wout_st, wup_st, wdown_st = out[3:]
        return wout_st.reshape(D_MODEL, D_MODEL), wup_st, wdown_st.reshape(D_FF, D_MODEL)

    red_a = _Reduction("a", (1024, D_MODEL, 256), first_id=0)
    red_b = _Reduction("b", (D_MODEL,), first_id=3)

    def on_grads_a(gs):
        d_wdown, d_wup, d_wout = gs
        return red_a.start([d_wdown.reshape(N_CHIPS, 1024, D_MODEL), d_wup, d_wout.reshape(N_CHIPS, 256, D_MODEL)])

    handlers = {"rest_weights": on_rest_weights, "grads_a": on_grads_a, "seam_1": red_a.after_pair,
                "seam_2": red_a.after_chips, "grads_b": lambda g: red_b.start([g])}
    steps = _local_step(x[0], loss_target[0], sm, win_st)
    kind, payload = next(steps)
    while True:
        try:
            kind, payload = steps.send(handlers[kind](payload))
        except StopIteration as done:
            loss, grad_x, grads = done.value
            break

    small_names = [n for n, _ in REPLICATED] + [n for n, _, _ in SMALL_SHARDED]

    def shard_cols(t, s):
        return t[:, s * (t.shape[1] // N_CHIPS):(s + 1) * (t.shape[1] // N_CHIPS)]

    for_chip = jnp.stack([_pack([loss[0]] + [grads[n] for n, _ in REPLICATED]
                                + [shard_cols(grads[n], s) for n, _, _ in SMALL_SHARDED]) for s in range(N_CHIPS)])
    land = lax.empty((N_DEV,) + for_chip.shape[1:], F32)

    def plan_small(refs):
        x, y, c, _ = _place()
        out = []
        for rel in range(N_DEV):
            px, py, pc = x ^ (rel >> 2), y ^ ((rel >> 1) & 1), c ^ (rel & 1)
            out.append((refs[0].at[2 * px + py], refs[1].at[4 * x + 2 * y + c], (px, py, pc)))
        return out

    s_sems, s_bufs, s_token = _copies_start("grad_small_start", [for_chip, land], plan_small, N_DEV)

    red_b.after_pair([grad_x, s_token])
    g_out = {}
    g_out["w_ffn_down"], g_out["w_ffn_up"], g_out["w_out"] = red_a.finish(grad_x)

    delta, new_m, new_v = {}, {}, {}

    def update(n, a, b):
        delta[n], new_m[n], new_v[n] = _adamw(given[n].reshape(a, b), g_out[n], given["m_" + n].reshape(a, b),
                                              given["v_" + n].reshape(a, b), "adamw_" + n, 256)

    for n, a, b in BIG[1:]:
        update(n, a, b)
    done = [delta[n] for n, _, _ in BIG[1:]]
    red_b.after_chips(done)
    parts = _copies_wait("grad_small_wait", s_sems, s_bufs, plan_small, N_DEV, done)[1]
    no_param = jnp.zeros((LANES,), F32)
    packs = [_pack([no_param] + [given[pre + n] for n in small_names]) for pre in ("", "m_", "v_")]

    def piece_shape(n):
        shape = given[n].shape
        rows, cols = int(np.prod(shape[:-1])), shape[-1]
        whole = cols % LANES == 0 or (LANES % cols == 0 and (rows * cols) % LANES == 0 and cols >= HEAD_DIM)
        return (rows, cols) if whole else (-(-rows * cols // LANES), LANES)

    shapes = [(1, LANES)] + [piece_shape(n) for n in small_names]
    upd = _adamw_small(packs[0], parts, packs[1], packs[2], shapes)
    loss = upd[3][0][0, 0]
    for i, n in enumerate(small_names):
        shape = given[n].shape
        size = int(np.prod(shape))
        delta[n], new_m[n], new_v[n], g_out[n] = (u[1 + i].reshape(-1)[:size].reshape(shape) for u in upd)
    g_out["w_in"], = red_b.finish(upd[0][0])
    update(*BIG[0])

    def shaped(d):
        return [d[n].reshape(given[n].shape) for n in order]

    return (loss, grad_x.reshape(x.shape), *shaped(g_out), *shaped(delta), *shaped(new_m), *shaped(new_v))
```

```python
import math

import numpy as np
import jax
import jax.numpy as jnp
from jax import lax
from jax.experimental import pallas as pl
from jax.experimental.pallas import tpu as pltpu

F32 = jnp.float32
BF16 = jnp.bfloat16
MESH = pl.DeviceIdType.MESH

SEQ = 2048
D_MODEL = 1024
HEAD_DIM = 64
D_ATTN = 512
D_RWKV = 512
D_KV = 128
N_Q_HEADS = 8
N_KV_HEADS = 2
Q_PER_KV = 4
BLOCK = 128
N_BUCKETS = 32
MAX_DISTANCE = 128
LORA_DECAY = 64
LORA_ICLR = 64
LORA_GATE = 128
RWKV_COLS = 3 * D_RWKV + LORA_DECAY + LORA_ICLR + LORA_GATE
P_OFF = D_ATTN + 2 * D_KV
D_IN = P_OFF + RWKV_COLS
D_FF = 4096
NORM_EPS = 1e-6
GN_EPS = 64e-5
NEG_INF = -1e30
N_CHIPS = 4
N_DEV = 8
HEAD_SHIFT = HEAD_DIM.bit_length() - 1
BLOCK_SHIFT = BLOCK.bit_length() - 1

ADAM_LR = 0.001
ADAM_B1 = 0.9
ADAM_B2 = 0.999
ADAM_EPS = 1e-08
ADAM_WD = 0.01
ADAM_STEP = 10

VMEM_LIMIT = 52 * 1024 * 1024
LANES = 128


def _cp(sem=None, vmem=VMEM_LIMIT):
    kw = dict(vmem_limit_bytes=vmem)
    if sem is not None:
        kw["dimension_semantics"] = sem
    return pltpu.CompilerParams(**kw)


def _rows(tr, nc):
    return pl.BlockSpec((tr, nc), lambda i: (i, 0))


def _const(shape):
    return pl.BlockSpec(shape, lambda *_: (0,) * len(shape))


ANY = pl.BlockSpec(memory_space=pl.ANY)


def _split(x, n):
    parts = []
    for _ in range(n - 1):
        h = x.astype(BF16)
        parts.append(h)
        x = x - h.astype(F32)
    parts.append(x.astype(BF16))
    return parts


NN = (((1,), (0,)), ((), ()))
NT = (((1,), (1,)), ((), ()))
TN = (((0,), (0,)), ((), ()))


def _dot(a, b, dn=NN):
    return lax.dot_general(a, b, dn, preferred_element_type=F32)


def _dot_ind(x, ind_bf16, n=3):
    acc = None
    for part in _split(x, n):
        t = _dot(part, ind_bf16)
        acc = t if acc is None else acc + t
    return acc


def _head_ones(n):
    r = lax.broadcasted_iota(jnp.int32, (n, n), 0) >> HEAD_SHIFT
    c = lax.broadcasted_iota(jnp.int32, (n, n), 1) >> HEAD_SHIFT
    return jnp.where(r == c, 1.0, 0.0).astype(BF16)


def _matmul(a, b, mode, name, *, m, n, k, tm, tn, a_spec=None, b_spec=None, out=None):
    dn = {"nn": NN, "nt": NT, "tn": TN}[mode]

    def body(a_ref, b_ref, o_ref):
        o_ref[...] = _dot(a_ref[...], b_ref[...], dn)

    if a_spec is None:
        a_spec = pl.BlockSpec((k, tm), lambda i, j: (0, i)) if mode == "tn" else pl.BlockSpec((tm, k), lambda i, j: (i, 0))
    if b_spec is None:
        b_spec = pl.BlockSpec((tn, k), lambda i, j: (j, 0)) if mode == "nt" else pl.BlockSpec((k, tn), lambda i, j: (0, j))
    return pl.pallas_call(
        body, name=name, grid=(m // tm, n // tn),
        in_specs=[a_spec, b_spec],
        out_specs=pl.BlockSpec((tm, tn), lambda i, j: (i, j)) if out is None else out[1],
        out_shape=jax.ShapeDtypeStruct((m, n) if out is None else out[0], F32),
        compiler_params=_cp(("parallel", "parallel")),
    )(a, b)


def _matmul_nt_shards(a, b_st, name, *, m, n, tm, tn, a_spec, a_piece):
    ks = b_st.shape[2]

    def body(a_ref, b_ref, o_ref):
        acc = _dot(a_piece(a_ref, 0), b_ref[0], NT)
        for s in range(1, N_CHIPS):
            acc = acc + _dot(a_piece(a_ref, s), b_ref[s], NT)
        o_ref[...] = acc

    return pl.pallas_call(
        body, name=name, grid=(m // tm, n // tn),
        in_specs=[a_spec, pl.BlockSpec((N_CHIPS, tn, ks), lambda i, j: (0, j, 0))],
        out_specs=pl.BlockSpec((tm, tn), lambda i, j: (i, j)),
        out_shape=jax.ShapeDtypeStruct((m, n), F32),
        compiler_params=_cp(("parallel", "parallel")),
    )(a, b_st)


def _rstd(x):
    return lax.rsqrt(jnp.mean(x * x, axis=-1, keepdims=True) + NORM_EPS)


def _rms_bwd(x, r, g, dy):
    gy = dy * g
    return r * gy - x * ((r * r * r) * (jnp.sum(x * gy, axis=-1, keepdims=True) / x.shape[-1]))


TR = 256
TRN = 512


def _norm_cast(x, g, name):
    def body(x_ref, g_ref, h_ref):
        x = x_ref[...]
        h_ref[...] = (x * _rstd(x) * g_ref[...]).astype(BF16)

    return pl.pallas_call(
        body, name=name, grid=(SEQ // TRN,),
        in_specs=[_rows(TRN, D_MODEL), _const((1, D_MODEL))],
        out_specs=_rows(TRN, D_MODEL),
        out_shape=jax.ShapeDtypeStruct((SEQ, D_MODEL), BF16),
        compiler_params=_cp(("parallel",)),
    )(x, g)


def _mix_norm(x, mix, g2, g3):
    def body(x_ref, mix_ref, g2_ref, g3_ref, x2_ref, h3_ref):
        mixv = mix_ref[...]
        x2 = x_ref[...] + mixv * _rstd(mixv) * g2_ref[...]
        x2_ref[...] = x2
        h3_ref[...] = (x2 * _rstd(x2) * g3_ref[...]).astype(BF16)

    return pl.pallas_call(
        body, name="mix_norm", grid=(SEQ // TRN,),
        in_specs=[_rows(TRN, D_MODEL), _rows(TRN, D_MODEL), _const((1, D_MODEL)), _const((1, D_MODEL))],
        out_specs=[_rows(TRN, D_MODEL), _rows(TRN, D_MODEL)],
        out_shape=[jax.ShapeDtypeStruct((SEQ, D_MODEL), F32), jax.ShapeDtypeStruct((SEQ, D_MODEL), BF16)],
        compiler_params=_cp(("parallel",)),
    )(x, mix, g2, g3)


def _loss_head(x2, f, g4, target):
    def body(x2_ref, f_ref, g4_ref, t_ref, loss_ref, dy_ref, df_ref, dg_ref):
        i = pl.program_id(0)
        f = f_ref[...]
        g4 = g4_ref[...]
        r = _rstd(f)
        e = x2_ref[...] + f * r * g4 - t_ref[...]
        dy = e * (1.0 / D_MODEL)
        dy_ref[...] = dy
        df_ref[...] = _rms_bwd(f, r, g4, dy).astype(BF16)
        part = 0.5 * jnp.sum(jnp.sum(e * e, axis=-1, keepdims=True), axis=0, keepdims=True) * (1.0 / D_MODEL)
        dg = jnp.sum(dy * f * r, axis=0, keepdims=True)

        @pl.when(i == 0)
        def _():
            loss_ref[...] = jnp.zeros_like(loss_ref)
            dg_ref[...] = jnp.zeros_like(dg_ref)

        loss_ref[...] += jnp.broadcast_to(part, loss_ref.shape)
        dg_ref[...] += dg

    return pl.pallas_call(
        body, name="loss_head", grid=(SEQ // TRN,),
        in_specs=[_rows(TRN, D_MODEL), _rows(TRN, D_MODEL), _const((1, D_MODEL)), _rows(TRN, D_MODEL)],
        out_specs=[_const((8, LANES)), _rows(TRN, D_MODEL), _rows(TRN, D_MODEL), _const((1, D_MODEL))],
        out_shape=[jax.ShapeDtypeStruct((8, LANES), F32), jax.ShapeDtypeStruct((SEQ, D_MODEL), F32),
                   jax.ShapeDtypeStruct((SEQ, D_MODEL), BF16), jax.ShapeDtypeStruct((1, D_MODEL), F32)],
        compiler_params=_cp(("arbitrary",)),
    )(x2, f, g4, target)


def _mid_bwd(x2, mix, dy, dh3, g2, g3):
    def body(x2_ref, mix_ref, dy_ref, dh3_ref, g2_ref, g3_ref, dx2_ref, dmix_ref, dg2_ref, dg3_ref):
        i = pl.program_id(0)
        x2 = x2_ref[...]
        mixv = mix_ref[...]
        dh3 = dh3_ref[...]
        r3 = _rstd(x2)
        dx2 = dy_ref[...] + _rms_bwd(x2, r3, g3_ref[...], dh3)
        dx2_ref[...] = dx2
        r2 = _rstd(mixv)
        dmix_ref[...] = _rms_bwd(mixv, r2, g2_ref[...], dx2).astype(BF16)

        @pl.when(i == 0)
        def _():
            dg2_ref[...] = jnp.zeros_like(dg2_ref)
            dg3_ref[...] = jnp.zeros_like(dg3_ref)

        dg3_ref[...] += jnp.sum(dh3 * x2 * r3, axis=0, keepdims=True)
        dg2_ref[...] += jnp.sum(dx2 * mixv * r2, axis=0, keepdims=True)

    return pl.pallas_call(
        body, name="mid_bwd", grid=(SEQ // TRN,),
        in_specs=[_rows(TRN, D_MODEL)] * 4 + [_const((1, D_MODEL))] * 2,
        out_specs=[_rows(TRN, D_MODEL), _rows(TRN, D_MODEL), _const((1, D_MODEL)), _const((1, D_MODEL))],
        out_shape=[jax.ShapeDtypeStruct((SEQ, D_MODEL), F32), jax.ShapeDtypeStruct((SEQ, D_MODEL), BF16),
                   jax.ShapeDtypeStruct((1, D_MODEL), F32), jax.ShapeDtypeStruct((1, D_MODEL), F32)],
        compiler_params=_cp(("arbitrary",)),
    )(x2, mix, dy, dh3, g2, g3)


def _first_bwd(x, dx2, dh1, g1):
    def body(x_ref, dx2_ref, dh1_ref, g1_ref, dx_ref, dg1_ref):
        i = pl.program_id(0)
        x = x_ref[...]
        dh1 = dh1_ref[...]
        r = _rstd(x)
        dx_ref[...] = dx2_ref[...] + _rms_bwd(x, r, g1_ref[...], dh1)

        @pl.when(i == 0)
        def _():
            dg1_ref[...] = jnp.zeros_like(dg1_ref)

        dg1_ref[...] += jnp.sum(dh1 * x * r, axis=0, keepdims=True)

    return pl.pallas_call(
        body, name="first_bwd", grid=(SEQ // TRN,),
        in_specs=[_rows(TRN, D_MODEL)] * 3 + [_const((1, D_MODEL))],
        out_specs=[_rows(TRN, D_MODEL), _const((1, D_MODEL))],
        out_shape=[jax.ShapeDtypeStruct((SEQ, D_MODEL), F32), jax.ShapeDtypeStruct((1, D_MODEL), F32)],
        compiler_params=_cp(("arbitrary",)),
    )(x, dx2, dh1, g1)


TC = 256
N_CB = D_FF // TC
GELU_C = math.sqrt(2.0 / math.pi)


def _shift_down(u, s):
    rolled = pltpu.roll(u, s, 0)
    row = lax.broadcasted_iota(jnp.int32, u.shape, 0)
    return jnp.where(row >= s, rolled, 0.0)


def _shift_up(u, s):
    n = u.shape[0]
    rolled = pltpu.roll(u, n - s, 0)
    row = lax.broadcasted_iota(jnp.int32, u.shape, 0)
    return jnp.where(row < n - s, rolled, 0.0)


def _conv3(u, w, b):
    return b + w[0:1] * _shift_down(u, 2) + w[1:2] * _shift_down(u, 1) + w[2:3] * u


def _gelu_and_grad(x):
    inner = GELU_C * (x + 0.044715 * (x * x * x))
    t = jnp.tanh(inner)
    gelu = 0.5 * x * (1.0 + t)
    dgelu = 0.5 * (1.0 + t) + 0.5 * x * (1.0 - t * t) * (GELU_C * (1.0 + 3 * 0.044715 * (x * x)))
    return gelu, dgelu


def _ffn_specs():
    col = lambda off: pl.BlockSpec((SEQ, TC), lambda *g: (0, g[-1] + off))
    w = lambda off: pl.BlockSpec((3, TC), lambda *g: (0, g[-1] + off))
    b = lambda off: pl.BlockSpec((1, TC), lambda *g: (0, g[-1] + off))
    return col, w, b


def _ffn_up_act(h3, wup_st, conv_w, conv_b):
    col, w, b = _ffn_specs()
    per_shard = wup_st.shape[2] // TC

    def body(h_ref, upg_ref, upv_ref, wg_ref, wv_ref, bg_ref, bv_ref, ug_ref, uv_ref, dgate_ref, dval_ref, act_ref):
        h = h_ref[...]
        ug = _dot(h, upg_ref[...])
        uv = _dot(h, upv_ref[...])
        ug_ref[...] = ug
        uv_ref[...] = uv
        gate = _conv3(ug, wg_ref[...], bg_ref[...])
        val = _conv3(uv, wv_ref[...], bv_ref[...])
        gelu, dgelu = _gelu_and_grad(gate)
        dgate_ref[...] = val * dgelu
        dval_ref[...] = gelu
        act_ref[...] = (gelu * val).astype(BF16)

    return pl.pallas_call(
        body, name="ffn_up_act", grid=(N_CB,),
        in_specs=[_const((SEQ, D_MODEL)),
                  pl.BlockSpec((None, D_MODEL, TC), lambda j: (j // per_shard, 0, j % per_shard)),
                  pl.BlockSpec((None, D_MODEL, TC), lambda j: (2 + j // per_shard, 0, j % per_shard)),
                  w(0), w(N_CB), b(0), b(N_CB)],
        out_specs=[col(0)] * 5,
        out_shape=[jax.ShapeDtypeStruct((SEQ, D_FF), F32)] * 4 + [jax.ShapeDtypeStruct((SEQ, D_FF), BF16)],
        compiler_params=_cp(("parallel",)),
    )(h3, wup_st, wup_st, conv_w, conv_w, conv_b, conv_b)


def _ffn_act_bwd(u_gate, u_val, dact_dgate, dact_dval, df, wdown, conv_w, h3):
    col, w, _ = _ffn_specs()
    both = lambda rows: pl.BlockSpec((2, rows, TC), lambda j: (0, 0, j))
    shard_cols = 2 * D_FF // N_CHIPS
    per_shard = shard_cols // TC

    def body(ug_ref, uv_ref, dgate_ref, dval_ref, df_ref, wd_ref, wg_ref, wv_ref, h_ref,
             du_ref, dw_ref, db_ref, dwup_ref, ht_ref):
        @pl.when(pl.program_id(0) == 0)
        def _():
            ht_ref[...] = h_ref[...].T

        da = _dot(df_ref[...], wd_ref[...], NT)
        halves = ((da * dgate_ref[...], ug_ref, wg_ref[...]), (da * dval_ref[...], uv_ref, wv_ref[...]))
        for h, (duc, u_ref, wh) in enumerate(halves):
            uh = u_ref[...]
            up1, up2 = _shift_up(duc, 1), _shift_up(duc, 2)
            du = (wh[2:3] * duc + wh[1:2] * up1 + wh[0:1] * up2).astype(BF16)
            du_ref[h] = du
            dwup_ref[h] = _dot(ht_ref[...], du)
            db_ref[h] = jnp.sum(duc, axis=0, keepdims=True)
            dw_ref[h] = jnp.concatenate(
                [jnp.sum(up2 * uh, axis=0, keepdims=True), jnp.sum(up1 * uh, axis=0, keepdims=True),
                 jnp.sum(duc * uh, axis=0, keepdims=True)], axis=0)

    du, d_convw, d_convb, d_wup = pl.pallas_call(
        body, name="ffn_act_bwd", grid=(N_CB,),
        in_specs=[col(0)] * 4 + [_const((SEQ, D_MODEL)), pl.BlockSpec((TC, D_MODEL), lambda j: (j, 0)), w(0), w(N_CB),
                                 _const((SEQ, D_MODEL))],
        out_specs=[both(SEQ), both(3), both(1),
                   pl.BlockSpec((2, None, D_MODEL, TC), lambda j: (0, j // per_shard, 0, j % per_shard))],
        out_shape=[jax.ShapeDtypeStruct((2, SEQ, D_FF), BF16), jax.ShapeDtypeStruct((2, 3, D_FF), F32),
                   jax.ShapeDtypeStruct((2, 1, D_FF), F32),
                   jax.ShapeDtypeStruct((2, N_CHIPS // 2, D_MODEL, shard_cols), F32)],
        scratch_shapes=[pltpu.VMEM((D_MODEL, SEQ), BF16)],
        compiler_params=_cp(("arbitrary",)),
    )(u_gate, u_val, dact_dgate, dact_dval, df, wdown, conv_w, conv_w, h3)
    return du, d_convw, d_convb, d_wup.reshape(N_CHIPS, D_MODEL, shard_cols)


def _t5_onehot():
    rel = (np.arange(BLOCK)[:, None] + BLOCK) - np.arange(2 * BLOCK)[None, :]
    n = np.maximum(rel, 0)
    max_exact = N_BUCKETS // 2
    large = max_exact + (np.log(np.maximum(n, 1).astype(np.float32) / np.float32(max_exact))
                         / np.float32(math.log(MAX_DISTANCE / max_exact))
                         * np.float32(N_BUCKETS - max_exact)).astype(np.int32)
    large = np.minimum(large, N_BUCKETS - 1)
    bucket = np.where(n < max_exact, n, large).reshape(-1)
    return (bucket[None, :] == np.arange(N_BUCKETS)[:, None]).astype(np.float32)


N_REL = BLOCK * 2 * BLOCK


def _bias_table(rel_bias_t, onehot):
    def body(rb_ref, oh_ref, o_ref):
        o_ref[...] = _dot_ind(rb_ref[...], oh_ref[...])

    return pl.pallas_call(
        body, name="bias_table", grid=(1,),
        in_specs=[_const((N_Q_HEADS, N_BUCKETS)), _const((N_BUCKETS, N_REL))],
        out_specs=_const((N_Q_HEADS, N_REL)),
        out_shape=jax.ShapeDtypeStruct((N_Q_HEADS, N_REL), F32),
        compiler_params=_cp(("arbitrary",)),
    )(rel_bias_t, onehot)


def _bias_table_bwd(dbias, onehot):
    def body(db_ref, oh_ref, o_ref):
        acc = None
        for part in _split(db_ref[...], 3):
            t = _dot(part, oh_ref[...], NT)
            acc = t if acc is None else acc + t
        o_ref[...] = acc

    return pl.pallas_call(
        body, name="bias_table_bwd", grid=(1,),
        in_specs=[_const((N_Q_HEADS, N_REL)), _const((N_BUCKETS, N_REL))],
        out_specs=_const((N_Q_HEADS, N_BUCKETS)),
        out_shape=jax.ShapeDtypeStruct((N_Q_HEADS, N_BUCKETS), F32),
        compiler_params=_cp(("arbitrary",)),
    )(dbias, onehot)


def _attn_pieces(n, q, kvp, kvc, bias_ref, sinks_ref, hk):
    qi = lax.broadcasted_iota(jnp.int32, (BLOCK, 2 * BLOCK), 0)
    kj = lax.broadcasted_iota(jnp.int32, (BLOCK, 2 * BLOCK), 1)
    rel = qi + BLOCK - kj
    first_key = jnp.where(n > 0, 0, BLOCK)
    ok = jnp.where(rel >= 0, jnp.where(rel < BLOCK, jnp.where(kj >= first_key, 1.0, 0.0), 0.0), 0.0)
    ok4 = jnp.concatenate([ok] * Q_PER_KV, axis=0) > 0.5
    c0 = hk * HEAD_DIM
    kcat = jnp.concatenate([kvp[:, c0:c0 + HEAD_DIM], kvc[:, c0:c0 + HEAD_DIM]], axis=0).astype(BF16)
    vcat = jnp.concatenate([kvp[:, D_KV + c0:D_KV + c0 + HEAD_DIM], kvc[:, D_KV + c0:D_KV + c0 + HEAD_DIM]],
                           axis=0).astype(BF16)
    q0 = hk * Q_PER_KV * HEAD_DIM
    qs = jnp.concatenate([q[:, q0 + g * HEAD_DIM:q0 + (g + 1) * HEAD_DIM] for g in range(Q_PER_KV)],
                         axis=0).astype(BF16)
    s = _dot(qs, kcat, NT) * (HEAD_DIM ** -0.5) + bias_ref[hk]
    s = jnp.where(ok4, s, NEG_INF)
    row = lax.broadcasted_iota(jnp.int32, (Q_PER_KV * BLOCK, 1), 0)
    sink = jnp.zeros((Q_PER_KV * BLOCK, 1), F32)
    for g in range(Q_PER_KV):
        sink = jnp.where((row >> BLOCK_SHIFT) == g, sinks_ref[hk * Q_PER_KV + g], sink)
    m = jnp.maximum(jnp.max(s, axis=-1, keepdims=True), sink)
    p = jnp.exp(s - m)
    es = jnp.exp(sink - m)
    inv = 1.0 / (jnp.sum(p, axis=-1, keepdims=True) + es)
    return qs, kcat, vcat, p * inv, es * inv


def _attn_in_specs():
    return [pl.BlockSpec((BLOCK, D_ATTN), lambda n: (n, 0)),
            pl.BlockSpec((BLOCK, 2 * D_KV), lambda n: (jnp.maximum(n - 1, 0), D_ATTN // (2 * D_KV))),
            pl.BlockSpec((BLOCK, 2 * D_KV), lambda n: (n, D_ATTN // (2 * D_KV))),
            _const((N_KV_HEADS, Q_PER_KV * BLOCK, 2 * BLOCK)),
            pl.BlockSpec(memory_space=pltpu.SMEM)]


def _unstack_heads(t):
    return jnp.concatenate([t[g * BLOCK:(g + 1) * BLOCK] for g in range(Q_PER_KV)], axis=1)


def _attn_fwd(proj, bias, sinks):
    def body(q_ref, kvp_ref, kvc_ref, bias_ref, sinks_ref, o_ref):
        n = pl.program_id(0)
        q, kvp, kvc = q_ref[...], kvp_ref[...], kvc_ref[...]
        outs = []
        for hk in range(N_KV_HEADS):
            _, _, vcat, probs, _ = _attn_pieces(n, q, kvp, kvc, bias_ref, sinks_ref, hk)
            outs.append(_unstack_heads(_dot(probs.astype(BF16), vcat)))
        o_ref[...] = jnp.concatenate(outs, axis=1)

    return pl.pallas_call(
        body, name="attn_fwd", grid=(SEQ // BLOCK,),
        in_specs=_attn_in_specs(),
        out_specs=pl.BlockSpec((BLOCK, D_ATTN), lambda n: (n, 0)),
        out_shape=jax.ShapeDtypeStruct((SEQ, D_ATTN), F32),
        compiler_params=_cp(("parallel",)),
    )(proj, proj, proj, bias, sinks)


def _attn_bwd(proj, bias, sinks, dcat):
    nb = SEQ // BLOCK

    def body(q_ref, kvp_ref, kvc_ref, bias_ref, sinks_ref, do_ref, dq_ref, dkv_ref, dbias_ref, dsink_ref, dsacc):
        n = pl.program_id(0)

        @pl.when(n == 0)
        def _():
            dkv_ref[...] = jnp.zeros_like(dkv_ref)
            dbias_ref[...] = jnp.zeros_like(dbias_ref)
            dsacc[...] = jnp.zeros_like(dsacc)

        q, kvp, kvc = q_ref[...], kvp_ref[...], kvc_ref[...]
        do_all = do_ref[...]
        dqs, dks, dvs = [], [], []
        for hk in range(N_KV_HEADS):
            qs, kcat, vcat, probs, psink = _attn_pieces(n, q, kvp, kvc, bias_ref, sinks_ref, hk)
            q0 = hk * Q_PER_KV * HEAD_DIM
            do = jnp.concatenate([do_all[:, q0 + g * HEAD_DIM:q0 + (g + 1) * HEAD_DIM] for g in range(Q_PER_KV)],
                                 axis=0).astype(BF16)
            dprobs = _dot(do, vcat, NT)
            dvs.append(_dot(probs.astype(BF16), do, TN))
            rowdot = jnp.sum(probs * dprobs, axis=-1, keepdims=True)
            ds = probs * (dprobs - rowdot)
            dsacc[hk] += -psink * rowdot
            dbias_ref[hk] += ds
            dsb = (ds * (HEAD_DIM ** -0.5)).astype(BF16)
            dqs.append(_unstack_heads(_dot(dsb, kcat)))
            dks.append(_dot(dsb, qs, TN))
        dq_ref[...] = jnp.concatenate(dqs, axis=1)
        upd = jnp.concatenate(dks + dvs, axis=1)
        cur = pl.multiple_of(n * BLOCK, BLOCK)
        dkv_ref[pl.ds(cur, BLOCK), :] += upd[BLOCK:]

        @pl.when(n > 0)
        def _():
            prev = pl.multiple_of((n - 1) * BLOCK, BLOCK)
            dkv_ref[pl.ds(prev, BLOCK), :] += upd[:BLOCK]

        @pl.when(n == nb - 1)
        def _():
            for hk in range(N_KV_HEADS):
                for g in range(Q_PER_KV):
                    tot = jnp.sum(dsacc[hk, g * BLOCK:(g + 1) * BLOCK, :], axis=0, keepdims=True)
                    h = hk * Q_PER_KV + g
                    dsink_ref[h:h + 1, :] = jnp.broadcast_to(tot, (1, LANES))

    return pl.pallas_call(
        body, name="attn_bwd", grid=(nb,),
        in_specs=_attn_in_specs() + [pl.BlockSpec((BLOCK, D_ATTN), lambda n: (n, 0))],
        out_specs=[pl.BlockSpec((BLOCK, D_ATTN), lambda n: (n, 0)), _const((SEQ, 2 * D_KV)),
                   _const((N_KV_HEADS, Q_PER_KV * BLOCK, 2 * BLOCK)), _const((N_Q_HEADS, LANES))],
        out_shape=[jax.ShapeDtypeStruct((SEQ, D_ATTN), F32), jax.ShapeDtypeStruct((SEQ, 2 * D_KV), F32),
                   jax.ShapeDtypeStruct((N_KV_HEADS, Q_PER_KV * BLOCK, 2 * BLOCK), F32),
                   jax.ShapeDtypeStruct((N_Q_HEADS, LANES), F32)],
        scratch_shapes=[pltpu.VMEM((N_KV_HEADS, Q_PER_KV * BLOCK, 1), F32)],
        compiler_params=_cp(("arbitrary",)),
    )(proj, proj, proj, bias, sinks, dcat)


@jax.custom_vjp
def _head_sum(x):
    ones = _head_ones(LANES)
    return jnp.concatenate([_dot_ind(x[:, c:c + LANES], ones, 2) for c in range(0, x.shape[-1], LANES)], axis=1)


_head_sum.defvjp(lambda x: (_head_sum(x), None), lambda _, ct: (_head_sum(ct),))


@jax.custom_vjp
def _bdot(a, w):
    return _dot(a.astype(BF16), w.astype(BF16))


def _bdot_bwd(res, ct):
    a, w = res
    ctb = ct.astype(BF16)
    return _dot(ctb, w.astype(BF16), NT), _dot(a.astype(BF16), ctb, TN)


_bdot.defvjp(lambda a, w: (_bdot(a, w), (a, w)), _bdot_bwd)


def _sigmoid(x):
    return 0.5 * (jnp.tanh(0.5 * x) + 1.0)


def _softplus(x):
    return jnp.maximum(x, 0.0) + jnp.log(1.0 + jnp.exp(-jnp.abs(x)))


def _rwkv_core(r, k, v, zwa, zg, w0, wdu, a0, wiu, wgu, k_k, k_a):
    w_log = -_softplus(-(w0 + _bdot(jnp.tanh(zwa), wdu))) - 0.5
    decay = jnp.exp(-jnp.exp(w_log))
    a = _sigmoid(a0 + _bdot(zwa, wiu))
    g = _bdot(_sigmoid(zg), wgu)
    kk = k * k_k
    kk = kk / jnp.maximum(jnp.sqrt(_head_sum(kk * kk)), 1e-12)
    k2 = k * (1.0 + (a - 1.0) * k_a)
    return r, decay, k2, v, -kk, kk * a, g


def _rwkv_out(o, r, k2, v, g, lng, lnb, rk):
    mu = _head_sum(o) * (1.0 / HEAD_DIM)
    d = o - mu
    var = _head_sum(d * d) * (1.0 / HEAD_DIM)
    on = d * lax.rsqrt(var + GN_EPS) * lng + lnb
    bonus = _head_sum(r * k2 * rk) * v
    return (on + bonus) * g


P_SPLITS = (0, 512, 1024, 1536, 1664, 1792)
N_PREP_PARAMS = 7
HALO = 8


def _shifted_pieces(i, p_ref, halo_ref, mix_ref):
    p = p_ref[:, P_OFF:]
    prev_row = halo_ref[HALO - 1:HALO, P_OFF:] * jnp.where(i > 0, 1.0, 0.0)
    row = lax.broadcasted_iota(jnp.int32, p.shape, 0)
    pprev = jnp.where(row == 0, prev_row, pltpu.roll(p, 1, 0))
    delta = pprev - p
    ps = p + delta * mix_ref[...]
    return [ps[:, a:b] for a, b in zip(P_SPLITS[:-1], P_SPLITS[1:])], delta


def _prep_in_specs():
    return [_rows(TR, D_IN),
            pl.BlockSpec((HALO, D_IN), lambda i: (jnp.maximum(i * (TR // HALO) - 1, 0), 0)),
            _const((1, RWKV_COLS)), _const((1, D_RWKV)), _const((LANES, D_RWKV)), _const((1, D_RWKV)),
            _const((LANES, D_RWKV)), _const((LANES, D_RWKV)), _const((1, D_RWKV)), _const((1, D_RWKV))]


def _rwkv_prep(proj, mix, prm):
    def body(p_ref, halo_ref, mix_ref, *refs):
        prm_refs, outs = refs[:N_PREP_PARAMS], refs[N_PREP_PARAMS:]
        pieces, _ = _shifted_pieces(pl.program_id(0), p_ref, halo_ref, mix_ref)
        vals = _rwkv_core(*pieces, *[t[...] for t in prm_refs])
        for ref, val in zip(outs, vals):
            ref[...] = val

    return pl.pallas_call(
        body, name="rwkv_prep", grid=(SEQ // TR,),
        in_specs=_prep_in_specs(),
        out_specs=[_rows(TR, D_RWKV)] * 7,
        out_shape=[jax.ShapeDtypeStruct((SEQ, D_RWKV), F32)] * 7,
        compiler_params=_cp(("parallel",)),
    )(proj, proj, mix, *prm)


def _rwkv_prep_bwd(proj, mix, prm, cts):
    def body(p_ref, halo_ref, mix_ref, *refs):
        i = pl.program_id(0)
        prm_refs = refs[:N_PREP_PARAMS]
        ct_refs = refs[N_PREP_PARAMS:N_PREP_PARAMS + 10]
        dps_ref, dmix_ref = refs[N_PREP_PARAMS + 10:N_PREP_PARAMS + 12]
        dprm_refs = refs[N_PREP_PARAMS + 12:]
        pieces, delta = _shifted_pieces(i, p_ref, halo_ref, mix_ref)
        _, vjp = jax.vjp(_rwkv_core, *pieces, *[t[...] for t in prm_refs])
        dr1, dr2, dw, dk1, dk2, dv1, dv2, dkkn, db, dg = [t[...] for t in ct_refs]
        grads = vjp((dr1 + dr2, dw, dk1 + dk2, dv1 + dv2, dkkn, db, dg))
        dps = jnp.concatenate(grads[:5], axis=1)
        dps_ref[...] = dps

        @pl.when(i == 0)
        def _():
            dmix_ref[...] = jnp.zeros_like(dmix_ref)
            for ref in dprm_refs:
                ref[...] = jnp.zeros_like(ref)

        dmix_ref[...] += jnp.sum(dps * delta, axis=0, keepdims=True)
        for ref, gval in zip(dprm_refs, grads[5:]):
            ref[...] += gval

    prm_shapes = [(1, D_RWKV), (LANES, D_RWKV), (1, D_RWKV), (LANES, D_RWKV), (LANES, D_RWKV), (1, D_RWKV), (1, D_RWKV)]
    return pl.pallas_call(
        body, name="rwkv_prep_bwd", grid=(SEQ // TR,),
        in_specs=_prep_in_specs() + [_rows(TR, D_RWKV)] * 10,
        out_specs=[_rows(TR, RWKV_COLS), _const((1, RWKV_COLS))] + [_const(s) for s in prm_shapes],
        out_shape=[jax.ShapeDtypeStruct((SEQ, RWKV_COLS), F32), jax.ShapeDtypeStruct((1, RWKV_COLS), F32)]
        + [jax.ShapeDtypeStruct(s, F32) for s in prm_shapes],
        compiler_params=_cp(("arbitrary",)),
    )(proj, proj, mix, *prm, *cts)


def _rwkv_post(o, r, k2, v, g, lng, lnb, rk, attn):
    def body(o_ref, r_ref, k_ref, v_ref, g_ref, lng_ref, lnb_ref, rk_ref, attn_ref, cat_ref):
        rw = _rwkv_out(*[t[...] for t in (o_ref, r_ref, k_ref, v_ref, g_ref, lng_ref, lnb_ref, rk_ref)])
        cat_ref[...] = jnp.concatenate([attn_ref[...], rw], axis=1).astype(BF16)

    return pl.pallas_call(
        body, name="rwkv_post", grid=(SEQ // TR,),
        in_specs=[_rows(TR, D_RWKV)] * 5 + [_const((1, D_RWKV))] * 3 + [_rows(TR, D_ATTN)],
        out_specs=_rows(TR, D_MODEL),
        out_shape=jax.ShapeDtypeStruct((SEQ, D_MODEL), BF16),
        compiler_params=_cp(("parallel",)),
    )(o, r, k2, v, g, lng, lnb, rk, attn)


def _rwkv_post_bwd(o, r, k2, v, g, lng, lnb, rk, dcat):
    def body(o_ref, r_ref, k_ref, v_ref, g_ref, lng_ref, lnb_ref, rk_ref, dcat_ref,
             do_ref, dr_ref, dk_ref, dv_ref, dg_ref, dlng_ref, dlnb_ref, drk_ref):
        i = pl.program_id(0)
        args = [t[...] for t in (o_ref, r_ref, k_ref, v_ref, g_ref, lng_ref, lnb_ref, rk_ref)]
        _, vjp = jax.vjp(_rwkv_out, *args)
        grads = vjp(dcat_ref[:, D_ATTN:])
        for ref, gval in zip((do_ref, dr_ref, dk_ref, dv_ref, dg_ref), grads[:5]):
            ref[...] = gval

        @pl.when(i == 0)
        def _():
            for ref in (dlng_ref, dlnb_ref, drk_ref):
                ref[...] = jnp.zeros_like(ref)

        for ref, gval in zip((dlng_ref, dlnb_ref, drk_ref), grads[5:]):
            ref[...] += gval

    return pl.pallas_call(
        body, name="rwkv_post_bwd", grid=(SEQ // TR,),
        in_specs=[_rows(TR, D_RWKV)] * 5 + [_const((1, D_RWKV))] * 3 + [_rows(TR, D_MODEL)],
        out_specs=[_rows(TR, D_RWKV)] * 5 + [_const((1, D_RWKV))] * 3,
        out_shape=[jax.ShapeDtypeStruct((SEQ, D_RWKV), F32)] * 5 + [jax.ShapeDtypeStruct((1, D_RWKV), F32)] * 3,
        compiler_params=_cp(("arbitrary",)),
    )(o, r, k2, v, g, lng, lnb, rk, dcat)


def _assemble_dproj(dq, dkv, dps, mix):
    last = SEQ // HALO - 1

    def body(dq_ref, dkv_ref, dps_ref, nxt_ref, mix_ref, o_ref):
        i = pl.program_id(0)
        dps = dps_ref[...]
        mixv = mix_ref[...]
        nxt_row = nxt_ref[0:1, :] * jnp.where(i < SEQ // TR - 1, 1.0, 0.0)
        row = lax.broadcasted_iota(jnp.int32, dps.shape, 0)
        up = jnp.where(row == TR - 1, nxt_row, pltpu.roll(dps, TR - 1, 0))
        dp = dps * (1.0 - mixv) + up * mixv
        o_ref[...] = jnp.concatenate([dq_ref[...], dkv_ref[...], dp], axis=1).astype(BF16)

    return pl.pallas_call(
        body, name="assemble_dproj", grid=(SEQ // TR,),
        in_specs=[_rows(TR, D_ATTN), _rows(TR, 2 * D_KV), _rows(TR, RWKV_COLS),
                  pl.BlockSpec((HALO, RWKV_COLS), lambda i: (jnp.minimum((i + 1) * (TR // HALO), last), 0)),
                  _const((1, RWKV_COLS))],
        out_specs=_rows(TR, D_IN),
        out_shape=jax.ShapeDtypeStruct((SEQ, D_IN), BF16),
        compiler_params=_cp(("parallel",)),
    )(dq, dkv, dps, dps, mix)


N_PAIR = D_RWKV // LANES
CHUNK = 64
N_CHUNK = SEQ // CHUNK
GROUP = 64
STATE = (N_PAIR, HEAD_DIM, LANES)


def _lane_sums(lhs_tiles, ones2):
    out = _dot(jnp.concatenate(lhs_tiles, axis=0), ones2)
    return [out[i * HEAD_DIM:(i + 1) * HEAD_DIM] for i in range(len(lhs_tiles))]


def _seg_sum(xs, ones2):
    return _lane_sums([jnp.concatenate(_split(x, 2), axis=1) for x in xs], ones2)


def _seg_sum_rows(xs, ones2):
    out = _dot(jnp.concatenate(_split(jnp.concatenate(xs, axis=0), 2), axis=1), ones2)
    return [out[i * GROUP:(i + 1) * GROUP] for i in range(len(xs))]


def _col_form(rows, diag, ones2):
    zero = jnp.zeros((HEAD_DIM, LANES), BF16)
    tiles = []
    for row in rows:
        hi = row.astype(BF16)
        lo = (row - hi.astype(F32)).astype(BF16)
        tiles.append(jnp.concatenate(
            [jnp.where(diag, jnp.broadcast_to(part, (HEAD_DIM, LANES)), zero) for part in (hi, lo)], axis=1))
    return _lane_sums(tiles, ones2)


def _scan_consts():
    ones2 = jnp.concatenate([_head_ones(LANES)] * 2, axis=0)
    sub = lax.broadcasted_iota(jnp.int32, (HEAD_DIM, LANES), 0)
    lane_in_head = lax.broadcasted_iota(jnp.int32, (HEAD_DIM, LANES), 1) & (HEAD_DIM - 1)
    return ones2, lane_in_head == sub, lane_in_head


def _rows_of_columns(tile):
    t = tile.T
    return jnp.concatenate([t[:CHUNK], t[HEAD_DIM:HEAD_DIM + CHUNK]], axis=1)


def _pair(j):
    return slice(j * LANES, (j + 1) * LANES)


def _scan_fwd(r, w, k, v, kkn, b):
    def body(r_ref, w_ref, k_ref, v_ref, kkn_ref, b_ref, o_ref, st_ref, sa_ref, s_scr):
        c = pl.program_id(0)
        ones2, diag, lane_in_head = _scan_consts()

        @pl.when(c == 0)
        def _():
            s_scr[...] = jnp.zeros_like(s_scr)

        def group(gi, carry):
            row0 = pl.multiple_of(gi * GROUP, GROUP)
            states, ocols = list(carry[:N_PAIR]), list(carry[N_PAIR:])
            tiles = [[t[pl.ds(row0, GROUP), _pair(j)] for t in (r_ref, w_ref, k_ref, v_ref, kkn_ref, b_ref)]
                     for j in range(N_PAIR)]
            def row(j, name, u):
                return tiles[j]["rwkvnb".index(name)][u:u + 1]

            def emit_out(u, after):
                outs = _seg_sum([s[j] * row(j, "r", u + d) for d, s in enumerate(after) for j in range(N_PAIR)], ones2)
                for d in range(2):
                    here = lane_in_head == gi * GROUP + u + d
                    for j in range(N_PAIR):
                        ocols[j] = jnp.where(here, outs[d * N_PAIR + j], ocols[j])

            def vcols_of(u):
                cols = _col_form([row(j, "v", u + d) for d in range(2) for j in range(N_PAIR)], diag, ones2)
                return cols[:N_PAIR], cols[N_PAIR:]

            n_next = [pltpu.roll(tiles[j][4], GROUP - 1, 0) for j in range(N_PAIR)]
            dots = _seg_sum_rows([tiles[j][5] * n_next[j] for j in range(N_PAIR)]
                                 + [tiles[j][2] * n_next[j] for j in range(N_PAIR)], ones2)
            b_n, k_n = dots[:N_PAIR], dots[N_PAIR:]
            w_n = [tiles[j][1] * n_next[j] for j in range(N_PAIR)]

            vcols = vcols_of(0)
            after = None
            for u in range(0, GROUP, 2):
                prods = _seg_sum([states[j] * row(j, "n", u) for j in range(N_PAIR)]
                                 + [states[j] * w_n[j][u:u + 1] for j in range(N_PAIR)], ones2)
                if after is not None:
                    emit_out(u - 2, after)
                nxt = vcols_of(u + 2) if u + 2 < GROUP else None
                first, second = [], []
                for j in range(N_PAIR):
                    sa1 = prods[j]
                    sa2 = prods[N_PAIR + j] + sa1 * b_n[j][u:u + 1] + vcols[0][j] * k_n[j][u:u + 1]
                    s1 = states[j] * row(j, "w", u) + sa1 * row(j, "b", u) + vcols[0][j] * row(j, "k", u)
                    s2 = s1 * row(j, "w", u + 1) + sa2 * row(j, "b", u + 1) + vcols[1][j] * row(j, "k", u + 1)
                    st_ref[row0 + u, j] = s1
                    sa_ref[row0 + u, j] = sa1
                    st_ref[row0 + u + 1, j] = s2
                    sa_ref[row0 + u + 1, j] = sa2
                    first.append(s1)
                    second.append(s2)
                    states[j] = s2
                after, vcols = (first, second), nxt
            emit_out(GROUP - 2, after)
            return tuple(states + ocols)

        zero = jnp.zeros((HEAD_DIM, LANES), F32)
        fin = lax.fori_loop(0, CHUNK // GROUP, group, tuple(s_scr[j] for j in range(N_PAIR)) + (zero,) * N_PAIR)
        for j in range(N_PAIR):
            s_scr[j] = fin[j]
            o_ref[:, _pair(j)] = _rows_of_columns(fin[N_PAIR + j])

    blk = pl.BlockSpec((CHUNK, D_RWKV), lambda c: (c, 0))
    per_step = pl.BlockSpec((CHUNK,) + STATE, lambda c: (c, 0, 0, 0))
    return pl.pallas_call(
        body, name="rwkv_scan_fwd", grid=(N_CHUNK,),
        in_specs=[blk] * 6,
        out_specs=[blk, per_step, per_step],
        out_shape=[jax.ShapeDtypeStruct((SEQ, D_RWKV), F32)] + [jax.ShapeDtypeStruct((SEQ,) + STATE, F32)] * 2,
        scratch_shapes=[pltpu.VMEM(STATE, F32)],
        compiler_params=_cp(("arbitrary",)),
    )(r, w, k, v, kkn, b)


def _scan_bwd(r, w, k, v, kkn, b, do, states, sas, ds_in, prev, name, first_chunk, n_chunks):
    top = first_chunk + n_chunks - 1

    def body(r_ref, w_ref, k_ref, v_ref, kkn_ref, b_ref, do_ref, st_ref, before_ref, sa_ref, ds_in_ref, *rest):
        dr_ref, dw_ref, dk_ref, dv_ref, dkkn_ref, db_ref, ds_out_ref, ds_scr = rest[-8:]
        i = pl.program_id(0)
        ones2, diag, lane_in_head = _scan_consts()

        @pl.when(i == 0)
        def _():
            ds_scr[...] = ds_in_ref[...]

        entry = [before_ref[0, j] * jnp.where(i < top, 1.0, 0.0) for j in range(N_PAIR)]

        def reverse(gr, carry):
            gi = CHUNK // GROUP - 1 - gr
            row0 = pl.multiple_of(gi * GROUP, GROUP)
            dstates, dvcols = list(carry[:N_PAIR]), list(carry[N_PAIR:])
            tiles = [[t[pl.ds(row0, GROUP), _pair(j)]
                      for t in (r_ref, w_ref, k_ref, v_ref, kkn_ref, b_ref, do_ref)] for j in range(N_PAIR)]
            rows = [[[None] * GROUP for _ in range(5)] for _ in range(N_PAIR)]

            def row(j, name, u):
                return tiles[j]["rwkvnbd".index(name)][u:u + 1]

            def cols_of(u):
                cols = _col_form([row(j, name, u - d) for d in range(2) for name in "dv" for j in range(N_PAIR)],
                                 diag, ones2)
                return [[(cols[(2 * d) * N_PAIR + j], cols[(2 * d + 1) * N_PAIR + j]) for j in range(N_PAIR)]
                        for d in range(2)]

            def emit_dv(u, dsps):
                outs = _seg_sum([dsp[j] * row(j, "k", u - d) for d, dsp in enumerate(dsps) for j in range(N_PAIR)], ones2)
                for d in range(2):
                    here = lane_in_head == gi * GROUP + u - d
                    for j in range(N_PAIR):
                        dvcols[j] = jnp.where(here, outs[d * N_PAIR + j], dvcols[j])

            b_prev = [pltpu.roll(tiles[j][5], 1, 0) for j in range(N_PAIR)]
            dots = _seg_sum_rows([tiles[j][4] * b_prev[j] for j in range(N_PAIR)]
                                 + [tiles[j][0] * tiles[j][5] for j in range(N_PAIR)], ones2)
            n_b, r_b = dots[:N_PAIR], dots[N_PAIR:]
            w_b = [tiles[j][1] * b_prev[j] for j in range(N_PAIR)]

            def outputs(u, j, dsp, dsa, docol, vcol):
                tl = gi * GROUP + u
                if u > 0:
                    s_prev = st_ref[tl - 1, j]
                else:
                    s_prev = jnp.where(gi == 0, entry[j], st_ref[jnp.maximum(tl - 1, 0), j])
                rows[j][0][u] = jnp.sum(st_ref[tl, j] * docol, axis=0, keepdims=True)
                rows[j][1][u] = jnp.sum(dsp * s_prev, axis=0, keepdims=True)
                rows[j][2][u] = jnp.sum(dsp * vcol, axis=0, keepdims=True)
                rows[j][3][u] = jnp.sum(s_prev * dsa, axis=0, keepdims=True)
                rows[j][4][u] = jnp.sum(dsp * sa_ref[tl, j], axis=0, keepdims=True)

            cols = cols_of(GROUP - 1)
            before = None
            for u in range(GROUP - 1, 0, -2):
                dsp1 = [dstates[j] + cols[0][j][0] * row(j, "r", u) for j in range(N_PAIR)]
                prods = _seg_sum([dsp1[j] * row(j, "b", u) for j in range(N_PAIR)]
                                 + [dsp1[j] * w_b[j][u:u + 1] for j in range(N_PAIR)], ones2)
                if before is not None:
                    emit_dv(u + 2, before)
                nxt = cols_of(u - 2) if u >= 2 else None
                dsp2 = []
                for j in range(N_PAIR):
                    dsa1 = prods[j]
                    dsa2 = prods[N_PAIR + j] + dsa1 * n_b[j][u:u + 1] + cols[1][j][0] * r_b[j][u - 1:u]
                    mid = dsp1[j] * row(j, "w", u) + dsa1 * row(j, "n", u) + cols[1][j][0] * row(j, "r", u - 1)
                    outputs(u, j, dsp1[j], dsa1, *cols[0][j])
                    outputs(u - 1, j, mid, dsa2, *cols[1][j])
                    dstates[j] = mid * row(j, "w", u - 1) + dsa2 * row(j, "n", u - 1)
                    dsp2.append(mid)
                before, cols = (dsp1, dsp2), nxt
            emit_dv(1, before)
            for j in range(N_PAIR):
                for ref, rr in zip((dr_ref, dw_ref, dk_ref, dkkn_ref, db_ref), rows[j]):
                    ref[pl.ds(row0, GROUP), _pair(j)] = jnp.concatenate(rr, axis=0)
            return tuple(dstates + dvcols)

        zero = jnp.zeros((HEAD_DIM, LANES), F32)
        dfin = lax.fori_loop(0, CHUNK // GROUP, reverse, tuple(ds_scr[j] for j in range(N_PAIR)) + (zero,) * N_PAIR)
        for j in range(N_PAIR):
            ds_scr[j] = dfin[j]
            dv_ref[:, _pair(j)] = _rows_of_columns(dfin[N_PAIR + j])

        @pl.when(i == n_chunks - 1)
        def _():
            ds_out_ref[...] = ds_scr[...]

    blk = pl.BlockSpec((CHUNK, D_RWKV), lambda i: (top - i, 0))
    per_step = pl.BlockSpec((CHUNK,) + STATE, lambda i: (top - i, 0, 0, 0))
    step_before = pl.BlockSpec((1,) + STATE, lambda i: (jnp.maximum((top - i) * CHUNK - 1, 0), 0, 0, 0))
    prev = [] if prev is None else list(prev)
    outs = pl.pallas_call(
        body, name=name, grid=(n_chunks,),
        in_specs=[blk] * 7 + [per_step, step_before, per_step, _const(STATE)] + [ANY] * len(prev),
        out_specs=[blk] * 6 + [_const(STATE)],
        out_shape=[jax.ShapeDtypeStruct((SEQ, D_RWKV), F32)] * 6 + [jax.ShapeDtypeStruct(STATE, F32)],
        scratch_shapes=[pltpu.VMEM(STATE, F32)],
        input_output_aliases={11 + t: t for t in range(len(prev))},
        compiler_params=_cp(("arbitrary",)),
    )(r, w, k, v, kkn, b, do, states, states, sas, ds_in, *prev)
    return outs[:6], outs[6]


def _stacked(rows, cols, pick):
    return pl.BlockSpec((None, rows, cols), pick)


def _local_step(x, target, sm, win_st):
    def tied(t, token):
        return t if token is None else t + token[0:1, 0:1].reshape((1,) * t.ndim)

    zpad = jnp.zeros((LORA_DECAY, D_RWKV), F32)
    prm = [sm["w0"], jnp.concatenate([sm["w_decay_up"], zpad], axis=0), sm["a0"],
           jnp.concatenate([zpad, sm["w_iclr_up"]], axis=0), sm["w_gate_up"], sm["k_k"], sm["k_a"]]
    mix = sm["rwkv_shift_mix"]
    onehot = jnp.asarray(_t5_onehot(), BF16)
    sinks = sm["sinks"].reshape(N_Q_HEADS)
    lng, lnb, rk = sm["ln_x_g"], sm["ln_x_b"], sm["r_k"].reshape(1, D_RWKV)

    h1 = _norm_cast(x, sm["norm_mix_pre"], "norm_in")
    proj = _matmul(h1, win_st, "nn", "proj", m=SEQ, n=D_IN, k=D_MODEL, tm=SEQ, tn=640,
                   b_spec=_stacked(D_MODEL, 640, lambda i, j: (j, 0, 0)))
    bias = _bias_table(sm["rel_bias"].T, onehot).reshape(N_KV_HEADS, Q_PER_KV * BLOCK, 2 * BLOCK)
    attn = _attn_fwd(proj, bias, sinks)
    r, w, k2, v, kkn, b, g = _rwkv_prep(proj, mix, prm)
    o, states, sas = _scan_fwd(r, w, k2, v, kkn, b)
    wout, wup_st, wdown = yield ("rest_weights", o)
    cat = _rwkv_post(o, r, k2, v, g, lng, lnb, rk, attn)
    mixo = _matmul(cat, wout, "nn", "out_proj", m=SEQ, n=D_MODEL, k=D_MODEL, tm=SEQ, tn=512)
    x2, h3 = _mix_norm(x, mixo, sm["norm_mix_post"], sm["norm_ffn_pre"])
    u_gate, u_val, dact_dgate, dact_dval, act = _ffn_up_act(h3, wup_st, sm["conv_w"], sm["conv_b"])
    f = _matmul(act, wdown, "nn", "ffn_down", m=SEQ, n=D_MODEL, k=D_FF, tm=1024, tn=512)
    loss, dy, df, d_g4 = _loss_head(x2, f, sm["norm_ffn_post"], target)

    d_wdown = _matmul(act, df, "tn", "d_wdown", m=D_FF, n=D_MODEL, k=SEQ, tm=1024, tn=D_MODEL)
    du, d_convw, d_convb, d_wup = _ffn_act_bwd(u_gate, u_val, dact_dgate, dact_dval, df, wdown, sm["conv_w"], h3)
    d_convw = d_convw.transpose(1, 0, 2).reshape(3, 2 * D_FF)
    d_convb = d_convb.reshape(1, 2 * D_FF)
    dh3 = _matmul_nt_shards(du, wup_st, "d_h3", m=SEQ, n=D_MODEL, tm=512, tn=512,
                            a_spec=pl.BlockSpec((2, 512, D_FF), lambda i, j: (0, i, 0)),
                            a_piece=lambda ref, s: ref[s // 2, :, (s % 2) * 2048:(s % 2 + 1) * 2048])
    dx2, dmix, d_g2, d_g3 = _mid_bwd(x2, mixo, dy, dh3, sm["norm_mix_post"], sm["norm_ffn_pre"])
    dcat = _matmul(dmix, wout, "nt", "d_cat", m=SEQ, n=D_MODEL, k=D_MODEL, tm=SEQ, tn=512)
    d_wout = _matmul(cat, dmix, "tn", "d_wout", m=D_MODEL, n=D_MODEL, k=SEQ, tm=512, tn=D_MODEL)
    token = yield ("grads_a", (d_wdown, d_wup, d_wout))
    do, dr_p, dk_p, dv_p, dg, d_lng, d_lnb, d_rk = _rwkv_post_bwd(o, r, k2, v, g, lng, tied(lnb, token), rk, dcat)
    half = N_CHUNK // 2
    ds_end = jnp.zeros(STATE, F32)
    late, ds_mid = _scan_bwd(r, w, k2, v, kkn, b, do, states, sas, ds_end, None, "rwkv_scan_bwd_late", half, half)
    token = yield ("seam_1", ds_mid)
    scan_cts, ds_first = _scan_bwd(r, w, k2, v, kkn, b, do, states, sas, tied(ds_mid, token), late,
                                   "rwkv_scan_bwd_early", 0, half)
    dr_s, dw_s, dk_s, dv_s, dkkn_s, db_s = scan_cts
    token = yield ("seam_2", ds_first)
    prep_grads = _rwkv_prep_bwd(proj, tied(mix, token), prm,
                                (dr_s, dr_p, dw_s, dk_s, dk_p, dv_s, dv_p, dkkn_s, db_s, dg))
    dps, d_mix, d_w0, d_wdu, d_a0, d_wiu, d_wgu, d_kk, d_ka = prep_grads
    dq, dkv, dbias, dsink = _attn_bwd(proj, bias, sinks, dcat)
    d_relb = _bias_table_bwd(dbias.reshape(N_Q_HEADS, N_REL), onehot).T
    dproj = _assemble_dproj(dq, dkv, dps, mix)
    d_win = _matmul(h1, dproj, "tn", "d_win", m=D_MODEL, n=D_IN, k=SEQ, tm=D_MODEL, tn=640,
                    out=((N_CHIPS, D_MODEL, 640), _stacked(D_MODEL, 640, lambda i, j: (j, 0, 0))))
    token = yield ("grads_b", d_win)
    dh1 = _matmul_nt_shards(dproj, win_st, "d_h1", m=SEQ, n=D_MODEL, tm=1024, tn=D_MODEL,
                            a_spec=pl.BlockSpec((1024, D_IN), lambda i, j: (i, 0)),
                            a_piece=lambda ref, s: ref[:, s * 640:(s + 1) * 640])
    grad_x, d_g1 = _first_bwd(x, dx2, dh1, tied(sm["norm_mix_pre"], token))

    grads = {
        "norm_mix_pre": d_g1, "norm_mix_post": d_g2, "norm_ffn_pre": d_g3, "norm_ffn_post": d_g4,
        "w_in": d_win, "rel_bias": d_relb, "sinks": dsink[:, 0].reshape(1, N_Q_HEADS),
        "rwkv_shift_mix": d_mix, "w0": d_w0, "w_decay_up": d_wdu[:LORA_DECAY], "a0": d_a0,
        "w_iclr_up": d_wiu[LORA_DECAY:], "w_gate_up": d_wgu, "k_k": d_kk, "k_a": d_ka,
        "r_k": d_rk.reshape(1, N_Q_HEADS, HEAD_DIM), "ln_x_g": d_lng, "ln_x_b": d_lnb,
        "w_out": d_wout, "w_ffn_up": d_wup, "conv_w": d_convw, "conv_b": d_convb, "w_ffn_down": d_wdown,
    }
    return loss, grad_x, grads


def _place():
    x, y, c = lax.axis_index("x"), lax.axis_index("y"), lax.axis_index("c")
    chips = [(1 - x, y), (x, 1 - y), (1 - x, 1 - y)]
    return x, y, c, chips


def _remote(src, dst, sems, idx, to):
    return pltpu.make_async_remote_copy(src_ref=src, dst_ref=dst, send_sem=sems[0].at[idx], recv_sem=sems[1].at[idx],
                                        device_id=to, device_id_type=MESH)


ROW_ALIGN = 16


def _half(c, rows):
    return pl.ds(pl.multiple_of(c * (rows // 2), ROW_ALIGN), rows // 2)


def _gather_weights(big, small):
    nb, ns = len(big), len(small)

    def body(*refs):
        ins, outs = refs[:nb + ns], refs[nb + ns:2 * (nb + ns)]
        ici, d2d, sml, loc = refs[2 * (nb + ns):2 * (nb + ns) + 2], refs[-5:-3], refs[-3:-1], refs[-1]
        x, y, c, chips = _place()
        me = 2 * x + y
        sib = (x, y, 1 - c)
        local = [pltpu.make_async_copy(ins[a], outs[a].at[me], loc.at[a]) for a in range(nb + ns)]
        for cp in local:
            cp.start()
        sends = []
        for a in range(nb):
            rows = _half(c, big[a].shape[0])
            for kk, chip in enumerate(chips):
                sends.append(_remote(ins[a].at[rows], outs[a].at[me, rows], ici, a * 3 + kk, (*chip, c)))
        for a in range(ns):
            for kk, chip in enumerate(chips):
                sends.append(_remote(ins[nb + a], outs[nb + a].at[me], sml, a * 3 + kk, (*chip, c)))
        for cp in sends:
            cp.start()
        passed = []
        for a in range(nb):
            rows = _half(c, big[a].shape[0])
            for kk, (px, py) in enumerate(chips):
                got = outs[a].at[2 * px + py, rows]
                _remote(got, got, ici, a * 3 + kk, sib).wait_recv()
                fwd = _remote(got, got, d2d, a * 3 + kk, sib)
                fwd.start()
                passed.append(fwd)
        for a in range(nb):
            other = _half(1 - c, big[a].shape[0])
            for kk, (px, py) in enumerate(chips):
                land = outs[a].at[2 * px + py, other]
                _remote(land, land, d2d, a * 3 + kk, sib).wait_recv()
        for a in range(ns):
            for kk, (px, py) in enumerate(chips):
                land = outs[nb + a].at[2 * px + py]
                _remote(land, land, sml, a * 3 + kk, sib).wait_recv()
        for cp in sends + passed:
            cp.wait_send()
        for cp in local:
            cp.wait()

    arrs = list(big) + list(small)
    in_vmem = pl.BlockSpec(memory_space=pltpu.VMEM)
    return pl.pallas_call(
        body, name="gather_weights",
        in_specs=[in_vmem] * len(arrs), out_specs=[in_vmem] * len(arrs),
        out_shape=[jax.ShapeDtypeStruct((N_CHIPS,) + t.shape, t.dtype) for t in arrs],
        scratch_shapes=[pltpu.SemaphoreType.DMA((3 * nb,)), pltpu.SemaphoreType.DMA((3 * nb,)),
                        pltpu.SemaphoreType.DMA((3 * nb,)), pltpu.SemaphoreType.DMA((3 * nb,)),
                        pltpu.SemaphoreType.DMA((3 * ns,)), pltpu.SemaphoreType.DMA((3 * ns,)),
                        pltpu.SemaphoreType.DMA((nb + ns,))],
        compiler_params=pltpu.CompilerParams(has_side_effects=True, vmem_limit_bytes=VMEM_LIMIT),
    )(*arrs)


HBM = pl.BlockSpec(memory_space=pltpu.HBM)
SEM = pl.BlockSpec(memory_space=pltpu.SEMAPHORE)
EFFECT = pltpu.SideEffectType.DATAFLOW_SIDE_EFFECTING


def _copies_start(name, bufs, plan, n, partners=None):
    nb = len(bufs)

    def body(*refs):
        ins, sems, token = refs[:nb], refs[nb:nb + 2 * n], refs[-1]
        if partners is not None:
            barrier = pltpu.get_barrier_semaphore()
            peers = partners[1]()
            for peer in peers:
                pl.semaphore_signal(barrier, inc=1, device_id=peer, device_id_type=MESH)
            pl.semaphore_wait(barrier, len(peers))
        for kk, (src, dst, dev) in enumerate(plan(ins)):
            pltpu.make_async_remote_copy(src_ref=src, dst_ref=dst, send_sem=sems[2 * kk], recv_sem=sems[2 * kk + 1],
                                         device_id=dev, device_id_type=MESH).start()
        token[...] = jnp.zeros_like(token)

    outs = pl.pallas_call(
        body, name=name,
        out_shape=tuple([pltpu.SemaphoreType.DMA(())] * (2 * n) + [pltpu.HBM(t.shape, t.dtype) for t in bufs]
                        + [jax.ShapeDtypeStruct((8, LANES), F32)]),
        in_specs=[HBM] * nb,
        out_specs=tuple([SEM] * (2 * n) + [HBM] * nb + [pl.BlockSpec(memory_space=pltpu.VMEM)]),
        input_output_aliases={t: 2 * n + t for t in range(nb)},
        compiler_params=pltpu.CompilerParams(has_side_effects=EFFECT,
                                             collective_id=None if partners is None else partners[0]),
    )(*[pltpu.with_memory_space_constraint(t, pltpu.HBM) for t in bufs])
    return outs[:2 * n], outs[2 * n:2 * n + nb], outs[-1]


def _copies_wait(name, sems, bufs, plan, n, after):
    nb = len(bufs)
    after = list(after) if isinstance(after, (list, tuple)) else [after]

    def body(*refs):
        ins, sem_refs = refs[:nb], refs[nb:nb + 2 * n]
        for kk, (src, dst, dev) in enumerate(plan(ins)):
            cp = pltpu.make_async_remote_copy(src_ref=src, dst_ref=dst, send_sem=sem_refs[2 * kk],
                                              recv_sem=sem_refs[2 * kk + 1], device_id=dev, device_id_type=MESH)
            cp.wait_send()
            cp.wait_recv()

    return pl.pallas_call(
        body, name=name,
        out_shape=tuple(pltpu.HBM(t.shape, t.dtype) for t in bufs),
        in_specs=[HBM] * nb + [SEM] * (2 * n) + [ANY] * len(after),
        out_specs=tuple([HBM] * nb),
        input_output_aliases={t: t for t in range(nb)},
        compiler_params=pltpu.CompilerParams(has_side_effects=EFFECT),
    )(*bufs, *sems, *after)


def _plan_gather(n_w):
    def plan(refs):
        x, y, c, chips = _place()
        me = 2 * x + y
        return [(refs[a], refs[n_w + a].at[me], (*chip, c)) for a in range(n_w) for chip in chips + [(x, y)]]
    return plan


def _plan_pair_halves(n_g, rows):
    def plan(refs):
        x, y, c, _ = _place()
        return [(refs[a].at[:, _half(1 - c, rows[a])], refs[n_g + a], (x, y, 1 - c)) for a in range(n_g)]
    return plan


def _plan_chip_parts(n_g):
    def plan(refs):
        x, y, c, chips = _place()
        me = 2 * x + y
        return [(refs[a].at[2 * px + py], refs[n_g + a].at[me], (px, py, c))
                for a in range(n_g) for (px, py) in chips]
    return plan


def _plan_pair_fill(n_g, rows):
    def plan(refs):
        x, y, c, _ = _place()
        return [(refs[a].at[_half(c, rows[a])], refs[a].at[_half(c, rows[a])], (x, y, 1 - c)) for a in range(n_g)]
    return plan


def _pair_add(g, got, name):
    _, rows, cols = g.shape
    hr = rows // 2
    tr = min(hr, 512)
    nb = hr // tr

    def body(g_ref, got_ref, p_ref, own_ref):
        val = (g_ref[...] + got_ref[...]).astype(BF16)
        p_ref[...] = val

        @pl.when(pl.program_id(1) == 2 * lax.axis_index("x") + lax.axis_index("y"))
        def _():
            own_ref[...] = val

    def mine(i, s):
        return (2 * lax.axis_index("x") + lax.axis_index("y"), i, 0)

    return pl.pallas_call(
        body, name=name, grid=(nb, N_CHIPS),
        in_specs=[pl.BlockSpec((None, tr, cols), lambda i, s: (s, lax.axis_index("c") * nb + i, 0)),
                  pl.BlockSpec((None, tr, cols), lambda i, s: (s, i, 0))],
        out_specs=[pl.BlockSpec((None, tr, cols), lambda i, s: (s, i, 0)), pl.BlockSpec((None, tr, cols), mine)],
        out_shape=[jax.ShapeDtypeStruct((N_CHIPS, hr, cols), BF16)] * 2,
        compiler_params=_cp(("parallel", "arbitrary")),
    )(g, got)


def _chip_sum(parts, name):
    _, hr, cols = parts.shape
    tr = min(hr, 256)
    nb = hr // tr

    def body(t_ref, o_ref):
        part = [t_ref[s].astype(F32) for s in range(N_CHIPS)]
        o_ref[...] = ((part[0] + part[1]) + part[2]) + part[3]

    return pl.pallas_call(
        body, name=name, grid=(nb,),
        in_specs=[pl.BlockSpec((N_CHIPS, tr, cols), lambda i: (0, i, 0))],
        out_specs=pl.BlockSpec((tr, cols), lambda i: (lax.axis_index("c") * nb + i, 0)),
        out_shape=jax.ShapeDtypeStruct((2 * hr, cols), F32),
        compiler_params=_cp(("parallel",)),
    )(parts)


class _Reduction:
    def __init__(self, tag, rows, first_id):
        self.tag, self.n, self.rows, self.first_id = tag, len(rows), rows, first_id
        self.plans = (_plan_pair_halves(self.n, rows), _plan_chip_parts(self.n), _plan_pair_fill(self.n, rows))
        self.flight = None

    def _name(self, what):
        return f"grad_{self.tag}_{what}"

    @staticmethod
    def _sibling():
        x, y, c, _ = _place()
        return [(x, y, 1 - c)]

    @staticmethod
    def _same_core_elsewhere():
        x, y, c, chips = _place()
        return [(*chip, c) for chip in chips]

    def start(self, gs):
        gots = [lax.empty((N_CHIPS, t.shape[1] // 2, t.shape[2]), F32) for t in gs]
        self.flight = _copies_start(self._name("pair_start"), list(gs) + gots, self.plans[0], self.n,
                                    (self.first_id, self._sibling))
        return self.flight[2]

    def after_pair(self, after):
        sems, bufs, _ = self.flight
        out = _copies_wait(self._name("pair_wait"), sems, bufs, self.plans[0], self.n, after)
        sums = [_pair_add(g, got, self._name(f"pair_add_{i}"))
                for i, (g, got) in enumerate(zip(out[:self.n], out[self.n:]))]
        self.flight = _copies_start(self._name("chip_start"), [p for p, _ in sums] + [own for _, own in sums],
                                    self.plans[1], 3 * self.n, (self.first_id + 1, self._same_core_elsewhere))
        return self.flight[2]

    def after_chips(self, after):
        sems, bufs, _ = self.flight
        out = _copies_wait(self._name("chip_wait"), sems, bufs, self.plans[1], 3 * self.n, after)
        fulls = [_chip_sum(t, self._name(f"chip_sum_{i}")) for i, t in enumerate(out[self.n:])]
        self.flight = _copies_start(self._name("fill_start"), fulls, self.plans[2], self.n,
                                    (self.first_id + 2, self._sibling))
        return self.flight[2]

    def finish(self, after):
        sems, bufs, _ = self.flight
        return _copies_wait(self._name("fill_wait"), sems, bufs, self.plans[2], self.n, after)


def _adamw_math(w, g, m, v):
    nm = ADAM_B1 * m + (1.0 - ADAM_B1) * g
    nv = ADAM_B2 * v + (1.0 - ADAM_B2) * (g * g)
    m_hat = nm / (1.0 - ADAM_B1 ** ADAM_STEP)
    v_hat = nv / (1.0 - ADAM_B2 ** ADAM_STEP)
    return -ADAM_LR * (m_hat / (jnp.sqrt(v_hat) + ADAM_EPS) + ADAM_WD * w), nm, nv


def _adamw(w, g, m, v, name, tr):
    r, cdim = w.shape

    def body(w_ref, g_ref, m_ref, v_ref, d_ref, nm_ref, nv_ref):
        d_ref[...], nm_ref[...], nv_ref[...] = _adamw_math(w_ref[...], g_ref[...], m_ref[...], v_ref[...])

    return pl.pallas_call(
        body, name=name, grid=(r // tr,), in_specs=[_rows(tr, cdim)] * 4, out_specs=[_rows(tr, cdim)] * 3,
        out_shape=[jax.ShapeDtypeStruct((r, cdim), F32)] * 3, compiler_params=_cp(("parallel",)),
    )(w, g, m, v)


def _adamw_small(w, parts, m, v, shapes):
    n_rows = w.shape[0]

    def scatter(src, outs):
        row = 0
        for (rows, cols), out in zip(shapes, outs):
            if cols == LANES:
                out[...] = src[row:row + rows, :]
            elif cols > LANES:
                per = cols // LANES
                for r in range(rows):
                    for cb in range(per):
                        out[r:r + 1, cb * LANES:(cb + 1) * LANES] = src[row + r * per + cb:row + r * per + cb + 1, :]
            else:
                per = LANES // cols
                for r in range(rows):
                    out[r:r + 1, :] = src[row + r // per:row + r // per + 1, (r % per) * cols:(r % per + 1) * cols]
            row += -(-rows * cols // LANES)

    def body(w_ref, p_ref, m_ref, v_ref, *rest):
        outs, scr = rest[:-4], rest[-4:]
        g = p_ref[0]
        for dev in range(1, N_DEV):
            g = g + p_ref[dev]
        scr[3][...] = g
        scr[0][...], scr[1][...], scr[2][...] = _adamw_math(w_ref[...], g, m_ref[...], v_ref[...])
        n = len(shapes)
        for kind in range(4):
            scatter(scr[kind], outs[kind * n:(kind + 1) * n])

    outs = pl.pallas_call(
        body, name="adamw_small", grid=(1,),
        in_specs=[_const(w.shape), _const(parts.shape), _const(w.shape), _const(w.shape)],
        out_specs=[_const(s) for s in shapes] * 4, out_shape=[jax.ShapeDtypeStruct(s, F32) for s in shapes] * 4,
        scratch_shapes=[pltpu.VMEM((n_rows, LANES), F32)] * 4,
        compiler_params=_cp(("arbitrary",)),
    )(w, parts, m, v)
    n = len(shapes)
    return [outs[kind * n:(kind + 1) * n] for kind in range(4)]


REPLICATED = (("norm_mix_pre", 1024), ("norm_mix_post", 1024), ("norm_ffn_pre", 1024), ("norm_ffn_post", 1024),
              ("rel_bias", 256), ("sinks", 8), ("rwkv_shift_mix", 1792), ("w0", 512), ("a0", 512), ("k_k", 512),
              ("k_a", 512), ("r_k", 512), ("ln_x_g", 512), ("ln_x_b", 512), ("conv_b", 8192))
SMALL_SHARDED = (("w_decay_up", LORA_DECAY, D_RWKV), ("w_iclr_up", LORA_ICLR, D_RWKV),
                 ("w_gate_up", LORA_GATE, D_RWKV), ("conv_w", 3, 2 * D_FF))
BIG = (("w_in", D_MODEL, 640), ("w_out", 256, D_MODEL), ("w_ffn_up", D_MODEL, 2048), ("w_ffn_down", 1024, D_MODEL))
PACK_ALIGN = 8 * LANES


def _pack(pieces):
    flat = []
    for t in pieces:
        t = t.reshape(-1)
        pad = (-t.shape[0]) % LANES
        flat.append(jnp.pad(t, (0, pad)) if pad else t)
    flat = jnp.concatenate(flat)
    pad = (-flat.shape[0]) % PACK_ALIGN
    return jnp.pad(flat, (0, pad)).reshape(-1, LANES)


def kernel(x, norm_mix_pre, norm_mix_post, norm_ffn_pre, norm_ffn_post, w_in, rel_bias, sinks, rwkv_shift_mix, w0, w_decay_up, a0, w_iclr_up, w_gate_up, k_k, k_a, r_k, ln_x_g, ln_x_b, w_out, w_ffn_up, conv_w, conv_b, w_ffn_down, loss_target, m_norm_mix_pre, m_norm_mix_post, m_norm_ffn_pre, m_norm_ffn_post, m_w_in, m_rel_bias, m_sinks, m_rwkv_shift_mix, m_w0, m_w_decay_up, m_a0, m_w_iclr_up, m_w_gate_up, m_k_k, m_k_a, m_r_k, m_ln_x_g, m_ln_x_b, m_w_out, m_w_ffn_up, m_conv_w, m_conv_b, m_w_ffn_down, v_norm_mix_pre, v_norm_mix_post, v_norm_ffn_pre, v_norm_ffn_post, v_w_in, v_rel_bias, v_sinks, v_rwkv_shift_mix, v_w0, v_w_decay_up, v_a0, v_w_iclr_up, v_w_gate_up, v_k_k, v_k_a, v_r_k, v_ln_x_g, v_ln_x_b, v_w_out, v_w_ffn_up, v_conv_w, v_conv_b, v_w_ffn_down):
    given = dict(locals())
    names = [n for n, _ in REPLICATED] + [n for n, _, _ in SMALL_SHARDED] + [n for n, _, _ in BIG]
    order = ["norm_mix_pre", "norm_mix_post", "norm_ffn_pre", "norm_ffn_post", "w_in", "rel_bias", "sinks",
             "rwkv_shift_mix", "w0", "w_decay_up", "a0", "w_iclr_up", "w_gate_up", "k_k", "k_a", "r_k", "ln_x_g",
             "ln_x_b", "w_out", "w_ffn_up", "conv_w", "conv_b", "w_ffn_down"]
    assert sorted(names) == sorted(order)

    big_sh = {n: given[n].reshape(a, b).astype(BF16) for n, a, b in BIG}
    small_sh = [given[n].reshape(r, c // N_CHIPS) for n, r, c in SMALL_SHARDED]
    gathered = _gather_weights([big_sh["w_in"]], small_sh)
    rest = ("w_out", "w_ffn_up", "w_ffn_down")
    win_st, rest_sh = lax.optimization_barrier((gathered[0], [big_sh[n] for n in rest]))
    sm = {n: given[n] for n, _ in REPLICATED}
    sm["r_k"] = r_k.reshape(N_Q_HEADS, HEAD_DIM)
    for (n, r, c), st in zip(SMALL_SHARDED, gathered[1:]):
        sm[n] = st.transpose(1, 0, 2).reshape(r, c)

    lands = [lax.empty((N_CHIPS,) + t.shape, BF16) for t in rest_sh]
    plan_w = _plan_gather(len(rest))
    n_w = N_CHIPS * len(rest)
    w_sems, w_bufs, token = _copies_start("gather_rest_start", rest_sh + lands, plan_w, n_w)
    sm["norm_mix_pre"] = norm_mix_pre + token[0:1, 0:1]

    def on_rest_weights(after):
        out = _copies_wait("gather_rest_wait", w_sems, w_bufs, plan_w, n_w, after)
        wout_st, wup_st, wdown_st = out[3:]
        return wout_st.reshape(D_MODEL, D_MODEL), wup_st, wdown_st.reshape(D_FF, D_MODEL)

    red_a = _Reduction("a", (1024, D_MODEL, 256), first_id=0)
    red_b = _Reduction("b", (D_MODEL,), first_id=3)

    def on_grads_a(gs):
        d_wdown, d_wup, d_wout = gs
        return red_a.start([d_wdown.reshape(N_CHIPS, 1024, D_MODEL), d_wup, d_wout.reshape(N_CHIPS, 256, D_MODEL)])

    handlers = {"rest_weights": on_rest_weights, "grads_a": on_grads_a, "seam_1": red_a.after_pair,
                "seam_2": red_a.after_chips, "grads_b": lambda g: red_b.start([g])}
    steps = _local_step(x[0], loss_target[0], sm, win_st)
    kind, payload = next(steps)
    while True:
        try:
            kind, payload = steps.send(handlers[kind](payload))
        except StopIteration as done:
            loss, grad_x, grads = done.value
            break

    small_names = [n for n, _ in REPLICATED] + [n for n, _, _ in SMALL_SHARDED]

    def shard_cols(t, s):
        return t[:, s * (t.shape[1] // N_CHIPS):(s + 1) * (t.shape[1] // N_CHIPS)]

    for_chip = jnp.stack([_pack([loss[0]] + [grads[n] for n, _ in REPLICATED]
                                + [shard_cols(grads[n], s) for n, _, _ in SMALL_SHARDED]) for s in range(N_CHIPS)])
    land = lax.empty((N_DEV,) + for_chip.shape[1:], F32)

    def plan_small(refs):
        x, y, c, _ = _place()
        out = []
        for rel in range(N_DEV):
            px, py, pc = x ^ (rel >> 2), y ^ ((rel >> 1) & 1), c ^ (rel & 1)
            out.append((refs[0].at[2 * px + py], refs[1].at[4 * x + 2 * y + c], (px, py, pc)))
        return out

    s_sems, s_bufs, s_token = _copies_start("grad_small_start", [for_chip, land], plan_small, N_DEV)

    red_b.after_pair([grad_x, s_token])
    g_out = {}
    g_out["w_ffn_down"], g_out["w_ffn_up"], g_out["w_out"] = red_a.finish(grad_x)

    delta, new_m, new_v = {}, {}, {}

    def update(n, a, b):
        delta[n], new_m[n], new_v[n] = _adamw(given[n].reshape(a, b), g_out[n], given["m_" + n].reshape(a, b),
                                              given["v_" + n].reshape(a, b), "adamw_" + n, 256)

    for n, a, b in BIG[1:]:
        update(n, a, b)
    done = [delta[n] for n, _, _ in BIG[1:]]
    red_b.after_chips(done)
    parts = _copies_wait("grad_small_wait", s_sems, s_bufs, plan_small, N_DEV, done)[1]
    no_param = jnp.zeros((LANES,), F32)
    packs = [_pack([no_param] + [given[pre + n] for n in small_names]) for pre in ("", "m_", "v_")]

    def piece_shape(n):
        shape = given[n].shape
        rows, cols = int(np.prod(shape[:-1])), shape[-1]
        whole = cols % LANES == 0 or (LANES % cols == 0 and (rows * cols) % LANES == 0 and cols >= HEAD_DIM)
        return (rows, cols) if whole else (-(-rows * cols // LANES), LANES)

    shapes = [(1, LANES)] + [piece_shape(n) for n in small_names]
    upd = _adamw_small(packs[0], parts, packs[1], packs[2], shapes)
    loss = upd[3][0][0, 0]
    for i, n in enumerate(small_names):
        shape = given[n].shape
        size = int(np.prod(shape))
        delta[n], new_m[n], new_v[n], g_out[n] = (u[1 + i].reshape(-1)[:size].reshape(shape) for u in upd)
    g_out["w_in"], = red_b.finish(upd[0][0])
    update(*BIG[0])

    def shaped(d):
        return [d[n].reshape(given[n].shape) for n in order]

    return (loss, grad_x.reshape(x.shape), *shaped(g_out), *shaped(delta), *shaped(new_m), *shaped(new_v))
```
